```python
import math
import jax, jax.numpy as jnp
from jax import lax
import numpy as np

D_MODEL = 1024
BATCH = 16
SEQ = 2048
DEPTH = 1

SSD_HEADS = 16
SSD_HEAD_DIM = 64
SSD_INNER = SSD_HEADS * SSD_HEAD_DIM
SSD_GROUPS = 2
SSD_STATE = 128
SSD_CONV = 4
SSD_CHUNK = 128
SSD_CONV_CH = SSD_INNER + 2 * SSD_GROUPS * SSD_STATE
MLA_HEADS = 16
MLA_Q_RANK = 384
MLA_KV_RANK = 256
MLA_NOPE = 64
MLA_ROPE = 32
MLA_V = 64
MLA_QK = MLA_NOPE + MLA_ROPE
ROPE_THETA = 10000.0
Q_BLOCK = 128
MEM_LEN = 256
XA_HEADS = 4
XA_HEAD_DIM = D_MODEL // XA_HEADS
D_FF = 2816
FFN_RES_WEIGHT = 0.5
N_BRANCHES = 2
EPS = 1e-6
IN_SIZES = (SSD_INNER, SSD_CONV_CH, SSD_HEADS, MLA_Q_RANK, MLA_KV_RANK, MLA_ROPE, N_BRANCHES * D_MODEL)
D_IN = SSD_INNER + SSD_CONV_CH + SSD_HEADS + MLA_Q_RANK + MLA_KV_RANK + MLA_ROPE + N_BRANCHES * D_MODEL

kernel_name = "hybrid_ssd_mla_gated_macaron"


def _split_points(sizes):
    pts, acc = [], 0
    for sz in sizes[:-1]:
        acc += sz
        pts.append(acc)
    return pts


def rms_norm(x, g):
    xf = x.astype(jnp.float32)
    y = xf * lax.rsqrt(jnp.mean(xf * xf, axis=-1, keepdims=True) + EPS)
    return (y * g.astype(jnp.float32)).astype(x.dtype)


def swiglu(h, w_gate, w_up, w_down):
    return (jax.nn.silu(h @ w_gate) * (h @ w_up)) @ w_down


def rope_cos_sin(positions, dim):
    inv = ROPE_THETA ** (-jnp.arange(0, dim, 2, dtype=jnp.float32) / dim)
    ang = positions.astype(jnp.float32)[..., None] * inv
    return jnp.cos(ang), jnp.sin(ang)


def apply_rope(x, cos, sin):
    x1, x2 = jnp.split(x.astype(jnp.float32), 2, axis=-1)
    return jnp.concatenate([x1 * cos - x2 * sin, x1 * sin + x2 * cos], axis=-1).astype(x.dtype)


def causal_depthwise_conv(u, w, b):
    out = lax.conv_general_dilated(u, w[:, None, :], window_strides=(1,),
                                   padding=((SSD_CONV - 1, 0),),
                                   dimension_numbers=('NWC', 'WIO', 'NWC'),
                                   feature_group_count=u.shape[-1])
    return out + b


def segsum(a):
    L = a.shape[-1]
    cs = jnp.cumsum(a, axis=-1)
    diff = cs[..., :, None] - cs[..., None, :]
    mask = jnp.tril(jnp.ones((L, L), dtype=bool))
    return jnp.where(mask, diff, -jnp.inf)


def ssd_chunked(xh, dt, a, bm, cm):
    bsz, s, h, p = xh.shape
    g, n = bm.shape[-2:]
    r = h // g
    L = SSD_CHUNK
    c = s // L
    f32 = jnp.float32
    xdt = (xh.astype(f32) * dt[..., None]).reshape(bsz, c, L, g, r, p)
    adt = (dt * a).reshape(bsz, c, L, g, r).transpose(0, 3, 4, 1, 2)
    bm = bm.astype(f32).reshape(bsz, c, L, g, n)
    cm = cm.astype(f32).reshape(bsz, c, L, g, n)
    a_cs = jnp.cumsum(adt, axis=-1)
    decay = jnp.exp(segsum(adt))
    cb = jnp.einsum('bclgn,bcsgn->bcgls', cm, bm)
    y_diag = jnp.einsum('bcgls,bgrcls,bcsgrp->bclgrp', cb, decay, xdt)
    decay_states = jnp.exp(a_cs[..., -1:] - a_cs)
    states = jnp.einsum('bclgn,bgrcl,bclgrp->bcgrpn', bm, decay_states, xdt)
    chunk_decay = jnp.exp(a_cs[..., -1])

    def step(carry, inp):
        st, dec = inp
        return carry * dec[..., None, None] + st, carry

    init = jnp.zeros((bsz, g, r, p, n), f32)
    _, prev = lax.scan(step, init, (states.transpose(1, 0, 2, 3, 4, 5), chunk_decay.transpose(3, 0, 1, 2)))
    prev = prev.transpose(1, 0, 2, 3, 4, 5)
    y_off = jnp.einsum('bclgn,bcgrpn,bgrcl->bclgrp', cm, prev, jnp.exp(a_cs))
    return (y_diag + y_off).reshape(bsz, s, h, p)


def causal_block_attention(q_nope, q_rope, k_nope, k_rope, v):
    bsz, s, h, _ = q_nope.shape
    nb = s // Q_BLOCK
    scale = MLA_QK ** -0.5
    kpos = jnp.arange(s)

    def blk(args):
        qn, qr, i = args
        sc = (jnp.einsum('bqhd,bkhd->bhqk', qn, k_nope)
              + jnp.einsum('bqhd,bkd->bhqk', qr, k_rope)).astype(jnp.float32) * scale
        qpos = i * Q_BLOCK + jnp.arange(Q_BLOCK)
        sc = jnp.where(kpos[None, :] <= qpos[:, None], sc, -jnp.inf)
        pr = jax.nn.softmax(sc, axis=-1).astype(v.dtype)
        return jnp.einsum('bhqk,bkhd->bqhd', pr, v)

    def to_blocks(t):
        return t.reshape(bsz, nb, Q_BLOCK, *t.shape[2:]).swapaxes(0, 1)

    out = lax.map(blk, (to_blocks(q_nope), to_blocks(q_rope), jnp.arange(nb)))
    return out.swapaxes(0, 1).reshape(bsz, s, h, -1)


def hybrid_mixer(h, positions, w_in, conv_w, conv_b, dt_bias, a_log, d_skip, ssd_norm_g, w_ssd_proj,
                 q_norm_g, w_uq, kv_norm_g, w_uk, w_uv, w_mla_proj, gate_bias, w_out):
    bsz, s, _ = h.shape
    f32 = jnp.float32
    proj = h @ w_in
    z, xbc, dt_raw, q_c, kv_c, k_r, gate_logits = jnp.split(proj, _split_points(IN_SIZES), axis=-1)

    xbc = jax.nn.silu(causal_depthwise_conv(xbc, conv_w, conv_b))
    xs, bm, cm = jnp.split(xbc, [SSD_INNER, SSD_INNER + SSD_GROUPS * SSD_STATE], axis=-1)
    xs = xs.reshape(bsz, s, SSD_HEADS, SSD_HEAD_DIM)
    bm = bm.reshape(bsz, s, SSD_GROUPS, SSD_STATE)
    cm = cm.reshape(bsz, s, SSD_GROUPS, SSD_STATE)
    dt = jax.nn.softplus((dt_raw + dt_bias).astype(f32))
    a = -jnp.exp(a_log.astype(f32))
    y = ssd_chunked(xs, dt, a, bm, cm) + d_skip.astype(f32)[:, None] * xs.astype(f32)
    y = y.reshape(bsz, s, SSD_INNER).astype(h.dtype) * jax.nn.silu(z)
    y = rms_norm(y.reshape(bsz, s, SSD_GROUPS, -1), ssd_norm_g.reshape(SSD_GROUPS, -1)).reshape(bsz, s, SSD_INNER)
    y_ssd = y @ w_ssd_proj

    cos, sin = rope_cos_sin(positions, MLA_ROPE)
    q = (rms_norm(q_c, q_norm_g) @ w_uq).reshape(bsz, s, MLA_HEADS, MLA_QK)
    q_nope = q[..., :MLA_NOPE]
    q_rope = apply_rope(q[..., MLA_NOPE:], cos[:, :, None], sin[:, :, None])
    kv_c = rms_norm(kv_c, kv_norm_g)
    k_nope = (kv_c @ w_uk).reshape(bsz, s, MLA_HEADS, MLA_NOPE)
    v = (kv_c @ w_uv).reshape(bsz, s, MLA_HEADS, MLA_V)
    k_rope = apply_rope(k_r, cos, sin)
    o = causal_block_attention(q_nope, q_rope, k_nope, k_rope, v).reshape(bsz, s, MLA_HEADS * MLA_V)
    y_mla = o @ w_mla_proj

    gates = jax.nn.sigmoid((gate_logits + gate_bias).astype(f32)).astype(h.dtype)
    g_ssd, g_mla = jnp.split(gates, N_BRANCHES, axis=-1)
    return (g_ssd * y_ssd + g_mla * y_mla) @ w_out


def memory_cross_attention(h, mem_n, w_q, w_k, w_v, w_o):
    bsz, s, _ = h.shape
    q = (h @ w_q).reshape(bsz, s, XA_HEADS, XA_HEAD_DIM)
    k = (mem_n @ w_k).reshape(bsz, -1, XA_HEADS, XA_HEAD_DIM)
    v = (mem_n @ w_v).reshape(bsz, -1, XA_HEADS, XA_HEAD_DIM)
    sc = jnp.einsum('bqhd,bkhd->bhqk', q, k).astype(jnp.float32) * (XA_HEAD_DIM ** -0.5)
    pr = jax.nn.softmax(sc, axis=-1).astype(v.dtype)
    o = jnp.einsum('bhqk,bkhd->bqhd', pr, v).reshape(bsz, s, D_MODEL)
    return o @ w_o


def _fwd_setup_inputs(seed: int = 0) -> dict:
    key = jax.random.key(seed)
    keys = iter(jax.random.split(key, 48))
    f32 = jnp.float32

    def dense(fan_in, *shape):
        return jax.random.normal(next(keys), (DEPTH,) + shape, f32) * fan_in ** -0.5

    def gain(*shape):
        return 1.0 + 0.02 * jax.random.normal(next(keys), (DEPTH,) + shape, f32)

    def small(*shape):
        return 0.01 * jax.random.normal(next(keys), (DEPTH,) + shape, f32)

    x = jax.random.normal(next(keys), (BATCH, SEQ, D_MODEL), f32)
    mem = jax.random.normal(next(keys), (BATCH, MEM_LEN, D_MODEL), f32)
    offset = jax.random.randint(next(keys), (BATCH, 1), 0, 1024, dtype=jnp.int32)
    positions = (offset + jnp.arange(SEQ, dtype=jnp.int32)[None, :]).astype(jnp.int32)

    u = jax.random.uniform(next(keys), (DEPTH, SSD_HEADS), f32)
    dt0 = jnp.exp(u * (math.log(0.1) - math.log(0.001)) + math.log(0.001))
    dt_bias = dt0 + jnp.log(-jnp.expm1(-dt0))
    a_log = jnp.log(jax.random.uniform(next(keys), (DEPTH, SSD_HEADS), f32, minval=1.0, maxval=16.0))

    return {
        "x": x, "mem": mem, "positions": positions,
        "ffn1_pre_g": gain(D_MODEL), "ffn1_w_gate": dense(D_MODEL, D_MODEL, D_FF),
        "ffn1_w_up": dense(D_MODEL, D_MODEL, D_FF), "ffn1_w_down": dense(D_FF, D_FF, D_MODEL),
        "ffn1_post_g": gain(D_MODEL),
        "mix_pre_g": gain(D_MODEL), "w_in": dense(D_MODEL, D_MODEL, D_IN),
        "conv_w": dense(SSD_CONV, SSD_CONV, SSD_CONV_CH), "conv_b": small(SSD_CONV_CH),
        "dt_bias": dt_bias, "a_log": a_log,
        "d_skip": 1.0 + 0.1 * jax.random.normal(next(keys), (DEPTH, SSD_HEADS), f32),
        "ssd_norm_g": gain(SSD_INNER), "w_ssd_proj": dense(SSD_INNER, SSD_INNER, D_MODEL),
        "q_norm_g": gain(MLA_Q_RANK), "w_uq": dense(MLA_Q_RANK, MLA_Q_RANK, MLA_HEADS * MLA_QK),
        "kv_norm_g": gain(MLA_KV_RANK), "w_uk": dense(MLA_KV_RANK, MLA_KV_RANK, MLA_HEADS * MLA_NOPE),
        "w_uv": dense(MLA_KV_RANK, MLA_KV_RANK, MLA_HEADS * MLA_V),
        "w_mla_proj": dense(MLA_HEADS * MLA_V, MLA_HEADS * MLA_V, D_MODEL),
        "gate_bias": small(N_BRANCHES * D_MODEL), "w_out": dense(D_MODEL, D_MODEL, D_MODEL),
        "mix_post_g": gain(D_MODEL),
        "xa_pre_g": gain(D_MODEL), "mem_norm_g": gain(D_MODEL),
        "w_xq": dense(D_MODEL, D_MODEL, D_MODEL), "w_xk": dense(D_MODEL, D_MODEL, D_MODEL),
        "w_xv": dense(D_MODEL, D_MODEL, D_MODEL), "w_xo": dense(D_MODEL, D_MODEL, D_MODEL),
        "xa_post_g": gain(D_MODEL),
        "ffn2_pre_g": gain(D_MODEL), "ffn2_w_gate": dense(D_MODEL, D_MODEL, D_FF),
        "ffn2_w_up": dense(D_MODEL, D_MODEL, D_FF), "ffn2_w_down": dense(D_FF, D_FF, D_MODEL),
        "ffn2_post_g": gain(D_MODEL),
    }


def _fwd_reference(x, mem, positions, ffn1_pre_g, ffn1_w_gate, ffn1_w_up, ffn1_w_down, ffn1_post_g,
              mix_pre_g, w_in, conv_w, conv_b, dt_bias, a_log, d_skip, ssd_norm_g, w_ssd_proj,
              q_norm_g, w_uq, kv_norm_g, w_uk, w_uv, w_mla_proj, gate_bias, w_out, mix_post_g,
              xa_pre_g, mem_norm_g, w_xq, w_xk, w_xv, w_xo, xa_post_g,
              ffn2_pre_g, ffn2_w_gate, ffn2_w_up, ffn2_w_down, ffn2_post_g):
    for l in range(DEPTH):
        h = swiglu(rms_norm(x, ffn1_pre_g[l]), ffn1_w_gate[l], ffn1_w_up[l], ffn1_w_down[l])
        x = x + FFN_RES_WEIGHT * rms_norm(h, ffn1_post_g[l])
        h = hybrid_mixer(rms_norm(x, mix_pre_g[l]), positions, w_in[l], conv_w[l], conv_b[l], dt_bias[l],
                         a_log[l], d_skip[l], ssd_norm_g[l], w_ssd_proj[l], q_norm_g[l], w_uq[l],
                         kv_norm_g[l], w_uk[l], w_uv[l], w_mla_proj[l], gate_bias[l], w_out[l])
        x = x + rms_norm(h, mix_post_g[l])
        h = memory_cross_attention(rms_norm(x, xa_pre_g[l]), rms_norm(mem, mem_norm_g[l]),
                                   w_xq[l], w_xk[l], w_xv[l], w_xo[l])
        x = x + rms_norm(h, xa_post_g[l])
        h = swiglu(rms_norm(x, ffn2_pre_g[l]), ffn2_w_gate[l], ffn2_w_up[l], ffn2_w_down[l])
        x = x + FFN_RES_WEIGHT * rms_norm(h, ffn2_post_g[l])
    return x


import jax as _jax
import jax.numpy as _jnp

TWIN_FORMAT = 'train_step'
FWD_PARAMS = ['x', 'mem', 'positions', 'ffn1_pre_g', 'ffn1_w_gate', 'ffn1_w_up', 'ffn1_w_down', 'ffn1_post_g', 'mix_pre_g', 'w_in', 'conv_w', 'conv_b', 'dt_bias', 'a_log', 'd_skip', 'ssd_norm_g', 'w_ssd_proj', 'q_norm_g', 'w_uq', 'kv_norm_g', 'w_uk', 'w_uv', 'w_mla_proj', 'gate_bias', 'w_out', 'mix_post_g', 'xa_pre_g', 'mem_norm_g', 'w_xq', 'w_xk', 'w_xv', 'w_xo', 'xa_post_g', 'ffn2_pre_g', 'ffn2_w_gate', 'ffn2_w_up', 'ffn2_w_down', 'ffn2_post_g']
TWIN_WEIGHTS = ['ffn1_pre_g', 'ffn1_w_gate', 'ffn1_w_up', 'ffn1_w_down', 'ffn1_post_g', 'mix_pre_g', 'w_in', 'conv_w', 'conv_b', 'dt_bias', 'a_log', 'd_skip', 'ssd_norm_g', 'w_ssd_proj', 'q_norm_g', 'w_uq', 'kv_norm_g', 'w_uk', 'w_uv', 'w_mla_proj', 'gate_bias', 'w_out', 'mix_post_g', 'xa_pre_g', 'mem_norm_g', 'w_xq', 'w_xk', 'w_xv', 'w_xo', 'xa_post_g', 'ffn2_pre_g', 'ffn2_w_gate', 'ffn2_w_up', 'ffn2_w_down', 'ffn2_post_g']
TWIN_DIFF_INPUT = 'x'
TWIN_INPUTS = ['x', 'mem', 'positions', 'ffn1_pre_g', 'ffn1_w_gate', 'ffn1_w_up', 'ffn1_w_down', 'ffn1_post_g', 'mix_pre_g', 'w_in', 'conv_w', 'conv_b', 'dt_bias', 'a_log', 'd_skip', 'ssd_norm_g', 'w_ssd_proj', 'q_norm_g', 'w_uq', 'kv_norm_g', 'w_uk', 'w_uv', 'w_mla_proj', 'gate_bias', 'w_out', 'mix_post_g', 'xa_pre_g', 'mem_norm_g', 'w_xq', 'w_xk', 'w_xv', 'w_xo', 'xa_post_g', 'ffn2_pre_g', 'ffn2_w_gate', 'ffn2_w_up', 'ffn2_w_down', 'ffn2_post_g', 'loss_target', 'm_ffn1_pre_g', 'm_ffn1_w_gate', 'm_ffn1_w_up', 'm_ffn1_w_down', 'm_ffn1_post_g', 'm_mix_pre_g', 'm_w_in', 'm_conv_w', 'm_conv_b', 'm_dt_bias', 'm_a_log', 'm_d_skip', 'm_ssd_norm_g', 'm_w_ssd_proj', 'm_q_norm_g', 'm_w_uq', 'm_kv_norm_g', 'm_w_uk', 'm_w_uv', 'm_w_mla_proj', 'm_gate_bias', 'm_w_out', 'm_mix_post_g', 'm_xa_pre_g', 'm_mem_norm_g', 'm_w_xq', 'm_w_xk', 'm_w_xv', 'm_w_xo', 'm_xa_post_g', 'm_ffn2_pre_g', 'm_ffn2_w_gate', 'm_ffn2_w_up', 'm_ffn2_w_down', 'm_ffn2_post_g', 'v_ffn1_pre_g', 'v_ffn1_w_gate', 'v_ffn1_w_up', 'v_ffn1_w_down', 'v_ffn1_post_g', 'v_mix_pre_g', 'v_w_in', 'v_conv_w', 'v_conv_b', 'v_dt_bias', 'v_a_log', 'v_d_skip', 'v_ssd_norm_g', 'v_w_ssd_proj', 'v_q_norm_g', 'v_w_uq', 'v_kv_norm_g', 'v_w_uk', 'v_w_uv', 'v_w_mla_proj', 'v_gate_bias', 'v_w_out', 'v_mix_post_g', 'v_xa_pre_g', 'v_mem_norm_g', 'v_w_xq', 'v_w_xk', 'v_w_xv', 'v_w_xo', 'v_xa_post_g', 'v_ffn2_pre_g', 'v_ffn2_w_gate', 'v_ffn2_w_up', 'v_ffn2_w_down', 'v_ffn2_post_g']
TWIN_OUTPUTS = ['loss', 'grad_x', 'grad_ffn1_pre_g', 'grad_ffn1_w_gate', 'grad_ffn1_w_up', 'grad_ffn1_w_down', 'grad_ffn1_post_g', 'grad_mix_pre_g', 'grad_w_in', 'grad_conv_w', 'grad_conv_b', 'grad_dt_bias', 'grad_a_log', 'grad_d_skip', 'grad_ssd_norm_g', 'grad_w_ssd_proj', 'grad_q_norm_g', 'grad_w_uq', 'grad_kv_norm_g', 'grad_w_uk', 'grad_w_uv', 'grad_w_mla_proj', 'grad_gate_bias', 'grad_w_out', 'grad_mix_post_g', 'grad_xa_pre_g', 'grad_mem_norm_g', 'grad_w_xq', 'grad_w_xk', 'grad_w_xv', 'grad_w_xo', 'grad_xa_post_g', 'grad_ffn2_pre_g', 'grad_ffn2_w_gate', 'grad_ffn2_w_up', 'grad_ffn2_w_down', 'grad_ffn2_post_g', 'delta_ffn1_pre_g', 'delta_ffn1_w_gate', 'delta_ffn1_w_up', 'delta_ffn1_w_down', 'delta_ffn1_post_g', 'delta_mix_pre_g', 'delta_w_in', 'delta_conv_w', 'delta_conv_b', 'delta_dt_bias', 'delta_a_log', 'delta_d_skip', 'delta_ssd_norm_g', 'delta_w_ssd_proj', 'delta_q_norm_g', 'delta_w_uq', 'delta_kv_norm_g', 'delta_w_uk', 'delta_w_uv', 'delta_w_mla_proj', 'delta_gate_bias', 'delta_w_out', 'delta_mix_post_g', 'delta_xa_pre_g', 'delta_mem_norm_g', 'delta_w_xq', 'delta_w_xk', 'delta_w_xv', 'delta_w_xo', 'delta_xa_post_g', 'delta_ffn2_pre_g', 'delta_ffn2_w_gate', 'delta_ffn2_w_up', 'delta_ffn2_w_down', 'delta_ffn2_post_g', 'new_m_ffn1_pre_g', 'new_m_ffn1_w_gate', 'new_m_ffn1_w_up', 'new_m_ffn1_w_down', 'new_m_ffn1_post_g', 'new_m_mix_pre_g', 'new_m_w_in', 'new_m_conv_w', 'new_m_conv_b', 'new_m_dt_bias', 'new_m_a_log', 'new_m_d_skip', 'new_m_ssd_norm_g', 'new_m_w_ssd_proj', 'new_m_q_norm_g', 'new_m_w_uq', 'new_m_kv_norm_g', 'new_m_w_uk', 'new_m_w_uv', 'new_m_w_mla_proj', 'new_m_gate_bias', 'new_m_w_out', 'new_m_mix_post_g', 'new_m_xa_pre_g', 'new_m_mem_norm_g', 'new_m_w_xq', 'new_m_w_xk', 'new_m_w_xv', 'new_m_w_xo', 'new_m_xa_post_g', 'new_m_ffn2_pre_g', 'new_m_ffn2_w_gate', 'new_m_ffn2_w_up', 'new_m_ffn2_w_down', 'new_m_ffn2_post_g', 'new_v_ffn1_pre_g', 'new_v_ffn1_w_gate', 'new_v_ffn1_w_up', 'new_v_ffn1_w_down', 'new_v_ffn1_post_g', 'new_v_mix_pre_g', 'new_v_w_in', 'new_v_conv_w', 'new_v_conv_b', 'new_v_dt_bias', 'new_v_a_log', 'new_v_d_skip', 'new_v_ssd_norm_g', 'new_v_w_ssd_proj', 'new_v_q_norm_g', 'new_v_w_uq', 'new_v_kv_norm_g', 'new_v_w_uk', 'new_v_w_uv', 'new_v_w_mla_proj', 'new_v_gate_bias', 'new_v_w_out', 'new_v_mix_post_g', 'new_v_xa_pre_g', 'new_v_mem_norm_g', 'new_v_w_xq', 'new_v_w_xk', 'new_v_w_xv', 'new_v_w_xo', 'new_v_xa_post_g', 'new_v_ffn2_pre_g', 'new_v_ffn2_w_gate', 'new_v_ffn2_w_up', 'new_v_ffn2_w_down', 'new_v_ffn2_post_g']
TWIN_LEAF_KINDS = {'loss': 'loss', 'grad_x': 'grad_x', 'grad_ffn1_pre_g': 'grad_w', 'grad_ffn1_w_gate': 'grad_w', 'grad_ffn1_w_up': 'grad_w', 'grad_ffn1_w_down': 'grad_w', 'grad_ffn1_post_g': 'grad_w', 'grad_mix_pre_g': 'grad_w', 'grad_w_in': 'grad_w', 'grad_conv_w': 'grad_w', 'grad_conv_b': 'grad_w', 'grad_dt_bias': 'grad_w', 'grad_a_log': 'grad_w', 'grad_d_skip': 'grad_w', 'grad_ssd_norm_g': 'grad_w', 'grad_w_ssd_proj': 'grad_w', 'grad_q_norm_g': 'grad_w', 'grad_w_uq': 'grad_w', 'grad_kv_norm_g': 'grad_w', 'grad_w_uk': 'grad_w', 'grad_w_uv': 'grad_w', 'grad_w_mla_proj': 'grad_w', 'grad_gate_bias': 'grad_w', 'grad_w_out': 'grad_w', 'grad_mix_post_g': 'grad_w', 'grad_xa_pre_g': 'grad_w', 'grad_mem_norm_g': 'grad_w', 'grad_w_xq': 'grad_w', 'grad_w_xk': 'grad_w', 'grad_w_xv': 'grad_w', 'grad_w_xo': 'grad_w', 'grad_xa_post_g': 'grad_w', 'grad_ffn2_pre_g': 'grad_w', 'grad_ffn2_w_gate': 'grad_w', 'grad_ffn2_w_up': 'grad_w', 'grad_ffn2_w_down': 'grad_w', 'grad_ffn2_post_g': 'grad_w', 'delta_ffn1_pre_g': 'delta_w', 'delta_ffn1_w_gate': 'delta_w', 'delta_ffn1_w_up': 'delta_w', 'delta_ffn1_w_down': 'delta_w', 'delta_ffn1_post_g': 'delta_w', 'delta_mix_pre_g': 'delta_w', 'delta_w_in': 'delta_w', 'delta_conv_w': 'delta_w', 'delta_conv_b': 'delta_w', 'delta_dt_bias': 'delta_w', 'delta_a_log': 'delta_w', 'delta_d_skip': 'delta_w', 'delta_ssd_norm_g': 'delta_w', 'delta_w_ssd_proj': 'delta_w', 'delta_q_norm_g': 'delta_w', 'delta_w_uq': 'delta_w', 'delta_kv_norm_g': 'delta_w', 'delta_w_uk': 'delta_w', 'delta_w_uv': 'delta_w', 'delta_w_mla_proj': 'delta_w', 'delta_gate_bias': 'delta_w', 'delta_w_out': 'delta_w', 'delta_mix_post_g': 'delta_w', 'delta_xa_pre_g': 'delta_w', 'delta_mem_norm_g': 'delta_w', 'delta_w_xq': 'delta_w', 'delta_w_xk': 'delta_w', 'delta_w_xv': 'delta_w', 'delta_w_xo': 'delta_w', 'delta_xa_post_g': 'delta_w', 'delta_ffn2_pre_g': 'delta_w', 'delta_ffn2_w_gate': 'delta_w', 'delta_ffn2_w_up': 'delta_w', 'delta_ffn2_w_down': 'delta_w', 'delta_ffn2_post_g': 'delta_w', 'new_m_ffn1_pre_g': 'new_m', 'new_m_ffn1_w_gate': 'new_m', 'new_m_ffn1_w_up': 'new_m', 'new_m_ffn1_w_down': 'new_m', 'new_m_ffn1_post_g': 'new_m', 'new_m_mix_pre_g': 'new_m', 'new_m_w_in': 'new_m', 'new_m_conv_w': 'new_m', 'new_m_conv_b': 'new_m', 'new_m_dt_bias': 'new_m', 'new_m_a_log': 'new_m', 'new_m_d_skip': 'new_m', 'new_m_ssd_norm_g': 'new_m', 'new_m_w_ssd_proj': 'new_m', 'new_m_q_norm_g': 'new_m', 'new_m_w_uq': 'new_m', 'new_m_kv_norm_g': 'new_m', 'new_m_w_uk': 'new_m', 'new_m_w_uv': 'new_m', 'new_m_w_mla_proj': 'new_m', 'new_m_gate_bias': 'new_m', 'new_m_w_out': 'new_m', 'new_m_mix_post_g': 'new_m', 'new_m_xa_pre_g': 'new_m', 'new_m_mem_norm_g': 'new_m', 'new_m_w_xq': 'new_m', 'new_m_w_xk': 'new_m', 'new_m_w_xv': 'new_m', 'new_m_w_xo': 'new_m', 'new_m_xa_post_g': 'new_m', 'new_m_ffn2_pre_g': 'new_m', 'new_m_ffn2_w_gate': 'new_m', 'new_m_ffn2_w_up': 'new_m', 'new_m_ffn2_w_down': 'new_m', 'new_m_ffn2_post_g': 'new_m', 'new_v_ffn1_pre_g': 'new_v', 'new_v_ffn1_w_gate': 'new_v', 'new_v_ffn1_w_up': 'new_v', 'new_v_ffn1_w_down': 'new_v', 'new_v_ffn1_post_g': 'new_v', 'new_v_mix_pre_g': 'new_v', 'new_v_w_in': 'new_v', 'new_v_conv_w': 'new_v', 'new_v_conv_b': 'new_v', 'new_v_dt_bias': 'new_v', 'new_v_a_log': 'new_v', 'new_v_d_skip': 'new_v', 'new_v_ssd_norm_g': 'new_v', 'new_v_w_ssd_proj': 'new_v', 'new_v_q_norm_g': 'new_v', 'new_v_w_uq': 'new_v', 'new_v_kv_norm_g': 'new_v', 'new_v_w_uk': 'new_v', 'new_v_w_uv': 'new_v', 'new_v_w_mla_proj': 'new_v', 'new_v_gate_bias': 'new_v', 'new_v_w_out': 'new_v', 'new_v_mix_post_g': 'new_v', 'new_v_xa_pre_g': 'new_v', 'new_v_mem_norm_g': 'new_v', 'new_v_w_xq': 'new_v', 'new_v_w_xk': 'new_v', 'new_v_w_xv': 'new_v', 'new_v_w_xo': 'new_v', 'new_v_xa_post_g': 'new_v', 'new_v_ffn2_pre_g': 'new_v', 'new_v_ffn2_w_gate': 'new_v', 'new_v_ffn2_w_up': 'new_v', 'new_v_ffn2_w_down': 'new_v', 'new_v_ffn2_post_g': 'new_v'}


def _forward(args):
    return _fwd_reference(*[args[k] for k in FWD_PARAMS])


def _output_shape():
    out = _jax.eval_shape(lambda: _forward(_fwd_setup_inputs(0)))
    return out.shape, out.dtype

N_MICROBATCH = 1
ADAM_LR = 0.001
ADAM_B1 = 0.9
ADAM_B2 = 0.999
ADAM_EPS = 1e-08
ADAM_WD = 0.01
ADAM_STEP = 10
PER_EXAMPLE_BATCH_AXIS = {'x': 0, 'mem': 0, 'positions': 0, 'loss_target': 0}
SHARED_INPUTS = []
_WEIGHT_DTYPES = {'ffn1_pre_g': _jnp.float32, 'ffn1_w_gate': _jnp.float32, 'ffn1_w_up': _jnp.float32, 'ffn1_w_down': _jnp.float32, 'ffn1_post_g': _jnp.float32, 'mix_pre_g': _jnp.float32, 'w_in': _jnp.float32, 'conv_w': _jnp.float32, 'conv_b': _jnp.float32, 'dt_bias': _jnp.float32, 'a_log': _jnp.float32, 'd_skip': _jnp.float32, 'ssd_norm_g': _jnp.float32, 'w_ssd_proj': _jnp.float32, 'q_norm_g': _jnp.float32, 'w_uq': _jnp.float32, 'kv_norm_g': _jnp.float32, 'w_uk': _jnp.float32, 'w_uv': _jnp.float32, 'w_mla_proj': _jnp.float32, 'gate_bias': _jnp.float32, 'w_out': _jnp.float32, 'mix_post_g': _jnp.float32, 'xa_pre_g': _jnp.float32, 'mem_norm_g': _jnp.float32, 'w_xq': _jnp.float32, 'w_xk': _jnp.float32, 'w_xv': _jnp.float32, 'w_xo': _jnp.float32, 'xa_post_g': _jnp.float32, 'ffn2_pre_g': _jnp.float32, 'ffn2_w_gate': _jnp.float32, 'ffn2_w_up': _jnp.float32, 'ffn2_w_down': _jnp.float32, 'ffn2_post_g': _jnp.float32}
MOMENT_SCALE = {'ffn1_pre_g': 6.023007e-01, 'ffn1_w_gate': 2.475843e-01, 'ffn1_w_up': 2.498841e-01, 'ffn1_w_down': 4.202511e-01, 'ffn1_post_g': 7.881426e+00, 'mix_pre_g': 8.670280e-01, 'w_in': 3.381315e-01, 'conv_w': 7.491409e-01, 'conv_b': 2.811099e+00, 'dt_bias': 1.058180e+00, 'a_log': 4.310728e+00, 'd_skip': 3.926949e+00, 'ssd_norm_g': 1.302269e+00, 'w_ssd_proj': 1.422080e+00, 'q_norm_g': 1.201224e-01, 'w_uq': 6.305796e-02, 'kv_norm_g': 5.213521e-01, 'w_uk': 6.576889e-02, 'w_uv': 1.607553e-01, 'w_mla_proj': 1.613614e-01, 'gate_bias': 4.234332e-01, 'w_out': 1.437311e+00, 'mix_post_g': 3.203529e+01, 'xa_pre_g': 6.149745e-01, 'mem_norm_g': 1.745286e+00, 'w_xq': 6.232067e-01, 'w_xk': 6.309292e-01, 'w_xv': 1.712142e+00, 'w_xo': 1.729146e+00, 'xa_post_g': 3.309572e+01, 'ffn2_pre_g': 8.000430e-01, 'ffn2_w_gate': 2.542485e-01, 'ffn2_w_up': 4.230637e-01, 'ffn2_w_down': 6.948327e-01, 'ffn2_post_g': 7.981045e+00}


def _to_microbatches(a, axis):
    t = _jnp.moveaxis(a, axis, 0)
    t = t.reshape((N_MICROBATCH, t.shape[0] // N_MICROBATCH) + t.shape[1:])
    return _jnp.moveaxis(t, 1, axis + 1)


def setup_inputs(seed: int = 0) -> dict:
    inp = _fwd_setup_inputs(seed)
    key = _jax.random.fold_in(_jax.random.key(seed), 7919)
    shape, _ = _output_shape()
    out = dict(inp)
    out["loss_target"] = _jax.random.normal(_jax.random.fold_in(key, 0), shape, _jnp.float32)
    for i, name in enumerate(TWIN_WEIGHTS):
        w = inp[name].astype(_jnp.float32)
        if MOMENT_SCALE is None:
            s = _jnp.sqrt(_jnp.mean(_jnp.square(w)) + 1e-30)
        else:
            s = MOMENT_SCALE[name]
        km, kv = _jax.random.split(_jax.random.fold_in(key, i + 1))
        out[name] = w
        out["m_" + name] = s * _jax.random.normal(km, w.shape, _jnp.float32)
        out["v_" + name] = (s * s) * _jax.random.uniform(kv, w.shape, _jnp.float32, 0.5, 1.5)
    if N_MICROBATCH > 1:
        for name, axis in PER_EXAMPLE_BATCH_AXIS.items():
            out[name] = _to_microbatches(out[name], axis)
    return {'x': out['x'], 'mem': out['mem'], 'positions': out['positions'], 'ffn1_pre_g': out['ffn1_pre_g'], 'ffn1_w_gate': out['ffn1_w_gate'], 'ffn1_w_up': out['ffn1_w_up'], 'ffn1_w_down': out['ffn1_w_down'], 'ffn1_post_g': out['ffn1_post_g'], 'mix_pre_g': out['mix_pre_g'], 'w_in': out['w_in'], 'conv_w': out['conv_w'], 'conv_b': out['conv_b'], 'dt_bias': out['dt_bias'], 'a_log': out['a_log'], 'd_skip': out['d_skip'], 'ssd_norm_g': out['ssd_norm_g'], 'w_ssd_proj': out['w_ssd_proj'], 'q_norm_g': out['q_norm_g'], 'w_uq': out['w_uq'], 'kv_norm_g': out['kv_norm_g'], 'w_uk': out['w_uk'], 'w_uv': out['w_uv'], 'w_mla_proj': out['w_mla_proj'], 'gate_bias': out['gate_bias'], 'w_out': out['w_out'], 'mix_post_g': out['mix_post_g'], 'xa_pre_g': out['xa_pre_g'], 'mem_norm_g': out['mem_norm_g'], 'w_xq': out['w_xq'], 'w_xk': out['w_xk'], 'w_xv': out['w_xv'], 'w_xo': out['w_xo'], 'xa_post_g': out['xa_post_g'], 'ffn2_pre_g': out['ffn2_pre_g'], 'ffn2_w_gate': out['ffn2_w_gate'], 'ffn2_w_up': out['ffn2_w_up'], 'ffn2_w_down': out['ffn2_w_down'], 'ffn2_post_g': out['ffn2_post_g'], 'loss_target': out['loss_target'], 'm_ffn1_pre_g': out['m_ffn1_pre_g'], 'm_ffn1_w_gate': out['m_ffn1_w_gate'], 'm_ffn1_w_up': out['m_ffn1_w_up'], 'm_ffn1_w_down': out['m_ffn1_w_down'], 'm_ffn1_post_g': out['m_ffn1_post_g'], 'm_mix_pre_g': out['m_mix_pre_g'], 'm_w_in': out['m_w_in'], 'm_conv_w': out['m_conv_w'], 'm_conv_b': out['m_conv_b'], 'm_dt_bias': out['m_dt_bias'], 'm_a_log': out['m_a_log'], 'm_d_skip': out['m_d_skip'], 'm_ssd_norm_g': out['m_ssd_norm_g'], 'm_w_ssd_proj': out['m_w_ssd_proj'], 'm_q_norm_g': out['m_q_norm_g'], 'm_w_uq': out['m_w_uq'], 'm_kv_norm_g': out['m_kv_norm_g'], 'm_w_uk': out['m_w_uk'], 'm_w_uv': out['m_w_uv'], 'm_w_mla_proj': out['m_w_mla_proj'], 'm_gate_bias': out['m_gate_bias'], 'm_w_out': out['m_w_out'], 'm_mix_post_g': out['m_mix_post_g'], 'm_xa_pre_g': out['m_xa_pre_g'], 'm_mem_norm_g': out['m_mem_norm_g'], 'm_w_xq': out['m_w_xq'], 'm_w_xk': out['m_w_xk'], 'm_w_xv': out['m_w_xv'], 'm_w_xo': out['m_w_xo'], 'm_xa_post_g': out['m_xa_post_g'], 'm_ffn2_pre_g': out['m_ffn2_pre_g'], 'm_ffn2_w_gate': out['m_ffn2_w_gate'], 'm_ffn2_w_up': out['m_ffn2_w_up'], 'm_ffn2_w_down': out['m_ffn2_w_down'], 'm_ffn2_post_g': out['m_ffn2_post_g'], 'v_ffn1_pre_g': out['v_ffn1_pre_g'], 'v_ffn1_w_gate': out['v_ffn1_w_gate'], 'v_ffn1_w_up': out['v_ffn1_w_up'], 'v_ffn1_w_down': out['v_ffn1_w_down'], 'v_ffn1_post_g': out['v_ffn1_post_g'], 'v_mix_pre_g': out['v_mix_pre_g'], 'v_w_in': out['v_w_in'], 'v_conv_w': out['v_conv_w'], 'v_conv_b': out['v_conv_b'], 'v_dt_bias': out['v_dt_bias'], 'v_a_log': out['v_a_log'], 'v_d_skip': out['v_d_skip'], 'v_ssd_norm_g': out['v_ssd_norm_g'], 'v_w_ssd_proj': out['v_w_ssd_proj'], 'v_q_norm_g': out['v_q_norm_g'], 'v_w_uq': out['v_w_uq'], 'v_kv_norm_g': out['v_kv_norm_g'], 'v_w_uk': out['v_w_uk'], 'v_w_uv': out['v_w_uv'], 'v_w_mla_proj': out['v_w_mla_proj'], 'v_gate_bias': out['v_gate_bias'], 'v_w_out': out['v_w_out'], 'v_mix_post_g': out['v_mix_post_g'], 'v_xa_pre_g': out['v_xa_pre_g'], 'v_mem_norm_g': out['v_mem_norm_g'], 'v_w_xq': out['v_w_xq'], 'v_w_xk': out['v_w_xk'], 'v_w_xv': out['v_w_xv'], 'v_w_xo': out['v_w_xo'], 'v_xa_post_g': out['v_xa_post_g'], 'v_ffn2_pre_g': out['v_ffn2_pre_g'], 'v_ffn2_w_gate': out['v_ffn2_w_gate'], 'v_ffn2_w_up': out['v_ffn2_w_up'], 'v_ffn2_w_down': out['v_ffn2_w_down'], 'v_ffn2_post_g': out['v_ffn2_post_g']}


def _loss(weights, diff, rest, loss_target):
    with _jax.named_scope("forward"):
        args = {**rest, TWIN_DIFF_INPUT: diff, **{k: w.astype(_WEIGHT_DTYPES[k]) for k, w in weights.items()}}
        y = _forward(args)
    with _jax.named_scope("loss_head"):
        err = _jnp.square(y.astype(_jnp.float32) - loss_target)
        return 0.5 * _jnp.sum(_jnp.mean(err, axis=-1)) if err.ndim else 0.5 * err


def _adamw(w, g, m, v):
    m = ADAM_B1 * m + (1.0 - ADAM_B1) * g
    v = ADAM_B2 * v + (1.0 - ADAM_B2) * _jnp.square(g)
    m_hat = m / (1.0 - ADAM_B1 ** ADAM_STEP)
    v_hat = v / (1.0 - ADAM_B2 ** ADAM_STEP)
    delta = -ADAM_LR * (m_hat / (_jnp.sqrt(v_hat) + ADAM_EPS) + ADAM_WD * w)
    return delta, m, v


def reference(x, mem, positions, ffn1_pre_g, ffn1_w_gate, ffn1_w_up, ffn1_w_down, ffn1_post_g, mix_pre_g, w_in, conv_w, conv_b, dt_bias, a_log, d_skip, ssd_norm_g, w_ssd_proj, q_norm_g, w_uq, kv_norm_g, w_uk, w_uv, w_mla_proj, gate_bias, w_out, mix_post_g, xa_pre_g, mem_norm_g, w_xq, w_xk, w_xv, w_xo, xa_post_g, ffn2_pre_g, ffn2_w_gate, ffn2_w_up, ffn2_w_down, ffn2_post_g, loss_target, m_ffn1_pre_g, m_ffn1_w_gate, m_ffn1_w_up, m_ffn1_w_down, m_ffn1_post_g, m_mix_pre_g, m_w_in, m_conv_w, m_conv_b, m_dt_bias, m_a_log, m_d_skip, m_ssd_norm_g, m_w_ssd_proj, m_q_norm_g, m_w_uq, m_kv_norm_g, m_w_uk, m_w_uv, m_w_mla_proj, m_gate_bias, m_w_out, m_mix_post_g, m_xa_pre_g, m_mem_norm_g, m_w_xq, m_w_xk, m_w_xv, m_w_xo, m_xa_post_g, m_ffn2_pre_g, m_ffn2_w_gate, m_ffn2_w_up, m_ffn2_w_down, m_ffn2_post_g, v_ffn1_pre_g, v_ffn1_w_gate, v_ffn1_w_up, v_ffn1_w_down, v_ffn1_post_g, v_mix_pre_g, v_w_in, v_conv_w, v_conv_b, v_dt_bias, v_a_log, v_d_skip, v_ssd_norm_g, v_w_ssd_proj, v_q_norm_g, v_w_uq, v_kv_norm_g, v_w_uk, v_w_uv, v_w_mla_proj, v_gate_bias, v_w_out, v_mix_post_g, v_xa_pre_g, v_mem_norm_g, v_w_xq, v_w_xk, v_w_xv, v_w_xo, v_xa_post_g, v_ffn2_pre_g, v_ffn2_w_gate, v_ffn2_w_up, v_ffn2_w_down, v_ffn2_post_g):
    given = dict(x=x, mem=mem, positions=positions, ffn1_pre_g=ffn1_pre_g, ffn1_w_gate=ffn1_w_gate, ffn1_w_up=ffn1_w_up, ffn1_w_down=ffn1_w_down, ffn1_post_g=ffn1_post_g, mix_pre_g=mix_pre_g, w_in=w_in, conv_w=conv_w, conv_b=conv_b, dt_bias=dt_bias, a_log=a_log, d_skip=d_skip, ssd_norm_g=ssd_norm_g, w_ssd_proj=w_ssd_proj, q_norm_g=q_norm_g, w_uq=w_uq, kv_norm_g=kv_norm_g, w_uk=w_uk, w_uv=w_uv, w_mla_proj=w_mla_proj, gate_bias=gate_bias, w_out=w_out, mix_post_g=mix_post_g, xa_pre_g=xa_pre_g, mem_norm_g=mem_norm_g, w_xq=w_xq, w_xk=w_xk, w_xv=w_xv, w_xo=w_xo, xa_post_g=xa_post_g, ffn2_pre_g=ffn2_pre_g, ffn2_w_gate=ffn2_w_gate, ffn2_w_up=ffn2_w_up, ffn2_w_down=ffn2_w_down, ffn2_post_g=ffn2_post_g, loss_target=loss_target, m_ffn1_pre_g=m_ffn1_pre_g, m_ffn1_w_gate=m_ffn1_w_gate, m_ffn1_w_up=m_ffn1_w_up, m_ffn1_w_down=m_ffn1_w_down, m_ffn1_post_g=m_ffn1_post_g, m_mix_pre_g=m_mix_pre_g, m_w_in=m_w_in, m_conv_w=m_conv_w, m_conv_b=m_conv_b, m_dt_bias=m_dt_bias, m_a_log=m_a_log, m_d_skip=m_d_skip, m_ssd_norm_g=m_ssd_norm_g, m_w_ssd_proj=m_w_ssd_proj, m_q_norm_g=m_q_norm_g, m_w_uq=m_w_uq, m_kv_norm_g=m_kv_norm_g, m_w_uk=m_w_uk, m_w_uv=m_w_uv, m_w_mla_proj=m_w_mla_proj, m_gate_bias=m_gate_bias, m_w_out=m_w_out, m_mix_post_g=m_mix_post_g, m_xa_pre_g=m_xa_pre_g, m_mem_norm_g=m_mem_norm_g, m_w_xq=m_w_xq, m_w_xk=m_w_xk, m_w_xv=m_w_xv, m_w_xo=m_w_xo, m_xa_post_g=m_xa_post_g, m_ffn2_pre_g=m_ffn2_pre_g, m_ffn2_w_gate=m_ffn2_w_gate, m_ffn2_w_up=m_ffn2_w_up, m_ffn2_w_down=m_ffn2_w_down, m_ffn2_post_g=m_ffn2_post_g, v_ffn1_pre_g=v_ffn1_pre_g, v_ffn1_w_gate=v_ffn1_w_gate, v_ffn1_w_up=v_ffn1_w_up, v_ffn1_w_down=v_ffn1_w_down, v_ffn1_post_g=v_ffn1_post_g, v_mix_pre_g=v_mix_pre_g, v_w_in=v_w_in, v_conv_w=v_conv_w, v_conv_b=v_conv_b, v_dt_bias=v_dt_bias, v_a_log=v_a_log, v_d_skip=v_d_skip, v_ssd_norm_g=v_ssd_norm_g, v_w_ssd_proj=v_w_ssd_proj, v_q_norm_g=v_q_norm_g, v_w_uq=v_w_uq, v_kv_norm_g=v_kv_norm_g, v_w_uk=v_w_uk, v_w_uv=v_w_uv, v_w_mla_proj=v_w_mla_proj, v_gate_bias=v_gate_bias, v_w_out=v_w_out, v_mix_post_g=v_mix_post_g, v_xa_pre_g=v_xa_pre_g, v_mem_norm_g=v_mem_norm_g, v_w_xq=v_w_xq, v_w_xk=v_w_xk, v_w_xv=v_w_xv, v_w_xo=v_w_xo, v_xa_post_g=v_xa_post_g, v_ffn2_pre_g=v_ffn2_pre_g, v_ffn2_w_gate=v_ffn2_w_gate, v_ffn2_w_up=v_ffn2_w_up, v_ffn2_w_down=v_ffn2_w_down, v_ffn2_post_g=v_ffn2_post_g)
    weights = {n: given[n] for n in TWIN_WEIGHTS}
    shared = {n: given[n] for n in SHARED_INPUTS}
    per_example = {n: given[n] for n in ['x', 'mem', 'positions']}
    grad_fn = _jax.value_and_grad(_loss, argnums=(0, 1))

    def one_microbatch(ex, loss_target):
        ex = dict(ex)
        diff = ex.pop(TWIN_DIFF_INPUT)
        return grad_fn(weights, diff, {**shared, **ex}, loss_target)

    if N_MICROBATCH == 1:
        loss, (grad_w, grad_x) = one_microbatch(per_example, given["loss_target"])
    else:
        def body(carry, xs):
            loss_sum, grad_sum = carry
            l_k, (gw_k, gx_k) = one_microbatch(xs[0], xs[1])
            with _jax.named_scope("update"):
                return (loss_sum + l_k, _jax.tree.map(_jnp.add, grad_sum, gw_k)), gx_k

        init = (_jnp.zeros((), _jnp.float32), _jax.tree.map(_jnp.zeros_like, weights))
        (loss, grad_w), grad_x = _jax.lax.scan(body, init, (per_example, given["loss_target"]))
    with _jax.named_scope("update"):
        delta_w, new_m, new_v = {}, {}, {}
        for n in TWIN_WEIGHTS:
            delta_w[n], new_m[n], new_v[n] = _adamw(weights[n], grad_w[n], given["m_" + n], given["v_" + n])
    return (loss, grad_x, *[grad_w[n] for n in TWIN_WEIGHTS], *[delta_w[n] for n in TWIN_WEIGHTS],
            *[new_m[n] for n in TWIN_WEIGHTS], *[new_v[n] for n in TWIN_WEIGHTS])
```

```python
import functools
import math

import jax
import jax.numpy as jnp
from jax import lax
from jax.experimental import pallas as pl
from jax.experimental.pallas import tpu as pltpu

F32, BF16 = jnp.float32, jnp.bfloat16
HI = lax.Precision.HIGHEST
MESH = pl.DeviceIdType.MESH
N_DEV = 8

D = 1024
DFF = 2816
SSD_H, SSD_P, SSD_G, SSD_N, SSD_L = 16, 64, 2, 128, 128
SSD_INNER = SSD_H * SSD_P
CONV_K, CONV_CH = 4, 1536
MLA_H, QR, KVR, NOPE, ROPE, VD = 16, 384, 256, 64, 32, 64
QK = NOPE + ROPE
ROPE_THETA = 10000.0
XA_H, XA_D = 4, 256
EPS = 1e-6
FFN_RES = 0.5
LR, B1, B2, AEPS, WD, STEP = 0.001, 0.9, 0.999, 1e-08, 0.01, 10

VMEM_LIMIT = 56 * 2**20


def _cp(*sem):
    return pltpu.CompilerParams(dimension_semantics=sem, vmem_limit_bytes=VMEM_LIMIT)


def _sigmoid(x):
    return 1.0 / (1.0 + jnp.exp(-x))


def _softplus(x):
    return jnp.where(x > 20.0, x, jnp.log(1.0 + jnp.exp(jnp.minimum(x, 20.0))))


def _dot(a, b, dims="nn", hi=False):
    ca = 0 if dims[0] == "t" else 1
    cb = 1 if dims[1] == "t" else 0
    if hi:
        return lax.dot_general(a, b, (((ca,), (cb,)), ((), ())), precision=HI, preferred_element_type=F32)
    return lax.dot_general(a.astype(BF16), b.astype(BF16), (((ca,), (cb,)), ((), ())), preferred_element_type=F32)


def _ssd_common(dtr, dtb, alog):
    L = dtr.shape[0]
    dt = _softplus(dtr + dtb)
    a = -jnp.exp(alog)
    adt = dt * a
    r = lax.broadcasted_iota(jnp.int32, (L, L), 0)
    c = lax.broadcasted_iota(jnp.int32, (L, L), 1)
    lower = r >= c
    tri = lower.astype(F32)
    cs = _dot(tri, adt, "nn", hi=True)
    cs_t = _dot(adt, tri, "tt", hi=True)
    return dt, a, cs, cs_t, lower


def _head_expand():
    hh = lax.broadcasted_iota(jnp.int32, (SSD_H, SSD_INNER), 0)
    jj = lax.broadcasted_iota(jnp.int32, (SSD_H, SSD_INNER), 1)
    return ((jj >= hh * SSD_P) & (jj < hh * SSD_P + SSD_P)).astype(F32)


def _head_reduce():
    hh = lax.broadcasted_iota(jnp.int32, (SSD_INNER, SSD_H), 1)
    jj = lax.broadcasted_iota(jnp.int32, (SSD_INNER, SSD_H), 0)
    return ((jj >= hh * SSD_P) & (jj < hh * SSD_P + SSD_P)).astype(F32)


def ssd_fwd(xbc, dtr, dtb, alog, dsk, nseq):
    T = xbc.shape[0]
    S = T // nseq
    C = S // SSD_L
    L = SSD_L
    NP = SSD_H // 2

    def body(x_ref, b_ref, c_ref, dtr_ref, dtb_ref, alog_ref, dsk_ref, y_ref, prev_ref, st_ref):
        ci = pl.program_id(1)

        @pl.when(ci == 0)
        def _():
            st_ref[...] = jnp.zeros_like(st_ref)

        dt, a, cs, cs_t, lower = _ssd_common(dtr_ref[...], dtb_ref[...], alog_ref[...])
        E = _head_expand()
        X = x_ref[...]
        dt_e = _dot(dt, E, hi=True)
        cs_e = _dot(cs, E, hi=True)
        csl_e = cs_e[L - 1:L, :]
        Xd = X * dt_e
        Xf = Xd * jnp.exp(csl_e - cs_e)
        e_e = jnp.exp(cs_e)
        y_ref[...] = _dot(dsk_ref[...], E, hi=True) * X
        lane = lax.broadcasted_iota(jnp.int32, (1, 2 * SSD_P), 1)
        rowp = lax.broadcasted_iota(jnp.int32, (2 * SSD_P, 1), 0)
        for g in range(SSD_G):
            Bg = b_ref[:, g * SSD_N:(g + 1) * SSD_N]
            Cg = c_ref[:, g * SSD_N:(g + 1) * SSD_N]
            cb = _dot(Cg, Bg, "nt")
            for pp in range(NP // SSD_G):
                p = g * (NP // SSD_G) + pp
                sl = slice(p * 2 * SSD_P, (p + 1) * 2 * SSD_P)
                Xd_p = Xd[:, sl]
                yd = jnp.zeros((L, 2 * SSD_P), F32)
                for q in range(2):
                    h = 2 * p + q
                    m = jnp.where(lower, jnp.exp(jnp.minimum(cs[:, h:h + 1] - cs_t[h:h + 1, :], 0.0)), 0.0)
                    mask = (lane >= q * SSD_P) & (lane < (q + 1) * SSD_P)
                    yd = yd + _dot(cb * m, jnp.where(mask, Xd_p, 0.0))
                S0 = st_ref[p]
                prev_ref[0, 0, p] = S0
                z = _dot(Cg, S0, "nt")
                y_ref[:, sl] += yd + z * e_e[:, sl]
                h0 = 2 * p
                dec = jnp.where(rowp < SSD_P, jnp.exp(cs[L - 1:L, h0:h0 + 1]), jnp.exp(cs[L - 1:L, h0 + 1:h0 + 2]))
                st_ref[p] = S0 * dec + _dot(Xf[:, sl], Bg, "tn")

    row = lambda b, c: (b * C + c, 0)
    return pl.pallas_call(
        body, grid=(nseq, C), name="ssd_fwd",
        in_specs=[pl.BlockSpec((L, SSD_INNER), row),
                  pl.BlockSpec((L, SSD_G * SSD_N), lambda b, c: (b * C + c, SSD_INNER // (SSD_G * SSD_N))),
                  pl.BlockSpec((L, SSD_G * SSD_N), lambda b, c: (b * C + c, SSD_INNER // (SSD_G * SSD_N) + 1)),
                  pl.BlockSpec((L, SSD_H), row),
                  pl.BlockSpec((1, SSD_H), lambda b, c: (0, 0)),
                  pl.BlockSpec((1, SSD_H), lambda b, c: (0, 0)),
                  pl.BlockSpec((1, SSD_H), lambda b, c: (0, 0))],
        out_specs=[pl.BlockSpec((L, SSD_INNER), row),
                   pl.BlockSpec((1, 1, NP, 2 * SSD_P, SSD_N), lambda b, c: (b, c, 0, 0, 0))],
        out_shape=[jax.ShapeDtypeStruct((T, SSD_INNER), F32),
                   jax.ShapeDtypeStruct((nseq, C, NP, 2 * SSD_P, SSD_N), F32)],
        scratch_shapes=[pltpu.VMEM((NP, 2 * SSD_P, SSD_N), F32)],
        compiler_params=_cp("parallel", "arbitrary"),
    )(xbc, xbc, xbc, dtr, dtb, alog, dsk)


def ssd_bwd(xbc, dtr, dtb, alog, dsk, prev, dy, nseq):
    T = xbc.shape[0]
    S = T // nseq
    C = S // SSD_L
    L = SSD_L
    NP = SSD_H // 2

    def body(x_ref, b_ref, c_ref, dtr_ref, dtb_ref, alog_ref, dsk_ref, prev_ref, dy_ref,
             dxbc_ref, ddtr_ref, ddtb_ref, dalog_ref, ddsk_ref, ds_ref):
        bi = pl.program_id(0)
        ci = pl.program_id(1)

        @pl.when(ci == 0)
        def _():
            ds_ref[...] = jnp.zeros_like(ds_ref)

        @pl.when((ci == 0) & (bi == 0))
        def _():
            ddtb_ref[...] = jnp.zeros_like(ddtb_ref)
            dalog_ref[...] = jnp.zeros_like(dalog_ref)
            ddsk_ref[...] = jnp.zeros_like(ddsk_ref)

        dtr = dtr_ref[...]
        dtb = dtb_ref[...]
        dt, a, cs, cs_t, lower = _ssd_common(dtr, dtb, alog_ref[...])
        upper = lax.broadcasted_iota(jnp.int32, (L, L), 1) >= lax.broadcasted_iota(jnp.int32, (L, L), 0)
        E = _head_expand()
        ET = _head_reduce()
        X = x_ref[...]
        dY = dy_ref[...]
        dt_e = _dot(dt, E, hi=True)
        cs_e = _dot(cs, E, hi=True)
        csl_e = cs_e[L - 1:L, :]
        f_e = jnp.exp(csl_e - cs_e)
        e_e = jnp.exp(cs_e)
        dsk_e = _dot(dsk_ref[...], E, hi=True)
        Xd = X * dt_e
        Xf = Xd * f_e
        lane = lax.broadcasted_iota(jnp.int32, (1, 2 * SSD_P), 1)
        rowp = lax.broadcasted_iota(jnp.int32, (2 * SSD_P, 1), 0)
        hsel = lax.broadcasted_iota(jnp.int32, (1, SSD_H), 1)
        dcs = jnp.zeros((L, SSD_H), F32)
        dcsl = jnp.zeros((1, SSD_H), F32)
        for g in range(SSD_G):
            Bg = b_ref[:, g * SSD_N:(g + 1) * SSD_N]
            Cg = c_ref[:, g * SSD_N:(g + 1) * SSD_N]
            cb = _dot(Cg, Bg, "nt")
            cbt = _dot(Bg, Cg, "nt")
            dB = jnp.zeros((L, SSD_N), F32)
            dC = jnp.zeros((L, SSD_N), F32)
            for pp in range(NP // SSD_G):
                p = g * (NP // SSD_G) + pp
                sl = slice(p * 2 * SSD_P, (p + 1) * 2 * SSD_P)
                Xd_p = Xd[:, sl]
                dY_p = dY[:, sl]
                dXd_p = jnp.zeros((L, 2 * SSD_P), F32)
                for q in range(2):
                    h = 2 * p + q
                    mask = (lane >= q * SSD_P) & (lane < (q + 1) * SSD_P)
                    col = cs[:, h:h + 1]
                    rw = cs_t[h:h + 1, :]
                    m = jnp.where(lower, jnp.exp(jnp.minimum(col - rw, 0.0)), 0.0)
                    mt = jnp.where(upper, jnp.exp(jnp.minimum(rw - col, 0.0)), 0.0)
                    dYm = jnp.where(mask, dY_p, 0.0)
                    dW = _dot(dYm, Xd_p, "nt")
                    dWt = _dot(Xd_p, dYm, "nt")
                    w = cb * m
                    wt = cbt * mt
                    dC = dC + _dot(dW * m, Bg)
                    dB = dB + _dot(dWt * mt, Cg)
                    dXd_p = dXd_p + jnp.where(mask, _dot(wt, dY_p), 0.0)
                    qcol = jnp.sum(dW * w, axis=1, keepdims=True) - jnp.sum(dWt * wt, axis=1, keepdims=True)
                    dcs = dcs + qcol * (hsel == h).astype(F32)
                S0 = prev_ref[0, 0, p]
                dSn = ds_ref[p]
                dZ = dY_p * e_e[:, sl]
                dC = dC + _dot(dZ, S0)
                h0 = 2 * p
                el0 = jnp.exp(cs[L - 1:L, h0:h0 + 1])
                el1 = jnp.exp(cs[L - 1:L, h0 + 1:h0 + 2])
                dec = jnp.where(rowp < SSD_P, el0, el1)
                ds_ref[p] = dSn * dec + _dot(dZ, Cg, "tn")
                dXf_p = _dot(Bg, dSn, "nt")
                dB = dB + _dot(Xf[:, sl], dSn)
                rs = jnp.sum(dSn * S0, axis=1, keepdims=True)
                s0 = jnp.sum(jnp.where(rowp < SSD_P, rs, 0.0), axis=0, keepdims=True) * el0
                s1 = jnp.sum(jnp.where(rowp >= SSD_P, rs, 0.0), axis=0, keepdims=True) * el1
                dcsl = dcsl + s0 * (hsel == h0).astype(F32) + s1 * (hsel == h0 + 1).astype(F32)
                dxbc_ref[:, sl] = dXd_p
                y_off = _dot(Cg, S0, "nt") * e_e[:, sl]
                t1 = dY_p * y_off - dXf_p * Xf[:, sl]
                r1 = jnp.where(lane < SSD_P, t1, 0.0)
                c0 = jnp.sum(r1, axis=1, keepdims=True)
                c1 = jnp.sum(t1 - r1, axis=1, keepdims=True)
                dcs = dcs + c0 * (hsel == h0).astype(F32) + c1 * (hsel == h0 + 1).astype(F32)
                t2 = dXf_p * Xf[:, sl]
                r2 = jnp.where(lane < SSD_P, t2, 0.0)
                dcsl = dcsl + jnp.sum(r2, keepdims=True) * (hsel == h0).astype(F32) \
                    + jnp.sum(t2 - r2, keepdims=True) * (hsel == h0 + 1).astype(F32)
                dxbc_ref[:, sl] += dXf_p * f_e[:, sl]
            dxbc_ref[:, SSD_INNER + g * SSD_N:SSD_INNER + (g + 1) * SSD_N] = dB
            dxbc_ref[:, SSD_INNER + (SSD_G + g) * SSD_N:SSD_INNER + (SSD_G + g + 1) * SSD_N] = dC
        dXd = dxbc_ref[:, 0:SSD_INNER]
        dxbc_ref[:, 0:SSD_INNER] = dXd * dt_e + dsk_e * dY
        rowl = lax.broadcasted_iota(jnp.int32, (L, 1), 0)
        dcs = dcs + jnp.where(rowl == L - 1, dcsl, 0.0)
        dalpha = _dot(upper.astype(F32), dcs, hi=True)
        ddt = _dot(dXd * X, ET, hi=True) + dalpha * a
        dalog_ref[...] += jnp.sum(dalpha * dt, axis=0, keepdims=True) * a
        ddtr = ddt * _sigmoid(dtr + dtb)
        ddtr_ref[...] = ddtr
        ddtb_ref[...] += jnp.sum(ddtr, axis=0, keepdims=True)
        ddsk_ref[...] += jnp.sum(_dot(dY * X, ET, hi=True), axis=0, keepdims=True)

    rowr = lambda b, c: (b * C + (C - 1 - c), 0)
    small = pl.BlockSpec((1, SSD_H), lambda b, c: (0, 0))
    return pl.pallas_call(
        body, grid=(nseq, C), name="ssd_bwd",
        in_specs=[pl.BlockSpec((L, SSD_INNER), rowr),
                  pl.BlockSpec((L, SSD_G * SSD_N), lambda b, c: (b * C + (C - 1 - c), SSD_INNER // (SSD_G * SSD_N))),
                  pl.BlockSpec((L, SSD_G * SSD_N), lambda b, c: (b * C + (C - 1 - c), SSD_INNER // (SSD_G * SSD_N) + 1)),
                  pl.BlockSpec((L, SSD_H), rowr), small, small, small,
                  pl.BlockSpec((1, 1, NP, 2 * SSD_P, SSD_N), lambda b, c: (b, C - 1 - c, 0, 0, 0)),
                  pl.BlockSpec((L, SSD_INNER), rowr)],
        out_specs=[pl.BlockSpec((L, CONV_CH), rowr), pl.BlockSpec((L, SSD_H), rowr), small, small, small],
        out_shape=[jax.ShapeDtypeStruct((T, CONV_CH), F32), jax.ShapeDtypeStruct((T, SSD_H), F32),
                   jax.ShapeDtypeStruct((1, SSD_H), F32), jax.ShapeDtypeStruct((1, SSD_H), F32),
                   jax.ShapeDtypeStruct((1, SSD_H), F32)],
        scratch_shapes=[pltpu.VMEM((NP, 2 * SSD_P, SSD_N), F32)],
        compiler_params=_cp("arbitrary", "arbitrary"),
    )(xbc, xbc, xbc, dtr, dtb, alog, dsk, prev, dy)


ATT_BLK = 256


def attn_fwd(q, k, v):
    B, H, S, dq = q.shape
    dv = v.shape[-1]
    tq = min(ATT_BLK, S)
    nq = S // tq
    scale = QK ** -0.5

    def body(q_ref, k_ref, v_ref, o_ref, lse_ref):
        qi = pl.program_id(2)
        qb = q_ref[0, 0]
        row = lax.broadcasted_iota(jnp.int32, (tq, tq), 0)
        col = lax.broadcasted_iota(jnp.int32, (tq, tq), 1)

        def step(j, carry):
            m, l, acc = carry
            off = pl.multiple_of(j * tq, tq)
            kb = k_ref[0, 0, pl.ds(off, tq), :]
            vb = v_ref[0, 0, pl.ds(off, tq), :]
            s = _dot(qb, kb, "nt") * scale
            s = jnp.where((j * tq + col) <= (qi * tq + row), s, -1e30)
            m_new = jnp.maximum(m, jnp.max(s, axis=1, keepdims=True))
            p = jnp.exp(s - m_new)
            corr = jnp.exp(m - m_new)
            return m_new, l * corr + jnp.sum(p, axis=1, keepdims=True), acc * corr + _dot(p, vb)

        m, l, acc = lax.fori_loop(0, qi + 1, step, (jnp.full((tq, 1), -1e30, F32), jnp.zeros((tq, 1), F32),
                                                    jnp.zeros((tq, dv), F32)))
        o_ref[0, 0] = (acc / l).astype(o_ref.dtype)
        lse_ref[0, 0] = m + jnp.log(l)

    full = lambda d: pl.BlockSpec((1, 1, S, d), lambda b, h, i: (b, h, 0, 0))
    blk = lambda d: pl.BlockSpec((1, 1, tq, d), lambda b, h, i: (b, h, i, 0))
    return pl.pallas_call(
        body, grid=(B, H, nq), name="attn_fwd",
        in_specs=[blk(dq), full(dq), full(dv)],
        out_specs=[blk(dv), blk(1)],
        out_shape=[jax.ShapeDtypeStruct((B, H, S, dv), BF16), jax.ShapeDtypeStruct((B, H, S, 1), F32)],
        compiler_params=_cp("parallel", "parallel", "arbitrary"),
    )(q, k, v)


def attn_bwd(q, k, v, o, lse, do):
    B, H, S, dq_ = q.shape
    dv_ = v.shape[-1]
    tk = min(ATT_BLK, S)
    nk = S // tk
    scale = QK ** -0.5

    def body(q_ref, k_ref, v_ref, o_ref, lse_ref, do_ref, dq_ref, dk_ref, dv_ref):
        kj = pl.program_id(2)

        @pl.when(kj == 0)
        def _():
            dq_ref[...] = jnp.zeros_like(dq_ref)

        kb = k_ref[0, 0]
        vb = v_ref[0, 0]
        row = lax.broadcasted_iota(jnp.int32, (tk, tk), 0)
        col = lax.broadcasted_iota(jnp.int32, (tk, tk), 1)

        def step(i, carry):
            dk, dv = carry
            off = pl.multiple_of(i * tk, tk)
            qb = q_ref[0, 0, pl.ds(off, tk), :]
            dob = do_ref[0, 0, pl.ds(off, tk), :]
            ob = o_ref[0, 0, pl.ds(off, tk), :]
            ls = lse_ref[0, 0, pl.ds(off, tk), :]
            s = _dot(qb, kb, "nt") * scale
            p = jnp.where((kj * tk + col) <= (i * tk + row), jnp.exp(s - ls), 0.0)
            dv = dv + _dot(p, dob, "tn")
            dp = _dot(dob, vb, "nt")
            delta = jnp.sum(dob.astype(F32) * ob.astype(F32), axis=1, keepdims=True)
            ds = p * (dp - delta) * scale
            dk = dk + _dot(ds, qb, "tn")
            dq_ref[0, 0, pl.ds(off, tk), :] += _dot(ds, kb)
            return dk, dv

        dk, dv = lax.fori_loop(kj, nk, step, (jnp.zeros((tk, dq_), F32), jnp.zeros((tk, dv_), F32)))
        dk_ref[0, 0] = dk
        dv_ref[0, 0] = dv.astype(dv_ref.dtype)

    full = lambda d: pl.BlockSpec((1, 1, S, d), lambda b, h, j: (b, h, 0, 0))
    blk = lambda d: pl.BlockSpec((1, 1, tk, d), lambda b, h, j: (b, h, j, 0))
    return pl.pallas_call(
        body, grid=(B, H, nk), name="attn_bwd",
        in_specs=[full(dq_), blk(dq_), blk(dv_), full(dv_), full(1), full(dv_)],
        out_specs=[full(dq_), blk(dq_), blk(dv_)],
        out_shape=[jax.ShapeDtypeStruct((B, H, S, dq_), F32), jax.ShapeDtypeStruct((B, H, S, dq_), F32),
                   jax.ShapeDtypeStruct((B, H, S, dv_), BF16)],
        compiler_params=_cp("parallel", "parallel", "arbitrary"),
    )(q, k, v, o, lse, do)


XA_BLK = 512


def xattn_fwd(q, k, v, nseq):
    T = q.shape[0]
    S = T // nseq
    M = k.shape[0] // nseq
    tq = min(XA_BLK, S)
    nq = S // tq
    scale = XA_D ** -0.5

    def body(q_ref, k_ref, v_ref, o_ref):
        s = _dot(q_ref[...], k_ref[...], "nt") * scale
        p = jnp.exp(s - jnp.max(s, axis=1, keepdims=True))
        p = p / jnp.sum(p, axis=1, keepdims=True)
        o_ref[...] = _dot(p, v_ref[...]).astype(o_ref.dtype)

    qs = pl.BlockSpec((tq, XA_D), lambda b, h, i: (b * nq + i, h))
    ks = pl.BlockSpec((M, XA_D), lambda b, h, i: (b, h))
    return pl.pallas_call(
        body, grid=(nseq, XA_H, nq), name="xattn_fwd", in_specs=[qs, ks, ks], out_specs=qs,
        out_shape=jax.ShapeDtypeStruct((T, XA_H * XA_D), BF16),
        compiler_params=_cp("parallel", "parallel", "parallel"),
    )(q, k, v)


def xattn_bwd(q, k, v, do, nseq):
    T = q.shape[0]
    S = T // nseq
    M = k.shape[0] // nseq
    tq = min(XA_BLK, S)
    nq = S // tq
    scale = XA_D ** -0.5

    def body(q_ref, k_ref, v_ref, do_ref, dq_ref, dk_ref, dv_ref):
        @pl.when(pl.program_id(2) == 0)
        def _():
            dk_ref[...] = jnp.zeros_like(dk_ref)
            dv_ref[...] = jnp.zeros_like(dv_ref)

        qb, kb, vb, dob = q_ref[...], k_ref[...], v_ref[...], do_ref[...]
        s = _dot(qb, kb, "nt") * scale
        p = jnp.exp(s - jnp.max(s, axis=1, keepdims=True))
        p = p / jnp.sum(p, axis=1, keepdims=True)
        dp = _dot(dob, vb, "nt")
        ds = p * (dp - jnp.sum(dp * p, axis=1, keepdims=True)) * scale
        dq_ref[...] = _dot(ds, kb).astype(dq_ref.dtype)
        dk_ref[...] += _dot(ds, qb, "tn")
        dv_ref[...] += _dot(p, dob, "tn")

    qs = pl.BlockSpec((tq, XA_D), lambda b, h, i: (b * nq + i, h))
    ks = pl.BlockSpec((M, XA_D), lambda b, h, i: (b, h))
    return pl.pallas_call(
        body, grid=(nseq, XA_H, nq), name="xattn_bwd", in_specs=[qs, ks, ks, qs], out_specs=[qs, ks, ks],
        out_shape=[jax.ShapeDtypeStruct((T, XA_H * XA_D), BF16), jax.ShapeDtypeStruct(k.shape, F32),
                   jax.ShapeDtypeStruct(k.shape, F32)],
        compiler_params=_cp("parallel", "parallel", "arbitrary"),
    )(q, k, v, do)


CONV_BLK = 256


def _shift_down(x, s, rows):
    if s == 0:
        return x
    return jnp.where(rows >= s, pltpu.roll(x, s, axis=0), 0.0)


def _shift_up(x, s, rows):
    if s == 0:
        return x
    S = x.shape[0]
    return jnp.where(rows < S - s, pltpu.roll(x, S - s, axis=0), 0.0)


def conv_fwd(x, w, b, nseq):
    T, CH = x.shape
    S = T // nseq

    def body(x_ref, w_ref, b_ref, o_ref):
        xv = x_ref[...]
        rows = lax.broadcasted_iota(jnp.int32, (S, 1), 0)
        c = jnp.zeros_like(xv) + b_ref[...]
        for kk in range(CONV_K):
            c = c + w_ref[kk:kk + 1, :] * _shift_down(xv, CONV_K - 1 - kk, rows)
        o_ref[...] = c * _sigmoid(c)

    xs = pl.BlockSpec((S, CONV_BLK), lambda j, bb: (bb, j))
    return pl.pallas_call(
        body, grid=(CH // CONV_BLK, nseq), name="conv_fwd",
        in_specs=[xs, pl.BlockSpec((CONV_K, CONV_BLK), lambda j, bb: (0, j)), pl.BlockSpec((1, CONV_BLK), lambda j, bb: (0, j))],
        out_specs=xs, out_shape=jax.ShapeDtypeStruct((T, CH), F32),
        compiler_params=_cp("parallel", "parallel"),
    )(x, w, b)


def conv_bwd(x, w, b, dout, nseq):
    T, CH = x.shape
    S = T // nseq

    def body(x_ref, w_ref, b_ref, do_ref, dx_ref, dw_ref, db_ref):
        @pl.when(pl.program_id(1) == 0)
        def _():
            dw_ref[...] = jnp.zeros_like(dw_ref)
            db_ref[...] = jnp.zeros_like(db_ref)

        xv = x_ref[...]
        rows = lax.broadcasted_iota(jnp.int32, (S, 1), 0)
        c = jnp.zeros_like(xv) + b_ref[...]
        sh = [_shift_down(xv, CONV_K - 1 - kk, rows) for kk in range(CONV_K)]
        for kk in range(CONV_K):
            c = c + w_ref[kk:kk + 1, :] * sh[kk]
        sg = _sigmoid(c)
        dc = do_ref[...] * sg * (1.0 + c * (1.0 - sg))
        dx = jnp.zeros_like(xv)
        for kk in range(CONV_K):
            dx = dx + w_ref[kk:kk + 1, :] * _shift_up(dc, CONV_K - 1 - kk, rows)
            dw_ref[kk:kk + 1, :] += jnp.sum(dc * sh[kk], axis=0, keepdims=True)
        dx_ref[...] = dx.astype(dx_ref.dtype)
        db_ref[...] += jnp.sum(dc, axis=0, keepdims=True)

    xs = pl.BlockSpec((S, CONV_BLK), lambda j, bb: (bb, j))
    ws = pl.BlockSpec((CONV_K, CONV_BLK), lambda j, bb: (0, j))
    bs = pl.BlockSpec((1, CONV_BLK), lambda j, bb: (0, j))
    return pl.pallas_call(
        body, grid=(CH // CONV_BLK, nseq), name="conv_bwd",
        in_specs=[xs, ws, bs, xs], out_specs=[xs, ws, bs],
        out_shape=[jax.ShapeDtypeStruct((T, CH), BF16), jax.ShapeDtypeStruct((CONV_K, CH), F32),
                   jax.ShapeDtypeStruct((1, CH), F32)],
        compiler_params=_cp("parallel", "arbitrary"),
    )(x, w, b, dout)


def _dims(a, b, mode):
    M = a.shape[1] if mode[0] == "t" else a.shape[0]
    K = a.shape[0] if mode[0] == "t" else a.shape[1]
    N = b.shape[0] if mode[1] == "t" else b.shape[1]
    return M, K, N


def _tile(dim, prefs):
    for p in prefs:
        if dim % p == 0:
            return p
    return dim


def mm(groups, out_dtypes, name, tm=None, tn=None, tk=None, epi=None, extras=()):
    a0, b0, m0 = groups[0][0]
    M, K0, N = _dims(a0, b0, m0)
    tm = tm or _tile(M, (1024, 512, 256, 128))
    tn = tn or _tile(N, (512, 256, 128))
    flat = [p for g in groups for p in g]
    nk = 1 if tk is None else K0 // tk
    in_specs, args = [], []
    for a, b, mode in flat:
        _, K, _ = _dims(a, b, mode)
        kb = K if tk is None else tk
        in_specs.append(pl.BlockSpec((kb, tm), lambda i, j, k: (k, i)) if mode[0] == "t"
                        else pl.BlockSpec((tm, kb), lambda i, j, k: (i, k)))
        in_specs.append(pl.BlockSpec((tn, kb), lambda i, j, k: (j, k)) if mode[1] == "t"
                        else pl.BlockSpec((kb, tn), lambda i, j, k: (k, j)))
        args += [a, b]
    for e in extras:
        in_specs.append(pl.BlockSpec((tm, tn), lambda i, j, k: (i, j)))
        args.append(e)
    n_in = len(args)
    n_out = len(out_dtypes)
    ng = len(groups)
    sizes = [len(g) for g in groups]

    def body(*refs):
        ins, outs, accs = refs[:n_in], refs[n_in:n_in + n_out], refs[n_in + n_out:]
        kk = pl.program_id(2)
        vals, pos = [], 0
        for gi in range(ng):
            acc = None
            for _ in range(sizes[gi]):
                mode = flat[pos // 2][2]
                d = _dot(ins[pos][...], ins[pos + 1][...], mode)
                acc = d if acc is None else acc + d
                pos += 2
            vals.append(acc)
        ex = [r[...] for r in ins[2 * len(flat):]]

        def finish(accv):
            res = epi(accv, ex) if epi is not None else tuple(accv)
            for o, r in zip(outs, res):
                o[...] = r.astype(o.dtype)

        if nk == 1:
            finish(vals)
        else:
            @pl.when(kk == 0)
            def _():
                for ar, vv in zip(accs, vals):
                    ar[...] = vv

            @pl.when(kk > 0)
            def _():
                for ar, vv in zip(accs, vals):
                    ar[...] += vv

            @pl.when(kk == nk - 1)
            def _():
                finish([ar[...] for ar in accs])

    return pl.pallas_call(
        body, grid=(M // tm, N // tn, nk), name=name, in_specs=in_specs,
        out_specs=[pl.BlockSpec((tm, tn), lambda i, j, k: (i, j)) for _ in out_dtypes],
        out_shape=[jax.ShapeDtypeStruct((M, N), dt) for dt in out_dtypes],
        scratch_shapes=[pltpu.VMEM((tm, tn), F32) for _ in range(ng if nk > 1 else 0)],
        compiler_params=_cp("parallel", "parallel", "arbitrary"),
    )(*args)


def mm1(a, b, mode, out_dtype, name, **kw):
    return mm([[(a, b, mode)]], [out_dtype], name, **kw)[0]


ROW_BLK = 256


def rowwise(fn, rows, consts, outs, accs, name, tb=ROW_BLK):
    T = rows[0].shape[0]
    tb = min(tb, T)
    n_r, n_c, n_o, n_a = len(rows), len(consts), len(outs), len(accs)

    def body(*refs):
        vals = [r[...] for r in refs[:n_r + n_c]]
        res = fn(*vals)
        o_refs = refs[n_r + n_c:n_r + n_c + n_o]
        a_refs = refs[n_r + n_c + n_o:]
        for o, r in zip(o_refs, res[:n_o]):
            o[...] = r.astype(o.dtype)
        if n_a:
            @pl.when(pl.program_id(0) == 0)
            def _():
                for ar in a_refs:
                    ar[...] = jnp.zeros_like(ar)
            for ar, r in zip(a_refs, res[n_o:]):
                ar[...] += r

    return pl.pallas_call(
        body, grid=(T // tb,), name=name,
        in_specs=[pl.BlockSpec((tb, r.shape[1]), lambda i: (i, 0)) for r in rows]
        + [pl.BlockSpec(c.shape, lambda i: (0, 0)) for c in consts],
        out_specs=[pl.BlockSpec((tb, d), lambda i: (i, 0)) for d, _ in outs]
        + [pl.BlockSpec(s, lambda i: (0, 0)) for s in accs],
        out_shape=[jax.ShapeDtypeStruct((T, d), dt) for d, dt in outs]
        + [jax.ShapeDtypeStruct(s, F32) for s in accs],
        compiler_params=_cp("arbitrary" if n_a else "parallel"),
    )(*rows, *consts)


def _rms_stats(x):
    r = lax.rsqrt(jnp.mean(x * x, axis=-1, keepdims=True) + EPS)
    return r, x * r


def _rms_bwd(x, g, dy):
    r, xn = _rms_stats(x)
    dyg = dy * g
    dx = r * (dyg - xn * jnp.mean(dyg * xn, axis=-1, keepdims=True))
    return dx, jnp.sum(dy * xn, axis=0, keepdims=True)


def rms_fwd(x, g, name):
    return rowwise(lambda xv, gv: (_rms_stats(xv)[1] * gv,), [x], [g], [(x.shape[1], BF16)], [], name)[0]


def rms_bwd(x, g, dy, name, resid=None, dx_dtype=F32):
    def fn(*v):
        if resid is None:
            xv, dyv, gv = v
            dx, dg = _rms_bwd(xv, gv, dyv)
        else:
            xv, dyv, rv, gv = v
            dx, dg = _rms_bwd(xv, gv, dyv)
            dx = dx + rv
        return dx, dg
    rows = [x, dy] + ([] if resid is None else [resid])
    return rowwise(fn, rows, [g], [(x.shape[1], dx_dtype)], [(1, x.shape[1])], name)


def resid_fwd(x, h, g, wgt, name):
    return rowwise(lambda xv, hv, gv: (xv + wgt * _rms_stats(hv)[1] * gv,), [x, h], [g], [(x.shape[1], F32)], [], name)[0]


def resid_bwd(h, g, dy, wgt, name):
    def fn(hv, dyv, gv):
        dx, dg = _rms_bwd(hv, gv, dyv)
        return wgt * dx, wgt * dg
    return rowwise(fn, [h, dy], [g], [(h.shape[1], BF16)], [(1, h.shape[1])], name)


def _silu_parts(g):
    s = _sigmoid(g)
    return g * s, s * (1.0 + g * (1.0 - s))


def gated_norm_fwd(y, z, g, name):
    W = SSD_INNER // SSD_G

    def fn(yv, zv, gv):
        yg = yv * _silu_parts(zv)[0]
        return (jnp.concatenate([_rms_stats(yg[:, i * W:(i + 1) * W])[1] for i in range(SSD_G)], axis=1) * gv,)
    return rowwise(fn, [y, z], [g], [(SSD_INNER, BF16)], [], name)[0]


def gated_norm_bwd(y, z, dyn, g, name):
    W = SSD_INNER // SSD_G

    def fn(yv, zv, dv, gv):
        sil, dsil = _silu_parts(zv)
        yg = yv * sil
        parts = [_rms_bwd(yg[:, i * W:(i + 1) * W], gv[:, i * W:(i + 1) * W], dv[:, i * W:(i + 1) * W]) for i in range(SSD_G)]
        dyg = jnp.concatenate([p[0] for p in parts], axis=1)
        dg = jnp.concatenate([p[1] for p in parts], axis=1)
        return dyg * sil, dyg * yv * dsil, dg
    return rowwise(fn, [y, z, dyn], [g], [(SSD_INNER, F32), (SSD_INNER, BF16)], [(1, SSD_INNER)], name)


def merge_fwd(gl, ys, ym, gb, name):
    def fn(glv, ysv, ymv, gbv):
        gt = _sigmoid(glv + gbv)
        return (gt[:, :D] * ysv + gt[:, D:] * ymv,)
    return rowwise(fn, [gl, ys, ym], [gb], [(D, BF16)], [], name)[0]


def merge_bwd(gl, ys, ym, dm, gb, name):
    def fn(glv, ysv, ymv, dmv, gbv):
        gt = _sigmoid(glv + gbv)
        gs, gm = gt[:, :D], gt[:, D:]
        dgl = jnp.concatenate([dmv * ysv * gs * (1.0 - gs), dmv * ymv * gm * (1.0 - gm)], axis=1)
        return dmv * gs, dmv * gm, dgl, jnp.sum(dgl, axis=0, keepdims=True)
    return rowwise(fn, [gl, ys, ym, dm], [gb], [(D, BF16), (D, BF16), (2 * D, BF16)], [(1, 2 * D)], name)


def rope_rot(x1, x2, cos, sin, name):
    fn = lambda a, b, c, s: (a * c - b * s, a * s + b * c)
    return rowwise(fn, [x1, x2, cos, sin], [], [(x1.shape[1], BF16), (x1.shape[1], BF16)], [], name)


def loss_head(y, tgt, name):
    def fn(yv, tv):
        d = yv - tv
        part = 0.5 * jnp.sum(jnp.sum(d * d, axis=1, keepdims=True), axis=0, keepdims=True) / D
        return d / D, jnp.broadcast_to(part, (1, 128))
    return rowwise(fn, [y, tgt], [], [(D, F32)], [(1, 128)], name)


def adamw(w, g, m, v, name):
    R, C = w.shape
    tb = _tile(R, (256, 128, 64, 32, 16, 8))

    def fn(wv, gv, mv, vv):
        mn = B1 * mv + (1.0 - B1) * gv
        vn = B2 * vv + (1.0 - B2) * (gv * gv)
        mh = mn / (1.0 - B1 ** STEP)
        vh = vn / (1.0 - B2 ** STEP)
        return -LR * (mh / (jnp.sqrt(vh) + AEPS) + WD * wv), mn, vn
    return rowwise(fn, [w, g, m, v], [], [(C, F32)] * 3, [], name, tb=tb)


def _me():
    return lax.axis_index("x"), lax.axis_index("y"), lax.axis_index("c")


def _dev_index():
    x, y, c = _me()
    return 4 * x + 2 * y + c


HBM_SPEC = pl.BlockSpec(memory_space=pl.ANY)


def allgather_rows(shard):
    R, C = shard.shape

    def body(x_ref, out_ref, send_sems, recv_sems, local_sem):
        x, y, c = _me()
        me, sibling = (x, y, c), (x, y, 1 - c)
        chips = [(1 - x, y), (x, 1 - y), (1 - x, 1 - y)]

        def slot(px, py, pc):
            return out_ref.at[4 * px + 2 * py + pc]

        def copy(k, block, to, src=None):
            return pltpu.make_async_remote_copy(
                src_ref=slot(*block) if src is None else src, dst_ref=slot(*block),
                send_sem=send_sems.at[k], recv_sem=recv_sems.at[k], device_id=to, device_id_type=MESH)

        mine = pltpu.make_async_copy(x_ref, slot(*me), local_sem)
        mine.start()
        first = [copy(0, me, sibling, src=x_ref)]
        first += [copy(1 + j, me, (*chip, c), src=x_ref) for j, chip in enumerate(chips)]
        for cp in first:
            cp.start()
        passed = [copy(4 + j, (*chip, c), sibling) for j, chip in enumerate(chips)]
        for j, chip in enumerate(chips):
            copy(1 + j, (*chip, c), me).wait_recv()
            passed[j].start()
        copy(0, sibling, me).wait_recv()
        for j, chip in enumerate(chips):
            copy(4 + j, (*chip, 1 - c), me).wait_recv()
        for cp in first + passed:
            cp.wait_send()
        mine.wait()

    return pl.pallas_call(
        body, name="allgather_weights", out_shape=jax.ShapeDtypeStruct((N_DEV, R, C), shard.dtype),
        in_specs=[HBM_SPEC], out_specs=HBM_SPEC,
        scratch_shapes=[pltpu.SemaphoreType.DMA((7,)), pltpu.SemaphoreType.DMA((7,)), pltpu.SemaphoreType.DMA],
    )(shard)


def exchange_grads(gp, sm):
    _, R, C = gp.shape

    def body(gp_ref, sm_ref, recv_ref, srecv_ref, send_sems, recv_sems, local_sems):
        x, y, c = _me()
        me = 4 * x + 2 * y + c
        l0 = pltpu.make_async_copy(gp_ref.at[me], recv_ref.at[me], local_sems.at[0])
        l1 = pltpu.make_async_copy(sm_ref, srecv_ref.at[me], local_sems.at[1])
        l0.start()
        l1.start()
        sends, recvs = [], []
        for k in range(1, N_DEV):
            px = 1 - x if k & 4 else x
            py = 1 - y if k & 2 else y
            pc = 1 - c if k & 1 else c
            peer = 4 * px + 2 * py + pc
            kw = dict(device_id=(px, py, pc), device_id_type=MESH)
            sends.append(pltpu.make_async_remote_copy(src_ref=gp_ref.at[peer], dst_ref=recv_ref.at[me],
                                                      send_sem=send_sems.at[k - 1], recv_sem=recv_sems.at[k - 1], **kw))
            sends.append(pltpu.make_async_remote_copy(src_ref=sm_ref, dst_ref=srecv_ref.at[me],
                                                      send_sem=send_sems.at[6 + k], recv_sem=recv_sems.at[6 + k], **kw))
            recvs.append(pltpu.make_async_remote_copy(src_ref=gp_ref.at[peer], dst_ref=recv_ref.at[peer],
                                                      send_sem=send_sems.at[k - 1], recv_sem=recv_sems.at[k - 1], **kw))
            recvs.append(pltpu.make_async_remote_copy(src_ref=sm_ref, dst_ref=srecv_ref.at[peer],
                                                      send_sem=send_sems.at[6 + k], recv_sem=recv_sems.at[6 + k], **kw))
        for cp in sends:
            cp.start()
        for cp in recvs:
            cp.wait_recv()
        for cp in sends:
            cp.wait_send()
        l0.wait()
        l1.wait()

    return pl.pallas_call(
        body, name="exchange_grads",
        out_shape=[jax.ShapeDtypeStruct(gp.shape, gp.dtype), jax.ShapeDtypeStruct((N_DEV,) + sm.shape, sm.dtype)],
        in_specs=[HBM_SPEC, HBM_SPEC], out_specs=[HBM_SPEC, HBM_SPEC],
        scratch_shapes=[pltpu.SemaphoreType.DMA((14,)), pltpu.SemaphoreType.DMA((14,)), pltpu.SemaphoreType.DMA((2,))],
    )(gp, sm)


def sum_slots(recv, name, tr):
    n, R, C = recv.shape

    def body(r_ref, o_ref):
        acc = r_ref[0].astype(F32)
        for s in range(1, n):
            acc = acc + r_ref[s].astype(F32)
        o_ref[...] = acc

    return pl.pallas_call(
        body, grid=(R // tr,), name=name,
        in_specs=[pl.BlockSpec((n, tr, C), lambda i: (0, i, 0))], out_specs=pl.BlockSpec((tr, C), lambda i: (i, 0)),
        out_shape=jax.ShapeDtypeStruct((R, C), F32), compiler_params=_cp("parallel"),
    )(recv)


PACK_W = 1024
BIG = [
    ("ffn1_w_gate", "col"), ("ffn1_w_up", "col"), ("ffn1_w_down", "row"),
    ("ffn2_w_gate", "col"), ("ffn2_w_up", "col"), ("ffn2_w_down", "row"),
    ("w_ssd_proj", "row"), ("w_mla_proj", "row"), ("w_out", "row"),
    ("w_xq", "row"), ("w_xk", "row"), ("w_xv", "row"), ("w_xo", "row"),
    ("w_uq", "col"), ("w_uk", "col"), ("w_uv", "col"), ("w_in", "col"),
]
SMALL = ["ffn1_pre_g", "ffn1_post_g", "mix_pre_g", "conv_b", "dt_bias", "a_log", "d_skip", "ssd_norm_g", "q_norm_g",
         "kv_norm_g", "gate_bias", "mix_post_g", "xa_pre_g", "mem_norm_g", "xa_post_g", "ffn2_pre_g", "ffn2_post_g"]
WEIGHTS = ['ffn1_pre_g', 'ffn1_w_gate', 'ffn1_w_up', 'ffn1_w_down', 'ffn1_post_g', 'mix_pre_g', 'w_in', 'conv_w', 'conv_b',
           'dt_bias', 'a_log', 'd_skip', 'ssd_norm_g', 'w_ssd_proj', 'q_norm_g', 'w_uq', 'kv_norm_g', 'w_uk', 'w_uv',
           'w_mla_proj', 'gate_bias', 'w_out', 'mix_post_g', 'xa_pre_g', 'mem_norm_g', 'w_xq', 'w_xk', 'w_xv', 'w_xo',
           'xa_post_g', 'ffn2_pre_g', 'ffn2_w_gate', 'ffn2_w_up', 'ffn2_w_down', 'ffn2_post_g']
IN_Z, IN_XBC, IN_MISC, IN_GATE = SSD_INNER, CONV_CH, SSD_H + QR + KVR + ROPE, 2 * D
IN_MISC_PAD = 768


def _pack_rows(w, kind):
    m = w[0].T if kind == "col" else w[0]
    return m.reshape(-1, PACK_W)


def _pad_rows(a, mult):
    r = (-a.shape[0]) % mult
    return a if r == 0 else jnp.concatenate([a, jnp.zeros((r,) + a.shape[1:], a.dtype)], axis=0)


def _pack_small(vals, loss_row=None, conv_w=None):
    rows = []
    for v in vals:
        f = v.reshape(-1)
        f = jnp.concatenate([f, jnp.zeros(((-f.shape[0]) % 128,), F32)])
        rows.append(f.reshape(-1, 128))
    if conv_w is not None:
        rows.append(conv_w.reshape(-1, 128))
    if loss_row is not None:
        rows.append(loss_row)
    return _pad_rows(jnp.concatenate(rows, axis=0), 8)


def _unpack_small(buf, shapes):
    out, r = [], 0
    for shp in shapes:
        n = math.prod(shp)
        nr = -(-n // 128)
        out.append(buf[r:r + nr].reshape(-1)[:n].reshape(shp))
        r += nr
    return out, r


def _tn(a, b, name, out_dtype=BF16):
    M, N = a.shape[1], b.shape[1]
    T = a.shape[0]
    tm = M if M <= 1536 else M // 2
    tk = 512 if T % 512 == 0 and T > 512 else None
    return mm1(a, b, "tn", out_dtype, name, tm=tm, tn=N, tk=tk)


def _ffn_fwd(x, gpre, gpost, wg_t, wu_t, wd, tag):
    h = rms_fwd(x, gpre, tag + "_pre")
    swi = lambda accs, ex: (accs[0], accs[1], _silu_parts(accs[0])[0] * accs[1])
    G, U, A = mm([[(h, wg_t, "nt")], [(h, wu_t, "nt")]], [F32, F32, BF16], tag + "_gate_up", tn=256, epi=swi)
    H = mm1(A, wd, "nn", F32, tag + "_down")
    y = resid_fwd(x, H, gpost, FFN_RES, tag + "_post")
    return y, (x, h, G, U, A, H)


def _ffn_bwd(dy, saved, gpre, gpost, wg_t, wu_t, wd, tag):
    x, h, G, U, A, H = saved
    dH, dgpost = resid_bwd(H, gpost, dy, FFN_RES, tag + "_post_bwd")

    def dswi(accs, ex):
        sil, dsil = _silu_parts(ex[0])
        return accs[0] * ex[1] * dsil, accs[0] * sil
    dG, dU = mm([[(dH, wd, "nt")]], [BF16, BF16], tag + "_down_bwd", tn=256, epi=dswi, extras=[G, U])
    dwd = _tn(A, dH, tag + "_dwd")
    dh = mm([[(dG, wg_t, "nn"), (dU, wu_t, "nn")]], [F32], tag + "_gate_up_bwd", tm=512)[0]
    dwg_t = _tn(dG, h, tag + "_dwg")
    dwu_t = _tn(dU, h, tag + "_dwu")
    dx, dgpre = rms_bwd(x, gpre, dh, tag + "_pre_bwd", resid=dy)
    return dx, dgpre, dgpost, dwg_t, dwu_t, dwd


def _rope_tables(positions):
    inv = ROPE_THETA ** (-jnp.arange(0, ROPE, 2, dtype=F32) / ROPE)
    ang = positions.astype(F32).reshape(-1)[:, None] * inv
    reps = MLA_H + 1
    return jnp.tile(jnp.cos(ang), (1, reps)), jnp.tile(jnp.sin(ang), (1, reps))


def _heads(t, nseq, width):
    T = t.shape[0]
    return t.reshape(nseq, T // nseq, MLA_H, width).transpose(0, 2, 1, 3)


def _unheads(t):
    b, h, s, w = t.shape
    return t.transpose(0, 2, 1, 3).reshape(b * s, h, w)


def _local_step(x, mem, positions, tgt, W, p):
    nseq = x.shape[0]
    T = nseq * x.shape[1]
    x0 = x.reshape(T, D)
    mem2 = mem.reshape(-1, D)
    cos, sin = _rope_tables(positions)
    HR = MLA_H * (ROPE // 2)

    x1, ffn1 = _ffn_fwd(x0, p["ffn1_pre_g"], p["ffn1_post_g"], W["ffn1_w_gate"], W["ffn1_w_up"], W["ffn1_w_down"], "ffn1")

    w_in_t = W["w_in"]
    o1, o2, o3 = IN_Z, IN_Z + IN_XBC, IN_Z + IN_XBC + IN_MISC
    wt_z, wt_xbc, wt_gate = w_in_t[:o1], w_in_t[o1:o2], w_in_t[o3:]
    wt_misc = _pad_rows(w_in_t[o2:o3], IN_MISC_PAD)
    hm = rms_fwd(x1, p["mix_pre_g"], "mix_pre")
    z = mm1(hm, wt_z, "nt", F32, "in_z")
    xbc = mm1(hm, wt_xbc, "nt", F32, "in_xbc")
    misc = mm1(hm, wt_misc, "nt", F32, "in_misc", tn=256)
    gl = mm1(hm, wt_gate, "nt", F32, "in_gate")
    dtr = misc[:, :SSD_H]
    q_c = misc[:, SSD_H:SSD_H + QR]
    kv_c = misc[:, SSD_H + QR:SSD_H + QR + KVR]
    k_r = misc[:, SSD_H + QR + KVR:IN_MISC]

    xbc_act = conv_fwd(xbc, p["conv_w"], p["conv_b"], nseq)
    y_ssd_core, prev = ssd_fwd(xbc_act, dtr, p["dt_bias"], p["a_log"], p["d_skip"], nseq)
    yn = gated_norm_fwd(y_ssd_core, z, p["ssd_norm_g"], "ssd_norm")
    y_ssd = mm1(yn, W["w_ssd_proj"], "nn", F32, "ssd_proj")

    qn = rms_fwd(q_c, p["q_norm_g"], "q_norm")
    q = mm1(qn, W["w_uq"], "nt", F32, "uq")
    kvn = rms_fwd(kv_c, p["kv_norm_g"], "kv_norm")
    k_nope = mm1(kvn, W["w_uk"], "nt", BF16, "uk")
    v = mm1(kvn, W["w_uv"], "nt", BF16, "uv")
    q3 = q.reshape(T, MLA_H, QK)
    half = ROPE // 2
    x1r = jnp.concatenate([q3[:, :, NOPE:NOPE + half].reshape(T, HR), k_r[:, :half]], axis=1)
    x2r = jnp.concatenate([q3[:, :, NOPE + half:].reshape(T, HR), k_r[:, half:]], axis=1)
    r1, r2 = rope_rot(x1r, x2r, cos, sin, "rope")
    qh = jnp.concatenate([q3[:, :, :NOPE].astype(BF16), r1[:, :HR].reshape(T, MLA_H, half), r2[:, :HR].reshape(T, MLA_H, half)], axis=2)
    kh = jnp.concatenate([k_nope.reshape(T, MLA_H, NOPE),
                          jnp.broadcast_to(r1[:, None, HR:], (T, MLA_H, half)),
                          jnp.broadcast_to(r2[:, None, HR:], (T, MLA_H, half))], axis=2)
    qh = _heads(qh.reshape(T, MLA_H * QK), nseq, QK)
    kh = _heads(kh.reshape(T, MLA_H * QK), nseq, QK)
    vh = _heads(v, nseq, VD)
    oh, lse = attn_fwd(qh, kh, vh)
    o2d = _unheads(oh).reshape(T, MLA_H * VD)
    y_mla = mm1(o2d, W["w_mla_proj"], "nn", F32, "mla_proj")

    merged = merge_fwd(gl, y_ssd, y_mla, p["gate_bias"], "merge")
    hmix = mm1(merged, W["w_out"], "nn", F32, "mix_out")
    x2 = resid_fwd(x1, hmix, p["mix_post_g"], 1.0, "mix_post")

    hq = rms_fwd(x2, p["xa_pre_g"], "xa_pre")
    mn = rms_fwd(mem2, p["mem_norm_g"], "mem_norm")
    xq = mm1(hq, W["w_xq"], "nn", BF16, "xq")
    xk = mm1(mn, W["w_xk"], "nn", BF16, "xk")
    xv = mm1(mn, W["w_xv"], "nn", BF16, "xv")
    xo = xattn_fwd(xq, xk, xv, nseq)
    ho = mm1(xo, W["w_xo"], "nn", F32, "xo")
    x3 = resid_fwd(x2, ho, p["xa_post_g"], 1.0, "xa_post")

    x4, ffn2 = _ffn_fwd(x3, p["ffn2_pre_g"], p["ffn2_post_g"], W["ffn2_w_gate"], W["ffn2_w_up"], W["ffn2_w_down"], "ffn2")
    dx4, loss_row = loss_head(x4, tgt.reshape(T, D), "loss")

    gw, gs = {}, {}
    dx3, gs["ffn2_pre_g"], gs["ffn2_post_g"], gw["ffn2_w_gate"], gw["ffn2_w_up"], gw["ffn2_w_down"] = _ffn_bwd(
        dx4, ffn2, p["ffn2_pre_g"], p["ffn2_post_g"], W["ffn2_w_gate"], W["ffn2_w_up"], W["ffn2_w_down"], "ffn2")

    dho, gs["xa_post_g"] = resid_bwd(ho, p["xa_post_g"], dx3, 1.0, "xa_post_bwd")
    dxo = mm1(dho, W["w_xo"], "nt", BF16, "xo_bwd")
    gw["w_xo"] = _tn(xo, dho, "d_w_xo")
    dxq, dxk, dxv = xattn_bwd(xq, xk, xv, dxo, nseq)
    dhq = mm1(dxq, W["w_xq"], "nt", F32, "xq_bwd")
    gw["w_xq"] = _tn(hq, dxq, "d_w_xq")
    dmn = mm([[(dxk, W["w_xk"], "nt"), (dxv, W["w_xv"], "nt")]], [F32], "xkv_bwd")[0]
    gw["w_xk"] = _tn(mn, dxk, "d_w_xk")
    gw["w_xv"] = _tn(mn, dxv, "d_w_xv")
    _, gs["mem_norm_g"] = rms_bwd(mem2, p["mem_norm_g"], dmn, "mem_norm_bwd", dx_dtype=BF16)
    dx2, gs["xa_pre_g"] = rms_bwd(x2, p["xa_pre_g"], dhq, "xa_pre_bwd", resid=dx3)

    dhmix, gs["mix_post_g"] = resid_bwd(hmix, p["mix_post_g"], dx2, 1.0, "mix_post_bwd")
    dmerged = mm1(dhmix, W["w_out"], "nt", F32, "mix_out_bwd")
    gw["w_out"] = _tn(merged, dhmix, "d_w_out")
    dys, dym, dgl, gs["gate_bias"] = merge_bwd(gl, y_ssd, y_mla, dmerged, p["gate_bias"], "merge_bwd")

    dyn = mm1(dys, W["w_ssd_proj"], "nt", F32, "ssd_proj_bwd")
    gw["w_ssd_proj"] = _tn(yn, dys, "d_w_ssd_proj")
    dyc, dz, gs["ssd_norm_g"] = gated_norm_bwd(y_ssd_core, z, dyn, p["ssd_norm_g"], "ssd_norm_bwd")
    dxbc_act, ddtr, gs["dt_bias"], gs["a_log"], gs["d_skip"] = ssd_bwd(
        xbc_act, dtr, p["dt_bias"], p["a_log"], p["d_skip"], prev, dyc, nseq)
    dxbc, gs["conv_w"], gs["conv_b"] = conv_bwd(xbc, p["conv_w"], p["conv_b"], dxbc_act, nseq)

    do2d = mm1(dym, W["w_mla_proj"], "nt", BF16, "mla_proj_bwd")
    gw["w_mla_proj"] = _tn(o2d, dym, "d_w_mla_proj")
    dqh, dkh, dvh = attn_bwd(qh, kh, vh, oh, lse, _heads(do2d, nseq, VD))
    dq3, dk3 = _unheads(dqh), _unheads(dkh)
    d1 = jnp.concatenate([dq3[:, :, NOPE:NOPE + half].reshape(T, HR), jnp.sum(dk3[:, :, NOPE:NOPE + half], axis=1)], axis=1)
    d2 = jnp.concatenate([dq3[:, :, NOPE + half:].reshape(T, HR), jnp.sum(dk3[:, :, NOPE + half:], axis=1)], axis=1)
    e1, e2 = rope_rot(d1, d2, cos, -sin, "rope_bwd")
    dq = jnp.concatenate([dq3[:, :, :NOPE].astype(BF16), e1[:, :HR].reshape(T, MLA_H, half), e2[:, :HR].reshape(T, MLA_H, half)],
                         axis=2).reshape(T, MLA_H * QK)
    dk_nope = dk3[:, :, :NOPE].astype(BF16).reshape(T, MLA_H * NOPE)
    dk_r = jnp.concatenate([e1[:, HR:], e2[:, HR:]], axis=1)
    dv2 = _unheads(dvh).reshape(T, MLA_H * VD)
    dqn = mm1(dq, W["w_uq"], "nn", F32, "uq_bwd", tn=128)
    gw["w_uq"] = _tn(dq, qn, "d_w_uq")
    dq_c, gs["q_norm_g"] = rms_bwd(q_c, p["q_norm_g"], dqn, "q_norm_bwd", dx_dtype=BF16)
    dkvn = mm([[(dk_nope, W["w_uk"], "nn"), (dv2, W["w_uv"], "nn")]], [F32], "ukv_bwd")[0]
    gw["w_uk"] = _tn(dk_nope, kvn, "d_w_uk")
    gw["w_uv"] = _tn(dv2, kvn, "d_w_uv")
    dkv_c, gs["kv_norm_g"] = rms_bwd(kv_c, p["kv_norm_g"], dkvn, "kv_norm_bwd", dx_dtype=BF16)

    dmisc = jnp.concatenate([ddtr.astype(BF16), dq_c, dkv_c, dk_r, jnp.zeros((T, IN_MISC_PAD - IN_MISC), BF16)], axis=1)
    dhm = mm([[(dz, wt_z, "nn"), (dxbc, wt_xbc, "nn"), (dmisc, wt_misc, "nn"), (dgl, wt_gate, "nn")]], [F32], "in_bwd", tm=512)[0]
    gw["w_in"] = jnp.concatenate([_tn(dz, hm, "d_w_in_z"), _tn(dxbc, hm, "d_w_in_xbc"), _tn(dmisc, hm, "d_w_in_misc")[:IN_MISC],
                                  _tn(dgl, hm, "d_w_in_gate")], axis=0)
    dx1, gs["mix_pre_g"] = rms_bwd(x1, p["mix_pre_g"], dhm, "mix_pre_bwd", resid=dx2)

    dx0, gs["ffn1_pre_g"], gs["ffn1_post_g"], gw["ffn1_w_gate"], gw["ffn1_w_up"], gw["ffn1_w_down"] = _ffn_bwd(
        dx1, ffn1, p["ffn1_pre_g"], p["ffn1_post_g"], W["ffn1_w_gate"], W["ffn1_w_up"], W["ffn1_w_down"], "ffn1")
    return loss_row, dx0.reshape(x.shape), gw, gs


def kernel(x, mem, positions, ffn1_pre_g, ffn1_w_gate, ffn1_w_up, ffn1_w_down, ffn1_post_g, mix_pre_g, w_in, conv_w, conv_b, dt_bias, a_log, d_skip, ssd_norm_g, w_ssd_proj, q_norm_g, w_uq, kv_norm_g, w_uk, w_uv, w_mla_proj, gate_bias, w_out, mix_post_g, xa_pre_g, mem_norm_g, w_xq, w_xk, w_xv, w_xo, xa_post_g, ffn2_pre_g, ffn2_w_gate, ffn2_w_up, ffn2_w_down, ffn2_post_g, loss_target, m_ffn1_pre_g, m_ffn1_w_gate, m_ffn1_w_up, m_ffn1_w_down, m_ffn1_post_g, m_mix_pre_g, m_w_in, m_conv_w, m_conv_b, m_dt_bias, m_a_log, m_d_skip, m_ssd_norm_g, m_w_ssd_proj, m_q_norm_g, m_w_uq, m_kv_norm_g, m_w_uk, m_w_uv, m_w_mla_proj, m_gate_bias, m_w_out, m_mix_post_g, m_xa_pre_g, m_mem_norm_g, m_w_xq, m_w_xk, m_w_xv, m_w_xo, m_xa_post_g, m_ffn2_pre_g, m_ffn2_w_gate, m_ffn2_w_up, m_ffn2_w_down, m_ffn2_post_g, v_ffn1_pre_g, v_ffn1_w_gate, v_ffn1_w_up, v_ffn1_w_down, v_ffn1_post_g, v_mix_pre_g, v_w_in, v_conv_w, v_conv_b, v_dt_bias, v_a_log, v_d_skip, v_ssd_norm_g, v_w_ssd_proj, v_q_norm_g, v_w_uq, v_kv_norm_g, v_w_uk, v_w_uv, v_w_mla_proj, v_gate_bias, v_w_out, v_mix_post_g, v_xa_pre_g, v_mem_norm_g, v_w_xq, v_w_xk, v_w_xv, v_w_xo, v_xa_post_g, v_ffn2_pre_g, v_ffn2_w_gate, v_ffn2_w_up, v_ffn2_w_down, v_ffn2_post_g):
    a = dict(locals())
    w = {n: a[n] for n in WEIGHTS}
    m = {n: a["m_" + n] for n in WEIGHTS}
    v = {n: a["v_" + n] for n in WEIGHTS}

    packs = [_pack_rows(w[n], kind).astype(BF16) for n, kind in BIG]
    nrows = [pk.shape[0] for pk in packs]
    cw_bits = lax.bitcast_convert_type(conv_w[0], BF16).reshape(-1)
    cw_rows = -(-cw_bits.shape[0] // PACK_W)
    cw_pack = jnp.concatenate([cw_bits, jnp.zeros((cw_rows * PACK_W - cw_bits.shape[0],), BF16)]).reshape(cw_rows, PACK_W)
    shard = _pad_rows(jnp.concatenate(packs + [cw_pack], axis=0), 16)
    R = shard.shape[0]
    gathered = allgather_rows(shard)
    W, r0 = {}, 0
    for (n, kind), nr in zip(BIG, nrows):
        K = w[n].shape[1] if kind == "col" else PACK_W
        W[n] = gathered[:, r0:r0 + nr].reshape(-1, K)
        r0 += nr
    cw_shape = conv_w.shape[1:]
    cw_all = gathered[:, r0:r0 + cw_rows].reshape(N_DEV, -1)[:, :cw_bits.shape[0]].reshape((N_DEV,) + cw_shape + (2,))
    conv_w_full = lax.bitcast_convert_type(cw_all, F32).transpose(1, 0, 2).reshape(cw_shape[0], -1)
    p = {n: w[n] for n in SMALL}
    p["conv_w"] = conv_w_full

    loss_row, grad_x, gw, gs = _local_step(x, mem, positions, loss_target, W, p)

    gp = jnp.concatenate([gw[n].reshape(N_DEV, nr, PACK_W) for (n, _), nr in zip(BIG, nrows)]
                         + [jnp.zeros((N_DEV, R - sum(nrows), PACK_W), BF16)], axis=1)
    sm = _pack_small([gs[n] for n in SMALL], loss_row=loss_row, conv_w=gs["conv_w"])
    recv, srecv = exchange_grads(gp, sm)
    g_rows = sum_slots(recv, "sum_big", tr=_tile(R, (224, 256, 128, 112, 64, 16)))
    s_rows = sum_slots(srecv, "sum_small", tr=sm.shape[0])
    grads, r0 = {}, 0
    for (n, kind), nr in zip(BIG, nrows):
        blk = g_rows[r0:r0 + nr]
        grads[n] = (blk.reshape(w[n].shape[2], w[n].shape[1]).T if kind == "col" else blk)[None]
        r0 += nr
    small_g, r1 = _unpack_small(s_rows, [w[n].shape for n in SMALL])
    for n, g in zip(SMALL, small_g):
        grads[n] = g
    ncw = math.prod(conv_w_full.shape) // 128
    cw_grad_full = s_rows[r1:r1 + ncw].reshape(conv_w_full.shape)
    wsh = conv_w.shape[2]
    grads["conv_w"] = lax.dynamic_slice_in_dim(cw_grad_full, _dev_index() * wsh, wsh, axis=1)[None]
    loss = s_rows[r1 + ncw, 0]

    delta, new_m, new_v = {}, {}, {}
    for n, _ in BIG + [("conv_w", "col")]:
        shp = w[n].shape
        d_, m_, v_ = adamw(w[n][0], grads[n][0], m[n][0], v[n][0], "adamw_" + n)
        delta[n], new_m[n], new_v[n] = d_.reshape(shp), m_.reshape(shp), v_.reshape(shp)
    sp = [_pack_small([t[n] for n in SMALL]) for t in (w, grads, m, v)]
    outs = adamw(sp[0], sp[1], sp[2], sp[3], "adamw_small")
    for t, buf in zip((delta, new_m, new_v), outs):
        vals, _ = _unpack_small(buf, [w[n].shape for n in SMALL])
        for n, val in zip(SMALL, vals):
            t[n] = val
    return (loss, grad_x, *[grads[n] for n in WEIGHTS], *[delta[n] for n in WEIGHTS],
            *[new_m[n] for n in WEIGHTS], *[new_v[n] for n in WEIGHTS])
```

```python
import functools
import math

import jax
import jax.numpy as jnp
from jax import lax
from jax.experimental import pallas as pl
from jax.experimental.pallas import tpu as pltpu

F32, BF16 = jnp.float32, jnp.bfloat16
HI = lax.Precision.HIGHEST
MESH = pl.DeviceIdType.MESH
N_DEV = 8

D = 1024
DFF = 2816
SSD_H, SSD_P, SSD_G, SSD_N, SSD_L = 16, 64, 2, 128, 128
SSD_INNER = SSD_H * SSD_P
CONV_K, CONV_CH = 4, 1536
MLA_H, QR, KVR, NOPE, ROPE, VD = 16, 384, 256, 64, 32, 64
QK = NOPE + ROPE
ROPE_THETA = 10000.0
XA_H, XA_D = 4, 256
EPS = 1e-6
FFN_RES = 0.5
LR, B1, B2, AEPS, WD, STEP = 0.001, 0.9, 0.999, 1e-08, 0.01, 10

VMEM_LIMIT = 56 * 2**20


def _cp(*sem):
    return pltpu.CompilerParams(dimension_semantics=sem, vmem_limit_bytes=VMEM_LIMIT)


def _sigmoid(x):
    return 1.0 / (1.0 + jnp.exp(-x))


def _softplus(x):
    return jnp.where(x > 20.0, x, jnp.log(1.0 + jnp.exp(jnp.minimum(x, 20.0))))


def _dot(a, b, dims="nn", hi=False):
    ca = 0 if dims[0] == "t" else 1
    cb = 1 if dims[1] == "t" else 0
    if hi:
        return lax.dot_general(a, b, (((ca,), (cb,)), ((), ())), precision=HI, preferred_element_type=F32)
    return lax.dot_general(a.astype(BF16), b.astype(BF16), (((ca,), (cb,)), ((), ())), preferred_element_type=F32)


def _ssd_common(dtr, dtb, alog):
    L = dtr.shape[0]
    dt = _softplus(dtr + dtb)
    a = -jnp.exp(alog)
    adt = dt * a
    r = lax.broadcasted_iota(jnp.int32, (L, L), 0)
    c = lax.broadcasted_iota(jnp.int32, (L, L), 1)
    lower = r >= c
    tri = lower.astype(F32)
    cs = _dot(tri, adt, "nn", hi=True)
    cs_t = _dot(adt, tri, "tt", hi=True)
    return dt, a, cs, cs_t, lower


def _head_expand():
    hh = lax.broadcasted_iota(jnp.int32, (SSD_H, SSD_INNER), 0)
    jj = lax.broadcasted_iota(jnp.int32, (SSD_H, SSD_INNER), 1)
    return ((jj >= hh * SSD_P) & (jj < hh * SSD_P + SSD_P)).astype(F32)


def _head_reduce():
    hh = lax.broadcasted_iota(jnp.int32, (SSD_INNER, SSD_H), 1)
    jj = lax.broadcasted_iota(jnp.int32, (SSD_INNER, SSD_H), 0)
    return ((jj >= hh * SSD_P) & (jj < hh * SSD_P + SSD_P)).astype(F32)


def ssd_fwd(xbc, dtr, dtb, alog, dsk, nseq):
    T = xbc.shape[0]
    S = T // nseq
    C = S // SSD_L
    L = SSD_L
    NP = SSD_H // 2

    def body(x_ref, b_ref, c_ref, dtr_ref, dtb_ref, alog_ref, dsk_ref, y_ref, prev_ref, st_ref):
        ci = pl.program_id(1)

        @pl.when(ci == 0)
        def _():
            st_ref[...] = jnp.zeros_like(st_ref)

        dt, a, cs, cs_t, lower = _ssd_common(dtr_ref[:, 0:SSD_H], dtb_ref[...], alog_ref[...])
        E = _head_expand()
        X = x_ref[...]
        dt_e = _dot(dt, E, hi=True)
        cs_e = _dot(cs, E, hi=True)
        csl_e = cs_e[L - 1:L, :]
        Xd = X * dt_e
        Xf = Xd * jnp.exp(csl_e - cs_e)
        e_e = jnp.exp(cs_e)
        y_ref[...] = _dot(dsk_ref[...], E, hi=True) * X
        lane = lax.broadcasted_iota(jnp.int32, (1, 2 * SSD_P), 1)
        rowp = lax.broadcasted_iota(jnp.int32, (2 * SSD_P, 1), 0)
        for g in range(SSD_G):
            Bg = b_ref[:, g * SSD_N:(g + 1) * SSD_N]
            Cg = c_ref[:, g * SSD_N:(g + 1) * SSD_N]
            cb = _dot(Cg, Bg, "nt")
            for pp in range(NP // SSD_G):
                p = g * (NP // SSD_G) + pp
                sl = slice(p * 2 * SSD_P, (p + 1) * 2 * SSD_P)
                Xd_p = Xd[:, sl]
                yd = jnp.zeros((L, 2 * SSD_P), F32)
                for q in range(2):
                    h = 2 * p + q
                    m = jnp.where(lower, jnp.exp(jnp.minimum(cs[:, h:h + 1] - cs_t[h:h + 1, :], 0.0)), 0.0)
                    mask = (lane >= q * SSD_P) & (lane < (q + 1) * SSD_P)
                    yd = yd + _dot(cb * m, jnp.where(mask, Xd_p, 0.0))
                S0 = st_ref[p]
                prev_ref[0, 0, p] = S0
                z = _dot(Cg, S0, "nt")
                y_ref[:, sl] += yd + z * e_e[:, sl]
                h0 = 2 * p
                dec = jnp.where(rowp < SSD_P, jnp.exp(cs[L - 1:L, h0:h0 + 1]), jnp.exp(cs[L - 1:L, h0 + 1:h0 + 2]))
                st_ref[p] = S0 * dec + _dot(Xf[:, sl], Bg, "tn")

    row = lambda b, c: (b * C + c, 0)
    return pl.pallas_call(
        body, grid=(nseq, C), name="ssd_fwd",
        in_specs=[pl.BlockSpec((L, SSD_INNER), row),
                  pl.BlockSpec((L, SSD_G * SSD_N), lambda b, c: (b * C + c, SSD_INNER // (SSD_G * SSD_N))),
                  pl.BlockSpec((L, SSD_G * SSD_N), lambda b, c: (b * C + c, SSD_INNER // (SSD_G * SSD_N) + 1)),
                  pl.BlockSpec((L, 128), row),
                  pl.BlockSpec((1, SSD_H), lambda b, c: (0, 0)),
                  pl.BlockSpec((1, SSD_H), lambda b, c: (0, 0)),
                  pl.BlockSpec((1, SSD_H), lambda b, c: (0, 0))],
        out_specs=[pl.BlockSpec((L, SSD_INNER), row),
                   pl.BlockSpec((1, 1, NP, 2 * SSD_P, SSD_N), lambda b, c: (b, c, 0, 0, 0))],
        out_shape=[jax.ShapeDtypeStruct((T, SSD_INNER), F32),
                   jax.ShapeDtypeStruct((nseq, C, NP, 2 * SSD_P, SSD_N), F32)],
        scratch_shapes=[pltpu.VMEM((NP, 2 * SSD_P, SSD_N), F32)],
        compiler_params=_cp("parallel", "arbitrary"),
    )(xbc, xbc, xbc, dtr, dtb, alog, dsk)


def ssd_bwd(xbc, dtr, dtb, alog, dsk, prev, dy, nseq):
    T = xbc.shape[0]
    S = T // nseq
    C = S // SSD_L
    L = SSD_L
    NP = SSD_H // 2

    def body(x_ref, b_ref, c_ref, dtr_ref, dtb_ref, alog_ref, dsk_ref, prev_ref, dy_ref,
             dxbc_ref, ddtr_ref, ddtb_ref, dalog_ref, ddsk_ref, ds_ref):
        bi = pl.program_id(0)
        ci = pl.program_id(1)

        @pl.when(ci == 0)
        def _():
            ds_ref[...] = jnp.zeros_like(ds_ref)

        @pl.when((ci == 0) & (bi == 0))
        def _():
            ddtb_ref[...] = jnp.zeros_like(ddtb_ref)
            dalog_ref[...] = jnp.zeros_like(dalog_ref)
            ddsk_ref[...] = jnp.zeros_like(ddsk_ref)

        dtr = dtr_ref[:, 0:SSD_H]
        dtb = dtb_ref[...]
        dt, a, cs, cs_t, lower = _ssd_common(dtr, dtb, alog_ref[...])
        upper = lax.broadcasted_iota(jnp.int32, (L, L), 1) >= lax.broadcasted_iota(jnp.int32, (L, L), 0)
        E = _head_expand()
        ET = _head_reduce()
        X = x_ref[...]
        dY = dy_ref[...]
        dt_e = _dot(dt, E, hi=True)
        cs_e = _dot(cs, E, hi=True)
        csl_e = cs_e[L - 1:L, :]
        f_e = jnp.exp(csl_e - cs_e)
        e_e = jnp.exp(cs_e)
        dsk_e = _dot(dsk_ref[...], E, hi=True)
        Xd = X * dt_e
        Xf = Xd * f_e
        lane = lax.broadcasted_iota(jnp.int32, (1, 2 * SSD_P), 1)
        rowp = lax.broadcasted_iota(jnp.int32, (2 * SSD_P, 1), 0)
        hsel = lax.broadcasted_iota(jnp.int32, (1, SSD_H), 1)
        dcs = jnp.zeros((L, SSD_H), F32)
        dcsl = jnp.zeros((1, SSD_H), F32)
        for g in range(SSD_G):
            Bg = b_ref[:, g * SSD_N:(g + 1) * SSD_N]
            Cg = c_ref[:, g * SSD_N:(g + 1) * SSD_N]
            cb = _dot(Cg, Bg, "nt")
            cbt = _dot(Bg, Cg, "nt")
            dB = jnp.zeros((L, SSD_N), F32)
            dC = jnp.zeros((L, SSD_N), F32)
            for pp in range(NP // SSD_G):
                p = g * (NP // SSD_G) + pp
                sl = slice(p * 2 * SSD_P, (p + 1) * 2 * SSD_P)
                Xd_p = Xd[:, sl]
                dY_p = dY[:, sl]
                dXd_p = jnp.zeros((L, 2 * SSD_P), F32)
                for q in range(2):
                    h = 2 * p + q
                    mask = (lane >= q * SSD_P) & (lane < (q + 1) * SSD_P)
                    col = cs[:, h:h + 1]
                    rw = cs_t[h:h + 1, :]
                    m = jnp.where(lower, jnp.exp(jnp.minimum(col - rw, 0.0)), 0.0)
                    mt = jnp.where(upper, jnp.exp(jnp.minimum(rw - col, 0.0)), 0.0)
                    dYm = jnp.where(mask, dY_p, 0.0)
                    dW = _dot(dYm, Xd_p, "nt")
                    dWt = _dot(Xd_p, dYm, "nt")
                    w = cb * m
                    wt = cbt * mt
                    dC = dC + _dot(dW * m, Bg)
                    dB = dB + _dot(dWt * mt, Cg)
                    dXd_p = dXd_p + jnp.where(mask, _dot(wt, dY_p), 0.0)
                    qcol = jnp.sum(dW * w, axis=1, keepdims=True) - jnp.sum(dWt * wt, axis=1, keepdims=True)
                    dcs = dcs + qcol * (hsel == h).astype(F32)
                S0 = prev_ref[0, 0, p]
                dSn = ds_ref[p]
                dZ = dY_p * e_e[:, sl]
                dC = dC + _dot(dZ, S0)
                h0 = 2 * p
                el0 = jnp.exp(cs[L - 1:L, h0:h0 + 1])
                el1 = jnp.exp(cs[L - 1:L, h0 + 1:h0 + 2])
                dec = jnp.where(rowp < SSD_P, el0, el1)
                ds_ref[p] = dSn * dec + _dot(dZ, Cg, "tn")
                dXf_p = _dot(Bg, dSn, "nt")
                dB = dB + _dot(Xf[:, sl], dSn)
                rs = jnp.sum(dSn * S0, axis=1, keepdims=True)
                s0 = jnp.sum(jnp.where(rowp < SSD_P, rs, 0.0), axis=0, keepdims=True) * el0
                s1 = jnp.sum(jnp.where(rowp >= SSD_P, rs, 0.0), axis=0, keepdims=True) * el1
                dcsl = dcsl + s0 * (hsel == h0).astype(F32) + s1 * (hsel == h0 + 1).astype(F32)
                dxbc_ref[:, sl] = dXd_p
                y_off = _dot(Cg, S0, "nt") * e_e[:, sl]
                t1 = dY_p * y_off - dXf_p * Xf[:, sl]
                r1 = jnp.where(lane < SSD_P, t1, 0.0)
                c0 = jnp.sum(r1, axis=1, keepdims=True)
                c1 = jnp.sum(t1 - r1, axis=1, keepdims=True)
                dcs = dcs + c0 * (hsel == h0).astype(F32) + c1 * (hsel == h0 + 1).astype(F32)
                t2 = dXf_p * Xf[:, sl]
                r2 = jnp.where(lane < SSD_P, t2, 0.0)
                dcsl = dcsl + jnp.sum(r2, keepdims=True) * (hsel == h0).astype(F32) \
                    + jnp.sum(t2 - r2, keepdims=True) * (hsel == h0 + 1).astype(F32)
                dxbc_ref[:, sl] += dXf_p * f_e[:, sl]
            dxbc_ref[:, SSD_INNER + g * SSD_N:SSD_INNER + (g + 1) * SSD_N] = dB
            dxbc_ref[:, SSD_INNER + (SSD_G + g) * SSD_N:SSD_INNER + (SSD_G + g + 1) * SSD_N] = dC
        dXd = dxbc_ref[:, 0:SSD_INNER]
        dxbc_ref[:, 0:SSD_INNER] = dXd * dt_e + dsk_e * dY
        rowl = lax.broadcasted_iota(jnp.int32, (L, 1), 0)
        dcs = dcs + jnp.where(rowl == L - 1, dcsl, 0.0)
        dalpha = _dot(upper.astype(F32), dcs, hi=True)
        ddt = _dot(dXd * X, ET, hi=True) + dalpha * a
        dalog_ref[...] += jnp.sum(dalpha * dt, axis=0, keepdims=True) * a
        ddtr = ddt * _sigmoid(dtr + dtb)
        spread = (lax.broadcasted_iota(jnp.int32, (SSD_H, 128), 0) == lax.broadcasted_iota(jnp.int32, (SSD_H, 128), 1)).astype(F32)
        ddtr_ref[...] = _dot(ddtr, spread, hi=True).astype(ddtr_ref.dtype)
        ddtb_ref[...] += jnp.sum(ddtr, axis=0, keepdims=True)
        ddsk_ref[...] += jnp.sum(_dot(dY * X, ET, hi=True), axis=0, keepdims=True)

    rowr = lambda b, c: (b * C + (C - 1 - c), 0)
    small = pl.BlockSpec((1, SSD_H), lambda b, c: (0, 0))
    return pl.pallas_call(
        body, grid=(nseq, C), name="ssd_bwd",
        in_specs=[pl.BlockSpec((L, SSD_INNER), rowr),
                  pl.BlockSpec((L, SSD_G * SSD_N), lambda b, c: (b * C + (C - 1 - c), SSD_INNER // (SSD_G * SSD_N))),
                  pl.BlockSpec((L, SSD_G * SSD_N), lambda b, c: (b * C + (C - 1 - c), SSD_INNER // (SSD_G * SSD_N) + 1)),
                  pl.BlockSpec((L, 128), rowr), small, small, small,
                  pl.BlockSpec((1, 1, NP, 2 * SSD_P, SSD_N), lambda b, c: (b, C - 1 - c, 0, 0, 0)),
                  pl.BlockSpec((L, SSD_INNER), rowr)],
        out_specs=[pl.BlockSpec((L, CONV_CH), rowr), pl.BlockSpec((L, 128), rowr), small, small, small],
        out_shape=[jax.ShapeDtypeStruct((T, CONV_CH), F32), jax.ShapeDtypeStruct((T, 128), BF16),
                   jax.ShapeDtypeStruct((1, SSD_H), F32), jax.ShapeDtypeStruct((1, SSD_H), F32),
                   jax.ShapeDtypeStruct((1, SSD_H), F32)],
        scratch_shapes=[pltpu.VMEM((NP, 2 * SSD_P, SSD_N), F32)],
        compiler_params=_cp("arbitrary", "arbitrary"),
    )(xbc, xbc, xbc, dtr, dtb, alog, dsk, prev, dy)


ATT_BLK = 256


def attn_fwd(q, k, v):
    B, H, S, dq = q.shape
    dv = v.shape[-1]
    tq = min(ATT_BLK, S)
    nq = S // tq
    scale = QK ** -0.5

    def body(q_ref, k_ref, v_ref, o_ref, lse_ref):
        qi = pl.program_id(2)
        qb = q_ref[0, 0]
        row = lax.broadcasted_iota(jnp.int32, (tq, tq), 0)
        col = lax.broadcasted_iota(jnp.int32, (tq, tq), 1)

        def step(j, carry):
            m, l, acc = carry
            off = pl.multiple_of(j * tq, tq)
            kb = k_ref[0, 0, pl.ds(off, tq), :]
            vb = v_ref[0, 0, pl.ds(off, tq), :]
            s = _dot(qb, kb, "nt") * scale
            s = jnp.where((j * tq + col) <= (qi * tq + row), s, -1e30)
            m_new = jnp.maximum(m, jnp.max(s, axis=1, keepdims=True))
            p = jnp.exp(s - m_new)
            corr = jnp.exp(m - m_new)
            return m_new, l * corr + jnp.sum(p, axis=1, keepdims=True), acc * corr + _dot(p, vb)

        m, l, acc = lax.fori_loop(0, qi + 1, step, (jnp.full((tq, 1), -1e30, F32), jnp.zeros((tq, 1), F32),
                                                    jnp.zeros((tq, dv), F32)))
        o_ref[0, 0] = (acc / l).astype(o_ref.dtype)
        lse_ref[0, 0] = m + jnp.log(l)

    full = lambda d: pl.BlockSpec((1, 1, S, d), lambda b, h, i: (b, h, 0, 0))
    blk = lambda d: pl.BlockSpec((1, 1, tq, d), lambda b, h, i: (b, h, i, 0))
    return pl.pallas_call(
        body, grid=(B, H, nq), name="attn_fwd",
        in_specs=[blk(dq), full(dq), full(dv)],
        out_specs=[blk(dv), blk(1)],
        out_shape=[jax.ShapeDtypeStruct((B, H, S, dv), BF16), jax.ShapeDtypeStruct((B, H, S, 1), F32)],
        compiler_params=_cp("parallel", "parallel", "arbitrary"),
    )(q, k, v)


def attn_bwd(q, k, v, o, lse, do):
    B, H, S, dq_ = q.shape
    dv_ = v.shape[-1]
    tk = min(ATT_BLK, S)
    nk = S // tk
    scale = QK ** -0.5

    def body(q_ref, k_ref, v_ref, o_ref, lse_ref, do_ref, dq_ref, dk_ref, dv_ref):
        kj = pl.program_id(2)

        @pl.when(kj == 0)
        def _():
            dq_ref[...] = jnp.zeros_like(dq_ref)

        kb = k_ref[0, 0]
        vb = v_ref[0, 0]
        row = lax.broadcasted_iota(jnp.int32, (tk, tk), 0)
        col = lax.broadcasted_iota(jnp.int32, (tk, tk), 1)

        def step(i, carry):
            dk, dv = carry
            off = pl.multiple_of(i * tk, tk)
            qb = q_ref[0, 0, pl.ds(off, tk), :]
            dob = do_ref[0, 0, pl.ds(off, tk), :]
            ob = o_ref[0, 0, pl.ds(off, tk), :]
            ls = lse_ref[0, 0, pl.ds(off, tk), :]
            s = _dot(qb, kb, "nt") * scale
            p = jnp.where((kj * tk + col) <= (i * tk + row), jnp.exp(s - ls), 0.0)
            dv = dv + _dot(p, dob, "tn")
            dp = _dot(dob, vb, "nt")
            delta = jnp.sum(dob.astype(F32) * ob.astype(F32), axis=1, keepdims=True)
            ds = p * (dp - delta) * scale
            dk = dk + _dot(ds, qb, "tn")
            dq_ref[0, 0, pl.ds(off, tk), :] += _dot(ds, kb)
            return dk, dv

        dk, dv = lax.fori_loop(kj, nk, step, (jnp.zeros((tk, dq_), F32), jnp.zeros((tk, dv_), F32)))
        dk_ref[0, 0] = dk
        dv_ref[0, 0] = dv.astype(dv_ref.dtype)

    full = lambda d: pl.BlockSpec((1, 1, S, d), lambda b, h, j: (b, h, 0, 0))
    blk = lambda d: pl.BlockSpec((1, 1, tk, d), lambda b, h, j: (b, h, j, 0))
    return pl.pallas_call(
        body, grid=(B, H, nk), name="attn_bwd",
        in_specs=[full(dq_), blk(dq_), blk(dv_), full(dv_), full(1), full(dv_)],
        out_specs=[full(dq_), blk(dq_), blk(dv_)],
        out_shape=[jax.ShapeDtypeStruct((B, H, S, dq_), F32), jax.ShapeDtypeStruct((B, H, S, dq_), F32),
                   jax.ShapeDtypeStruct((B, H, S, dv_), BF16)],
        compiler_params=_cp("parallel", "parallel", "arbitrary"),
    )(q, k, v, o, lse, do)


SLOT = 128
ATT_T = 512


def _col_to_row(col):
    n = col.shape[0]
    eye = lax.broadcasted_iota(jnp.int32, (n, n), 0) == lax.broadcasted_iota(jnp.int32, (n, n), 1)
    return jnp.sum(jnp.where(eye, col, 0.0), axis=0, keepdims=True)


def attn_slot_fwd(q, k, v, nseq):
    T = q.shape[0]
    S = T // nseq
    t = min(ATT_T, S)
    nb = S // t
    scale = QK ** -0.5

    def body(q_ref, k_ref, v_ref, o_ref, lse_ref):
        causal = lax.broadcasted_iota(jnp.int32, (t, t), 1) <= lax.broadcasted_iota(jnp.int32, (t, t), 0)
        for qi in range(nb):
            qb = q_ref[qi * t:(qi + 1) * t, :]
            m = l = acc = None
            for kj in range(qi + 1):
                s = _dot(qb, k_ref[kj * t:(kj + 1) * t, :], "nt") * scale
                if kj == qi:
                    s = jnp.where(causal, s, -1e30)
                bm = jnp.max(s, axis=1, keepdims=True)
                if kj == 0:
                    m = bm
                    p = jnp.exp(s - m)
                    l = jnp.sum(p, axis=1, keepdims=True)
                    acc = _dot(p, v_ref[0:t, :])
                else:
                    m_new = jnp.maximum(m, bm)
                    corr = jnp.exp(m - m_new)
                    p = jnp.exp(s - m_new)
                    l = l * corr + jnp.sum(p, axis=1, keepdims=True)
                    acc = acc * corr + _dot(p, v_ref[kj * t:(kj + 1) * t, :])
                    m = m_new
            o_ref[qi * t:(qi + 1) * t, :] = (acc / l).astype(o_ref.dtype)
            lse_ref[0, 0, :, qi * t:(qi + 1) * t] = _col_to_row(m + jnp.log(l))

    blk = pl.BlockSpec((S, SLOT), lambda b, h: (b, h))
    return pl.pallas_call(
        body, grid=(nseq, MLA_H), name="attn_fwd", in_specs=[blk, blk, blk],
        out_specs=[blk, pl.BlockSpec((1, 1, 1, S), lambda b, h: (b, h, 0, 0))],
        out_shape=[jax.ShapeDtypeStruct((T, MLA_H * SLOT), BF16), jax.ShapeDtypeStruct((nseq, MLA_H, 1, S), F32)],
        compiler_params=_cp("parallel", "parallel"),
    )(q, k, v)


def attn_slot_bwd(q, k, v, o, lse, do, nseq):
    T = q.shape[0]
    S = T // nseq
    t = min(ATT_T, S)
    nb = S // t
    scale = QK ** -0.5

    def body(q_ref, k_ref, v_ref, o_ref, lse_ref, do_ref, dq_ref, dk_ref, dv_ref):
        causal_t = lax.broadcasted_iota(jnp.int32, (t, t), 0) <= lax.broadcasted_iota(jnp.int32, (t, t), 1)
        ones = jnp.ones((8, SLOT), F32)
        delta = []
        for qi in range(nb):
            sl = slice(qi * t, (qi + 1) * t)
            prod = do_ref[sl, :].astype(F32) * o_ref[sl, :].astype(F32)
            delta.append(_dot(ones, prod, "nt", hi=True)[0:1, :])
        for kj in range(nb):
            ks = slice(kj * t, (kj + 1) * t)
            kb = k_ref[ks, :]
            vb = v_ref[ks, :]
            dk = dv = None
            for qi in range(kj, nb):
                sl = slice(qi * t, (qi + 1) * t)
                qb = q_ref[sl, :]
                dob = do_ref[sl, :]
                st = _dot(kb, qb, "nt") * scale
                pt = jnp.exp(st - lse_ref[0, 0, :, sl])
                if qi == kj:
                    pt = jnp.where(causal_t, pt, 0.0)
                dpt = _dot(vb, dob, "nt")
                dst = (pt * (dpt - delta[qi]) * scale).astype(BF16)
                dvc = _dot(pt, dob)
                dkc = _dot(dst, qb)
                dv = dvc if dv is None else dv + dvc
                dk = dkc if dk is None else dk + dkc
                dqc = _dot(dst, kb, "tn")
                if kj == 0:
                    dq_ref[sl, :] = dqc
                else:
                    dq_ref[sl, :] += dqc
            dk_ref[ks, :] = dk
            dv_ref[ks, :] = dv.astype(dv_ref.dtype)

    blk = pl.BlockSpec((S, SLOT), lambda b, h: (b, h))
    lse_spec = pl.BlockSpec((1, 1, 1, S), lambda b, h: (b, h, 0, 0))
    W = MLA_H * SLOT
    return pl.pallas_call(
        body, grid=(nseq, MLA_H), name="attn_bwd", in_specs=[blk, blk, blk, blk, lse_spec, blk],
        out_specs=[blk, blk, blk],
        out_shape=[jax.ShapeDtypeStruct((T, W), F32), jax.ShapeDtypeStruct((T, W), F32), jax.ShapeDtypeStruct((T, W), BF16)],
        compiler_params=_cp("parallel", "parallel"),
    )(q, k, v, o, lse, do)


def _rope_coeffs(cos, sin):
    half = ROPE // 2
    r = lax.broadcasted_iota(jnp.int32, (half, SLOT), 0)
    c = lax.broadcasted_iota(jnp.int32, (half, SLOT), 1)
    pc = ((c == r + NOPE) | (c == r + NOPE + half)).astype(F32)
    ps = (c == r + NOPE + half).astype(F32) - (c == r + NOPE).astype(F32)
    lane = lax.broadcasted_iota(jnp.int32, (1, SLOT), 1)
    return _dot(cos, pc, hi=True) + (lane < NOPE).astype(F32), _dot(sin, ps, hi=True)


def _rope_swap(x):
    W = x.shape[1]
    half = ROPE // 2
    lane = lax.broadcasted_iota(jnp.int32, (1, W), 1) & (SLOT - 1)
    up = pltpu.roll(x, W - half, axis=1)
    dn = pltpu.roll(x, half, axis=1)
    return jnp.where((lane >= NOPE) & (lane < NOPE + half), up, jnp.where((lane >= NOPE + half) & (lane < QK), dn, 0.0))


def rope_slot_fwd(q, kn, dtkr, cos, sin, name):
    def fn(qv, knv, krv, cv, sv):
        C, Sg = _rope_coeffs(cv, sv)
        ct, stl = jnp.tile(C, (1, MLA_H)), jnp.tile(Sg, (1, MLA_H))
        qo = qv * ct + _rope_swap(qv) * stl
        r = lax.broadcasted_iota(jnp.int32, (SLOT, SLOT), 0)
        c = lax.broadcasted_iota(jnp.int32, (SLOT, SLOT), 1)
        place = ((c == r + NOPE) & (r < ROPE)).astype(F32)
        kr = _dot(krv, place, hi=True)
        kr = kr * C + _rope_swap(kr) * Sg
        return qo, knv.astype(F32) + jnp.tile(kr, (1, MLA_H))
    W = MLA_H * SLOT
    return rowwise(fn, [q, kn, (dtkr, SLOT, 1), cos, sin], [], [(W, BF16), (W, BF16)], [], name)


def rope_slot_bwd(dq, dk, cos, sin, name):
    def fn(dqv, dkv, cv, sv):
        C, Sg = _rope_coeffs(cv, sv)
        ct, stl = jnp.tile(C, (1, MLA_H)), jnp.tile(Sg, (1, MLA_H))
        dqo = dqv * ct - _rope_swap(dqv) * stl
        tot = dkv[:, 0:SLOT]
        for h in range(1, MLA_H):
            tot = tot + dkv[:, h * SLOT:(h + 1) * SLOT]
        u = tot * C - _rope_swap(tot) * Sg
        r = lax.broadcasted_iota(jnp.int32, (SLOT, SLOT), 0)
        c = lax.broadcasted_iota(jnp.int32, (SLOT, SLOT), 1)
        unplace = ((r == c + NOPE) & (c < ROPE)).astype(F32)
        return dqo, dkv, _dot(u, unplace, hi=True)
    W = MLA_H * SLOT
    return rowwise(fn, [dq, dk, cos, sin], [], [(W, BF16), (W, BF16), (SLOT, BF16)], [], name)


XA_BLK = 512


def xattn_fwd(q, k, v, nseq):
    T = q.shape[0]
    S = T // nseq
    M = k.shape[0] // nseq
    tq = min(XA_BLK, S)
    nq = S // tq
    scale = XA_D ** -0.5

    def body(q_ref, k_ref, v_ref, o_ref):
        s = _dot(q_ref[...], k_ref[...], "nt") * scale
        p = jnp.exp(s - jnp.max(s, axis=1, keepdims=True))
        p = p / jnp.sum(p, axis=1, keepdims=True)
        o_ref[...] = _dot(p, v_ref[...]).astype(o_ref.dtype)

    qs = pl.BlockSpec((tq, XA_D), lambda b, h, i: (b * nq + i, h))
    ks = pl.BlockSpec((M, XA_D), lambda b, h, i: (b, h))
    return pl.pallas_call(
        body, grid=(nseq, XA_H, nq), name="xattn_fwd", in_specs=[qs, ks, ks], out_specs=qs,
        out_shape=jax.ShapeDtypeStruct((T, XA_H * XA_D), BF16),
        compiler_params=_cp("parallel", "parallel", "parallel"),
    )(q, k, v)


def xattn_bwd(q, k, v, do, nseq):
    T = q.shape[0]
    S = T // nseq
    M = k.shape[0] // nseq
    tq = min(XA_BLK, S)
    nq = S // tq
    scale = XA_D ** -0.5

    def body(q_ref, k_ref, v_ref, do_ref, dq_ref, dk_ref, dv_ref):
        @pl.when(pl.program_id(2) == 0)
        def _():
            dk_ref[...] = jnp.zeros_like(dk_ref)
            dv_ref[...] = jnp.zeros_like(dv_ref)

        qb, kb, vb, dob = q_ref[...], k_ref[...], v_ref[...], do_ref[...]
        s = _dot(qb, kb, "nt") * scale
        p = jnp.exp(s - jnp.max(s, axis=1, keepdims=True))
        p = p / jnp.sum(p, axis=1, keepdims=True)
        dp = _dot(dob, vb, "nt")
        ds = p * (dp - jnp.sum(dp * p, axis=1, keepdims=True)) * scale
        dq_ref[...] = _dot(ds, kb).astype(dq_ref.dtype)
        dk_ref[...] += _dot(ds, qb, "tn")
        dv_ref[...] += _dot(p, dob, "tn")

    qs = pl.BlockSpec((tq, XA_D), lambda b, h, i: (b * nq + i, h))
    ks = pl.BlockSpec((M, XA_D), lambda b, h, i: (b, h))
    return pl.pallas_call(
        body, grid=(nseq, XA_H, nq), name="xattn_bwd", in_specs=[qs, ks, ks, qs], out_specs=[qs, ks, ks],
        out_shape=[jax.ShapeDtypeStruct((T, XA_H * XA_D), BF16), jax.ShapeDtypeStruct(k.shape, F32),
                   jax.ShapeDtypeStruct(k.shape, F32)],
        compiler_params=_cp("parallel", "parallel", "arbitrary"),
    )(q, k, v, do)


CONV_BLK = 256


def _shift_down(x, s, rows):
    if s == 0:
        return x
    return jnp.where(rows >= s, pltpu.roll(x, s, axis=0), 0.0)


def _shift_up(x, s, rows):
    if s == 0:
        return x
    S = x.shape[0]
    return jnp.where(rows < S - s, pltpu.roll(x, S - s, axis=0), 0.0)


def conv_fwd(x, w, b, nseq):
    T, CH = x.shape
    S = T // nseq

    def body(x_ref, w_ref, b_ref, o_ref):
        xv = x_ref[...]
        rows = lax.broadcasted_iota(jnp.int32, (S, 1), 0)
        c = jnp.zeros_like(xv) + b_ref[...]
        for kk in range(CONV_K):
            c = c + w_ref[kk:kk + 1, :] * _shift_down(xv, CONV_K - 1 - kk, rows)
        o_ref[...] = c * _sigmoid(c)

    xs = pl.BlockSpec((S, CONV_BLK), lambda j, bb: (bb, j))
    return pl.pallas_call(
        body, grid=(CH // CONV_BLK, nseq), name="conv_fwd",
        in_specs=[xs, pl.BlockSpec((CONV_K, CONV_BLK), lambda j, bb: (0, j)), pl.BlockSpec((1, CONV_BLK), lambda j, bb: (0, j))],
        out_specs=xs, out_shape=jax.ShapeDtypeStruct((T, CH), F32),
        compiler_params=_cp("parallel", "parallel"),
    )(x, w, b)


def conv_bwd(x, w, b, dout, nseq):
    T, CH = x.shape
    S = T // nseq

    def body(x_ref, w_ref, b_ref, do_ref, dx_ref, dw_ref, db_ref):
        @pl.when(pl.program_id(1) == 0)
        def _():
            dw_ref[...] = jnp.zeros_like(dw_ref)
            db_ref[...] = jnp.zeros_like(db_ref)

        xv = x_ref[...]
        rows = lax.broadcasted_iota(jnp.int32, (S, 1), 0)
        c = jnp.zeros_like(xv) + b_ref[...]
        sh = [_shift_down(xv, CONV_K - 1 - kk, rows) for kk in range(CONV_K)]
        for kk in range(CONV_K):
            c = c + w_ref[kk:kk + 1, :] * sh[kk]
        sg = _sigmoid(c)
        dc = do_ref[...] * sg * (1.0 + c * (1.0 - sg))
        dx = jnp.zeros_like(xv)
        for kk in range(CONV_K):
            dx = dx + w_ref[kk:kk + 1, :] * _shift_up(dc, CONV_K - 1 - kk, rows)
            dw_ref[kk:kk + 1, :] += jnp.sum(dc * sh[kk], axis=0, keepdims=True)
        dx_ref[...] = dx.astype(dx_ref.dtype)
        db_ref[...] += jnp.sum(dc, axis=0, keepdims=True)

    xs = pl.BlockSpec((S, CONV_BLK), lambda j, bb: (bb, j))
    ws = pl.BlockSpec((CONV_K, CONV_BLK), lambda j, bb: (0, j))
    bs = pl.BlockSpec((1, CONV_BLK), lambda j, bb: (0, j))
    return pl.pallas_call(
        body, grid=(CH // CONV_BLK, nseq), name="conv_bwd",
        in_specs=[xs, ws, bs, xs], out_specs=[xs, ws, bs],
        out_shape=[jax.ShapeDtypeStruct((T, CH), BF16), jax.ShapeDtypeStruct((CONV_K, CH), F32),
                   jax.ShapeDtypeStruct((1, CH), F32)],
        compiler_params=_cp("parallel", "arbitrary"),
    )(x, w, b, dout)


def _dims(a, b, mode):
    M = a.shape[1] if mode[0] == "t" else a.shape[0]
    K = a.shape[0] if mode[0] == "t" else a.shape[1]
    N = b.shape[0] if mode[1] == "t" else b.shape[1]
    return M, K, N


def _tile(dim, prefs):
    for p in prefs:
        if dim % p == 0:
            return p
    return dim


def mm(groups, out_dtypes, name, tm=None, tn=None, tk=None, epi=None, extras=()):
    a0, b0, m0 = groups[0][0]
    M, K0, N = _dims(a0, b0, m0)
    tm = tm or _tile(M, (1024, 512, 256, 128))
    tn = tn or _tile(N, (512, 256, 128))
    flat = [p for g in groups for p in g]
    nk = 1 if tk is None else K0 // tk
    in_specs, args = [], []
    for a, b, mode in flat:
        _, K, _ = _dims(a, b, mode)
        kb = K if tk is None else tk
        in_specs.append(pl.BlockSpec((kb, tm), lambda i, j, k: (k, i)) if mode[0] == "t"
                        else pl.BlockSpec((tm, kb), lambda i, j, k: (i, k)))
        in_specs.append(pl.BlockSpec((tn, kb), lambda i, j, k: (j, k)) if mode[1] == "t"
                        else pl.BlockSpec((kb, tn), lambda i, j, k: (k, j)))
        args += [a, b]
    for e in extras:
        in_specs.append(pl.BlockSpec((tm, tn), lambda i, j, k: (i, j)))
        args.append(e)
    n_in = len(args)
    n_out = len(out_dtypes)
    ng = len(groups)
    sizes = [len(g) for g in groups]

    def body(*refs):
        ins, outs, accs = refs[:n_in], refs[n_in:n_in + n_out], refs[n_in + n_out:]
        kk = pl.program_id(2)
        vals, pos = [], 0
        for gi in range(ng):
            acc = None
            for _ in range(sizes[gi]):
                mode = flat[pos // 2][2]
                d = _dot(ins[pos][...], ins[pos + 1][...], mode)
                acc = d if acc is None else acc + d
                pos += 2
            vals.append(acc)
        ex = [r[...] for r in ins[2 * len(flat):]]

        def finish(accv):
            res = epi(accv, ex) if epi is not None else tuple(accv)
            for o, r in zip(outs, res):
                o[...] = r.astype(o.dtype)

        if nk == 1:
            finish(vals)
        else:
            @pl.when(kk == 0)
            def _():
                for ar, vv in zip(accs, vals):
                    ar[...] = vv

            @pl.when(kk > 0)
            def _():
                for ar, vv in zip(accs, vals):
                    ar[...] += vv

            @pl.when(kk == nk - 1)
            def _():
                finish([ar[...] for ar in accs])

    return pl.pallas_call(
        body, grid=(M // tm, N // tn, nk), name=name, in_specs=in_specs,
        out_specs=[pl.BlockSpec((tm, tn), lambda i, j, k: (i, j)) for _ in out_dtypes],
        out_shape=[jax.ShapeDtypeStruct((M, N), dt) for dt in out_dtypes],
        scratch_shapes=[pltpu.VMEM((tm, tn), F32) for _ in range(ng if nk > 1 else 0)],
        compiler_params=_cp("parallel", "parallel", "arbitrary"),
    )(*args)


def mm1(a, b, mode, out_dtype, name, **kw):
    return mm([[(a, b, mode)]], [out_dtype], name, **kw)[0]


ROW_BLK = 256


def rowwise(fn, rows, consts, outs, accs, name, tb=ROW_BLK):
    rows = [r if isinstance(r, tuple) else (r, r.shape[1], 0) for r in rows]
    T = rows[0][0].shape[0]
    tb = min(tb, T)
    n_r, n_c, n_o, n_a = len(rows), len(consts), len(outs), len(accs)

    def body(*refs):
        vals = [r[...] for r in refs[:n_r + n_c]]
        res = fn(*vals)
        o_refs = refs[n_r + n_c:n_r + n_c + n_o]
        a_refs = refs[n_r + n_c + n_o:]
        for o, r in zip(o_refs, res[:n_o]):
            o[...] = r.astype(o.dtype)
        if n_a:
            @pl.when(pl.program_id(0) == 0)
            def _():
                for ar in a_refs:
                    ar[...] = jnp.zeros_like(ar)
            for ar, r in zip(a_refs, res[n_o:]):
                ar[...] += r

    return pl.pallas_call(
        body, grid=(T // tb,), name=name,
        in_specs=[pl.BlockSpec((tb, w), functools.partial(lambda i, j: (i, j), j=j)) for _, w, j in rows]
        + [pl.BlockSpec(c.shape, lambda i: (0, 0)) for c in consts],
        out_specs=[pl.BlockSpec((tb, d), lambda i: (i, 0)) for d, _ in outs]
        + [pl.BlockSpec(s, lambda i: (0, 0)) for s in accs],
        out_shape=[jax.ShapeDtypeStruct((T, d), dt) for d, dt in outs]
        + [jax.ShapeDtypeStruct(s, F32) for s in accs],
        compiler_params=_cp("arbitrary" if n_a else "parallel"),
    )(*[r[0] for r in rows], *consts)


def _rms_stats(x):
    r = lax.rsqrt(jnp.mean(x * x, axis=-1, keepdims=True) + EPS)
    return r, x * r


def _rms_bwd(x, g, dy):
    r, xn = _rms_stats(x)
    dyg = dy * g
    dx = r * (dyg - xn * jnp.mean(dyg * xn, axis=-1, keepdims=True))
    return dx, jnp.sum(dy * xn, axis=0, keepdims=True)


def rms_fwd(x, g, name):
    return rowwise(lambda xv, gv: (_rms_stats(xv)[1] * gv,), [x], [g], [(x.shape[1], BF16)], [], name)[0]


def rms_bwd(x, g, dy, name, resid=None, dx_dtype=F32):
    def fn(*v):
        if resid is None:
            xv, dyv, gv = v
            dx, dg = _rms_bwd(xv, gv, dyv)
        else:
            xv, dyv, rv, gv = v
            dx, dg = _rms_bwd(xv, gv, dyv)
            dx = dx + rv
        return dx, dg
    rows = [x, dy] + ([] if resid is None else [resid])
    return rowwise(fn, rows, [g], [(x.shape[1], dx_dtype)], [(1, x.shape[1])], name)


def resid_fwd(x, h, g, wgt, name):
    return rowwise(lambda xv, hv, gv: (xv + wgt * _rms_stats(hv)[1] * gv,), [x, h], [g], [(x.shape[1], F32)], [], name)[0]


def resid_bwd(h, g, dy, wgt, name):
    def fn(hv, dyv, gv):
        dx, dg = _rms_bwd(hv, gv, dyv)
        return wgt * dx, wgt * dg
    return rowwise(fn, [h, dy], [g], [(h.shape[1], BF16)], [(1, h.shape[1])], name)


def _silu_parts(g):
    s = _sigmoid(g)
    return g * s, s * (1.0 + g * (1.0 - s))


def gated_norm_fwd(y, z, g, name):
    W = SSD_INNER // SSD_G

    def fn(yv, zv, gv):
        yg = yv * _silu_parts(zv)[0]
        return (jnp.concatenate([_rms_stats(yg[:, i * W:(i + 1) * W])[1] for i in range(SSD_G)], axis=1) * gv,)
    return rowwise(fn, [y, z], [g], [(SSD_INNER, BF16)], [], name)[0]


def gated_norm_bwd(y, z, dyn, g, name):
    W = SSD_INNER // SSD_G

    def fn(yv, zv, dv, gv):
        sil, dsil = _silu_parts(zv)
        yg = yv * sil
        parts = [_rms_bwd(yg[:, i * W:(i + 1) * W], gv[:, i * W:(i + 1) * W], dv[:, i * W:(i + 1) * W]) for i in range(SSD_G)]
        dyg = jnp.concatenate([p[0] for p in parts], axis=1)
        dg = jnp.concatenate([p[1] for p in parts], axis=1)
        return dyg * sil, dyg * yv * dsil, dg
    return rowwise(fn, [y, z, dyn], [g], [(SSD_INNER, F32), (SSD_INNER, BF16)], [(1, SSD_INNER)], name)


def merge_fwd(gl, ys, ym, gb, name):
    def fn(glv, ysv, ymv, gbv):
        gt = _sigmoid(glv + gbv)
        return (gt[:, :D] * ysv + gt[:, D:] * ymv,)
    return rowwise(fn, [gl, ys, ym], [gb], [(D, BF16)], [], name)[0]


def merge_bwd(gl, ys, ym, dm, gb, name):
    def fn(glv, ysv, ymv, dmv, gbv):
        gt = _sigmoid(glv + gbv)
        gs, gm = gt[:, :D], gt[:, D:]
        dgl = jnp.concatenate([dmv * ysv * gs * (1.0 - gs), dmv * ymv * gm * (1.0 - gm)], axis=1)
        return dmv * gs, dmv * gm, dgl, jnp.sum(dgl, axis=0, keepdims=True)
    return rowwise(fn, [gl, ys, ym, dm], [gb], [(D, BF16), (D, BF16), (2 * D, BF16)], [(1, 2 * D)], name)


def rope_rot(x1, x2, cos, sin, name):
    fn = lambda a, b, c, s: (a * c - b * s, a * s + b * c)
    return rowwise(fn, [x1, x2, cos, sin], [], [(x1.shape[1], BF16), (x1.shape[1], BF16)], [], name)


def loss_head(y, tgt, name):
    def fn(yv, tv):
        d = yv - tv
        part = 0.5 * jnp.sum(jnp.sum(d * d, axis=1, keepdims=True), axis=0, keepdims=True) / D
        return d / D, jnp.broadcast_to(part, (1, 128))
    return rowwise(fn, [y, tgt], [], [(D, F32)], [(1, 128)], name)


def adamw(w, g, m, v, name):
    R, C = w.shape
    tb = _tile(R, (256, 128, 64, 32, 16, 8))

    def fn(wv, gv, mv, vv):
        mn = B1 * mv + (1.0 - B1) * gv
        vn = B2 * vv + (1.0 - B2) * (gv * gv)
        mh = mn / (1.0 - B1 ** STEP)
        vh = vn / (1.0 - B2 ** STEP)
        return -LR * (mh / (jnp.sqrt(vh) + AEPS) + WD * wv), mn, vn
    return rowwise(fn, [w, g, m, v], [], [(C, F32)] * 3, [], name, tb=tb)


def _me():
    return lax.axis_index("x"), lax.axis_index("y"), lax.axis_index("c")


def _dev_index():
    x, y, c = _me()
    return 4 * x + 2 * y + c


HBM_SPEC = pl.BlockSpec(memory_space=pl.ANY)


def allgather_rows(shards):
    n = len(shards)

    def body(*refs):
        x_refs, out_refs = refs[:n], refs[n:2 * n]
        send_sems, recv_sems, local_sems = refs[2 * n:]
        x, y, c = _me()
        me, sibling = (x, y, c), (x, y, 1 - c)
        chips = [(1 - x, y), (x, 1 - y), (1 - x, 1 - y)]

        def slot(i, px, py, pc):
            return out_refs[i].at[4 * px + 2 * py + pc]

        def copy(i, k, block, to, src=None):
            return pltpu.make_async_remote_copy(
                src_ref=slot(i, *block) if src is None else src, dst_ref=slot(i, *block),
                send_sem=send_sems.at[7 * i + k], recv_sem=recv_sems.at[7 * i + k], device_id=to, device_id_type=MESH)

        mine = [pltpu.make_async_copy(x_refs[i], slot(i, *me), local_sems.at[i]) for i in range(n)]
        for cp in mine:
            cp.start()
        first = []
        for i in range(n):
            first.append(copy(i, 0, me, sibling, src=x_refs[i]))
            first += [copy(i, 1 + j, me, (*chip, c), src=x_refs[i]) for j, chip in enumerate(chips)]
        for cp in first:
            cp.start()
        passed = [[copy(i, 4 + j, (*chip, c), sibling) for j, chip in enumerate(chips)] for i in range(n)]
        for i in range(n):
            for j, chip in enumerate(chips):
                copy(i, 1 + j, (*chip, c), me).wait_recv()
                passed[i][j].start()
        for i in range(n):
            copy(i, 0, sibling, me).wait_recv()
            for j, chip in enumerate(chips):
                copy(i, 4 + j, (*chip, 1 - c), me).wait_recv()
        for cp in first + [cp for row in passed for cp in row]:
            cp.wait_send()
        for cp in mine:
            cp.wait()

    return pl.pallas_call(
        body, name="allgather_weights",
        out_shape=[jax.ShapeDtypeStruct((N_DEV,) + s.shape, s.dtype) for s in shards],
        in_specs=[HBM_SPEC] * n, out_specs=[HBM_SPEC] * n,
        scratch_shapes=[pltpu.SemaphoreType.DMA((7 * n,)), pltpu.SemaphoreType.DMA((7 * n,)), pltpu.SemaphoreType.DMA((n,))],
    )(*shards)


def exchange_grads(groups):
    ng = len(groups)
    sizes = [len(g) for g in groups]
    rows = [[pc.shape[1] for pc in g] for g in groups]
    tot = [sum(r) for r in rows]
    n_in = sum(sizes)

    def body(*refs):
        ins, outs = refs[:n_in], refs[n_in:n_in + ng]
        send_sems, recv_sems, local_sems = refs[n_in + ng:]
        x, y, c = _me()
        me = 4 * x + 2 * y + c
        pieces, pos = [], 0
        for gi in range(ng):
            pieces.append(ins[pos:pos + sizes[gi]])
            pos += sizes[gi]

        def dst(gi, i, slot):
            return outs[gi].at[slot, pl.ds(sum(rows[gi][:i]), rows[gi][i])]

        for gi in range(ng):
            for i, pc in enumerate(pieces[gi]):
                pltpu.make_async_copy(pc.at[me], dst(gi, i, me), local_sems.at[gi]).start()
        whole = []
        for k in range(1, N_DEV):
            px = 1 - x if k & 4 else x
            py = 1 - y if k & 2 else y
            pc_ = 1 - c if k & 1 else c
            peer = 4 * px + 2 * py + pc_
            kw = dict(device_id=(px, py, pc_), device_id_type=MESH)
            for gi in range(ng):
                sems = dict(send_sem=send_sems.at[7 * gi + k - 1], recv_sem=recv_sems.at[7 * gi + k - 1])
                for i, pc in enumerate(pieces[gi]):
                    pltpu.make_async_remote_copy(src_ref=pc.at[peer], dst_ref=dst(gi, i, me), **sems, **kw).start()
                whole.append(pltpu.make_async_remote_copy(src_ref=outs[gi].at[peer], dst_ref=outs[gi].at[peer], **sems, **kw))
        for cp in whole:
            cp.wait_recv()
        for cp in whole:
            cp.wait_send()
        for gi in range(ng):
            pltpu.make_async_copy(outs[gi].at[me], outs[gi].at[me], local_sems.at[gi]).wait()

    return pl.pallas_call(
        body, name="exchange_grads",
        out_shape=[jax.ShapeDtypeStruct((N_DEV, tot[gi], g[0].shape[2]), g[0].dtype) for gi, g in enumerate(groups)],
        in_specs=[HBM_SPEC] * n_in, out_specs=[HBM_SPEC] * ng,
        scratch_shapes=[pltpu.SemaphoreType.DMA((7 * ng,)), pltpu.SemaphoreType.DMA((7 * ng,)), pltpu.SemaphoreType.DMA((ng,))],
    )(*[pc for g in groups for pc in g])


def sum_slots(recv, name, tr):
    n, R, C = recv.shape

    def body(r_ref, o_ref):
        acc = r_ref[0].astype(F32)
        for s in range(1, n):
            acc = acc + r_ref[s].astype(F32)
        o_ref[...] = acc

    return pl.pallas_call(
        body, grid=(R // tr,), name=name,
        in_specs=[pl.BlockSpec((n, tr, C), lambda i: (0, i, 0))], out_specs=pl.BlockSpec((tr, C), lambda i: (i, 0)),
        out_shape=jax.ShapeDtypeStruct((R, C), F32), compiler_params=_cp("parallel"),
    )(recv)


PACK_W, FLAT_W = 1024, 128
MAIN = [
    ("ffn1_w_gate", "col"), ("ffn1_w_up", "col"), ("ffn1_w_down", "row"),
    ("ffn2_w_gate", "col"), ("ffn2_w_up", "col"), ("ffn2_w_down", "row"),
    ("w_ssd_proj", "row"), ("w_mla_proj", "row"), ("w_out", "row"),
    ("w_xq", "row"), ("w_xk", "row"), ("w_xv", "row"), ("w_xo", "row"),
    ("w_uk", "col"), ("w_uv", "col"),
]
FLAT = [("w_in", "col"), ("w_uq", "col")]
BIG = MAIN + FLAT
SMALL = ["ffn1_pre_g", "ffn1_post_g", "mix_pre_g", "conv_b", "dt_bias", "a_log", "d_skip", "ssd_norm_g", "q_norm_g",
         "kv_norm_g", "gate_bias", "mix_post_g", "xa_pre_g", "mem_norm_g", "xa_post_g", "ffn2_pre_g", "ffn2_post_g"]
WEIGHTS = ['ffn1_pre_g', 'ffn1_w_gate', 'ffn1_w_up', 'ffn1_w_down', 'ffn1_post_g', 'mix_pre_g', 'w_in', 'conv_w', 'conv_b',
           'dt_bias', 'a_log', 'd_skip', 'ssd_norm_g', 'w_ssd_proj', 'q_norm_g', 'w_uq', 'kv_norm_g', 'w_uk', 'w_uv',
           'w_mla_proj', 'gate_bias', 'w_out', 'mix_post_g', 'xa_pre_g', 'mem_norm_g', 'w_xq', 'w_xk', 'w_xv', 'w_xo',
           'xa_post_g', 'ffn2_pre_g', 'ffn2_w_gate', 'ffn2_w_up', 'ffn2_w_down', 'ffn2_post_g']


def _pack_rows(w, kind, width):
    m = w[0].T if kind == "col" else w[0]
    return m.reshape(-1, width)


def _pad_rows(a, mult):
    r = (-a.shape[0]) % mult
    return a if r == 0 else jnp.concatenate([a, jnp.zeros((r,) + a.shape[1:], a.dtype)], axis=0)


def _pack_small(vals, loss_row=None, conv_w=None):
    rows = []
    for v in vals:
        f = v.reshape(-1)
        f = jnp.concatenate([f, jnp.zeros(((-f.shape[0]) % 128,), F32)])
        rows.append(f.reshape(-1, 128))
    if conv_w is not None:
        rows.append(conv_w.reshape(-1, 128))
    if loss_row is not None:
        rows.append(loss_row)
    return _pad_rows(jnp.concatenate(rows, axis=0), 8)


def _unpack_small(buf, shapes):
    out, r = [], 0
    for shp in shapes:
        n = math.prod(shp)
        nr = -(-n // 128)
        out.append(buf[r:r + nr].reshape(-1)[:n].reshape(shp))
        r += nr
    return out, r


def _tn(a, b, name, out_dtype=BF16):
    M, N = a.shape[1], b.shape[1]
    T = a.shape[0]
    tm = M if M <= 1536 else M // 2
    tk = 512 if T % 512 == 0 and T > 512 else None
    return mm1(a, b, "tn", out_dtype, name, tm=tm, tn=N, tk=tk)


def _ffn_fwd(x, gpre, gpost, wg_t, wu_t, wd, tag):
    h = rms_fwd(x, gpre, tag + "_pre")
    swi = lambda accs, ex: (accs[0], accs[1], _silu_parts(accs[0])[0] * accs[1])
    G, U, A = mm([[(h, wg_t, "nt")], [(h, wu_t, "nt")]], [F32, F32, BF16], tag + "_gate_up", tn=256, epi=swi)
    H = mm1(A, wd, "nn", F32, tag + "_down")
    y = resid_fwd(x, H, gpost, FFN_RES, tag + "_post")
    return y, (x, h, G, U, A, H)


def _ffn_bwd(dy, saved, gpre, gpost, wg_t, wu_t, wd, tag):
    x, h, G, U, A, H = saved
    dH, dgpost = resid_bwd(H, gpost, dy, FFN_RES, tag + "_post_bwd")

    def dswi(accs, ex):
        sil, dsil = _silu_parts(ex[0])
        return accs[0] * ex[1] * dsil, accs[0] * sil
    dG, dU = mm([[(dH, wd, "nt")]], [BF16, BF16], tag + "_down_bwd", tn=256, epi=dswi, extras=[G, U])
    dwd = _tn(A, dH, tag + "_dwd")
    dh = mm([[(dG, wg_t, "nn"), (dU, wu_t, "nn")]], [F32], tag + "_gate_up_bwd", tm=512)[0]
    dwg_t = _tn(dG, h, tag + "_dwg")
    dwu_t = _tn(dU, h, tag + "_dwu")
    dx, dgpre = rms_bwd(x, gpre, dh, tag + "_pre_bwd", resid=dy)
    return dx, dgpre, dgpost, dwg_t, dwu_t, dwd


def _rope_tables(positions):
    inv = ROPE_THETA ** (-jnp.arange(0, ROPE, 2, dtype=F32) / ROPE)
    ang = positions.astype(F32).reshape(-1)[:, None] * inv
    return jnp.cos(ang), jnp.sin(ang)


def _heads(t, nseq, width):
    T = t.shape[0]
    return t.reshape(nseq, T // nseq, MLA_H, width).transpose(0, 2, 1, 3)


def _unheads(t):
    b, h, s, w = t.shape
    return t.transpose(0, 2, 1, 3).reshape(b * s, h, w)


def _local_step(x, mem, positions, tgt, W, p):
    nseq = x.shape[0]
    T = nseq * x.shape[1]
    x0 = x.reshape(T, D)
    mem2 = mem.reshape(-1, D)
    cos, sin = _rope_tables(positions)

    x1, ffn1 = _ffn_fwd(x0, p["ffn1_pre_g"], p["ffn1_post_g"], W["ffn1_w_gate"], W["ffn1_w_up"], W["ffn1_w_down"], "ffn1")

    w_in_t = W["w_in"]
    bounds = [0]
    for n in (SSD_INNER, CONV_CH, SSD_H, QR, KVR, ROPE, 2 * D):
        bounds.append(bounds[-1] + n)
    wt_z, wt_xbc, wt_dt, wt_q, wt_kv, wt_kr, wt_gate = [w_in_t[bounds[i]:bounds[i + 1]] for i in range(7)]
    wt_dt, wt_kr = _pad_rows(wt_dt, SLOT), _pad_rows(wt_kr, SLOT)
    wt_dtkr = jnp.concatenate([wt_dt, wt_kr], axis=0)
    hm = rms_fwd(x1, p["mix_pre_g"], "mix_pre")
    z = mm1(hm, wt_z, "nt", F32, "in_z")
    xbc = mm1(hm, wt_xbc, "nt", F32, "in_xbc")
    q_c = mm1(hm, wt_q, "nt", F32, "in_q", tn=QR)
    kv_c = mm1(hm, wt_kv, "nt", F32, "in_kv")
    dtkr = mm1(hm, wt_dtkr, "nt", F32, "in_dtkr")
    gl = mm1(hm, wt_gate, "nt", F32, "in_gate")

    xbc_act = conv_fwd(xbc, p["conv_w"], p["conv_b"], nseq)
    y_ssd_core, prev = ssd_fwd(xbc_act, dtkr, p["dt_bias"], p["a_log"], p["d_skip"], nseq)
    yn = gated_norm_fwd(y_ssd_core, z, p["ssd_norm_g"], "ssd_norm")
    y_ssd = mm1(yn, W["w_ssd_proj"], "nn", F32, "ssd_proj")

    slot_rows = lambda wt, per: jnp.pad(wt.reshape(MLA_H, per, -1), ((0, 0), (0, SLOT - per), (0, 0))).reshape(MLA_H * SLOT, -1)
    wq_s, wk_s, wv_s = slot_rows(W["w_uq"], QK), slot_rows(W["w_uk"], NOPE), slot_rows(W["w_uv"], VD)
    wo_s = slot_rows(W["w_mla_proj"], VD)
    qn = rms_fwd(q_c, p["q_norm_g"], "q_norm")
    q_s = mm1(qn, wq_s, "nt", F32, "uq")
    kvn = rms_fwd(kv_c, p["kv_norm_g"], "kv_norm")
    kn_s = mm1(kvn, wk_s, "nt", BF16, "uk")
    v_s = mm1(kvn, wv_s, "nt", BF16, "uv")
    cos16, sin16 = cos, sin
    Qc, Kc = rope_slot_fwd(q_s, kn_s, dtkr, cos16, sin16, "rope")
    o_s, lse = attn_slot_fwd(Qc, Kc, v_s, nseq)
    y_mla = mm1(o_s, wo_s, "nn", F32, "mla_proj")

    merged = merge_fwd(gl, y_ssd, y_mla, p["gate_bias"], "merge")
    hmix = mm1(merged, W["w_out"], "nn", F32, "mix_out")
    x2 = resid_fwd(x1, hmix, p["mix_post_g"], 1.0, "mix_post")

    hq = rms_fwd(x2, p["xa_pre_g"], "xa_pre")
    mn = rms_fwd(mem2, p["mem_norm_g"], "mem_norm")
    xq = mm1(hq, W["w_xq"], "nn", BF16, "xq")
    xk = mm1(mn, W["w_xk"], "nn", BF16, "xk")
    xv = mm1(mn, W["w_xv"], "nn", BF16, "xv")
    xo = xattn_fwd(xq, xk, xv, nseq)
    ho = mm1(xo, W["w_xo"], "nn", F32, "xo")
    x3 = resid_fwd(x2, ho, p["xa_post_g"], 1.0, "xa_post")

    x4, ffn2 = _ffn_fwd(x3, p["ffn2_pre_g"], p["ffn2_post_g"], W["ffn2_w_gate"], W["ffn2_w_up"], W["ffn2_w_down"], "ffn2")
    dx4, loss_row = loss_head(x4, tgt.reshape(T, D), "loss")

    gw, gs = {}, {}
    dx3, gs["ffn2_pre_g"], gs["ffn2_post_g"], gw["ffn2_w_gate"], gw["ffn2_w_up"], gw["ffn2_w_down"] = _ffn_bwd(
        dx4, ffn2, p["ffn2_pre_g"], p["ffn2_post_g"], W["ffn2_w_gate"], W["ffn2_w_up"], W["ffn2_w_down"], "ffn2")

    dho, gs["xa_post_g"] = resid_bwd(ho, p["xa_post_g"], dx3, 1.0, "xa_post_bwd")
    dxo = mm1(dho, W["w_xo"], "nt", BF16, "xo_bwd")
    gw["w_xo"] = _tn(xo, dho, "d_w_xo")
    dxq, dxk, dxv = xattn_bwd(xq, xk, xv, dxo, nseq)
    dhq = mm1(dxq, W["w_xq"], "nt", F32, "xq_bwd")
    gw["w_xq"] = _tn(hq, dxq, "d_w_xq")
    dmn = mm([[(dxk, W["w_xk"], "nt"), (dxv, W["w_xv"], "nt")]], [F32], "xkv_bwd")[0]
    gw["w_xk"] = _tn(mn, dxk, "d_w_xk")
    gw["w_xv"] = _tn(mn, dxv, "d_w_xv")
    _, gs["mem_norm_g"] = rms_bwd(mem2, p["mem_norm_g"], dmn, "mem_norm_bwd", dx_dtype=BF16)
    dx2, gs["xa_pre_g"] = rms_bwd(x2, p["xa_pre_g"], dhq, "xa_pre_bwd", resid=dx3)

    dhmix, gs["mix_post_g"] = resid_bwd(hmix, p["mix_post_g"], dx2, 1.0, "mix_post_bwd")
    dmerged = mm1(dhmix, W["w_out"], "nt", F32, "mix_out_bwd")
    gw["w_out"] = _tn(merged, dhmix, "d_w_out")
    dys, dym, dgl, gs["gate_bias"] = merge_bwd(gl, y_ssd, y_mla, dmerged, p["gate_bias"], "merge_bwd")

    dyn = mm1(dys, W["w_ssd_proj"], "nt", F32, "ssd_proj_bwd")
    gw["w_ssd_proj"] = _tn(yn, dys, "d_w_ssd_proj")
    dyc, dz, gs["ssd_norm_g"] = gated_norm_bwd(y_ssd_core, z, dyn, p["ssd_norm_g"], "ssd_norm_bwd")
    dxbc_act, ddtr, gs["dt_bias"], gs["a_log"], gs["d_skip"] = ssd_bwd(
        xbc_act, dtkr, p["dt_bias"], p["a_log"], p["d_skip"], prev, dyc, nseq)
    dxbc, gs["conv_w"], gs["conv_b"] = conv_bwd(xbc, p["conv_w"], p["conv_b"], dxbc_act, nseq)

    unslot = lambda g, per: g.reshape(MLA_H, SLOT, -1)[:, :per].reshape(MLA_H * per, -1)
    do_s = mm1(dym, wo_s, "nt", BF16, "mla_proj_bwd")
    gw["w_mla_proj"] = unslot(_tn(o_s, dym, "d_w_mla_proj"), VD)
    dQc, dKc, dv_s = attn_slot_bwd(Qc, Kc, v_s, o_s, lse, do_s, nseq)
    dq_s, dkn_s, dkr = rope_slot_bwd(dQc, dKc, cos16, sin16, "rope_bwd")
    dqn = mm1(dq_s, wq_s, "nn", F32, "uq_bwd", tn=QR)
    gw["w_uq"] = unslot(_tn(dq_s, qn, "d_w_uq"), QK)
    dq_c, gs["q_norm_g"] = rms_bwd(q_c, p["q_norm_g"], dqn, "q_norm_bwd", dx_dtype=BF16)
    dkvn = mm([[(dkn_s, wk_s, "nn"), (dv_s, wv_s, "nn")]], [F32], "ukv_bwd")[0]
    gw["w_uk"] = unslot(_tn(dkn_s, kvn, "d_w_uk"), NOPE)
    gw["w_uv"] = unslot(_tn(dv_s, kvn, "d_w_uv"), VD)
    dkv_c, gs["kv_norm_g"] = rms_bwd(kv_c, p["kv_norm_g"], dkvn, "kv_norm_bwd", dx_dtype=BF16)

    dhm = mm([[(dz, wt_z, "nn"), (dxbc, wt_xbc, "nn"), (ddtr, wt_dt, "nn"), (dq_c, wt_q, "nn"), (dkv_c, wt_kv, "nn"),
               (dkr, wt_kr, "nn"), (dgl, wt_gate, "nn")]], [F32], "in_bwd", tm=512)[0]
    gw["w_in"] = jnp.concatenate([_tn(dz, hm, "d_w_in_z"), _tn(dxbc, hm, "d_w_in_xbc"), _tn(ddtr, hm, "d_w_in_dt")[:SSD_H],
                                  _tn(dq_c, hm, "d_w_in_q"), _tn(dkv_c, hm, "d_w_in_kv"), _tn(dkr, hm, "d_w_in_kr")[:ROPE],
                                  _tn(dgl, hm, "d_w_in_gate")], axis=0)
    dx1, gs["mix_pre_g"] = rms_bwd(x1, p["mix_pre_g"], dhm, "mix_pre_bwd", resid=dx2)

    dx0, gs["ffn1_pre_g"], gs["ffn1_post_g"], gw["ffn1_w_gate"], gw["ffn1_w_up"], gw["ffn1_w_down"] = _ffn_bwd(
        dx1, ffn1, p["ffn1_pre_g"], p["ffn1_post_g"], W["ffn1_w_gate"], W["ffn1_w_up"], W["ffn1_w_down"], "ffn1")
    return loss_row, dx0.reshape(x.shape), gw, gs


def kernel(x, mem, positions, ffn1_pre_g, ffn1_w_gate, ffn1_w_up, ffn1_w_down, ffn1_post_g, mix_pre_g, w_in, conv_w, conv_b, dt_bias, a_log, d_skip, ssd_norm_g, w_ssd_proj, q_norm_g, w_uq, kv_norm_g, w_uk, w_uv, w_mla_proj, gate_bias, w_out, mix_post_g, xa_pre_g, mem_norm_g, w_xq, w_xk, w_xv, w_xo, xa_post_g, ffn2_pre_g, ffn2_w_gate, ffn2_w_up, ffn2_w_down, ffn2_post_g, loss_target, m_ffn1_pre_g, m_ffn1_w_gate, m_ffn1_w_up, m_ffn1_w_down, m_ffn1_post_g, m_mix_pre_g, m_w_in, m_conv_w, m_conv_b, m_dt_bias, m_a_log, m_d_skip, m_ssd_norm_g, m_w_ssd_proj, m_q_norm_g, m_w_uq, m_kv_norm_g, m_w_uk, m_w_uv, m_w_mla_proj, m_gate_bias, m_w_out, m_mix_post_g, m_xa_pre_g, m_mem_norm_g, m_w_xq, m_w_xk, m_w_xv, m_w_xo, m_xa_post_g, m_ffn2_pre_g, m_ffn2_w_gate, m_ffn2_w_up, m_ffn2_w_down, m_ffn2_post_g, v_ffn1_pre_g, v_ffn1_w_gate, v_ffn1_w_up, v_ffn1_w_down, v_ffn1_post_g, v_mix_pre_g, v_w_in, v_conv_w, v_conv_b, v_dt_bias, v_a_log, v_d_skip, v_ssd_norm_g, v_w_ssd_proj, v_q_norm_g, v_w_uq, v_kv_norm_g, v_w_uk, v_w_uv, v_w_mla_proj, v_gate_bias, v_w_out, v_mix_post_g, v_xa_pre_g, v_mem_norm_g, v_w_xq, v_w_xk, v_w_xv, v_w_xo, v_xa_post_g, v_ffn2_pre_g, v_ffn2_w_gate, v_ffn2_w_up, v_ffn2_w_down, v_ffn2_post_g):
    a = dict(locals())
    w = {n: a[n] for n in WEIGHTS}
    m = {n: a["m_" + n] for n in WEIGHTS}
    v = {n: a["v_" + n] for n in WEIGHTS}

    packs = {n: _pack_rows(w[n], kind, PACK_W if (n, kind) in MAIN else FLAT_W).astype(BF16) for n, kind in BIG}
    nrows = {n: pk.shape[0] for n, pk in packs.items()}
    cw_bits = lax.bitcast_convert_type(conv_w[0], BF16).reshape(-1, FLAT_W)
    cw_rows = cw_bits.shape[0]
    shard_main = jnp.concatenate([packs[n] for n, _ in MAIN], axis=0)
    shard_flat = _pad_rows(jnp.concatenate([packs[n] for n, _ in FLAT] + [cw_bits], axis=0), 16)
    g_main, g_flat = allgather_rows([shard_main, shard_flat])
    W = {}
    for group, gathered in ((MAIN, g_main), (FLAT, g_flat)):
        r0 = 0
        for n, kind in group:
            K = w[n].shape[1] if kind == "col" else PACK_W
            W[n] = gathered[:, r0:r0 + nrows[n]].reshape(-1, K)
            r0 += nrows[n]
    cw_shape = conv_w.shape[1:]
    cw_all = g_flat[:, r0:r0 + cw_rows].reshape((N_DEV,) + cw_shape + (2,))
    conv_w_full = lax.bitcast_convert_type(cw_all, F32).transpose(1, 0, 2).reshape(cw_shape[0], -1)
    p = {n: w[n] for n in SMALL}
    p["conv_w"] = conv_w_full

    loss_row, grad_x, gw, gs = _local_step(x, mem, positions, loss_target, W, p)

    sm = _pack_small([gs[n] for n in SMALL], loss_row=loss_row, conv_w=gs["conv_w"])
    recv_main, recv_flat, srecv = exchange_grads([
        [gw[n].reshape(N_DEV, nrows[n], PACK_W) for n, _ in MAIN],
        [gw[n].reshape(N_DEV, nrows[n], FLAT_W) for n, _ in FLAT],
        [jnp.broadcast_to(sm[None], (N_DEV,) + sm.shape)]])
    rows_main = sum_slots(recv_main, "sum_main", tr=_tile(recv_main.shape[1], (256, 128, 64, 32, 16)))
    rows_flat = sum_slots(recv_flat, "sum_flat", tr=recv_flat.shape[1])
    s_rows = sum_slots(srecv, "sum_small", tr=sm.shape[0])
    grads = {}
    for group, g_rows in ((MAIN, rows_main), (FLAT, rows_flat)):
        r0 = 0
        for n, kind in group:
            blk = g_rows[r0:r0 + nrows[n]]
            grads[n] = (blk.reshape(w[n].shape[2], w[n].shape[1]).T if kind == "col" else blk)[None]
            r0 += nrows[n]
    small_g, r1 = _unpack_small(s_rows, [w[n].shape for n in SMALL])
    for n, g in zip(SMALL, small_g):
        grads[n] = g
    ncw = math.prod(conv_w_full.shape) // 128
    cw_grad_full = s_rows[r1:r1 + ncw].reshape(conv_w_full.shape)
    wsh = conv_w.shape[2]
    grads["conv_w"] = lax.dynamic_slice_in_dim(cw_grad_full, _dev_index() * wsh, wsh, axis=1)[None]
    loss = s_rows[r1 + ncw, 0]

    delta, new_m, new_v = {}, {}, {}
    for n, _ in BIG + [("conv_w", "col")]:
        shp = w[n].shape
        d_, m_, v_ = adamw(w[n][0], grads[n][0], m[n][0], v[n][0], "adamw_" + n)
        delta[n], new_m[n], new_v[n] = d_.reshape(shp), m_.reshape(shp), v_.reshape(shp)
    sp = [_pack_small([t[n] for n in SMALL]) for t in (w, grads, m, v)]
    outs = adamw(sp[0], sp[1], sp[2], sp[3], "adamw_small")
    for t, buf in zip((delta, new_m, new_v), outs):
        vals, _ = _unpack_small(buf, [w[n].shape for n in SMALL])
        for n, val in zip(SMALL, vals):
            t[n] = val
    return (loss, grad_x, *[grads[n] for n in WEIGHTS], *[delta[n] for n in WEIGHTS],
            *[new_m[n] for n in WEIGHTS], *[new_v[n] for n in WEIGHTS])
```

```python
import functools
import math

import jax
import jax.numpy as jnp
from jax import lax
from jax.experimental import pallas as pl
from jax.experimental.pallas import tpu as pltpu

F32, BF16 = jnp.float32, jnp.bfloat16
HI = lax.Precision.HIGHEST
MESH = pl.DeviceIdType.MESH
N_DEV = 8

D = 1024
DFF = 2816
SSD_H, SSD_P, SSD_G, SSD_N, SSD_L = 16, 64, 2, 128, 128
SSD_INNER = SSD_H * SSD_P
CONV_K, CONV_CH = 4, 1536
MLA_H, QR, KVR, NOPE, ROPE, VD = 16, 384, 256, 64, 32, 64
QK = NOPE + ROPE
ROPE_THETA = 10000.0
XA_H, XA_D = 4, 256
EPS = 1e-6
FFN_RES = 0.5
LR, B1, B2, AEPS, WD, STEP = 0.001, 0.9, 0.999, 1e-08, 0.01, 10

VMEM_LIMIT = 56 * 2**20


def _cp(*sem):
    return pltpu.CompilerParams(dimension_semantics=sem, vmem_limit_bytes=VMEM_LIMIT)


def _sigmoid(x):
    return 1.0 / (1.0 + jnp.exp(-x))


def _softplus(x):
    return jnp.where(x > 20.0, x, jnp.log(1.0 + jnp.exp(jnp.minimum(x, 20.0))))


def _dot(a, b, dims="nn", hi=False):
    ca = 0 if dims[0] == "t" else 1
    cb = 1 if dims[1] == "t" else 0
    if hi:
        return lax.dot_general(a, b, (((ca,), (cb,)), ((), ())), precision=HI, preferred_element_type=F32)
    return lax.dot_general(a.astype(BF16), b.astype(BF16), (((ca,), (cb,)), ((), ())), preferred_element_type=F32)


def _ssd_common(dtr, dtb, alog):
    L = dtr.shape[0]
    dt = _softplus(dtr + dtb)
    a = -jnp.exp(alog)
    adt = dt * a
    r = lax.broadcasted_iota(jnp.int32, (L, L), 0)
    c = lax.broadcasted_iota(jnp.int32, (L, L), 1)
    lower = r >= c
    tri = lower.astype(F32)
    cs = _dot(tri, adt, "nn", hi=True)
    cs_t = _dot(adt, tri, "tt", hi=True)
    return dt, a, cs, cs_t, lower


def _head_expand():
    hh = lax.broadcasted_iota(jnp.int32, (SSD_H, SSD_INNER), 0)
    jj = lax.broadcasted_iota(jnp.int32, (SSD_H, SSD_INNER), 1)
    return ((jj >= hh * SSD_P) & (jj < hh * SSD_P + SSD_P)).astype(F32)


def _head_reduce():
    hh = lax.broadcasted_iota(jnp.int32, (SSD_INNER, SSD_H), 1)
    jj = lax.broadcasted_iota(jnp.int32, (SSD_INNER, SSD_H), 0)
    return ((jj >= hh * SSD_P) & (jj < hh * SSD_P + SSD_P)).astype(F32)


def ssd_fwd(xbc, dtr, dtb, alog, dsk, nseq):
    T = xbc.shape[0]
    S = T // nseq
    C = S // SSD_L
    L = SSD_L
    NP = SSD_H // 2

    def body(x_ref, b_ref, c_ref, dtr_ref, dtb_ref, alog_ref, dsk_ref, y_ref, prev_ref, st_ref):
        ci = pl.program_id(1)

        @pl.when(ci == 0)
        def _():
            st_ref[...] = jnp.zeros_like(st_ref)

        dt, a, cs, cs_t, lower = _ssd_common(dtr_ref[:, 0:SSD_H], dtb_ref[...], alog_ref[...])
        E = _head_expand()
        X = x_ref[...]
        dt_e = _dot(dt, E, hi=True)
        cs_e = _dot(cs, E, hi=True)
        csl_e = cs_e[L - 1:L, :]
        Xd = X * dt_e
        Xf = Xd * jnp.exp(csl_e - cs_e)
        e_e = jnp.exp(cs_e)
        y_ref[...] = _dot(dsk_ref[...], E, hi=True) * X
        lane = lax.broadcasted_iota(jnp.int32, (1, 2 * SSD_P), 1)
        rowp = lax.broadcasted_iota(jnp.int32, (2 * SSD_P, 1), 0)
        for g in range(SSD_G):
            Bg = b_ref[:, g * SSD_N:(g + 1) * SSD_N]
            Cg = c_ref[:, g * SSD_N:(g + 1) * SSD_N]
            cb = _dot(Cg, Bg, "nt")
            for pp in range(NP // SSD_G):
                p = g * (NP // SSD_G) + pp
                sl = slice(p * 2 * SSD_P, (p + 1) * 2 * SSD_P)
                Xd_p = Xd[:, sl]
                yd = jnp.zeros((L, 2 * SSD_P), F32)
                for q in range(2):
                    h = 2 * p + q
                    m = jnp.where(lower, jnp.exp(jnp.minimum(cs[:, h:h + 1] - cs_t[h:h + 1, :], 0.0)), 0.0)
                    mask = (lane >= q * SSD_P) & (lane < (q + 1) * SSD_P)
                    yd = yd + _dot(cb * m, jnp.where(mask, Xd_p, 0.0))
                S0 = st_ref[p]
                prev_ref[0, 0, p] = S0
                z = _dot(Cg, S0, "nt")
                y_ref[:, sl] += yd + z * e_e[:, sl]
                h0 = 2 * p
                dec = jnp.where(rowp < SSD_P, jnp.exp(cs[L - 1:L, h0:h0 + 1]), jnp.exp(cs[L - 1:L, h0 + 1:h0 + 2]))
                st_ref[p] = S0 * dec + _dot(Xf[:, sl], Bg, "tn")

    row = lambda b, c: (b * C + c, 0)
    return pl.pallas_call(
        body, grid=(nseq, C), name="ssd_fwd",
        in_specs=[pl.BlockSpec((L, SSD_INNER), row),
                  pl.BlockSpec((L, SSD_G * SSD_N), lambda b, c: (b * C + c, SSD_INNER // (SSD_G * SSD_N))),
                  pl.BlockSpec((L, SSD_G * SSD_N), lambda b, c: (b * C + c, SSD_INNER // (SSD_G * SSD_N) + 1)),
                  pl.BlockSpec((L, 128), row),
                  pl.BlockSpec((1, SSD_H), lambda b, c: (0, 0)),
                  pl.BlockSpec((1, SSD_H), lambda b, c: (0, 0)),
                  pl.BlockSpec((1, SSD_H), lambda b, c: (0, 0))],
        out_specs=[pl.BlockSpec((L, SSD_INNER), row),
                   pl.BlockSpec((1, 1, NP, 2 * SSD_P, SSD_N), lambda b, c: (b, c, 0, 0, 0))],
        out_shape=[jax.ShapeDtypeStruct((T, SSD_INNER), F32),
                   jax.ShapeDtypeStruct((nseq, C, NP, 2 * SSD_P, SSD_N), F32)],
        scratch_shapes=[pltpu.VMEM((NP, 2 * SSD_P, SSD_N), F32)],
        compiler_params=_cp("parallel", "arbitrary"),
    )(xbc, xbc, xbc, dtr, dtb, alog, dsk)


def ssd_bwd(xbc, dtr, dtb, alog, dsk, prev, dy, nseq):
    T = xbc.shape[0]
    S = T // nseq
    C = S // SSD_L
    L = SSD_L
    NP = SSD_H // 2

    def body(x_ref, b_ref, c_ref, dtr_ref, dtb_ref, alog_ref, dsk_ref, prev_ref, dy_ref,
             dxbc_ref, ddtr_ref, ddtb_ref, dalog_ref, ddsk_ref, ds_ref):
        bi = pl.program_id(0)
        ci = pl.program_id(1)

        @pl.when(ci == 0)
        def _():
            ds_ref[...] = jnp.zeros_like(ds_ref)

        @pl.when((ci == 0) & (bi == 0))
        def _():
            ddtb_ref[...] = jnp.zeros_like(ddtb_ref)
            dalog_ref[...] = jnp.zeros_like(dalog_ref)
            ddsk_ref[...] = jnp.zeros_like(ddsk_ref)

        dtr = dtr_ref[:, 0:SSD_H]
        dtb = dtb_ref[...]
        dt, a, cs, cs_t, lower = _ssd_common(dtr, dtb, alog_ref[...])
        upper = lax.broadcasted_iota(jnp.int32, (L, L), 1) >= lax.broadcasted_iota(jnp.int32, (L, L), 0)
        E = _head_expand()
        ET = _head_reduce()
        X = x_ref[...]
        dY = dy_ref[...]
        dt_e = _dot(dt, E, hi=True)
        cs_e = _dot(cs, E, hi=True)
        csl_e = cs_e[L - 1:L, :]
        f_e = jnp.exp(csl_e - cs_e)
        e_e = jnp.exp(cs_e)
        dsk_e = _dot(dsk_ref[...], E, hi=True)
        Xd = X * dt_e
        Xf = Xd * f_e
        lane = lax.broadcasted_iota(jnp.int32, (1, 2 * SSD_P), 1)
        rowp = lax.broadcasted_iota(jnp.int32, (2 * SSD_P, 1), 0)
        hsel = lax.broadcasted_iota(jnp.int32, (1, SSD_H), 1)
        dcs = jnp.zeros((L, SSD_H), F32)
        dcsl = jnp.zeros((1, SSD_H), F32)
        for g in range(SSD_G):
            Bg = b_ref[:, g * SSD_N:(g + 1) * SSD_N]
            Cg = c_ref[:, g * SSD_N:(g + 1) * SSD_N]
            cb = _dot(Cg, Bg, "nt")
            cbt = _dot(Bg, Cg, "nt")
            dB = jnp.zeros((L, SSD_N), F32)
            dC = jnp.zeros((L, SSD_N), F32)
            for pp in range(NP // SSD_G):
                p = g * (NP // SSD_G) + pp
                sl = slice(p * 2 * SSD_P, (p + 1) * 2 * SSD_P)
                Xd_p = Xd[:, sl]
                dY_p = dY[:, sl]
                dXd_p = jnp.zeros((L, 2 * SSD_P), F32)
                for q in range(2):
                    h = 2 * p + q
                    mask = (lane >= q * SSD_P) & (lane < (q + 1) * SSD_P)
                    col = cs[:, h:h + 1]
                    rw = cs_t[h:h + 1, :]
                    m = jnp.where(lower, jnp.exp(jnp.minimum(col - rw, 0.0)), 0.0)
                    mt = jnp.where(upper, jnp.exp(jnp.minimum(rw - col, 0.0)), 0.0)
                    dYm = jnp.where(mask, dY_p, 0.0)
                    dW = _dot(dYm, Xd_p, "nt")
                    dWt = _dot(Xd_p, dYm, "nt")
                    w = cb * m
                    wt = cbt * mt
                    dC = dC + _dot(dW * m, Bg)
                    dB = dB + _dot(dWt * mt, Cg)
                    dXd_p = dXd_p + jnp.where(mask, _dot(wt, dY_p), 0.0)
                    qcol = jnp.sum(dW * w, axis=1, keepdims=True) - jnp.sum(dWt * wt, axis=1, keepdims=True)
                    dcs = dcs + qcol * (hsel == h).astype(F32)
                S0 = prev_ref[0, 0, p]
                dSn = ds_ref[p]
                dZ = dY_p * e_e[:, sl]
                dC = dC + _dot(dZ, S0)
                h0 = 2 * p
                el0 = jnp.exp(cs[L - 1:L, h0:h0 + 1])
                el1 = jnp.exp(cs[L - 1:L, h0 + 1:h0 + 2])
                dec = jnp.where(rowp < SSD_P, el0, el1)
                ds_ref[p] = dSn * dec + _dot(dZ, Cg, "tn")
                dXf_p = _dot(Bg, dSn, "nt")
                dB = dB + _dot(Xf[:, sl], dSn)
                rs = jnp.sum(dSn * S0, axis=1, keepdims=True)
                s0 = jnp.sum(jnp.where(rowp < SSD_P, rs, 0.0), axis=0, keepdims=True) * el0
                s1 = jnp.sum(jnp.where(rowp >= SSD_P, rs, 0.0), axis=0, keepdims=True) * el1
                dcsl = dcsl + s0 * (hsel == h0).astype(F32) + s1 * (hsel == h0 + 1).astype(F32)
                dxbc_ref[:, sl] = dXd_p
                y_off = _dot(Cg, S0, "nt") * e_e[:, sl]
                t1 = dY_p * y_off - dXf_p * Xf[:, sl]
                r1 = jnp.where(lane < SSD_P, t1, 0.0)
                c0 = jnp.sum(r1, axis=1, keepdims=True)
                c1 = jnp.sum(t1 - r1, axis=1, keepdims=True)
                dcs = dcs + c0 * (hsel == h0).astype(F32) + c1 * (hsel == h0 + 1).astype(F32)
                t2 = dXf_p * Xf[:, sl]
                r2 = jnp.where(lane < SSD_P, t2, 0.0)
                dcsl = dcsl + jnp.sum(r2, keepdims=True) * (hsel == h0).astype(F32) \
                    + jnp.sum(t2 - r2, keepdims=True) * (hsel == h0 + 1).astype(F32)
                dxbc_ref[:, sl] += dXf_p * f_e[:, sl]
            dxbc_ref[:, SSD_INNER + g * SSD_N:SSD_INNER + (g + 1) * SSD_N] = dB
            dxbc_ref[:, SSD_INNER + (SSD_G + g) * SSD_N:SSD_INNER + (SSD_G + g + 1) * SSD_N] = dC
        dXd = dxbc_ref[:, 0:SSD_INNER]
        dxbc_ref[:, 0:SSD_INNER] = dXd * dt_e + dsk_e * dY
        rowl = lax.broadcasted_iota(jnp.int32, (L, 1), 0)
        dcs = dcs + jnp.where(rowl == L - 1, dcsl, 0.0)
        dalpha = _dot(upper.astype(F32), dcs, hi=True)
        ddt = _dot(dXd * X, ET, hi=True) + dalpha * a
        dalog_ref[...] += jnp.sum(dalpha * dt, axis=0, keepdims=True) * a
        ddtr = ddt * _sigmoid(dtr + dtb)
        spread = (lax.broadcasted_iota(jnp.int32, (SSD_H, 128), 0) == lax.broadcasted_iota(jnp.int32, (SSD_H, 128), 1)).astype(F32)
        ddtr_ref[...] = _dot(ddtr, spread, hi=True).astype(ddtr_ref.dtype)
        ddtb_ref[...] += jnp.sum(ddtr, axis=0, keepdims=True)
        ddsk_ref[...] += jnp.sum(_dot(dY * X, ET, hi=True), axis=0, keepdims=True)

    rowr = lambda b, c: (b * C + (C - 1 - c), 0)
    small = pl.BlockSpec((1, SSD_H), lambda b, c: (0, 0))
    return pl.pallas_call(
        body, grid=(nseq, C), name="ssd_bwd",
        in_specs=[pl.BlockSpec((L, SSD_INNER), rowr),
                  pl.BlockSpec((L, SSD_G * SSD_N), lambda b, c: (b * C + (C - 1 - c), SSD_INNER // (SSD_G * SSD_N))),
                  pl.BlockSpec((L, SSD_G * SSD_N), lambda b, c: (b * C + (C - 1 - c), SSD_INNER // (SSD_G * SSD_N) + 1)),
                  pl.BlockSpec((L, 128), rowr), small, small, small,
                  pl.BlockSpec((1, 1, NP, 2 * SSD_P, SSD_N), lambda b, c: (b, C - 1 - c, 0, 0, 0)),
                  pl.BlockSpec((L, SSD_INNER), rowr)],
        out_specs=[pl.BlockSpec((L, CONV_CH), rowr), pl.BlockSpec((L, 128), rowr), small, small, small],
        out_shape=[jax.ShapeDtypeStruct((T, CONV_CH), F32), jax.ShapeDtypeStruct((T, 128), BF16),
                   jax.ShapeDtypeStruct((1, SSD_H), F32), jax.ShapeDtypeStruct((1, SSD_H), F32),
                   jax.ShapeDtypeStruct((1, SSD_H), F32)],
        scratch_shapes=[pltpu.VMEM((NP, 2 * SSD_P, SSD_N), F32)],
        compiler_params=_cp("arbitrary", "arbitrary"),
    )(xbc, xbc, xbc, dtr, dtb, alog, dsk, prev, dy)


ATT_BLK = 256


def attn_fwd(q, k, v):
    B, H, S, dq = q.shape
    dv = v.shape[-1]
    tq = min(ATT_BLK, S)
    nq = S // tq
    scale = QK ** -0.5

    def body(q_ref, k_ref, v_ref, o_ref, lse_ref):
        qi = pl.program_id(2)
        qb = q_ref[0, 0]
        row = lax.broadcasted_iota(jnp.int32, (tq, tq), 0)
        col = lax.broadcasted_iota(jnp.int32, (tq, tq), 1)

        def step(j, carry):
            m, l, acc = carry
            off = pl.multiple_of(j * tq, tq)
            kb = k_ref[0, 0, pl.ds(off, tq), :]
            vb = v_ref[0, 0, pl.ds(off, tq), :]
            s = _dot(qb, kb, "nt") * scale
            s = jnp.where((j * tq + col) <= (qi * tq + row), s, -1e30)
            m_new = jnp.maximum(m, jnp.max(s, axis=1, keepdims=True))
            p = jnp.exp(s - m_new)
            corr = jnp.exp(m - m_new)
            return m_new, l * corr + jnp.sum(p, axis=1, keepdims=True), acc * corr + _dot(p, vb)

        m, l, acc = lax.fori_loop(0, qi + 1, step, (jnp.full((tq, 1), -1e30, F32), jnp.zeros((tq, 1), F32),
                                                    jnp.zeros((tq, dv), F32)))
        o_ref[0, 0] = (acc / l).astype(o_ref.dtype)
        lse_ref[0, 0] = m + jnp.log(l)

    full = lambda d: pl.BlockSpec((1, 1, S, d), lambda b, h, i: (b, h, 0, 0))
    blk = lambda d: pl.BlockSpec((1, 1, tq, d), lambda b, h, i: (b, h, i, 0))
    return pl.pallas_call(
        body, grid=(B, H, nq), name="attn_fwd",
        in_specs=[blk(dq), full(dq), full(dv)],
        out_specs=[blk(dv), blk(1)],
        out_shape=[jax.ShapeDtypeStruct((B, H, S, dv), BF16), jax.ShapeDtypeStruct((B, H, S, 1), F32)],
        compiler_params=_cp("parallel", "parallel", "arbitrary"),
    )(q, k, v)


def attn_bwd(q, k, v, o, lse, do):
    B, H, S, dq_ = q.shape
    dv_ = v.shape[-1]
    tk = min(ATT_BLK, S)
    nk = S // tk
    scale = QK ** -0.5

    def body(q_ref, k_ref, v_ref, o_ref, lse_ref, do_ref, dq_ref, dk_ref, dv_ref):
        kj = pl.program_id(2)

        @pl.when(kj == 0)
        def _():
            dq_ref[...] = jnp.zeros_like(dq_ref)

        kb = k_ref[0, 0]
        vb = v_ref[0, 0]
        row = lax.broadcasted_iota(jnp.int32, (tk, tk), 0)
        col = lax.broadcasted_iota(jnp.int32, (tk, tk), 1)

        def step(i, carry):
            dk, dv = carry
            off = pl.multiple_of(i * tk, tk)
            qb = q_ref[0, 0, pl.ds(off, tk), :]
            dob = do_ref[0, 0, pl.ds(off, tk), :]
            ob = o_ref[0, 0, pl.ds(off, tk), :]
            ls = lse_ref[0, 0, pl.ds(off, tk), :]
            s = _dot(qb, kb, "nt") * scale
            p = jnp.where((kj * tk + col) <= (i * tk + row), jnp.exp(s - ls), 0.0)
            dv = dv + _dot(p, dob, "tn")
            dp = _dot(dob, vb, "nt")
            delta = jnp.sum(dob.astype(F32) * ob.astype(F32), axis=1, keepdims=True)
            ds = p * (dp - delta) * scale
            dk = dk + _dot(ds, qb, "tn")
            dq_ref[0, 0, pl.ds(off, tk), :] += _dot(ds, kb)
            return dk, dv

        dk, dv = lax.fori_loop(kj, nk, step, (jnp.zeros((tk, dq_), F32), jnp.zeros((tk, dv_), F32)))
        dk_ref[0, 0] = dk
        dv_ref[0, 0] = dv.astype(dv_ref.dtype)

    full = lambda d: pl.BlockSpec((1, 1, S, d), lambda b, h, j: (b, h, 0, 0))
    blk = lambda d: pl.BlockSpec((1, 1, tk, d), lambda b, h, j: (b, h, j, 0))
    return pl.pallas_call(
        body, grid=(B, H, nk), name="attn_bwd",
        in_specs=[full(dq_), blk(dq_), blk(dv_), full(dv_), full(1), full(dv_)],
        out_specs=[full(dq_), blk(dq_), blk(dv_)],
        out_shape=[jax.ShapeDtypeStruct((B, H, S, dq_), F32), jax.ShapeDtypeStruct((B, H, S, dq_), F32),
                   jax.ShapeDtypeStruct((B, H, S, dv_), BF16)],
        compiler_params=_cp("parallel", "parallel", "arbitrary"),
    )(q, k, v, o, lse, do)


SLOT = 128
ATT_T = 512


def _col_to_row(col):
    n = col.shape[0]
    eye = lax.broadcasted_iota(jnp.int32, (n, n), 0) == lax.broadcasted_iota(jnp.int32, (n, n), 1)
    return jnp.sum(jnp.where(eye, col, 0.0), axis=0, keepdims=True)


def attn_slot_fwd(q, k, v, nseq, comm=None):
    T = q.shape[0]
    S = T // nseq
    t = min(ATT_T, S)
    nb = S // t
    scale = QK ** -0.5

    def body(q_ref, k_ref, v_ref, o_ref, lse_ref):
        causal = lax.broadcasted_iota(jnp.int32, (t, t), 1) <= lax.broadcasted_iota(jnp.int32, (t, t), 0)
        for qi in range(nb):
            qb = q_ref[qi * t:(qi + 1) * t, :]
            m = l = acc = None
            for kj in range(qi + 1):
                s = _dot(qb, k_ref[kj * t:(kj + 1) * t, :], "nt") * scale
                if kj == qi:
                    s = jnp.where(causal, s, -1e30)
                bm = jnp.max(s, axis=1, keepdims=True)
                if kj == 0:
                    m = bm
                    p = jnp.exp(s - m)
                    l = jnp.sum(p, axis=1, keepdims=True)
                    acc = _dot(p, v_ref[0:t, :])
                else:
                    m_new = jnp.maximum(m, bm)
                    corr = jnp.exp(m - m_new)
                    p = jnp.exp(s - m_new)
                    l = l * corr + jnp.sum(p, axis=1, keepdims=True)
                    acc = acc * corr + _dot(p, v_ref[kj * t:(kj + 1) * t, :])
                    m = m_new
            o_ref[qi * t:(qi + 1) * t, :] = (acc / l).astype(o_ref.dtype)
            lse_ref[0, 0, :, qi * t:(qi + 1) * t] = _col_to_row(m + jnp.log(l))

    blk = pl.BlockSpec((S, SLOT), lambda b, h: (b, h))
    return _call_with_comm(
        body, (nseq, MLA_H), "attn_fwd", [blk, blk, blk], [q, k, v],
        [blk, pl.BlockSpec((1, 1, 1, S), lambda b, h: (b, h, 0, 0))],
        [jax.ShapeDtypeStruct((T, MLA_H * SLOT), BF16), jax.ShapeDtypeStruct((nseq, MLA_H, 1, S), F32)], comm)


def attn_slot_bwd(q, k, v, o, lse, do, nseq, comm=None):
    T = q.shape[0]
    S = T // nseq
    t = min(ATT_T, S)
    nb = S // t
    scale = QK ** -0.5

    def body(q_ref, k_ref, v_ref, o_ref, lse_ref, do_ref, dq_ref, dk_ref, dv_ref):
        causal_t = lax.broadcasted_iota(jnp.int32, (t, t), 0) <= lax.broadcasted_iota(jnp.int32, (t, t), 1)
        ones = jnp.ones((8, SLOT), F32)
        delta = []
        for qi in range(nb):
            sl = slice(qi * t, (qi + 1) * t)
            prod = do_ref[sl, :].astype(F32) * o_ref[sl, :].astype(F32)
            delta.append(_dot(ones, prod, "nt", hi=True)[0:1, :])
        for kj in range(nb):
            ks = slice(kj * t, (kj + 1) * t)
            kb = k_ref[ks, :]
            vb = v_ref[ks, :]
            dk = dv = None
            for qi in range(kj, nb):
                sl = slice(qi * t, (qi + 1) * t)
                qb = q_ref[sl, :]
                dob = do_ref[sl, :]
                st = _dot(kb, qb, "nt") * scale
                pt = jnp.exp(st - lse_ref[0, 0, :, sl])
                if qi == kj:
                    pt = jnp.where(causal_t, pt, 0.0)
                dpt = _dot(vb, dob, "nt")
                dst = (pt * (dpt - delta[qi]) * scale).astype(BF16)
                dvc = _dot(pt, dob)
                dkc = _dot(dst, qb)
                dv = dvc if dv is None else dv + dvc
                dk = dkc if dk is None else dk + dkc
                dqc = _dot(dst, kb, "tn")
                if kj == 0:
                    dq_ref[sl, :] = dqc
                else:
                    dq_ref[sl, :] += dqc
            dk_ref[ks, :] = dk
            dv_ref[ks, :] = dv.astype(dv_ref.dtype)

    blk = pl.BlockSpec((S, SLOT), lambda b, h: (b, h))
    lse_spec = pl.BlockSpec((1, 1, 1, S), lambda b, h: (b, h, 0, 0))
    W = MLA_H * SLOT
    return _call_with_comm(
        body, (nseq, MLA_H), "attn_bwd", [blk, blk, blk, blk, lse_spec, blk], [q, k, v, o, lse, do], [blk, blk, blk],
        [jax.ShapeDtypeStruct((T, W), F32), jax.ShapeDtypeStruct((T, W), F32), jax.ShapeDtypeStruct((T, W), BF16)], comm)


def _rope_coeffs(cos, sin):
    half = ROPE // 2
    r = lax.broadcasted_iota(jnp.int32, (half, SLOT), 0)
    c = lax.broadcasted_iota(jnp.int32, (half, SLOT), 1)
    pc = ((c == r + NOPE) | (c == r + NOPE + half)).astype(F32)
    ps = (c == r + NOPE + half).astype(F32) - (c == r + NOPE).astype(F32)
    lane = lax.broadcasted_iota(jnp.int32, (1, SLOT), 1)
    return _dot(cos, pc, hi=True) + (lane < NOPE).astype(F32), _dot(sin, ps, hi=True)


def _rope_swap(x):
    W = x.shape[1]
    half = ROPE // 2
    lane = lax.broadcasted_iota(jnp.int32, (1, W), 1) & (SLOT - 1)
    up = pltpu.roll(x, W - half, axis=1)
    dn = pltpu.roll(x, half, axis=1)
    return jnp.where((lane >= NOPE) & (lane < NOPE + half), up, jnp.where((lane >= NOPE + half) & (lane < QK), dn, 0.0))


def rope_slot_fwd(q, kn, dtkr, cos, sin, name):
    def fn(qv, knv, krv, cv, sv):
        C, Sg = _rope_coeffs(cv, sv)
        ct, stl = jnp.tile(C, (1, MLA_H)), jnp.tile(Sg, (1, MLA_H))
        qo = qv * ct + _rope_swap(qv) * stl
        r = lax.broadcasted_iota(jnp.int32, (SLOT, SLOT), 0)
        c = lax.broadcasted_iota(jnp.int32, (SLOT, SLOT), 1)
        place = ((c == r + NOPE) & (r < ROPE)).astype(F32)
        kr = _dot(krv, place, hi=True)
        kr = kr * C + _rope_swap(kr) * Sg
        return qo, knv.astype(F32) + jnp.tile(kr, (1, MLA_H))
    W = MLA_H * SLOT
    return rowwise(fn, [q, kn, (dtkr, SLOT, 1), cos, sin], [], [(W, BF16), (W, BF16)], [], name)


def rope_slot_bwd(dq, dk, cos, sin, name):
    def fn(dqv, dkv, cv, sv):
        C, Sg = _rope_coeffs(cv, sv)
        ct, stl = jnp.tile(C, (1, MLA_H)), jnp.tile(Sg, (1, MLA_H))
        dqo = dqv * ct - _rope_swap(dqv) * stl
        tot = dkv[:, 0:SLOT]
        for h in range(1, MLA_H):
            tot = tot + dkv[:, h * SLOT:(h + 1) * SLOT]
        u = tot * C - _rope_swap(tot) * Sg
        r = lax.broadcasted_iota(jnp.int32, (SLOT, SLOT), 0)
        c = lax.broadcasted_iota(jnp.int32, (SLOT, SLOT), 1)
        unplace = ((r == c + NOPE) & (c < ROPE)).astype(F32)
        return dqo, dkv, _dot(u, unplace, hi=True)
    W = MLA_H * SLOT
    return rowwise(fn, [dq, dk, cos, sin], [], [(W, BF16), (W, BF16), (SLOT, BF16)], [], name)


XA_BLK = 512


def xattn_fwd(q, k, v, nseq):
    T = q.shape[0]
    S = T // nseq
    M = k.shape[0] // nseq
    tq = min(XA_BLK, S)
    nq = S // tq
    scale = XA_D ** -0.5

    def body(q_ref, k_ref, v_ref, o_ref):
        s = _dot(q_ref[...], k_ref[...], "nt") * scale
        p = jnp.exp(s - jnp.max(s, axis=1, keepdims=True))
        p = p / jnp.sum(p, axis=1, keepdims=True)
        o_ref[...] = _dot(p, v_ref[...]).astype(o_ref.dtype)

    qs = pl.BlockSpec((tq, XA_D), lambda b, h, i: (b * nq + i, h))
    ks = pl.BlockSpec((M, XA_D), lambda b, h, i: (b, h))
    return pl.pallas_call(
        body, grid=(nseq, XA_H, nq), name="xattn_fwd", in_specs=[qs, ks, ks], out_specs=qs,
        out_shape=jax.ShapeDtypeStruct((T, XA_H * XA_D), BF16),
        compiler_params=_cp("parallel", "parallel", "parallel"),
    )(q, k, v)


def xattn_bwd(q, k, v, do, nseq):
    T = q.shape[0]
    S = T // nseq
    M = k.shape[0] // nseq
    tq = min(XA_BLK, S)
    nq = S // tq
    scale = XA_D ** -0.5

    def body(q_ref, k_ref, v_ref, do_ref, dq_ref, dk_ref, dv_ref):
        @pl.when(pl.program_id(2) == 0)
        def _():
            dk_ref[...] = jnp.zeros_like(dk_ref)
            dv_ref[...] = jnp.zeros_like(dv_ref)

        qb, kb, vb, dob = q_ref[...], k_ref[...], v_ref[...], do_ref[...]
        s = _dot(qb, kb, "nt") * scale
        p = jnp.exp(s - jnp.max(s, axis=1, keepdims=True))
        p = p / jnp.sum(p, axis=1, keepdims=True)
        dp = _dot(dob, vb, "nt")
        ds = p * (dp - jnp.sum(dp * p, axis=1, keepdims=True)) * scale
        dq_ref[...] = _dot(ds, kb).astype(dq_ref.dtype)
        dk_ref[...] += _dot(ds, qb, "tn")
        dv_ref[...] += _dot(p, dob, "tn")

    qs = pl.BlockSpec((tq, XA_D), lambda b, h, i: (b * nq + i, h))
    ks = pl.BlockSpec((M, XA_D), lambda b, h, i: (b, h))
    return pl.pallas_call(
        body, grid=(nseq, XA_H, nq), name="xattn_bwd", in_specs=[qs, ks, ks, qs], out_specs=[qs, ks, ks],
        out_shape=[jax.ShapeDtypeStruct((T, XA_H * XA_D), BF16), jax.ShapeDtypeStruct(k.shape, F32),
                   jax.ShapeDtypeStruct(k.shape, F32)],
        compiler_params=_cp("parallel", "parallel", "arbitrary"),
    )(q, k, v, do)


CONV_BLK = 256


def _shift_down(x, s, rows):
    if s == 0:
        return x
    return jnp.where(rows >= s, pltpu.roll(x, s, axis=0), 0.0)


def _shift_up(x, s, rows):
    if s == 0:
        return x
    S = x.shape[0]
    return jnp.where(rows < S - s, pltpu.roll(x, S - s, axis=0), 0.0)


def conv_fwd(x, w, b, nseq):
    T, CH = x.shape
    S = T // nseq

    def body(x_ref, w_ref, b_ref, o_ref):
        xv = x_ref[...]
        rows = lax.broadcasted_iota(jnp.int32, (S, 1), 0)
        c = jnp.zeros_like(xv) + b_ref[...]
        for kk in range(CONV_K):
            c = c + w_ref[kk:kk + 1, :] * _shift_down(xv, CONV_K - 1 - kk, rows)
        o_ref[...] = c * _sigmoid(c)

    xs = pl.BlockSpec((S, CONV_BLK), lambda j, bb: (bb, j))
    return pl.pallas_call(
        body, grid=(CH // CONV_BLK, nseq), name="conv_fwd",
        in_specs=[xs, pl.BlockSpec((CONV_K, CONV_BLK), lambda j, bb: (0, j)), pl.BlockSpec((1, CONV_BLK), lambda j, bb: (0, j))],
        out_specs=xs, out_shape=jax.ShapeDtypeStruct((T, CH), F32),
        compiler_params=_cp("parallel", "parallel"),
    )(x, w, b)


def conv_bwd(x, w, b, dout, nseq):
    T, CH = x.shape
    S = T // nseq

    def body(x_ref, w_ref, b_ref, do_ref, dx_ref, dw_ref, db_ref):
        @pl.when(pl.program_id(1) == 0)
        def _():
            dw_ref[...] = jnp.zeros_like(dw_ref)
            db_ref[...] = jnp.zeros_like(db_ref)

        xv = x_ref[...]
        rows = lax.broadcasted_iota(jnp.int32, (S, 1), 0)
        c = jnp.zeros_like(xv) + b_ref[...]
        sh = [_shift_down(xv, CONV_K - 1 - kk, rows) for kk in range(CONV_K)]
        for kk in range(CONV_K):
            c = c + w_ref[kk:kk + 1, :] * sh[kk]
        sg = _sigmoid(c)
        dc = do_ref[...] * sg * (1.0 + c * (1.0 - sg))
        dx = jnp.zeros_like(xv)
        for kk in range(CONV_K):
            dx = dx + w_ref[kk:kk + 1, :] * _shift_up(dc, CONV_K - 1 - kk, rows)
            dw_ref[kk:kk + 1, :] += jnp.sum(dc * sh[kk], axis=0, keepdims=True)
        dx_ref[...] = dx.astype(dx_ref.dtype)
        db_ref[...] += jnp.sum(dc, axis=0, keepdims=True)

    xs = pl.BlockSpec((S, CONV_BLK), lambda j, bb: (bb, j))
    ws = pl.BlockSpec((CONV_K, CONV_BLK), lambda j, bb: (0, j))
    bs = pl.BlockSpec((1, CONV_BLK), lambda j, bb: (0, j))
    return pl.pallas_call(
        body, grid=(CH // CONV_BLK, nseq), name="conv_bwd",
        in_specs=[xs, ws, bs, xs], out_specs=[xs, ws, bs],
        out_shape=[jax.ShapeDtypeStruct((T, CH), BF16), jax.ShapeDtypeStruct((CONV_K, CH), F32),
                   jax.ShapeDtypeStruct((1, CH), F32)],
        compiler_params=_cp("parallel", "arbitrary"),
    )(x, w, b, dout)


def _dims(a, b, mode):
    M = a.shape[1] if mode[0] == "t" else a.shape[0]
    K = a.shape[0] if mode[0] == "t" else a.shape[1]
    N = b.shape[0] if mode[1] == "t" else b.shape[1]
    return M, K, N


def _tile(dim, prefs):
    for p in prefs:
        if dim % p == 0:
            return p
    return dim


def mm(groups, out_dtypes, name, tm=None, tn=None, tk=None, epi=None, extras=(), comm=None):
    a0, b0, m0 = groups[0][0]
    M, K0, N = _dims(a0, b0, m0)
    tm = tm or _tile(M, (1024, 512, 256, 128))
    tn = tn or _tile(N, (512, 256, 128))
    flat = [p for g in groups for p in g]
    nk = 1 if tk is None else K0 // tk
    in_specs, args = [], []
    for a, b, mode in flat:
        _, K, _ = _dims(a, b, mode)
        kb = K if tk is None else tk
        in_specs.append(pl.BlockSpec((kb, tm), lambda i, j, k: (k, i)) if mode[0] == "t"
                        else pl.BlockSpec((tm, kb), lambda i, j, k: (i, k)))
        in_specs.append(pl.BlockSpec((tn, kb), lambda i, j, k: (j, k)) if mode[1] == "t"
                        else pl.BlockSpec((kb, tn), lambda i, j, k: (k, j)))
        args += [a, b]
    for e in extras:
        in_specs.append(pl.BlockSpec((tm, tn), lambda i, j, k: (i, j)))
        args.append(e)
    n_in = len(args)
    n_out = len(out_dtypes)
    ng = len(groups)
    sizes = [len(g) for g in groups]

    def body(*refs):
        ins, outs, accs = refs[:n_in], refs[n_in:n_in + n_out], refs[n_in + n_out:]
        kk = pl.program_id(2)
        vals, pos = [], 0
        for gi in range(ng):
            acc = None
            for _ in range(sizes[gi]):
                mode = flat[pos // 2][2]
                d = _dot(ins[pos][...], ins[pos + 1][...], mode)
                acc = d if acc is None else acc + d
                pos += 2
            vals.append(acc)
        ex = [r[...] for r in ins[2 * len(flat):]]

        def finish(accv):
            res = epi(accv, ex) if epi is not None else tuple(accv)
            for o, r in zip(outs, res):
                o[...] = r.astype(o.dtype)

        if nk == 1:
            finish(vals)
        else:
            @pl.when(kk == 0)
            def _():
                for ar, vv in zip(accs, vals):
                    ar[...] = vv

            @pl.when(kk > 0)
            def _():
                for ar, vv in zip(accs, vals):
                    ar[...] += vv

            @pl.when(kk == nk - 1)
            def _():
                finish([ar[...] for ar in accs])

    grid = (M // tm, N // tn, nk)
    out_specs = [pl.BlockSpec((tm, tn), lambda i, j, k: (i, j)) for _ in out_dtypes]
    out_shape = [jax.ShapeDtypeStruct((M, N), dt) for dt in out_dtypes]
    scratch = [pltpu.VMEM((tm, tn), F32) for _ in range(ng if nk > 1 else 0)]
    sem = ("parallel", "parallel", "arbitrary")
    if comm is not None:
        body = _attach(comm, body, n_in, n_out, *_grid_ends(grid))
        in_specs, args = in_specs + [HBM_SPEC] * len(comm.inputs), args + comm.inputs
        out_specs, out_shape = out_specs + [HBM_SPEC] * len(comm.out_shapes), out_shape + comm.out_shapes
        scratch, sem = scratch + comm.sems, ("arbitrary",) * 3
    return pl.pallas_call(body, grid=grid, name=name, in_specs=in_specs, out_specs=out_specs, out_shape=out_shape,
                          scratch_shapes=scratch, compiler_params=_cp(*sem))(*args)


def mm1(a, b, mode, out_dtype, name, **kw):
    return mm([[(a, b, mode)]], [out_dtype], name, **kw)[0]


ROW_BLK = 256


def rowwise(fn, rows, consts, outs, accs, name, tb=ROW_BLK):
    rows = [r if isinstance(r, tuple) else (r, r.shape[1], 0) for r in rows]
    T = rows[0][0].shape[0]
    tb = min(tb, T)
    n_r, n_c, n_o, n_a = len(rows), len(consts), len(outs), len(accs)

    def body(*refs):
        vals = [r[...] for r in refs[:n_r + n_c]]
        res = fn(*vals)
        o_refs = refs[n_r + n_c:n_r + n_c + n_o]
        a_refs = refs[n_r + n_c + n_o:]
        for o, r in zip(o_refs, res[:n_o]):
            o[...] = r.astype(o.dtype)
        if n_a:
            @pl.when(pl.program_id(0) == 0)
            def _():
                for ar in a_refs:
                    ar[...] = jnp.zeros_like(ar)
            for ar, r in zip(a_refs, res[n_o:]):
                ar[...] += r

    return pl.pallas_call(
        body, grid=(T // tb,), name=name,
        in_specs=[pl.BlockSpec((tb, w), functools.partial(lambda i, j: (i, j), j=j)) for _, w, j in rows]
        + [pl.BlockSpec(c.shape, lambda i: (0, 0)) for c in consts],
        out_specs=[pl.BlockSpec((tb, d), lambda i: (i, 0)) for d, _ in outs]
        + [pl.BlockSpec(s, lambda i: (0, 0)) for s in accs],
        out_shape=[jax.ShapeDtypeStruct((T, d), dt) for d, dt in outs]
        + [jax.ShapeDtypeStruct(s, F32) for s in accs],
        compiler_params=_cp("arbitrary" if n_a else "parallel"),
    )(*[r[0] for r in rows], *consts)


def _rms_stats(x):
    r = lax.rsqrt(jnp.mean(x * x, axis=-1, keepdims=True) + EPS)
    return r, x * r


def _rms_bwd(x, g, dy):
    r, xn = _rms_stats(x)
    dyg = dy * g
    dx = r * (dyg - xn * jnp.mean(dyg * xn, axis=-1, keepdims=True))
    return dx, jnp.sum(dy * xn, axis=0, keepdims=True)


def rms_fwd(x, g, name):
    return rowwise(lambda xv, gv: (_rms_stats(xv)[1] * gv,), [x], [g], [(x.shape[1], BF16)], [], name)[0]


def rms_bwd(x, g, dy, name, resid=None, dx_dtype=F32):
    def fn(*v):
        if resid is None:
            xv, dyv, gv = v
            dx, dg = _rms_bwd(xv, gv, dyv)
        else:
            xv, dyv, rv, gv = v
            dx, dg = _rms_bwd(xv, gv, dyv)
            dx = dx + rv
        return dx, dg
    rows = [x, dy] + ([] if resid is None else [resid])
    return rowwise(fn, rows, [g], [(x.shape[1], dx_dtype)], [(1, x.shape[1])], name)


def resid_fwd(x, h, g, wgt, name):
    return rowwise(lambda xv, hv, gv: (xv + wgt * _rms_stats(hv)[1] * gv,), [x, h], [g], [(x.shape[1], F32)], [], name)[0]


def resid_bwd(h, g, dy, wgt, name):
    def fn(hv, dyv, gv):
        dx, dg = _rms_bwd(hv, gv, dyv)
        return wgt * dx, wgt * dg
    return rowwise(fn, [h, dy], [g], [(h.shape[1], BF16)], [(1, h.shape[1])], name)


def _silu_parts(g):
    s = _sigmoid(g)
    return g * s, s * (1.0 + g * (1.0 - s))


def gated_norm_fwd(y, z, g, name):
    W = SSD_INNER // SSD_G

    def fn(yv, zv, gv):
        yg = yv * _silu_parts(zv)[0]
        return (jnp.concatenate([_rms_stats(yg[:, i * W:(i + 1) * W])[1] for i in range(SSD_G)], axis=1) * gv,)
    return rowwise(fn, [y, z], [g], [(SSD_INNER, BF16)], [], name)[0]


def gated_norm_bwd(y, z, dyn, g, name):
    W = SSD_INNER // SSD_G

    def fn(yv, zv, dv, gv):
        sil, dsil = _silu_parts(zv)
        yg = yv * sil
        parts = [_rms_bwd(yg[:, i * W:(i + 1) * W], gv[:, i * W:(i + 1) * W], dv[:, i * W:(i + 1) * W]) for i in range(SSD_G)]
        dyg = jnp.concatenate([p[0] for p in parts], axis=1)
        dg = jnp.concatenate([p[1] for p in parts], axis=1)
        return dyg * sil, dyg * yv * dsil, dg
    return rowwise(fn, [y, z, dyn], [g], [(SSD_INNER, F32), (SSD_INNER, BF16)], [(1, SSD_INNER)], name)


def merge_fwd(gl, ys, ym, gb, name):
    def fn(glv, ysv, ymv, gbv):
        gt = _sigmoid(glv + gbv)
        return (gt[:, :D] * ysv + gt[:, D:] * ymv,)
    return rowwise(fn, [gl, ys, ym], [gb], [(D, BF16)], [], name)[0]


def merge_bwd(gl, ys, ym, dm, gb, name):
    def fn(glv, ysv, ymv, dmv, gbv):
        gt = _sigmoid(glv + gbv)
        gs, gm = gt[:, :D], gt[:, D:]
        dgl = jnp.concatenate([dmv * ysv * gs * (1.0 - gs), dmv * ymv * gm * (1.0 - gm)], axis=1)
        return dmv * gs, dmv * gm, dgl, jnp.sum(dgl, axis=0, keepdims=True)
    return rowwise(fn, [gl, ys, ym, dm], [gb], [(D, BF16), (D, BF16), (2 * D, BF16)], [(1, 2 * D)], name)


def rope_rot(x1, x2, cos, sin, name):
    fn = lambda a, b, c, s: (a * c - b * s, a * s + b * c)
    return rowwise(fn, [x1, x2, cos, sin], [], [(x1.shape[1], BF16), (x1.shape[1], BF16)], [], name)


def loss_head(y, tgt, name):
    def fn(yv, tv):
        d = yv - tv
        part = 0.5 * jnp.sum(jnp.sum(d * d, axis=1, keepdims=True), axis=0, keepdims=True) / D
        return d / D, jnp.broadcast_to(part, (1, 128))
    return rowwise(fn, [y, tgt], [], [(D, F32)], [(1, 128)], name)


def adamw(w, g, m, v, name):
    R, C = w.shape
    tb = _tile(R, (256, 128, 64, 32, 16, 8))

    def fn(wv, gv, mv, vv):
        mn = B1 * mv + (1.0 - B1) * gv
        vn = B2 * vv + (1.0 - B2) * (gv * gv)
        mh = mn / (1.0 - B1 ** STEP)
        vh = vn / (1.0 - B2 ** STEP)
        return -LR * (mh / (jnp.sqrt(vh) + AEPS) + WD * wv), mn, vn
    return rowwise(fn, [w, g, m, v], [], [(C, F32)] * 3, [], name, tb=tb)


def _me():
    return lax.axis_index("x"), lax.axis_index("y"), lax.axis_index("c")


def _dev_index():
    x, y, c = _me()
    return 4 * x + 2 * y + c


HBM_SPEC = pl.BlockSpec(memory_space=pl.ANY)


class GatherComm:
    def __init__(self, shards):
        self.inputs = list(shards)
        n = len(shards)
        self.out_shapes = [jax.ShapeDtypeStruct((N_DEV,) + s.shape, s.dtype) for s in shards]
        self.sems = [pltpu.SemaphoreType.DMA((7 * n,)), pltpu.SemaphoreType.DMA((7 * n,)), pltpu.SemaphoreType.DMA((n,))]

    def _plan(self, x_refs, out_refs, sems):
        send_sems, recv_sems, local_sems = sems
        n = len(x_refs)
        x, y, c = _me()
        me, sibling = (x, y, c), (x, y, 1 - c)
        chips = [(1 - x, y), (x, 1 - y), (1 - x, 1 - y)]

        def slot(i, px, py, pc):
            return out_refs[i].at[4 * px + 2 * py + pc]

        def copy(i, k, block, to, src=None):
            return pltpu.make_async_remote_copy(
                src_ref=slot(i, *block) if src is None else src, dst_ref=slot(i, *block),
                send_sem=send_sems.at[7 * i + k], recv_sem=recv_sems.at[7 * i + k], device_id=to, device_id_type=MESH)

        mine = [pltpu.make_async_copy(x_refs[i], slot(i, *me), local_sems.at[i]) for i in range(n)]
        first = []
        for i in range(n):
            first.append(copy(i, 0, me, sibling, src=x_refs[i]))
            first += [copy(i, 1 + j, me, (*chip, c), src=x_refs[i]) for j, chip in enumerate(chips)]
        passed = [[copy(i, 4 + j, (*chip, c), sibling) for j, chip in enumerate(chips)] for i in range(n)]
        from_ici = [[copy(i, 1 + j, (*chip, c), me) for j, chip in enumerate(chips)] for i in range(n)]
        from_sib = [[copy(i, 0, sibling, me)] + [copy(i, 4 + j, (*chip, 1 - c), me) for j, chip in enumerate(chips)] for i in range(n)]
        return mine, first, passed, from_ici, from_sib

    def start(self, x_refs, out_refs, sems):
        mine, first, _, _, _ = self._plan(x_refs, out_refs, sems)
        for cp in mine + first:
            cp.start()

    def finish(self, x_refs, out_refs, sems):
        mine, first, passed, from_ici, from_sib = self._plan(x_refs, out_refs, sems)
        for i in range(len(x_refs)):
            for arrival, forward in zip(from_ici[i], passed[i]):
                arrival.wait_recv()
                forward.start()
        for row in from_sib:
            for arrival in row:
                arrival.wait_recv()
        for cp in first + [cp for row in passed for cp in row]:
            cp.wait_send()
        for cp in mine:
            cp.wait()


def run_comm(comm, name):
    n_in, n_out = len(comm.inputs), len(comm.out_shapes)

    def body(*refs):
        ins, outs, sems = refs[:n_in], refs[n_in:n_in + n_out], refs[n_in + n_out:]
        comm.start(ins, outs, sems)
        comm.finish(ins, outs, sems)

    return pl.pallas_call(body, name=name, out_shape=comm.out_shapes, in_specs=[HBM_SPEC] * n_in,
                          out_specs=[HBM_SPEC] * n_out, scratch_shapes=comm.sems)(*comm.inputs)


def _attach(comm, body, n_in, n_out, first, last):
    if comm is None:
        return body
    ci, co, cs = len(comm.inputs), len(comm.out_shapes), len(comm.sems)

    def wrapped(*refs):
        h_in, c_in = refs[:n_in], refs[n_in:n_in + ci]
        h_out, c_out = refs[n_in + ci:n_in + ci + n_out], refs[n_in + ci + n_out:n_in + ci + n_out + co]
        rest = refs[n_in + ci + n_out + co:]
        h_scr, c_sem = rest[:len(rest) - cs], rest[len(rest) - cs:]

        @pl.when(first())
        def _():
            comm.start(c_in, c_out, c_sem)

        body(*h_in, *h_out, *h_scr)

        @pl.when(last())
        def _():
            comm.finish(c_in, c_out, c_sem)

    return wrapped


def _grid_ends(grid):
    first = lambda: functools.reduce(lambda a, b: a & b, [pl.program_id(i) == 0 for i in range(len(grid))])
    last = lambda: functools.reduce(lambda a, b: a & b, [pl.program_id(i) == g - 1 for i, g in enumerate(grid)])
    return first, last


def _call_with_comm(body, grid, name, in_specs, args, out_specs, out_shape, comm):
    sem = ("parallel",) * len(grid)
    scratch = []
    if comm is not None:
        body = _attach(comm, body, len(args), len(out_shape), *_grid_ends(grid))
        in_specs, args = in_specs + [HBM_SPEC] * len(comm.inputs), args + comm.inputs
        out_specs, out_shape = out_specs + [HBM_SPEC] * len(comm.out_shapes), out_shape + comm.out_shapes
        scratch, sem = comm.sems, ("arbitrary",) * len(grid)
    return pl.pallas_call(body, grid=grid, name=name, in_specs=in_specs, out_specs=out_specs, out_shape=out_shape,
                          scratch_shapes=scratch, compiler_params=_cp(*sem))(*args)


class ScatterComm:
    def __init__(self, groups):
        self.sizes = [len(g) for g in groups]
        self.rows = [[pc.shape[1] for pc in g] for g in groups]
        ng = len(groups)
        self.inputs = [pc for g in groups for pc in g]
        self.out_shapes = [jax.ShapeDtypeStruct((N_DEV, sum(self.rows[gi]), g[0].shape[2]), g[0].dtype) for gi, g in enumerate(groups)]
        self.sems = [pltpu.SemaphoreType.DMA((7 * ng,)), pltpu.SemaphoreType.DMA((7 * ng,)), pltpu.SemaphoreType.DMA((ng,))]

    def _peers(self):
        x, y, c = _me()
        out = []
        for k in range(1, N_DEV):
            px = 1 - x if k & 4 else x
            py = 1 - y if k & 2 else y
            pc = 1 - c if k & 1 else c
            out.append((k, 4 * px + 2 * py + pc, dict(device_id=(px, py, pc), device_id_type=MESH)))
        return 4 * x + 2 * y + c, out

    def start(self, ins, outs, sems):
        send_sems, recv_sems, local_sems = sems
        me, peers = self._peers()
        pos = 0
        for gi, size in enumerate(self.sizes):
            for i, pc in enumerate(ins[pos:pos + size]):
                dst = outs[gi].at[me, pl.ds(sum(self.rows[gi][:i]), self.rows[gi][i])]
                pltpu.make_async_copy(pc.at[me], dst, local_sems.at[gi]).start()
                for k, peer, kw in peers:
                    pltpu.make_async_remote_copy(src_ref=pc.at[peer], dst_ref=dst, send_sem=send_sems.at[7 * gi + k - 1],
                                                 recv_sem=recv_sems.at[7 * gi + k - 1], **kw).start()
            pos += size

    def finish(self, ins, outs, sems):
        send_sems, recv_sems, local_sems = sems
        me, peers = self._peers()
        whole = [pltpu.make_async_remote_copy(src_ref=outs[gi].at[peer], dst_ref=outs[gi].at[peer],
                                              send_sem=send_sems.at[7 * gi + k - 1], recv_sem=recv_sems.at[7 * gi + k - 1], **kw)
                 for gi in range(len(self.sizes)) for k, peer, kw in peers]
        for cp in whole:
            cp.wait_recv()
        for cp in whole:
            cp.wait_send()
        for gi in range(len(self.sizes)):
            pltpu.make_async_copy(outs[gi].at[me], outs[gi].at[me], local_sems.at[gi]).wait()


def sum_slots(recv, name, tr):
    n, R, C = recv.shape

    def body(r_ref, o_ref):
        acc = r_ref[0].astype(F32)
        for s in range(1, n):
            acc = acc + r_ref[s].astype(F32)
        o_ref[...] = acc

    return pl.pallas_call(
        body, grid=(R // tr,), name=name,
        in_specs=[pl.BlockSpec((n, tr, C), lambda i: (0, i, 0))], out_specs=pl.BlockSpec((tr, C), lambda i: (i, 0)),
        out_shape=jax.ShapeDtypeStruct((R, C), F32), compiler_params=_cp("parallel"),
    )(recv)


PACK_W, FLAT_W = 1024, 128
MAIN = [
    ("ffn1_w_gate", "col"), ("ffn1_w_up", "col"), ("ffn1_w_down", "row"),
    ("ffn2_w_gate", "col"), ("ffn2_w_up", "col"), ("ffn2_w_down", "row"),
    ("w_ssd_proj", "row"), ("w_mla_proj", "row"), ("w_out", "row"),
    ("w_xq", "row"), ("w_xk", "row"), ("w_xv", "row"), ("w_xo", "row"),
    ("w_uk", "col"), ("w_uv", "col"),
]
FLAT = [("w_in", "col"), ("w_uq", "col")]
BIG = MAIN + FLAT
SMALL = ["ffn1_pre_g", "ffn1_post_g", "mix_pre_g", "conv_b", "dt_bias", "a_log", "d_skip", "ssd_norm_g", "q_norm_g",
         "kv_norm_g", "gate_bias", "mix_post_g", "xa_pre_g", "mem_norm_g", "xa_post_g", "ffn2_pre_g", "ffn2_post_g"]
WEIGHTS = ['ffn1_pre_g', 'ffn1_w_gate', 'ffn1_w_up', 'ffn1_w_down', 'ffn1_post_g', 'mix_pre_g', 'w_in', 'conv_w', 'conv_b',
           'dt_bias', 'a_log', 'd_skip', 'ssd_norm_g', 'w_ssd_proj', 'q_norm_g', 'w_uq', 'kv_norm_g', 'w_uk', 'w_uv',
           'w_mla_proj', 'gate_bias', 'w_out', 'mix_post_g', 'xa_pre_g', 'mem_norm_g', 'w_xq', 'w_xk', 'w_xv', 'w_xo',
           'xa_post_g', 'ffn2_pre_g', 'ffn2_w_gate', 'ffn2_w_up', 'ffn2_w_down', 'ffn2_post_g']


def _pack_rows(w, kind, width):
    m = w[0].T if kind == "col" else w[0]
    return m.reshape(-1, width)


KIND = dict(BIG)
GATHER_PLAN = {
    "first": (["ffn1_w_gate", "ffn1_w_up", "ffn1_w_down"], []),
    "ffn1_gate_up": (["w_ssd_proj", "w_mla_proj", "w_out", "w_uk", "w_uv"], ["w_in", "w_uq"]),
    "attn_fwd": (["w_xq", "w_xk", "w_xv", "w_xo", "ffn2_w_gate", "ffn2_w_up", "ffn2_w_down"], []),
}
SCATTER_PLAN = {
    "attn_bwd": (["ffn2_w_gate", "ffn2_w_up", "ffn2_w_down", "w_xq", "w_xk", "w_xv", "w_xo"], []),
    "in_bwd": (["w_ssd_proj", "w_mla_proj", "w_out", "w_uk", "w_uv"], ["w_uq"]),
    "ffn1:down_bwd": ([], ["w_in"]),
    "ffn1:dwg": (["ffn1_w_down"], []),
    "ffn1:dwu": (["ffn1_w_gate"], []),
    "ffn1:gate_up_bwd": (["ffn1_w_up"], []),
}


class Stage:
    def __init__(self, w):
        self.w = w
        self.width = {n: PACK_W if (n, k) in MAIN else FLAT_W for n, k in BIG}
        self.nrows = {n: math.prod(w[n].shape) // self.width[n] for n, _ in BIG}
        self.recv = {}

    def _shards(self, tag):
        names_main, names_flat = GATHER_PLAN[tag]
        pack = lambda n: _pack_rows(self.w[n], KIND[n], self.width[n]).astype(BF16)
        shards = []
        if names_main:
            shards.append(jnp.concatenate([pack(n) for n in names_main], axis=0))
        if names_flat:
            bits = lax.bitcast_convert_type(self.w["conv_w"][0], BF16).reshape(-1, FLAT_W)
            shards.append(_pad_rows(jnp.concatenate([pack(n) for n in names_flat] + [bits], axis=0), 16))
        return shards

    def gather(self, tag):
        return GatherComm(self._shards(tag))

    def gathered(self, tag, outs, W, p):
        names_main, names_flat = GATHER_PLAN[tag]
        outs = list(outs)
        for names in (names_main, names_flat):
            if not names:
                continue
            buf, r0 = outs.pop(0), 0
            for n in names:
                K = self.w[n].shape[1] if KIND[n] == "col" else PACK_W
                W[n] = buf[:, r0:r0 + self.nrows[n]].reshape(-1, K)
                r0 += self.nrows[n]
            if names is names_flat:
                cw = self.w["conv_w"]
                nbits = 2 * math.prod(cw.shape) // FLAT_W
                bits = buf[:, r0:r0 + nbits].reshape((N_DEV,) + cw.shape[1:] + (2,))
                p["conv_w"] = lax.bitcast_convert_type(bits, F32).transpose(1, 0, 2).reshape(cw.shape[1], -1)

    def scatter(self, tag, gw):
        if tag not in SCATTER_PLAN:
            return None
        groups = [[gw[n].reshape(N_DEV, self.nrows[n], self.width[n]) for n in names] for names in SCATTER_PLAN[tag] if names]
        return ScatterComm(groups)

    def scattered(self, tag, outs):
        if tag in SCATTER_PLAN:
            self.recv[tag] = outs


def _pad_rows(a, mult):
    r = (-a.shape[0]) % mult
    return a if r == 0 else jnp.concatenate([a, jnp.zeros((r,) + a.shape[1:], a.dtype)], axis=0)


def _pack_small(vals, loss_row=None, conv_w=None):
    rows = []
    for v in vals:
        f = v.reshape(-1)
        f = jnp.concatenate([f, jnp.zeros(((-f.shape[0]) % 128,), F32)])
        rows.append(f.reshape(-1, 128))
    if conv_w is not None:
        rows.append(conv_w.reshape(-1, 128))
    if loss_row is not None:
        rows.append(loss_row)
    return _pad_rows(jnp.concatenate(rows, axis=0), 8)


def _unpack_small(buf, shapes):
    out, r = [], 0
    for shp in shapes:
        n = math.prod(shp)
        nr = -(-n // 128)
        out.append(buf[r:r + nr].reshape(-1)[:n].reshape(shp))
        r += nr
    return out, r


def _tn(a, b, name, out_dtype=BF16, comm=None):
    M, N = a.shape[1], b.shape[1]
    T = a.shape[0]
    tm = M if M <= 1536 else M // 2
    tk = 512 if T % 512 == 0 and T > 512 else None
    res = mm([[(a, b, "tn")]], [out_dtype], name, tm=tm, tn=N, tk=tk, comm=comm)
    return res[0] if comm is None else (res[0], res[1:])


class NoStage:
    def gather(self, tag):
        return None

    def gathered(self, tag, outs, W, p):
        pass

    def scatter(self, tag, gw):
        return None

    def scattered(self, tag, outs):
        pass


def _ffn_fwd(x, gpre, gpost, wg_t, wu_t, wd, tag, comm=None):
    h = rms_fwd(x, gpre, tag + "_pre")
    swi = lambda accs, ex: (accs[0], accs[1], _silu_parts(accs[0])[0] * accs[1])
    res = mm([[(h, wg_t, "nt")], [(h, wu_t, "nt")]], [F32, F32, BF16], tag + "_gate_up", tn=256, epi=swi, comm=comm)
    G, U, A = res[:3]
    H = mm1(A, wd, "nn", F32, tag + "_down")
    y = resid_fwd(x, H, gpost, FFN_RES, tag + "_post")
    return y, (x, h, G, U, A, H), res[3:]


def _ffn_bwd(dy, saved, gpre, gpost, wg_t, wu_t, wd, tag, stage, gw):
    x, h, G, U, A, H = saved
    dH, dgpost = resid_bwd(H, gpost, dy, FFN_RES, tag + "_post_bwd")

    def dswi(accs, ex):
        sil, dsil = _silu_parts(ex[0])
        return accs[0] * ex[1] * dsil, accs[0] * sil

    def hosted(where, call):
        comm = stage.scatter(tag + ":" + where, gw)
        res = call(comm)
        if comm is None:
            return res
        stage.scattered(tag + ":" + where, res[1])
        return res[0]

    res = hosted("down_bwd", lambda comm: (lambda r: r if comm is None else (r[:2], r[2:]))(
        mm([[(dH, wd, "nt")]], [BF16, BF16], tag + "_down_bwd", tn=256, epi=dswi, extras=[G, U], comm=comm)))
    dG, dU = res
    gw[tag + "_w_down"] = _tn(A, dH, tag + "_dwd")
    gw[tag + "_w_gate"] = hosted("dwg", lambda comm: _tn(dG, h, tag + "_dwg", comm=comm))
    gw[tag + "_w_up"] = hosted("dwu", lambda comm: _tn(dU, h, tag + "_dwu", comm=comm))
    dh = hosted("gate_up_bwd", lambda comm: (lambda r: r[0] if comm is None else (r[0], r[1:]))(
        mm([[(dG, wg_t, "nn"), (dU, wu_t, "nn")]], [F32], tag + "_gate_up_bwd", tm=512, comm=comm)))
    dx, dgpre = rms_bwd(x, gpre, dh, tag + "_pre_bwd", resid=dy)
    return dx, dgpre, dgpost


def _rope_tables(positions):
    inv = ROPE_THETA ** (-jnp.arange(0, ROPE, 2, dtype=F32) / ROPE)
    ang = positions.astype(F32).reshape(-1)[:, None] * inv
    return jnp.cos(ang), jnp.sin(ang)


def _heads(t, nseq, width):
    T = t.shape[0]
    return t.reshape(nseq, T // nseq, MLA_H, width).transpose(0, 2, 1, 3)


def _unheads(t):
    b, h, s, w = t.shape
    return t.transpose(0, 2, 1, 3).reshape(b * s, h, w)


def _local_step(x, mem, positions, tgt, W, p, stage=None):
    stage = stage or NoStage()
    nseq = x.shape[0]
    T = nseq * x.shape[1]
    x0 = x.reshape(T, D)
    mem2 = mem.reshape(-1, D)
    cos, sin = _rope_tables(positions)

    x1, ffn1, arrived = _ffn_fwd(x0, p["ffn1_pre_g"], p["ffn1_post_g"], W["ffn1_w_gate"], W["ffn1_w_up"], W["ffn1_w_down"],
                                 "ffn1", comm=stage.gather("ffn1_gate_up"))
    stage.gathered("ffn1_gate_up", arrived, W, p)

    w_in_t = W["w_in"]
    bounds = [0]
    for n in (SSD_INNER, CONV_CH, SSD_H, QR, KVR, ROPE, 2 * D):
        bounds.append(bounds[-1] + n)
    wt_z, wt_xbc, wt_dt, wt_q, wt_kv, wt_kr, wt_gate = [w_in_t[bounds[i]:bounds[i + 1]] for i in range(7)]
    wt_dt, wt_kr = _pad_rows(wt_dt, SLOT), _pad_rows(wt_kr, SLOT)
    wt_dtkr = jnp.concatenate([wt_dt, wt_kr], axis=0)
    hm = rms_fwd(x1, p["mix_pre_g"], "mix_pre")
    z = mm1(hm, wt_z, "nt", F32, "in_z")
    xbc = mm1(hm, wt_xbc, "nt", F32, "in_xbc")
    q_c = mm1(hm, wt_q, "nt", F32, "in_q", tn=QR)
    kv_c = mm1(hm, wt_kv, "nt", F32, "in_kv")
    dtkr = mm1(hm, wt_dtkr, "nt", F32, "in_dtkr")
    gl = mm1(hm, wt_gate, "nt", F32, "in_gate")

    xbc_act = conv_fwd(xbc, p["conv_w"], p["conv_b"], nseq)
    y_ssd_core, prev = ssd_fwd(xbc_act, dtkr, p["dt_bias"], p["a_log"], p["d_skip"], nseq)
    yn = gated_norm_fwd(y_ssd_core, z, p["ssd_norm_g"], "ssd_norm")
    y_ssd = mm1(yn, W["w_ssd_proj"], "nn", F32, "ssd_proj")

    slot_rows = lambda wt, per: jnp.pad(wt.reshape(MLA_H, per, -1), ((0, 0), (0, SLOT - per), (0, 0))).reshape(MLA_H * SLOT, -1)
    wq_s, wk_s, wv_s = slot_rows(W["w_uq"], QK), slot_rows(W["w_uk"], NOPE), slot_rows(W["w_uv"], VD)
    wo_s = slot_rows(W["w_mla_proj"], VD)
    qn = rms_fwd(q_c, p["q_norm_g"], "q_norm")
    q_s = mm1(qn, wq_s, "nt", F32, "uq")
    kvn = rms_fwd(kv_c, p["kv_norm_g"], "kv_norm")
    kn_s = mm1(kvn, wk_s, "nt", BF16, "uk")
    v_s = mm1(kvn, wv_s, "nt", BF16, "uv")
    cos16, sin16 = cos, sin
    Qc, Kc = rope_slot_fwd(q_s, kn_s, dtkr, cos16, sin16, "rope")
    o_s, lse, *arrived = attn_slot_fwd(Qc, Kc, v_s, nseq, comm=stage.gather("attn_fwd"))
    stage.gathered("attn_fwd", arrived, W, p)
    y_mla = mm1(o_s, wo_s, "nn", F32, "mla_proj")

    merged = merge_fwd(gl, y_ssd, y_mla, p["gate_bias"], "merge")
    hmix = mm1(merged, W["w_out"], "nn", F32, "mix_out")
    x2 = resid_fwd(x1, hmix, p["mix_post_g"], 1.0, "mix_post")

    hq = rms_fwd(x2, p["xa_pre_g"], "xa_pre")
    mn = rms_fwd(mem2, p["mem_norm_g"], "mem_norm")
    xq = mm1(hq, W["w_xq"], "nn", BF16, "xq")
    xk = mm1(mn, W["w_xk"], "nn", BF16, "xk")
    xv = mm1(mn, W["w_xv"], "nn", BF16, "xv")
    xo = xattn_fwd(xq, xk, xv, nseq)
    ho = mm1(xo, W["w_xo"], "nn", F32, "xo")
    x3 = resid_fwd(x2, ho, p["xa_post_g"], 1.0, "xa_post")

    x4, ffn2, _ = _ffn_fwd(x3, p["ffn2_pre_g"], p["ffn2_post_g"], W["ffn2_w_gate"], W["ffn2_w_up"], W["ffn2_w_down"], "ffn2")
    dx4, loss_row = loss_head(x4, tgt.reshape(T, D), "loss")

    gw, gs = {}, {}
    dx3, gs["ffn2_pre_g"], gs["ffn2_post_g"] = _ffn_bwd(
        dx4, ffn2, p["ffn2_pre_g"], p["ffn2_post_g"], W["ffn2_w_gate"], W["ffn2_w_up"], W["ffn2_w_down"], "ffn2", stage, gw)

    dho, gs["xa_post_g"] = resid_bwd(ho, p["xa_post_g"], dx3, 1.0, "xa_post_bwd")
    dxo = mm1(dho, W["w_xo"], "nt", BF16, "xo_bwd")
    gw["w_xo"] = _tn(xo, dho, "d_w_xo")
    dxq, dxk, dxv = xattn_bwd(xq, xk, xv, dxo, nseq)
    dhq = mm1(dxq, W["w_xq"], "nt", F32, "xq_bwd")
    gw["w_xq"] = _tn(hq, dxq, "d_w_xq")
    dmn = mm([[(dxk, W["w_xk"], "nt"), (dxv, W["w_xv"], "nt")]], [F32], "xkv_bwd")[0]
    gw["w_xk"] = _tn(mn, dxk, "d_w_xk")
    gw["w_xv"] = _tn(mn, dxv, "d_w_xv")
    _, gs["mem_norm_g"] = rms_bwd(mem2, p["mem_norm_g"], dmn, "mem_norm_bwd", dx_dtype=BF16)
    dx2, gs["xa_pre_g"] = rms_bwd(x2, p["xa_pre_g"], dhq, "xa_pre_bwd", resid=dx3)

    dhmix, gs["mix_post_g"] = resid_bwd(hmix, p["mix_post_g"], dx2, 1.0, "mix_post_bwd")
    dmerged = mm1(dhmix, W["w_out"], "nt", F32, "mix_out_bwd")
    gw["w_out"] = _tn(merged, dhmix, "d_w_out")
    dys, dym, dgl, gs["gate_bias"] = merge_bwd(gl, y_ssd, y_mla, dmerged, p["gate_bias"], "merge_bwd")

    unslot = lambda g, per: g.reshape(MLA_H, SLOT, -1)[:, :per].reshape(MLA_H * per, -1)
    do_s = mm1(dym, wo_s, "nt", BF16, "mla_proj_bwd")
    gw["w_mla_proj"] = unslot(_tn(o_s, dym, "d_w_mla_proj"), VD)
    dQc, dKc, dv_s, *sent = attn_slot_bwd(Qc, Kc, v_s, o_s, lse, do_s, nseq, comm=stage.scatter("attn_bwd", gw))
    stage.scattered("attn_bwd", sent)
    dq_s, dkn_s, dkr = rope_slot_bwd(dQc, dKc, cos16, sin16, "rope_bwd")

    dyn = mm1(dys, W["w_ssd_proj"], "nt", F32, "ssd_proj_bwd")
    gw["w_ssd_proj"] = _tn(yn, dys, "d_w_ssd_proj")
    dyc, dz, gs["ssd_norm_g"] = gated_norm_bwd(y_ssd_core, z, dyn, p["ssd_norm_g"], "ssd_norm_bwd")
    dxbc_act, ddtr, gs["dt_bias"], gs["a_log"], gs["d_skip"] = ssd_bwd(
        xbc_act, dtkr, p["dt_bias"], p["a_log"], p["d_skip"], prev, dyc, nseq)
    dxbc, gs["conv_w"], gs["conv_b"] = conv_bwd(xbc, p["conv_w"], p["conv_b"], dxbc_act, nseq)

    dqn = mm1(dq_s, wq_s, "nn", F32, "uq_bwd", tn=QR)
    gw["w_uq"] = unslot(_tn(dq_s, qn, "d_w_uq"), QK)
    dq_c, gs["q_norm_g"] = rms_bwd(q_c, p["q_norm_g"], dqn, "q_norm_bwd", dx_dtype=BF16)
    dkvn = mm([[(dkn_s, wk_s, "nn"), (dv_s, wv_s, "nn")]], [F32], "ukv_bwd")[0]
    gw["w_uk"] = unslot(_tn(dkn_s, kvn, "d_w_uk"), NOPE)
    gw["w_uv"] = unslot(_tn(dv_s, kvn, "d_w_uv"), VD)
    dkv_c, gs["kv_norm_g"] = rms_bwd(kv_c, p["kv_norm_g"], dkvn, "kv_norm_bwd", dx_dtype=BF16)

    dhm, *sent = mm([[(dz, wt_z, "nn"), (dxbc, wt_xbc, "nn"), (ddtr, wt_dt, "nn"), (dq_c, wt_q, "nn"), (dkv_c, wt_kv, "nn"),
                      (dkr, wt_kr, "nn"), (dgl, wt_gate, "nn")]], [F32], "in_bwd", tm=512, comm=stage.scatter("in_bwd", gw))
    stage.scattered("in_bwd", sent)
    gw["w_in"] = jnp.concatenate([_tn(dz, hm, "d_w_in_z"), _tn(dxbc, hm, "d_w_in_xbc"), _tn(ddtr, hm, "d_w_in_dt")[:SSD_H],
                                  _tn(dq_c, hm, "d_w_in_q"), _tn(dkv_c, hm, "d_w_in_kv"), _tn(dkr, hm, "d_w_in_kr")[:ROPE],
                                  _tn(dgl, hm, "d_w_in_gate")], axis=0)
    dx1, gs["mix_pre_g"] = rms_bwd(x1, p["mix_pre_g"], dhm, "mix_pre_bwd", resid=dx2)

    dx0, gs["ffn1_pre_g"], gs["ffn1_post_g"] = _ffn_bwd(
        dx1, ffn1, p["ffn1_pre_g"], p["ffn1_post_g"], W["ffn1_w_gate"], W["ffn1_w_up"], W["ffn1_w_down"], "ffn1", stage, gw)
    return loss_row, dx0.reshape(x.shape), gw, gs


def kernel(x, mem, positions, ffn1_pre_g, ffn1_w_gate, ffn1_w_up, ffn1_w_down, ffn1_post_g, mix_pre_g, w_in, conv_w, conv_b, dt_bias, a_log, d_skip, ssd_norm_g, w_ssd_proj, q_norm_g, w_uq, kv_norm_g, w_uk, w_uv, w_mla_proj, gate_bias, w_out, mix_post_g, xa_pre_g, mem_norm_g, w_xq, w_xk, w_xv, w_xo, xa_post_g, ffn2_pre_g, ffn2_w_gate, ffn2_w_up, ffn2_w_down, ffn2_post_g, loss_target, m_ffn1_pre_g, m_ffn1_w_gate, m_ffn1_w_up, m_ffn1_w_down, m_ffn1_post_g, m_mix_pre_g, m_w_in, m_conv_w, m_conv_b, m_dt_bias, m_a_log, m_d_skip, m_ssd_norm_g, m_w_ssd_proj, m_q_norm_g, m_w_uq, m_kv_norm_g, m_w_uk, m_w_uv, m_w_mla_proj, m_gate_bias, m_w_out, m_mix_post_g, m_xa_pre_g, m_mem_norm_g, m_w_xq, m_w_xk, m_w_xv, m_w_xo, m_xa_post_g, m_ffn2_pre_g, m_ffn2_w_gate, m_ffn2_w_up, m_ffn2_w_down, m_ffn2_post_g, v_ffn1_pre_g, v_ffn1_w_gate, v_ffn1_w_up, v_ffn1_w_down, v_ffn1_post_g, v_mix_pre_g, v_w_in, v_conv_w, v_conv_b, v_dt_bias, v_a_log, v_d_skip, v_ssd_norm_g, v_w_ssd_proj, v_q_norm_g, v_w_uq, v_kv_norm_g, v_w_uk, v_w_uv, v_w_mla_proj, v_gate_bias, v_w_out, v_mix_post_g, v_xa_pre_g, v_mem_norm_g, v_w_xq, v_w_xk, v_w_xv, v_w_xo, v_xa_post_g, v_ffn2_pre_g, v_ffn2_w_gate, v_ffn2_w_up, v_ffn2_w_down, v_ffn2_post_g):
    a = dict(locals())
    w = {n: a[n] for n in WEIGHTS}
    m = {n: a["m_" + n] for n in WEIGHTS}
    v = {n: a["v_" + n] for n in WEIGHTS}

    stage = Stage(w)
    W, p = {}, {n: w[n] for n in SMALL}
    stage.gathered("first", run_comm(stage.gather("first"), "allgather_first"), W, p)

    loss_row, grad_x, gw, gs = _local_step(x, mem, positions, loss_target, W, p, stage)

    sm = _pack_small([gs[n] for n in SMALL], loss_row=loss_row, conv_w=gs["conv_w"])
    srecv, = run_comm(ScatterComm([[jnp.broadcast_to(sm[None], (N_DEV,) + sm.shape)]]), "exchange_small")
    s_rows = sum_slots(srecv, "sum_small", tr=sm.shape[0])
    grads = {}
    for tag, (names_main, names_flat) in SCATTER_PLAN.items():
        bufs = list(stage.recv[tag])
        for names, whole in ((names_main, False), (names_flat, True)):
            if not names:
                continue
            buf = bufs.pop(0)
            rows = buf.shape[1]
            g_rows = sum_slots(buf, "sum_" + tag.replace(":", "_") + ("_flat" if whole else ""),
                               tr=rows if whole else _tile(rows, (256, 224, 176, 128, 64, 32, 16)))
            r0 = 0
            for n in names:
                blk = g_rows[r0:r0 + stage.nrows[n]]
                grads[n] = (blk.reshape(w[n].shape[2], w[n].shape[1]).T if KIND[n] == "col" else blk)[None]
                r0 += stage.nrows[n]
    conv_w_full = p["conv_w"]
    small_g, r1 = _unpack_small(s_rows, [w[n].shape for n in SMALL])
    for n, g in zip(SMALL, small_g):
        grads[n] = g
    ncw = math.prod(conv_w_full.shape) // 128
    cw_grad_full = s_rows[r1:r1 + ncw].reshape(conv_w_full.shape)
    wsh = conv_w.shape[2]
    grads["conv_w"] = lax.dynamic_slice_in_dim(cw_grad_full, _dev_index() * wsh, wsh, axis=1)[None]
    loss = s_rows[r1 + ncw, 0]

    delta, new_m, new_v = {}, {}, {}
    for n, _ in BIG + [("conv_w", "col")]:
        shp = w[n].shape
        d_, m_, v_ = adamw(w[n][0], grads[n][0], m[n][0], v[n][0], "adamw_" + n)
        delta[n], new_m[n], new_v[n] = d_.reshape(shp), m_.reshape(shp), v_.reshape(shp)
    sp = [_pack_small([t[n] for n in SMALL]) for t in (w, grads, m, v)]
    outs = adamw(sp[0], sp[1], sp[2], sp[3], "adamw_small")
    for t, buf in zip((delta, new_m, new_v), outs):
        vals, _ = _unpack_small(buf, [w[n].shape for n in SMALL])
        for n, val in zip(SMALL, vals):
            t[n] = val
    return (loss, grad_x, *[grads[n] for n in WEIGHTS], *[delta[n] for n in WEIGHTS],
            *[new_m[n] for n in WEIGHTS], *[new_v[n] for n in WEIGHTS])
```

```python
import functools
import math

import jax
import jax.numpy as jnp
from jax import lax
from jax.experimental import pallas as pl
from jax.experimental.pallas import tpu as pltpu

F32, BF16 = jnp.float32, jnp.bfloat16
HI = lax.Precision.HIGHEST
MESH = pl.DeviceIdType.MESH
N_DEV = 8

D = 1024
DFF = 2816
SSD_H, SSD_P, SSD_G, SSD_N, SSD_L = 16, 64, 2, 128, 128
SSD_INNER = SSD_H * SSD_P
CONV_K, CONV_CH = 4, 1536
MLA_H, QR, KVR, NOPE, ROPE, VD = 16, 384, 256, 64, 32, 64
QK = NOPE + ROPE
ROPE_THETA = 10000.0
XA_H, XA_D = 4, 256
EPS = 1e-6
FFN_RES = 0.5
LR, B1, B2, AEPS, WD, STEP = 0.001, 0.9, 0.999, 1e-08, 0.01, 10

VMEM_LIMIT = 56 * 2**20


def _cp(*sem):
    return pltpu.CompilerParams(dimension_semantics=sem, vmem_limit_bytes=VMEM_LIMIT)


def _sigmoid(x):
    return 1.0 / (1.0 + jnp.exp(-x))


def _softplus(x):
    return jnp.where(x > 20.0, x, jnp.log(1.0 + jnp.exp(jnp.minimum(x, 20.0))))


def _dot(a, b, dims="nn", hi=False):
    ca = 0 if dims[0] == "t" else 1
    cb = 1 if dims[1] == "t" else 0
    if hi:
        return lax.dot_general(a, b, (((ca,), (cb,)), ((), ())), precision=HI, preferred_element_type=F32)
    return lax.dot_general(a.astype(BF16), b.astype(BF16), (((ca,), (cb,)), ((), ())), preferred_element_type=F32)


def _ssd_common(dtr, dtb, alog):
    L = dtr.shape[0]
    dt = _softplus(dtr + dtb)
    a = -jnp.exp(alog)
    adt = dt * a
    r = lax.broadcasted_iota(jnp.int32, (L, L), 0)
    c = lax.broadcasted_iota(jnp.int32, (L, L), 1)
    lower = r >= c
    tri = lower.astype(F32)
    cs = _dot(tri, adt, "nn", hi=True)
    cs_t = _dot(adt, tri, "tt", hi=True)
    return dt, a, cs, cs_t, lower


def _head_expand():
    hh = lax.broadcasted_iota(jnp.int32, (SSD_H, SSD_INNER), 0)
    jj = lax.broadcasted_iota(jnp.int32, (SSD_H, SSD_INNER), 1)
    return ((jj >= hh * SSD_P) & (jj < hh * SSD_P + SSD_P)).astype(F32)


def _head_reduce():
    hh = lax.broadcasted_iota(jnp.int32, (SSD_INNER, SSD_H), 1)
    jj = lax.broadcasted_iota(jnp.int32, (SSD_INNER, SSD_H), 0)
    return ((jj >= hh * SSD_P) & (jj < hh * SSD_P + SSD_P)).astype(F32)


def ssd_fwd(xbc, dtr, dtb, alog, dsk, nseq):
    T = xbc.shape[0]
    S = T // nseq
    C = S // SSD_L
    L = SSD_L
    NP = SSD_H // 2

    def body(x_ref, b_ref, c_ref, dtr_ref, dtb_ref, alog_ref, dsk_ref, y_ref, prev_ref, st_ref):
        ci = pl.program_id(1)

        @pl.when(ci == 0)
        def _():
            st_ref[...] = jnp.zeros_like(st_ref)

        dt, a, cs, cs_t, lower = _ssd_common(dtr_ref[:, 0:SSD_H], dtb_ref[...], alog_ref[...])
        E = _head_expand()
        X = x_ref[...]
        dt_e = _dot(dt, E, hi=True)
        cs_e = _dot(cs, E, hi=True)
        csl_e = cs_e[L - 1:L, :]
        Xd = X * dt_e
        Xf = Xd * jnp.exp(csl_e - cs_e)
        e_e = jnp.exp(cs_e)
        y_ref[...] = _dot(dsk_ref[...], E, hi=True) * X
        lane = lax.broadcasted_iota(jnp.int32, (1, 2 * SSD_P), 1)
        rowp = lax.broadcasted_iota(jnp.int32, (2 * SSD_P, 1), 0)
        for g in range(SSD_G):
            Bg = b_ref[:, g * SSD_N:(g + 1) * SSD_N]
            Cg = c_ref[:, g * SSD_N:(g + 1) * SSD_N]
            cb = _dot(Cg, Bg, "nt")
            for pp in range(NP // SSD_G):
                p = g * (NP // SSD_G) + pp
                sl = slice(p * 2 * SSD_P, (p + 1) * 2 * SSD_P)
                Xd_p = Xd[:, sl]
                yd = jnp.zeros((L, 2 * SSD_P), F32)
                for q in range(2):
                    h = 2 * p + q
                    m = jnp.where(lower, jnp.exp(jnp.minimum(cs[:, h:h + 1] - cs_t[h:h + 1, :], 0.0)), 0.0)
                    mask = (lane >= q * SSD_P) & (lane < (q + 1) * SSD_P)
                    yd = yd + _dot(cb * m, jnp.where(mask, Xd_p, 0.0))
                S0 = st_ref[p]
                prev_ref[0, 0, p] = S0
                z = _dot(Cg, S0, "nt")
                y_ref[:, sl] += yd + z * e_e[:, sl]
                h0 = 2 * p
                dec = jnp.where(rowp < SSD_P, jnp.exp(cs[L - 1:L, h0:h0 + 1]), jnp.exp(cs[L - 1:L, h0 + 1:h0 + 2]))
                st_ref[p] = S0 * dec + _dot(Xf[:, sl], Bg, "tn")

    row = lambda b, c: (b * C + c, 0)
    return pl.pallas_call(
        body, grid=(nseq, C), name="ssd_fwd",
        in_specs=[pl.BlockSpec((L, SSD_INNER), row),
                  pl.BlockSpec((L, SSD_G * SSD_N), lambda b, c: (b * C + c, SSD_INNER // (SSD_G * SSD_N))),
                  pl.BlockSpec((L, SSD_G * SSD_N), lambda b, c: (b * C + c, SSD_INNER // (SSD_G * SSD_N) + 1)),
                  pl.BlockSpec((L, 128), row),
                  pl.BlockSpec((1, SSD_H), lambda b, c: (0, 0)),
                  pl.BlockSpec((1, SSD_H), lambda b, c: (0, 0)),
                  pl.BlockSpec((1, SSD_H), lambda b, c: (0, 0))],
        out_specs=[pl.BlockSpec((L, SSD_INNER), row),
                   pl.BlockSpec((1, 1, NP, 2 * SSD_P, SSD_N), lambda b, c: (b, c, 0, 0, 0))],
        out_shape=[jax.ShapeDtypeStruct((T, SSD_INNER), F32),
                   jax.ShapeDtypeStruct((nseq, C, NP, 2 * SSD_P, SSD_N), F32)],
        scratch_shapes=[pltpu.VMEM((NP, 2 * SSD_P, SSD_N), F32)],
        compiler_params=_cp("parallel", "arbitrary"),
    )(xbc, xbc, xbc, dtr, dtb, alog, dsk)


def ssd_bwd(xbc, dtr, dtb, alog, dsk, prev, dy, nseq, comm=None):
    T = xbc.shape[0]
    S = T // nseq
    C = S // SSD_L
    L = SSD_L
    NP = SSD_H // 2

    def body(x_ref, b_ref, c_ref, dtr_ref, dtb_ref, alog_ref, dsk_ref, prev_ref, dy_ref,
             dxbc_ref, ddtr_ref, ddtb_ref, dalog_ref, ddsk_ref, ds_ref):
        bi = pl.program_id(0)
        ci = pl.program_id(1)

        @pl.when(ci == 0)
        def _():
            ds_ref[...] = jnp.zeros_like(ds_ref)

        @pl.when((ci == 0) & (bi == 0))
        def _():
            ddtb_ref[...] = jnp.zeros_like(ddtb_ref)
            dalog_ref[...] = jnp.zeros_like(dalog_ref)
            ddsk_ref[...] = jnp.zeros_like(ddsk_ref)

        dtr = dtr_ref[:, 0:SSD_H]
        dtb = dtb_ref[...]
        dt, a, cs, cs_t, lower = _ssd_common(dtr, dtb, alog_ref[...])
        upper = lax.broadcasted_iota(jnp.int32, (L, L), 1) >= lax.broadcasted_iota(jnp.int32, (L, L), 0)
        E = _head_expand()
        ET = _head_reduce()
        X = x_ref[...]
        dY = dy_ref[...]
        dt_e = _dot(dt, E, hi=True)
        cs_e = _dot(cs, E, hi=True)
        csl_e = cs_e[L - 1:L, :]
        f_e = jnp.exp(csl_e - cs_e)
        e_e = jnp.exp(cs_e)
        dsk_e = _dot(dsk_ref[...], E, hi=True)
        Xd = X * dt_e
        Xf = Xd * f_e
        lane = lax.broadcasted_iota(jnp.int32, (1, 2 * SSD_P), 1)
        rowp = lax.broadcasted_iota(jnp.int32, (2 * SSD_P, 1), 0)
        hsel = lax.broadcasted_iota(jnp.int32, (1, SSD_H), 1)
        dcs = jnp.zeros((L, SSD_H), F32)
        dcsl = jnp.zeros((1, SSD_H), F32)
        for g in range(SSD_G):
            Bg = b_ref[:, g * SSD_N:(g + 1) * SSD_N]
            Cg = c_ref[:, g * SSD_N:(g + 1) * SSD_N]
            cb = _dot(Cg, Bg, "nt")
            cbt = _dot(Bg, Cg, "nt")
            dB = jnp.zeros((L, SSD_N), F32)
            dC = jnp.zeros((L, SSD_N), F32)
            for pp in range(NP // SSD_G):
                p = g * (NP // SSD_G) + pp
                sl = slice(p * 2 * SSD_P, (p + 1) * 2 * SSD_P)
                Xd_p = Xd[:, sl]
                dY_p = dY[:, sl]
                dXd_p = jnp.zeros((L, 2 * SSD_P), F32)
                for q in range(2):
                    h = 2 * p + q
                    mask = (lane >= q * SSD_P) & (lane < (q + 1) * SSD_P)
                    col = cs[:, h:h + 1]
                    rw = cs_t[h:h + 1, :]
                    m = jnp.where(lower, jnp.exp(jnp.minimum(col - rw, 0.0)), 0.0)
                    mt = jnp.where(upper, jnp.exp(jnp.minimum(rw - col, 0.0)), 0.0)
                    dYm = jnp.where(mask, dY_p, 0.0)
                    dW = _dot(dYm, Xd_p, "nt")
                    dWt = _dot(Xd_p, dYm, "nt")
                    w = cb * m
                    wt = cbt * mt
                    dC = dC + _dot(dW * m, Bg)
                    dB = dB + _dot(dWt * mt, Cg)
                    dXd_p = dXd_p + jnp.where(mask, _dot(wt, dY_p), 0.0)
                    qcol = jnp.sum(dW * w, axis=1, keepdims=True) - jnp.sum(dWt * wt, axis=1, keepdims=True)
                    dcs = dcs + qcol * (hsel == h).astype(F32)
                S0 = prev_ref[0, 0, p]
                dSn = ds_ref[p]
                dZ = dY_p * e_e[:, sl]
                dC = dC + _dot(dZ, S0)
                h0 = 2 * p
                el0 = jnp.exp(cs[L - 1:L, h0:h0 + 1])
                el1 = jnp.exp(cs[L - 1:L, h0 + 1:h0 + 2])
                dec = jnp.where(rowp < SSD_P, el0, el1)
                ds_ref[p] = dSn * dec + _dot(dZ, Cg, "tn")
                dXf_p = _dot(Bg, dSn, "nt")
                dB = dB + _dot(Xf[:, sl], dSn)
                rs = jnp.sum(dSn * S0, axis=1, keepdims=True)
                s0 = jnp.sum(jnp.where(rowp < SSD_P, rs, 0.0), axis=0, keepdims=True) * el0
                s1 = jnp.sum(jnp.where(rowp >= SSD_P, rs, 0.0), axis=0, keepdims=True) * el1
                dcsl = dcsl + s0 * (hsel == h0).astype(F32) + s1 * (hsel == h0 + 1).astype(F32)
                dxbc_ref[:, sl] = dXd_p
                y_off = _dot(Cg, S0, "nt") * e_e[:, sl]
                t1 = dY_p * y_off - dXf_p * Xf[:, sl]
                r1 = jnp.where(lane < SSD_P, t1, 0.0)
                c0 = jnp.sum(r1, axis=1, keepdims=True)
                c1 = jnp.sum(t1 - r1, axis=1, keepdims=True)
                dcs = dcs + c0 * (hsel == h0).astype(F32) + c1 * (hsel == h0 + 1).astype(F32)
                t2 = dXf_p * Xf[:, sl]
                r2 = jnp.where(lane < SSD_P, t2, 0.0)
                dcsl = dcsl + jnp.sum(r2, keepdims=True) * (hsel == h0).astype(F32) \
                    + jnp.sum(t2 - r2, keepdims=True) * (hsel == h0 + 1).astype(F32)
                dxbc_ref[:, sl] += dXf_p * f_e[:, sl]
            dxbc_ref[:, SSD_INNER + g * SSD_N:SSD_INNER + (g + 1) * SSD_N] = dB
            dxbc_ref[:, SSD_INNER + (SSD_G + g) * SSD_N:SSD_INNER + (SSD_G + g + 1) * SSD_N] = dC
        dXd = dxbc_ref[:, 0:SSD_INNER]
        dxbc_ref[:, 0:SSD_INNER] = dXd * dt_e + dsk_e * dY
        rowl = lax.broadcasted_iota(jnp.int32, (L, 1), 0)
        dcs = dcs + jnp.where(rowl == L - 1, dcsl, 0.0)
        dalpha = _dot(upper.astype(F32), dcs, hi=True)
        ddt = _dot(dXd * X, ET, hi=True) + dalpha * a
        dalog_ref[...] += jnp.sum(dalpha * dt, axis=0, keepdims=True) * a
        ddtr = ddt * _sigmoid(dtr + dtb)
        spread = (lax.broadcasted_iota(jnp.int32, (SSD_H, 128), 0) == lax.broadcasted_iota(jnp.int32, (SSD_H, 128), 1)).astype(F32)
        ddtr_ref[...] = _dot(ddtr, spread, hi=True).astype(ddtr_ref.dtype)
        ddtb_ref[...] += jnp.sum(ddtr, axis=0, keepdims=True)
        ddsk_ref[...] += jnp.sum(_dot(dY * X, ET, hi=True), axis=0, keepdims=True)

    rowr = lambda b, c: (b * C + (C - 1 - c), 0)
    small = pl.BlockSpec((1, SSD_H), lambda b, c: (0, 0))
    return _call_with_comm(
        body, (nseq, C), "ssd_bwd",
        [pl.BlockSpec((L, SSD_INNER), rowr),
         pl.BlockSpec((L, SSD_G * SSD_N), lambda b, c: (b * C + (C - 1 - c), SSD_INNER // (SSD_G * SSD_N))),
         pl.BlockSpec((L, SSD_G * SSD_N), lambda b, c: (b * C + (C - 1 - c), SSD_INNER // (SSD_G * SSD_N) + 1)),
         pl.BlockSpec((L, 128), rowr), small, small, small,
         pl.BlockSpec((1, 1, NP, 2 * SSD_P, SSD_N), lambda b, c: (b, C - 1 - c, 0, 0, 0)),
         pl.BlockSpec((L, SSD_INNER), rowr)],
        [xbc, xbc, xbc, dtr, dtb, alog, dsk, prev, dy],
        [pl.BlockSpec((L, CONV_CH), rowr), pl.BlockSpec((L, 128), rowr), small, small, small],
        [jax.ShapeDtypeStruct((T, CONV_CH), F32), jax.ShapeDtypeStruct((T, 128), BF16),
         jax.ShapeDtypeStruct((1, SSD_H), F32), jax.ShapeDtypeStruct((1, SSD_H), F32), jax.ShapeDtypeStruct((1, SSD_H), F32)],
        comm, scratch=[pltpu.VMEM((NP, 2 * SSD_P, SSD_N), F32)], sem=("arbitrary", "arbitrary"))


SLOT = 128
ATT_T = 512


def _col_to_row(col):
    n = col.shape[0]
    eye = lax.broadcasted_iota(jnp.int32, (n, n), 0) == lax.broadcasted_iota(jnp.int32, (n, n), 1)
    return jnp.sum(jnp.where(eye, col, 0.0), axis=0, keepdims=True)


def attn_slot_fwd(q, k, v, nseq, comm=None):
    T = q.shape[0]
    S = T // nseq
    t = min(ATT_T, S)
    nb = S // t
    scale = QK ** -0.5

    def body(q_ref, k_ref, v_ref, o_ref, lse_ref):
        causal = lax.broadcasted_iota(jnp.int32, (t, t), 1) <= lax.broadcasted_iota(jnp.int32, (t, t), 0)
        for qi in range(nb):
            qb = q_ref[qi * t:(qi + 1) * t, :]
            m = l = acc = None
            for kj in range(qi + 1):
                s = _dot(qb, k_ref[kj * t:(kj + 1) * t, :], "nt") * scale
                if kj == qi:
                    s = jnp.where(causal, s, -1e30)
                bm = jnp.max(s, axis=1, keepdims=True)
                if kj == 0:
                    m = bm
                    p = jnp.exp(s - m)
                    l = jnp.sum(p, axis=1, keepdims=True)
                    acc = _dot(p, v_ref[0:t, :])
                else:
                    m_new = jnp.maximum(m, bm)
                    corr = jnp.exp(m - m_new)
                    p = jnp.exp(s - m_new)
                    l = l * corr + jnp.sum(p, axis=1, keepdims=True)
                    acc = acc * corr + _dot(p, v_ref[kj * t:(kj + 1) * t, :])
                    m = m_new
            o_ref[qi * t:(qi + 1) * t, :] = (acc / l).astype(o_ref.dtype)
            lse_ref[0, 0, :, qi * t:(qi + 1) * t] = _col_to_row(m + jnp.log(l))

    blk = pl.BlockSpec((S, SLOT), lambda b, h: (b, h))
    return _call_with_comm(
        body, (nseq, MLA_H), "attn_fwd", [blk, blk, blk], [q, k, v],
        [blk, pl.BlockSpec((1, 1, 1, S), lambda b, h: (b, h, 0, 0))],
        [jax.ShapeDtypeStruct((T, MLA_H * SLOT), BF16), jax.ShapeDtypeStruct((nseq, MLA_H, 1, S), F32)], comm)


def attn_slot_bwd(q, k, v, o, lse, do, nseq, comm=None):
    T = q.shape[0]
    S = T // nseq
    t = min(ATT_T, S)
    nb = S // t
    scale = QK ** -0.5

    def body(q_ref, k_ref, v_ref, o_ref, lse_ref, do_ref, dq_ref, dk_ref, dv_ref):
        causal_t = lax.broadcasted_iota(jnp.int32, (t, t), 0) <= lax.broadcasted_iota(jnp.int32, (t, t), 1)
        ones = jnp.ones((8, SLOT), F32)
        delta = []
        for qi in range(nb):
            sl = slice(qi * t, (qi + 1) * t)
            prod = do_ref[sl, :].astype(F32) * o_ref[sl, :].astype(F32)
            delta.append(_dot(ones, prod, "nt", hi=True)[0:1, :])
        for kj in range(nb):
            ks = slice(kj * t, (kj + 1) * t)
            kb = k_ref[ks, :]
            vb = v_ref[ks, :]
            dk = dv = None
            for qi in range(kj, nb):
                sl = slice(qi * t, (qi + 1) * t)
                qb = q_ref[sl, :]
                dob = do_ref[sl, :]
                st = _dot(kb, qb, "nt") * scale
                pt = jnp.exp(st - lse_ref[0, 0, :, sl])
                if qi == kj:
                    pt = jnp.where(causal_t, pt, 0.0)
                dpt = _dot(vb, dob, "nt")
                dst = (pt * (dpt - delta[qi]) * scale).astype(BF16)
                dvc = _dot(pt, dob)
                dkc = _dot(dst, qb)
                dv = dvc if dv is None else dv + dvc
                dk = dkc if dk is None else dk + dkc
                dqc = _dot(dst, kb, "tn")
                if kj == 0:
                    dq_ref[sl, :] = dqc
                else:
                    dq_ref[sl, :] += dqc
            dk_ref[ks, :] = dk
            dv_ref[ks, :] = dv.astype(dv_ref.dtype)

    blk = pl.BlockSpec((S, SLOT), lambda b, h: (b, h))
    lse_spec = pl.BlockSpec((1, 1, 1, S), lambda b, h: (b, h, 0, 0))
    W = MLA_H * SLOT
    return _call_with_comm(
        body, (nseq, MLA_H), "attn_bwd", [blk, blk, blk, blk, lse_spec, blk], [q, k, v, o, lse, do], [blk, blk, blk],
        [jax.ShapeDtypeStruct((T, W), F32), jax.ShapeDtypeStruct((T, W), F32), jax.ShapeDtypeStruct((T, W), BF16)], comm)


def _rope_coeffs(cos, sin):
    half = ROPE // 2
    r = lax.broadcasted_iota(jnp.int32, (half, SLOT), 0)
    c = lax.broadcasted_iota(jnp.int32, (half, SLOT), 1)
    pc = ((c == r + NOPE) | (c == r + NOPE + half)).astype(F32)
    ps = (c == r + NOPE + half).astype(F32) - (c == r + NOPE).astype(F32)
    lane = lax.broadcasted_iota(jnp.int32, (1, SLOT), 1)
    return _dot(cos, pc, hi=True) + (lane < NOPE).astype(F32), _dot(sin, ps, hi=True)


def _rope_swap(x):
    W = x.shape[1]
    half = ROPE // 2
    lane = lax.broadcasted_iota(jnp.int32, (1, W), 1) & (SLOT - 1)
    up = pltpu.roll(x, W - half, axis=1)
    dn = pltpu.roll(x, half, axis=1)
    return jnp.where((lane >= NOPE) & (lane < NOPE + half), up, jnp.where((lane >= NOPE + half) & (lane < QK), dn, 0.0))


def rope_slot_fwd(q, kn, dtkr, cos, sin, name):
    def fn(qv, knv, krv, cv, sv):
        C, Sg = _rope_coeffs(cv, sv)
        ct, stl = jnp.tile(C, (1, MLA_H)), jnp.tile(Sg, (1, MLA_H))
        qo = qv * ct + _rope_swap(qv) * stl
        r = lax.broadcasted_iota(jnp.int32, (SLOT, SLOT), 0)
        c = lax.broadcasted_iota(jnp.int32, (SLOT, SLOT), 1)
        place = ((c == r + NOPE) & (r < ROPE)).astype(F32)
        kr = _dot(krv, place, hi=True)
        kr = kr * C + _rope_swap(kr) * Sg
        return qo, knv.astype(F32) + jnp.tile(kr, (1, MLA_H))
    W = MLA_H * SLOT
    return rowwise(fn, [q, kn, (dtkr, SLOT, 1), cos, sin], [], [(W, BF16), (W, BF16)], [], name)


def rope_slot_bwd(dq, dk, cos, sin, name):
    def fn(dqv, dkv, cv, sv):
        C, Sg = _rope_coeffs(cv, sv)
        ct, stl = jnp.tile(C, (1, MLA_H)), jnp.tile(Sg, (1, MLA_H))
        dqo = dqv * ct - _rope_swap(dqv) * stl
        tot = dkv[:, 0:SLOT]
        for h in range(1, MLA_H):
            tot = tot + dkv[:, h * SLOT:(h + 1) * SLOT]
        u = tot * C - _rope_swap(tot) * Sg
        r = lax.broadcasted_iota(jnp.int32, (SLOT, SLOT), 0)
        c = lax.broadcasted_iota(jnp.int32, (SLOT, SLOT), 1)
        unplace = ((r == c + NOPE) & (c < ROPE)).astype(F32)
        return dqo, dkv, _dot(u, unplace, hi=True)
    W = MLA_H * SLOT
    return rowwise(fn, [dq, dk, cos, sin], [], [(W, BF16), (W, BF16), (SLOT, BF16)], [], name)


XA_BLK = 512


def xattn_fwd(q, k, v, nseq):
    T = q.shape[0]
    S = T // nseq
    M = k.shape[0] // nseq
    tq = min(XA_BLK, S)
    nq = S // tq
    scale = XA_D ** -0.5

    def body(q_ref, k_ref, v_ref, o_ref):
        s = _dot(q_ref[...], k_ref[...], "nt") * scale
        p = jnp.exp(s - jnp.max(s, axis=1, keepdims=True))
        p = p / jnp.sum(p, axis=1, keepdims=True)
        o_ref[...] = _dot(p, v_ref[...]).astype(o_ref.dtype)

    qs = pl.BlockSpec((tq, XA_D), lambda b, h, i: (b * nq + i, h))
    ks = pl.BlockSpec((M, XA_D), lambda b, h, i: (b, h))
    return pl.pallas_call(
        body, grid=(nseq, XA_H, nq), name="xattn_fwd", in_specs=[qs, ks, ks], out_specs=qs,
        out_shape=jax.ShapeDtypeStruct((T, XA_H * XA_D), BF16),
        compiler_params=_cp("parallel", "parallel", "parallel"),
    )(q, k, v)


def xattn_bwd(q, k, v, do, nseq):
    T = q.shape[0]
    S = T // nseq
    M = k.shape[0] // nseq
    tq = min(XA_BLK, S)
    nq = S // tq
    scale = XA_D ** -0.5

    def body(q_ref, k_ref, v_ref, do_ref, dq_ref, dk_ref, dv_ref):
        @pl.when(pl.program_id(2) == 0)
        def _():
            dk_ref[...] = jnp.zeros_like(dk_ref)
            dv_ref[...] = jnp.zeros_like(dv_ref)

        qb, kb, vb, dob = q_ref[...], k_ref[...], v_ref[...], do_ref[...]
        s = _dot(qb, kb, "nt") * scale
        p = jnp.exp(s - jnp.max(s, axis=1, keepdims=True))
        p = p / jnp.sum(p, axis=1, keepdims=True)
        dp = _dot(dob, vb, "nt")
        ds = p * (dp - jnp.sum(dp * p, axis=1, keepdims=True)) * scale
        dq_ref[...] = _dot(ds, kb).astype(dq_ref.dtype)
        dk_ref[...] += _dot(ds, qb, "tn")
        dv_ref[...] += _dot(p, dob, "tn")

    qs = pl.BlockSpec((tq, XA_D), lambda b, h, i: (b * nq + i, h))
    ks = pl.BlockSpec((M, XA_D), lambda b, h, i: (b, h))
    return pl.pallas_call(
        body, grid=(nseq, XA_H, nq), name="xattn_bwd", in_specs=[qs, ks, ks, qs], out_specs=[qs, ks, ks],
        out_shape=[jax.ShapeDtypeStruct((T, XA_H * XA_D), BF16), jax.ShapeDtypeStruct(k.shape, F32),
                   jax.ShapeDtypeStruct(k.shape, F32)],
        compiler_params=_cp("parallel", "parallel", "arbitrary"),
    )(q, k, v, do)


CONV_BLK = 256


def _shift_down(x, s, rows):
    if s == 0:
        return x
    return jnp.where(rows >= s, pltpu.roll(x, s, axis=0), 0.0)


def _shift_up(x, s, rows):
    if s == 0:
        return x
    S = x.shape[0]
    return jnp.where(rows < S - s, pltpu.roll(x, S - s, axis=0), 0.0)


def conv_fwd(x, w, b, nseq):
    T, CH = x.shape
    S = T // nseq

    def body(x_ref, w_ref, b_ref, o_ref):
        xv = x_ref[...]
        rows = lax.broadcasted_iota(jnp.int32, (S, 1), 0)
        c = jnp.zeros_like(xv) + b_ref[...]
        for kk in range(CONV_K):
            c = c + w_ref[kk:kk + 1, :] * _shift_down(xv, CONV_K - 1 - kk, rows)
        o_ref[...] = c * _sigmoid(c)

    xs = pl.BlockSpec((S, CONV_BLK), lambda j, bb: (bb, j))
    return pl.pallas_call(
        body, grid=(CH // CONV_BLK, nseq), name="conv_fwd",
        in_specs=[xs, pl.BlockSpec((CONV_K, CONV_BLK), lambda j, bb: (0, j)), pl.BlockSpec((1, CONV_BLK), lambda j, bb: (0, j))],
        out_specs=xs, out_shape=jax.ShapeDtypeStruct((T, CH), F32),
        compiler_params=_cp("parallel", "parallel"),
    )(x, w, b)


def conv_bwd(x, w, b, dout, nseq):
    T, CH = x.shape
    S = T // nseq

    def body(x_ref, w_ref, b_ref, do_ref, dx_ref, dw_ref, db_ref):
        @pl.when(pl.program_id(1) == 0)
        def _():
            dw_ref[...] = jnp.zeros_like(dw_ref)
            db_ref[...] = jnp.zeros_like(db_ref)

        xv = x_ref[...]
        rows = lax.broadcasted_iota(jnp.int32, (S, 1), 0)
        c = jnp.zeros_like(xv) + b_ref[...]
        sh = [_shift_down(xv, CONV_K - 1 - kk, rows) for kk in range(CONV_K)]
        for kk in range(CONV_K):
            c = c + w_ref[kk:kk + 1, :] * sh[kk]
        sg = _sigmoid(c)
        dc = do_ref[...] * sg * (1.0 + c * (1.0 - sg))
        dx = jnp.zeros_like(xv)
        for kk in range(CONV_K):
            dx = dx + w_ref[kk:kk + 1, :] * _shift_up(dc, CONV_K - 1 - kk, rows)
            dw_ref[kk:kk + 1, :] += jnp.sum(dc * sh[kk], axis=0, keepdims=True)
        dx_ref[...] = dx.astype(dx_ref.dtype)
        db_ref[...] += jnp.sum(dc, axis=0, keepdims=True)

    xs = pl.BlockSpec((S, CONV_BLK), lambda j, bb: (bb, j))
    ws = pl.BlockSpec((CONV_K, CONV_BLK), lambda j, bb: (0, j))
    bs = pl.BlockSpec((1, CONV_BLK), lambda j, bb: (0, j))
    return pl.pallas_call(
        body, grid=(CH // CONV_BLK, nseq), name="conv_bwd",
        in_specs=[xs, ws, bs, xs], out_specs=[xs, ws, bs],
        out_shape=[jax.ShapeDtypeStruct((T, CH), BF16), jax.ShapeDtypeStruct((CONV_K, CH), F32),
                   jax.ShapeDtypeStruct((1, CH), F32)],
        compiler_params=_cp("parallel", "arbitrary"),
    )(x, w, b, dout)


def _dims(a, b, mode):
    M = a.shape[1] if mode[0] == "t" else a.shape[0]
    K = a.shape[0] if mode[0] == "t" else a.shape[1]
    N = b.shape[0] if mode[1] == "t" else b.shape[1]
    return M, K, N


def _tile(dim, prefs):
    for p in prefs:
        if dim % p == 0:
            return p
    return dim


def mm(groups, out_dtypes, name, tm=None, tn=None, tk=None, epi=None, extras=(), comm=None):
    a0, b0, m0 = groups[0][0]
    M, K0, N = _dims(a0, b0, m0)
    tm = tm or _tile(M, (1024, 512, 256, 128))
    tn = tn or _tile(N, (512, 256, 128))
    flat = [p for g in groups for p in g]
    nk = 1 if tk is None else K0 // tk
    in_specs, args = [], []
    for a, b, mode in flat:
        _, K, _ = _dims(a, b, mode)
        kb = K if tk is None else tk
        in_specs.append(pl.BlockSpec((kb, tm), lambda i, j, k: (k, i)) if mode[0] == "t"
                        else pl.BlockSpec((tm, kb), lambda i, j, k: (i, k)))
        in_specs.append(pl.BlockSpec((tn, kb), lambda i, j, k: (j, k)) if mode[1] == "t"
                        else pl.BlockSpec((kb, tn), lambda i, j, k: (k, j)))
        args += [a, b]
    for e in extras:
        in_specs.append(pl.BlockSpec((1, tn), lambda i, j, k: (0, j)) if e.shape[0] == 1 and M != 1
                        else pl.BlockSpec((tm, tn), lambda i, j, k: (i, j)))
        args.append(e)
    n_in = len(args)
    n_out = len(out_dtypes)
    ng = len(groups)
    sizes = [len(g) for g in groups]

    def body(*refs):
        ins, outs, accs = refs[:n_in], refs[n_in:n_in + n_out], refs[n_in + n_out:]
        kk = pl.program_id(2)
        vals, pos = [], 0
        for gi in range(ng):
            acc = None
            for _ in range(sizes[gi]):
                mode = flat[pos // 2][2]
                d = _dot(ins[pos][...], ins[pos + 1][...], mode)
                acc = d if acc is None else acc + d
                pos += 2
            vals.append(acc)
        ex = [r[...] for r in ins[2 * len(flat):]]

        def finish(accv):
            res = epi(accv, ex) if epi is not None else tuple(accv)
            for o, r in zip(outs, res):
                o[...] = r.astype(o.dtype)

        if nk == 1:
            finish(vals)
        else:
            @pl.when(kk == 0)
            def _():
                for ar, vv in zip(accs, vals):
                    ar[...] = vv

            @pl.when(kk > 0)
            def _():
                for ar, vv in zip(accs, vals):
                    ar[...] += vv

            @pl.when(kk == nk - 1)
            def _():
                finish([ar[...] for ar in accs])

    grid = (M // tm, N // tn, nk)
    out_specs = [pl.BlockSpec((tm, tn), lambda i, j, k: (i, j)) for _ in out_dtypes]
    out_shape = [jax.ShapeDtypeStruct((M, N), dt) for dt in out_dtypes]
    scratch = [pltpu.VMEM((tm, tn), F32) for _ in range(ng if nk > 1 else 0)]
    sem = ("parallel", "parallel", "arbitrary")
    if comm is not None:
        body = _attach(comm, body, n_in, n_out, *_grid_ends(grid))
        in_specs, args = in_specs + [HBM_SPEC] * len(comm.inputs), args + comm.inputs
        out_specs, out_shape = out_specs + [HBM_SPEC] * len(comm.out_shapes), out_shape + comm.out_shapes
        scratch, sem = scratch + comm.sems, ("arbitrary",) * 3
    return pl.pallas_call(body, grid=grid, name=name, in_specs=in_specs, out_specs=out_specs, out_shape=out_shape,
                          scratch_shapes=scratch, compiler_params=_cp(*sem))(*args)


def mm1(a, b, mode, out_dtype, name, **kw):
    return mm([[(a, b, mode)]], [out_dtype], name, **kw)[0]


ROW_BLK = 256


def rowwise(fn, rows, consts, outs, accs, name, tb=ROW_BLK):
    rows = [r if isinstance(r, tuple) else (r, r.shape[1], 0) for r in rows]
    T = rows[0][0].shape[0]
    tb = min(tb, T)
    n_r, n_c, n_o, n_a = len(rows), len(consts), len(outs), len(accs)

    def body(*refs):
        vals = [r[...] for r in refs[:n_r + n_c]]
        res = fn(*vals)
        o_refs = refs[n_r + n_c:n_r + n_c + n_o]
        a_refs = refs[n_r + n_c + n_o:]
        for o, r in zip(o_refs, res[:n_o]):
            o[...] = r.astype(o.dtype)
        if n_a:
            @pl.when(pl.program_id(0) == 0)
            def _():
                for ar in a_refs:
                    ar[...] = jnp.zeros_like(ar)
            for ar, r in zip(a_refs, res[n_o:]):
                ar[...] += r

    return pl.pallas_call(
        body, grid=(T // tb,), name=name,
        in_specs=[pl.BlockSpec((tb, w), functools.partial(lambda i, j: (i, j), j=j)) for _, w, j in rows]
        + [pl.BlockSpec(c.shape, lambda i: (0, 0)) for c in consts],
        out_specs=[pl.BlockSpec((tb, d), lambda i: (i, 0)) for d, _ in outs]
        + [pl.BlockSpec(s, lambda i: (0, 0)) for s in accs],
        out_shape=[jax.ShapeDtypeStruct((T, d), dt) for d, dt in outs]
        + [jax.ShapeDtypeStruct(s, F32) for s in accs],
        compiler_params=_cp("arbitrary" if n_a else "parallel"),
    )(*[r[0] for r in rows], *consts)


def _rms_stats(x):
    r = lax.rsqrt(jnp.mean(x * x, axis=-1, keepdims=True) + EPS)
    return r, x * r


def _rms_bwd(x, g, dy):
    r, xn = _rms_stats(x)
    dyg = dy * g
    dx = r * (dyg - xn * jnp.mean(dyg * xn, axis=-1, keepdims=True))
    return dx, jnp.sum(dy * xn, axis=0, keepdims=True)


def rms_fwd(x, g, name):
    return rowwise(lambda xv, gv: (_rms_stats(xv)[1] * gv,), [x], [g], [(x.shape[1], BF16)], [], name)[0]


def rms_bwd(x, g, dy, name, resid=None, dx_dtype=F32):
    def fn(*v):
        if resid is None:
            xv, dyv, gv = v
            dx, dg = _rms_bwd(xv, gv, dyv)
        else:
            xv, dyv, rv, gv = v
            dx, dg = _rms_bwd(xv, gv, dyv)
            dx = dx + rv
        return dx, dg
    rows = [x, dy] + ([] if resid is None else [resid])
    return rowwise(fn, rows, [g], [(x.shape[1], dx_dtype)], [(1, x.shape[1])], name)


def mm_resid(a, b, x, g, wgt, name):
    epi = lambda accs, ex: (accs[0], ex[0] + wgt * _rms_stats(accs[0])[1] * ex[1])
    return mm([[(a, b, "nn")]], [F32, F32], name, tm=min(512, a.shape[0]), tn=b.shape[1], epi=epi, extras=[x, g])


def resid_bwd(h, g, dy, wgt, name):
    def fn(hv, dyv, gv):
        dx, dg = _rms_bwd(hv, gv, dyv)
        return wgt * dx, wgt * dg
    return rowwise(fn, [h, dy], [g], [(h.shape[1], BF16)], [(1, h.shape[1])], name)


def _silu_parts(g):
    s = _sigmoid(g)
    return g * s, s * (1.0 + g * (1.0 - s))


def gated_norm_fwd(y, z, g, name):
    W = SSD_INNER // SSD_G

    def fn(yv, zv, gv):
        yg = yv * _silu_parts(zv)[0]
        return (jnp.concatenate([_rms_stats(yg[:, i * W:(i + 1) * W])[1] for i in range(SSD_G)], axis=1) * gv,)
    return rowwise(fn, [y, z], [g], [(SSD_INNER, BF16)], [], name)[0]


def gated_norm_bwd(y, z, dyn, g, name):
    W = SSD_INNER // SSD_G

    def fn(yv, zv, dv, gv):
        sil, dsil = _silu_parts(zv)
        yg = yv * sil
        parts = [_rms_bwd(yg[:, i * W:(i + 1) * W], gv[:, i * W:(i + 1) * W], dv[:, i * W:(i + 1) * W]) for i in range(SSD_G)]
        dyg = jnp.concatenate([p[0] for p in parts], axis=1)
        dg = jnp.concatenate([p[1] for p in parts], axis=1)
        return dyg * sil, dyg * yv * dsil, dg
    return rowwise(fn, [y, z, dyn], [g], [(SSD_INNER, F32), (SSD_INNER, BF16)], [(1, SSD_INNER)], name)


def merge_fwd(gl, ys, ym, gb, name):
    def fn(glv, ysv, ymv, gbv):
        gt = _sigmoid(glv + gbv)
        return (gt[:, :D] * ysv + gt[:, D:] * ymv,)
    return rowwise(fn, [gl, ys, ym], [gb], [(D, BF16)], [], name)[0]


def merge_bwd(gl, ys, ym, dm, gb, name):
    def fn(glv, ysv, ymv, dmv, gbv):
        gt = _sigmoid(glv + gbv)
        gs, gm = gt[:, :D], gt[:, D:]
        dgl = jnp.concatenate([dmv * ysv * gs * (1.0 - gs), dmv * ymv * gm * (1.0 - gm)], axis=1)
        return dmv * gs, dmv * gm, dgl, jnp.sum(dgl, axis=0, keepdims=True)
    return rowwise(fn, [gl, ys, ym, dm], [gb], [(D, BF16), (D, BF16), (2 * D, BF16)], [(1, 2 * D)], name)


def loss_head(y, tgt, name):
    def fn(yv, tv):
        d = yv - tv
        part = 0.5 * jnp.sum(jnp.sum(d * d, axis=1, keepdims=True), axis=0, keepdims=True) / D
        return d / D, jnp.broadcast_to(part, (1, 128))
    return rowwise(fn, [y, tgt], [], [(D, F32)], [(1, 128)], name)


def adamw(w, g, m, v, name):
    R, C = w.shape
    tb = _tile(R, (256, 128, 64, 32, 16, 8))

    def fn(wv, gv, mv, vv):
        mn = B1 * mv + (1.0 - B1) * gv
        vn = B2 * vv + (1.0 - B2) * (gv * gv)
        mh = mn / (1.0 - B1 ** STEP)
        vh = vn / (1.0 - B2 ** STEP)
        return -LR * (mh / (jnp.sqrt(vh) + AEPS) + WD * wv), mn, vn
    return rowwise(fn, [w, g, m, v], [], [(C, F32)] * 3, [], name, tb=tb)


def _me():
    return lax.axis_index("x"), lax.axis_index("y"), lax.axis_index("c")


def _dev_index():
    x, y, c = _me()
    return 4 * x + 2 * y + c


HBM_SPEC = pl.BlockSpec(memory_space=pl.ANY)


class GatherComm:
    def __init__(self, shards):
        self.inputs = list(shards)
        n = len(shards)
        self.out_shapes = [jax.ShapeDtypeStruct((N_DEV,) + s.shape, s.dtype) for s in shards]
        self.sems = [pltpu.SemaphoreType.DMA((7 * n,)), pltpu.SemaphoreType.DMA((7 * n,)), pltpu.SemaphoreType.DMA((n,))]

    def _plan(self, x_refs, out_refs, sems):
        send_sems, recv_sems, local_sems = sems
        n = len(x_refs)
        x, y, c = _me()
        me, sibling = (x, y, c), (x, y, 1 - c)
        chips = [(1 - x, y), (x, 1 - y), (1 - x, 1 - y)]

        def slot(i, px, py, pc):
            return out_refs[i].at[4 * px + 2 * py + pc]

        def copy(i, k, block, to, src=None):
            return pltpu.make_async_remote_copy(
                src_ref=slot(i, *block) if src is None else src, dst_ref=slot(i, *block),
                send_sem=send_sems.at[7 * i + k], recv_sem=recv_sems.at[7 * i + k], device_id=to, device_id_type=MESH)

        mine = [pltpu.make_async_copy(x_refs[i], slot(i, *me), local_sems.at[i]) for i in range(n)]
        first = []
        for i in range(n):
            first.append(copy(i, 0, me, sibling, src=x_refs[i]))
            first += [copy(i, 1 + j, me, (*chip, c), src=x_refs[i]) for j, chip in enumerate(chips)]
        passed = [[copy(i, 4 + j, (*chip, c), sibling) for j, chip in enumerate(chips)] for i in range(n)]
        from_ici = [[copy(i, 1 + j, (*chip, c), me) for j, chip in enumerate(chips)] for i in range(n)]
        from_sib = [[copy(i, 0, sibling, me)] + [copy(i, 4 + j, (*chip, 1 - c), me) for j, chip in enumerate(chips)] for i in range(n)]
        return mine, first, passed, from_ici, from_sib

    def start(self, x_refs, out_refs, sems):
        mine, first, _, _, _ = self._plan(x_refs, out_refs, sems)
        for cp in mine + first:
            cp.start()

    def finish(self, x_refs, out_refs, sems):
        mine, first, passed, from_ici, from_sib = self._plan(x_refs, out_refs, sems)
        for i in range(len(x_refs)):
            for arrival, forward in zip(from_ici[i], passed[i]):
                arrival.wait_recv()
                forward.start()
        for row in from_sib:
            for arrival in row:
                arrival.wait_recv()
        for cp in first + [cp for row in passed for cp in row]:
            cp.wait_send()
        for cp in mine:
            cp.wait()


def run_comm(comm, name):
    n_in, n_out = len(comm.inputs), len(comm.out_shapes)

    def body(*refs):
        ins, outs, sems = refs[:n_in], refs[n_in:n_in + n_out], refs[n_in + n_out:]
        comm.start(ins, outs, sems)
        comm.finish(ins, outs, sems)

    return pl.pallas_call(body, name=name, out_shape=comm.out_shapes, in_specs=[HBM_SPEC] * n_in,
                          out_specs=[HBM_SPEC] * n_out, scratch_shapes=comm.sems)(*comm.inputs)


def _attach(comm, body, n_in, n_out, first, last):
    if comm is None:
        return body
    ci, co, cs = len(comm.inputs), len(comm.out_shapes), len(comm.sems)

    def wrapped(*refs):
        h_in, c_in = refs[:n_in], refs[n_in:n_in + ci]
        h_out, c_out = refs[n_in + ci:n_in + ci + n_out], refs[n_in + ci + n_out:n_in + ci + n_out + co]
        rest = refs[n_in + ci + n_out + co:]
        h_scr, c_sem = rest[:len(rest) - cs], rest[len(rest) - cs:]

        @pl.when(first())
        def _():
            comm.start(c_in, c_out, c_sem)

        body(*h_in, *h_out, *h_scr)

        @pl.when(last())
        def _():
            comm.finish(c_in, c_out, c_sem)

    return wrapped


def _grid_ends(grid):
    first = lambda: functools.reduce(lambda a, b: a & b, [pl.program_id(i) == 0 for i in range(len(grid))])
    last = lambda: functools.reduce(lambda a, b: a & b, [pl.program_id(i) == g - 1 for i, g in enumerate(grid)])
    return first, last


def _call_with_comm(body, grid, name, in_specs, args, out_specs, out_shape, comm, scratch=(), sem=None):
    sem = sem or ("parallel",) * len(grid)
    scratch = list(scratch)
    if comm is not None:
        body = _attach(comm, body, len(args), len(out_shape), *_grid_ends(grid))
        in_specs, args = in_specs + [HBM_SPEC] * len(comm.inputs), args + comm.inputs
        out_specs, out_shape = out_specs + [HBM_SPEC] * len(comm.out_shapes), out_shape + comm.out_shapes
        scratch, sem = scratch + comm.sems, ("arbitrary",) * len(grid)
    return pl.pallas_call(body, grid=grid, name=name, in_specs=in_specs, out_specs=out_specs, out_shape=out_shape,
                          scratch_shapes=scratch, compiler_params=_cp(*sem))(*args)


class ScatterComm:
    def __init__(self, groups):
        self.sizes = [len(g) for g in groups]
        self.rows = [[pc.shape[1] for pc in g] for g in groups]
        ng = len(groups)
        self.inputs = [pc for g in groups for pc in g]
        self.out_shapes = [jax.ShapeDtypeStruct((N_DEV, sum(self.rows[gi]), g[0].shape[2]), g[0].dtype) for gi, g in enumerate(groups)]
        self.sems = [pltpu.SemaphoreType.DMA((7 * ng,)), pltpu.SemaphoreType.DMA((7 * ng,)), pltpu.SemaphoreType.DMA((ng,))]

    def _peers(self):
        x, y, c = _me()
        out = []
        for k in range(1, N_DEV):
            px = 1 - x if k & 4 else x
            py = 1 - y if k & 2 else y
            pc = 1 - c if k & 1 else c
            out.append((k, 4 * px + 2 * py + pc, dict(device_id=(px, py, pc), device_id_type=MESH)))
        return 4 * x + 2 * y + c, out

    def start(self, ins, outs, sems):
        send_sems, recv_sems, local_sems = sems
        me, peers = self._peers()
        pos = 0
        for gi, size in enumerate(self.sizes):
            for i, pc in enumerate(ins[pos:pos + size]):
                dst = outs[gi].at[me, pl.ds(sum(self.rows[gi][:i]), self.rows[gi][i])]
                pltpu.make_async_copy(pc.at[me], dst, local_sems.at[gi]).start()
                for k, peer, kw in peers:
                    pltpu.make_async_remote_copy(src_ref=pc.at[peer], dst_ref=dst, send_sem=send_sems.at[7 * gi + k - 1],
                                                 recv_sem=recv_sems.at[7 * gi + k - 1], **kw).start()
            pos += size

    def finish(self, ins, outs, sems):
        send_sems, recv_sems, local_sems = sems
        me, peers = self._peers()
        whole = [pltpu.make_async_remote_copy(src_ref=outs[gi].at[peer], dst_ref=outs[gi].at[peer],
                                              send_sem=send_sems.at[7 * gi + k - 1], recv_sem=recv_sems.at[7 * gi + k - 1], **kw)
                 for gi in range(len(self.sizes)) for k, peer, kw in peers]
        for cp in whole:
            cp.wait_recv()
        for cp in whole:
            cp.wait_send()
        for gi in range(len(self.sizes)):
            pltpu.make_async_copy(outs[gi].at[me], outs[gi].at[me], local_sems.at[gi]).wait()


def sum_slots(recv, name, tr):
    n, R, C = recv.shape

    def body(r_ref, o_ref):
        acc = r_ref[0].astype(F32)
        for s in range(1, n):
            acc = acc + r_ref[s].astype(F32)
        o_ref[...] = acc

    return pl.pallas_call(
        body, grid=(R // tr,), name=name,
        in_specs=[pl.BlockSpec((n, tr, C), lambda i: (0, i, 0))], out_specs=pl.BlockSpec((tr, C), lambda i: (i, 0)),
        out_shape=jax.ShapeDtypeStruct((R, C), F32), compiler_params=_cp("parallel"),
    )(recv)


PACK_W, FLAT_W = 1024, 128
MAIN = [
    ("ffn1_w_gate", "col"), ("ffn1_w_up", "col"), ("ffn1_w_down", "row"),
    ("ffn2_w_gate", "col"), ("ffn2_w_up", "col"), ("ffn2_w_down", "row"),
    ("w_ssd_proj", "row"), ("w_mla_proj", "row"), ("w_out", "row"),
    ("w_xq", "row"), ("w_xk", "row"), ("w_xv", "row"), ("w_xo", "row"),
    ("w_uk", "col"), ("w_uv", "col"),
]
FLAT = [("w_in", "col"), ("w_uq", "col")]
BIG = MAIN + FLAT
SMALL = ["ffn1_pre_g", "ffn1_post_g", "mix_pre_g", "conv_b", "dt_bias", "a_log", "d_skip", "ssd_norm_g", "q_norm_g",
         "kv_norm_g", "gate_bias", "mix_post_g", "xa_pre_g", "mem_norm_g", "xa_post_g", "ffn2_pre_g", "ffn2_post_g"]
WEIGHTS = ['ffn1_pre_g', 'ffn1_w_gate', 'ffn1_w_up', 'ffn1_w_down', 'ffn1_post_g', 'mix_pre_g', 'w_in', 'conv_w', 'conv_b',
           'dt_bias', 'a_log', 'd_skip', 'ssd_norm_g', 'w_ssd_proj', 'q_norm_g', 'w_uq', 'kv_norm_g', 'w_uk', 'w_uv',
           'w_mla_proj', 'gate_bias', 'w_out', 'mix_post_g', 'xa_pre_g', 'mem_norm_g', 'w_xq', 'w_xk', 'w_xv', 'w_xo',
           'xa_post_g', 'ffn2_pre_g', 'ffn2_w_gate', 'ffn2_w_up', 'ffn2_w_down', 'ffn2_post_g']


def _pack_rows(w, kind, width):
    m = w[0].T if kind == "col" else w[0]
    return m.reshape(-1, width)


KIND = dict(BIG)
GATHER_PLAN = {
    "first": (["ffn1_w_gate", "ffn1_w_up", "ffn1_w_down"], []),
    "ffn1_gate_up": (["w_ssd_proj", "w_mla_proj", "w_out", "w_uk", "w_uv"], ["w_in", "w_uq"]),
    "attn_fwd": (["w_xq", "w_xk", "w_xv", "w_xo", "ffn2_w_gate", "ffn2_w_up", "ffn2_w_down"], []),
}
SCATTER_PLAN = {
    "attn_bwd": (["ffn2_w_gate", "ffn2_w_up", "ffn2_w_down", "w_xq", "w_xk", "w_xv", "w_xo"], []),
    "ssd_bwd": (["w_ssd_proj", "w_mla_proj", "w_out", "w_uk", "w_uv"], ["w_uq"]),
    "in_bwd": ([], ["w_in#0"]),
    "ffn1:down_bwd": ([], ["w_in#1"]),
    "ffn1:dwg": (["ffn1_w_down"], []),
    "ffn1:dwu": (["ffn1_w_gate"], []),
    "ffn1:gate_up_bwd": (["ffn1_w_up"], []),
}
PARTS = {"w_in#0": ("w_in", 0, 2656), "w_in#1": ("w_in", 2656, 5296)}


class Stage:
    def __init__(self, w):
        self.w = w
        self.width = {n: PACK_W if (n, k) in MAIN else FLAT_W for n, k in BIG}
        self.nrows = {n: math.prod(w[n].shape) // self.width[n] for n, _ in BIG}
        self.recv = {}

    def _shards(self, tag):
        names_main, names_flat = GATHER_PLAN[tag]
        pack = lambda n: _pack_rows(self.w[n], KIND[n], self.width[n]).astype(BF16)
        shards = []
        if names_main:
            shards.append(jnp.concatenate([pack(n) for n in names_main], axis=0))
        if names_flat:
            bits = lax.bitcast_convert_type(self.w["conv_w"][0], BF16).reshape(-1, FLAT_W)
            shards.append(_pad_rows(jnp.concatenate([pack(n) for n in names_flat] + [bits], axis=0), 16))
        return shards

    def gather(self, tag):
        return GatherComm(self._shards(tag))

    def gathered(self, tag, outs, W, p):
        names_main, names_flat = GATHER_PLAN[tag]
        outs = list(outs)
        for names in (names_main, names_flat):
            if not names:
                continue
            buf, r0 = outs.pop(0), 0
            for n in names:
                K = self.w[n].shape[1] if KIND[n] == "col" else PACK_W
                W[n] = buf[:, r0:r0 + self.nrows[n]].reshape(-1, K)
                r0 += self.nrows[n]
            if names is names_flat:
                cw = self.w["conv_w"]
                nbits = 2 * math.prod(cw.shape) // FLAT_W
                bits = buf[:, r0:r0 + nbits].reshape((N_DEV,) + cw.shape[1:] + (2,))
                p["conv_w"] = lax.bitcast_convert_type(bits, F32).transpose(1, 0, 2).reshape(cw.shape[1], -1)

    def scatter(self, tag, gw):
        if tag not in SCATTER_PLAN:
            return None
        def piece(n):
            if n in PARTS:
                base, r0, r1 = PARTS[n]
                return gw[base].reshape(N_DEV, self.nrows[base], self.width[base])[:, r0:r1]
            return gw[n].reshape(N_DEV, self.nrows[n], self.width[n])
        return ScatterComm([[piece(n) for n in names] for names in SCATTER_PLAN[tag] if names])

    def scattered(self, tag, outs):
        if tag in SCATTER_PLAN:
            self.recv[tag] = outs


def _pad_rows(a, mult):
    r = (-a.shape[0]) % mult
    return a if r == 0 else jnp.concatenate([a, jnp.zeros((r,) + a.shape[1:], a.dtype)], axis=0)


def _pack_small(vals, loss_row=None, conv_w=None):
    rows = []
    for v in vals:
        f = v.reshape(-1)
        f = jnp.concatenate([f, jnp.zeros(((-f.shape[0]) % 128,), F32)])
        rows.append(f.reshape(-1, 128))
    if conv_w is not None:
        rows.append(conv_w.reshape(-1, 128))
    if loss_row is not None:
        rows.append(loss_row)
    return _pad_rows(jnp.concatenate(rows, axis=0), 8)


def _unpack_small(buf, shapes):
    out, r = [], 0
    for shp in shapes:
        n = math.prod(shp)
        nr = -(-n // 128)
        out.append(buf[r:r + nr].reshape(-1)[:n].reshape(shp))
        r += nr
    return out, r


def _tn(a, b, name, out_dtype=BF16, comm=None):
    M, N = a.shape[1], b.shape[1]
    T = a.shape[0]
    tm = M if M <= 1536 else M // 2
    tk = 512 if T % 512 == 0 and T > 512 else None
    res = mm([[(a, b, "tn")]], [out_dtype], name, tm=tm, tn=N, tk=tk, comm=comm)
    return res[0] if comm is None else (res[0], res[1:])


class NoStage:
    def gather(self, tag):
        return None

    def gathered(self, tag, outs, W, p):
        pass

    def scatter(self, tag, gw):
        return None

    def scattered(self, tag, outs):
        pass


def _ffn_fwd(x, gpre, gpost, wg_t, wu_t, wd, tag, comm=None):
    h = rms_fwd(x, gpre, tag + "_pre")
    def swi(accs, ex):
        sil, dsil = _silu_parts(accs[0])
        return sil, accs[1] * dsil, sil * accs[1]
    res = mm([[(h, wg_t, "nt")], [(h, wu_t, "nt")]], [F32, F32, BF16], tag + "_gate_up", tn=256, epi=swi, comm=comm)
    G, U, A = res[:3]
    H, y = mm_resid(A, wd, x, gpost, FFN_RES, tag + "_down")
    return y, (x, h, G, U, A, H), res[3:]


def _ffn_bwd(dy, saved, gpre, gpost, wg_t, wu_t, wd, tag, stage, gw):
    x, h, G, U, A, H = saved
    dH, dgpost = resid_bwd(H, gpost, dy, FFN_RES, tag + "_post_bwd")

    def dswi(accs, ex):
        return accs[0] * ex[1], accs[0] * ex[0]

    def hosted(where, call):
        comm = stage.scatter(tag + ":" + where, gw)
        res = call(comm)
        if comm is None:
            return res
        stage.scattered(tag + ":" + where, res[1])
        return res[0]

    res = hosted("down_bwd", lambda comm: (lambda r: r if comm is None else (r[:2], r[2:]))(
        mm([[(dH, wd, "nt")]], [BF16, BF16], tag + "_down_bwd", tn=256, epi=dswi, extras=[G, U], comm=comm)))
    dG, dU = res
    gw[tag + "_w_down"] = _tn(A, dH, tag + "_dwd")
    gw[tag + "_w_gate"] = hosted("dwg", lambda comm: _tn(dG, h, tag + "_dwg", comm=comm))
    gw[tag + "_w_up"] = hosted("dwu", lambda comm: _tn(dU, h, tag + "_dwu", comm=comm))
    dh = hosted("gate_up_bwd", lambda comm: (lambda r: r[0] if comm is None else (r[0], r[1:]))(
        mm([[(dG, wg_t, "nn"), (dU, wu_t, "nn")]], [F32], tag + "_gate_up_bwd", tm=512, comm=comm)))
    dx, dgpre = rms_bwd(x, gpre, dh, tag + "_pre_bwd", resid=dy)
    return dx, dgpre, dgpost


def _rope_tables(positions):
    inv = ROPE_THETA ** (-jnp.arange(0, ROPE, 2, dtype=F32) / ROPE)
    ang = positions.astype(F32).reshape(-1)[:, None] * inv
    return jnp.cos(ang), jnp.sin(ang)


def _local_step(x, mem, positions, tgt, W, p, stage=None):
    stage = stage or NoStage()
    nseq = x.shape[0]
    T = nseq * x.shape[1]
    x0 = x.reshape(T, D)
    mem2 = mem.reshape(-1, D)
    cos, sin = _rope_tables(positions)

    x1, ffn1, arrived = _ffn_fwd(x0, p["ffn1_pre_g"], p["ffn1_post_g"], W["ffn1_w_gate"], W["ffn1_w_up"], W["ffn1_w_down"],
                                 "ffn1", comm=stage.gather("ffn1_gate_up"))
    stage.gathered("ffn1_gate_up", arrived, W, p)

    w_in_t = W["w_in"]
    bounds = [0]
    for n in (SSD_INNER, CONV_CH, SSD_H, QR, KVR, ROPE, 2 * D):
        bounds.append(bounds[-1] + n)
    wt_z, wt_xbc, wt_dt, wt_q, wt_kv, wt_kr, wt_gate = [w_in_t[bounds[i]:bounds[i + 1]] for i in range(7)]
    wt_dt, wt_kr = _pad_rows(wt_dt, SLOT), _pad_rows(wt_kr, SLOT)
    wt_dtkr = jnp.concatenate([wt_dt, wt_kr], axis=0)
    hm = rms_fwd(x1, p["mix_pre_g"], "mix_pre")
    z = mm1(hm, wt_z, "nt", F32, "in_z")
    xbc = mm1(hm, wt_xbc, "nt", F32, "in_xbc")
    q_c = mm1(hm, wt_q, "nt", F32, "in_q", tn=QR)
    kv_c = mm1(hm, wt_kv, "nt", F32, "in_kv")
    dtkr = mm1(hm, wt_dtkr, "nt", F32, "in_dtkr")
    gl = mm1(hm, wt_gate, "nt", F32, "in_gate")

    xbc_act = conv_fwd(xbc, p["conv_w"], p["conv_b"], nseq)
    y_ssd_core, prev = ssd_fwd(xbc_act, dtkr, p["dt_bias"], p["a_log"], p["d_skip"], nseq)
    yn = gated_norm_fwd(y_ssd_core, z, p["ssd_norm_g"], "ssd_norm")
    y_ssd = mm1(yn, W["w_ssd_proj"], "nn", F32, "ssd_proj")

    slot_rows = lambda wt, per: jnp.pad(wt.reshape(MLA_H, per, -1), ((0, 0), (0, SLOT - per), (0, 0))).reshape(MLA_H * SLOT, -1)
    wq_s, wk_s, wv_s = slot_rows(W["w_uq"], QK), slot_rows(W["w_uk"], NOPE), slot_rows(W["w_uv"], VD)
    wo_s = slot_rows(W["w_mla_proj"], VD)
    qn = rms_fwd(q_c, p["q_norm_g"], "q_norm")
    q_s = mm1(qn, wq_s, "nt", F32, "uq")
    kvn = rms_fwd(kv_c, p["kv_norm_g"], "kv_norm")
    kn_s = mm1(kvn, wk_s, "nt", BF16, "uk")
    v_s = mm1(kvn, wv_s, "nt", BF16, "uv")
    cos16, sin16 = cos, sin
    Qc, Kc = rope_slot_fwd(q_s, kn_s, dtkr, cos16, sin16, "rope")
    o_s, lse, *arrived = attn_slot_fwd(Qc, Kc, v_s, nseq, comm=stage.gather("attn_fwd"))
    stage.gathered("attn_fwd", arrived, W, p)
    y_mla = mm1(o_s, wo_s, "nn", F32, "mla_proj")

    merged = merge_fwd(gl, y_ssd, y_mla, p["gate_bias"], "merge")
    hmix, x2 = mm_resid(merged, W["w_out"], x1, p["mix_post_g"], 1.0, "mix_out")

    hq = rms_fwd(x2, p["xa_pre_g"], "xa_pre")
    mn = rms_fwd(mem2, p["mem_norm_g"], "mem_norm")
    xq = mm1(hq, W["w_xq"], "nn", BF16, "xq")
    xk = mm1(mn, W["w_xk"], "nn", BF16, "xk")
    xv = mm1(mn, W["w_xv"], "nn", BF16, "xv")
    xo = xattn_fwd(xq, xk, xv, nseq)
    ho, x3 = mm_resid(xo, W["w_xo"], x2, p["xa_post_g"], 1.0, "xo")

    x4, ffn2, _ = _ffn_fwd(x3, p["ffn2_pre_g"], p["ffn2_post_g"], W["ffn2_w_gate"], W["ffn2_w_up"], W["ffn2_w_down"], "ffn2")
    dx4, loss_row = loss_head(x4, tgt.reshape(T, D), "loss")

    gw, gs = {}, {}
    dx3, gs["ffn2_pre_g"], gs["ffn2_post_g"] = _ffn_bwd(
        dx4, ffn2, p["ffn2_pre_g"], p["ffn2_post_g"], W["ffn2_w_gate"], W["ffn2_w_up"], W["ffn2_w_down"], "ffn2", stage, gw)

    dho, gs["xa_post_g"] = resid_bwd(ho, p["xa_post_g"], dx3, 1.0, "xa_post_bwd")
    dxo = mm1(dho, W["w_xo"], "nt", BF16, "xo_bwd")
    gw["w_xo"] = _tn(xo, dho, "d_w_xo")
    dxq, dxk, dxv = xattn_bwd(xq, xk, xv, dxo, nseq)
    dhq = mm1(dxq, W["w_xq"], "nt", F32, "xq_bwd")
    gw["w_xq"] = _tn(hq, dxq, "d_w_xq")
    dmn = mm([[(dxk, W["w_xk"], "nt"), (dxv, W["w_xv"], "nt")]], [F32], "xkv_bwd")[0]
    gw["w_xk"] = _tn(mn, dxk, "d_w_xk")
    gw["w_xv"] = _tn(mn, dxv, "d_w_xv")
    _, gs["mem_norm_g"] = rms_bwd(mem2, p["mem_norm_g"], dmn, "mem_norm_bwd", dx_dtype=BF16)
    dx2, gs["xa_pre_g"] = rms_bwd(x2, p["xa_pre_g"], dhq, "xa_pre_bwd", resid=dx3)

    dhmix, gs["mix_post_g"] = resid_bwd(hmix, p["mix_post_g"], dx2, 1.0, "mix_post_bwd")
    dmerged = mm1(dhmix, W["w_out"], "nt", F32, "mix_out_bwd")
    gw["w_out"] = _tn(merged, dhmix, "d_w_out")
    dys, dym, dgl, gs["gate_bias"] = merge_bwd(gl, y_ssd, y_mla, dmerged, p["gate_bias"], "merge_bwd")

    unslot = lambda g, per: g.reshape(MLA_H, SLOT, -1)[:, :per].reshape(MLA_H * per, -1)
    do_s = mm1(dym, wo_s, "nt", BF16, "mla_proj_bwd")
    gw["w_mla_proj"] = unslot(_tn(o_s, dym, "d_w_mla_proj"), VD)
    dQc, dKc, dv_s, *sent = attn_slot_bwd(Qc, Kc, v_s, o_s, lse, do_s, nseq, comm=stage.scatter("attn_bwd", gw))
    stage.scattered("attn_bwd", sent)
    dq_s, dkn_s, dkr = rope_slot_bwd(dQc, dKc, cos16, sin16, "rope_bwd")
    dqn = mm1(dq_s, wq_s, "nn", F32, "uq_bwd", tn=QR)
    gw["w_uq"] = unslot(_tn(dq_s, qn, "d_w_uq"), QK)
    dq_c, gs["q_norm_g"] = rms_bwd(q_c, p["q_norm_g"], dqn, "q_norm_bwd", dx_dtype=BF16)
    dkvn = mm([[(dkn_s, wk_s, "nn"), (dv_s, wv_s, "nn")]], [F32], "ukv_bwd")[0]
    gw["w_uk"] = unslot(_tn(dkn_s, kvn, "d_w_uk"), NOPE)
    gw["w_uv"] = unslot(_tn(dv_s, kvn, "d_w_uv"), VD)
    dkv_c, gs["kv_norm_g"] = rms_bwd(kv_c, p["kv_norm_g"], dkvn, "kv_norm_bwd", dx_dtype=BF16)

    dyn = mm1(dys, W["w_ssd_proj"], "nt", F32, "ssd_proj_bwd")
    gw["w_ssd_proj"] = _tn(yn, dys, "d_w_ssd_proj")
    dyc, dz, gs["ssd_norm_g"] = gated_norm_bwd(y_ssd_core, z, dyn, p["ssd_norm_g"], "ssd_norm_bwd")
    dxbc_act, ddtr, gs["dt_bias"], gs["a_log"], gs["d_skip"], *sent = ssd_bwd(
        xbc_act, dtkr, p["dt_bias"], p["a_log"], p["d_skip"], prev, dyc, nseq, comm=stage.scatter("ssd_bwd", gw))
    stage.scattered("ssd_bwd", sent)
    dxbc, gs["conv_w"], gs["conv_b"] = conv_bwd(xbc, p["conv_w"], p["conv_b"], dxbc_act, nseq)

    gw["w_in"] = jnp.concatenate([_tn(dz, hm, "d_w_in_z"), _tn(dxbc, hm, "d_w_in_xbc"), _tn(ddtr, hm, "d_w_in_dt")[:SSD_H],
                                  _tn(dq_c, hm, "d_w_in_q"), _tn(dkv_c, hm, "d_w_in_kv"), _tn(dkr, hm, "d_w_in_kr")[:ROPE],
                                  _tn(dgl, hm, "d_w_in_gate")], axis=0)
    dhm, *sent = mm([[(dz, wt_z, "nn"), (dxbc, wt_xbc, "nn"), (ddtr, wt_dt, "nn"), (dq_c, wt_q, "nn"), (dkv_c, wt_kv, "nn"),
                      (dkr, wt_kr, "nn"), (dgl, wt_gate, "nn")]], [F32], "in_bwd", tm=512, comm=stage.scatter("in_bwd", gw))
    stage.scattered("in_bwd", sent)
    dx1, gs["mix_pre_g"] = rms_bwd(x1, p["mix_pre_g"], dhm, "mix_pre_bwd", resid=dx2)

    dx0, gs["ffn1_pre_g"], gs["ffn1_post_g"] = _ffn_bwd(
        dx1, ffn1, p["ffn1_pre_g"], p["ffn1_post_g"], W["ffn1_w_gate"], W["ffn1_w_up"], W["ffn1_w_down"], "ffn1", stage, gw)
    return loss_row, dx0.reshape(x.shape), gw, gs


def kernel(x, mem, positions, ffn1_pre_g, ffn1_w_gate, ffn1_w_up, ffn1_w_down, ffn1_post_g, mix_pre_g, w_in, conv_w, conv_b, dt_bias, a_log, d_skip, ssd_norm_g, w_ssd_proj, q_norm_g, w_uq, kv_norm_g, w_uk, w_uv, w_mla_proj, gate_bias, w_out, mix_post_g, xa_pre_g, mem_norm_g, w_xq, w_xk, w_xv, w_xo, xa_post_g, ffn2_pre_g, ffn2_w_gate, ffn2_w_up, ffn2_w_down, ffn2_post_g, loss_target, m_ffn1_pre_g, m_ffn1_w_gate, m_ffn1_w_up, m_ffn1_w_down, m_ffn1_post_g, m_mix_pre_g, m_w_in, m_conv_w, m_conv_b, m_dt_bias, m_a_log, m_d_skip, m_ssd_norm_g, m_w_ssd_proj, m_q_norm_g, m_w_uq, m_kv_norm_g, m_w_uk, m_w_uv, m_w_mla_proj, m_gate_bias, m_w_out, m_mix_post_g, m_xa_pre_g, m_mem_norm_g, m_w_xq, m_w_xk, m_w_xv, m_w_xo, m_xa_post_g, m_ffn2_pre_g, m_ffn2_w_gate, m_ffn2_w_up, m_ffn2_w_down, m_ffn2_post_g, v_ffn1_pre_g, v_ffn1_w_gate, v_ffn1_w_up, v_ffn1_w_down, v_ffn1_post_g, v_mix_pre_g, v_w_in, v_conv_w, v_conv_b, v_dt_bias, v_a_log, v_d_skip, v_ssd_norm_g, v_w_ssd_proj, v_q_norm_g, v_w_uq, v_kv_norm_g, v_w_uk, v_w_uv, v_w_mla_proj, v_gate_bias, v_w_out, v_mix_post_g, v_xa_pre_g, v_mem_norm_g, v_w_xq, v_w_xk, v_w_xv, v_w_xo, v_xa_post_g, v_ffn2_pre_g, v_ffn2_w_gate, v_ffn2_w_up, v_ffn2_w_down, v_ffn2_post_g):
    a = dict(locals())
    w = {n: a[n] for n in WEIGHTS}
    m = {n: a["m_" + n] for n in WEIGHTS}
    v = {n: a["v_" + n] for n in WEIGHTS}

    stage = Stage(w)
    W, p = {}, {n: w[n] for n in SMALL}
    stage.gathered("first", run_comm(stage.gather("first"), "allgather_first"), W, p)

    loss_row, grad_x, gw, gs = _local_step(x, mem, positions, loss_target, W, p, stage)

    sm = _pack_small([gs[n] for n in SMALL], loss_row=loss_row, conv_w=gs["conv_w"])
    srecv, = run_comm(ScatterComm([[jnp.broadcast_to(sm[None], (N_DEV,) + sm.shape)]]), "exchange_small")
    s_rows = sum_slots(srecv, "sum_small", tr=sm.shape[0])
    summed = {}
    for tag, (names_main, names_flat) in SCATTER_PLAN.items():
        bufs = list(stage.recv[tag])
        for names, whole in ((names_main, False), (names_flat, True)):
            if not names:
                continue
            buf = bufs.pop(0)
            rows = buf.shape[1]
            g_rows = sum_slots(buf, "sum_" + tag.replace(":", "_") + ("_flat" if whole else ""),
                               tr=rows if whole else _tile(rows, (256, 224, 176, 128, 64, 32, 16)))
            r0 = 0
            for n in names:
                nr = PARTS[n][2] - PARTS[n][1] if n in PARTS else stage.nrows[n]
                summed[n] = g_rows[r0:r0 + nr]
                r0 += nr
    for base in {b for b, _, _ in PARTS.values()}:
        summed[base] = jnp.concatenate([summed[pn] for pn in sorted(PARTS) if PARTS[pn][0] == base], axis=0)
    grads = {}
    for n, kind in BIG:
        blk = summed[n]
        grads[n] = (blk.reshape(w[n].shape[2], w[n].shape[1]).T if kind == "col" else blk)[None]
    conv_w_full = p["conv_w"]
    small_g, r1 = _unpack_small(s_rows, [w[n].shape for n in SMALL])
    for n, g in zip(SMALL, small_g):
        grads[n] = g
    ncw = math.prod(conv_w_full.shape) // 128
    cw_grad_full = s_rows[r1:r1 + ncw].reshape(conv_w_full.shape)
    wsh = conv_w.shape[2]
    grads["conv_w"] = lax.dynamic_slice_in_dim(cw_grad_full, _dev_index() * wsh, wsh, axis=1)[None]
    loss = s_rows[r1 + ncw, 0]

    delta, new_m, new_v = {}, {}, {}
    for n, _ in BIG + [("conv_w", "col")]:
        shp = w[n].shape
        d_, m_, v_ = adamw(w[n][0], grads[n][0], m[n][0], v[n][0], "adamw_" + n)
        delta[n], new_m[n], new_v[n] = d_.reshape(shp), m_.reshape(shp), v_.reshape(shp)
    sp = [_pack_small([t[n] for n in SMALL]) for t in (w, grads, m, v)]
    outs = adamw(sp[0], sp[1], sp[2], sp[3], "adamw_small")
    for t, buf in zip((delta, new_m, new_v), outs):
        vals, _ = _unpack_small(buf, [w[n].shape for n in SMALL])
        for n, val in zip(SMALL, vals):
            t[n] = val
    return (loss, grad_x, *[grads[n] for n in WEIGHTS], *[delta[n] for n in WEIGHTS],
            *[new_m[n] for n in WEIGHTS], *[new_v[n] for n in WEIGHTS])
```

```python
import functools
import math

import jax
import jax.numpy as jnp
from jax import lax
from jax.experimental import pallas as pl
from jax.experimental.pallas import tpu as pltpu

F32, BF16 = jnp.float32, jnp.bfloat16
HI = lax.Precision.HIGHEST
MESH = pl.DeviceIdType.MESH
N_DEV = 8

D = 1024
DFF = 2816
SSD_H, SSD_P, SSD_G, SSD_N, SSD_L = 16, 64, 2, 128, 128
SSD_INNER = SSD_H * SSD_P
CONV_K, CONV_CH = 4, 1536
MLA_H, QR, KVR, NOPE, ROPE, VD = 16, 384, 256, 64, 32, 64
QK = NOPE + ROPE
ROPE_THETA = 10000.0
XA_H, XA_D = 4, 256
EPS = 1e-6
FFN_RES = 0.5
LR, B1, B2, AEPS, WD, STEP = 0.001, 0.9, 0.999, 1e-08, 0.01, 10

VMEM_LIMIT = 56 * 2**20


def _cp(*sem):
    return pltpu.CompilerParams(dimension_semantics=sem, vmem_limit_bytes=VMEM_LIMIT)


def _sigmoid(x):
    return 1.0 / (1.0 + jnp.exp(-x))


def _softplus(x):
    return jnp.where(x > 20.0, x, jnp.log(1.0 + jnp.exp(jnp.minimum(x, 20.0))))


def _dot(a, b, dims="nn", hi=False):
    ca = 0 if dims[0] == "t" else 1
    cb = 1 if dims[1] == "t" else 0
    if hi:
        return lax.dot_general(a, b, (((ca,), (cb,)), ((), ())), precision=HI, preferred_element_type=F32)
    return lax.dot_general(a.astype(BF16), b.astype(BF16), (((ca,), (cb,)), ((), ())), preferred_element_type=F32)


def _ssd_common(dtr, dtb, alog):
    L = dtr.shape[0]
    dt = _softplus(dtr + dtb)
    a = -jnp.exp(alog)
    adt = dt * a
    r = lax.broadcasted_iota(jnp.int32, (L, L), 0)
    c = lax.broadcasted_iota(jnp.int32, (L, L), 1)
    lower = r >= c
    tri = lower.astype(F32)
    cs = _dot(tri, adt, "nn", hi=True)
    cs_t = _dot(adt, tri, "tt", hi=True)
    return dt, a, cs, cs_t, lower


def _head_expand():
    hh = lax.broadcasted_iota(jnp.int32, (SSD_H, SSD_INNER), 0)
    jj = lax.broadcasted_iota(jnp.int32, (SSD_H, SSD_INNER), 1)
    return ((jj >= hh * SSD_P) & (jj < hh * SSD_P + SSD_P)).astype(F32)


def _head_reduce():
    hh = lax.broadcasted_iota(jnp.int32, (SSD_INNER, SSD_H), 1)
    jj = lax.broadcasted_iota(jnp.int32, (SSD_INNER, SSD_H), 0)
    return ((jj >= hh * SSD_P) & (jj < hh * SSD_P + SSD_P)).astype(F32)


def ssd_fwd(xbc, dtr, dtb, alog, dsk, nseq):
    T = xbc.shape[0]
    S = T // nseq
    C = S // SSD_L
    L = SSD_L
    NP = SSD_H // 2

    def body(x_ref, b_ref, c_ref, dtr_ref, dtb_ref, alog_ref, dsk_ref, y_ref, prev_ref, st_ref):
        ci = pl.program_id(1)

        @pl.when(ci == 0)
        def _():
            st_ref[...] = jnp.zeros_like(st_ref)

        dt, a, cs, cs_t, lower = _ssd_common(dtr_ref[:, 0:SSD_H], dtb_ref[...], alog_ref[...])
        E = _head_expand()
        X = x_ref[...].astype(F32)
        dt_e = _dot(dt, E, hi=True)
        cs_e = _dot(cs, E, hi=True)
        csl_e = cs_e[L - 1:L, :]
        Xd = X * dt_e
        Xf = Xd * jnp.exp(csl_e - cs_e)
        e_e = jnp.exp(cs_e)
        skip = _dot(dsk_ref[...], E, hi=True) * X
        lane = lax.broadcasted_iota(jnp.int32, (1, 2 * SSD_P), 1)
        rowp = lax.broadcasted_iota(jnp.int32, (2 * SSD_P, 1), 0)
        for g in range(SSD_G):
            Bg = b_ref[:, g * SSD_N:(g + 1) * SSD_N]
            Cg = c_ref[:, g * SSD_N:(g + 1) * SSD_N]
            cb = _dot(Cg, Bg, "nt")
            for pp in range(NP // SSD_G):
                p = g * (NP // SSD_G) + pp
                sl = slice(p * 2 * SSD_P, (p + 1) * 2 * SSD_P)
                Xd_p = Xd[:, sl]
                yd = jnp.zeros((L, 2 * SSD_P), F32)
                for q in range(2):
                    h = 2 * p + q
                    m = jnp.where(lower, jnp.exp(jnp.minimum(cs[:, h:h + 1] - cs_t[h:h + 1, :], 0.0)), 0.0)
                    mask = (lane >= q * SSD_P) & (lane < (q + 1) * SSD_P)
                    yd = yd + _dot(cb * m, jnp.where(mask, Xd_p, 0.0))
                S0 = st_ref[p]
                prev_ref[0, 0, p] = S0
                z = _dot(Cg, S0, "nt")
                y_ref[:, sl] = (skip[:, sl] + yd + z * e_e[:, sl]).astype(y_ref.dtype)
                h0 = 2 * p
                dec = jnp.where(rowp < SSD_P, jnp.exp(cs[L - 1:L, h0:h0 + 1]), jnp.exp(cs[L - 1:L, h0 + 1:h0 + 2]))
                st_ref[p] = S0 * dec + _dot(Xf[:, sl], Bg, "tn")

    row = lambda b, c: (b * C + c, 0)
    return pl.pallas_call(
        body, grid=(nseq, C), name="ssd_fwd",
        in_specs=[pl.BlockSpec((L, SSD_INNER), row),
                  pl.BlockSpec((L, SSD_G * SSD_N), lambda b, c: (b * C + c, SSD_INNER // (SSD_G * SSD_N))),
                  pl.BlockSpec((L, SSD_G * SSD_N), lambda b, c: (b * C + c, SSD_INNER // (SSD_G * SSD_N) + 1)),
                  pl.BlockSpec((L, 128), row),
                  pl.BlockSpec((1, SSD_H), lambda b, c: (0, 0)),
                  pl.BlockSpec((1, SSD_H), lambda b, c: (0, 0)),
                  pl.BlockSpec((1, SSD_H), lambda b, c: (0, 0))],
        out_specs=[pl.BlockSpec((L, SSD_INNER), row),
                   pl.BlockSpec((1, 1, NP, 2 * SSD_P, SSD_N), lambda b, c: (b, c, 0, 0, 0))],
        out_shape=[jax.ShapeDtypeStruct((T, SSD_INNER), BF16),
                   jax.ShapeDtypeStruct((nseq, C, NP, 2 * SSD_P, SSD_N), F32)],
        scratch_shapes=[pltpu.VMEM((NP, 2 * SSD_P, SSD_N), F32)],
        compiler_params=_cp("parallel", "arbitrary"),
    )(xbc, xbc, xbc, dtr, dtb, alog, dsk)


def ssd_bwd(xbc, dtr, dtb, alog, dsk, prev, dy, nseq, comm=None):
    T = xbc.shape[0]
    S = T // nseq
    C = S // SSD_L
    L = SSD_L
    NP = SSD_H // 2

    def body(x_ref, b_ref, c_ref, dtr_ref, dtb_ref, alog_ref, dsk_ref, prev_ref, dy_ref,
             dxbc_ref, ddtr_ref, ddtb_ref, dalog_ref, ddsk_ref, ds_ref, stg_ref):
        bi = pl.program_id(0)
        ci = pl.program_id(1)

        @pl.when(ci == 0)
        def _():
            ds_ref[...] = jnp.zeros_like(ds_ref)

        @pl.when((ci == 0) & (bi == 0))
        def _():
            ddtb_ref[...] = jnp.zeros_like(ddtb_ref)
            dalog_ref[...] = jnp.zeros_like(dalog_ref)
            ddsk_ref[...] = jnp.zeros_like(ddsk_ref)

        dtr = dtr_ref[:, 0:SSD_H]
        dtb = dtb_ref[...]
        dt, a, cs, cs_t, lower = _ssd_common(dtr, dtb, alog_ref[...])
        upper = lax.broadcasted_iota(jnp.int32, (L, L), 1) >= lax.broadcasted_iota(jnp.int32, (L, L), 0)
        E = _head_expand()
        ET = _head_reduce()
        X = x_ref[...].astype(F32)
        dY = dy_ref[...].astype(F32)
        dt_e = _dot(dt, E, hi=True)
        cs_e = _dot(cs, E, hi=True)
        csl_e = cs_e[L - 1:L, :]
        f_e = jnp.exp(csl_e - cs_e)
        e_e = jnp.exp(cs_e)
        dsk_e = _dot(dsk_ref[...], E, hi=True)
        Xd = X * dt_e
        Xf = Xd * f_e
        lane = lax.broadcasted_iota(jnp.int32, (1, 2 * SSD_P), 1)
        rowp = lax.broadcasted_iota(jnp.int32, (2 * SSD_P, 1), 0)
        hsel = lax.broadcasted_iota(jnp.int32, (1, SSD_H), 1)
        dcs = jnp.zeros((L, SSD_H), F32)
        dcsl = jnp.zeros((1, SSD_H), F32)
        for g in range(SSD_G):
            Bg = b_ref[:, g * SSD_N:(g + 1) * SSD_N]
            Cg = c_ref[:, g * SSD_N:(g + 1) * SSD_N]
            cb = _dot(Cg, Bg, "nt")
            cbt = _dot(Bg, Cg, "nt")
            dB = jnp.zeros((L, SSD_N), F32)
            dC = jnp.zeros((L, SSD_N), F32)
            for pp in range(NP // SSD_G):
                p = g * (NP // SSD_G) + pp
                sl = slice(p * 2 * SSD_P, (p + 1) * 2 * SSD_P)
                Xd_p = Xd[:, sl]
                dY_p = dY[:, sl]
                dXd_p = jnp.zeros((L, 2 * SSD_P), F32)
                for q in range(2):
                    h = 2 * p + q
                    mask = (lane >= q * SSD_P) & (lane < (q + 1) * SSD_P)
                    col = cs[:, h:h + 1]
                    rw = cs_t[h:h + 1, :]
                    m = jnp.where(lower, jnp.exp(jnp.minimum(col - rw, 0.0)), 0.0)
                    mt = jnp.where(upper, jnp.exp(jnp.minimum(rw - col, 0.0)), 0.0)
                    dYm = jnp.where(mask, dY_p, 0.0)
                    dW = _dot(dYm, Xd_p, "nt")
                    dWt = _dot(Xd_p, dYm, "nt")
                    w = cb * m
                    wt = cbt * mt
                    dC = dC + _dot(dW * m, Bg)
                    dB = dB + _dot(dWt * mt, Cg)
                    dXd_p = dXd_p + jnp.where(mask, _dot(wt, dY_p), 0.0)
                    qcol = jnp.sum(dW * w, axis=1, keepdims=True) - jnp.sum(dWt * wt, axis=1, keepdims=True)
                    dcs = dcs + qcol * (hsel == h).astype(F32)
                S0 = prev_ref[0, 0, p]
                dSn = ds_ref[p]
                dZ = dY_p * e_e[:, sl]
                dC = dC + _dot(dZ, S0)
                h0 = 2 * p
                el0 = jnp.exp(cs[L - 1:L, h0:h0 + 1])
                el1 = jnp.exp(cs[L - 1:L, h0 + 1:h0 + 2])
                dec = jnp.where(rowp < SSD_P, el0, el1)
                ds_ref[p] = dSn * dec + _dot(dZ, Cg, "tn")
                dXf_p = _dot(Bg, dSn, "nt")
                dB = dB + _dot(Xf[:, sl], dSn)
                rs = jnp.sum(dSn * S0, axis=1, keepdims=True)
                s0 = jnp.sum(jnp.where(rowp < SSD_P, rs, 0.0), axis=0, keepdims=True) * el0
                s1 = jnp.sum(jnp.where(rowp >= SSD_P, rs, 0.0), axis=0, keepdims=True) * el1
                dcsl = dcsl + s0 * (hsel == h0).astype(F32) + s1 * (hsel == h0 + 1).astype(F32)
                y_off = _dot(Cg, S0, "nt") * e_e[:, sl]
                t1 = dY_p * y_off - dXf_p * Xf[:, sl]
                r1 = jnp.where(lane < SSD_P, t1, 0.0)
                c0 = jnp.sum(r1, axis=1, keepdims=True)
                c1 = jnp.sum(t1 - r1, axis=1, keepdims=True)
                dcs = dcs + c0 * (hsel == h0).astype(F32) + c1 * (hsel == h0 + 1).astype(F32)
                t2 = dXf_p * Xf[:, sl]
                r2 = jnp.where(lane < SSD_P, t2, 0.0)
                dcsl = dcsl + jnp.sum(r2, keepdims=True) * (hsel == h0).astype(F32) \
                    + jnp.sum(t2 - r2, keepdims=True) * (hsel == h0 + 1).astype(F32)
                stg_ref[:, sl] = dXd_p + dXf_p * f_e[:, sl]
            dxbc_ref[:, SSD_INNER + g * SSD_N:SSD_INNER + (g + 1) * SSD_N] = dB.astype(dxbc_ref.dtype)
            dxbc_ref[:, SSD_INNER + (SSD_G + g) * SSD_N:SSD_INNER + (SSD_G + g + 1) * SSD_N] = dC.astype(dxbc_ref.dtype)
        dXd = stg_ref[...]
        dxbc_ref[:, 0:SSD_INNER] = (dXd * dt_e + dsk_e * dY).astype(dxbc_ref.dtype)
        rowl = lax.broadcasted_iota(jnp.int32, (L, 1), 0)
        dcs = dcs + jnp.where(rowl == L - 1, dcsl, 0.0)
        dalpha = _dot(upper.astype(F32), dcs, hi=True)
        ddt = _dot(dXd * X, ET, hi=True) + dalpha * a
        dalog_ref[...] += jnp.sum(dalpha * dt, axis=0, keepdims=True) * a
        ddtr = ddt * _sigmoid(dtr + dtb)
        spread = (lax.broadcasted_iota(jnp.int32, (SSD_H, 128), 0) == lax.broadcasted_iota(jnp.int32, (SSD_H, 128), 1)).astype(F32)
        ddtr_ref[...] = _dot(ddtr, spread, hi=True).astype(ddtr_ref.dtype)
        ddtb_ref[...] += jnp.sum(ddtr, axis=0, keepdims=True)
        ddsk_ref[...] += jnp.sum(_dot(dY * X, ET, hi=True), axis=0, keepdims=True)

    rowr = lambda b, c: (b * C + (C - 1 - c), 0)
    small = pl.BlockSpec((1, SSD_H), lambda b, c: (0, 0))
    return _call_with_comm(
        body, (nseq, C), "ssd_bwd",
        [pl.BlockSpec((L, SSD_INNER), rowr),
         pl.BlockSpec((L, SSD_G * SSD_N), lambda b, c: (b * C + (C - 1 - c), SSD_INNER // (SSD_G * SSD_N))),
         pl.BlockSpec((L, SSD_G * SSD_N), lambda b, c: (b * C + (C - 1 - c), SSD_INNER // (SSD_G * SSD_N) + 1)),
         pl.BlockSpec((L, 128), rowr), small, small, small,
         pl.BlockSpec((1, 1, NP, 2 * SSD_P, SSD_N), lambda b, c: (b, C - 1 - c, 0, 0, 0)),
         pl.BlockSpec((L, SSD_INNER), rowr)],
        [xbc, xbc, xbc, dtr, dtb, alog, dsk, prev, dy],
        [pl.BlockSpec((L, CONV_CH), rowr), pl.BlockSpec((L, 128), rowr), small, small, small],
        [jax.ShapeDtypeStruct((T, CONV_CH), BF16), jax.ShapeDtypeStruct((T, 128), BF16),
         jax.ShapeDtypeStruct((1, SSD_H), F32), jax.ShapeDtypeStruct((1, SSD_H), F32), jax.ShapeDtypeStruct((1, SSD_H), F32)],
        comm, scratch=[pltpu.VMEM((NP, 2 * SSD_P, SSD_N), F32), pltpu.VMEM((L, SSD_INNER), F32)], sem=("arbitrary", "arbitrary"))


SLOT = 128
ATT_T = 512


def _col_to_row(col):
    n = col.shape[0]
    eye = lax.broadcasted_iota(jnp.int32, (n, n), 0) == lax.broadcasted_iota(jnp.int32, (n, n), 1)
    return jnp.sum(jnp.where(eye, col, 0.0), axis=0, keepdims=True)


def attn_slot_fwd(q, k, v, nseq, comm=None):
    T = q.shape[0]
    S = T // nseq
    t = min(ATT_T, S)
    nb = S // t
    scale = QK ** -0.5

    def body(q_ref, k_ref, v_ref, o_ref, lse_ref):
        causal = lax.broadcasted_iota(jnp.int32, (t, t), 1) <= lax.broadcasted_iota(jnp.int32, (t, t), 0)
        for qi in range(nb):
            qb = q_ref[qi * t:(qi + 1) * t, :]
            m = l = acc = None
            for kj in range(qi + 1):
                s = _dot(qb, k_ref[kj * t:(kj + 1) * t, :], "nt") * scale
                if kj == qi:
                    s = jnp.where(causal, s, -1e30)
                bm = jnp.max(s, axis=1, keepdims=True)
                if kj == 0:
                    m = bm
                    p = jnp.exp(s - m)
                    l = jnp.sum(p, axis=1, keepdims=True)
                    acc = _dot(p, v_ref[0:t, :])
                else:
                    m_new = jnp.maximum(m, bm)
                    corr = jnp.exp(m - m_new)
                    p = jnp.exp(s - m_new)
                    l = l * corr + jnp.sum(p, axis=1, keepdims=True)
                    acc = acc * corr + _dot(p, v_ref[kj * t:(kj + 1) * t, :])
                    m = m_new
            o_ref[qi * t:(qi + 1) * t, :] = (acc / l).astype(o_ref.dtype)
            lse_ref[0, 0, :, qi * t:(qi + 1) * t] = _col_to_row(m + jnp.log(l))

    blk = pl.BlockSpec((S, SLOT), lambda b, h: (b, h))
    return _call_with_comm(
        body, (nseq, MLA_H), "attn_fwd", [blk, blk, blk], [q, k, v],
        [blk, pl.BlockSpec((1, 1, 1, S), lambda b, h: (b, h, 0, 0))],
        [jax.ShapeDtypeStruct((T, MLA_H * SLOT), BF16), jax.ShapeDtypeStruct((nseq, MLA_H, 1, S), F32)], comm)


def attn_slot_bwd(q, k, v, o, lse, do, nseq, comm=None):
    T = q.shape[0]
    S = T // nseq
    t = min(ATT_T, S)
    nb = S // t
    scale = QK ** -0.5

    def body(q_ref, k_ref, v_ref, o_ref, lse_ref, do_ref, dq_ref, dk_ref, dv_ref, dqa_ref):
        causal_t =lax.broadcasted_iota(jnp.int32, (t, t), 0) <= lax.broadcasted_iota(jnp.int32, (t, t), 1)
        ones = jnp.ones((8, SLOT), F32)
        delta = []
        for qi in range(nb):
            sl = slice(qi * t, (qi + 1) * t)
            prod = do_ref[sl, :].astype(F32) * o_ref[sl, :].astype(F32)
            delta.append(_dot(ones, prod, "nt", hi=True)[0:1, :])
        for kj in range(nb):
            ks = slice(kj * t, (kj + 1) * t)
            kb = k_ref[ks, :]
            vb = v_ref[ks, :]
            dk = dv = None
            for qi in range(kj, nb):
                sl = slice(qi * t, (qi + 1) * t)
                qb = q_ref[sl, :]
                dob = do_ref[sl, :]
                st = _dot(kb, qb, "nt") * scale
                pt = jnp.exp(st - lse_ref[0, 0, :, sl])
                if qi == kj:
                    pt = jnp.where(causal_t, pt, 0.0)
                dpt = _dot(vb, dob, "nt")
                dst = (pt * (dpt - delta[qi]) * scale).astype(BF16)
                dvc = _dot(pt, dob)
                dkc = _dot(dst, qb)
                dv = dvc if dv is None else dv + dvc
                dk = dkc if dk is None else dk + dkc
                dqc = _dot(dst, kb, "tn")
                if kj > 0:
                    dqc = dqc + dqa_ref[sl, :]
                if qi == kj:
                    dq_ref[sl, :] = dqc.astype(dq_ref.dtype)
                else:
                    dqa_ref[sl, :] = dqc
            dk_ref[ks, :] = dk.astype(dk_ref.dtype)
            dv_ref[ks, :] = dv.astype(dv_ref.dtype)

    blk = pl.BlockSpec((S, SLOT), lambda b, h: (b, h))
    lse_spec = pl.BlockSpec((1, 1, 1, S), lambda b, h: (b, h, 0, 0))
    W = MLA_H * SLOT
    return _call_with_comm(
        body, (nseq, MLA_H), "attn_bwd", [blk, blk, blk, blk, lse_spec, blk], [q, k, v, o, lse, do], [blk, blk, blk],
        [jax.ShapeDtypeStruct((T, W), BF16)] * 3, comm, scratch=[pltpu.VMEM((S, SLOT), F32)])


def _rope_coeffs(cos, sin):
    half = ROPE // 2
    r = lax.broadcasted_iota(jnp.int32, (half, SLOT), 0)
    c = lax.broadcasted_iota(jnp.int32, (half, SLOT), 1)
    pc = ((c == r + NOPE) | (c == r + NOPE + half)).astype(F32)
    ps = (c == r + NOPE + half).astype(F32) - (c == r + NOPE).astype(F32)
    lane = lax.broadcasted_iota(jnp.int32, (1, SLOT), 1)
    return _dot(cos, pc, hi=True) + (lane < NOPE).astype(F32), _dot(sin, ps, hi=True)


def _rope_swap(x):
    W = x.shape[1]
    half = ROPE // 2
    lane = lax.broadcasted_iota(jnp.int32, (1, W), 1) & (SLOT - 1)
    up = pltpu.roll(x, W - half, axis=1)
    dn = pltpu.roll(x, half, axis=1)
    return jnp.where((lane >= NOPE) & (lane < NOPE + half), up, jnp.where((lane >= NOPE + half) & (lane < QK), dn, 0.0))


def rope_slot_fwd(q, kn, dtkr, cos, sin, name):
    def fn(qv, knv, krv, cv, sv):
        C, Sg = _rope_coeffs(cv, sv)
        ct, stl = jnp.tile(C, (1, MLA_H)), jnp.tile(Sg, (1, MLA_H))
        qo = qv * ct + _rope_swap(qv) * stl
        r = lax.broadcasted_iota(jnp.int32, (SLOT, SLOT), 0)
        c = lax.broadcasted_iota(jnp.int32, (SLOT, SLOT), 1)
        place = ((c == r + NOPE) & (r < ROPE)).astype(F32)
        kr = _dot(krv, place, hi=True)
        kr = kr * C + _rope_swap(kr) * Sg
        return qo, knv.astype(F32) + jnp.tile(kr, (1, MLA_H))
    W = MLA_H * SLOT
    return rowwise(fn, [q, kn, (dtkr, SLOT, 1), cos, sin], [], [(W, BF16), (W, BF16)], [], name)


def rope_slot_bwd(dq, dk, cos, sin, name):
    def fn(dqv, dkv, cv, sv):
        C, Sg = _rope_coeffs(cv, sv)
        ct, stl = jnp.tile(C, (1, MLA_H)), jnp.tile(Sg, (1, MLA_H))
        dqo = dqv * ct - _rope_swap(dqv) * stl
        tot = dkv[:, 0:SLOT]
        for h in range(1, MLA_H):
            tot = tot + dkv[:, h * SLOT:(h + 1) * SLOT]
        u = tot * C - _rope_swap(tot) * Sg
        r = lax.broadcasted_iota(jnp.int32, (SLOT, SLOT), 0)
        c = lax.broadcasted_iota(jnp.int32, (SLOT, SLOT), 1)
        unplace = ((r == c + NOPE) & (c < ROPE)).astype(F32)
        return dqo, dkv, _dot(u, unplace, hi=True)
    W = MLA_H * SLOT
    return rowwise(fn, [dq, dk, cos, sin], [], [(W, BF16), (W, BF16), (SLOT, BF16)], [], name)


XA_BLK = 512


def xattn_fwd(q, k, v, nseq):
    T = q.shape[0]
    S = T // nseq
    M = k.shape[0] // nseq
    tq = min(XA_BLK, S)
    nq = S // tq
    scale = XA_D ** -0.5

    def body(q_ref, k_ref, v_ref, o_ref):
        s = _dot(q_ref[...], k_ref[...], "nt") * scale
        p = jnp.exp(s - jnp.max(s, axis=1, keepdims=True))
        p = p / jnp.sum(p, axis=1, keepdims=True)
        o_ref[...] = _dot(p, v_ref[...]).astype(o_ref.dtype)

    qs = pl.BlockSpec((tq, XA_D), lambda b, h, i: (b * nq + i, h))
    ks = pl.BlockSpec((M, XA_D), lambda b, h, i: (b, h))
    return pl.pallas_call(
        body, grid=(nseq, XA_H, nq), name="xattn_fwd", in_specs=[qs, ks, ks], out_specs=qs,
        out_shape=jax.ShapeDtypeStruct((T, XA_H * XA_D), BF16),
        compiler_params=_cp("parallel", "parallel", "parallel"),
    )(q, k, v)


def xattn_bwd(q, k, v, do, nseq):
    T = q.shape[0]
    S = T // nseq
    M = k.shape[0] // nseq
    tq = min(XA_BLK, S)
    nq = S // tq
    scale = XA_D ** -0.5

    def body(q_ref, k_ref, v_ref, do_ref, dq_ref, dk_ref, dv_ref):
        @pl.when(pl.program_id(2) == 0)
        def _():
            dk_ref[...] = jnp.zeros_like(dk_ref)
            dv_ref[...] = jnp.zeros_like(dv_ref)

        qb, kb, vb, dob = q_ref[...], k_ref[...], v_ref[...], do_ref[...]
        s = _dot(qb, kb, "nt") * scale
        p = jnp.exp(s - jnp.max(s, axis=1, keepdims=True))
        p = p / jnp.sum(p, axis=1, keepdims=True)
        dp = _dot(dob, vb, "nt")
        ds = p * (dp - jnp.sum(dp * p, axis=1, keepdims=True)) * scale
        dq_ref[...] = _dot(ds, kb).astype(dq_ref.dtype)
        dk_ref[...] += _dot(ds, qb, "tn")
        dv_ref[...] += _dot(p, dob, "tn")

    qs = pl.BlockSpec((tq, XA_D), lambda b, h, i: (b * nq + i, h))
    ks = pl.BlockSpec((M, XA_D), lambda b, h, i: (b, h))
    return pl.pallas_call(
        body, grid=(nseq, XA_H, nq), name="xattn_bwd", in_specs=[qs, ks, ks, qs], out_specs=[qs, ks, ks],
        out_shape=[jax.ShapeDtypeStruct((T, XA_H * XA_D), BF16), jax.ShapeDtypeStruct(k.shape, F32),
                   jax.ShapeDtypeStruct(k.shape, F32)],
        compiler_params=_cp("parallel", "parallel", "arbitrary"),
    )(q, k, v, do)


CONV_BLK = 256


def _shift_down(x, s, rows):
    if s == 0:
        return x
    return jnp.where(rows >= s, pltpu.roll(x, s, axis=0), 0.0)


def _shift_up(x, s, rows):
    if s == 0:
        return x
    S = x.shape[0]
    return jnp.where(rows < S - s, pltpu.roll(x, S - s, axis=0), 0.0)


def conv_fwd(x, w, b, nseq):
    T, CH = x.shape
    S = T // nseq

    def body(x_ref, w_ref, b_ref, o_ref):
        xv = x_ref[...].astype(F32)
        rows = lax.broadcasted_iota(jnp.int32, (S, 1), 0)
        c = jnp.zeros_like(xv) + b_ref[...]
        for kk in range(CONV_K):
            c = c + w_ref[kk:kk + 1, :] * _shift_down(xv, CONV_K - 1 - kk, rows)
        o_ref[...] = (c * _sigmoid(c)).astype(o_ref.dtype)

    xs = pl.BlockSpec((S, CONV_BLK), lambda j, bb: (bb, j))
    return pl.pallas_call(
        body, grid=(CH // CONV_BLK, nseq), name="conv_fwd",
        in_specs=[xs, pl.BlockSpec((CONV_K, CONV_BLK), lambda j, bb: (0, j)), pl.BlockSpec((1, CONV_BLK), lambda j, bb: (0, j))],
        out_specs=xs, out_shape=jax.ShapeDtypeStruct((T, CH), BF16),
        compiler_params=_cp("parallel", "parallel"),
    )(x, w, b)


def conv_bwd(x, w, b, dout, nseq):
    T, CH = x.shape
    S = T // nseq

    def body(x_ref, w_ref, b_ref, do_ref, dx_ref, dw_ref, db_ref):
        @pl.when(pl.program_id(1) == 0)
        def _():
            dw_ref[...] = jnp.zeros_like(dw_ref)
            db_ref[...] = jnp.zeros_like(db_ref)

        xv = x_ref[...].astype(F32)
        rows = lax.broadcasted_iota(jnp.int32, (S, 1), 0)
        c = jnp.zeros_like(xv) + b_ref[...]
        sh = [_shift_down(xv, CONV_K - 1 - kk, rows) for kk in range(CONV_K)]
        for kk in range(CONV_K):
            c = c + w_ref[kk:kk + 1, :] * sh[kk]
        sg = _sigmoid(c)
        dc = do_ref[...].astype(F32) * sg * (1.0 + c * (1.0 - sg))
        dx = jnp.zeros_like(xv)
        for kk in range(CONV_K):
            dx = dx + w_ref[kk:kk + 1, :] * _shift_up(dc, CONV_K - 1 - kk, rows)
            dw_ref[kk:kk + 1, :] += jnp.sum(dc * sh[kk], axis=0, keepdims=True)
        dx_ref[...] = dx.astype(dx_ref.dtype)
        db_ref[...] += jnp.sum(dc, axis=0, keepdims=True)

    xs = pl.BlockSpec((S, CONV_BLK), lambda j, bb: (bb, j))
    ws = pl.BlockSpec((CONV_K, CONV_BLK), lambda j, bb: (0, j))
    bs = pl.BlockSpec((1, CONV_BLK), lambda j, bb: (0, j))
    return pl.pallas_call(
        body, grid=(CH // CONV_BLK, nseq), name="conv_bwd",
        in_specs=[xs, ws, bs, xs], out_specs=[xs, ws, bs],
        out_shape=[jax.ShapeDtypeStruct((T, CH), BF16), jax.ShapeDtypeStruct((CONV_K, CH), F32),
                   jax.ShapeDtypeStruct((1, CH), F32)],
        compiler_params=_cp("parallel", "arbitrary"),
    )(x, w, b, dout)


def _dims(a, b, mode):
    M = a.shape[1] if mode[0] == "t" else a.shape[0]
    K = a.shape[0] if mode[0] == "t" else a.shape[1]
    N = b.shape[0] if mode[1] == "t" else b.shape[1]
    return M, K, N


def _tile(dim, prefs):
    for p in prefs:
        if dim % p == 0:
            return p
    return dim


def mm(groups, out_dtypes, name, tm=None, tn=None, tk=None, epi=None, extras=(), comm=None):
    a0, b0, m0 = groups[0][0]
    M, K0, N = _dims(a0, b0, m0)
    tm = tm or _tile(M, (1024, 512, 256, 128))
    tn = tn or _tile(N, (512, 256, 128))
    flat = [p for g in groups for p in g]
    nk = 1 if tk is None else K0 // tk
    in_specs, args = [], []
    for a, b, mode in flat:
        _, K, _ = _dims(a, b, mode)
        kb = K if tk is None else tk
        in_specs.append(pl.BlockSpec((kb, tm), lambda i, j, k: (k, i)) if mode[0] == "t"
                        else pl.BlockSpec((tm, kb), lambda i, j, k: (i, k)))
        in_specs.append(pl.BlockSpec((tn, kb), lambda i, j, k: (j, k)) if mode[1] == "t"
                        else pl.BlockSpec((kb, tn), lambda i, j, k: (k, j)))
        args += [a, b]
    for e in extras:
        in_specs.append(pl.BlockSpec((1, tn), lambda i, j, k: (0, j)) if e.shape[0] == 1 and M != 1
                        else pl.BlockSpec((tm, tn), lambda i, j, k: (i, j)))
        args.append(e)
    n_in = len(args)
    n_out = len(out_dtypes)
    ng = len(groups)
    sizes = [len(g) for g in groups]

    def body(*refs):
        ins, outs, accs = refs[:n_in], refs[n_in:n_in + n_out], refs[n_in + n_out:]
        kk = pl.program_id(2)
        vals, pos = [], 0
        for gi in range(ng):
            acc = None
            for _ in range(sizes[gi]):
                mode = flat[pos // 2][2]
                d = _dot(ins[pos][...], ins[pos + 1][...], mode)
                acc = d if acc is None else acc + d
                pos += 2
            vals.append(acc)
        ex = [r[...].astype(F32) for r in ins[2 * len(flat):]]

        def finish(accv):
            res = epi(accv, ex) if epi is not None else tuple(accv)
            for o, r in zip(outs, res):
                o[...] = r.astype(o.dtype)

        if nk == 1:
            finish(vals)
        else:
            @pl.when(kk == 0)
            def _():
                for ar, vv in zip(accs, vals):
                    ar[...] = vv

            @pl.when(kk > 0)
            def _():
                for ar, vv in zip(accs, vals):
                    ar[...] += vv

            @pl.when(kk == nk - 1)
            def _():
                finish([ar[...] for ar in accs])

    grid = (M // tm, N // tn, nk)
    out_specs = [pl.BlockSpec((tm, tn), lambda i, j, k: (i, j)) for _ in out_dtypes]
    out_shape = [jax.ShapeDtypeStruct((M, N), dt) for dt in out_dtypes]
    scratch = [pltpu.VMEM((tm, tn), F32) for _ in range(ng if nk > 1 else 0)]
    sem = ("parallel", "parallel", "arbitrary")
    if comm is not None:
        body = _attach(comm, body, n_in, n_out, *_grid_ends(grid))
        in_specs, args = in_specs + [HBM_SPEC] * len(comm.inputs), args + comm.inputs
        out_specs, out_shape = out_specs + [HBM_SPEC] * len(comm.out_shapes), out_shape + comm.out_shapes
        scratch, sem = scratch + comm.sems, ("arbitrary",) * 3
    return pl.pallas_call(body, grid=grid, name=name, in_specs=in_specs, out_specs=out_specs, out_shape=out_shape,
                          scratch_shapes=scratch, compiler_params=_cp(*sem))(*args)


def mm1(a, b, mode, out_dtype, name, **kw):
    return mm([[(a, b, mode)]], [out_dtype], name, **kw)[0]


ROW_BLK = 256


def rowwise(fn, rows, consts, outs, accs, name, tb=ROW_BLK):
    rows = [r if isinstance(r, tuple) else (r, r.shape[1], 0) for r in rows]
    T = rows[0][0].shape[0]
    tb = min(tb, T)
    n_r, n_c, n_o, n_a = len(rows), len(consts), len(outs), len(accs)

    def body(*refs):
        vals = [r[...].astype(F32) for r in refs[:n_r + n_c]]
        res = fn(*vals)
        o_refs = refs[n_r + n_c:n_r + n_c + n_o]
        a_refs = refs[n_r + n_c + n_o:]
        for o, r in zip(o_refs, res[:n_o]):
            o[...] = r.astype(o.dtype)
        if n_a:
            @pl.when(pl.program_id(0) == 0)
            def _():
                for ar in a_refs:
                    ar[...] = jnp.zeros_like(ar)
            for ar, r in zip(a_refs, res[n_o:]):
                ar[...] += r

    return pl.pallas_call(
        body, grid=(T // tb,), name=name,
        in_specs=[pl.BlockSpec((tb, w), functools.partial(lambda i, j: (i, j), j=j)) for _, w, j in rows]
        + [pl.BlockSpec(c.shape, lambda i: (0, 0)) for c in consts],
        out_specs=[pl.BlockSpec((tb, d), lambda i: (i, 0)) for d, _ in outs]
        + [pl.BlockSpec(s, lambda i: (0, 0)) for s in accs],
        out_shape=[jax.ShapeDtypeStruct((T, d), dt) for d, dt in outs]
        + [jax.ShapeDtypeStruct(s, F32) for s in accs],
        compiler_params=_cp("arbitrary" if n_a else "parallel"),
    )(*[r[0] for r in rows], *consts)


def _rms_stats(x):
    r = lax.rsqrt(jnp.mean(x * x, axis=-1, keepdims=True) + EPS)
    return r, x * r


def _rms_bwd(x, g, dy):
    r, xn = _rms_stats(x)
    dyg = dy * g
    dx = r * (dyg - xn * jnp.mean(dyg * xn, axis=-1, keepdims=True))
    return dx, jnp.sum(dy * xn, axis=0, keepdims=True)


def rms_fwd(x, g, name):
    return rowwise(lambda xv, gv: (_rms_stats(xv)[1] * gv,), [x], [g], [(x.shape[1], BF16)], [], name)[0]


def rms_bwd(x, g, dy, name, resid=None, dx_dtype=F32):
    def fn(*v):
        if resid is None:
            xv, dyv, gv = v
            dx, dg = _rms_bwd(xv, gv, dyv)
        else:
            xv, dyv, rv, gv = v
            dx, dg = _rms_bwd(xv, gv, dyv)
            dx = dx + rv
        return dx, dg
    rows = [x, dy] + ([] if resid is None else [resid])
    return rowwise(fn, rows, [g], [(x.shape[1], dx_dtype)], [(1, x.shape[1])], name)


def mm_resid(a, b, x, g, wgt, name):
    epi = lambda accs, ex: (accs[0], ex[0] + wgt * _rms_stats(accs[0])[1] * ex[1])
    return mm([[(a, b, "nn")]], [F32, F32], name, tm=min(512, a.shape[0]), tn=b.shape[1], epi=epi, extras=[x, g])


def resid_bwd(h, g, dy, wgt, name):
    def fn(hv, dyv, gv):
        dx, dg = _rms_bwd(hv, gv, dyv)
        return wgt * dx, wgt * dg
    return rowwise(fn, [h, dy], [g], [(h.shape[1], BF16)], [(1, h.shape[1])], name)


def _silu_parts(g):
    s = _sigmoid(g)
    return g * s, s * (1.0 + g * (1.0 - s))


def gated_norm_fwd(y, z, g, name):
    W = SSD_INNER // SSD_G

    def fn(yv, zv, gv):
        yg = yv * _silu_parts(zv)[0]
        return (jnp.concatenate([_rms_stats(yg[:, i * W:(i + 1) * W])[1] for i in range(SSD_G)], axis=1) * gv,)
    return rowwise(fn, [y, z], [g], [(SSD_INNER, BF16)], [], name)[0]


def gated_norm_bwd(y, z, dyn, g, name):
    W = SSD_INNER // SSD_G

    def fn(yv, zv, dv, gv):
        sil, dsil = _silu_parts(zv)
        yg = yv * sil
        parts = [_rms_bwd(yg[:, i * W:(i + 1) * W], gv[:, i * W:(i + 1) * W], dv[:, i * W:(i + 1) * W]) for i in range(SSD_G)]
        dyg = jnp.concatenate([p[0] for p in parts], axis=1)
        dg = jnp.concatenate([p[1] for p in parts], axis=1)
        return dyg * sil, dyg * yv * dsil, dg
    return rowwise(fn, [y, z, dyn], [g], [(SSD_INNER, BF16), (SSD_INNER, BF16)], [(1, SSD_INNER)], name)


def merge_fwd(gl, ys, ym, gb, name):
    def fn(glv, ysv, ymv, gbv):
        gt = _sigmoid(glv + gbv)
        return (gt[:, :D] * ysv + gt[:, D:] * ymv,)
    return rowwise(fn, [gl, ys, ym], [gb], [(D, BF16)], [], name)[0]


def merge_bwd(gl, ys, ym, dm, gb, name):
    def fn(glv, ysv, ymv, dmv, gbv):
        gt = _sigmoid(glv + gbv)
        gs, gm = gt[:, :D], gt[:, D:]
        dgl = jnp.concatenate([dmv * ysv * gs * (1.0 - gs), dmv * ymv * gm * (1.0 - gm)], axis=1)
        return dmv * gs, dmv * gm, dgl, jnp.sum(dgl, axis=0, keepdims=True)
    return rowwise(fn, [gl, ys, ym, dm], [gb], [(D, BF16), (D, BF16), (2 * D, BF16)], [(1, 2 * D)], name)


def loss_head(y, tgt, name):
    def fn(yv, tv):
        d = yv - tv
        part = 0.5 * jnp.sum(jnp.sum(d * d, axis=1, keepdims=True), axis=0, keepdims=True) / D
        return d / D, jnp.broadcast_to(part, (1, 128))
    return rowwise(fn, [y, tgt], [], [(D, F32)], [(1, 128)], name)


def adamw(w, g, m, v, name):
    R, C = w.shape
    tb = _tile(R, (256, 128, 64, 32, 16, 8))

    def fn(wv, gv, mv, vv):
        mn = B1 * mv + (1.0 - B1) * gv
        vn = B2 * vv + (1.0 - B2) * (gv * gv)
        mh = mn / (1.0 - B1 ** STEP)
        vh = vn / (1.0 - B2 ** STEP)
        return -LR * (mh / (jnp.sqrt(vh) + AEPS) + WD * wv), mn, vn
    return rowwise(fn, [w, g, m, v], [], [(C, F32)] * 3, [], name, tb=tb)


def _me():
    return lax.axis_index("x"), lax.axis_index("y"), lax.axis_index("c")


def _dev_index():
    x, y, c = _me()
    return 4 * x + 2 * y + c


HBM_SPEC = pl.BlockSpec(memory_space=pl.ANY)


class GatherComm:
    def __init__(self, shards):
        self.inputs = list(shards)
        n = len(shards)
        self.out_shapes = [jax.ShapeDtypeStruct((N_DEV,) + s.shape, s.dtype) for s in shards]
        self.sems = [pltpu.SemaphoreType.DMA((7 * n,)), pltpu.SemaphoreType.DMA((7 * n,)), pltpu.SemaphoreType.DMA((n,))]

    def _plan(self, x_refs, out_refs, sems):
        send_sems, recv_sems, local_sems = sems
        n = len(x_refs)
        x, y, c = _me()
        me, sibling = (x, y, c), (x, y, 1 - c)
        chips = [(1 - x, y), (x, 1 - y), (1 - x, 1 - y)]

        def slot(i, px, py, pc):
            return out_refs[i].at[4 * px + 2 * py + pc]

        def copy(i, k, block, to, src=None):
            return pltpu.make_async_remote_copy(
                src_ref=slot(i, *block) if src is None else src, dst_ref=slot(i, *block),
                send_sem=send_sems.at[7 * i + k], recv_sem=recv_sems.at[7 * i + k], device_id=to, device_id_type=MESH)

        mine = [pltpu.make_async_copy(x_refs[i], slot(i, *me), local_sems.at[i]) for i in range(n)]
        first = []
        for i in range(n):
            first.append(copy(i, 0, me, sibling, src=x_refs[i]))
            first += [copy(i, 1 + j, me, (*chip, c), src=x_refs[i]) for j, chip in enumerate(chips)]
        passed = [[copy(i, 4 + j, (*chip, c), sibling) for j, chip in enumerate(chips)] for i in range(n)]
        from_ici = [[copy(i, 1 + j, (*chip, c), me) for j, chip in enumerate(chips)] for i in range(n)]
        from_sib = [[copy(i, 0, sibling, me)] + [copy(i, 4 + j, (*chip, 1 - c), me) for j, chip in enumerate(chips)] for i in range(n)]
        return mine, first, passed, from_ici, from_sib

    def start(self, x_refs, out_refs, sems):
        mine, first, _, _, _ = self._plan(x_refs, out_refs, sems)
        for cp in mine + first:
            cp.start()

    def finish(self, x_refs, out_refs, sems):
        mine, first, passed, from_ici, from_sib = self._plan(x_refs, out_refs, sems)
        for i in range(len(x_refs)):
            for arrival, forward in zip(from_ici[i], passed[i]):
                arrival.wait_recv()
                forward.start()
        for row in from_sib:
            for arrival in row:
                arrival.wait_recv()
        for cp in first + [cp for row in passed for cp in row]:
            cp.wait_send()
        for cp in mine:
            cp.wait()


def run_comm(comm, name):
    n_in, n_out = len(comm.inputs), len(comm.out_shapes)

    def body(*refs):
        ins, outs, sems = refs[:n_in], refs[n_in:n_in + n_out], refs[n_in + n_out:]
        comm.start(ins, outs, sems)
        comm.finish(ins, outs, sems)

    return pl.pallas_call(body, name=name, out_shape=comm.out_shapes, in_specs=[HBM_SPEC] * n_in,
                          out_specs=[HBM_SPEC] * n_out, scratch_shapes=comm.sems)(*comm.inputs)


def _attach(comm, body, n_in, n_out, first, last):
    if comm is None:
        return body
    ci, co, cs = len(comm.inputs), len(comm.out_shapes), len(comm.sems)

    def wrapped(*refs):
        h_in, c_in = refs[:n_in], refs[n_in:n_in + ci]
        h_out, c_out = refs[n_in + ci:n_in + ci + n_out], refs[n_in + ci + n_out:n_in + ci + n_out + co]
        rest = refs[n_in + ci + n_out + co:]
        h_scr, c_sem = rest[:len(rest) - cs], rest[len(rest) - cs:]

        @pl.when(first())
        def _():
            comm.start(c_in, c_out, c_sem)

        body(*h_in, *h_out, *h_scr)

        @pl.when(last())
        def _():
            comm.finish(c_in, c_out, c_sem)

    return wrapped


def _grid_ends(grid):
    first = lambda: functools.reduce(lambda a, b: a & b, [pl.program_id(i) == 0 for i in range(len(grid))])
    last = lambda: functools.reduce(lambda a, b: a & b, [pl.program_id(i) == g - 1 for i, g in enumerate(grid)])
    return first, last


def _call_with_comm(body, grid, name, in_specs, args, out_specs, out_shape, comm, scratch=(), sem=None):
    sem = sem or ("parallel",) * len(grid)
    scratch = list(scratch)
    if comm is not None:
        body = _attach(comm, body, len(args), len(out_shape), *_grid_ends(grid))
        in_specs, args = in_specs + [HBM_SPEC] * len(comm.inputs), args + comm.inputs
        out_specs, out_shape = out_specs + [HBM_SPEC] * len(comm.out_shapes), out_shape + comm.out_shapes
        scratch, sem = scratch + comm.sems, ("arbitrary",) * len(grid)
    return pl.pallas_call(body, grid=grid, name=name, in_specs=in_specs, out_specs=out_specs, out_shape=out_shape,
                          scratch_shapes=scratch, compiler_params=_cp(*sem))(*args)


class ScatterComm:
    def __init__(self, groups):
        self.sizes = [len(g) for g in groups]
        self.rows = [[pc.shape[1] for pc in g] for g in groups]
        ng = len(groups)
        self.inputs = [pc for g in groups for pc in g]
        self.out_shapes = [jax.ShapeDtypeStruct((N_DEV, sum(self.rows[gi]), g[0].shape[2]), g[0].dtype) for gi, g in enumerate(groups)]
        self.sems = [pltpu.SemaphoreType.DMA((7 * ng,)), pltpu.SemaphoreType.DMA((7 * ng,)), pltpu.SemaphoreType.DMA((ng,))]

    def _peers(self):
        x, y, c = _me()
        out = []
        for k in range(1, N_DEV):
            px = 1 - x if k & 4 else x
            py = 1 - y if k & 2 else y
            pc = 1 - c if k & 1 else c
            out.append((k, 4 * px + 2 * py + pc, dict(device_id=(px, py, pc), device_id_type=MESH)))
        return 4 * x + 2 * y + c, out

    def start(self, ins, outs, sems):
        send_sems, recv_sems, local_sems = sems
        me, peers = self._peers()
        pos = 0
        for gi, size in enumerate(self.sizes):
            for i, pc in enumerate(ins[pos:pos + size]):
                dst = outs[gi].at[me, pl.ds(sum(self.rows[gi][:i]), self.rows[gi][i])]
                pltpu.make_async_copy(pc.at[me], dst, local_sems.at[gi]).start()
                for k, peer, kw in peers:
                    pltpu.make_async_remote_copy(src_ref=pc.at[peer], dst_ref=dst, send_sem=send_sems.at[7 * gi + k - 1],
                                                 recv_sem=recv_sems.at[7 * gi + k - 1], **kw).start()
            pos += size

    def finish(self, ins, outs, sems):
        send_sems, recv_sems, local_sems = sems
        me, peers = self._peers()
        whole = [pltpu.make_async_remote_copy(src_ref=outs[gi].at[peer], dst_ref=outs[gi].at[peer],
                                              send_sem=send_sems.at[7 * gi + k - 1], recv_sem=recv_sems.at[7 * gi + k - 1], **kw)
                 for gi in range(len(self.sizes)) for k, peer, kw in peers]
        for cp in whole:
            cp.wait_recv()
        for cp in whole:
            cp.wait_send()
        for gi in range(len(self.sizes)):
            pltpu.make_async_copy(outs[gi].at[me], outs[gi].at[me], local_sems.at[gi]).wait()


def sum_slots(recv, name, tr):
    n, R, C = recv.shape

    def body(r_ref, o_ref):
        acc = r_ref[0].astype(F32)
        for s in range(1, n):
            acc = acc + r_ref[s].astype(F32)
        o_ref[...] = acc

    return pl.pallas_call(
        body, grid=(R // tr,), name=name,
        in_specs=[pl.BlockSpec((n, tr, C), lambda i: (0, i, 0))], out_specs=pl.BlockSpec((tr, C), lambda i: (i, 0)),
        out_shape=jax.ShapeDtypeStruct((R, C), F32), compiler_params=_cp("parallel"),
    )(recv)


PACK_W, FLAT_W = 1024, 128
MAIN = [
    ("ffn1_w_gate", "col"), ("ffn1_w_up", "col"), ("ffn1_w_down", "row"),
    ("ffn2_w_gate", "col"), ("ffn2_w_up", "col"), ("ffn2_w_down", "row"),
    ("w_ssd_proj", "row"), ("w_mla_proj", "row"), ("w_out", "row"),
    ("w_xq", "row"), ("w_xk", "row"), ("w_xv", "row"), ("w_xo", "row"),
    ("w_uk", "col"), ("w_uv", "col"),
]
FLAT = [("w_in", "col"), ("w_uq", "col")]
BIG = MAIN + FLAT
SMALL = ["ffn1_pre_g", "ffn1_post_g", "mix_pre_g", "conv_b", "dt_bias", "a_log", "d_skip", "ssd_norm_g", "q_norm_g",
         "kv_norm_g", "gate_bias", "mix_post_g", "xa_pre_g", "mem_norm_g", "xa_post_g", "ffn2_pre_g", "ffn2_post_g"]
WEIGHTS = ['ffn1_pre_g', 'ffn1_w_gate', 'ffn1_w_up', 'ffn1_w_down', 'ffn1_post_g', 'mix_pre_g', 'w_in', 'conv_w', 'conv_b',
           'dt_bias', 'a_log', 'd_skip', 'ssd_norm_g', 'w_ssd_proj', 'q_norm_g', 'w_uq', 'kv_norm_g', 'w_uk', 'w_uv',
           'w_mla_proj', 'gate_bias', 'w_out', 'mix_post_g', 'xa_pre_g', 'mem_norm_g', 'w_xq', 'w_xk', 'w_xv', 'w_xo',
           'xa_post_g', 'ffn2_pre_g', 'ffn2_w_gate', 'ffn2_w_up', 'ffn2_w_down', 'ffn2_post_g']


def _pack_rows(w, kind, width):
    m = w[0].T if kind == "col" else w[0]
    return m.reshape(-1, width)


KIND = dict(BIG)
GATHER_PLAN = {
    "first": (["ffn1_w_gate", "ffn1_w_up", "ffn1_w_down"], []),
    "ffn1_gate_up": (["w_ssd_proj", "w_mla_proj", "w_out", "w_uk", "w_uv"], ["w_in", "w_uq"]),
    "attn_fwd": (["w_xq", "w_xk", "w_xv", "w_xo", "ffn2_w_gate", "ffn2_w_up", "ffn2_w_down"], []),
}
SCATTER_PLAN = {
    "attn_bwd": (["ffn2_w_gate", "ffn2_w_up", "ffn2_w_down", "w_xq", "w_xk", "w_xv", "w_xo"], []),
    "ssd_bwd": (["w_ssd_proj", "w_mla_proj", "w_out", "w_uk", "w_uv"], ["w_uq"]),
    "in_bwd": ([], ["w_in#0"]),
    "ffn1:down_bwd": ([], ["w_in#1"]),
    "ffn1:dwg": (["ffn1_w_down"], []),
    "ffn1:dwu": (["ffn1_w_gate"], []),
    "ffn1:gate_up_bwd": (["ffn1_w_up"], []),
}
PARTS = {"w_in#0": ("w_in", 0, 2656), "w_in#1": ("w_in", 2656, 5296)}


class Stage:
    def __init__(self, w):
        self.w = w
        self.width = {n: PACK_W if (n, k) in MAIN else FLAT_W for n, k in BIG}
        self.nrows = {n: math.prod(w[n].shape) // self.width[n] for n, _ in BIG}
        self.recv = {}

    def _shards(self, tag):
        names_main, names_flat = GATHER_PLAN[tag]
        pack = lambda n: _pack_rows(self.w[n], KIND[n], self.width[n]).astype(BF16)
        shards = []
        if names_main:
            shards.append(jnp.concatenate([pack(n) for n in names_main], axis=0))
        if names_flat:
            bits = lax.bitcast_convert_type(self.w["conv_w"][0], BF16).reshape(-1, FLAT_W)
            shards.append(_pad_rows(jnp.concatenate([pack(n) for n in names_flat] + [bits], axis=0), 16))
        return shards

    def gather(self, tag):
        return GatherComm(self._shards(tag))

    def gathered(self, tag, outs, W, p):
        names_main, names_flat = GATHER_PLAN[tag]
        outs = list(outs)
        for names in (names_main, names_flat):
            if not names:
                continue
            buf, r0 = outs.pop(0), 0
            for n in names:
                K = self.w[n].shape[1] if KIND[n] == "col" else PACK_W
                W[n] = buf[:, r0:r0 + self.nrows[n]].reshape(-1, K)
                r0 += self.nrows[n]
            if names is names_flat:
                cw = self.w["conv_w"]
                nbits = 2 * math.prod(cw.shape) // FLAT_W
                bits = buf[:, r0:r0 + nbits].reshape((N_DEV,) + cw.shape[1:] + (2,))
                p["conv_w"] = lax.bitcast_convert_type(bits, F32).transpose(1, 0, 2).reshape(cw.shape[1], -1)

    def scatter(self, tag, gw):
        if tag not in SCATTER_PLAN:
            return None
        def piece(n):
            if n in PARTS:
                base, r0, r1 = PARTS[n]
                return gw[base].reshape(N_DEV, self.nrows[base], self.width[base])[:, r0:r1]
            return gw[n].reshape(N_DEV, self.nrows[n], self.width[n])
        return ScatterComm([[piece(n) for n in names] for names in SCATTER_PLAN[tag] if names])

    def scattered(self, tag, outs):
        if tag in SCATTER_PLAN:
            self.recv[tag] = outs


def _pad_rows(a, mult):
    r = (-a.shape[0]) % mult
    return a if r == 0 else jnp.concatenate([a, jnp.zeros((r,) + a.shape[1:], a.dtype)], axis=0)


def _pack_small(vals, loss_row=None, conv_w=None):
    rows = []
    for v in vals:
        f = v.reshape(-1)
        f = jnp.concatenate([f, jnp.zeros(((-f.shape[0]) % 128,), F32)])
        rows.append(f.reshape(-1, 128))
    if conv_w is not None:
        rows.append(conv_w.reshape(-1, 128))
    if loss_row is not None:
        rows.append(loss_row)
    return _pad_rows(jnp.concatenate(rows, axis=0), 8)


def _unpack_small(buf, shapes):
    out, r = [], 0
    for shp in shapes:
        n = math.prod(shp)
        nr = -(-n // 128)
        out.append(buf[r:r + nr].reshape(-1)[:n].reshape(shp))
        r += nr
    return out, r


def _tn(a, b, name, out_dtype=BF16, comm=None):
    M, N = a.shape[1], b.shape[1]
    T = a.shape[0]
    tm = M if M <= 1536 else M // 2
    tk = 512 if T % 512 == 0 and T > 512 else None
    res = mm([[(a, b, "tn")]], [out_dtype], name, tm=tm, tn=N, tk=tk, comm=comm)
    return res[0] if comm is None else (res[0], res[1:])


class NoStage:
    def gather(self, tag):
        return None

    def gathered(self, tag, outs, W, p):
        pass

    def scatter(self, tag, gw):
        return None

    def scattered(self, tag, outs):
        pass


def _ffn_fwd(x, gpre, gpost, wg_t, wu_t, wd, tag, comm=None):
    h = rms_fwd(x, gpre, tag + "_pre")
    def swi(accs, ex):
        sil, dsil = _silu_parts(accs[0])
        return sil, accs[1] * dsil, sil * accs[1]
    res = mm([[(h, wg_t, "nt")], [(h, wu_t, "nt")]], [BF16, BF16, BF16], tag + "_gate_up", tn=256, epi=swi, comm=comm)
    G, U, A = res[:3]
    H, y = mm_resid(A, wd, x, gpost, FFN_RES, tag + "_down")
    return y, (x, h, G, U, A, H), res[3:]


def _ffn_bwd(dy, saved, gpre, gpost, wg_t, wu_t, wd, tag, stage, gw):
    x, h, G, U, A, H = saved
    dH, dgpost = resid_bwd(H, gpost, dy, FFN_RES, tag + "_post_bwd")

    def dswi(accs, ex):
        return accs[0] * ex[1], accs[0] * ex[0]

    def hosted(where, call):
        comm = stage.scatter(tag + ":" + where, gw)
        res = call(comm)
        if comm is None:
            return res
        stage.scattered(tag + ":" + where, res[1])
        return res[0]

    res = hosted("down_bwd", lambda comm: (lambda r: r if comm is None else (r[:2], r[2:]))(
        mm([[(dH, wd, "nt")]], [BF16, BF16], tag + "_down_bwd", tn=256, epi=dswi, extras=[G, U], comm=comm)))
    dG, dU = res
    gw[tag + "_w_down"] = _tn(A, dH, tag + "_dwd")
    gw[tag + "_w_gate"] = hosted("dwg", lambda comm: _tn(dG, h, tag + "_dwg", comm=comm))
    gw[tag + "_w_up"] = hosted("dwu", lambda comm: _tn(dU, h, tag + "_dwu", comm=comm))
    dh = hosted("gate_up_bwd", lambda comm: (lambda r: r[0] if comm is None else (r[0], r[1:]))(
        mm([[(dG, wg_t, "nn"), (dU, wu_t, "nn")]], [F32], tag + "_gate_up_bwd", tm=512, comm=comm)))
    dx, dgpre = rms_bwd(x, gpre, dh, tag + "_pre_bwd", resid=dy)
    return dx, dgpre, dgpost


def _rope_tables(positions):
    inv = ROPE_THETA ** (-jnp.arange(0, ROPE, 2, dtype=F32) / ROPE)
    ang = positions.astype(F32).reshape(-1)[:, None] * inv
    return jnp.cos(ang), jnp.sin(ang)


def _local_step(x, mem, positions, tgt, W, p, stage=None):
    stage = stage or NoStage()
    nseq = x.shape[0]
    T = nseq * x.shape[1]
    x0 = x.reshape(T, D)
    mem2 = mem.reshape(-1, D)
    cos, sin = _rope_tables(positions)

    x1, ffn1, arrived = _ffn_fwd(x0, p["ffn1_pre_g"], p["ffn1_post_g"], W["ffn1_w_gate"], W["ffn1_w_up"], W["ffn1_w_down"],
                                 "ffn1", comm=stage.gather("ffn1_gate_up"))
    stage.gathered("ffn1_gate_up", arrived, W, p)

    w_in_t = W["w_in"]
    bounds = [0]
    for n in (SSD_INNER, CONV_CH, SSD_H, QR, KVR, ROPE, 2 * D):
        bounds.append(bounds[-1] + n)
    wt_z, wt_xbc, wt_dt, wt_q, wt_kv, wt_kr, wt_gate = [w_in_t[bounds[i]:bounds[i + 1]] for i in range(7)]
    wt_dt, wt_kr = _pad_rows(wt_dt, SLOT), _pad_rows(wt_kr, SLOT)
    wt_dtkr = jnp.concatenate([wt_dt, wt_kr], axis=0)
    hm = rms_fwd(x1, p["mix_pre_g"], "mix_pre")
    z = mm1(hm, wt_z, "nt", BF16, "in_z")
    xbc = mm1(hm, wt_xbc, "nt", BF16, "in_xbc")
    q_c = mm1(hm, wt_q, "nt", F32, "in_q", tn=QR)
    kv_c = mm1(hm, wt_kv, "nt", F32, "in_kv")
    dtkr = mm1(hm, wt_dtkr, "nt", F32, "in_dtkr")
    gl = mm1(hm, wt_gate, "nt", BF16, "in_gate")

    xbc_act = conv_fwd(xbc, p["conv_w"], p["conv_b"], nseq)
    y_ssd_core, prev = ssd_fwd(xbc_act, dtkr, p["dt_bias"], p["a_log"], p["d_skip"], nseq)
    yn = gated_norm_fwd(y_ssd_core, z, p["ssd_norm_g"], "ssd_norm")
    y_ssd = mm1(yn, W["w_ssd_proj"], "nn", BF16, "ssd_proj")

    slot_rows = lambda wt, per: jnp.pad(wt.reshape(MLA_H, per, -1), ((0, 0), (0, SLOT - per), (0, 0))).reshape(MLA_H * SLOT, -1)
    wq_s, wk_s, wv_s = slot_rows(W["w_uq"], QK), slot_rows(W["w_uk"], NOPE), slot_rows(W["w_uv"], VD)
    wo_s = slot_rows(W["w_mla_proj"], VD)
    qn = rms_fwd(q_c, p["q_norm_g"], "q_norm")
    q_s = mm1(qn, wq_s, "nt", BF16, "uq")
    kvn = rms_fwd(kv_c, p["kv_norm_g"], "kv_norm")
    kn_s = mm1(kvn, wk_s, "nt", BF16, "uk")
    v_s = mm1(kvn, wv_s, "nt", BF16, "uv")
    cos16, sin16 = cos, sin
    Qc, Kc = rope_slot_fwd(q_s, kn_s, dtkr, cos16, sin16, "rope")
    o_s, lse, *arrived = attn_slot_fwd(Qc, Kc, v_s, nseq, comm=stage.gather("attn_fwd"))
    stage.gathered("attn_fwd", arrived, W, p)
    y_mla = mm1(o_s, wo_s, "nn", BF16, "mla_proj")

    merged = merge_fwd(gl, y_ssd, y_mla, p["gate_bias"], "merge")
    hmix, x2 = mm_resid(merged, W["w_out"], x1, p["mix_post_g"], 1.0, "mix_out")

    hq = rms_fwd(x2, p["xa_pre_g"], "xa_pre")
    mn = rms_fwd(mem2, p["mem_norm_g"], "mem_norm")
    xq = mm1(hq, W["w_xq"], "nn", BF16, "xq")
    xk = mm1(mn, W["w_xk"], "nn", BF16, "xk")
    xv = mm1(mn, W["w_xv"], "nn", BF16, "xv")
    xo = xattn_fwd(xq, xk, xv, nseq)
    ho, x3 = mm_resid(xo, W["w_xo"], x2, p["xa_post_g"], 1.0, "xo")

    x4, ffn2, _ = _ffn_fwd(x3, p["ffn2_pre_g"], p["ffn2_post_g"], W["ffn2_w_gate"], W["ffn2_w_up"], W["ffn2_w_down"], "ffn2")
    dx4, loss_row = loss_head(x4, tgt.reshape(T, D), "loss")

    gw, gs = {}, {}
    dx3, gs["ffn2_pre_g"], gs["ffn2_post_g"] = _ffn_bwd(
        dx4, ffn2, p["ffn2_pre_g"], p["ffn2_post_g"], W["ffn2_w_gate"], W["ffn2_w_up"], W["ffn2_w_down"], "ffn2", stage, gw)

    dho, gs["xa_post_g"] = resid_bwd(ho, p["xa_post_g"], dx3, 1.0, "xa_post_bwd")
    dxo = mm1(dho, W["w_xo"], "nt", BF16, "xo_bwd")
    gw["w_xo"] = _tn(xo, dho, "d_w_xo")
    dxq, dxk, dxv = xattn_bwd(xq, xk, xv, dxo, nseq)
    dhq = mm1(dxq, W["w_xq"], "nt", F32, "xq_bwd")
    gw["w_xq"] = _tn(hq, dxq, "d_w_xq")
    dmn = mm([[(dxk, W["w_xk"], "nt"), (dxv, W["w_xv"], "nt")]], [F32], "xkv_bwd")[0]
    gw["w_xk"] = _tn(mn, dxk, "d_w_xk")
    gw["w_xv"] = _tn(mn, dxv, "d_w_xv")
    _, gs["mem_norm_g"] = rms_bwd(mem2, p["mem_norm_g"], dmn, "mem_norm_bwd", dx_dtype=BF16)
    dx2, gs["xa_pre_g"] = rms_bwd(x2, p["xa_pre_g"], dhq, "xa_pre_bwd", resid=dx3)

    dhmix, gs["mix_post_g"] = resid_bwd(hmix, p["mix_post_g"], dx2, 1.0, "mix_post_bwd")
    dmerged = mm1(dhmix, W["w_out"], "nt", F32, "mix_out_bwd")
    gw["w_out"] = _tn(merged, dhmix, "d_w_out")
    dys, dym, dgl, gs["gate_bias"] = merge_bwd(gl, y_ssd, y_mla, dmerged, p["gate_bias"], "merge_bwd")

    unslot = lambda g, per: g.reshape(MLA_H, SLOT, -1)[:, :per].reshape(MLA_H * per, -1)
    do_s = mm1(dym, wo_s, "nt", BF16, "mla_proj_bwd")
    gw["w_mla_proj"] = unslot(_tn(o_s, dym, "d_w_mla_proj"), VD)
    dQc, dKc, dv_s, *sent = attn_slot_bwd(Qc, Kc, v_s, o_s, lse, do_s, nseq, comm=stage.scatter("attn_bwd", gw))
    stage.scattered("attn_bwd", sent)
    dq_s, dkn_s, dkr = rope_slot_bwd(dQc, dKc, cos16, sin16, "rope_bwd")
    dqn = mm1(dq_s, wq_s, "nn", F32, "uq_bwd", tn=QR)
    gw["w_uq"] = unslot(_tn(dq_s, qn, "d_w_uq"), QK)
    dq_c, gs["q_norm_g"] = rms_bwd(q_c, p["q_norm_g"], dqn, "q_norm_bwd", dx_dtype=BF16)
    dkvn = mm([[(dkn_s, wk_s, "nn"), (dv_s, wv_s, "nn")]], [F32], "ukv_bwd")[0]
    gw["w_uk"] = unslot(_tn(dkn_s, kvn, "d_w_uk"), NOPE)
    gw["w_uv"] = unslot(_tn(dv_s, kvn, "d_w_uv"), VD)
    dkv_c, gs["kv_norm_g"] = rms_bwd(kv_c, p["kv_norm_g"], dkvn, "kv_norm_bwd", dx_dtype=BF16)

    dyn = mm1(dys, W["w_ssd_proj"], "nt", F32, "ssd_proj_bwd")
    gw["w_ssd_proj"] = _tn(yn, dys, "d_w_ssd_proj")
    dyc, dz, gs["ssd_norm_g"] = gated_norm_bwd(y_ssd_core, z, dyn, p["ssd_norm_g"], "ssd_norm_bwd")
    dxbc_act, ddtr, gs["dt_bias"], gs["a_log"], gs["d_skip"], *sent = ssd_bwd(
        xbc_act, dtkr, p["dt_bias"], p["a_log"], p["d_skip"], prev, dyc, nseq, comm=stage.scatter("ssd_bwd", gw))
    stage.scattered("ssd_bwd", sent)
    dxbc, gs["conv_w"], gs["conv_b"] = conv_bwd(xbc, p["conv_w"], p["conv_b"], dxbc_act, nseq)

    gw["w_in"] = jnp.concatenate([_tn(dz, hm, "d_w_in_z"), _tn(dxbc, hm, "d_w_in_xbc"), _tn(ddtr, hm, "d_w_in_dt")[:SSD_H],
                                  _tn(dq_c, hm, "d_w_in_q"), _tn(dkv_c, hm, "d_w_in_kv"), _tn(dkr, hm, "d_w_in_kr")[:ROPE],
                                  _tn(dgl, hm, "d_w_in_gate")], axis=0)
    dhm, *sent = mm([[(dz, wt_z, "nn"), (dxbc, wt_xbc, "nn"), (ddtr, wt_dt, "nn"), (dq_c, wt_q, "nn"), (dkv_c, wt_kv, "nn"),
                      (dkr, wt_kr, "nn"), (dgl, wt_gate, "nn")]], [F32], "in_bwd", tm=512, comm=stage.scatter("in_bwd", gw))
    stage.scattered("in_bwd", sent)
    dx1, gs["mix_pre_g"] = rms_bwd(x1, p["mix_pre_g"], dhm, "mix_pre_bwd", resid=dx2)

    dx0, gs["ffn1_pre_g"], gs["ffn1_post_g"] = _ffn_bwd(
        dx1, ffn1, p["ffn1_pre_g"], p["ffn1_post_g"], W["ffn1_w_gate"], W["ffn1_w_up"], W["ffn1_w_down"], "ffn1", stage, gw)
    return loss_row, dx0.reshape(x.shape), gw, gs


def kernel(x, mem, positions, ffn1_pre_g, ffn1_w_gate, ffn1_w_up, ffn1_w_down, ffn1_post_g, mix_pre_g, w_in, conv_w, conv_b, dt_bias, a_log, d_skip, ssd_norm_g, w_ssd_proj, q_norm_g, w_uq, kv_norm_g, w_uk, w_uv, w_mla_proj, gate_bias, w_out, mix_post_g, xa_pre_g, mem_norm_g, w_xq, w_xk, w_xv, w_xo, xa_post_g, ffn2_pre_g, ffn2_w_gate, ffn2_w_up, ffn2_w_down, ffn2_post_g, loss_target, m_ffn1_pre_g, m_ffn1_w_gate, m_ffn1_w_up, m_ffn1_w_down, m_ffn1_post_g, m_mix_pre_g, m_w_in, m_conv_w, m_conv_b, m_dt_bias, m_a_log, m_d_skip, m_ssd_norm_g, m_w_ssd_proj, m_q_norm_g, m_w_uq, m_kv_norm_g, m_w_uk, m_w_uv, m_w_mla_proj, m_gate_bias, m_w_out, m_mix_post_g, m_xa_pre_g, m_mem_norm_g, m_w_xq, m_w_xk, m_w_xv, m_w_xo, m_xa_post_g, m_ffn2_pre_g, m_ffn2_w_gate, m_ffn2_w_up, m_ffn2_w_down, m_ffn2_post_g, v_ffn1_pre_g, v_ffn1_w_gate, v_ffn1_w_up, v_ffn1_w_down, v_ffn1_post_g, v_mix_pre_g, v_w_in, v_conv_w, v_conv_b, v_dt_bias, v_a_log, v_d_skip, v_ssd_norm_g, v_w_ssd_proj, v_q_norm_g, v_w_uq, v_kv_norm_g, v_w_uk, v_w_uv, v_w_mla_proj, v_gate_bias, v_w_out, v_mix_post_g, v_xa_pre_g, v_mem_norm_g, v_w_xq, v_w_xk, v_w_xv, v_w_xo, v_xa_post_g, v_ffn2_pre_g, v_ffn2_w_gate, v_ffn2_w_up, v_ffn2_w_down, v_ffn2_post_g):
    a = dict(locals())
    w = {n: a[n] for n in WEIGHTS}
    m = {n: a["m_" + n] for n in WEIGHTS}
    v = {n: a["v_" + n] for n in WEIGHTS}

    stage = Stage(w)
    W, p = {}, {n: w[n] for n in SMALL}
    stage.gathered("first", run_comm(stage.gather("first"), "allgather_first"), W, p)

    loss_row, grad_x, gw, gs = _local_step(x, mem, positions, loss_target, W, p, stage)

    sm = _pack_small([gs[n] for n in SMALL], loss_row=loss_row, conv_w=gs["conv_w"])
    srecv, = run_comm(ScatterComm([[jnp.broadcast_to(sm[None], (N_DEV,) + sm.shape)]]), "exchange_small")
    s_rows = sum_slots(srecv, "sum_small", tr=sm.shape[0])
    summed = {}
    for tag, (names_main, names_flat) in SCATTER_PLAN.items():
        bufs = list(stage.recv[tag])
        for names, whole in ((names_main, False), (names_flat, True)):
            if not names:
                continue
            buf = bufs.pop(0)
            rows = buf.shape[1]
            g_rows = sum_slots(buf, "sum_" + tag.replace(":", "_") + ("_flat" if whole else ""),
                               tr=rows if whole else _tile(rows, (256, 224, 176, 128, 64, 32, 16)))
            r0 = 0
            for n in names:
                nr = PARTS[n][2] - PARTS[n][1] if n in PARTS else stage.nrows[n]
                summed[n] = g_rows[r0:r0 + nr]
                r0 += nr
    for base in {b for b, _, _ in PARTS.values()}:
        summed[base] = jnp.concatenate([summed[pn] for pn in sorted(PARTS) if PARTS[pn][0] == base], axis=0)
    grads = {}
    for n, kind in BIG:
        blk = summed[n]
        grads[n] = (blk.reshape(w[n].shape[2], w[n].shape[1]).T if kind == "col" else blk)[None]
    conv_w_full = p["conv_w"]
    small_g, r1 = _unpack_small(s_rows, [w[n].shape for n in SMALL])
    for n, g in zip(SMALL, small_g):
        grads[n] = g
    ncw = math.prod(conv_w_full.shape) // 128
    cw_grad_full = s_rows[r1:r1 + ncw].reshape(conv_w_full.shape)
    wsh = conv_w.shape[2]
    grads["conv_w"] = lax.dynamic_slice_in_dim(cw_grad_full, _dev_index() * wsh, wsh, axis=1)[None]
    loss = s_rows[r1 + ncw, 0]

    delta, new_m, new_v = {}, {}, {}
    for n, _ in BIG + [("conv_w", "col")]:
        shp = w[n].shape
        d_, m_, v_ = adamw(w[n][0], grads[n][0], m[n][0], v[n][0], "adamw_" + n)
        delta[n], new_m[n], new_v[n] = d_.reshape(shp), m_.reshape(shp), v_.reshape(shp)
    sp = [_pack_small([t[n] for n in SMALL]) for t in (w, grads, m, v)]
    outs = adamw(sp[0], sp[1], sp[2], sp[3], "adamw_small")
    for t, buf in zip((delta, new_m, new_v), outs):
        vals, _ = _unpack_small(buf, [w[n].shape for n in SMALL])
        for n, val in zip(SMALL, vals):
            t[n] = val
    return (loss, grad_x, *[grads[n] for n in WEIGHTS], *[delta[n] for n in WEIGHTS],
            *[new_m[n] for n in WEIGHTS], *[new_v[n] for n in WEIGHTS])
```

```python
import functools
import math

import jax
import jax.numpy as jnp
from jax import lax
from jax.experimental import pallas as pl
from jax.experimental.pallas import tpu as pltpu

F32, BF16 = jnp.float32, jnp.bfloat16
HI = lax.Precision.HIGHEST
MESH = pl.DeviceIdType.MESH
N_DEV = 8

D = 1024
DFF = 2816
SSD_H, SSD_P, SSD_G, SSD_N, SSD_L = 16, 64, 2, 128, 128
SSD_INNER = SSD_H * SSD_P
CONV_K, CONV_CH = 4, 1536
MLA_H, QR, KVR, NOPE, ROPE, VD = 16, 384, 256, 64, 32, 64
QK = NOPE + ROPE
ROPE_THETA = 10000.0
XA_H, XA_D = 4, 256
EPS = 1e-6
FFN_RES = 0.5
LR, B1, B2, AEPS, WD, STEP = 0.001, 0.9, 0.999, 1e-08, 0.01, 10

VMEM_LIMIT = 56 * 2**20


def _cp(*sem):
    return pltpu.CompilerParams(dimension_semantics=sem, vmem_limit_bytes=VMEM_LIMIT)


def _sigmoid(x):
    return 1.0 / (1.0 + jnp.exp(-x))


def _softplus(x):
    return jnp.where(x > 20.0, x, jnp.log(1.0 + jnp.exp(jnp.minimum(x, 20.0))))


def _dot(a, b, dims="nn", hi=False):
    ca = 0 if dims[0] == "t" else 1
    cb = 1 if dims[1] == "t" else 0
    if hi:
        return lax.dot_general(a, b, (((ca,), (cb,)), ((), ())), precision=HI, preferred_element_type=F32)
    return lax.dot_general(a.astype(BF16), b.astype(BF16), (((ca,), (cb,)), ((), ())), preferred_element_type=F32)


def _ssd_common(dtr, dtb, alog):
    L = dtr.shape[0]
    dt = _softplus(dtr + dtb)
    a = -jnp.exp(alog)
    adt = dt * a
    r = lax.broadcasted_iota(jnp.int32, (L, L), 0)
    c = lax.broadcasted_iota(jnp.int32, (L, L), 1)
    lower = r >= c
    tri = lower.astype(F32)
    cs = _dot(tri, adt, "nn", hi=True)
    cs_t = _dot(adt, tri, "tt", hi=True)
    return dt, a, cs, cs_t, lower


def _head_expand():
    hh = lax.broadcasted_iota(jnp.int32, (SSD_H, SSD_INNER), 0)
    jj = lax.broadcasted_iota(jnp.int32, (SSD_H, SSD_INNER), 1)
    return ((jj >= hh * SSD_P) & (jj < hh * SSD_P + SSD_P)).astype(F32)


def _head_reduce():
    hh = lax.broadcasted_iota(jnp.int32, (SSD_INNER, SSD_H), 1)
    jj = lax.broadcasted_iota(jnp.int32, (SSD_INNER, SSD_H), 0)
    return ((jj >= hh * SSD_P) & (jj < hh * SSD_P + SSD_P)).astype(F32)


def ssd_fwd(xbc, dtr, dtb, alog, dsk, nseq):
    T = xbc.shape[0]
    S = T // nseq
    C = S // SSD_L
    L = SSD_L
    NP = SSD_H // 2

    def body(x_ref, b_ref, c_ref, dtr_ref, dtb_ref, alog_ref, dsk_ref, y_ref, prev_ref, st_ref):
        ci = pl.program_id(1)

        @pl.when(ci == 0)
        def _():
            st_ref[...] = jnp.zeros_like(st_ref)

        dt, a, cs, cs_t, lower = _ssd_common(dtr_ref[:, 0:SSD_H], dtb_ref[...], alog_ref[...])
        E = _head_expand()
        X = x_ref[...].astype(F32)
        dt_e = _dot(dt, E, hi=True)
        cs_e = _dot(cs, E, hi=True)
        csl_e = cs_e[L - 1:L, :]
        Xd = X * dt_e
        Xf = Xd * jnp.exp(csl_e - cs_e)
        e_e = jnp.exp(cs_e)
        skip = _dot(dsk_ref[...], E, hi=True) * X
        lane = lax.broadcasted_iota(jnp.int32, (1, 2 * SSD_P), 1)
        rowp = lax.broadcasted_iota(jnp.int32, (2 * SSD_P, 1), 0)
        for g in range(SSD_G):
            Bg = b_ref[:, g * SSD_N:(g + 1) * SSD_N]
            Cg = c_ref[:, g * SSD_N:(g + 1) * SSD_N]
            cb = _dot(Cg, Bg, "nt")
            for pp in range(NP // SSD_G):
                p = g * (NP // SSD_G) + pp
                sl = slice(p * 2 * SSD_P, (p + 1) * 2 * SSD_P)
                Xd_p = Xd[:, sl]
                yd = jnp.zeros((L, 2 * SSD_P), F32)
                for q in range(2):
                    h = 2 * p + q
                    m = jnp.where(lower, jnp.exp(jnp.minimum(cs[:, h:h + 1] - cs_t[h:h + 1, :], 0.0)), 0.0)
                    mask = (lane >= q * SSD_P) & (lane < (q + 1) * SSD_P)
                    yd = yd + _dot(cb * m, jnp.where(mask, Xd_p, 0.0))
                S0 = st_ref[p]
                prev_ref[0, 0, p] = S0
                z = _dot(Cg, S0, "nt")
                y_ref[:, sl] = (skip[:, sl] + yd + z * e_e[:, sl]).astype(y_ref.dtype)
                h0 = 2 * p
                dec = jnp.where(rowp < SSD_P, jnp.exp(cs[L - 1:L, h0:h0 + 1]), jnp.exp(cs[L - 1:L, h0 + 1:h0 + 2]))
                st_ref[p] = S0 * dec + _dot(Xf[:, sl], Bg, "tn")

    row = lambda b, c: (b * C + c, 0)
    return pl.pallas_call(
        body, grid=(nseq, C), name="ssd_fwd",
        in_specs=[pl.BlockSpec((L, SSD_INNER), row),
                  pl.BlockSpec((L, SSD_G * SSD_N), lambda b, c: (b * C + c, SSD_INNER // (SSD_G * SSD_N))),
                  pl.BlockSpec((L, SSD_G * SSD_N), lambda b, c: (b * C + c, SSD_INNER // (SSD_G * SSD_N) + 1)),
                  pl.BlockSpec((L, 128), row),
                  pl.BlockSpec((1, SSD_H), lambda b, c: (0, 0)),
                  pl.BlockSpec((1, SSD_H), lambda b, c: (0, 0)),
                  pl.BlockSpec((1, SSD_H), lambda b, c: (0, 0))],
        out_specs=[pl.BlockSpec((L, SSD_INNER), row),
                   pl.BlockSpec((1, 1, NP, 2 * SSD_P, SSD_N), lambda b, c: (b, c, 0, 0, 0))],
        out_shape=[jax.ShapeDtypeStruct((T, SSD_INNER), BF16),
                   jax.ShapeDtypeStruct((nseq, C, NP, 2 * SSD_P, SSD_N), F32)],
        scratch_shapes=[pltpu.VMEM((NP, 2 * SSD_P, SSD_N), F32)],
        compiler_params=_cp("parallel", "arbitrary"),
    )(xbc, xbc, xbc, dtr, dtb, alog, dsk)


def ssd_bwd(xbc, dtr, dtb, alog, dsk, prev, dy, nseq, comm=None):
    T = xbc.shape[0]
    S = T // nseq
    C = S // SSD_L
    L = SSD_L
    NP = SSD_H // 2

    def body(x_ref, b_ref, c_ref, dtr_ref, dtb_ref, alog_ref, dsk_ref, prev_ref, dy_ref,
             dxbc_ref, ddtr_ref, ddtb_ref, dalog_ref, ddsk_ref, ds_ref, stg_ref):
        bi = pl.program_id(0)
        ci = pl.program_id(1)

        @pl.when(ci == 0)
        def _():
            ds_ref[...] = jnp.zeros_like(ds_ref)

        @pl.when((ci == 0) & (bi == 0))
        def _():
            ddtb_ref[...] = jnp.zeros_like(ddtb_ref)
            dalog_ref[...] = jnp.zeros_like(dalog_ref)
            ddsk_ref[...] = jnp.zeros_like(ddsk_ref)

        dtr = dtr_ref[:, 0:SSD_H]
        dtb = dtb_ref[...]
        dt, a, cs, cs_t, lower = _ssd_common(dtr, dtb, alog_ref[...])
        upper = lax.broadcasted_iota(jnp.int32, (L, L), 1) >= lax.broadcasted_iota(jnp.int32, (L, L), 0)
        E = _head_expand()
        ET = _head_reduce()
        X = x_ref[...].astype(F32)
        dY = dy_ref[...].astype(F32)
        dt_e = _dot(dt, E, hi=True)
        cs_e = _dot(cs, E, hi=True)
        csl_e = cs_e[L - 1:L, :]
        f_e = jnp.exp(csl_e - cs_e)
        e_e = jnp.exp(cs_e)
        dsk_e = _dot(dsk_ref[...], E, hi=True)
        Xd = X * dt_e
        Xf = Xd * f_e
        lane = lax.broadcasted_iota(jnp.int32, (1, 2 * SSD_P), 1)
        rowp = lax.broadcasted_iota(jnp.int32, (2 * SSD_P, 1), 0)
        hsel = lax.broadcasted_iota(jnp.int32, (1, SSD_H), 1)
        dcs = jnp.zeros((L, SSD_H), F32)
        dcsl = jnp.zeros((1, SSD_H), F32)
        for g in range(SSD_G):
            Bg = b_ref[:, g * SSD_N:(g + 1) * SSD_N]
            Cg = c_ref[:, g * SSD_N:(g + 1) * SSD_N]
            cb = _dot(Cg, Bg, "nt")
            cbt = _dot(Bg, Cg, "nt")
            dB = jnp.zeros((L, SSD_N), F32)
            dC = jnp.zeros((L, SSD_N), F32)
            for pp in range(NP // SSD_G):
                p = g * (NP // SSD_G) + pp
                sl = slice(p * 2 * SSD_P, (p + 1) * 2 * SSD_P)
                Xd_p = Xd[:, sl]
                dY_p = dY[:, sl]
                dXd_p = jnp.zeros((L, 2 * SSD_P), F32)
                for q in range(2):
                    h = 2 * p + q
                    mask = (lane >= q * SSD_P) & (lane < (q + 1) * SSD_P)
                    col = cs[:, h:h + 1]
                    rw = cs_t[h:h + 1, :]
                    m = jnp.where(lower, jnp.exp(jnp.minimum(col - rw, 0.0)), 0.0)
                    mt = jnp.where(upper, jnp.exp(jnp.minimum(rw - col, 0.0)), 0.0)
                    dYm = jnp.where(mask, dY_p, 0.0)
                    dW = _dot(dYm, Xd_p, "nt")
                    dWt = _dot(Xd_p, dYm, "nt")
                    w = cb * m
                    wt = cbt * mt
                    dC = dC + _dot(dW * m, Bg)
                    dB = dB + _dot(dWt * mt, Cg)
                    dXd_p = dXd_p + jnp.where(mask, _dot(wt, dY_p), 0.0)
                    qcol = jnp.sum(dW * w, axis=1, keepdims=True) - jnp.sum(dWt * wt, axis=1, keepdims=True)
                    dcs = dcs + qcol * (hsel == h).astype(F32)
                S0 = prev_ref[0, 0, p]
                dSn = ds_ref[p]
                dZ = dY_p * e_e[:, sl]
                dC = dC + _dot(dZ, S0)
                h0 = 2 * p
                el0 = jnp.exp(cs[L - 1:L, h0:h0 + 1])
                el1 = jnp.exp(cs[L - 1:L, h0 + 1:h0 + 2])
                dec = jnp.where(rowp < SSD_P, el0, el1)
                ds_ref[p] = dSn * dec + _dot(dZ, Cg, "tn")
                dXf_p = _dot(Bg, dSn, "nt")
                dB = dB + _dot(Xf[:, sl], dSn)
                rs = jnp.sum(dSn * S0, axis=1, keepdims=True)
                s0 = jnp.sum(jnp.where(rowp < SSD_P, rs, 0.0), axis=0, keepdims=True) * el0
                s1 = jnp.sum(jnp.where(rowp >= SSD_P, rs, 0.0), axis=0, keepdims=True) * el1
                dcsl = dcsl + s0 * (hsel == h0).astype(F32) + s1 * (hsel == h0 + 1).astype(F32)
                y_off = _dot(Cg, S0, "nt") * e_e[:, sl]
                t1 = dY_p * y_off - dXf_p * Xf[:, sl]
                r1 = jnp.where(lane < SSD_P, t1, 0.0)
                c0 = jnp.sum(r1, axis=1, keepdims=True)
                c1 = jnp.sum(t1 - r1, axis=1, keepdims=True)
                dcs = dcs + c0 * (hsel == h0).astype(F32) + c1 * (hsel == h0 + 1).astype(F32)
                t2 = dXf_p * Xf[:, sl]
                r2 = jnp.where(lane < SSD_P, t2, 0.0)
                dcsl = dcsl + jnp.sum(r2, keepdims=True) * (hsel == h0).astype(F32) \
                    + jnp.sum(t2 - r2, keepdims=True) * (hsel == h0 + 1).astype(F32)
                stg_ref[:, sl] = dXd_p + dXf_p * f_e[:, sl]
            dxbc_ref[:, SSD_INNER + g * SSD_N:SSD_INNER + (g + 1) * SSD_N] = dB.astype(dxbc_ref.dtype)
            dxbc_ref[:, SSD_INNER + (SSD_G + g) * SSD_N:SSD_INNER + (SSD_G + g + 1) * SSD_N] = dC.astype(dxbc_ref.dtype)
        dXd = stg_ref[...]
        dxbc_ref[:, 0:SSD_INNER] = (dXd * dt_e + dsk_e * dY).astype(dxbc_ref.dtype)
        rowl = lax.broadcasted_iota(jnp.int32, (L, 1), 0)
        dcs = dcs + jnp.where(rowl == L - 1, dcsl, 0.0)
        dalpha = _dot(upper.astype(F32), dcs, hi=True)
        ddt = _dot(dXd * X, ET, hi=True) + dalpha * a
        dalog_ref[...] += jnp.sum(dalpha * dt, axis=0, keepdims=True) * a
        ddtr = ddt * _sigmoid(dtr + dtb)
        spread = (lax.broadcasted_iota(jnp.int32, (SSD_H, 128), 0) == lax.broadcasted_iota(jnp.int32, (SSD_H, 128), 1)).astype(F32)
        ddtr_ref[...] = _dot(ddtr, spread, hi=True).astype(ddtr_ref.dtype)
        ddtb_ref[...] += jnp.sum(ddtr, axis=0, keepdims=True)
        ddsk_ref[...] += jnp.sum(_dot(dY * X, ET, hi=True), axis=0, keepdims=True)

    rowr = lambda b, c: (b * C + (C - 1 - c), 0)
    small = pl.BlockSpec((1, SSD_H), lambda b, c: (0, 0))
    return _call_with_comm(
        body, (nseq, C), "ssd_bwd",
        [pl.BlockSpec((L, SSD_INNER), rowr),
         pl.BlockSpec((L, SSD_G * SSD_N), lambda b, c: (b * C + (C - 1 - c), SSD_INNER // (SSD_G * SSD_N))),
         pl.BlockSpec((L, SSD_G * SSD_N), lambda b, c: (b * C + (C - 1 - c), SSD_INNER // (SSD_G * SSD_N) + 1)),
         pl.BlockSpec((L, 128), rowr), small, small, small,
         pl.BlockSpec((1, 1, NP, 2 * SSD_P, SSD_N), lambda b, c: (b, C - 1 - c, 0, 0, 0)),
         pl.BlockSpec((L, SSD_INNER), rowr)],
        [xbc, xbc, xbc, dtr, dtb, alog, dsk, prev, dy],
        [pl.BlockSpec((L, CONV_CH), rowr), pl.BlockSpec((L, 128), rowr), small, small, small],
        [jax.ShapeDtypeStruct((T, CONV_CH), BF16), jax.ShapeDtypeStruct((T, 128), BF16),
         jax.ShapeDtypeStruct((1, SSD_H), F32), jax.ShapeDtypeStruct((1, SSD_H), F32), jax.ShapeDtypeStruct((1, SSD_H), F32)],
        comm, scratch=[pltpu.VMEM((NP, 2 * SSD_P, SSD_N), F32), pltpu.VMEM((L, SSD_INNER), F32)], sem=("arbitrary", "arbitrary"))


SLOT = 128
ATT_T = 512


def _col_to_row(col):
    n = col.shape[0]
    eye = lax.broadcasted_iota(jnp.int32, (n, n), 0) == lax.broadcasted_iota(jnp.int32, (n, n), 1)
    return jnp.sum(jnp.where(eye, col, 0.0), axis=0, keepdims=True)


def attn_slot_fwd(q, k, v, nseq, comm=None):
    T = q.shape[0]
    S = T // nseq
    t = min(ATT_T, S)
    nb = S // t
    scale = QK ** -0.5

    def body(q_ref, k_ref, v_ref, o_ref, lse_ref):
        causal = lax.broadcasted_iota(jnp.int32, (t, t), 1) <= lax.broadcasted_iota(jnp.int32, (t, t), 0)
        for qi in range(nb):
            qb = q_ref[qi * t:(qi + 1) * t, :]
            m = l = acc = None
            for kj in range(qi + 1):
                s = _dot(qb, k_ref[kj * t:(kj + 1) * t, :], "nt") * scale
                if kj == qi:
                    s = jnp.where(causal, s, -1e30)
                bm = jnp.max(s, axis=1, keepdims=True)
                if kj == 0:
                    m = bm
                    p = jnp.exp(s - m)
                    l = jnp.sum(p, axis=1, keepdims=True)
                    acc = _dot(p, v_ref[0:t, :])
                else:
                    m_new = jnp.maximum(m, bm)
                    corr = jnp.exp(m - m_new)
                    p = jnp.exp(s - m_new)
                    l = l * corr + jnp.sum(p, axis=1, keepdims=True)
                    acc = acc * corr + _dot(p, v_ref[kj * t:(kj + 1) * t, :])
                    m = m_new
            o_ref[qi * t:(qi + 1) * t, :] = (acc / l).astype(o_ref.dtype)
            lse_ref[0, 0, :, qi * t:(qi + 1) * t] = _col_to_row(m + jnp.log(l))

    blk = pl.BlockSpec((S, SLOT), lambda b, h: (b, h))
    return _call_with_comm(
        body, (nseq, MLA_H), "attn_fwd", [blk, blk, blk], [q, k, v],
        [blk, pl.BlockSpec((1, 1, 1, S), lambda b, h: (b, h, 0, 0))],
        [jax.ShapeDtypeStruct((T, MLA_H * SLOT), BF16), jax.ShapeDtypeStruct((nseq, MLA_H, 1, S), F32)], comm)


def attn_slot_bwd(q, k, v, o, lse, do, nseq, comm=None):
    T = q.shape[0]
    S = T // nseq
    t = min(ATT_T, S)
    nb = S // t
    scale = QK ** -0.5

    def body(q_ref, k_ref, v_ref, o_ref, lse_ref, do_ref, dq_ref, dk_ref, dv_ref, dqa_ref):
        causal_t =lax.broadcasted_iota(jnp.int32, (t, t), 0) <= lax.broadcasted_iota(jnp.int32, (t, t), 1)
        ones = jnp.ones((8, SLOT), F32)
        delta = []
        for qi in range(nb):
            sl = slice(qi * t, (qi + 1) * t)
            prod = do_ref[sl, :].astype(F32) * o_ref[sl, :].astype(F32)
            delta.append(_dot(ones, prod, "nt", hi=True)[0:1, :])
        for kj in range(nb):
            ks = slice(kj * t, (kj + 1) * t)
            kb = k_ref[ks, :]
            vb = v_ref[ks, :]
            dk = dv = None
            for qi in range(kj, nb):
                sl = slice(qi * t, (qi + 1) * t)
                qb = q_ref[sl, :]
                dob = do_ref[sl, :]
                st = _dot(kb, qb, "nt") * scale
                pt = jnp.exp(st - lse_ref[0, 0, :, sl])
                if qi == kj:
                    pt = jnp.where(causal_t, pt, 0.0)
                dpt = _dot(vb, dob, "nt")
                dst = (pt * (dpt - delta[qi]) * scale).astype(BF16)
                dvc = _dot(pt, dob)
                dkc = _dot(dst, qb)
                dv = dvc if dv is None else dv + dvc
                dk = dkc if dk is None else dk + dkc
                dqc = _dot(dst, kb, "tn")
                if kj > 0:
                    dqc = dqc + dqa_ref[sl, :]
                if qi == kj:
                    dq_ref[sl, :] = dqc.astype(dq_ref.dtype)
                else:
                    dqa_ref[sl, :] = dqc
            dk_ref[ks, :] = dk.astype(dk_ref.dtype)
            dv_ref[ks, :] = dv.astype(dv_ref.dtype)

    blk = pl.BlockSpec((S, SLOT), lambda b, h: (b, h))
    lse_spec = pl.BlockSpec((1, 1, 1, S), lambda b, h: (b, h, 0, 0))
    W = MLA_H * SLOT
    return _call_with_comm(
        body, (nseq, MLA_H), "attn_bwd", [blk, blk, blk, blk, lse_spec, blk], [q, k, v, o, lse, do], [blk, blk, blk],
        [jax.ShapeDtypeStruct((T, W), BF16)] * 3, comm, scratch=[pltpu.VMEM((S, SLOT), F32)])


def _rope_coeffs(cos, sin):
    half = ROPE // 2
    r = lax.broadcasted_iota(jnp.int32, (half, SLOT), 0)
    c = lax.broadcasted_iota(jnp.int32, (half, SLOT), 1)
    pc = ((c == r + NOPE) | (c == r + NOPE + half)).astype(F32)
    ps = (c == r + NOPE + half).astype(F32) - (c == r + NOPE).astype(F32)
    lane = lax.broadcasted_iota(jnp.int32, (1, SLOT), 1)
    return _dot(cos, pc, hi=True) + (lane < NOPE).astype(F32), _dot(sin, ps, hi=True)


def _rope_swap(x):
    W = x.shape[1]
    half = ROPE // 2
    lane = lax.broadcasted_iota(jnp.int32, (1, W), 1) & (SLOT - 1)
    up = pltpu.roll(x, W - half, axis=1)
    dn = pltpu.roll(x, half, axis=1)
    return jnp.where((lane >= NOPE) & (lane < NOPE + half), up, jnp.where((lane >= NOPE + half) & (lane < QK), dn, 0.0))


def rope_slot_fwd(q, kn, dtkr, cos, sin, name):
    def fn(qv, knv, krv, cv, sv):
        C, Sg = _rope_coeffs(cv, sv)
        ct, stl = jnp.tile(C, (1, MLA_H)), jnp.tile(Sg, (1, MLA_H))
        qo = qv * ct + _rope_swap(qv) * stl
        r = lax.broadcasted_iota(jnp.int32, (SLOT, SLOT), 0)
        c = lax.broadcasted_iota(jnp.int32, (SLOT, SLOT), 1)
        place = ((c == r + NOPE) & (r < ROPE)).astype(F32)
        kr = _dot(krv, place, hi=True)
        kr = kr * C + _rope_swap(kr) * Sg
        return qo, knv.astype(F32) + jnp.tile(kr, (1, MLA_H))
    W = MLA_H * SLOT
    return rowwise(fn, [q, kn, (dtkr, SLOT, 1), cos, sin], [], [(W, BF16), (W, BF16)], [], name)


def rope_slot_bwd(dq, dk, cos, sin, name):
    def fn(dqv, dkv, cv, sv):
        C, Sg = _rope_coeffs(cv, sv)
        ct, stl = jnp.tile(C, (1, MLA_H)), jnp.tile(Sg, (1, MLA_H))
        dqo = dqv * ct - _rope_swap(dqv) * stl
        tot = dkv[:, 0:SLOT]
        for h in range(1, MLA_H):
            tot = tot + dkv[:, h * SLOT:(h + 1) * SLOT]
        u = tot * C - _rope_swap(tot) * Sg
        r = lax.broadcasted_iota(jnp.int32, (SLOT, SLOT), 0)
        c = lax.broadcasted_iota(jnp.int32, (SLOT, SLOT), 1)
        unplace = ((r == c + NOPE) & (c < ROPE)).astype(F32)
        return dqo, dkv, _dot(u, unplace, hi=True)
    W = MLA_H * SLOT
    return rowwise(fn, [dq, dk, cos, sin], [], [(W, BF16), (W, BF16), (SLOT, BF16)], [], name)


XA_BLK = 512


def xattn_fwd(q, k, v, nseq):
    T = q.shape[0]
    S = T // nseq
    M = k.shape[0] // nseq
    tq = min(XA_BLK, S)
    nq = S // tq
    scale = XA_D ** -0.5

    def body(q_ref, k_ref, v_ref, o_ref):
        s = _dot(q_ref[...], k_ref[...], "nt") * scale
        p = jnp.exp(s - jnp.max(s, axis=1, keepdims=True))
        p = p / jnp.sum(p, axis=1, keepdims=True)
        o_ref[...] = _dot(p, v_ref[...]).astype(o_ref.dtype)

    qs = pl.BlockSpec((tq, XA_D), lambda b, h, i: (b * nq + i, h))
    ks = pl.BlockSpec((M, XA_D), lambda b, h, i: (b, h))
    return pl.pallas_call(
        body, grid=(nseq, XA_H, nq), name="xattn_fwd", in_specs=[qs, ks, ks], out_specs=qs,
        out_shape=jax.ShapeDtypeStruct((T, XA_H * XA_D), BF16),
        compiler_params=_cp("parallel", "parallel", "parallel"),
    )(q, k, v)


def xattn_bwd(q, k, v, do, nseq):
    T = q.shape[0]
    S = T // nseq
    M = k.shape[0] // nseq
    tq = min(XA_BLK, S)
    nq = S // tq
    scale = XA_D ** -0.5

    def body(q_ref, k_ref, v_ref, do_ref, dq_ref, dk_ref, dv_ref):
        @pl.when(pl.program_id(2) == 0)
        def _():
            dk_ref[...] = jnp.zeros_like(dk_ref)
            dv_ref[...] = jnp.zeros_like(dv_ref)

        qb, kb, vb, dob = q_ref[...], k_ref[...], v_ref[...], do_ref[...]
        s = _dot(qb, kb, "nt") * scale
        p = jnp.exp(s - jnp.max(s, axis=1, keepdims=True))
        p = p / jnp.sum(p, axis=1, keepdims=True)
        dp = _dot(dob, vb, "nt")
        ds = p * (dp - jnp.sum(dp * p, axis=1, keepdims=True)) * scale
        dq_ref[...] = _dot(ds, kb).astype(dq_ref.dtype)
        dk_ref[...] += _dot(ds, qb, "tn")
        dv_ref[...] += _dot(p, dob, "tn")

    qs = pl.BlockSpec((tq, XA_D), lambda b, h, i: (b * nq + i, h))
    ks = pl.BlockSpec((M, XA_D), lambda b, h, i: (b, h))
    return pl.pallas_call(
        body, grid=(nseq, XA_H, nq), name="xattn_bwd", in_specs=[qs, ks, ks, qs], out_specs=[qs, ks, ks],
        out_shape=[jax.ShapeDtypeStruct((T, XA_H * XA_D), BF16), jax.ShapeDtypeStruct(k.shape, F32),
                   jax.ShapeDtypeStruct(k.shape, F32)],
        compiler_params=_cp("parallel", "parallel", "arbitrary"),
    )(q, k, v, do)


CONV_BLK = 256


def _shift_down(x, s, rows):
    if s == 0:
        return x
    return jnp.where(rows >= s, pltpu.roll(x, s, axis=0), 0.0)


def _shift_up(x, s, rows):
    if s == 0:
        return x
    S = x.shape[0]
    return jnp.where(rows < S - s, pltpu.roll(x, S - s, axis=0), 0.0)


def conv_fwd(x, w, b, nseq):
    T, CH = x.shape
    S = T // nseq

    def body(x_ref, w_ref, b_ref, o_ref):
        xv = x_ref[...].astype(F32)
        rows = lax.broadcasted_iota(jnp.int32, (S, 1), 0)
        c = jnp.zeros_like(xv) + b_ref[...]
        for kk in range(CONV_K):
            c = c + w_ref[kk:kk + 1, :] * _shift_down(xv, CONV_K - 1 - kk, rows)
        o_ref[...] = (c * _sigmoid(c)).astype(o_ref.dtype)

    xs = pl.BlockSpec((S, CONV_BLK), lambda j, bb: (bb, j))
    return pl.pallas_call(
        body, grid=(CH // CONV_BLK, nseq), name="conv_fwd",
        in_specs=[xs, pl.BlockSpec((CONV_K, CONV_BLK), lambda j, bb: (0, j)), pl.BlockSpec((1, CONV_BLK), lambda j, bb: (0, j))],
        out_specs=xs, out_shape=jax.ShapeDtypeStruct((T, CH), BF16),
        compiler_params=_cp("parallel", "parallel"),
    )(x, w, b)


def conv_bwd(x, w, b, dout, nseq):
    T, CH = x.shape
    S = T // nseq

    def body(x_ref, w_ref, b_ref, do_ref, dx_ref, dw_ref, db_ref):
        @pl.when(pl.program_id(1) == 0)
        def _():
            dw_ref[...] = jnp.zeros_like(dw_ref)
            db_ref[...] = jnp.zeros_like(db_ref)

        xv = x_ref[...].astype(F32)
        rows = lax.broadcasted_iota(jnp.int32, (S, 1), 0)
        c = jnp.zeros_like(xv) + b_ref[...]
        sh = [_shift_down(xv, CONV_K - 1 - kk, rows) for kk in range(CONV_K)]
        for kk in range(CONV_K):
            c = c + w_ref[kk:kk + 1, :] * sh[kk]
        sg = _sigmoid(c)
        dc = do_ref[...].astype(F32) * sg * (1.0 + c * (1.0 - sg))
        dx = jnp.zeros_like(xv)
        for kk in range(CONV_K):
            dx = dx + w_ref[kk:kk + 1, :] * _shift_up(dc, CONV_K - 1 - kk, rows)
            dw_ref[kk:kk + 1, :] += jnp.sum(dc * sh[kk], axis=0, keepdims=True)
        dx_ref[...] = dx.astype(dx_ref.dtype)
        db_ref[...] += jnp.sum(dc, axis=0, keepdims=True)

    xs = pl.BlockSpec((S, CONV_BLK), lambda j, bb: (bb, j))
    ws = pl.BlockSpec((CONV_K, CONV_BLK), lambda j, bb: (0, j))
    bs = pl.BlockSpec((1, CONV_BLK), lambda j, bb: (0, j))
    return pl.pallas_call(
        body, grid=(CH // CONV_BLK, nseq), name="conv_bwd",
        in_specs=[xs, ws, bs, xs], out_specs=[xs, ws, bs],
        out_shape=[jax.ShapeDtypeStruct((T, CH), BF16), jax.ShapeDtypeStruct((CONV_K, CH), F32),
                   jax.ShapeDtypeStruct((1, CH), F32)],
        compiler_params=_cp("parallel", "arbitrary"),
    )(x, w, b, dout)


def _dims(a, b, mode):
    M = a.shape[1] if mode[0] == "t" else a.shape[0]
    K = a.shape[0] if mode[0] == "t" else a.shape[1]
    N = b.shape[0] if mode[1] == "t" else b.shape[1]
    return M, K, N


def _tile(dim, prefs):
    for p in prefs:
        if dim % p == 0:
            return p
    return dim


def mm(groups, out_dtypes, name, tm=None, tn=None, tk=None, epi=None, extras=(), comm=None, sub=1, n_sum=0):
    a0, b0, m0 = groups[0][0]
    M, K0, N = _dims(a0, b0, m0)
    tm = tm or _tile(M, (1024, 512, 256, 128))
    tn = tn or _tile(N, (512, 256, 128))
    flat = [p for g in groups for p in g]
    nk = 1 if tk is None else K0 // tk
    in_specs, args = [], []
    for a, b, mode in flat:
        _, K, _ = _dims(a, b, mode)
        kb = K if tk is None else tk
        in_specs.append(pl.BlockSpec((kb, tm), lambda i, j, k: (k, i)) if mode[0] == "t"
                        else pl.BlockSpec((tm, kb), lambda i, j, k: (i, k)))
        in_specs.append(pl.BlockSpec((tn, kb), lambda i, j, k: (j, k)) if mode[1] == "t"
                        else pl.BlockSpec((kb, tn), lambda i, j, k: (k, j)))
        args += [a, b]
    for e in extras:
        in_specs.append(pl.BlockSpec((1, tn), lambda i, j, k: (0, j)) if e.shape[0] == 1 and M != 1
                        else pl.BlockSpec((tm, tn), lambda i, j, k: (i, j)))
        args.append(e)
    n_in = len(args)
    n_main = len(out_dtypes)
    n_out = n_main + n_sum
    assert n_sum == 0 or (tn == N and tk is None)
    ng = len(groups)
    sizes = [len(g) for g in groups]

    def body(*refs):
        ins, outs, accs = refs[:n_in], refs[n_in:n_in + n_out], refs[n_in + n_out:]
        kk = pl.program_id(2)

        def dots(rs):
            vals, pos = [], 0
            for gi in range(ng):
                acc = None
                for _ in range(sizes[gi]):
                    mode = flat[pos // 2][2]
                    av = ins[pos][:, rs] if mode[0] == "t" else ins[pos][rs, :]
                    d = _dot(av, ins[pos + 1][...], mode)
                    acc = d if acc is None else acc + d
                    pos += 2
                vals.append(acc)
            return vals

        def finish(accv, rs, first_chunk=True):
            ex = [(r[...] if r.shape[0] == 1 and tm != 1 else r[rs, :]).astype(F32) for r in ins[2 * len(flat):]]
            res = epi(accv, ex) if epi is not None else tuple(accv)
            for o, r in zip(outs[:n_main], res[:n_main]):
                o[rs, :] = r.astype(o.dtype)
            for o, r in zip(outs[n_main:], res[n_main:]):
                if first_chunk:
                    @pl.when(pl.program_id(0) == 0)
                    def _():
                        o[...] = r

                    @pl.when(pl.program_id(0) > 0)
                    def _():
                        o[...] += r
                else:
                    o[...] += r

        if nk == 1:
            for r in range(sub):
                rs = slice(r * (tm // sub), (r + 1) * (tm // sub))
                finish(dots(rs), rs, r == 0)
        else:
            vals = dots(slice(0, tm))
            finish = functools.partial(finish, rs=slice(0, tm))
            @pl.when(kk == 0)
            def _():
                for ar, vv in zip(accs, vals):
                    ar[...] = vv

            @pl.when(kk > 0)
            def _():
                for ar, vv in zip(accs, vals):
                    ar[...] += vv

            @pl.when(kk == nk - 1)
            def _():
                finish([ar[...] for ar in accs])

    grid = (M // tm, N // tn, nk)
    out_specs = [pl.BlockSpec((tm, tn), lambda i, j, k: (i, j)) for _ in out_dtypes] \
        + [pl.BlockSpec((1, tn), lambda i, j, k: (0, j))] * n_sum
    out_shape = [jax.ShapeDtypeStruct((M, N), dt) for dt in out_dtypes] + [jax.ShapeDtypeStruct((1, N), F32)] * n_sum
    scratch = [pltpu.VMEM((tm, tn), F32) for _ in range(ng if nk > 1 else 0)]
    sem = ("arbitrary" if n_sum else "parallel", "parallel", "arbitrary")
    if comm is not None:
        body = _attach(comm, body, n_in, n_out, *_grid_ends(grid))
        in_specs, args = in_specs + [HBM_SPEC] * len(comm.inputs), args + comm.inputs
        out_specs, out_shape = out_specs + [HBM_SPEC] * len(comm.out_shapes), out_shape + comm.out_shapes
        scratch, sem = scratch + comm.sems, ("arbitrary",) * 3
    return pl.pallas_call(body, grid=grid, name=name, in_specs=in_specs, out_specs=out_specs, out_shape=out_shape,
                          scratch_shapes=scratch, compiler_params=_cp(*sem))(*args)


def mm1(a, b, mode, out_dtype, name, **kw):
    return mm([[(a, b, mode)]], [out_dtype], name, **kw)[0]


ROW_BLK = 512


def rowwise(fn, rows, consts, outs, accs, name, tb=ROW_BLK):
    rows = [r if isinstance(r, tuple) else (r, r.shape[1], 0) for r in rows]
    T = rows[0][0].shape[0]
    tb = min(tb, T)
    n_r, n_c, n_o, n_a = len(rows), len(consts), len(outs), len(accs)

    def body(*refs):
        vals = [r[...].astype(F32) for r in refs[:n_r + n_c]]
        res = fn(*vals)
        o_refs = refs[n_r + n_c:n_r + n_c + n_o]
        a_refs = refs[n_r + n_c + n_o:]
        for o, r in zip(o_refs, res[:n_o]):
            o[...] = r.astype(o.dtype)
        if n_a:
            @pl.when(pl.program_id(0) == 0)
            def _():
                for ar in a_refs:
                    ar[...] = jnp.zeros_like(ar)
            for ar, r in zip(a_refs, res[n_o:]):
                ar[...] += r

    return pl.pallas_call(
        body, grid=(T // tb,), name=name,
        in_specs=[pl.BlockSpec((tb, w), functools.partial(lambda i, j: (i, j), j=j)) for _, w, j in rows]
        + [pl.BlockSpec(c.shape, lambda i: (0, 0)) for c in consts],
        out_specs=[pl.BlockSpec((tb, d), lambda i: (i, 0)) for d, _ in outs]
        + [pl.BlockSpec(s, lambda i: (0, 0)) for s in accs],
        out_shape=[jax.ShapeDtypeStruct((T, d), dt) for d, dt in outs]
        + [jax.ShapeDtypeStruct(s, F32) for s in accs],
        compiler_params=_cp("arbitrary" if n_a else "parallel"),
    )(*[r[0] for r in rows], *consts)


def _rms_stats(x):
    r = lax.rsqrt(jnp.mean(x * x, axis=-1, keepdims=True) + EPS)
    return r, x * r


def _rms_bwd(x, g, dy):
    r, xn = _rms_stats(x)
    dyg = dy * g
    dx = r * (dyg - xn * jnp.mean(dyg * xn, axis=-1, keepdims=True))
    return dx, jnp.sum(dy * xn, axis=0, keepdims=True)


def rms_fwd(x, g, name):
    return rowwise(lambda xv, gv: (_rms_stats(xv)[1] * gv,), [x], [g], [(x.shape[1], BF16)], [], name)[0]


def rms_bwd(x, g, dy, name, resid=None, dx_dtype=F32):
    def fn(*v):
        if resid is None:
            xv, dyv, gv = v
            dx, dg = _rms_bwd(xv, gv, dyv)
        else:
            xv, dyv, rv, gv = v
            dx, dg = _rms_bwd(xv, gv, dyv)
            dx = dx + rv
        return dx, dg
    rows = [x, dy] + ([] if resid is None else [resid])
    return rowwise(fn, rows, [g], [(x.shape[1], dx_dtype)], [(1, x.shape[1])], name)


def mm_rms_bwd(pairs, x, g, name, resid=None, dx_dtype=F32, comm=None):
    def epi(accs, ex):
        dx, dg = _rms_bwd(ex[0], ex[-1], accs[0])
        return (dx if resid is None else dx + ex[1]), dg
    extras = [x] + ([] if resid is None else [resid]) + [g]
    return mm([pairs], [dx_dtype], name, tm=min(256, x.shape[0]), tn=x.shape[1], epi=epi, extras=extras, comm=comm, n_sum=1)


def mm_resid(a, b, x, g, wgt, name):
    epi = lambda accs, ex: (accs[0], ex[0] + wgt * _rms_stats(accs[0])[1] * ex[1])
    return mm([[(a, b, "nn")]], [F32, F32], name, tm=min(512, a.shape[0]), tn=b.shape[1], epi=epi, extras=[x, g], sub=2)


def resid_bwd(h, g, dy, wgt, name):
    def fn(hv, dyv, gv):
        dx, dg = _rms_bwd(hv, gv, dyv)
        return wgt * dx, wgt * dg
    return rowwise(fn, [h, dy], [g], [(h.shape[1], BF16)], [(1, h.shape[1])], name)


def _silu_parts(g):
    s = _sigmoid(g)
    return g * s, s * (1.0 + g * (1.0 - s))


def gated_norm_fwd(y, z, g, name):
    W = SSD_INNER // SSD_G

    def fn(yv, zv, gv):
        yg = yv * _silu_parts(zv)[0]
        return (jnp.concatenate([_rms_stats(yg[:, i * W:(i + 1) * W])[1] for i in range(SSD_G)], axis=1) * gv,)
    return rowwise(fn, [y, z], [g], [(SSD_INNER, BF16)], [], name)[0]


def gated_norm_bwd(y, z, dyn, g, name):
    W = SSD_INNER // SSD_G

    def fn(yv, zv, dv, gv):
        sil, dsil = _silu_parts(zv)
        yg = yv * sil
        parts = [_rms_bwd(yg[:, i * W:(i + 1) * W], gv[:, i * W:(i + 1) * W], dv[:, i * W:(i + 1) * W]) for i in range(SSD_G)]
        dyg = jnp.concatenate([p[0] for p in parts], axis=1)
        dg = jnp.concatenate([p[1] for p in parts], axis=1)
        return dyg * sil, dyg * yv * dsil, dg
    return rowwise(fn, [y, z, dyn], [g], [(SSD_INNER, BF16), (SSD_INNER, BF16)], [(1, SSD_INNER)], name)


def merge_fwd(gl, ys, ym, gb, name):
    def fn(glv, ysv, ymv, gbv):
        gt = _sigmoid(glv + gbv)
        return (gt[:, :D] * ysv + gt[:, D:] * ymv,)
    return rowwise(fn, [gl, ys, ym], [gb], [(D, BF16)], [], name)[0]


def merge_bwd(gl, ys, ym, dm, gb, name):
    def fn(glv, ysv, ymv, dmv, gbv):
        gt = _sigmoid(glv + gbv)
        gs, gm = gt[:, :D], gt[:, D:]
        dgl = jnp.concatenate([dmv * ysv * gs * (1.0 - gs), dmv * ymv * gm * (1.0 - gm)], axis=1)
        return dmv * gs, dmv * gm, dgl, jnp.sum(dgl, axis=0, keepdims=True)
    return rowwise(fn, [gl, ys, ym, dm], [gb], [(D, BF16), (D, BF16), (2 * D, BF16)], [(1, 2 * D)], name)


def loss_head(y, tgt, name):
    def fn(yv, tv):
        d = yv - tv
        part = 0.5 * jnp.sum(jnp.sum(d * d, axis=1, keepdims=True), axis=0, keepdims=True) / D
        return d / D, jnp.broadcast_to(part, (1, 128))
    return rowwise(fn, [y, tgt], [], [(D, F32)], [(1, 128)], name)


def adamw(w, g, m, v, name):
    R, C = w.shape
    tb = _tile(R, (256, 128, 64, 32, 16, 8))

    def fn(wv, gv, mv, vv):
        mn = B1 * mv + (1.0 - B1) * gv
        vn = B2 * vv + (1.0 - B2) * (gv * gv)
        mh = mn / (1.0 - B1 ** STEP)
        vh = vn / (1.0 - B2 ** STEP)
        return -LR * (mh / (jnp.sqrt(vh) + AEPS) + WD * wv), mn, vn
    return rowwise(fn, [w, g, m, v], [], [(C, F32)] * 3, [], name, tb=tb)


def _me():
    return lax.axis_index("x"), lax.axis_index("y"), lax.axis_index("c")


def _dev_index():
    x, y, c = _me()
    return 4 * x + 2 * y + c


HBM_SPEC = pl.BlockSpec(memory_space=pl.ANY)


class GatherComm:
    def __init__(self, shards):
        self.inputs = list(shards)
        n = len(shards)
        self.out_shapes = [jax.ShapeDtypeStruct((N_DEV,) + s.shape, s.dtype) for s in shards]
        self.sems = [pltpu.SemaphoreType.DMA((7 * n,)), pltpu.SemaphoreType.DMA((7 * n,)), pltpu.SemaphoreType.DMA((n,))]

    def _plan(self, x_refs, out_refs, sems):
        send_sems, recv_sems, local_sems = sems
        n = len(x_refs)
        x, y, c = _me()
        me, sibling = (x, y, c), (x, y, 1 - c)
        chips = [(1 - x, y), (x, 1 - y), (1 - x, 1 - y)]

        def slot(i, px, py, pc):
            return out_refs[i].at[4 * px + 2 * py + pc]

        def copy(i, k, block, to, src=None):
            return pltpu.make_async_remote_copy(
                src_ref=slot(i, *block) if src is None else src, dst_ref=slot(i, *block),
                send_sem=send_sems.at[7 * i + k], recv_sem=recv_sems.at[7 * i + k], device_id=to, device_id_type=MESH)

        mine = [pltpu.make_async_copy(x_refs[i], slot(i, *me), local_sems.at[i]) for i in range(n)]
        first = []
        for i in range(n):
            first.append(copy(i, 0, me, sibling, src=x_refs[i]))
            first += [copy(i, 1 + j, me, (*chip, c), src=x_refs[i]) for j, chip in enumerate(chips)]
        passed = [[copy(i, 4 + j, (*chip, c), sibling) for j, chip in enumerate(chips)] for i in range(n)]
        from_ici = [[copy(i, 1 + j, (*chip, c), me) for j, chip in enumerate(chips)] for i in range(n)]
        from_sib = [[copy(i, 0, sibling, me)] + [copy(i, 4 + j, (*chip, 1 - c), me) for j, chip in enumerate(chips)] for i in range(n)]
        return mine, first, passed, from_ici, from_sib

    def start(self, x_refs, out_refs, sems):
        mine, first, _, _, _ = self._plan(x_refs, out_refs, sems)
        for cp in mine + first:
            cp.start()

    def finish(self, x_refs, out_refs, sems):
        mine, first, passed, from_ici, from_sib = self._plan(x_refs, out_refs, sems)
        for i in range(len(x_refs)):
            for arrival, forward in zip(from_ici[i], passed[i]):
                arrival.wait_recv()
                forward.start()
        for row in from_sib:
            for arrival in row:
                arrival.wait_recv()
        for cp in first + [cp for row in passed for cp in row]:
            cp.wait_send()
        for cp in mine:
            cp.wait()


def run_comm(comm, name):
    n_in, n_out = len(comm.inputs), len(comm.out_shapes)

    def body(*refs):
        ins, outs, sems = refs[:n_in], refs[n_in:n_in + n_out], refs[n_in + n_out:]
        comm.start(ins, outs, sems)
        comm.finish(ins, outs, sems)

    return pl.pallas_call(body, name=name, out_shape=comm.out_shapes, in_specs=[HBM_SPEC] * n_in,
                          out_specs=[HBM_SPEC] * n_out, scratch_shapes=comm.sems)(*comm.inputs)


def _attach(comm, body, n_in, n_out, first, last):
    if comm is None:
        return body
    ci, co, cs = len(comm.inputs), len(comm.out_shapes), len(comm.sems)

    def wrapped(*refs):
        h_in, c_in = refs[:n_in], refs[n_in:n_in + ci]
        h_out, c_out = refs[n_in + ci:n_in + ci + n_out], refs[n_in + ci + n_out:n_in + ci + n_out + co]
        rest = refs[n_in + ci + n_out + co:]
        h_scr, c_sem = rest[:len(rest) - cs], rest[len(rest) - cs:]

        @pl.when(first())
        def _():
            comm.start(c_in, c_out, c_sem)

        body(*h_in, *h_out, *h_scr)

        @pl.when(last())
        def _():
            comm.finish(c_in, c_out, c_sem)

    return wrapped


def _grid_ends(grid):
    first = lambda: functools.reduce(lambda a, b: a & b, [pl.program_id(i) == 0 for i in range(len(grid))])
    last = lambda: functools.reduce(lambda a, b: a & b, [pl.program_id(i) == g - 1 for i, g in enumerate(grid)])
    return first, last


def _call_with_comm(body, grid, name, in_specs, args, out_specs, out_shape, comm, scratch=(), sem=None):
    sem = sem or ("parallel",) * len(grid)
    scratch = list(scratch)
    if comm is not None:
        body = _attach(comm, body, len(args), len(out_shape), *_grid_ends(grid))
        in_specs, args = in_specs + [HBM_SPEC] * len(comm.inputs), args + comm.inputs
        out_specs, out_shape = out_specs + [HBM_SPEC] * len(comm.out_shapes), out_shape + comm.out_shapes
        scratch, sem = scratch + comm.sems, ("arbitrary",) * len(grid)
    return pl.pallas_call(body, grid=grid, name=name, in_specs=in_specs, out_specs=out_specs, out_shape=out_shape,
                          scratch_shapes=scratch, compiler_params=_cp(*sem))(*args)


class ScatterComm:
    def __init__(self, groups):
        self.sizes = [len(g) for g in groups]
        self.rows = [[pc.shape[1] for pc in g] for g in groups]
        ng = len(groups)
        self.inputs = [pc for g in groups for pc in g]
        self.out_shapes = [jax.ShapeDtypeStruct((N_DEV, sum(self.rows[gi]), g[0].shape[2]), g[0].dtype) for gi, g in enumerate(groups)]
        self.sems = [pltpu.SemaphoreType.DMA((7 * ng,)), pltpu.SemaphoreType.DMA((7 * ng,)), pltpu.SemaphoreType.DMA((ng,))]

    def _peers(self):
        x, y, c = _me()
        out = []
        for k in range(1, N_DEV):
            px = 1 - x if k & 4 else x
            py = 1 - y if k & 2 else y
            pc = 1 - c if k & 1 else c
            out.append((k, 4 * px + 2 * py + pc, dict(device_id=(px, py, pc), device_id_type=MESH)))
        return 4 * x + 2 * y + c, out

    def start(self, ins, outs, sems):
        send_sems, recv_sems, local_sems = sems
        me, peers = self._peers()
        pos = 0
        for gi, size in enumerate(self.sizes):
            for i, pc in enumerate(ins[pos:pos + size]):
                dst = outs[gi].at[me, pl.ds(sum(self.rows[gi][:i]), self.rows[gi][i])]
                pltpu.make_async_copy(pc.at[me], dst, local_sems.at[gi]).start()
                for k, peer, kw in peers:
                    pltpu.make_async_remote_copy(src_ref=pc.at[peer], dst_ref=dst, send_sem=send_sems.at[7 * gi + k - 1],
                                                 recv_sem=recv_sems.at[7 * gi + k - 1], **kw).start()
            pos += size

    def finish(self, ins, outs, sems):
        send_sems, recv_sems, local_sems = sems
        me, peers = self._peers()
        whole = [pltpu.make_async_remote_copy(src_ref=outs[gi].at[peer], dst_ref=outs[gi].at[peer],
                                              send_sem=send_sems.at[7 * gi + k - 1], recv_sem=recv_sems.at[7 * gi + k - 1], **kw)
                 for gi in range(len(self.sizes)) for k, peer, kw in peers]
        for cp in whole:
            cp.wait_recv()
        for cp in whole:
            cp.wait_send()
        for gi in range(len(self.sizes)):
            pltpu.make_async_copy(outs[gi].at[me], outs[gi].at[me], local_sems.at[gi]).wait()


def sum_slots(recv, name, tr):
    n, R, C = recv.shape

    def body(r_ref, o_ref):
        acc = r_ref[0].astype(F32)
        for s in range(1, n):
            acc = acc + r_ref[s].astype(F32)
        o_ref[...] = acc

    return pl.pallas_call(
        body, grid=(R // tr,), name=name,
        in_specs=[pl.BlockSpec((n, tr, C), lambda i: (0, i, 0))], out_specs=pl.BlockSpec((tr, C), lambda i: (i, 0)),
        out_shape=jax.ShapeDtypeStruct((R, C), F32), compiler_params=_cp("parallel"),
    )(recv)


PACK_W, FLAT_W = 1024, 128
MAIN = [
    ("ffn1_w_gate", "col"), ("ffn1_w_up", "col"), ("ffn1_w_down", "row"),
    ("ffn2_w_gate", "col"), ("ffn2_w_up", "col"), ("ffn2_w_down", "row"),
    ("w_ssd_proj", "row"), ("w_mla_proj", "row"), ("w_out", "row"),
    ("w_xq", "row"), ("w_xk", "row"), ("w_xv", "row"), ("w_xo", "row"),
    ("w_uk", "col"), ("w_uv", "col"),
]
FLAT = [("w_in", "col"), ("w_uq", "col")]
BIG = MAIN + FLAT
SMALL = ["ffn1_pre_g", "ffn1_post_g", "mix_pre_g", "conv_b", "dt_bias", "a_log", "d_skip", "ssd_norm_g", "q_norm_g",
         "kv_norm_g", "gate_bias", "mix_post_g", "xa_pre_g", "mem_norm_g", "xa_post_g", "ffn2_pre_g", "ffn2_post_g"]
WEIGHTS = ['ffn1_pre_g', 'ffn1_w_gate', 'ffn1_w_up', 'ffn1_w_down', 'ffn1_post_g', 'mix_pre_g', 'w_in', 'conv_w', 'conv_b',
           'dt_bias', 'a_log', 'd_skip', 'ssd_norm_g', 'w_ssd_proj', 'q_norm_g', 'w_uq', 'kv_norm_g', 'w_uk', 'w_uv',
           'w_mla_proj', 'gate_bias', 'w_out', 'mix_post_g', 'xa_pre_g', 'mem_norm_g', 'w_xq', 'w_xk', 'w_xv', 'w_xo',
           'xa_post_g', 'ffn2_pre_g', 'ffn2_w_gate', 'ffn2_w_up', 'ffn2_w_down', 'ffn2_post_g']


def _pack_rows(w, kind, width):
    m = w[0].T if kind == "col" else w[0]
    return m.reshape(-1, width)


KIND = dict(BIG)
GATHER_PLAN = {
    "first": (["ffn1_w_gate", "ffn1_w_up", "ffn1_w_down"], []),
    "ffn1_gate_up": (["w_ssd_proj", "w_mla_proj", "w_out", "w_uk", "w_uv"], ["w_in", "w_uq"]),
    "attn_fwd": (["w_xq", "w_xk", "w_xv", "w_xo", "ffn2_w_gate", "ffn2_w_up", "ffn2_w_down"], []),
}
SCATTER_PLAN = {
    "attn_bwd": (["ffn2_w_gate", "ffn2_w_up", "ffn2_w_down", "w_xq", "w_xk", "w_xv", "w_xo"], []),
    "ssd_bwd": (["w_ssd_proj", "w_mla_proj", "w_out", "w_uk", "w_uv"], ["w_uq"]),
    "in_bwd": ([], ["w_in#0"]),
    "ffn1:down_bwd": ([], ["w_in#1"]),
    "ffn1:dwg": (["ffn1_w_down"], []),
    "ffn1:dwu": (["ffn1_w_gate"], []),
    "ffn1:gate_up_bwd": (["ffn1_w_up"], []),
}
PARTS = {"w_in#0": ("w_in", 0, 2656), "w_in#1": ("w_in", 2656, 5296)}


class Stage:
    def __init__(self, w):
        self.w = w
        self.width = {n: PACK_W if (n, k) in MAIN else FLAT_W for n, k in BIG}
        self.nrows = {n: math.prod(w[n].shape) // self.width[n] for n, _ in BIG}
        self.recv = {}

    def _shards(self, tag):
        names_main, names_flat = GATHER_PLAN[tag]
        pack = lambda n: _pack_rows(self.w[n], KIND[n], self.width[n]).astype(BF16)
        shards = []
        if names_main:
            shards.append(jnp.concatenate([pack(n) for n in names_main], axis=0))
        if names_flat:
            bits = lax.bitcast_convert_type(self.w["conv_w"][0], BF16).reshape(-1, FLAT_W)
            shards.append(_pad_rows(jnp.concatenate([pack(n) for n in names_flat] + [bits], axis=0), 16))
        return shards

    def gather(self, tag):
        return GatherComm(self._shards(tag))

    def gathered(self, tag, outs, W, p):
        names_main, names_flat = GATHER_PLAN[tag]
        outs = list(outs)
        for names in (names_main, names_flat):
            if not names:
                continue
            buf, r0 = outs.pop(0), 0
            for n in names:
                K = self.w[n].shape[1] if KIND[n] == "col" else PACK_W
                W[n] = buf[:, r0:r0 + self.nrows[n]].reshape(-1, K)
                r0 += self.nrows[n]
            if names is names_flat:
                cw = self.w["conv_w"]
                nbits = 2 * math.prod(cw.shape) // FLAT_W
                bits = buf[:, r0:r0 + nbits].reshape((N_DEV,) + cw.shape[1:] + (2,))
                p["conv_w"] = lax.bitcast_convert_type(bits, F32).transpose(1, 0, 2).reshape(cw.shape[1], -1)

    def scatter(self, tag, gw):
        if tag not in SCATTER_PLAN:
            return None
        def piece(n):
            if n in PARTS:
                base, r0, r1 = PARTS[n]
                return gw[base].reshape(N_DEV, self.nrows[base], self.width[base])[:, r0:r1]
            return gw[n].reshape(N_DEV, self.nrows[n], self.width[n])
        return ScatterComm([[piece(n) for n in names] for names in SCATTER_PLAN[tag] if names])

    def scattered(self, tag, outs):
        if tag in SCATTER_PLAN:
            self.recv[tag] = outs


def _pad_rows(a, mult):
    r = (-a.shape[0]) % mult
    return a if r == 0 else jnp.concatenate([a, jnp.zeros((r,) + a.shape[1:], a.dtype)], axis=0)


def _pack_small(vals, loss_row=None, conv_w=None):
    rows = []
    for v in vals:
        f = v.reshape(-1)
        f = jnp.concatenate([f, jnp.zeros(((-f.shape[0]) % 128,), F32)])
        rows.append(f.reshape(-1, 128))
    if conv_w is not None:
        rows.append(conv_w.reshape(-1, 128))
    if loss_row is not None:
        rows.append(loss_row)
    return _pad_rows(jnp.concatenate(rows, axis=0), 8)


def _unpack_small(buf, shapes):
    out, r = [], 0
    for shp in shapes:
        n = math.prod(shp)
        nr = -(-n // 128)
        out.append(buf[r:r + nr].reshape(-1)[:n].reshape(shp))
        r += nr
    return out, r


def _tn(a, b, name, out_dtype=BF16, comm=None):
    M, N = a.shape[1], b.shape[1]
    T = a.shape[0]
    tm = M if M <= 1536 else M // 2
    tk = 1024 if T % 1024 == 0 and T > 1024 else None
    res = mm([[(a, b, "tn")]], [out_dtype], name, tm=tm, tn=N, tk=tk, comm=comm)
    return res[0] if comm is None else (res[0], res[1:])


class NoStage:
    def gather(self, tag):
        return None

    def gathered(self, tag, outs, W, p):
        pass

    def scatter(self, tag, gw):
        return None

    def scattered(self, tag, outs):
        pass


def _ffn_fwd(x, gpre, gpost, wg_t, wu_t, wd, tag, comm=None):
    h = rms_fwd(x, gpre, tag + "_pre")
    def swi(accs, ex):
        sil, dsil = _silu_parts(accs[0])
        return sil, accs[1] * dsil, sil * accs[1]
    res = mm([[(h, wg_t, "nt")], [(h, wu_t, "nt")]], [BF16, BF16, BF16], tag + "_gate_up", tn=256, epi=swi, comm=comm,
             sub=4 if h.shape[0] % 1024 == 0 else 1)
    G, U, A = res[:3]
    H, y = mm_resid(A, wd, x, gpost, FFN_RES, tag + "_down")
    return y, (x, h, G, U, A, H), res[3:]


def _ffn_bwd(dy, saved, gpre, gpost, wg_t, wu_t, wd, tag, stage, gw):
    x, h, G, U, A, H = saved
    dH, dgpost = resid_bwd(H, gpost, dy, FFN_RES, tag + "_post_bwd")

    def dswi(accs, ex):
        return accs[0] * ex[1], accs[0] * ex[0]

    def hosted(where, call):
        comm = stage.scatter(tag + ":" + where, gw)
        res = call(comm)
        if comm is None:
            return res
        stage.scattered(tag + ":" + where, res[1])
        return res[0]

    res = hosted("down_bwd", lambda comm: (lambda r: r if comm is None else (r[:2], r[2:]))(
        mm([[(dH, wd, "nt")]], [BF16, BF16], tag + "_down_bwd", tn=256, epi=dswi, extras=[G, U], comm=comm,
           sub=4 if dH.shape[0] % 1024 == 0 else 1)))
    dG, dU = res
    gw[tag + "_w_down"] = _tn(A, dH, tag + "_dwd")
    gw[tag + "_w_gate"] = hosted("dwg", lambda comm: _tn(dG, h, tag + "_dwg", comm=comm))
    gw[tag + "_w_up"] = hosted("dwu", lambda comm: _tn(dU, h, tag + "_dwu", comm=comm))
    dx, dgpre = hosted("gate_up_bwd", lambda comm: (lambda r: r[:2] if comm is None else (r[:2], r[2:]))(
        mm_rms_bwd([(dG, wg_t, "nn"), (dU, wu_t, "nn")], x, gpre, tag + "_gate_up_bwd", resid=dy, comm=comm)))
    return dx, dgpre, dgpost


def _rope_tables(positions):
    inv = ROPE_THETA ** (-jnp.arange(0, ROPE, 2, dtype=F32) / ROPE)
    ang = positions.astype(F32).reshape(-1)[:, None] * inv
    return jnp.cos(ang), jnp.sin(ang)


def _local_step(x, mem, positions, tgt, W, p, stage=None):
    stage = stage or NoStage()
    nseq = x.shape[0]
    T = nseq * x.shape[1]
    x0 = x.reshape(T, D)
    mem2 = mem.reshape(-1, D)
    cos, sin = _rope_tables(positions)

    x1, ffn1, arrived = _ffn_fwd(x0, p["ffn1_pre_g"], p["ffn1_post_g"], W["ffn1_w_gate"], W["ffn1_w_up"], W["ffn1_w_down"],
                                 "ffn1", comm=stage.gather("ffn1_gate_up"))
    stage.gathered("ffn1_gate_up", arrived, W, p)

    w_in_t = W["w_in"]
    bounds = [0]
    for n in (SSD_INNER, CONV_CH, SSD_H, QR, KVR, ROPE, 2 * D):
        bounds.append(bounds[-1] + n)
    wt_z, wt_xbc, wt_dt, wt_q, wt_kv, wt_kr, wt_gate = [w_in_t[bounds[i]:bounds[i + 1]] for i in range(7)]
    wt_dt, wt_kr = _pad_rows(wt_dt, SLOT), _pad_rows(wt_kr, SLOT)
    wt_dtkr = jnp.concatenate([wt_dt, wt_kr], axis=0)
    hm = rms_fwd(x1, p["mix_pre_g"], "mix_pre")
    z = mm1(hm, wt_z, "nt", BF16, "in_z")
    xbc = mm1(hm, wt_xbc, "nt", BF16, "in_xbc")
    q_c = mm1(hm, wt_q, "nt", F32, "in_q", tn=QR)
    kv_c = mm1(hm, wt_kv, "nt", F32, "in_kv")
    dtkr = mm1(hm, wt_dtkr, "nt", F32, "in_dtkr")
    gl = mm1(hm, wt_gate, "nt", BF16, "in_gate")

    xbc_act = conv_fwd(xbc, p["conv_w"], p["conv_b"], nseq)
    y_ssd_core, prev = ssd_fwd(xbc_act, dtkr, p["dt_bias"], p["a_log"], p["d_skip"], nseq)
    yn = gated_norm_fwd(y_ssd_core, z, p["ssd_norm_g"], "ssd_norm")
    y_ssd = mm1(yn, W["w_ssd_proj"], "nn", BF16, "ssd_proj")

    slot_rows = lambda wt, per: jnp.pad(wt.reshape(MLA_H, per, -1), ((0, 0), (0, SLOT - per), (0, 0))).reshape(MLA_H * SLOT, -1)
    wq_s, wk_s, wv_s = slot_rows(W["w_uq"], QK), slot_rows(W["w_uk"], NOPE), slot_rows(W["w_uv"], VD)
    wo_s = slot_rows(W["w_mla_proj"], VD)
    qn = rms_fwd(q_c, p["q_norm_g"], "q_norm")
    q_s = mm1(qn, wq_s, "nt", BF16, "uq")
    kvn = rms_fwd(kv_c, p["kv_norm_g"], "kv_norm")
    kn_s = mm1(kvn, wk_s, "nt", BF16, "uk")
    v_s = mm1(kvn, wv_s, "nt", BF16, "uv")
    cos16, sin16 = cos, sin
    Qc, Kc = rope_slot_fwd(q_s, kn_s, dtkr, cos16, sin16, "rope")
    o_s, lse, *arrived = attn_slot_fwd(Qc, Kc, v_s, nseq, comm=stage.gather("attn_fwd"))
    stage.gathered("attn_fwd", arrived, W, p)
    y_mla = mm1(o_s, wo_s, "nn", BF16, "mla_proj")

    merged = merge_fwd(gl, y_ssd, y_mla, p["gate_bias"], "merge")
    hmix, x2 = mm_resid(merged, W["w_out"], x1, p["mix_post_g"], 1.0, "mix_out")

    hq = rms_fwd(x2, p["xa_pre_g"], "xa_pre")
    mn = rms_fwd(mem2, p["mem_norm_g"], "mem_norm")
    xq = mm1(hq, W["w_xq"], "nn", BF16, "xq")
    xk = mm1(mn, W["w_xk"], "nn", BF16, "xk")
    xv = mm1(mn, W["w_xv"], "nn", BF16, "xv")
    xo = xattn_fwd(xq, xk, xv, nseq)
    ho, x3 = mm_resid(xo, W["w_xo"], x2, p["xa_post_g"], 1.0, "xo")

    x4, ffn2, _ = _ffn_fwd(x3, p["ffn2_pre_g"], p["ffn2_post_g"], W["ffn2_w_gate"], W["ffn2_w_up"], W["ffn2_w_down"], "ffn2")
    dx4, loss_row = loss_head(x4, tgt.reshape(T, D), "loss")

    gw, gs = {}, {}
    dx3, gs["ffn2_pre_g"], gs["ffn2_post_g"] = _ffn_bwd(
        dx4, ffn2, p["ffn2_pre_g"], p["ffn2_post_g"], W["ffn2_w_gate"], W["ffn2_w_up"], W["ffn2_w_down"], "ffn2", stage, gw)

    dho, gs["xa_post_g"] = resid_bwd(ho, p["xa_post_g"], dx3, 1.0, "xa_post_bwd")
    dxo = mm1(dho, W["w_xo"], "nt", BF16, "xo_bwd")
    gw["w_xo"] = _tn(xo, dho, "d_w_xo")
    dxq, dxk, dxv = xattn_bwd(xq, xk, xv, dxo, nseq)
    dx2, gs["xa_pre_g"] = mm_rms_bwd([(dxq, W["w_xq"], "nt")], x2, p["xa_pre_g"], "xq_bwd", resid=dx3)
    gw["w_xq"] = _tn(hq, dxq, "d_w_xq")
    dmn = mm([[(dxk, W["w_xk"], "nt"), (dxv, W["w_xv"], "nt")]], [F32], "xkv_bwd")[0]
    gw["w_xk"] = _tn(mn, dxk, "d_w_xk")
    gw["w_xv"] = _tn(mn, dxv, "d_w_xv")
    _, gs["mem_norm_g"] = rms_bwd(mem2, p["mem_norm_g"], dmn, "mem_norm_bwd", dx_dtype=BF16)

    dhmix, gs["mix_post_g"] = resid_bwd(hmix, p["mix_post_g"], dx2, 1.0, "mix_post_bwd")
    dmerged = mm1(dhmix, W["w_out"], "nt", F32, "mix_out_bwd")
    gw["w_out"] = _tn(merged, dhmix, "d_w_out")
    dys, dym, dgl, gs["gate_bias"] = merge_bwd(gl, y_ssd, y_mla, dmerged, p["gate_bias"], "merge_bwd")

    unslot = lambda g, per: g.reshape(MLA_H, SLOT, -1)[:, :per].reshape(MLA_H * per, -1)
    do_s = mm1(dym, wo_s, "nt", BF16, "mla_proj_bwd")
    gw["w_mla_proj"] = unslot(_tn(o_s, dym, "d_w_mla_proj"), VD)
    dQc, dKc, dv_s, *sent = attn_slot_bwd(Qc, Kc, v_s, o_s, lse, do_s, nseq, comm=stage.scatter("attn_bwd", gw))
    stage.scattered("attn_bwd", sent)
    dq_s, dkn_s, dkr = rope_slot_bwd(dQc, dKc, cos16, sin16, "rope_bwd")
    dq_c, gs["q_norm_g"] = mm_rms_bwd([(dq_s, wq_s, "nn")], q_c, p["q_norm_g"], "uq_bwd", dx_dtype=BF16)
    gw["w_uq"] = unslot(_tn(dq_s, qn, "d_w_uq"), QK)
    dkv_c, gs["kv_norm_g"] = mm_rms_bwd([(dkn_s, wk_s, "nn"), (dv_s, wv_s, "nn")], kv_c, p["kv_norm_g"], "ukv_bwd", dx_dtype=BF16)
    gw["w_uk"] = unslot(_tn(dkn_s, kvn, "d_w_uk"), NOPE)
    gw["w_uv"] = unslot(_tn(dv_s, kvn, "d_w_uv"), VD)

    dyn = mm1(dys, W["w_ssd_proj"], "nt", F32, "ssd_proj_bwd")
    gw["w_ssd_proj"] = _tn(yn, dys, "d_w_ssd_proj")
    dyc, dz, gs["ssd_norm_g"] = gated_norm_bwd(y_ssd_core, z, dyn, p["ssd_norm_g"], "ssd_norm_bwd")
    dxbc_act, ddtr, gs["dt_bias"], gs["a_log"], gs["d_skip"], *sent = ssd_bwd(
        xbc_act, dtkr, p["dt_bias"], p["a_log"], p["d_skip"], prev, dyc, nseq, comm=stage.scatter("ssd_bwd", gw))
    stage.scattered("ssd_bwd", sent)
    dxbc, gs["conv_w"], gs["conv_b"] = conv_bwd(xbc, p["conv_w"], p["conv_b"], dxbc_act, nseq)

    gw["w_in"] = jnp.concatenate([_tn(dz, hm, "d_w_in_z"), _tn(dxbc, hm, "d_w_in_xbc"), _tn(ddtr, hm, "d_w_in_dt")[:SSD_H],
                                  _tn(dq_c, hm, "d_w_in_q"), _tn(dkv_c, hm, "d_w_in_kv"), _tn(dkr, hm, "d_w_in_kr")[:ROPE],
                                  _tn(dgl, hm, "d_w_in_gate")], axis=0)
    dx1, gs["mix_pre_g"], *sent = mm_rms_bwd(
        [(dz, wt_z, "nn"), (dxbc, wt_xbc, "nn"), (ddtr, wt_dt, "nn"), (dq_c, wt_q, "nn"), (dkv_c, wt_kv, "nn"),
         (dkr, wt_kr, "nn"), (dgl, wt_gate, "nn")], x1, p["mix_pre_g"], "in_bwd", resid=dx2, comm=stage.scatter("in_bwd", gw))
    stage.scattered("in_bwd", sent)

    dx0, gs["ffn1_pre_g"], gs["ffn1_post_g"] = _ffn_bwd(
        dx1, ffn1, p["ffn1_pre_g"], p["ffn1_post_g"], W["ffn1_w_gate"], W["ffn1_w_up"], W["ffn1_w_down"], "ffn1", stage, gw)
    return loss_row, dx0.reshape(x.shape), gw, gs


def kernel(x, mem, positions, ffn1_pre_g, ffn1_w_gate, ffn1_w_up, ffn1_w_down, ffn1_post_g, mix_pre_g, w_in, conv_w, conv_b, dt_bias, a_log, d_skip, ssd_norm_g, w_ssd_proj, q_norm_g, w_uq, kv_norm_g, w_uk, w_uv, w_mla_proj, gate_bias, w_out, mix_post_g, xa_pre_g, mem_norm_g, w_xq, w_xk, w_xv, w_xo, xa_post_g, ffn2_pre_g, ffn2_w_gate, ffn2_w_up, ffn2_w_down, ffn2_post_g, loss_target, m_ffn1_pre_g, m_ffn1_w_gate, m_ffn1_w_up, m_ffn1_w_down, m_ffn1_post_g, m_mix_pre_g, m_w_in, m_conv_w, m_conv_b, m_dt_bias, m_a_log, m_d_skip, m_ssd_norm_g, m_w_ssd_proj, m_q_norm_g, m_w_uq, m_kv_norm_g, m_w_uk, m_w_uv, m_w_mla_proj, m_gate_bias, m_w_out, m_mix_post_g, m_xa_pre_g, m_mem_norm_g, m_w_xq, m_w_xk, m_w_xv, m_w_xo, m_xa_post_g, m_ffn2_pre_g, m_ffn2_w_gate, m_ffn2_w_up, m_ffn2_w_down, m_ffn2_post_g, v_ffn1_pre_g, v_ffn1_w_gate, v_ffn1_w_up, v_ffn1_w_down, v_ffn1_post_g, v_mix_pre_g, v_w_in, v_conv_w, v_conv_b, v_dt_bias, v_a_log, v_d_skip, v_ssd_norm_g, v_w_ssd_proj, v_q_norm_g, v_w_uq, v_kv_norm_g, v_w_uk, v_w_uv, v_w_mla_proj, v_gate_bias, v_w_out, v_mix_post_g, v_xa_pre_g, v_mem_norm_g, v_w_xq, v_w_xk, v_w_xv, v_w_xo, v_xa_post_g, v_ffn2_pre_g, v_ffn2_w_gate, v_ffn2_w_up, v_ffn2_w_down, v_ffn2_post_g):
    a = dict(locals())
    w = {n: a[n] for n in WEIGHTS}
    m = {n: a["m_" + n] for n in WEIGHTS}
    v = {n: a["v_" + n] for n in WEIGHTS}

    stage = Stage(w)
    W, p = {}, {n: w[n] for n in SMALL}
    stage.gathered("first", run_comm(stage.gather("first"), "allgather_first"), W, p)

    loss_row, grad_x, gw, gs = _local_step(x, mem, positions, loss_target, W, p, stage)

    sm = _pack_small([gs[n] for n in SMALL], loss_row=loss_row, conv_w=gs["conv_w"])
    srecv, = run_comm(ScatterComm([[jnp.broadcast_to(sm[None], (N_DEV,) + sm.shape)]]), "exchange_small")
    s_rows = sum_slots(srecv, "sum_small", tr=sm.shape[0])
    summed = {}
    for tag, (names_main, names_flat) in SCATTER_PLAN.items():
        bufs = list(stage.recv[tag])
        for names, whole in ((names_main, False), (names_flat, True)):
            if not names:
                continue
            buf = bufs.pop(0)
            rows = buf.shape[1]
            g_rows = sum_slots(buf, "sum_" + tag.replace(":", "_") + ("_flat" if whole else ""),
                               tr=rows if whole else _tile(rows, (256, 224, 176, 128, 64, 32, 16)))
            r0 = 0
            for n in names:
                nr = PARTS[n][2] - PARTS[n][1] if n in PARTS else stage.nrows[n]
                summed[n] = g_rows[r0:r0 + nr]
                r0 += nr
    for base in {b for b, _, _ in PARTS.values()}:
        summed[base] = jnp.concatenate([summed[pn] for pn in sorted(PARTS) if PARTS[pn][0] == base], axis=0)
    grads = {}
    for n, kind in BIG:
        blk = summed[n]
        grads[n] = (blk.reshape(w[n].shape[2], w[n].shape[1]).T if kind == "col" else blk)[None]
    conv_w_full = p["conv_w"]
    small_g, r1 = _unpack_small(s_rows, [w[n].shape for n in SMALL])
    for n, g in zip(SMALL, small_g):
        grads[n] = g
    ncw = math.prod(conv_w_full.shape) // 128
    cw_grad_full = s_rows[r1:r1 + ncw].reshape(conv_w_full.shape)
    wsh = conv_w.shape[2]
    grads["conv_w"] = lax.dynamic_slice_in_dim(cw_grad_full, _dev_index() * wsh, wsh, axis=1)[None]
    loss = s_rows[r1 + ncw, 0]

    delta, new_m, new_v = {}, {}, {}
    for n, _ in BIG + [("conv_w", "col")]:
        shp = w[n].shape
        d_, m_, v_ = adamw(w[n][0], grads[n][0], m[n][0], v[n][0], "adamw_" + n)
        delta[n], new_m[n], new_v[n] = d_.reshape(shp), m_.reshape(shp), v_.reshape(shp)
    sp = [_pack_small([t[n] for n in SMALL]) for t in (w, grads, m, v)]
    outs = adamw(sp[0], sp[1], sp[2], sp[3], "adamw_small")
    for t, buf in zip((delta, new_m, new_v), outs):
        vals, _ = _unpack_small(buf, [w[n].shape for n in SMALL])
        for n, val in zip(SMALL, vals):
            t[n] = val
    return (loss, grad_x, *[grads[n] for n in WEIGHTS], *[delta[n] for n in WEIGHTS],
            *[new_m[n] for n in WEIGHTS], *[new_v[n] for n in WEIGHTS])
```

```python
import functools
import math

import jax
import jax.numpy as jnp
from jax import lax
from jax.experimental import pallas as pl
from jax.experimental.pallas import tpu as pltpu

F32, BF16 = jnp.float32, jnp.bfloat16
HI = lax.Precision.HIGHEST
MESH = pl.DeviceIdType.MESH
N_DEV = 8

D = 1024
DFF = 2816
SSD_H, SSD_P, SSD_G, SSD_N, SSD_L = 16, 64, 2, 128, 128
SSD_INNER = SSD_H * SSD_P
CONV_K, CONV_CH = 4, 1536
MLA_H, QR, KVR, NOPE, ROPE, VD = 16, 384, 256, 64, 32, 64
QK = NOPE + ROPE
ROPE_THETA = 10000.0
XA_H, XA_D = 4, 256
EPS = 1e-6
FFN_RES = 0.5
LR, B1, B2, AEPS, WD, STEP = 0.001, 0.9, 0.999, 1e-08, 0.01, 10

VMEM_LIMIT = 56 * 2**20


def _cp(*sem):
    return pltpu.CompilerParams(dimension_semantics=sem, vmem_limit_bytes=VMEM_LIMIT)


def _sigmoid(x):
    return 1.0 / (1.0 + jnp.exp(-x))


def _softplus(x):
    return jnp.where(x > 20.0, x, jnp.log(1.0 + jnp.exp(jnp.minimum(x, 20.0))))


def _dot(a, b, dims="nn"):
    ca = 0 if dims[0] == "t" else 1
    cb = 1 if dims[1] == "t" else 0
    return lax.dot_general(a.astype(BF16), b.astype(BF16), (((ca,), (cb,)), ((), ())), preferred_element_type=F32)


def _dot_sel(a, b, dims="nn", split="a", terms=3):
    r = (a if split == "a" else b).astype(F32)
    out = None
    for t in range(terms):
        piece = r.astype(BF16)
        if t + 1 < terms:
            r = r - piece.astype(F32)
        d = _dot(piece, b, dims) if split == "a" else _dot(a, piece, dims)
        out = d if out is None else out + d
    return out


def _ssd_common(dtr, dtb, alog):
    L = dtr.shape[0]
    dt = _softplus(dtr + dtb)
    a = -jnp.exp(alog)
    adt = dt * a
    r = lax.broadcasted_iota(jnp.int32, (L, L), 0)
    c = lax.broadcasted_iota(jnp.int32, (L, L), 1)
    lower = r >= c
    tri = lower.astype(F32)
    cs = _dot_sel(tri, adt, "nn", split="b")
    cs_t = _dot_sel(adt, tri, "tt")
    return dt, a, cs, cs_t, lower


def _head_expand():
    hh = lax.broadcasted_iota(jnp.int32, (SSD_H, SSD_INNER), 0)
    jj = lax.broadcasted_iota(jnp.int32, (SSD_H, SSD_INNER), 1)
    return ((jj >= hh * SSD_P) & (jj < hh * SSD_P + SSD_P)).astype(F32)


def _head_reduce():
    hh = lax.broadcasted_iota(jnp.int32, (SSD_INNER, SSD_H), 1)
    jj = lax.broadcasted_iota(jnp.int32, (SSD_INNER, SSD_H), 0)
    return ((jj >= hh * SSD_P) & (jj < hh * SSD_P + SSD_P)).astype(F32)


def ssd_fwd(xbc, dtr, dtb, alog, dsk, nseq):
    T = xbc.shape[0]
    S = T // nseq
    C = S // SSD_L
    L = SSD_L
    NP = SSD_H // 2

    def body(x_ref, b_ref, c_ref, dtr_ref, dtb_ref, alog_ref, dsk_ref, y_ref, prev_ref, st_ref):
        ci = pl.program_id(1)

        @pl.when(ci == 0)
        def _():
            st_ref[...] = jnp.zeros_like(st_ref)

        dt, a, cs, cs_t, lower = _ssd_common(dtr_ref[:, 0:SSD_H], dtb_ref[...], alog_ref[...])
        E = _head_expand()
        X = x_ref[...].astype(F32)
        dt_e = _dot_sel(dt, E)
        cs_e = _dot_sel(cs, E)
        csl_e = cs_e[L - 1:L, :]
        Xd = X * dt_e
        Xf = Xd * jnp.exp(csl_e - cs_e)
        e_e = jnp.exp(cs_e)
        skip = _dot_sel(dsk_ref[...], E) * X
        lane = lax.broadcasted_iota(jnp.int32, (1, 2 * SSD_P), 1)
        rowp = lax.broadcasted_iota(jnp.int32, (2 * SSD_P, 1), 0)
        for g in range(SSD_G):
            Bg = b_ref[:, g * SSD_N:(g + 1) * SSD_N]
            Cg = c_ref[:, g * SSD_N:(g + 1) * SSD_N]
            cb = _dot(Cg, Bg, "nt")
            for pp in range(NP // SSD_G):
                p = g * (NP // SSD_G) + pp
                sl = slice(p * 2 * SSD_P, (p + 1) * 2 * SSD_P)
                Xd_p = Xd[:, sl]
                yd = jnp.zeros((L, 2 * SSD_P), F32)
                for q in range(2):
                    h = 2 * p + q
                    m = jnp.where(lower, jnp.exp(jnp.minimum(cs[:, h:h + 1] - cs_t[h:h + 1, :], 0.0)), 0.0)
                    mask = (lane >= q * SSD_P) & (lane < (q + 1) * SSD_P)
                    yd = yd + _dot(cb * m, jnp.where(mask, Xd_p, 0.0))
                S0 = st_ref[p]
                prev_ref[0, 0, p] = S0
                z = _dot(Cg, S0, "nt")
                y_ref[:, sl] = (skip[:, sl] + yd + z * e_e[:, sl]).astype(y_ref.dtype)
                h0 = 2 * p
                dec = jnp.where(rowp < SSD_P, jnp.exp(cs[L - 1:L, h0:h0 + 1]), jnp.exp(cs[L - 1:L, h0 + 1:h0 + 2]))
                st_ref[p] = S0 * dec + _dot(Xf[:, sl], Bg, "tn")

    row = lambda b, c: (b * C + c, 0)
    return pl.pallas_call(
        body, grid=(nseq, C), name="ssd_fwd",
        in_specs=[pl.BlockSpec((L, SSD_INNER), row),
                  pl.BlockSpec((L, SSD_G * SSD_N), lambda b, c: (b * C + c, SSD_INNER // (SSD_G * SSD_N))),
                  pl.BlockSpec((L, SSD_G * SSD_N), lambda b, c: (b * C + c, SSD_INNER // (SSD_G * SSD_N) + 1)),
                  pl.BlockSpec((L, 128), row),
                  pl.BlockSpec((1, SSD_H), lambda b, c: (0, 0)),
                  pl.BlockSpec((1, SSD_H), lambda b, c: (0, 0)),
                  pl.BlockSpec((1, SSD_H), lambda b, c: (0, 0))],
        out_specs=[pl.BlockSpec((L, SSD_INNER), row),
                   pl.BlockSpec((1, 1, NP, 2 * SSD_P, SSD_N), lambda b, c: (b, c, 0, 0, 0))],
        out_shape=[jax.ShapeDtypeStruct((T, SSD_INNER), BF16),
                   jax.ShapeDtypeStruct((nseq, C, NP, 2 * SSD_P, SSD_N), F32)],
        scratch_shapes=[pltpu.VMEM((NP, 2 * SSD_P, SSD_N), F32)],
        compiler_params=_cp("parallel", "arbitrary"),
    )(xbc, xbc, xbc, dtr, dtb, alog, dsk)


def ssd_bwd(xbc, dtr, dtb, alog, dsk, prev, dy, nseq, comm=None):
    T = xbc.shape[0]
    S = T // nseq
    C = S // SSD_L
    L = SSD_L
    NP = SSD_H // 2

    def body(x_ref, b_ref, c_ref, dtr_ref, dtb_ref, alog_ref, dsk_ref, prev_ref, dy_ref,
             dxbc_ref, ddtr_ref, ddtb_ref, dalog_ref, ddsk_ref, ds_ref, stg_ref):
        bi = pl.program_id(0)
        ci = pl.program_id(1)

        @pl.when(ci == 0)
        def _():
            ds_ref[...] = jnp.zeros_like(ds_ref)

        @pl.when((ci == 0) & (bi == 0))
        def _():
            ddtb_ref[...] = jnp.zeros_like(ddtb_ref)
            dalog_ref[...] = jnp.zeros_like(dalog_ref)
            ddsk_ref[...] = jnp.zeros_like(ddsk_ref)

        dtr = dtr_ref[:, 0:SSD_H]
        dtb = dtb_ref[...]
        dt, a, cs, cs_t, lower = _ssd_common(dtr, dtb, alog_ref[...])
        upper = lax.broadcasted_iota(jnp.int32, (L, L), 1) >= lax.broadcasted_iota(jnp.int32, (L, L), 0)
        E = _head_expand()
        ET = _head_reduce()
        X = x_ref[...].astype(F32)
        dY = dy_ref[...].astype(F32)
        dt_e = _dot_sel(dt, E)
        cs_e = _dot_sel(cs, E)
        csl_e = cs_e[L - 1:L, :]
        f_e = jnp.exp(csl_e - cs_e)
        e_e = jnp.exp(cs_e)
        dsk_e = _dot_sel(dsk_ref[...], E)
        Xd = X * dt_e
        Xf = Xd * f_e
        lane = lax.broadcasted_iota(jnp.int32, (1, 2 * SSD_P), 1)
        rowp = lax.broadcasted_iota(jnp.int32, (2 * SSD_P, 1), 0)
        hsel = lax.broadcasted_iota(jnp.int32, (1, SSD_H), 1)
        dcs = jnp.zeros((L, SSD_H), F32)
        dcsl = jnp.zeros((1, SSD_H), F32)
        for g in range(SSD_G):
            Bg = b_ref[:, g * SSD_N:(g + 1) * SSD_N]
            Cg = c_ref[:, g * SSD_N:(g + 1) * SSD_N]
            cb = _dot(Cg, Bg, "nt")
            cbt = _dot(Bg, Cg, "nt")
            dB = jnp.zeros((L, SSD_N), F32)
            dC = jnp.zeros((L, SSD_N), F32)
            for pp in range(NP // SSD_G):
                p = g * (NP // SSD_G) + pp
                sl = slice(p * 2 * SSD_P, (p + 1) * 2 * SSD_P)
                Xd_p = Xd[:, sl]
                dY_p = dY[:, sl]
                dXd_p = jnp.zeros((L, 2 * SSD_P), F32)
                for q in range(2):
                    h = 2 * p + q
                    mask = (lane >= q * SSD_P) & (lane < (q + 1) * SSD_P)
                    col = cs[:, h:h + 1]
                    rw = cs_t[h:h + 1, :]
                    m = jnp.where(lower, jnp.exp(jnp.minimum(col - rw, 0.0)), 0.0)
                    mt = jnp.where(upper, jnp.exp(jnp.minimum(rw - col, 0.0)), 0.0)
                    dYm = jnp.where(mask, dY_p, 0.0)
                    dW = _dot(dYm, Xd_p, "nt")
                    dWt = _dot(Xd_p, dYm, "nt")
                    w = cb * m
                    wt = cbt * mt
                    dC = dC + _dot(dW * m, Bg)
                    dB = dB + _dot(dWt * mt, Cg)
                    dXd_p = dXd_p + jnp.where(mask, _dot(wt, dY_p), 0.0)
                    qcol = jnp.sum(dW * w, axis=1, keepdims=True) - jnp.sum(dWt * wt, axis=1, keepdims=True)
                    dcs = dcs + qcol * (hsel == h).astype(F32)
                S0 = prev_ref[0, 0, p]
                dSn = ds_ref[p]
                dZ = dY_p * e_e[:, sl]
                dC = dC + _dot(dZ, S0)
                h0 = 2 * p
                el0 = jnp.exp(cs[L - 1:L, h0:h0 + 1])
                el1 = jnp.exp(cs[L - 1:L, h0 + 1:h0 + 2])
                dec = jnp.where(rowp < SSD_P, el0, el1)
                ds_ref[p] = dSn * dec + _dot(dZ, Cg, "tn")
                dXf_p = _dot(Bg, dSn, "nt")
                dB = dB + _dot(Xf[:, sl], dSn)
                rs = jnp.sum(dSn * S0, axis=1, keepdims=True)
                s0 = jnp.sum(jnp.where(rowp < SSD_P, rs, 0.0), axis=0, keepdims=True) * el0
                s1 = jnp.sum(jnp.where(rowp >= SSD_P, rs, 0.0), axis=0, keepdims=True) * el1
                dcsl = dcsl + s0 * (hsel == h0).astype(F32) + s1 * (hsel == h0 + 1).astype(F32)
                y_off = _dot(Cg, S0, "nt") * e_e[:, sl]
                t1 = dY_p * y_off - dXf_p * Xf[:, sl]
                r1 = jnp.where(lane < SSD_P, t1, 0.0)
                c0 = jnp.sum(r1, axis=1, keepdims=True)
                c1 = jnp.sum(t1 - r1, axis=1, keepdims=True)
                dcs = dcs + c0 * (hsel == h0).astype(F32) + c1 * (hsel == h0 + 1).astype(F32)
                t2 = dXf_p * Xf[:, sl]
                r2 = jnp.where(lane < SSD_P, t2, 0.0)
                dcsl = dcsl + jnp.sum(r2, keepdims=True) * (hsel == h0).astype(F32) \
                    + jnp.sum(t2 - r2, keepdims=True) * (hsel == h0 + 1).astype(F32)
                stg_ref[:, sl] = dXd_p + dXf_p * f_e[:, sl]
            dxbc_ref[:, SSD_INNER + g * SSD_N:SSD_INNER + (g + 1) * SSD_N] = dB.astype(dxbc_ref.dtype)
            dxbc_ref[:, SSD_INNER + (SSD_G + g) * SSD_N:SSD_INNER + (SSD_G + g + 1) * SSD_N] = dC.astype(dxbc_ref.dtype)
        dXd = stg_ref[...]
        dxbc_ref[:, 0:SSD_INNER] = (dXd * dt_e + dsk_e * dY).astype(dxbc_ref.dtype)
        rowl = lax.broadcasted_iota(jnp.int32, (L, 1), 0)
        dcs = dcs + jnp.where(rowl == L - 1, dcsl, 0.0)
        dalpha = _dot_sel(upper.astype(F32), dcs, split="b")
        ddt = _dot_sel(dXd * X, ET, terms=2) + dalpha * a
        dalog_ref[...] += jnp.sum(dalpha * dt, axis=0, keepdims=True) * a
        ddtr = ddt * _sigmoid(dtr + dtb)
        spread = (lax.broadcasted_iota(jnp.int32, (SSD_H, 128), 0) == lax.broadcasted_iota(jnp.int32, (SSD_H, 128), 1)).astype(F32)
        ddtr_ref[...] = _dot(ddtr, spread).astype(ddtr_ref.dtype)
        ddtb_ref[...] += jnp.sum(ddtr, axis=0, keepdims=True)
        ddsk_ref[...] += jnp.sum(_dot_sel(dY * X, ET, terms=2), axis=0, keepdims=True)

    rowr = lambda b, c: (b * C + (C - 1 - c), 0)
    small = pl.BlockSpec((1, SSD_H), lambda b, c: (0, 0))
    return _call_with_comm(
        body, (nseq, C), "ssd_bwd",
        [pl.BlockSpec((L, SSD_INNER), rowr),
         pl.BlockSpec((L, SSD_G * SSD_N), lambda b, c: (b * C + (C - 1 - c), SSD_INNER // (SSD_G * SSD_N))),
         pl.BlockSpec((L, SSD_G * SSD_N), lambda b, c: (b * C + (C - 1 - c), SSD_INNER // (SSD_G * SSD_N) + 1)),
         pl.BlockSpec((L, 128), rowr), small, small, small,
         pl.BlockSpec((1, 1, NP, 2 * SSD_P, SSD_N), lambda b, c: (b, C - 1 - c, 0, 0, 0)),
         pl.BlockSpec((L, SSD_INNER), rowr)],
        [xbc, xbc, xbc, dtr, dtb, alog, dsk, prev, dy],
        [pl.BlockSpec((L, CONV_CH), rowr), pl.BlockSpec((L, 128), rowr), small, small, small],
        [jax.ShapeDtypeStruct((T, CONV_CH), BF16), jax.ShapeDtypeStruct((T, 128), BF16),
         jax.ShapeDtypeStruct((1, SSD_H), F32), jax.ShapeDtypeStruct((1, SSD_H), F32), jax.ShapeDtypeStruct((1, SSD_H), F32)],
        comm, scratch=[pltpu.VMEM((NP, 2 * SSD_P, SSD_N), F32), pltpu.VMEM((L, SSD_INNER), F32)], sem=("arbitrary", "arbitrary"))


SLOT = 128
ATT_T = 512
LOG2E = math.log2(math.e)
Q_SCALE = QK ** -0.5 * LOG2E


def _col_to_row(col):
    n = col.shape[0]
    eye = lax.broadcasted_iota(jnp.int32, (n, n), 0) == lax.broadcasted_iota(jnp.int32, (n, n), 1)
    return jnp.sum(jnp.where(eye, col, 0.0), axis=0, keepdims=True)


def attn_slot_fwd(q, k, v, nseq, comm=None):
    T = q.shape[0]
    S = T // nseq
    t = min(ATT_T, S)
    nb = S // t

    def body(q_ref, k_ref, v_ref, o_ref, lse_ref):
        causal = lax.broadcasted_iota(jnp.int32, (t, t), 1) <= lax.broadcasted_iota(jnp.int32, (t, t), 0)
        for qi in range(nb):
            qb = q_ref[qi * t:(qi + 1) * t, :]
            m = l = acc = None
            for kj in range(qi + 1):
                s = _dot(qb, k_ref[kj * t:(kj + 1) * t, :], "nt")
                if kj == qi:
                    s = jnp.where(causal, s, -1e30)
                bm = jnp.max(s, axis=1, keepdims=True)
                if kj == 0:
                    m = bm
                    p = jnp.exp2(s - m)
                    l = jnp.sum(p, axis=1, keepdims=True)
                    acc = _dot(p, v_ref[0:t, :])
                else:
                    m_new = jnp.maximum(m, bm)
                    corr = jnp.exp2(m - m_new)
                    p = jnp.exp2(s - m_new)
                    l = l * corr + jnp.sum(p, axis=1, keepdims=True)
                    acc = acc * corr + _dot(p, v_ref[kj * t:(kj + 1) * t, :])
                    m = m_new
            o_ref[qi * t:(qi + 1) * t, :] = (acc / l).astype(o_ref.dtype)
            lse_ref[0, 0, :, qi * t:(qi + 1) * t] = _col_to_row(m + jnp.log2(l))

    blk = pl.BlockSpec((S, SLOT), lambda b, h: (b, h))
    return _call_with_comm(
        body, (nseq, MLA_H), "attn_fwd", [blk, blk, blk], [q, k, v],
        [blk, pl.BlockSpec((1, 1, 1, S), lambda b, h: (b, h, 0, 0))],
        [jax.ShapeDtypeStruct((T, MLA_H * SLOT), BF16), jax.ShapeDtypeStruct((nseq, MLA_H, 1, S), F32)], comm)


def attn_slot_bwd(q, k, v, o, lse, do, nseq, comm=None):
    T = q.shape[0]
    S = T // nseq
    t = min(ATT_T, S)
    nb = S // t
    scale = QK ** -0.5

    def body(q_ref, k_ref, v_ref, o_ref, lse_ref, do_ref, dq_ref, dk_ref, dv_ref, dqa_ref):
        causal_t =lax.broadcasted_iota(jnp.int32, (t, t), 0) <= lax.broadcasted_iota(jnp.int32, (t, t), 1)
        ones = jnp.ones((8, SLOT), F32)
        delta = []
        for qi in range(nb):
            sl = slice(qi * t, (qi + 1) * t)
            prod = do_ref[sl, :].astype(F32) * o_ref[sl, :].astype(F32)
            delta.append(_dot_sel(ones, prod, "nt", split="b", terms=2)[0:1, :])
        for kj in range(nb):
            ks = slice(kj * t, (kj + 1) * t)
            kb = k_ref[ks, :]
            vb = v_ref[ks, :]
            dk = dv = None
            for qi in range(kj, nb):
                sl = slice(qi * t, (qi + 1) * t)
                qb = q_ref[sl, :]
                dob = do_ref[sl, :]
                st = _dot(kb, qb, "nt")
                pt = jnp.exp2(st - lse_ref[0, 0, :, sl])
                if qi == kj:
                    pt = jnp.where(causal_t, pt, 0.0)
                dpt = _dot(vb, dob, "nt")
                dst = (pt * (dpt - delta[qi])).astype(BF16)
                dvc = _dot(pt, dob)
                dkc = _dot(dst, qb) * (1.0 / LOG2E)
                dv = dvc if dv is None else dv + dvc
                dk = dkc if dk is None else dk + dkc
                dqc = _dot(dst, kb, "tn") * scale
                if kj > 0:
                    dqc = dqc + dqa_ref[sl, :]
                if qi == kj:
                    dq_ref[sl, :] = dqc.astype(dq_ref.dtype)
                else:
                    dqa_ref[sl, :] = dqc
            dk_ref[ks, :] = dk.astype(dk_ref.dtype)
            dv_ref[ks, :] = dv.astype(dv_ref.dtype)

    blk = pl.BlockSpec((S, SLOT), lambda b, h: (b, h))
    lse_spec = pl.BlockSpec((1, 1, 1, S), lambda b, h: (b, h, 0, 0))
    W = MLA_H * SLOT
    return _call_with_comm(
        body, (nseq, MLA_H), "attn_bwd", [blk, blk, blk, blk, lse_spec, blk], [q, k, v, o, lse, do], [blk, blk, blk],
        [jax.ShapeDtypeStruct((T, W), BF16)] * 3, comm, scratch=[pltpu.VMEM((S, SLOT), F32)])


def _rope_coeffs(cos, sin):
    half = ROPE // 2
    r = lax.broadcasted_iota(jnp.int32, (half, SLOT), 0)
    c = lax.broadcasted_iota(jnp.int32, (half, SLOT), 1)
    pc = ((c == r + NOPE) | (c == r + NOPE + half)).astype(F32)
    ps = (c == r + NOPE + half).astype(F32) - (c == r + NOPE).astype(F32)
    lane = lax.broadcasted_iota(jnp.int32, (1, SLOT), 1)
    return _dot_sel(cos, pc) + (lane < NOPE).astype(F32), _dot_sel(sin, ps)


def _rope_swap(x):
    W = x.shape[1]
    half = ROPE // 2
    lane = lax.broadcasted_iota(jnp.int32, (1, W), 1) & (SLOT - 1)
    up = pltpu.roll(x, W - half, axis=1)
    dn = pltpu.roll(x, half, axis=1)
    return jnp.where((lane >= NOPE) & (lane < NOPE + half), up, jnp.where((lane >= NOPE + half) & (lane < QK), dn, 0.0))


def rope_slot_fwd(q, kn, dtkr, cos, sin, name):
    def fn(qv, knv, krv, cv, sv):
        C, Sg = _rope_coeffs(cv, sv)
        ct, stl = jnp.tile(C, (1, MLA_H)), jnp.tile(Sg, (1, MLA_H))
        qo = (qv * ct + _rope_swap(qv) * stl) * Q_SCALE
        r = lax.broadcasted_iota(jnp.int32, (SLOT, SLOT), 0)
        c = lax.broadcasted_iota(jnp.int32, (SLOT, SLOT), 1)
        place = ((c == r + NOPE) & (r < ROPE)).astype(F32)
        kr = _dot_sel(krv, place)
        kr = kr * C + _rope_swap(kr) * Sg
        return qo, knv.astype(F32) + jnp.tile(kr, (1, MLA_H))
    W = MLA_H * SLOT
    return rowwise(fn, [q, kn, (dtkr, SLOT, 1), cos, sin], [], [(W, BF16), (W, BF16)], [], name)


def rope_slot_bwd(dq, dk, cos, sin, name):
    def fn(dqv, dkv, cv, sv):
        C, Sg = _rope_coeffs(cv, sv)
        ct, stl = jnp.tile(C, (1, MLA_H)), jnp.tile(Sg, (1, MLA_H))
        dqo = dqv * ct - _rope_swap(dqv) * stl
        tot = dkv[:, 0:SLOT]
        for h in range(1, MLA_H):
            tot = tot + dkv[:, h * SLOT:(h + 1) * SLOT]
        u = tot * C - _rope_swap(tot) * Sg
        r = lax.broadcasted_iota(jnp.int32, (SLOT, SLOT), 0)
        c = lax.broadcasted_iota(jnp.int32, (SLOT, SLOT), 1)
        unplace = ((r == c + NOPE) & (c < ROPE)).astype(F32)
        return dqo, dkv, _dot_sel(u, unplace, terms=2)
    W = MLA_H * SLOT
    return rowwise(fn, [dq, dk, cos, sin], [], [(W, BF16), (W, BF16), (SLOT, BF16)], [], name)


XA_BLK = 512


def xattn_fwd(q, k, v, nseq):
    T = q.shape[0]
    S = T // nseq
    M = k.shape[0] // nseq
    tq = min(XA_BLK, S)
    nq = S // tq
    scale = XA_D ** -0.5

    def body(q_ref, k_ref, v_ref, o_ref):
        s = _dot(q_ref[...], k_ref[...], "nt") * scale
        p = jnp.exp(s - jnp.max(s, axis=1, keepdims=True))
        p = p / jnp.sum(p, axis=1, keepdims=True)
        o_ref[...] = _dot(p, v_ref[...]).astype(o_ref.dtype)

    qs = pl.BlockSpec((tq, XA_D), lambda b, h, i: (b * nq + i, h))
    ks = pl.BlockSpec((M, XA_D), lambda b, h, i: (b, h))
    return pl.pallas_call(
        body, grid=(nseq, XA_H, nq), name="xattn_fwd", in_specs=[qs, ks, ks], out_specs=qs,
        out_shape=jax.ShapeDtypeStruct((T, XA_H * XA_D), BF16),
        compiler_params=_cp("parallel", "parallel", "parallel"),
    )(q, k, v)


def xattn_bwd(q, k, v, do, nseq):
    T = q.shape[0]
    S = T // nseq
    M = k.shape[0] // nseq
    tq = min(XA_BLK, S)
    nq = S // tq
    scale = XA_D ** -0.5

    def body(q_ref, k_ref, v_ref, do_ref, dq_ref, dk_ref, dv_ref):
        @pl.when(pl.program_id(2) == 0)
        def _():
            dk_ref[...] = jnp.zeros_like(dk_ref)
            dv_ref[...] = jnp.zeros_like(dv_ref)

        qb, kb, vb, dob = q_ref[...], k_ref[...], v_ref[...], do_ref[...]
        s = _dot(qb, kb, "nt") * scale
        p = jnp.exp(s - jnp.max(s, axis=1, keepdims=True))
        p = p / jnp.sum(p, axis=1, keepdims=True)
        dp = _dot(dob, vb, "nt")
        ds = p * (dp - jnp.sum(dp * p, axis=1, keepdims=True)) * scale
        dq_ref[...] = _dot(ds, kb).astype(dq_ref.dtype)
        dk_ref[...] += _dot(ds, qb, "tn")
        dv_ref[...] += _dot(p, dob, "tn")

    qs = pl.BlockSpec((tq, XA_D), lambda b, h, i: (b * nq + i, h))
    ks = pl.BlockSpec((M, XA_D), lambda b, h, i: (b, h))
    return pl.pallas_call(
        body, grid=(nseq, XA_H, nq), name="xattn_bwd", in_specs=[qs, ks, ks, qs], out_specs=[qs, ks, ks],
        out_shape=[jax.ShapeDtypeStruct((T, XA_H * XA_D), BF16), jax.ShapeDtypeStruct(k.shape, F32),
                   jax.ShapeDtypeStruct(k.shape, F32)],
        compiler_params=_cp("parallel", "parallel", "arbitrary"),
    )(q, k, v, do)


CONV_BLK = 256


def _shift_down(x, s, rows):
    if s == 0:
        return x
    return jnp.where(rows >= s, pltpu.roll(x, s, axis=0), 0.0)


def _shift_up(x, s, rows):
    if s == 0:
        return x
    S = x.shape[0]
    return jnp.where(rows < S - s, pltpu.roll(x, S - s, axis=0), 0.0)


def conv_fwd(x, w, b, nseq):
    T, CH = x.shape
    S = T // nseq

    def body(x_ref, w_ref, b_ref, o_ref):
        xv = x_ref[...].astype(F32)
        rows = lax.broadcasted_iota(jnp.int32, (S, 1), 0)
        c = jnp.zeros_like(xv) + b_ref[...]
        for kk in range(CONV_K):
            c = c + w_ref[kk:kk + 1, :] * _shift_down(xv, CONV_K - 1 - kk, rows)
        o_ref[...] = (c * _sigmoid(c)).astype(o_ref.dtype)

    xs = pl.BlockSpec((S, CONV_BLK), lambda j, bb: (bb, j))
    return pl.pallas_call(
        body, grid=(CH // CONV_BLK, nseq), name="conv_fwd",
        in_specs=[xs, pl.BlockSpec((CONV_K, CONV_BLK), lambda j, bb: (0, j)), pl.BlockSpec((1, CONV_BLK), lambda j, bb: (0, j))],
        out_specs=xs, out_shape=jax.ShapeDtypeStruct((T, CH), BF16),
        compiler_params=_cp("parallel", "parallel"),
    )(x, w, b)


def conv_bwd(x, w, b, dout, nseq):
    T, CH = x.shape
    S = T // nseq

    def body(x_ref, w_ref, b_ref, do_ref, dx_ref, dw_ref, db_ref):
        @pl.when(pl.program_id(1) == 0)
        def _():
            dw_ref[...] = jnp.zeros_like(dw_ref)
            db_ref[...] = jnp.zeros_like(db_ref)

        xv = x_ref[...].astype(F32)
        rows = lax.broadcasted_iota(jnp.int32, (S, 1), 0)
        c = jnp.zeros_like(xv) + b_ref[...]
        sh = [_shift_down(xv, CONV_K - 1 - kk, rows) for kk in range(CONV_K)]
        for kk in range(CONV_K):
            c = c + w_ref[kk:kk + 1, :] * sh[kk]
        sg = _sigmoid(c)
        dc = do_ref[...].astype(F32) * sg * (1.0 + c * (1.0 - sg))
        dx = jnp.zeros_like(xv)
        for kk in range(CONV_K):
            dx = dx + w_ref[kk:kk + 1, :] * _shift_up(dc, CONV_K - 1 - kk, rows)
            dw_ref[kk:kk + 1, :] += jnp.sum(dc * sh[kk], axis=0, keepdims=True)
        dx_ref[...] = dx.astype(dx_ref.dtype)
        db_ref[...] += jnp.sum(dc, axis=0, keepdims=True)

    xs = pl.BlockSpec((S, CONV_BLK), lambda j, bb: (bb, j))
    ws = pl.BlockSpec((CONV_K, CONV_BLK), lambda j, bb: (0, j))
    bs = pl.BlockSpec((1, CONV_BLK), lambda j, bb: (0, j))
    return pl.pallas_call(
        body, grid=(CH // CONV_BLK, nseq), name="conv_bwd",
        in_specs=[xs, ws, bs, xs], out_specs=[xs, ws, bs],
        out_shape=[jax.ShapeDtypeStruct((T, CH), BF16), jax.ShapeDtypeStruct((CONV_K, CH), F32),
                   jax.ShapeDtypeStruct((1, CH), F32)],
        compiler_params=_cp("parallel", "arbitrary"),
    )(x, w, b, dout)


def _dims(a, b, mode):
    M = a.shape[1] if mode[0] == "t" else a.shape[0]
    K = a.shape[0] if mode[0] == "t" else a.shape[1]
    N = b.shape[0] if mode[1] == "t" else b.shape[1]
    return M, K, N


def _tile(dim, prefs):
    for p in prefs:
        if dim % p == 0:
            return p
    return dim


def mm(groups, out_dtypes, name, tm=None, tn=None, tk=None, epi=None, extras=(), comm=None, sub=1, n_sum=0):
    a0, b0, m0 = groups[0][0]
    M, K0, N = _dims(a0, b0, m0)
    tm = tm or _tile(M, (1024, 512, 256, 128))
    tn = tn or _tile(N, (512, 256, 128))
    flat = [p for g in groups for p in g]
    nk = 1 if tk is None else K0 // tk
    in_specs, args = [], []
    for a, b, mode in flat:
        _, K, _ = _dims(a, b, mode)
        kb = K if tk is None else tk
        in_specs.append(pl.BlockSpec((kb, tm), lambda i, j, k: (k, i)) if mode[0] == "t"
                        else pl.BlockSpec((tm, kb), lambda i, j, k: (i, k)))
        in_specs.append(pl.BlockSpec((tn, kb), lambda i, j, k: (j, k)) if mode[1] == "t"
                        else pl.BlockSpec((kb, tn), lambda i, j, k: (k, j)))
        args += [a, b]
    for e in extras:
        in_specs.append(pl.BlockSpec((1, tn), lambda i, j, k: (0, j)) if e.shape[0] == 1 and M != 1
                        else pl.BlockSpec((tm, tn), lambda i, j, k: (i, j)))
        args.append(e)
    n_in = len(args)
    n_main = len(out_dtypes)
    n_out = n_main + n_sum
    assert n_sum == 0 or (tn == N and tk is None)
    ng = len(groups)
    sizes = [len(g) for g in groups]

    def body(*refs):
        ins, outs, accs = refs[:n_in], refs[n_in:n_in + n_out], refs[n_in + n_out:]
        kk = pl.program_id(2)

        def dots(rs):
            vals, pos = [], 0
            for gi in range(ng):
                acc = None
                for _ in range(sizes[gi]):
                    mode = flat[pos // 2][2]
                    av = ins[pos][:, rs] if mode[0] == "t" else ins[pos][rs, :]
                    d = _dot(av, ins[pos + 1][...], mode)
                    acc = d if acc is None else acc + d
                    pos += 2
                vals.append(acc)
            return vals

        def finish(accv, rs, first_chunk=True):
            ex = [(r[...] if r.shape[0] == 1 and tm != 1 else r[rs, :]).astype(F32) for r in ins[2 * len(flat):]]
            res = epi(accv, ex) if epi is not None else tuple(accv)
            for o, r in zip(outs[:n_main], res[:n_main]):
                o[rs, :] = r.astype(o.dtype)
            for o, r in zip(outs[n_main:], res[n_main:]):
                if first_chunk:
                    @pl.when(pl.program_id(0) == 0)
                    def _():
                        o[...] = r

                    @pl.when(pl.program_id(0) > 0)
                    def _():
                        o[...] += r
                else:
                    o[...] += r

        if nk == 1:
            for r in range(sub):
                rs = slice(r * (tm // sub), (r + 1) * (tm // sub))
                finish(dots(rs), rs, r == 0)
        else:
            vals = dots(slice(0, tm))
            finish = functools.partial(finish, rs=slice(0, tm))
            @pl.when(kk == 0)
            def _():
                for ar, vv in zip(accs, vals):
                    ar[...] = vv

            @pl.when(kk > 0)
            def _():
                for ar, vv in zip(accs, vals):
                    ar[...] += vv

            @pl.when(kk == nk - 1)
            def _():
                finish([ar[...] for ar in accs])

    grid = (M // tm, N // tn, nk)
    out_specs = [pl.BlockSpec((tm, tn), lambda i, j, k: (i, j)) for _ in out_dtypes] \
        + [pl.BlockSpec((1, tn), lambda i, j, k: (0, j))] * n_sum
    out_shape = [jax.ShapeDtypeStruct((M, N), dt) for dt in out_dtypes] + [jax.ShapeDtypeStruct((1, N), F32)] * n_sum
    scratch = [pltpu.VMEM((tm, tn), F32) for _ in range(ng if nk > 1 else 0)]
    sem = ("arbitrary" if n_sum else "parallel", "parallel", "arbitrary")
    if comm is not None:
        body = _attach(comm, body, n_in, n_out, *_grid_ends(grid))
        in_specs, args = in_specs + [HBM_SPEC] * len(comm.inputs), args + comm.inputs
        out_specs, out_shape = out_specs + [HBM_SPEC] * len(comm.out_shapes), out_shape + comm.out_shapes
        scratch, sem = scratch + comm.sems, ("arbitrary",) * 3
    return pl.pallas_call(body, grid=grid, name=name, in_specs=in_specs, out_specs=out_specs, out_shape=out_shape,
                          scratch_shapes=scratch, compiler_params=_cp(*sem))(*args)


def mm1(a, b, mode, out_dtype, name, **kw):
    return mm([[(a, b, mode)]], [out_dtype], name, **kw)[0]


ROW_BLK = 512


def rowwise(fn, rows, consts, outs, accs, name, tb=ROW_BLK):
    rows = [r if isinstance(r, tuple) else (r, r.shape[1], 0) for r in rows]
    T = rows[0][0].shape[0]
    tb = min(tb, T)
    n_r, n_c, n_o, n_a = len(rows), len(consts), len(outs), len(accs)

    def body(*refs):
        vals = [r[...].astype(F32) for r in refs[:n_r + n_c]]
        res = fn(*vals)
        o_refs = refs[n_r + n_c:n_r + n_c + n_o]
        a_refs = refs[n_r + n_c + n_o:]
        for o, r in zip(o_refs, res[:n_o]):
            o[...] = r.astype(o.dtype)
        if n_a:
            @pl.when(pl.program_id(0) == 0)
            def _():
                for ar in a_refs:
                    ar[...] = jnp.zeros_like(ar)
            for ar, r in zip(a_refs, res[n_o:]):
                ar[...] += r

    return pl.pallas_call(
        body, grid=(T // tb,), name=name,
        in_specs=[pl.BlockSpec((tb, w), functools.partial(lambda i, j: (i, j), j=j)) for _, w, j in rows]
        + [pl.BlockSpec(c.shape, lambda i: (0, 0)) for c in consts],
        out_specs=[pl.BlockSpec((tb, d), lambda i: (i, 0)) for d, _ in outs]
        + [pl.BlockSpec(s, lambda i: (0, 0)) for s in accs],
        out_shape=[jax.ShapeDtypeStruct((T, d), dt) for d, dt in outs]
        + [jax.ShapeDtypeStruct(s, F32) for s in accs],
        compiler_params=_cp("arbitrary" if n_a else "parallel"),
    )(*[r[0] for r in rows], *consts)


def _rms_stats(x):
    r = lax.rsqrt(jnp.mean(x * x, axis=-1, keepdims=True) + EPS)
    return r, x * r


def _rms_bwd(x, g, dy):
    r, xn = _rms_stats(x)
    dyg = dy * g
    dx = r * (dyg - xn * jnp.mean(dyg * xn, axis=-1, keepdims=True))
    return dx, jnp.sum(dy * xn, axis=0, keepdims=True)


def rms_fwd(x, g, name):
    return rowwise(lambda xv, gv: (_rms_stats(xv)[1] * gv,), [x], [g], [(x.shape[1], BF16)], [], name)[0]


def rms_bwd(x, g, dy, name, resid=None, dx_dtype=F32):
    def fn(*v):
        if resid is None:
            xv, dyv, gv = v
            dx, dg = _rms_bwd(xv, gv, dyv)
        else:
            xv, dyv, rv, gv = v
            dx, dg = _rms_bwd(xv, gv, dyv)
            dx = dx + rv
        return dx, dg
    rows = [x, dy] + ([] if resid is None else [resid])
    return rowwise(fn, rows, [g], [(x.shape[1], dx_dtype)], [(1, x.shape[1])], name)


def mm_rms_bwd(pairs, x, g, name, resid=None, dx_dtype=F32, comm=None):
    def epi(accs, ex):
        dx, dg = _rms_bwd(ex[0], ex[-1], accs[0])
        return (dx if resid is None else dx + ex[1]), dg
    extras = [x] + ([] if resid is None else [resid]) + [g]
    return mm([pairs], [dx_dtype], name, tm=min(256, x.shape[0]), tn=x.shape[1], epi=epi, extras=extras, comm=comm, n_sum=1)


def mm_resid(a, b, x, g, wgt, name):
    epi = lambda accs, ex: (accs[0], ex[0] + wgt * _rms_stats(accs[0])[1] * ex[1])
    return mm([[(a, b, "nn")]], [F32, F32], name, tm=min(512, a.shape[0]), tn=b.shape[1], epi=epi, extras=[x, g], sub=2)


def resid_bwd(h, g, dy, wgt, name):
    def fn(hv, dyv, gv):
        dx, dg = _rms_bwd(hv, gv, dyv)
        return wgt * dx, wgt * dg
    return rowwise(fn, [h, dy], [g], [(h.shape[1], BF16)], [(1, h.shape[1])], name)


def _silu_parts(g):
    s = _sigmoid(g)
    return g * s, s * (1.0 + g * (1.0 - s))


def gated_norm_fwd(y, z, g, name):
    W = SSD_INNER // SSD_G

    def fn(yv, zv, gv):
        yg = yv * _silu_parts(zv)[0]
        return (jnp.concatenate([_rms_stats(yg[:, i * W:(i + 1) * W])[1] for i in range(SSD_G)], axis=1) * gv,)
    return rowwise(fn, [y, z], [g], [(SSD_INNER, BF16)], [], name)[0]


def gated_norm_bwd(y, z, dyn, g, name):
    W = SSD_INNER // SSD_G

    def fn(yv, zv, dv, gv):
        sil, dsil = _silu_parts(zv)
        yg = yv * sil
        parts = [_rms_bwd(yg[:, i * W:(i + 1) * W], gv[:, i * W:(i + 1) * W], dv[:, i * W:(i + 1) * W]) for i in range(SSD_G)]
        dyg = jnp.concatenate([p[0] for p in parts], axis=1)
        dg = jnp.concatenate([p[1] for p in parts], axis=1)
        return dyg * sil, dyg * yv * dsil, dg
    return rowwise(fn, [y, z, dyn], [g], [(SSD_INNER, BF16), (SSD_INNER, BF16)], [(1, SSD_INNER)], name)


def merge_fwd(gl, ys, ym, gb, name):
    def fn(glv, ysv, ymv, gbv):
        gt = _sigmoid(glv + gbv)
        return (gt[:, :D] * ysv + gt[:, D:] * ymv,)
    return rowwise(fn, [gl, ys, ym], [gb], [(D, BF16)], [], name)[0]


def merge_bwd(gl, ys, ym, dm, gb, name):
    def fn(glv, ysv, ymv, dmv, gbv):
        gt = _sigmoid(glv + gbv)
        gs, gm = gt[:, :D], gt[:, D:]
        dgl = jnp.concatenate([dmv * ysv * gs * (1.0 - gs), dmv * ymv * gm * (1.0 - gm)], axis=1)
        return dmv * gs, dmv * gm, dgl, jnp.sum(dgl, axis=0, keepdims=True)
    return rowwise(fn, [gl, ys, ym, dm], [gb], [(D, BF16), (D, BF16), (2 * D, BF16)], [(1, 2 * D)], name)


def loss_head(y, tgt, name):
    def fn(yv, tv):
        d = yv - tv
        part = 0.5 * jnp.sum(jnp.sum(d * d, axis=1, keepdims=True), axis=0, keepdims=True) / D
        return d / D, jnp.broadcast_to(part, (1, 128))
    return rowwise(fn, [y, tgt], [], [(D, F32)], [(1, 128)], name)


def adamw(w, g, m, v, name):
    R, C = w.shape
    tb = _tile(R, (256, 128, 64, 32, 16, 8))

    def fn(wv, gv, mv, vv):
        mn = B1 * mv + (1.0 - B1) * gv
        vn = B2 * vv + (1.0 - B2) * (gv * gv)
        mh = mn / (1.0 - B1 ** STEP)
        vh = vn / (1.0 - B2 ** STEP)
        return -LR * (mh / (jnp.sqrt(vh) + AEPS) + WD * wv), mn, vn
    return rowwise(fn, [w, g, m, v], [], [(C, F32)] * 3, [], name, tb=tb)


def _me():
    return lax.axis_index("x"), lax.axis_index("y"), lax.axis_index("c")


def _dev_index():
    x, y, c = _me()
    return 4 * x + 2 * y + c


HBM_SPEC = pl.BlockSpec(memory_space=pl.ANY)


class GatherComm:
    def __init__(self, shards):
        self.inputs = list(shards)
        n = len(shards)
        self.out_shapes = [jax.ShapeDtypeStruct((N_DEV,) + s.shape, s.dtype) for s in shards]
        self.sems = [pltpu.SemaphoreType.DMA((7 * n,)), pltpu.SemaphoreType.DMA((7 * n,)), pltpu.SemaphoreType.DMA((n,))]

    def _plan(self, x_refs, out_refs, sems):
        send_sems, recv_sems, local_sems = sems
        n = len(x_refs)
        x, y, c = _me()
        me, sibling = (x, y, c), (x, y, 1 - c)
        chips = [(1 - x, y), (x, 1 - y), (1 - x, 1 - y)]

        def slot(i, px, py, pc):
            return out_refs[i].at[4 * px + 2 * py + pc]

        def copy(i, k, block, to, src=None):
            return pltpu.make_async_remote_copy(
                src_ref=slot(i, *block) if src is None else src, dst_ref=slot(i, *block),
                send_sem=send_sems.at[7 * i + k], recv_sem=recv_sems.at[7 * i + k], device_id=to, device_id_type=MESH)

        mine = [pltpu.make_async_copy(x_refs[i], slot(i, *me), local_sems.at[i]) for i in range(n)]
        first = []
        for i in range(n):
            first.append(copy(i, 0, me, sibling, src=x_refs[i]))
            first += [copy(i, 1 + j, me, (*chip, c), src=x_refs[i]) for j, chip in enumerate(chips)]
        passed = [[copy(i, 4 + j, (*chip, c), sibling) for j, chip in enumerate(chips)] for i in range(n)]
        from_ici = [[copy(i, 1 + j, (*chip, c), me) for j, chip in enumerate(chips)] for i in range(n)]
        from_sib = [[copy(i, 0, sibling, me)] + [copy(i, 4 + j, (*chip, 1 - c), me) for j, chip in enumerate(chips)] for i in range(n)]
        return mine, first, passed, from_ici, from_sib

    def start(self, x_refs, out_refs, sems):
        mine, first, _, _, _ = self._plan(x_refs, out_refs, sems)
        for cp in mine + first:
            cp.start()

    def finish(self, x_refs, out_refs, sems):
        mine, first, passed, from_ici, from_sib = self._plan(x_refs, out_refs, sems)
        for i in range(len(x_refs)):
            for arrival, forward in zip(from_ici[i], passed[i]):
                arrival.wait_recv()
                forward.start()
        for row in from_sib:
            for arrival in row:
                arrival.wait_recv()
        for cp in first + [cp for row in passed for cp in row]:
            cp.wait_send()
        for cp in mine:
            cp.wait()


def run_comm(comm, name):
    n_in, n_out = len(comm.inputs), len(comm.out_shapes)

    def body(*refs):
        ins, outs, sems = refs[:n_in], refs[n_in:n_in + n_out], refs[n_in + n_out:]
        comm.start(ins, outs, sems)
        comm.finish(ins, outs, sems)

    return pl.pallas_call(body, name=name, out_shape=comm.out_shapes, in_specs=[HBM_SPEC] * n_in,
                          out_specs=[HBM_SPEC] * n_out, scratch_shapes=comm.sems)(*comm.inputs)


def _attach(comm, body, n_in, n_out, first, last):
    if comm is None:
        return body
    ci, co, cs = len(comm.inputs), len(comm.out_shapes), len(comm.sems)

    def wrapped(*refs):
        h_in, c_in = refs[:n_in], refs[n_in:n_in + ci]
        h_out, c_out = refs[n_in + ci:n_in + ci + n_out], refs[n_in + ci + n_out:n_in + ci + n_out + co]
        rest = refs[n_in + ci + n_out + co:]
        h_scr, c_sem = rest[:len(rest) - cs], rest[len(rest) - cs:]

        @pl.when(first())
        def _():
            comm.start(c_in, c_out, c_sem)

        body(*h_in, *h_out, *h_scr)

        @pl.when(last())
        def _():
            comm.finish(c_in, c_out, c_sem)

    return wrapped


def _grid_ends(grid):
    first = lambda: functools.reduce(lambda a, b: a & b, [pl.program_id(i) == 0 for i in range(len(grid))])
    last = lambda: functools.reduce(lambda a, b: a & b, [pl.program_id(i) == g - 1 for i, g in enumerate(grid)])
    return first, last


def _call_with_comm(body, grid, name, in_specs, args, out_specs, out_shape, comm, scratch=(), sem=None):
    sem = sem or ("parallel",) * len(grid)
    scratch = list(scratch)
    if comm is not None:
        body = _attach(comm, body, len(args), len(out_shape), *_grid_ends(grid))
        in_specs, args = in_specs + [HBM_SPEC] * len(comm.inputs), args + comm.inputs
        out_specs, out_shape = out_specs + [HBM_SPEC] * len(comm.out_shapes), out_shape + comm.out_shapes
        scratch, sem = scratch + comm.sems, ("arbitrary",) * len(grid)
    return pl.pallas_call(body, grid=grid, name=name, in_specs=in_specs, out_specs=out_specs, out_shape=out_shape,
                          scratch_shapes=scratch, compiler_params=_cp(*sem))(*args)


class ScatterComm:
    def __init__(self, groups):
        self.sizes = [len(g) for g in groups]
        self.rows = [[pc.shape[1] for pc in g] for g in groups]
        ng = len(groups)
        self.inputs = [pc for g in groups for pc in g]
        self.out_shapes = [jax.ShapeDtypeStruct((N_DEV, sum(self.rows[gi]), g[0].shape[2]), g[0].dtype) for gi, g in enumerate(groups)]
        self.sems = [pltpu.SemaphoreType.DMA((7 * ng,)), pltpu.SemaphoreType.DMA((7 * ng,)), pltpu.SemaphoreType.DMA((ng,))]

    def _peers(self):
        x, y, c = _me()
        out = []
        for k in range(1, N_DEV):
            px = 1 - x if k & 4 else x
            py = 1 - y if k & 2 else y
            pc = 1 - c if k & 1 else c
            out.append((k, 4 * px + 2 * py + pc, dict(device_id=(px, py, pc), device_id_type=MESH)))
        return 4 * x + 2 * y + c, out

    def start(self, ins, outs, sems):
        send_sems, recv_sems, local_sems = sems
        me, peers = self._peers()
        pos = 0
        for gi, size in enumerate(self.sizes):
            for i, pc in enumerate(ins[pos:pos + size]):
                dst = outs[gi].at[me, pl.ds(sum(self.rows[gi][:i]), self.rows[gi][i])]
                pltpu.make_async_copy(pc.at[me], dst, local_sems.at[gi]).start()
                for k, peer, kw in peers:
                    pltpu.make_async_remote_copy(src_ref=pc.at[peer], dst_ref=dst, send_sem=send_sems.at[7 * gi + k - 1],
                                                 recv_sem=recv_sems.at[7 * gi + k - 1], **kw).start()
            pos += size

    def finish(self, ins, outs, sems):
        send_sems, recv_sems, local_sems = sems
        me, peers = self._peers()
        whole = [pltpu.make_async_remote_copy(src_ref=outs[gi].at[peer], dst_ref=outs[gi].at[peer],
                                              send_sem=send_sems.at[7 * gi + k - 1], recv_sem=recv_sems.at[7 * gi + k - 1], **kw)
                 for gi in range(len(self.sizes)) for k, peer, kw in peers]
        for cp in whole:
            cp.wait_recv()
        for cp in whole:
            cp.wait_send()
        for gi in range(len(self.sizes)):
            pltpu.make_async_copy(outs[gi].at[me], outs[gi].at[me], local_sems.at[gi]).wait()


def sum_slots(recv, name, tr):
    n, R, C = recv.shape

    def body(r_ref, o_ref):
        acc = r_ref[0].astype(F32)
        for s in range(1, n):
            acc = acc + r_ref[s].astype(F32)
        o_ref[...] = acc

    return pl.pallas_call(
        body, grid=(R // tr,), name=name,
        in_specs=[pl.BlockSpec((n, tr, C), lambda i: (0, i, 0))], out_specs=pl.BlockSpec((tr, C), lambda i: (i, 0)),
        out_shape=jax.ShapeDtypeStruct((R, C), F32), compiler_params=_cp("parallel"),
    )(recv)


PACK_W, FLAT_W = 1024, 128
MAIN = [
    ("ffn1_w_gate", "col"), ("ffn1_w_up", "col"), ("ffn1_w_down", "row"),
    ("ffn2_w_gate", "col"), ("ffn2_w_up", "col"), ("ffn2_w_down", "row"),
    ("w_ssd_proj", "row"), ("w_mla_proj", "row"), ("w_out", "row"),
    ("w_xq", "row"), ("w_xk", "row"), ("w_xv", "row"), ("w_xo", "row"),
    ("w_uk", "col"), ("w_uv", "col"),
]
FLAT = [("w_in", "col"), ("w_uq", "col")]
BIG = MAIN + FLAT
SMALL = ["ffn1_pre_g", "ffn1_post_g", "mix_pre_g", "conv_b", "dt_bias", "a_log", "d_skip", "ssd_norm_g", "q_norm_g",
         "kv_norm_g", "gate_bias", "mix_post_g", "xa_pre_g", "mem_norm_g", "xa_post_g", "ffn2_pre_g", "ffn2_post_g"]
WEIGHTS = ['ffn1_pre_g', 'ffn1_w_gate', 'ffn1_w_up', 'ffn1_w_down', 'ffn1_post_g', 'mix_pre_g', 'w_in', 'conv_w', 'conv_b',
           'dt_bias', 'a_log', 'd_skip', 'ssd_norm_g', 'w_ssd_proj', 'q_norm_g', 'w_uq', 'kv_norm_g', 'w_uk', 'w_uv',
           'w_mla_proj', 'gate_bias', 'w_out', 'mix_post_g', 'xa_pre_g', 'mem_norm_g', 'w_xq', 'w_xk', 'w_xv', 'w_xo',
           'xa_post_g', 'ffn2_pre_g', 'ffn2_w_gate', 'ffn2_w_up', 'ffn2_w_down', 'ffn2_post_g']


def _pack_rows(w, kind, width):
    m = w[0].T if kind == "col" else w[0]
    return m.reshape(-1, width)


KIND = dict(BIG)
GATHER_PLAN = {
    "first": (["ffn1_w_gate", "ffn1_w_up", "ffn1_w_down"], []),
    "ffn1_gate_up": (["w_ssd_proj", "w_mla_proj", "w_out", "w_uk", "w_uv"], ["w_in", "w_uq"]),
    "attn_fwd": (["w_xq", "w_xk", "w_xv", "w_xo", "ffn2_w_gate", "ffn2_w_up", "ffn2_w_down"], []),
}
SCATTER_PLAN = {
    "attn_bwd": (["ffn2_w_gate", "ffn2_w_up", "ffn2_w_down", "w_xq", "w_xk", "w_xv", "w_xo"], []),
    "ssd_bwd": (["w_ssd_proj", "w_mla_proj", "w_out", "w_uk", "w_uv"], ["w_uq"]),
    "in_bwd": ([], ["w_in#0"]),
    "ffn1:down_bwd": ([], ["w_in#1"]),
    "ffn1:dwg": (["ffn1_w_down"], []),
    "ffn1:dwu": (["ffn1_w_gate"], []),
    "ffn1:gate_up_bwd": (["ffn1_w_up"], []),
}
PARTS = {"w_in#0": ("w_in", 0, 2656), "w_in#1": ("w_in", 2656, 5296)}


class Stage:
    def __init__(self, w):
        self.w = w
        self.width = {n: PACK_W if (n, k) in MAIN else FLAT_W for n, k in BIG}
        self.nrows = {n: math.prod(w[n].shape) // self.width[n] for n, _ in BIG}
        self.recv = {}

    def _shards(self, tag):
        names_main, names_flat = GATHER_PLAN[tag]
        pack = lambda n: _pack_rows(self.w[n], KIND[n], self.width[n]).astype(BF16)
        shards = []
        if names_main:
            shards.append(jnp.concatenate([pack(n) for n in names_main], axis=0))
        if names_flat:
            bits = lax.bitcast_convert_type(self.w["conv_w"][0], BF16).reshape(-1, FLAT_W)
            shards.append(_pad_rows(jnp.concatenate([pack(n) for n in names_flat] + [bits], axis=0), 16))
        return shards

    def gather(self, tag):
        return GatherComm(self._shards(tag))

    def gathered(self, tag, outs, W, p):
        names_main, names_flat = GATHER_PLAN[tag]
        outs = list(outs)
        for names in (names_main, names_flat):
            if not names:
                continue
            buf, r0 = outs.pop(0), 0
            for n in names:
                K = self.w[n].shape[1] if KIND[n] == "col" else PACK_W
                W[n] = buf[:, r0:r0 + self.nrows[n]].reshape(-1, K)
                r0 += self.nrows[n]
            if names is names_flat:
                cw = self.w["conv_w"]
                nbits = 2 * math.prod(cw.shape) // FLAT_W
                bits = buf[:, r0:r0 + nbits].reshape((N_DEV,) + cw.shape[1:] + (2,))
                p["conv_w"] = lax.bitcast_convert_type(bits, F32).transpose(1, 0, 2).reshape(cw.shape[1], -1)

    def scatter(self, tag, gw):
        if tag not in SCATTER_PLAN:
            return None
        def piece(n):
            if n in PARTS:
                base, r0, r1 = PARTS[n]
                return gw[base].reshape(N_DEV, self.nrows[base], self.width[base])[:, r0:r1]
            return gw[n].reshape(N_DEV, self.nrows[n], self.width[n])
        return ScatterComm([[piece(n) for n in names] for names in SCATTER_PLAN[tag] if names])

    def scattered(self, tag, outs):
        if tag in SCATTER_PLAN:
            self.recv[tag] = outs


def _pad_rows(a, mult):
    r = (-a.shape[0]) % mult
    return a if r == 0 else jnp.concatenate([a, jnp.zeros((r,) + a.shape[1:], a.dtype)], axis=0)


def _pack_small(vals, loss_row=None, conv_w=None):
    rows = []
    for v in vals:
        f = v.reshape(-1)
        f = jnp.concatenate([f, jnp.zeros(((-f.shape[0]) % 128,), F32)])
        rows.append(f.reshape(-1, 128))
    if conv_w is not None:
        rows.append(conv_w.reshape(-1, 128))
    if loss_row is not None:
        rows.append(loss_row)
    return _pad_rows(jnp.concatenate(rows, axis=0), 8)


def _unpack_small(buf, shapes):
    out, r = [], 0
    for shp in shapes:
        n = math.prod(shp)
        nr = -(-n // 128)
        out.append(buf[r:r + nr].reshape(-1)[:n].reshape(shp))
        r += nr
    return out, r


def _tn(a, b, name, out_dtype=BF16, comm=None):
    M, N = a.shape[1], b.shape[1]
    T = a.shape[0]
    tm = M if M <= 1536 else M // 2
    tk = 1024 if T % 1024 == 0 and T > 1024 else None
    res = mm([[(a, b, "tn")]], [out_dtype], name, tm=tm, tn=N, tk=tk, comm=comm)
    return res[0] if comm is None else (res[0], res[1:])


class NoStage:
    def gather(self, tag):
        return None

    def gathered(self, tag, outs, W, p):
        pass

    def scatter(self, tag, gw):
        return None

    def scattered(self, tag, outs):
        pass


def _ffn_fwd(x, gpre, gpost, wg_t, wu_t, wd, tag, comm=None):
    h = rms_fwd(x, gpre, tag + "_pre")
    def swi(accs, ex):
        sil, dsil = _silu_parts(accs[0])
        return sil, accs[1] * dsil, sil * accs[1]
    res = mm([[(h, wg_t, "nt")], [(h, wu_t, "nt")]], [BF16, BF16, BF16], tag + "_gate_up", tn=256, epi=swi, comm=comm,
             sub=4 if h.shape[0] % 1024 == 0 else 1)
    G, U, A = res[:3]
    H, y = mm_resid(A, wd, x, gpost, FFN_RES, tag + "_down")
    return y, (x, h, G, U, A, H), res[3:]


def _ffn_bwd(dy, saved, gpre, gpost, wg_t, wu_t, wd, tag, stage, gw):
    x, h, G, U, A, H = saved
    dH, dgpost = resid_bwd(H, gpost, dy, FFN_RES, tag + "_post_bwd")

    def dswi(accs, ex):
        return accs[0] * ex[1], accs[0] * ex[0]

    def hosted(where, call):
        comm = stage.scatter(tag + ":" + where, gw)
        res = call(comm)
        if comm is None:
            return res
        stage.scattered(tag + ":" + where, res[1])
        return res[0]

    res = hosted("down_bwd", lambda comm: (lambda r: r if comm is None else (r[:2], r[2:]))(
        mm([[(dH, wd, "nt")]], [BF16, BF16], tag + "_down_bwd", tn=256, epi=dswi, extras=[G, U], comm=comm,
           sub=4 if dH.shape[0] % 1024 == 0 else 1)))
    dG, dU = res
    gw[tag + "_w_down"] = _tn(A, dH, tag + "_dwd")
    gw[tag + "_w_gate"] = hosted("dwg", lambda comm: _tn(dG, h, tag + "_dwg", comm=comm))
    gw[tag + "_w_up"] = hosted("dwu", lambda comm: _tn(dU, h, tag + "_dwu", comm=comm))
    dx, dgpre = hosted("gate_up_bwd", lambda comm: (lambda r: r[:2] if comm is None else (r[:2], r[2:]))(
        mm_rms_bwd([(dG, wg_t, "nn"), (dU, wu_t, "nn")], x, gpre, tag + "_gate_up_bwd", resid=dy, comm=comm)))
    return dx, dgpre, dgpost


def _rope_tables(positions):
    inv = ROPE_THETA ** (-jnp.arange(0, ROPE, 2, dtype=F32) / ROPE)
    ang = positions.astype(F32).reshape(-1)[:, None] * inv
    return jnp.cos(ang), jnp.sin(ang)


def _local_step(x, mem, positions, tgt, W, p, stage=None):
    stage = stage or NoStage()
    nseq = x.shape[0]
    T = nseq * x.shape[1]
    x0 = x.reshape(T, D)
    mem2 = mem.reshape(-1, D)
    cos, sin = _rope_tables(positions)

    x1, ffn1, arrived = _ffn_fwd(x0, p["ffn1_pre_g"], p["ffn1_post_g"], W["ffn1_w_gate"], W["ffn1_w_up"], W["ffn1_w_down"],
                                 "ffn1", comm=stage.gather("ffn1_gate_up"))
    stage.gathered("ffn1_gate_up", arrived, W, p)

    w_in_t = W["w_in"]
    bounds = [0]
    for n in (SSD_INNER, CONV_CH, SSD_H, QR, KVR, ROPE, 2 * D):
        bounds.append(bounds[-1] + n)
    wt_z, wt_xbc, wt_dt, wt_q, wt_kv, wt_kr, wt_gate = [w_in_t[bounds[i]:bounds[i + 1]] for i in range(7)]
    wt_dt, wt_kr = _pad_rows(wt_dt, SLOT), _pad_rows(wt_kr, SLOT)
    wt_dtkr = jnp.concatenate([wt_dt, wt_kr], axis=0)
    hm = rms_fwd(x1, p["mix_pre_g"], "mix_pre")
    z = mm1(hm, wt_z, "nt", BF16, "in_z")
    xbc = mm1(hm, wt_xbc, "nt", BF16, "in_xbc")
    q_c = mm1(hm, wt_q, "nt", F32, "in_q", tn=QR)
    kv_c = mm1(hm, wt_kv, "nt", F32, "in_kv")
    dtkr = mm1(hm, wt_dtkr, "nt", F32, "in_dtkr")
    gl = mm1(hm, wt_gate, "nt", BF16, "in_gate")

    xbc_act = conv_fwd(xbc, p["conv_w"], p["conv_b"], nseq)
    y_ssd_core, prev = ssd_fwd(xbc_act, dtkr, p["dt_bias"], p["a_log"], p["d_skip"], nseq)
    yn = gated_norm_fwd(y_ssd_core, z, p["ssd_norm_g"], "ssd_norm")
    y_ssd = mm1(yn, W["w_ssd_proj"], "nn", BF16, "ssd_proj")

    slot_rows = lambda wt, per: jnp.pad(wt.reshape(MLA_H, per, -1), ((0, 0), (0, SLOT - per), (0, 0))).reshape(MLA_H * SLOT, -1)
    wq_s, wk_s, wv_s = slot_rows(W["w_uq"], QK), slot_rows(W["w_uk"], NOPE), slot_rows(W["w_uv"], VD)
    wo_s = slot_rows(W["w_mla_proj"], VD)
    qn = rms_fwd(q_c, p["q_norm_g"], "q_norm")
    q_s = mm1(qn, wq_s, "nt", BF16, "uq")
    kvn = rms_fwd(kv_c, p["kv_norm_g"], "kv_norm")
    kn_s = mm1(kvn, wk_s, "nt", BF16, "uk")
    v_s = mm1(kvn, wv_s, "nt", BF16, "uv")
    cos16, sin16 = cos, sin
    Qc, Kc = rope_slot_fwd(q_s, kn_s, dtkr, cos16, sin16, "rope")
    o_s, lse, *arrived = attn_slot_fwd(Qc, Kc, v_s, nseq, comm=stage.gather("attn_fwd"))
    stage.gathered("attn_fwd", arrived, W, p)
    y_mla = mm1(o_s, wo_s, "nn", BF16, "mla_proj")

    merged = merge_fwd(gl, y_ssd, y_mla, p["gate_bias"], "merge")
    hmix, x2 = mm_resid(merged, W["w_out"], x1, p["mix_post_g"], 1.0, "mix_out")

    hq = rms_fwd(x2, p["xa_pre_g"], "xa_pre")
    mn = rms_fwd(mem2, p["mem_norm_g"], "mem_norm")
    xq = mm1(hq, W["w_xq"], "nn", BF16, "xq")
    xk = mm1(mn, W["w_xk"], "nn", BF16, "xk")
    xv = mm1(mn, W["w_xv"], "nn", BF16, "xv")
    xo = xattn_fwd(xq, xk, xv, nseq)
    ho, x3 = mm_resid(xo, W["w_xo"], x2, p["xa_post_g"], 1.0, "xo")

    x4, ffn2, _ = _ffn_fwd(x3, p["ffn2_pre_g"], p["ffn2_post_g"], W["ffn2_w_gate"], W["ffn2_w_up"], W["ffn2_w_down"], "ffn2")
    dx4, loss_row = loss_head(x4, tgt.reshape(T, D), "loss")

    gw, gs = {}, {}
    dx3, gs["ffn2_pre_g"], gs["ffn2_post_g"] = _ffn_bwd(
        dx4, ffn2, p["ffn2_pre_g"], p["ffn2_post_g"], W["ffn2_w_gate"], W["ffn2_w_up"], W["ffn2_w_down"], "ffn2", stage, gw)

    dho, gs["xa_post_g"] = resid_bwd(ho, p["xa_post_g"], dx3, 1.0, "xa_post_bwd")
    dxo = mm1(dho, W["w_xo"], "nt", BF16, "xo_bwd")
    gw["w_xo"] = _tn(xo, dho, "d_w_xo")
    dxq, dxk, dxv = xattn_bwd(xq, xk, xv, dxo, nseq)
    dx2, gs["xa_pre_g"] = mm_rms_bwd([(dxq, W["w_xq"], "nt")], x2, p["xa_pre_g"], "xq_bwd", resid=dx3)
    gw["w_xq"] = _tn(hq, dxq, "d_w_xq")
    dmn = mm([[(dxk, W["w_xk"], "nt"), (dxv, W["w_xv"], "nt")]], [F32], "xkv_bwd")[0]
    gw["w_xk"] = _tn(mn, dxk, "d_w_xk")
    gw["w_xv"] = _tn(mn, dxv, "d_w_xv")
    _, gs["mem_norm_g"] = rms_bwd(mem2, p["mem_norm_g"], dmn, "mem_norm_bwd", dx_dtype=BF16)

    dhmix, gs["mix_post_g"] = resid_bwd(hmix, p["mix_post_g"], dx2, 1.0, "mix_post_bwd")
    dmerged = mm1(dhmix, W["w_out"], "nt", F32, "mix_out_bwd")
    gw["w_out"] = _tn(merged, dhmix, "d_w_out")
    dys, dym, dgl, gs["gate_bias"] = merge_bwd(gl, y_ssd, y_mla, dmerged, p["gate_bias"], "merge_bwd")

    unslot = lambda g, per: g.reshape(MLA_H, SLOT, -1)[:, :per].reshape(MLA_H * per, -1)
    do_s = mm1(dym, wo_s, "nt", BF16, "mla_proj_bwd")
    gw["w_mla_proj"] = unslot(_tn(o_s, dym, "d_w_mla_proj"), VD)
    dQc, dKc, dv_s, *sent = attn_slot_bwd(Qc, Kc, v_s, o_s, lse, do_s, nseq, comm=stage.scatter("attn_bwd", gw))
    stage.scattered("attn_bwd", sent)
    dq_s, dkn_s, dkr = rope_slot_bwd(dQc, dKc, cos16, sin16, "rope_bwd")
    dq_c, gs["q_norm_g"] = mm_rms_bwd([(dq_s, wq_s, "nn")], q_c, p["q_norm_g"], "uq_bwd", dx_dtype=BF16)
    gw["w_uq"] = unslot(_tn(dq_s, qn, "d_w_uq"), QK)
    dkv_c, gs["kv_norm_g"] = mm_rms_bwd([(dkn_s, wk_s, "nn"), (dv_s, wv_s, "nn")], kv_c, p["kv_norm_g"], "ukv_bwd", dx_dtype=BF16)
    gw["w_uk"] = unslot(_tn(dkn_s, kvn, "d_w_uk"), NOPE)
    gw["w_uv"] = unslot(_tn(dv_s, kvn, "d_w_uv"), VD)

    dyn = mm1(dys, W["w_ssd_proj"], "nt", F32, "ssd_proj_bwd")
    gw["w_ssd_proj"] = _tn(yn, dys, "d_w_ssd_proj")
    dyc, dz, gs["ssd_norm_g"] = gated_norm_bwd(y_ssd_core, z, dyn, p["ssd_norm_g"], "ssd_norm_bwd")
    dxbc_act, ddtr, gs["dt_bias"], gs["a_log"], gs["d_skip"], *sent = ssd_bwd(
        xbc_act, dtkr, p["dt_bias"], p["a_log"], p["d_skip"], prev, dyc, nseq, comm=stage.scatter("ssd_bwd", gw))
    stage.scattered("ssd_bwd", sent)
    dxbc, gs["conv_w"], gs["conv_b"] = conv_bwd(xbc, p["conv_w"], p["conv_b"], dxbc_act, nseq)

    gw["w_in"] = jnp.concatenate([_tn(dz, hm, "d_w_in_z"), _tn(dxbc, hm, "d_w_in_xbc"), _tn(ddtr, hm, "d_w_in_dt")[:SSD_H],
                                  _tn(dq_c, hm, "d_w_in_q"), _tn(dkv_c, hm, "d_w_in_kv"), _tn(dkr, hm, "d_w_in_kr")[:ROPE],
                                  _tn(dgl, hm, "d_w_in_gate")], axis=0)
    dx1, gs["mix_pre_g"], *sent = mm_rms_bwd(
        [(dz, wt_z, "nn"), (dxbc, wt_xbc, "nn"), (ddtr, wt_dt, "nn"), (dq_c, wt_q, "nn"), (dkv_c, wt_kv, "nn"),
         (dkr, wt_kr, "nn"), (dgl, wt_gate, "nn")], x1, p["mix_pre_g"], "in_bwd", resid=dx2, comm=stage.scatter("in_bwd", gw))
    stage.scattered("in_bwd", sent)

    dx0, gs["ffn1_pre_g"], gs["ffn1_post_g"] = _ffn_bwd(
        dx1, ffn1, p["ffn1_pre_g"], p["ffn1_post_g"], W["ffn1_w_gate"], W["ffn1_w_up"], W["ffn1_w_down"], "ffn1", stage, gw)
    return loss_row, dx0.reshape(x.shape), gw, gs


def kernel(x, mem, positions, ffn1_pre_g, ffn1_w_gate, ffn1_w_up, ffn1_w_down, ffn1_post_g, mix_pre_g, w_in, conv_w, conv_b, dt_bias, a_log, d_skip, ssd_norm_g, w_ssd_proj, q_norm_g, w_uq, kv_norm_g, w_uk, w_uv, w_mla_proj, gate_bias, w_out, mix_post_g, xa_pre_g, mem_norm_g, w_xq, w_xk, w_xv, w_xo, xa_post_g, ffn2_pre_g, ffn2_w_gate, ffn2_w_up, ffn2_w_down, ffn2_post_g, loss_target, m_ffn1_pre_g, m_ffn1_w_gate, m_ffn1_w_up, m_ffn1_w_down, m_ffn1_post_g, m_mix_pre_g, m_w_in, m_conv_w, m_conv_b, m_dt_bias, m_a_log, m_d_skip, m_ssd_norm_g, m_w_ssd_proj, m_q_norm_g, m_w_uq, m_kv_norm_g, m_w_uk, m_w_uv, m_w_mla_proj, m_gate_bias, m_w_out, m_mix_post_g, m_xa_pre_g, m_mem_norm_g, m_w_xq, m_w_xk, m_w_xv, m_w_xo, m_xa_post_g, m_ffn2_pre_g, m_ffn2_w_gate, m_ffn2_w_up, m_ffn2_w_down, m_ffn2_post_g, v_ffn1_pre_g, v_ffn1_w_gate, v_ffn1_w_up, v_ffn1_w_down, v_ffn1_post_g, v_mix_pre_g, v_w_in, v_conv_w, v_conv_b, v_dt_bias, v_a_log, v_d_skip, v_ssd_norm_g, v_w_ssd_proj, v_q_norm_g, v_w_uq, v_kv_norm_g, v_w_uk, v_w_uv, v_w_mla_proj, v_gate_bias, v_w_out, v_mix_post_g, v_xa_pre_g, v_mem_norm_g, v_w_xq, v_w_xk, v_w_xv, v_w_xo, v_xa_post_g, v_ffn2_pre_g, v_ffn2_w_gate, v_ffn2_w_up, v_ffn2_w_down, v_ffn2_post_g):
    a = dict(locals())
    w = {n: a[n] for n in WEIGHTS}
    m = {n: a["m_" + n] for n in WEIGHTS}
    v = {n: a["v_" + n] for n in WEIGHTS}

    stage = Stage(w)
    W, p = {}, {n: w[n] for n in SMALL}
    stage.gathered("first", run_comm(stage.gather("first"), "allgather_first"), W, p)

    loss_row, grad_x, gw, gs = _local_step(x, mem, positions, loss_target, W, p, stage)

    sm = _pack_small([gs[n] for n in SMALL], loss_row=loss_row, conv_w=gs["conv_w"])
    srecv, = run_comm(ScatterComm([[jnp.broadcast_to(sm[None], (N_DEV,) + sm.shape)]]), "exchange_small")
    s_rows = sum_slots(srecv, "sum_small", tr=sm.shape[0])
    summed = {}
    for tag, (names_main, names_flat) in SCATTER_PLAN.items():
        bufs = list(stage.recv[tag])
        for names, whole in ((names_main, False), (names_flat, True)):
            if not names:
                continue
            buf = bufs.pop(0)
            rows = buf.shape[1]
            g_rows = sum_slots(buf, "sum_" + tag.replace(":", "_") + ("_flat" if whole else ""),
                               tr=rows if whole else _tile(rows, (256, 224, 176, 128, 64, 32, 16)))
            r0 = 0
            for n in names:
                nr = PARTS[n][2] - PARTS[n][1] if n in PARTS else stage.nrows[n]
                summed[n] = g_rows[r0:r0 + nr]
                r0 += nr
    for base in {b for b, _, _ in PARTS.values()}:
        summed[base] = jnp.concatenate([summed[pn] for pn in sorted(PARTS) if PARTS[pn][0] == base], axis=0)
    grads = {}
    for n, kind in BIG:
        blk = summed[n]
        grads[n] = (blk.reshape(w[n].shape[2], w[n].shape[1]).T if kind == "col" else blk)[None]
    conv_w_full = p["conv_w"]
    small_g, r1 = _unpack_small(s_rows, [w[n].shape for n in SMALL])
    for n, g in zip(SMALL, small_g):
        grads[n] = g
    ncw = math.prod(conv_w_full.shape) // 128
    cw_grad_full = s_rows[r1:r1 + ncw].reshape(conv_w_full.shape)
    wsh = conv_w.shape[2]
    grads["conv_w"] = lax.dynamic_slice_in_dim(cw_grad_full, _dev_index() * wsh, wsh, axis=1)[None]
    loss = s_rows[r1 + ncw, 0]

    delta, new_m, new_v = {}, {}, {}
    for n, _ in BIG + [("conv_w", "col")]:
        shp = w[n].shape
        d_, m_, v_ = adamw(w[n][0], grads[n][0], m[n][0], v[n][0], "adamw_" + n)
        delta[n], new_m[n], new_v[n] = d_.reshape(shp), m_.reshape(shp), v_.reshape(shp)
    sp = [_pack_small([t[n] for n in SMALL]) for t in (w, grads, m, v)]
    outs = adamw(sp[0], sp[1], sp[2], sp[3], "adamw_small")
    for t, buf in zip((delta, new_m, new_v), outs):
        vals, _ = _unpack_small(buf, [w[n].shape for n in SMALL])
        for n, val in zip(SMALL, vals):
            t[n] = val
    return (loss, grad_x, *[grads[n] for n in WEIGHTS], *[delta[n] for n in WEIGHTS],
            *[new_m[n] for n in WEIGHTS], *[new_v[n] for n in WEIGHTS])
```

```python
import functools
import math

import jax
import jax.numpy as jnp
from jax import lax
from jax.experimental import pallas as pl
from jax.experimental.pallas import tpu as pltpu

F32, BF16 = jnp.float32, jnp.bfloat16
HI = lax.Precision.HIGHEST
MESH = pl.DeviceIdType.MESH
N_DEV = 8

D = 1024
DFF = 2816
SSD_H, SSD_P, SSD_G, SSD_N, SSD_L = 16, 64, 2, 128, 128
SSD_INNER = SSD_H * SSD_P
CONV_K, CONV_CH = 4, 1536
MLA_H, QR, KVR, NOPE, ROPE, VD = 16, 384, 256, 64, 32, 64
QK = NOPE + ROPE
ROPE_THETA = 10000.0
XA_H, XA_D = 4, 256
EPS = 1e-6
FFN_RES = 0.5
LR, B1, B2, AEPS, WD, STEP = 0.001, 0.9, 0.999, 1e-08, 0.01, 10

VMEM_LIMIT = 56 * 2**20


def _cp(*sem):
    return pltpu.CompilerParams(dimension_semantics=sem, vmem_limit_bytes=VMEM_LIMIT)


def _sigmoid(x):
    return 1.0 / (1.0 + jnp.exp(-x))


def _softplus(x):
    return jnp.where(x > 20.0, x, jnp.log(1.0 + jnp.exp(jnp.minimum(x, 20.0))))


def _dot(a, b, dims="nn"):
    ca = 0 if dims[0] == "t" else 1
    cb = 1 if dims[1] == "t" else 0
    return lax.dot_general(a.astype(BF16), b.astype(BF16), (((ca,), (cb,)), ((), ())), preferred_element_type=F32)


def _dot_sel(a, b, dims="nn", split="a", terms=3):
    r = (a if split == "a" else b).astype(F32)
    out = None
    for t in range(terms):
        piece = r.astype(BF16)
        if t + 1 < terms:
            r = r - piece.astype(F32)
        d = _dot(piece, b, dims) if split == "a" else _dot(a, piece, dims)
        out = d if out is None else out + d
    return out


def _ssd_common(dtr, dtb, alog):
    L = dtr.shape[0]
    dt = _softplus(dtr + dtb)
    a = -jnp.exp(alog)
    adt = dt * a
    r = lax.broadcasted_iota(jnp.int32, (L, L), 0)
    c = lax.broadcasted_iota(jnp.int32, (L, L), 1)
    lower = r >= c
    tri = lower.astype(F32)
    cs = _dot_sel(tri, adt, "nn", split="b")
    cs_t = _dot_sel(adt, tri, "tt")
    return dt, a, cs, cs_t, lower


def _head_expand():
    hh = lax.broadcasted_iota(jnp.int32, (SSD_H, SSD_INNER), 0)
    jj = lax.broadcasted_iota(jnp.int32, (SSD_H, SSD_INNER), 1)
    return ((jj >= hh * SSD_P) & (jj < hh * SSD_P + SSD_P)).astype(F32)


def _head_reduce():
    hh = lax.broadcasted_iota(jnp.int32, (SSD_INNER, SSD_H), 1)
    jj = lax.broadcasted_iota(jnp.int32, (SSD_INNER, SSD_H), 0)
    return ((jj >= hh * SSD_P) & (jj < hh * SSD_P + SSD_P)).astype(F32)


def ssd_fwd(xbc, dtr, dtb, alog, dsk, nseq):
    T = xbc.shape[0]
    S = T // nseq
    C = S // SSD_L
    L = SSD_L
    NP = SSD_H // 2

    def body(x_ref, b_ref, c_ref, dtr_ref, dtb_ref, alog_ref, dsk_ref, y_ref, prev_ref, st_ref):
        ci = pl.program_id(1)

        @pl.when(ci == 0)
        def _():
            st_ref[...] = jnp.zeros_like(st_ref)

        dt, a, cs, cs_t, lower = _ssd_common(dtr_ref[:, 0:SSD_H], dtb_ref[...], alog_ref[...])
        E = _head_expand()
        X = x_ref[...].astype(F32)
        dt_e = _dot_sel(dt, E)
        cs_e = _dot_sel(cs, E)
        csl_e = cs_e[L - 1:L, :]
        Xd = X * dt_e
        Xf = Xd * jnp.exp(csl_e - cs_e)
        e_e = jnp.exp(cs_e)
        skip = _dot_sel(dsk_ref[...], E) * X
        lane = lax.broadcasted_iota(jnp.int32, (1, 2 * SSD_P), 1)
        rowp = lax.broadcasted_iota(jnp.int32, (2 * SSD_P, 1), 0)
        for g in range(SSD_G):
            Bg = b_ref[:, g * SSD_N:(g + 1) * SSD_N]
            Cg = c_ref[:, g * SSD_N:(g + 1) * SSD_N]
            cb = _dot(Cg, Bg, "nt")
            for pp in range(NP // SSD_G):
                p = g * (NP // SSD_G) + pp
                sl = slice(p * 2 * SSD_P, (p + 1) * 2 * SSD_P)
                Xd_p = Xd[:, sl]
                yd = jnp.zeros((L, 2 * SSD_P), F32)
                for q in range(2):
                    h = 2 * p + q
                    m = jnp.where(lower, jnp.exp(jnp.minimum(cs[:, h:h + 1] - cs_t[h:h + 1, :], 0.0)), 0.0)
                    mask = (lane >= q * SSD_P) & (lane < (q + 1) * SSD_P)
                    yd = yd + _dot(cb * m, jnp.where(mask, Xd_p, 0.0))
                S0 = st_ref[p]
                prev_ref[0, 0, p] = S0
                z = _dot(Cg, S0, "nt")
                y_ref[:, sl] = (skip[:, sl] + yd + z * e_e[:, sl]).astype(y_ref.dtype)
                h0 = 2 * p
                dec = jnp.where(rowp < SSD_P, jnp.exp(cs[L - 1:L, h0:h0 + 1]), jnp.exp(cs[L - 1:L, h0 + 1:h0 + 2]))
                st_ref[p] = S0 * dec + _dot(Xf[:, sl], Bg, "tn")

    row = lambda b, c: (b * C + c, 0)
    return pl.pallas_call(
        body, grid=(nseq, C), name="ssd_fwd",
        in_specs=[pl.BlockSpec((L, SSD_INNER), row),
                  pl.BlockSpec((L, SSD_G * SSD_N), lambda b, c: (b * C + c, SSD_INNER // (SSD_G * SSD_N))),
                  pl.BlockSpec((L, SSD_G * SSD_N), lambda b, c: (b * C + c, SSD_INNER // (SSD_G * SSD_N) + 1)),
                  pl.BlockSpec((L, 128), row),
                  pl.BlockSpec((1, SSD_H), lambda b, c: (0, 0)),
                  pl.BlockSpec((1, SSD_H), lambda b, c: (0, 0)),
                  pl.BlockSpec((1, SSD_H), lambda b, c: (0, 0))],
        out_specs=[pl.BlockSpec((L, SSD_INNER), row),
                   pl.BlockSpec((1, 1, NP, 2 * SSD_P, SSD_N), lambda b, c: (b, c, 0, 0, 0))],
        out_shape=[jax.ShapeDtypeStruct((T, SSD_INNER), BF16),
                   jax.ShapeDtypeStruct((nseq, C, NP, 2 * SSD_P, SSD_N), F32)],
        scratch_shapes=[pltpu.VMEM((NP, 2 * SSD_P, SSD_N), F32)],
        compiler_params=_cp("parallel", "arbitrary"),
    )(xbc, xbc, xbc, dtr, dtb, alog, dsk)


def ssd_bwd(xbc, dtr, dtb, alog, dsk, prev, dy, nseq, comm=None):
    T = xbc.shape[0]
    S = T // nseq
    C = S // SSD_L
    L = SSD_L
    NP = SSD_H // 2

    def body(x_ref, b_ref, c_ref, dtr_ref, dtb_ref, alog_ref, dsk_ref, prev_ref, dy_ref,
             dxbc_ref, ddtr_ref, ddtb_ref, dalog_ref, ddsk_ref, ds_ref, stg_ref):
        bi = pl.program_id(0)
        ci = pl.program_id(1)

        @pl.when(ci == 0)
        def _():
            ds_ref[...] = jnp.zeros_like(ds_ref)

        @pl.when((ci == 0) & (bi == 0))
        def _():
            ddtb_ref[...] = jnp.zeros_like(ddtb_ref)
            dalog_ref[...] = jnp.zeros_like(dalog_ref)
            ddsk_ref[...] = jnp.zeros_like(ddsk_ref)

        dtr = dtr_ref[:, 0:SSD_H]
        dtb = dtb_ref[...]
        dt, a, cs, cs_t, lower = _ssd_common(dtr, dtb, alog_ref[...])
        upper = lax.broadcasted_iota(jnp.int32, (L, L), 1) >= lax.broadcasted_iota(jnp.int32, (L, L), 0)
        E = _head_expand()
        ET = _head_reduce()
        X = x_ref[...].astype(F32)
        dY = dy_ref[...].astype(F32)
        dt_e = _dot_sel(dt, E)
        cs_e = _dot_sel(cs, E)
        csl_e = cs_e[L - 1:L, :]
        f_e = jnp.exp(csl_e - cs_e)
        e_e = jnp.exp(cs_e)
        dsk_e = _dot_sel(dsk_ref[...], E)
        Xd = X * dt_e
        Xf = Xd * f_e
        lane = lax.broadcasted_iota(jnp.int32, (1, 2 * SSD_P), 1)
        rowp = lax.broadcasted_iota(jnp.int32, (2 * SSD_P, 1), 0)
        hsel = lax.broadcasted_iota(jnp.int32, (1, SSD_H), 1)
        dcs = jnp.zeros((L, SSD_H), F32)
        dcsl = jnp.zeros((1, SSD_H), F32)
        for g in range(SSD_G):
            Bg = b_ref[:, g * SSD_N:(g + 1) * SSD_N]
            Cg = c_ref[:, g * SSD_N:(g + 1) * SSD_N]
            cb = _dot(Cg, Bg, "nt")
            cbt = _dot(Bg, Cg, "nt")
            dB = jnp.zeros((L, SSD_N), F32)
            dC = jnp.zeros((L, SSD_N), F32)
            for pp in range(NP // SSD_G):
                p = g * (NP // SSD_G) + pp
                sl = slice(p * 2 * SSD_P, (p + 1) * 2 * SSD_P)
                Xd_p = Xd[:, sl]
                dY_p = dY[:, sl]
                dXd_p = jnp.zeros((L, 2 * SSD_P), F32)
                for q in range(2):
                    h = 2 * p + q
                    mask = (lane >= q * SSD_P) & (lane < (q + 1) * SSD_P)
                    col = cs[:, h:h + 1]
                    rw = cs_t[h:h + 1, :]
                    m = jnp.where(lower, jnp.exp(jnp.minimum(col - rw, 0.0)), 0.0)
                    mt = jnp.where(upper, jnp.exp(jnp.minimum(rw - col, 0.0)), 0.0)
                    dYm = jnp.where(mask, dY_p, 0.0)
                    dW = _dot(dYm, Xd_p, "nt")
                    dWt = _dot(Xd_p, dYm, "nt")
                    w = cb * m
                    wt = cbt * mt
                    dC = dC + _dot(dW * m, Bg)
                    dB = dB + _dot(dWt * mt, Cg)
                    dXd_p = dXd_p + jnp.where(mask, _dot(wt, dY_p), 0.0)
                    qcol = jnp.sum(dW * w, axis=1, keepdims=True) - jnp.sum(dWt * wt, axis=1, keepdims=True)
                    dcs = dcs + qcol * (hsel == h).astype(F32)
                S0 = prev_ref[0, 0, p]
                dSn = ds_ref[p]
                dZ = dY_p * e_e[:, sl]
                dC = dC + _dot(dZ, S0)
                h0 = 2 * p
                el0 = jnp.exp(cs[L - 1:L, h0:h0 + 1])
                el1 = jnp.exp(cs[L - 1:L, h0 + 1:h0 + 2])
                dec = jnp.where(rowp < SSD_P, el0, el1)
                ds_ref[p] = dSn * dec + _dot(dZ, Cg, "tn")
                dXf_p = _dot(Bg, dSn, "nt")
                dB = dB + _dot(Xf[:, sl], dSn)
                rs = jnp.sum(dSn * S0, axis=1, keepdims=True)
                s0 = jnp.sum(jnp.where(rowp < SSD_P, rs, 0.0), axis=0, keepdims=True) * el0
                s1 = jnp.sum(jnp.where(rowp >= SSD_P, rs, 0.0), axis=0, keepdims=True) * el1
                dcsl = dcsl + s0 * (hsel == h0).astype(F32) + s1 * (hsel == h0 + 1).astype(F32)
                y_off = _dot(Cg, S0, "nt") * e_e[:, sl]
                t1 = dY_p * y_off - dXf_p * Xf[:, sl]
                r1 = jnp.where(lane < SSD_P, t1, 0.0)
                c0 = jnp.sum(r1, axis=1, keepdims=True)
                c1 = jnp.sum(t1 - r1, axis=1, keepdims=True)
                dcs = dcs + c0 * (hsel == h0).astype(F32) + c1 * (hsel == h0 + 1).astype(F32)
                t2 = dXf_p * Xf[:, sl]
                r2 = jnp.where(lane < SSD_P, t2, 0.0)
                dcsl = dcsl + jnp.sum(r2, keepdims=True) * (hsel == h0).astype(F32) \
                    + jnp.sum(t2 - r2, keepdims=True) * (hsel == h0 + 1).astype(F32)
                stg_ref[:, sl] = dXd_p + dXf_p * f_e[:, sl]
            dxbc_ref[:, SSD_INNER + g * SSD_N:SSD_INNER + (g + 1) * SSD_N] = dB.astype(dxbc_ref.dtype)
            dxbc_ref[:, SSD_INNER + (SSD_G + g) * SSD_N:SSD_INNER + (SSD_G + g + 1) * SSD_N] = dC.astype(dxbc_ref.dtype)
        dXd = stg_ref[...]
        dxbc_ref[:, 0:SSD_INNER] = (dXd * dt_e + dsk_e * dY).astype(dxbc_ref.dtype)
        rowl = lax.broadcasted_iota(jnp.int32, (L, 1), 0)
        dcs = dcs + jnp.where(rowl == L - 1, dcsl, 0.0)
        dalpha = _dot_sel(upper.astype(F32), dcs, split="b")
        ddt = _dot_sel(dXd * X, ET, terms=2) + dalpha * a
        dalog_ref[...] += jnp.sum(dalpha * dt, axis=0, keepdims=True) * a
        ddtr = ddt * _sigmoid(dtr + dtb)
        spread = (lax.broadcasted_iota(jnp.int32, (SSD_H, 128), 0) == lax.broadcasted_iota(jnp.int32, (SSD_H, 128), 1)).astype(F32)
        ddtr_ref[...] = _dot(ddtr, spread).astype(ddtr_ref.dtype)
        ddtb_ref[...] += jnp.sum(ddtr, axis=0, keepdims=True)
        ddsk_ref[...] += jnp.sum(_dot_sel(dY * X, ET, terms=2), axis=0, keepdims=True)

    rowr = lambda b, c: (b * C + (C - 1 - c), 0)
    small = pl.BlockSpec((1, SSD_H), lambda b, c: (0, 0))
    return _call_with_comm(
        body, (nseq, C), "ssd_bwd",
        [pl.BlockSpec((L, SSD_INNER), rowr),
         pl.BlockSpec((L, SSD_G * SSD_N), lambda b, c: (b * C + (C - 1 - c), SSD_INNER // (SSD_G * SSD_N))),
         pl.BlockSpec((L, SSD_G * SSD_N), lambda b, c: (b * C + (C - 1 - c), SSD_INNER // (SSD_G * SSD_N) + 1)),
         pl.BlockSpec((L, 128), rowr), small, small, small,
         pl.BlockSpec((1, 1, NP, 2 * SSD_P, SSD_N), lambda b, c: (b, C - 1 - c, 0, 0, 0)),
         pl.BlockSpec((L, SSD_INNER), rowr)],
        [xbc, xbc, xbc, dtr, dtb, alog, dsk, prev, dy],
        [pl.BlockSpec((L, CONV_CH), rowr), pl.BlockSpec((L, 128), rowr), small, small, small],
        [jax.ShapeDtypeStruct((T, CONV_CH), BF16), jax.ShapeDtypeStruct((T, 128), BF16),
         jax.ShapeDtypeStruct((1, SSD_H), F32), jax.ShapeDtypeStruct((1, SSD_H), F32), jax.ShapeDtypeStruct((1, SSD_H), F32)],
        comm, scratch=[pltpu.VMEM((NP, 2 * SSD_P, SSD_N), F32), pltpu.VMEM((L, SSD_INNER), F32)], sem=("arbitrary", "arbitrary"))


SLOT = 128
ATT_T = 512
LOG2E = math.log2(math.e)
Q_SCALE = QK ** -0.5 * LOG2E


def _col_to_row(col):
    n = col.shape[0]
    eye = lax.broadcasted_iota(jnp.int32, (n, n), 0) == lax.broadcasted_iota(jnp.int32, (n, n), 1)
    return jnp.sum(jnp.where(eye, col, 0.0), axis=0, keepdims=True)


def attn_slot_fwd(q, k, v, nseq, comm=None):
    T = q.shape[0]
    S = T // nseq
    t = min(ATT_T, S)
    nb = S // t

    def body(q_ref, k_ref, v_ref, o_ref, lse_ref):
        causal = lax.broadcasted_iota(jnp.int32, (t, t), 1) <= lax.broadcasted_iota(jnp.int32, (t, t), 0)
        for qi in range(nb):
            qb = q_ref[qi * t:(qi + 1) * t, :]
            m = l = acc = None
            for kj in range(qi + 1):
                s = _dot(qb, k_ref[kj * t:(kj + 1) * t, :], "nt")
                if kj == qi:
                    s = jnp.where(causal, s, -1e30)
                bm = jnp.max(s, axis=1, keepdims=True)
                if kj == 0:
                    m = bm
                    p = jnp.exp2(s - m)
                    l = jnp.sum(p, axis=1, keepdims=True)
                    acc = _dot(p, v_ref[0:t, :])
                else:
                    m_new = jnp.maximum(m, bm)
                    corr = jnp.exp2(m - m_new)
                    p = jnp.exp2(s - m_new)
                    l = l * corr + jnp.sum(p, axis=1, keepdims=True)
                    acc = acc * corr + _dot(p, v_ref[kj * t:(kj + 1) * t, :])
                    m = m_new
            o_ref[qi * t:(qi + 1) * t, :] = (acc / l).astype(o_ref.dtype)
            lse_ref[0, 0, :, qi * t:(qi + 1) * t] = _col_to_row(m + jnp.log2(l))

    blk = pl.BlockSpec((S, SLOT), lambda b, h: (b, h))
    return _call_with_comm(
        body, (nseq, MLA_H), "attn_fwd", [blk, blk, blk], [q, k, v],
        [blk, pl.BlockSpec((1, 1, 1, S), lambda b, h: (b, h, 0, 0))],
        [jax.ShapeDtypeStruct((T, MLA_H * SLOT), BF16), jax.ShapeDtypeStruct((nseq, MLA_H, 1, S), F32)], comm)


def attn_slot_bwd(q, k, v, o, lse, do, nseq, comm=None):
    T = q.shape[0]
    S = T // nseq
    t = min(ATT_T, S)
    nb = S // t
    scale = QK ** -0.5

    def body(q_ref, k_ref, v_ref, o_ref, lse_ref, do_ref, dq_ref, dk_ref, dv_ref, dqa_ref):
        causal_t =lax.broadcasted_iota(jnp.int32, (t, t), 0) <= lax.broadcasted_iota(jnp.int32, (t, t), 1)
        ones = jnp.ones((8, SLOT), F32)
        delta = []
        for qi in range(nb):
            sl = slice(qi * t, (qi + 1) * t)
            prod = do_ref[sl, :].astype(F32) * o_ref[sl, :].astype(F32)
            delta.append(_dot_sel(ones, prod, "nt", split="b", terms=2)[0:1, :])
        for kj in range(nb):
            ks = slice(kj * t, (kj + 1) * t)
            kb = k_ref[ks, :]
            vb = v_ref[ks, :]
            dk = dv = None
            for qi in range(kj, nb):
                sl = slice(qi * t, (qi + 1) * t)
                qb = q_ref[sl, :]
                dob = do_ref[sl, :]
                st = _dot(kb, qb, "nt")
                pt = jnp.exp2(st - lse_ref[0, 0, :, sl])
                if qi == kj:
                    pt = jnp.where(causal_t, pt, 0.0)
                dpt = _dot(vb, dob, "nt")
                dst = (pt * (dpt - delta[qi])).astype(BF16)
                dvc = _dot(pt, dob)
                dkc = _dot(dst, qb) * (1.0 / LOG2E)
                dv = dvc if dv is None else dv + dvc
                dk = dkc if dk is None else dk + dkc
                dqc = _dot(dst, kb, "tn") * scale
                if kj > 0:
                    dqc = dqc + dqa_ref[sl, :]
                if qi == kj:
                    dq_ref[sl, :] = dqc.astype(dq_ref.dtype)
                else:
                    dqa_ref[sl, :] = dqc
            dk_ref[ks, :] = dk.astype(dk_ref.dtype)
            dv_ref[ks, :] = dv.astype(dv_ref.dtype)

    blk = pl.BlockSpec((S, SLOT), lambda b, h: (b, h))
    lse_spec = pl.BlockSpec((1, 1, 1, S), lambda b, h: (b, h, 0, 0))
    W = MLA_H * SLOT
    return _call_with_comm(
        body, (nseq, MLA_H), "attn_bwd", [blk, blk, blk, blk, lse_spec, blk], [q, k, v, o, lse, do], [blk, blk, blk],
        [jax.ShapeDtypeStruct((T, W), BF16)] * 3, comm, scratch=[pltpu.VMEM((S, SLOT), F32)])


def _rope_coeffs(cos, sin):
    half = ROPE // 2
    r = lax.broadcasted_iota(jnp.int32, (half, SLOT), 0)
    c = lax.broadcasted_iota(jnp.int32, (half, SLOT), 1)
    pc = ((c == r + NOPE) | (c == r + NOPE + half)).astype(F32)
    ps = (c == r + NOPE + half).astype(F32) - (c == r + NOPE).astype(F32)
    lane = lax.broadcasted_iota(jnp.int32, (1, SLOT), 1)
    return _dot_sel(cos, pc) + (lane < NOPE).astype(F32), _dot_sel(sin, ps)


def _rope_swap(x):
    W = x.shape[1]
    half = ROPE // 2
    lane = lax.broadcasted_iota(jnp.int32, (1, W), 1) & (SLOT - 1)
    up = pltpu.roll(x, W - half, axis=1)
    dn = pltpu.roll(x, half, axis=1)
    return jnp.where((lane >= NOPE) & (lane < NOPE + half), up, jnp.where((lane >= NOPE + half) & (lane < QK), dn, 0.0))


def rope_slot_fwd(q, kn, dtkr, cos, sin, name):
    def fn(qv, knv, krv, cv, sv):
        C, Sg = _rope_coeffs(cv, sv)
        ct, stl = jnp.tile(C, (1, MLA_H)), jnp.tile(Sg, (1, MLA_H))
        qo = (qv * ct + _rope_swap(qv) * stl) * Q_SCALE
        r = lax.broadcasted_iota(jnp.int32, (SLOT, SLOT), 0)
        c = lax.broadcasted_iota(jnp.int32, (SLOT, SLOT), 1)
        place = ((c == r + NOPE) & (r < ROPE)).astype(F32)
        kr = _dot_sel(krv, place)
        kr = kr * C + _rope_swap(kr) * Sg
        return qo, knv.astype(F32) + jnp.tile(kr, (1, MLA_H))
    W = MLA_H * SLOT
    return rowwise(fn, [q, kn, (dtkr, SLOT, 1), cos, sin], [], [(W, BF16), (W, BF16)], [], name)


def rope_slot_bwd(dq, dk, cos, sin, name):
    def fn(dqv, dkv, cv, sv):
        C, Sg = _rope_coeffs(cv, sv)
        ct, stl = jnp.tile(C, (1, MLA_H)), jnp.tile(Sg, (1, MLA_H))
        dqo = dqv * ct - _rope_swap(dqv) * stl
        tot = dkv[:, 0:SLOT]
        for h in range(1, MLA_H):
            tot = tot + dkv[:, h * SLOT:(h + 1) * SLOT]
        u = tot * C - _rope_swap(tot) * Sg
        r = lax.broadcasted_iota(jnp.int32, (SLOT, SLOT), 0)
        c = lax.broadcasted_iota(jnp.int32, (SLOT, SLOT), 1)
        unplace = ((r == c + NOPE) & (c < ROPE)).astype(F32)
        return dqo, dkv, _dot_sel(u, unplace, terms=2)
    W = MLA_H * SLOT
    return rowwise(fn, [dq, dk, cos, sin], [], [(W, BF16), (W, BF16), (SLOT, BF16)], [], name)


XA_BLK = 512


def xattn_fwd(q, k, v, nseq):
    T = q.shape[0]
    S = T // nseq
    M = k.shape[0] // nseq
    tq = min(XA_BLK, S)
    nq = S // tq
    scale = XA_D ** -0.5

    def body(q_ref, k_ref, v_ref, o_ref):
        s = _dot(q_ref[...], k_ref[...], "nt") * scale
        p = jnp.exp(s - jnp.max(s, axis=1, keepdims=True))
        p = p / jnp.sum(p, axis=1, keepdims=True)
        o_ref[...] = _dot(p, v_ref[...]).astype(o_ref.dtype)

    qs = pl.BlockSpec((tq, XA_D), lambda b, h, i: (b * nq + i, h))
    ks = pl.BlockSpec((M, XA_D), lambda b, h, i: (b, h))
    return pl.pallas_call(
        body, grid=(nseq, XA_H, nq), name="xattn_fwd", in_specs=[qs, ks, ks], out_specs=qs,
        out_shape=jax.ShapeDtypeStruct((T, XA_H * XA_D), BF16),
        compiler_params=_cp("parallel", "parallel", "parallel"),
    )(q, k, v)


def xattn_bwd(q, k, v, do, nseq):
    T = q.shape[0]
    S = T // nseq
    M = k.shape[0] // nseq
    tq = min(XA_BLK, S)
    nq = S // tq
    scale = XA_D ** -0.5

    def body(q_ref, k_ref, v_ref, do_ref, dq_ref, dk_ref, dv_ref):
        @pl.when(pl.program_id(2) == 0)
        def _():
            dk_ref[...] = jnp.zeros_like(dk_ref)
            dv_ref[...] = jnp.zeros_like(dv_ref)

        qb, kb, vb, dob = q_ref[...], k_ref[...], v_ref[...], do_ref[...]
        s = _dot(qb, kb, "nt") * scale
        p = jnp.exp(s - jnp.max(s, axis=1, keepdims=True))
        p = p / jnp.sum(p, axis=1, keepdims=True)
        dp = _dot(dob, vb, "nt")
        ds = p * (dp - jnp.sum(dp * p, axis=1, keepdims=True)) * scale
        dq_ref[...] = _dot(ds, kb).astype(dq_ref.dtype)
        dk_ref[...] += _dot(ds, qb, "tn")
        dv_ref[...] += _dot(p, dob, "tn")

    qs = pl.BlockSpec((tq, XA_D), lambda b, h, i: (b * nq + i, h))
    ks = pl.BlockSpec((M, XA_D), lambda b, h, i: (b, h))
    return pl.pallas_call(
        body, grid=(nseq, XA_H, nq), name="xattn_bwd", in_specs=[qs, ks, ks, qs], out_specs=[qs, ks, ks],
        out_shape=[jax.ShapeDtypeStruct((T, XA_H * XA_D), BF16), jax.ShapeDtypeStruct(k.shape, F32),
                   jax.ShapeDtypeStruct(k.shape, F32)],
        compiler_params=_cp("parallel", "parallel", "arbitrary"),
    )(q, k, v, do)


CONV_BLK = 256


def _shift_down(x, s, rows):
    if s == 0:
        return x
    return jnp.where(rows >= s, pltpu.roll(x, s, axis=0), 0.0)


def _shift_up(x, s, rows):
    if s == 0:
        return x
    S = x.shape[0]
    return jnp.where(rows < S - s, pltpu.roll(x, S - s, axis=0), 0.0)


def conv_fwd(x, w, b, nseq):
    T, CH = x.shape
    S = T // nseq

    def body(x_ref, w_ref, b_ref, o_ref):
        xv = x_ref[...].astype(F32)
        rows = lax.broadcasted_iota(jnp.int32, (S, 1), 0)
        c = jnp.zeros_like(xv) + b_ref[...]
        for kk in range(CONV_K):
            c = c + w_ref[kk:kk + 1, :] * _shift_down(xv, CONV_K - 1 - kk, rows)
        o_ref[...] = (c * _sigmoid(c)).astype(o_ref.dtype)

    xs = pl.BlockSpec((S, CONV_BLK), lambda j, bb: (bb, j))
    return pl.pallas_call(
        body, grid=(CH // CONV_BLK, nseq), name="conv_fwd",
        in_specs=[xs, pl.BlockSpec((CONV_K, CONV_BLK), lambda j, bb: (0, j)), pl.BlockSpec((1, CONV_BLK), lambda j, bb: (0, j))],
        out_specs=xs, out_shape=jax.ShapeDtypeStruct((T, CH), BF16),
        compiler_params=_cp("parallel", "parallel"),
    )(x, w, b)


def conv_bwd(x, w, b, dout, nseq):
    T, CH = x.shape
    S = T // nseq

    def body(x_ref, w_ref, b_ref, do_ref, dx_ref, dw_ref, db_ref):
        @pl.when(pl.program_id(1) == 0)
        def _():
            dw_ref[...] = jnp.zeros_like(dw_ref)
            db_ref[...] = jnp.zeros_like(db_ref)

        xv = x_ref[...].astype(F32)
        rows = lax.broadcasted_iota(jnp.int32, (S, 1), 0)
        c = jnp.zeros_like(xv) + b_ref[...]
        sh = [_shift_down(xv, CONV_K - 1 - kk, rows) for kk in range(CONV_K)]
        for kk in range(CONV_K):
            c = c + w_ref[kk:kk + 1, :] * sh[kk]
        sg = _sigmoid(c)
        dc = do_ref[...].astype(F32) * sg * (1.0 + c * (1.0 - sg))
        dx = jnp.zeros_like(xv)
        for kk in range(CONV_K):
            dx = dx + w_ref[kk:kk + 1, :] * _shift_up(dc, CONV_K - 1 - kk, rows)
            dw_ref[kk:kk + 1, :] += jnp.sum(dc * sh[kk], axis=0, keepdims=True)
        dx_ref[...] = dx.astype(dx_ref.dtype)
        db_ref[...] += jnp.sum(dc, axis=0, keepdims=True)

    xs = pl.BlockSpec((S, CONV_BLK), lambda j, bb: (bb, j))
    ws = pl.BlockSpec((CONV_K, CONV_BLK), lambda j, bb: (0, j))
    bs = pl.BlockSpec((1, CONV_BLK), lambda j, bb: (0, j))
    return pl.pallas_call(
        body, grid=(CH // CONV_BLK, nseq), name="conv_bwd",
        in_specs=[xs, ws, bs, xs], out_specs=[xs, ws, bs],
        out_shape=[jax.ShapeDtypeStruct((T, CH), BF16), jax.ShapeDtypeStruct((CONV_K, CH), F32),
                   jax.ShapeDtypeStruct((1, CH), F32)],
        compiler_params=_cp("parallel", "arbitrary"),
    )(x, w, b, dout)


def _dims(a, b, mode):
    M = a.shape[1] if mode[0] == "t" else a.shape[0]
    K = a.shape[0] if mode[0] == "t" else a.shape[1]
    N = b.shape[0] if mode[1] == "t" else b.shape[1]
    return M, K, N


def _tile(dim, prefs):
    for p in prefs:
        if dim % p == 0:
            return p
    return dim


def mm(groups, out_dtypes, name, tm=None, tn=None, tk=None, epi=None, extras=(), comm=None, sub=1, n_sum=0):
    a0, b0, m0 = groups[0][0]
    M, K0, N = _dims(a0, b0, m0)
    tm = tm or _tile(M, (1024, 512, 256, 128))
    tn = tn or _tile(N, (512, 256, 128))
    flat = [p for g in groups for p in g]
    nk = 1 if tk is None else K0 // tk
    in_specs, args = [], []
    for a, b, mode in flat:
        _, K, _ = _dims(a, b, mode)
        kb = K if tk is None else tk
        in_specs.append(pl.BlockSpec((kb, tm), lambda i, j, k: (k, i)) if mode[0] == "t"
                        else pl.BlockSpec((tm, kb), lambda i, j, k: (i, k)))
        in_specs.append(pl.BlockSpec((tn, kb), lambda i, j, k: (j, k)) if mode[1] == "t"
                        else pl.BlockSpec((kb, tn), lambda i, j, k: (k, j)))
        args += [a, b]
    for e in extras:
        in_specs.append(pl.BlockSpec((1, tn), lambda i, j, k: (0, j)) if e.shape[0] == 1 and M != 1
                        else pl.BlockSpec((tm, tn), lambda i, j, k: (i, j)))
        args.append(e)
    n_in = len(args)
    n_main = len(out_dtypes)
    n_out = n_main + n_sum
    assert n_sum == 0 or (tn == N and tk is None)
    ng = len(groups)
    sizes = [len(g) for g in groups]

    def body(*refs):
        ins, outs, accs = refs[:n_in], refs[n_in:n_in + n_out], refs[n_in + n_out:]
        kk = pl.program_id(2)

        def dots(rs):
            vals, pos = [], 0
            for gi in range(ng):
                acc = None
                for _ in range(sizes[gi]):
                    mode = flat[pos // 2][2]
                    av = ins[pos][:, rs] if mode[0] == "t" else ins[pos][rs, :]
                    d = _dot(av, ins[pos + 1][...], mode)
                    acc = d if acc is None else acc + d
                    pos += 2
                vals.append(acc)
            return vals

        def finish(accv, rs, first_chunk=True):
            ex = [(r[...] if r.shape[0] == 1 and tm != 1 else r[rs, :]).astype(F32) for r in ins[2 * len(flat):]]
            res = epi(accv, ex) if epi is not None else tuple(accv)
            for o, r in zip(outs[:n_main], res[:n_main]):
                o[rs, :] = r.astype(o.dtype)
            for o, r in zip(outs[n_main:], res[n_main:]):
                if first_chunk:
                    @pl.when(pl.program_id(0) == 0)
                    def _():
                        o[...] = r

                    @pl.when(pl.program_id(0) > 0)
                    def _():
                        o[...] += r
                else:
                    o[...] += r

        if nk == 1:
            for r in range(sub):
                rs = slice(r * (tm // sub), (r + 1) * (tm // sub))
                finish(dots(rs), rs, r == 0)
        else:
            vals = dots(slice(0, tm))
            finish = functools.partial(finish, rs=slice(0, tm))
            @pl.when(kk == 0)
            def _():
                for ar, vv in zip(accs, vals):
                    ar[...] = vv

            @pl.when(kk > 0)
            def _():
                for ar, vv in zip(accs, vals):
                    ar[...] += vv

            @pl.when(kk == nk - 1)
            def _():
                finish([ar[...] for ar in accs])

    grid = (M // tm, N // tn, nk)
    out_specs = [pl.BlockSpec((tm, tn), lambda i, j, k: (i, j)) for _ in out_dtypes] \
        + [pl.BlockSpec((1, tn), lambda i, j, k: (0, j))] * n_sum
    out_shape = [jax.ShapeDtypeStruct((M, N), dt) for dt in out_dtypes] + [jax.ShapeDtypeStruct((1, N), F32)] * n_sum
    scratch = [pltpu.VMEM((tm, tn), F32) for _ in range(ng if nk > 1 else 0)]
    sem = ("arbitrary" if n_sum else "parallel", "parallel", "arbitrary")
    if comm is not None:
        body = _attach(comm, body, n_in, n_out, *_grid_ends(grid))
        in_specs, args = in_specs + [HBM_SPEC] * len(comm.inputs), args + comm.inputs
        out_specs, out_shape = out_specs + [HBM_SPEC] * len(comm.out_shapes), out_shape + comm.out_shapes
        scratch, sem = scratch + comm.sems, ("arbitrary",) * 3
    return pl.pallas_call(body, grid=grid, name=name, in_specs=in_specs, out_specs=out_specs, out_shape=out_shape,
                          scratch_shapes=scratch, compiler_params=_cp(*sem))(*args)


def mm1(a, b, mode, out_dtype, name, **kw):
    return mm([[(a, b, mode)]], [out_dtype], name, **kw)[0]


ROW_BLK = 512


def rowwise(fn, rows, consts, outs, accs, name, tb=ROW_BLK):
    rows = [r if isinstance(r, tuple) else (r, r.shape[1], 0) for r in rows]
    T = rows[0][0].shape[0]
    tb = min(tb, T)
    n_r, n_c, n_o, n_a = len(rows), len(consts), len(outs), len(accs)

    def body(*refs):
        vals = [r[...].astype(F32) for r in refs[:n_r + n_c]]
        res = fn(*vals)
        o_refs = refs[n_r + n_c:n_r + n_c + n_o]
        a_refs = refs[n_r + n_c + n_o:]
        for o, r in zip(o_refs, res[:n_o]):
            o[...] = r.astype(o.dtype)
        if n_a:
            @pl.when(pl.program_id(0) == 0)
            def _():
                for ar in a_refs:
                    ar[...] = jnp.zeros_like(ar)
            for ar, r in zip(a_refs, res[n_o:]):
                ar[...] += r

    return pl.pallas_call(
        body, grid=(T // tb,), name=name,
        in_specs=[pl.BlockSpec((tb, w), functools.partial(lambda i, j: (i, j), j=j)) for _, w, j in rows]
        + [pl.BlockSpec(c.shape, lambda i: (0, 0)) for c in consts],
        out_specs=[pl.BlockSpec((tb, d), lambda i: (i, 0)) for d, _ in outs]
        + [pl.BlockSpec(s, lambda i: (0, 0)) for s in accs],
        out_shape=[jax.ShapeDtypeStruct((T, d), dt) for d, dt in outs]
        + [jax.ShapeDtypeStruct(s, F32) for s in accs],
        compiler_params=_cp("arbitrary" if n_a else "parallel"),
    )(*[r[0] for r in rows], *consts)


def _rms_stats(x):
    r = lax.rsqrt(jnp.mean(x * x, axis=-1, keepdims=True) + EPS)
    return r, x * r


def _rms_bwd(x, g, dy):
    r, xn = _rms_stats(x)
    dyg = dy * g
    dx = r * (dyg - xn * jnp.mean(dyg * xn, axis=-1, keepdims=True))
    return dx, jnp.sum(dy * xn, axis=0, keepdims=True)


def rms_fwd(x, g, name):
    return rowwise(lambda xv, gv: (_rms_stats(xv)[1] * gv,), [x], [g], [(x.shape[1], BF16)], [], name)[0]


def rms_bwd(x, g, dy, name, resid=None, dx_dtype=F32):
    def fn(*v):
        if resid is None:
            xv, dyv, gv = v
            dx, dg = _rms_bwd(xv, gv, dyv)
        else:
            xv, dyv, rv, gv = v
            dx, dg = _rms_bwd(xv, gv, dyv)
            dx = dx + rv
        return dx, dg
    rows = [x, dy] + ([] if resid is None else [resid])
    return rowwise(fn, rows, [g], [(x.shape[1], dx_dtype)], [(1, x.shape[1])], name)


def mm_rms_bwd(pairs, x, g, name, resid=None, dx_dtype=F32, comm=None):
    def epi(accs, ex):
        dx, dg = _rms_bwd(ex[0], ex[-1], accs[0])
        return (dx if resid is None else dx + ex[1]), dg
    extras = [x] + ([] if resid is None else [resid]) + [g]
    return mm([pairs], [dx_dtype], name, tm=min(256, x.shape[0]), tn=x.shape[1], epi=epi, extras=extras, comm=comm, n_sum=1)


def mm_resid(a, b, x, g, wgt, name):
    epi = lambda accs, ex: (accs[0], ex[0] + wgt * _rms_stats(accs[0])[1] * ex[1])
    return mm([[(a, b, "nn")]], [F32, F32], name, tm=min(512, a.shape[0]), tn=b.shape[1], epi=epi, extras=[x, g], sub=2)


def resid_bwd(h, g, dy, wgt, name):
    def fn(hv, dyv, gv):
        dx, dg = _rms_bwd(hv, gv, dyv)
        return wgt * dx, wgt * dg
    return rowwise(fn, [h, dy], [g], [(h.shape[1], BF16)], [(1, h.shape[1])], name)


def _silu_parts(g):
    s = _sigmoid(g)
    return g * s, s * (1.0 + g * (1.0 - s))


def gated_norm_fwd(y, z, g, name):
    W = SSD_INNER // SSD_G

    def fn(yv, zv, gv):
        yg = yv * _silu_parts(zv)[0]
        return (jnp.concatenate([_rms_stats(yg[:, i * W:(i + 1) * W])[1] for i in range(SSD_G)], axis=1) * gv,)
    return rowwise(fn, [y, z], [g], [(SSD_INNER, BF16)], [], name)[0]


def gated_norm_bwd(y, z, dyn, g, name):
    W = SSD_INNER // SSD_G

    def fn(yv, zv, dv, gv):
        sil, dsil = _silu_parts(zv)
        yg = yv * sil
        parts = [_rms_bwd(yg[:, i * W:(i + 1) * W], gv[:, i * W:(i + 1) * W], dv[:, i * W:(i + 1) * W]) for i in range(SSD_G)]
        dyg = jnp.concatenate([p[0] for p in parts], axis=1)
        dg = jnp.concatenate([p[1] for p in parts], axis=1)
        return dyg * sil, dyg * yv * dsil, dg
    return rowwise(fn, [y, z, dyn], [g], [(SSD_INNER, BF16), (SSD_INNER, BF16)], [(1, SSD_INNER)], name)


def merge_fwd(gl, ys, ym, gb, name):
    def fn(glv, ysv, ymv, gbv):
        gt = _sigmoid(glv + gbv)
        return (gt[:, :D] * ysv + gt[:, D:] * ymv,)
    return rowwise(fn, [gl, ys, ym], [gb], [(D, BF16)], [], name)[0]


def merge_bwd(gl, ys, ym, dm, gb, name):
    def fn(glv, ysv, ymv, dmv, gbv):
        gt = _sigmoid(glv + gbv)
        gs, gm = gt[:, :D], gt[:, D:]
        dgl = jnp.concatenate([dmv * ysv * gs * (1.0 - gs), dmv * ymv * gm * (1.0 - gm)], axis=1)
        return dmv * gs, dmv * gm, dgl, jnp.sum(dgl, axis=0, keepdims=True)
    return rowwise(fn, [gl, ys, ym, dm], [gb], [(D, BF16), (D, BF16), (2 * D, BF16)], [(1, 2 * D)], name)


def loss_head(y, tgt, name):
    def fn(yv, tv):
        d = yv - tv
        part = 0.5 * jnp.sum(jnp.sum(d * d, axis=1, keepdims=True), axis=0, keepdims=True) / D
        return d / D, jnp.broadcast_to(part, (1, 128))
    return rowwise(fn, [y, tgt], [], [(D, F32)], [(1, 128)], name)


def _adamw_math(wv, gv, mv, vv):
    mn = B1 * mv + (1.0 - B1) * gv
    vn = B2 * vv + (1.0 - B2) * (gv * gv)
    mh = mn / (1.0 - B1 ** STEP)
    vh = vn / (1.0 - B2 ** STEP)
    return -LR * (mh / (jnp.sqrt(vh) + AEPS) + WD * wv), mn, vn


def adamw(w, g, m, v, name):
    R, C = w.shape
    tb = _tile(R, (256, 128, 64, 32, 16, 8))
    return rowwise(_adamw_math, [w, g, m, v], [], [(C, F32)] * 3, [], name, tb=tb)


def adamw_from_slots(recv, piece, w, m, v, name, transposed):
    K, n = w.shape
    ns = recv.shape[0]
    if transposed:
        assert recv.shape[2] == K and recv.shape[1] % n == 0
        tk = _tile(K, (256, 128)) if K > 384 else K
        grid = (K // tk,)
        r_spec = pl.BlockSpec((ns, n, tk), lambda i: (0, piece, i))
        w_spec = pl.BlockSpec((tk, n), lambda i: (i, 0))
    else:
        assert recv.shape[2] == n and recv.shape[1] % K == 0
        tb = _tile(K, (256, 176, 128, 64, 32, 16, 8))
        grid = (K // tb,)
        r_spec = pl.BlockSpec((ns, tb, n), lambda i: (0, piece * (K // tb) + i, 0))
        w_spec = pl.BlockSpec((tb, n), lambda i: (i, 0))

    def body(r_ref, w_ref, m_ref, v_ref, g_ref, d_ref, mo_ref, vo_ref):
        g = r_ref[0].astype(F32)
        for s in range(1, ns):
            g = g + r_ref[s].astype(F32)
        if transposed:
            eye = (lax.broadcasted_iota(jnp.int32, (n, n), 0) == lax.broadcasted_iota(jnp.int32, (n, n), 1)).astype(F32)
            g = _dot_sel(g, eye, "tn")
        g_ref[...] = g
        d_ref[...], mo_ref[...], vo_ref[...] = _adamw_math(w_ref[...], g, m_ref[...], v_ref[...])

    return pl.pallas_call(
        body, grid=grid, name=name, in_specs=[r_spec, w_spec, w_spec, w_spec], out_specs=[w_spec] * 4,
        out_shape=[jax.ShapeDtypeStruct((K, n), F32)] * 4, compiler_params=_cp("parallel"),
    )(recv, w, m, v)


def _me():
    return lax.axis_index("x"), lax.axis_index("y"), lax.axis_index("c")


def _dev_index():
    x, y, c = _me()
    return 4 * x + 2 * y + c


HBM_SPEC = pl.BlockSpec(memory_space=pl.ANY)


class GatherComm:
    def __init__(self, shards):
        self.inputs = list(shards)
        n = len(shards)
        self.out_shapes = [jax.ShapeDtypeStruct((N_DEV,) + s.shape, s.dtype) for s in shards]
        self.sems = [pltpu.SemaphoreType.DMA((7 * n,)), pltpu.SemaphoreType.DMA((7 * n,)), pltpu.SemaphoreType.DMA((n,))]

    def _plan(self, x_refs, out_refs, sems):
        send_sems, recv_sems, local_sems = sems
        n = len(x_refs)
        x, y, c = _me()
        me, sibling = (x, y, c), (x, y, 1 - c)
        chips = [(1 - x, y), (x, 1 - y), (1 - x, 1 - y)]

        def slot(i, px, py, pc):
            return out_refs[i].at[4 * px + 2 * py + pc]

        def copy(i, k, block, to, src=None):
            return pltpu.make_async_remote_copy(
                src_ref=slot(i, *block) if src is None else src, dst_ref=slot(i, *block),
                send_sem=send_sems.at[7 * i + k], recv_sem=recv_sems.at[7 * i + k], device_id=to, device_id_type=MESH)

        mine = [pltpu.make_async_copy(x_refs[i], slot(i, *me), local_sems.at[i]) for i in range(n)]
        first = []
        for i in range(n):
            first.append(copy(i, 0, me, sibling, src=x_refs[i]))
            first += [copy(i, 1 + j, me, (*chip, c), src=x_refs[i]) for j, chip in enumerate(chips)]
        passed = [[copy(i, 4 + j, (*chip, c), sibling) for j, chip in enumerate(chips)] for i in range(n)]
        from_ici = [[copy(i, 1 + j, (*chip, c), me) for j, chip in enumerate(chips)] for i in range(n)]
        from_sib = [[copy(i, 0, sibling, me)] + [copy(i, 4 + j, (*chip, 1 - c), me) for j, chip in enumerate(chips)] for i in range(n)]
        return mine, first, passed, from_ici, from_sib

    def start(self, x_refs, out_refs, sems):
        mine, first, _, _, _ = self._plan(x_refs, out_refs, sems)
        for cp in mine + first:
            cp.start()

    def finish(self, x_refs, out_refs, sems):
        mine, first, passed, from_ici, from_sib = self._plan(x_refs, out_refs, sems)
        for i in range(len(x_refs)):
            for arrival, forward in zip(from_ici[i], passed[i]):
                arrival.wait_recv()
                forward.start()
        for row in from_sib:
            for arrival in row:
                arrival.wait_recv()
        for cp in first + [cp for row in passed for cp in row]:
            cp.wait_send()
        for cp in mine:
            cp.wait()


def run_comm(comm, name):
    n_in, n_out = len(comm.inputs), len(comm.out_shapes)

    def body(*refs):
        ins, outs, sems = refs[:n_in], refs[n_in:n_in + n_out], refs[n_in + n_out:]
        comm.start(ins, outs, sems)
        comm.finish(ins, outs, sems)

    return pl.pallas_call(body, name=name, out_shape=comm.out_shapes, in_specs=[HBM_SPEC] * n_in,
                          out_specs=[HBM_SPEC] * n_out, scratch_shapes=comm.sems)(*comm.inputs)


def _attach(comm, body, n_in, n_out, first, last):
    if comm is None:
        return body
    ci, co, cs = len(comm.inputs), len(comm.out_shapes), len(comm.sems)

    def wrapped(*refs):
        h_in, c_in = refs[:n_in], refs[n_in:n_in + ci]
        h_out, c_out = refs[n_in + ci:n_in + ci + n_out], refs[n_in + ci + n_out:n_in + ci + n_out + co]
        rest = refs[n_in + ci + n_out + co:]
        h_scr, c_sem = rest[:len(rest) - cs], rest[len(rest) - cs:]

        @pl.when(first())
        def _():
            comm.start(c_in, c_out, c_sem)

        body(*h_in, *h_out, *h_scr)

        @pl.when(last())
        def _():
            comm.finish(c_in, c_out, c_sem)

    return wrapped


def _grid_ends(grid):
    first = lambda: functools.reduce(lambda a, b: a & b, [pl.program_id(i) == 0 for i in range(len(grid))])
    last = lambda: functools.reduce(lambda a, b: a & b, [pl.program_id(i) == g - 1 for i, g in enumerate(grid)])
    return first, last


def _call_with_comm(body, grid, name, in_specs, args, out_specs, out_shape, comm, scratch=(), sem=None):
    sem = sem or ("parallel",) * len(grid)
    scratch = list(scratch)
    if comm is not None:
        body = _attach(comm, body, len(args), len(out_shape), *_grid_ends(grid))
        in_specs, args = in_specs + [HBM_SPEC] * len(comm.inputs), args + comm.inputs
        out_specs, out_shape = out_specs + [HBM_SPEC] * len(comm.out_shapes), out_shape + comm.out_shapes
        scratch, sem = scratch + comm.sems, ("arbitrary",) * len(grid)
    return pl.pallas_call(body, grid=grid, name=name, in_specs=in_specs, out_specs=out_specs, out_shape=out_shape,
                          scratch_shapes=scratch, compiler_params=_cp(*sem))(*args)


class ScatterComm:
    def __init__(self, groups):
        self.sizes = [len(g) for g in groups]
        self.rows = [[pc.shape[1] for pc in g] for g in groups]
        ng = len(groups)
        self.inputs = [pc for g in groups for pc in g]
        self.out_shapes = [jax.ShapeDtypeStruct((N_DEV, sum(self.rows[gi]), g[0].shape[2]), g[0].dtype) for gi, g in enumerate(groups)]
        self.sems = [pltpu.SemaphoreType.DMA((7 * ng,)), pltpu.SemaphoreType.DMA((7 * ng,)), pltpu.SemaphoreType.DMA((ng,))]

    def _peers(self):
        x, y, c = _me()
        out = []
        for k in range(1, N_DEV):
            px = 1 - x if k & 4 else x
            py = 1 - y if k & 2 else y
            pc = 1 - c if k & 1 else c
            out.append((k, 4 * px + 2 * py + pc, dict(device_id=(px, py, pc), device_id_type=MESH)))
        return 4 * x + 2 * y + c, out

    def start(self, ins, outs, sems):
        send_sems, recv_sems, local_sems = sems
        me, peers = self._peers()
        pos = 0
        for gi, size in enumerate(self.sizes):
            for i, pc in enumerate(ins[pos:pos + size]):
                dst = outs[gi].at[me, pl.ds(sum(self.rows[gi][:i]), self.rows[gi][i])]
                pltpu.make_async_copy(pc.at[me], dst, local_sems.at[gi]).start()
                for k, peer, kw in peers:
                    pltpu.make_async_remote_copy(src_ref=pc.at[peer], dst_ref=dst, send_sem=send_sems.at[7 * gi + k - 1],
                                                 recv_sem=recv_sems.at[7 * gi + k - 1], **kw).start()
            pos += size

    def finish(self, ins, outs, sems):
        send_sems, recv_sems, local_sems = sems
        me, peers = self._peers()
        whole = [pltpu.make_async_remote_copy(src_ref=outs[gi].at[peer], dst_ref=outs[gi].at[peer],
                                              send_sem=send_sems.at[7 * gi + k - 1], recv_sem=recv_sems.at[7 * gi + k - 1], **kw)
                 for gi in range(len(self.sizes)) for k, peer, kw in peers]
        for cp in whole:
            cp.wait_recv()
        for cp in whole:
            cp.wait_send()
        for gi in range(len(self.sizes)):
            pltpu.make_async_copy(outs[gi].at[me], outs[gi].at[me], local_sems.at[gi]).wait()


def sum_slots(recv, name, tr):
    n, R, C = recv.shape

    def body(r_ref, o_ref):
        acc = r_ref[0].astype(F32)
        for s in range(1, n):
            acc = acc + r_ref[s].astype(F32)
        o_ref[...] = acc

    return pl.pallas_call(
        body, grid=(R // tr,), name=name,
        in_specs=[pl.BlockSpec((n, tr, C), lambda i: (0, i, 0))], out_specs=pl.BlockSpec((tr, C), lambda i: (i, 0)),
        out_shape=jax.ShapeDtypeStruct((R, C), F32), compiler_params=_cp("parallel"),
    )(recv)


PACK_W, FLAT_W = 1024, 128
MAIN = [
    ("ffn1_w_gate", "col"), ("ffn1_w_up", "col"), ("ffn1_w_down", "row"),
    ("ffn2_w_gate", "col"), ("ffn2_w_up", "col"), ("ffn2_w_down", "row"),
    ("w_ssd_proj", "row"), ("w_mla_proj", "row"), ("w_out", "row"),
    ("w_xq", "row"), ("w_xk", "row"), ("w_xv", "row"), ("w_xo", "row"),
    ("w_uk", "col"), ("w_uv", "col"),
]
FLAT = [("w_in", "col"), ("w_uq", "col")]
BIG = MAIN + FLAT
SMALL = ["ffn1_pre_g", "ffn1_post_g", "mix_pre_g", "conv_b", "dt_bias", "a_log", "d_skip", "ssd_norm_g", "q_norm_g",
         "kv_norm_g", "gate_bias", "mix_post_g", "xa_pre_g", "mem_norm_g", "xa_post_g", "ffn2_pre_g", "ffn2_post_g"]
WEIGHTS = ['ffn1_pre_g', 'ffn1_w_gate', 'ffn1_w_up', 'ffn1_w_down', 'ffn1_post_g', 'mix_pre_g', 'w_in', 'conv_w', 'conv_b',
           'dt_bias', 'a_log', 'd_skip', 'ssd_norm_g', 'w_ssd_proj', 'q_norm_g', 'w_uq', 'kv_norm_g', 'w_uk', 'w_uv',
           'w_mla_proj', 'gate_bias', 'w_out', 'mix_post_g', 'xa_pre_g', 'mem_norm_g', 'w_xq', 'w_xk', 'w_xv', 'w_xo',
           'xa_post_g', 'ffn2_pre_g', 'ffn2_w_gate', 'ffn2_w_up', 'ffn2_w_down', 'ffn2_post_g']


def _pack_rows(w, kind, width):
    m = w[0].T if kind == "col" else w[0]
    return m.reshape(-1, width)


KIND = dict(BIG)
GATHER_PLAN = {
    "first": (["ffn1_w_gate", "ffn1_w_up", "ffn1_w_down"], []),
    "ffn1_gate_up": (["w_ssd_proj", "w_mla_proj", "w_out", "w_uk", "w_uv"], ["w_in", "w_uq"]),
    "attn_fwd": (["w_xq", "w_xk", "w_xv", "w_xo", "ffn2_w_gate", "ffn2_w_up", "ffn2_w_down"], []),
}
SCATTER_PLAN = {
    "attn_bwd": [["ffn2_w_gate", "ffn2_w_up", "ffn2_w_down"], ["w_xq", "w_xk", "w_xv", "w_xo"]],
    "ssd_bwd": [["w_ssd_proj", "w_mla_proj", "w_out"], ["w_uk", "w_uv"], ["w_uq"]],
    "in_bwd": [["w_in#0"]],
    "ffn1:down_bwd": [["w_in#1"]],
    "ffn1:dwg": [["ffn1_w_down"]],
    "ffn1:dwu": [["ffn1_w_gate"]],
    "ffn1:gate_up_bwd": [["ffn1_w_up"]],
}
PARTS = {"w_in#0": ("w_in", 0, 2656), "w_in#1": ("w_in", 2656, 5296)}


class Stage:
    def __init__(self, w):
        self.w = w
        self.width = {n: PACK_W if (n, k) in MAIN else FLAT_W for n, k in BIG}
        self.nrows = {n: math.prod(w[n].shape) // self.width[n] for n, _ in BIG}
        self.recv = {}

    def _shards(self, tag):
        names_main, names_flat = GATHER_PLAN[tag]
        pack = lambda n: _pack_rows(self.w[n], KIND[n], self.width[n]).astype(BF16)
        shards = []
        if names_main:
            shards.append(jnp.concatenate([pack(n) for n in names_main], axis=0))
        if names_flat:
            bits = lax.bitcast_convert_type(self.w["conv_w"][0], BF16).reshape(-1, FLAT_W)
            shards.append(_pad_rows(jnp.concatenate([pack(n) for n in names_flat] + [bits], axis=0), 16))
        return shards

    def gather(self, tag):
        return GatherComm(self._shards(tag))

    def gathered(self, tag, outs, W, p):
        names_main, names_flat = GATHER_PLAN[tag]
        outs = list(outs)
        for names in (names_main, names_flat):
            if not names:
                continue
            buf, r0 = outs.pop(0), 0
            for n in names:
                K = self.w[n].shape[1] if KIND[n] == "col" else PACK_W
                W[n] = buf[:, r0:r0 + self.nrows[n]].reshape(-1, K)
                r0 += self.nrows[n]
            if names is names_flat:
                cw = self.w["conv_w"]
                nbits = 2 * math.prod(cw.shape) // FLAT_W
                bits = buf[:, r0:r0 + nbits].reshape((N_DEV,) + cw.shape[1:] + (2,))
                p["conv_w"] = lax.bitcast_convert_type(bits, F32).transpose(1, 0, 2).reshape(cw.shape[1], -1)

    def scatter(self, tag, gw):
        if tag not in SCATTER_PLAN:
            return None
        def piece(n):
            if n in PARTS:
                base, r0, r1 = PARTS[n]
                return gw[base].reshape(N_DEV, self.nrows[base], self.width[base])[:, r0:r1]
            return gw[n].reshape(N_DEV, self.nrows[n], self.width[n])
        return ScatterComm([[piece(n) for n in names] for names in SCATTER_PLAN[tag]])

    def scattered(self, tag, outs):
        if tag in SCATTER_PLAN:
            self.recv[tag] = outs


def _pad_rows(a, mult):
    r = (-a.shape[0]) % mult
    return a if r == 0 else jnp.concatenate([a, jnp.zeros((r,) + a.shape[1:], a.dtype)], axis=0)


def _pack_small(vals, loss_row=None, conv_w=None):
    rows = []
    for v in vals:
        f = v.reshape(-1)
        f = jnp.concatenate([f, jnp.zeros(((-f.shape[0]) % 128,), F32)])
        rows.append(f.reshape(-1, 128))
    if conv_w is not None:
        rows.append(conv_w.reshape(-1, 128))
    if loss_row is not None:
        rows.append(loss_row)
    return _pad_rows(jnp.concatenate(rows, axis=0), 8)


def _unpack_small(buf, shapes):
    out, r = [], 0
    for shp in shapes:
        n = math.prod(shp)
        nr = -(-n // 128)
        out.append(buf[r:r + nr].reshape(-1)[:n].reshape(shp))
        r += nr
    return out, r


def _tn(a, b, name, out_dtype=BF16, comm=None):
    M, N = a.shape[1], b.shape[1]
    T = a.shape[0]
    tm = M if M <= 1536 else M // 2
    tk = 1024 if T % 1024 == 0 and T > 1024 else None
    res = mm([[(a, b, "tn")]], [out_dtype], name, tm=tm, tn=N, tk=tk, comm=comm)
    return res[0] if comm is None else (res[0], res[1:])


class NoStage:
    def gather(self, tag):
        return None

    def gathered(self, tag, outs, W, p):
        pass

    def scatter(self, tag, gw):
        return None

    def scattered(self, tag, outs):
        pass


def _ffn_fwd(x, gpre, gpost, wg_t, wu_t, wd, tag, comm=None):
    h = rms_fwd(x, gpre, tag + "_pre")
    def swi(accs, ex):
        sil, dsil = _silu_parts(accs[0])
        return sil, accs[1] * dsil, sil * accs[1]
    res = mm([[(h, wg_t, "nt")], [(h, wu_t, "nt")]], [BF16, BF16, BF16], tag + "_gate_up", tn=DFF // 2, epi=swi, comm=comm,
             sub=4 if h.shape[0] % 1024 == 0 else 1)
    G, U, A = res[:3]
    H, y = mm_resid(A, wd, x, gpost, FFN_RES, tag + "_down")
    return y, (x, h, G, U, A, H), res[3:]


def _ffn_bwd(dy, saved, gpre, gpost, wg_t, wu_t, wd, tag, stage, gw):
    x, h, G, U, A, H = saved
    dH, dgpost = resid_bwd(H, gpost, dy, FFN_RES, tag + "_post_bwd")

    def dswi(accs, ex):
        return accs[0] * ex[1], accs[0] * ex[0]

    def hosted(where, call):
        comm = stage.scatter(tag + ":" + where, gw)
        res = call(comm)
        if comm is None:
            return res
        stage.scattered(tag + ":" + where, res[1])
        return res[0]

    res = hosted("down_bwd", lambda comm: (lambda r: r if comm is None else (r[:2], r[2:]))(
        mm([[(dH, wd, "nt")]], [BF16, BF16], tag + "_down_bwd", tn=DFF // 2, epi=dswi, extras=[G, U], comm=comm,
           sub=4 if dH.shape[0] % 1024 == 0 else 1)))
    dG, dU = res
    gw[tag + "_w_down"] = _tn(A, dH, tag + "_dwd")
    gw[tag + "_w_gate"] = hosted("dwg", lambda comm: _tn(dG, h, tag + "_dwg", comm=comm))
    gw[tag + "_w_up"] = hosted("dwu", lambda comm: _tn(dU, h, tag + "_dwu", comm=comm))
    dx, dgpre = hosted("gate_up_bwd", lambda comm: (lambda r: r[:2] if comm is None else (r[:2], r[2:]))(
        mm_rms_bwd([(dG, wg_t, "nn"), (dU, wu_t, "nn")], x, gpre, tag + "_gate_up_bwd", resid=dy, comm=comm)))
    return dx, dgpre, dgpost


def _rope_tables(positions):
    inv = ROPE_THETA ** (-jnp.arange(0, ROPE, 2, dtype=F32) / ROPE)
    ang = positions.astype(F32).reshape(-1)[:, None] * inv
    return jnp.cos(ang), jnp.sin(ang)


def _local_step(x, mem, positions, tgt, W, p, stage=None):
    stage = stage or NoStage()
    nseq = x.shape[0]
    T = nseq * x.shape[1]
    x0 = x.reshape(T, D)
    mem2 = mem.reshape(-1, D)
    cos, sin = _rope_tables(positions)

    x1, ffn1, arrived = _ffn_fwd(x0, p["ffn1_pre_g"], p["ffn1_post_g"], W["ffn1_w_gate"], W["ffn1_w_up"], W["ffn1_w_down"],
                                 "ffn1", comm=stage.gather("ffn1_gate_up"))
    stage.gathered("ffn1_gate_up", arrived, W, p)

    w_in_t = W["w_in"]
    bounds = [0]
    for n in (SSD_INNER, CONV_CH, SSD_H, QR, KVR, ROPE, 2 * D):
        bounds.append(bounds[-1] + n)
    wt_z, wt_xbc, wt_dt, wt_q, wt_kv, wt_kr, wt_gate = [w_in_t[bounds[i]:bounds[i + 1]] for i in range(7)]
    wt_dt, wt_kr = _pad_rows(wt_dt, SLOT), _pad_rows(wt_kr, SLOT)
    wt_dtkr = jnp.concatenate([wt_dt, wt_kr], axis=0)
    hm = rms_fwd(x1, p["mix_pre_g"], "mix_pre")
    z = mm1(hm, wt_z, "nt", BF16, "in_z")
    xbc = mm1(hm, wt_xbc, "nt", BF16, "in_xbc")
    q_c = mm1(hm, wt_q, "nt", F32, "in_q", tn=QR)
    kv_c = mm1(hm, wt_kv, "nt", F32, "in_kv")
    dtkr = mm1(hm, wt_dtkr, "nt", F32, "in_dtkr")
    gl = mm1(hm, wt_gate, "nt", BF16, "in_gate")

    xbc_act = conv_fwd(xbc, p["conv_w"], p["conv_b"], nseq)
    y_ssd_core, prev = ssd_fwd(xbc_act, dtkr, p["dt_bias"], p["a_log"], p["d_skip"], nseq)
    yn = gated_norm_fwd(y_ssd_core, z, p["ssd_norm_g"], "ssd_norm")
    y_ssd = mm1(yn, W["w_ssd_proj"], "nn", BF16, "ssd_proj")

    slot_rows = lambda wt, per: jnp.pad(wt.reshape(MLA_H, per, -1), ((0, 0), (0, SLOT - per), (0, 0))).reshape(MLA_H * SLOT, -1)
    wq_s, wk_s, wv_s = slot_rows(W["w_uq"], QK), slot_rows(W["w_uk"], NOPE), slot_rows(W["w_uv"], VD)
    wo_s = slot_rows(W["w_mla_proj"], VD)
    qn = rms_fwd(q_c, p["q_norm_g"], "q_norm")
    q_s = mm1(qn, wq_s, "nt", BF16, "uq")
    kvn = rms_fwd(kv_c, p["kv_norm_g"], "kv_norm")
    kn_s = mm1(kvn, wk_s, "nt", BF16, "uk")
    v_s = mm1(kvn, wv_s, "nt", BF16, "uv")
    cos16, sin16 = cos, sin
    Qc, Kc = rope_slot_fwd(q_s, kn_s, dtkr, cos16, sin16, "rope")
    o_s, lse, *arrived = attn_slot_fwd(Qc, Kc, v_s, nseq, comm=stage.gather("attn_fwd"))
    stage.gathered("attn_fwd", arrived, W, p)
    y_mla = mm1(o_s, wo_s, "nn", BF16, "mla_proj")

    merged = merge_fwd(gl, y_ssd, y_mla, p["gate_bias"], "merge")
    hmix, x2 = mm_resid(merged, W["w_out"], x1, p["mix_post_g"], 1.0, "mix_out")

    hq = rms_fwd(x2, p["xa_pre_g"], "xa_pre")
    mn = rms_fwd(mem2, p["mem_norm_g"], "mem_norm")
    xq = mm1(hq, W["w_xq"], "nn", BF16, "xq")
    xk = mm1(mn, W["w_xk"], "nn", BF16, "xk")
    xv = mm1(mn, W["w_xv"], "nn", BF16, "xv")
    xo = xattn_fwd(xq, xk, xv, nseq)
    ho, x3 = mm_resid(xo, W["w_xo"], x2, p["xa_post_g"], 1.0, "xo")

    x4, ffn2, _ = _ffn_fwd(x3, p["ffn2_pre_g"], p["ffn2_post_g"], W["ffn2_w_gate"], W["ffn2_w_up"], W["ffn2_w_down"], "ffn2")
    dx4, loss_row = loss_head(x4, tgt.reshape(T, D), "loss")

    gw, gs = {}, {}
    dx3, gs["ffn2_pre_g"], gs["ffn2_post_g"] = _ffn_bwd(
        dx4, ffn2, p["ffn2_pre_g"], p["ffn2_post_g"], W["ffn2_w_gate"], W["ffn2_w_up"], W["ffn2_w_down"], "ffn2", stage, gw)

    dho, gs["xa_post_g"] = resid_bwd(ho, p["xa_post_g"], dx3, 1.0, "xa_post_bwd")
    dxo = mm1(dho, W["w_xo"], "nt", BF16, "xo_bwd")
    gw["w_xo"] = _tn(xo, dho, "d_w_xo")
    dxq, dxk, dxv = xattn_bwd(xq, xk, xv, dxo, nseq)
    dx2, gs["xa_pre_g"] = mm_rms_bwd([(dxq, W["w_xq"], "nt")], x2, p["xa_pre_g"], "xq_bwd", resid=dx3)
    gw["w_xq"] = _tn(hq, dxq, "d_w_xq")
    dmn = mm([[(dxk, W["w_xk"], "nt"), (dxv, W["w_xv"], "nt")]], [F32], "xkv_bwd")[0]
    gw["w_xk"] = _tn(mn, dxk, "d_w_xk")
    gw["w_xv"] = _tn(mn, dxv, "d_w_xv")
    _, gs["mem_norm_g"] = rms_bwd(mem2, p["mem_norm_g"], dmn, "mem_norm_bwd", dx_dtype=BF16)

    dhmix, gs["mix_post_g"] = resid_bwd(hmix, p["mix_post_g"], dx2, 1.0, "mix_post_bwd")
    dmerged = mm1(dhmix, W["w_out"], "nt", F32, "mix_out_bwd")
    gw["w_out"] = _tn(merged, dhmix, "d_w_out")
    dys, dym, dgl, gs["gate_bias"] = merge_bwd(gl, y_ssd, y_mla, dmerged, p["gate_bias"], "merge_bwd")

    unslot = lambda g, per: g.reshape(MLA_H, SLOT, -1)[:, :per].reshape(MLA_H * per, -1)
    do_s = mm1(dym, wo_s, "nt", BF16, "mla_proj_bwd")
    gw["w_mla_proj"] = unslot(_tn(o_s, dym, "d_w_mla_proj"), VD)
    dQc, dKc, dv_s, *sent = attn_slot_bwd(Qc, Kc, v_s, o_s, lse, do_s, nseq, comm=stage.scatter("attn_bwd", gw))
    stage.scattered("attn_bwd", sent)
    dq_s, dkn_s, dkr = rope_slot_bwd(dQc, dKc, cos16, sin16, "rope_bwd")
    dq_c, gs["q_norm_g"] = mm_rms_bwd([(dq_s, wq_s, "nn")], q_c, p["q_norm_g"], "uq_bwd", dx_dtype=BF16)
    gw["w_uq"] = unslot(_tn(dq_s, qn, "d_w_uq"), QK)
    dkv_c, gs["kv_norm_g"] = mm_rms_bwd([(dkn_s, wk_s, "nn"), (dv_s, wv_s, "nn")], kv_c, p["kv_norm_g"], "ukv_bwd", dx_dtype=BF16)
    gw["w_uk"] = unslot(_tn(dkn_s, kvn, "d_w_uk"), NOPE)
    gw["w_uv"] = unslot(_tn(dv_s, kvn, "d_w_uv"), VD)

    dyn = mm1(dys, W["w_ssd_proj"], "nt", F32, "ssd_proj_bwd")
    gw["w_ssd_proj"] = _tn(yn, dys, "d_w_ssd_proj")
    dyc, dz, gs["ssd_norm_g"] = gated_norm_bwd(y_ssd_core, z, dyn, p["ssd_norm_g"], "ssd_norm_bwd")
    dxbc_act, ddtr, gs["dt_bias"], gs["a_log"], gs["d_skip"], *sent = ssd_bwd(
        xbc_act, dtkr, p["dt_bias"], p["a_log"], p["d_skip"], prev, dyc, nseq, comm=stage.scatter("ssd_bwd", gw))
    stage.scattered("ssd_bwd", sent)
    dxbc, gs["conv_w"], gs["conv_b"] = conv_bwd(xbc, p["conv_w"], p["conv_b"], dxbc_act, nseq)

    gw["w_in"] = jnp.concatenate([_tn(dz, hm, "d_w_in_z"), _tn(dxbc, hm, "d_w_in_xbc"), _tn(ddtr, hm, "d_w_in_dt")[:SSD_H],
                                  _tn(dq_c, hm, "d_w_in_q"), _tn(dkv_c, hm, "d_w_in_kv"), _tn(dkr, hm, "d_w_in_kr")[:ROPE],
                                  _tn(dgl, hm, "d_w_in_gate")], axis=0)
    dx1, gs["mix_pre_g"], *sent = mm_rms_bwd(
        [(dz, wt_z, "nn"), (dxbc, wt_xbc, "nn"), (ddtr, wt_dt, "nn"), (dq_c, wt_q, "nn"), (dkv_c, wt_kv, "nn"),
         (dkr, wt_kr, "nn"), (dgl, wt_gate, "nn")], x1, p["mix_pre_g"], "in_bwd", resid=dx2, comm=stage.scatter("in_bwd", gw))
    stage.scattered("in_bwd", sent)

    dx0, gs["ffn1_pre_g"], gs["ffn1_post_g"] = _ffn_bwd(
        dx1, ffn1, p["ffn1_pre_g"], p["ffn1_post_g"], W["ffn1_w_gate"], W["ffn1_w_up"], W["ffn1_w_down"], "ffn1", stage, gw)
    return loss_row, dx0.reshape(x.shape), gw, gs


def kernel(x, mem, positions, ffn1_pre_g, ffn1_w_gate, ffn1_w_up, ffn1_w_down, ffn1_post_g, mix_pre_g, w_in, conv_w, conv_b, dt_bias, a_log, d_skip, ssd_norm_g, w_ssd_proj, q_norm_g, w_uq, kv_norm_g, w_uk, w_uv, w_mla_proj, gate_bias, w_out, mix_post_g, xa_pre_g, mem_norm_g, w_xq, w_xk, w_xv, w_xo, xa_post_g, ffn2_pre_g, ffn2_w_gate, ffn2_w_up, ffn2_w_down, ffn2_post_g, loss_target, m_ffn1_pre_g, m_ffn1_w_gate, m_ffn1_w_up, m_ffn1_w_down, m_ffn1_post_g, m_mix_pre_g, m_w_in, m_conv_w, m_conv_b, m_dt_bias, m_a_log, m_d_skip, m_ssd_norm_g, m_w_ssd_proj, m_q_norm_g, m_w_uq, m_kv_norm_g, m_w_uk, m_w_uv, m_w_mla_proj, m_gate_bias, m_w_out, m_mix_post_g, m_xa_pre_g, m_mem_norm_g, m_w_xq, m_w_xk, m_w_xv, m_w_xo, m_xa_post_g, m_ffn2_pre_g, m_ffn2_w_gate, m_ffn2_w_up, m_ffn2_w_down, m_ffn2_post_g, v_ffn1_pre_g, v_ffn1_w_gate, v_ffn1_w_up, v_ffn1_w_down, v_ffn1_post_g, v_mix_pre_g, v_w_in, v_conv_w, v_conv_b, v_dt_bias, v_a_log, v_d_skip, v_ssd_norm_g, v_w_ssd_proj, v_q_norm_g, v_w_uq, v_kv_norm_g, v_w_uk, v_w_uv, v_w_mla_proj, v_gate_bias, v_w_out, v_mix_post_g, v_xa_pre_g, v_mem_norm_g, v_w_xq, v_w_xk, v_w_xv, v_w_xo, v_xa_post_g, v_ffn2_pre_g, v_ffn2_w_gate, v_ffn2_w_up, v_ffn2_w_down, v_ffn2_post_g):
    a = dict(locals())
    w = {n: a[n] for n in WEIGHTS}
    m = {n: a["m_" + n] for n in WEIGHTS}
    v = {n: a["v_" + n] for n in WEIGHTS}

    stage = Stage(w)
    W, p = {}, {n: w[n] for n in SMALL}
    stage.gathered("first", run_comm(stage.gather("first"), "allgather_first"), W, p)

    loss_row, grad_x, gw, gs = _local_step(x, mem, positions, loss_target, W, p, stage)

    sm = _pack_small([gs[n] for n in SMALL], loss_row=loss_row, conv_w=gs["conv_w"])
    srecv, = run_comm(ScatterComm([[jnp.broadcast_to(sm[None], (N_DEV,) + sm.shape)]]), "exchange_small")
    s_rows = sum_slots(srecv, "sum_small", tr=sm.shape[0])
    grads, delta, new_m, new_v = {}, {}, {}, {}

    def finish(n, buf, piece):
        K = w[n].shape[1]
        if KIND[n] == "col" and buf.shape[2] != K:
            buf = buf.reshape(buf.shape[0], -1, K)
        res = adamw_from_slots(buf, piece, w[n][0], m[n][0], v[n][0], "adamw_" + n, transposed=KIND[n] == "col")
        grads[n], delta[n], new_m[n], new_v[n] = [r[None] for r in res]

    parts = {}
    for tag, groups in SCATTER_PLAN.items():
        for names, buf in zip(groups, stage.recv[tag]):
            for piece, n in enumerate(names):
                if n in PARTS:
                    parts[n] = sum_slots(buf, "sum_" + n.replace("#", "_"), tr=buf.shape[1])
                else:
                    finish(n, buf, piece)
    for base in sorted({b for b, _, _ in PARTS.values()}):
        rows = jnp.concatenate([parts[pn] for pn in sorted(PARTS) if PARTS[pn][0] == base], axis=0)
        finish(base, rows[None], 0)
    conv_w_full = p["conv_w"]
    small_g, r1 = _unpack_small(s_rows, [w[n].shape for n in SMALL])
    for n, g in zip(SMALL, small_g):
        grads[n] = g
    ncw = math.prod(conv_w_full.shape) // 128
    cw_grad_full = s_rows[r1:r1 + ncw].reshape(conv_w_full.shape)
    wsh = conv_w.shape[2]
    grads["conv_w"] = lax.dynamic_slice_in_dim(cw_grad_full, _dev_index() * wsh, wsh, axis=1)[None]
    loss = s_rows[r1 + ncw, 0]

    d_, m_, v_ = adamw(conv_w[0], grads["conv_w"][0], m["conv_w"][0], v["conv_w"][0], "adamw_conv_w")
    delta["conv_w"], new_m["conv_w"], new_v["conv_w"] = d_[None], m_[None], v_[None]
    sp =[_pack_small([t[n] for n in SMALL]) for t in (w, grads, m, v)]
    outs = adamw(sp[0], sp[1], sp[2], sp[3], "adamw_small")
    for t, buf in zip((delta, new_m, new_v), outs):
        vals, _ = _unpack_small(buf, [w[n].shape for n in SMALL])
        for n, val in zip(SMALL, vals):
            t[n] = val
    return (loss, grad_x, *[grads[n] for n in WEIGHTS], *[delta[n] for n in WEIGHTS],
            *[new_m[n] for n in WEIGHTS], *[new_v[n] for n in WEIGHTS])
```

```python
import functools
import math

import jax
import jax.numpy as jnp
from jax import lax
from jax.experimental import pallas as pl
from jax.experimental.pallas import tpu as pltpu

F32, BF16 = jnp.float32, jnp.bfloat16
HI = lax.Precision.HIGHEST
MESH = pl.DeviceIdType.MESH
N_DEV = 8

D = 1024
DFF = 2816
SSD_H, SSD_P, SSD_G, SSD_N, SSD_L = 16, 64, 2, 128, 128
SSD_INNER = SSD_H * SSD_P
CONV_K, CONV_CH = 4, 1536
MLA_H, QR, KVR, NOPE, ROPE, VD = 16, 384, 256, 64, 32, 64
QK = NOPE + ROPE
ROPE_THETA = 10000.0
XA_H, XA_D = 4, 256
EPS = 1e-6
FFN_RES = 0.5
LR, B1, B2, AEPS, WD, STEP = 0.001, 0.9, 0.999, 1e-08, 0.01, 10

VMEM_LIMIT = 56 * 2**20


def _cp(*sem):
    return pltpu.CompilerParams(dimension_semantics=sem, vmem_limit_bytes=VMEM_LIMIT)


def _sigmoid(x):
    return 1.0 / (1.0 + jnp.exp(-x))


def _softplus(x):
    return jnp.where(x > 20.0, x, jnp.log(1.0 + jnp.exp(jnp.minimum(x, 20.0))))


def _dot(a, b, dims="nn"):
    ca = 0 if dims[0] == "t" else 1
    cb = 1 if dims[1] == "t" else 0
    return lax.dot_general(a.astype(BF16), b.astype(BF16), (((ca,), (cb,)), ((), ())), preferred_element_type=F32)


def _dot_sel(a, b, dims="nn", split="a", terms=3):
    r = (a if split == "a" else b).astype(F32)
    out = None
    for t in range(terms):
        piece = r.astype(BF16)
        if t + 1 < terms:
            r = r - piece.astype(F32)
        d = _dot(piece, b, dims) if split == "a" else _dot(a, piece, dims)
        out = d if out is None else out + d
    return out


def _ssd_common(dtr, dtb, alog):
    L = dtr.shape[0]
    dt = _softplus(dtr + dtb)
    a = -jnp.exp(alog)
    adt = dt * a
    r = lax.broadcasted_iota(jnp.int32, (L, L), 0)
    c = lax.broadcasted_iota(jnp.int32, (L, L), 1)
    lower = r >= c
    tri = lower.astype(F32)
    cs = _dot_sel(tri, adt, "nn", split="b")
    cs_t = _dot_sel(adt, tri, "tt")
    return dt, a, cs, cs_t, lower


def _head_expand():
    hh = lax.broadcasted_iota(jnp.int32, (SSD_H, SSD_INNER), 0)
    jj = lax.broadcasted_iota(jnp.int32, (SSD_H, SSD_INNER), 1)
    return ((jj >= hh * SSD_P) & (jj < hh * SSD_P + SSD_P)).astype(F32)


def _head_reduce():
    hh = lax.broadcasted_iota(jnp.int32, (SSD_INNER, SSD_H), 1)
    jj = lax.broadcasted_iota(jnp.int32, (SSD_INNER, SSD_H), 0)
    return ((jj >= hh * SSD_P) & (jj < hh * SSD_P + SSD_P)).astype(F32)


def ssd_fwd(xbc, dtr, dtb, alog, dsk, nseq, comm=None):
    T = xbc.shape[0]
    S = T // nseq
    C = S // SSD_L
    L = SSD_L
    NP = SSD_H // 2

    def body(x_ref, b_ref, c_ref, dtr_ref, dtb_ref, alog_ref, dsk_ref, y_ref, prev_ref, st_ref):
        ci = pl.program_id(1)

        @pl.when(ci == 0)
        def _():
            st_ref[...] = jnp.zeros_like(st_ref)

        dt, a, cs, cs_t, lower = _ssd_common(dtr_ref[:, 0:SSD_H], dtb_ref[...], alog_ref[...])
        E = _head_expand()
        X = x_ref[...].astype(F32)
        dt_e = _dot_sel(dt, E)
        cs_e = _dot_sel(cs, E)
        csl_e = cs_e[L - 1:L, :]
        Xd = X * dt_e
        Xf = Xd * jnp.exp(csl_e - cs_e)
        e_e = jnp.exp(cs_e)
        skip = _dot_sel(dsk_ref[...], E) * X
        lane = lax.broadcasted_iota(jnp.int32, (1, 2 * SSD_P), 1)
        rowp = lax.broadcasted_iota(jnp.int32, (2 * SSD_P, 1), 0)
        for g in range(SSD_G):
            Bg = b_ref[:, g * SSD_N:(g + 1) * SSD_N]
            Cg = c_ref[:, g * SSD_N:(g + 1) * SSD_N]
            cb = _dot(Cg, Bg, "nt")
            for pp in range(NP // SSD_G):
                p = g * (NP // SSD_G) + pp
                sl = slice(p * 2 * SSD_P, (p + 1) * 2 * SSD_P)
                Xd_p = Xd[:, sl]
                yd = jnp.zeros((L, 2 * SSD_P), F32)
                for q in range(2):
                    h = 2 * p + q
                    m = jnp.where(lower, jnp.exp(jnp.minimum(cs[:, h:h + 1] - cs_t[h:h + 1, :], 0.0)), 0.0)
                    mask = (lane >= q * SSD_P) & (lane < (q + 1) * SSD_P)
                    yd = yd + _dot(cb * m, jnp.where(mask, Xd_p, 0.0))
                S0 = st_ref[p]
                prev_ref[0, 0, p] = S0
                z = _dot(Cg, S0, "nt")
                y_ref[:, sl] = (skip[:, sl] + yd + z * e_e[:, sl]).astype(y_ref.dtype)
                h0 = 2 * p
                dec = jnp.where(rowp < SSD_P, jnp.exp(cs[L - 1:L, h0:h0 + 1]), jnp.exp(cs[L - 1:L, h0 + 1:h0 + 2]))
                st_ref[p] = S0 * dec + _dot(Xf[:, sl], Bg, "tn")

    row = lambda b, c: (b * C + c, 0)
    small = pl.BlockSpec((1, SSD_H), lambda b, c: (0, 0))
    return _call_with_comm(
        body, (nseq, C), "ssd_fwd",
        [pl.BlockSpec((L, SSD_INNER), row),
         pl.BlockSpec((L, SSD_G * SSD_N), lambda b, c: (b * C + c, SSD_INNER // (SSD_G * SSD_N))),
         pl.BlockSpec((L, SSD_G * SSD_N), lambda b, c: (b * C + c, SSD_INNER // (SSD_G * SSD_N) + 1)),
         pl.BlockSpec((L, 128), row), small, small, small],
        [xbc, xbc, xbc, dtr, dtb, alog, dsk],
        [pl.BlockSpec((L, SSD_INNER), row), pl.BlockSpec((1, 1, NP, 2 * SSD_P, SSD_N), lambda b, c: (b, c, 0, 0, 0))],
        [jax.ShapeDtypeStruct((T, SSD_INNER), BF16), jax.ShapeDtypeStruct((nseq, C, NP, 2 * SSD_P, SSD_N), F32)],
        comm, scratch=[pltpu.VMEM((NP, 2 * SSD_P, SSD_N), F32)], sem=("parallel", "arbitrary"))


def ssd_bwd(xbc, dtr, dtb, alog, dsk, prev, dy, nseq, comm=None):
    T = xbc.shape[0]
    S = T // nseq
    C = S // SSD_L
    L = SSD_L
    NP = SSD_H // 2

    def body(x_ref, b_ref, c_ref, dtr_ref, dtb_ref, alog_ref, dsk_ref, prev_ref, dy_ref,
             dxbc_ref, ddtr_ref, ddtb_ref, dalog_ref, ddsk_ref, ds_ref, stg_ref):
        bi = pl.program_id(0)
        ci = pl.program_id(1)

        @pl.when(ci == 0)
        def _():
            ds_ref[...] = jnp.zeros_like(ds_ref)

        @pl.when((ci == 0) & (bi == 0))
        def _():
            ddtb_ref[...] = jnp.zeros_like(ddtb_ref)
            dalog_ref[...] = jnp.zeros_like(dalog_ref)
            ddsk_ref[...] = jnp.zeros_like(ddsk_ref)

        dtr = dtr_ref[:, 0:SSD_H]
        dtb = dtb_ref[...]
        dt, a, cs, cs_t, lower = _ssd_common(dtr, dtb, alog_ref[...])
        upper = lax.broadcasted_iota(jnp.int32, (L, L), 1) >= lax.broadcasted_iota(jnp.int32, (L, L), 0)
        E = _head_expand()
        ET = _head_reduce()
        X = x_ref[...].astype(F32)
        dY = dy_ref[...].astype(F32)
        dt_e = _dot_sel(dt, E)
        cs_e = _dot_sel(cs, E)
        csl_e = cs_e[L - 1:L, :]
        f_e = jnp.exp(csl_e - cs_e)
        e_e = jnp.exp(cs_e)
        dsk_e = _dot_sel(dsk_ref[...], E)
        Xd = X * dt_e
        Xf = Xd * f_e
        lane = lax.broadcasted_iota(jnp.int32, (1, 2 * SSD_P), 1)
        rowp = lax.broadcasted_iota(jnp.int32, (2 * SSD_P, 1), 0)
        hsel = lax.broadcasted_iota(jnp.int32, (1, SSD_H), 1)
        dcs = jnp.zeros((L, SSD_H), F32)
        dcsl = jnp.zeros((1, SSD_H), F32)
        for g in range(SSD_G):
            Bg = b_ref[:, g * SSD_N:(g + 1) * SSD_N]
            Cg = c_ref[:, g * SSD_N:(g + 1) * SSD_N]
            cb = _dot(Cg, Bg, "nt")
            cbt = _dot(Bg, Cg, "nt")
            dB = jnp.zeros((L, SSD_N), F32)
            dC = jnp.zeros((L, SSD_N), F32)
            for pp in range(NP // SSD_G):
                p = g * (NP // SSD_G) + pp
                sl = slice(p * 2 * SSD_P, (p + 1) * 2 * SSD_P)
                Xd_p = Xd[:, sl]
                dY_p = dY[:, sl]
                dXd_p = jnp.zeros((L, 2 * SSD_P), F32)
                for q in range(2):
                    h = 2 * p + q
                    mask = (lane >= q * SSD_P) & (lane < (q + 1) * SSD_P)
                    col = cs[:, h:h + 1]
                    rw = cs_t[h:h + 1, :]
                    m = jnp.where(lower, jnp.exp(jnp.minimum(col - rw, 0.0)), 0.0)
                    mt = jnp.where(upper, jnp.exp(jnp.minimum(rw - col, 0.0)), 0.0)
                    dYm = jnp.where(mask, dY_p, 0.0)
                    dW = _dot(dYm, Xd_p, "nt")
                    dWt = _dot(Xd_p, dYm, "nt")
                    w = cb * m
                    wt = cbt * mt
                    dC = dC + _dot(dW * m, Bg)
                    dB = dB + _dot(dWt * mt, Cg)
                    dXd_p = dXd_p + jnp.where(mask, _dot(wt, dY_p), 0.0)
                    qcol = jnp.sum(dW * w, axis=1, keepdims=True) - jnp.sum(dWt * wt, axis=1, keepdims=True)
                    dcs = dcs + qcol * (hsel == h).astype(F32)
                S0 = prev_ref[0, 0, p]
                dSn = ds_ref[p]
                dZ = dY_p * e_e[:, sl]
                dC = dC + _dot(dZ, S0)
                h0 = 2 * p
                el0 = jnp.exp(cs[L - 1:L, h0:h0 + 1])
                el1 = jnp.exp(cs[L - 1:L, h0 + 1:h0 + 2])
                dec = jnp.where(rowp < SSD_P, el0, el1)
                ds_ref[p] = dSn * dec + _dot(dZ, Cg, "tn")
                dXf_p = _dot(Bg, dSn, "nt")
                dB = dB + _dot(Xf[:, sl], dSn)
                rs = jnp.sum(dSn * S0, axis=1, keepdims=True)
                s0 = jnp.sum(jnp.where(rowp < SSD_P, rs, 0.0), axis=0, keepdims=True) * el0
                s1 = jnp.sum(jnp.where(rowp >= SSD_P, rs, 0.0), axis=0, keepdims=True) * el1
                dcsl = dcsl + s0 * (hsel == h0).astype(F32) + s1 * (hsel == h0 + 1).astype(F32)
                y_off = _dot(Cg, S0, "nt") * e_e[:, sl]
                t1 = dY_p * y_off - dXf_p * Xf[:, sl]
                r1 = jnp.where(lane < SSD_P, t1, 0.0)
                c0 = jnp.sum(r1, axis=1, keepdims=True)
                c1 = jnp.sum(t1 - r1, axis=1, keepdims=True)
                dcs = dcs + c0 * (hsel == h0).astype(F32) + c1 * (hsel == h0 + 1).astype(F32)
                t2 = dXf_p * Xf[:, sl]
                r2 = jnp.where(lane < SSD_P, t2, 0.0)
                dcsl = dcsl + jnp.sum(r2, keepdims=True) * (hsel == h0).astype(F32) \
                    + jnp.sum(t2 - r2, keepdims=True) * (hsel == h0 + 1).astype(F32)
                stg_ref[:, sl] = dXd_p + dXf_p * f_e[:, sl]
            dxbc_ref[:, SSD_INNER + g * SSD_N:SSD_INNER + (g + 1) * SSD_N] = dB.astype(dxbc_ref.dtype)
            dxbc_ref[:, SSD_INNER + (SSD_G + g) * SSD_N:SSD_INNER + (SSD_G + g + 1) * SSD_N] = dC.astype(dxbc_ref.dtype)
        dXd = stg_ref[...]
        dxbc_ref[:, 0:SSD_INNER] = (dXd * dt_e + dsk_e * dY).astype(dxbc_ref.dtype)
        rowl = lax.broadcasted_iota(jnp.int32, (L, 1), 0)
        dcs = dcs + jnp.where(rowl == L - 1, dcsl, 0.0)
        dalpha = _dot_sel(upper.astype(F32), dcs, split="b")
        ddt = _dot_sel(dXd * X, ET, terms=2) + dalpha * a
        dalog_ref[...] += jnp.sum(dalpha * dt, axis=0, keepdims=True) * a
        ddtr = ddt * _sigmoid(dtr + dtb)
        spread = (lax.broadcasted_iota(jnp.int32, (SSD_H, 128), 0) == lax.broadcasted_iota(jnp.int32, (SSD_H, 128), 1)).astype(F32)
        ddtr_ref[...] = _dot(ddtr, spread).astype(ddtr_ref.dtype)
        ddtb_ref[...] += jnp.sum(ddtr, axis=0, keepdims=True)
        ddsk_ref[...] += jnp.sum(_dot_sel(dY * X, ET, terms=2), axis=0, keepdims=True)

    rowr = lambda b, c: (b * C + (C - 1 - c), 0)
    small = pl.BlockSpec((1, SSD_H), lambda b, c: (0, 0))
    return _call_with_comm(
        body, (nseq, C), "ssd_bwd",
        [pl.BlockSpec((L, SSD_INNER), rowr),
         pl.BlockSpec((L, SSD_G * SSD_N), lambda b, c: (b * C + (C - 1 - c), SSD_INNER // (SSD_G * SSD_N))),
         pl.BlockSpec((L, SSD_G * SSD_N), lambda b, c: (b * C + (C - 1 - c), SSD_INNER // (SSD_G * SSD_N) + 1)),
         pl.BlockSpec((L, 128), rowr), small, small, small,
         pl.BlockSpec((1, 1, NP, 2 * SSD_P, SSD_N), lambda b, c: (b, C - 1 - c, 0, 0, 0)),
         pl.BlockSpec((L, SSD_INNER), rowr)],
        [xbc, xbc, xbc, dtr, dtb, alog, dsk, prev, dy],
        [pl.BlockSpec((L, CONV_CH), rowr), pl.BlockSpec((L, 128), rowr), small, small, small],
        [jax.ShapeDtypeStruct((T, CONV_CH), BF16), jax.ShapeDtypeStruct((T, 128), BF16),
         jax.ShapeDtypeStruct((1, SSD_H), F32), jax.ShapeDtypeStruct((1, SSD_H), F32), jax.ShapeDtypeStruct((1, SSD_H), F32)],
        comm, scratch=[pltpu.VMEM((NP, 2 * SSD_P, SSD_N), F32), pltpu.VMEM((L, SSD_INNER), F32)], sem=("arbitrary", "arbitrary"))


SLOT = 128
ATT_T = 512
LOG2E = math.log2(math.e)
Q_SCALE = QK ** -0.5 * LOG2E


def _col_to_row(col):
    n = col.shape[0]
    eye = lax.broadcasted_iota(jnp.int32, (n, n), 0) == lax.broadcasted_iota(jnp.int32, (n, n), 1)
    return jnp.sum(jnp.where(eye, col, 0.0), axis=0, keepdims=True)


def attn_slot_fwd(q, k, v, nseq, comm=None):
    T = q.shape[0]
    S = T // nseq
    t = min(ATT_T, S)
    nb = S // t

    def body(q_ref, k_ref, v_ref, o_ref, lse_ref):
        causal = lax.broadcasted_iota(jnp.int32, (t, t), 1) <= lax.broadcasted_iota(jnp.int32, (t, t), 0)
        for qi in range(nb):
            qb = q_ref[qi * t:(qi + 1) * t, :]
            m = l = acc = None
            for kj in range(qi + 1):
                s = _dot(qb, k_ref[kj * t:(kj + 1) * t, :], "nt")
                if kj == qi:
                    s = jnp.where(causal, s, -1e30)
                bm = jnp.max(s, axis=1, keepdims=True)
                if kj == 0:
                    m = bm
                    p = jnp.exp2(s - m)
                    l = jnp.sum(p, axis=1, keepdims=True)
                    acc = _dot(p, v_ref[0:t, :])
                else:
                    m_new = jnp.maximum(m, bm)
                    corr = jnp.exp2(m - m_new)
                    p = jnp.exp2(s - m_new)
                    l = l * corr + jnp.sum(p, axis=1, keepdims=True)
                    acc = acc * corr + _dot(p, v_ref[kj * t:(kj + 1) * t, :])
                    m = m_new
            o_ref[qi * t:(qi + 1) * t, :] = (acc / l).astype(o_ref.dtype)
            lse_ref[0, 0, :, qi * t:(qi + 1) * t] = _col_to_row(m + jnp.log2(l))

    blk = pl.BlockSpec((S, SLOT), lambda b, h: (b, h))
    return _call_with_comm(
        body, (nseq, MLA_H), "attn_fwd", [blk, blk, blk], [q, k, v],
        [blk, pl.BlockSpec((1, 1, 1, S), lambda b, h: (b, h, 0, 0))],
        [jax.ShapeDtypeStruct((T, MLA_H * SLOT), BF16), jax.ShapeDtypeStruct((nseq, MLA_H, 1, S), F32)], comm)


def attn_slot_bwd(q, k, v, o, lse, do, nseq, comm=None):
    T = q.shape[0]
    S = T // nseq
    t = min(ATT_T, S)
    nb = S // t
    scale = QK ** -0.5

    def body(q_ref, k_ref, v_ref, o_ref, lse_ref, do_ref, dq_ref, dk_ref, dv_ref, dqa_ref):
        causal_t =lax.broadcasted_iota(jnp.int32, (t, t), 0) <= lax.broadcasted_iota(jnp.int32, (t, t), 1)
        ones = jnp.ones((8, SLOT), F32)
        delta = []
        for qi in range(nb):
            sl = slice(qi * t, (qi + 1) * t)
            prod = do_ref[sl, :].astype(F32) * o_ref[sl, :].astype(F32)
            delta.append(_dot_sel(ones, prod, "nt", split="b", terms=2)[0:1, :])
        for kj in range(nb):
            ks = slice(kj * t, (kj + 1) * t)
            kb = k_ref[ks, :]
            vb = v_ref[ks, :]
            dk = dv = None
            for qi in range(kj, nb):
                sl = slice(qi * t, (qi + 1) * t)
                qb = q_ref[sl, :]
                dob = do_ref[sl, :]
                st = _dot(kb, qb, "nt")
                pt = jnp.exp2(st - lse_ref[0, 0, :, sl])
                if qi == kj:
                    pt = jnp.where(causal_t, pt, 0.0)
                dpt = _dot(vb, dob, "nt")
                dst = (pt * (dpt - delta[qi])).astype(BF16)
                dvc = _dot(pt, dob)
                dkc = _dot(dst, qb) * (1.0 / LOG2E)
                dv = dvc if dv is None else dv + dvc
                dk = dkc if dk is None else dk + dkc
                dqc = _dot(dst, kb, "tn") * scale
                if kj > 0:
                    dqc = dqc + dqa_ref[sl, :]
                if qi == kj:
                    dq_ref[sl, :] = dqc.astype(dq_ref.dtype)
                else:
                    dqa_ref[sl, :] = dqc
            dk_ref[ks, :] = dk.astype(dk_ref.dtype)
            dv_ref[ks, :] = dv.astype(dv_ref.dtype)

    blk = pl.BlockSpec((S, SLOT), lambda b, h: (b, h))
    lse_spec = pl.BlockSpec((1, 1, 1, S), lambda b, h: (b, h, 0, 0))
    W = MLA_H * SLOT
    return _call_with_comm(
        body, (nseq, MLA_H), "attn_bwd", [blk, blk, blk, blk, lse_spec, blk], [q, k, v, o, lse, do], [blk, blk, blk],
        [jax.ShapeDtypeStruct((T, W), BF16)] * 3, comm, scratch=[pltpu.VMEM((S, SLOT), F32)])


def _rope_coeffs(cos, sin):
    half = ROPE // 2
    r = lax.broadcasted_iota(jnp.int32, (half, SLOT), 0)
    c = lax.broadcasted_iota(jnp.int32, (half, SLOT), 1)
    pc = ((c == r + NOPE) | (c == r + NOPE + half)).astype(F32)
    ps = (c == r + NOPE + half).astype(F32) - (c == r + NOPE).astype(F32)
    lane = lax.broadcasted_iota(jnp.int32, (1, SLOT), 1)
    return _dot_sel(cos, pc) + (lane < NOPE).astype(F32), _dot_sel(sin, ps)


def _rope_swap(x):
    W = x.shape[1]
    half = ROPE // 2
    lane = lax.broadcasted_iota(jnp.int32, (1, W), 1) & (SLOT - 1)
    up = pltpu.roll(x, W - half, axis=1)
    dn = pltpu.roll(x, half, axis=1)
    return jnp.where((lane >= NOPE) & (lane < NOPE + half), up, jnp.where((lane >= NOPE + half) & (lane < QK), dn, 0.0))


def rope_slot_fwd(q, kn, dtkr, cos, sin, name):
    def fn(qv, knv, krv, cv, sv):
        C, Sg = _rope_coeffs(cv, sv)
        ct, stl = jnp.tile(C, (1, MLA_H)), jnp.tile(Sg, (1, MLA_H))
        qo = (qv * ct + _rope_swap(qv) * stl) * Q_SCALE
        r = lax.broadcasted_iota(jnp.int32, (SLOT, SLOT), 0)
        c = lax.broadcasted_iota(jnp.int32, (SLOT, SLOT), 1)
        place = ((c == r + NOPE) & (r < ROPE)).astype(F32)
        kr = _dot_sel(krv, place)
        kr = kr * C + _rope_swap(kr) * Sg
        return qo, knv.astype(F32) + jnp.tile(kr, (1, MLA_H))
    W = MLA_H * SLOT
    return rowwise(fn, [q, kn, (dtkr, SLOT, 1), cos, sin], [], [(W, BF16), (W, BF16)], [], name)


def rope_slot_bwd(dq, dk, cos, sin, name):
    def fn(dqv, dkv, cv, sv):
        C, Sg = _rope_coeffs(cv, sv)
        ct, stl = jnp.tile(C, (1, MLA_H)), jnp.tile(Sg, (1, MLA_H))
        dqo = dqv * ct - _rope_swap(dqv) * stl
        tot = dkv[:, 0:SLOT]
        for h in range(1, MLA_H):
            tot = tot + dkv[:, h * SLOT:(h + 1) * SLOT]
        u = tot * C - _rope_swap(tot) * Sg
        r = lax.broadcasted_iota(jnp.int32, (SLOT, SLOT), 0)
        c = lax.broadcasted_iota(jnp.int32, (SLOT, SLOT), 1)
        unplace = ((r == c + NOPE) & (c < ROPE)).astype(F32)
        return dqo, dkv, _dot_sel(u, unplace, terms=2)
    W = MLA_H * SLOT
    return rowwise(fn, [dq, dk, cos, sin], [], [(W, BF16), (W, BF16), (SLOT, BF16)], [], name)


XA_BLK = 512


def xattn_fwd(q, k, v, nseq):
    T = q.shape[0]
    S = T // nseq
    M = k.shape[0] // nseq
    tq = min(XA_BLK, S)
    nq = S // tq
    scale = XA_D ** -0.5

    def body(q_ref, k_ref, v_ref, o_ref):
        s = _dot(q_ref[...], k_ref[...], "nt") * scale
        p = jnp.exp(s - jnp.max(s, axis=1, keepdims=True))
        p = p / jnp.sum(p, axis=1, keepdims=True)
        o_ref[...] = _dot(p, v_ref[...]).astype(o_ref.dtype)

    qs = pl.BlockSpec((tq, XA_D), lambda b, h, i: (b * nq + i, h))
    ks = pl.BlockSpec((M, XA_D), lambda b, h, i: (b, h))
    return pl.pallas_call(
        body, grid=(nseq, XA_H, nq), name="xattn_fwd", in_specs=[qs, ks, ks], out_specs=qs,
        out_shape=jax.ShapeDtypeStruct((T, XA_H * XA_D), BF16),
        compiler_params=_cp("parallel", "parallel", "parallel"),
    )(q, k, v)


def xattn_bwd(q, k, v, do, nseq):
    T = q.shape[0]
    S = T // nseq
    M = k.shape[0] // nseq
    tq = min(XA_BLK, S)
    nq = S // tq
    scale = XA_D ** -0.5

    def body(q_ref, k_ref, v_ref, do_ref, dq_ref, dk_ref, dv_ref):
        @pl.when(pl.program_id(2) == 0)
        def _():
            dk_ref[...] = jnp.zeros_like(dk_ref)
            dv_ref[...] = jnp.zeros_like(dv_ref)

        qb, kb, vb, dob = q_ref[...], k_ref[...], v_ref[...], do_ref[...]
        s = _dot(qb, kb, "nt") * scale
        p = jnp.exp(s - jnp.max(s, axis=1, keepdims=True))
        p = p / jnp.sum(p, axis=1, keepdims=True)
        dp = _dot(dob, vb, "nt")
        ds = p * (dp - jnp.sum(dp * p, axis=1, keepdims=True)) * scale
        dq_ref[...] = _dot(ds, kb).astype(dq_ref.dtype)
        dk_ref[...] += _dot(ds, qb, "tn")
        dv_ref[...] += _dot(p, dob, "tn")

    qs = pl.BlockSpec((tq, XA_D), lambda b, h, i: (b * nq + i, h))
    ks = pl.BlockSpec((M, XA_D), lambda b, h, i: (b, h))
    return pl.pallas_call(
        body, grid=(nseq, XA_H, nq), name="xattn_bwd", in_specs=[qs, ks, ks, qs], out_specs=[qs, ks, ks],
        out_shape=[jax.ShapeDtypeStruct((T, XA_H * XA_D), BF16), jax.ShapeDtypeStruct(k.shape, F32),
                   jax.ShapeDtypeStruct(k.shape, F32)],
        compiler_params=_cp("parallel", "parallel", "arbitrary"),
    )(q, k, v, do)


CONV_BLK = 256


def _shift_down(x, s, rows):
    if s == 0:
        return x
    return jnp.where(rows >= s, pltpu.roll(x, s, axis=0), 0.0)


def _shift_up(x, s, rows):
    if s == 0:
        return x
    S = x.shape[0]
    return jnp.where(rows < S - s, pltpu.roll(x, S - s, axis=0), 0.0)


def conv_fwd(x, w, b, nseq):
    T, CH = x.shape
    S = T // nseq

    def body(x_ref, w_ref, b_ref, o_ref):
        xv = x_ref[...].astype(F32)
        rows = lax.broadcasted_iota(jnp.int32, (S, 1), 0)
        c = jnp.zeros_like(xv) + b_ref[...]
        for kk in range(CONV_K):
            c = c + w_ref[kk:kk + 1, :] * _shift_down(xv, CONV_K - 1 - kk, rows)
        o_ref[...] = (c * _sigmoid(c)).astype(o_ref.dtype)

    xs = pl.BlockSpec((S, CONV_BLK), lambda j, bb: (bb, j))
    return pl.pallas_call(
        body, grid=(CH // CONV_BLK, nseq), name="conv_fwd",
        in_specs=[xs, pl.BlockSpec((CONV_K, CONV_BLK), lambda j, bb: (0, j)), pl.BlockSpec((1, CONV_BLK), lambda j, bb: (0, j))],
        out_specs=xs, out_shape=jax.ShapeDtypeStruct((T, CH), BF16),
        compiler_params=_cp("parallel", "parallel"),
    )(x, w, b)


def conv_bwd(x, w, b, dout, nseq):
    T, CH = x.shape
    S = T // nseq

    def body(x_ref, w_ref, b_ref, do_ref, dx_ref, dw_ref, db_ref):
        @pl.when(pl.program_id(1) == 0)
        def _():
            dw_ref[...] = jnp.zeros_like(dw_ref)
            db_ref[...] = jnp.zeros_like(db_ref)

        xv = x_ref[...].astype(F32)
        rows = lax.broadcasted_iota(jnp.int32, (S, 1), 0)
        c = jnp.zeros_like(xv) + b_ref[...]
        sh = [_shift_down(xv, CONV_K - 1 - kk, rows) for kk in range(CONV_K)]
        for kk in range(CONV_K):
            c = c + w_ref[kk:kk + 1, :] * sh[kk]
        sg = _sigmoid(c)
        dc = do_ref[...].astype(F32) * sg * (1.0 + c * (1.0 - sg))
        dx = jnp.zeros_like(xv)
        for kk in range(CONV_K):
            dx = dx + w_ref[kk:kk + 1, :] * _shift_up(dc, CONV_K - 1 - kk, rows)
            dw_ref[kk:kk + 1, :] += jnp.sum(dc * sh[kk], axis=0, keepdims=True)
        dx_ref[...] = dx.astype(dx_ref.dtype)
        db_ref[...] += jnp.sum(dc, axis=0, keepdims=True)

    xs = pl.BlockSpec((S, CONV_BLK), lambda j, bb: (bb, j))
    ws = pl.BlockSpec((CONV_K, CONV_BLK), lambda j, bb: (0, j))
    bs = pl.BlockSpec((1, CONV_BLK), lambda j, bb: (0, j))
    return pl.pallas_call(
        body, grid=(CH // CONV_BLK, nseq), name="conv_bwd",
        in_specs=[xs, ws, bs, xs], out_specs=[xs, ws, bs],
        out_shape=[jax.ShapeDtypeStruct((T, CH), BF16), jax.ShapeDtypeStruct((CONV_K, CH), F32),
                   jax.ShapeDtypeStruct((1, CH), F32)],
        compiler_params=_cp("parallel", "arbitrary"),
    )(x, w, b, dout)


def _dims(a, b, mode):
    M = a.shape[1] if mode[0] == "t" else a.shape[0]
    K = a.shape[0] if mode[0] == "t" else a.shape[1]
    N = b.shape[0] if mode[1] == "t" else b.shape[1]
    return M, K, N


def _tile(dim, prefs):
    for p in prefs:
        if dim % p == 0:
            return p
    return dim


def mm(groups, out_dtypes, name, tm=None, tn=None, tk=None, epi=None, extras=(), comm=None, sub=1, n_sum=0):
    a0, b0, m0 = groups[0][0]
    M, K0, N = _dims(a0, b0, m0)
    tm = tm or _tile(M, (1024, 512, 256, 128))
    tn = tn or _tile(N, (512, 256, 128))
    flat = [p for g in groups for p in g]
    nk = 1 if tk is None else K0 // tk
    in_specs, args = [], []
    for a, b, mode in flat:
        _, K, _ = _dims(a, b, mode)
        kb = K if tk is None else tk
        in_specs.append(pl.BlockSpec((kb, tm), lambda i, j, k: (k, i)) if mode[0] == "t"
                        else pl.BlockSpec((tm, kb), lambda i, j, k: (i, k)))
        in_specs.append(pl.BlockSpec((tn, kb), lambda i, j, k: (j, k)) if mode[1] == "t"
                        else pl.BlockSpec((kb, tn), lambda i, j, k: (k, j)))
        args += [a, b]
    for e in extras:
        in_specs.append(pl.BlockSpec((1, tn), lambda i, j, k: (0, j)) if e.shape[0] == 1 and M != 1
                        else pl.BlockSpec((tm, tn), lambda i, j, k: (i, j)))
        args.append(e)
    n_in = len(args)
    n_main = len(out_dtypes)
    n_out = n_main + n_sum
    assert n_sum == 0 or (tn == N and tk is None)
    ng = len(groups)
    sizes = [len(g) for g in groups]

    def body(*refs):
        ins, outs, accs = refs[:n_in], refs[n_in:n_in + n_out], refs[n_in + n_out:]
        kk = pl.program_id(2)

        def dots(rs):
            vals, pos = [], 0
            for gi in range(ng):
                acc = None
                for _ in range(sizes[gi]):
                    mode = flat[pos // 2][2]
                    av = ins[pos][:, rs] if mode[0] == "t" else ins[pos][rs, :]
                    d = _dot(av, ins[pos + 1][...], mode)
                    acc = d if acc is None else acc + d
                    pos += 2
                vals.append(acc)
            return vals

        def finish(accv, rs, first_chunk=True):
            ex = [(r[...] if r.shape[0] == 1 and tm != 1 else r[rs, :]).astype(F32) for r in ins[2 * len(flat):]]
            res = epi(accv, ex) if epi is not None else tuple(accv)
            for o, r in zip(outs[:n_main], res[:n_main]):
                o[rs, :] = r.astype(o.dtype)
            for o, r in zip(outs[n_main:], res[n_main:]):
                if first_chunk:
                    @pl.when(pl.program_id(0) == 0)
                    def _():
                        o[...] = r

                    @pl.when(pl.program_id(0) > 0)
                    def _():
                        o[...] += r
                else:
                    o[...] += r

        if nk == 1:
            for r in range(sub):
                rs = slice(r * (tm // sub), (r + 1) * (tm // sub))
                finish(dots(rs), rs, r == 0)
        else:
            vals = dots(slice(0, tm))
            finish = functools.partial(finish, rs=slice(0, tm))
            @pl.when(kk == 0)
            def _():
                for ar, vv in zip(accs, vals):
                    ar[...] = vv

            @pl.when(kk > 0)
            def _():
                for ar, vv in zip(accs, vals):
                    ar[...] += vv

            @pl.when(kk == nk - 1)
            def _():
                finish([ar[...] for ar in accs])

    grid = (M // tm, N // tn, nk)
    out_specs = [pl.BlockSpec((tm, tn), lambda i, j, k: (i, j)) for _ in out_dtypes] \
        + [pl.BlockSpec((1, tn), lambda i, j, k: (0, j))] * n_sum
    out_shape = [jax.ShapeDtypeStruct((M, N), dt) for dt in out_dtypes] + [jax.ShapeDtypeStruct((1, N), F32)] * n_sum
    scratch = [pltpu.VMEM((tm, tn), F32) for _ in range(ng if nk > 1 else 0)]
    sem = ("arbitrary" if n_sum else "parallel", "parallel", "arbitrary")
    if comm is not None:
        body = _attach(comm, body, n_in, n_out, *_grid_ends(grid))
        in_specs, args = in_specs + [HBM_SPEC] * len(comm.inputs), args + comm.inputs
        out_specs, out_shape = out_specs + [HBM_SPEC] * len(comm.out_shapes), out_shape + comm.out_shapes
        scratch, sem = scratch + comm.sems, ("arbitrary",) * 3
    return pl.pallas_call(body, grid=grid, name=name, in_specs=in_specs, out_specs=out_specs, out_shape=out_shape,
                          scratch_shapes=scratch, compiler_params=_cp(*sem))(*args)


def mm1(a, b, mode, out_dtype, name, **kw):
    return mm([[(a, b, mode)]], [out_dtype], name, **kw)[0]


ROW_BLK = 512


def rowwise(fn, rows, consts, outs, accs, name, tb=ROW_BLK):
    rows = [r if isinstance(r, tuple) else (r, r.shape[1], 0) for r in rows]
    T = rows[0][0].shape[0]
    tb = min(tb, T)
    n_r, n_c, n_o, n_a = len(rows), len(consts), len(outs), len(accs)

    def body(*refs):
        vals = [r[...].astype(F32) for r in refs[:n_r + n_c]]
        res = fn(*vals)
        o_refs = refs[n_r + n_c:n_r + n_c + n_o]
        a_refs = refs[n_r + n_c + n_o:]
        for o, r in zip(o_refs, res[:n_o]):
            o[...] = r.astype(o.dtype)
        if n_a:
            @pl.when(pl.program_id(0) == 0)
            def _():
                for ar in a_refs:
                    ar[...] = jnp.zeros_like(ar)
            for ar, r in zip(a_refs, res[n_o:]):
                ar[...] += r

    return pl.pallas_call(
        body, grid=(T // tb,), name=name,
        in_specs=[pl.BlockSpec((tb, w), functools.partial(lambda i, j: (i, j), j=j)) for _, w, j in rows]
        + [pl.BlockSpec(c.shape, lambda i: (0, 0)) for c in consts],
        out_specs=[pl.BlockSpec((tb, d), lambda i: (i, 0)) for d, _ in outs]
        + [pl.BlockSpec(s, lambda i: (0, 0)) for s in accs],
        out_shape=[jax.ShapeDtypeStruct((T, d), dt) for d, dt in outs]
        + [jax.ShapeDtypeStruct(s, F32) for s in accs],
        compiler_params=_cp("arbitrary" if n_a else "parallel"),
    )(*[r[0] for r in rows], *consts)


def _rms_stats(x):
    r = lax.rsqrt(jnp.mean(x * x, axis=-1, keepdims=True) + EPS)
    return r, x * r


def _rms_bwd(x, g, dy):
    r, xn = _rms_stats(x)
    dyg = dy * g
    dx = r * (dyg - xn * jnp.mean(dyg * xn, axis=-1, keepdims=True))
    return dx, jnp.sum(dy * xn, axis=0, keepdims=True)


def rms_fwd(x, g, name):
    return rowwise(lambda xv, gv: (_rms_stats(xv)[1] * gv,), [x], [g], [(x.shape[1], BF16)], [], name)[0]


def rms_bwd(x, g, dy, name, resid=None, dx_dtype=F32):
    def fn(*v):
        if resid is None:
            xv, dyv, gv = v
            dx, dg = _rms_bwd(xv, gv, dyv)
        else:
            xv, dyv, rv, gv = v
            dx, dg = _rms_bwd(xv, gv, dyv)
            dx = dx + rv
        return dx, dg
    rows = [x, dy] + ([] if resid is None else [resid])
    return rowwise(fn, rows, [g], [(x.shape[1], dx_dtype)], [(1, x.shape[1])], name)


def mm_rms_bwd(pairs, x, g, name, resid=None, dx_dtype=F32, comm=None):
    def epi(accs, ex):
        dx, dg = _rms_bwd(ex[0], ex[-1], accs[0])
        return (dx if resid is None else dx + ex[1]), dg
    extras = [x] + ([] if resid is None else [resid]) + [g]
    return mm([pairs], [dx_dtype], name, tm=min(256, x.shape[0]), tn=x.shape[1], epi=epi, extras=extras, comm=comm, n_sum=1)


def mm_resid(a, b, x, g, wgt, name, comm=None):
    epi = lambda accs, ex: (accs[0], ex[0] + wgt * _rms_stats(accs[0])[1] * ex[1])
    return mm([[(a, b, "nn")]], [F32, F32], name, tm=min(512, a.shape[0]), tn=b.shape[1], epi=epi, extras=[x, g], sub=2,
              comm=comm)


def resid_bwd(h, g, dy, wgt, name):
    def fn(hv, dyv, gv):
        dx, dg = _rms_bwd(hv, gv, dyv)
        return wgt * dx, wgt * dg
    return rowwise(fn, [h, dy], [g], [(h.shape[1], BF16)], [(1, h.shape[1])], name)


def _silu_parts(g):
    s = _sigmoid(g)
    return g * s, s * (1.0 + g * (1.0 - s))


def gated_norm_fwd(y, z, g, name):
    W = SSD_INNER // SSD_G

    def fn(yv, zv, gv):
        yg = yv * _silu_parts(zv)[0]
        return (jnp.concatenate([_rms_stats(yg[:, i * W:(i + 1) * W])[1] for i in range(SSD_G)], axis=1) * gv,)
    return rowwise(fn, [y, z], [g], [(SSD_INNER, BF16)], [], name)[0]


def gated_norm_bwd(y, z, dyn, g, name):
    W = SSD_INNER // SSD_G

    def fn(yv, zv, dv, gv):
        sil, dsil = _silu_parts(zv)
        yg = yv * sil
        parts = [_rms_bwd(yg[:, i * W:(i + 1) * W], gv[:, i * W:(i + 1) * W], dv[:, i * W:(i + 1) * W]) for i in range(SSD_G)]
        dyg = jnp.concatenate([p[0] for p in parts], axis=1)
        dg = jnp.concatenate([p[1] for p in parts], axis=1)
        return dyg * sil, dyg * yv * dsil, dg
    return rowwise(fn, [y, z, dyn], [g], [(SSD_INNER, BF16), (SSD_INNER, BF16)], [(1, SSD_INNER)], name)


def merge_fwd(gl, ys, ym, gb, name):
    def fn(glv, ysv, ymv, gbv):
        gt = _sigmoid(glv + gbv)
        return (gt[:, :D] * ysv + gt[:, D:] * ymv,)
    return rowwise(fn, [gl, ys, ym], [gb], [(D, BF16)], [], name)[0]


def merge_bwd(gl, ys, ym, dm, gb, name):
    def fn(glv, ysv, ymv, dmv, gbv):
        gt = _sigmoid(glv + gbv)
        gs, gm = gt[:, :D], gt[:, D:]
        dgl = jnp.concatenate([dmv * ysv * gs * (1.0 - gs), dmv * ymv * gm * (1.0 - gm)], axis=1)
        return dmv * gs, dmv * gm, dgl, jnp.sum(dgl, axis=0, keepdims=True)
    return rowwise(fn, [gl, ys, ym, dm], [gb], [(D, BF16), (D, BF16), (2 * D, BF16)], [(1, 2 * D)], name)


def loss_head(y, tgt, name):
    def fn(yv, tv):
        d = yv - tv
        part = 0.5 * jnp.sum(jnp.sum(d * d, axis=1, keepdims=True), axis=0, keepdims=True) / D
        return d / D, jnp.broadcast_to(part, (1, 128))
    return rowwise(fn, [y, tgt], [], [(D, F32)], [(1, 128)], name)


def _adamw_math(wv, gv, mv, vv):
    mn = B1 * mv + (1.0 - B1) * gv
    vn = B2 * vv + (1.0 - B2) * (gv * gv)
    mh = mn / (1.0 - B1 ** STEP)
    vh = vn / (1.0 - B2 ** STEP)
    return -LR * (mh / (jnp.sqrt(vh) + AEPS) + WD * wv), mn, vn


def adamw(w, g, m, v, name):
    R, C = w.shape
    tb = _tile(R, (256, 128, 64, 32, 16, 8))
    return rowwise(_adamw_math, [w, g, m, v], [], [(C, F32)] * 3, [], name, tb=tb)


def adamw_from_slots(recv, piece, w, m, v, name):
    K, n = w.shape
    ns = recv.shape[0]
    assert recv.shape[2] == n and recv.shape[1] % K == 0
    tb = _tile(K, (256, 176, 128, 64, 32, 16, 8)) if K % 8 == 0 else K
    r_spec = pl.BlockSpec((ns, tb, n), lambda i: (0, piece * (K // tb) + i, 0))
    w_spec = pl.BlockSpec((tb, n), lambda i: (i, 0))

    def body(r_ref, w_ref, m_ref, v_ref, g_ref, d_ref, mo_ref, vo_ref):
        g = r_ref[0].astype(F32)
        for s in range(1, ns):
            g = g + r_ref[s].astype(F32)
        g_ref[...] = g
        d_ref[...], mo_ref[...], vo_ref[...] = _adamw_math(w_ref[...], g, m_ref[...], v_ref[...])

    return pl.pallas_call(
        body, grid=(K // tb,), name=name, in_specs=[r_spec, w_spec, w_spec, w_spec], out_specs=[w_spec] * 4,
        out_shape=[jax.ShapeDtypeStruct((K, n), F32)] * 4, compiler_params=_cp("parallel"),
    )(recv, w, m, v)


def _me():
    return lax.axis_index("x"), lax.axis_index("y"), lax.axis_index("c")


def _dev_index():
    x, y, c = _me()
    return 4 * x + 2 * y + c


HBM_SPEC = pl.BlockSpec(memory_space=pl.ANY)


class GatherComm:
    def __init__(self, shards):
        self.inputs = list(shards)
        n = len(shards)
        self.out_shapes = [jax.ShapeDtypeStruct((N_DEV,) + s.shape, s.dtype) for s in shards]
        self.sems = [pltpu.SemaphoreType.DMA((7 * n,)), pltpu.SemaphoreType.DMA((7 * n,)), pltpu.SemaphoreType.DMA((n,))]

    def _plan(self, x_refs, out_refs, sems):
        send_sems, recv_sems, local_sems = sems
        n = len(x_refs)
        x, y, c = _me()
        me, sibling = (x, y, c), (x, y, 1 - c)
        chips = [(1 - x, y), (x, 1 - y), (1 - x, 1 - y)]

        def slot(i, px, py, pc):
            return out_refs[i].at[4 * px + 2 * py + pc]

        def copy(i, k, block, to, src=None):
            return pltpu.make_async_remote_copy(
                src_ref=slot(i, *block) if src is None else src, dst_ref=slot(i, *block),
                send_sem=send_sems.at[7 * i + k], recv_sem=recv_sems.at[7 * i + k], device_id=to, device_id_type=MESH)

        mine = [pltpu.make_async_copy(x_refs[i], slot(i, *me), local_sems.at[i]) for i in range(n)]
        first = []
        for i in range(n):
            first.append(copy(i, 0, me, sibling, src=x_refs[i]))
            first += [copy(i, 1 + j, me, (*chip, c), src=x_refs[i]) for j, chip in enumerate(chips)]
        passed = [[copy(i, 4 + j, (*chip, c), sibling) for j, chip in enumerate(chips)] for i in range(n)]
        from_ici = [[copy(i, 1 + j, (*chip, c), me) for j, chip in enumerate(chips)] for i in range(n)]
        from_sib = [[copy(i, 0, sibling, me)] + [copy(i, 4 + j, (*chip, 1 - c), me) for j, chip in enumerate(chips)] for i in range(n)]
        return mine, first, passed, from_ici, from_sib

    def start(self, x_refs, out_refs, sems):
        mine, first, _, _, _ = self._plan(x_refs, out_refs, sems)
        for cp in mine + first:
            cp.start()

    def finish(self, x_refs, out_refs, sems):
        mine, first, passed, from_ici, from_sib = self._plan(x_refs, out_refs, sems)
        for i in range(len(x_refs)):
            for arrival, forward in zip(from_ici[i], passed[i]):
                arrival.wait_recv()
                forward.start()
        for row in from_sib:
            for arrival in row:
                arrival.wait_recv()
        for cp in first + [cp for row in passed for cp in row]:
            cp.wait_send()
        for cp in mine:
            cp.wait()


def run_comm(comm, name):
    n_in, n_out = len(comm.inputs), len(comm.out_shapes)

    def body(*refs):
        ins, outs, sems = refs[:n_in], refs[n_in:n_in + n_out], refs[n_in + n_out:]
        comm.start(ins, outs, sems)
        comm.finish(ins, outs, sems)

    return pl.pallas_call(body, name=name, out_shape=comm.out_shapes, in_specs=[HBM_SPEC] * n_in,
                          out_specs=[HBM_SPEC] * n_out, scratch_shapes=comm.sems)(*comm.inputs)


def _attach(comm, body, n_in, n_out, first, last):
    if comm is None:
        return body
    ci, co, cs = len(comm.inputs), len(comm.out_shapes), len(comm.sems)

    def wrapped(*refs):
        h_in, c_in = refs[:n_in], refs[n_in:n_in + ci]
        h_out, c_out = refs[n_in + ci:n_in + ci + n_out], refs[n_in + ci + n_out:n_in + ci + n_out + co]
        rest = refs[n_in + ci + n_out + co:]
        h_scr, c_sem = rest[:len(rest) - cs], rest[len(rest) - cs:]

        @pl.when(first())
        def _():
            comm.start(c_in, c_out, c_sem)

        body(*h_in, *h_out, *h_scr)

        @pl.when(last())
        def _():
            comm.finish(c_in, c_out, c_sem)

    return wrapped


def _grid_ends(grid):
    first = lambda: functools.reduce(lambda a, b: a & b, [pl.program_id(i) == 0 for i in range(len(grid))])
    last = lambda: functools.reduce(lambda a, b: a & b, [pl.program_id(i) == g - 1 for i, g in enumerate(grid)])
    return first, last


def _call_with_comm(body, grid, name, in_specs, args, out_specs, out_shape, comm, scratch=(), sem=None):
    sem = sem or ("parallel",) * len(grid)
    scratch = list(scratch)
    if comm is not None:
        body = _attach(comm, body, len(args), len(out_shape), *_grid_ends(grid))
        in_specs, args = in_specs + [HBM_SPEC] * len(comm.inputs), args + comm.inputs
        out_specs, out_shape = out_specs + [HBM_SPEC] * len(comm.out_shapes), out_shape + comm.out_shapes
        scratch, sem = scratch + comm.sems, ("arbitrary",) * len(grid)
    return pl.pallas_call(body, grid=grid, name=name, in_specs=in_specs, out_specs=out_specs, out_shape=out_shape,
                          scratch_shapes=scratch, compiler_params=_cp(*sem))(*args)


class ScatterComm:
    def __init__(self, groups):
        self.sizes = [len(g) for g in groups]
        self.rows = [[pc.shape[1] for pc in g] for g in groups]
        ng = len(groups)
        self.inputs = [pc for g in groups for pc in g]
        self.out_shapes = [jax.ShapeDtypeStruct((N_DEV, sum(self.rows[gi]), g[0].shape[2]), g[0].dtype) for gi, g in enumerate(groups)]
        self.sems = [pltpu.SemaphoreType.DMA((7 * ng,)), pltpu.SemaphoreType.DMA((7 * ng,)), pltpu.SemaphoreType.DMA((ng,))]

    def _peers(self):
        x, y, c = _me()
        out = []
        for k in range(1, N_DEV):
            px = 1 - x if k & 4 else x
            py = 1 - y if k & 2 else y
            pc = 1 - c if k & 1 else c
            out.append((k, 4 * px + 2 * py + pc, dict(device_id=(px, py, pc), device_id_type=MESH)))
        return 4 * x + 2 * y + c, out

    def start(self, ins, outs, sems):
        send_sems, recv_sems, local_sems = sems
        me, peers = self._peers()
        pos = 0
        for gi, size in enumerate(self.sizes):
            for i, pc in enumerate(ins[pos:pos + size]):
                dst = outs[gi].at[me, pl.ds(sum(self.rows[gi][:i]), self.rows[gi][i])]
                pltpu.make_async_copy(pc.at[me], dst, local_sems.at[gi]).start()
                for k, peer, kw in peers:
                    pltpu.make_async_remote_copy(src_ref=pc.at[peer], dst_ref=dst, send_sem=send_sems.at[7 * gi + k - 1],
                                                 recv_sem=recv_sems.at[7 * gi + k - 1], **kw).start()
            pos += size

    def finish(self, ins, outs, sems):
        send_sems, recv_sems, local_sems = sems
        me, peers = self._peers()
        whole = [pltpu.make_async_remote_copy(src_ref=outs[gi].at[peer], dst_ref=outs[gi].at[peer],
                                              send_sem=send_sems.at[7 * gi + k - 1], recv_sem=recv_sems.at[7 * gi + k - 1], **kw)
                 for gi in range(len(self.sizes)) for k, peer, kw in peers]
        for cp in whole:
            cp.wait_recv()
        for cp in whole:
            cp.wait_send()
        for gi in range(len(self.sizes)):
            pltpu.make_async_copy(outs[gi].at[me], outs[gi].at[me], local_sems.at[gi]).wait()


def sum_slots(recv, name, tr):
    n, R, C = recv.shape

    def body(r_ref, o_ref):
        acc = r_ref[0].astype(F32)
        for s in range(1, n):
            acc = acc + r_ref[s].astype(F32)
        o_ref[...] = acc

    return pl.pallas_call(
        body, grid=(R // tr,), name=name,
        in_specs=[pl.BlockSpec((n, tr, C), lambda i: (0, i, 0))], out_specs=pl.BlockSpec((tr, C), lambda i: (i, 0)),
        out_shape=jax.ShapeDtypeStruct((R, C), F32), compiler_params=_cp("parallel"),
    )(recv)


PACK_W, FLAT_W = 1024, 128
MAIN = [
    ("ffn1_w_gate", "col"), ("ffn1_w_up", "col"), ("ffn1_w_down", "row"),
    ("ffn2_w_gate", "col"), ("ffn2_w_up", "col"), ("ffn2_w_down", "row"),
    ("w_ssd_proj", "row"), ("w_mla_proj", "row"), ("w_out", "row"),
    ("w_xq", "row"), ("w_xk", "row"), ("w_xv", "row"), ("w_xo", "row"),
    ("w_uk", "col"), ("w_uv", "col"),
]
FLAT = [("w_in", "col"), ("w_uq", "col")]
BIG = MAIN + FLAT
SMALL = ["ffn1_pre_g", "ffn1_post_g", "mix_pre_g", "conv_b", "dt_bias", "a_log", "d_skip", "ssd_norm_g", "q_norm_g",
         "kv_norm_g", "gate_bias", "mix_post_g", "xa_pre_g", "mem_norm_g", "xa_post_g", "ffn2_pre_g", "ffn2_post_g"]
WEIGHTS = ['ffn1_pre_g', 'ffn1_w_gate', 'ffn1_w_up', 'ffn1_w_down', 'ffn1_post_g', 'mix_pre_g', 'w_in', 'conv_w', 'conv_b',
           'dt_bias', 'a_log', 'd_skip', 'ssd_norm_g', 'w_ssd_proj', 'q_norm_g', 'w_uq', 'kv_norm_g', 'w_uk', 'w_uv',
           'w_mla_proj', 'gate_bias', 'w_out', 'mix_post_g', 'xa_pre_g', 'mem_norm_g', 'w_xq', 'w_xk', 'w_xv', 'w_xo',
           'xa_post_g', 'ffn2_pre_g', 'ffn2_w_gate', 'ffn2_w_up', 'ffn2_w_down', 'ffn2_post_g']


def _pack_rows(w, kind, width):
    m = w[0].T if kind == "col" else w[0]
    return m.reshape(-1, width)


KIND = dict(BIG)
GATHER_PLAN = {
    "first": (["ffn1_w_gate", "ffn1_w_up", "ffn1_w_down"], []),
    "ffn1_gate_up": (["w_uk", "w_uv"], ["w_in", "w_uq"]),
    "ffn1_down": (["w_ssd_proj", "w_mla_proj", "w_out"], []),
    "ssd_fwd": (["w_xq", "w_xk", "w_xv", "w_xo"], []),
    "attn_fwd": (["ffn2_w_gate", "ffn2_w_up", "ffn2_w_down"], []),
}
SCATTER_PLAN = {
    "attn_bwd": [["ffn2_w_gate", "ffn2_w_up", "ffn2_w_down"], ["w_xq", "w_xk", "w_xv", "w_xo"]],
    "ssd_bwd": [["w_ssd_proj", "w_mla_proj", "w_out"], ["w_uk", "w_uv"], ["w_uq"]],
    "in_bwd": [["w_in#0"]],
    "ffn1:down_bwd": [["w_in#1"]],
    "ffn1:dwg": [["ffn1_w_down"]],
    "ffn1:dwu": [["ffn1_w_gate"]],
    "ffn1:gate_up_bwd": [["ffn1_w_up"]],
}
PARTS = {"w_in#0": ("w_in", 0, 2656), "w_in#1": ("w_in", 2656, 5296)}


class Stage:
    def __init__(self, w):
        self.w = w
        self.width = {n: PACK_W if (n, k) in MAIN else FLAT_W for n, k in BIG}
        self.nrows = {n: math.prod(w[n].shape) // self.width[n] for n, _ in BIG}
        self.recv = {}

    def _shards(self, tag):
        names_main, names_flat = GATHER_PLAN[tag]
        pack = lambda n: _pack_rows(self.w[n], KIND[n], self.width[n]).astype(BF16)
        shards = []
        if names_main:
            shards.append(jnp.concatenate([pack(n) for n in names_main], axis=0))
        if names_flat:
            bits = lax.bitcast_convert_type(self.w["conv_w"][0], BF16).reshape(-1, FLAT_W)
            shards.append(_pad_rows(jnp.concatenate([pack(n) for n in names_flat] + [bits], axis=0), 16))
        return shards

    def gather(self, tag):
        return GatherComm(self._shards(tag))

    def gathered(self, tag, outs, W, p):
        names_main, names_flat = GATHER_PLAN[tag]
        outs = list(outs)
        for names in (names_main, names_flat):
            if not names:
                continue
            buf, r0 = outs.pop(0), 0
            for n in names:
                K = self.w[n].shape[1] if KIND[n] == "col" else PACK_W
                W[n] = buf[:, r0:r0 + self.nrows[n]].reshape(-1, K)
                r0 += self.nrows[n]
            if names is names_flat:
                cw = self.w["conv_w"]
                nbits = 2 * math.prod(cw.shape) // FLAT_W
                bits = buf[:, r0:r0 + nbits].reshape((N_DEV,) + cw.shape[1:] + (2,))
                p["conv_w"] = lax.bitcast_convert_type(bits, F32).transpose(1, 0, 2).reshape(cw.shape[1], -1)

    def scatter(self, tag, gw):
        if tag not in SCATTER_PLAN:
            return None
        def piece(n):
            if n in PARTS:
                base, r0, r1 = PARTS[n]
                return gw[base].reshape(N_DEV, self.nrows[base], self.width[base])[:, r0:r1]
            return gw[n].reshape(N_DEV, self.nrows[n], self.width[n])
        return ScatterComm([[piece(n) for n in names] for names in SCATTER_PLAN[tag]])

    def scattered(self, tag, outs):
        if tag in SCATTER_PLAN:
            self.recv[tag] = outs


def _pad_rows(a, mult):
    r = (-a.shape[0]) % mult
    return a if r == 0 else jnp.concatenate([a, jnp.zeros((r,) + a.shape[1:], a.dtype)], axis=0)


def _pack_small(vals, loss_row=None, conv_w=None):
    rows = []
    for v in vals:
        f = v.reshape(-1)
        f = jnp.concatenate([f, jnp.zeros(((-f.shape[0]) % 128,), F32)])
        rows.append(f.reshape(-1, 128))
    if conv_w is not None:
        rows.append(conv_w.reshape(-1, 128))
    if loss_row is not None:
        rows.append(loss_row)
    return _pad_rows(jnp.concatenate(rows, axis=0), 8)


def _unpack_small(buf, shapes):
    out, r = [], 0
    for shp in shapes:
        n = math.prod(shp)
        nr = -(-n // 128)
        out.append(buf[r:r + nr].reshape(-1)[:n].reshape(shp))
        r += nr
    return out, r


def _tn(a, b, name, out_dtype=BF16, comm=None):
    M, N = a.shape[1], b.shape[1]
    T = a.shape[0]
    tm = M if M <= 1536 else M // 2
    tk = 1024 if T % 1024 == 0 and T > 1024 else None
    res = mm([[(a, b, "tn")]], [out_dtype], name, tm=tm, tn=N, tk=tk, comm=comm)
    return res[0] if comm is None else (res[0], res[1:])


class NoStage:
    def gather(self, tag):
        return None

    def gathered(self, tag, outs, W, p):
        pass

    def scatter(self, tag, gw):
        return None

    def scattered(self, tag, outs):
        pass


def _ffn_fwd(x, gpre, gpost, wg_t, wu_t, wd, tag, comm=None, comm_down=None):
    h = rms_fwd(x, gpre, tag + "_pre")
    def swi(accs, ex):
        sil, dsil = _silu_parts(accs[0])
        return sil, accs[1] * dsil, sil * accs[1]
    res = mm([[(h, wg_t, "nt")], [(h, wu_t, "nt")]], [BF16, BF16, BF16], tag + "_gate_up", tn=DFF // 2, epi=swi, comm=comm,
             sub=4 if h.shape[0] % 1024 == 0 else 1)
    G, U, A = res[:3]
    H, y, *down_outs = mm_resid(A, wd, x, gpost, FFN_RES, tag + "_down", comm=comm_down)
    return y, (x, h, G, U, A, H), res[3:], down_outs


def _ffn_bwd(dy, saved, gpre, gpost, wg_t, wu_t, wd, tag, stage, gw):
    x, h, G, U, A, H = saved
    dH, dgpost = resid_bwd(H, gpost, dy, FFN_RES, tag + "_post_bwd")

    def dswi(accs, ex):
        return accs[0] * ex[1], accs[0] * ex[0]

    def hosted(where, call):
        comm = stage.scatter(tag + ":" + where, gw)
        res = call(comm)
        if comm is None:
            return res
        stage.scattered(tag + ":" + where, res[1])
        return res[0]

    res = hosted("down_bwd", lambda comm: (lambda r: r if comm is None else (r[:2], r[2:]))(
        mm([[(dH, wd, "nt")]], [BF16, BF16], tag + "_down_bwd", tn=DFF // 2, epi=dswi, extras=[G, U], comm=comm,
           sub=4 if dH.shape[0] % 1024 == 0 else 1)))
    dG, dU = res
    gw[tag + "_w_down"] = _tn(A, dH, tag + "_dwd")
    gw[tag + "_w_gate"] = hosted("dwg", lambda comm: _tn(dG, h, tag + "_dwg", comm=comm))
    gw[tag + "_w_up"] = hosted("dwu", lambda comm: _tn(dU, h, tag + "_dwu", comm=comm))
    dx, dgpre = hosted("gate_up_bwd", lambda comm: (lambda r: r[:2] if comm is None else (r[:2], r[2:]))(
        mm_rms_bwd([(dG, wg_t, "nn"), (dU, wu_t, "nn")], x, gpre, tag + "_gate_up_bwd", resid=dy, comm=comm)))
    return dx, dgpre, dgpost


def _rope_tables(positions):
    inv = ROPE_THETA ** (-jnp.arange(0, ROPE, 2, dtype=F32) / ROPE)
    ang = positions.astype(F32).reshape(-1)[:, None] * inv
    return jnp.cos(ang), jnp.sin(ang)


def _local_step(x, mem, positions, tgt, W, p, stage=None):
    stage = stage or NoStage()
    nseq = x.shape[0]
    T = nseq * x.shape[1]
    x0 = x.reshape(T, D)
    mem2 = mem.reshape(-1, D)
    cos, sin = _rope_tables(positions)

    x1, ffn1, arrived, arrived_down = _ffn_fwd(
        x0, p["ffn1_pre_g"], p["ffn1_post_g"], W["ffn1_w_gate"], W["ffn1_w_up"], W["ffn1_w_down"], "ffn1",
        comm=stage.gather("ffn1_gate_up"), comm_down=stage.gather("ffn1_down"))
    stage.gathered("ffn1_gate_up", arrived, W, p)
    stage.gathered("ffn1_down", arrived_down, W, p)

    w_in_t = W["w_in"]
    bounds = [0]
    for n in (SSD_INNER, CONV_CH, SSD_H, QR, KVR, ROPE, 2 * D):
        bounds.append(bounds[-1] + n)
    wt_z, wt_xbc, wt_dt, wt_q, wt_kv, wt_kr, wt_gate = [w_in_t[bounds[i]:bounds[i + 1]] for i in range(7)]
    wt_dt, wt_kr = _pad_rows(wt_dt, SLOT), _pad_rows(wt_kr, SLOT)
    wt_dtkr = jnp.concatenate([wt_dt, wt_kr], axis=0)
    hm = rms_fwd(x1, p["mix_pre_g"], "mix_pre")
    z = mm1(hm, wt_z, "nt", BF16, "in_z")
    xbc = mm1(hm, wt_xbc, "nt", BF16, "in_xbc")
    q_c = mm1(hm, wt_q, "nt", F32, "in_q", tn=QR)
    kv_c = mm1(hm, wt_kv, "nt", F32, "in_kv")
    dtkr = mm1(hm, wt_dtkr, "nt", F32, "in_dtkr")
    gl = mm1(hm, wt_gate, "nt", BF16, "in_gate")

    xbc_act = conv_fwd(xbc, p["conv_w"], p["conv_b"], nseq)
    y_ssd_core, prev, *arrived = ssd_fwd(xbc_act, dtkr, p["dt_bias"], p["a_log"], p["d_skip"], nseq, comm=stage.gather("ssd_fwd"))
    stage.gathered("ssd_fwd", arrived, W, p)
    yn = gated_norm_fwd(y_ssd_core, z, p["ssd_norm_g"], "ssd_norm")
    y_ssd = mm1(yn, W["w_ssd_proj"], "nn", BF16, "ssd_proj")

    slot_rows = lambda wt, per: jnp.pad(wt.reshape(MLA_H, per, -1), ((0, 0), (0, SLOT - per), (0, 0))).reshape(MLA_H * SLOT, -1)
    wq_s, wk_s, wv_s = slot_rows(W["w_uq"], QK), slot_rows(W["w_uk"], NOPE), slot_rows(W["w_uv"], VD)
    wo_s = slot_rows(W["w_mla_proj"], VD)
    qn = rms_fwd(q_c, p["q_norm_g"], "q_norm")
    q_s = mm1(qn, wq_s, "nt", BF16, "uq")
    kvn = rms_fwd(kv_c, p["kv_norm_g"], "kv_norm")
    kn_s = mm1(kvn, wk_s, "nt", BF16, "uk")
    v_s = mm1(kvn, wv_s, "nt", BF16, "uv")
    cos16, sin16 = cos, sin
    Qc, Kc = rope_slot_fwd(q_s, kn_s, dtkr, cos16, sin16, "rope")
    o_s, lse, *arrived = attn_slot_fwd(Qc, Kc, v_s, nseq, comm=stage.gather("attn_fwd"))
    stage.gathered("attn_fwd", arrived, W, p)
    y_mla = mm1(o_s, wo_s, "nn", BF16, "mla_proj")

    merged = merge_fwd(gl, y_ssd, y_mla, p["gate_bias"], "merge")
    hmix, x2 = mm_resid(merged, W["w_out"], x1, p["mix_post_g"], 1.0, "mix_out")

    hq = rms_fwd(x2, p["xa_pre_g"], "xa_pre")
    mn = rms_fwd(mem2, p["mem_norm_g"], "mem_norm")
    xq = mm1(hq, W["w_xq"], "nn", BF16, "xq")
    xk = mm1(mn, W["w_xk"], "nn", BF16, "xk")
    xv = mm1(mn, W["w_xv"], "nn", BF16, "xv")
    xo = xattn_fwd(xq, xk, xv, nseq)
    ho, x3 = mm_resid(xo, W["w_xo"], x2, p["xa_post_g"], 1.0, "xo")

    x4, ffn2, _, _ = _ffn_fwd(x3, p["ffn2_pre_g"], p["ffn2_post_g"], W["ffn2_w_gate"], W["ffn2_w_up"], W["ffn2_w_down"], "ffn2")
    dx4, loss_row = loss_head(x4, tgt.reshape(T, D), "loss")

    gw, gs = {}, {}
    dx3, gs["ffn2_pre_g"], gs["ffn2_post_g"] = _ffn_bwd(
        dx4, ffn2, p["ffn2_pre_g"], p["ffn2_post_g"], W["ffn2_w_gate"], W["ffn2_w_up"], W["ffn2_w_down"], "ffn2", stage, gw)

    dho, gs["xa_post_g"] = resid_bwd(ho, p["xa_post_g"], dx3, 1.0, "xa_post_bwd")
    dxo = mm1(dho, W["w_xo"], "nt", BF16, "xo_bwd")
    gw["w_xo"] = _tn(xo, dho, "d_w_xo")
    dxq, dxk, dxv = xattn_bwd(xq, xk, xv, dxo, nseq)
    dx2, gs["xa_pre_g"] = mm_rms_bwd([(dxq, W["w_xq"], "nt")], x2, p["xa_pre_g"], "xq_bwd", resid=dx3)
    gw["w_xq"] = _tn(hq, dxq, "d_w_xq")
    dmn = mm([[(dxk, W["w_xk"], "nt"), (dxv, W["w_xv"], "nt")]], [F32], "xkv_bwd")[0]
    gw["w_xk"] = _tn(mn, dxk, "d_w_xk")
    gw["w_xv"] = _tn(mn, dxv, "d_w_xv")
    _, gs["mem_norm_g"] = rms_bwd(mem2, p["mem_norm_g"], dmn, "mem_norm_bwd", dx_dtype=BF16)

    dhmix, gs["mix_post_g"] = resid_bwd(hmix, p["mix_post_g"], dx2, 1.0, "mix_post_bwd")
    dmerged = mm1(dhmix, W["w_out"], "nt", F32, "mix_out_bwd")
    gw["w_out"] = _tn(merged, dhmix, "d_w_out")
    dys, dym, dgl, gs["gate_bias"] = merge_bwd(gl, y_ssd, y_mla, dmerged, p["gate_bias"], "merge_bwd")

    unslot = lambda g, per: g.reshape(MLA_H, SLOT, -1)[:, :per].reshape(MLA_H * per, -1)
    do_s = mm1(dym, wo_s, "nt", BF16, "mla_proj_bwd")
    gw["w_mla_proj"] = unslot(_tn(o_s, dym, "d_w_mla_proj"), VD)
    dQc, dKc, dv_s, *sent = attn_slot_bwd(Qc, Kc, v_s, o_s, lse, do_s, nseq, comm=stage.scatter("attn_bwd", gw))
    stage.scattered("attn_bwd", sent)
    dq_s, dkn_s, dkr = rope_slot_bwd(dQc, dKc, cos16, sin16, "rope_bwd")
    dq_c, gs["q_norm_g"] = mm_rms_bwd([(dq_s, wq_s, "nn")], q_c, p["q_norm_g"], "uq_bwd", dx_dtype=BF16)
    gw["w_uq"] = unslot(_tn(dq_s, qn, "d_w_uq"), QK)
    dkv_c, gs["kv_norm_g"] = mm_rms_bwd([(dkn_s, wk_s, "nn"), (dv_s, wv_s, "nn")], kv_c, p["kv_norm_g"], "ukv_bwd", dx_dtype=BF16)
    gw["w_uk"] = unslot(_tn(dkn_s, kvn, "d_w_uk"), NOPE)
    gw["w_uv"] = unslot(_tn(dv_s, kvn, "d_w_uv"), VD)

    dyn = mm1(dys, W["w_ssd_proj"], "nt", F32, "ssd_proj_bwd")
    gw["w_ssd_proj"] = _tn(yn, dys, "d_w_ssd_proj")
    dyc, dz, gs["ssd_norm_g"] = gated_norm_bwd(y_ssd_core, z, dyn, p["ssd_norm_g"], "ssd_norm_bwd")
    dxbc_act, ddtr, gs["dt_bias"], gs["a_log"], gs["d_skip"], *sent = ssd_bwd(
        xbc_act, dtkr, p["dt_bias"], p["a_log"], p["d_skip"], prev, dyc, nseq, comm=stage.scatter("ssd_bwd", gw))
    stage.scattered("ssd_bwd", sent)
    dxbc, gs["conv_w"], gs["conv_b"] = conv_bwd(xbc, p["conv_w"], p["conv_b"], dxbc_act, nseq)

    gw["w_in"] = jnp.concatenate([_tn(dz, hm, "d_w_in_z"), _tn(dxbc, hm, "d_w_in_xbc"), _tn(ddtr, hm, "d_w_in_dt")[:SSD_H],
                                  _tn(dq_c, hm, "d_w_in_q"), _tn(dkv_c, hm, "d_w_in_kv"), _tn(dkr, hm, "d_w_in_kr")[:ROPE],
                                  _tn(dgl, hm, "d_w_in_gate")], axis=0)
    dx1, gs["mix_pre_g"], *sent = mm_rms_bwd(
        [(dz, wt_z, "nn"), (dxbc, wt_xbc, "nn"), (ddtr, wt_dt, "nn"), (dq_c, wt_q, "nn"), (dkv_c, wt_kv, "nn"),
         (dkr, wt_kr, "nn"), (dgl, wt_gate, "nn")], x1, p["mix_pre_g"], "in_bwd", resid=dx2, comm=stage.scatter("in_bwd", gw))
    stage.scattered("in_bwd", sent)

    dx0, gs["ffn1_pre_g"], gs["ffn1_post_g"] = _ffn_bwd(
        dx1, ffn1, p["ffn1_pre_g"], p["ffn1_post_g"], W["ffn1_w_gate"], W["ffn1_w_up"], W["ffn1_w_down"], "ffn1", stage, gw)
    return loss_row, dx0.reshape(x.shape), gw, gs


def kernel(x, mem, positions, ffn1_pre_g, ffn1_w_gate, ffn1_w_up, ffn1_w_down, ffn1_post_g, mix_pre_g, w_in, conv_w, conv_b, dt_bias, a_log, d_skip, ssd_norm_g, w_ssd_proj, q_norm_g, w_uq, kv_norm_g, w_uk, w_uv, w_mla_proj, gate_bias, w_out, mix_post_g, xa_pre_g, mem_norm_g, w_xq, w_xk, w_xv, w_xo, xa_post_g, ffn2_pre_g, ffn2_w_gate, ffn2_w_up, ffn2_w_down, ffn2_post_g, loss_target, m_ffn1_pre_g, m_ffn1_w_gate, m_ffn1_w_up, m_ffn1_w_down, m_ffn1_post_g, m_mix_pre_g, m_w_in, m_conv_w, m_conv_b, m_dt_bias, m_a_log, m_d_skip, m_ssd_norm_g, m_w_ssd_proj, m_q_norm_g, m_w_uq, m_kv_norm_g, m_w_uk, m_w_uv, m_w_mla_proj, m_gate_bias, m_w_out, m_mix_post_g, m_xa_pre_g, m_mem_norm_g, m_w_xq, m_w_xk, m_w_xv, m_w_xo, m_xa_post_g, m_ffn2_pre_g, m_ffn2_w_gate, m_ffn2_w_up, m_ffn2_w_down, m_ffn2_post_g, v_ffn1_pre_g, v_ffn1_w_gate, v_ffn1_w_up, v_ffn1_w_down, v_ffn1_post_g, v_mix_pre_g, v_w_in, v_conv_w, v_conv_b, v_dt_bias, v_a_log, v_d_skip, v_ssd_norm_g, v_w_ssd_proj, v_q_norm_g, v_w_uq, v_kv_norm_g, v_w_uk, v_w_uv, v_w_mla_proj, v_gate_bias, v_w_out, v_mix_post_g, v_xa_pre_g, v_mem_norm_g, v_w_xq, v_w_xk, v_w_xv, v_w_xo, v_xa_post_g, v_ffn2_pre_g, v_ffn2_w_gate, v_ffn2_w_up, v_ffn2_w_down, v_ffn2_post_g):
    a = dict(locals())
    w = {n: a[n] for n in WEIGHTS}
    m = {n: a["m_" + n] for n in WEIGHTS}
    v = {n: a["v_" + n] for n in WEIGHTS}

    stage = Stage(w)
    W, p = {}, {n: w[n] for n in SMALL}
    stage.gathered("first", run_comm(stage.gather("first"), "allgather_first"), W, p)

    loss_row, grad_x, gw, gs = _local_step(x, mem, positions, loss_target, W, p, stage)

    sm = _pack_small([gs[n] for n in SMALL], loss_row=loss_row, conv_w=gs["conv_w"])
    srecv, = run_comm(ScatterComm([[jnp.broadcast_to(sm[None], (N_DEV,) + sm.shape)]]), "exchange_small")
    s_rows = sum_slots(srecv, "sum_small", tr=sm.shape[0])
    grads, delta, new_m, new_v = {}, {}, {}, {}

    def finish(n, buf, piece):
        col = KIND[n] == "col"
        turn = (lambda t: t.T) if col else (lambda t: t)
        K = w[n].shape[1]
        if col and buf.shape[2] != K:
            buf = buf.reshape(buf.shape[0], -1, K)
        res = adamw_from_slots(buf, piece, turn(w[n][0]), turn(m[n][0]), turn(v[n][0]), "adamw_" + n)
        grads[n], delta[n], new_m[n], new_v[n] = [turn(r)[None] for r in res]

    parts = {}
    for tag, groups in SCATTER_PLAN.items():
        for names, buf in zip(groups, stage.recv[tag]):
            for piece, n in enumerate(names):
                if n in PARTS:
                    parts[n] = sum_slots(buf, "sum_" + n.replace("#", "_"), tr=buf.shape[1])
                else:
                    finish(n, buf, piece)
    for base in sorted({b for b, _, _ in PARTS.values()}):
        rows = jnp.concatenate([parts[pn] for pn in sorted(PARTS) if PARTS[pn][0] == base], axis=0)
        finish(base, rows[None], 0)
    conv_w_full = p["conv_w"]
    small_g, r1 = _unpack_small(s_rows, [w[n].shape for n in SMALL])
    for n, g in zip(SMALL, small_g):
        grads[n] = g
    ncw = math.prod(conv_w_full.shape) // 128
    cw_grad_full = s_rows[r1:r1 + ncw].reshape(conv_w_full.shape)
    wsh = conv_w.shape[2]
    grads["conv_w"] = lax.dynamic_slice_in_dim(cw_grad_full, _dev_index() * wsh, wsh, axis=1)[None]
    loss = s_rows[r1 + ncw, 0]

    d_, m_, v_ = adamw(conv_w[0], grads["conv_w"][0], m["conv_w"][0], v["conv_w"][0], "adamw_conv_w")
    delta["conv_w"], new_m["conv_w"], new_v["conv_w"] = d_[None], m_[None], v_[None]
    sp =[_pack_small([t[n] for n in SMALL]) for t in (w, grads, m, v)]
    outs = adamw(sp[0], sp[1], sp[2], sp[3], "adamw_small")
    for t, buf in zip((delta, new_m, new_v), outs):
        vals, _ = _unpack_small(buf, [w[n].shape for n in SMALL])
        for n, val in zip(SMALL, vals):
            t[n] = val
    return (loss, grad_x, *[grads[n] for n in WEIGHTS], *[delta[n] for n in WEIGHTS],
            *[new_m[n] for n in WEIGHTS], *[new_v[n] for n in WEIGHTS])
```

```python
import functools
import math

import jax
import jax.numpy as jnp
from jax import lax
from jax.experimental import pallas as pl
from jax.experimental.pallas import tpu as pltpu

F32, BF16 = jnp.float32, jnp.bfloat16
HI = lax.Precision.HIGHEST
MESH = pl.DeviceIdType.MESH
N_DEV = 8

D = 1024
DFF = 2816
SSD_H, SSD_P, SSD_G, SSD_N, SSD_L = 16, 64, 2, 128, 128
SSD_INNER = SSD_H * SSD_P
CONV_K, CONV_CH = 4, 1536
MLA_H, QR, KVR, NOPE, ROPE, VD = 16, 384, 256, 64, 32, 64
QK = NOPE + ROPE
ROPE_THETA = 10000.0
XA_H, XA_D = 4, 256
EPS = 1e-6
FFN_RES = 0.5
LR, B1, B2, AEPS, WD, STEP = 0.001, 0.9, 0.999, 1e-08, 0.01, 10

VMEM_LIMIT = 56 * 2**20


def _cp(*sem):
    return pltpu.CompilerParams(dimension_semantics=sem, vmem_limit_bytes=VMEM_LIMIT)


def _sigmoid(x):
    return 1.0 / (1.0 + jnp.exp(-x))


def _softplus(x):
    return jnp.where(x > 20.0, x, jnp.log(1.0 + jnp.exp(jnp.minimum(x, 20.0))))


def _dot(a, b, dims="nn"):
    ca = 0 if dims[0] == "t" else 1
    cb = 1 if dims[1] == "t" else 0
    return lax.dot_general(a.astype(BF16), b.astype(BF16), (((ca,), (cb,)), ((), ())), preferred_element_type=F32)


def _dot_sel(a, b, dims="nn", split="a", terms=3):
    r = (a if split == "a" else b).astype(F32)
    out = None
    for t in range(terms):
        piece = r.astype(BF16)
        if t + 1 < terms:
            r = r - piece.astype(F32)
        d = _dot(piece, b, dims) if split == "a" else _dot(a, piece, dims)
        out = d if out is None else out + d
    return out


def _ssd_common(dtr, dtb, alog):
    L = dtr.shape[0]
    dt = _softplus(dtr + dtb)
    a = -jnp.exp(alog)
    adt = dt * a
    r = lax.broadcasted_iota(jnp.int32, (L, L), 0)
    c = lax.broadcasted_iota(jnp.int32, (L, L), 1)
    lower = r >= c
    tri = lower.astype(F32)
    cs = _dot_sel(tri, adt, "nn", split="b")
    cs_t = _dot_sel(adt, tri, "tt")
    return dt, a, cs, cs_t, lower


def _head_expand():
    hh = lax.broadcasted_iota(jnp.int32, (SSD_H, SSD_INNER), 0)
    jj = lax.broadcasted_iota(jnp.int32, (SSD_H, SSD_INNER), 1)
    return ((jj >= hh * SSD_P) & (jj < hh * SSD_P + SSD_P)).astype(F32)


def _head_reduce():
    hh = lax.broadcasted_iota(jnp.int32, (SSD_INNER, SSD_H), 1)
    jj = lax.broadcasted_iota(jnp.int32, (SSD_INNER, SSD_H), 0)
    return ((jj >= hh * SSD_P) & (jj < hh * SSD_P + SSD_P)).astype(F32)


def ssd_fwd(xbc, dtr, dtb, alog, dsk, nseq, comm=None):
    T = xbc.shape[0]
    S = T // nseq
    C = S // SSD_L
    L = SSD_L
    NP = SSD_H // 2

    def body(x_ref, b_ref, c_ref, dtr_ref, dtb_ref, alog_ref, dsk_ref, y_ref, prev_ref, st_ref):
        ci = pl.program_id(1)

        @pl.when(ci == 0)
        def _():
            st_ref[...] = jnp.zeros_like(st_ref)

        dt, a, cs, cs_t, lower = _ssd_common(dtr_ref[:, 0:SSD_H], dtb_ref[...], alog_ref[...])
        E = _head_expand()
        X = x_ref[...].astype(F32)
        dt_e = _dot_sel(dt, E)
        cs_e = _dot_sel(cs, E)
        csl_e = cs_e[L - 1:L, :]
        Xd = X * dt_e
        Xf = Xd * jnp.exp(csl_e - cs_e)
        e_e = jnp.exp(cs_e)
        skip = _dot_sel(dsk_ref[...], E) * X
        lane = lax.broadcasted_iota(jnp.int32, (1, 2 * SSD_P), 1)
        rowp = lax.broadcasted_iota(jnp.int32, (2 * SSD_P, 1), 0)
        for g in range(SSD_G):
            Bg = b_ref[:, g * SSD_N:(g + 1) * SSD_N]
            Cg = c_ref[:, g * SSD_N:(g + 1) * SSD_N]
            cb = _dot(Cg, Bg, "nt")
            for pp in range(NP // SSD_G):
                p = g * (NP // SSD_G) + pp
                sl = slice(p * 2 * SSD_P, (p + 1) * 2 * SSD_P)
                Xd_p = Xd[:, sl]
                yd = jnp.zeros((L, 2 * SSD_P), F32)
                for q in range(2):
                    h = 2 * p + q
                    m = jnp.where(lower, jnp.exp(jnp.minimum(cs[:, h:h + 1] - cs_t[h:h + 1, :], 0.0)), 0.0)
                    mask = (lane >= q * SSD_P) & (lane < (q + 1) * SSD_P)
                    yd = yd + _dot(cb * m, jnp.where(mask, Xd_p, 0.0))
                S0 = st_ref[p]
                prev_ref[0, 0, p] = S0
                z = _dot(Cg, S0, "nt")
                y_ref[:, sl] = (skip[:, sl] + yd + z * e_e[:, sl]).astype(y_ref.dtype)
                h0 = 2 * p
                dec = jnp.where(rowp < SSD_P, jnp.exp(cs[L - 1:L, h0:h0 + 1]), jnp.exp(cs[L - 1:L, h0 + 1:h0 + 2]))
                st_ref[p] = S0 * dec + _dot(Xf[:, sl], Bg, "tn")

    row = lambda b, c: (b * C + c, 0)
    small = pl.BlockSpec((1, SSD_H), lambda b, c: (0, 0))
    return _call_with_comm(
        body, (nseq, C), "ssd_fwd",
        [pl.BlockSpec((L, SSD_INNER), row),
         pl.BlockSpec((L, SSD_G * SSD_N), lambda b, c: (b * C + c, SSD_INNER // (SSD_G * SSD_N))),
         pl.BlockSpec((L, SSD_G * SSD_N), lambda b, c: (b * C + c, SSD_INNER // (SSD_G * SSD_N) + 1)),
         pl.BlockSpec((L, 128), row), small, small, small],
        [xbc, xbc, xbc, dtr, dtb, alog, dsk],
        [pl.BlockSpec((L, SSD_INNER), row), pl.BlockSpec((1, 1, NP, 2 * SSD_P, SSD_N), lambda b, c: (b, c, 0, 0, 0))],
        [jax.ShapeDtypeStruct((T, SSD_INNER), BF16), jax.ShapeDtypeStruct((nseq, C, NP, 2 * SSD_P, SSD_N), F32)],
        comm, scratch=[pltpu.VMEM((NP, 2 * SSD_P, SSD_N), F32)], sem=("parallel", "arbitrary"))


def ssd_bwd(xbc, dtr, dtb, alog, dsk, prev, dy, nseq, comm=None):
    T = xbc.shape[0]
    S = T // nseq
    C = S // SSD_L
    L = SSD_L
    NP = SSD_H // 2

    def body(x_ref, b_ref, c_ref, dtr_ref, dtb_ref, alog_ref, dsk_ref, prev_ref, dy_ref,
             dxbc_ref, ddtr_ref, ddtb_ref, dalog_ref, ddsk_ref, ds_ref, stg_ref):
        bi = pl.program_id(0)
        ci = pl.program_id(1)

        @pl.when(ci == 0)
        def _():
            ds_ref[...] = jnp.zeros_like(ds_ref)

        @pl.when((ci == 0) & (bi == 0))
        def _():
            ddtb_ref[...] = jnp.zeros_like(ddtb_ref)
            dalog_ref[...] = jnp.zeros_like(dalog_ref)
            ddsk_ref[...] = jnp.zeros_like(ddsk_ref)

        dtr = dtr_ref[:, 0:SSD_H]
        dtb = dtb_ref[...]
        dt, a, cs, cs_t, lower = _ssd_common(dtr, dtb, alog_ref[...])
        upper = lax.broadcasted_iota(jnp.int32, (L, L), 1) >= lax.broadcasted_iota(jnp.int32, (L, L), 0)
        E = _head_expand()
        ET = _head_reduce()
        X = x_ref[...].astype(F32)
        dY = dy_ref[...].astype(F32)
        dt_e = _dot_sel(dt, E)
        cs_e = _dot_sel(cs, E)
        csl_e = cs_e[L - 1:L, :]
        f_e = jnp.exp(csl_e - cs_e)
        e_e = jnp.exp(cs_e)
        dsk_e = _dot_sel(dsk_ref[...], E)
        Xd = X * dt_e
        Xf = Xd * f_e
        lane = lax.broadcasted_iota(jnp.int32, (1, 2 * SSD_P), 1)
        rowp = lax.broadcasted_iota(jnp.int32, (2 * SSD_P, 1), 0)
        hsel = lax.broadcasted_iota(jnp.int32, (1, SSD_H), 1)
        dcs = jnp.zeros((L, SSD_H), F32)
        dcsl = jnp.zeros((1, SSD_H), F32)
        for g in range(SSD_G):
            Bg = b_ref[:, g * SSD_N:(g + 1) * SSD_N]
            Cg = c_ref[:, g * SSD_N:(g + 1) * SSD_N]
            cb = _dot(Cg, Bg, "nt")
            cbt = _dot(Bg, Cg, "nt")
            dB = jnp.zeros((L, SSD_N), F32)
            dC = jnp.zeros((L, SSD_N), F32)
            for pp in range(NP // SSD_G):
                p = g * (NP // SSD_G) + pp
                sl = slice(p * 2 * SSD_P, (p + 1) * 2 * SSD_P)
                Xd_p = Xd[:, sl]
                dY_p = dY[:, sl]
                dXd_p = jnp.zeros((L, 2 * SSD_P), F32)
                for q in range(2):
                    h = 2 * p + q
                    mask = (lane >= q * SSD_P) & (lane < (q + 1) * SSD_P)
                    col = cs[:, h:h + 1]
                    rw = cs_t[h:h + 1, :]
                    m = jnp.where(lower, jnp.exp(jnp.minimum(col - rw, 0.0)), 0.0)
                    mt = jnp.where(upper, jnp.exp(jnp.minimum(rw - col, 0.0)), 0.0)
                    dYm = jnp.where(mask, dY_p, 0.0)
                    dW = _dot(dYm, Xd_p, "nt")
                    dWt = _dot(Xd_p, dYm, "nt")
                    w = cb * m
                    wt = cbt * mt
                    dC = dC + _dot(dW * m, Bg)
                    dB = dB + _dot(dWt * mt, Cg)
                    dXd_p = dXd_p + jnp.where(mask, _dot(wt, dY_p), 0.0)
                    qcol = jnp.sum(dW * w, axis=1, keepdims=True) - jnp.sum(dWt * wt, axis=1, keepdims=True)
                    dcs = dcs + qcol * (hsel == h).astype(F32)
                S0 = prev_ref[0, 0, p]
                dSn = ds_ref[p]
                dZ = dY_p * e_e[:, sl]
                dC = dC + _dot(dZ, S0)
                h0 = 2 * p
                el0 = jnp.exp(cs[L - 1:L, h0:h0 + 1])
                el1 = jnp.exp(cs[L - 1:L, h0 + 1:h0 + 2])
                dec = jnp.where(rowp < SSD_P, el0, el1)
                ds_ref[p] = dSn * dec + _dot(dZ, Cg, "tn")
                dXf_p = _dot(Bg, dSn, "nt")
                dB = dB + _dot(Xf[:, sl], dSn)
                rs = jnp.sum(dSn * S0, axis=1, keepdims=True)
                s0 = jnp.sum(jnp.where(rowp < SSD_P, rs, 0.0), axis=0, keepdims=True) * el0
                s1 = jnp.sum(jnp.where(rowp >= SSD_P, rs, 0.0), axis=0, keepdims=True) * el1
                dcsl = dcsl + s0 * (hsel == h0).astype(F32) + s1 * (hsel == h0 + 1).astype(F32)
                y_off = _dot(Cg, S0, "nt") * e_e[:, sl]
                t1 = dY_p * y_off - dXf_p * Xf[:, sl]
                r1 = jnp.where(lane < SSD_P, t1, 0.0)
                c0 = jnp.sum(r1, axis=1, keepdims=True)
                c1 = jnp.sum(t1 - r1, axis=1, keepdims=True)
                dcs = dcs + c0 * (hsel == h0).astype(F32) + c1 * (hsel == h0 + 1).astype(F32)
                t2 = dXf_p * Xf[:, sl]
                r2 = jnp.where(lane < SSD_P, t2, 0.0)
                dcsl = dcsl + jnp.sum(r2, keepdims=True) * (hsel == h0).astype(F32) \
                    + jnp.sum(t2 - r2, keepdims=True) * (hsel == h0 + 1).astype(F32)
                stg_ref[:, sl] = dXd_p + dXf_p * f_e[:, sl]
            dxbc_ref[:, SSD_INNER + g * SSD_N:SSD_INNER + (g + 1) * SSD_N] = dB.astype(dxbc_ref.dtype)
            dxbc_ref[:, SSD_INNER + (SSD_G + g) * SSD_N:SSD_INNER + (SSD_G + g + 1) * SSD_N] = dC.astype(dxbc_ref.dtype)
        dXd = stg_ref[...]
        dxbc_ref[:, 0:SSD_INNER] = (dXd * dt_e + dsk_e * dY).astype(dxbc_ref.dtype)
        rowl = lax.broadcasted_iota(jnp.int32, (L, 1), 0)
        dcs = dcs + jnp.where(rowl == L - 1, dcsl, 0.0)
        dalpha = _dot_sel(upper.astype(F32), dcs, split="b")
        ddt = _dot_sel(dXd * X, ET, terms=2) + dalpha * a
        dalog_ref[...] += jnp.sum(dalpha * dt, axis=0, keepdims=True) * a
        ddtr = ddt * _sigmoid(dtr + dtb)
        spread = (lax.broadcasted_iota(jnp.int32, (SSD_H, 128), 0) == lax.broadcasted_iota(jnp.int32, (SSD_H, 128), 1)).astype(F32)
        ddtr_ref[...] = _dot(ddtr, spread).astype(ddtr_ref.dtype)
        ddtb_ref[...] += jnp.sum(ddtr, axis=0, keepdims=True)
        ddsk_ref[...] += jnp.sum(_dot_sel(dY * X, ET, terms=2), axis=0, keepdims=True)

    rowr = lambda b, c: (b * C + (C - 1 - c), 0)
    small = pl.BlockSpec((1, SSD_H), lambda b, c: (0, 0))
    return _call_with_comm(
        body, (nseq, C), "ssd_bwd",
        [pl.BlockSpec((L, SSD_INNER), rowr),
         pl.BlockSpec((L, SSD_G * SSD_N), lambda b, c: (b * C + (C - 1 - c), SSD_INNER // (SSD_G * SSD_N))),
         pl.BlockSpec((L, SSD_G * SSD_N), lambda b, c: (b * C + (C - 1 - c), SSD_INNER // (SSD_G * SSD_N) + 1)),
         pl.BlockSpec((L, 128), rowr), small, small, small,
         pl.BlockSpec((1, 1, NP, 2 * SSD_P, SSD_N), lambda b, c: (b, C - 1 - c, 0, 0, 0)),
         pl.BlockSpec((L, SSD_INNER), rowr)],
        [xbc, xbc, xbc, dtr, dtb, alog, dsk, prev, dy],
        [pl.BlockSpec((L, CONV_CH), rowr), pl.BlockSpec((L, 128), rowr), small, small, small],
        [jax.ShapeDtypeStruct((T, CONV_CH), BF16), jax.ShapeDtypeStruct((T, 128), BF16),
         jax.ShapeDtypeStruct((1, SSD_H), F32), jax.ShapeDtypeStruct((1, SSD_H), F32), jax.ShapeDtypeStruct((1, SSD_H), F32)],
        comm, scratch=[pltpu.VMEM((NP, 2 * SSD_P, SSD_N), F32), pltpu.VMEM((L, SSD_INNER), F32)], sem=("arbitrary", "arbitrary"))


SLOT = 128
ATT_T = 512
LOG2E = math.log2(math.e)
Q_SCALE = QK ** -0.5 * LOG2E


def _col_to_row(col):
    n = col.shape[0]
    eye = lax.broadcasted_iota(jnp.int32, (n, n), 0) == lax.broadcasted_iota(jnp.int32, (n, n), 1)
    return jnp.sum(jnp.where(eye, col, 0.0), axis=0, keepdims=True)


def attn_slot_fwd(q, k, v, nseq, comm=None):
    T = q.shape[0]
    S = T // nseq
    t = min(ATT_T, S)
    nb = S // t

    def body(q_ref, k_ref, v_ref, o_ref, lse_ref):
        causal = lax.broadcasted_iota(jnp.int32, (t, t), 1) <= lax.broadcasted_iota(jnp.int32, (t, t), 0)
        for qi in range(nb):
            qb = q_ref[qi * t:(qi + 1) * t, :]
            m = l = acc = None
            for kj in range(qi + 1):
                s = _dot(qb, k_ref[kj * t:(kj + 1) * t, :], "nt")
                if kj == qi:
                    s = jnp.where(causal, s, -1e30)
                bm = jnp.max(s, axis=1, keepdims=True)
                if kj == 0:
                    m = bm
                    p = jnp.exp2(s - m)
                    l = jnp.sum(p, axis=1, keepdims=True)
                    acc = _dot(p, v_ref[0:t, :])
                else:
                    m_new = jnp.maximum(m, bm)
                    corr = jnp.exp2(m - m_new)
                    p = jnp.exp2(s - m_new)
                    l = l * corr + jnp.sum(p, axis=1, keepdims=True)
                    acc = acc * corr + _dot(p, v_ref[kj * t:(kj + 1) * t, :])
                    m = m_new
            o_ref[qi * t:(qi + 1) * t, :] = (acc / l).astype(o_ref.dtype)
            lse_ref[0, 0, :, qi * t:(qi + 1) * t] = _col_to_row(m + jnp.log2(l))

    blk = pl.BlockSpec((S, SLOT), lambda b, h: (b, h))
    return _call_with_comm(
        body, (nseq, MLA_H), "attn_fwd", [blk, blk, blk], [q, k, v],
        [blk, pl.BlockSpec((1, 1, 1, S), lambda b, h: (b, h, 0, 0))],
        [jax.ShapeDtypeStruct((T, MLA_H * SLOT), BF16), jax.ShapeDtypeStruct((nseq, MLA_H, 1, S), F32)], comm)


def attn_slot_bwd(q, k, v, o, lse, do, nseq, comm=None):
    T = q.shape[0]
    S = T // nseq
    t = min(ATT_T, S)
    nb = S // t
    scale = QK ** -0.5

    def body(q_ref, k_ref, v_ref, o_ref, lse_ref, do_ref, dq_ref, dk_ref, dv_ref, dqa_ref):
        causal_t =lax.broadcasted_iota(jnp.int32, (t, t), 0) <= lax.broadcasted_iota(jnp.int32, (t, t), 1)
        ones = jnp.ones((8, SLOT), F32)
        delta = []
        for qi in range(nb):
            sl = slice(qi * t, (qi + 1) * t)
            prod = do_ref[sl, :].astype(F32) * o_ref[sl, :].astype(F32)
            delta.append(_dot_sel(ones, prod, "nt", split="b", terms=2)[0:1, :])
        for kj in range(nb):
            ks = slice(kj * t, (kj + 1) * t)
            kb = k_ref[ks, :]
            vb = v_ref[ks, :]
            dk = dv = None
            for qi in range(kj, nb):
                sl = slice(qi * t, (qi + 1) * t)
                qb = q_ref[sl, :]
                dob = do_ref[sl, :]
                st = _dot(kb, qb, "nt")
                pt = jnp.exp2(st - lse_ref[0, 0, :, sl])
                if qi == kj:
                    pt = jnp.where(causal_t, pt, 0.0)
                dpt = _dot(vb, dob, "nt")
                dst = (pt * (dpt - delta[qi])).astype(BF16)
                dvc = _dot(pt, dob)
                dkc = _dot(dst, qb) * (1.0 / LOG2E)
                dv = dvc if dv is None else dv + dvc
                dk = dkc if dk is None else dk + dkc
                dqc = _dot(dst, kb, "tn") * scale
                if kj > 0:
                    dqc = dqc + dqa_ref[sl, :]
                if qi == kj:
                    dq_ref[sl, :] = dqc.astype(dq_ref.dtype)
                else:
                    dqa_ref[sl, :] = dqc
            dk_ref[ks, :] = dk.astype(dk_ref.dtype)
            dv_ref[ks, :] = dv.astype(dv_ref.dtype)

    blk = pl.BlockSpec((S, SLOT), lambda b, h: (b, h))
    lse_spec = pl.BlockSpec((1, 1, 1, S), lambda b, h: (b, h, 0, 0))
    W = MLA_H * SLOT
    return _call_with_comm(
        body, (nseq, MLA_H), "attn_bwd", [blk, blk, blk, blk, lse_spec, blk], [q, k, v, o, lse, do], [blk, blk, blk],
        [jax.ShapeDtypeStruct((T, W), BF16)] * 3, comm, scratch=[pltpu.VMEM((S, SLOT), F32)])


def _rope_coeffs(cos, sin):
    half = ROPE // 2
    r = lax.broadcasted_iota(jnp.int32, (half, SLOT), 0)
    c = lax.broadcasted_iota(jnp.int32, (half, SLOT), 1)
    pc = ((c == r + NOPE) | (c == r + NOPE + half)).astype(F32)
    ps = (c == r + NOPE + half).astype(F32) - (c == r + NOPE).astype(F32)
    lane = lax.broadcasted_iota(jnp.int32, (1, SLOT), 1)
    return _dot_sel(cos, pc) + (lane < NOPE).astype(F32), _dot_sel(sin, ps)


def _rope_swap(x):
    W = x.shape[1]
    half = ROPE // 2
    lane = lax.broadcasted_iota(jnp.int32, (1, W), 1) & (SLOT - 1)
    up = pltpu.roll(x, W - half, axis=1)
    dn = pltpu.roll(x, half, axis=1)
    return jnp.where((lane >= NOPE) & (lane < NOPE + half), up, jnp.where((lane >= NOPE + half) & (lane < QK), dn, 0.0))


def rope_slot_fwd(q, kn, dtkr, cos, sin, name):
    def fn(qv, knv, krv, cv, sv):
        C, Sg = _rope_coeffs(cv, sv)
        ct, stl = jnp.tile(C, (1, MLA_H)), jnp.tile(Sg, (1, MLA_H))
        qo = (qv * ct + _rope_swap(qv) * stl) * Q_SCALE
        r = lax.broadcasted_iota(jnp.int32, (SLOT, SLOT), 0)
        c = lax.broadcasted_iota(jnp.int32, (SLOT, SLOT), 1)
        place = ((c == r + NOPE) & (r < ROPE)).astype(F32)
        kr = _dot_sel(krv, place)
        kr = kr * C + _rope_swap(kr) * Sg
        return qo, knv.astype(F32) + jnp.tile(kr, (1, MLA_H))
    W = MLA_H * SLOT
    return rowwise(fn, [q, kn, (dtkr, SLOT, 1), cos, sin], [], [(W, BF16), (W, BF16)], [], name)


def rope_slot_bwd(dq, dk, cos, sin, name):
    def fn(dqv, dkv, cv, sv):
        C, Sg = _rope_coeffs(cv, sv)
        ct, stl = jnp.tile(C, (1, MLA_H)), jnp.tile(Sg, (1, MLA_H))
        dqo = dqv * ct - _rope_swap(dqv) * stl
        tot = dkv[:, 0:SLOT]
        for h in range(1, MLA_H):
            tot = tot + dkv[:, h * SLOT:(h + 1) * SLOT]
        u = tot * C - _rope_swap(tot) * Sg
        r = lax.broadcasted_iota(jnp.int32, (SLOT, SLOT), 0)
        c = lax.broadcasted_iota(jnp.int32, (SLOT, SLOT), 1)
        unplace = ((r == c + NOPE) & (c < ROPE)).astype(F32)
        return dqo, dkv, _dot_sel(u, unplace, terms=2)
    W = MLA_H * SLOT
    return rowwise(fn, [dq, dk, cos, sin], [], [(W, BF16), (W, BF16), (SLOT, BF16)], [], name)


XA_BLK = 512


def xattn_fwd(q, k, v, nseq):
    T = q.shape[0]
    S = T // nseq
    M = k.shape[0] // nseq
    tq = min(XA_BLK, S)
    nq = S // tq
    scale = XA_D ** -0.5

    def body(q_ref, k_ref, v_ref, o_ref):
        s = _dot(q_ref[...], k_ref[...], "nt") * scale
        p = jnp.exp(s - jnp.max(s, axis=1, keepdims=True))
        p = p / jnp.sum(p, axis=1, keepdims=True)
        o_ref[...] = _dot(p, v_ref[...]).astype(o_ref.dtype)

    qs = pl.BlockSpec((tq, XA_D), lambda b, h, i: (b * nq + i, h))
    ks = pl.BlockSpec((M, XA_D), lambda b, h, i: (b, h))
    return pl.pallas_call(
        body, grid=(nseq, XA_H, nq), name="xattn_fwd", in_specs=[qs, ks, ks], out_specs=qs,
        out_shape=jax.ShapeDtypeStruct((T, XA_H * XA_D), BF16),
        compiler_params=_cp("parallel", "parallel", "parallel"),
    )(q, k, v)


def xattn_bwd(q, k, v, do, nseq):
    T = q.shape[0]
    S = T // nseq
    M = k.shape[0] // nseq
    tq = min(XA_BLK, S)
    nq = S // tq
    scale = XA_D ** -0.5

    def body(q_ref, k_ref, v_ref, do_ref, dq_ref, dk_ref, dv_ref):
        @pl.when(pl.program_id(2) == 0)
        def _():
            dk_ref[...] = jnp.zeros_like(dk_ref)
            dv_ref[...] = jnp.zeros_like(dv_ref)

        qb, kb, vb, dob = q_ref[...], k_ref[...], v_ref[...], do_ref[...]
        s = _dot(qb, kb, "nt") * scale
        p = jnp.exp(s - jnp.max(s, axis=1, keepdims=True))
        p = p / jnp.sum(p, axis=1, keepdims=True)
        dp = _dot(dob, vb, "nt")
        ds = p * (dp - jnp.sum(dp * p, axis=1, keepdims=True)) * scale
        dq_ref[...] = _dot(ds, kb).astype(dq_ref.dtype)
        dk_ref[...] += _dot(ds, qb, "tn")
        dv_ref[...] += _dot(p, dob, "tn")

    qs = pl.BlockSpec((tq, XA_D), lambda b, h, i: (b * nq + i, h))
    ks = pl.BlockSpec((M, XA_D), lambda b, h, i: (b, h))
    return pl.pallas_call(
        body, grid=(nseq, XA_H, nq), name="xattn_bwd", in_specs=[qs, ks, ks, qs], out_specs=[qs, ks, ks],
        out_shape=[jax.ShapeDtypeStruct((T, XA_H * XA_D), BF16), jax.ShapeDtypeStruct(k.shape, F32),
                   jax.ShapeDtypeStruct(k.shape, F32)],
        compiler_params=_cp("parallel", "parallel", "arbitrary"),
    )(q, k, v, do)


CONV_BLK = 256


def _shift_down(x, s, rows):
    if s == 0:
        return x
    return jnp.where(rows >= s, pltpu.roll(x, s, axis=0), 0.0)


def _shift_up(x, s, rows):
    if s == 0:
        return x
    S = x.shape[0]
    return jnp.where(rows < S - s, pltpu.roll(x, S - s, axis=0), 0.0)


def conv_fwd(x, w, b, nseq):
    T, CH = x.shape
    S = T // nseq

    def body(x_ref, w_ref, b_ref, o_ref):
        xv = x_ref[...].astype(F32)
        rows = lax.broadcasted_iota(jnp.int32, (S, 1), 0)
        c = jnp.zeros_like(xv) + b_ref[...]
        for kk in range(CONV_K):
            c = c + w_ref[kk:kk + 1, :] * _shift_down(xv, CONV_K - 1 - kk, rows)
        o_ref[...] = (c * _sigmoid(c)).astype(o_ref.dtype)

    xs = pl.BlockSpec((S, CONV_BLK), lambda j, bb: (bb, j))
    return pl.pallas_call(
        body, grid=(CH // CONV_BLK, nseq), name="conv_fwd",
        in_specs=[xs, pl.BlockSpec((CONV_K, CONV_BLK), lambda j, bb: (0, j)), pl.BlockSpec((1, CONV_BLK), lambda j, bb: (0, j))],
        out_specs=xs, out_shape=jax.ShapeDtypeStruct((T, CH), BF16),
        compiler_params=_cp("parallel", "parallel"),
    )(x, w, b)


def conv_bwd(x, w, b, dout, nseq):
    T, CH = x.shape
    S = T // nseq

    def body(x_ref, w_ref, b_ref, do_ref, dx_ref, dw_ref, db_ref):
        @pl.when(pl.program_id(1) == 0)
        def _():
            dw_ref[...] = jnp.zeros_like(dw_ref)
            db_ref[...] = jnp.zeros_like(db_ref)

        xv = x_ref[...].astype(F32)
        rows = lax.broadcasted_iota(jnp.int32, (S, 1), 0)
        c = jnp.zeros_like(xv) + b_ref[...]
        sh = [_shift_down(xv, CONV_K - 1 - kk, rows) for kk in range(CONV_K)]
        for kk in range(CONV_K):
            c = c + w_ref[kk:kk + 1, :] * sh[kk]
        sg = _sigmoid(c)
        dc = do_ref[...].astype(F32) * sg * (1.0 + c * (1.0 - sg))
        dx = jnp.zeros_like(xv)
        for kk in range(CONV_K):
            dx = dx + w_ref[kk:kk + 1, :] * _shift_up(dc, CONV_K - 1 - kk, rows)
            dw_ref[kk:kk + 1, :] += jnp.sum(dc * sh[kk], axis=0, keepdims=True)
        dx_ref[...] = dx.astype(dx_ref.dtype)
        db_ref[...] += jnp.sum(dc, axis=0, keepdims=True)

    xs = pl.BlockSpec((S, CONV_BLK), lambda j, bb: (bb, j))
    ws = pl.BlockSpec((CONV_K, CONV_BLK), lambda j, bb: (0, j))
    bs = pl.BlockSpec((1, CONV_BLK), lambda j, bb: (0, j))
    return pl.pallas_call(
        body, grid=(CH // CONV_BLK, nseq), name="conv_bwd",
        in_specs=[xs, ws, bs, xs], out_specs=[xs, ws, bs],
        out_shape=[jax.ShapeDtypeStruct((T, CH), BF16), jax.ShapeDtypeStruct((CONV_K, CH), F32),
                   jax.ShapeDtypeStruct((1, CH), F32)],
        compiler_params=_cp("parallel", "arbitrary"),
    )(x, w, b, dout)


def _dims(a, b, mode):
    M = a.shape[1] if mode[0] == "t" else a.shape[0]
    K = a.shape[0] if mode[0] == "t" else a.shape[1]
    N = b.shape[0] if mode[1] == "t" else b.shape[1]
    return M, K, N


def _tile(dim, prefs):
    for p in prefs:
        if dim % p == 0:
            return p
    return dim


def mm(groups, out_dtypes, name, tm=None, tn=None, tk=None, epi=None, extras=(), comm=None, sub=1, n_sum=0):
    a0, b0, m0 = groups[0][0]
    M, K0, N = _dims(a0, b0, m0)
    tm = tm or _tile(M, (1024, 512, 256, 128))
    tn = tn or _tile(N, (512, 256, 128))
    flat = [p for g in groups for p in g]
    nk = 1 if tk is None else K0 // tk
    in_specs, args = [], []
    for a, b, mode in flat:
        _, K, _ = _dims(a, b, mode)
        kb = K if tk is None else tk
        in_specs.append(pl.BlockSpec((kb, tm), lambda i, j, k: (k, i)) if mode[0] == "t"
                        else pl.BlockSpec((tm, kb), lambda i, j, k: (i, k)))
        in_specs.append(pl.BlockSpec((tn, kb), lambda i, j, k: (j, k)) if mode[1] == "t"
                        else pl.BlockSpec((kb, tn), lambda i, j, k: (k, j)))
        args += [a, b]
    for e in extras:
        in_specs.append(pl.BlockSpec((1, tn), lambda i, j, k: (0, j)) if e.shape[0] == 1 and M != 1
                        else pl.BlockSpec((tm, tn), lambda i, j, k: (i, j)))
        args.append(e)
    n_in = len(args)
    n_main = len(out_dtypes)
    n_out = n_main + n_sum
    assert n_sum == 0 or (tn == N and tk is None)
    ng = len(groups)
    sizes = [len(g) for g in groups]

    def body(*refs):
        ins, outs, accs = refs[:n_in], refs[n_in:n_in + n_out], refs[n_in + n_out:]
        kk = pl.program_id(2)

        def dots(rs):
            vals, pos = [], 0
            for gi in range(ng):
                acc = None
                for _ in range(sizes[gi]):
                    mode = flat[pos // 2][2]
                    av = ins[pos][:, rs] if mode[0] == "t" else ins[pos][rs, :]
                    d = _dot(av, ins[pos + 1][...], mode)
                    acc = d if acc is None else acc + d
                    pos += 2
                vals.append(acc)
            return vals

        def finish(accv, rs, first_chunk=True):
            ex = [(r[...] if r.shape[0] == 1 and tm != 1 else r[rs, :]).astype(F32) for r in ins[2 * len(flat):]]
            res = epi(accv, ex) if epi is not None else tuple(accv)
            for o, r in zip(outs[:n_main], res[:n_main]):
                o[rs, :] = r.astype(o.dtype)
            for o, r in zip(outs[n_main:], res[n_main:]):
                if first_chunk:
                    @pl.when(pl.program_id(0) == 0)
                    def _():
                        o[...] = r

                    @pl.when(pl.program_id(0) > 0)
                    def _():
                        o[...] += r
                else:
                    o[...] += r

        if nk == 1:
            for r in range(sub):
                rs = slice(r * (tm // sub), (r + 1) * (tm // sub))
                finish(dots(rs), rs, r == 0)
        else:
            vals = dots(slice(0, tm))
            finish = functools.partial(finish, rs=slice(0, tm))
            @pl.when(kk == 0)
            def _():
                for ar, vv in zip(accs, vals):
                    ar[...] = vv

            @pl.when(kk > 0)
            def _():
                for ar, vv in zip(accs, vals):
                    ar[...] += vv

            @pl.when(kk == nk - 1)
            def _():
                finish([ar[...] for ar in accs])

    grid = (M // tm, N // tn, nk)
    out_specs = [pl.BlockSpec((tm, tn), lambda i, j, k: (i, j)) for _ in out_dtypes] \
        + [pl.BlockSpec((1, tn), lambda i, j, k: (0, j))] * n_sum
    out_shape = [jax.ShapeDtypeStruct((M, N), dt) for dt in out_dtypes] + [jax.ShapeDtypeStruct((1, N), F32)] * n_sum
    scratch = [pltpu.VMEM((tm, tn), F32) for _ in range(ng if nk > 1 else 0)]
    sem = ("arbitrary" if n_sum else "parallel", "parallel", "arbitrary")
    if comm is not None:
        body = _attach(comm, body, n_in, n_out, *_grid_ends(grid))
        in_specs, args = in_specs + [HBM_SPEC] * len(comm.inputs), args + comm.inputs
        out_specs, out_shape = out_specs + [HBM_SPEC] * len(comm.out_shapes), out_shape + comm.out_shapes
        scratch, sem = scratch + comm.sems, ("arbitrary",) * 3
    return pl.pallas_call(body, grid=grid, name=name, in_specs=in_specs, out_specs=out_specs, out_shape=out_shape,
                          scratch_shapes=scratch, compiler_params=_cp(*sem))(*args)


def mm1(a, b, mode, out_dtype, name, **kw):
    return mm([[(a, b, mode)]], [out_dtype], name, **kw)[0]


ROW_BLK = 512


def rowwise(fn, rows, consts, outs, accs, name, tb=ROW_BLK):
    rows = [r if isinstance(r, tuple) else (r, r.shape[1], 0) for r in rows]
    T = rows[0][0].shape[0]
    tb = min(tb, T)
    n_r, n_c, n_o, n_a = len(rows), len(consts), len(outs), len(accs)

    def body(*refs):
        vals = [r[...].astype(F32) for r in refs[:n_r + n_c]]
        res = fn(*vals)
        o_refs = refs[n_r + n_c:n_r + n_c + n_o]
        a_refs = refs[n_r + n_c + n_o:]
        for o, r in zip(o_refs, res[:n_o]):
            o[...] = r.astype(o.dtype)
        if n_a:
            @pl.when(pl.program_id(0) == 0)
            def _():
                for ar in a_refs:
                    ar[...] = jnp.zeros_like(ar)
            for ar, r in zip(a_refs, res[n_o:]):
                ar[...] += r

    return pl.pallas_call(
        body, grid=(T // tb,), name=name,
        in_specs=[pl.BlockSpec((tb, w), functools.partial(lambda i, j: (i, j), j=j)) for _, w, j in rows]
        + [pl.BlockSpec(c.shape, lambda i: (0, 0)) for c in consts],
        out_specs=[pl.BlockSpec((tb, d), lambda i: (i, 0)) for d, _ in outs]
        + [pl.BlockSpec(s, lambda i: (0, 0)) for s in accs],
        out_shape=[jax.ShapeDtypeStruct((T, d), dt) for d, dt in outs]
        + [jax.ShapeDtypeStruct(s, F32) for s in accs],
        compiler_params=_cp("arbitrary" if n_a else "parallel"),
    )(*[r[0] for r in rows], *consts)


def _rms_stats(x):
    r = lax.rsqrt(jnp.mean(x * x, axis=-1, keepdims=True) + EPS)
    return r, x * r


def _rms_bwd(x, g, dy):
    r, xn = _rms_stats(x)
    dyg = dy * g
    dx = r * (dyg - xn * jnp.mean(dyg * xn, axis=-1, keepdims=True))
    return dx, jnp.sum(dy * xn, axis=0, keepdims=True)


def rms_fwd(x, g, name):
    return rowwise(lambda xv, gv: (_rms_stats(xv)[1] * gv,), [x], [g], [(x.shape[1], BF16)], [], name)[0]


def rms_bwd(x, g, dy, name, resid=None, dx_dtype=F32):
    def fn(*v):
        if resid is None:
            xv, dyv, gv = v
            dx, dg = _rms_bwd(xv, gv, dyv)
        else:
            xv, dyv, rv, gv = v
            dx, dg = _rms_bwd(xv, gv, dyv)
            dx = dx + rv
        return dx, dg
    rows = [x, dy] + ([] if resid is None else [resid])
    return rowwise(fn, rows, [g], [(x.shape[1], dx_dtype)], [(1, x.shape[1])], name)


def mm_rms_bwd(pairs, x, g, name, resid=None, dx_dtype=F32, comm=None):
    def epi(accs, ex):
        dx, dg = _rms_bwd(ex[0], ex[-1], accs[0])
        return (dx if resid is None else dx + ex[1]), dg
    extras = [x] + ([] if resid is None else [resid]) + [g]
    return mm([pairs], [dx_dtype], name, tm=min(256, x.shape[0]), tn=x.shape[1], epi=epi, extras=extras, comm=comm, n_sum=1)


def mm_resid(a, b, x, g, wgt, name, comm=None):
    epi = lambda accs, ex: (accs[0], ex[0] + wgt * _rms_stats(accs[0])[1] * ex[1])
    return mm([[(a, b, "nn")]], [F32, F32], name, tm=min(512, a.shape[0]), tn=b.shape[1], epi=epi, extras=[x, g], sub=2,
              comm=comm)


def resid_bwd(h, g, dy, wgt, name):
    def fn(hv, dyv, gv):
        dx, dg = _rms_bwd(hv, gv, dyv)
        return wgt * dx, wgt * dg
    return rowwise(fn, [h, dy], [g], [(h.shape[1], BF16)], [(1, h.shape[1])], name)


def _silu_parts(g):
    s = _sigmoid(g)
    return g * s, s * (1.0 + g * (1.0 - s))


def gated_norm_fwd(y, z, g, name):
    W = SSD_INNER // SSD_G

    def fn(yv, zv, gv):
        yg = yv * _silu_parts(zv)[0]
        return (jnp.concatenate([_rms_stats(yg[:, i * W:(i + 1) * W])[1] for i in range(SSD_G)], axis=1) * gv,)
    return rowwise(fn, [y, z], [g], [(SSD_INNER, BF16)], [], name)[0]


def gated_norm_bwd(y, z, dyn, g, name):
    W = SSD_INNER // SSD_G

    def fn(yv, zv, dv, gv):
        sil, dsil = _silu_parts(zv)
        yg = yv * sil
        parts = [_rms_bwd(yg[:, i * W:(i + 1) * W], gv[:, i * W:(i + 1) * W], dv[:, i * W:(i + 1) * W]) for i in range(SSD_G)]
        dyg = jnp.concatenate([p[0] for p in parts], axis=1)
        dg = jnp.concatenate([p[1] for p in parts], axis=1)
        return dyg * sil, dyg * yv * dsil, dg
    return rowwise(fn, [y, z, dyn], [g], [(SSD_INNER, BF16), (SSD_INNER, BF16)], [(1, SSD_INNER)], name)


def merge_fwd(gl, ys, ym, gb, name):
    def fn(glv, ysv, ymv, gbv):
        gt = _sigmoid(glv + gbv)
        return (gt[:, :D] * ysv + gt[:, D:] * ymv,)
    return rowwise(fn, [gl, ys, ym], [gb], [(D, BF16)], [], name)[0]


def merge_bwd(gl, ys, ym, dm, gb, name):
    def fn(glv, ysv, ymv, dmv, gbv):
        gt = _sigmoid(glv + gbv)
        gs, gm = gt[:, :D], gt[:, D:]
        dgl = jnp.concatenate([dmv * ysv * gs * (1.0 - gs), dmv * ymv * gm * (1.0 - gm)], axis=1)
        return dmv * gs, dmv * gm, dgl, jnp.sum(dgl, axis=0, keepdims=True)
    return rowwise(fn, [gl, ys, ym, dm], [gb], [(D, BF16), (D, BF16), (2 * D, BF16)], [(1, 2 * D)], name)


def loss_head(y, tgt, name):
    def fn(yv, tv):
        d = yv - tv
        part = 0.5 * jnp.sum(jnp.sum(d * d, axis=1, keepdims=True), axis=0, keepdims=True) / D
        return d / D, jnp.broadcast_to(part, (1, 128))
    return rowwise(fn, [y, tgt], [], [(D, F32)], [(1, 128)], name)


def _adamw_math(wv, gv, mv, vv):
    mn = B1 * mv + (1.0 - B1) * gv
    vn = B2 * vv + (1.0 - B2) * (gv * gv)
    mh = mn / (1.0 - B1 ** STEP)
    vh = vn / (1.0 - B2 ** STEP)
    return -LR * (mh / (jnp.sqrt(vh) + AEPS) + WD * wv), mn, vn


def adamw(w, g, m, v, name):
    R, C = w.shape
    tb = _tile(R, (256, 128, 64, 32, 16, 8))
    return rowwise(_adamw_math, [w, g, m, v], [], [(C, F32)] * 3, [], name, tb=tb)


def adamw_from_slots(recv, piece, w, m, v, name):
    K, n = w.shape
    ns = recv.shape[0]
    assert recv.shape[2] == n and recv.shape[1] % K == 0
    tb = _tile(K, (256, 176, 128, 64, 32, 16, 8)) if K % 8 == 0 else K
    r_spec = pl.BlockSpec((ns, tb, n), lambda i: (0, piece * (K // tb) + i, 0))
    w_spec = pl.BlockSpec((tb, n), lambda i: (i, 0))

    def body(r_ref, w_ref, m_ref, v_ref, g_ref, d_ref, mo_ref, vo_ref):
        g = r_ref[0].astype(F32)
        for s in range(1, ns):
            g = g + r_ref[s].astype(F32)
        g_ref[...] = g
        d_ref[...], mo_ref[...], vo_ref[...] = _adamw_math(w_ref[...], g, m_ref[...], v_ref[...])

    return pl.pallas_call(
        body, grid=(K // tb,), name=name, in_specs=[r_spec, w_spec, w_spec, w_spec], out_specs=[w_spec] * 4,
        out_shape=[jax.ShapeDtypeStruct((K, n), F32)] * 4, compiler_params=_cp("parallel"),
    )(recv, w, m, v)


def _me():
    return lax.axis_index("x"), lax.axis_index("y"), lax.axis_index("c")


def _dev_index():
    x, y, c = _me()
    return 4 * x + 2 * y + c


HBM_SPEC = pl.BlockSpec(memory_space=pl.ANY)


class GatherComm:
    def __init__(self, shards):
        self.inputs = list(shards)
        n = len(shards)
        self.out_shapes = [jax.ShapeDtypeStruct((N_DEV,) + s.shape, s.dtype) for s in shards]
        self.sems = [pltpu.SemaphoreType.DMA((7 * n,)), pltpu.SemaphoreType.DMA((7 * n,)), pltpu.SemaphoreType.DMA((n,))]

    def _plan(self, x_refs, out_refs, sems):
        send_sems, recv_sems, local_sems = sems
        n = len(x_refs)
        x, y, c = _me()
        me, sibling = (x, y, c), (x, y, 1 - c)
        chips = [(1 - x, y), (x, 1 - y), (1 - x, 1 - y)]

        def slot(i, px, py, pc):
            return out_refs[i].at[4 * px + 2 * py + pc]

        def copy(i, k, block, to, src=None):
            return pltpu.make_async_remote_copy(
                src_ref=slot(i, *block) if src is None else src, dst_ref=slot(i, *block),
                send_sem=send_sems.at[7 * i + k], recv_sem=recv_sems.at[7 * i + k], device_id=to, device_id_type=MESH)

        mine = [pltpu.make_async_copy(x_refs[i], slot(i, *me), local_sems.at[i]) for i in range(n)]
        first = []
        for i in range(n):
            first.append(copy(i, 0, me, sibling, src=x_refs[i]))
            first += [copy(i, 1 + j, me, (*chip, c), src=x_refs[i]) for j, chip in enumerate(chips)]
        passed = [[copy(i, 4 + j, (*chip, c), sibling) for j, chip in enumerate(chips)] for i in range(n)]
        from_ici = [[copy(i, 1 + j, (*chip, c), me) for j, chip in enumerate(chips)] for i in range(n)]
        from_sib = [[copy(i, 0, sibling, me)] + [copy(i, 4 + j, (*chip, 1 - c), me) for j, chip in enumerate(chips)] for i in range(n)]
        return mine, first, passed, from_ici, from_sib

    def start(self, x_refs, out_refs, sems):
        mine, first, _, _, _ = self._plan(x_refs, out_refs, sems)
        for cp in mine + first:
            cp.start()

    def finish(self, x_refs, out_refs, sems):
        mine, first, passed, from_ici, from_sib = self._plan(x_refs, out_refs, sems)
        for i in range(len(x_refs)):
            for arrival, forward in zip(from_ici[i], passed[i]):
                arrival.wait_recv()
                forward.start()
        for row in from_sib:
            for arrival in row:
                arrival.wait_recv()
        for cp in first + [cp for row in passed for cp in row]:
            cp.wait_send()
        for cp in mine:
            cp.wait()


def run_comm(comm, name):
    n_in, n_out = len(comm.inputs), len(comm.out_shapes)

    def body(*refs):
        ins, outs, sems = refs[:n_in], refs[n_in:n_in + n_out], refs[n_in + n_out:]
        comm.start(ins, outs, sems)
        comm.finish(ins, outs, sems)

    return pl.pallas_call(body, name=name, out_shape=comm.out_shapes, in_specs=[HBM_SPEC] * n_in,
                          out_specs=[HBM_SPEC] * n_out, scratch_shapes=comm.sems)(*comm.inputs)


def _attach(comm, body, n_in, n_out, first, last):
    if comm is None:
        return body
    ci, co, cs = len(comm.inputs), len(comm.out_shapes), len(comm.sems)

    def wrapped(*refs):
        h_in, c_in = refs[:n_in], refs[n_in:n_in + ci]
        h_out, c_out = refs[n_in + ci:n_in + ci + n_out], refs[n_in + ci + n_out:n_in + ci + n_out + co]
        rest = refs[n_in + ci + n_out + co:]
        h_scr, c_sem = rest[:len(rest) - cs], rest[len(rest) - cs:]

        @pl.when(first())
        def _():
            comm.start(c_in, c_out, c_sem)

        body(*h_in, *h_out, *h_scr)

        @pl.when(last())
        def _():
            comm.finish(c_in, c_out, c_sem)

    return wrapped


def _grid_ends(grid):
    first = lambda: functools.reduce(lambda a, b: a & b, [pl.program_id(i) == 0 for i in range(len(grid))])
    last = lambda: functools.reduce(lambda a, b: a & b, [pl.program_id(i) == g - 1 for i, g in enumerate(grid)])
    return first, last


def _call_with_comm(body, grid, name, in_specs, args, out_specs, out_shape, comm, scratch=(), sem=None):
    sem = sem or ("parallel",) * len(grid)
    scratch = list(scratch)
    if comm is not None:
        body = _attach(comm, body, len(args), len(out_shape), *_grid_ends(grid))
        in_specs, args = in_specs + [HBM_SPEC] * len(comm.inputs), args + comm.inputs
        out_specs, out_shape = out_specs + [HBM_SPEC] * len(comm.out_shapes), out_shape + comm.out_shapes
        scratch, sem = scratch + comm.sems, ("arbitrary",) * len(grid)
    return pl.pallas_call(body, grid=grid, name=name, in_specs=in_specs, out_specs=out_specs, out_shape=out_shape,
                          scratch_shapes=scratch, compiler_params=_cp(*sem))(*args)


class ScatterComm:
    def __init__(self, groups):
        self.sizes = [len(g) for g in groups]
        self.rows = [[pc.shape[1] for pc in g] for g in groups]
        ng = len(groups)
        self.inputs = [pc for g in groups for pc in g]
        self.out_shapes = [jax.ShapeDtypeStruct((N_DEV, sum(self.rows[gi]), g[0].shape[2]), g[0].dtype) for gi, g in enumerate(groups)]
        self.sems = [pltpu.SemaphoreType.DMA((7 * ng,)), pltpu.SemaphoreType.DMA((7 * ng,)), pltpu.SemaphoreType.DMA((ng,))]

    def _peers(self):
        x, y, c = _me()
        out = []
        for k in range(1, N_DEV):
            px = 1 - x if k & 4 else x
            py = 1 - y if k & 2 else y
            pc = 1 - c if k & 1 else c
            out.append((k, 4 * px + 2 * py + pc, dict(device_id=(px, py, pc), device_id_type=MESH)))
        return 4 * x + 2 * y + c, out

    def start(self, ins, outs, sems):
        send_sems, recv_sems, local_sems = sems
        me, peers = self._peers()
        pos = 0
        for gi, size in enumerate(self.sizes):
            for i, pc in enumerate(ins[pos:pos + size]):
                dst = outs[gi].at[me, pl.ds(sum(self.rows[gi][:i]), self.rows[gi][i])]
                pltpu.make_async_copy(pc.at[me], dst, local_sems.at[gi]).start()
                for k, peer, kw in peers:
                    pltpu.make_async_remote_copy(src_ref=pc.at[peer], dst_ref=dst, send_sem=send_sems.at[7 * gi + k - 1],
                                                 recv_sem=recv_sems.at[7 * gi + k - 1], **kw).start()
            pos += size

    def finish(self, ins, outs, sems):
        send_sems, recv_sems, local_sems = sems
        me, peers = self._peers()
        whole = [pltpu.make_async_remote_copy(src_ref=outs[gi].at[peer], dst_ref=outs[gi].at[peer],
                                              send_sem=send_sems.at[7 * gi + k - 1], recv_sem=recv_sems.at[7 * gi + k - 1], **kw)
                 for gi in range(len(self.sizes)) for k, peer, kw in peers]
        for cp in whole:
            cp.wait_recv()
        for cp in whole:
            cp.wait_send()
        for gi in range(len(self.sizes)):
            pltpu.make_async_copy(outs[gi].at[me], outs[gi].at[me], local_sems.at[gi]).wait()


def sum_slots(recv, name, tr):
    n, R, C = recv.shape

    def body(r_ref, o_ref):
        acc = r_ref[0].astype(F32)
        for s in range(1, n):
            acc = acc + r_ref[s].astype(F32)
        o_ref[...] = acc

    return pl.pallas_call(
        body, grid=(R // tr,), name=name,
        in_specs=[pl.BlockSpec((n, tr, C), lambda i: (0, i, 0))], out_specs=pl.BlockSpec((tr, C), lambda i: (i, 0)),
        out_shape=jax.ShapeDtypeStruct((R, C), F32), compiler_params=_cp("parallel"),
    )(recv)


PACK_W, FLAT_W = 1024, 128
MAIN = [
    ("ffn1_w_gate", "col"), ("ffn1_w_up", "col"), ("ffn1_w_down", "row"),
    ("ffn2_w_gate", "col"), ("ffn2_w_up", "col"), ("ffn2_w_down", "row"),
    ("w_ssd_proj", "row"), ("w_mla_proj", "row"), ("w_out", "row"),
    ("w_xq", "row"), ("w_xk", "row"), ("w_xv", "row"), ("w_xo", "row"),
    ("w_uk", "col"), ("w_uv", "col"),
]
FLAT = [("w_in", "col"), ("w_uq", "col")]
BIG = MAIN + FLAT
SMALL = ["ffn1_pre_g", "ffn1_post_g", "mix_pre_g", "conv_b", "dt_bias", "a_log", "d_skip", "ssd_norm_g", "q_norm_g",
         "kv_norm_g", "gate_bias", "mix_post_g", "xa_pre_g", "mem_norm_g", "xa_post_g", "ffn2_pre_g", "ffn2_post_g"]
WEIGHTS = ['ffn1_pre_g', 'ffn1_w_gate', 'ffn1_w_up', 'ffn1_w_down', 'ffn1_post_g', 'mix_pre_g', 'w_in', 'conv_w', 'conv_b',
           'dt_bias', 'a_log', 'd_skip', 'ssd_norm_g', 'w_ssd_proj', 'q_norm_g', 'w_uq', 'kv_norm_g', 'w_uk', 'w_uv',
           'w_mla_proj', 'gate_bias', 'w_out', 'mix_post_g', 'xa_pre_g', 'mem_norm_g', 'w_xq', 'w_xk', 'w_xv', 'w_xo',
           'xa_post_g', 'ffn2_pre_g', 'ffn2_w_gate', 'ffn2_w_up', 'ffn2_w_down', 'ffn2_post_g']


def _pack_rows(w, kind, width):
    m = w[0].T if kind == "col" else w[0]
    return m.reshape(-1, width)


KIND = dict(BIG)
GATHER_PLAN = {
    "first": (["ffn1_w_gate", "ffn1_w_up"], []),
    "ffn1_gate_up": (["ffn1_w_down"], ["w_in"]),
    "ffn1_down": (["w_ssd_proj", "w_mla_proj", "w_out", "w_uk", "w_uv"], ["w_uq"]),
    "ssd_fwd": (["w_xq", "w_xk", "w_xv", "w_xo"], []),
    "attn_fwd": (["ffn2_w_gate", "ffn2_w_up", "ffn2_w_down"], []),
}
SCATTER_PLAN = {
    "attn_bwd": [["ffn2_w_gate", "ffn2_w_up", "ffn2_w_down"], ["w_xq", "w_xk", "w_xv", "w_xo"]],
    "ssd_bwd": [["w_ssd_proj", "w_mla_proj", "w_out"], ["w_uk", "w_uv"], ["w_uq"]],
    "in_bwd": [["w_in#0"]],
    "ffn1:down_bwd": [["w_in#1"]],
    "ffn1:dwg": [["ffn1_w_down"]],
    "ffn1:dwu": [["ffn1_w_gate"]],
    "ffn1:gate_up_bwd": [["ffn1_w_up"]],
}
PARTS = {"w_in#0": ("w_in", 0, 2656), "w_in#1": ("w_in", 2656, 5296)}


class Stage:
    def __init__(self, w):
        self.w = w
        self.width = {n: PACK_W if (n, k) in MAIN else FLAT_W for n, k in BIG}
        self.nrows = {n: math.prod(w[n].shape) // self.width[n] for n, _ in BIG}
        self.recv = {}

    def _shards(self, tag):
        names_main, names_flat = GATHER_PLAN[tag]
        pack = lambda n: _pack_rows(self.w[n], KIND[n], self.width[n]).astype(BF16)
        shards = []
        if names_main:
            shards.append(jnp.concatenate([pack(n) for n in names_main], axis=0))
        if names_flat:
            pieces = [pack(n) for n in names_flat]
            if "w_in" in names_flat:
                pieces.append(lax.bitcast_convert_type(self.w["conv_w"][0], BF16).reshape(-1, FLAT_W))
            shards.append(_pad_rows(jnp.concatenate(pieces, axis=0), 16))
        return shards

    def gather(self, tag):
        return GatherComm(self._shards(tag)) if tag in GATHER_PLAN else None

    def gathered(self, tag, outs, W, p):
        if tag not in GATHER_PLAN:
            return
        names_main, names_flat = GATHER_PLAN[tag]
        outs = list(outs)
        for names in (names_main, names_flat):
            if not names:
                continue
            buf, r0 = outs.pop(0), 0
            for n in names:
                K = self.w[n].shape[1] if KIND[n] == "col" else PACK_W
                W[n] = buf[:, r0:r0 + self.nrows[n]].reshape(-1, K)
                r0 += self.nrows[n]
            if names is names_flat and "w_in" in names:
                cw = self.w["conv_w"]
                nbits = 2 * math.prod(cw.shape) // FLAT_W
                bits = buf[:, r0:r0 + nbits].reshape((N_DEV,) + cw.shape[1:] + (2,))
                p["conv_w"] = lax.bitcast_convert_type(bits, F32).transpose(1, 0, 2).reshape(cw.shape[1], -1)

    def scatter(self, tag, gw):
        if tag not in SCATTER_PLAN:
            return None
        def piece(n):
            if n in PARTS:
                base, r0, r1 = PARTS[n]
                return gw[base].reshape(N_DEV, self.nrows[base], self.width[base])[:, r0:r1]
            return gw[n].reshape(N_DEV, self.nrows[n], self.width[n])
        return ScatterComm([[piece(n) for n in names] for names in SCATTER_PLAN[tag]])

    def scattered(self, tag, outs):
        if tag in SCATTER_PLAN:
            self.recv[tag] = outs


def _pad_rows(a, mult):
    r = (-a.shape[0]) % mult
    return a if r == 0 else jnp.concatenate([a, jnp.zeros((r,) + a.shape[1:], a.dtype)], axis=0)


def _pack_small(vals, loss_row=None, conv_w=None):
    rows = []
    for v in vals:
        f = v.reshape(-1)
        f = jnp.concatenate([f, jnp.zeros(((-f.shape[0]) % 128,), F32)])
        rows.append(f.reshape(-1, 128))
    if conv_w is not None:
        rows.append(conv_w.reshape(-1, 128))
    if loss_row is not None:
        rows.append(loss_row)
    return _pad_rows(jnp.concatenate(rows, axis=0), 8)


def _unpack_small(buf, shapes):
    out, r = [], 0
    for shp in shapes:
        n = math.prod(shp)
        nr = -(-n // 128)
        out.append(buf[r:r + nr].reshape(-1)[:n].reshape(shp))
        r += nr
    return out, r


def _tn(a, b, name, out_dtype=BF16, comm=None):
    M, N = a.shape[1], b.shape[1]
    T = a.shape[0]
    tm = M if M <= 1536 else M // 2
    tk = 1024 if T % 1024 == 0 and T > 1024 else None
    res = mm([[(a, b, "tn")]], [out_dtype], name, tm=tm, tn=N, tk=tk, comm=comm)
    return res[0] if comm is None else (res[0], res[1:])


class NoStage:
    def gather(self, tag):
        return None

    def gathered(self, tag, outs, W, p):
        pass

    def scatter(self, tag, gw):
        return None

    def scattered(self, tag, outs):
        pass


def _ffn_fwd(x, gpre, gpost, W, p, tag, stage):
    h = rms_fwd(x, gpre, tag + "_pre")

    def swi(accs, ex):
        sil, dsil = _silu_parts(accs[0])
        return sil, accs[1] * dsil, sil * accs[1]
    G, U, A, *arrived = mm([[(h, W[tag + "_w_gate"], "nt")], [(h, W[tag + "_w_up"], "nt")]], [BF16, BF16, BF16], tag + "_gate_up",
                           tn=DFF // 2, epi=swi, comm=stage.gather(tag + "_gate_up"), sub=4 if h.shape[0] % 1024 == 0 else 1)
    stage.gathered(tag + "_gate_up", arrived, W, p)
    H, y, *arrived = mm_resid(A, W[tag + "_w_down"], x, gpost, FFN_RES, tag + "_down", comm=stage.gather(tag + "_down"))
    stage.gathered(tag + "_down", arrived, W, p)
    return y, (x, h, G, U, A, H)


def _ffn_bwd(dy, saved, gpre, gpost, wg_t, wu_t, wd, tag, stage, gw):
    x, h, G, U, A, H = saved
    dH, dgpost = resid_bwd(H, gpost, dy, FFN_RES, tag + "_post_bwd")

    def dswi(accs, ex):
        return accs[0] * ex[1], accs[0] * ex[0]

    def hosted(where, call):
        comm = stage.scatter(tag + ":" + where, gw)
        res = call(comm)
        if comm is None:
            return res
        stage.scattered(tag + ":" + where, res[1])
        return res[0]

    res = hosted("down_bwd", lambda comm: (lambda r: r if comm is None else (r[:2], r[2:]))(
        mm([[(dH, wd, "nt")]], [BF16, BF16], tag + "_down_bwd", tn=DFF // 2, epi=dswi, extras=[G, U], comm=comm,
           sub=4 if dH.shape[0] % 1024 == 0 else 1)))
    dG, dU = res
    gw[tag + "_w_down"] = _tn(A, dH, tag + "_dwd")
    gw[tag + "_w_gate"] = hosted("dwg", lambda comm: _tn(dG, h, tag + "_dwg", comm=comm))
    gw[tag + "_w_up"] = hosted("dwu", lambda comm: _tn(dU, h, tag + "_dwu", comm=comm))
    dx, dgpre = hosted("gate_up_bwd", lambda comm: (lambda r: r[:2] if comm is None else (r[:2], r[2:]))(
        mm_rms_bwd([(dG, wg_t, "nn"), (dU, wu_t, "nn")], x, gpre, tag + "_gate_up_bwd", resid=dy, comm=comm)))
    return dx, dgpre, dgpost


def _rope_tables(positions):
    inv = ROPE_THETA ** (-jnp.arange(0, ROPE, 2, dtype=F32) / ROPE)
    ang = positions.astype(F32).reshape(-1)[:, None] * inv
    return jnp.cos(ang), jnp.sin(ang)


def _local_step(x, mem, positions, tgt, W, p, stage=None):
    stage = stage or NoStage()
    nseq = x.shape[0]
    T = nseq * x.shape[1]
    x0 = x.reshape(T, D)
    mem2 = mem.reshape(-1, D)
    cos, sin = _rope_tables(positions)

    x1, ffn1 = _ffn_fwd(x0, p["ffn1_pre_g"], p["ffn1_post_g"], W, p, "ffn1", stage)

    w_in_t = W["w_in"]
    bounds = [0]
    for n in (SSD_INNER, CONV_CH, SSD_H, QR, KVR, ROPE, 2 * D):
        bounds.append(bounds[-1] + n)
    wt_z, wt_xbc, wt_dt, wt_q, wt_kv, wt_kr, wt_gate = [w_in_t[bounds[i]:bounds[i + 1]] for i in range(7)]
    wt_dt, wt_kr = _pad_rows(wt_dt, SLOT), _pad_rows(wt_kr, SLOT)
    wt_dtkr = jnp.concatenate([wt_dt, wt_kr], axis=0)
    hm = rms_fwd(x1, p["mix_pre_g"], "mix_pre")
    z = mm1(hm, wt_z, "nt", BF16, "in_z")
    xbc = mm1(hm, wt_xbc, "nt", BF16, "in_xbc")
    q_c = mm1(hm, wt_q, "nt", F32, "in_q", tn=QR)
    kv_c = mm1(hm, wt_kv, "nt", F32, "in_kv")
    dtkr = mm1(hm, wt_dtkr, "nt", F32, "in_dtkr")
    gl = mm1(hm, wt_gate, "nt", BF16, "in_gate")

    xbc_act = conv_fwd(xbc, p["conv_w"], p["conv_b"], nseq)
    y_ssd_core, prev, *arrived = ssd_fwd(xbc_act, dtkr, p["dt_bias"], p["a_log"], p["d_skip"], nseq, comm=stage.gather("ssd_fwd"))
    stage.gathered("ssd_fwd", arrived, W, p)
    yn = gated_norm_fwd(y_ssd_core, z, p["ssd_norm_g"], "ssd_norm")
    y_ssd = mm1(yn, W["w_ssd_proj"], "nn", BF16, "ssd_proj")

    slot_rows = lambda wt, per: jnp.pad(wt.reshape(MLA_H, per, -1), ((0, 0), (0, SLOT - per), (0, 0))).reshape(MLA_H * SLOT, -1)
    wq_s, wk_s, wv_s = slot_rows(W["w_uq"], QK), slot_rows(W["w_uk"], NOPE), slot_rows(W["w_uv"], VD)
    wo_s = slot_rows(W["w_mla_proj"], VD)
    qn = rms_fwd(q_c, p["q_norm_g"], "q_norm")
    q_s = mm1(qn, wq_s, "nt", BF16, "uq")
    kvn = rms_fwd(kv_c, p["kv_norm_g"], "kv_norm")
    kn_s = mm1(kvn, wk_s, "nt", BF16, "uk")
    v_s = mm1(kvn, wv_s, "nt", BF16, "uv")
    cos16, sin16 = cos, sin
    Qc, Kc = rope_slot_fwd(q_s, kn_s, dtkr, cos16, sin16, "rope")
    o_s, lse, *arrived = attn_slot_fwd(Qc, Kc, v_s, nseq, comm=stage.gather("attn_fwd"))
    stage.gathered("attn_fwd", arrived, W, p)
    y_mla = mm1(o_s, wo_s, "nn", BF16, "mla_proj")

    merged = merge_fwd(gl, y_ssd, y_mla, p["gate_bias"], "merge")
    hmix, x2 = mm_resid(merged, W["w_out"], x1, p["mix_post_g"], 1.0, "mix_out")

    hq = rms_fwd(x2, p["xa_pre_g"], "xa_pre")
    mn = rms_fwd(mem2, p["mem_norm_g"], "mem_norm")
    xq = mm1(hq, W["w_xq"], "nn", BF16, "xq")
    xk = mm1(mn, W["w_xk"], "nn", BF16, "xk")
    xv = mm1(mn, W["w_xv"], "nn", BF16, "xv")
    xo = xattn_fwd(xq, xk, xv, nseq)
    ho, x3 = mm_resid(xo, W["w_xo"], x2, p["xa_post_g"], 1.0, "xo")

    x4, ffn2 = _ffn_fwd(x3, p["ffn2_pre_g"], p["ffn2_post_g"], W, p, "ffn2", stage)
    dx4, loss_row = loss_head(x4, tgt.reshape(T, D), "loss")

    gw, gs = {}, {}
    dx3, gs["ffn2_pre_g"], gs["ffn2_post_g"] = _ffn_bwd(
        dx4, ffn2, p["ffn2_pre_g"], p["ffn2_post_g"], W["ffn2_w_gate"], W["ffn2_w_up"], W["ffn2_w_down"], "ffn2", stage, gw)

    dho, gs["xa_post_g"] = resid_bwd(ho, p["xa_post_g"], dx3, 1.0, "xa_post_bwd")
    dxo = mm1(dho, W["w_xo"], "nt", BF16, "xo_bwd")
    gw["w_xo"] = _tn(xo, dho, "d_w_xo")
    dxq, dxk, dxv = xattn_bwd(xq, xk, xv, dxo, nseq)
    dx2, gs["xa_pre_g"] = mm_rms_bwd([(dxq, W["w_xq"], "nt")], x2, p["xa_pre_g"], "xq_bwd", resid=dx3)
    gw["w_xq"] = _tn(hq, dxq, "d_w_xq")
    dmn = mm([[(dxk, W["w_xk"], "nt"), (dxv, W["w_xv"], "nt")]], [F32], "xkv_bwd")[0]
    gw["w_xk"] = _tn(mn, dxk, "d_w_xk")
    gw["w_xv"] = _tn(mn, dxv, "d_w_xv")
    _, gs["mem_norm_g"] = rms_bwd(mem2, p["mem_norm_g"], dmn, "mem_norm_bwd", dx_dtype=BF16)

    dhmix, gs["mix_post_g"] = resid_bwd(hmix, p["mix_post_g"], dx2, 1.0, "mix_post_bwd")
    dmerged = mm1(dhmix, W["w_out"], "nt", F32, "mix_out_bwd")
    gw["w_out"] = _tn(merged, dhmix, "d_w_out")
    dys, dym, dgl, gs["gate_bias"] = merge_bwd(gl, y_ssd, y_mla, dmerged, p["gate_bias"], "merge_bwd")

    unslot = lambda g, per: g.reshape(MLA_H, SLOT, -1)[:, :per].reshape(MLA_H * per, -1)
    do_s = mm1(dym, wo_s, "nt", BF16, "mla_proj_bwd")
    gw["w_mla_proj"] = unslot(_tn(o_s, dym, "d_w_mla_proj"), VD)
    dQc, dKc, dv_s, *sent = attn_slot_bwd(Qc, Kc, v_s, o_s, lse, do_s, nseq, comm=stage.scatter("attn_bwd", gw))
    stage.scattered("attn_bwd", sent)
    dq_s, dkn_s, dkr = rope_slot_bwd(dQc, dKc, cos16, sin16, "rope_bwd")
    dq_c, gs["q_norm_g"] = mm_rms_bwd([(dq_s, wq_s, "nn")], q_c, p["q_norm_g"], "uq_bwd", dx_dtype=BF16)
    gw["w_uq"] = unslot(_tn(dq_s, qn, "d_w_uq"), QK)
    dkv_c, gs["kv_norm_g"] = mm_rms_bwd([(dkn_s, wk_s, "nn"), (dv_s, wv_s, "nn")], kv_c, p["kv_norm_g"], "ukv_bwd", dx_dtype=BF16)
    gw["w_uk"] = unslot(_tn(dkn_s, kvn, "d_w_uk"), NOPE)
    gw["w_uv"] = unslot(_tn(dv_s, kvn, "d_w_uv"), VD)

    dyn = mm1(dys, W["w_ssd_proj"], "nt", F32, "ssd_proj_bwd")
    gw["w_ssd_proj"] = _tn(yn, dys, "d_w_ssd_proj")
    dyc, dz, gs["ssd_norm_g"] = gated_norm_bwd(y_ssd_core, z, dyn, p["ssd_norm_g"], "ssd_norm_bwd")
    dxbc_act, ddtr, gs["dt_bias"], gs["a_log"], gs["d_skip"], *sent = ssd_bwd(
        xbc_act, dtkr, p["dt_bias"], p["a_log"], p["d_skip"], prev, dyc, nseq, comm=stage.scatter("ssd_bwd", gw))
    stage.scattered("ssd_bwd", sent)
    dxbc, gs["conv_w"], gs["conv_b"] = conv_bwd(xbc, p["conv_w"], p["conv_b"], dxbc_act, nseq)

    gw["w_in"] = jnp.concatenate([_tn(dz, hm, "d_w_in_z"), _tn(dxbc, hm, "d_w_in_xbc"), _tn(ddtr, hm, "d_w_in_dt")[:SSD_H],
                                  _tn(dq_c, hm, "d_w_in_q"), _tn(dkv_c, hm, "d_w_in_kv"), _tn(dkr, hm, "d_w_in_kr")[:ROPE],
                                  _tn(dgl, hm, "d_w_in_gate")], axis=0)
    dx1, gs["mix_pre_g"], *sent = mm_rms_bwd(
        [(dz, wt_z, "nn"), (dxbc, wt_xbc, "nn"), (ddtr, wt_dt, "nn"), (dq_c, wt_q, "nn"), (dkv_c, wt_kv, "nn"),
         (dkr, wt_kr, "nn"), (dgl, wt_gate, "nn")], x1, p["mix_pre_g"], "in_bwd", resid=dx2, comm=stage.scatter("in_bwd", gw))
    stage.scattered("in_bwd", sent)

    dx0, gs["ffn1_pre_g"], gs["ffn1_post_g"] = _ffn_bwd(
        dx1, ffn1, p["ffn1_pre_g"], p["ffn1_post_g"], W["ffn1_w_gate"], W["ffn1_w_up"], W["ffn1_w_down"], "ffn1", stage, gw)
    return loss_row, dx0.reshape(x.shape), gw, gs


def kernel(x, mem, positions, ffn1_pre_g, ffn1_w_gate, ffn1_w_up, ffn1_w_down, ffn1_post_g, mix_pre_g, w_in, conv_w, conv_b, dt_bias, a_log, d_skip, ssd_norm_g, w_ssd_proj, q_norm_g, w_uq, kv_norm_g, w_uk, w_uv, w_mla_proj, gate_bias, w_out, mix_post_g, xa_pre_g, mem_norm_g, w_xq, w_xk, w_xv, w_xo, xa_post_g, ffn2_pre_g, ffn2_w_gate, ffn2_w_up, ffn2_w_down, ffn2_post_g, loss_target, m_ffn1_pre_g, m_ffn1_w_gate, m_ffn1_w_up, m_ffn1_w_down, m_ffn1_post_g, m_mix_pre_g, m_w_in, m_conv_w, m_conv_b, m_dt_bias, m_a_log, m_d_skip, m_ssd_norm_g, m_w_ssd_proj, m_q_norm_g, m_w_uq, m_kv_norm_g, m_w_uk, m_w_uv, m_w_mla_proj, m_gate_bias, m_w_out, m_mix_post_g, m_xa_pre_g, m_mem_norm_g, m_w_xq, m_w_xk, m_w_xv, m_w_xo, m_xa_post_g, m_ffn2_pre_g, m_ffn2_w_gate, m_ffn2_w_up, m_ffn2_w_down, m_ffn2_post_g, v_ffn1_pre_g, v_ffn1_w_gate, v_ffn1_w_up, v_ffn1_w_down, v_ffn1_post_g, v_mix_pre_g, v_w_in, v_conv_w, v_conv_b, v_dt_bias, v_a_log, v_d_skip, v_ssd_norm_g, v_w_ssd_proj, v_q_norm_g, v_w_uq, v_kv_norm_g, v_w_uk, v_w_uv, v_w_mla_proj, v_gate_bias, v_w_out, v_mix_post_g, v_xa_pre_g, v_mem_norm_g, v_w_xq, v_w_xk, v_w_xv, v_w_xo, v_xa_post_g, v_ffn2_pre_g, v_ffn2_w_gate, v_ffn2_w_up, v_ffn2_w_down, v_ffn2_post_g):
    a = dict(locals())
    w = {n: a[n] for n in WEIGHTS}
    m = {n: a["m_" + n] for n in WEIGHTS}
    v = {n: a["v_" + n] for n in WEIGHTS}

    stage = Stage(w)
    W, p = {}, {n: w[n] for n in SMALL}
    stage.gathered("first", run_comm(stage.gather("first"), "allgather_first"), W, p)

    loss_row, grad_x, gw, gs = _local_step(x, mem, positions, loss_target, W, p, stage)

    sm = _pack_small([gs[n] for n in SMALL], loss_row=loss_row, conv_w=gs["conv_w"])
    srecv, = run_comm(ScatterComm([[jnp.broadcast_to(sm[None], (N_DEV,) + sm.shape)]]), "exchange_small")
    s_rows = sum_slots(srecv, "sum_small", tr=sm.shape[0])
    grads, delta, new_m, new_v = {}, {}, {}, {}

    def finish(n, buf, piece):
        col = KIND[n] == "col"
        turn = (lambda t: t.T) if col else (lambda t: t)
        K = w[n].shape[1]
        if col and buf.shape[2] != K:
            buf = buf.reshape(buf.shape[0], -1, K)
        res = adamw_from_slots(buf, piece, turn(w[n][0]), turn(m[n][0]), turn(v[n][0]), "adamw_" + n)
        grads[n], delta[n], new_m[n], new_v[n] = [turn(r)[None] for r in res]

    parts = {}
    for tag, groups in SCATTER_PLAN.items():
        for names, buf in zip(groups, stage.recv[tag]):
            for piece, n in enumerate(names):
                if n in PARTS:
                    parts[n] = sum_slots(buf, "sum_" + n.replace("#", "_"), tr=buf.shape[1])
                else:
                    finish(n, buf, piece)
    for base in sorted({b for b, _, _ in PARTS.values()}):
        rows = jnp.concatenate([parts[pn] for pn in sorted(PARTS) if PARTS[pn][0] == base], axis=0)
        finish(base, rows[None], 0)
    conv_w_full = p["conv_w"]
    small_g, r1 = _unpack_small(s_rows, [w[n].shape for n in SMALL])
    for n, g in zip(SMALL, small_g):
        grads[n] = g
    ncw = math.prod(conv_w_full.shape) // 128
    cw_grad_full = s_rows[r1:r1 + ncw].reshape(conv_w_full.shape)
    wsh = conv_w.shape[2]
    grads["conv_w"] = lax.dynamic_slice_in_dim(cw_grad_full, _dev_index() * wsh, wsh, axis=1)[None]
    loss = s_rows[r1 + ncw, 0]

    d_, m_, v_ = adamw(conv_w[0], grads["conv_w"][0], m["conv_w"][0], v["conv_w"][0], "adamw_conv_w")
    delta["conv_w"], new_m["conv_w"], new_v["conv_w"] = d_[None], m_[None], v_[None]
    sp =[_pack_small([t[n] for n in SMALL]) for t in (w, grads, m, v)]
    outs = adamw(sp[0], sp[1], sp[2], sp[3], "adamw_small")
    for t, buf in zip((delta, new_m, new_v), outs):
        vals, _ = _unpack_small(buf, [w[n].shape for n in SMALL])
        for n, val in zip(SMALL, vals):
            t[n] = val
    return (loss, grad_x, *[grads[n] for n in WEIGHTS], *[delta[n] for n in WEIGHTS],
            *[new_m[n] for n in WEIGHTS], *[new_v[n] for n in WEIGHTS])
```

```python
import functools
import math

import jax
import jax.numpy as jnp
from jax import lax
from jax.experimental import pallas as pl
from jax.experimental.pallas import tpu as pltpu

F32, BF16 = jnp.float32, jnp.bfloat16
HI = lax.Precision.HIGHEST
MESH = pl.DeviceIdType.MESH
N_DEV = 8

D = 1024
DFF = 2816
SSD_H, SSD_P, SSD_G, SSD_N, SSD_L = 16, 64, 2, 128, 128
SSD_INNER = SSD_H * SSD_P
CONV_K, CONV_CH = 4, 1536
MLA_H, QR, KVR, NOPE, ROPE, VD = 16, 384, 256, 64, 32, 64
QK = NOPE + ROPE
ROPE_THETA = 10000.0
XA_H, XA_D = 4, 256
EPS = 1e-6
FFN_RES = 0.5
LR, B1, B2, AEPS, WD, STEP = 0.001, 0.9, 0.999, 1e-08, 0.01, 10

VMEM_LIMIT = 56 * 2**20


def _cp(*sem):
    return pltpu.CompilerParams(dimension_semantics=sem, vmem_limit_bytes=VMEM_LIMIT)


def _sigmoid(x):
    return 1.0 / (1.0 + jnp.exp(-x))


def _softplus(x):
    return jnp.where(x > 20.0, x, jnp.log(1.0 + jnp.exp(jnp.minimum(x, 20.0))))


def _dot(a, b, dims="nn"):
    ca = 0 if dims[0] == "t" else 1
    cb = 1 if dims[1] == "t" else 0
    return lax.dot_general(a.astype(BF16), b.astype(BF16), (((ca,), (cb,)), ((), ())), preferred_element_type=F32)


def _dot_sel(a, b, dims="nn", split="a", terms=3):
    r = (a if split == "a" else b).astype(F32)
    out = None
    for t in range(terms):
        piece = r.astype(BF16)
        if t + 1 < terms:
            r = r - piece.astype(F32)
        d = _dot(piece, b, dims) if split == "a" else _dot(a, piece, dims)
        out = d if out is None else out + d
    return out


def _ssd_common(dtr, dtb, alog):
    L = dtr.shape[0]
    dt = _softplus(dtr + dtb)
    a = -jnp.exp(alog)
    adt = dt * a
    r = lax.broadcasted_iota(jnp.int32, (L, L), 0)
    c = lax.broadcasted_iota(jnp.int32, (L, L), 1)
    lower = r >= c
    tri = lower.astype(F32)
    cs = _dot_sel(tri, adt, "nn", split="b")
    cs_t = _dot_sel(adt, tri, "tt")
    return dt, a, cs, cs_t, lower


def _head_expand():
    hh = lax.broadcasted_iota(jnp.int32, (SSD_H, SSD_INNER), 0)
    jj = lax.broadcasted_iota(jnp.int32, (SSD_H, SSD_INNER), 1)
    return ((jj >= hh * SSD_P) & (jj < hh * SSD_P + SSD_P)).astype(F32)


def _head_reduce():
    hh = lax.broadcasted_iota(jnp.int32, (SSD_INNER, SSD_H), 1)
    jj = lax.broadcasted_iota(jnp.int32, (SSD_INNER, SSD_H), 0)
    return ((jj >= hh * SSD_P) & (jj < hh * SSD_P + SSD_P)).astype(F32)


def ssd_fwd(xbc, dtr, dtb, alog, dsk, nseq, comm=None):
    T = xbc.shape[0]
    S = T // nseq
    C = S // SSD_L
    L = SSD_L
    NP = SSD_H // 2

    def body(x_ref, b_ref, c_ref, dtr_ref, dtb_ref, alog_ref, dsk_ref, y_ref, prev_ref, st_ref):
        ci = pl.program_id(1)

        @pl.when(ci == 0)
        def _():
            st_ref[...] = jnp.zeros_like(st_ref)

        dt, a, cs, cs_t, lower = _ssd_common(dtr_ref[:, 0:SSD_H], dtb_ref[...], alog_ref[...])
        E = _head_expand()
        X = x_ref[...].astype(F32)
        dt_e = _dot_sel(dt, E)
        cs_e = _dot_sel(cs, E)
        csl_e = cs_e[L - 1:L, :]
        Xd = X * dt_e
        Xf = Xd * jnp.exp(csl_e - cs_e)
        e_e = jnp.exp(cs_e)
        skip = _dot_sel(dsk_ref[...], E) * X
        lane = lax.broadcasted_iota(jnp.int32, (1, 2 * SSD_P), 1)
        rowp = lax.broadcasted_iota(jnp.int32, (2 * SSD_P, 1), 0)
        for g in range(SSD_G):
            Bg = b_ref[:, g * SSD_N:(g + 1) * SSD_N]
            Cg = c_ref[:, g * SSD_N:(g + 1) * SSD_N]
            cb = _dot(Cg, Bg, "nt")
            for pp in range(NP // SSD_G):
                p = g * (NP // SSD_G) + pp
                sl = slice(p * 2 * SSD_P, (p + 1) * 2 * SSD_P)
                Xd_p = Xd[:, sl]
                yd = jnp.zeros((L, 2 * SSD_P), F32)
                for q in range(2):
                    h = 2 * p + q
                    m = jnp.where(lower, jnp.exp(jnp.minimum(cs[:, h:h + 1] - cs_t[h:h + 1, :], 0.0)), 0.0)
                    mask = (lane >= q * SSD_P) & (lane < (q + 1) * SSD_P)
                    yd = yd + _dot(cb * m, jnp.where(mask, Xd_p, 0.0))
                S0 = st_ref[p]
                prev_ref[0, 0, p] = S0
                z = _dot(Cg, S0, "nt")
                y_ref[:, sl] = (skip[:, sl] + yd + z * e_e[:, sl]).astype(y_ref.dtype)
                h0 = 2 * p
                dec = jnp.where(rowp < SSD_P, jnp.exp(cs[L - 1:L, h0:h0 + 1]), jnp.exp(cs[L - 1:L, h0 + 1:h0 + 2]))
                st_ref[p] = S0 * dec + _dot(Xf[:, sl], Bg, "tn")

    row = lambda b, c: (b * C + c, 0)
    small = pl.BlockSpec((1, SSD_H), lambda b, c: (0, 0))
    return _call_with_comm(
        body, (nseq, C), "ssd_fwd",
        [pl.BlockSpec((L, SSD_INNER), row),
         pl.BlockSpec((L, SSD_G * SSD_N), lambda b, c: (b * C + c, SSD_INNER // (SSD_G * SSD_N))),
         pl.BlockSpec((L, SSD_G * SSD_N), lambda b, c: (b * C + c, SSD_INNER // (SSD_G * SSD_N) + 1)),
         pl.BlockSpec((L, 128), row), small, small, small],
        [xbc, xbc, xbc, dtr, dtb, alog, dsk],
        [pl.BlockSpec((L, SSD_INNER), row), pl.BlockSpec((1, 1, NP, 2 * SSD_P, SSD_N), lambda b, c: (b, c, 0, 0, 0))],
        [jax.ShapeDtypeStruct((T, SSD_INNER), BF16), jax.ShapeDtypeStruct((nseq, C, NP, 2 * SSD_P, SSD_N), F32)],
        comm, scratch=[pltpu.VMEM((NP, 2 * SSD_P, SSD_N), F32)], sem=("parallel", "arbitrary"))


def ssd_bwd(xbc, dtr, dtb, alog, dsk, prev, dy, nseq, comm=None):
    T = xbc.shape[0]
    S = T // nseq
    C = S // SSD_L
    L = SSD_L
    NP = SSD_H // 2

    def body(x_ref, b_ref, c_ref, dtr_ref, dtb_ref, alog_ref, dsk_ref, prev_ref, dy_ref,
             dxbc_ref, ddtr_ref, ddtb_ref, dalog_ref, ddsk_ref, ds_ref, stg_ref):
        bi = pl.program_id(0)
        ci = pl.program_id(1)

        @pl.when(ci == 0)
        def _():
            ds_ref[...] = jnp.zeros_like(ds_ref)

        @pl.when((ci == 0) & (bi == 0))
        def _():
            ddtb_ref[...] = jnp.zeros_like(ddtb_ref)
            dalog_ref[...] = jnp.zeros_like(dalog_ref)
            ddsk_ref[...] = jnp.zeros_like(ddsk_ref)

        dtr = dtr_ref[:, 0:SSD_H]
        dtb = dtb_ref[...]
        dt, a, cs, cs_t, lower = _ssd_common(dtr, dtb, alog_ref[...])
        upper = lax.broadcasted_iota(jnp.int32, (L, L), 1) >= lax.broadcasted_iota(jnp.int32, (L, L), 0)
        E = _head_expand()
        ET = _head_reduce()
        X = x_ref[...].astype(F32)
        dY = dy_ref[...].astype(F32)
        dt_e = _dot_sel(dt, E)
        cs_e = _dot_sel(cs, E)
        csl_e = cs_e[L - 1:L, :]
        f_e = jnp.exp(csl_e - cs_e)
        e_e = jnp.exp(cs_e)
        dsk_e = _dot_sel(dsk_ref[...], E)
        Xd = X * dt_e
        Xf = Xd * f_e
        lane = lax.broadcasted_iota(jnp.int32, (1, 2 * SSD_P), 1)
        rowp = lax.broadcasted_iota(jnp.int32, (2 * SSD_P, 1), 0)
        hsel = lax.broadcasted_iota(jnp.int32, (1, SSD_H), 1)
        dcs = jnp.zeros((L, SSD_H), F32)
        dcsl = jnp.zeros((1, SSD_H), F32)
        for g in range(SSD_G):
            Bg = b_ref[:, g * SSD_N:(g + 1) * SSD_N]
            Cg = c_ref[:, g * SSD_N:(g + 1) * SSD_N]
            cb = _dot(Cg, Bg, "nt")
            cbt = _dot(Bg, Cg, "nt")
            dB = jnp.zeros((L, SSD_N), F32)
            dC = jnp.zeros((L, SSD_N), F32)
            for pp in range(NP // SSD_G):
                p = g * (NP // SSD_G) + pp
                sl = slice(p * 2 * SSD_P, (p + 1) * 2 * SSD_P)
                Xd_p = Xd[:, sl]
                dY_p = dY[:, sl]
                dXd_p = jnp.zeros((L, 2 * SSD_P), F32)
                for q in range(2):
                    h = 2 * p + q
                    mask = (lane >= q * SSD_P) & (lane < (q + 1) * SSD_P)
                    col = cs[:, h:h + 1]
                    rw = cs_t[h:h + 1, :]
                    m = jnp.where(lower, jnp.exp(jnp.minimum(col - rw, 0.0)), 0.0)
                    mt = jnp.where(upper, jnp.exp(jnp.minimum(rw - col, 0.0)), 0.0)
                    dYm = jnp.where(mask, dY_p, 0.0)
                    dW = _dot(dYm, Xd_p, "nt")
                    dWt = _dot(Xd_p, dYm, "nt")
                    w = cb * m
                    wt = cbt * mt
                    dC = dC + _dot(dW * m, Bg)
                    dB = dB + _dot(dWt * mt, Cg)
                    dXd_p = dXd_p + jnp.where(mask, _dot(wt, dY_p), 0.0)
                    qcol = jnp.sum(dW * w, axis=1, keepdims=True) - jnp.sum(dWt * wt, axis=1, keepdims=True)
                    dcs = dcs + qcol * (hsel == h).astype(F32)
                S0 = prev_ref[0, 0, p]
                dSn = ds_ref[p]
                dZ = dY_p * e_e[:, sl]
                dC = dC + _dot(dZ, S0)
                h0 = 2 * p
                el0 = jnp.exp(cs[L - 1:L, h0:h0 + 1])
                el1 = jnp.exp(cs[L - 1:L, h0 + 1:h0 + 2])
                dec = jnp.where(rowp < SSD_P, el0, el1)
                ds_ref[p] = dSn * dec + _dot(dZ, Cg, "tn")
                dXf_p = _dot(Bg, dSn, "nt")
                dB = dB + _dot(Xf[:, sl], dSn)
                rs = jnp.sum(dSn * S0, axis=1, keepdims=True)
                s0 = jnp.sum(jnp.where(rowp < SSD_P, rs, 0.0), axis=0, keepdims=True) * el0
                s1 = jnp.sum(jnp.where(rowp >= SSD_P, rs, 0.0), axis=0, keepdims=True) * el1
                dcsl = dcsl + s0 * (hsel == h0).astype(F32) + s1 * (hsel == h0 + 1).astype(F32)
                y_off = _dot(Cg, S0, "nt") * e_e[:, sl]
                t1 = dY_p * y_off - dXf_p * Xf[:, sl]
                r1 = jnp.where(lane < SSD_P, t1, 0.0)
                c0 = jnp.sum(r1, axis=1, keepdims=True)
                c1 = jnp.sum(t1 - r1, axis=1, keepdims=True)
                dcs = dcs + c0 * (hsel == h0).astype(F32) + c1 * (hsel == h0 + 1).astype(F32)
                t2 = dXf_p * Xf[:, sl]
                r2 = jnp.where(lane < SSD_P, t2, 0.0)
                dcsl = dcsl + jnp.sum(r2, keepdims=True) * (hsel == h0).astype(F32) \
                    + jnp.sum(t2 - r2, keepdims=True) * (hsel == h0 + 1).astype(F32)
                stg_ref[:, sl] = dXd_p + dXf_p * f_e[:, sl]
            dxbc_ref[:, SSD_INNER + g * SSD_N:SSD_INNER + (g + 1) * SSD_N] = dB.astype(dxbc_ref.dtype)
            dxbc_ref[:, SSD_INNER + (SSD_G + g) * SSD_N:SSD_INNER + (SSD_G + g + 1) * SSD_N] = dC.astype(dxbc_ref.dtype)
        dXd = stg_ref[...]
        dxbc_ref[:, 0:SSD_INNER] = (dXd * dt_e + dsk_e * dY).astype(dxbc_ref.dtype)
        rowl = lax.broadcasted_iota(jnp.int32, (L, 1), 0)
        dcs = dcs + jnp.where(rowl == L - 1, dcsl, 0.0)
        dalpha = _dot_sel(upper.astype(F32), dcs, split="b")
        ddt = _dot_sel(dXd * X, ET, terms=2) + dalpha * a
        dalog_ref[...] += jnp.sum(dalpha * dt, axis=0, keepdims=True) * a
        ddtr = ddt * _sigmoid(dtr + dtb)
        spread = (lax.broadcasted_iota(jnp.int32, (SSD_H, 128), 0) == lax.broadcasted_iota(jnp.int32, (SSD_H, 128), 1)).astype(F32)
        ddtr_ref[...] = _dot(ddtr, spread).astype(ddtr_ref.dtype)
        ddtb_ref[...] += jnp.sum(ddtr, axis=0, keepdims=True)
        ddsk_ref[...] += jnp.sum(_dot_sel(dY * X, ET, terms=2), axis=0, keepdims=True)

    rowr = lambda b, c: (b * C + (C - 1 - c), 0)
    small = pl.BlockSpec((1, SSD_H), lambda b, c: (0, 0))
    return _call_with_comm(
        body, (nseq, C), "ssd_bwd",
        [pl.BlockSpec((L, SSD_INNER), rowr),
         pl.BlockSpec((L, SSD_G * SSD_N), lambda b, c: (b * C + (C - 1 - c), SSD_INNER // (SSD_G * SSD_N))),
         pl.BlockSpec((L, SSD_G * SSD_N), lambda b, c: (b * C + (C - 1 - c), SSD_INNER // (SSD_G * SSD_N) + 1)),
         pl.BlockSpec((L, 128), rowr), small, small, small,
         pl.BlockSpec((1, 1, NP, 2 * SSD_P, SSD_N), lambda b, c: (b, C - 1 - c, 0, 0, 0)),
         pl.BlockSpec((L, SSD_INNER), rowr)],
        [xbc, xbc, xbc, dtr, dtb, alog, dsk, prev, dy],
        [pl.BlockSpec((L, CONV_CH), rowr), pl.BlockSpec((L, 128), rowr), small, small, small],
        [jax.ShapeDtypeStruct((T, CONV_CH), BF16), jax.ShapeDtypeStruct((T, 128), BF16),
         jax.ShapeDtypeStruct((1, SSD_H), F32), jax.ShapeDtypeStruct((1, SSD_H), F32), jax.ShapeDtypeStruct((1, SSD_H), F32)],
        comm, scratch=[pltpu.VMEM((NP, 2 * SSD_P, SSD_N), F32), pltpu.VMEM((L, SSD_INNER), F32)], sem=("arbitrary", "arbitrary"))


SLOT = 128
ATT_T = 512
LOG2E = math.log2(math.e)
Q_SCALE = QK ** -0.5 * LOG2E


def _col_to_row(col):
    n = col.shape[0]
    eye = lax.broadcasted_iota(jnp.int32, (n, n), 0) == lax.broadcasted_iota(jnp.int32, (n, n), 1)
    return jnp.sum(jnp.where(eye, col, 0.0), axis=0, keepdims=True)


def attn_slot_fwd(q, k, v, nseq, comm=None):
    T = q.shape[0]
    S = T // nseq
    t = min(ATT_T, S)
    nb = S // t

    def body(q_ref, k_ref, v_ref, o_ref, lse_ref):
        causal = lax.broadcasted_iota(jnp.int32, (t, t), 1) <= lax.broadcasted_iota(jnp.int32, (t, t), 0)
        for qi in range(nb):
            qb = q_ref[qi * t:(qi + 1) * t, :]
            m = l = acc = None
            for kj in range(qi + 1):
                s = _dot(qb, k_ref[kj * t:(kj + 1) * t, :], "nt")
                if kj == qi:
                    s = jnp.where(causal, s, -1e30)
                bm = jnp.max(s, axis=1, keepdims=True)
                if kj == 0:
                    m = bm
                    p = jnp.exp2(s - m)
                    l = jnp.sum(p, axis=1, keepdims=True)
                    acc = _dot(p, v_ref[0:t, :])
                else:
                    m_new = jnp.maximum(m, bm)
                    corr = jnp.exp2(m - m_new)
                    p = jnp.exp2(s - m_new)
                    l = l * corr + jnp.sum(p, axis=1, keepdims=True)
                    acc = acc * corr + _dot(p, v_ref[kj * t:(kj + 1) * t, :])
                    m = m_new
            o_ref[qi * t:(qi + 1) * t, :] = (acc / l).astype(o_ref.dtype)
            lse_ref[0, 0, :, qi * t:(qi + 1) * t] = _col_to_row(m + jnp.log2(l))

    blk = pl.BlockSpec((S, SLOT), lambda b, h: (b, h))
    return _call_with_comm(
        body, (nseq, MLA_H), "attn_fwd", [blk, blk, blk], [q, k, v],
        [blk, pl.BlockSpec((1, 1, 1, S), lambda b, h: (b, h, 0, 0))],
        [jax.ShapeDtypeStruct((T, MLA_H * SLOT), BF16), jax.ShapeDtypeStruct((nseq, MLA_H, 1, S), F32)], comm)


def attn_slot_bwd(q, k, v, o, lse, do, nseq, comm=None):
    T = q.shape[0]
    S = T // nseq
    t = min(ATT_T, S)
    nb = S // t
    scale = QK ** -0.5

    def body(q_ref, k_ref, v_ref, o_ref, lse_ref, do_ref, dq_ref, dk_ref, dv_ref, dqa_ref):
        causal_t =lax.broadcasted_iota(jnp.int32, (t, t), 0) <= lax.broadcasted_iota(jnp.int32, (t, t), 1)
        ones = jnp.ones((8, SLOT), F32)
        delta = []
        for qi in range(nb):
            sl = slice(qi * t, (qi + 1) * t)
            prod = do_ref[sl, :].astype(F32) * o_ref[sl, :].astype(F32)
            delta.append(_dot_sel(ones, prod, "nt", split="b", terms=2)[0:1, :])
        for kj in range(nb):
            ks = slice(kj * t, (kj + 1) * t)
            kb = k_ref[ks, :]
            vb = v_ref[ks, :]
            dk = dv = None
            for qi in range(kj, nb):
                sl = slice(qi * t, (qi + 1) * t)
                qb = q_ref[sl, :]
                dob = do_ref[sl, :]
                st = _dot(kb, qb, "nt")
                pt = jnp.exp2(st - lse_ref[0, 0, :, sl])
                if qi == kj:
                    pt = jnp.where(causal_t, pt, 0.0)
                dpt = _dot(vb, dob, "nt")
                dst = (pt * (dpt - delta[qi])).astype(BF16)
                dvc = _dot(pt, dob)
                dkc = _dot(dst, qb) * (1.0 / LOG2E)
                dv = dvc if dv is None else dv + dvc
                dk = dkc if dk is None else dk + dkc
                dqc = _dot(dst, kb, "tn") * scale
                if kj > 0:
                    dqc = dqc + dqa_ref[sl, :]
                if qi == kj:
                    dq_ref[sl, :] = dqc.astype(dq_ref.dtype)
                else:
                    dqa_ref[sl, :] = dqc
            dk_ref[ks, :] = dk.astype(dk_ref.dtype)
            dv_ref[ks, :] = dv.astype(dv_ref.dtype)

    blk = pl.BlockSpec((S, SLOT), lambda b, h: (b, h))
    lse_spec = pl.BlockSpec((1, 1, 1, S), lambda b, h: (b, h, 0, 0))
    W = MLA_H * SLOT
    return _call_with_comm(
        body, (nseq, MLA_H), "attn_bwd", [blk, blk, blk, blk, lse_spec, blk], [q, k, v, o, lse, do], [blk, blk, blk],
        [jax.ShapeDtypeStruct((T, W), BF16)] * 3, comm, scratch=[pltpu.VMEM((S, SLOT), F32)])


def _rope_coeffs(cos, sin):
    half = ROPE // 2
    r = lax.broadcasted_iota(jnp.int32, (half, SLOT), 0)
    c = lax.broadcasted_iota(jnp.int32, (half, SLOT), 1)
    pc = ((c == r + NOPE) | (c == r + NOPE + half)).astype(F32)
    ps = (c == r + NOPE + half).astype(F32) - (c == r + NOPE).astype(F32)
    lane = lax.broadcasted_iota(jnp.int32, (1, SLOT), 1)
    return _dot_sel(cos, pc) + (lane < NOPE).astype(F32), _dot_sel(sin, ps)


def _rope_swap(x):
    W = x.shape[1]
    half = ROPE // 2
    lane = lax.broadcasted_iota(jnp.int32, (1, W), 1) & (SLOT - 1)
    up = pltpu.roll(x, W - half, axis=1)
    dn = pltpu.roll(x, half, axis=1)
    return jnp.where((lane >= NOPE) & (lane < NOPE + half), up, jnp.where((lane >= NOPE + half) & (lane < QK), dn, 0.0))


def rope_slot_fwd(q, kn, dtkr, cos, sin, name):
    def fn(qv, knv, krv, cv, sv):
        C, Sg = _rope_coeffs(cv, sv)
        ct, stl = jnp.tile(C, (1, MLA_H)), jnp.tile(Sg, (1, MLA_H))
        qo = (qv * ct + _rope_swap(qv) * stl) * Q_SCALE
        r = lax.broadcasted_iota(jnp.int32, (SLOT, SLOT), 0)
        c = lax.broadcasted_iota(jnp.int32, (SLOT, SLOT), 1)
        place = ((c == r + NOPE) & (r < ROPE)).astype(F32)
        kr = _dot_sel(krv, place)
        kr = kr * C + _rope_swap(kr) * Sg
        return qo, knv.astype(F32) + jnp.tile(kr, (1, MLA_H))
    W = MLA_H * SLOT
    return rowwise(fn, [q, kn, (dtkr, SLOT, 1), cos, sin], [], [(W, BF16), (W, BF16)], [], name)


def rope_slot_bwd(dq, dk, cos, sin, name):
    def fn(dqv, dkv, cv, sv):
        C, Sg = _rope_coeffs(cv, sv)
        ct, stl = jnp.tile(C, (1, MLA_H)), jnp.tile(Sg, (1, MLA_H))
        dqo = dqv * ct - _rope_swap(dqv) * stl
        tot = dkv[:, 0:SLOT]
        for h in range(1, MLA_H):
            tot = tot + dkv[:, h * SLOT:(h + 1) * SLOT]
        u = tot * C - _rope_swap(tot) * Sg
        r = lax.broadcasted_iota(jnp.int32, (SLOT, SLOT), 0)
        c = lax.broadcasted_iota(jnp.int32, (SLOT, SLOT), 1)
        unplace = ((r == c + NOPE) & (c < ROPE)).astype(F32)
        return dqo, dkv, _dot_sel(u, unplace, terms=2)
    W = MLA_H * SLOT
    return rowwise(fn, [dq, dk, cos, sin], [], [(W, BF16), (W, BF16), (SLOT, BF16)], [], name)


XA_BLK = 512


def xattn_fwd(q, k, v, nseq):
    T = q.shape[0]
    S = T // nseq
    M = k.shape[0] // nseq
    tq = min(XA_BLK, S)
    nq = S // tq
    scale = XA_D ** -0.5

    def body(q_ref, k_ref, v_ref, o_ref):
        s = _dot(q_ref[...], k_ref[...], "nt") * scale
        p = jnp.exp(s - jnp.max(s, axis=1, keepdims=True))
        p = p / jnp.sum(p, axis=1, keepdims=True)
        o_ref[...] = _dot(p, v_ref[...]).astype(o_ref.dtype)

    qs = pl.BlockSpec((tq, XA_D), lambda b, h, i: (b * nq + i, h))
    ks = pl.BlockSpec((M, XA_D), lambda b, h, i: (b, h))
    return pl.pallas_call(
        body, grid=(nseq, XA_H, nq), name="xattn_fwd", in_specs=[qs, ks, ks], out_specs=qs,
        out_shape=jax.ShapeDtypeStruct((T, XA_H * XA_D), BF16),
        compiler_params=_cp("parallel", "parallel", "parallel"),
    )(q, k, v)


def xattn_bwd(q, k, v, do, nseq):
    T = q.shape[0]
    S = T // nseq
    M = k.shape[0] // nseq
    tq = min(XA_BLK, S)
    nq = S // tq
    scale = XA_D ** -0.5

    def body(q_ref, k_ref, v_ref, do_ref, dq_ref, dk_ref, dv_ref):
        @pl.when(pl.program_id(2) == 0)
        def _():
            dk_ref[...] = jnp.zeros_like(dk_ref)
            dv_ref[...] = jnp.zeros_like(dv_ref)

        qb, kb, vb, dob = q_ref[...], k_ref[...], v_ref[...], do_ref[...]
        s = _dot(qb, kb, "nt") * scale
        p = jnp.exp(s - jnp.max(s, axis=1, keepdims=True))
        p = p / jnp.sum(p, axis=1, keepdims=True)
        dp = _dot(dob, vb, "nt")
        ds = p * (dp - jnp.sum(dp * p, axis=1, keepdims=True)) * scale
        dq_ref[...] = _dot(ds, kb).astype(dq_ref.dtype)
        dk_ref[...] += _dot(ds, qb, "tn")
        dv_ref[...] += _dot(p, dob, "tn")

    qs = pl.BlockSpec((tq, XA_D), lambda b, h, i: (b * nq + i, h))
    ks = pl.BlockSpec((M, XA_D), lambda b, h, i: (b, h))
    return pl.pallas_call(
        body, grid=(nseq, XA_H, nq), name="xattn_bwd", in_specs=[qs, ks, ks, qs], out_specs=[qs, ks, ks],
        out_shape=[jax.ShapeDtypeStruct((T, XA_H * XA_D), BF16), jax.ShapeDtypeStruct(k.shape, F32),
                   jax.ShapeDtypeStruct(k.shape, F32)],
        compiler_params=_cp("parallel", "parallel", "arbitrary"),
    )(q, k, v, do)


CONV_BLK = 256


def _shift_down(x, s, rows):
    if s == 0:
        return x
    return jnp.where(rows >= s, pltpu.roll(x, s, axis=0), 0.0)


def _shift_up(x, s, rows):
    if s == 0:
        return x
    S = x.shape[0]
    return jnp.where(rows < S - s, pltpu.roll(x, S - s, axis=0), 0.0)


def conv_fwd(x, w, b, nseq):
    T, CH = x.shape
    S = T // nseq

    def body(x_ref, w_ref, b_ref, o_ref):
        xv = x_ref[...].astype(F32)
        rows = lax.broadcasted_iota(jnp.int32, (S, 1), 0)
        c = jnp.zeros_like(xv) + b_ref[...]
        for kk in range(CONV_K):
            c = c + w_ref[kk:kk + 1, :] * _shift_down(xv, CONV_K - 1 - kk, rows)
        o_ref[...] = (c * _sigmoid(c)).astype(o_ref.dtype)

    xs = pl.BlockSpec((S, CONV_BLK), lambda j, bb: (bb, j))
    return pl.pallas_call(
        body, grid=(CH // CONV_BLK, nseq), name="conv_fwd",
        in_specs=[xs, pl.BlockSpec((CONV_K, CONV_BLK), lambda j, bb: (0, j)), pl.BlockSpec((1, CONV_BLK), lambda j, bb: (0, j))],
        out_specs=xs, out_shape=jax.ShapeDtypeStruct((T, CH), BF16),
        compiler_params=_cp("parallel", "parallel"),
    )(x, w, b)


def conv_bwd(x, w, b, dout, nseq):
    T, CH = x.shape
    S = T // nseq

    def body(x_ref, w_ref, b_ref, do_ref, dx_ref, dw_ref, db_ref):
        @pl.when(pl.program_id(1) == 0)
        def _():
            dw_ref[...] = jnp.zeros_like(dw_ref)
            db_ref[...] = jnp.zeros_like(db_ref)

        xv = x_ref[...].astype(F32)
        rows = lax.broadcasted_iota(jnp.int32, (S, 1), 0)
        c = jnp.zeros_like(xv) + b_ref[...]
        sh = [_shift_down(xv, CONV_K - 1 - kk, rows) for kk in range(CONV_K)]
        for kk in range(CONV_K):
            c = c + w_ref[kk:kk + 1, :] * sh[kk]
        sg = _sigmoid(c)
        dc = do_ref[...].astype(F32) * sg * (1.0 + c * (1.0 - sg))
        dx = jnp.zeros_like(xv)
        for kk in range(CONV_K):
            dx = dx + w_ref[kk:kk + 1, :] * _shift_up(dc, CONV_K - 1 - kk, rows)
            dw_ref[kk:kk + 1, :] += jnp.sum(dc * sh[kk], axis=0, keepdims=True)
        dx_ref[...] = dx.astype(dx_ref.dtype)
        db_ref[...] += jnp.sum(dc, axis=0, keepdims=True)

    xs = pl.BlockSpec((S, CONV_BLK), lambda j, bb: (bb, j))
    ws = pl.BlockSpec((CONV_K, CONV_BLK), lambda j, bb: (0, j))
    bs = pl.BlockSpec((1, CONV_BLK), lambda j, bb: (0, j))
    return pl.pallas_call(
        body, grid=(CH // CONV_BLK, nseq), name="conv_bwd",
        in_specs=[xs, ws, bs, xs], out_specs=[xs, ws, bs],
        out_shape=[jax.ShapeDtypeStruct((T, CH), BF16), jax.ShapeDtypeStruct((CONV_K, CH), F32),
                   jax.ShapeDtypeStruct((1, CH), F32)],
        compiler_params=_cp("parallel", "arbitrary"),
    )(x, w, b, dout)


def _dims(a, b, mode):
    M = a.shape[1] if mode[0] == "t" else a.shape[0]
    K = a.shape[0] if mode[0] == "t" else a.shape[1]
    N = b.shape[0] if mode[1] == "t" else b.shape[1]
    return M, K, N


def _tile(dim, prefs):
    for p in prefs:
        if dim % p == 0:
            return p
    return dim


def mm(groups, out_dtypes, name, tm=None, tn=None, tk=None, epi=None, extras=(), comm=None, sub=1, n_sum=0):
    a0, b0, m0 = groups[0][0]
    M, K0, N = _dims(a0, b0, m0)
    tm = tm or _tile(M, (1024, 512, 256, 128))
    tn = tn or _tile(N, (512, 256, 128))
    flat = [p for g in groups for p in g]
    nk = 1 if tk is None else K0 // tk
    in_specs, args = [], []
    for a, b, mode in flat:
        _, K, _ = _dims(a, b, mode)
        kb = K if tk is None else tk
        in_specs.append(pl.BlockSpec((kb, tm), lambda i, j, k: (k, i)) if mode[0] == "t"
                        else pl.BlockSpec((tm, kb), lambda i, j, k: (i, k)))
        in_specs.append(pl.BlockSpec((tn, kb), lambda i, j, k: (j, k)) if mode[1] == "t"
                        else pl.BlockSpec((kb, tn), lambda i, j, k: (k, j)))
        args += [a, b]
    for e in extras:
        in_specs.append(pl.BlockSpec((1, tn), lambda i, j, k: (0, j)) if e.shape[0] == 1 and M != 1
                        else pl.BlockSpec((tm, tn), lambda i, j, k: (i, j)))
        args.append(e)
    n_in = len(args)
    n_main = len(out_dtypes)
    n_out = n_main + n_sum
    assert n_sum == 0 or (tn == N and tk is None)
    ng = len(groups)
    sizes = [len(g) for g in groups]

    def body(*refs):
        ins, outs, accs = refs[:n_in], refs[n_in:n_in + n_out], refs[n_in + n_out:]
        kk = pl.program_id(2)

        def dots(rs):
            vals, pos = [], 0
            for gi in range(ng):
                acc = None
                for _ in range(sizes[gi]):
                    mode = flat[pos // 2][2]
                    av = ins[pos][:, rs] if mode[0] == "t" else ins[pos][rs, :]
                    d = _dot(av, ins[pos + 1][...], mode)
                    acc = d if acc is None else acc + d
                    pos += 2
                vals.append(acc)
            return vals

        def finish(accv, rs, first_chunk=True):
            ex = [(r[...] if r.shape[0] == 1 and tm != 1 else r[rs, :]).astype(F32) for r in ins[2 * len(flat):]]
            res = epi(accv, ex) if epi is not None else tuple(accv)
            for o, r in zip(outs[:n_main], res[:n_main]):
                o[rs, :] = r.astype(o.dtype)
            for o, r in zip(outs[n_main:], res[n_main:]):
                if first_chunk:
                    @pl.when(pl.program_id(0) == 0)
                    def _():
                        o[...] = r

                    @pl.when(pl.program_id(0) > 0)
                    def _():
                        o[...] += r
                else:
                    o[...] += r

        if nk == 1:
            for r in range(sub):
                rs = slice(r * (tm // sub), (r + 1) * (tm // sub))
                finish(dots(rs), rs, r == 0)
        else:
            vals = dots(slice(0, tm))
            finish = functools.partial(finish, rs=slice(0, tm))
            @pl.when(kk == 0)
            def _():
                for ar, vv in zip(accs, vals):
                    ar[...] = vv

            @pl.when(kk > 0)
            def _():
                for ar, vv in zip(accs, vals):
                    ar[...] += vv

            @pl.when(kk == nk - 1)
            def _():
                finish([ar[...] for ar in accs])

    grid = (M // tm, N // tn, nk)
    out_specs = [pl.BlockSpec((tm, tn), lambda i, j, k: (i, j)) for _ in out_dtypes] \
        + [pl.BlockSpec((1, tn), lambda i, j, k: (0, j))] * n_sum
    out_shape = [jax.ShapeDtypeStruct((M, N), dt) for dt in out_dtypes] + [jax.ShapeDtypeStruct((1, N), F32)] * n_sum
    scratch = [pltpu.VMEM((tm, tn), F32) for _ in range(ng if nk > 1 else 0)]
    sem = ("arbitrary" if n_sum else "parallel", "parallel", "arbitrary")
    if comm is not None:
        body = _attach(comm, body, n_in, n_out, *_grid_ends(grid))
        in_specs, args = in_specs + [HBM_SPEC] * len(comm.inputs), args + comm.inputs
        out_specs, out_shape = out_specs + [HBM_SPEC] * len(comm.out_shapes), out_shape + comm.out_shapes
        scratch, sem = scratch + comm.sems, ("arbitrary",) * 3
    return pl.pallas_call(body, grid=grid, name=name, in_specs=in_specs, out_specs=out_specs, out_shape=out_shape,
                          scratch_shapes=scratch, compiler_params=_cp(*sem))(*args)


def mm1(a, b, mode, out_dtype, name, **kw):
    return mm([[(a, b, mode)]], [out_dtype], name, **kw)[0]


ROW_BLK = 512


def rowwise(fn, rows, consts, outs, accs, name, tb=ROW_BLK):
    rows = [r if isinstance(r, tuple) else (r, r.shape[1], 0) for r in rows]
    T = rows[0][0].shape[0]
    tb = min(tb, T)
    n_r, n_c, n_o, n_a = len(rows), len(consts), len(outs), len(accs)

    def body(*refs):
        vals = [r[...].astype(F32) for r in refs[:n_r + n_c]]
        res = fn(*vals)
        o_refs = refs[n_r + n_c:n_r + n_c + n_o]
        a_refs = refs[n_r + n_c + n_o:]
        for o, r in zip(o_refs, res[:n_o]):
            o[...] = r.astype(o.dtype)
        if n_a:
            @pl.when(pl.program_id(0) == 0)
            def _():
                for ar in a_refs:
                    ar[...] = jnp.zeros_like(ar)
            for ar, r in zip(a_refs, res[n_o:]):
                ar[...] += r

    return pl.pallas_call(
        body, grid=(T // tb,), name=name,
        in_specs=[pl.BlockSpec((tb, w), functools.partial(lambda i, j: (i, j), j=j)) for _, w, j in rows]
        + [pl.BlockSpec(c.shape, lambda i: (0, 0)) for c in consts],
        out_specs=[pl.BlockSpec((tb, d), lambda i: (i, 0)) for d, _ in outs]
        + [pl.BlockSpec(s, lambda i: (0, 0)) for s in accs],
        out_shape=[jax.ShapeDtypeStruct((T, d), dt) for d, dt in outs]
        + [jax.ShapeDtypeStruct(s, F32) for s in accs],
        compiler_params=_cp("arbitrary" if n_a else "parallel"),
    )(*[r[0] for r in rows], *consts)


def _rms_stats(x):
    r = lax.rsqrt(jnp.mean(x * x, axis=-1, keepdims=True) + EPS)
    return r, x * r


def _rms_bwd(x, g, dy):
    r, xn = _rms_stats(x)
    dyg = dy * g
    dx = r * (dyg - xn * jnp.mean(dyg * xn, axis=-1, keepdims=True))
    return dx, jnp.sum(dy * xn, axis=0, keepdims=True)


def rms_fwd(x, g, name):
    return rowwise(lambda xv, gv: (_rms_stats(xv)[1] * gv,), [x], [g], [(x.shape[1], BF16)], [], name)[0]


def rms_bwd(x, g, dy, name, resid=None, dx_dtype=F32):
    def fn(*v):
        if resid is None:
            xv, dyv, gv = v
            dx, dg = _rms_bwd(xv, gv, dyv)
        else:
            xv, dyv, rv, gv = v
            dx, dg = _rms_bwd(xv, gv, dyv)
            dx = dx + rv
        return dx, dg
    rows = [x, dy] + ([] if resid is None else [resid])
    return rowwise(fn, rows, [g], [(x.shape[1], dx_dtype)], [(1, x.shape[1])], name)


def mm_rms_bwd(pairs, x, g, name, resid=None, dx_dtype=F32, comm=None):
    def epi(accs, ex):
        dx, dg = _rms_bwd(ex[0], ex[-1], accs[0])
        return (dx if resid is None else dx + ex[1]), dg
    extras = [x] + ([] if resid is None else [resid]) + [g]
    return mm([pairs], [dx_dtype], name, tm=min(256, x.shape[0]), tn=x.shape[1], epi=epi, extras=extras, comm=comm, n_sum=1)


def mm_resid(a, b, x, g, wgt, name, comm=None):
    epi = lambda accs, ex: (accs[0], ex[0] + wgt * _rms_stats(accs[0])[1] * ex[1])
    return mm([[(a, b, "nn")]], [F32, F32], name, tm=min(512, a.shape[0]), tn=b.shape[1], epi=epi, extras=[x, g], sub=2,
              comm=comm)


def resid_bwd(h, g, dy, wgt, name):
    def fn(hv, dyv, gv):
        dx, dg = _rms_bwd(hv, gv, dyv)
        return wgt * dx, wgt * dg
    return rowwise(fn, [h, dy], [g], [(h.shape[1], BF16)], [(1, h.shape[1])], name)


def _silu_parts(g):
    s = _sigmoid(g)
    return g * s, s * (1.0 + g * (1.0 - s))


def gated_norm_fwd(y, z, g, name):
    W = SSD_INNER // SSD_G

    def fn(yv, zv, gv):
        yg = yv * _silu_parts(zv)[0]
        return (jnp.concatenate([_rms_stats(yg[:, i * W:(i + 1) * W])[1] for i in range(SSD_G)], axis=1) * gv,)
    return rowwise(fn, [y, z], [g], [(SSD_INNER, BF16)], [], name)[0]


def gated_norm_bwd(y, z, dyn, g, name):
    W = SSD_INNER // SSD_G

    def fn(yv, zv, dv, gv):
        sil, dsil = _silu_parts(zv)
        yg = yv * sil
        parts = [_rms_bwd(yg[:, i * W:(i + 1) * W], gv[:, i * W:(i + 1) * W], dv[:, i * W:(i + 1) * W]) for i in range(SSD_G)]
        dyg = jnp.concatenate([p[0] for p in parts], axis=1)
        dg = jnp.concatenate([p[1] for p in parts], axis=1)
        return dyg * sil, dyg * yv * dsil, dg
    return rowwise(fn, [y, z, dyn], [g], [(SSD_INNER, BF16), (SSD_INNER, BF16)], [(1, SSD_INNER)], name)


def merge_fwd(gl, ys, ym, gb, name):
    def fn(glv, ysv, ymv, gbv):
        gt = _sigmoid(glv + gbv)
        return (gt[:, :D] * ysv + gt[:, D:] * ymv,)
    return rowwise(fn, [gl, ys, ym], [gb], [(D, BF16)], [], name)[0]


def merge_bwd(gl, ys, ym, dm, gb, name):
    def fn(glv, ysv, ymv, dmv, gbv):
        gt = _sigmoid(glv + gbv)
        gs, gm = gt[:, :D], gt[:, D:]
        dgl = jnp.concatenate([dmv * ysv * gs * (1.0 - gs), dmv * ymv * gm * (1.0 - gm)], axis=1)
        return dmv * gs, dmv * gm, dgl, jnp.sum(dgl, axis=0, keepdims=True)
    return rowwise(fn, [gl, ys, ym, dm], [gb], [(D, BF16), (D, BF16), (2 * D, BF16)], [(1, 2 * D)], name)


def loss_head(y, tgt, name):
    def fn(yv, tv):
        d = yv - tv
        part = 0.5 * jnp.sum(jnp.sum(d * d, axis=1, keepdims=True), axis=0, keepdims=True) / D
        return d / D, jnp.broadcast_to(part, (1, 128))
    return rowwise(fn, [y, tgt], [], [(D, F32)], [(1, 128)], name)


def _adamw_math(wv, gv, mv, vv):
    mn = B1 * mv + (1.0 - B1) * gv
    vn = B2 * vv + (1.0 - B2) * (gv * gv)
    mh = mn / (1.0 - B1 ** STEP)
    vh = vn / (1.0 - B2 ** STEP)
    return -LR * (mh / (jnp.sqrt(vh) + AEPS) + WD * wv), mn, vn


def adamw(w, g, m, v, name):
    R, C = w.shape
    tb = _tile(R, (256, 128, 64, 32, 16, 8))
    return rowwise(_adamw_math, [w, g, m, v], [], [(C, F32)] * 3, [], name, tb=tb)


def adamw_from_slots(recv, piece, w, m, v, name):
    K, n = w.shape
    ns = recv.shape[0]
    assert recv.shape[2] == n and recv.shape[1] % K == 0
    tb = _tile(K, (256, 176, 128, 64, 32, 16, 8)) if K % 8 == 0 else K
    r_spec = pl.BlockSpec((ns, tb, n), lambda i: (0, piece * (K // tb) + i, 0))
    w_spec = pl.BlockSpec((tb, n), lambda i: (i, 0))

    def body(r_ref, w_ref, m_ref, v_ref, g_ref, d_ref, mo_ref, vo_ref):
        g = r_ref[0].astype(F32)
        for s in range(1, ns):
            g = g + r_ref[s].astype(F32)
        g_ref[...] = g
        d_ref[...], mo_ref[...], vo_ref[...] = _adamw_math(w_ref[...], g, m_ref[...], v_ref[...])

    return pl.pallas_call(
        body, grid=(K // tb,), name=name, in_specs=[r_spec, w_spec, w_spec, w_spec], out_specs=[w_spec] * 4,
        out_shape=[jax.ShapeDtypeStruct((K, n), F32)] * 4, compiler_params=_cp("parallel"),
    )(recv, w, m, v)


def _me():
    return lax.axis_index("x"), lax.axis_index("y"), lax.axis_index("c")


def _dev_index():
    x, y, c = _me()
    return 4 * x + 2 * y + c


HBM_SPEC = pl.BlockSpec(memory_space=pl.ANY)


class GatherComm:
    def __init__(self, shards):
        self.inputs = list(shards)
        n = len(shards)
        self.out_shapes = [jax.ShapeDtypeStruct((N_DEV,) + s.shape, s.dtype) for s in shards]
        self.sems = [pltpu.SemaphoreType.DMA((7 * n,)), pltpu.SemaphoreType.DMA((7 * n,)), pltpu.SemaphoreType.DMA((n,))]

    def _plan(self, x_refs, out_refs, sems):
        send_sems, recv_sems, local_sems = sems
        n = len(x_refs)
        x, y, c = _me()
        me, sibling = (x, y, c), (x, y, 1 - c)
        chips = [(1 - x, y), (x, 1 - y), (1 - x, 1 - y)]

        def slot(i, px, py, pc):
            return out_refs[i].at[4 * px + 2 * py + pc]

        def copy(i, k, block, to, src=None):
            return pltpu.make_async_remote_copy(
                src_ref=slot(i, *block) if src is None else src, dst_ref=slot(i, *block),
                send_sem=send_sems.at[7 * i + k], recv_sem=recv_sems.at[7 * i + k], device_id=to, device_id_type=MESH)

        mine = [pltpu.make_async_copy(x_refs[i], slot(i, *me), local_sems.at[i]) for i in range(n)]
        first = []
        for i in range(n):
            first.append(copy(i, 0, me, sibling, src=x_refs[i]))
            first += [copy(i, 1 + j, me, (*chip, c), src=x_refs[i]) for j, chip in enumerate(chips)]
        passed = [[copy(i, 4 + j, (*chip, c), sibling) for j, chip in enumerate(chips)] for i in range(n)]
        from_ici = [[copy(i, 1 + j, (*chip, c), me) for j, chip in enumerate(chips)] for i in range(n)]
        from_sib = [[copy(i, 0, sibling, me)] + [copy(i, 4 + j, (*chip, 1 - c), me) for j, chip in enumerate(chips)] for i in range(n)]
        return mine, first, passed, from_ici, from_sib

    def start(self, x_refs, out_refs, sems):
        mine, first, _, _, _ = self._plan(x_refs, out_refs, sems)
        for cp in mine + first:
            cp.start()

    def finish(self, x_refs, out_refs, sems):
        mine, first, passed, from_ici, from_sib = self._plan(x_refs, out_refs, sems)
        for i in range(len(x_refs)):
            for arrival, forward in zip(from_ici[i], passed[i]):
                arrival.wait_recv()
                forward.start()
        for row in from_sib:
            for arrival in row:
                arrival.wait_recv()
        for cp in first + [cp for row in passed for cp in row]:
            cp.wait_send()
        for cp in mine:
            cp.wait()


def run_comm(comm, name):
    n_in, n_out = len(comm.inputs), len(comm.out_shapes)

    def body(*refs):
        ins, outs, sems = refs[:n_in], refs[n_in:n_in + n_out], refs[n_in + n_out:]
        comm.start(ins, outs, sems)
        comm.finish(ins, outs, sems)

    return pl.pallas_call(body, name=name, out_shape=comm.out_shapes, in_specs=[HBM_SPEC] * n_in,
                          out_specs=[HBM_SPEC] * n_out, scratch_shapes=comm.sems)(*comm.inputs)


def _attach(comm, body, n_in, n_out, first, last):
    if comm is None:
        return body
    ci, co, cs = len(comm.inputs), len(comm.out_shapes), len(comm.sems)

    def wrapped(*refs):
        h_in, c_in = refs[:n_in], refs[n_in:n_in + ci]
        h_out, c_out = refs[n_in + ci:n_in + ci + n_out], refs[n_in + ci + n_out:n_in + ci + n_out + co]
        rest = refs[n_in + ci + n_out + co:]
        h_scr, c_sem = rest[:len(rest) - cs], rest[len(rest) - cs:]

        @pl.when(first())
        def _():
            comm.start(c_in, c_out, c_sem)

        body(*h_in, *h_out, *h_scr)

        @pl.when(last())
        def _():
            comm.finish(c_in, c_out, c_sem)

    return wrapped


def _grid_ends(grid):
    first = lambda: functools.reduce(lambda a, b: a & b, [pl.program_id(i) == 0 for i in range(len(grid))])
    last = lambda: functools.reduce(lambda a, b: a & b, [pl.program_id(i) == g - 1 for i, g in enumerate(grid)])
    return first, last


def _call_with_comm(body, grid, name, in_specs, args, out_specs, out_shape, comm, scratch=(), sem=None):
    sem = sem or ("parallel",) * len(grid)
    scratch = list(scratch)
    if comm is not None:
        body = _attach(comm, body, len(args), len(out_shape), *_grid_ends(grid))
        in_specs, args = in_specs + [HBM_SPEC] * len(comm.inputs), args + comm.inputs
        out_specs, out_shape = out_specs + [HBM_SPEC] * len(comm.out_shapes), out_shape + comm.out_shapes
        scratch, sem = scratch + comm.sems, ("arbitrary",) * len(grid)
    return pl.pallas_call(body, grid=grid, name=name, in_specs=in_specs, out_specs=out_specs, out_shape=out_shape,
                          scratch_shapes=scratch, compiler_params=_cp(*sem))(*args)


class ScatterComm:
    def __init__(self, groups):
        self.sizes = [len(g) for g in groups]
        self.rows = [[pc.shape[1] for pc in g] for g in groups]
        ng = len(groups)
        self.inputs = [pc for g in groups for pc in g]
        self.out_shapes = [jax.ShapeDtypeStruct((N_DEV, sum(self.rows[gi]), g[0].shape[2]), g[0].dtype) for gi, g in enumerate(groups)]
        self.sems = [pltpu.SemaphoreType.DMA((7 * ng,)), pltpu.SemaphoreType.DMA((7 * ng,)), pltpu.SemaphoreType.DMA((ng,))]

    def _peers(self):
        x, y, c = _me()
        out = []
        for k in range(1, N_DEV):
            px = 1 - x if k & 4 else x
            py = 1 - y if k & 2 else y
            pc = 1 - c if k & 1 else c
            out.append((k, 4 * px + 2 * py + pc, dict(device_id=(px, py, pc), device_id_type=MESH)))
        return 4 * x + 2 * y + c, out

    def start(self, ins, outs, sems):
        send_sems, recv_sems, local_sems = sems
        me, peers = self._peers()
        pos = 0
        for gi, size in enumerate(self.sizes):
            for i, pc in enumerate(ins[pos:pos + size]):
                dst = outs[gi].at[me, pl.ds(sum(self.rows[gi][:i]), self.rows[gi][i])]
                pltpu.make_async_copy(pc.at[me], dst, local_sems.at[gi]).start()
                for k, peer, kw in peers:
                    pltpu.make_async_remote_copy(src_ref=pc.at[peer], dst_ref=dst, send_sem=send_sems.at[7 * gi + k - 1],
                                                 recv_sem=recv_sems.at[7 * gi + k - 1], **kw).start()
            pos += size

    def finish(self, ins, outs, sems):
        send_sems, recv_sems, local_sems = sems
        me, peers = self._peers()
        whole = [pltpu.make_async_remote_copy(src_ref=outs[gi].at[peer], dst_ref=outs[gi].at[peer],
                                              send_sem=send_sems.at[7 * gi + k - 1], recv_sem=recv_sems.at[7 * gi + k - 1], **kw)
                 for gi in range(len(self.sizes)) for k, peer, kw in peers]
        for cp in whole:
            cp.wait_recv()
        for cp in whole:
            cp.wait_send()
        for gi in range(len(self.sizes)):
            pltpu.make_async_copy(outs[gi].at[me], outs[gi].at[me], local_sems.at[gi]).wait()


def sum_slots(recv, name, tr):
    n, R, C = recv.shape

    def body(r_ref, o_ref):
        acc = r_ref[0].astype(F32)
        for s in range(1, n):
            acc = acc + r_ref[s].astype(F32)
        o_ref[...] = acc

    return pl.pallas_call(
        body, grid=(R // tr,), name=name,
        in_specs=[pl.BlockSpec((n, tr, C), lambda i: (0, i, 0))], out_specs=pl.BlockSpec((tr, C), lambda i: (i, 0)),
        out_shape=jax.ShapeDtypeStruct((R, C), F32), compiler_params=_cp("parallel"),
    )(recv)


PACK_W, FLAT_W = 1024, 128
MAIN = [
    ("ffn1_w_gate", "col"), ("ffn1_w_up", "col"), ("ffn1_w_down", "row"),
    ("ffn2_w_gate", "col"), ("ffn2_w_up", "col"), ("ffn2_w_down", "row"),
    ("w_ssd_proj", "row"), ("w_mla_proj", "row"), ("w_out", "row"),
    ("w_xq", "row"), ("w_xk", "row"), ("w_xv", "row"), ("w_xo", "row"),
    ("w_uk", "col"), ("w_uv", "col"),
]
FLAT = [("w_in", "col"), ("w_uq", "col")]
BIG = MAIN + FLAT
SMALL = ["ffn1_pre_g", "ffn1_post_g", "mix_pre_g", "conv_b", "dt_bias", "a_log", "d_skip", "ssd_norm_g", "q_norm_g",
         "kv_norm_g", "gate_bias", "mix_post_g", "xa_pre_g", "mem_norm_g", "xa_post_g", "ffn2_pre_g", "ffn2_post_g"]
WEIGHTS = ['ffn1_pre_g', 'ffn1_w_gate', 'ffn1_w_up', 'ffn1_w_down', 'ffn1_post_g', 'mix_pre_g', 'w_in', 'conv_w', 'conv_b',
           'dt_bias', 'a_log', 'd_skip', 'ssd_norm_g', 'w_ssd_proj', 'q_norm_g', 'w_uq', 'kv_norm_g', 'w_uk', 'w_uv',
           'w_mla_proj', 'gate_bias', 'w_out', 'mix_post_g', 'xa_pre_g', 'mem_norm_g', 'w_xq', 'w_xk', 'w_xv', 'w_xo',
           'xa_post_g', 'ffn2_pre_g', 'ffn2_w_gate', 'ffn2_w_up', 'ffn2_w_down', 'ffn2_post_g']


def _pack_rows(w, kind, width):
    m = w[0].T if kind == "col" else w[0]
    return m.reshape(-1, width)


KIND = dict(BIG)
GATHER_PLAN = {
    "first": (["ffn1_w_gate", "ffn1_w_up"], []),
    "ffn1_gate_up": (["ffn1_w_down"], ["w_in#0"]),
    "ffn1_down": ([], ["w_in#1"]),
    "ssd_fwd": (["w_ssd_proj", "w_mla_proj", "w_out", "w_uk", "w_uv"], ["w_uq"]),
    "attn_fwd": (["w_xq", "w_xk", "w_xv", "w_xo", "ffn2_w_gate", "ffn2_w_up", "ffn2_w_down"], []),
}
CONV_RIDES_WITH = "w_in#1"
SCATTER_PLAN = {
    "attn_bwd": [["ffn2_w_gate", "ffn2_w_up", "ffn2_w_down"], ["w_xq", "w_xk", "w_xv", "w_xo"]],
    "ssd_bwd": [["w_ssd_proj", "w_mla_proj", "w_out"], ["w_uk", "w_uv"], ["w_uq"]],
    "in_bwd": [["w_in#0"]],
    "ffn1:down_bwd": [["w_in#1"]],
    "ffn1:dwg": [["ffn1_w_down"]],
    "ffn1:dwu": [["ffn1_w_gate"]],
    "ffn1:gate_up_bwd": [["ffn1_w_up"]],
}
PARTS = {"w_in#0": ("w_in", 0, 2656), "w_in#1": ("w_in", 2656, 5296)}


class Stage:
    def __init__(self, w):
        self.w = w
        self.width = {n: PACK_W if (n, k) in MAIN else FLAT_W for n, k in BIG}
        self.nrows = {n: math.prod(w[n].shape) // self.width[n] for n, _ in BIG}
        self.recv = {}
        self.arrived_parts = {}

    def _rows(self, n):
        return PARTS[n][2] - PARTS[n][1] if n in PARTS else self.nrows[n]

    def _shards(self, tag):
        names_main, names_flat = GATHER_PLAN[tag]

        def pack(n):
            base, r0, r1 = PARTS.get(n, (n, 0, None))
            return _pack_rows(self.w[base], KIND[base], self.width[base])[r0:r1].astype(BF16)
        shards = []
        if names_main:
            shards.append(jnp.concatenate([pack(n) for n in names_main], axis=0))
        if names_flat:
            pieces = [pack(n) for n in names_flat]
            if CONV_RIDES_WITH in names_flat:
                pieces.append(lax.bitcast_convert_type(self.w["conv_w"][0], BF16).reshape(-1, FLAT_W))
            shards.append(_pad_rows(jnp.concatenate(pieces, axis=0), 16))
        return shards

    def gather(self, tag):
        return GatherComm(self._shards(tag)) if tag in GATHER_PLAN else None

    def gathered(self, tag, outs, W, p):
        if tag not in GATHER_PLAN:
            return
        names_main, names_flat = GATHER_PLAN[tag]
        outs = list(outs)
        for names in (names_main, names_flat):
            if not names:
                continue
            buf, r0 = outs.pop(0), 0
            for n in names:
                rows = buf[:, r0:r0 + self._rows(n)]
                r0 += self._rows(n)
                if n in PARTS:
                    self.arrived_parts[n] = rows
                    base = PARTS[n][0]
                    mine = sorted(pn for pn in PARTS if PARTS[pn][0] == base)
                    if not all(pn in self.arrived_parts for pn in mine):
                        continue
                    n, rows = base, jnp.concatenate([self.arrived_parts[pn] for pn in mine], axis=1)
                K = self.w[n].shape[1] if KIND[n] == "col" else PACK_W
                W[n] = rows.reshape(-1, K)
            if names is names_flat and CONV_RIDES_WITH in names:
                cw = self.w["conv_w"]
                nbits = 2 * math.prod(cw.shape) // FLAT_W
                bits = buf[:, r0:r0 + nbits].reshape((N_DEV,) + cw.shape[1:] + (2,))
                p["conv_w"] = lax.bitcast_convert_type(bits, F32).transpose(1, 0, 2).reshape(cw.shape[1], -1)

    def scatter(self, tag, gw):
        if tag not in SCATTER_PLAN:
            return None
        def piece(n):
            if n in PARTS:
                base, r0, r1 = PARTS[n]
                return gw[base].reshape(N_DEV, self.nrows[base], self.width[base])[:, r0:r1]
            return gw[n].reshape(N_DEV, self.nrows[n], self.width[n])
        return ScatterComm([[piece(n) for n in names] for names in SCATTER_PLAN[tag]])

    def scattered(self, tag, outs):
        if tag in SCATTER_PLAN:
            self.recv[tag] = outs


def _pad_rows(a, mult):
    r = (-a.shape[0]) % mult
    return a if r == 0 else jnp.concatenate([a, jnp.zeros((r,) + a.shape[1:], a.dtype)], axis=0)


def _pack_small(vals, loss_row=None, conv_w=None):
    rows = []
    for v in vals:
        f = v.reshape(-1)
        f = jnp.concatenate([f, jnp.zeros(((-f.shape[0]) % 128,), F32)])
        rows.append(f.reshape(-1, 128))
    if conv_w is not None:
        rows.append(conv_w.reshape(-1, 128))
    if loss_row is not None:
        rows.append(loss_row)
    return _pad_rows(jnp.concatenate(rows, axis=0), 8)


def _unpack_small(buf, shapes):
    out, r = [], 0
    for shp in shapes:
        n = math.prod(shp)
        nr = -(-n // 128)
        out.append(buf[r:r + nr].reshape(-1)[:n].reshape(shp))
        r += nr
    return out, r


def _tn(a, b, name, out_dtype=BF16, comm=None):
    M, N = a.shape[1], b.shape[1]
    T = a.shape[0]
    tm = M if M <= 1536 else M // 2
    tk = 1024 if T % 1024 == 0 and T > 1024 else None
    res = mm([[(a, b, "tn")]], [out_dtype], name, tm=tm, tn=N, tk=tk, comm=comm)
    return res[0] if comm is None else (res[0], res[1:])


class NoStage:
    def gather(self, tag):
        return None

    def gathered(self, tag, outs, W, p):
        pass

    def scatter(self, tag, gw):
        return None

    def scattered(self, tag, outs):
        pass


def _ffn_fwd(x, gpre, gpost, W, p, tag, stage):
    h = rms_fwd(x, gpre, tag + "_pre")

    def swi(accs, ex):
        sil, dsil = _silu_parts(accs[0])
        return sil, accs[1] * dsil, sil * accs[1]
    G, U, A, *arrived = mm([[(h, W[tag + "_w_gate"], "nt")], [(h, W[tag + "_w_up"], "nt")]], [BF16, BF16, BF16], tag + "_gate_up",
                           tn=DFF // 2, epi=swi, comm=stage.gather(tag + "_gate_up"), sub=4 if h.shape[0] % 1024 == 0 else 1)
    stage.gathered(tag + "_gate_up", arrived, W, p)
    H, y, *arrived = mm_resid(A, W[tag + "_w_down"], x, gpost, FFN_RES, tag + "_down", comm=stage.gather(tag + "_down"))
    stage.gathered(tag + "_down", arrived, W, p)
    return y, (x, h, G, U, A, H)


def _ffn_bwd(dy, saved, gpre, gpost, wg_t, wu_t, wd, tag, stage, gw):
    x, h, G, U, A, H = saved
    dH, dgpost = resid_bwd(H, gpost, dy, FFN_RES, tag + "_post_bwd")

    def dswi(accs, ex):
        return accs[0] * ex[1], accs[0] * ex[0]

    def hosted(where, call):
        comm = stage.scatter(tag + ":" + where, gw)
        res = call(comm)
        if comm is None:
            return res
        stage.scattered(tag + ":" + where, res[1])
        return res[0]

    res = hosted("down_bwd", lambda comm: (lambda r: r if comm is None else (r[:2], r[2:]))(
        mm([[(dH, wd, "nt")]], [BF16, BF16], tag + "_down_bwd", tn=DFF // 2, epi=dswi, extras=[G, U], comm=comm,
           sub=4 if dH.shape[0] % 1024 == 0 else 1)))
    dG, dU = res
    gw[tag + "_w_down"] = _tn(A, dH, tag + "_dwd")
    gw[tag + "_w_gate"] = hosted("dwg", lambda comm: _tn(dG, h, tag + "_dwg", comm=comm))
    gw[tag + "_w_up"] = hosted("dwu", lambda comm: _tn(dU, h, tag + "_dwu", comm=comm))
    dx, dgpre = hosted("gate_up_bwd", lambda comm: (lambda r: r[:2] if comm is None else (r[:2], r[2:]))(
        mm_rms_bwd([(dG, wg_t, "nn"), (dU, wu_t, "nn")], x, gpre, tag + "_gate_up_bwd", resid=dy, comm=comm)))
    return dx, dgpre, dgpost


def _rope_tables(positions):
    inv = ROPE_THETA ** (-jnp.arange(0, ROPE, 2, dtype=F32) / ROPE)
    ang = positions.astype(F32).reshape(-1)[:, None] * inv
    return jnp.cos(ang), jnp.sin(ang)


def _local_step(x, mem, positions, tgt, W, p, stage=None):
    stage = stage or NoStage()
    nseq = x.shape[0]
    T = nseq * x.shape[1]
    x0 = x.reshape(T, D)
    mem2 = mem.reshape(-1, D)
    cos, sin = _rope_tables(positions)

    x1, ffn1 = _ffn_fwd(x0, p["ffn1_pre_g"], p["ffn1_post_g"], W, p, "ffn1", stage)

    w_in_t = W["w_in"]
    bounds = [0]
    for n in (SSD_INNER, CONV_CH, SSD_H, QR, KVR, ROPE, 2 * D):
        bounds.append(bounds[-1] + n)
    wt_z, wt_xbc, wt_dt, wt_q, wt_kv, wt_kr, wt_gate = [w_in_t[bounds[i]:bounds[i + 1]] for i in range(7)]
    wt_dt, wt_kr = _pad_rows(wt_dt, SLOT), _pad_rows(wt_kr, SLOT)
    wt_dtkr = jnp.concatenate([wt_dt, wt_kr], axis=0)
    hm = rms_fwd(x1, p["mix_pre_g"], "mix_pre")
    z = mm1(hm, wt_z, "nt", BF16, "in_z")
    xbc = mm1(hm, wt_xbc, "nt", BF16, "in_xbc")
    q_c = mm1(hm, wt_q, "nt", F32, "in_q", tn=QR)
    kv_c = mm1(hm, wt_kv, "nt", F32, "in_kv")
    dtkr = mm1(hm, wt_dtkr, "nt", F32, "in_dtkr")
    gl = mm1(hm, wt_gate, "nt", BF16, "in_gate")

    xbc_act = conv_fwd(xbc, p["conv_w"], p["conv_b"], nseq)
    y_ssd_core, prev, *arrived = ssd_fwd(xbc_act, dtkr, p["dt_bias"], p["a_log"], p["d_skip"], nseq, comm=stage.gather("ssd_fwd"))
    stage.gathered("ssd_fwd", arrived, W, p)
    yn = gated_norm_fwd(y_ssd_core, z, p["ssd_norm_g"], "ssd_norm")
    y_ssd = mm1(yn, W["w_ssd_proj"], "nn", BF16, "ssd_proj")

    slot_rows = lambda wt, per: jnp.pad(wt.reshape(MLA_H, per, -1), ((0, 0), (0, SLOT - per), (0, 0))).reshape(MLA_H * SLOT, -1)
    wq_s, wk_s, wv_s = slot_rows(W["w_uq"], QK), slot_rows(W["w_uk"], NOPE), slot_rows(W["w_uv"], VD)
    wo_s = slot_rows(W["w_mla_proj"], VD)
    qn = rms_fwd(q_c, p["q_norm_g"], "q_norm")
    q_s = mm1(qn, wq_s, "nt", BF16, "uq")
    kvn = rms_fwd(kv_c, p["kv_norm_g"], "kv_norm")
    kn_s = mm1(kvn, wk_s, "nt", BF16, "uk")
    v_s = mm1(kvn, wv_s, "nt", BF16, "uv")
    cos16, sin16 = cos, sin
    Qc, Kc = rope_slot_fwd(q_s, kn_s, dtkr, cos16, sin16, "rope")
    o_s, lse, *arrived = attn_slot_fwd(Qc, Kc, v_s, nseq, comm=stage.gather("attn_fwd"))
    stage.gathered("attn_fwd", arrived, W, p)
    y_mla = mm1(o_s, wo_s, "nn", BF16, "mla_proj")

    merged = merge_fwd(gl, y_ssd, y_mla, p["gate_bias"], "merge")
    hmix, x2 = mm_resid(merged, W["w_out"], x1, p["mix_post_g"], 1.0, "mix_out")

    hq = rms_fwd(x2, p["xa_pre_g"], "xa_pre")
    mn = rms_fwd(mem2, p["mem_norm_g"], "mem_norm")
    xq = mm1(hq, W["w_xq"], "nn", BF16, "xq")
    xk = mm1(mn, W["w_xk"], "nn", BF16, "xk")
    xv = mm1(mn, W["w_xv"], "nn", BF16, "xv")
    xo = xattn_fwd(xq, xk, xv, nseq)
    ho, x3 = mm_resid(xo, W["w_xo"], x2, p["xa_post_g"], 1.0, "xo")

    x4, ffn2 = _ffn_fwd(x3, p["ffn2_pre_g"], p["ffn2_post_g"], W, p, "ffn2", stage)
    dx4, loss_row = loss_head(x4, tgt.reshape(T, D), "loss")

    gw, gs = {}, {}
    dx3, gs["ffn2_pre_g"], gs["ffn2_post_g"] = _ffn_bwd(
        dx4, ffn2, p["ffn2_pre_g"], p["ffn2_post_g"], W["ffn2_w_gate"], W["ffn2_w_up"], W["ffn2_w_down"], "ffn2", stage, gw)

    dho, gs["xa_post_g"] = resid_bwd(ho, p["xa_post_g"], dx3, 1.0, "xa_post_bwd")
    dxo = mm1(dho, W["w_xo"], "nt", BF16, "xo_bwd")
    gw["w_xo"] = _tn(xo, dho, "d_w_xo")
    dxq, dxk, dxv = xattn_bwd(xq, xk, xv, dxo, nseq)
    dx2, gs["xa_pre_g"] = mm_rms_bwd([(dxq, W["w_xq"], "nt")], x2, p["xa_pre_g"], "xq_bwd", resid=dx3)
    gw["w_xq"] = _tn(hq, dxq, "d_w_xq")
    dmn = mm([[(dxk, W["w_xk"], "nt"), (dxv, W["w_xv"], "nt")]], [F32], "xkv_bwd")[0]
    gw["w_xk"] = _tn(mn, dxk, "d_w_xk")
    gw["w_xv"] = _tn(mn, dxv, "d_w_xv")
    _, gs["mem_norm_g"] = rms_bwd(mem2, p["mem_norm_g"], dmn, "mem_norm_bwd", dx_dtype=BF16)

    dhmix, gs["mix_post_g"] = resid_bwd(hmix, p["mix_post_g"], dx2, 1.0, "mix_post_bwd")
    dmerged = mm1(dhmix, W["w_out"], "nt", F32, "mix_out_bwd")
    gw["w_out"] = _tn(merged, dhmix, "d_w_out")
    dys, dym, dgl, gs["gate_bias"] = merge_bwd(gl, y_ssd, y_mla, dmerged, p["gate_bias"], "merge_bwd")

    unslot = lambda g, per: g.reshape(MLA_H, SLOT, -1)[:, :per].reshape(MLA_H * per, -1)
    do_s = mm1(dym, wo_s, "nt", BF16, "mla_proj_bwd")
    gw["w_mla_proj"] = unslot(_tn(o_s, dym, "d_w_mla_proj"), VD)
    dQc, dKc, dv_s, *sent = attn_slot_bwd(Qc, Kc, v_s, o_s, lse, do_s, nseq, comm=stage.scatter("attn_bwd", gw))
    stage.scattered("attn_bwd", sent)
    dq_s, dkn_s, dkr = rope_slot_bwd(dQc, dKc, cos16, sin16, "rope_bwd")
    dq_c, gs["q_norm_g"] = mm_rms_bwd([(dq_s, wq_s, "nn")], q_c, p["q_norm_g"], "uq_bwd", dx_dtype=BF16)
    gw["w_uq"] = unslot(_tn(dq_s, qn, "d_w_uq"), QK)
    dkv_c, gs["kv_norm_g"] = mm_rms_bwd([(dkn_s, wk_s, "nn"), (dv_s, wv_s, "nn")], kv_c, p["kv_norm_g"], "ukv_bwd", dx_dtype=BF16)
    gw["w_uk"] = unslot(_tn(dkn_s, kvn, "d_w_uk"), NOPE)
    gw["w_uv"] = unslot(_tn(dv_s, kvn, "d_w_uv"), VD)

    dyn = mm1(dys, W["w_ssd_proj"], "nt", F32, "ssd_proj_bwd")
    gw["w_ssd_proj"] = _tn(yn, dys, "d_w_ssd_proj")
    dyc, dz, gs["ssd_norm_g"] = gated_norm_bwd(y_ssd_core, z, dyn, p["ssd_norm_g"], "ssd_norm_bwd")
    dxbc_act, ddtr, gs["dt_bias"], gs["a_log"], gs["d_skip"], *sent = ssd_bwd(
        xbc_act, dtkr, p["dt_bias"], p["a_log"], p["d_skip"], prev, dyc, nseq, comm=stage.scatter("ssd_bwd", gw))
    stage.scattered("ssd_bwd", sent)
    dxbc, gs["conv_w"], gs["conv_b"] = conv_bwd(xbc, p["conv_w"], p["conv_b"], dxbc_act, nseq)

    gw["w_in"] = jnp.concatenate([_tn(dz, hm, "d_w_in_z"), _tn(dxbc, hm, "d_w_in_xbc"), _tn(ddtr, hm, "d_w_in_dt")[:SSD_H],
                                  _tn(dq_c, hm, "d_w_in_q"), _tn(dkv_c, hm, "d_w_in_kv"), _tn(dkr, hm, "d_w_in_kr")[:ROPE],
                                  _tn(dgl, hm, "d_w_in_gate")], axis=0)
    dx1, gs["mix_pre_g"], *sent = mm_rms_bwd(
        [(dz, wt_z, "nn"), (dxbc, wt_xbc, "nn"), (ddtr, wt_dt, "nn"), (dq_c, wt_q, "nn"), (dkv_c, wt_kv, "nn"),
         (dkr, wt_kr, "nn"), (dgl, wt_gate, "nn")], x1, p["mix_pre_g"], "in_bwd", resid=dx2, comm=stage.scatter("in_bwd", gw))
    stage.scattered("in_bwd", sent)

    dx0, gs["ffn1_pre_g"], gs["ffn1_post_g"] = _ffn_bwd(
        dx1, ffn1, p["ffn1_pre_g"], p["ffn1_post_g"], W["ffn1_w_gate"], W["ffn1_w_up"], W["ffn1_w_down"], "ffn1", stage, gw)
    return loss_row, dx0.reshape(x.shape), gw, gs


def kernel(x, mem, positions, ffn1_pre_g, ffn1_w_gate, ffn1_w_up, ffn1_w_down, ffn1_post_g, mix_pre_g, w_in, conv_w, conv_b, dt_bias, a_log, d_skip, ssd_norm_g, w_ssd_proj, q_norm_g, w_uq, kv_norm_g, w_uk, w_uv, w_mla_proj, gate_bias, w_out, mix_post_g, xa_pre_g, mem_norm_g, w_xq, w_xk, w_xv, w_xo, xa_post_g, ffn2_pre_g, ffn2_w_gate, ffn2_w_up, ffn2_w_down, ffn2_post_g, loss_target, m_ffn1_pre_g, m_ffn1_w_gate, m_ffn1_w_up, m_ffn1_w_down, m_ffn1_post_g, m_mix_pre_g, m_w_in, m_conv_w, m_conv_b, m_dt_bias, m_a_log, m_d_skip, m_ssd_norm_g, m_w_ssd_proj, m_q_norm_g, m_w_uq, m_kv_norm_g, m_w_uk, m_w_uv, m_w_mla_proj, m_gate_bias, m_w_out, m_mix_post_g, m_xa_pre_g, m_mem_norm_g, m_w_xq, m_w_xk, m_w_xv, m_w_xo, m_xa_post_g, m_ffn2_pre_g, m_ffn2_w_gate, m_ffn2_w_up, m_ffn2_w_down, m_ffn2_post_g, v_ffn1_pre_g, v_ffn1_w_gate, v_ffn1_w_up, v_ffn1_w_down, v_ffn1_post_g, v_mix_pre_g, v_w_in, v_conv_w, v_conv_b, v_dt_bias, v_a_log, v_d_skip, v_ssd_norm_g, v_w_ssd_proj, v_q_norm_g, v_w_uq, v_kv_norm_g, v_w_uk, v_w_uv, v_w_mla_proj, v_gate_bias, v_w_out, v_mix_post_g, v_xa_pre_g, v_mem_norm_g, v_w_xq, v_w_xk, v_w_xv, v_w_xo, v_xa_post_g, v_ffn2_pre_g, v_ffn2_w_gate, v_ffn2_w_up, v_ffn2_w_down, v_ffn2_post_g):
    a = dict(locals())
    w = {n: a[n] for n in WEIGHTS}
    m = {n: a["m_" + n] for n in WEIGHTS}
    v = {n: a["v_" + n] for n in WEIGHTS}

    stage = Stage(w)
    W, p = {}, {n: w[n] for n in SMALL}
    stage.gathered("first", run_comm(stage.gather("first"), "allgather_first"), W, p)

    loss_row, grad_x, gw, gs = _local_step(x, mem, positions, loss_target, W, p, stage)

    sm = _pack_small([gs[n] for n in SMALL], loss_row=loss_row, conv_w=gs["conv_w"])
    srecv, = run_comm(ScatterComm([[jnp.broadcast_to(sm[None], (N_DEV,) + sm.shape)]]), "exchange_small")
    s_rows = sum_slots(srecv, "sum_small", tr=sm.shape[0])
    grads, delta, new_m, new_v = {}, {}, {}, {}

    def finish(n, buf, piece):
        col = KIND[n] == "col"
        turn = (lambda t: t.T) if col else (lambda t: t)
        K = w[n].shape[1]
        if col and buf.shape[2] != K:
            buf = buf.reshape(buf.shape[0], -1, K)
        res = adamw_from_slots(buf, piece, turn(w[n][0]), turn(m[n][0]), turn(v[n][0]), "adamw_" + n)
        grads[n], delta[n], new_m[n], new_v[n] = [turn(r)[None] for r in res]

    parts = {}
    for tag, groups in SCATTER_PLAN.items():
        for names, buf in zip(groups, stage.recv[tag]):
            for piece, n in enumerate(names):
                if n in PARTS:
                    parts[n] = sum_slots(buf, "sum_" + n.replace("#", "_"), tr=buf.shape[1])
                else:
                    finish(n, buf, piece)
    for base in sorted({b for b, _, _ in PARTS.values()}):
        rows = jnp.concatenate([parts[pn] for pn in sorted(PARTS) if PARTS[pn][0] == base], axis=0)
        finish(base, rows[None], 0)
    conv_w_full = p["conv_w"]
    small_g, r1 = _unpack_small(s_rows, [w[n].shape for n in SMALL])
    for n, g in zip(SMALL, small_g):
        grads[n] = g
    ncw = math.prod(conv_w_full.shape) // 128
    cw_grad_full = s_rows[r1:r1 + ncw].reshape(conv_w_full.shape)
    wsh = conv_w.shape[2]
    grads["conv_w"] = lax.dynamic_slice_in_dim(cw_grad_full, _dev_index() * wsh, wsh, axis=1)[None]
    loss = s_rows[r1 + ncw, 0]

    d_, m_, v_ = adamw(conv_w[0], grads["conv_w"][0], m["conv_w"][0], v["conv_w"][0], "adamw_conv_w")
    delta["conv_w"], new_m["conv_w"], new_v["conv_w"] = d_[None], m_[None], v_[None]
    sp =[_pack_small([t[n] for n in SMALL]) for t in (w, grads, m, v)]
    outs = adamw(sp[0], sp[1], sp[2], sp[3], "adamw_small")
    for t, buf in zip((delta, new_m, new_v), outs):
        vals, _ = _unpack_small(buf, [w[n].shape for n in SMALL])
        for n, val in zip(SMALL, vals):
            t[n] = val
    return (loss, grad_x, *[grads[n] for n in WEIGHTS], *[delta[n] for n in WEIGHTS],
            *[new_m[n] for n in WEIGHTS], *[new_v[n] for n in WEIGHTS])
```

```python
import functools
import math

import jax
import jax.numpy as jnp
from jax import lax
from jax.experimental import pallas as pl
from jax.experimental.pallas import tpu as pltpu

F32, BF16 = jnp.float32, jnp.bfloat16
HI = lax.Precision.HIGHEST
MESH = pl.DeviceIdType.MESH
N_DEV = 8

D = 1024
DFF = 2816
SSD_H, SSD_P, SSD_G, SSD_N, SSD_L = 16, 64, 2, 128, 128
SSD_INNER = SSD_H * SSD_P
CONV_K, CONV_CH = 4, 1536
MLA_H, QR, KVR, NOPE, ROPE, VD = 16, 384, 256, 64, 32, 64
QK = NOPE + ROPE
ROPE_THETA = 10000.0
XA_H, XA_D = 4, 256
EPS = 1e-6
FFN_RES = 0.5
LR, B1, B2, AEPS, WD, STEP = 0.001, 0.9, 0.999, 1e-08, 0.01, 10

VMEM_LIMIT = 56 * 2**20


def _cp(*sem):
    return pltpu.CompilerParams(dimension_semantics=sem, vmem_limit_bytes=VMEM_LIMIT)


def _sigmoid(x):
    return 1.0 / (1.0 + jnp.exp(-x))


def _softplus(x):
    return jnp.where(x > 20.0, x, jnp.log(1.0 + jnp.exp(jnp.minimum(x, 20.0))))


def _dot(a, b, dims="nn"):
    ca = 0 if dims[0] == "t" else 1
    cb = 1 if dims[1] == "t" else 0
    return lax.dot_general(a.astype(BF16), b.astype(BF16), (((ca,), (cb,)), ((), ())), preferred_element_type=F32)


def _dot_sel(a, b, dims="nn", split="a", terms=3):
    r = (a if split == "a" else b).astype(F32)
    out = None
    for t in range(terms):
        piece = r.astype(BF16)
        if t + 1 < terms:
            r = r - piece.astype(F32)
        d = _dot(piece, b, dims) if split == "a" else _dot(a, piece, dims)
        out = d if out is None else out + d
    return out


def _ssd_common(dtr, dtb, alog):
    L = dtr.shape[0]
    dt = _softplus(dtr + dtb)
    a = -jnp.exp(alog)
    adt = dt * a
    r = lax.broadcasted_iota(jnp.int32, (L, L), 0)
    c = lax.broadcasted_iota(jnp.int32, (L, L), 1)
    lower = r >= c
    tri = lower.astype(F32)
    cs = _dot_sel(tri, adt, "nn", split="b")
    cs_t = _dot_sel(adt, tri, "tt")
    return dt, a, cs, cs_t, lower


def _head_expand():
    hh = lax.broadcasted_iota(jnp.int32, (SSD_H, SSD_INNER), 0)
    jj = lax.broadcasted_iota(jnp.int32, (SSD_H, SSD_INNER), 1)
    return ((jj >= hh * SSD_P) & (jj < hh * SSD_P + SSD_P)).astype(F32)


def _head_reduce():
    hh = lax.broadcasted_iota(jnp.int32, (SSD_INNER, SSD_H), 1)
    jj = lax.broadcasted_iota(jnp.int32, (SSD_INNER, SSD_H), 0)
    return ((jj >= hh * SSD_P) & (jj < hh * SSD_P + SSD_P)).astype(F32)


def ssd_fwd(xbc, dtr, dtb, alog, dsk, nseq, comm=None):
    T = xbc.shape[0]
    S = T // nseq
    C = S // SSD_L
    L = SSD_L
    NP = SSD_H // 2

    def body(x_ref, b_ref, c_ref, dtr_ref, dtb_ref, alog_ref, dsk_ref, y_ref, prev_ref, st_ref):
        ci = pl.program_id(1)

        @pl.when(ci == 0)
        def _():
            st_ref[...] = jnp.zeros_like(st_ref)

        dt, a, cs, cs_t, lower = _ssd_common(dtr_ref[:, 0:SSD_H], dtb_ref[...], alog_ref[...])
        E = _head_expand()
        X = x_ref[...].astype(F32)
        dt_e = _dot_sel(dt, E)
        cs_e = _dot_sel(cs, E)
        csl_e = cs_e[L - 1:L, :]
        Xd = X * dt_e
        Xf = Xd * jnp.exp(csl_e - cs_e)
        e_e = jnp.exp(cs_e)
        skip = _dot_sel(dsk_ref[...], E) * X
        lane = lax.broadcasted_iota(jnp.int32, (1, 2 * SSD_P), 1)
        rowp = lax.broadcasted_iota(jnp.int32, (2 * SSD_P, 1), 0)
        for g in range(SSD_G):
            Bg = b_ref[:, g * SSD_N:(g + 1) * SSD_N]
            Cg = c_ref[:, g * SSD_N:(g + 1) * SSD_N]
            cb = _dot(Cg, Bg, "nt")
            for pp in range(NP // SSD_G):
                p = g * (NP // SSD_G) + pp
                sl = slice(p * 2 * SSD_P, (p + 1) * 2 * SSD_P)
                Xd_p = Xd[:, sl]
                yd = jnp.zeros((L, 2 * SSD_P), F32)
                for q in range(2):
                    h = 2 * p + q
                    m = jnp.where(lower, jnp.exp(jnp.minimum(cs[:, h:h + 1] - cs_t[h:h + 1, :], 0.0)), 0.0)
                    mask = (lane >= q * SSD_P) & (lane < (q + 1) * SSD_P)
                    yd = yd + _dot(cb * m, jnp.where(mask, Xd_p, 0.0))
                S0 = st_ref[p]
                prev_ref[0, 0, p] = S0
                z = _dot(Cg, S0, "nt")
                y_ref[:, sl] = (skip[:, sl] + yd + z * e_e[:, sl]).astype(y_ref.dtype)
                h0 = 2 * p
                dec = jnp.where(rowp < SSD_P, jnp.exp(cs[L - 1:L, h0:h0 + 1]), jnp.exp(cs[L - 1:L, h0 + 1:h0 + 2]))
                st_ref[p] = S0 * dec + _dot(Xf[:, sl], Bg, "tn")

    row = lambda b, c: (b * C + c, 0)
    small = pl.BlockSpec((1, SSD_H), lambda b, c: (0, 0))
    return _call_with_comm(
        body, (nseq, C), "ssd_fwd",
        [pl.BlockSpec((L, SSD_INNER), row),
         pl.BlockSpec((L, SSD_G * SSD_N), lambda b, c: (b * C + c, SSD_INNER // (SSD_G * SSD_N))),
         pl.BlockSpec((L, SSD_G * SSD_N), lambda b, c: (b * C + c, SSD_INNER // (SSD_G * SSD_N) + 1)),
         pl.BlockSpec((L, 128), row), small, small, small],
        [xbc, xbc, xbc, dtr, dtb, alog, dsk],
        [pl.BlockSpec((L, SSD_INNER), row), pl.BlockSpec((1, 1, NP, 2 * SSD_P, SSD_N), lambda b, c: (b, c, 0, 0, 0))],
        [jax.ShapeDtypeStruct((T, SSD_INNER), BF16), jax.ShapeDtypeStruct((nseq, C, NP, 2 * SSD_P, SSD_N), F32)],
        comm, scratch=[pltpu.VMEM((NP, 2 * SSD_P, SSD_N), F32)], sem=("parallel", "arbitrary"))


def ssd_bwd(xbc, dtr, dtb, alog, dsk, prev, dy, nseq, comm=None):
    T = xbc.shape[0]
    S = T // nseq
    C = S // SSD_L
    L = SSD_L
    NP = SSD_H // 2

    def body(x_ref, b_ref, c_ref, dtr_ref, dtb_ref, alog_ref, dsk_ref, prev_ref, dy_ref,
             dxbc_ref, ddtr_ref, ddtb_ref, dalog_ref, ddsk_ref, ds_ref, stg_ref):
        bi = pl.program_id(0)
        ci = pl.program_id(1)

        @pl.when(ci == 0)
        def _():
            ds_ref[...] = jnp.zeros_like(ds_ref)

        @pl.when((ci == 0) & (bi == 0))
        def _():
            ddtb_ref[...] = jnp.zeros_like(ddtb_ref)
            dalog_ref[...] = jnp.zeros_like(dalog_ref)
            ddsk_ref[...] = jnp.zeros_like(ddsk_ref)

        dtr = dtr_ref[:, 0:SSD_H]
        dtb = dtb_ref[...]
        dt, a, cs, cs_t, lower = _ssd_common(dtr, dtb, alog_ref[...])
        upper = lax.broadcasted_iota(jnp.int32, (L, L), 1) >= lax.broadcasted_iota(jnp.int32, (L, L), 0)
        E = _head_expand()
        ET = _head_reduce()
        X = x_ref[...].astype(F32)
        dY = dy_ref[...].astype(F32)
        dt_e = _dot_sel(dt, E)
        cs_e = _dot_sel(cs, E)
        csl_e = cs_e[L - 1:L, :]
        f_e = jnp.exp(csl_e - cs_e)
        e_e = jnp.exp(cs_e)
        dsk_e = _dot_sel(dsk_ref[...], E)
        Xd = X * dt_e
        Xf = Xd * f_e
        lane = lax.broadcasted_iota(jnp.int32, (1, 2 * SSD_P), 1)
        rowp = lax.broadcasted_iota(jnp.int32, (2 * SSD_P, 1), 0)
        hsel = lax.broadcasted_iota(jnp.int32, (1, SSD_H), 1)
        dcs = jnp.zeros((L, SSD_H), F32)
        dcsl = jnp.zeros((1, SSD_H), F32)
        for g in range(SSD_G):
            Bg = b_ref[:, g * SSD_N:(g + 1) * SSD_N]
            Cg = c_ref[:, g * SSD_N:(g + 1) * SSD_N]
            cb = _dot(Cg, Bg, "nt")
            cbt = _dot(Bg, Cg, "nt")
            dB = jnp.zeros((L, SSD_N), F32)
            dC = jnp.zeros((L, SSD_N), F32)
            for pp in range(NP // SSD_G):
                p = g * (NP // SSD_G) + pp
                sl = slice(p * 2 * SSD_P, (p + 1) * 2 * SSD_P)
                Xd_p = Xd[:, sl]
                dY_p = dY[:, sl]
                dXd_p = jnp.zeros((L, 2 * SSD_P), F32)
                for q in range(2):
                    h = 2 * p + q
                    mask = (lane >= q * SSD_P) & (lane < (q + 1) * SSD_P)
                    col = cs[:, h:h + 1]
                    rw = cs_t[h:h + 1, :]
                    m = jnp.where(lower, jnp.exp(jnp.minimum(col - rw, 0.0)), 0.0)
                    mt = jnp.where(upper, jnp.exp(jnp.minimum(rw - col, 0.0)), 0.0)
                    dYm = jnp.where(mask, dY_p, 0.0)
                    dW = _dot(dYm, Xd_p, "nt")
                    dWt = _dot(Xd_p, dYm, "nt")
                    w = cb * m
                    wt = cbt * mt
                    dC = dC + _dot(dW * m, Bg)
                    dB = dB + _dot(dWt * mt, Cg)
                    dXd_p = dXd_p + jnp.where(mask, _dot(wt, dY_p), 0.0)
                    qcol = jnp.sum(dW * w, axis=1, keepdims=True) - jnp.sum(dWt * wt, axis=1, keepdims=True)
                    dcs = dcs + qcol * (hsel == h).astype(F32)
                S0 = prev_ref[0, 0, p]
                dSn = ds_ref[p]
                dZ = dY_p * e_e[:, sl]
                dC = dC + _dot(dZ, S0)
                h0 = 2 * p
                el0 = jnp.exp(cs[L - 1:L, h0:h0 + 1])
                el1 = jnp.exp(cs[L - 1:L, h0 + 1:h0 + 2])
                dec = jnp.where(rowp < SSD_P, el0, el1)
                ds_ref[p] = dSn * dec + _dot(dZ, Cg, "tn")
                dXf_p = _dot(Bg, dSn, "nt")
                dB = dB + _dot(Xf[:, sl], dSn)
                rs = jnp.sum(dSn * S0, axis=1, keepdims=True)
                s0 = jnp.sum(jnp.where(rowp < SSD_P, rs, 0.0), axis=0, keepdims=True) * el0
                s1 = jnp.sum(jnp.where(rowp >= SSD_P, rs, 0.0), axis=0, keepdims=True) * el1
                dcsl = dcsl + s0 * (hsel == h0).astype(F32) + s1 * (hsel == h0 + 1).astype(F32)
                y_off = _dot(Cg, S0, "nt") * e_e[:, sl]
                t1 = dY_p * y_off - dXf_p * Xf[:, sl]
                r1 = jnp.where(lane < SSD_P, t1, 0.0)
                c0 = jnp.sum(r1, axis=1, keepdims=True)
                c1 = jnp.sum(t1 - r1, axis=1, keepdims=True)
                dcs = dcs + c0 * (hsel == h0).astype(F32) + c1 * (hsel == h0 + 1).astype(F32)
                t2 = dXf_p * Xf[:, sl]
                r2 = jnp.where(lane < SSD_P, t2, 0.0)
                dcsl = dcsl + jnp.sum(r2, keepdims=True) * (hsel == h0).astype(F32) \
                    + jnp.sum(t2 - r2, keepdims=True) * (hsel == h0 + 1).astype(F32)
                stg_ref[:, sl] = dXd_p + dXf_p * f_e[:, sl]
            dxbc_ref[:, SSD_INNER + g * SSD_N:SSD_INNER + (g + 1) * SSD_N] = dB.astype(dxbc_ref.dtype)
            dxbc_ref[:, SSD_INNER + (SSD_G + g) * SSD_N:SSD_INNER + (SSD_G + g + 1) * SSD_N] = dC.astype(dxbc_ref.dtype)
        dXd = stg_ref[...]
        dxbc_ref[:, 0:SSD_INNER] = (dXd * dt_e + dsk_e * dY).astype(dxbc_ref.dtype)
        rowl = lax.broadcasted_iota(jnp.int32, (L, 1), 0)
        dcs = dcs + jnp.where(rowl == L - 1, dcsl, 0.0)
        dalpha = _dot_sel(upper.astype(F32), dcs, split="b")
        ddt = _dot_sel(dXd * X, ET, terms=2) + dalpha * a
        dalog_ref[...] += jnp.sum(dalpha * dt, axis=0, keepdims=True) * a
        ddtr = ddt * _sigmoid(dtr + dtb)
        spread = (lax.broadcasted_iota(jnp.int32, (SSD_H, 128), 0) == lax.broadcasted_iota(jnp.int32, (SSD_H, 128), 1)).astype(F32)
        ddtr_ref[...] = _dot(ddtr, spread).astype(ddtr_ref.dtype)
        ddtb_ref[...] += jnp.sum(ddtr, axis=0, keepdims=True)
        ddsk_ref[...] += jnp.sum(_dot_sel(dY * X, ET, terms=2), axis=0, keepdims=True)

    rowr = lambda b, c: (b * C + (C - 1 - c), 0)
    small = pl.BlockSpec((1, SSD_H), lambda b, c: (0, 0))
    return _call_with_comm(
        body, (nseq, C), "ssd_bwd",
        [pl.BlockSpec((L, SSD_INNER), rowr),
         pl.BlockSpec((L, SSD_G * SSD_N), lambda b, c: (b * C + (C - 1 - c), SSD_INNER // (SSD_G * SSD_N))),
         pl.BlockSpec((L, SSD_G * SSD_N), lambda b, c: (b * C + (C - 1 - c), SSD_INNER // (SSD_G * SSD_N) + 1)),
         pl.BlockSpec((L, 128), rowr), small, small, small,
         pl.BlockSpec((1, 1, NP, 2 * SSD_P, SSD_N), lambda b, c: (b, C - 1 - c, 0, 0, 0)),
         pl.BlockSpec((L, SSD_INNER), rowr)],
        [xbc, xbc, xbc, dtr, dtb, alog, dsk, prev, dy],
        [pl.BlockSpec((L, CONV_CH), rowr), pl.BlockSpec((L, 128), rowr), small, small, small],
        [jax.ShapeDtypeStruct((T, CONV_CH), BF16), jax.ShapeDtypeStruct((T, 128), BF16),
         jax.ShapeDtypeStruct((1, SSD_H), F32), jax.ShapeDtypeStruct((1, SSD_H), F32), jax.ShapeDtypeStruct((1, SSD_H), F32)],
        comm, scratch=[pltpu.VMEM((NP, 2 * SSD_P, SSD_N), F32), pltpu.VMEM((L, SSD_INNER), F32)], sem=("arbitrary", "arbitrary"))


SLOT = 128
ATT_T = 512
LOG2E = math.log2(math.e)
Q_SCALE = QK ** -0.5 * LOG2E


def _col_to_row(col):
    n = col.shape[0]
    eye = lax.broadcasted_iota(jnp.int32, (n, n), 0) == lax.broadcasted_iota(jnp.int32, (n, n), 1)
    return jnp.sum(jnp.where(eye, col, 0.0), axis=0, keepdims=True)


def attn_slot_fwd(q, k, v, nseq, comm=None):
    T = q.shape[0]
    S = T // nseq
    t = min(ATT_T, S)
    nb = S // t

    def body(q_ref, k_ref, v_ref, o_ref, lse_ref):
        causal = lax.broadcasted_iota(jnp.int32, (t, t), 1) <= lax.broadcasted_iota(jnp.int32, (t, t), 0)
        for qi in range(nb):
            qb = q_ref[qi * t:(qi + 1) * t, :]
            m = l = acc = None
            for kj in range(qi + 1):
                s = _dot(qb, k_ref[kj * t:(kj + 1) * t, :], "nt")
                if kj == qi:
                    s = jnp.where(causal, s, -1e30)
                bm = jnp.max(s, axis=1, keepdims=True)
                if kj == 0:
                    m = bm
                    p = jnp.exp2(s - m)
                    l = jnp.sum(p, axis=1, keepdims=True)
                    acc = _dot(p, v_ref[0:t, :])
                else:
                    m_new = jnp.maximum(m, bm)
                    corr = jnp.exp2(m - m_new)
                    p = jnp.exp2(s - m_new)
                    l = l * corr + jnp.sum(p, axis=1, keepdims=True)
                    acc = acc * corr + _dot(p, v_ref[kj * t:(kj + 1) * t, :])
                    m = m_new
            o_ref[qi * t:(qi + 1) * t, :] = (acc / l).astype(o_ref.dtype)
            lse_ref[0, 0, :, qi * t:(qi + 1) * t] = _col_to_row(m + jnp.log2(l))

    blk = pl.BlockSpec((S, SLOT), lambda b, h: (b, h))
    return _call_with_comm(
        body, (nseq, MLA_H), "attn_fwd", [blk, blk, blk], [q, k, v],
        [blk, pl.BlockSpec((1, 1, 1, S), lambda b, h: (b, h, 0, 0))],
        [jax.ShapeDtypeStruct((T, MLA_H * SLOT), BF16), jax.ShapeDtypeStruct((nseq, MLA_H, 1, S), F32)], comm)


def attn_slot_bwd(q, k, v, o, lse, do, nseq, comm=None):
    T = q.shape[0]
    S = T // nseq
    t = min(ATT_T, S)
    nb = S // t
    scale = QK ** -0.5

    def body(q_ref, k_ref, v_ref, o_ref, lse_ref, do_ref, dq_ref, dk_ref, dv_ref, dqa_ref):
        causal_t =lax.broadcasted_iota(jnp.int32, (t, t), 0) <= lax.broadcasted_iota(jnp.int32, (t, t), 1)
        ones = jnp.ones((8, SLOT), F32)
        delta = []
        for qi in range(nb):
            sl = slice(qi * t, (qi + 1) * t)
            prod = do_ref[sl, :].astype(F32) * o_ref[sl, :].astype(F32)
            delta.append(_dot_sel(ones, prod, "nt", split="b", terms=2)[0:1, :])
        for kj in range(nb):
            ks = slice(kj * t, (kj + 1) * t)
            kb = k_ref[ks, :]
            vb = v_ref[ks, :]
            dk = dv = None
            for qi in range(kj, nb):
                sl = slice(qi * t, (qi + 1) * t)
                qb = q_ref[sl, :]
                dob = do_ref[sl, :]
                st = _dot(kb, qb, "nt")
                pt = jnp.exp2(st - lse_ref[0, 0, :, sl])
                if qi == kj:
                    pt = jnp.where(causal_t, pt, 0.0)
                dpt = _dot(vb, dob, "nt")
                dst = (pt * (dpt - delta[qi])).astype(BF16)
                dvc = _dot(pt, dob)
                dkc = _dot(dst, qb) * (1.0 / LOG2E)
                dv = dvc if dv is None else dv + dvc
                dk = dkc if dk is None else dk + dkc
                dqc = _dot(dst, kb, "tn") * scale
                if kj > 0:
                    dqc = dqc + dqa_ref[sl, :]
                if qi == kj:
                    dq_ref[sl, :] = dqc.astype(dq_ref.dtype)
                else:
                    dqa_ref[sl, :] = dqc
            dk_ref[ks, :] = dk.astype(dk_ref.dtype)
            dv_ref[ks, :] = dv.astype(dv_ref.dtype)

    blk = pl.BlockSpec((S, SLOT), lambda b, h: (b, h))
    lse_spec = pl.BlockSpec((1, 1, 1, S), lambda b, h: (b, h, 0, 0))
    W = MLA_H * SLOT
    return _call_with_comm(
        body, (nseq, MLA_H), "attn_bwd", [blk, blk, blk, blk, lse_spec, blk], [q, k, v, o, lse, do], [blk, blk, blk],
        [jax.ShapeDtypeStruct((T, W), BF16)] * 3, comm, scratch=[pltpu.VMEM((S, SLOT), F32)])


def _rope_coeffs(cos, sin):
    half = ROPE // 2
    r = lax.broadcasted_iota(jnp.int32, (half, SLOT), 0)
    c = lax.broadcasted_iota(jnp.int32, (half, SLOT), 1)
    pc = ((c == r + NOPE) | (c == r + NOPE + half)).astype(F32)
    ps = (c == r + NOPE + half).astype(F32) - (c == r + NOPE).astype(F32)
    lane = lax.broadcasted_iota(jnp.int32, (1, SLOT), 1)
    return _dot_sel(cos, pc) + (lane < NOPE).astype(F32), _dot_sel(sin, ps)


def _rope_swap(x):
    W = x.shape[1]
    half = ROPE // 2
    lane = lax.broadcasted_iota(jnp.int32, (1, W), 1) & (SLOT - 1)
    up = pltpu.roll(x, W - half, axis=1)
    dn = pltpu.roll(x, half, axis=1)
    return jnp.where((lane >= NOPE) & (lane < NOPE + half), up, jnp.where((lane >= NOPE + half) & (lane < QK), dn, 0.0))


def rope_slot_fwd(q, kn, dtkr, cos, sin, name):
    def fn(qv, knv, krv, cv, sv):
        C, Sg = _rope_coeffs(cv, sv)
        ct, stl = jnp.tile(C, (1, MLA_H)), jnp.tile(Sg, (1, MLA_H))
        qo = (qv * ct + _rope_swap(qv) * stl) * Q_SCALE
        r = lax.broadcasted_iota(jnp.int32, (SLOT, SLOT), 0)
        c = lax.broadcasted_iota(jnp.int32, (SLOT, SLOT), 1)
        place = ((c == r + NOPE) & (r < ROPE)).astype(F32)
        kr = _dot_sel(krv, place)
        kr = kr * C + _rope_swap(kr) * Sg
        return qo, knv.astype(F32) + jnp.tile(kr, (1, MLA_H))
    W = MLA_H * SLOT
    return rowwise(fn, [q, kn, (dtkr, SLOT, 1), cos, sin], [], [(W, BF16), (W, BF16)], [], name)


def rope_slot_bwd(dq, dk, cos, sin, name):
    def fn(dqv, dkv, cv, sv):
        C, Sg = _rope_coeffs(cv, sv)
        ct, stl = jnp.tile(C, (1, MLA_H)), jnp.tile(Sg, (1, MLA_H))
        dqo = dqv * ct - _rope_swap(dqv) * stl
        tot = dkv[:, 0:SLOT]
        for h in range(1, MLA_H):
            tot = tot + dkv[:, h * SLOT:(h + 1) * SLOT]
        u = tot * C - _rope_swap(tot) * Sg
        r = lax.broadcasted_iota(jnp.int32, (SLOT, SLOT), 0)
        c = lax.broadcasted_iota(jnp.int32, (SLOT, SLOT), 1)
        unplace = ((r == c + NOPE) & (c < ROPE)).astype(F32)
        return dqo, dkv, _dot_sel(u, unplace, terms=2)
    W = MLA_H * SLOT
    return rowwise(fn, [dq, dk, cos, sin], [], [(W, BF16), (W, BF16), (SLOT, BF16)], [], name)


XA_BLK = 512


def xattn_fwd(q, k, v, nseq):
    T = q.shape[0]
    S = T // nseq
    M = k.shape[0] // nseq
    tq = min(XA_BLK, S)
    nq = S // tq
    scale = XA_D ** -0.5

    def body(q_ref, k_ref, v_ref, o_ref):
        s = _dot(q_ref[...], k_ref[...], "nt") * scale
        p = jnp.exp(s - jnp.max(s, axis=1, keepdims=True))
        p = p / jnp.sum(p, axis=1, keepdims=True)
        o_ref[...] = _dot(p, v_ref[...]).astype(o_ref.dtype)

    qs = pl.BlockSpec((tq, XA_D), lambda b, h, i: (b * nq + i, h))
    ks = pl.BlockSpec((M, XA_D), lambda b, h, i: (b, h))
    return pl.pallas_call(
        body, grid=(nseq, XA_H, nq), name="xattn_fwd", in_specs=[qs, ks, ks], out_specs=qs,
        out_shape=jax.ShapeDtypeStruct((T, XA_H * XA_D), BF16),
        compiler_params=_cp("parallel", "parallel", "parallel"),
    )(q, k, v)


def xattn_bwd(q, k, v, do, nseq):
    T = q.shape[0]
    S = T // nseq
    M = k.shape[0] // nseq
    tq = min(XA_BLK, S)
    nq = S // tq
    scale = XA_D ** -0.5

    def body(q_ref, k_ref, v_ref, do_ref, dq_ref, dk_ref, dv_ref):
        @pl.when(pl.program_id(2) == 0)
        def _():
            dk_ref[...] = jnp.zeros_like(dk_ref)
            dv_ref[...] = jnp.zeros_like(dv_ref)

        qb, kb, vb, dob = q_ref[...], k_ref[...], v_ref[...], do_ref[...]
        s = _dot(qb, kb, "nt") * scale
        p = jnp.exp(s - jnp.max(s, axis=1, keepdims=True))
        p = p / jnp.sum(p, axis=1, keepdims=True)
        dp = _dot(dob, vb, "nt")
        ds = p * (dp - jnp.sum(dp * p, axis=1, keepdims=True)) * scale
        dq_ref[...] = _dot(ds, kb).astype(dq_ref.dtype)
        dk_ref[...] += _dot(ds, qb, "tn")
        dv_ref[...] += _dot(p, dob, "tn")

    qs = pl.BlockSpec((tq, XA_D), lambda b, h, i: (b * nq + i, h))
    ks = pl.BlockSpec((M, XA_D), lambda b, h, i: (b, h))
    return pl.pallas_call(
        body, grid=(nseq, XA_H, nq), name="xattn_bwd", in_specs=[qs, ks, ks, qs], out_specs=[qs, ks, ks],
        out_shape=[jax.ShapeDtypeStruct((T, XA_H * XA_D), BF16), jax.ShapeDtypeStruct(k.shape, F32),
                   jax.ShapeDtypeStruct(k.shape, F32)],
        compiler_params=_cp("parallel", "parallel", "arbitrary"),
    )(q, k, v, do)


CONV_BLK = 256


def _shift_down(x, s, rows):
    if s == 0:
        return x
    return jnp.where(rows >= s, pltpu.roll(x, s, axis=0), 0.0)


def _shift_up(x, s, rows):
    if s == 0:
        return x
    S = x.shape[0]
    return jnp.where(rows < S - s, pltpu.roll(x, S - s, axis=0), 0.0)


def conv_fwd(x, w, b, nseq):
    T, CH = x.shape
    S = T // nseq

    def body(x_ref, w_ref, b_ref, o_ref):
        xv = x_ref[...].astype(F32)
        rows = lax.broadcasted_iota(jnp.int32, (S, 1), 0)
        c = jnp.zeros_like(xv) + b_ref[...]
        for kk in range(CONV_K):
            c = c + w_ref[kk:kk + 1, :] * _shift_down(xv, CONV_K - 1 - kk, rows)
        o_ref[...] = (c * _sigmoid(c)).astype(o_ref.dtype)

    xs = pl.BlockSpec((S, CONV_BLK), lambda j, bb: (bb, j))
    return pl.pallas_call(
        body, grid=(CH // CONV_BLK, nseq), name="conv_fwd",
        in_specs=[xs, pl.BlockSpec((CONV_K, CONV_BLK), lambda j, bb: (0, j)), pl.BlockSpec((1, CONV_BLK), lambda j, bb: (0, j))],
        out_specs=xs, out_shape=jax.ShapeDtypeStruct((T, CH), BF16),
        compiler_params=_cp("parallel", "parallel"),
    )(x, w, b)


def conv_bwd(x, w, b, dout, nseq):
    T, CH = x.shape
    S = T // nseq

    def body(x_ref, w_ref, b_ref, do_ref, dx_ref, dw_ref, db_ref):
        @pl.when(pl.program_id(1) == 0)
        def _():
            dw_ref[...] = jnp.zeros_like(dw_ref)
            db_ref[...] = jnp.zeros_like(db_ref)

        xv = x_ref[...].astype(F32)
        rows = lax.broadcasted_iota(jnp.int32, (S, 1), 0)
        c = jnp.zeros_like(xv) + b_ref[...]
        sh = [_shift_down(xv, CONV_K - 1 - kk, rows) for kk in range(CONV_K)]
        for kk in range(CONV_K):
            c = c + w_ref[kk:kk + 1, :] * sh[kk]
        sg = _sigmoid(c)
        dc = do_ref[...].astype(F32) * sg * (1.0 + c * (1.0 - sg))
        dx = jnp.zeros_like(xv)
        for kk in range(CONV_K):
            dx = dx + w_ref[kk:kk + 1, :] * _shift_up(dc, CONV_K - 1 - kk, rows)
            dw_ref[kk:kk + 1, :] += jnp.sum(dc * sh[kk], axis=0, keepdims=True)
        dx_ref[...] = dx.astype(dx_ref.dtype)
        db_ref[...] += jnp.sum(dc, axis=0, keepdims=True)

    xs = pl.BlockSpec((S, CONV_BLK), lambda j, bb: (bb, j))
    ws = pl.BlockSpec((CONV_K, CONV_BLK), lambda j, bb: (0, j))
    bs = pl.BlockSpec((1, CONV_BLK), lambda j, bb: (0, j))
    return pl.pallas_call(
        body, grid=(CH // CONV_BLK, nseq), name="conv_bwd",
        in_specs=[xs, ws, bs, xs], out_specs=[xs, ws, bs],
        out_shape=[jax.ShapeDtypeStruct((T, CH), BF16), jax.ShapeDtypeStruct((CONV_K, CH), F32),
                   jax.ShapeDtypeStruct((1, CH), F32)],
        compiler_params=_cp("parallel", "arbitrary"),
    )(x, w, b, dout)


def _dims(a, b, mode):
    M = a.shape[1] if mode[0] == "t" else a.shape[0]
    K = a.shape[0] if mode[0] == "t" else a.shape[1]
    N = b.shape[0] if mode[1] == "t" else b.shape[1]
    return M, K, N


def _tile(dim, prefs):
    for p in prefs:
        if dim % p == 0:
            return p
    return dim


def mm(groups, out_dtypes, name, tm=None, tn=None, tk=None, epi=None, extras=(), comm=None, sub=1, n_sum=0):
    a0, b0, m0 = groups[0][0]
    M, K0, N = _dims(a0, b0, m0)
    tm = tm or _tile(M, (1024, 512, 256, 128))
    tn = tn or _tile(N, (1024, 512, 256, 128))
    flat = [p for g in groups for p in g]
    nk = 1 if tk is None else K0 // tk
    in_specs, args = [], []
    for a, b, mode in flat:
        _, K, _ = _dims(a, b, mode)
        kb = K if tk is None else tk
        in_specs.append(pl.BlockSpec((kb, tm), lambda i, j, k: (k, i)) if mode[0] == "t"
                        else pl.BlockSpec((tm, kb), lambda i, j, k: (i, k)))
        in_specs.append(pl.BlockSpec((tn, kb), lambda i, j, k: (j, k)) if mode[1] == "t"
                        else pl.BlockSpec((kb, tn), lambda i, j, k: (k, j)))
        args += [a, b]
    for e in extras:
        in_specs.append(pl.BlockSpec((1, tn), lambda i, j, k: (0, j)) if e.shape[0] == 1 and M != 1
                        else pl.BlockSpec((tm, tn), lambda i, j, k: (i, j)))
        args.append(e)
    n_in = len(args)
    n_main = len(out_dtypes)
    n_out = n_main + n_sum
    assert n_sum == 0 or (tn == N and tk is None)
    ng = len(groups)
    sizes = [len(g) for g in groups]

    def body(*refs):
        ins, outs, accs = refs[:n_in], refs[n_in:n_in + n_out], refs[n_in + n_out:]
        kk = pl.program_id(2)

        def dots(rs):
            vals, pos = [], 0
            for gi in range(ng):
                acc = None
                for _ in range(sizes[gi]):
                    mode = flat[pos // 2][2]
                    av = ins[pos][:, rs] if mode[0] == "t" else ins[pos][rs, :]
                    d = _dot(av, ins[pos + 1][...], mode)
                    acc = d if acc is None else acc + d
                    pos += 2
                vals.append(acc)
            return vals

        def finish(accv, rs, first_chunk=True):
            ex = [(r[...] if r.shape[0] == 1 and tm != 1 else r[rs, :]).astype(F32) for r in ins[2 * len(flat):]]
            res = epi(accv, ex) if epi is not None else tuple(accv)
            for o, r in zip(outs[:n_main], res[:n_main]):
                o[rs, :] = r.astype(o.dtype)
            for o, r in zip(outs[n_main:], res[n_main:]):
                if first_chunk:
                    @pl.when(pl.program_id(0) == 0)
                    def _():
                        o[...] = r

                    @pl.when(pl.program_id(0) > 0)
                    def _():
                        o[...] += r
                else:
                    o[...] += r

        if nk == 1:
            for r in range(sub):
                rs = slice(r * (tm // sub), (r + 1) * (tm // sub))
                finish(dots(rs), rs, r == 0)
        else:
            vals = dots(slice(0, tm))
            finish = functools.partial(finish, rs=slice(0, tm))
            @pl.when(kk == 0)
            def _():
                for ar, vv in zip(accs, vals):
                    ar[...] = vv

            @pl.when(kk > 0)
            def _():
                for ar, vv in zip(accs, vals):
                    ar[...] += vv

            @pl.when(kk == nk - 1)
            def _():
                finish([ar[...] for ar in accs])

    grid = (M // tm, N // tn, nk)
    out_specs = [pl.BlockSpec((tm, tn), lambda i, j, k: (i, j)) for _ in out_dtypes] \
        + [pl.BlockSpec((1, tn), lambda i, j, k: (0, j))] * n_sum
    out_shape = [jax.ShapeDtypeStruct((M, N), dt) for dt in out_dtypes] + [jax.ShapeDtypeStruct((1, N), F32)] * n_sum
    scratch = [pltpu.VMEM((tm, tn), F32) for _ in range(ng if nk > 1 else 0)]
    sem = ("arbitrary" if n_sum else "parallel", "parallel", "arbitrary")
    if comm is not None:
        body = _attach(comm, body, n_in, n_out, *_grid_ends(grid))
        in_specs, args = in_specs + [HBM_SPEC] * len(comm.inputs), args + comm.inputs
        out_specs, out_shape = out_specs + [HBM_SPEC] * len(comm.out_shapes), out_shape + comm.out_shapes
        scratch, sem = scratch + comm.sems, ("arbitrary",) * 3
    return pl.pallas_call(body, grid=grid, name=name, in_specs=in_specs, out_specs=out_specs, out_shape=out_shape,
                          scratch_shapes=scratch, compiler_params=_cp(*sem))(*args)


def mm1(a, b, mode, out_dtype, name, **kw):
    return mm([[(a, b, mode)]], [out_dtype], name, **kw)[0]


ROW_BLK = 512


def rowwise(fn, rows, consts, outs, accs, name, tb=ROW_BLK):
    rows = [r if isinstance(r, tuple) else (r, r.shape[1], 0) for r in rows]
    T = rows[0][0].shape[0]
    tb = min(tb, T)
    n_r, n_c, n_o, n_a = len(rows), len(consts), len(outs), len(accs)

    def body(*refs):
        vals = [r[...].astype(F32) for r in refs[:n_r + n_c]]
        res = fn(*vals)
        o_refs = refs[n_r + n_c:n_r + n_c + n_o]
        a_refs = refs[n_r + n_c + n_o:]
        for o, r in zip(o_refs, res[:n_o]):
            o[...] = r.astype(o.dtype)
        if n_a:
            @pl.when(pl.program_id(0) == 0)
            def _():
                for ar in a_refs:
                    ar[...] = jnp.zeros_like(ar)
            for ar, r in zip(a_refs, res[n_o:]):
                ar[...] += r

    return pl.pallas_call(
        body, grid=(T // tb,), name=name,
        in_specs=[pl.BlockSpec((tb, w), functools.partial(lambda i, j: (i, j), j=j)) for _, w, j in rows]
        + [pl.BlockSpec(c.shape, lambda i: (0, 0)) for c in consts],
        out_specs=[pl.BlockSpec((tb, d), lambda i: (i, 0)) for d, _ in outs]
        + [pl.BlockSpec(s, lambda i: (0, 0)) for s in accs],
        out_shape=[jax.ShapeDtypeStruct((T, d), dt) for d, dt in outs]
        + [jax.ShapeDtypeStruct(s, F32) for s in accs],
        compiler_params=_cp("arbitrary" if n_a else "parallel"),
    )(*[r[0] for r in rows], *consts)


def _rms_stats(x):
    r = lax.rsqrt(jnp.mean(x * x, axis=-1, keepdims=True) + EPS)
    return r, x * r


def _rms_bwd(x, g, dy):
    r, xn = _rms_stats(x)
    dyg = dy * g
    dx = r * (dyg - xn * jnp.mean(dyg * xn, axis=-1, keepdims=True))
    return dx, jnp.sum(dy * xn, axis=0, keepdims=True)


def rms_fwd(x, g, name):
    return rowwise(lambda xv, gv: (_rms_stats(xv)[1] * gv,), [x], [g], [(x.shape[1], BF16)], [], name)[0]


def rms_bwd(x, g, dy, name, resid=None, dx_dtype=F32):
    def fn(*v):
        if resid is None:
            xv, dyv, gv = v
            dx, dg = _rms_bwd(xv, gv, dyv)
        else:
            xv, dyv, rv, gv = v
            dx, dg = _rms_bwd(xv, gv, dyv)
            dx = dx + rv
        return dx, dg
    rows = [x, dy] + ([] if resid is None else [resid])
    return rowwise(fn, rows, [g], [(x.shape[1], dx_dtype)], [(1, x.shape[1])], name)


def mm_rms_bwd(pairs, x, g, name, resid=None, dx_dtype=F32, comm=None):
    def epi(accs, ex):
        dx, dg = _rms_bwd(ex[0], ex[-1], accs[0])
        return (dx if resid is None else dx + ex[1]), dg
    extras = [x] + ([] if resid is None else [resid]) + [g]
    return mm([pairs], [dx_dtype], name, tm=min(256, x.shape[0]), tn=x.shape[1], epi=epi, extras=extras, comm=comm, n_sum=1)


def mm_resid(a, b, x, g, wgt, name, comm=None):
    epi = lambda accs, ex: (accs[0], ex[0] + wgt * _rms_stats(accs[0])[1] * ex[1])
    return mm([[(a, b, "nn")]], [F32, F32], name, tm=min(512, a.shape[0]), tn=b.shape[1], epi=epi, extras=[x, g], sub=2,
              comm=comm)


def resid_bwd(h, g, dy, wgt, name):
    def fn(hv, dyv, gv):
        dx, dg = _rms_bwd(hv, gv, dyv)
        return wgt * dx, wgt * dg
    return rowwise(fn, [h, dy], [g], [(h.shape[1], BF16)], [(1, h.shape[1])], name)


def _silu_parts(g):
    s = _sigmoid(g)
    return g * s, s * (1.0 + g * (1.0 - s))


def gated_norm_fwd(y, z, g, name):
    W = SSD_INNER // SSD_G

    def fn(yv, zv, gv):
        yg = yv * _silu_parts(zv)[0]
        return (jnp.concatenate([_rms_stats(yg[:, i * W:(i + 1) * W])[1] for i in range(SSD_G)], axis=1) * gv,)
    return rowwise(fn, [y, z], [g], [(SSD_INNER, BF16)], [], name)[0]


def gated_norm_bwd(y, z, dyn, g, name):
    W = SSD_INNER // SSD_G

    def fn(yv, zv, dv, gv):
        sil, dsil = _silu_parts(zv)
        yg = yv * sil
        parts = [_rms_bwd(yg[:, i * W:(i + 1) * W], gv[:, i * W:(i + 1) * W], dv[:, i * W:(i + 1) * W]) for i in range(SSD_G)]
        dyg = jnp.concatenate([p[0] for p in parts], axis=1)
        dg = jnp.concatenate([p[1] for p in parts], axis=1)
        return dyg * sil, dyg * yv * dsil, dg
    return rowwise(fn, [y, z, dyn], [g], [(SSD_INNER, BF16), (SSD_INNER, BF16)], [(1, SSD_INNER)], name)


def merge_fwd(gl, ys, ym, gb, name):
    def fn(glv, ysv, ymv, gbv):
        gt = _sigmoid(glv + gbv)
        return (gt[:, :D] * ysv + gt[:, D:] * ymv,)
    return rowwise(fn, [gl, ys, ym], [gb], [(D, BF16)], [], name)[0]


def merge_bwd(gl, ys, ym, dm, gb, name):
    def fn(glv, ysv, ymv, dmv, gbv):
        gt = _sigmoid(glv + gbv)
        gs, gm = gt[:, :D], gt[:, D:]
        dgl = jnp.concatenate([dmv * ysv * gs * (1.0 - gs), dmv * ymv * gm * (1.0 - gm)], axis=1)
        return dmv * gs, dmv * gm, dgl, jnp.sum(dgl, axis=0, keepdims=True)
    return rowwise(fn, [gl, ys, ym, dm], [gb], [(D, BF16), (D, BF16), (2 * D, BF16)], [(1, 2 * D)], name)


def loss_head(y, tgt, name):
    def fn(yv, tv):
        d = yv - tv
        part = 0.5 * jnp.sum(jnp.sum(d * d, axis=1, keepdims=True), axis=0, keepdims=True) / D
        return d / D, jnp.broadcast_to(part, (1, 128))
    return rowwise(fn, [y, tgt], [], [(D, F32)], [(1, 128)], name)


def _adamw_math(wv, gv, mv, vv):
    mn = B1 * mv + (1.0 - B1) * gv
    vn = B2 * vv + (1.0 - B2) * (gv * gv)
    mh = mn / (1.0 - B1 ** STEP)
    vh = vn / (1.0 - B2 ** STEP)
    return -LR * (mh / (jnp.sqrt(vh) + AEPS) + WD * wv), mn, vn


def adamw(w, g, m, v, name):
    R, C = w.shape
    tb = _tile(R, (256, 128, 64, 32, 16, 8))
    return rowwise(_adamw_math, [w, g, m, v], [], [(C, F32)] * 3, [], name, tb=tb)


def adamw_from_slots(recv, piece, w, m, v, name):
    K, n = w.shape
    ns = recv.shape[0]
    assert recv.shape[2] == n and recv.shape[1] % K == 0
    tb = _tile(K, (256, 176, 128, 64, 32, 16, 8)) if K % 8 == 0 else K
    r_spec = pl.BlockSpec((ns, tb, n), lambda i: (0, piece * (K // tb) + i, 0))
    w_spec = pl.BlockSpec((tb, n), lambda i: (i, 0))

    def body(r_ref, w_ref, m_ref, v_ref, g_ref, d_ref, mo_ref, vo_ref):
        g = r_ref[0].astype(F32)
        for s in range(1, ns):
            g = g + r_ref[s].astype(F32)
        g_ref[...] = g
        d_ref[...], mo_ref[...], vo_ref[...] = _adamw_math(w_ref[...], g, m_ref[...], v_ref[...])

    return pl.pallas_call(
        body, grid=(K // tb,), name=name, in_specs=[r_spec, w_spec, w_spec, w_spec], out_specs=[w_spec] * 4,
        out_shape=[jax.ShapeDtypeStruct((K, n), F32)] * 4, compiler_params=_cp("parallel"),
    )(recv, w, m, v)


def _me():
    return lax.axis_index("x"), lax.axis_index("y"), lax.axis_index("c")


def _dev_index():
    x, y, c = _me()
    return 4 * x + 2 * y + c


HBM_SPEC = pl.BlockSpec(memory_space=pl.ANY)


class GatherComm:
    def __init__(self, shards):
        self.inputs = list(shards)
        n = len(shards)
        self.out_shapes = [jax.ShapeDtypeStruct((N_DEV,) + s.shape, s.dtype) for s in shards]
        self.sems = [pltpu.SemaphoreType.DMA((7 * n,)), pltpu.SemaphoreType.DMA((7 * n,)), pltpu.SemaphoreType.DMA((n,))]

    def _plan(self, x_refs, out_refs, sems):
        send_sems, recv_sems, local_sems = sems
        n = len(x_refs)
        x, y, c = _me()
        me, sibling = (x, y, c), (x, y, 1 - c)
        chips = [(1 - x, y), (x, 1 - y), (1 - x, 1 - y)]

        def slot(i, px, py, pc):
            return out_refs[i].at[4 * px + 2 * py + pc]

        def copy(i, k, block, to, src=None):
            return pltpu.make_async_remote_copy(
                src_ref=slot(i, *block) if src is None else src, dst_ref=slot(i, *block),
                send_sem=send_sems.at[7 * i + k], recv_sem=recv_sems.at[7 * i + k], device_id=to, device_id_type=MESH)

        mine = [pltpu.make_async_copy(x_refs[i], slot(i, *me), local_sems.at[i]) for i in range(n)]
        first = []
        for i in range(n):
            first.append(copy(i, 0, me, sibling, src=x_refs[i]))
            first += [copy(i, 1 + j, me, (*chip, c), src=x_refs[i]) for j, chip in enumerate(chips)]
        passed = [[copy(i, 4 + j, (*chip, c), sibling) for j, chip in enumerate(chips)] for i in range(n)]
        from_ici = [[copy(i, 1 + j, (*chip, c), me) for j, chip in enumerate(chips)] for i in range(n)]
        from_sib = [[copy(i, 0, sibling, me)] + [copy(i, 4 + j, (*chip, 1 - c), me) for j, chip in enumerate(chips)] for i in range(n)]
        return mine, first, passed, from_ici, from_sib

    def start(self, x_refs, out_refs, sems):
        mine, first, _, _, _ = self._plan(x_refs, out_refs, sems)
        for cp in mine + first:
            cp.start()

    def finish(self, x_refs, out_refs, sems):
        mine, first, passed, from_ici, from_sib = self._plan(x_refs, out_refs, sems)
        for i in range(len(x_refs)):
            for arrival, forward in zip(from_ici[i], passed[i]):
                arrival.wait_recv()
                forward.start()
        for row in from_sib:
            for arrival in row:
                arrival.wait_recv()
        for cp in first + [cp for row in passed for cp in row]:
            cp.wait_send()
        for cp in mine:
            cp.wait()


def run_comm(comm, name):
    n_in, n_out = len(comm.inputs), len(comm.out_shapes)

    def body(*refs):
        ins, outs, sems = refs[:n_in], refs[n_in:n_in + n_out], refs[n_in + n_out:]
        comm.start(ins, outs, sems)
        comm.finish(ins, outs, sems)

    return pl.pallas_call(body, name=name, out_shape=comm.out_shapes, in_specs=[HBM_SPEC] * n_in,
                          out_specs=[HBM_SPEC] * n_out, scratch_shapes=comm.sems)(*comm.inputs)


def _attach(comm, body, n_in, n_out, first, last):
    if comm is None:
        return body
    ci, co, cs = len(comm.inputs), len(comm.out_shapes), len(comm.sems)

    def wrapped(*refs):
        h_in, c_in = refs[:n_in], refs[n_in:n_in + ci]
        h_out, c_out = refs[n_in + ci:n_in + ci + n_out], refs[n_in + ci + n_out:n_in + ci + n_out + co]
        rest = refs[n_in + ci + n_out + co:]
        h_scr, c_sem = rest[:len(rest) - cs], rest[len(rest) - cs:]

        @pl.when(first())
        def _():
            comm.start(c_in, c_out, c_sem)

        body(*h_in, *h_out, *h_scr)

        @pl.when(last())
        def _():
            comm.finish(c_in, c_out, c_sem)

    return wrapped


def _grid_ends(grid):
    first = lambda: functools.reduce(lambda a, b: a & b, [pl.program_id(i) == 0 for i in range(len(grid))])
    last = lambda: functools.reduce(lambda a, b: a & b, [pl.program_id(i) == g - 1 for i, g in enumerate(grid)])
    return first, last


def _call_with_comm(body, grid, name, in_specs, args, out_specs, out_shape, comm, scratch=(), sem=None):
    sem = sem or ("parallel",) * len(grid)
    scratch = list(scratch)
    if comm is not None:
        body = _attach(comm, body, len(args), len(out_shape), *_grid_ends(grid))
        in_specs, args = in_specs + [HBM_SPEC] * len(comm.inputs), args + comm.inputs
        out_specs, out_shape = out_specs + [HBM_SPEC] * len(comm.out_shapes), out_shape + comm.out_shapes
        scratch, sem = scratch + comm.sems, ("arbitrary",) * len(grid)
    return pl.pallas_call(body, grid=grid, name=name, in_specs=in_specs, out_specs=out_specs, out_shape=out_shape,
                          scratch_shapes=scratch, compiler_params=_cp(*sem))(*args)


class ScatterComm:
    def __init__(self, groups):
        self.sizes = [len(g) for g in groups]
        self.rows = [[pc.shape[1] for pc in g] for g in groups]
        ng = len(groups)
        self.inputs = [pc for g in groups for pc in g]
        self.out_shapes = [jax.ShapeDtypeStruct((N_DEV, sum(self.rows[gi]), g[0].shape[2]), g[0].dtype) for gi, g in enumerate(groups)]
        self.sems = [pltpu.SemaphoreType.DMA((7 * ng,)), pltpu.SemaphoreType.DMA((7 * ng,)), pltpu.SemaphoreType.DMA((ng,))]

    def _peers(self):
        x, y, c = _me()
        out = []
        for k in range(1, N_DEV):
            px = 1 - x if k & 4 else x
            py = 1 - y if k & 2 else y
            pc = 1 - c if k & 1 else c
            out.append((k, 4 * px + 2 * py + pc, dict(device_id=(px, py, pc), device_id_type=MESH)))
        return 4 * x + 2 * y + c, out

    def start(self, ins, outs, sems):
        send_sems, recv_sems, local_sems = sems
        me, peers = self._peers()
        pos = 0
        for gi, size in enumerate(self.sizes):
            for i, pc in enumerate(ins[pos:pos + size]):
                dst = outs[gi].at[me, pl.ds(sum(self.rows[gi][:i]), self.rows[gi][i])]
                pltpu.make_async_copy(pc.at[me], dst, local_sems.at[gi]).start()
                for k, peer, kw in peers:
                    pltpu.make_async_remote_copy(src_ref=pc.at[peer], dst_ref=dst, send_sem=send_sems.at[7 * gi + k - 1],
                                                 recv_sem=recv_sems.at[7 * gi + k - 1], **kw).start()
            pos += size

    def finish(self, ins, outs, sems):
        send_sems, recv_sems, local_sems = sems
        me, peers = self._peers()
        whole = [pltpu.make_async_remote_copy(src_ref=outs[gi].at[peer], dst_ref=outs[gi].at[peer],
                                              send_sem=send_sems.at[7 * gi + k - 1], recv_sem=recv_sems.at[7 * gi + k - 1], **kw)
                 for gi in range(len(self.sizes)) for k, peer, kw in peers]
        for cp in whole:
            cp.wait_recv()
        for cp in whole:
            cp.wait_send()
        for gi in range(len(self.sizes)):
            pltpu.make_async_copy(outs[gi].at[me], outs[gi].at[me], local_sems.at[gi]).wait()


def sum_slots(recv, name, tr):
    n, R, C = recv.shape

    def body(r_ref, o_ref):
        acc = r_ref[0].astype(F32)
        for s in range(1, n):
            acc = acc + r_ref[s].astype(F32)
        o_ref[...] = acc

    return pl.pallas_call(
        body, grid=(R // tr,), name=name,
        in_specs=[pl.BlockSpec((n, tr, C), lambda i: (0, i, 0))], out_specs=pl.BlockSpec((tr, C), lambda i: (i, 0)),
        out_shape=jax.ShapeDtypeStruct((R, C), F32), compiler_params=_cp("parallel"),
    )(recv)


PACK_W, FLAT_W = 1024, 128
MAIN = [
    ("ffn1_w_gate", "col"), ("ffn1_w_up", "col"), ("ffn1_w_down", "row"),
    ("ffn2_w_gate", "col"), ("ffn2_w_up", "col"), ("ffn2_w_down", "row"),
    ("w_ssd_proj", "row"), ("w_mla_proj", "row"), ("w_out", "row"),
    ("w_xq", "row"), ("w_xk", "row"), ("w_xv", "row"), ("w_xo", "row"),
    ("w_uk", "col"), ("w_uv", "col"),
]
FLAT = [("w_in", "col"), ("w_uq", "col")]
BIG = MAIN + FLAT
SMALL = ["ffn1_pre_g", "ffn1_post_g", "mix_pre_g", "conv_b", "dt_bias", "a_log", "d_skip", "ssd_norm_g", "q_norm_g",
         "kv_norm_g", "gate_bias", "mix_post_g", "xa_pre_g", "mem_norm_g", "xa_post_g", "ffn2_pre_g", "ffn2_post_g"]
WEIGHTS = ['ffn1_pre_g', 'ffn1_w_gate', 'ffn1_w_up', 'ffn1_w_down', 'ffn1_post_g', 'mix_pre_g', 'w_in', 'conv_w', 'conv_b',
           'dt_bias', 'a_log', 'd_skip', 'ssd_norm_g', 'w_ssd_proj', 'q_norm_g', 'w_uq', 'kv_norm_g', 'w_uk', 'w_uv',
           'w_mla_proj', 'gate_bias', 'w_out', 'mix_post_g', 'xa_pre_g', 'mem_norm_g', 'w_xq', 'w_xk', 'w_xv', 'w_xo',
           'xa_post_g', 'ffn2_pre_g', 'ffn2_w_gate', 'ffn2_w_up', 'ffn2_w_down', 'ffn2_post_g']


def _pack_rows(w, kind, width):
    m = w[0].T if kind == "col" else w[0]
    return m.reshape(-1, width)


KIND = dict(BIG)
GATHER_PLAN = {
    "first": (["ffn1_w_gate", "ffn1_w_up"], []),
    "ffn1_gate_up": (["ffn1_w_down"], ["w_in@0"]),
    "ffn1_down": ([], ["w_in@1"]),
    "ssd_fwd": (["w_ssd_proj", "w_mla_proj", "w_out", "w_uk", "w_uv"], ["w_uq"]),
    "attn_fwd": (["w_xq", "w_xk", "w_xv", "w_xo", "ffn2_w_gate", "ffn2_w_up", "ffn2_w_down"], []),
}
CONV_RIDES_WITH = "w_in@1"
SCATTER_PLAN = {
    "attn_bwd": [["ffn2_w_gate", "ffn2_w_up", "ffn2_w_down"], ["w_xq", "w_xk", "w_xv", "w_xo"]],
    "ssd_bwd": [["w_ssd_proj", "w_mla_proj", "w_out"], ["w_uk", "w_uv"], ["w_uq"]],
    "in_bwd": [["w_in#0"]],
    "ffn1:down_bwd": [["w_in#1"]],
    "ffn1:dwd": [["w_in#2"]],
    "ffn1:dwg": [["ffn1_w_down#0"]],
    "ffn1:dwu": [["ffn1_w_down#1"]],
    "ffn1:gate_up_bwd": [["ffn1_w_gate"]],
    "last": [["ffn1_w_up"]],
}
PARTS = {"w_in@0": ("w_in", 0, 2656), "w_in@1": ("w_in", 2656, 5296),
         "w_in#0": ("w_in", 0, 2656), "w_in#1": ("w_in", 2656, 3984), "w_in#2": ("w_in", 3984, 5296),
         "ffn1_w_down#0": ("ffn1_w_down", 0, 176), "ffn1_w_down#1": ("ffn1_w_down", 176, 352)}


def _parts_of(base, mark):
    return sorted(pn for pn, (b, _, _) in PARTS.items() if b == base and mark in pn)


class Stage:
    def __init__(self, w):
        self.w = w
        self.width = {n: PACK_W if (n, k) in MAIN else FLAT_W for n, k in BIG}
        self.nrows = {n: math.prod(w[n].shape) // self.width[n] for n, _ in BIG}
        self.recv = {}
        self.arrived_parts = {}

    def _rows(self, n):
        return PARTS[n][2] - PARTS[n][1] if n in PARTS else self.nrows[n]

    def _shards(self, tag):
        names_main, names_flat = GATHER_PLAN[tag]

        def pack(n):
            base, r0, r1 = PARTS.get(n, (n, 0, None))
            return _pack_rows(self.w[base], KIND[base], self.width[base])[r0:r1].astype(BF16)
        shards = []
        if names_main:
            shards.append(jnp.concatenate([pack(n) for n in names_main], axis=0))
        if names_flat:
            pieces = [pack(n) for n in names_flat]
            if CONV_RIDES_WITH in names_flat:
                pieces.append(lax.bitcast_convert_type(self.w["conv_w"][0], BF16).reshape(-1, FLAT_W))
            shards.append(_pad_rows(jnp.concatenate(pieces, axis=0), 16))
        return shards

    def gather(self, tag):
        return GatherComm(self._shards(tag)) if tag in GATHER_PLAN else None

    def gathered(self, tag, outs, W, p):
        if tag not in GATHER_PLAN:
            return
        names_main, names_flat = GATHER_PLAN[tag]
        outs = list(outs)
        for names in (names_main, names_flat):
            if not names:
                continue
            buf, r0 = outs.pop(0), 0
            for n in names:
                rows = buf[:, r0:r0 + self._rows(n)]
                r0 += self._rows(n)
                if n in PARTS:
                    self.arrived_parts[n] = rows
                    base = PARTS[n][0]
                    mine = _parts_of(base, "@")
                    if not all(pn in self.arrived_parts for pn in mine):
                        continue
                    n, rows = base, jnp.concatenate([self.arrived_parts[pn] for pn in mine], axis=1)
                K = self.w[n].shape[1] if KIND[n] == "col" else PACK_W
                W[n] = rows.reshape(-1, K)
            if names is names_flat and CONV_RIDES_WITH in names:
                cw = self.w["conv_w"]
                nbits = 2 * math.prod(cw.shape) // FLAT_W
                bits = buf[:, r0:r0 + nbits].reshape((N_DEV,) + cw.shape[1:] + (2,))
                p["conv_w"] = lax.bitcast_convert_type(bits, F32).transpose(1, 0, 2).reshape(cw.shape[1], -1)

    def pieces(self, tag, gw):
        def piece(n):
            if n in PARTS:
                base, r0, r1 = PARTS[n]
                return gw[base].reshape(N_DEV, self.nrows[base], self.width[base])[:, r0:r1]
            return gw[n].reshape(N_DEV, self.nrows[n], self.width[n])
        return [[piece(n) for n in names] for names in SCATTER_PLAN[tag]]

    def scatter(self, tag, gw):
        return ScatterComm(self.pieces(tag, gw)) if tag in SCATTER_PLAN else None

    def scattered(self, tag, outs):
        if tag in SCATTER_PLAN:
            self.recv[tag] = outs


def _pad_rows(a, mult):
    r = (-a.shape[0]) % mult
    return a if r == 0 else jnp.concatenate([a, jnp.zeros((r,) + a.shape[1:], a.dtype)], axis=0)


def _pack_small(vals, loss_row=None, conv_w=None):
    rows = []
    for v in vals:
        f = v.reshape(-1)
        f = jnp.concatenate([f, jnp.zeros(((-f.shape[0]) % 128,), F32)])
        rows.append(f.reshape(-1, 128))
    if conv_w is not None:
        rows.append(conv_w.reshape(-1, 128))
    if loss_row is not None:
        rows.append(loss_row)
    return _pad_rows(jnp.concatenate(rows, axis=0), 8)


def _unpack_small(buf, shapes):
    out, r = [], 0
    for shp in shapes:
        n = math.prod(shp)
        nr = -(-n // 128)
        out.append(buf[r:r + nr].reshape(-1)[:n].reshape(shp))
        r += nr
    return out, r


def _tn(a, b, name, out_dtype=BF16, comm=None):
    M, N = a.shape[1], b.shape[1]
    T = a.shape[0]
    tm = M if M <= 1536 else M // 2
    tk = 1024 if T % 1024 == 0 and T > 1024 else None
    res = mm([[(a, b, "tn")]], [out_dtype], name, tm=tm, tn=N, tk=tk, comm=comm)
    return res[0] if comm is None else (res[0], res[1:])


class NoStage:
    def gather(self, tag):
        return None

    def gathered(self, tag, outs, W, p):
        pass

    def scatter(self, tag, gw):
        return None

    def scattered(self, tag, outs):
        pass


def _ffn_fwd(x, gpre, gpost, W, p, tag, stage):
    h = rms_fwd(x, gpre, tag + "_pre")

    def swi(accs, ex):
        sil, dsil = _silu_parts(accs[0])
        return sil, accs[1] * dsil, sil * accs[1]
    G, U, A, *arrived = mm([[(h, W[tag + "_w_gate"], "nt")], [(h, W[tag + "_w_up"], "nt")]], [BF16, BF16, BF16], tag + "_gate_up",
                           tn=DFF // 2, epi=swi, comm=stage.gather(tag + "_gate_up"), sub=4 if h.shape[0] % 1024 == 0 else 1)
    stage.gathered(tag + "_gate_up", arrived, W, p)
    H, y, *arrived = mm_resid(A, W[tag + "_w_down"], x, gpost, FFN_RES, tag + "_down", comm=stage.gather(tag + "_down"))
    stage.gathered(tag + "_down", arrived, W, p)
    return y, (x, h, G, U, A, H)


def _ffn_bwd(dy, saved, gpre, gpost, wg_t, wu_t, wd, tag, stage, gw):
    x, h, G, U, A, H = saved
    dH, dgpost = resid_bwd(H, gpost, dy, FFN_RES, tag + "_post_bwd")

    def dswi(accs, ex):
        return accs[0] * ex[1], accs[0] * ex[0]

    def hosted(where, call):
        comm = stage.scatter(tag + ":" + where, gw)
        res = call(comm)
        if comm is None:
            return res
        stage.scattered(tag + ":" + where, res[1])
        return res[0]

    res = hosted("down_bwd", lambda comm: (lambda r: r if comm is None else (r[:2], r[2:]))(
        mm([[(dH, wd, "nt")]], [BF16, BF16], tag + "_down_bwd", tn=DFF // 2, epi=dswi, extras=[G, U], comm=comm,
           sub=4 if dH.shape[0] % 1024 == 0 else 1)))
    dG, dU = res
    gw[tag + "_w_down"] = hosted("dwd", lambda comm: _tn(A, dH, tag + "_dwd", comm=comm))
    gw[tag + "_w_gate"] = hosted("dwg", lambda comm: _tn(dG, h, tag + "_dwg", comm=comm))
    gw[tag + "_w_up"] = hosted("dwu", lambda comm: _tn(dU, h, tag + "_dwu", comm=comm))
    dx, dgpre = hosted("gate_up_bwd", lambda comm: (lambda r: r[:2] if comm is None else (r[:2], r[2:]))(
        mm_rms_bwd([(dG, wg_t, "nn"), (dU, wu_t, "nn")], x, gpre, tag + "_gate_up_bwd", resid=dy, comm=comm)))
    return dx, dgpre, dgpost


def _rope_tables(positions):
    inv = ROPE_THETA ** (-jnp.arange(0, ROPE, 2, dtype=F32) / ROPE)
    ang = positions.astype(F32).reshape(-1)[:, None] * inv
    return jnp.cos(ang), jnp.sin(ang)


def _local_step(x, mem, positions, tgt, W, p, stage=None):
    stage = stage or NoStage()
    nseq = x.shape[0]
    T = nseq * x.shape[1]
    x0 = x.reshape(T, D)
    mem2 = mem.reshape(-1, D)
    cos, sin = _rope_tables(positions)

    x1, ffn1 = _ffn_fwd(x0, p["ffn1_pre_g"], p["ffn1_post_g"], W, p, "ffn1", stage)

    w_in_t = W["w_in"]
    bounds = [0]
    for n in (SSD_INNER, CONV_CH, SSD_H, QR, KVR, ROPE, 2 * D):
        bounds.append(bounds[-1] + n)
    wt_z, wt_xbc, wt_dt, wt_q, wt_kv, wt_kr, wt_gate = [w_in_t[bounds[i]:bounds[i + 1]] for i in range(7)]
    wt_dt, wt_kr = _pad_rows(wt_dt, SLOT), _pad_rows(wt_kr, SLOT)
    wt_dtkr = jnp.concatenate([wt_dt, wt_kr], axis=0)
    hm = rms_fwd(x1, p["mix_pre_g"], "mix_pre")
    z = mm1(hm, wt_z, "nt", BF16, "in_z")
    xbc = mm1(hm, wt_xbc, "nt", BF16, "in_xbc")
    q_c = mm1(hm, wt_q, "nt", F32, "in_q", tn=QR)
    kv_c = mm1(hm, wt_kv, "nt", F32, "in_kv")
    dtkr = mm1(hm, wt_dtkr, "nt", F32, "in_dtkr")
    gl = mm1(hm, wt_gate, "nt", BF16, "in_gate")

    xbc_act = conv_fwd(xbc, p["conv_w"], p["conv_b"], nseq)
    y_ssd_core, prev, *arrived = ssd_fwd(xbc_act, dtkr, p["dt_bias"], p["a_log"], p["d_skip"], nseq, comm=stage.gather("ssd_fwd"))
    stage.gathered("ssd_fwd", arrived, W, p)
    yn = gated_norm_fwd(y_ssd_core, z, p["ssd_norm_g"], "ssd_norm")
    y_ssd = mm1(yn, W["w_ssd_proj"], "nn", BF16, "ssd_proj")

    slot_rows = lambda wt, per: jnp.pad(wt.reshape(MLA_H, per, -1), ((0, 0), (0, SLOT - per), (0, 0))).reshape(MLA_H * SLOT, -1)
    wq_s, wk_s, wv_s = slot_rows(W["w_uq"], QK), slot_rows(W["w_uk"], NOPE), slot_rows(W["w_uv"], VD)
    wo_s = slot_rows(W["w_mla_proj"], VD)
    qn = rms_fwd(q_c, p["q_norm_g"], "q_norm")
    q_s = mm1(qn, wq_s, "nt", BF16, "uq")
    kvn = rms_fwd(kv_c, p["kv_norm_g"], "kv_norm")
    kn_s = mm1(kvn, wk_s, "nt", BF16, "uk")
    v_s = mm1(kvn, wv_s, "nt", BF16, "uv")
    cos16, sin16 = cos, sin
    Qc, Kc = rope_slot_fwd(q_s, kn_s, dtkr, cos16, sin16, "rope")
    o_s, lse, *arrived = attn_slot_fwd(Qc, Kc, v_s, nseq, comm=stage.gather("attn_fwd"))
    stage.gathered("attn_fwd", arrived, W, p)
    y_mla = mm1(o_s, wo_s, "nn", BF16, "mla_proj")

    merged = merge_fwd(gl, y_ssd, y_mla, p["gate_bias"], "merge")
    hmix, x2 = mm_resid(merged, W["w_out"], x1, p["mix_post_g"], 1.0, "mix_out")

    hq = rms_fwd(x2, p["xa_pre_g"], "xa_pre")
    mn = rms_fwd(mem2, p["mem_norm_g"], "mem_norm")
    xq = mm1(hq, W["w_xq"], "nn", BF16, "xq")
    xk = mm1(mn, W["w_xk"], "nn", BF16, "xk")
    xv = mm1(mn, W["w_xv"], "nn", BF16, "xv")
    xo = xattn_fwd(xq, xk, xv, nseq)
    ho, x3 = mm_resid(xo, W["w_xo"], x2, p["xa_post_g"], 1.0, "xo")

    x4, ffn2 = _ffn_fwd(x3, p["ffn2_pre_g"], p["ffn2_post_g"], W, p, "ffn2", stage)
    dx4, loss_row = loss_head(x4, tgt.reshape(T, D), "loss")

    gw, gs = {}, {}
    dx3, gs["ffn2_pre_g"], gs["ffn2_post_g"] = _ffn_bwd(
        dx4, ffn2, p["ffn2_pre_g"], p["ffn2_post_g"], W["ffn2_w_gate"], W["ffn2_w_up"], W["ffn2_w_down"], "ffn2", stage, gw)

    dho, gs["xa_post_g"] = resid_bwd(ho, p["xa_post_g"], dx3, 1.0, "xa_post_bwd")
    dxo = mm1(dho, W["w_xo"], "nt", BF16, "xo_bwd")
    gw["w_xo"] = _tn(xo, dho, "d_w_xo")
    dxq, dxk, dxv = xattn_bwd(xq, xk, xv, dxo, nseq)
    dx2, gs["xa_pre_g"] = mm_rms_bwd([(dxq, W["w_xq"], "nt")], x2, p["xa_pre_g"], "xq_bwd", resid=dx3)
    gw["w_xq"] = _tn(hq, dxq, "d_w_xq")
    dmn = mm([[(dxk, W["w_xk"], "nt"), (dxv, W["w_xv"], "nt")]], [F32], "xkv_bwd")[0]
    gw["w_xk"] = _tn(mn, dxk, "d_w_xk")
    gw["w_xv"] = _tn(mn, dxv, "d_w_xv")
    _, gs["mem_norm_g"] = rms_bwd(mem2, p["mem_norm_g"], dmn, "mem_norm_bwd", dx_dtype=BF16)

    dhmix, gs["mix_post_g"] = resid_bwd(hmix, p["mix_post_g"], dx2, 1.0, "mix_post_bwd")
    dmerged = mm1(dhmix, W["w_out"], "nt", F32, "mix_out_bwd")
    gw["w_out"] = _tn(merged, dhmix, "d_w_out")
    dys, dym, dgl, gs["gate_bias"] = merge_bwd(gl, y_ssd, y_mla, dmerged, p["gate_bias"], "merge_bwd")

    unslot = lambda g, per: g.reshape(MLA_H, SLOT, -1)[:, :per].reshape(MLA_H * per, -1)
    do_s = mm1(dym, wo_s, "nt", BF16, "mla_proj_bwd")
    gw["w_mla_proj"] = unslot(_tn(o_s, dym, "d_w_mla_proj"), VD)
    dQc, dKc, dv_s, *sent = attn_slot_bwd(Qc, Kc, v_s, o_s, lse, do_s, nseq, comm=stage.scatter("attn_bwd", gw))
    stage.scattered("attn_bwd", sent)
    dq_s, dkn_s, dkr = rope_slot_bwd(dQc, dKc, cos16, sin16, "rope_bwd")
    dq_c, gs["q_norm_g"] = mm_rms_bwd([(dq_s, wq_s, "nn")], q_c, p["q_norm_g"], "uq_bwd", dx_dtype=BF16)
    gw["w_uq"] = unslot(_tn(dq_s, qn, "d_w_uq"), QK)
    dkv_c, gs["kv_norm_g"] = mm_rms_bwd([(dkn_s, wk_s, "nn"), (dv_s, wv_s, "nn")], kv_c, p["kv_norm_g"], "ukv_bwd", dx_dtype=BF16)
    gw["w_uk"] = unslot(_tn(dkn_s, kvn, "d_w_uk"), NOPE)
    gw["w_uv"] = unslot(_tn(dv_s, kvn, "d_w_uv"), VD)

    dyn = mm1(dys, W["w_ssd_proj"], "nt", F32, "ssd_proj_bwd")
    gw["w_ssd_proj"] = _tn(yn, dys, "d_w_ssd_proj")
    dyc, dz, gs["ssd_norm_g"] = gated_norm_bwd(y_ssd_core, z, dyn, p["ssd_norm_g"], "ssd_norm_bwd")
    dxbc_act, ddtr, gs["dt_bias"], gs["a_log"], gs["d_skip"], *sent = ssd_bwd(
        xbc_act, dtkr, p["dt_bias"], p["a_log"], p["d_skip"], prev, dyc, nseq, comm=stage.scatter("ssd_bwd", gw))
    stage.scattered("ssd_bwd", sent)
    dxbc, gs["conv_w"], gs["conv_b"] = conv_bwd(xbc, p["conv_w"], p["conv_b"], dxbc_act, nseq)

    gw["w_in"] = jnp.concatenate([_tn(dz, hm, "d_w_in_z"), _tn(dxbc, hm, "d_w_in_xbc"), _tn(ddtr, hm, "d_w_in_dt")[:SSD_H],
                                  _tn(dq_c, hm, "d_w_in_q"), _tn(dkv_c, hm, "d_w_in_kv"), _tn(dkr, hm, "d_w_in_kr")[:ROPE],
                                  _tn(dgl, hm, "d_w_in_gate")], axis=0)
    dx1, gs["mix_pre_g"], *sent = mm_rms_bwd(
        [(dz, wt_z, "nn"), (dxbc, wt_xbc, "nn"), (ddtr, wt_dt, "nn"), (dq_c, wt_q, "nn"), (dkv_c, wt_kv, "nn"),
         (dkr, wt_kr, "nn"), (dgl, wt_gate, "nn")], x1, p["mix_pre_g"], "in_bwd", resid=dx2, comm=stage.scatter("in_bwd", gw))
    stage.scattered("in_bwd", sent)

    dx0, gs["ffn1_pre_g"], gs["ffn1_post_g"] = _ffn_bwd(
        dx1, ffn1, p["ffn1_pre_g"], p["ffn1_post_g"], W["ffn1_w_gate"], W["ffn1_w_up"], W["ffn1_w_down"], "ffn1", stage, gw)
    return loss_row, dx0.reshape(x.shape), gw, gs


def kernel(x, mem, positions, ffn1_pre_g, ffn1_w_gate, ffn1_w_up, ffn1_w_down, ffn1_post_g, mix_pre_g, w_in, conv_w, conv_b, dt_bias, a_log, d_skip, ssd_norm_g, w_ssd_proj, q_norm_g, w_uq, kv_norm_g, w_uk, w_uv, w_mla_proj, gate_bias, w_out, mix_post_g, xa_pre_g, mem_norm_g, w_xq, w_xk, w_xv, w_xo, xa_post_g, ffn2_pre_g, ffn2_w_gate, ffn2_w_up, ffn2_w_down, ffn2_post_g, loss_target, m_ffn1_pre_g, m_ffn1_w_gate, m_ffn1_w_up, m_ffn1_w_down, m_ffn1_post_g, m_mix_pre_g, m_w_in, m_conv_w, m_conv_b, m_dt_bias, m_a_log, m_d_skip, m_ssd_norm_g, m_w_ssd_proj, m_q_norm_g, m_w_uq, m_kv_norm_g, m_w_uk, m_w_uv, m_w_mla_proj, m_gate_bias, m_w_out, m_mix_post_g, m_xa_pre_g, m_mem_norm_g, m_w_xq, m_w_xk, m_w_xv, m_w_xo, m_xa_post_g, m_ffn2_pre_g, m_ffn2_w_gate, m_ffn2_w_up, m_ffn2_w_down, m_ffn2_post_g, v_ffn1_pre_g, v_ffn1_w_gate, v_ffn1_w_up, v_ffn1_w_down, v_ffn1_post_g, v_mix_pre_g, v_w_in, v_conv_w, v_conv_b, v_dt_bias, v_a_log, v_d_skip, v_ssd_norm_g, v_w_ssd_proj, v_q_norm_g, v_w_uq, v_kv_norm_g, v_w_uk, v_w_uv, v_w_mla_proj, v_gate_bias, v_w_out, v_mix_post_g, v_xa_pre_g, v_mem_norm_g, v_w_xq, v_w_xk, v_w_xv, v_w_xo, v_xa_post_g, v_ffn2_pre_g, v_ffn2_w_gate, v_ffn2_w_up, v_ffn2_w_down, v_ffn2_post_g):
    a = dict(locals())
    w = {n: a[n] for n in WEIGHTS}
    m = {n: a["m_" + n] for n in WEIGHTS}
    v = {n: a["v_" + n] for n in WEIGHTS}

    stage = Stage(w)
    W, p = {}, {n: w[n] for n in SMALL}
    stage.gathered("first", run_comm(stage.gather("first"), "allgather_first"), W, p)

    loss_row, grad_x, gw, gs = _local_step(x, mem, positions, loss_target, W, p, stage)

    sm = _pack_small([gs[n] for n in SMALL], loss_row=loss_row, conv_w=gs["conv_w"])
    *recv_last, srecv = run_comm(ScatterComm(stage.pieces("last", gw) + [[jnp.broadcast_to(sm[None], (N_DEV,) + sm.shape)]]),
                                 "exchange_last")
    stage.scattered("last", recv_last)
    s_rows = sum_slots(srecv, "sum_small", tr=sm.shape[0])
    grads, delta, new_m, new_v = {}, {}, {}, {}

    def finish(n, buf, piece):
        col = KIND[n] == "col"
        turn = (lambda t: t.T) if col else (lambda t: t)
        K = w[n].shape[1]
        if col and buf.shape[2] != K:
            buf = buf.reshape(buf.shape[0], -1, K)
        res = adamw_from_slots(buf, piece, turn(w[n][0]), turn(m[n][0]), turn(v[n][0]), "adamw_" + n)
        grads[n], delta[n], new_m[n], new_v[n] = [turn(r)[None] for r in res]

    parts = {}
    for tag, groups in SCATTER_PLAN.items():
        for names, buf in zip(groups, stage.recv[tag]):
            for piece, n in enumerate(names):
                if n in PARTS:
                    parts[n] = sum_slots(buf, "sum_" + n.replace("#", "_"), tr=buf.shape[1])
                else:
                    finish(n, buf, piece)
    for base in sorted({PARTS[pn][0] for pn in parts}):
        rows = jnp.concatenate([parts[pn] for pn in _parts_of(base, "#")], axis=0)
        finish(base, rows[None], 0)
    conv_w_full = p["conv_w"]
    small_g, r1 = _unpack_small(s_rows, [w[n].shape for n in SMALL])
    for n, g in zip(SMALL, small_g):
        grads[n] = g
    ncw = math.prod(conv_w_full.shape) // 128
    cw_grad_full = s_rows[r1:r1 + ncw].reshape(conv_w_full.shape)
    wsh = conv_w.shape[2]
    grads["conv_w"] = lax.dynamic_slice_in_dim(cw_grad_full, _dev_index() * wsh, wsh, axis=1)[None]
    loss = s_rows[r1 + ncw, 0]

    d_, m_, v_ = adamw(conv_w[0], grads["conv_w"][0], m["conv_w"][0], v["conv_w"][0], "adamw_conv_w")
    delta["conv_w"], new_m["conv_w"], new_v["conv_w"] = d_[None], m_[None], v_[None]
    sp =[_pack_small([t[n] for n in SMALL]) for t in (w, grads, m, v)]
    outs = adamw(sp[0], sp[1], sp[2], sp[3], "adamw_small")
    for t, buf in zip((delta, new_m, new_v), outs):
        vals, _ = _unpack_small(buf, [w[n].shape for n in SMALL])
        for n, val in zip(SMALL, vals):
            t[n] = val
    return (loss, grad_x, *[grads[n] for n in WEIGHTS], *[delta[n] for n in WEIGHTS],
            *[new_m[n] for n in WEIGHTS], *[new_v[n] for n in WEIGHTS])
```

```python
import functools
import math

import jax
import jax.numpy as jnp
from jax import lax
from jax.experimental import pallas as pl
from jax.experimental.pallas import tpu as pltpu

F32, BF16 = jnp.float32, jnp.bfloat16
HI = lax.Precision.HIGHEST
MESH = pl.DeviceIdType.MESH
N_DEV = 8

D = 1024
DFF = 2816
SSD_H, SSD_P, SSD_G, SSD_N, SSD_L = 16, 64, 2, 128, 128
SSD_INNER = SSD_H * SSD_P
CONV_K, CONV_CH = 4, 1536
MLA_H, QR, KVR, NOPE, ROPE, VD = 16, 384, 256, 64, 32, 64
QK = NOPE + ROPE
ROPE_THETA = 10000.0
XA_H, XA_D = 4, 256
EPS = 1e-6
FFN_RES = 0.5
LR, B1, B2, AEPS, WD, STEP = 0.001, 0.9, 0.999, 1e-08, 0.01, 10

VMEM_LIMIT = 56 * 2**20


def _cp(*sem):
    return pltpu.CompilerParams(dimension_semantics=sem, vmem_limit_bytes=VMEM_LIMIT)


def _sigmoid(x):
    return 1.0 / (1.0 + jnp.exp(-x))


def _softplus(x):
    return jnp.where(x > 20.0, x, jnp.log(1.0 + jnp.exp(jnp.minimum(x, 20.0))))


def _dot(a, b, dims="nn"):
    ca = 0 if dims[0] == "t" else 1
    cb = 1 if dims[1] == "t" else 0
    return lax.dot_general(a.astype(BF16), b.astype(BF16), (((ca,), (cb,)), ((), ())), preferred_element_type=F32)


def _dot_sel(a, b, dims="nn", split="a", terms=3):
    r = (a if split == "a" else b).astype(F32)
    out = None
    for t in range(terms):
        piece = r.astype(BF16)
        if t + 1 < terms:
            r = r - piece.astype(F32)
        d = _dot(piece, b, dims) if split == "a" else _dot(a, piece, dims)
        out = d if out is None else out + d
    return out


def _ssd_common(dtr, dtb, alog):
    L = dtr.shape[0]
    dt = _softplus(dtr + dtb)
    a = -jnp.exp(alog)
    adt = dt * a
    r = lax.broadcasted_iota(jnp.int32, (L, L), 0)
    c = lax.broadcasted_iota(jnp.int32, (L, L), 1)
    lower = r >= c
    tri = lower.astype(F32)
    cs = _dot_sel(tri, adt, "nn", split="b")
    cs_t = _dot_sel(adt, tri, "tt")
    return dt, a, cs, cs_t, lower


def _head_expand():
    hh = lax.broadcasted_iota(jnp.int32, (SSD_H, SSD_INNER), 0)
    jj = lax.broadcasted_iota(jnp.int32, (SSD_H, SSD_INNER), 1)
    return ((jj >= hh * SSD_P) & (jj < hh * SSD_P + SSD_P)).astype(F32)


def _head_reduce():
    hh = lax.broadcasted_iota(jnp.int32, (SSD_INNER, SSD_H), 1)
    jj = lax.broadcasted_iota(jnp.int32, (SSD_INNER, SSD_H), 0)
    return ((jj >= hh * SSD_P) & (jj < hh * SSD_P + SSD_P)).astype(F32)


def ssd_fwd(xbc, dtr, dtb, alog, dsk, nseq, comm=None):
    T = xbc.shape[0]
    S = T // nseq
    C = S // SSD_L
    L = SSD_L
    NP = SSD_H // 2

    def body(x_ref, b_ref, c_ref, dtr_ref, dtb_ref, alog_ref, dsk_ref, y_ref, prev_ref, st_ref):
        ci = pl.program_id(1)

        @pl.when(ci == 0)
        def _():
            st_ref[...] = jnp.zeros_like(st_ref)

        dt, a, cs, cs_t, lower = _ssd_common(dtr_ref[:, 0:SSD_H], dtb_ref[...], alog_ref[...])
        E = _head_expand()
        X = x_ref[...].astype(F32)
        dt_e = _dot_sel(dt, E)
        cs_e = _dot_sel(cs, E)
        csl_e = cs_e[L - 1:L, :]
        Xd = X * dt_e
        Xf = Xd * jnp.exp(csl_e - cs_e)
        e_e = jnp.exp(cs_e)
        skip = _dot_sel(dsk_ref[...], E) * X
        lane = lax.broadcasted_iota(jnp.int32, (1, 2 * SSD_P), 1)
        rowp = lax.broadcasted_iota(jnp.int32, (2 * SSD_P, 1), 0)
        for g in range(SSD_G):
            Bg = b_ref[:, g * SSD_N:(g + 1) * SSD_N]
            Cg = c_ref[:, g * SSD_N:(g + 1) * SSD_N]
            cb = _dot(Cg, Bg, "nt")
            for pp in range(NP // SSD_G):
                p = g * (NP // SSD_G) + pp
                sl = slice(p * 2 * SSD_P, (p + 1) * 2 * SSD_P)
                Xd_p = Xd[:, sl]
                yd = jnp.zeros((L, 2 * SSD_P), F32)
                for q in range(2):
                    h = 2 * p + q
                    m = jnp.where(lower, jnp.exp(jnp.minimum(cs[:, h:h + 1] - cs_t[h:h + 1, :], 0.0)), 0.0)
                    mask = (lane >= q * SSD_P) & (lane < (q + 1) * SSD_P)
                    yd = yd + _dot(cb * m, jnp.where(mask, Xd_p, 0.0))
                S0 = st_ref[p]
                prev_ref[0, 0, p] = S0
                z = _dot(Cg, S0, "nt")
                y_ref[:, sl] = (skip[:, sl] + yd + z * e_e[:, sl]).astype(y_ref.dtype)
                h0 = 2 * p
                dec = jnp.where(rowp < SSD_P, jnp.exp(cs[L - 1:L, h0:h0 + 1]), jnp.exp(cs[L - 1:L, h0 + 1:h0 + 2]))
                st_ref[p] = S0 * dec + _dot(Xf[:, sl], Bg, "tn")

    row = lambda b, c: (b * C + c, 0)
    small = pl.BlockSpec((1, SSD_H), lambda b, c: (0, 0))
    return _call_with_comm(
        body, (nseq, C), "ssd_fwd",
        [pl.BlockSpec((L, SSD_INNER), row),
         pl.BlockSpec((L, SSD_G * SSD_N), lambda b, c: (b * C + c, SSD_INNER // (SSD_G * SSD_N))),
         pl.BlockSpec((L, SSD_G * SSD_N), lambda b, c: (b * C + c, SSD_INNER // (SSD_G * SSD_N) + 1)),
         pl.BlockSpec((L, 128), row), small, small, small],
        [xbc, xbc, xbc, dtr, dtb, alog, dsk],
        [pl.BlockSpec((L, SSD_INNER), row), pl.BlockSpec((1, 1, NP, 2 * SSD_P, SSD_N), lambda b, c: (b, c, 0, 0, 0))],
        [jax.ShapeDtypeStruct((T, SSD_INNER), BF16), jax.ShapeDtypeStruct((nseq, C, NP, 2 * SSD_P, SSD_N), F32)],
        comm, scratch=[pltpu.VMEM((NP, 2 * SSD_P, SSD_N), F32)], sem=("parallel", "arbitrary"))


def ssd_bwd(xbc, dtr, dtb, alog, dsk, prev, dy, nseq, comm=None):
    T = xbc.shape[0]
    S = T // nseq
    C = S // SSD_L
    L = SSD_L
    NP = SSD_H // 2

    def body(x_ref, b_ref, c_ref, dtr_ref, dtb_ref, alog_ref, dsk_ref, prev_ref, dy_ref,
             dxbc_ref, ddtr_ref, ddtb_ref, dalog_ref, ddsk_ref, ds_ref, stg_ref):
        bi = pl.program_id(0)
        ci = pl.program_id(1)

        @pl.when(ci == 0)
        def _():
            ds_ref[...] = jnp.zeros_like(ds_ref)

        @pl.when((ci == 0) & (bi == 0))
        def _():
            ddtb_ref[...] = jnp.zeros_like(ddtb_ref)
            dalog_ref[...] = jnp.zeros_like(dalog_ref)
            ddsk_ref[...] = jnp.zeros_like(ddsk_ref)

        dtr = dtr_ref[:, 0:SSD_H]
        dtb = dtb_ref[...]
        dt, a, cs, cs_t, lower = _ssd_common(dtr, dtb, alog_ref[...])
        upper = lax.broadcasted_iota(jnp.int32, (L, L), 1) >= lax.broadcasted_iota(jnp.int32, (L, L), 0)
        E = _head_expand()
        ET = _head_reduce()
        X = x_ref[...].astype(F32)
        dY = dy_ref[...].astype(F32)
        dt_e = _dot_sel(dt, E)
        cs_e = _dot_sel(cs, E)
        csl_e = cs_e[L - 1:L, :]
        f_e = jnp.exp(csl_e - cs_e)
        e_e = jnp.exp(cs_e)
        dsk_e = _dot_sel(dsk_ref[...], E)
        Xd = X * dt_e
        Xf = Xd * f_e
        lane = lax.broadcasted_iota(jnp.int32, (1, 2 * SSD_P), 1)
        rowp = lax.broadcasted_iota(jnp.int32, (2 * SSD_P, 1), 0)
        hsel = lax.broadcasted_iota(jnp.int32, (1, SSD_H), 1)
        dcs = jnp.zeros((L, SSD_H), F32)
        dcsl = jnp.zeros((1, SSD_H), F32)
        for g in range(SSD_G):
            Bg = b_ref[:, g * SSD_N:(g + 1) * SSD_N]
            Cg = c_ref[:, g * SSD_N:(g + 1) * SSD_N]
            cb = _dot(Cg, Bg, "nt")
            cbt = _dot(Bg, Cg, "nt")
            dB = jnp.zeros((L, SSD_N), F32)
            dC = jnp.zeros((L, SSD_N), F32)
            for pp in range(NP // SSD_G):
                p = g * (NP // SSD_G) + pp
                sl = slice(p * 2 * SSD_P, (p + 1) * 2 * SSD_P)
                Xd_p = Xd[:, sl]
                dY_p = dY[:, sl]
                dXd_p = jnp.zeros((L, 2 * SSD_P), F32)
                for q in range(2):
                    h = 2 * p + q
                    mask = (lane >= q * SSD_P) & (lane < (q + 1) * SSD_P)
                    col = cs[:, h:h + 1]
                    rw = cs_t[h:h + 1, :]
                    m = jnp.where(lower, jnp.exp(jnp.minimum(col - rw, 0.0)), 0.0)
                    mt = jnp.where(upper, jnp.exp(jnp.minimum(rw - col, 0.0)), 0.0)
                    dYm = jnp.where(mask, dY_p, 0.0)
                    dW = _dot(dYm, Xd_p, "nt")
                    dWt = _dot(Xd_p, dYm, "nt")
                    w = cb * m
                    wt = cbt * mt
                    dC = dC + _dot(dW * m, Bg)
                    dB = dB + _dot(dWt * mt, Cg)
                    dXd_p = dXd_p + jnp.where(mask, _dot(wt, dY_p), 0.0)
                    qcol = jnp.sum(dW * w, axis=1, keepdims=True) - jnp.sum(dWt * wt, axis=1, keepdims=True)
                    dcs = dcs + qcol * (hsel == h).astype(F32)
                S0 = prev_ref[0, 0, p]
                dSn = ds_ref[p]
                dZ = dY_p * e_e[:, sl]
                dC = dC + _dot(dZ, S0)
                h0 = 2 * p
                el0 = jnp.exp(cs[L - 1:L, h0:h0 + 1])
                el1 = jnp.exp(cs[L - 1:L, h0 + 1:h0 + 2])
                dec = jnp.where(rowp < SSD_P, el0, el1)
                ds_ref[p] = dSn * dec + _dot(dZ, Cg, "tn")
                dXf_p = _dot(Bg, dSn, "nt")
                dB = dB + _dot(Xf[:, sl], dSn)
                rs = jnp.sum(dSn * S0, axis=1, keepdims=True)
                s0 = jnp.sum(jnp.where(rowp < SSD_P, rs, 0.0), axis=0, keepdims=True) * el0
                s1 = jnp.sum(jnp.where(rowp >= SSD_P, rs, 0.0), axis=0, keepdims=True) * el1
                dcsl = dcsl + s0 * (hsel == h0).astype(F32) + s1 * (hsel == h0 + 1).astype(F32)
                y_off = _dot(Cg, S0, "nt") * e_e[:, sl]
                t1 = dY_p * y_off - dXf_p * Xf[:, sl]
                r1 = jnp.where(lane < SSD_P, t1, 0.0)
                c0 = jnp.sum(r1, axis=1, keepdims=True)
                c1 = jnp.sum(t1 - r1, axis=1, keepdims=True)
                dcs = dcs + c0 * (hsel == h0).astype(F32) + c1 * (hsel == h0 + 1).astype(F32)
                t2 = dXf_p * Xf[:, sl]
                r2 = jnp.where(lane < SSD_P, t2, 0.0)
                dcsl = dcsl + jnp.sum(r2, keepdims=True) * (hsel == h0).astype(F32) \
                    + jnp.sum(t2 - r2, keepdims=True) * (hsel == h0 + 1).astype(F32)
                stg_ref[:, sl] = dXd_p + dXf_p * f_e[:, sl]
            dxbc_ref[:, SSD_INNER + g * SSD_N:SSD_INNER + (g + 1) * SSD_N] = dB.astype(dxbc_ref.dtype)
            dxbc_ref[:, SSD_INNER + (SSD_G + g) * SSD_N:SSD_INNER + (SSD_G + g + 1) * SSD_N] = dC.astype(dxbc_ref.dtype)
        dXd = stg_ref[...]
        dxbc_ref[:, 0:SSD_INNER] = (dXd * dt_e + dsk_e * dY).astype(dxbc_ref.dtype)
        rowl = lax.broadcasted_iota(jnp.int32, (L, 1), 0)
        dcs = dcs + jnp.where(rowl == L - 1, dcsl, 0.0)
        dalpha = _dot_sel(upper.astype(F32), dcs, split="b")
        ddt = _dot_sel(dXd * X, ET, terms=2) + dalpha * a
        dalog_ref[...] += jnp.sum(dalpha * dt, axis=0, keepdims=True) * a
        ddtr = ddt * _sigmoid(dtr + dtb)
        spread = (lax.broadcasted_iota(jnp.int32, (SSD_H, 128), 0) == lax.broadcasted_iota(jnp.int32, (SSD_H, 128), 1)).astype(F32)
        ddtr_ref[...] = _dot(ddtr, spread).astype(ddtr_ref.dtype)
        ddtb_ref[...] += jnp.sum(ddtr, axis=0, keepdims=True)
        ddsk_ref[...] += jnp.sum(_dot_sel(dY * X, ET, terms=2), axis=0, keepdims=True)

    rowr = lambda b, c: (b * C + (C - 1 - c), 0)
    small = pl.BlockSpec((1, SSD_H), lambda b, c: (0, 0))
    return _call_with_comm(
        body, (nseq, C), "ssd_bwd",
        [pl.BlockSpec((L, SSD_INNER), rowr),
         pl.BlockSpec((L, SSD_G * SSD_N), lambda b, c: (b * C + (C - 1 - c), SSD_INNER // (SSD_G * SSD_N))),
         pl.BlockSpec((L, SSD_G * SSD_N), lambda b, c: (b * C + (C - 1 - c), SSD_INNER // (SSD_G * SSD_N) + 1)),
         pl.BlockSpec((L, 128), rowr), small, small, small,
         pl.BlockSpec((1, 1, NP, 2 * SSD_P, SSD_N), lambda b, c: (b, C - 1 - c, 0, 0, 0)),
         pl.BlockSpec((L, SSD_INNER), rowr)],
        [xbc, xbc, xbc, dtr, dtb, alog, dsk, prev, dy],
        [pl.BlockSpec((L, CONV_CH), rowr), pl.BlockSpec((L, 128), rowr), small, small, small],
        [jax.ShapeDtypeStruct((T, CONV_CH), BF16), jax.ShapeDtypeStruct((T, 128), BF16),
         jax.ShapeDtypeStruct((1, SSD_H), F32), jax.ShapeDtypeStruct((1, SSD_H), F32), jax.ShapeDtypeStruct((1, SSD_H), F32)],
        comm, scratch=[pltpu.VMEM((NP, 2 * SSD_P, SSD_N), F32), pltpu.VMEM((L, SSD_INNER), F32)], sem=("arbitrary", "arbitrary"))


SLOT = 128
ATT_T = 512
ATT_HP = 1
LOG2E = math.log2(math.e)
Q_SCALE = QK ** -0.5 * LOG2E


def _col_to_row(col):
    n = col.shape[0]
    eye = lax.broadcasted_iota(jnp.int32, (n, n), 0) == lax.broadcasted_iota(jnp.int32, (n, n), 1)
    return jnp.sum(jnp.where(eye, col, 0.0), axis=0, keepdims=True)


def attn_slot_fwd(q, k, v, nseq, comm=None):
    T = q.shape[0]
    S = T // nseq
    t = min(ATT_T, S)
    nb = S // t
    cols = [slice(h * SLOT, (h + 1) * SLOT) for h in range(ATT_HP)]

    def body(q_ref, k_ref, v_ref, o_ref, lse_ref):
        causal = lax.broadcasted_iota(jnp.int32, (t, t), 1) <= lax.broadcasted_iota(jnp.int32, (t, t), 0)
        for qi in range(nb):
            rows = slice(qi * t, (qi + 1) * t)
            state = [None] * ATT_HP
            for kj in range(qi + 1):
                keys = slice(kj * t, (kj + 1) * t)
                for h, c in enumerate(cols):
                    s = _dot(q_ref[rows, c], k_ref[keys, c], "nt")
                    if kj == qi:
                        s = jnp.where(causal, s, -1e30)
                    bm = jnp.max(s, axis=1, keepdims=True)
                    if kj == 0:
                        p = jnp.exp2(s - bm)
                        state[h] = (bm, jnp.sum(p, axis=1, keepdims=True), _dot(p, v_ref[keys, c]))
                    else:
                        m, l, acc = state[h]
                        m_new = jnp.maximum(m, bm)
                        corr = jnp.exp2(m - m_new)
                        p = jnp.exp2(s - m_new)
                        state[h] = (m_new, l * corr + jnp.sum(p, axis=1, keepdims=True), acc * corr + _dot(p, v_ref[keys, c]))
            for h, c in enumerate(cols):
                m, l, acc = state[h]
                o_ref[rows, c] = (acc / l).astype(o_ref.dtype)
                lse_ref[0, h, :, rows] = _col_to_row(m + jnp.log2(l))

    blk = pl.BlockSpec((S, ATT_HP * SLOT), lambda b, h: (b, h))
    return _call_with_comm(
        body, (nseq, MLA_H // ATT_HP), "attn_fwd", [blk, blk, blk], [q, k, v],
        [blk, pl.BlockSpec((1, ATT_HP, 1, S), lambda b, h: (b, h, 0, 0))],
        [jax.ShapeDtypeStruct((T, MLA_H * SLOT), BF16), jax.ShapeDtypeStruct((nseq, MLA_H, 1, S), F32)], comm)


def attn_slot_bwd(q, k, v, o, lse, do, nseq, comm=None):
    T = q.shape[0]
    S = T // nseq
    t = min(ATT_T, S)
    nb = S // t
    scale = QK ** -0.5
    cols = [slice(h * SLOT, (h + 1) * SLOT) for h in range(ATT_HP)]

    def body(q_ref, k_ref, v_ref, o_ref, lse_ref, do_ref, dq_ref, dk_ref, dv_ref, dqa_ref):
        causal_t = lax.broadcasted_iota(jnp.int32, (t, t), 0) <= lax.broadcasted_iota(jnp.int32, (t, t), 1)
        ones = jnp.ones((8, SLOT), F32)
        delta = {}
        for qi in range(nb):
            sl = slice(qi * t, (qi + 1) * t)
            for h, c in enumerate(cols):
                prod = do_ref[sl, c].astype(F32) * o_ref[sl, c].astype(F32)
                delta[h, qi] = _dot_sel(ones, prod, "nt", split="b", terms=2)[0:1, :]
        for kj in range(nb):
            ks = slice(kj * t, (kj + 1) * t)
            dk = [None] * ATT_HP
            dv = [None] * ATT_HP
            for qi in range(kj, nb):
                sl = slice(qi * t, (qi + 1) * t)
                for h, c in enumerate(cols):
                    kb, vb, qb, dob = k_ref[ks, c], v_ref[ks, c], q_ref[sl, c], do_ref[sl, c]
                    st = _dot(kb, qb, "nt")
                    pt = jnp.exp2(st - lse_ref[0, h, :, sl])
                    if qi == kj:
                        pt = jnp.where(causal_t, pt, 0.0)
                    dpt = _dot(vb, dob, "nt")
                    dst = (pt * (dpt - delta[h, qi])).astype(BF16)
                    dvc = _dot(pt, dob)
                    dkc = _dot(dst, qb) * (1.0 / LOG2E)
                    dv[h] = dvc if dv[h] is None else dv[h] + dvc
                    dk[h] = dkc if dk[h] is None else dk[h] + dkc
                    dqc = _dot(dst, kb, "tn") * scale
                    if kj > 0:
                        dqc = dqc + dqa_ref[sl, c]
                    if qi == kj:
                        dq_ref[sl, c] = dqc.astype(dq_ref.dtype)
                    else:
                        dqa_ref[sl, c] = dqc
            for h, c in enumerate(cols):
                dk_ref[ks, c] = dk[h].astype(dk_ref.dtype)
                dv_ref[ks, c] = dv[h].astype(dv_ref.dtype)

    blk = pl.BlockSpec((S, ATT_HP * SLOT), lambda b, h: (b, h))
    lse_spec = pl.BlockSpec((1, ATT_HP, 1, S), lambda b, h: (b, h, 0, 0))
    W = MLA_H * SLOT
    return _call_with_comm(
        body, (nseq, MLA_H // ATT_HP), "attn_bwd", [blk, blk, blk, blk, lse_spec, blk], [q, k, v, o, lse, do], [blk, blk, blk],
        [jax.ShapeDtypeStruct((T, W), BF16)] * 3, comm, scratch=[pltpu.VMEM((S, ATT_HP * SLOT), F32)])


def _rope_coeffs(cos, sin):
    half = ROPE // 2
    r = lax.broadcasted_iota(jnp.int32, (half, SLOT), 0)
    c = lax.broadcasted_iota(jnp.int32, (half, SLOT), 1)
    pc = ((c == r + NOPE) | (c == r + NOPE + half)).astype(F32)
    ps = (c == r + NOPE + half).astype(F32) - (c == r + NOPE).astype(F32)
    lane = lax.broadcasted_iota(jnp.int32, (1, SLOT), 1)
    return _dot_sel(cos, pc) + (lane < NOPE).astype(F32), _dot_sel(sin, ps)


def _rope_swap(x):
    W = x.shape[1]
    half = ROPE // 2
    lane = lax.broadcasted_iota(jnp.int32, (1, W), 1) & (SLOT - 1)
    up = pltpu.roll(x, W - half, axis=1)
    dn = pltpu.roll(x, half, axis=1)
    return jnp.where((lane >= NOPE) & (lane < NOPE + half), up, jnp.where((lane >= NOPE + half) & (lane < QK), dn, 0.0))


def rope_slot_fwd(q, kn, dtkr, cos, sin, name):
    def fn(qv, knv, krv, cv, sv):
        C, Sg = _rope_coeffs(cv, sv)
        ct, stl = jnp.tile(C, (1, MLA_H)), jnp.tile(Sg, (1, MLA_H))
        qo = (qv * ct + _rope_swap(qv) * stl) * Q_SCALE
        r = lax.broadcasted_iota(jnp.int32, (SLOT, SLOT), 0)
        c = lax.broadcasted_iota(jnp.int32, (SLOT, SLOT), 1)
        place = ((c == r + NOPE) & (r < ROPE)).astype(F32)
        kr = _dot_sel(krv, place)
        kr = kr * C + _rope_swap(kr) * Sg
        return qo, knv.astype(F32) + jnp.tile(kr, (1, MLA_H))
    W = MLA_H * SLOT
    return rowwise(fn, [q, kn, (dtkr, SLOT, 1), cos, sin], [], [(W, BF16), (W, BF16)], [], name)


def rope_slot_bwd(dq, dk, cos, sin, name):
    def fn(dqv, dkv, cv, sv):
        C, Sg = _rope_coeffs(cv, sv)
        ct, stl = jnp.tile(C, (1, MLA_H)), jnp.tile(Sg, (1, MLA_H))
        dqo = dqv * ct - _rope_swap(dqv) * stl
        tot = dkv[:, 0:SLOT]
        for h in range(1, MLA_H):
            tot = tot + dkv[:, h * SLOT:(h + 1) * SLOT]
        u = tot * C - _rope_swap(tot) * Sg
        r = lax.broadcasted_iota(jnp.int32, (SLOT, SLOT), 0)
        c = lax.broadcasted_iota(jnp.int32, (SLOT, SLOT), 1)
        unplace = ((r == c + NOPE) & (c < ROPE)).astype(F32)
        return dqo, dkv, _dot_sel(u, unplace, terms=2)
    W = MLA_H * SLOT
    return rowwise(fn, [dq, dk, cos, sin], [], [(W, BF16), (W, BF16), (SLOT, BF16)], [], name)


XA_BLK = 512


def xattn_fwd(q, k, v, nseq):
    T = q.shape[0]
    S = T // nseq
    M = k.shape[0] // nseq
    tq = min(XA_BLK, S)
    nq = S // tq
    scale = XA_D ** -0.5

    def body(q_ref, k_ref, v_ref, o_ref):
        s = _dot(q_ref[...], k_ref[...], "nt") * scale
        p = jnp.exp(s - jnp.max(s, axis=1, keepdims=True))
        p = p / jnp.sum(p, axis=1, keepdims=True)
        o_ref[...] = _dot(p, v_ref[...]).astype(o_ref.dtype)

    qs = pl.BlockSpec((tq, XA_D), lambda b, h, i: (b * nq + i, h))
    ks = pl.BlockSpec((M, XA_D), lambda b, h, i: (b, h))
    return pl.pallas_call(
        body, grid=(nseq, XA_H, nq), name="xattn_fwd", in_specs=[qs, ks, ks], out_specs=qs,
        out_shape=jax.ShapeDtypeStruct((T, XA_H * XA_D), BF16),
        compiler_params=_cp("parallel", "parallel", "parallel"),
    )(q, k, v)


def xattn_bwd(q, k, v, do, nseq):
    T = q.shape[0]
    S = T // nseq
    M = k.shape[0] // nseq
    tq = min(XA_BLK, S)
    nq = S // tq
    scale = XA_D ** -0.5

    def body(q_ref, k_ref, v_ref, do_ref, dq_ref, dk_ref, dv_ref):
        @pl.when(pl.program_id(2) == 0)
        def _():
            dk_ref[...] = jnp.zeros_like(dk_ref)
            dv_ref[...] = jnp.zeros_like(dv_ref)

        qb, kb, vb, dob = q_ref[...], k_ref[...], v_ref[...], do_ref[...]
        s = _dot(qb, kb, "nt") * scale
        p = jnp.exp(s - jnp.max(s, axis=1, keepdims=True))
        p = p / jnp.sum(p, axis=1, keepdims=True)
        dp = _dot(dob, vb, "nt")
        ds = p * (dp - jnp.sum(dp * p, axis=1, keepdims=True)) * scale
        dq_ref[...] = _dot(ds, kb).astype(dq_ref.dtype)
        dk_ref[...] += _dot(ds, qb, "tn")
        dv_ref[...] += _dot(p, dob, "tn")

    qs = pl.BlockSpec((tq, XA_D), lambda b, h, i: (b * nq + i, h))
    ks = pl.BlockSpec((M, XA_D), lambda b, h, i: (b, h))
    return pl.pallas_call(
        body, grid=(nseq, XA_H, nq), name="xattn_bwd", in_specs=[qs, ks, ks, qs], out_specs=[qs, ks, ks],
        out_shape=[jax.ShapeDtypeStruct((T, XA_H * XA_D), BF16), jax.ShapeDtypeStruct(k.shape, F32),
                   jax.ShapeDtypeStruct(k.shape, F32)],
        compiler_params=_cp("parallel", "parallel", "arbitrary"),
    )(q, k, v, do)


CONV_BLK = 256


def _shift_down(x, s, rows):
    if s == 0:
        return x
    return jnp.where(rows >= s, pltpu.roll(x, s, axis=0), 0.0)


def _shift_up(x, s, rows):
    if s == 0:
        return x
    S = x.shape[0]
    return jnp.where(rows < S - s, pltpu.roll(x, S - s, axis=0), 0.0)


def conv_fwd(x, w, b, nseq):
    T, CH = x.shape
    S = T // nseq

    def body(x_ref, w_ref, b_ref, o_ref):
        xv = x_ref[...].astype(F32)
        rows = lax.broadcasted_iota(jnp.int32, (S, 1), 0)
        c = jnp.zeros_like(xv) + b_ref[...]
        for kk in range(CONV_K):
            c = c + w_ref[kk:kk + 1, :] * _shift_down(xv, CONV_K - 1 - kk, rows)
        o_ref[...] = (c * _sigmoid(c)).astype(o_ref.dtype)

    xs = pl.BlockSpec((S, CONV_BLK), lambda j, bb: (bb, j))
    return pl.pallas_call(
        body, grid=(CH // CONV_BLK, nseq), name="conv_fwd",
        in_specs=[xs, pl.BlockSpec((CONV_K, CONV_BLK), lambda j, bb: (0, j)), pl.BlockSpec((1, CONV_BLK), lambda j, bb: (0, j))],
        out_specs=xs, out_shape=jax.ShapeDtypeStruct((T, CH), BF16),
        compiler_params=_cp("parallel", "parallel"),
    )(x, w, b)


def conv_bwd(x, w, b, dout, nseq):
    T, CH = x.shape
    S = T // nseq

    def body(x_ref, w_ref, b_ref, do_ref, dx_ref, dw_ref, db_ref):
        @pl.when(pl.program_id(1) == 0)
        def _():
            dw_ref[...] = jnp.zeros_like(dw_ref)
            db_ref[...] = jnp.zeros_like(db_ref)

        xv = x_ref[...].astype(F32)
        rows = lax.broadcasted_iota(jnp.int32, (S, 1), 0)
        c = jnp.zeros_like(xv) + b_ref[...]
        sh = [_shift_down(xv, CONV_K - 1 - kk, rows) for kk in range(CONV_K)]
        for kk in range(CONV_K):
            c = c + w_ref[kk:kk + 1, :] * sh[kk]
        sg = _sigmoid(c)
        dc = do_ref[...].astype(F32) * sg * (1.0 + c * (1.0 - sg))
        dx = jnp.zeros_like(xv)
        for kk in range(CONV_K):
            dx = dx + w_ref[kk:kk + 1, :] * _shift_up(dc, CONV_K - 1 - kk, rows)
            dw_ref[kk:kk + 1, :] += jnp.sum(dc * sh[kk], axis=0, keepdims=True)
        dx_ref[...] = dx.astype(dx_ref.dtype)
        db_ref[...] += jnp.sum(dc, axis=0, keepdims=True)

    xs = pl.BlockSpec((S, CONV_BLK), lambda j, bb: (bb, j))
    ws = pl.BlockSpec((CONV_K, CONV_BLK), lambda j, bb: (0, j))
    bs = pl.BlockSpec((1, CONV_BLK), lambda j, bb: (0, j))
    return pl.pallas_call(
        body, grid=(CH // CONV_BLK, nseq), name="conv_bwd",
        in_specs=[xs, ws, bs, xs], out_specs=[xs, ws, bs],
        out_shape=[jax.ShapeDtypeStruct((T, CH), BF16), jax.ShapeDtypeStruct((CONV_K, CH), F32),
                   jax.ShapeDtypeStruct((1, CH), F32)],
        compiler_params=_cp("parallel", "arbitrary"),
    )(x, w, b, dout)


def _dims(a, b, mode):
    M = a.shape[1] if mode[0] == "t" else a.shape[0]
    K = a.shape[0] if mode[0] == "t" else a.shape[1]
    N = b.shape[0] if mode[1] == "t" else b.shape[1]
    return M, K, N


def _tile(dim, prefs):
    for p in prefs:
        if dim % p == 0:
            return p
    return dim


def mm(groups, out_dtypes, name, tm=None, tn=None, tk=None, epi=None, extras=(), comm=None, sub=1, n_sum=0):
    a0, b0, m0 = groups[0][0]
    M, K0, N = _dims(a0, b0, m0)
    tm = tm or _tile(M, (1024, 512, 256, 128))
    tn = tn or _tile(N, (1024, 512, 256, 128))
    flat = [p for g in groups for p in g]
    nk = 1 if tk is None else K0 // tk
    in_specs, args = [], []
    for a, b, mode in flat:
        _, K, _ = _dims(a, b, mode)
        kb = K if tk is None else tk
        in_specs.append(pl.BlockSpec((kb, tm), lambda i, j, k: (k, i)) if mode[0] == "t"
                        else pl.BlockSpec((tm, kb), lambda i, j, k: (i, k)))
        in_specs.append(pl.BlockSpec((tn, kb), lambda i, j, k: (j, k)) if mode[1] == "t"
                        else pl.BlockSpec((kb, tn), lambda i, j, k: (k, j)))
        args += [a, b]
    for e in extras:
        in_specs.append(pl.BlockSpec((1, tn), lambda i, j, k: (0, j)) if e.shape[0] == 1 and M != 1
                        else pl.BlockSpec((tm, tn), lambda i, j, k: (i, j)))
        args.append(e)
    n_in = len(args)
    n_main = len(out_dtypes)
    n_out = n_main + n_sum
    assert n_sum == 0 or (tn == N and tk is None)
    ng = len(groups)
    sizes = [len(g) for g in groups]

    def body(*refs):
        ins, outs, accs = refs[:n_in], refs[n_in:n_in + n_out], refs[n_in + n_out:]
        kk = pl.program_id(2)

        def dots(rs):
            vals, pos = [], 0
            for gi in range(ng):
                acc = None
                for _ in range(sizes[gi]):
                    mode = flat[pos // 2][2]
                    av = ins[pos][:, rs] if mode[0] == "t" else ins[pos][rs, :]
                    d = _dot(av, ins[pos + 1][...], mode)
                    acc = d if acc is None else acc + d
                    pos += 2
                vals.append(acc)
            return vals

        def finish(accv, rs, first_chunk=True):
            ex = [(r[...] if r.shape[0] == 1 and tm != 1 else r[rs, :]).astype(F32) for r in ins[2 * len(flat):]]
            res = epi(accv, ex) if epi is not None else tuple(accv)
            for o, r in zip(outs[:n_main], res[:n_main]):
                o[rs, :] = r.astype(o.dtype)
            for o, r in zip(outs[n_main:], res[n_main:]):
                if first_chunk:
                    @pl.when(pl.program_id(0) == 0)
                    def _():
                        o[...] = r

                    @pl.when(pl.program_id(0) > 0)
                    def _():
                        o[...] += r
                else:
                    o[...] += r

        if nk == 1:
            for r in range(sub):
                rs = slice(r * (tm // sub), (r + 1) * (tm // sub))
                finish(dots(rs), rs, r == 0)
        else:
            vals = dots(slice(0, tm))
            finish = functools.partial(finish, rs=slice(0, tm))
            @pl.when(kk == 0)
            def _():
                for ar, vv in zip(accs, vals):
                    ar[...] = vv

            @pl.when(kk > 0)
            def _():
                for ar, vv in zip(accs, vals):
                    ar[...] += vv

            @pl.when(kk == nk - 1)
            def _():
                finish([ar[...] for ar in accs])

    grid = (M // tm, N // tn, nk)
    out_specs = [pl.BlockSpec((tm, tn), lambda i, j, k: (i, j)) for _ in out_dtypes] \
        + [pl.BlockSpec((1, tn), lambda i, j, k: (0, j))] * n_sum
    out_shape = [jax.ShapeDtypeStruct((M, N), dt) for dt in out_dtypes] + [jax.ShapeDtypeStruct((1, N), F32)] * n_sum
    scratch = [pltpu.VMEM((tm, tn), F32) for _ in range(ng if nk > 1 else 0)]
    sem = ("arbitrary" if n_sum else "parallel", "parallel", "arbitrary")
    if comm is not None:
        body = _attach(comm, body, n_in, n_out, *_grid_ends(grid))
        in_specs, args = in_specs + [HBM_SPEC] * len(comm.inputs), args + comm.inputs
        out_specs, out_shape = out_specs + [HBM_SPEC] * len(comm.out_shapes), out_shape + comm.out_shapes
        scratch, sem = scratch + comm.sems, ("arbitrary",) * 3
    return pl.pallas_call(body, grid=grid, name=name, in_specs=in_specs, out_specs=out_specs, out_shape=out_shape,
                          scratch_shapes=scratch, compiler_params=_cp(*sem))(*args)


def mm1(a, b, mode, out_dtype, name, **kw):
    return mm([[(a, b, mode)]], [out_dtype], name, **kw)[0]


ROW_BLK = 512


def rowwise(fn, rows, consts, outs, accs, name, tb=ROW_BLK):
    rows = [r if isinstance(r, tuple) else (r, r.shape[1], 0) for r in rows]
    T = rows[0][0].shape[0]
    tb = min(tb, T)
    n_r, n_c, n_o, n_a = len(rows), len(consts), len(outs), len(accs)

    def body(*refs):
        vals = [r[...].astype(F32) for r in refs[:n_r + n_c]]
        res = fn(*vals)
        o_refs = refs[n_r + n_c:n_r + n_c + n_o]
        a_refs = refs[n_r + n_c + n_o:]
        for o, r in zip(o_refs, res[:n_o]):
            o[...] = r.astype(o.dtype)
        if n_a:
            @pl.when(pl.program_id(0) == 0)
            def _():
                for ar in a_refs:
                    ar[...] = jnp.zeros_like(ar)
            for ar, r in zip(a_refs, res[n_o:]):
                ar[...] += r

    return pl.pallas_call(
        body, grid=(T // tb,), name=name,
        in_specs=[pl.BlockSpec((tb, w), functools.partial(lambda i, j: (i, j), j=j)) for _, w, j in rows]
        + [pl.BlockSpec(c.shape, lambda i: (0, 0)) for c in consts],
        out_specs=[pl.BlockSpec((tb, d), lambda i: (i, 0)) for d, _ in outs]
        + [pl.BlockSpec(s, lambda i: (0, 0)) for s in accs],
        out_shape=[jax.ShapeDtypeStruct((T, d), dt) for d, dt in outs]
        + [jax.ShapeDtypeStruct(s, F32) for s in accs],
        compiler_params=_cp("arbitrary" if n_a else "parallel"),
    )(*[r[0] for r in rows], *consts)


def _rms_stats(x):
    r = lax.rsqrt(jnp.mean(x * x, axis=-1, keepdims=True) + EPS)
    return r, x * r


def _rms_bwd(x, g, dy):
    r, xn = _rms_stats(x)
    dyg = dy * g
    dx = r * (dyg - xn * jnp.mean(dyg * xn, axis=-1, keepdims=True))
    return dx, jnp.sum(dy * xn, axis=0, keepdims=True)


def rms_fwd(x, g, name):
    return rowwise(lambda xv, gv: (_rms_stats(xv)[1] * gv,), [x], [g], [(x.shape[1], BF16)], [], name)[0]


def rms_bwd(x, g, dy, name, resid=None, dx_dtype=F32):
    def fn(*v):
        if resid is None:
            xv, dyv, gv = v
            dx, dg = _rms_bwd(xv, gv, dyv)
        else:
            xv, dyv, rv, gv = v
            dx, dg = _rms_bwd(xv, gv, dyv)
            dx = dx + rv
        return dx, dg
    rows = [x, dy] + ([] if resid is None else [resid])
    return rowwise(fn, rows, [g], [(x.shape[1], dx_dtype)], [(1, x.shape[1])], name)


def mm_rms_bwd(pairs, x, g, name, resid=None, dx_dtype=F32, comm=None):
    def epi(accs, ex):
        dx, dg = _rms_bwd(ex[0], ex[-1], accs[0])
        return (dx if resid is None else dx + ex[1]), dg
    extras = [x] + ([] if resid is None else [resid]) + [g]
    return mm([pairs], [dx_dtype], name, tm=min(256, x.shape[0]), tn=x.shape[1], epi=epi, extras=extras, comm=comm, n_sum=1)


def mm_resid(a, b, x, g, wgt, name, comm=None, target=None):
    def epi(accs, ex):
        y = ex[0] + wgt * _rms_stats(accs[0])[1] * ex[1]
        if target is None:
            return accs[0], y
        d = y - ex[2]
        return accs[0], d / D, jnp.sum(d * d, axis=0, keepdims=True)
    return mm([[(a, b, "nn")]], [F32, F32], name, tm=min(512, a.shape[0]), tn=b.shape[1], epi=epi,
              extras=[x, g] + ([] if target is None else [target]), sub=2, comm=comm, n_sum=0 if target is None else 1)


def resid_bwd(h, g, dy, wgt, name):
    def fn(hv, dyv, gv):
        dx, dg = _rms_bwd(hv, gv, dyv)
        return wgt * dx, wgt * dg
    return rowwise(fn, [h, dy], [g], [(h.shape[1], BF16)], [(1, h.shape[1])], name)


def _silu_parts(g):
    s = _sigmoid(g)
    return g * s, s * (1.0 + g * (1.0 - s))


def gated_norm_fwd(y, z, g, name):
    W = SSD_INNER // SSD_G

    def fn(yv, zv, gv):
        yg = yv * _silu_parts(zv)[0]
        return (jnp.concatenate([_rms_stats(yg[:, i * W:(i + 1) * W])[1] for i in range(SSD_G)], axis=1) * gv,)
    return rowwise(fn, [y, z], [g], [(SSD_INNER, BF16)], [], name)[0]


def gated_norm_bwd(y, z, dyn, g, name):
    W = SSD_INNER // SSD_G

    def fn(yv, zv, dv, gv):
        sil, dsil = _silu_parts(zv)
        yg = yv * sil
        parts = [_rms_bwd(yg[:, i * W:(i + 1) * W], gv[:, i * W:(i + 1) * W], dv[:, i * W:(i + 1) * W]) for i in range(SSD_G)]
        dyg = jnp.concatenate([p[0] for p in parts], axis=1)
        dg = jnp.concatenate([p[1] for p in parts], axis=1)
        return dyg * sil, dyg * yv * dsil, dg
    return rowwise(fn, [y, z, dyn], [g], [(SSD_INNER, BF16), (SSD_INNER, BF16)], [(1, SSD_INNER)], name)


def merge_fwd(gl, ys, ym, gb, name):
    def fn(glv, ysv, ymv, gbv):
        gt = _sigmoid(glv + gbv)
        return (gt[:, :D] * ysv + gt[:, D:] * ymv,)
    return rowwise(fn, [gl, ys, ym], [gb], [(D, BF16)], [], name)[0]


def merge_bwd(gl, ys, ym, dm, gb, name):
    def fn(glv, ysv, ymv, dmv, gbv):
        gt = _sigmoid(glv + gbv)
        gs, gm = gt[:, :D], gt[:, D:]
        dgl = jnp.concatenate([dmv * ysv * gs * (1.0 - gs), dmv * ymv * gm * (1.0 - gm)], axis=1)
        return dmv * gs, dmv * gm, dgl, jnp.sum(dgl, axis=0, keepdims=True)
    return rowwise(fn, [gl, ys, ym, dm], [gb], [(D, BF16), (D, BF16), (2 * D, BF16)], [(1, 2 * D)], name)


def loss_head(y, tgt, name):
    def fn(yv, tv):
        d = yv - tv
        part = 0.5 * jnp.sum(jnp.sum(d * d, axis=1, keepdims=True), axis=0, keepdims=True) / D
        return d / D, jnp.broadcast_to(part, (1, 128))
    return rowwise(fn, [y, tgt], [], [(D, F32)], [(1, 128)], name)


def _adamw_math(wv, gv, mv, vv):
    mn = B1 * mv + (1.0 - B1) * gv
    vn = B2 * vv + (1.0 - B2) * (gv * gv)
    mh = mn / (1.0 - B1 ** STEP)
    vh = vn / (1.0 - B2 ** STEP)
    return -LR * (mh / (jnp.sqrt(vh) + AEPS) + WD * wv), mn, vn


def adamw(w, g, m, v, name):
    R, C = w.shape
    tb = _tile(R, (256, 128, 64, 32, 16, 8))
    return rowwise(_adamw_math, [w, g, m, v], [], [(C, F32)] * 3, [], name, tb=tb)


def adamw_from_slots(recv, piece, w, m, v, name):
    K, n = w.shape
    ns = recv.shape[0]
    assert recv.shape[2] == n and recv.shape[1] % K == 0
    tb = _tile(K, (256, 176, 128, 64, 32, 16, 8)) if K % 8 == 0 else K
    r_spec = pl.BlockSpec((ns, tb, n), lambda i: (0, piece * (K // tb) + i, 0))
    w_spec = pl.BlockSpec((tb, n), lambda i: (i, 0))

    def body(r_ref, w_ref, m_ref, v_ref, g_ref, d_ref, mo_ref, vo_ref):
        g = r_ref[0].astype(F32)
        for s in range(1, ns):
            g = g + r_ref[s].astype(F32)
        g_ref[...] = g
        d_ref[...], mo_ref[...], vo_ref[...] = _adamw_math(w_ref[...], g, m_ref[...], v_ref[...])

    return pl.pallas_call(
        body, grid=(K // tb,), name=name, in_specs=[r_spec, w_spec, w_spec, w_spec], out_specs=[w_spec] * 4,
        out_shape=[jax.ShapeDtypeStruct((K, n), F32)] * 4, compiler_params=_cp("parallel"),
    )(recv, w, m, v)


def _me():
    return lax.axis_index("x"), lax.axis_index("y"), lax.axis_index("c")


def _dev_index():
    x, y, c = _me()
    return 4 * x + 2 * y + c


HBM_SPEC = pl.BlockSpec(memory_space=pl.ANY)


class GatherComm:
    def __init__(self, shards):
        self.inputs = [s for s, _ in shards]
        self.rows = [list(r) for _, r in shards]
        n = len(shards)
        self.out_shapes = [jax.ShapeDtypeStruct((N_DEV, r, s.shape[1]), s.dtype) for s, rows in shards for r in rows]
        self.sems = [pltpu.SemaphoreType.DMA((7 * n,)), pltpu.SemaphoreType.DMA((7 * n,)), pltpu.SemaphoreType.DMA((n,))]

    def _plan(self, x_refs, out_refs, sems):
        send_sems, recv_sems, local_sems = sems
        x, y, c = _me()
        me, sibling = (x, y, c), (x, y, 1 - c)
        chips = [(1 - x, y), (x, 1 - y), (1 - x, 1 - y)]
        index = lambda px, py, pc: 4 * px + 2 * py + pc
        mine, first, passed, whole = [], [], [], []
        pos = 0
        for i, rows in enumerate(self.rows):
            kw = lambda k: dict(send_sem=send_sems.at[7 * i + k], recv_sem=recv_sems.at[7 * i + k], device_id_type=MESH)
            r0 = 0
            fwd = [[] for _ in chips]
            for j, nr in enumerate(rows):
                out, src = out_refs[pos + j], x_refs[i].at[pl.ds(r0, nr)]
                mine.append(pltpu.make_async_copy(src, out.at[index(*me)], local_sems.at[i]))
                first.append(pltpu.make_async_remote_copy(src_ref=src, dst_ref=out.at[index(*me)], device_id=sibling, **kw(0)))
                for jj, chip in enumerate(chips):
                    first.append(pltpu.make_async_remote_copy(src_ref=src, dst_ref=out.at[index(*me)], device_id=(*chip, c),
                                                              **kw(1 + jj)))
                    blk = out.at[index(*chip, c)]
                    fwd[jj].append(pltpu.make_async_remote_copy(src_ref=blk, dst_ref=blk, device_id=sibling, **kw(4 + jj)))
                r0 += nr
            passed.append(fwd)
            whole.append([pltpu.make_async_remote_copy(src_ref=x_refs[i], dst_ref=x_refs[i], device_id=sibling, **kw(k))
                          for k in range(7)])
            pos += len(rows)
        return mine, first, passed, whole

    def start(self, x_refs, out_refs, sems):
        mine, first, _, _ = self._plan(x_refs, out_refs, sems)
        for cp in mine + first:
            cp.start()

    def finish(self, x_refs, out_refs, sems):
        _, _, passed, whole = self._plan(x_refs, out_refs, sems)
        local_sems = sems[2]
        for i, fwd in enumerate(passed):
            for jj in range(3):
                whole[i][1 + jj].wait_recv()
                for cp in fwd[jj]:
                    cp.start()
        for i in range(len(passed)):
            whole[i][0].wait_recv()
            for jj in range(3):
                whole[i][4 + jj].wait_recv()
        for i in range(len(passed)):
            for k in range(7):
                whole[i][k].wait_send()
            pltpu.make_async_copy(x_refs[i], x_refs[i], local_sems.at[i]).wait()


def run_comm(comm, name):
    n_in, n_out = len(comm.inputs), len(comm.out_shapes)

    def body(*refs):
        ins, outs, sems = refs[:n_in], refs[n_in:n_in + n_out], refs[n_in + n_out:]
        comm.start(ins, outs, sems)
        comm.finish(ins, outs, sems)

    return pl.pallas_call(body, name=name, out_shape=comm.out_shapes, in_specs=[HBM_SPEC] * n_in,
                          out_specs=[HBM_SPEC] * n_out, scratch_shapes=comm.sems)(*comm.inputs)


def _attach(comm, body, n_in, n_out, first, last):
    if comm is None:
        return body
    ci, co, cs = len(comm.inputs), len(comm.out_shapes), len(comm.sems)

    def wrapped(*refs):
        h_in, c_in = refs[:n_in], refs[n_in:n_in + ci]
        h_out, c_out = refs[n_in + ci:n_in + ci + n_out], refs[n_in + ci + n_out:n_in + ci + n_out + co]
        rest = refs[n_in + ci + n_out + co:]
        h_scr, c_sem = rest[:len(rest) - cs], rest[len(rest) - cs:]

        @pl.when(first())
        def _():
            comm.start(c_in, c_out, c_sem)

        body(*h_in, *h_out, *h_scr)

        @pl.when(last())
        def _():
            comm.finish(c_in, c_out, c_sem)

    return wrapped


def _grid_ends(grid):
    first = lambda: functools.reduce(lambda a, b: a & b, [pl.program_id(i) == 0 for i in range(len(grid))])
    last = lambda: functools.reduce(lambda a, b: a & b, [pl.program_id(i) == g - 1 for i, g in enumerate(grid)])
    return first, last


def _call_with_comm(body, grid, name, in_specs, args, out_specs, out_shape, comm, scratch=(), sem=None):
    sem = sem or ("parallel",) * len(grid)
    scratch = list(scratch)
    if comm is not None:
        body = _attach(comm, body, len(args), len(out_shape), *_grid_ends(grid))
        in_specs, args = in_specs + [HBM_SPEC] * len(comm.inputs), args + comm.inputs
        out_specs, out_shape = out_specs + [HBM_SPEC] * len(comm.out_shapes), out_shape + comm.out_shapes
        scratch, sem = scratch + comm.sems, ("arbitrary",) * len(grid)
    return pl.pallas_call(body, grid=grid, name=name, in_specs=in_specs, out_specs=out_specs, out_shape=out_shape,
                          scratch_shapes=scratch, compiler_params=_cp(*sem))(*args)


class ScatterComm:
    def __init__(self, groups):
        self.sizes = [len(g) for g in groups]
        self.rows = [[pc.shape[1] for pc in g] for g in groups]
        ng = len(groups)
        self.inputs = [pc for g in groups for pc in g]
        self.out_shapes = [jax.ShapeDtypeStruct((N_DEV, sum(self.rows[gi]), g[0].shape[2]), g[0].dtype) for gi, g in enumerate(groups)]
        self.sems = [pltpu.SemaphoreType.DMA((7 * ng,)), pltpu.SemaphoreType.DMA((7 * ng,)), pltpu.SemaphoreType.DMA((ng,))]

    def _peers(self):
        x, y, c = _me()
        out = []
        for k in range(1, N_DEV):
            px = 1 - x if k & 4 else x
            py = 1 - y if k & 2 else y
            pc = 1 - c if k & 1 else c
            out.append((k, 4 * px + 2 * py + pc, dict(device_id=(px, py, pc), device_id_type=MESH)))
        return 4 * x + 2 * y + c, out

    def start(self, ins, outs, sems):
        send_sems, recv_sems, local_sems = sems
        me, peers = self._peers()
        pos = 0
        for gi, size in enumerate(self.sizes):
            for i, pc in enumerate(ins[pos:pos + size]):
                dst = outs[gi].at[me, pl.ds(sum(self.rows[gi][:i]), self.rows[gi][i])]
                pltpu.make_async_copy(pc.at[me], dst, local_sems.at[gi]).start()
                for k, peer, kw in peers:
                    pltpu.make_async_remote_copy(src_ref=pc.at[peer], dst_ref=dst, send_sem=send_sems.at[7 * gi + k - 1],
                                                 recv_sem=recv_sems.at[7 * gi + k - 1], **kw).start()
            pos += size

    def finish(self, ins, outs, sems):
        send_sems, recv_sems, local_sems = sems
        me, peers = self._peers()
        whole = [pltpu.make_async_remote_copy(src_ref=outs[gi].at[peer], dst_ref=outs[gi].at[peer],
                                              send_sem=send_sems.at[7 * gi + k - 1], recv_sem=recv_sems.at[7 * gi + k - 1], **kw)
                 for gi in range(len(self.sizes)) for k, peer, kw in peers]
        for cp in whole:
            cp.wait_recv()
        for cp in whole:
            cp.wait_send()
        for gi in range(len(self.sizes)):
            pltpu.make_async_copy(outs[gi].at[me], outs[gi].at[me], local_sems.at[gi]).wait()


def sum_slots(recv, name, tr):
    n, R, C = recv.shape

    def body(r_ref, o_ref):
        acc = r_ref[0].astype(F32)
        for s in range(1, n):
            acc = acc + r_ref[s].astype(F32)
        o_ref[...] = acc

    return pl.pallas_call(
        body, grid=(R // tr,), name=name,
        in_specs=[pl.BlockSpec((n, tr, C), lambda i: (0, i, 0))], out_specs=pl.BlockSpec((tr, C), lambda i: (i, 0)),
        out_shape=jax.ShapeDtypeStruct((R, C), F32), compiler_params=_cp("parallel"),
    )(recv)


PACK_W, FLAT_W = 1024, 128
MAIN = [
    ("ffn1_w_gate", "col"), ("ffn1_w_up", "col"), ("ffn1_w_down", "row"),
    ("ffn2_w_gate", "col"), ("ffn2_w_up", "col"), ("ffn2_w_down", "row"),
    ("w_ssd_proj", "row"), ("w_mla_proj", "row"), ("w_out", "row"),
    ("w_xq", "row"), ("w_xk", "row"), ("w_xv", "row"), ("w_xo", "row"),
    ("w_uk", "col"), ("w_uv", "col"),
]
FLAT = [("w_in", "col"), ("w_uq", "col")]
BIG = MAIN + FLAT
SMALL = ["ffn1_pre_g", "ffn1_post_g", "mix_pre_g", "conv_b", "dt_bias", "a_log", "d_skip", "ssd_norm_g", "q_norm_g",
         "kv_norm_g", "gate_bias", "mix_post_g", "xa_pre_g", "mem_norm_g", "xa_post_g", "ffn2_pre_g", "ffn2_post_g"]
WEIGHTS = ['ffn1_pre_g', 'ffn1_w_gate', 'ffn1_w_up', 'ffn1_w_down', 'ffn1_post_g', 'mix_pre_g', 'w_in', 'conv_w', 'conv_b',
           'dt_bias', 'a_log', 'd_skip', 'ssd_norm_g', 'w_ssd_proj', 'q_norm_g', 'w_uq', 'kv_norm_g', 'w_uk', 'w_uv',
           'w_mla_proj', 'gate_bias', 'w_out', 'mix_post_g', 'xa_pre_g', 'mem_norm_g', 'w_xq', 'w_xk', 'w_xv', 'w_xo',
           'xa_post_g', 'ffn2_pre_g', 'ffn2_w_gate', 'ffn2_w_up', 'ffn2_w_down', 'ffn2_post_g']


def _pack_rows(w, kind, width):
    m = w[0].T if kind == "col" else w[0]
    return m.reshape(-1, width)


KIND = dict(BIG)
GATHER_PLAN = {
    "first": (["ffn1_w_gate", "ffn1_w_up"], []),
    "ffn1_gate_up": (["ffn1_w_down"], ["w_in@0"]),
    "ffn1_down": ([], ["w_in@1"]),
    "ssd_fwd": (["w_ssd_proj", "w_mla_proj", "w_out", "w_uk", "w_uv"], ["w_uq"]),
    "attn_fwd": (["w_xq", "w_xk", "w_xv", "w_xo", "ffn2_w_gate", "ffn2_w_up", "ffn2_w_down"], []),
}
CONV_RIDES_WITH = "w_in@1"
SCATTER_PLAN = {
    "attn_bwd": [["ffn2_w_gate", "ffn2_w_up", "ffn2_w_down"], ["w_xq", "w_xk", "w_xv", "w_xo"]],
    "ssd_bwd": [["w_ssd_proj", "w_mla_proj", "w_out"], ["w_uk", "w_uv"], ["w_uq"]],
    "in_bwd": [["w_in#0"]],
    "ffn1:down_bwd": [["w_in#1"]],
    "ffn1:dwd": [["w_in#2"]],
    "ffn1:dwg": [["ffn1_w_down#0"]],
    "ffn1:dwu": [["ffn1_w_down#1"]],
    "ffn1:gate_up_bwd": [["ffn1_w_gate"]],
    "last": [["ffn1_w_up"]],
}
PARTS = {"w_in@0": ("w_in", 0, 2656), "w_in@1": ("w_in", 2656, 5296),
         "w_in#0": ("w_in", 0, 2656), "w_in#1": ("w_in", 2656, 3984), "w_in#2": ("w_in", 3984, 5296),
         "ffn1_w_down#0": ("ffn1_w_down", 0, 176), "ffn1_w_down#1": ("ffn1_w_down", 176, 352)}


def _parts_of(base, mark):
    return sorted(pn for pn, (b, _, _) in PARTS.items() if b == base and mark in pn)


class Stage:
    def __init__(self, w):
        self.w = w
        self.width = {n: PACK_W if (n, k) in MAIN else FLAT_W for n, k in BIG}
        self.nrows = {n: math.prod(w[n].shape) // self.width[n] for n, _ in BIG}
        self.recv = {}
        self.arrived_parts = {}

    def _rows(self, n):
        return PARTS[n][2] - PARTS[n][1] if n in PARTS else self.nrows[n]

    def _shards(self, tag):
        names_main, names_flat = GATHER_PLAN[tag]

        def pack(n):
            base, r0, r1 = PARTS.get(n, (n, 0, None))
            return _pack_rows(self.w[base], KIND[base], self.width[base])[r0:r1].astype(BF16)
        shards = []
        if names_main:
            pieces = [pack(n) for n in names_main]
            shards.append((jnp.concatenate(pieces, axis=0), [pc.shape[0] for pc in pieces]))
        if names_flat:
            pieces = [pack(n) for n in names_flat]
            if CONV_RIDES_WITH in names_flat:
                pieces.append(_pad_rows(lax.bitcast_convert_type(self.w["conv_w"][0], BF16).reshape(-1, FLAT_W), 16))
            shards.append((jnp.concatenate(pieces, axis=0), [pc.shape[0] for pc in pieces]))
        return shards

    def gather(self, tag):
        return GatherComm(self._shards(tag)) if tag in GATHER_PLAN else None

    def gathered(self, tag, outs, W, p):
        if tag not in GATHER_PLAN:
            return
        names_main, names_flat = GATHER_PLAN[tag]
        outs = list(outs)
        for n in names_main + names_flat:
            rows = outs.pop(0)
            if n in PARTS:
                self.arrived_parts[n] = rows
                base = PARTS[n][0]
                mine = _parts_of(base, "@")
                if not all(pn in self.arrived_parts for pn in mine):
                    continue
                n, rows = base, jnp.concatenate([self.arrived_parts[pn] for pn in mine], axis=1)
            K = self.w[n].shape[1] if KIND[n] == "col" else PACK_W
            W[n] = rows.reshape(-1, K)
        if CONV_RIDES_WITH in names_flat:
            cw = self.w["conv_w"]
            nbits = 2 * math.prod(cw.shape) // FLAT_W
            bits = outs.pop(0)[:, :nbits].reshape((N_DEV,) + cw.shape[1:] + (2,))
            p["conv_w"] = lax.bitcast_convert_type(bits, F32).transpose(1, 0, 2).reshape(cw.shape[1], -1)

    def pieces(self, tag, gw):
        def piece(n):
            if n in PARTS:
                base, r0, r1 = PARTS[n]
                return gw[base].reshape(N_DEV, self.nrows[base], self.width[base])[:, r0:r1]
            return gw[n].reshape(N_DEV, self.nrows[n], self.width[n])
        return [[piece(n) for n in names] for names in SCATTER_PLAN[tag]]

    def scatter(self, tag, gw):
        return ScatterComm(self.pieces(tag, gw)) if tag in SCATTER_PLAN else None

    def scattered(self, tag, outs):
        if tag in SCATTER_PLAN:
            self.recv[tag] = outs


def _pad_rows(a, mult):
    r = (-a.shape[0]) % mult
    return a if r == 0 else jnp.concatenate([a, jnp.zeros((r,) + a.shape[1:], a.dtype)], axis=0)


def _pack_small(vals, loss_row=None, conv_w=None):
    rows = []
    for v in vals:
        f = v.reshape(-1)
        f = jnp.concatenate([f, jnp.zeros(((-f.shape[0]) % 128,), F32)])
        rows.append(f.reshape(-1, 128))
    if conv_w is not None:
        rows.append(conv_w.reshape(-1, 128))
    if loss_row is not None:
        rows.append(loss_row)
    return _pad_rows(jnp.concatenate(rows, axis=0), 8)


def _unpack_small(buf, shapes):
    out, r = [], 0
    for shp in shapes:
        n = math.prod(shp)
        nr = -(-n // 128)
        out.append(buf[r:r + nr].reshape(-1)[:n].reshape(shp))
        r += nr
    return out, r


def _tn(a, b, name, out_dtype=BF16, comm=None):
    M, N = a.shape[1], b.shape[1]
    T = a.shape[0]
    tm = M if M <= 1536 else M // 2
    tk = 1024 if T % 1024 == 0 and T > 1024 else None
    res = mm([[(a, b, "tn")]], [out_dtype], name, tm=tm, tn=N, tk=tk, comm=comm)
    return res[0] if comm is None else (res[0], res[1:])


class NoStage:
    def gather(self, tag):
        return None

    def gathered(self, tag, outs, W, p):
        pass

    def scatter(self, tag, gw):
        return None

    def scattered(self, tag, outs):
        pass


def _ffn_fwd(x, gpre, gpost, W, p, tag, stage, target=None):
    h = rms_fwd(x, gpre, tag + "_pre")

    def swi(accs, ex):
        sil, dsil = _silu_parts(accs[0])
        return sil, accs[1] * dsil, sil * accs[1]
    G, U, A, *arrived = mm([[(h, W[tag + "_w_gate"], "nt")], [(h, W[tag + "_w_up"], "nt")]], [BF16, BF16, BF16], tag + "_gate_up",
                           tn=DFF // 2, epi=swi, comm=stage.gather(tag + "_gate_up"), sub=4 if h.shape[0] % 1024 == 0 else 1)
    stage.gathered(tag + "_gate_up", arrived, W, p)
    H, y, *rest = mm_resid(A, W[tag + "_w_down"], x, gpost, FFN_RES, tag + "_down", comm=stage.gather(tag + "_down"), target=target)
    saved = (x, h, G, U, A, H)
    if target is not None:
        return y, saved, rest[0]
    stage.gathered(tag + "_down", rest, W, p)
    return y, saved


def _ffn_bwd(dy, saved, gpre, gpost, wg_t, wu_t, wd, tag, stage, gw):
    x, h, G, U, A, H = saved
    dH, dgpost = resid_bwd(H, gpost, dy, FFN_RES, tag + "_post_bwd")

    def dswi(accs, ex):
        return accs[0] * ex[1], accs[0] * ex[0]

    def hosted(where, call):
        comm = stage.scatter(tag + ":" + where, gw)
        res = call(comm)
        if comm is None:
            return res
        stage.scattered(tag + ":" + where, res[1])
        return res[0]

    res = hosted("down_bwd", lambda comm: (lambda r: r if comm is None else (r[:2], r[2:]))(
        mm([[(dH, wd, "nt")]], [BF16, BF16], tag + "_down_bwd", tn=DFF // 2, epi=dswi, extras=[G, U], comm=comm,
           sub=4 if dH.shape[0] % 1024 == 0 else 1)))
    dG, dU = res
    gw[tag + "_w_down"] = hosted("dwd", lambda comm: _tn(A, dH, tag + "_dwd", comm=comm))
    gw[tag + "_w_gate"] = hosted("dwg", lambda comm: _tn(dG, h, tag + "_dwg", comm=comm))
    gw[tag + "_w_up"] = hosted("dwu", lambda comm: _tn(dU, h, tag + "_dwu", comm=comm))
    dx, dgpre = hosted("gate_up_bwd", lambda comm: (lambda r: r[:2] if comm is None else (r[:2], r[2:]))(
        mm_rms_bwd([(dG, wg_t, "nn"), (dU, wu_t, "nn")], x, gpre, tag + "_gate_up_bwd", resid=dy, comm=comm)))
    return dx, dgpre, dgpost


def _rope_tables(positions):
    inv = ROPE_THETA ** (-jnp.arange(0, ROPE, 2, dtype=F32) / ROPE)
    ang = positions.astype(F32).reshape(-1)[:, None] * inv
    return jnp.cos(ang), jnp.sin(ang)


def _local_step(x, mem, positions, tgt, W, p, stage=None):
    stage = stage or NoStage()
    nseq = x.shape[0]
    T = nseq * x.shape[1]
    x0 = x.reshape(T, D)
    mem2 = mem.reshape(-1, D)
    cos, sin = _rope_tables(positions)

    x1, ffn1 = _ffn_fwd(x0, p["ffn1_pre_g"], p["ffn1_post_g"], W, p, "ffn1", stage)

    w_in_t = W["w_in"]
    bounds = [0]
    for n in (SSD_INNER, CONV_CH, SSD_H, QR, KVR, ROPE, 2 * D):
        bounds.append(bounds[-1] + n)
    wt_z, wt_xbc, wt_dt, wt_q, wt_kv, wt_kr, wt_gate = [w_in_t[bounds[i]:bounds[i + 1]] for i in range(7)]
    wt_dt, wt_kr = _pad_rows(wt_dt, SLOT), _pad_rows(wt_kr, SLOT)
    wt_dtkr = jnp.concatenate([wt_dt, wt_kr], axis=0)
    hm = rms_fwd(x1, p["mix_pre_g"], "mix_pre")
    z = mm1(hm, wt_z, "nt", BF16, "in_z")
    xbc = mm1(hm, wt_xbc, "nt", BF16, "in_xbc")
    q_c = mm1(hm, wt_q, "nt", F32, "in_q", tn=QR)
    kv_c = mm1(hm, wt_kv, "nt", F32, "in_kv")
    dtkr = mm1(hm, wt_dtkr, "nt", F32, "in_dtkr")
    gl = mm1(hm, wt_gate, "nt", BF16, "in_gate")

    xbc_act = conv_fwd(xbc, p["conv_w"], p["conv_b"], nseq)
    y_ssd_core, prev, *arrived = ssd_fwd(xbc_act, dtkr, p["dt_bias"], p["a_log"], p["d_skip"], nseq, comm=stage.gather("ssd_fwd"))
    stage.gathered("ssd_fwd", arrived, W, p)
    yn = gated_norm_fwd(y_ssd_core, z, p["ssd_norm_g"], "ssd_norm")
    y_ssd = mm1(yn, W["w_ssd_proj"], "nn", BF16, "ssd_proj")

    slot_rows = lambda wt, per: jnp.pad(wt.reshape(MLA_H, per, -1), ((0, 0), (0, SLOT - per), (0, 0))).reshape(MLA_H * SLOT, -1)
    wq_s, wk_s, wv_s = slot_rows(W["w_uq"], QK), slot_rows(W["w_uk"], NOPE), slot_rows(W["w_uv"], VD)
    wo_s = slot_rows(W["w_mla_proj"], VD)
    qn = rms_fwd(q_c, p["q_norm_g"], "q_norm")
    q_s = mm1(qn, wq_s, "nt", BF16, "uq")
    kvn = rms_fwd(kv_c, p["kv_norm_g"], "kv_norm")
    kn_s = mm1(kvn, wk_s, "nt", BF16, "uk")
    v_s = mm1(kvn, wv_s, "nt", BF16, "uv")
    cos16, sin16 = cos, sin
    Qc, Kc = rope_slot_fwd(q_s, kn_s, dtkr, cos16, sin16, "rope")
    o_s, lse, *arrived = attn_slot_fwd(Qc, Kc, v_s, nseq, comm=stage.gather("attn_fwd"))
    stage.gathered("attn_fwd", arrived, W, p)
    y_mla = mm1(o_s, wo_s, "nn", BF16, "mla_proj")

    merged = merge_fwd(gl, y_ssd, y_mla, p["gate_bias"], "merge")
    hmix, x2 = mm_resid(merged, W["w_out"], x1, p["mix_post_g"], 1.0, "mix_out")

    hq = rms_fwd(x2, p["xa_pre_g"], "xa_pre")
    mn = rms_fwd(mem2, p["mem_norm_g"], "mem_norm")
    xq = mm1(hq, W["w_xq"], "nn", BF16, "xq")
    xk = mm1(mn, W["w_xk"], "nn", BF16, "xk")
    xv = mm1(mn, W["w_xv"], "nn", BF16, "xv")
    xo = xattn_fwd(xq, xk, xv, nseq)
    ho, x3 = mm_resid(xo, W["w_xo"], x2, p["xa_post_g"], 1.0, "xo")

    dx4, ffn2, sq_cols = _ffn_fwd(x3, p["ffn2_pre_g"], p["ffn2_post_g"], W, p, "ffn2", stage, target=tgt.reshape(T, D))
    loss_row = (0.5 / D) * jnp.sum(sq_cols.reshape(-1, 128), axis=0, keepdims=True)

    gw, gs = {}, {}
    dx3, gs["ffn2_pre_g"], gs["ffn2_post_g"] = _ffn_bwd(
        dx4, ffn2, p["ffn2_pre_g"], p["ffn2_post_g"], W["ffn2_w_gate"], W["ffn2_w_up"], W["ffn2_w_down"], "ffn2", stage, gw)

    dho, gs["xa_post_g"] = resid_bwd(ho, p["xa_post_g"], dx3, 1.0, "xa_post_bwd")
    dxo = mm1(dho, W["w_xo"], "nt", BF16, "xo_bwd")
    gw["w_xo"] = _tn(xo, dho, "d_w_xo")
    dxq, dxk, dxv = xattn_bwd(xq, xk, xv, dxo, nseq)
    dx2, gs["xa_pre_g"] = mm_rms_bwd([(dxq, W["w_xq"], "nt")], x2, p["xa_pre_g"], "xq_bwd", resid=dx3)
    gw["w_xq"] = _tn(hq, dxq, "d_w_xq")
    dmn = mm([[(dxk, W["w_xk"], "nt"), (dxv, W["w_xv"], "nt")]], [F32], "xkv_bwd")[0]
    gw["w_xk"] = _tn(mn, dxk, "d_w_xk")
    gw["w_xv"] = _tn(mn, dxv, "d_w_xv")
    _, gs["mem_norm_g"] = rms_bwd(mem2, p["mem_norm_g"], dmn, "mem_norm_bwd", dx_dtype=BF16)

    dhmix, gs["mix_post_g"] = resid_bwd(hmix, p["mix_post_g"], dx2, 1.0, "mix_post_bwd")
    dmerged = mm1(dhmix, W["w_out"], "nt", F32, "mix_out_bwd")
    gw["w_out"] = _tn(merged, dhmix, "d_w_out")
    dys, dym, dgl, gs["gate_bias"] = merge_bwd(gl, y_ssd, y_mla, dmerged, p["gate_bias"], "merge_bwd")

    unslot = lambda g, per: g.reshape(MLA_H, SLOT, -1)[:, :per].reshape(MLA_H * per, -1)
    do_s = mm1(dym, wo_s, "nt", BF16, "mla_proj_bwd")
    gw["w_mla_proj"] = unslot(_tn(o_s, dym, "d_w_mla_proj"), VD)
    dQc, dKc, dv_s, *sent = attn_slot_bwd(Qc, Kc, v_s, o_s, lse, do_s, nseq, comm=stage.scatter("attn_bwd", gw))
    stage.scattered("attn_bwd", sent)
    dq_s, dkn_s, dkr = rope_slot_bwd(dQc, dKc, cos16, sin16, "rope_bwd")
    dq_c, gs["q_norm_g"] = mm_rms_bwd([(dq_s, wq_s, "nn")], q_c, p["q_norm_g"], "uq_bwd", dx_dtype=BF16)
    gw["w_uq"] = unslot(_tn(dq_s, qn, "d_w_uq"), QK)
    dkv_c, gs["kv_norm_g"] = mm_rms_bwd([(dkn_s, wk_s, "nn"), (dv_s, wv_s, "nn")], kv_c, p["kv_norm_g"], "ukv_bwd", dx_dtype=BF16)
    gw["w_uk"] = unslot(_tn(dkn_s, kvn, "d_w_uk"), NOPE)
    gw["w_uv"] = unslot(_tn(dv_s, kvn, "d_w_uv"), VD)

    dyn = mm1(dys, W["w_ssd_proj"], "nt", F32, "ssd_proj_bwd")
    gw["w_ssd_proj"] = _tn(yn, dys, "d_w_ssd_proj")
    dyc, dz, gs["ssd_norm_g"] = gated_norm_bwd(y_ssd_core, z, dyn, p["ssd_norm_g"], "ssd_norm_bwd")
    dxbc_act, ddtr, gs["dt_bias"], gs["a_log"], gs["d_skip"], *sent = ssd_bwd(
        xbc_act, dtkr, p["dt_bias"], p["a_log"], p["d_skip"], prev, dyc, nseq, comm=stage.scatter("ssd_bwd", gw))
    stage.scattered("ssd_bwd", sent)
    dxbc, gs["conv_w"], gs["conv_b"] = conv_bwd(xbc, p["conv_w"], p["conv_b"], dxbc_act, nseq)

    gw["w_in"] = jnp.concatenate([_tn(dz, hm, "d_w_in_z"), _tn(dxbc, hm, "d_w_in_xbc"), _tn(ddtr, hm, "d_w_in_dt")[:SSD_H],
                                  _tn(dq_c, hm, "d_w_in_q"), _tn(dkv_c, hm, "d_w_in_kv"), _tn(dkr, hm, "d_w_in_kr")[:ROPE],
                                  _tn(dgl, hm, "d_w_in_gate")], axis=0)
    dx1, gs["mix_pre_g"], *sent = mm_rms_bwd(
        [(dz, wt_z, "nn"), (dxbc, wt_xbc, "nn"), (ddtr, wt_dt, "nn"), (dq_c, wt_q, "nn"), (dkv_c, wt_kv, "nn"),
         (dkr, wt_kr, "nn"), (dgl, wt_gate, "nn")], x1, p["mix_pre_g"], "in_bwd", resid=dx2, comm=stage.scatter("in_bwd", gw))
    stage.scattered("in_bwd", sent)

    dx0, gs["ffn1_pre_g"], gs["ffn1_post_g"] = _ffn_bwd(
        dx1, ffn1, p["ffn1_pre_g"], p["ffn1_post_g"], W["ffn1_w_gate"], W["ffn1_w_up"], W["ffn1_w_down"], "ffn1", stage, gw)
    return loss_row, dx0.reshape(x.shape), gw, gs


def kernel(x, mem, positions, ffn1_pre_g, ffn1_w_gate, ffn1_w_up, ffn1_w_down, ffn1_post_g, mix_pre_g, w_in, conv_w, conv_b, dt_bias, a_log, d_skip, ssd_norm_g, w_ssd_proj, q_norm_g, w_uq, kv_norm_g, w_uk, w_uv, w_mla_proj, gate_bias, w_out, mix_post_g, xa_pre_g, mem_norm_g, w_xq, w_xk, w_xv, w_xo, xa_post_g, ffn2_pre_g, ffn2_w_gate, ffn2_w_up, ffn2_w_down, ffn2_post_g, loss_target, m_ffn1_pre_g, m_ffn1_w_gate, m_ffn1_w_up, m_ffn1_w_down, m_ffn1_post_g, m_mix_pre_g, m_w_in, m_conv_w, m_conv_b, m_dt_bias, m_a_log, m_d_skip, m_ssd_norm_g, m_w_ssd_proj, m_q_norm_g, m_w_uq, m_kv_norm_g, m_w_uk, m_w_uv, m_w_mla_proj, m_gate_bias, m_w_out, m_mix_post_g, m_xa_pre_g, m_mem_norm_g, m_w_xq, m_w_xk, m_w_xv, m_w_xo, m_xa_post_g, m_ffn2_pre_g, m_ffn2_w_gate, m_ffn2_w_up, m_ffn2_w_down, m_ffn2_post_g, v_ffn1_pre_g, v_ffn1_w_gate, v_ffn1_w_up, v_ffn1_w_down, v_ffn1_post_g, v_mix_pre_g, v_w_in, v_conv_w, v_conv_b, v_dt_bias, v_a_log, v_d_skip, v_ssd_norm_g, v_w_ssd_proj, v_q_norm_g, v_w_uq, v_kv_norm_g, v_w_uk, v_w_uv, v_w_mla_proj, v_gate_bias, v_w_out, v_mix_post_g, v_xa_pre_g, v_mem_norm_g, v_w_xq, v_w_xk, v_w_xv, v_w_xo, v_xa_post_g, v_ffn2_pre_g, v_ffn2_w_gate, v_ffn2_w_up, v_ffn2_w_down, v_ffn2_post_g):
    a = dict(locals())
    w = {n: a[n] for n in WEIGHTS}
    m = {n: a["m_" + n] for n in WEIGHTS}
    v = {n: a["v_" + n] for n in WEIGHTS}

    stage = Stage(w)
    W, p = {}, {n: w[n] for n in SMALL}
    stage.gathered("first", run_comm(stage.gather("first"), "allgather_first"), W, p)

    loss_row, grad_x, gw, gs = _local_step(x, mem, positions, loss_target, W, p, stage)

    sm = _pack_small([gs[n] for n in SMALL], loss_row=loss_row, conv_w=gs["conv_w"])
    *recv_last, srecv = run_comm(ScatterComm(stage.pieces("last", gw) + [[jnp.broadcast_to(sm[None], (N_DEV,) + sm.shape)]]),
                                 "exchange_last")
    stage.scattered("last", recv_last)
    s_rows = sum_slots(srecv, "sum_small", tr=sm.shape[0])
    grads, delta, new_m, new_v = {}, {}, {}, {}

    def finish(n, buf, piece):
        col = KIND[n] == "col"
        turn = (lambda t: t.T) if col else (lambda t: t)
        K = w[n].shape[1]
        if col and buf.shape[2] != K:
            buf = buf.reshape(buf.shape[0], -1, K)
        res = adamw_from_slots(buf, piece, turn(w[n][0]), turn(m[n][0]), turn(v[n][0]), "adamw_" + n)
        grads[n], delta[n], new_m[n], new_v[n] = [turn(r)[None] for r in res]

    parts = {}
    for tag, groups in SCATTER_PLAN.items():
        for names, buf in zip(groups, stage.recv[tag]):
            for piece, n in enumerate(names):
                if n in PARTS:
                    parts[n] = sum_slots(buf, "sum_" + n.replace("#", "_"), tr=buf.shape[1])
                else:
                    finish(n, buf, piece)
    for base in sorted({PARTS[pn][0] for pn in parts}):
        rows = jnp.concatenate([parts[pn] for pn in _parts_of(base, "#")], axis=0)
        finish(base, rows[None], 0)
    conv_w_full = p["conv_w"]
    small_g, r1 = _unpack_small(s_rows, [w[n].shape for n in SMALL])
    for n, g in zip(SMALL, small_g):
        grads[n] = g
    ncw = math.prod(conv_w_full.shape) // 128
    cw_grad_full = s_rows[r1:r1 + ncw].reshape(conv_w_full.shape)
    wsh = conv_w.shape[2]
    grads["conv_w"] = lax.dynamic_slice_in_dim(cw_grad_full, _dev_index() * wsh, wsh, axis=1)[None]
    loss = jnp.sum(s_rows[r1 + ncw])

    d_, m_, v_ = adamw(conv_w[0], grads["conv_w"][0], m["conv_w"][0], v["conv_w"][0], "adamw_conv_w")
    delta["conv_w"], new_m["conv_w"], new_v["conv_w"] = d_[None], m_[None], v_[None]
    sp =[_pack_small([t[n] for n in SMALL]) for t in (w, grads, m, v)]
    outs = adamw(sp[0], sp[1], sp[2], sp[3], "adamw_small")
    for t, buf in zip((delta, new_m, new_v), outs):
        vals, _ = _unpack_small(buf, [w[n].shape for n in SMALL])
        for n, val in zip(SMALL, vals):
            t[n] = val
    return (loss, grad_x, *[grads[n] for n in WEIGHTS], *[delta[n] for n in WEIGHTS],
            *[new_m[n] for n in WEIGHTS], *[new_v[n] for n in WEIGHTS])
```

```python
import functools
import math

import jax
import jax.numpy as jnp
from jax import lax
from jax.experimental import pallas as pl
from jax.experimental.pallas import tpu as pltpu

F32, BF16 = jnp.float32, jnp.bfloat16
HI = lax.Precision.HIGHEST
MESH = pl.DeviceIdType.MESH
N_DEV = 8

D = 1024
DFF = 2816
SSD_H, SSD_P, SSD_G, SSD_N, SSD_L = 16, 64, 2, 128, 128
SSD_INNER = SSD_H * SSD_P
CONV_K, CONV_CH = 4, 1536
MLA_H, QR, KVR, NOPE, ROPE, VD = 16, 384, 256, 64, 32, 64
QK = NOPE + ROPE
ROPE_THETA = 10000.0
XA_H, XA_D = 4, 256
EPS = 1e-6
FFN_RES = 0.5
LR, B1, B2, AEPS, WD, STEP = 0.001, 0.9, 0.999, 1e-08, 0.01, 10

VMEM_LIMIT = 56 * 2**20


def _cp(*sem):
    return pltpu.CompilerParams(dimension_semantics=sem, vmem_limit_bytes=VMEM_LIMIT)


def _sigmoid(x):
    return 1.0 / (1.0 + jnp.exp(-x))


def _softplus(x):
    return jnp.where(x > 20.0, x, jnp.log(1.0 + jnp.exp(jnp.minimum(x, 20.0))))


def _dot(a, b, dims="nn"):
    ca = 0 if dims[0] == "t" else 1
    cb = 1 if dims[1] == "t" else 0
    return lax.dot_general(a.astype(BF16), b.astype(BF16), (((ca,), (cb,)), ((), ())), preferred_element_type=F32)


def _dot_sel(a, b, dims="nn", split="a", terms=3):
    r = (a if split == "a" else b).astype(F32)
    out = None
    for t in range(terms):
        piece = r.astype(BF16)
        if t + 1 < terms:
            r = r - piece.astype(F32)
        d = _dot(piece, b, dims) if split == "a" else _dot(a, piece, dims)
        out = d if out is None else out + d
    return out


def _ssd_common(dtr, dtb, alog):
    L = dtr.shape[0]
    dt = _softplus(dtr + dtb)
    a = -jnp.exp(alog)
    adt = dt * a
    r = lax.broadcasted_iota(jnp.int32, (L, L), 0)
    c = lax.broadcasted_iota(jnp.int32, (L, L), 1)
    lower = r >= c
    tri = lower.astype(F32)
    cs = _dot_sel(tri, adt, "nn", split="b")
    cs_t = _dot_sel(adt, tri, "tt")
    return dt, a, cs, cs_t, lower


def _head_expand():
    hh = lax.broadcasted_iota(jnp.int32, (SSD_H, SSD_INNER), 0)
    jj = lax.broadcasted_iota(jnp.int32, (SSD_H, SSD_INNER), 1)
    return ((jj >= hh * SSD_P) & (jj < hh * SSD_P + SSD_P)).astype(F32)


def _head_reduce():
    hh = lax.broadcasted_iota(jnp.int32, (SSD_INNER, SSD_H), 1)
    jj = lax.broadcasted_iota(jnp.int32, (SSD_INNER, SSD_H), 0)
    return ((jj >= hh * SSD_P) & (jj < hh * SSD_P + SSD_P)).astype(F32)


def ssd_fwd(xbc, dtr, dtb, alog, dsk, nseq, comm=None):
    T = xbc.shape[0]
    S = T // nseq
    C = S // SSD_L
    L = SSD_L
    NP = SSD_H // 2

    def body(x_ref, b_ref, c_ref, dtr_ref, dtb_ref, alog_ref, dsk_ref, y_ref, prev_ref, st_ref):
        ci = pl.program_id(1)

        @pl.when(ci == 0)
        def _():
            st_ref[...] = jnp.zeros_like(st_ref)

        dt, a, cs, cs_t, lower = _ssd_common(dtr_ref[:, 0:SSD_H], dtb_ref[...], alog_ref[...])
        E = _head_expand()
        X = x_ref[...].astype(F32)
        dt_e = _dot_sel(dt, E)
        cs_e = _dot_sel(cs, E)
        csl_e = cs_e[L - 1:L, :]
        Xd = X * dt_e
        Xf = Xd * jnp.exp(csl_e - cs_e)
        e_e = jnp.exp(cs_e)
        skip = _dot_sel(dsk_ref[...], E) * X
        lane = lax.broadcasted_iota(jnp.int32, (1, 2 * SSD_P), 1)
        rowp = lax.broadcasted_iota(jnp.int32, (2 * SSD_P, 1), 0)
        for g in range(SSD_G):
            Bg = b_ref[:, g * SSD_N:(g + 1) * SSD_N]
            Cg = c_ref[:, g * SSD_N:(g + 1) * SSD_N]
            cb = _dot(Cg, Bg, "nt")
            for pp in range(NP // SSD_G):
                p = g * (NP // SSD_G) + pp
                sl = slice(p * 2 * SSD_P, (p + 1) * 2 * SSD_P)
                Xd_p = Xd[:, sl]
                yd = jnp.zeros((L, 2 * SSD_P), F32)
                for q in range(2):
                    h = 2 * p + q
                    m = jnp.where(lower, jnp.exp(jnp.minimum(cs[:, h:h + 1] - cs_t[h:h + 1, :], 0.0)), 0.0)
                    mask = (lane >= q * SSD_P) & (lane < (q + 1) * SSD_P)
                    yd = yd + _dot(cb * m, jnp.where(mask, Xd_p, 0.0))
                S0 = st_ref[p]
                prev_ref[0, 0, p] = S0
                z = _dot(Cg, S0, "nt")
                y_ref[:, sl] = (skip[:, sl] + yd + z * e_e[:, sl]).astype(y_ref.dtype)
                h0 = 2 * p
                dec = jnp.where(rowp < SSD_P, jnp.exp(cs[L - 1:L, h0:h0 + 1]), jnp.exp(cs[L - 1:L, h0 + 1:h0 + 2]))
                st_ref[p] = S0 * dec + _dot(Xf[:, sl], Bg, "tn")

    row = lambda b, c: (b * C + c, 0)
    small = pl.BlockSpec((1, SSD_H), lambda b, c: (0, 0))
    return _call_with_comm(
        body, (nseq, C), "ssd_fwd",
        [pl.BlockSpec((L, SSD_INNER), row),
         pl.BlockSpec((L, SSD_G * SSD_N), lambda b, c: (b * C + c, SSD_INNER // (SSD_G * SSD_N))),
         pl.BlockSpec((L, SSD_G * SSD_N), lambda b, c: (b * C + c, SSD_INNER // (SSD_G * SSD_N) + 1)),
         pl.BlockSpec((L, 128), row), small, small, small],
        [xbc, xbc, xbc, dtr, dtb, alog, dsk],
        [pl.BlockSpec((L, SSD_INNER), row), pl.BlockSpec((1, 1, NP, 2 * SSD_P, SSD_N), lambda b, c: (b, c, 0, 0, 0))],
        [jax.ShapeDtypeStruct((T, SSD_INNER), BF16), jax.ShapeDtypeStruct((nseq, C, NP, 2 * SSD_P, SSD_N), F32)],
        comm, scratch=[pltpu.VMEM((NP, 2 * SSD_P, SSD_N), F32)], sem=("parallel", "arbitrary"))


def ssd_bwd(xbc, dtr, dtb, alog, dsk, prev, dy, nseq, comm=None):
    T = xbc.shape[0]
    S = T // nseq
    C = S // SSD_L
    L = SSD_L
    NP = SSD_H // 2

    def body(x_ref, b_ref, c_ref, dtr_ref, dtb_ref, alog_ref, dsk_ref, prev_ref, dy_ref,
             dxbc_ref, ddtr_ref, ddtb_ref, dalog_ref, ddsk_ref, ds_ref, stg_ref):
        bi = pl.program_id(0)
        ci = pl.program_id(1)

        @pl.when(ci == 0)
        def _():
            ds_ref[...] = jnp.zeros_like(ds_ref)

        @pl.when((ci == 0) & (bi == 0))
        def _():
            ddtb_ref[...] = jnp.zeros_like(ddtb_ref)
            dalog_ref[...] = jnp.zeros_like(dalog_ref)
            ddsk_ref[...] = jnp.zeros_like(ddsk_ref)

        dtr = dtr_ref[:, 0:SSD_H]
        dtb = dtb_ref[...]
        dt, a, cs, cs_t, lower = _ssd_common(dtr, dtb, alog_ref[...])
        upper = lax.broadcasted_iota(jnp.int32, (L, L), 1) >= lax.broadcasted_iota(jnp.int32, (L, L), 0)
        E = _head_expand()
        ET = _head_reduce()
        X = x_ref[...].astype(F32)
        dY = dy_ref[...].astype(F32)
        dt_e = _dot_sel(dt, E)
        cs_e = _dot_sel(cs, E)
        csl_e = cs_e[L - 1:L, :]
        f_e = jnp.exp(csl_e - cs_e)
        e_e = jnp.exp(cs_e)
        dsk_e = _dot_sel(dsk_ref[...], E)
        Xd = X * dt_e
        Xf = Xd * f_e
        lane = lax.broadcasted_iota(jnp.int32, (1, 2 * SSD_P), 1)
        rowp = lax.broadcasted_iota(jnp.int32, (2 * SSD_P, 1), 0)
        hsel = lax.broadcasted_iota(jnp.int32, (1, SSD_H), 1)
        dcs = jnp.zeros((L, SSD_H), F32)
        dcsl = jnp.zeros((1, SSD_H), F32)
        for g in range(SSD_G):
            Bg = b_ref[:, g * SSD_N:(g + 1) * SSD_N]
            Cg = c_ref[:, g * SSD_N:(g + 1) * SSD_N]
            cb = _dot(Cg, Bg, "nt")
            cbt = _dot(Bg, Cg, "nt")
            dB = jnp.zeros((L, SSD_N), F32)
            dC = jnp.zeros((L, SSD_N), F32)
            for pp in range(NP // SSD_G):
                p = g * (NP // SSD_G) + pp
                sl = slice(p * 2 * SSD_P, (p + 1) * 2 * SSD_P)
                Xd_p = Xd[:, sl]
                dY_p = dY[:, sl]
                dXd_p = jnp.zeros((L, 2 * SSD_P), F32)
                for q in range(2):
                    h = 2 * p + q
                    mask = (lane >= q * SSD_P) & (lane < (q + 1) * SSD_P)
                    col = cs[:, h:h + 1]
                    rw = cs_t[h:h + 1, :]
                    m = jnp.where(lower, jnp.exp(jnp.minimum(col - rw, 0.0)), 0.0)
                    mt = jnp.where(upper, jnp.exp(jnp.minimum(rw - col, 0.0)), 0.0)
                    dYm = jnp.where(mask, dY_p, 0.0)
                    dW = _dot(dYm, Xd_p, "nt")
                    dWt = _dot(Xd_p, dYm, "nt")
                    w = cb * m
                    wt = cbt * mt
                    dC = dC + _dot(dW * m, Bg)
                    dB = dB + _dot(dWt * mt, Cg)
                    dXd_p = dXd_p + jnp.where(mask, _dot(wt, dY_p), 0.0)
                    qcol = jnp.sum(dW * w, axis=1, keepdims=True) - jnp.sum(dWt * wt, axis=1, keepdims=True)
                    dcs = dcs + qcol * (hsel == h).astype(F32)
                S0 = prev_ref[0, 0, p]
                dSn = ds_ref[p]
                dZ = dY_p * e_e[:, sl]
                dC = dC + _dot(dZ, S0)
                h0 = 2 * p
                el0 = jnp.exp(cs[L - 1:L, h0:h0 + 1])
                el1 = jnp.exp(cs[L - 1:L, h0 + 1:h0 + 2])
                dec = jnp.where(rowp < SSD_P, el0, el1)
                ds_ref[p] = dSn * dec + _dot(dZ, Cg, "tn")
                dXf_p = _dot(Bg, dSn, "nt")
                dB = dB + _dot(Xf[:, sl], dSn)
                rs = jnp.sum(dSn * S0, axis=1, keepdims=True)
                s0 = jnp.sum(jnp.where(rowp < SSD_P, rs, 0.0), axis=0, keepdims=True) * el0
                s1 = jnp.sum(jnp.where(rowp >= SSD_P, rs, 0.0), axis=0, keepdims=True) * el1
                dcsl = dcsl + s0 * (hsel == h0).astype(F32) + s1 * (hsel == h0 + 1).astype(F32)
                y_off = _dot(Cg, S0, "nt") * e_e[:, sl]
                t1 = dY_p * y_off - dXf_p * Xf[:, sl]
                r1 = jnp.where(lane < SSD_P, t1, 0.0)
                c0 = jnp.sum(r1, axis=1, keepdims=True)
                c1 = jnp.sum(t1 - r1, axis=1, keepdims=True)
                dcs = dcs + c0 * (hsel == h0).astype(F32) + c1 * (hsel == h0 + 1).astype(F32)
                t2 = dXf_p * Xf[:, sl]
                r2 = jnp.where(lane < SSD_P, t2, 0.0)
                dcsl = dcsl + jnp.sum(r2, keepdims=True) * (hsel == h0).astype(F32) \
                    + jnp.sum(t2 - r2, keepdims=True) * (hsel == h0 + 1).astype(F32)
                stg_ref[:, sl] = dXd_p + dXf_p * f_e[:, sl]
            dxbc_ref[:, SSD_INNER + g * SSD_N:SSD_INNER + (g + 1) * SSD_N] = dB.astype(dxbc_ref.dtype)
            dxbc_ref[:, SSD_INNER + (SSD_G + g) * SSD_N:SSD_INNER + (SSD_G + g + 1) * SSD_N] = dC.astype(dxbc_ref.dtype)
        dXd = stg_ref[...]
        dxbc_ref[:, 0:SSD_INNER] = (dXd * dt_e + dsk_e * dY).astype(dxbc_ref.dtype)
        rowl = lax.broadcasted_iota(jnp.int32, (L, 1), 0)
        dcs = dcs + jnp.where(rowl == L - 1, dcsl, 0.0)
        dalpha = _dot_sel(upper.astype(F32), dcs, split="b")
        ddt = _dot_sel(dXd * X, ET, terms=2) + dalpha * a
        dalog_ref[...] += jnp.sum(dalpha * dt, axis=0, keepdims=True) * a
        ddtr = ddt * _sigmoid(dtr + dtb)
        spread = (lax.broadcasted_iota(jnp.int32, (SSD_H, 128), 0) == lax.broadcasted_iota(jnp.int32, (SSD_H, 128), 1)).astype(F32)
        ddtr_ref[...] = _dot(ddtr, spread).astype(ddtr_ref.dtype)
        ddtb_ref[...] += jnp.sum(ddtr, axis=0, keepdims=True)
        ddsk_ref[...] += jnp.sum(_dot_sel(dY * X, ET, terms=2), axis=0, keepdims=True)

    rowr = lambda b, c: (b * C + (C - 1 - c), 0)
    small = pl.BlockSpec((1, SSD_H), lambda b, c: (0, 0))
    return _call_with_comm(
        body, (nseq, C), "ssd_bwd",
        [pl.BlockSpec((L, SSD_INNER), rowr),
         pl.BlockSpec((L, SSD_G * SSD_N), lambda b, c: (b * C + (C - 1 - c), SSD_INNER // (SSD_G * SSD_N))),
         pl.BlockSpec((L, SSD_G * SSD_N), lambda b, c: (b * C + (C - 1 - c), SSD_INNER // (SSD_G * SSD_N) + 1)),
         pl.BlockSpec((L, 128), rowr), small, small, small,
         pl.BlockSpec((1, 1, NP, 2 * SSD_P, SSD_N), lambda b, c: (b, C - 1 - c, 0, 0, 0)),
         pl.BlockSpec((L, SSD_INNER), rowr)],
        [xbc, xbc, xbc, dtr, dtb, alog, dsk, prev, dy],
        [pl.BlockSpec((L, CONV_CH), rowr), pl.BlockSpec((L, 128), rowr), small, small, small],
        [jax.ShapeDtypeStruct((T, CONV_CH), BF16), jax.ShapeDtypeStruct((T, 128), BF16),
         jax.ShapeDtypeStruct((1, SSD_H), F32), jax.ShapeDtypeStruct((1, SSD_H), F32), jax.ShapeDtypeStruct((1, SSD_H), F32)],
        comm, scratch=[pltpu.VMEM((NP, 2 * SSD_P, SSD_N), F32), pltpu.VMEM((L, SSD_INNER), F32)], sem=("arbitrary", "arbitrary"))


SLOT = 128
ATT_T = 512
ATT_HP = 1
LOG2E = math.log2(math.e)
Q_SCALE = QK ** -0.5 * LOG2E


def _col_to_row(col):
    n = col.shape[0]
    eye = lax.broadcasted_iota(jnp.int32, (n, n), 0) == lax.broadcasted_iota(jnp.int32, (n, n), 1)
    return jnp.sum(jnp.where(eye, col, 0.0), axis=0, keepdims=True)


def attn_slot_fwd(q, k, v, nseq, comm=None):
    T = q.shape[0]
    S = T // nseq
    t = min(ATT_T, S)
    nb = S // t
    cols = [slice(h * SLOT, (h + 1) * SLOT) for h in range(ATT_HP)]

    def body(q_ref, k_ref, v_ref, o_ref, lse_ref):
        causal = lax.broadcasted_iota(jnp.int32, (t, t), 1) <= lax.broadcasted_iota(jnp.int32, (t, t), 0)
        for qi in range(nb):
            rows = slice(qi * t, (qi + 1) * t)
            state = [None] * ATT_HP
            for kj in range(qi + 1):
                keys = slice(kj * t, (kj + 1) * t)
                for h, c in enumerate(cols):
                    s = _dot(q_ref[rows, c], k_ref[keys, c], "nt")
                    if kj == qi:
                        s = jnp.where(causal, s, -1e30)
                    bm = jnp.max(s, axis=1, keepdims=True)
                    if kj == 0:
                        p = jnp.exp2(s - bm)
                        state[h] = (bm, jnp.sum(p, axis=1, keepdims=True), _dot(p, v_ref[keys, c]))
                    else:
                        m, l, acc = state[h]
                        m_new = jnp.maximum(m, bm)
                        corr = jnp.exp2(m - m_new)
                        p = jnp.exp2(s - m_new)
                        state[h] = (m_new, l * corr + jnp.sum(p, axis=1, keepdims=True), acc * corr + _dot(p, v_ref[keys, c]))
            for h, c in enumerate(cols):
                m, l, acc = state[h]
                o_ref[rows, c] = (acc / l).astype(o_ref.dtype)
                lse_ref[0, h, :, rows] = _col_to_row(m + jnp.log2(l))

    blk = pl.BlockSpec((S, ATT_HP * SLOT), lambda b, h: (b, h))
    return _call_with_comm(
        body, (nseq, MLA_H // ATT_HP), "attn_fwd", [blk, blk, blk], [q, k, v],
        [blk, pl.BlockSpec((1, ATT_HP, 1, S), lambda b, h: (b, h, 0, 0))],
        [jax.ShapeDtypeStruct((T, MLA_H * SLOT), BF16), jax.ShapeDtypeStruct((nseq, MLA_H, 1, S), F32)], comm)


def attn_slot_bwd(q, k, v, o, lse, do, nseq, comm=None):
    T = q.shape[0]
    S = T // nseq
    t = min(ATT_T, S)
    nb = S // t
    scale = QK ** -0.5
    cols = [slice(h * SLOT, (h + 1) * SLOT) for h in range(ATT_HP)]

    def body(q_ref, k_ref, v_ref, o_ref, lse_ref, do_ref, dq_ref, dk_ref, dv_ref, dqa_ref):
        causal_t = lax.broadcasted_iota(jnp.int32, (t, t), 0) <= lax.broadcasted_iota(jnp.int32, (t, t), 1)
        ones = jnp.ones((8, SLOT), F32)
        delta = {}
        for qi in range(nb):
            sl = slice(qi * t, (qi + 1) * t)
            for h, c in enumerate(cols):
                prod = do_ref[sl, c].astype(F32) * o_ref[sl, c].astype(F32)
                delta[h, qi] = _dot_sel(ones, prod, "nt", split="b", terms=2)[0:1, :]
        for kj in range(nb):
            ks = slice(kj * t, (kj + 1) * t)
            dk = [None] * ATT_HP
            dv = [None] * ATT_HP
            for qi in range(kj, nb):
                sl = slice(qi * t, (qi + 1) * t)
                for h, c in enumerate(cols):
                    kb, vb, qb, dob = k_ref[ks, c], v_ref[ks, c], q_ref[sl, c], do_ref[sl, c]
                    st = _dot(kb, qb, "nt")
                    pt = jnp.exp2(st - lse_ref[0, h, :, sl])
                    if qi == kj:
                        pt = jnp.where(causal_t, pt, 0.0)
                    dpt = _dot(vb, dob, "nt")
                    dst = (pt * (dpt - delta[h, qi])).astype(BF16)
                    dvc = _dot(pt, dob)
                    dkc = _dot(dst, qb) * (1.0 / LOG2E)
                    dv[h] = dvc if dv[h] is None else dv[h] + dvc
                    dk[h] = dkc if dk[h] is None else dk[h] + dkc
                    dqc = _dot(dst, kb, "tn") * scale
                    if kj > 0:
                        dqc = dqc + dqa_ref[sl, c]
                    if qi == kj:
                        dq_ref[sl, c] = dqc.astype(dq_ref.dtype)
                    else:
                        dqa_ref[sl, c] = dqc
            for h, c in enumerate(cols):
                dk_ref[ks, c] = dk[h].astype(dk_ref.dtype)
                dv_ref[ks, c] = dv[h].astype(dv_ref.dtype)

    blk = pl.BlockSpec((S, ATT_HP * SLOT), lambda b, h: (b, h))
    lse_spec = pl.BlockSpec((1, ATT_HP, 1, S), lambda b, h: (b, h, 0, 0))
    W = MLA_H * SLOT
    return _call_with_comm(
        body, (nseq, MLA_H // ATT_HP), "attn_bwd", [blk, blk, blk, blk, lse_spec, blk], [q, k, v, o, lse, do], [blk, blk, blk],
        [jax.ShapeDtypeStruct((T, W), BF16)] * 3, comm, scratch=[pltpu.VMEM((S, ATT_HP * SLOT), F32)])


def _rope_coeffs(cos, sin):
    half = ROPE // 2
    r = lax.broadcasted_iota(jnp.int32, (half, SLOT), 0)
    c = lax.broadcasted_iota(jnp.int32, (half, SLOT), 1)
    pc = ((c == r + NOPE) | (c == r + NOPE + half)).astype(F32)
    ps = (c == r + NOPE + half).astype(F32) - (c == r + NOPE).astype(F32)
    lane = lax.broadcasted_iota(jnp.int32, (1, SLOT), 1)
    return _dot_sel(cos, pc) + (lane < NOPE).astype(F32), _dot_sel(sin, ps)


def _rope_swap(x):
    W = x.shape[1]
    half = ROPE // 2
    lane = lax.broadcasted_iota(jnp.int32, (1, W), 1) & (SLOT - 1)
    up = pltpu.roll(x, W - half, axis=1)
    dn = pltpu.roll(x, half, axis=1)
    return jnp.where((lane >= NOPE) & (lane < NOPE + half), up, jnp.where((lane >= NOPE + half) & (lane < QK), dn, 0.0))


def rope_slot_fwd(q, kn, dtkr, cos, sin, name):
    def fn(qv, knv, krv, cv, sv):
        C, Sg = _rope_coeffs(cv, sv)
        ct, stl = jnp.tile(C, (1, MLA_H)), jnp.tile(Sg, (1, MLA_H))
        qo = (qv * ct + _rope_swap(qv) * stl) * Q_SCALE
        r = lax.broadcasted_iota(jnp.int32, (SLOT, SLOT), 0)
        c = lax.broadcasted_iota(jnp.int32, (SLOT, SLOT), 1)
        place = ((c == r + NOPE) & (r < ROPE)).astype(F32)
        kr = _dot_sel(krv, place)
        kr = kr * C + _rope_swap(kr) * Sg
        return qo, knv.astype(F32) + jnp.tile(kr, (1, MLA_H))
    W = MLA_H * SLOT
    return rowwise(fn, [q, kn, (dtkr, SLOT, 1), cos, sin], [], [(W, BF16), (W, BF16)], [], name)


def rope_slot_bwd(dq, dk, cos, sin, name):
    def fn(dqv, dkv, cv, sv):
        C, Sg = _rope_coeffs(cv, sv)
        ct, stl = jnp.tile(C, (1, MLA_H)), jnp.tile(Sg, (1, MLA_H))
        dqo = dqv * ct - _rope_swap(dqv) * stl
        tot = dkv[:, 0:SLOT]
        for h in range(1, MLA_H):
            tot = tot + dkv[:, h * SLOT:(h + 1) * SLOT]
        u = tot * C - _rope_swap(tot) * Sg
        r = lax.broadcasted_iota(jnp.int32, (SLOT, SLOT), 0)
        c = lax.broadcasted_iota(jnp.int32, (SLOT, SLOT), 1)
        unplace = ((r == c + NOPE) & (c < ROPE)).astype(F32)
        return dqo, dkv, _dot_sel(u, unplace, terms=2)
    W = MLA_H * SLOT
    return rowwise(fn, [dq, dk, cos, sin], [], [(W, BF16), (W, BF16), (SLOT, BF16)], [], name)


XA_BLK = 512


def xattn_fwd(q, k, v, nseq):
    T = q.shape[0]
    S = T // nseq
    M = k.shape[0] // nseq
    tq = min(XA_BLK, S)
    nq = S // tq
    scale = XA_D ** -0.5

    def body(q_ref, k_ref, v_ref, o_ref):
        s = _dot(q_ref[...], k_ref[...], "nt") * scale
        p = jnp.exp(s - jnp.max(s, axis=1, keepdims=True))
        p = p / jnp.sum(p, axis=1, keepdims=True)
        o_ref[...] = _dot(p, v_ref[...]).astype(o_ref.dtype)

    qs = pl.BlockSpec((tq, XA_D), lambda b, h, i: (b * nq + i, h))
    ks = pl.BlockSpec((M, XA_D), lambda b, h, i: (b, h))
    return pl.pallas_call(
        body, grid=(nseq, XA_H, nq), name="xattn_fwd", in_specs=[qs, ks, ks], out_specs=qs,
        out_shape=jax.ShapeDtypeStruct((T, XA_H * XA_D), BF16),
        compiler_params=_cp("parallel", "parallel", "parallel"),
    )(q, k, v)


def xattn_bwd(q, k, v, do, nseq):
    T = q.shape[0]
    S = T // nseq
    M = k.shape[0] // nseq
    tq = min(XA_BLK, S)
    nq = S // tq
    scale = XA_D ** -0.5

    def body(q_ref, k_ref, v_ref, do_ref, dq_ref, dk_ref, dv_ref):
        @pl.when(pl.program_id(2) == 0)
        def _():
            dk_ref[...] = jnp.zeros_like(dk_ref)
            dv_ref[...] = jnp.zeros_like(dv_ref)

        qb, kb, vb, dob = q_ref[...], k_ref[...], v_ref[...], do_ref[...]
        s = _dot(qb, kb, "nt") * scale
        p = jnp.exp(s - jnp.max(s, axis=1, keepdims=True))
        p = p / jnp.sum(p, axis=1, keepdims=True)
        dp = _dot(dob, vb, "nt")
        ds = p * (dp - jnp.sum(dp * p, axis=1, keepdims=True)) * scale
        dq_ref[...] = _dot(ds, kb).astype(dq_ref.dtype)
        dk_ref[...] += _dot(ds, qb, "tn")
        dv_ref[...] += _dot(p, dob, "tn")

    qs = pl.BlockSpec((tq, XA_D), lambda b, h, i: (b * nq + i, h))
    ks = pl.BlockSpec((M, XA_D), lambda b, h, i: (b, h))
    return pl.pallas_call(
        body, grid=(nseq, XA_H, nq), name="xattn_bwd", in_specs=[qs, ks, ks, qs], out_specs=[qs, ks, ks],
        out_shape=[jax.ShapeDtypeStruct((T, XA_H * XA_D), BF16), jax.ShapeDtypeStruct(k.shape, F32),
                   jax.ShapeDtypeStruct(k.shape, F32)],
        compiler_params=_cp("parallel", "parallel", "arbitrary"),
    )(q, k, v, do)


CONV_BLK = 256


def _shift_down(x, s, rows):
    if s == 0:
        return x
    return jnp.where(rows >= s, pltpu.roll(x, s, axis=0), 0.0)


def _shift_up(x, s, rows):
    if s == 0:
        return x
    S = x.shape[0]
    return jnp.where(rows < S - s, pltpu.roll(x, S - s, axis=0), 0.0)


def conv_fwd(x, w, b, nseq):
    T, CH = x.shape
    S = T // nseq

    def body(x_ref, w_ref, b_ref, o_ref):
        xv = x_ref[...].astype(F32)
        rows = lax.broadcasted_iota(jnp.int32, (S, 1), 0)
        c = jnp.zeros_like(xv) + b_ref[...]
        for kk in range(CONV_K):
            c = c + w_ref[kk:kk + 1, :] * _shift_down(xv, CONV_K - 1 - kk, rows)
        o_ref[...] = (c * _sigmoid(c)).astype(o_ref.dtype)

    xs = pl.BlockSpec((S, CONV_BLK), lambda j, bb: (bb, j))
    return pl.pallas_call(
        body, grid=(CH // CONV_BLK, nseq), name="conv_fwd",
        in_specs=[xs, pl.BlockSpec((CONV_K, CONV_BLK), lambda j, bb: (0, j)), pl.BlockSpec((1, CONV_BLK), lambda j, bb: (0, j))],
        out_specs=xs, out_shape=jax.ShapeDtypeStruct((T, CH), BF16),
        compiler_params=_cp("parallel", "parallel"),
    )(x, w, b)


def conv_bwd(x, w, b, dout, nseq):
    T, CH = x.shape
    S = T // nseq

    def body(x_ref, w_ref, b_ref, do_ref, dx_ref, dw_ref, db_ref):
        @pl.when(pl.program_id(1) == 0)
        def _():
            dw_ref[...] = jnp.zeros_like(dw_ref)
            db_ref[...] = jnp.zeros_like(db_ref)

        xv = x_ref[...].astype(F32)
        rows = lax.broadcasted_iota(jnp.int32, (S, 1), 0)
        c = jnp.zeros_like(xv) + b_ref[...]
        sh = [_shift_down(xv, CONV_K - 1 - kk, rows) for kk in range(CONV_K)]
        for kk in range(CONV_K):
            c = c + w_ref[kk:kk + 1, :] * sh[kk]
        sg = _sigmoid(c)
        dc = do_ref[...].astype(F32) * sg * (1.0 + c * (1.0 - sg))
        dx = jnp.zeros_like(xv)
        for kk in range(CONV_K):
            dx = dx + w_ref[kk:kk + 1, :] * _shift_up(dc, CONV_K - 1 - kk, rows)
            dw_ref[kk:kk + 1, :] += jnp.sum(dc * sh[kk], axis=0, keepdims=True)
        dx_ref[...] = dx.astype(dx_ref.dtype)
        db_ref[...] += jnp.sum(dc, axis=0, keepdims=True)

    xs = pl.BlockSpec((S, CONV_BLK), lambda j, bb: (bb, j))
    ws = pl.BlockSpec((CONV_K, CONV_BLK), lambda j, bb: (0, j))
    bs = pl.BlockSpec((1, CONV_BLK), lambda j, bb: (0, j))
    return pl.pallas_call(
        body, grid=(CH // CONV_BLK, nseq), name="conv_bwd",
        in_specs=[xs, ws, bs, xs], out_specs=[xs, ws, bs],
        out_shape=[jax.ShapeDtypeStruct((T, CH), BF16), jax.ShapeDtypeStruct((CONV_K, CH), F32),
                   jax.ShapeDtypeStruct((1, CH), F32)],
        compiler_params=_cp("parallel", "arbitrary"),
    )(x, w, b, dout)


def _dims(a, b, mode):
    M = a.shape[1] if mode[0] == "t" else a.shape[0]
    K = a.shape[0] if mode[0] == "t" else a.shape[1]
    N = b.shape[0] if mode[1] == "t" else b.shape[1]
    return M, K, N


def _tile(dim, prefs):
    for p in prefs:
        if dim % p == 0:
            return p
    return dim


def mm(groups, out_dtypes, name, tm=None, tn=None, tk=None, epi=None, extras=(), comm=None, sub=1, n_sum=0):
    a0, b0, m0 = groups[0][0]
    M, K0, N = _dims(a0, b0, m0)
    tm = tm or _tile(M, (1024, 512, 256, 128))
    tn = tn or _tile(N, (1024, 512, 256, 128))
    flat = [p for g in groups for p in g]
    nk = 1 if tk is None else K0 // tk
    in_specs, args = [], []
    for a, b, mode in flat:
        _, K, _ = _dims(a, b, mode)
        kb = K if tk is None else tk
        in_specs.append(pl.BlockSpec((kb, tm), lambda i, j, k: (k, i)) if mode[0] == "t"
                        else pl.BlockSpec((tm, kb), lambda i, j, k: (i, k)))
        in_specs.append(pl.BlockSpec((tn, kb), lambda i, j, k: (j, k)) if mode[1] == "t"
                        else pl.BlockSpec((kb, tn), lambda i, j, k: (k, j)))
        args += [a, b]
    for e in extras:
        in_specs.append(pl.BlockSpec((1, tn), lambda i, j, k: (0, j)) if e.shape[0] == 1 and M != 1
                        else pl.BlockSpec((tm, tn), lambda i, j, k: (i, j)))
        args.append(e)
    n_in = len(args)
    n_main = len(out_dtypes)
    n_out = n_main + n_sum
    assert n_sum == 0 or (tn == N and tk is None)
    ng = len(groups)
    sizes = [len(g) for g in groups]

    def body(*refs):
        ins, outs, accs = refs[:n_in], refs[n_in:n_in + n_out], refs[n_in + n_out:]
        kk = pl.program_id(2)

        def dots(rs):
            vals, pos = [], 0
            for gi in range(ng):
                acc = None
                for _ in range(sizes[gi]):
                    mode = flat[pos // 2][2]
                    av = ins[pos][:, rs] if mode[0] == "t" else ins[pos][rs, :]
                    d = _dot(av, ins[pos + 1][...], mode)
                    acc = d if acc is None else acc + d
                    pos += 2
                vals.append(acc)
            return vals

        def finish(accv, rs, first_chunk=True):
            ex = [(r[...] if r.shape[0] == 1 and tm != 1 else r[rs, :]).astype(F32) for r in ins[2 * len(flat):]]
            res = epi(accv, ex) if epi is not None else tuple(accv)
            for o, r in zip(outs[:n_main], res[:n_main]):
                o[rs, :] = r.astype(o.dtype)
            for o, r in zip(outs[n_main:], res[n_main:]):
                if first_chunk:
                    @pl.when(pl.program_id(0) == 0)
                    def _():
                        o[...] = r

                    @pl.when(pl.program_id(0) > 0)
                    def _():
                        o[...] += r
                else:
                    o[...] += r

        if nk == 1:
            for r in range(sub):
                rs = slice(r * (tm // sub), (r + 1) * (tm // sub))
                finish(dots(rs), rs, r == 0)
        else:
            vals = dots(slice(0, tm))
            finish = functools.partial(finish, rs=slice(0, tm))
            @pl.when(kk == 0)
            def _():
                for ar, vv in zip(accs, vals):
                    ar[...] = vv

            @pl.when(kk > 0)
            def _():
                for ar, vv in zip(accs, vals):
                    ar[...] += vv

            @pl.when(kk == nk - 1)
            def _():
                finish([ar[...] for ar in accs])

    grid = (M // tm, N // tn, nk)
    out_specs = [pl.BlockSpec((tm, tn), lambda i, j, k: (i, j)) for _ in out_dtypes] \
        + [pl.BlockSpec((1, tn), lambda i, j, k: (0, j))] * n_sum
    out_shape = [jax.ShapeDtypeStruct((M, N), dt) for dt in out_dtypes] + [jax.ShapeDtypeStruct((1, N), F32)] * n_sum
    scratch = [pltpu.VMEM((tm, tn), F32) for _ in range(ng if nk > 1 else 0)]
    sem = ("arbitrary" if n_sum else "parallel", "parallel", "arbitrary")
    if comm is not None:
        body = _attach(comm, body, n_in, n_out, *_grid_ends(grid))
        in_specs, args = in_specs + [HBM_SPEC] * len(comm.inputs), args + comm.inputs
        out_specs, out_shape = out_specs + [HBM_SPEC] * len(comm.out_shapes), out_shape + comm.out_shapes
        scratch, sem = scratch + comm.sems, ("arbitrary",) * 3
    return pl.pallas_call(body, grid=grid, name=name, in_specs=in_specs, out_specs=out_specs, out_shape=out_shape,
                          scratch_shapes=scratch, compiler_params=_cp(*sem))(*args)


def mm1(a, b, mode, out_dtype, name, **kw):
    return mm([[(a, b, mode)]], [out_dtype], name, **kw)[0]


ROW_BLK = 512


def rowwise(fn, rows, consts, outs, accs, name, tb=ROW_BLK):
    rows = [r if isinstance(r, tuple) else (r, r.shape[1], 0) for r in rows]
    T = rows[0][0].shape[0]
    tb = min(tb, T)
    n_r, n_c, n_o, n_a = len(rows), len(consts), len(outs), len(accs)

    def body(*refs):
        vals = [r[...].astype(F32) for r in refs[:n_r + n_c]]
        res = fn(*vals)
        o_refs = refs[n_r + n_c:n_r + n_c + n_o]
        a_refs = refs[n_r + n_c + n_o:]
        for o, r in zip(o_refs, res[:n_o]):
            o[...] = r.astype(o.dtype)
        if n_a:
            @pl.when(pl.program_id(0) == 0)
            def _():
                for ar in a_refs:
                    ar[...] = jnp.zeros_like(ar)
            for ar, r in zip(a_refs, res[n_o:]):
                ar[...] += r

    return pl.pallas_call(
        body, grid=(T // tb,), name=name,
        in_specs=[pl.BlockSpec((tb, w), functools.partial(lambda i, j: (i, j), j=j)) for _, w, j in rows]
        + [pl.BlockSpec(c.shape, lambda i: (0, 0)) for c in consts],
        out_specs=[pl.BlockSpec((tb, d), lambda i: (i, 0)) for d, _ in outs]
        + [pl.BlockSpec(s, lambda i: (0, 0)) for s in accs],
        out_shape=[jax.ShapeDtypeStruct((T, d), dt) for d, dt in outs]
        + [jax.ShapeDtypeStruct(s, F32) for s in accs],
        compiler_params=_cp("arbitrary" if n_a else "parallel"),
    )(*[r[0] for r in rows], *consts)


def _rms_stats(x):
    r = lax.rsqrt(jnp.mean(x * x, axis=-1, keepdims=True) + EPS)
    return r, x * r


def _rms_bwd(x, g, dy):
    r, xn = _rms_stats(x)
    dyg = dy * g
    dx = r * (dyg - xn * jnp.mean(dyg * xn, axis=-1, keepdims=True))
    return dx, jnp.sum(dy * xn, axis=0, keepdims=True)


def rms_fwd(x, g, name):
    return rowwise(lambda xv, gv: (_rms_stats(xv)[1] * gv,), [x], [g], [(x.shape[1], BF16)], [], name)[0]


def rms_bwd(x, g, dy, name, resid=None, dx_dtype=F32):
    def fn(*v):
        if resid is None:
            xv, dyv, gv = v
            dx, dg = _rms_bwd(xv, gv, dyv)
        else:
            xv, dyv, rv, gv = v
            dx, dg = _rms_bwd(xv, gv, dyv)
            dx = dx + rv
        return dx, dg
    rows = [x, dy] + ([] if resid is None else [resid])
    return rowwise(fn, rows, [g], [(x.shape[1], dx_dtype)], [(1, x.shape[1])], name)


def mm_rms_bwd(pairs, x, g, name, resid=None, dx_dtype=F32, comm=None):
    def epi(accs, ex):
        dx, dg = _rms_bwd(ex[0], ex[-1], accs[0])
        return (dx if resid is None else dx + ex[1]), dg
    extras = [x] + ([] if resid is None else [resid]) + [g]
    return mm([pairs], [dx_dtype], name, tm=min(256, x.shape[0]), tn=x.shape[1], epi=epi, extras=extras, comm=comm, n_sum=1)


def mm_resid(a, b, x, g, wgt, name, comm=None, target=None):
    def epi(accs, ex):
        y = ex[0] + wgt * _rms_stats(accs[0])[1] * ex[1]
        if target is None:
            return accs[0], y
        d = y - ex[2]
        return accs[0], d / D, jnp.sum(d * d, axis=0, keepdims=True)
    return mm([[(a, b, "nn")]], [F32, F32], name, tm=min(512, a.shape[0]), tn=b.shape[1], epi=epi,
              extras=[x, g] + ([] if target is None else [target]), sub=2, comm=comm, n_sum=0 if target is None else 1)


def resid_bwd(h, g, dy, wgt, name):
    def fn(hv, dyv, gv):
        dx, dg = _rms_bwd(hv, gv, dyv)
        return wgt * dx, wgt * dg
    return rowwise(fn, [h, dy], [g], [(h.shape[1], BF16)], [(1, h.shape[1])], name)


def _silu_parts(g):
    s = _sigmoid(g)
    return g * s, s * (1.0 + g * (1.0 - s))


def gated_norm_fwd(y, z, g, name):
    W = SSD_INNER // SSD_G

    def fn(yv, zv, gv):
        yg = yv * _silu_parts(zv)[0]
        return (jnp.concatenate([_rms_stats(yg[:, i * W:(i + 1) * W])[1] for i in range(SSD_G)], axis=1) * gv,)
    return rowwise(fn, [y, z], [g], [(SSD_INNER, BF16)], [], name)[0]


def gated_norm_bwd(y, z, dyn, g, name):
    W = SSD_INNER // SSD_G

    def fn(yv, zv, dv, gv):
        sil, dsil = _silu_parts(zv)
        yg = yv * sil
        parts = [_rms_bwd(yg[:, i * W:(i + 1) * W], gv[:, i * W:(i + 1) * W], dv[:, i * W:(i + 1) * W]) for i in range(SSD_G)]
        dyg = jnp.concatenate([p[0] for p in parts], axis=1)
        dg = jnp.concatenate([p[1] for p in parts], axis=1)
        return dyg * sil, dyg * yv * dsil, dg
    return rowwise(fn, [y, z, dyn], [g], [(SSD_INNER, BF16), (SSD_INNER, BF16)], [(1, SSD_INNER)], name)


def merge_fwd(gl, ys, ym, gb, name):
    def fn(glv, ysv, ymv, gbv):
        gt = _sigmoid(glv + gbv)
        return (gt[:, :D] * ysv + gt[:, D:] * ymv,)
    return rowwise(fn, [gl, ys, ym], [gb], [(D, BF16)], [], name)[0]


def merge_bwd(gl, ys, ym, dm, gb, name):
    def fn(glv, ysv, ymv, dmv, gbv):
        gt = _sigmoid(glv + gbv)
        gs, gm = gt[:, :D], gt[:, D:]
        dgl = jnp.concatenate([dmv * ysv * gs * (1.0 - gs), dmv * ymv * gm * (1.0 - gm)], axis=1)
        return dmv * gs, dmv * gm, dgl, jnp.sum(dgl, axis=0, keepdims=True)
    return rowwise(fn, [gl, ys, ym, dm], [gb], [(D, BF16), (D, BF16), (2 * D, BF16)], [(1, 2 * D)], name)


def loss_head(y, tgt, name):
    def fn(yv, tv):
        d = yv - tv
        part = 0.5 * jnp.sum(jnp.sum(d * d, axis=1, keepdims=True), axis=0, keepdims=True) / D
        return d / D, jnp.broadcast_to(part, (1, 128))
    return rowwise(fn, [y, tgt], [], [(D, F32)], [(1, 128)], name)


def _adamw_math(wv, gv, mv, vv):
    mn = B1 * mv + (1.0 - B1) * gv
    vn = B2 * vv + (1.0 - B2) * (gv * gv)
    mh = mn / (1.0 - B1 ** STEP)
    vh = vn / (1.0 - B2 ** STEP)
    return -LR * (mh / (jnp.sqrt(vh) + AEPS) + WD * wv), mn, vn


def adamw(w, g, m, v, name):
    R, C = w.shape
    tb = _tile(R, (256, 128, 64, 32, 16, 8))
    return rowwise(_adamw_math, [w, g, m, v], [], [(C, F32)] * 3, [], name, tb=tb)


def adamw_small(packed, ws, ms, vs):
    k = len(ws)
    sizes = [x.shape[1] for x in ws]

    def body(*refs):
        p_ref, w_refs, m_refs, v_refs = refs[0], refs[1:1 + k], refs[1 + k:1 + 2 * k], refs[1 + 2 * k:1 + 3 * k]
        outs = refs[1 + 3 * k:]
        r0 = 0
        for i, n in enumerate(sizes):
            nr = -(-n // 128)
            g = jnp.concatenate([p_ref[r0 + r:r0 + r + 1, :] for r in range(nr)], axis=1)[:, :n]
            r0 += nr
            outs[i][...] = g
            outs[k + i][...], outs[2 * k + i][...], outs[3 * k + i][...] = _adamw_math(w_refs[i][...], g, m_refs[i][...], v_refs[i][...])

    res = pl.pallas_call(body, name="adamw_small",
                         out_shape=[jax.ShapeDtypeStruct((1, n), F32) for _ in range(4) for n in sizes])(packed, *ws, *ms, *vs)
    return [res[j * k:(j + 1) * k] for j in range(4)]


def adamw_from_slots(recv, piece, w, m, v, name):
    K, n = w.shape
    ns = recv.shape[0]
    assert recv.shape[2] == n and recv.shape[1] % K == 0
    tb = _tile(K, (256, 176, 128, 64, 32, 16, 8)) if K % 8 == 0 else K
    r_spec = pl.BlockSpec((ns, tb, n), lambda i: (0, piece * (K // tb) + i, 0))
    w_spec = pl.BlockSpec((tb, n), lambda i: (i, 0))

    def body(r_ref, w_ref, m_ref, v_ref, g_ref, d_ref, mo_ref, vo_ref):
        g = r_ref[0].astype(F32)
        for s in range(1, ns):
            g = g + r_ref[s].astype(F32)
        g_ref[...] = g
        d_ref[...], mo_ref[...], vo_ref[...] = _adamw_math(w_ref[...], g, m_ref[...], v_ref[...])

    return pl.pallas_call(
        body, grid=(K // tb,), name=name, in_specs=[r_spec, w_spec, w_spec, w_spec], out_specs=[w_spec] * 4,
        out_shape=[jax.ShapeDtypeStruct((K, n), F32)] * 4, compiler_params=_cp("parallel"),
    )(recv, w, m, v)


def _me():
    return lax.axis_index("x"), lax.axis_index("y"), lax.axis_index("c")


def _dev_index():
    x, y, c = _me()
    return 4 * x + 2 * y + c


HBM_SPEC = pl.BlockSpec(memory_space=pl.ANY)


class GatherComm:
    def __init__(self, shards):
        self.inputs = [s for s, _ in shards]
        self.rows = [list(r) for _, r in shards]
        n = len(shards)
        self.out_shapes = [jax.ShapeDtypeStruct((N_DEV, r, s.shape[1]), s.dtype) for s, rows in shards for r in rows]
        self.sems = [pltpu.SemaphoreType.DMA((7 * n,)), pltpu.SemaphoreType.DMA((7 * n,)), pltpu.SemaphoreType.DMA((n,))]

    def _plan(self, x_refs, out_refs, sems):
        send_sems, recv_sems, local_sems = sems
        x, y, c = _me()
        me, sibling = (x, y, c), (x, y, 1 - c)
        chips = [(1 - x, y), (x, 1 - y), (1 - x, 1 - y)]
        index = lambda px, py, pc: 4 * px + 2 * py + pc
        mine, first, passed, whole = [], [], [], []
        pos = 0
        for i, rows in enumerate(self.rows):
            kw = lambda k: dict(send_sem=send_sems.at[7 * i + k], recv_sem=recv_sems.at[7 * i + k], device_id_type=MESH)
            r0 = 0
            fwd = [[] for _ in chips]
            for j, nr in enumerate(rows):
                out, src = out_refs[pos + j], x_refs[i].at[pl.ds(r0, nr)]
                mine.append(pltpu.make_async_copy(src, out.at[index(*me)], local_sems.at[i]))
                first.append(pltpu.make_async_remote_copy(src_ref=src, dst_ref=out.at[index(*me)], device_id=sibling, **kw(0)))
                for jj, chip in enumerate(chips):
                    first.append(pltpu.make_async_remote_copy(src_ref=src, dst_ref=out.at[index(*me)], device_id=(*chip, c),
                                                              **kw(1 + jj)))
                    blk = out.at[index(*chip, c)]
                    fwd[jj].append(pltpu.make_async_remote_copy(src_ref=blk, dst_ref=blk, device_id=sibling, **kw(4 + jj)))
                r0 += nr
            passed.append(fwd)
            whole.append([pltpu.make_async_remote_copy(src_ref=x_refs[i], dst_ref=x_refs[i], device_id=sibling, **kw(k))
                          for k in range(7)])
            pos += len(rows)
        return mine, first, passed, whole

    def start(self, x_refs, out_refs, sems):
        mine, first, _, _ = self._plan(x_refs, out_refs, sems)
        for cp in mine + first:
            cp.start()

    def finish(self, x_refs, out_refs, sems):
        _, _, passed, whole = self._plan(x_refs, out_refs, sems)
        local_sems = sems[2]
        for i, fwd in enumerate(passed):
            for jj in range(3):
                whole[i][1 + jj].wait_recv()
                for cp in fwd[jj]:
                    cp.start()
        for i in range(len(passed)):
            whole[i][0].wait_recv()
            for jj in range(3):
                whole[i][4 + jj].wait_recv()
        for i in range(len(passed)):
            for k in range(7):
                whole[i][k].wait_send()
            pltpu.make_async_copy(x_refs[i], x_refs[i], local_sems.at[i]).wait()


def run_comm(comm, name):
    n_in, n_out = len(comm.inputs), len(comm.out_shapes)

    def body(*refs):
        ins, outs, sems = refs[:n_in], refs[n_in:n_in + n_out], refs[n_in + n_out:]
        comm.start(ins, outs, sems)
        comm.finish(ins, outs, sems)

    return pl.pallas_call(body, name=name, out_shape=comm.out_shapes, in_specs=[HBM_SPEC] * n_in,
                          out_specs=[HBM_SPEC] * n_out, scratch_shapes=comm.sems)(*comm.inputs)


def _attach(comm, body, n_in, n_out, first, last):
    if comm is None:
        return body
    ci, co, cs = len(comm.inputs), len(comm.out_shapes), len(comm.sems)

    def wrapped(*refs):
        h_in, c_in = refs[:n_in], refs[n_in:n_in + ci]
        h_out, c_out = refs[n_in + ci:n_in + ci + n_out], refs[n_in + ci + n_out:n_in + ci + n_out + co]
        rest = refs[n_in + ci + n_out + co:]
        h_scr, c_sem = rest[:len(rest) - cs], rest[len(rest) - cs:]

        @pl.when(first())
        def _():
            comm.start(c_in, c_out, c_sem)

        body(*h_in, *h_out, *h_scr)

        @pl.when(last())
        def _():
            comm.finish(c_in, c_out, c_sem)

    return wrapped


def _grid_ends(grid):
    first = lambda: functools.reduce(lambda a, b: a & b, [pl.program_id(i) == 0 for i in range(len(grid))])
    last = lambda: functools.reduce(lambda a, b: a & b, [pl.program_id(i) == g - 1 for i, g in enumerate(grid)])
    return first, last


def _call_with_comm(body, grid, name, in_specs, args, out_specs, out_shape, comm, scratch=(), sem=None):
    sem = sem or ("parallel",) * len(grid)
    scratch = list(scratch)
    if comm is not None:
        body = _attach(comm, body, len(args), len(out_shape), *_grid_ends(grid))
        in_specs, args = in_specs + [HBM_SPEC] * len(comm.inputs), args + comm.inputs
        out_specs, out_shape = out_specs + [HBM_SPEC] * len(comm.out_shapes), out_shape + comm.out_shapes
        scratch, sem = scratch + comm.sems, ("arbitrary",) * len(grid)
    return pl.pallas_call(body, grid=grid, name=name, in_specs=in_specs, out_specs=out_specs, out_shape=out_shape,
                          scratch_shapes=scratch, compiler_params=_cp(*sem))(*args)


class ScatterComm:
    def __init__(self, groups):
        self.sizes = [len(g) for g in groups]
        self.rows = [[pc.shape[1] for pc in g] for g in groups]
        ng = len(groups)
        self.inputs = [pc for g in groups for pc in g]
        self.out_shapes = [jax.ShapeDtypeStruct((N_DEV, sum(self.rows[gi]), g[0].shape[2]), g[0].dtype) for gi, g in enumerate(groups)]
        self.sems = [pltpu.SemaphoreType.DMA((7 * ng,)), pltpu.SemaphoreType.DMA((7 * ng,)), pltpu.SemaphoreType.DMA((ng,))]

    def _peers(self):
        x, y, c = _me()
        out = []
        for k in range(1, N_DEV):
            px = 1 - x if k & 4 else x
            py = 1 - y if k & 2 else y
            pc = 1 - c if k & 1 else c
            out.append((k, 4 * px + 2 * py + pc, dict(device_id=(px, py, pc), device_id_type=MESH)))
        return 4 * x + 2 * y + c, out

    def start(self, ins, outs, sems):
        send_sems, recv_sems, local_sems = sems
        me, peers = self._peers()
        pos = 0
        for gi, size in enumerate(self.sizes):
            for i, pc in enumerate(ins[pos:pos + size]):
                dst = outs[gi].at[me, pl.ds(sum(self.rows[gi][:i]), self.rows[gi][i])]
                pltpu.make_async_copy(pc.at[me], dst, local_sems.at[gi]).start()
                for k, peer, kw in peers:
                    pltpu.make_async_remote_copy(src_ref=pc.at[peer], dst_ref=dst, send_sem=send_sems.at[7 * gi + k - 1],
                                                 recv_sem=recv_sems.at[7 * gi + k - 1], **kw).start()
            pos += size

    def finish(self, ins, outs, sems):
        send_sems, recv_sems, local_sems = sems
        me, peers = self._peers()
        whole = [pltpu.make_async_remote_copy(src_ref=outs[gi].at[peer], dst_ref=outs[gi].at[peer],
                                              send_sem=send_sems.at[7 * gi + k - 1], recv_sem=recv_sems.at[7 * gi + k - 1], **kw)
                 for gi in range(len(self.sizes)) for k, peer, kw in peers]
        for cp in whole:
            cp.wait_recv()
        for cp in whole:
            cp.wait_send()
        for gi in range(len(self.sizes)):
            pltpu.make_async_copy(outs[gi].at[me], outs[gi].at[me], local_sems.at[gi]).wait()


def sum_slots(recv, name, tr):
    n, R, C = recv.shape

    def body(r_ref, o_ref):
        acc = r_ref[0].astype(F32)
        for s in range(1, n):
            acc = acc + r_ref[s].astype(F32)
        o_ref[...] = acc

    return pl.pallas_call(
        body, grid=(R // tr,), name=name,
        in_specs=[pl.BlockSpec((n, tr, C), lambda i: (0, i, 0))], out_specs=pl.BlockSpec((tr, C), lambda i: (i, 0)),
        out_shape=jax.ShapeDtypeStruct((R, C), F32), compiler_params=_cp("parallel"),
    )(recv)


PACK_W, FLAT_W = 1024, 128
MAIN = [
    ("ffn1_w_gate", "col"), ("ffn1_w_up", "col"), ("ffn1_w_down", "row"),
    ("ffn2_w_gate", "col"), ("ffn2_w_up", "col"), ("ffn2_w_down", "row"),
    ("w_ssd_proj", "row"), ("w_mla_proj", "row"), ("w_out", "row"),
    ("w_xq", "row"), ("w_xk", "row"), ("w_xv", "row"), ("w_xo", "row"),
    ("w_uk", "col"), ("w_uv", "col"),
]
FLAT = [("w_in", "col"), ("w_uq", "col")]
BIG = MAIN + FLAT
SMALL = ["ffn1_pre_g", "ffn1_post_g", "mix_pre_g", "conv_b", "dt_bias", "a_log", "d_skip", "ssd_norm_g", "q_norm_g",
         "kv_norm_g", "gate_bias", "mix_post_g", "xa_pre_g", "mem_norm_g", "xa_post_g", "ffn2_pre_g", "ffn2_post_g"]
WEIGHTS = ['ffn1_pre_g', 'ffn1_w_gate', 'ffn1_w_up', 'ffn1_w_down', 'ffn1_post_g', 'mix_pre_g', 'w_in', 'conv_w', 'conv_b',
           'dt_bias', 'a_log', 'd_skip', 'ssd_norm_g', 'w_ssd_proj', 'q_norm_g', 'w_uq', 'kv_norm_g', 'w_uk', 'w_uv',
           'w_mla_proj', 'gate_bias', 'w_out', 'mix_post_g', 'xa_pre_g', 'mem_norm_g', 'w_xq', 'w_xk', 'w_xv', 'w_xo',
           'xa_post_g', 'ffn2_pre_g', 'ffn2_w_gate', 'ffn2_w_up', 'ffn2_w_down', 'ffn2_post_g']


def _pack_rows(w, kind, width):
    m = w[0].T if kind == "col" else w[0]
    return m.reshape(-1, width)


KIND = dict(BIG)
GATHER_PLAN = {
    "first": (["ffn1_w_gate", "ffn1_w_up"], []),
    "ffn1_gate_up": (["ffn1_w_down"], ["w_in@0"]),
    "ffn1_down": ([], ["w_in@1"]),
    "ssd_fwd": (["w_ssd_proj", "w_mla_proj", "w_out", "w_uk", "w_uv"], ["w_uq"]),
    "attn_fwd": (["w_xq", "w_xk", "w_xv", "w_xo", "ffn2_w_gate", "ffn2_w_up", "ffn2_w_down"], []),
}
CONV_RIDES_WITH = "w_in@1"
SCATTER_PLAN = {
    "attn_bwd": [["ffn2_w_gate", "ffn2_w_up", "ffn2_w_down"], ["w_xq", "w_xk", "w_xv", "w_xo"]],
    "ssd_bwd": [["w_ssd_proj", "w_mla_proj", "w_out"], ["w_uk", "w_uv"], ["w_uq"]],
    "in_bwd": [["w_in#0"]],
    "ffn1:down_bwd": [["w_in#1"]],
    "ffn1:dwd": [["w_in#2"]],
    "ffn1:dwg": [["ffn1_w_down#0"]],
    "ffn1:dwu": [["ffn1_w_down#1"]],
    "ffn1:gate_up_bwd": [["ffn1_w_gate"]],
    "last": [["ffn1_w_up"]],
}
PARTS = {"w_in@0": ("w_in", 0, 2656), "w_in@1": ("w_in", 2656, 5296),
         "w_in#0": ("w_in", 0, 2656), "w_in#1": ("w_in", 2656, 3984), "w_in#2": ("w_in", 3984, 5296),
         "ffn1_w_down#0": ("ffn1_w_down", 0, 176), "ffn1_w_down#1": ("ffn1_w_down", 176, 352)}


def _parts_of(base, mark):
    return sorted(pn for pn, (b, _, _) in PARTS.items() if b == base and mark in pn)


class Stage:
    def __init__(self, w):
        self.w = w
        self.width = {n: PACK_W if (n, k) in MAIN else FLAT_W for n, k in BIG}
        self.nrows = {n: math.prod(w[n].shape) // self.width[n] for n, _ in BIG}
        self.recv = {}
        self.arrived_parts = {}

    def _rows(self, n):
        return PARTS[n][2] - PARTS[n][1] if n in PARTS else self.nrows[n]

    def _shards(self, tag):
        names_main, names_flat = GATHER_PLAN[tag]

        def pack(n):
            base, r0, r1 = PARTS.get(n, (n, 0, None))
            return _pack_rows(self.w[base], KIND[base], self.width[base])[r0:r1].astype(BF16)
        shards = []
        if names_main:
            pieces = [pack(n) for n in names_main]
            shards.append((jnp.concatenate(pieces, axis=0), [pc.shape[0] for pc in pieces]))
        if names_flat:
            pieces = [pack(n) for n in names_flat]
            if CONV_RIDES_WITH in names_flat:
                pieces.append(_pad_rows(lax.bitcast_convert_type(self.w["conv_w"][0], BF16).reshape(-1, FLAT_W), 16))
            shards.append((jnp.concatenate(pieces, axis=0), [pc.shape[0] for pc in pieces]))
        return shards

    def gather(self, tag):
        return GatherComm(self._shards(tag)) if tag in GATHER_PLAN else None

    def gathered(self, tag, outs, W, p):
        if tag not in GATHER_PLAN:
            return
        names_main, names_flat = GATHER_PLAN[tag]
        outs = list(outs)
        for n in names_main + names_flat:
            rows = outs.pop(0)
            if n in PARTS:
                self.arrived_parts[n] = rows
                base = PARTS[n][0]
                mine = _parts_of(base, "@")
                if not all(pn in self.arrived_parts for pn in mine):
                    continue
                n, rows = base, jnp.concatenate([self.arrived_parts[pn] for pn in mine], axis=1)
            K = self.w[n].shape[1] if KIND[n] == "col" else PACK_W
            W[n] = rows.reshape(-1, K)
        if CONV_RIDES_WITH in names_flat:
            cw = self.w["conv_w"]
            nbits = 2 * math.prod(cw.shape) // FLAT_W
            bits = outs.pop(0)[:, :nbits].reshape((N_DEV,) + cw.shape[1:] + (2,))
            p["conv_w"] = lax.bitcast_convert_type(bits, F32).transpose(1, 0, 2).reshape(cw.shape[1], -1)

    def pieces(self, tag, gw):
        def piece(n):
            if n in PARTS:
                base, r0, r1 = PARTS[n]
                return gw[base].reshape(N_DEV, self.nrows[base], self.width[base])[:, r0:r1]
            return gw[n].reshape(N_DEV, self.nrows[n], self.width[n])
        return [[piece(n) for n in names] for names in SCATTER_PLAN[tag]]

    def scatter(self, tag, gw):
        return ScatterComm(self.pieces(tag, gw)) if tag in SCATTER_PLAN else None

    def scattered(self, tag, outs):
        if tag in SCATTER_PLAN:
            self.recv[tag] = outs


def _pad_rows(a, mult):
    r = (-a.shape[0]) % mult
    return a if r == 0 else jnp.concatenate([a, jnp.zeros((r,) + a.shape[1:], a.dtype)], axis=0)


def _pack_small(vals, loss_row=None, conv_w=None):
    rows = []
    for v in vals:
        f = v.reshape(-1)
        f = jnp.concatenate([f, jnp.zeros(((-f.shape[0]) % 128,), F32)])
        rows.append(f.reshape(-1, 128))
    if conv_w is not None:
        rows.append(conv_w.reshape(-1, 128))
    if loss_row is not None:
        rows.append(loss_row)
    return _pad_rows(jnp.concatenate(rows, axis=0), 8)


def _unpack_small(buf, shapes):
    out, r = [], 0
    for shp in shapes:
        n = math.prod(shp)
        nr = -(-n // 128)
        out.append(buf[r:r + nr].reshape(-1)[:n].reshape(shp))
        r += nr
    return out, r


def _tn(a, b, name, out_dtype=BF16, comm=None):
    M, N = a.shape[1], b.shape[1]
    T = a.shape[0]
    tm = M if M <= 1536 else M // 2
    tk = 1024 if T % 1024 == 0 and T > 1024 else None
    res = mm([[(a, b, "tn")]], [out_dtype], name, tm=tm, tn=N, tk=tk, comm=comm)
    return res[0] if comm is None else (res[0], res[1:])


class NoStage:
    def gather(self, tag):
        return None

    def gathered(self, tag, outs, W, p):
        pass

    def scatter(self, tag, gw):
        return None

    def scattered(self, tag, outs):
        pass


def _ffn_fwd(x, gpre, gpost, W, p, tag, stage, target=None):
    h = rms_fwd(x, gpre, tag + "_pre")

    def swi(accs, ex):
        sil, dsil = _silu_parts(accs[0])
        return sil, accs[1] * dsil, sil * accs[1]
    G, U, A, *arrived = mm([[(h, W[tag + "_w_gate"], "nt")], [(h, W[tag + "_w_up"], "nt")]], [BF16, BF16, BF16], tag + "_gate_up",
                           tn=DFF // 2, epi=swi, comm=stage.gather(tag + "_gate_up"), sub=4 if h.shape[0] % 1024 == 0 else 1)
    stage.gathered(tag + "_gate_up", arrived, W, p)
    H, y, *rest = mm_resid(A, W[tag + "_w_down"], x, gpost, FFN_RES, tag + "_down", comm=stage.gather(tag + "_down"), target=target)
    saved = (x, h, G, U, A, H)
    if target is not None:
        return y, saved, rest[0]
    stage.gathered(tag + "_down", rest, W, p)
    return y, saved


def _ffn_bwd(dy, saved, gpre, gpost, wg_t, wu_t, wd, tag, stage, gw):
    x, h, G, U, A, H = saved
    dH, dgpost = resid_bwd(H, gpost, dy, FFN_RES, tag + "_post_bwd")

    def dswi(accs, ex):
        return accs[0] * ex[1], accs[0] * ex[0]

    def hosted(where, call):
        comm = stage.scatter(tag + ":" + where, gw)
        res = call(comm)
        if comm is None:
            return res
        stage.scattered(tag + ":" + where, res[1])
        return res[0]

    res = hosted("down_bwd", lambda comm: (lambda r: r if comm is None else (r[:2], r[2:]))(
        mm([[(dH, wd, "nt")]], [BF16, BF16], tag + "_down_bwd", tn=DFF // 2, epi=dswi, extras=[G, U], comm=comm,
           sub=4 if dH.shape[0] % 1024 == 0 else 1)))
    dG, dU = res
    gw[tag + "_w_down"] = hosted("dwd", lambda comm: _tn(A, dH, tag + "_dwd", comm=comm))
    gw[tag + "_w_gate"] = hosted("dwg", lambda comm: _tn(dG, h, tag + "_dwg", comm=comm))
    gw[tag + "_w_up"] = hosted("dwu", lambda comm: _tn(dU, h, tag + "_dwu", comm=comm))
    dx, dgpre = hosted("gate_up_bwd", lambda comm: (lambda r: r[:2] if comm is None else (r[:2], r[2:]))(
        mm_rms_bwd([(dG, wg_t, "nn"), (dU, wu_t, "nn")], x, gpre, tag + "_gate_up_bwd", resid=dy, comm=comm)))
    return dx, dgpre, dgpost


def _rope_tables(positions):
    inv = ROPE_THETA ** (-jnp.arange(0, ROPE, 2, dtype=F32) / ROPE)
    ang = positions.astype(F32).reshape(-1)[:, None] * inv
    return jnp.cos(ang), jnp.sin(ang)


def _local_step(x, mem, positions, tgt, W, p, stage=None):
    stage = stage or NoStage()
    nseq = x.shape[0]
    T = nseq * x.shape[1]
    x0 = x.reshape(T, D)
    mem2 = mem.reshape(-1, D)
    cos, sin = _rope_tables(positions)

    x1, ffn1 = _ffn_fwd(x0, p["ffn1_pre_g"], p["ffn1_post_g"], W, p, "ffn1", stage)

    w_in_t = W["w_in"]
    bounds = [0]
    for n in (SSD_INNER, CONV_CH, SSD_H, QR, KVR, ROPE, 2 * D):
        bounds.append(bounds[-1] + n)
    wt_z, wt_xbc, wt_dt, wt_q, wt_kv, wt_kr, wt_gate = [w_in_t[bounds[i]:bounds[i + 1]] for i in range(7)]
    wt_dt, wt_kr = _pad_rows(wt_dt, SLOT), _pad_rows(wt_kr, SLOT)
    wt_dtkr = jnp.concatenate([wt_dt, wt_kr], axis=0)
    hm = rms_fwd(x1, p["mix_pre_g"], "mix_pre")
    z = mm1(hm, wt_z, "nt", BF16, "in_z")
    xbc = mm1(hm, wt_xbc, "nt", BF16, "in_xbc")
    q_c = mm1(hm, wt_q, "nt", F32, "in_q", tn=QR)
    kv_c = mm1(hm, wt_kv, "nt", F32, "in_kv")
    dtkr = mm1(hm, wt_dtkr, "nt", F32, "in_dtkr")
    gl = mm1(hm, wt_gate, "nt", BF16, "in_gate")

    xbc_act = conv_fwd(xbc, p["conv_w"], p["conv_b"], nseq)
    y_ssd_core, prev, *arrived = ssd_fwd(xbc_act, dtkr, p["dt_bias"], p["a_log"], p["d_skip"], nseq, comm=stage.gather("ssd_fwd"))
    stage.gathered("ssd_fwd", arrived, W, p)
    yn = gated_norm_fwd(y_ssd_core, z, p["ssd_norm_g"], "ssd_norm")
    y_ssd = mm1(yn, W["w_ssd_proj"], "nn", BF16, "ssd_proj")

    slot_rows = lambda wt, per: jnp.pad(wt.reshape(MLA_H, per, -1), ((0, 0), (0, SLOT - per), (0, 0))).reshape(MLA_H * SLOT, -1)
    wq_s, wk_s, wv_s = slot_rows(W["w_uq"], QK), slot_rows(W["w_uk"], NOPE), slot_rows(W["w_uv"], VD)
    wo_s = slot_rows(W["w_mla_proj"], VD)
    qn = rms_fwd(q_c, p["q_norm_g"], "q_norm")
    q_s = mm1(qn, wq_s, "nt", BF16, "uq")
    kvn = rms_fwd(kv_c, p["kv_norm_g"], "kv_norm")
    kn_s = mm1(kvn, wk_s, "nt", BF16, "uk")
    v_s = mm1(kvn, wv_s, "nt", BF16, "uv")
    cos16, sin16 = cos, sin
    Qc, Kc = rope_slot_fwd(q_s, kn_s, dtkr, cos16, sin16, "rope")
    o_s, lse, *arrived = attn_slot_fwd(Qc, Kc, v_s, nseq, comm=stage.gather("attn_fwd"))
    stage.gathered("attn_fwd", arrived, W, p)
    y_mla = mm1(o_s, wo_s, "nn", BF16, "mla_proj")

    merged = merge_fwd(gl, y_ssd, y_mla, p["gate_bias"], "merge")
    hmix, x2 = mm_resid(merged, W["w_out"], x1, p["mix_post_g"], 1.0, "mix_out")

    hq = rms_fwd(x2, p["xa_pre_g"], "xa_pre")
    mn = rms_fwd(mem2, p["mem_norm_g"], "mem_norm")
    xq = mm1(hq, W["w_xq"], "nn", BF16, "xq")
    xk = mm1(mn, W["w_xk"], "nn", BF16, "xk")
    xv = mm1(mn, W["w_xv"], "nn", BF16, "xv")
    xo = xattn_fwd(xq, xk, xv, nseq)
    ho, x3 = mm_resid(xo, W["w_xo"], x2, p["xa_post_g"], 1.0, "xo")

    dx4, ffn2, sq_cols = _ffn_fwd(x3, p["ffn2_pre_g"], p["ffn2_post_g"], W, p, "ffn2", stage, target=tgt.reshape(T, D))
    loss_row = (0.5 / D) * jnp.sum(sq_cols.reshape(-1, 128), axis=0, keepdims=True)

    gw, gs = {}, {}
    dx3, gs["ffn2_pre_g"], gs["ffn2_post_g"] = _ffn_bwd(
        dx4, ffn2, p["ffn2_pre_g"], p["ffn2_post_g"], W["ffn2_w_gate"], W["ffn2_w_up"], W["ffn2_w_down"], "ffn2", stage, gw)

    dho, gs["xa_post_g"] = resid_bwd(ho, p["xa_post_g"], dx3, 1.0, "xa_post_bwd")
    dxo = mm1(dho, W["w_xo"], "nt", BF16, "xo_bwd")
    gw["w_xo"] = _tn(xo, dho, "d_w_xo")
    dxq, dxk, dxv = xattn_bwd(xq, xk, xv, dxo, nseq)
    dx2, gs["xa_pre_g"] = mm_rms_bwd([(dxq, W["w_xq"], "nt")], x2, p["xa_pre_g"], "xq_bwd", resid=dx3)
    gw["w_xq"] = _tn(hq, dxq, "d_w_xq")
    dmn = mm([[(dxk, W["w_xk"], "nt"), (dxv, W["w_xv"], "nt")]], [F32], "xkv_bwd")[0]
    gw["w_xk"] = _tn(mn, dxk, "d_w_xk")
    gw["w_xv"] = _tn(mn, dxv, "d_w_xv")
    _, gs["mem_norm_g"] = rms_bwd(mem2, p["mem_norm_g"], dmn, "mem_norm_bwd", dx_dtype=BF16)

    dhmix, gs["mix_post_g"] = resid_bwd(hmix, p["mix_post_g"], dx2, 1.0, "mix_post_bwd")
    dmerged = mm1(dhmix, W["w_out"], "nt", F32, "mix_out_bwd")
    gw["w_out"] = _tn(merged, dhmix, "d_w_out")
    dys, dym, dgl, gs["gate_bias"] = merge_bwd(gl, y_ssd, y_mla, dmerged, p["gate_bias"], "merge_bwd")

    unslot = lambda g, per: g.reshape(MLA_H, SLOT, -1)[:, :per].reshape(MLA_H * per, -1)
    do_s = mm1(dym, wo_s, "nt", BF16, "mla_proj_bwd")
    gw["w_mla_proj"] = unslot(_tn(o_s, dym, "d_w_mla_proj"), VD)
    dQc, dKc, dv_s, *sent = attn_slot_bwd(Qc, Kc, v_s, o_s, lse, do_s, nseq, comm=stage.scatter("attn_bwd", gw))
    stage.scattered("attn_bwd", sent)
    dq_s, dkn_s, dkr = rope_slot_bwd(dQc, dKc, cos16, sin16, "rope_bwd")
    dq_c, gs["q_norm_g"] = mm_rms_bwd([(dq_s, wq_s, "nn")], q_c, p["q_norm_g"], "uq_bwd", dx_dtype=BF16)
    gw["w_uq"] = unslot(_tn(dq_s, qn, "d_w_uq"), QK)
    dkv_c, gs["kv_norm_g"] = mm_rms_bwd([(dkn_s, wk_s, "nn"), (dv_s, wv_s, "nn")], kv_c, p["kv_norm_g"], "ukv_bwd", dx_dtype=BF16)
    gw["w_uk"] = unslot(_tn(dkn_s, kvn, "d_w_uk"), NOPE)
    gw["w_uv"] = unslot(_tn(dv_s, kvn, "d_w_uv"), VD)

    dyn = mm1(dys, W["w_ssd_proj"], "nt", F32, "ssd_proj_bwd")
    gw["w_ssd_proj"] = _tn(yn, dys, "d_w_ssd_proj")
    dyc, dz, gs["ssd_norm_g"] = gated_norm_bwd(y_ssd_core, z, dyn, p["ssd_norm_g"], "ssd_norm_bwd")
    dxbc_act, ddtr, gs["dt_bias"], gs["a_log"], gs["d_skip"], *sent = ssd_bwd(
        xbc_act, dtkr, p["dt_bias"], p["a_log"], p["d_skip"], prev, dyc, nseq, comm=stage.scatter("ssd_bwd", gw))
    stage.scattered("ssd_bwd", sent)
    dxbc, gs["conv_w"], gs["conv_b"] = conv_bwd(xbc, p["conv_w"], p["conv_b"], dxbc_act, nseq)

    gw["w_in"] = jnp.concatenate([_tn(dz, hm, "d_w_in_z"), _tn(dxbc, hm, "d_w_in_xbc"), _tn(ddtr, hm, "d_w_in_dt")[:SSD_H],
                                  _tn(dq_c, hm, "d_w_in_q"), _tn(dkv_c, hm, "d_w_in_kv"), _tn(dkr, hm, "d_w_in_kr")[:ROPE],
                                  _tn(dgl, hm, "d_w_in_gate")], axis=0)
    dx1, gs["mix_pre_g"], *sent = mm_rms_bwd(
        [(dz, wt_z, "nn"), (dxbc, wt_xbc, "nn"), (ddtr, wt_dt, "nn"), (dq_c, wt_q, "nn"), (dkv_c, wt_kv, "nn"),
         (dkr, wt_kr, "nn"), (dgl, wt_gate, "nn")], x1, p["mix_pre_g"], "in_bwd", resid=dx2, comm=stage.scatter("in_bwd", gw))
    stage.scattered("in_bwd", sent)

    dx0, gs["ffn1_pre_g"], gs["ffn1_post_g"] = _ffn_bwd(
        dx1, ffn1, p["ffn1_pre_g"], p["ffn1_post_g"], W["ffn1_w_gate"], W["ffn1_w_up"], W["ffn1_w_down"], "ffn1", stage, gw)
    return loss_row, dx0.reshape(x.shape), gw, gs


def kernel(x, mem, positions, ffn1_pre_g, ffn1_w_gate, ffn1_w_up, ffn1_w_down, ffn1_post_g, mix_pre_g, w_in, conv_w, conv_b, dt_bias, a_log, d_skip, ssd_norm_g, w_ssd_proj, q_norm_g, w_uq, kv_norm_g, w_uk, w_uv, w_mla_proj, gate_bias, w_out, mix_post_g, xa_pre_g, mem_norm_g, w_xq, w_xk, w_xv, w_xo, xa_post_g, ffn2_pre_g, ffn2_w_gate, ffn2_w_up, ffn2_w_down, ffn2_post_g, loss_target, m_ffn1_pre_g, m_ffn1_w_gate, m_ffn1_w_up, m_ffn1_w_down, m_ffn1_post_g, m_mix_pre_g, m_w_in, m_conv_w, m_conv_b, m_dt_bias, m_a_log, m_d_skip, m_ssd_norm_g, m_w_ssd_proj, m_q_norm_g, m_w_uq, m_kv_norm_g, m_w_uk, m_w_uv, m_w_mla_proj, m_gate_bias, m_w_out, m_mix_post_g, m_xa_pre_g, m_mem_norm_g, m_w_xq, m_w_xk, m_w_xv, m_w_xo, m_xa_post_g, m_ffn2_pre_g, m_ffn2_w_gate, m_ffn2_w_up, m_ffn2_w_down, m_ffn2_post_g, v_ffn1_pre_g, v_ffn1_w_gate, v_ffn1_w_up, v_ffn1_w_down, v_ffn1_post_g, v_mix_pre_g, v_w_in, v_conv_w, v_conv_b, v_dt_bias, v_a_log, v_d_skip, v_ssd_norm_g, v_w_ssd_proj, v_q_norm_g, v_w_uq, v_kv_norm_g, v_w_uk, v_w_uv, v_w_mla_proj, v_gate_bias, v_w_out, v_mix_post_g, v_xa_pre_g, v_mem_norm_g, v_w_xq, v_w_xk, v_w_xv, v_w_xo, v_xa_post_g, v_ffn2_pre_g, v_ffn2_w_gate, v_ffn2_w_up, v_ffn2_w_down, v_ffn2_post_g):
    a = dict(locals())
    w = {n: a[n] for n in WEIGHTS}
    m = {n: a["m_" + n] for n in WEIGHTS}
    v = {n: a["v_" + n] for n in WEIGHTS}

    stage = Stage(w)
    W, p = {}, {n: w[n] for n in SMALL}
    stage.gathered("first", run_comm(stage.gather("first"), "allgather_first"), W, p)

    loss_row, grad_x, gw, gs = _local_step(x, mem, positions, loss_target, W, p, stage)

    sm = _pack_small([gs[n] for n in SMALL], loss_row=loss_row, conv_w=gs["conv_w"])
    *recv_last, srecv = run_comm(ScatterComm(stage.pieces("last", gw) + [[jnp.broadcast_to(sm[None], (N_DEV,) + sm.shape)]]),
                                 "exchange_last")
    stage.scattered("last", recv_last)
    s_rows = sum_slots(srecv, "sum_small", tr=sm.shape[0])
    grads, delta, new_m, new_v = {}, {}, {}, {}

    def finish(n, buf, piece):
        col = KIND[n] == "col"
        turn = (lambda t: t.T) if col else (lambda t: t)
        K = w[n].shape[1]
        if col and buf.shape[2] != K:
            buf = buf.reshape(buf.shape[0], -1, K)
        res = adamw_from_slots(buf, piece, turn(w[n][0]), turn(m[n][0]), turn(v[n][0]), "adamw_" + n)
        grads[n], delta[n], new_m[n], new_v[n] = [turn(r)[None] for r in res]

    parts = {}
    for tag, groups in SCATTER_PLAN.items():
        for names, buf in zip(groups, stage.recv[tag]):
            for piece, n in enumerate(names):
                if n in PARTS:
                    parts[n] = sum_slots(buf, "sum_" + n.replace("#", "_"), tr=buf.shape[1])
                else:
                    finish(n, buf, piece)
    for base in sorted({PARTS[pn][0] for pn in parts}):
        rows = jnp.concatenate([parts[pn] for pn in _parts_of(base, "#")], axis=0)
        finish(base, rows[None], 0)
    conv_w_full = p["conv_w"]
    small = adamw_small(s_rows, [w[n] for n in SMALL], [m[n] for n in SMALL], [v[n] for n in SMALL])
    for t, vals in zip((grads, delta, new_m, new_v), small):
        t.update(zip(SMALL, vals))
    r1 = sum(-(-w[n].shape[1] // 128) for n in SMALL)
    ncw = math.prod(conv_w_full.shape) // 128
    cw_grad_full = s_rows[r1:r1 + ncw].reshape(conv_w_full.shape)
    wsh = conv_w.shape[2]
    grads["conv_w"] = lax.dynamic_slice_in_dim(cw_grad_full, _dev_index() * wsh, wsh, axis=1)[None]
    loss = jnp.sum(s_rows[r1 + ncw])
    d_, m_, v_ = adamw(conv_w[0], grads["conv_w"][0], m["conv_w"][0], v["conv_w"][0], "adamw_conv_w")
    delta["conv_w"], new_m["conv_w"], new_v["conv_w"] = d_[None], m_[None], v_[None]
    return (loss, grad_x, *[grads[n] for n in WEIGHTS], *[delta[n] for n in WEIGHTS],
            *[new_m[n] for n in WEIGHTS], *[new_v[n] for n in WEIGHTS])
```

```python
import functools
import math

import jax
import jax.numpy as jnp
from jax import lax
from jax.experimental import pallas as pl
from jax.experimental.pallas import tpu as pltpu

F32, BF16 = jnp.float32, jnp.bfloat16
HI = lax.Precision.HIGHEST
MESH = pl.DeviceIdType.MESH
N_DEV = 8

D = 1024
DFF = 2816
SSD_H, SSD_P, SSD_G, SSD_N, SSD_L = 16, 64, 2, 128, 128
SSD_INNER = SSD_H * SSD_P
CONV_K, CONV_CH = 4, 1536
MLA_H, QR, KVR, NOPE, ROPE, VD = 16, 384, 256, 64, 32, 64
QK = NOPE + ROPE
ROPE_THETA = 10000.0
XA_H, XA_D = 4, 256
EPS = 1e-6
FFN_RES = 0.5
LR, B1, B2, AEPS, WD, STEP = 0.001, 0.9, 0.999, 1e-08, 0.01, 10

VMEM_LIMIT = 56 * 2**20


def _cp(*sem):
    return pltpu.CompilerParams(dimension_semantics=sem, vmem_limit_bytes=VMEM_LIMIT)


def _sigmoid(x):
    return 1.0 / (1.0 + jnp.exp(-x))


def _softplus(x):
    return jnp.where(x > 20.0, x, jnp.log(1.0 + jnp.exp(jnp.minimum(x, 20.0))))


def _dot(a, b, dims="nn"):
    ca = 0 if dims[0] == "t" else 1
    cb = 1 if dims[1] == "t" else 0
    return lax.dot_general(a.astype(BF16), b.astype(BF16), (((ca,), (cb,)), ((), ())), preferred_element_type=F32)


def _dot_sel(a, b, dims="nn", split="a", terms=3):
    r = (a if split == "a" else b).astype(F32)
    out = None
    for t in range(terms):
        piece = r.astype(BF16)
        if t + 1 < terms:
            r = r - piece.astype(F32)
        d = _dot(piece, b, dims) if split == "a" else _dot(a, piece, dims)
        out = d if out is None else out + d
    return out


def _ssd_common(dtr, dtb, alog):
    L = dtr.shape[0]
    dt = _softplus(dtr + dtb)
    a = -jnp.exp(alog)
    adt = dt * a
    r = lax.broadcasted_iota(jnp.int32, (L, L), 0)
    c = lax.broadcasted_iota(jnp.int32, (L, L), 1)
    lower = r >= c
    tri = lower.astype(F32)
    cs = _dot_sel(tri, adt, "nn", split="b")
    cs_t = _dot_sel(adt, tri, "tt")
    return dt, a, cs, cs_t, lower


def _head_expand():
    hh = lax.broadcasted_iota(jnp.int32, (SSD_H, SSD_INNER), 0)
    jj = lax.broadcasted_iota(jnp.int32, (SSD_H, SSD_INNER), 1)
    return ((jj >= hh * SSD_P) & (jj < hh * SSD_P + SSD_P)).astype(F32)


def _head_reduce():
    hh = lax.broadcasted_iota(jnp.int32, (SSD_INNER, SSD_H), 1)
    jj = lax.broadcasted_iota(jnp.int32, (SSD_INNER, SSD_H), 0)
    return ((jj >= hh * SSD_P) & (jj < hh * SSD_P + SSD_P)).astype(F32)


def ssd_fwd(xbc, dtr, dtb, alog, dsk, nseq, comm=None):
    T = xbc.shape[0]
    S = T // nseq
    C = S // SSD_L
    L = SSD_L
    NP = SSD_H // 2

    def body(x_ref, b_ref, c_ref, dtr_ref, dtb_ref, alog_ref, dsk_ref, y_ref, prev_ref, st_ref):
        ci = pl.program_id(1)

        @pl.when(ci == 0)
        def _():
            st_ref[...] = jnp.zeros_like(st_ref)

        dt, a, cs, cs_t, lower = _ssd_common(dtr_ref[:, 0:SSD_H], dtb_ref[...], alog_ref[...])
        E = _head_expand()
        X = x_ref[...].astype(F32)
        dt_e = _dot_sel(dt, E)
        cs_e = _dot_sel(cs, E)
        csl_e = cs_e[L - 1:L, :]
        Xd = X * dt_e
        Xf = Xd * jnp.exp(csl_e - cs_e)
        e_e = jnp.exp(cs_e)
        skip = _dot_sel(dsk_ref[...], E) * X
        lane = lax.broadcasted_iota(jnp.int32, (1, 2 * SSD_P), 1)
        rowp = lax.broadcasted_iota(jnp.int32, (2 * SSD_P, 1), 0)
        for g in range(SSD_G):
            Bg = b_ref[:, g * SSD_N:(g + 1) * SSD_N]
            Cg = c_ref[:, g * SSD_N:(g + 1) * SSD_N]
            cb = _dot(Cg, Bg, "nt")
            for pp in range(NP // SSD_G):
                p = g * (NP // SSD_G) + pp
                sl = slice(p * 2 * SSD_P, (p + 1) * 2 * SSD_P)
                Xd_p = Xd[:, sl]
                yd = jnp.zeros((L, 2 * SSD_P), F32)
                for q in range(2):
                    h = 2 * p + q
                    m = jnp.where(lower, jnp.exp(jnp.minimum(cs[:, h:h + 1] - cs_t[h:h + 1, :], 0.0)), 0.0)
                    mask = (lane >= q * SSD_P) & (lane < (q + 1) * SSD_P)
                    yd = yd + _dot(cb * m, jnp.where(mask, Xd_p, 0.0))
                S0 = st_ref[p]
                prev_ref[0, 0, p] = S0
                z = _dot(Cg, S0, "nt")
                y_ref[:, sl] = (skip[:, sl] + yd + z * e_e[:, sl]).astype(y_ref.dtype)
                h0 = 2 * p
                dec = jnp.where(rowp < SSD_P, jnp.exp(cs[L - 1:L, h0:h0 + 1]), jnp.exp(cs[L - 1:L, h0 + 1:h0 + 2]))
                st_ref[p] = S0 * dec + _dot(Xf[:, sl], Bg, "tn")

    row = lambda b, c: (b * C + c, 0)
    small = pl.BlockSpec((1, SSD_H), lambda b, c: (0, 0))
    return _call_with_comm(
        body, (nseq, C), "ssd_fwd",
        [pl.BlockSpec((L, SSD_INNER), row),
         pl.BlockSpec((L, SSD_G * SSD_N), lambda b, c: (b * C + c, SSD_INNER // (SSD_G * SSD_N))),
         pl.BlockSpec((L, SSD_G * SSD_N), lambda b, c: (b * C + c, SSD_INNER // (SSD_G * SSD_N) + 1)),
         pl.BlockSpec((L, 128), row), small, small, small],
        [xbc, xbc, xbc, dtr, dtb, alog, dsk],
        [pl.BlockSpec((L, SSD_INNER), row), pl.BlockSpec((1, 1, NP, 2 * SSD_P, SSD_N), lambda b, c: (b, c, 0, 0, 0))],
        [jax.ShapeDtypeStruct((T, SSD_INNER), BF16), jax.ShapeDtypeStruct((nseq, C, NP, 2 * SSD_P, SSD_N), F32)],
        comm, scratch=[pltpu.VMEM((NP, 2 * SSD_P, SSD_N), F32)], sem=("parallel", "arbitrary"))


def ssd_bwd(xbc, dtr, dtb, alog, dsk, prev, dy, nseq, comm=None):
    T = xbc.shape[0]
    S = T // nseq
    C = S // SSD_L
    L = SSD_L
    NP = SSD_H // 2

    def body(x_ref, b_ref, c_ref, dtr_ref, dtb_ref, alog_ref, dsk_ref, prev_ref, dy_ref,
             dxbc_ref, ddtr_ref, ddtb_ref, dalog_ref, ddsk_ref, ds_ref, stg_ref):
        bi = pl.program_id(0)
        ci = pl.program_id(1)

        @pl.when(ci == 0)
        def _():
            ds_ref[...] = jnp.zeros_like(ds_ref)

        @pl.when((ci == 0) & (bi == 0))
        def _():
            ddtb_ref[...] = jnp.zeros_like(ddtb_ref)
            dalog_ref[...] = jnp.zeros_like(dalog_ref)
            ddsk_ref[...] = jnp.zeros_like(ddsk_ref)

        dtr = dtr_ref[:, 0:SSD_H]
        dtb = dtb_ref[...]
        dt, a, cs, cs_t, lower = _ssd_common(dtr, dtb, alog_ref[...])
        upper = lax.broadcasted_iota(jnp.int32, (L, L), 1) >= lax.broadcasted_iota(jnp.int32, (L, L), 0)
        E = _head_expand()
        ET = _head_reduce()
        X = x_ref[...].astype(F32)
        dY = dy_ref[...].astype(F32)
        dt_e = _dot_sel(dt, E)
        cs_e = _dot_sel(cs, E)
        csl_e = cs_e[L - 1:L, :]
        f_e = jnp.exp(csl_e - cs_e)
        e_e = jnp.exp(cs_e)
        dsk_e = _dot_sel(dsk_ref[...], E)
        Xd = X * dt_e
        Xf = Xd * f_e
        lane = lax.broadcasted_iota(jnp.int32, (1, 2 * SSD_P), 1)
        rowp = lax.broadcasted_iota(jnp.int32, (2 * SSD_P, 1), 0)
        hsel = lax.broadcasted_iota(jnp.int32, (1, SSD_H), 1)
        dcs = jnp.zeros((L, SSD_H), F32)
        dcsl = jnp.zeros((1, SSD_H), F32)
        for g in range(SSD_G):
            Bg = b_ref[:, g * SSD_N:(g + 1) * SSD_N]
            Cg = c_ref[:, g * SSD_N:(g + 1) * SSD_N]
            cb = _dot(Cg, Bg, "nt")
            cbt = _dot(Bg, Cg, "nt")
            dB = jnp.zeros((L, SSD_N), F32)
            dC = jnp.zeros((L, SSD_N), F32)
            for pp in range(NP // SSD_G):
                p = g * (NP // SSD_G) + pp
                sl = slice(p * 2 * SSD_P, (p + 1) * 2 * SSD_P)
                Xd_p = Xd[:, sl]
                dY_p = dY[:, sl]
                dXd_p = jnp.zeros((L, 2 * SSD_P), F32)
                for q in range(2):
                    h = 2 * p + q
                    mask = (lane >= q * SSD_P) & (lane < (q + 1) * SSD_P)
                    col = cs[:, h:h + 1]
                    rw = cs_t[h:h + 1, :]
                    m = jnp.where(lower, jnp.exp(jnp.minimum(col - rw, 0.0)), 0.0)
                    mt = jnp.where(upper, jnp.exp(jnp.minimum(rw - col, 0.0)), 0.0)
                    dYm = jnp.where(mask, dY_p, 0.0)
                    dW = _dot(dYm, Xd_p, "nt")
                    dWt = _dot(Xd_p, dYm, "nt")
                    w = cb * m
                    wt = cbt * mt
                    dC = dC + _dot(dW * m, Bg)
                    dB = dB + _dot(dWt * mt, Cg)
                    dXd_p = dXd_p + jnp.where(mask, _dot(wt, dY_p), 0.0)
                    qcol = jnp.sum(dW * w, axis=1, keepdims=True) - jnp.sum(dWt * wt, axis=1, keepdims=True)
                    dcs = dcs + qcol * (hsel == h).astype(F32)
                S0 = prev_ref[0, 0, p]
                dSn = ds_ref[p]
                dZ = dY_p * e_e[:, sl]
                dC = dC + _dot(dZ, S0)
                h0 = 2 * p
                el0 = jnp.exp(cs[L - 1:L, h0:h0 + 1])
                el1 = jnp.exp(cs[L - 1:L, h0 + 1:h0 + 2])
                dec = jnp.where(rowp < SSD_P, el0, el1)
                ds_ref[p] = dSn * dec + _dot(dZ, Cg, "tn")
                dXf_p = _dot(Bg, dSn, "nt")
                dB = dB + _dot(Xf[:, sl], dSn)
                rs = jnp.sum(dSn * S0, axis=1, keepdims=True)
                s0 = jnp.sum(jnp.where(rowp < SSD_P, rs, 0.0), axis=0, keepdims=True) * el0
                s1 = jnp.sum(jnp.where(rowp >= SSD_P, rs, 0.0), axis=0, keepdims=True) * el1
                dcsl = dcsl + s0 * (hsel == h0).astype(F32) + s1 * (hsel == h0 + 1).astype(F32)
                y_off = _dot(Cg, S0, "nt") * e_e[:, sl]
                t1 = dY_p * y_off - dXf_p * Xf[:, sl]
                r1 = jnp.where(lane < SSD_P, t1, 0.0)
                c0 = jnp.sum(r1, axis=1, keepdims=True)
                c1 = jnp.sum(t1 - r1, axis=1, keepdims=True)
                dcs = dcs + c0 * (hsel == h0).astype(F32) + c1 * (hsel == h0 + 1).astype(F32)
                t2 = dXf_p * Xf[:, sl]
                r2 = jnp.where(lane < SSD_P, t2, 0.0)
                dcsl = dcsl + jnp.sum(r2, keepdims=True) * (hsel == h0).astype(F32) \
                    + jnp.sum(t2 - r2, keepdims=True) * (hsel == h0 + 1).astype(F32)
                stg_ref[:, sl] = dXd_p + dXf_p * f_e[:, sl]
            dxbc_ref[:, SSD_INNER + g * SSD_N:SSD_INNER + (g + 1) * SSD_N] = dB.astype(dxbc_ref.dtype)
            dxbc_ref[:, SSD_INNER + (SSD_G + g) * SSD_N:SSD_INNER + (SSD_G + g + 1) * SSD_N] = dC.astype(dxbc_ref.dtype)
        dXd = stg_ref[...]
        dxbc_ref[:, 0:SSD_INNER] = (dXd * dt_e + dsk_e * dY).astype(dxbc_ref.dtype)
        rowl = lax.broadcasted_iota(jnp.int32, (L, 1), 0)
        dcs = dcs + jnp.where(rowl == L - 1, dcsl, 0.0)
        dalpha = _dot_sel(upper.astype(F32), dcs, split="b")
        ddt = _dot_sel(dXd * X, ET, terms=2) + dalpha * a
        dalog_ref[...] += jnp.sum(dalpha * dt, axis=0, keepdims=True) * a
        ddtr = ddt * _sigmoid(dtr + dtb)
        spread = (lax.broadcasted_iota(jnp.int32, (SSD_H, 128), 0) == lax.broadcasted_iota(jnp.int32, (SSD_H, 128), 1)).astype(F32)
        ddtr_ref[...] = _dot(ddtr, spread).astype(ddtr_ref.dtype)
        ddtb_ref[...] += jnp.sum(ddtr, axis=0, keepdims=True)
        ddsk_ref[...] += jnp.sum(_dot_sel(dY * X, ET, terms=2), axis=0, keepdims=True)

    rowr = lambda b, c: (b * C + (C - 1 - c), 0)
    small = pl.BlockSpec((1, SSD_H), lambda b, c: (0, 0))
    return _call_with_comm(
        body, (nseq, C), "ssd_bwd",
        [pl.BlockSpec((L, SSD_INNER), rowr),
         pl.BlockSpec((L, SSD_G * SSD_N), lambda b, c: (b * C + (C - 1 - c), SSD_INNER // (SSD_G * SSD_N))),
         pl.BlockSpec((L, SSD_G * SSD_N), lambda b, c: (b * C + (C - 1 - c), SSD_INNER // (SSD_G * SSD_N) + 1)),
         pl.BlockSpec((L, 128), rowr), small, small, small,
         pl.BlockSpec((1, 1, NP, 2 * SSD_P, SSD_N), lambda b, c: (b, C - 1 - c, 0, 0, 0)),
         pl.BlockSpec((L, SSD_INNER), rowr)],
        [xbc, xbc, xbc, dtr, dtb, alog, dsk, prev, dy],
        [pl.BlockSpec((L, CONV_CH), rowr), pl.BlockSpec((L, 128), rowr), small, small, small],
        [jax.ShapeDtypeStruct((T, CONV_CH), BF16), jax.ShapeDtypeStruct((T, 128), BF16),
         jax.ShapeDtypeStruct((1, SSD_H), F32), jax.ShapeDtypeStruct((1, SSD_H), F32), jax.ShapeDtypeStruct((1, SSD_H), F32)],
        comm, scratch=[pltpu.VMEM((NP, 2 * SSD_P, SSD_N), F32), pltpu.VMEM((L, SSD_INNER), F32)], sem=("arbitrary", "arbitrary"))


SLOT = 128
ATT_T = 512
ATT_HP = 1
LOG2E = math.log2(math.e)
Q_SCALE = QK ** -0.5 * LOG2E


def _col_to_row(col):
    n = col.shape[0]
    eye = lax.broadcasted_iota(jnp.int32, (n, n), 0) == lax.broadcasted_iota(jnp.int32, (n, n), 1)
    return jnp.sum(jnp.where(eye, col, 0.0), axis=0, keepdims=True)


def attn_slot_fwd(q, k, v, nseq, comm=None):
    T = q.shape[0]
    S = T // nseq
    t = min(ATT_T, S)
    nb = S // t
    cols = [slice(h * SLOT, (h + 1) * SLOT) for h in range(ATT_HP)]

    def body(q_ref, k_ref, v_ref, o_ref, lse_ref):
        causal = lax.broadcasted_iota(jnp.int32, (t, t), 1) <= lax.broadcasted_iota(jnp.int32, (t, t), 0)
        for qi in range(nb):
            rows = slice(qi * t, (qi + 1) * t)
            state = [None] * ATT_HP
            for kj in range(qi + 1):
                keys = slice(kj * t, (kj + 1) * t)
                for h, c in enumerate(cols):
                    s = _dot(q_ref[rows, c], k_ref[keys, c], "nt")
                    if kj == qi:
                        s = jnp.where(causal, s, -1e30)
                    bm = jnp.max(s, axis=1, keepdims=True)
                    if kj == 0:
                        p = jnp.exp2(s - bm)
                        state[h] = (bm, jnp.sum(p, axis=1, keepdims=True), _dot(p, v_ref[keys, c]))
                    else:
                        m, l, acc = state[h]
                        m_new = jnp.maximum(m, bm)
                        corr = jnp.exp2(m - m_new)
                        p = jnp.exp2(s - m_new)
                        state[h] = (m_new, l * corr + jnp.sum(p, axis=1, keepdims=True), acc * corr + _dot(p, v_ref[keys, c]))
            for h, c in enumerate(cols):
                m, l, acc = state[h]
                o_ref[rows, c] = (acc / l).astype(o_ref.dtype)
                lse_ref[0, h, :, rows] = _col_to_row(m + jnp.log2(l))

    blk = pl.BlockSpec((S, ATT_HP * SLOT), lambda b, h: (b, h))
    return _call_with_comm(
        body, (nseq, MLA_H // ATT_HP), "attn_fwd", [blk, blk, blk], [q, k, v],
        [blk, pl.BlockSpec((1, ATT_HP, 1, S), lambda b, h: (b, h, 0, 0))],
        [jax.ShapeDtypeStruct((T, MLA_H * SLOT), BF16), jax.ShapeDtypeStruct((nseq, MLA_H, 1, S), F32)], comm)


def attn_slot_bwd(q, k, v, o, lse, do, nseq, comm=None):
    T = q.shape[0]
    S = T // nseq
    t = min(ATT_T, S)
    nb = S // t
    scale = QK ** -0.5
    cols = [slice(h * SLOT, (h + 1) * SLOT) for h in range(ATT_HP)]

    def body(q_ref, k_ref, v_ref, o_ref, lse_ref, do_ref, dq_ref, dk_ref, dv_ref, dqa_ref):
        causal_t = lax.broadcasted_iota(jnp.int32, (t, t), 0) <= lax.broadcasted_iota(jnp.int32, (t, t), 1)
        ones = jnp.ones((8, SLOT), F32)
        delta = {}
        for qi in range(nb):
            sl = slice(qi * t, (qi + 1) * t)
            for h, c in enumerate(cols):
                prod = do_ref[sl, c].astype(F32) * o_ref[sl, c].astype(F32)
                delta[h, qi] = _dot_sel(ones, prod, "nt", split="b", terms=2)[0:1, :]
        for kj in range(nb):
            ks = slice(kj * t, (kj + 1) * t)
            dk = [None] * ATT_HP
            dv = [None] * ATT_HP
            for qi in range(kj, nb):
                sl = slice(qi * t, (qi + 1) * t)
                for h, c in enumerate(cols):
                    kb, vb, qb, dob = k_ref[ks, c], v_ref[ks, c], q_ref[sl, c], do_ref[sl, c]
                    st = _dot(kb, qb, "nt")
                    pt = jnp.exp2(st - lse_ref[0, h, :, sl])
                    if qi == kj:
                        pt = jnp.where(causal_t, pt, 0.0)
                    dpt = _dot(vb, dob, "nt")
                    dst = (pt * (dpt - delta[h, qi])).astype(BF16)
                    dvc = _dot(pt, dob)
                    dkc = _dot(dst, qb) * (1.0 / LOG2E)
                    dv[h] = dvc if dv[h] is None else dv[h] + dvc
                    dk[h] = dkc if dk[h] is None else dk[h] + dkc
                    dqc = _dot(dst, kb, "tn") * scale
                    if kj > 0:
                        dqc = dqc + dqa_ref[sl, c]
                    if qi == kj:
                        dq_ref[sl, c] = dqc.astype(dq_ref.dtype)
                    else:
                        dqa_ref[sl, c] = dqc
            for h, c in enumerate(cols):
                dk_ref[ks, c] = dk[h].astype(dk_ref.dtype)
                dv_ref[ks, c] = dv[h].astype(dv_ref.dtype)

    blk = pl.BlockSpec((S, ATT_HP * SLOT), lambda b, h: (b, h))
    lse_spec = pl.BlockSpec((1, ATT_HP, 1, S), lambda b, h: (b, h, 0, 0))
    W = MLA_H * SLOT
    return _call_with_comm(
        body, (nseq, MLA_H // ATT_HP), "attn_bwd", [blk, blk, blk, blk, lse_spec, blk], [q, k, v, o, lse, do], [blk, blk, blk],
        [jax.ShapeDtypeStruct((T, W), BF16)] * 3, comm, scratch=[pltpu.VMEM((S, ATT_HP * SLOT), F32)])


def _rope_coeffs(pos, inv):
    half = ROPE // 2
    ang = pos * inv
    lane = lax.broadcasted_iota(jnp.int32, (1, SLOT), 1)
    sn = jnp.sin(ang)
    C = jnp.where(lane < NOPE, 1.0, jnp.where(lane < QK, jnp.cos(ang), 0.0))
    Sg = jnp.where((lane >= NOPE) & (lane < NOPE + half), -sn, jnp.where((lane >= NOPE + half) & (lane < QK), sn, 0.0))
    return C, Sg


def _rope_inputs(positions):
    half = ROPE // 2
    inv = ROPE_THETA ** (-jnp.arange(0, ROPE, 2, dtype=F32) / ROPE)
    row = jnp.zeros((1, SLOT), F32).at[0, NOPE:NOPE + half].set(inv).at[0, NOPE + half:QK].set(inv)
    return positions.astype(F32).reshape(-1, 1), row


def _place_k_rope(kr_lanes):
    r = lax.broadcasted_iota(jnp.int32, (SLOT, SLOT), 0)
    c = lax.broadcasted_iota(jnp.int32, (SLOT, SLOT), 1)
    return _dot_sel(kr_lanes, ((c == r + NOPE) & (r < ROPE)).astype(F32))


def rope_q_epilogue(accs, ex):
    C, Sg = _rope_coeffs(ex[0], ex[1])
    reps = accs[0].shape[1] // SLOT
    return ((accs[0] * jnp.tile(C, (1, reps)) + _rope_swap(accs[0]) * jnp.tile(Sg, (1, reps))) * Q_SCALE,)


def rope_k_epilogue(accs, ex):
    C, Sg = _rope_coeffs(ex[0], ex[1])
    kr = _place_k_rope(ex[2][:, SLOT:2 * SLOT])
    kr = kr * C + _rope_swap(kr) * Sg
    return (accs[0] + jnp.tile(kr, (1, accs[0].shape[1] // SLOT)),)


def _rope_swap(x):
    W = x.shape[1]
    half = ROPE // 2
    lane = lax.broadcasted_iota(jnp.int32, (1, W), 1) & (SLOT - 1)
    up = pltpu.roll(x, W - half, axis=1)
    dn = pltpu.roll(x, half, axis=1)
    return jnp.where((lane >= NOPE) & (lane < NOPE + half), up, jnp.where((lane >= NOPE + half) & (lane < QK), dn, 0.0))


def rope_slot_bwd(dq, dk, pos, inv, name):
    def fn(dqv, dkv, pv, iv):
        C, Sg = _rope_coeffs(pv, iv)
        ct, stl = jnp.tile(C, (1, MLA_H)), jnp.tile(Sg, (1, MLA_H))
        dqo = dqv * ct - _rope_swap(dqv) * stl
        tot = dkv[:, 0:SLOT]
        for h in range(1, MLA_H):
            tot = tot + dkv[:, h * SLOT:(h + 1) * SLOT]
        u = tot * C - _rope_swap(tot) * Sg
        r = lax.broadcasted_iota(jnp.int32, (SLOT, SLOT), 0)
        c = lax.broadcasted_iota(jnp.int32, (SLOT, SLOT), 1)
        unplace = ((r == c + NOPE) & (c < ROPE)).astype(F32)
        return dqo, dkv, _dot_sel(u, unplace, terms=2)
    W = MLA_H * SLOT
    return rowwise(fn, [dq, dk, pos], [inv], [(W, BF16), (W, BF16), (SLOT, BF16)], [], name)


XA_BLK = 512


def xattn_fwd(q, k, v, nseq):
    T = q.shape[0]
    S = T // nseq
    M = k.shape[0] // nseq
    tq = min(XA_BLK, S)
    nq = S // tq
    scale = XA_D ** -0.5

    def body(q_ref, k_ref, v_ref, o_ref):
        s = _dot(q_ref[...], k_ref[...], "nt") * scale
        p = jnp.exp(s - jnp.max(s, axis=1, keepdims=True))
        p = p / jnp.sum(p, axis=1, keepdims=True)
        o_ref[...] = _dot(p, v_ref[...]).astype(o_ref.dtype)

    qs = pl.BlockSpec((tq, XA_D), lambda b, h, i: (b * nq + i, h))
    ks = pl.BlockSpec((M, XA_D), lambda b, h, i: (b, h))
    return pl.pallas_call(
        body, grid=(nseq, XA_H, nq), name="xattn_fwd", in_specs=[qs, ks, ks], out_specs=qs,
        out_shape=jax.ShapeDtypeStruct((T, XA_H * XA_D), BF16),
        compiler_params=_cp("parallel", "parallel", "parallel"),
    )(q, k, v)


def xattn_bwd(q, k, v, do, nseq):
    T = q.shape[0]
    S = T // nseq
    M = k.shape[0] // nseq
    tq = min(XA_BLK, S)
    nq = S // tq
    scale = XA_D ** -0.5

    def body(q_ref, k_ref, v_ref, do_ref, dq_ref, dk_ref, dv_ref):
        @pl.when(pl.program_id(2) == 0)
        def _():
            dk_ref[...] = jnp.zeros_like(dk_ref)
            dv_ref[...] = jnp.zeros_like(dv_ref)

        qb, kb, vb, dob = q_ref[...], k_ref[...], v_ref[...], do_ref[...]
        s = _dot(qb, kb, "nt") * scale
        p = jnp.exp(s - jnp.max(s, axis=1, keepdims=True))
        p = p / jnp.sum(p, axis=1, keepdims=True)
        dp = _dot(dob, vb, "nt")
        ds = p * (dp - jnp.sum(dp * p, axis=1, keepdims=True)) * scale
        dq_ref[...] = _dot(ds, kb).astype(dq_ref.dtype)
        dk_ref[...] += _dot(ds, qb, "tn")
        dv_ref[...] += _dot(p, dob, "tn")

    qs = pl.BlockSpec((tq, XA_D), lambda b, h, i: (b * nq + i, h))
    ks = pl.BlockSpec((M, XA_D), lambda b, h, i: (b, h))
    return pl.pallas_call(
        body, grid=(nseq, XA_H, nq), name="xattn_bwd", in_specs=[qs, ks, ks, qs], out_specs=[qs, ks, ks],
        out_shape=[jax.ShapeDtypeStruct((T, XA_H * XA_D), BF16), jax.ShapeDtypeStruct(k.shape, F32),
                   jax.ShapeDtypeStruct(k.shape, F32)],
        compiler_params=_cp("parallel", "parallel", "arbitrary"),
    )(q, k, v, do)


CONV_BLK = 256


def _shift_down(x, s, rows):
    if s == 0:
        return x
    return jnp.where(rows >= s, pltpu.roll(x, s, axis=0), 0.0)


def _shift_up(x, s, rows):
    if s == 0:
        return x
    S = x.shape[0]
    return jnp.where(rows < S - s, pltpu.roll(x, S - s, axis=0), 0.0)


def conv_fwd(x, w, b, nseq):
    T, CH = x.shape
    S = T // nseq

    def body(x_ref, w_ref, b_ref, o_ref):
        xv = x_ref[...].astype(F32)
        rows = lax.broadcasted_iota(jnp.int32, (S, 1), 0)
        c = jnp.zeros_like(xv) + b_ref[...]
        for kk in range(CONV_K):
            c = c + w_ref[kk:kk + 1, :] * _shift_down(xv, CONV_K - 1 - kk, rows)
        o_ref[...] = (c * _sigmoid(c)).astype(o_ref.dtype)

    xs = pl.BlockSpec((S, CONV_BLK), lambda j, bb: (bb, j))
    return pl.pallas_call(
        body, grid=(CH // CONV_BLK, nseq), name="conv_fwd",
        in_specs=[xs, pl.BlockSpec((CONV_K, CONV_BLK), lambda j, bb: (0, j)), pl.BlockSpec((1, CONV_BLK), lambda j, bb: (0, j))],
        out_specs=xs, out_shape=jax.ShapeDtypeStruct((T, CH), BF16),
        compiler_params=_cp("parallel", "parallel"),
    )(x, w, b)


def conv_bwd(x, w, b, dout, nseq):
    T, CH = x.shape
    S = T // nseq

    def body(x_ref, w_ref, b_ref, do_ref, dx_ref, dw_ref, db_ref):
        @pl.when(pl.program_id(1) == 0)
        def _():
            dw_ref[...] = jnp.zeros_like(dw_ref)
            db_ref[...] = jnp.zeros_like(db_ref)

        xv = x_ref[...].astype(F32)
        rows = lax.broadcasted_iota(jnp.int32, (S, 1), 0)
        c = jnp.zeros_like(xv) + b_ref[...]
        sh = [_shift_down(xv, CONV_K - 1 - kk, rows) for kk in range(CONV_K)]
        for kk in range(CONV_K):
            c = c + w_ref[kk:kk + 1, :] * sh[kk]
        sg = _sigmoid(c)
        dc = do_ref[...].astype(F32) * sg * (1.0 + c * (1.0 - sg))
        dx = jnp.zeros_like(xv)
        for kk in range(CONV_K):
            dx = dx + w_ref[kk:kk + 1, :] * _shift_up(dc, CONV_K - 1 - kk, rows)
            dw_ref[kk:kk + 1, :] += jnp.sum(dc * sh[kk], axis=0, keepdims=True)
        dx_ref[...] = dx.astype(dx_ref.dtype)
        db_ref[...] += jnp.sum(dc, axis=0, keepdims=True)

    xs = pl.BlockSpec((S, CONV_BLK), lambda j, bb: (bb, j))
    ws = pl.BlockSpec((CONV_K, CONV_BLK), lambda j, bb: (0, j))
    bs = pl.BlockSpec((1, CONV_BLK), lambda j, bb: (0, j))
    return pl.pallas_call(
        body, grid=(CH // CONV_BLK, nseq), name="conv_bwd",
        in_specs=[xs, ws, bs, xs], out_specs=[xs, ws, bs],
        out_shape=[jax.ShapeDtypeStruct((T, CH), BF16), jax.ShapeDtypeStruct((CONV_K, CH), F32),
                   jax.ShapeDtypeStruct((1, CH), F32)],
        compiler_params=_cp("parallel", "arbitrary"),
    )(x, w, b, dout)


def _dims(a, b, mode):
    M = a.shape[1] if mode[0] == "t" else a.shape[0]
    K = a.shape[0] if mode[0] == "t" else a.shape[1]
    N = b.shape[0] if mode[1] == "t" else b.shape[1]
    return M, K, N


def _tile(dim, prefs):
    for p in prefs:
        if dim % p == 0:
            return p
    return dim


def mm(groups, out_dtypes, name, tm=None, tn=None, tk=None, epi=None, extras=(), comm=None, sub=1, n_sum=0):
    a0, b0, m0 = groups[0][0]
    M, K0, N = _dims(a0, b0, m0)
    tm = tm or _tile(M, (1024, 512, 256, 128))
    tn = tn or _tile(N, (1024, 512, 256, 128))
    flat = [p for g in groups for p in g]
    nk = 1 if tk is None else K0 // tk
    in_specs, args = [], []
    for a, b, mode in flat:
        _, K, _ = _dims(a, b, mode)
        kb = K if tk is None else tk
        in_specs.append(pl.BlockSpec((kb, tm), lambda i, j, k: (k, i)) if mode[0] == "t"
                        else pl.BlockSpec((tm, kb), lambda i, j, k: (i, k)))
        in_specs.append(pl.BlockSpec((tn, kb), lambda i, j, k: (j, k)) if mode[1] == "t"
                        else pl.BlockSpec((kb, tn), lambda i, j, k: (k, j)))
        args += [a, b]
    kinds = []
    for e in extras:
        kind, e = e if isinstance(e, tuple) else ("vec" if e.shape[0] == 1 and M != 1 else "tile", e)
        in_specs.append({"tile": pl.BlockSpec((tm, tn), lambda i, j, k: (i, j)),
                         "vec": pl.BlockSpec((1, tn), lambda i, j, k: (0, j)),
                         "rows": pl.BlockSpec((tm, e.shape[1]), lambda i, j, k: (i, 0)),
                         "whole": pl.BlockSpec(e.shape, lambda i, j, k: (0, 0))}[kind])
        kinds.append(kind)
        args.append(e)
    n_in = len(args)
    n_main = len(out_dtypes)
    n_out = n_main + n_sum
    assert n_sum == 0 or (tn == N and tk is None)
    ng = len(groups)
    sizes = [len(g) for g in groups]

    def body(*refs):
        ins, outs, accs = refs[:n_in], refs[n_in:n_in + n_out], refs[n_in + n_out:]
        kk = pl.program_id(2)

        def dots(rs):
            vals, pos = [], 0
            for gi in range(ng):
                acc = None
                for _ in range(sizes[gi]):
                    mode = flat[pos // 2][2]
                    av = ins[pos][:, rs] if mode[0] == "t" else ins[pos][rs, :]
                    d = _dot(av, ins[pos + 1][...], mode)
                    acc = d if acc is None else acc + d
                    pos += 2
                vals.append(acc)
            return vals

        def finish(accv, rs, first_chunk=True):
            ex = [(r[rs, :] if kind in ("tile", "rows") else r[...]).astype(F32) for kind, r in zip(kinds, ins[2 * len(flat):])]
            res = epi(accv, ex) if epi is not None else tuple(accv)
            for o, r in zip(outs[:n_main], res[:n_main]):
                o[rs, :] = r.astype(o.dtype)
            for o, r in zip(outs[n_main:], res[n_main:]):
                if first_chunk:
                    @pl.when(pl.program_id(0) == 0)
                    def _():
                        o[...] = r

                    @pl.when(pl.program_id(0) > 0)
                    def _():
                        o[...] += r
                else:
                    o[...] += r

        if nk == 1:
            for r in range(sub):
                rs = slice(r * (tm // sub), (r + 1) * (tm // sub))
                finish(dots(rs), rs, r == 0)
        else:
            vals = dots(slice(0, tm))
            finish = functools.partial(finish, rs=slice(0, tm))
            @pl.when(kk == 0)
            def _():
                for ar, vv in zip(accs, vals):
                    ar[...] = vv

            @pl.when(kk > 0)
            def _():
                for ar, vv in zip(accs, vals):
                    ar[...] += vv

            @pl.when(kk == nk - 1)
            def _():
                finish([ar[...] for ar in accs])

    grid = (M // tm, N // tn, nk)
    out_specs = [pl.BlockSpec((tm, tn), lambda i, j, k: (i, j)) for _ in out_dtypes] \
        + [pl.BlockSpec((1, tn), lambda i, j, k: (0, j))] * n_sum
    out_shape = [jax.ShapeDtypeStruct((M, N), dt) for dt in out_dtypes] + [jax.ShapeDtypeStruct((1, N), F32)] * n_sum
    scratch = [pltpu.VMEM((tm, tn), F32) for _ in range(ng if nk > 1 else 0)]
    sem = ("arbitrary" if n_sum else "parallel", "parallel", "arbitrary")
    if comm is not None:
        body = _attach(comm, body, n_in, n_out, *_grid_ends(grid))
        in_specs, args = in_specs + [HBM_SPEC] * len(comm.inputs), args + comm.inputs
        out_specs, out_shape = out_specs + [HBM_SPEC] * len(comm.out_shapes), out_shape + comm.out_shapes
        scratch, sem = scratch + comm.sems, ("arbitrary",) * 3
    return pl.pallas_call(body, grid=grid, name=name, in_specs=in_specs, out_specs=out_specs, out_shape=out_shape,
                          scratch_shapes=scratch, compiler_params=_cp(*sem))(*args)


def mm1(a, b, mode, out_dtype, name, **kw):
    return mm([[(a, b, mode)]], [out_dtype], name, **kw)[0]


ROW_BLK = 512


def rowwise(fn, rows, consts, outs, accs, name, tb=ROW_BLK):
    rows = [r if isinstance(r, tuple) else (r, r.shape[1], 0) for r in rows]
    T = rows[0][0].shape[0]
    tb = min(tb, T)
    n_r, n_c, n_o, n_a = len(rows), len(consts), len(outs), len(accs)

    def body(*refs):
        vals = [r[...].astype(F32) for r in refs[:n_r + n_c]]
        res = fn(*vals)
        o_refs = refs[n_r + n_c:n_r + n_c + n_o]
        a_refs = refs[n_r + n_c + n_o:]
        for o, r in zip(o_refs, res[:n_o]):
            o[...] = r.astype(o.dtype)
        if n_a:
            @pl.when(pl.program_id(0) == 0)
            def _():
                for ar in a_refs:
                    ar[...] = jnp.zeros_like(ar)
            for ar, r in zip(a_refs, res[n_o:]):
                ar[...] += r

    return pl.pallas_call(
        body, grid=(T // tb,), name=name,
        in_specs=[pl.BlockSpec((tb, w), functools.partial(lambda i, j: (i, j), j=j)) for _, w, j in rows]
        + [pl.BlockSpec(c.shape, lambda i: (0, 0)) for c in consts],
        out_specs=[pl.BlockSpec((tb, d), lambda i: (i, 0)) for d, _ in outs]
        + [pl.BlockSpec(s, lambda i: (0, 0)) for s in accs],
        out_shape=[jax.ShapeDtypeStruct((T, d), dt) for d, dt in outs]
        + [jax.ShapeDtypeStruct(s, F32) for s in accs],
        compiler_params=_cp("arbitrary" if n_a else "parallel"),
    )(*[r[0] for r in rows], *consts)


def _rms_stats(x):
    r = lax.rsqrt(jnp.mean(x * x, axis=-1, keepdims=True) + EPS)
    return r, x * r


def _rms_bwd(x, g, dy):
    r, xn = _rms_stats(x)
    dyg = dy * g
    dx = r * (dyg - xn * jnp.mean(dyg * xn, axis=-1, keepdims=True))
    return dx, jnp.sum(dy * xn, axis=0, keepdims=True)


def rms_fwd(x, g, name):
    return rowwise(lambda xv, gv: (_rms_stats(xv)[1] * gv,), [x], [g], [(x.shape[1], BF16)], [], name)[0]


def rms_bwd(x, g, dy, name, resid=None, dx_dtype=F32):
    def fn(*v):
        if resid is None:
            xv, dyv, gv = v
            dx, dg = _rms_bwd(xv, gv, dyv)
        else:
            xv, dyv, rv, gv = v
            dx, dg = _rms_bwd(xv, gv, dyv)
            dx = dx + rv
        return dx, dg
    rows = [x, dy] + ([] if resid is None else [resid])
    return rowwise(fn, rows, [g], [(x.shape[1], dx_dtype)], [(1, x.shape[1])], name)


def mm_rms_bwd(pairs, x, g, name, resid=None, dx_dtype=F32, comm=None):
    def epi(accs, ex):
        dx, dg = _rms_bwd(ex[0], ex[-1], accs[0])
        return (dx if resid is None else dx + ex[1]), dg
    extras = [x] + ([] if resid is None else [resid]) + [g]
    return mm([pairs], [dx_dtype], name, tm=min(256, x.shape[0]), tn=x.shape[1], epi=epi, extras=extras, comm=comm, n_sum=1)


def mm_resid(a, b, x, g, wgt, name, comm=None, target=None):
    def epi(accs, ex):
        y = ex[0] + wgt * _rms_stats(accs[0])[1] * ex[1]
        if target is None:
            return accs[0], y
        d = y - ex[2]
        return accs[0], d / D, jnp.sum(d * d, axis=0, keepdims=True)
    return mm([[(a, b, "nn")]], [F32, F32], name, tm=min(512, a.shape[0]), tn=b.shape[1], epi=epi,
              extras=[x, g] + ([] if target is None else [target]), sub=2, comm=comm, n_sum=0 if target is None else 1)


def resid_bwd(h, g, dy, wgt, name):
    def fn(hv, dyv, gv):
        dx, dg = _rms_bwd(hv, gv, dyv)
        return wgt * dx, wgt * dg
    return rowwise(fn, [h, dy], [g], [(h.shape[1], BF16)], [(1, h.shape[1])], name)


def _silu_parts(g):
    s = _sigmoid(g)
    return g * s, s * (1.0 + g * (1.0 - s))


def gated_norm_fwd(y, z, g, name):
    W = SSD_INNER // SSD_G

    def fn(yv, zv, gv):
        yg = yv * _silu_parts(zv)[0]
        return (jnp.concatenate([_rms_stats(yg[:, i * W:(i + 1) * W])[1] for i in range(SSD_G)], axis=1) * gv,)
    return rowwise(fn, [y, z], [g], [(SSD_INNER, BF16)], [], name)[0]


def gated_norm_bwd(y, z, dyn, g, name):
    W = SSD_INNER // SSD_G

    def fn(yv, zv, dv, gv):
        sil, dsil = _silu_parts(zv)
        yg = yv * sil
        parts = [_rms_bwd(yg[:, i * W:(i + 1) * W], gv[:, i * W:(i + 1) * W], dv[:, i * W:(i + 1) * W]) for i in range(SSD_G)]
        dyg = jnp.concatenate([p[0] for p in parts], axis=1)
        dg = jnp.concatenate([p[1] for p in parts], axis=1)
        return dyg * sil, dyg * yv * dsil, dg
    return rowwise(fn, [y, z, dyn], [g], [(SSD_INNER, BF16), (SSD_INNER, BF16)], [(1, SSD_INNER)], name)


def merge_fwd(gl, ys, ym, gb, name):
    def fn(glv, ysv, ymv, gbv):
        gt = _sigmoid(glv + gbv)
        return (gt[:, :D] * ysv + gt[:, D:] * ymv,)
    return rowwise(fn, [gl, ys, ym], [gb], [(D, BF16)], [], name)[0]


def merge_bwd(gl, ys, ym, dm, gb, name):
    def fn(glv, ysv, ymv, dmv, gbv):
        gt = _sigmoid(glv + gbv)
        gs, gm = gt[:, :D], gt[:, D:]
        dgl = jnp.concatenate([dmv * ysv * gs * (1.0 - gs), dmv * ymv * gm * (1.0 - gm)], axis=1)
        return dmv * gs, dmv * gm, dgl, jnp.sum(dgl, axis=0, keepdims=True)
    return rowwise(fn, [gl, ys, ym, dm], [gb], [(D, BF16), (D, BF16), (2 * D, BF16)], [(1, 2 * D)], name)


def loss_head(y, tgt, name):
    def fn(yv, tv):
        d = yv - tv
        part = 0.5 * jnp.sum(jnp.sum(d * d, axis=1, keepdims=True), axis=0, keepdims=True) / D
        return d / D, jnp.broadcast_to(part, (1, 128))
    return rowwise(fn, [y, tgt], [], [(D, F32)], [(1, 128)], name)


def _adamw_math(wv, gv, mv, vv):
    mn = B1 * mv + (1.0 - B1) * gv
    vn = B2 * vv + (1.0 - B2) * (gv * gv)
    mh = mn / (1.0 - B1 ** STEP)
    vh = vn / (1.0 - B2 ** STEP)
    return -LR * (mh / (jnp.sqrt(vh) + AEPS) + WD * wv), mn, vn


def adamw(w, g, m, v, name):
    R, C = w.shape
    tb = _tile(R, (256, 128, 64, 32, 16, 8))
    return rowwise(_adamw_math, [w, g, m, v], [], [(C, F32)] * 3, [], name, tb=tb)


def adamw_small(packed, ws, ms, vs):
    k = len(ws)
    sizes = [x.shape[1] for x in ws]

    def body(*refs):
        p_ref, w_refs, m_refs, v_refs = refs[0], refs[1:1 + k], refs[1 + k:1 + 2 * k], refs[1 + 2 * k:1 + 3 * k]
        outs = refs[1 + 3 * k:]
        r0 = 0
        for i, n in enumerate(sizes):
            nr = -(-n // 128)
            g = jnp.concatenate([p_ref[r0 + r:r0 + r + 1, :] for r in range(nr)], axis=1)[:, :n]
            r0 += nr
            outs[i][...] = g
            outs[k + i][...], outs[2 * k + i][...], outs[3 * k + i][...] = _adamw_math(w_refs[i][...], g, m_refs[i][...], v_refs[i][...])

    res = pl.pallas_call(body, name="adamw_small",
                         out_shape=[jax.ShapeDtypeStruct((1, n), F32) for _ in range(4) for n in sizes])(packed, *ws, *ms, *vs)
    return [res[j * k:(j + 1) * k] for j in range(4)]


def adamw_from_slots(recv, piece, w, m, v, name):
    K, n = w.shape
    ns = recv.shape[0]
    assert recv.shape[2] == n and recv.shape[1] % K == 0
    tb = _tile(K, (256, 176, 128, 64, 32, 16, 8)) if K % 8 == 0 else K
    r_spec = pl.BlockSpec((ns, tb, n), lambda i: (0, piece * (K // tb) + i, 0))
    w_spec = pl.BlockSpec((tb, n), lambda i: (i, 0))

    def body(r_ref, w_ref, m_ref, v_ref, g_ref, d_ref, mo_ref, vo_ref):
        g = r_ref[0].astype(F32)
        for s in range(1, ns):
            g = g + r_ref[s].astype(F32)
        g_ref[...] = g
        d_ref[...], mo_ref[...], vo_ref[...] = _adamw_math(w_ref[...], g, m_ref[...], v_ref[...])

    return pl.pallas_call(
        body, grid=(K // tb,), name=name, in_specs=[r_spec, w_spec, w_spec, w_spec], out_specs=[w_spec] * 4,
        out_shape=[jax.ShapeDtypeStruct((K, n), F32)] * 4, compiler_params=_cp("parallel"),
    )(recv, w, m, v)


def _me():
    return lax.axis_index("x"), lax.axis_index("y"), lax.axis_index("c")


def _dev_index():
    x, y, c = _me()
    return 4 * x + 2 * y + c


HBM_SPEC = pl.BlockSpec(memory_space=pl.ANY)


class GatherComm:
    def __init__(self, shards):
        self.inputs = [s for s, _ in shards]
        self.rows = [list(r) for _, r in shards]
        n = len(shards)
        self.out_shapes = [jax.ShapeDtypeStruct((N_DEV, r, s.shape[1]), s.dtype) for s, rows in shards for r in rows]
        self.sems = [pltpu.SemaphoreType.DMA((7 * n,)), pltpu.SemaphoreType.DMA((7 * n,)), pltpu.SemaphoreType.DMA((n,))]

    def _plan(self, x_refs, out_refs, sems):
        send_sems, recv_sems, local_sems = sems
        x, y, c = _me()
        me, sibling = (x, y, c), (x, y, 1 - c)
        chips = [(1 - x, y), (x, 1 - y), (1 - x, 1 - y)]
        index = lambda px, py, pc: 4 * px + 2 * py + pc
        mine, first, passed, whole = [], [], [], []
        pos = 0
        for i, rows in enumerate(self.rows):
            kw = lambda k: dict(send_sem=send_sems.at[7 * i + k], recv_sem=recv_sems.at[7 * i + k], device_id_type=MESH)
            r0 = 0
            fwd = [[] for _ in chips]
            for j, nr in enumerate(rows):
                out, src = out_refs[pos + j], x_refs[i].at[pl.ds(r0, nr)]
                mine.append(pltpu.make_async_copy(src, out.at[index(*me)], local_sems.at[i]))
                first.append(pltpu.make_async_remote_copy(src_ref=src, dst_ref=out.at[index(*me)], device_id=sibling, **kw(0)))
                for jj, chip in enumerate(chips):
                    first.append(pltpu.make_async_remote_copy(src_ref=src, dst_ref=out.at[index(*me)], device_id=(*chip, c),
                                                              **kw(1 + jj)))
                    blk = out.at[index(*chip, c)]
                    fwd[jj].append(pltpu.make_async_remote_copy(src_ref=blk, dst_ref=blk, device_id=sibling, **kw(4 + jj)))
                r0 += nr
            passed.append(fwd)
            whole.append([pltpu.make_async_remote_copy(src_ref=x_refs[i], dst_ref=x_refs[i], device_id=sibling, **kw(k))
                          for k in range(7)])
            pos += len(rows)
        return mine, first, passed, whole

    def start(self, x_refs, out_refs, sems):
        mine, first, _, _ = self._plan(x_refs, out_refs, sems)
        for cp in mine + first:
            cp.start()

    def finish(self, x_refs, out_refs, sems):
        _, _, passed, whole = self._plan(x_refs, out_refs, sems)
        local_sems = sems[2]
        for i, fwd in enumerate(passed):
            for jj in range(3):
                whole[i][1 + jj].wait_recv()
                for cp in fwd[jj]:
                    cp.start()
        for i in range(len(passed)):
            whole[i][0].wait_recv()
            for jj in range(3):
                whole[i][4 + jj].wait_recv()
        for i in range(len(passed)):
            for k in range(7):
                whole[i][k].wait_send()
            pltpu.make_async_copy(x_refs[i], x_refs[i], local_sems.at[i]).wait()


def run_comm(comm, name):
    n_in, n_out = len(comm.inputs), len(comm.out_shapes)

    def body(*refs):
        ins, outs, sems = refs[:n_in], refs[n_in:n_in + n_out], refs[n_in + n_out:]
        comm.start(ins, outs, sems)
        comm.finish(ins, outs, sems)

    return pl.pallas_call(body, name=name, out_shape=comm.out_shapes, in_specs=[HBM_SPEC] * n_in,
                          out_specs=[HBM_SPEC] * n_out, scratch_shapes=comm.sems)(*comm.inputs)


def _attach(comm, body, n_in, n_out, first, last):
    if comm is None:
        return body
    ci, co, cs = len(comm.inputs), len(comm.out_shapes), len(comm.sems)

    def wrapped(*refs):
        h_in, c_in = refs[:n_in], refs[n_in:n_in + ci]
        h_out, c_out = refs[n_in + ci:n_in + ci + n_out], refs[n_in + ci + n_out:n_in + ci + n_out + co]
        rest = refs[n_in + ci + n_out + co:]
        h_scr, c_sem = rest[:len(rest) - cs], rest[len(rest) - cs:]

        @pl.when(first())
        def _():
            comm.start(c_in, c_out, c_sem)

        body(*h_in, *h_out, *h_scr)

        @pl.when(last())
        def _():
            comm.finish(c_in, c_out, c_sem)

    return wrapped


def _grid_ends(grid):
    first = lambda: functools.reduce(lambda a, b: a & b, [pl.program_id(i) == 0 for i in range(len(grid))])
    last = lambda: functools.reduce(lambda a, b: a & b, [pl.program_id(i) == g - 1 for i, g in enumerate(grid)])
    return first, last


def _call_with_comm(body, grid, name, in_specs, args, out_specs, out_shape, comm, scratch=(), sem=None):
    sem = sem or ("parallel",) * len(grid)
    scratch = list(scratch)
    if comm is not None:
        body = _attach(comm, body, len(args), len(out_shape), *_grid_ends(grid))
        in_specs, args = in_specs + [HBM_SPEC] * len(comm.inputs), args + comm.inputs
        out_specs, out_shape = out_specs + [HBM_SPEC] * len(comm.out_shapes), out_shape + comm.out_shapes
        scratch, sem = scratch + comm.sems, ("arbitrary",) * len(grid)
    return pl.pallas_call(body, grid=grid, name=name, in_specs=in_specs, out_specs=out_specs, out_shape=out_shape,
                          scratch_shapes=scratch, compiler_params=_cp(*sem))(*args)


class ScatterComm:
    def __init__(self, groups):
        self.sizes = [len(g) for g in groups]
        self.rows = [[pc.shape[1] for pc in g] for g in groups]
        ng = len(groups)
        self.inputs = [pc for g in groups for pc in g]
        self.out_shapes = [jax.ShapeDtypeStruct((N_DEV, sum(self.rows[gi]), g[0].shape[2]), g[0].dtype) for gi, g in enumerate(groups)]
        self.sems = [pltpu.SemaphoreType.DMA((7 * ng,)), pltpu.SemaphoreType.DMA((7 * ng,)), pltpu.SemaphoreType.DMA((ng,))]

    def _peers(self):
        x, y, c = _me()
        out = []
        for k in range(1, N_DEV):
            px = 1 - x if k & 4 else x
            py = 1 - y if k & 2 else y
            pc = 1 - c if k & 1 else c
            out.append((k, 4 * px + 2 * py + pc, dict(device_id=(px, py, pc), device_id_type=MESH)))
        return 4 * x + 2 * y + c, out

    def start(self, ins, outs, sems):
        send_sems, recv_sems, local_sems = sems
        me, peers = self._peers()
        pos = 0
        for gi, size in enumerate(self.sizes):
            for i, pc in enumerate(ins[pos:pos + size]):
                dst = outs[gi].at[me, pl.ds(sum(self.rows[gi][:i]), self.rows[gi][i])]
                pltpu.make_async_copy(pc.at[me], dst, local_sems.at[gi]).start()
                for k, peer, kw in peers:
                    pltpu.make_async_remote_copy(src_ref=pc.at[peer], dst_ref=dst, send_sem=send_sems.at[7 * gi + k - 1],
                                                 recv_sem=recv_sems.at[7 * gi + k - 1], **kw).start()
            pos += size

    def finish(self, ins, outs, sems):
        send_sems, recv_sems, local_sems = sems
        me, peers = self._peers()
        whole = [pltpu.make_async_remote_copy(src_ref=outs[gi].at[peer], dst_ref=outs[gi].at[peer],
                                              send_sem=send_sems.at[7 * gi + k - 1], recv_sem=recv_sems.at[7 * gi + k - 1], **kw)
                 for gi in range(len(self.sizes)) for k, peer, kw in peers]
        for cp in whole:
            cp.wait_recv()
        for cp in whole:
            cp.wait_send()
        for gi in range(len(self.sizes)):
            pltpu.make_async_copy(outs[gi].at[me], outs[gi].at[me], local_sems.at[gi]).wait()


def sum_slots(recv, name, tr):
    n, R, C = recv.shape

    def body(r_ref, o_ref):
        acc = r_ref[0].astype(F32)
        for s in range(1, n):
            acc = acc + r_ref[s].astype(F32)
        o_ref[...] = acc

    return pl.pallas_call(
        body, grid=(R // tr,), name=name,
        in_specs=[pl.BlockSpec((n, tr, C), lambda i: (0, i, 0))], out_specs=pl.BlockSpec((tr, C), lambda i: (i, 0)),
        out_shape=jax.ShapeDtypeStruct((R, C), F32), compiler_params=_cp("parallel"),
    )(recv)


PACK_W, FLAT_W = 1024, 128
MAIN = [
    ("ffn1_w_gate", "col"), ("ffn1_w_up", "col"), ("ffn1_w_down", "row"),
    ("ffn2_w_gate", "col"), ("ffn2_w_up", "col"), ("ffn2_w_down", "row"),
    ("w_ssd_proj", "row"), ("w_mla_proj", "row"), ("w_out", "row"),
    ("w_xq", "row"), ("w_xk", "row"), ("w_xv", "row"), ("w_xo", "row"),
    ("w_uk", "col"), ("w_uv", "col"),
]
FLAT = [("w_in", "col"), ("w_uq", "col")]
BIG = MAIN + FLAT
SMALL = ["ffn1_pre_g", "ffn1_post_g", "mix_pre_g", "conv_b", "dt_bias", "a_log", "d_skip", "ssd_norm_g", "q_norm_g",
         "kv_norm_g", "gate_bias", "mix_post_g", "xa_pre_g", "mem_norm_g", "xa_post_g", "ffn2_pre_g", "ffn2_post_g"]
WEIGHTS = ['ffn1_pre_g', 'ffn1_w_gate', 'ffn1_w_up', 'ffn1_w_down', 'ffn1_post_g', 'mix_pre_g', 'w_in', 'conv_w', 'conv_b',
           'dt_bias', 'a_log', 'd_skip', 'ssd_norm_g', 'w_ssd_proj', 'q_norm_g', 'w_uq', 'kv_norm_g', 'w_uk', 'w_uv',
           'w_mla_proj', 'gate_bias', 'w_out', 'mix_post_g', 'xa_pre_g', 'mem_norm_g', 'w_xq', 'w_xk', 'w_xv', 'w_xo',
           'xa_post_g', 'ffn2_pre_g', 'ffn2_w_gate', 'ffn2_w_up', 'ffn2_w_down', 'ffn2_post_g']


def _pack_rows(w, kind, width):
    m = w[0].T if kind == "col" else w[0]
    return m.reshape(-1, width)


KIND = dict(BIG)
GATHER_PLAN = {
    "first": (["ffn1_w_gate", "ffn1_w_up"], []),
    "ffn1_gate_up": (["ffn1_w_down"], ["w_in@0"]),
    "ffn1_down": ([], ["w_in@1"]),
    "ssd_fwd": (["w_ssd_proj", "w_mla_proj", "w_out", "w_uk", "w_uv"], ["w_uq"]),
    "attn_fwd": (["w_xq", "w_xk", "w_xv", "w_xo", "ffn2_w_gate", "ffn2_w_up", "ffn2_w_down"], []),
}
CONV_RIDES_WITH = "w_in@1"
SCATTER_PLAN = {
    "attn_bwd": [["ffn2_w_gate", "ffn2_w_up", "ffn2_w_down"], ["w_xq", "w_xk", "w_xv", "w_xo"]],
    "ssd_bwd": [["w_ssd_proj", "w_mla_proj", "w_out"], ["w_uk", "w_uv"], ["w_uq"]],
    "in_bwd": [["w_in#0"]],
    "ffn1:down_bwd": [["w_in#1"]],
    "ffn1:dwd": [["w_in#2"]],
    "ffn1:dwg": [["ffn1_w_down#0"]],
    "ffn1:dwu": [["ffn1_w_down#1"]],
    "ffn1:gate_up_bwd": [["ffn1_w_gate"]],
    "last": [["ffn1_w_up"]],
}
PARTS = {"w_in@0": ("w_in", 0, 2656), "w_in@1": ("w_in", 2656, 5296),
         "w_in#0": ("w_in", 0, 2656), "w_in#1": ("w_in", 2656, 3984), "w_in#2": ("w_in", 3984, 5296),
         "ffn1_w_down#0": ("ffn1_w_down", 0, 176), "ffn1_w_down#1": ("ffn1_w_down", 176, 352)}


def _parts_of(base, mark):
    return sorted(pn for pn, (b, _, _) in PARTS.items() if b == base and mark in pn)


class Stage:
    def __init__(self, w):
        self.w = w
        self.width = {n: PACK_W if (n, k) in MAIN else FLAT_W for n, k in BIG}
        self.nrows = {n: math.prod(w[n].shape) // self.width[n] for n, _ in BIG}
        self.recv = {}
        self.arrived_parts = {}

    def _rows(self, n):
        return PARTS[n][2] - PARTS[n][1] if n in PARTS else self.nrows[n]

    def _shards(self, tag):
        names_main, names_flat = GATHER_PLAN[tag]

        def pack(n):
            base, r0, r1 = PARTS.get(n, (n, 0, None))
            return _pack_rows(self.w[base], KIND[base], self.width[base])[r0:r1].astype(BF16)
        shards = []
        if names_main:
            pieces = [pack(n) for n in names_main]
            shards.append((jnp.concatenate(pieces, axis=0), [pc.shape[0] for pc in pieces]))
        if names_flat:
            pieces = [pack(n) for n in names_flat]
            if CONV_RIDES_WITH in names_flat:
                pieces.append(_pad_rows(lax.bitcast_convert_type(self.w["conv_w"][0], BF16).reshape(-1, FLAT_W), 16))
            shards.append((jnp.concatenate(pieces, axis=0), [pc.shape[0] for pc in pieces]))
        return shards

    def gather(self, tag):
        return GatherComm(self._shards(tag)) if tag in GATHER_PLAN else None

    def gathered(self, tag, outs, W, p):
        if tag not in GATHER_PLAN:
            return
        names_main, names_flat = GATHER_PLAN[tag]
        outs = list(outs)
        for n in names_main + names_flat:
            rows = outs.pop(0)
            if n in PARTS:
                self.arrived_parts[n] = rows
                base = PARTS[n][0]
                mine = _parts_of(base, "@")
                if not all(pn in self.arrived_parts for pn in mine):
                    continue
                n, rows = base, jnp.concatenate([self.arrived_parts[pn] for pn in mine], axis=1)
            K = self.w[n].shape[1] if KIND[n] == "col" else PACK_W
            W[n] = rows.reshape(-1, K)
        if CONV_RIDES_WITH in names_flat:
            cw = self.w["conv_w"]
            nbits = 2 * math.prod(cw.shape) // FLAT_W
            bits = outs.pop(0)[:, :nbits].reshape((N_DEV,) + cw.shape[1:] + (2,))
            p["conv_w"] = lax.bitcast_convert_type(bits, F32).transpose(1, 0, 2).reshape(cw.shape[1], -1)

    def pieces(self, tag, gw):
        def piece(n):
            if n in PARTS:
                base, r0, r1 = PARTS[n]
                return gw[base].reshape(N_DEV, self.nrows[base], self.width[base])[:, r0:r1]
            return gw[n].reshape(N_DEV, self.nrows[n], self.width[n])
        return [[piece(n) for n in names] for names in SCATTER_PLAN[tag]]

    def scatter(self, tag, gw):
        return ScatterComm(self.pieces(tag, gw)) if tag in SCATTER_PLAN else None

    def scattered(self, tag, outs):
        if tag in SCATTER_PLAN:
            self.recv[tag] = outs


def _pad_rows(a, mult):
    r = (-a.shape[0]) % mult
    return a if r == 0 else jnp.concatenate([a, jnp.zeros((r,) + a.shape[1:], a.dtype)], axis=0)


def _pack_small(vals, loss_row=None, conv_w=None):
    rows = []
    for v in vals:
        f = v.reshape(-1)
        f = jnp.concatenate([f, jnp.zeros(((-f.shape[0]) % 128,), F32)])
        rows.append(f.reshape(-1, 128))
    if conv_w is not None:
        rows.append(conv_w.reshape(-1, 128))
    if loss_row is not None:
        rows.append(loss_row)
    return _pad_rows(jnp.concatenate(rows, axis=0), 8)


def _unpack_small(buf, shapes):
    out, r = [], 0
    for shp in shapes:
        n = math.prod(shp)
        nr = -(-n // 128)
        out.append(buf[r:r + nr].reshape(-1)[:n].reshape(shp))
        r += nr
    return out, r


def _tn(a, b, name, out_dtype=BF16, comm=None):
    M, N = a.shape[1], b.shape[1]
    T = a.shape[0]
    tm = M if M <= 1536 else M // 2
    tk = 1024 if T % 1024 == 0 and T > 1024 else None
    res = mm([[(a, b, "tn")]], [out_dtype], name, tm=tm, tn=N, tk=tk, comm=comm)
    return res[0] if comm is None else (res[0], res[1:])


class NoStage:
    def gather(self, tag):
        return None

    def gathered(self, tag, outs, W, p):
        pass

    def scatter(self, tag, gw):
        return None

    def scattered(self, tag, outs):
        pass


def _ffn_fwd(x, gpre, gpost, W, p, tag, stage, target=None):
    h = rms_fwd(x, gpre, tag + "_pre")

    def swi(accs, ex):
        sil, dsil = _silu_parts(accs[0])
        return sil, accs[1] * dsil, sil * accs[1]
    G, U, A, *arrived = mm([[(h, W[tag + "_w_gate"], "nt")], [(h, W[tag + "_w_up"], "nt")]], [BF16, BF16, BF16], tag + "_gate_up",
                           tn=DFF // 2, epi=swi, comm=stage.gather(tag + "_gate_up"), sub=4 if h.shape[0] % 1024 == 0 else 1)
    stage.gathered(tag + "_gate_up", arrived, W, p)
    H, y, *rest = mm_resid(A, W[tag + "_w_down"], x, gpost, FFN_RES, tag + "_down", comm=stage.gather(tag + "_down"), target=target)
    saved = (x, h, G, U, A, H)
    if target is not None:
        return y, saved, rest[0]
    stage.gathered(tag + "_down", rest, W, p)
    return y, saved


def _ffn_bwd(dy, saved, gpre, gpost, wg_t, wu_t, wd, tag, stage, gw):
    x, h, G, U, A, H = saved
    dH, dgpost = resid_bwd(H, gpost, dy, FFN_RES, tag + "_post_bwd")

    def dswi(accs, ex):
        return accs[0] * ex[1], accs[0] * ex[0]

    def hosted(where, call):
        comm = stage.scatter(tag + ":" + where, gw)
        res = call(comm)
        if comm is None:
            return res
        stage.scattered(tag + ":" + where, res[1])
        return res[0]

    res = hosted("down_bwd", lambda comm: (lambda r: r if comm is None else (r[:2], r[2:]))(
        mm([[(dH, wd, "nt")]], [BF16, BF16], tag + "_down_bwd", tn=DFF // 2, epi=dswi, extras=[G, U], comm=comm,
           sub=4 if dH.shape[0] % 1024 == 0 else 1)))
    dG, dU = res
    gw[tag + "_w_down"] = hosted("dwd", lambda comm: _tn(A, dH, tag + "_dwd", comm=comm))
    gw[tag + "_w_gate"] = hosted("dwg", lambda comm: _tn(dG, h, tag + "_dwg", comm=comm))
    gw[tag + "_w_up"] = hosted("dwu", lambda comm: _tn(dU, h, tag + "_dwu", comm=comm))
    dx, dgpre = hosted("gate_up_bwd", lambda comm: (lambda r: r[:2] if comm is None else (r[:2], r[2:]))(
        mm_rms_bwd([(dG, wg_t, "nn"), (dU, wu_t, "nn")], x, gpre, tag + "_gate_up_bwd", resid=dy, comm=comm)))
    return dx, dgpre, dgpost


def _local_step(x, mem, positions, tgt, W, p, stage=None):
    stage = stage or NoStage()
    nseq = x.shape[0]
    T = nseq * x.shape[1]
    x0 = x.reshape(T, D)
    mem2 = mem.reshape(-1, D)
    pos, inv = _rope_inputs(positions)

    x1, ffn1 = _ffn_fwd(x0, p["ffn1_pre_g"], p["ffn1_post_g"], W, p, "ffn1", stage)

    w_in_t = W["w_in"]
    bounds = [0]
    for n in (SSD_INNER, CONV_CH, SSD_H, QR, KVR, ROPE, 2 * D):
        bounds.append(bounds[-1] + n)
    wt_z, wt_xbc, wt_dt, wt_q, wt_kv, wt_kr, wt_gate = [w_in_t[bounds[i]:bounds[i + 1]] for i in range(7)]
    wt_dt, wt_kr = _pad_rows(wt_dt, SLOT), _pad_rows(wt_kr, SLOT)
    wt_dtkr = jnp.concatenate([wt_dt, wt_kr], axis=0)
    hm = rms_fwd(x1, p["mix_pre_g"], "mix_pre")
    z = mm1(hm, wt_z, "nt", BF16, "in_z")
    xbc = mm1(hm, wt_xbc, "nt", BF16, "in_xbc")
    q_c = mm1(hm, wt_q, "nt", F32, "in_q", tn=QR)
    kv_c = mm1(hm, wt_kv, "nt", F32, "in_kv")
    dtkr = mm1(hm, wt_dtkr, "nt", F32, "in_dtkr")
    gl = mm1(hm, wt_gate, "nt", BF16, "in_gate")

    xbc_act = conv_fwd(xbc, p["conv_w"], p["conv_b"], nseq)
    y_ssd_core, prev, *arrived = ssd_fwd(xbc_act, dtkr, p["dt_bias"], p["a_log"], p["d_skip"], nseq, comm=stage.gather("ssd_fwd"))
    stage.gathered("ssd_fwd", arrived, W, p)
    yn = gated_norm_fwd(y_ssd_core, z, p["ssd_norm_g"], "ssd_norm")
    y_ssd = mm1(yn, W["w_ssd_proj"], "nn", BF16, "ssd_proj")

    slot_rows = lambda wt, per: jnp.pad(wt.reshape(MLA_H, per, -1), ((0, 0), (0, SLOT - per), (0, 0))).reshape(MLA_H * SLOT, -1)
    wq_s, wk_s, wv_s = slot_rows(W["w_uq"], QK), slot_rows(W["w_uk"], NOPE), slot_rows(W["w_uv"], VD)
    wo_s = slot_rows(W["w_mla_proj"], VD)
    qn = rms_fwd(q_c, p["q_norm_g"], "q_norm")
    rope_args = [("rows", pos), ("whole", inv)]
    Qc, = mm([[(qn, wq_s, "nt")]], [BF16], "uq", epi=rope_q_epilogue, extras=rope_args, sub=4 if T % 1024 == 0 else 1)
    kvn = rms_fwd(kv_c, p["kv_norm_g"], "kv_norm")
    Kc, = mm([[(kvn, wk_s, "nt")]], [BF16], "uk", epi=rope_k_epilogue, extras=rope_args + [("rows", dtkr)],
             sub=4 if T % 1024 == 0 else 1)
    v_s = mm1(kvn, wv_s, "nt", BF16, "uv")
    o_s, lse, *arrived = attn_slot_fwd(Qc, Kc, v_s, nseq, comm=stage.gather("attn_fwd"))
    stage.gathered("attn_fwd", arrived, W, p)
    y_mla = mm1(o_s, wo_s, "nn", BF16, "mla_proj")

    merged = merge_fwd(gl, y_ssd, y_mla, p["gate_bias"], "merge")
    hmix, x2 = mm_resid(merged, W["w_out"], x1, p["mix_post_g"], 1.0, "mix_out")

    hq = rms_fwd(x2, p["xa_pre_g"], "xa_pre")
    mn = rms_fwd(mem2, p["mem_norm_g"], "mem_norm")
    xq = mm1(hq, W["w_xq"], "nn", BF16, "xq")
    xk = mm1(mn, W["w_xk"], "nn", BF16, "xk")
    xv = mm1(mn, W["w_xv"], "nn", BF16, "xv")
    xo = xattn_fwd(xq, xk, xv, nseq)
    ho, x3 = mm_resid(xo, W["w_xo"], x2, p["xa_post_g"], 1.0, "xo")

    dx4, ffn2, sq_cols = _ffn_fwd(x3, p["ffn2_pre_g"], p["ffn2_post_g"], W, p, "ffn2", stage, target=tgt.reshape(T, D))
    loss_row = (0.5 / D) * jnp.sum(sq_cols.reshape(-1, 128), axis=0, keepdims=True)

    gw, gs = {}, {}
    dx3, gs["ffn2_pre_g"], gs["ffn2_post_g"] = _ffn_bwd(
        dx4, ffn2, p["ffn2_pre_g"], p["ffn2_post_g"], W["ffn2_w_gate"], W["ffn2_w_up"], W["ffn2_w_down"], "ffn2", stage, gw)

    dho, gs["xa_post_g"] = resid_bwd(ho, p["xa_post_g"], dx3, 1.0, "xa_post_bwd")
    dxo = mm1(dho, W["w_xo"], "nt", BF16, "xo_bwd")
    gw["w_xo"] = _tn(xo, dho, "d_w_xo")
    dxq, dxk, dxv = xattn_bwd(xq, xk, xv, dxo, nseq)
    dx2, gs["xa_pre_g"] = mm_rms_bwd([(dxq, W["w_xq"], "nt")], x2, p["xa_pre_g"], "xq_bwd", resid=dx3)
    gw["w_xq"] = _tn(hq, dxq, "d_w_xq")
    dmn = mm([[(dxk, W["w_xk"], "nt"), (dxv, W["w_xv"], "nt")]], [F32], "xkv_bwd")[0]
    gw["w_xk"] = _tn(mn, dxk, "d_w_xk")
    gw["w_xv"] = _tn(mn, dxv, "d_w_xv")
    _, gs["mem_norm_g"] = rms_bwd(mem2, p["mem_norm_g"], dmn, "mem_norm_bwd", dx_dtype=BF16)

    dhmix, gs["mix_post_g"] = resid_bwd(hmix, p["mix_post_g"], dx2, 1.0, "mix_post_bwd")
    dmerged = mm1(dhmix, W["w_out"], "nt", F32, "mix_out_bwd")
    gw["w_out"] = _tn(merged, dhmix, "d_w_out")
    dys, dym, dgl, gs["gate_bias"] = merge_bwd(gl, y_ssd, y_mla, dmerged, p["gate_bias"], "merge_bwd")

    unslot = lambda g, per: g.reshape(MLA_H, SLOT, -1)[:, :per].reshape(MLA_H * per, -1)
    do_s = mm1(dym, wo_s, "nt", BF16, "mla_proj_bwd")
    gw["w_mla_proj"] = unslot(_tn(o_s, dym, "d_w_mla_proj"), VD)
    dQc, dKc, dv_s, *sent = attn_slot_bwd(Qc, Kc, v_s, o_s, lse, do_s, nseq, comm=stage.scatter("attn_bwd", gw))
    stage.scattered("attn_bwd", sent)
    dq_s, dkn_s, dkr = rope_slot_bwd(dQc, dKc, pos, inv, "rope_bwd")
    dq_c, gs["q_norm_g"] = mm_rms_bwd([(dq_s, wq_s, "nn")], q_c, p["q_norm_g"], "uq_bwd", dx_dtype=BF16)
    gw["w_uq"] = unslot(_tn(dq_s, qn, "d_w_uq"), QK)
    dkv_c, gs["kv_norm_g"] = mm_rms_bwd([(dkn_s, wk_s, "nn"), (dv_s, wv_s, "nn")], kv_c, p["kv_norm_g"], "ukv_bwd", dx_dtype=BF16)
    gw["w_uk"] = unslot(_tn(dkn_s, kvn, "d_w_uk"), NOPE)
    gw["w_uv"] = unslot(_tn(dv_s, kvn, "d_w_uv"), VD)

    dyn = mm1(dys, W["w_ssd_proj"], "nt", F32, "ssd_proj_bwd")
    gw["w_ssd_proj"] = _tn(yn, dys, "d_w_ssd_proj")
    dyc, dz, gs["ssd_norm_g"] = gated_norm_bwd(y_ssd_core, z, dyn, p["ssd_norm_g"], "ssd_norm_bwd")
    dxbc_act, ddtr, gs["dt_bias"], gs["a_log"], gs["d_skip"], *sent = ssd_bwd(
        xbc_act, dtkr, p["dt_bias"], p["a_log"], p["d_skip"], prev, dyc, nseq, comm=stage.scatter("ssd_bwd", gw))
    stage.scattered("ssd_bwd", sent)
    dxbc, gs["conv_w"], gs["conv_b"] = conv_bwd(xbc, p["conv_w"], p["conv_b"], dxbc_act, nseq)

    gw["w_in"] = jnp.concatenate([_tn(dz, hm, "d_w_in_z"), _tn(dxbc, hm, "d_w_in_xbc"), _tn(ddtr, hm, "d_w_in_dt")[:SSD_H],
                                  _tn(dq_c, hm, "d_w_in_q"), _tn(dkv_c, hm, "d_w_in_kv"), _tn(dkr, hm, "d_w_in_kr")[:ROPE],
                                  _tn(dgl, hm, "d_w_in_gate")], axis=0)
    dx1, gs["mix_pre_g"], *sent = mm_rms_bwd(
        [(dz, wt_z, "nn"), (dxbc, wt_xbc, "nn"), (ddtr, wt_dt, "nn"), (dq_c, wt_q, "nn"), (dkv_c, wt_kv, "nn"),
         (dkr, wt_kr, "nn"), (dgl, wt_gate, "nn")], x1, p["mix_pre_g"], "in_bwd", resid=dx2, comm=stage.scatter("in_bwd", gw))
    stage.scattered("in_bwd", sent)

    dx0, gs["ffn1_pre_g"], gs["ffn1_post_g"] = _ffn_bwd(
        dx1, ffn1, p["ffn1_pre_g"], p["ffn1_post_g"], W["ffn1_w_gate"], W["ffn1_w_up"], W["ffn1_w_down"], "ffn1", stage, gw)
    return loss_row, dx0.reshape(x.shape), gw, gs


def kernel(x, mem, positions, ffn1_pre_g, ffn1_w_gate, ffn1_w_up, ffn1_w_down, ffn1_post_g, mix_pre_g, w_in, conv_w, conv_b, dt_bias, a_log, d_skip, ssd_norm_g, w_ssd_proj, q_norm_g, w_uq, kv_norm_g, w_uk, w_uv, w_mla_proj, gate_bias, w_out, mix_post_g, xa_pre_g, mem_norm_g, w_xq, w_xk, w_xv, w_xo, xa_post_g, ffn2_pre_g, ffn2_w_gate, ffn2_w_up, ffn2_w_down, ffn2_post_g, loss_target, m_ffn1_pre_g, m_ffn1_w_gate, m_ffn1_w_up, m_ffn1_w_down, m_ffn1_post_g, m_mix_pre_g, m_w_in, m_conv_w, m_conv_b, m_dt_bias, m_a_log, m_d_skip, m_ssd_norm_g, m_w_ssd_proj, m_q_norm_g, m_w_uq, m_kv_norm_g, m_w_uk, m_w_uv, m_w_mla_proj, m_gate_bias, m_w_out, m_mix_post_g, m_xa_pre_g, m_mem_norm_g, m_w_xq, m_w_xk, m_w_xv, m_w_xo, m_xa_post_g, m_ffn2_pre_g, m_ffn2_w_gate, m_ffn2_w_up, m_ffn2_w_down, m_ffn2_post_g, v_ffn1_pre_g, v_ffn1_w_gate, v_ffn1_w_up, v_ffn1_w_down, v_ffn1_post_g, v_mix_pre_g, v_w_in, v_conv_w, v_conv_b, v_dt_bias, v_a_log, v_d_skip, v_ssd_norm_g, v_w_ssd_proj, v_q_norm_g, v_w_uq, v_kv_norm_g, v_w_uk, v_w_uv, v_w_mla_proj, v_gate_bias, v_w_out, v_mix_post_g, v_xa_pre_g, v_mem_norm_g, v_w_xq, v_w_xk, v_w_xv, v_w_xo, v_xa_post_g, v_ffn2_pre_g, v_ffn2_w_gate, v_ffn2_w_up, v_ffn2_w_down, v_ffn2_post_g):
    a = dict(locals())
    w = {n: a[n] for n in WEIGHTS}
    m = {n: a["m_" + n] for n in WEIGHTS}
    v = {n: a["v_" + n] for n in WEIGHTS}

    stage = Stage(w)
    W, p = {}, {n: w[n] for n in SMALL}
    stage.gathered("first", run_comm(stage.gather("first"), "allgather_first"), W, p)

    loss_row, grad_x, gw, gs = _local_step(x, mem, positions, loss_target, W, p, stage)

    sm = _pack_small([gs[n] for n in SMALL], loss_row=loss_row, conv_w=gs["conv_w"])
    *recv_last, srecv = run_comm(ScatterComm(stage.pieces("last", gw) + [[jnp.broadcast_to(sm[None], (N_DEV,) + sm.shape)]]),
                                 "exchange_last")
    stage.scattered("last", recv_last)
    s_rows = sum_slots(srecv, "sum_small", tr=sm.shape[0])
    grads, delta, new_m, new_v = {}, {}, {}, {}

    def finish(n, buf, piece):
        col = KIND[n] == "col"
        turn = (lambda t: t.T) if col else (lambda t: t)
        K = w[n].shape[1]
        if col and buf.shape[2] != K:
            buf = buf.reshape(buf.shape[0], -1, K)
        res = adamw_from_slots(buf, piece, turn(w[n][0]), turn(m[n][0]), turn(v[n][0]), "adamw_" + n)
        grads[n], delta[n], new_m[n], new_v[n] = [turn(r)[None] for r in res]

    parts = {}
    for tag, groups in SCATTER_PLAN.items():
        for names, buf in zip(groups, stage.recv[tag]):
            for piece, n in enumerate(names):
                if n in PARTS:
                    parts[n] = sum_slots(buf, "sum_" + n.replace("#", "_"), tr=buf.shape[1])
                else:
                    finish(n, buf, piece)
    for base in sorted({PARTS[pn][0] for pn in parts}):
        rows = jnp.concatenate([parts[pn] for pn in _parts_of(base, "#")], axis=0)
        finish(base, rows[None], 0)
    conv_w_full = p["conv_w"]
    small = adamw_small(s_rows, [w[n] for n in SMALL], [m[n] for n in SMALL], [v[n] for n in SMALL])
    for t, vals in zip((grads, delta, new_m, new_v), small):
        t.update(zip(SMALL, vals))
    r1 = sum(-(-w[n].shape[1] // 128) for n in SMALL)
    ncw = math.prod(conv_w_full.shape) // 128
    cw_grad_full = s_rows[r1:r1 + ncw].reshape(conv_w_full.shape)
    wsh = conv_w.shape[2]
    grads["conv_w"] = lax.dynamic_slice_in_dim(cw_grad_full, _dev_index() * wsh, wsh, axis=1)[None]
    loss = jnp.sum(s_rows[r1 + ncw])
    d_, m_, v_ = adamw(conv_w[0], grads["conv_w"][0], m["conv_w"][0], v["conv_w"][0], "adamw_conv_w")
    delta["conv_w"], new_m["conv_w"], new_v["conv_w"] = d_[None], m_[None], v_[None]
    return (loss, grad_x, *[grads[n] for n in WEIGHTS], *[delta[n] for n in WEIGHTS],
            *[new_m[n] for n in WEIGHTS], *[new_v[n] for n in WEIGHTS])
```

```python
import functools
import math

import jax
import jax.numpy as jnp
from jax import lax
from jax.experimental import pallas as pl
from jax.experimental.pallas import tpu as pltpu

F32, BF16 = jnp.float32, jnp.bfloat16
HI = lax.Precision.HIGHEST
MESH = pl.DeviceIdType.MESH
N_DEV = 8

D = 1024
DFF = 2816
SSD_H, SSD_P, SSD_G, SSD_N, SSD_L = 16, 64, 2, 128, 128
SSD_INNER = SSD_H * SSD_P
CONV_K, CONV_CH = 4, 1536
MLA_H, QR, KVR, NOPE, ROPE, VD = 16, 384, 256, 64, 32, 64
QK = NOPE + ROPE
ROPE_THETA = 10000.0
XA_H, XA_D = 4, 256
EPS = 1e-6
FFN_RES = 0.5
LR, B1, B2, AEPS, WD, STEP = 0.001, 0.9, 0.999, 1e-08, 0.01, 10

VMEM_LIMIT = 56 * 2**20


def _cp(*sem):
    return pltpu.CompilerParams(dimension_semantics=sem, vmem_limit_bytes=VMEM_LIMIT)


def _sigmoid(x):
    return 1.0 / (1.0 + jnp.exp(-x))


def _softplus(x):
    return jnp.where(x > 20.0, x, jnp.log(1.0 + jnp.exp(jnp.minimum(x, 20.0))))


def _dot(a, b, dims="nn"):
    ca = 0 if dims[0] == "t" else 1
    cb = 1 if dims[1] == "t" else 0
    return lax.dot_general(a.astype(BF16), b.astype(BF16), (((ca,), (cb,)), ((), ())), preferred_element_type=F32)


def _dot_sel(a, b, dims="nn", split="a", terms=3):
    r = (a if split == "a" else b).astype(F32)
    out = None
    for t in range(terms):
        piece = r.astype(BF16)
        if t + 1 < terms:
            r = r - piece.astype(F32)
        d = _dot(piece, b, dims) if split == "a" else _dot(a, piece, dims)
        out = d if out is None else out + d
    return out


def _ssd_common(dtr, dtb, alog):
    L = dtr.shape[0]
    dt = _softplus(dtr + dtb)
    a = -jnp.exp(alog)
    adt = dt * a
    r = lax.broadcasted_iota(jnp.int32, (L, L), 0)
    c = lax.broadcasted_iota(jnp.int32, (L, L), 1)
    lower = r >= c
    tri = lower.astype(F32)
    cs = _dot_sel(tri, adt, "nn", split="b")
    cs_t = _dot_sel(adt, tri, "tt")
    return dt, a, cs, cs_t, lower


def _head_expand():
    hh = lax.broadcasted_iota(jnp.int32, (SSD_H, SSD_INNER), 0)
    jj = lax.broadcasted_iota(jnp.int32, (SSD_H, SSD_INNER), 1)
    return ((jj >= hh * SSD_P) & (jj < hh * SSD_P + SSD_P)).astype(F32)


def _head_reduce():
    hh = lax.broadcasted_iota(jnp.int32, (SSD_INNER, SSD_H), 1)
    jj = lax.broadcasted_iota(jnp.int32, (SSD_INNER, SSD_H), 0)
    return ((jj >= hh * SSD_P) & (jj < hh * SSD_P + SSD_P)).astype(F32)


def ssd_fwd(xbc, dtr, dtb, alog, dsk, nseq, comm=None):
    T = xbc.shape[0]
    S = T // nseq
    C = S // SSD_L
    L = SSD_L
    NP = SSD_H // 2

    def body(x_ref, b_ref, c_ref, dtr_ref, dtb_ref, alog_ref, dsk_ref, y_ref, prev_ref, st_ref):
        ci = pl.program_id(1)

        @pl.when(ci == 0)
        def _():
            st_ref[...] = jnp.zeros_like(st_ref)

        dt, a, cs, cs_t, lower = _ssd_common(dtr_ref[:, 0:SSD_H], dtb_ref[...], alog_ref[...])
        E = _head_expand()
        X = x_ref[...].astype(F32)
        dt_e = _dot_sel(dt, E)
        cs_e = _dot_sel(cs, E)
        csl_e = cs_e[L - 1:L, :]
        Xd = X * dt_e
        Xf = Xd * jnp.exp(csl_e - cs_e)
        e_e = jnp.exp(cs_e)
        skip = _dot_sel(dsk_ref[...], E) * X
        lane = lax.broadcasted_iota(jnp.int32, (1, 2 * SSD_P), 1)
        rowp = lax.broadcasted_iota(jnp.int32, (2 * SSD_P, 1), 0)
        for g in range(SSD_G):
            Bg = b_ref[:, g * SSD_N:(g + 1) * SSD_N]
            Cg = c_ref[:, g * SSD_N:(g + 1) * SSD_N]
            cb = _dot(Cg, Bg, "nt")
            for pp in range(NP // SSD_G):
                p = g * (NP // SSD_G) + pp
                sl = slice(p * 2 * SSD_P, (p + 1) * 2 * SSD_P)
                Xd_p = Xd[:, sl]
                yd = jnp.zeros((L, 2 * SSD_P), F32)
                for q in range(2):
                    h = 2 * p + q
                    m = jnp.where(lower, jnp.exp(jnp.minimum(cs[:, h:h + 1] - cs_t[h:h + 1, :], 0.0)), 0.0)
                    mask = (lane >= q * SSD_P) & (lane < (q + 1) * SSD_P)
                    yd = yd + _dot(cb * m, jnp.where(mask, Xd_p, 0.0))
                S0 = st_ref[p]
                prev_ref[0, 0, p] = S0
                z = _dot(Cg, S0, "nt")
                y_ref[:, sl] = (skip[:, sl] + yd + z * e_e[:, sl]).astype(y_ref.dtype)
                h0 = 2 * p
                dec = jnp.where(rowp < SSD_P, jnp.exp(cs[L - 1:L, h0:h0 + 1]), jnp.exp(cs[L - 1:L, h0 + 1:h0 + 2]))
                st_ref[p] = S0 * dec + _dot(Xf[:, sl], Bg, "tn")

    row = lambda b, c: (b * C + c, 0)
    small = pl.BlockSpec((1, SSD_H), lambda b, c: (0, 0))
    return _call_with_comm(
        body, (nseq, C), "ssd_fwd",
        [pl.BlockSpec((L, SSD_INNER), row),
         pl.BlockSpec((L, SSD_G * SSD_N), lambda b, c: (b * C + c, SSD_INNER // (SSD_G * SSD_N))),
         pl.BlockSpec((L, SSD_G * SSD_N), lambda b, c: (b * C + c, SSD_INNER // (SSD_G * SSD_N) + 1)),
         pl.BlockSpec((L, 128), row), small, small, small],
        [xbc, xbc, xbc, dtr, dtb, alog, dsk],
        [pl.BlockSpec((L, SSD_INNER), row), pl.BlockSpec((1, 1, NP, 2 * SSD_P, SSD_N), lambda b, c: (b, c, 0, 0, 0))],
        [jax.ShapeDtypeStruct((T, SSD_INNER), BF16), jax.ShapeDtypeStruct((nseq, C, NP, 2 * SSD_P, SSD_N), F32)],
        comm, scratch=[pltpu.VMEM((NP, 2 * SSD_P, SSD_N), F32)], sem=("parallel", "arbitrary"))


def ssd_bwd(xbc, dtr, dtb, alog, dsk, prev, dy, nseq, comm=None):
    T = xbc.shape[0]
    S = T // nseq
    C = S // SSD_L
    L = SSD_L
    NP = SSD_H // 2

    def body(x_ref, b_ref, c_ref, dtr_ref, dtb_ref, alog_ref, dsk_ref, prev_ref, dy_ref,
             dxbc_ref, ddtr_ref, ddtb_ref, dalog_ref, ddsk_ref, ds_ref, stg_ref):
        bi = pl.program_id(0)
        ci = pl.program_id(1)

        @pl.when(ci == 0)
        def _():
            ds_ref[...] = jnp.zeros_like(ds_ref)

        @pl.when((ci == 0) & (bi == 0))
        def _():
            ddtb_ref[...] = jnp.zeros_like(ddtb_ref)
            dalog_ref[...] = jnp.zeros_like(dalog_ref)
            ddsk_ref[...] = jnp.zeros_like(ddsk_ref)

        dtr = dtr_ref[:, 0:SSD_H]
        dtb = dtb_ref[...]
        dt, a, cs, cs_t, lower = _ssd_common(dtr, dtb, alog_ref[...])
        upper = lax.broadcasted_iota(jnp.int32, (L, L), 1) >= lax.broadcasted_iota(jnp.int32, (L, L), 0)
        E = _head_expand()
        ET = _head_reduce()
        X = x_ref[...].astype(F32)
        dY = dy_ref[...].astype(F32)
        dt_e = _dot_sel(dt, E)
        cs_e = _dot_sel(cs, E)
        csl_e = cs_e[L - 1:L, :]
        f_e = jnp.exp(csl_e - cs_e)
        e_e = jnp.exp(cs_e)
        dsk_e = _dot_sel(dsk_ref[...], E)
        Xd = X * dt_e
        Xf = Xd * f_e
        lane = lax.broadcasted_iota(jnp.int32, (1, 2 * SSD_P), 1)
        rowp = lax.broadcasted_iota(jnp.int32, (2 * SSD_P, 1), 0)
        hsel = lax.broadcasted_iota(jnp.int32, (1, SSD_H), 1)
        dcs = jnp.zeros((L, SSD_H), F32)
        dcsl = jnp.zeros((1, SSD_H), F32)
        for g in range(SSD_G):
            Bg = b_ref[:, g * SSD_N:(g + 1) * SSD_N]
            Cg = c_ref[:, g * SSD_N:(g + 1) * SSD_N]
            cb = _dot(Cg, Bg, "nt")
            cbt = _dot(Bg, Cg, "nt")
            dB = jnp.zeros((L, SSD_N), F32)
            dC = jnp.zeros((L, SSD_N), F32)
            for pp in range(NP // SSD_G):
                p = g * (NP // SSD_G) + pp
                sl = slice(p * 2 * SSD_P, (p + 1) * 2 * SSD_P)
                Xd_p = Xd[:, sl]
                dY_p = dY[:, sl]
                dXd_p = jnp.zeros((L, 2 * SSD_P), F32)
                for q in range(2):
                    h = 2 * p + q
                    mask = (lane >= q * SSD_P) & (lane < (q + 1) * SSD_P)
                    col = cs[:, h:h + 1]
                    rw = cs_t[h:h + 1, :]
                    m = jnp.where(lower, jnp.exp(jnp.minimum(col - rw, 0.0)), 0.0)
                    mt = jnp.where(upper, jnp.exp(jnp.minimum(rw - col, 0.0)), 0.0)
                    dYm = jnp.where(mask, dY_p, 0.0)
                    dW = _dot(dYm, Xd_p, "nt")
                    dWt = _dot(Xd_p, dYm, "nt")
                    w = cb * m
                    wt = cbt * mt
                    dC = dC + _dot(dW * m, Bg)
                    dB = dB + _dot(dWt * mt, Cg)
                    dXd_p = dXd_p + jnp.where(mask, _dot(wt, dY_p), 0.0)
                    qcol = jnp.sum(dW * w, axis=1, keepdims=True) - jnp.sum(dWt * wt, axis=1, keepdims=True)
                    dcs = dcs + qcol * (hsel == h).astype(F32)
                S0 = prev_ref[0, 0, p]
                dSn = ds_ref[p]
                dZ = dY_p * e_e[:, sl]
                dC = dC + _dot(dZ, S0)
                h0 = 2 * p
                el0 = jnp.exp(cs[L - 1:L, h0:h0 + 1])
                el1 = jnp.exp(cs[L - 1:L, h0 + 1:h0 + 2])
                dec = jnp.where(rowp < SSD_P, el0, el1)
                ds_ref[p] = dSn * dec + _dot(dZ, Cg, "tn")
                dXf_p = _dot(Bg, dSn, "nt")
                dB = dB + _dot(Xf[:, sl], dSn)
                rs = jnp.sum(dSn * S0, axis=1, keepdims=True)
                s0 = jnp.sum(jnp.where(rowp < SSD_P, rs, 0.0), axis=0, keepdims=True) * el0
                s1 = jnp.sum(jnp.where(rowp >= SSD_P, rs, 0.0), axis=0, keepdims=True) * el1
                dcsl = dcsl + s0 * (hsel == h0).astype(F32) + s1 * (hsel == h0 + 1).astype(F32)
                y_off = _dot(Cg, S0, "nt") * e_e[:, sl]
                t1 = dY_p * y_off - dXf_p * Xf[:, sl]
                r1 = jnp.where(lane < SSD_P, t1, 0.0)
                c0 = jnp.sum(r1, axis=1, keepdims=True)
                c1 = jnp.sum(t1 - r1, axis=1, keepdims=True)
                dcs = dcs + c0 * (hsel == h0).astype(F32) + c1 * (hsel == h0 + 1).astype(F32)
                t2 = dXf_p * Xf[:, sl]
                r2 = jnp.where(lane < SSD_P, t2, 0.0)
                dcsl = dcsl + jnp.sum(r2, keepdims=True) * (hsel == h0).astype(F32) \
                    + jnp.sum(t2 - r2, keepdims=True) * (hsel == h0 + 1).astype(F32)
                stg_ref[:, sl] = dXd_p + dXf_p * f_e[:, sl]
            dxbc_ref[:, SSD_INNER + g * SSD_N:SSD_INNER + (g + 1) * SSD_N] = dB.astype(dxbc_ref.dtype)
            dxbc_ref[:, SSD_INNER + (SSD_G + g) * SSD_N:SSD_INNER + (SSD_G + g + 1) * SSD_N] = dC.astype(dxbc_ref.dtype)
        dXd = stg_ref[...]
        dxbc_ref[:, 0:SSD_INNER] = (dXd * dt_e + dsk_e * dY).astype(dxbc_ref.dtype)
        rowl = lax.broadcasted_iota(jnp.int32, (L, 1), 0)
        dcs = dcs + jnp.where(rowl == L - 1, dcsl, 0.0)
        dalpha = _dot_sel(upper.astype(F32), dcs, split="b")
        ddt = _dot_sel(dXd * X, ET, terms=2) + dalpha * a
        dalog_ref[...] += jnp.sum(dalpha * dt, axis=0, keepdims=True) * a
        ddtr = ddt * _sigmoid(dtr + dtb)
        spread = (lax.broadcasted_iota(jnp.int32, (SSD_H, 128), 0) == lax.broadcasted_iota(jnp.int32, (SSD_H, 128), 1)).astype(F32)
        ddtr_ref[...] = _dot(ddtr, spread).astype(ddtr_ref.dtype)
        ddtb_ref[...] += jnp.sum(ddtr, axis=0, keepdims=True)
        ddsk_ref[...] += jnp.sum(_dot_sel(dY * X, ET, terms=2), axis=0, keepdims=True)

    rowr = lambda b, c: (b * C + (C - 1 - c), 0)
    small = pl.BlockSpec((1, SSD_H), lambda b, c: (0, 0))
    return _call_with_comm(
        body, (nseq, C), "ssd_bwd",
        [pl.BlockSpec((L, SSD_INNER), rowr),
         pl.BlockSpec((L, SSD_G * SSD_N), lambda b, c: (b * C + (C - 1 - c), SSD_INNER // (SSD_G * SSD_N))),
         pl.BlockSpec((L, SSD_G * SSD_N), lambda b, c: (b * C + (C - 1 - c), SSD_INNER // (SSD_G * SSD_N) + 1)),
         pl.BlockSpec((L, 128), rowr), small, small, small,
         pl.BlockSpec((1, 1, NP, 2 * SSD_P, SSD_N), lambda b, c: (b, C - 1 - c, 0, 0, 0)),
         pl.BlockSpec((L, SSD_INNER), rowr)],
        [xbc, xbc, xbc, dtr, dtb, alog, dsk, prev, dy],
        [pl.BlockSpec((L, CONV_CH), rowr), pl.BlockSpec((L, 128), rowr), small, small, small],
        [jax.ShapeDtypeStruct((T, CONV_CH), BF16), jax.ShapeDtypeStruct((T, 128), BF16),
         jax.ShapeDtypeStruct((1, SSD_H), F32), jax.ShapeDtypeStruct((1, SSD_H), F32), jax.ShapeDtypeStruct((1, SSD_H), F32)],
        comm, scratch=[pltpu.VMEM((NP, 2 * SSD_P, SSD_N), F32), pltpu.VMEM((L, SSD_INNER), F32)], sem=("arbitrary", "arbitrary"))


SLOT = 128
ATT_T = 512
ATT_HP = 1
LOG2E = math.log2(math.e)
Q_SCALE = QK ** -0.5 * LOG2E


def _col_to_row(col):
    n = col.shape[0]
    eye = lax.broadcasted_iota(jnp.int32, (n, n), 0) == lax.broadcasted_iota(jnp.int32, (n, n), 1)
    return jnp.sum(jnp.where(eye, col, 0.0), axis=0, keepdims=True)


def attn_slot_fwd(q, k, v, nseq, comm=None):
    T = q.shape[0]
    S = T // nseq
    t = min(ATT_T, S)
    nb = S // t
    cols = [slice(h * SLOT, (h + 1) * SLOT) for h in range(ATT_HP)]

    def body(q_ref, k_ref, v_ref, o_ref, lse_ref):
        causal = lax.broadcasted_iota(jnp.int32, (t, t), 1) <= lax.broadcasted_iota(jnp.int32, (t, t), 0)
        for qi in range(nb):
            rows = slice(qi * t, (qi + 1) * t)
            state = [None] * ATT_HP
            for kj in range(qi + 1):
                keys = slice(kj * t, (kj + 1) * t)
                for h, c in enumerate(cols):
                    s = _dot(q_ref[rows, c], k_ref[keys, c], "nt")
                    if kj == qi:
                        s = jnp.where(causal, s, -1e30)
                    bm = jnp.max(s, axis=1, keepdims=True)
                    if kj == 0:
                        p = jnp.exp2(s - bm)
                        state[h] = (bm, jnp.sum(p, axis=1, keepdims=True), _dot(p, v_ref[keys, c]))
                    else:
                        m, l, acc = state[h]
                        m_new = jnp.maximum(m, bm)
                        corr = jnp.exp2(m - m_new)
                        p = jnp.exp2(s - m_new)
                        state[h] = (m_new, l * corr + jnp.sum(p, axis=1, keepdims=True), acc * corr + _dot(p, v_ref[keys, c]))
            for h, c in enumerate(cols):
                m, l, acc = state[h]
                o_ref[rows, c] = (acc / l).astype(o_ref.dtype)
                lse_ref[0, h, :, rows] = _col_to_row(m + jnp.log2(l))

    blk = pl.BlockSpec((S, ATT_HP * SLOT), lambda b, h: (b, h))
    return _call_with_comm(
        body, (nseq, MLA_H // ATT_HP), "attn_fwd", [blk, blk, blk], [q, k, v],
        [blk, pl.BlockSpec((1, ATT_HP, 1, S), lambda b, h: (b, h, 0, 0))],
        [jax.ShapeDtypeStruct((T, MLA_H * SLOT), BF16), jax.ShapeDtypeStruct((nseq, MLA_H, 1, S), F32)], comm)


def attn_slot_bwd(q, k, v, o, lse, do, nseq, comm=None):
    T = q.shape[0]
    S = T // nseq
    t = min(ATT_T, S)
    nb = S // t
    scale = QK ** -0.5
    cols = [slice(h * SLOT, (h + 1) * SLOT) for h in range(ATT_HP)]

    def body(q_ref, k_ref, v_ref, o_ref, lse_ref, do_ref, dq_ref, dk_ref, dv_ref, dqa_ref):
        causal_t = lax.broadcasted_iota(jnp.int32, (t, t), 0) <= lax.broadcasted_iota(jnp.int32, (t, t), 1)
        ones = jnp.ones((8, SLOT), F32)
        delta = {}
        for qi in range(nb):
            sl = slice(qi * t, (qi + 1) * t)
            for h, c in enumerate(cols):
                prod = do_ref[sl, c].astype(F32) * o_ref[sl, c].astype(F32)
                delta[h, qi] = _dot_sel(ones, prod, "nt", split="b", terms=2)[0:1, :]
        for kj in range(nb):
            ks = slice(kj * t, (kj + 1) * t)
            dk = [None] * ATT_HP
            dv = [None] * ATT_HP
            for qi in range(kj, nb):
                sl = slice(qi * t, (qi + 1) * t)
                for h, c in enumerate(cols):
                    kb, vb, qb, dob = k_ref[ks, c], v_ref[ks, c], q_ref[sl, c], do_ref[sl, c]
                    st = _dot(kb, qb, "nt")
                    pt = jnp.exp2(st - lse_ref[0, h, :, sl])
                    if qi == kj:
                        pt = jnp.where(causal_t, pt, 0.0)
                    dpt = _dot(vb, dob, "nt")
                    dst = (pt * (dpt - delta[h, qi])).astype(BF16)
                    dvc = _dot(pt, dob)
                    dkc = _dot(dst, qb) * (1.0 / LOG2E)
                    dv[h] = dvc if dv[h] is None else dv[h] + dvc
                    dk[h] = dkc if dk[h] is None else dk[h] + dkc
                    dqc = _dot(dst, kb, "tn") * scale
                    if kj > 0:
                        dqc = dqc + dqa_ref[sl, c]
                    if qi == kj:
                        dq_ref[sl, c] = dqc.astype(dq_ref.dtype)
                    else:
                        dqa_ref[sl, c] = dqc
            for h, c in enumerate(cols):
                dk_ref[ks, c] = dk[h].astype(dk_ref.dtype)
                dv_ref[ks, c] = dv[h].astype(dv_ref.dtype)

    blk = pl.BlockSpec((S, ATT_HP * SLOT), lambda b, h: (b, h))
    lse_spec = pl.BlockSpec((1, ATT_HP, 1, S), lambda b, h: (b, h, 0, 0))
    W = MLA_H * SLOT
    return _call_with_comm(
        body, (nseq, MLA_H // ATT_HP), "attn_bwd", [blk, blk, blk, blk, lse_spec, blk], [q, k, v, o, lse, do], [blk, blk, blk],
        [jax.ShapeDtypeStruct((T, W), BF16)] * 3, comm, scratch=[pltpu.VMEM((S, ATT_HP * SLOT), F32)])


def _rope_coeffs(pos, inv):
    half = ROPE // 2
    ang = pos * inv
    lane = lax.broadcasted_iota(jnp.int32, (1, SLOT), 1)
    sn = jnp.sin(ang)
    C = jnp.where(lane < NOPE, 1.0, jnp.where(lane < QK, jnp.cos(ang), 0.0))
    Sg = jnp.where((lane >= NOPE) & (lane < NOPE + half), -sn, jnp.where((lane >= NOPE + half) & (lane < QK), sn, 0.0))
    return C, Sg


def _rope_inputs(positions):
    half = ROPE // 2
    inv = ROPE_THETA ** (-jnp.arange(0, ROPE, 2, dtype=F32) / ROPE)
    row = jnp.zeros((1, SLOT), F32).at[0, NOPE:NOPE + half].set(inv).at[0, NOPE + half:QK].set(inv)
    return positions.astype(F32).reshape(-1, 1), row


def _place_k_rope(kr_lanes):
    r = lax.broadcasted_iota(jnp.int32, (SLOT, SLOT), 0)
    c = lax.broadcasted_iota(jnp.int32, (SLOT, SLOT), 1)
    return _dot_sel(kr_lanes, ((c == r + NOPE) & (r < ROPE)).astype(F32))


def rope_table(pos, inv):
    return rowwise(_rope_coeffs, [pos], [inv], [(SLOT, F32), (SLOT, F32)], [], "rope_table")


def rope_q_epilogue(accs, ex):
    C, Sg = ex[0], ex[1]
    reps = accs[0].shape[1] // SLOT
    return ((accs[0] * jnp.tile(C, (1, reps)) + _rope_swap(accs[0]) * jnp.tile(Sg, (1, reps))) * Q_SCALE,)


def rope_k_epilogue(accs, ex):
    C, Sg = ex[0], ex[1]
    kr = _place_k_rope(ex[2][:, SLOT:2 * SLOT])
    kr = kr * C + _rope_swap(kr) * Sg
    return (accs[0] + jnp.tile(kr, (1, accs[0].shape[1] // SLOT)),)


def _rope_swap(x):
    W = x.shape[1]
    half = ROPE // 2
    lane = lax.broadcasted_iota(jnp.int32, (1, W), 1) & (SLOT - 1)
    up = pltpu.roll(x, W - half, axis=1)
    dn = pltpu.roll(x, half, axis=1)
    return jnp.where((lane >= NOPE) & (lane < NOPE + half), up, jnp.where((lane >= NOPE + half) & (lane < QK), dn, 0.0))


def rope_slot_bwd(dq, dk, C, Sg, name):
    def fn(dqv, dkv, C, Sg):
        ct, stl = jnp.tile(C, (1, MLA_H)), jnp.tile(Sg, (1, MLA_H))
        dqo = dqv * ct - _rope_swap(dqv) * stl
        tot = dkv[:, 0:SLOT]
        for h in range(1, MLA_H):
            tot = tot + dkv[:, h * SLOT:(h + 1) * SLOT]
        u = tot * C - _rope_swap(tot) * Sg
        r = lax.broadcasted_iota(jnp.int32, (SLOT, SLOT), 0)
        c = lax.broadcasted_iota(jnp.int32, (SLOT, SLOT), 1)
        unplace = ((r == c + NOPE) & (c < ROPE)).astype(F32)
        return dqo, dkv, _dot_sel(u, unplace, terms=2)
    W = MLA_H * SLOT
    return rowwise(fn, [dq, dk, C, Sg], [], [(W, BF16), (W, BF16), (SLOT, BF16)], [], name)


XA_BLK = 512


def xattn_fwd(q, k, v, nseq):
    T = q.shape[0]
    S = T // nseq
    M = k.shape[0] // nseq
    tq = min(XA_BLK, S)
    nq = S // tq
    scale = XA_D ** -0.5

    def body(q_ref, k_ref, v_ref, o_ref):
        s = _dot(q_ref[...], k_ref[...], "nt") * scale
        p = jnp.exp(s - jnp.max(s, axis=1, keepdims=True))
        p = p / jnp.sum(p, axis=1, keepdims=True)
        o_ref[...] = _dot(p, v_ref[...]).astype(o_ref.dtype)

    qs = pl.BlockSpec((tq, XA_D), lambda b, h, i: (b * nq + i, h))
    ks = pl.BlockSpec((M, XA_D), lambda b, h, i: (b, h))
    return pl.pallas_call(
        body, grid=(nseq, XA_H, nq), name="xattn_fwd", in_specs=[qs, ks, ks], out_specs=qs,
        out_shape=jax.ShapeDtypeStruct((T, XA_H * XA_D), BF16),
        compiler_params=_cp("parallel", "parallel", "parallel"),
    )(q, k, v)


def xattn_bwd(q, k, v, do, nseq):
    T = q.shape[0]
    S = T // nseq
    M = k.shape[0] // nseq
    tq = min(XA_BLK, S)
    nq = S // tq
    scale = XA_D ** -0.5

    def body(q_ref, k_ref, v_ref, do_ref, dq_ref, dk_ref, dv_ref):
        @pl.when(pl.program_id(2) == 0)
        def _():
            dk_ref[...] = jnp.zeros_like(dk_ref)
            dv_ref[...] = jnp.zeros_like(dv_ref)

        qb, kb, vb, dob = q_ref[...], k_ref[...], v_ref[...], do_ref[...]
        s = _dot(qb, kb, "nt") * scale
        p = jnp.exp(s - jnp.max(s, axis=1, keepdims=True))
        p = p / jnp.sum(p, axis=1, keepdims=True)
        dp = _dot(dob, vb, "nt")
        ds = p * (dp - jnp.sum(dp * p, axis=1, keepdims=True)) * scale
        dq_ref[...] = _dot(ds, kb).astype(dq_ref.dtype)
        dk_ref[...] += _dot(ds, qb, "tn")
        dv_ref[...] += _dot(p, dob, "tn")

    qs = pl.BlockSpec((tq, XA_D), lambda b, h, i: (b * nq + i, h))
    ks = pl.BlockSpec((M, XA_D), lambda b, h, i: (b, h))
    return pl.pallas_call(
        body, grid=(nseq, XA_H, nq), name="xattn_bwd", in_specs=[qs, ks, ks, qs], out_specs=[qs, ks, ks],
        out_shape=[jax.ShapeDtypeStruct((T, XA_H * XA_D), BF16), jax.ShapeDtypeStruct(k.shape, F32),
                   jax.ShapeDtypeStruct(k.shape, F32)],
        compiler_params=_cp("parallel", "parallel", "arbitrary"),
    )(q, k, v, do)


CONV_BLK = 256


def _shift_down(x, s, rows):
    if s == 0:
        return x
    return jnp.where(rows >= s, pltpu.roll(x, s, axis=0), 0.0)


def _shift_up(x, s, rows):
    if s == 0:
        return x
    S = x.shape[0]
    return jnp.where(rows < S - s, pltpu.roll(x, S - s, axis=0), 0.0)


def conv_fwd(x, w, b, nseq):
    T, CH = x.shape
    S = T // nseq

    def body(x_ref, w_ref, b_ref, o_ref):
        xv = x_ref[...].astype(F32)
        rows = lax.broadcasted_iota(jnp.int32, (S, 1), 0)
        c = jnp.zeros_like(xv) + b_ref[...]
        for kk in range(CONV_K):
            c = c + w_ref[kk:kk + 1, :] * _shift_down(xv, CONV_K - 1 - kk, rows)
        o_ref[...] = (c * _sigmoid(c)).astype(o_ref.dtype)

    xs = pl.BlockSpec((S, CONV_BLK), lambda j, bb: (bb, j))
    return pl.pallas_call(
        body, grid=(CH // CONV_BLK, nseq), name="conv_fwd",
        in_specs=[xs, pl.BlockSpec((CONV_K, CONV_BLK), lambda j, bb: (0, j)), pl.BlockSpec((1, CONV_BLK), lambda j, bb: (0, j))],
        out_specs=xs, out_shape=jax.ShapeDtypeStruct((T, CH), BF16),
        compiler_params=_cp("parallel", "parallel"),
    )(x, w, b)


def conv_bwd(x, w, b, dout, nseq):
    T, CH = x.shape
    S = T // nseq

    def body(x_ref, w_ref, b_ref, do_ref, dx_ref, dw_ref, db_ref):
        @pl.when(pl.program_id(1) == 0)
        def _():
            dw_ref[...] = jnp.zeros_like(dw_ref)
            db_ref[...] = jnp.zeros_like(db_ref)

        xv = x_ref[...].astype(F32)
        rows = lax.broadcasted_iota(jnp.int32, (S, 1), 0)
        c = jnp.zeros_like(xv) + b_ref[...]
        sh = [_shift_down(xv, CONV_K - 1 - kk, rows) for kk in range(CONV_K)]
        for kk in range(CONV_K):
            c = c + w_ref[kk:kk + 1, :] * sh[kk]
        sg = _sigmoid(c)
        dc = do_ref[...].astype(F32) * sg * (1.0 + c * (1.0 - sg))
        dx = jnp.zeros_like(xv)
        for kk in range(CONV_K):
            dx = dx + w_ref[kk:kk + 1, :] * _shift_up(dc, CONV_K - 1 - kk, rows)
            dw_ref[kk:kk + 1, :] += jnp.sum(dc * sh[kk], axis=0, keepdims=True)
        dx_ref[...] = dx.astype(dx_ref.dtype)
        db_ref[...] += jnp.sum(dc, axis=0, keepdims=True)

    xs = pl.BlockSpec((S, CONV_BLK), lambda j, bb: (bb, j))
    ws = pl.BlockSpec((CONV_K, CONV_BLK), lambda j, bb: (0, j))
    bs = pl.BlockSpec((1, CONV_BLK), lambda j, bb: (0, j))
    return pl.pallas_call(
        body, grid=(CH // CONV_BLK, nseq), name="conv_bwd",
        in_specs=[xs, ws, bs, xs], out_specs=[xs, ws, bs],
        out_shape=[jax.ShapeDtypeStruct((T, CH), BF16), jax.ShapeDtypeStruct((CONV_K, CH), F32),
                   jax.ShapeDtypeStruct((1, CH), F32)],
        compiler_params=_cp("parallel", "arbitrary"),
    )(x, w, b, dout)


def _dims(a, b, mode):
    M = a.shape[1] if mode[0] == "t" else a.shape[0]
    K = a.shape[0] if mode[0] == "t" else a.shape[1]
    N = b.shape[0] if mode[1] == "t" else b.shape[1]
    return M, K, N


def _tile(dim, prefs):
    for p in prefs:
        if dim % p == 0:
            return p
    return dim


def mm(groups, out_dtypes, name, tm=None, tn=None, tk=None, epi=None, extras=(), comm=None, sub=1, n_sum=0):
    a0, b0, m0 = groups[0][0]
    M, K0, N = _dims(a0, b0, m0)
    tm = tm or _tile(M, (1024, 512, 256, 128))
    tn = tn or _tile(N, (1024, 512, 256, 128))
    flat = [p for g in groups for p in g]
    nk = 1 if tk is None else K0 // tk
    in_specs, args = [], []
    for a, b, mode in flat:
        _, K, _ = _dims(a, b, mode)
        kb = K if tk is None else tk
        in_specs.append(pl.BlockSpec((kb, tm), lambda i, j, k: (k, i)) if mode[0] == "t"
                        else pl.BlockSpec((tm, kb), lambda i, j, k: (i, k)))
        in_specs.append(pl.BlockSpec((tn, kb), lambda i, j, k: (j, k)) if mode[1] == "t"
                        else pl.BlockSpec((kb, tn), lambda i, j, k: (k, j)))
        args += [a, b]
    kinds = []
    for e in extras:
        kind, e = e if isinstance(e, tuple) else ("vec" if e.shape[0] == 1 and M != 1 else "tile", e)
        in_specs.append({"tile": pl.BlockSpec((tm, tn), lambda i, j, k: (i, j)),
                         "vec": pl.BlockSpec((1, tn), lambda i, j, k: (0, j)),
                         "rows": pl.BlockSpec((tm, e.shape[1]), lambda i, j, k: (i, 0)),
                         "whole": pl.BlockSpec(e.shape, lambda i, j, k: (0, 0))}[kind])
        kinds.append(kind)
        args.append(e)
    n_in = len(args)
    n_main = len(out_dtypes)
    n_out = n_main + n_sum
    assert n_sum == 0 or (tn == N and tk is None)
    ng = len(groups)
    sizes = [len(g) for g in groups]

    def body(*refs):
        ins, outs, accs = refs[:n_in], refs[n_in:n_in + n_out], refs[n_in + n_out:]
        kk = pl.program_id(2)

        def dots(rs):
            vals, pos = [], 0
            for gi in range(ng):
                acc = None
                for _ in range(sizes[gi]):
                    mode = flat[pos // 2][2]
                    av = ins[pos][:, rs] if mode[0] == "t" else ins[pos][rs, :]
                    d = _dot(av, ins[pos + 1][...], mode)
                    acc = d if acc is None else acc + d
                    pos += 2
                vals.append(acc)
            return vals

        def finish(accv, rs, first_chunk=True):
            ex = [(r[rs, :] if kind in ("tile", "rows") else r[...]).astype(F32) for kind, r in zip(kinds, ins[2 * len(flat):])]
            res = epi(accv, ex) if epi is not None else tuple(accv)
            for o, r in zip(outs[:n_main], res[:n_main]):
                o[rs, :] = r.astype(o.dtype)
            for o, r in zip(outs[n_main:], res[n_main:]):
                if first_chunk:
                    @pl.when(pl.program_id(0) == 0)
                    def _():
                        o[...] = r

                    @pl.when(pl.program_id(0) > 0)
                    def _():
                        o[...] += r
                else:
                    o[...] += r

        if nk == 1:
            for r in range(sub):
                rs = slice(r * (tm // sub), (r + 1) * (tm // sub))
                finish(dots(rs), rs, r == 0)
        else:
            vals = dots(slice(0, tm))
            finish = functools.partial(finish, rs=slice(0, tm))
            @pl.when(kk == 0)
            def _():
                for ar, vv in zip(accs, vals):
                    ar[...] = vv

            @pl.when(kk > 0)
            def _():
                for ar, vv in zip(accs, vals):
                    ar[...] += vv

            @pl.when(kk == nk - 1)
            def _():
                finish([ar[...] for ar in accs])

    grid = (M // tm, N // tn, nk)
    out_specs = [pl.BlockSpec((tm, tn), lambda i, j, k: (i, j)) for _ in out_dtypes] \
        + [pl.BlockSpec((1, tn), lambda i, j, k: (0, j))] * n_sum
    out_shape = [jax.ShapeDtypeStruct((M, N), dt) for dt in out_dtypes] + [jax.ShapeDtypeStruct((1, N), F32)] * n_sum
    scratch = [pltpu.VMEM((tm, tn), F32) for _ in range(ng if nk > 1 else 0)]
    sem = ("arbitrary" if n_sum else "parallel", "parallel", "arbitrary")
    if comm is not None:
        body = _attach(comm, body, n_in, n_out, *_grid_ends(grid))
        in_specs, args = in_specs + [HBM_SPEC] * len(comm.inputs), args + comm.inputs
        out_specs, out_shape = out_specs + [HBM_SPEC] * len(comm.out_shapes), out_shape + comm.out_shapes
        scratch, sem = scratch + comm.sems, ("arbitrary",) * 3
    return pl.pallas_call(body, grid=grid, name=name, in_specs=in_specs, out_specs=out_specs, out_shape=out_shape,
                          scratch_shapes=scratch, compiler_params=_cp(*sem))(*args)


def mm1(a, b, mode, out_dtype, name, **kw):
    return mm([[(a, b, mode)]], [out_dtype], name, **kw)[0]


ROW_BLK = 512


def rowwise(fn, rows, consts, outs, accs, name, tb=ROW_BLK):
    rows = [r if isinstance(r, tuple) else (r, r.shape[1], 0) for r in rows]
    T = rows[0][0].shape[0]
    tb = min(tb, T)
    n_r, n_c, n_o, n_a = len(rows), len(consts), len(outs), len(accs)

    def body(*refs):
        vals = [r[...].astype(F32) for r in refs[:n_r + n_c]]
        res = fn(*vals)
        o_refs = refs[n_r + n_c:n_r + n_c + n_o]
        a_refs = refs[n_r + n_c + n_o:]
        for o, r in zip(o_refs, res[:n_o]):
            o[...] = r.astype(o.dtype)
        if n_a:
            @pl.when(pl.program_id(0) == 0)
            def _():
                for ar in a_refs:
                    ar[...] = jnp.zeros_like(ar)
            for ar, r in zip(a_refs, res[n_o:]):
                ar[...] += r

    return pl.pallas_call(
        body, grid=(T // tb,), name=name,
        in_specs=[pl.BlockSpec((tb, w), functools.partial(lambda i, j: (i, j), j=j)) for _, w, j in rows]
        + [pl.BlockSpec(c.shape, lambda i: (0, 0)) for c in consts],
        out_specs=[pl.BlockSpec((tb, d), lambda i: (i, 0)) for d, _ in outs]
        + [pl.BlockSpec(s, lambda i: (0, 0)) for s in accs],
        out_shape=[jax.ShapeDtypeStruct((T, d), dt) for d, dt in outs]
        + [jax.ShapeDtypeStruct(s, F32) for s in accs],
        compiler_params=_cp("arbitrary" if n_a else "parallel"),
    )(*[r[0] for r in rows], *consts)


def _rms_stats(x):
    r = lax.rsqrt(jnp.mean(x * x, axis=-1, keepdims=True) + EPS)
    return r, x * r


def _rms_bwd(x, g, dy):
    r, xn = _rms_stats(x)
    dyg = dy * g
    dx = r * (dyg - xn * jnp.mean(dyg * xn, axis=-1, keepdims=True))
    return dx, jnp.sum(dy * xn, axis=0, keepdims=True)


def rms_fwd(x, g, name):
    return rowwise(lambda xv, gv: (_rms_stats(xv)[1] * gv,), [x], [g], [(x.shape[1], BF16)], [], name)[0]


def rms_bwd(x, g, dy, name, resid=None, dx_dtype=F32):
    def fn(*v):
        if resid is None:
            xv, dyv, gv = v
            dx, dg = _rms_bwd(xv, gv, dyv)
        else:
            xv, dyv, rv, gv = v
            dx, dg = _rms_bwd(xv, gv, dyv)
            dx = dx + rv
        return dx, dg
    rows = [x, dy] + ([] if resid is None else [resid])
    return rowwise(fn, rows, [g], [(x.shape[1], dx_dtype)], [(1, x.shape[1])], name)


def mm_rms_bwd(pairs, x, g, name, resid=None, dx_dtype=F32, comm=None):
    def epi(accs, ex):
        dx, dg = _rms_bwd(ex[0], ex[-1], accs[0])
        return (dx if resid is None else dx + ex[1]), dg
    extras = [x] + ([] if resid is None else [resid]) + [g]
    return mm([pairs], [dx_dtype], name, tm=min(256, x.shape[0]), tn=x.shape[1], epi=epi, extras=extras, comm=comm, n_sum=1)


def mm_resid(a, b, x, g, wgt, name, comm=None, target=None):
    def epi(accs, ex):
        y = ex[0] + wgt * _rms_stats(accs[0])[1] * ex[1]
        if target is None:
            return accs[0], y
        d = y - ex[2]
        return accs[0], d / D, jnp.sum(d * d, axis=0, keepdims=True)
    return mm([[(a, b, "nn")]], [F32, F32], name, tm=min(512, a.shape[0]), tn=b.shape[1], epi=epi,
              extras=[x, g] + ([] if target is None else [target]), sub=2, comm=comm, n_sum=0 if target is None else 1)


def resid_bwd(h, g, dy, wgt, name):
    def fn(hv, dyv, gv):
        dx, dg = _rms_bwd(hv, gv, dyv)
        return wgt * dx, wgt * dg
    return rowwise(fn, [h, dy], [g], [(h.shape[1], BF16)], [(1, h.shape[1])], name)


def _silu_parts(g):
    s = _sigmoid(g)
    return g * s, s * (1.0 + g * (1.0 - s))


def gated_norm_fwd(y, z, g, name):
    W = SSD_INNER // SSD_G

    def fn(yv, zv, gv):
        yg = yv * _silu_parts(zv)[0]
        return (jnp.concatenate([_rms_stats(yg[:, i * W:(i + 1) * W])[1] for i in range(SSD_G)], axis=1) * gv,)
    return rowwise(fn, [y, z], [g], [(SSD_INNER, BF16)], [], name)[0]


def gated_norm_bwd(y, z, dyn, g, name):
    W = SSD_INNER // SSD_G

    def fn(yv, zv, dv, gv):
        sil, dsil = _silu_parts(zv)
        yg = yv * sil
        parts = [_rms_bwd(yg[:, i * W:(i + 1) * W], gv[:, i * W:(i + 1) * W], dv[:, i * W:(i + 1) * W]) for i in range(SSD_G)]
        dyg = jnp.concatenate([p[0] for p in parts], axis=1)
        dg = jnp.concatenate([p[1] for p in parts], axis=1)
        return dyg * sil, dyg * yv * dsil, dg
    return rowwise(fn, [y, z, dyn], [g], [(SSD_INNER, BF16), (SSD_INNER, BF16)], [(1, SSD_INNER)], name)


def merge_fwd(gl, ys, ym, gb, name):
    def fn(glv, ysv, ymv, gbv):
        gt = _sigmoid(glv + gbv)
        return (gt[:, :D] * ysv + gt[:, D:] * ymv,)
    return rowwise(fn, [gl, ys, ym], [gb], [(D, BF16)], [], name)[0]


def merge_bwd(gl, ys, ym, dm, gb, name):
    def fn(glv, ysv, ymv, dmv, gbv):
        gt = _sigmoid(glv + gbv)
        gs, gm = gt[:, :D], gt[:, D:]
        dgl = jnp.concatenate([dmv * ysv * gs * (1.0 - gs), dmv * ymv * gm * (1.0 - gm)], axis=1)
        return dmv * gs, dmv * gm, dgl, jnp.sum(dgl, axis=0, keepdims=True)
    return rowwise(fn, [gl, ys, ym, dm], [gb], [(D, BF16), (D, BF16), (2 * D, BF16)], [(1, 2 * D)], name)


def loss_head(y, tgt, name):
    def fn(yv, tv):
        d = yv - tv
        part = 0.5 * jnp.sum(jnp.sum(d * d, axis=1, keepdims=True), axis=0, keepdims=True) / D
        return d / D, jnp.broadcast_to(part, (1, 128))
    return rowwise(fn, [y, tgt], [], [(D, F32)], [(1, 128)], name)


def _adamw_math(wv, gv, mv, vv):
    mn = B1 * mv + (1.0 - B1) * gv
    vn = B2 * vv + (1.0 - B2) * (gv * gv)
    mh = mn / (1.0 - B1 ** STEP)
    vh = vn / (1.0 - B2 ** STEP)
    return -LR * (mh / (jnp.sqrt(vh) + AEPS) + WD * wv), mn, vn


def adamw(w, g, m, v, name):
    R, C = w.shape
    tb = _tile(R, (256, 128, 64, 32, 16, 8))
    return rowwise(_adamw_math, [w, g, m, v], [], [(C, F32)] * 3, [], name, tb=tb)


def adamw_small(packed, ws, ms, vs):
    k = len(ws)
    sizes = [x.shape[1] for x in ws]

    def body(*refs):
        p_ref, w_refs, m_refs, v_refs = refs[0], refs[1:1 + k], refs[1 + k:1 + 2 * k], refs[1 + 2 * k:1 + 3 * k]
        outs = refs[1 + 3 * k:]
        r0 = 0
        for i, n in enumerate(sizes):
            nr = -(-n // 128)
            g = jnp.concatenate([p_ref[r0 + r:r0 + r + 1, :] for r in range(nr)], axis=1)[:, :n]
            r0 += nr
            outs[i][...] = g
            outs[k + i][...], outs[2 * k + i][...], outs[3 * k + i][...] = _adamw_math(w_refs[i][...], g, m_refs[i][...], v_refs[i][...])

    res = pl.pallas_call(body, name="adamw_small",
                         out_shape=[jax.ShapeDtypeStruct((1, n), F32) for _ in range(4) for n in sizes])(packed, *ws, *ms, *vs)
    return [res[j * k:(j + 1) * k] for j in range(4)]


def adamw_from_slots(recv, piece, w, m, v, name):
    K, n = w.shape
    ns = recv.shape[0]
    assert recv.shape[2] == n and recv.shape[1] % K == 0
    tb = _tile(K, (256, 176, 128, 64, 32, 16, 8)) if K % 8 == 0 else K
    r_spec = pl.BlockSpec((ns, tb, n), lambda i: (0, piece * (K // tb) + i, 0))
    w_spec = pl.BlockSpec((tb, n), lambda i: (i, 0))

    def body(r_ref, w_ref, m_ref, v_ref, g_ref, d_ref, mo_ref, vo_ref):
        g = r_ref[0].astype(F32)
        for s in range(1, ns):
            g = g + r_ref[s].astype(F32)
        g_ref[...] = g
        d_ref[...], mo_ref[...], vo_ref[...] = _adamw_math(w_ref[...], g, m_ref[...], v_ref[...])

    return pl.pallas_call(
        body, grid=(K // tb,), name=name, in_specs=[r_spec, w_spec, w_spec, w_spec], out_specs=[w_spec] * 4,
        out_shape=[jax.ShapeDtypeStruct((K, n), F32)] * 4, compiler_params=_cp("parallel"),
    )(recv, w, m, v)


def _me():
    return lax.axis_index("x"), lax.axis_index("y"), lax.axis_index("c")


def _dev_index():
    x, y, c = _me()
    return 4 * x + 2 * y + c


HBM_SPEC = pl.BlockSpec(memory_space=pl.ANY)


class GatherComm:
    def __init__(self, shards):
        self.inputs = [s for s, _ in shards]
        self.rows = [list(r) for _, r in shards]
        n = len(shards)
        self.out_shapes = [jax.ShapeDtypeStruct((N_DEV, r, s.shape[1]), s.dtype) for s, rows in shards for r in rows]
        self.sems = [pltpu.SemaphoreType.DMA((7 * n,)), pltpu.SemaphoreType.DMA((7 * n,)), pltpu.SemaphoreType.DMA((n,))]

    def _plan(self, x_refs, out_refs, sems):
        send_sems, recv_sems, local_sems = sems
        x, y, c = _me()
        me, sibling = (x, y, c), (x, y, 1 - c)
        chips = [(1 - x, y), (x, 1 - y), (1 - x, 1 - y)]
        index = lambda px, py, pc: 4 * px + 2 * py + pc
        mine, first, passed, whole = [], [], [], []
        pos = 0
        for i, rows in enumerate(self.rows):
            kw = lambda k: dict(send_sem=send_sems.at[7 * i + k], recv_sem=recv_sems.at[7 * i + k], device_id_type=MESH)
            r0 = 0
            fwd = [[] for _ in chips]
            for j, nr in enumerate(rows):
                out, src = out_refs[pos + j], x_refs[i].at[pl.ds(r0, nr)]
                mine.append(pltpu.make_async_copy(src, out.at[index(*me)], local_sems.at[i]))
                first.append(pltpu.make_async_remote_copy(src_ref=src, dst_ref=out.at[index(*me)], device_id=sibling, **kw(0)))
                for jj, chip in enumerate(chips):
                    first.append(pltpu.make_async_remote_copy(src_ref=src, dst_ref=out.at[index(*me)], device_id=(*chip, c),
                                                              **kw(1 + jj)))
                    blk = out.at[index(*chip, c)]
                    fwd[jj].append(pltpu.make_async_remote_copy(src_ref=blk, dst_ref=blk, device_id=sibling, **kw(4 + jj)))
                r0 += nr
            passed.append(fwd)
            whole.append([pltpu.make_async_remote_copy(src_ref=x_refs[i], dst_ref=x_refs[i], device_id=sibling, **kw(k))
                          for k in range(7)])
            pos += len(rows)
        return mine, first, passed, whole

    def start(self, x_refs, out_refs, sems):
        mine, first, _, _ = self._plan(x_refs, out_refs, sems)
        for cp in mine + first:
            cp.start()

    def finish(self, x_refs, out_refs, sems):
        _, _, passed, whole = self._plan(x_refs, out_refs, sems)
        local_sems = sems[2]
        for i, fwd in enumerate(passed):
            for jj in range(3):
                whole[i][1 + jj].wait_recv()
                for cp in fwd[jj]:
                    cp.start()
        for i in range(len(passed)):
            whole[i][0].wait_recv()
            for jj in range(3):
                whole[i][4 + jj].wait_recv()
        for i in range(len(passed)):
            for k in range(7):
                whole[i][k].wait_send()
            pltpu.make_async_copy(x_refs[i], x_refs[i], local_sems.at[i]).wait()


def run_comm(comm, name):
    n_in, n_out = len(comm.inputs), len(comm.out_shapes)

    def body(*refs):
        ins, outs, sems = refs[:n_in], refs[n_in:n_in + n_out], refs[n_in + n_out:]
        comm.start(ins, outs, sems)
        comm.finish(ins, outs, sems)

    return pl.pallas_call(body, name=name, out_shape=comm.out_shapes, in_specs=[HBM_SPEC] * n_in,
                          out_specs=[HBM_SPEC] * n_out, scratch_shapes=comm.sems)(*comm.inputs)


def _attach(comm, body, n_in, n_out, first, last):
    if comm is None:
        return body
    ci, co, cs = len(comm.inputs), len(comm.out_shapes), len(comm.sems)

    def wrapped(*refs):
        h_in, c_in = refs[:n_in], refs[n_in:n_in + ci]
        h_out, c_out = refs[n_in + ci:n_in + ci + n_out], refs[n_in + ci + n_out:n_in + ci + n_out + co]
        rest = refs[n_in + ci + n_out + co:]
        h_scr, c_sem = rest[:len(rest) - cs], rest[len(rest) - cs:]

        @pl.when(first())
        def _():
            comm.start(c_in, c_out, c_sem)

        body(*h_in, *h_out, *h_scr)

        @pl.when(last())
        def _():
            comm.finish(c_in, c_out, c_sem)

    return wrapped


def _grid_ends(grid):
    first = lambda: functools.reduce(lambda a, b: a & b, [pl.program_id(i) == 0 for i in range(len(grid))])
    last = lambda: functools.reduce(lambda a, b: a & b, [pl.program_id(i) == g - 1 for i, g in enumerate(grid)])
    return first, last


def _call_with_comm(body, grid, name, in_specs, args, out_specs, out_shape, comm, scratch=(), sem=None):
    sem = sem or ("parallel",) * len(grid)
    scratch = list(scratch)
    if comm is not None:
        body = _attach(comm, body, len(args), len(out_shape), *_grid_ends(grid))
        in_specs, args = in_specs + [HBM_SPEC] * len(comm.inputs), args + comm.inputs
        out_specs, out_shape = out_specs + [HBM_SPEC] * len(comm.out_shapes), out_shape + comm.out_shapes
        scratch, sem = scratch + comm.sems, ("arbitrary",) * len(grid)
    return pl.pallas_call(body, grid=grid, name=name, in_specs=in_specs, out_specs=out_specs, out_shape=out_shape,
                          scratch_shapes=scratch, compiler_params=_cp(*sem))(*args)


class ScatterComm:
    def __init__(self, groups):
        self.sizes = [len(g) for g in groups]
        self.rows = [[pc.shape[1] for pc in g] for g in groups]
        ng = len(groups)
        self.inputs = [pc for g in groups for pc in g]
        self.out_shapes = [jax.ShapeDtypeStruct((N_DEV, sum(self.rows[gi]), g[0].shape[2]), g[0].dtype) for gi, g in enumerate(groups)]
        self.sems = [pltpu.SemaphoreType.DMA((7 * ng,)), pltpu.SemaphoreType.DMA((7 * ng,)), pltpu.SemaphoreType.DMA((ng,))]

    def _peers(self):
        x, y, c = _me()
        out = []
        for k in range(1, N_DEV):
            px = 1 - x if k & 4 else x
            py = 1 - y if k & 2 else y
            pc = 1 - c if k & 1 else c
            out.append((k, 4 * px + 2 * py + pc, dict(device_id=(px, py, pc), device_id_type=MESH)))
        return 4 * x + 2 * y + c, out

    def start(self, ins, outs, sems):
        send_sems, recv_sems, local_sems = sems
        me, peers = self._peers()
        pos = 0
        for gi, size in enumerate(self.sizes):
            for i, pc in enumerate(ins[pos:pos + size]):
                dst = outs[gi].at[me, pl.ds(sum(self.rows[gi][:i]), self.rows[gi][i])]
                pltpu.make_async_copy(pc.at[me], dst, local_sems.at[gi]).start()
                for k, peer, kw in peers:
                    pltpu.make_async_remote_copy(src_ref=pc.at[peer], dst_ref=dst, send_sem=send_sems.at[7 * gi + k - 1],
                                                 recv_sem=recv_sems.at[7 * gi + k - 1], **kw).start()
            pos += size

    def finish(self, ins, outs, sems):
        send_sems, recv_sems, local_sems = sems
        me, peers = self._peers()
        whole = [pltpu.make_async_remote_copy(src_ref=outs[gi].at[peer], dst_ref=outs[gi].at[peer],
                                              send_sem=send_sems.at[7 * gi + k - 1], recv_sem=recv_sems.at[7 * gi + k - 1], **kw)
                 for gi in range(len(self.sizes)) for k, peer, kw in peers]
        for cp in whole:
            cp.wait_recv()
        for cp in whole:
            cp.wait_send()
        for gi in range(len(self.sizes)):
            pltpu.make_async_copy(outs[gi].at[me], outs[gi].at[me], local_sems.at[gi]).wait()


def sum_slots(recv, name, tr):
    n, R, C = recv.shape

    def body(r_ref, o_ref):
        acc = r_ref[0].astype(F32)
        for s in range(1, n):
            acc = acc + r_ref[s].astype(F32)
        o_ref[...] = acc

    return pl.pallas_call(
        body, grid=(R // tr,), name=name,
        in_specs=[pl.BlockSpec((n, tr, C), lambda i: (0, i, 0))], out_specs=pl.BlockSpec((tr, C), lambda i: (i, 0)),
        out_shape=jax.ShapeDtypeStruct((R, C), F32), compiler_params=_cp("parallel"),
    )(recv)


PACK_W, FLAT_W = 1024, 128
MAIN = [
    ("ffn1_w_gate", "col"), ("ffn1_w_up", "col"), ("ffn1_w_down", "row"),
    ("ffn2_w_gate", "col"), ("ffn2_w_up", "col"), ("ffn2_w_down", "row"),
    ("w_ssd_proj", "row"), ("w_mla_proj", "row"), ("w_out", "row"),
    ("w_xq", "row"), ("w_xk", "row"), ("w_xv", "row"), ("w_xo", "row"),
    ("w_uk", "col"), ("w_uv", "col"),
]
FLAT = [("w_in", "col"), ("w_uq", "col")]
BIG = MAIN + FLAT
SMALL = ["ffn1_pre_g", "ffn1_post_g", "mix_pre_g", "conv_b", "dt_bias", "a_log", "d_skip", "ssd_norm_g", "q_norm_g",
         "kv_norm_g", "gate_bias", "mix_post_g", "xa_pre_g", "mem_norm_g", "xa_post_g", "ffn2_pre_g", "ffn2_post_g"]
WEIGHTS = ['ffn1_pre_g', 'ffn1_w_gate', 'ffn1_w_up', 'ffn1_w_down', 'ffn1_post_g', 'mix_pre_g', 'w_in', 'conv_w', 'conv_b',
           'dt_bias', 'a_log', 'd_skip', 'ssd_norm_g', 'w_ssd_proj', 'q_norm_g', 'w_uq', 'kv_norm_g', 'w_uk', 'w_uv',
           'w_mla_proj', 'gate_bias', 'w_out', 'mix_post_g', 'xa_pre_g', 'mem_norm_g', 'w_xq', 'w_xk', 'w_xv', 'w_xo',
           'xa_post_g', 'ffn2_pre_g', 'ffn2_w_gate', 'ffn2_w_up', 'ffn2_w_down', 'ffn2_post_g']


def _pack_rows(w, kind, width):
    m = w[0].T if kind == "col" else w[0]
    return m.reshape(-1, width)


KIND = dict(BIG)
GATHER_PLAN = {
    "first": (["ffn1_w_gate", "ffn1_w_up"], []),
    "ffn1_gate_up": (["ffn1_w_down"], ["w_in@0"]),
    "ffn1_down": ([], ["w_in@1"]),
    "ssd_fwd": (["w_ssd_proj", "w_mla_proj", "w_out", "w_uk", "w_uv"], ["w_uq"]),
    "attn_fwd": (["w_xq", "w_xk", "w_xv", "w_xo", "ffn2_w_gate", "ffn2_w_up", "ffn2_w_down"], []),
}
CONV_RIDES_WITH = "w_in@1"
SCATTER_PLAN = {
    "attn_bwd": [["ffn2_w_gate", "ffn2_w_up", "ffn2_w_down"], ["w_xq", "w_xk", "w_xv", "w_xo"]],
    "ssd_bwd": [["w_ssd_proj", "w_mla_proj", "w_out"], ["w_uk", "w_uv"], ["w_uq"]],
    "in_bwd": [["w_in#0"]],
    "ffn1:down_bwd": [["w_in#1"]],
    "ffn1:dwd": [["w_in#2"]],
    "ffn1:dwg": [["ffn1_w_down#0"]],
    "ffn1:dwu": [["ffn1_w_down#1"]],
    "ffn1:gate_up_bwd": [["ffn1_w_gate"]],
    "last": [["ffn1_w_up"]],
}
PARTS = {"w_in@0": ("w_in", 0, 2656), "w_in@1": ("w_in", 2656, 5296),
         "w_in#0": ("w_in", 0, 2656), "w_in#1": ("w_in", 2656, 3984), "w_in#2": ("w_in", 3984, 5296),
         "ffn1_w_down#0": ("ffn1_w_down", 0, 176), "ffn1_w_down#1": ("ffn1_w_down", 176, 352)}


def _parts_of(base, mark):
    return sorted(pn for pn, (b, _, _) in PARTS.items() if b == base and mark in pn)


class Stage:
    def __init__(self, w):
        self.w = w
        self.width = {n: PACK_W if (n, k) in MAIN else FLAT_W for n, k in BIG}
        self.nrows = {n: math.prod(w[n].shape) // self.width[n] for n, _ in BIG}
        self.recv = {}
        self.arrived_parts = {}

    def _rows(self, n):
        return PARTS[n][2] - PARTS[n][1] if n in PARTS else self.nrows[n]

    def _shards(self, tag):
        names_main, names_flat = GATHER_PLAN[tag]

        def pack(n):
            base, r0, r1 = PARTS.get(n, (n, 0, None))
            return _pack_rows(self.w[base], KIND[base], self.width[base])[r0:r1].astype(BF16)
        shards = []
        if names_main:
            pieces = [pack(n) for n in names_main]
            shards.append((jnp.concatenate(pieces, axis=0), [pc.shape[0] for pc in pieces]))
        if names_flat:
            pieces = [pack(n) for n in names_flat]
            if CONV_RIDES_WITH in names_flat:
                pieces.append(_pad_rows(lax.bitcast_convert_type(self.w["conv_w"][0], BF16).reshape(-1, FLAT_W), 16))
            shards.append((jnp.concatenate(pieces, axis=0), [pc.shape[0] for pc in pieces]))
        return shards

    def gather(self, tag):
        return GatherComm(self._shards(tag)) if tag in GATHER_PLAN else None

    def gathered(self, tag, outs, W, p):
        if tag not in GATHER_PLAN:
            return
        names_main, names_flat = GATHER_PLAN[tag]
        outs = list(outs)
        for n in names_main + names_flat:
            rows = outs.pop(0)
            if n in PARTS:
                self.arrived_parts[n] = rows
                base = PARTS[n][0]
                mine = _parts_of(base, "@")
                if not all(pn in self.arrived_parts for pn in mine):
                    continue
                n, rows = base, jnp.concatenate([self.arrived_parts[pn] for pn in mine], axis=1)
            K = self.w[n].shape[1] if KIND[n] == "col" else PACK_W
            W[n] = rows.reshape(-1, K)
        if CONV_RIDES_WITH in names_flat:
            cw = self.w["conv_w"]
            nbits = 2 * math.prod(cw.shape) // FLAT_W
            bits = outs.pop(0)[:, :nbits].reshape((N_DEV,) + cw.shape[1:] + (2,))
            p["conv_w"] = lax.bitcast_convert_type(bits, F32).transpose(1, 0, 2).reshape(cw.shape[1], -1)

    def pieces(self, tag, gw):
        def piece(n):
            if n in PARTS:
                base, r0, r1 = PARTS[n]
                return gw[base].reshape(N_DEV, self.nrows[base], self.width[base])[:, r0:r1]
            return gw[n].reshape(N_DEV, self.nrows[n], self.width[n])
        return [[piece(n) for n in names] for names in SCATTER_PLAN[tag]]

    def scatter(self, tag, gw):
        return ScatterComm(self.pieces(tag, gw)) if tag in SCATTER_PLAN else None

    def scattered(self, tag, outs):
        if tag in SCATTER_PLAN:
            self.recv[tag] = outs


def _pad_rows(a, mult):
    r = (-a.shape[0]) % mult
    return a if r == 0 else jnp.concatenate([a, jnp.zeros((r,) + a.shape[1:], a.dtype)], axis=0)


def _pack_small(vals, loss_row=None, conv_w=None):
    rows = []
    for v in vals:
        f = v.reshape(-1)
        f = jnp.concatenate([f, jnp.zeros(((-f.shape[0]) % 128,), F32)])
        rows.append(f.reshape(-1, 128))
    if conv_w is not None:
        rows.append(conv_w.reshape(-1, 128))
    if loss_row is not None:
        rows.append(loss_row)
    return _pad_rows(jnp.concatenate(rows, axis=0), 8)


def _unpack_small(buf, shapes):
    out, r = [], 0
    for shp in shapes:
        n = math.prod(shp)
        nr = -(-n // 128)
        out.append(buf[r:r + nr].reshape(-1)[:n].reshape(shp))
        r += nr
    return out, r


def _tn(a, b, name, out_dtype=BF16, comm=None):
    M, N = a.shape[1], b.shape[1]
    T = a.shape[0]
    tm = M if M <= 1536 else M // 2
    tk = 1024 if T % 1024 == 0 and T > 1024 else None
    res = mm([[(a, b, "tn")]], [out_dtype], name, tm=tm, tn=N, tk=tk, comm=comm)
    return res[0] if comm is None else (res[0], res[1:])


class NoStage:
    def gather(self, tag):
        return None

    def gathered(self, tag, outs, W, p):
        pass

    def scatter(self, tag, gw):
        return None

    def scattered(self, tag, outs):
        pass


def _ffn_fwd(x, gpre, gpost, W, p, tag, stage, target=None):
    h = rms_fwd(x, gpre, tag + "_pre")

    def swi(accs, ex):
        sil, dsil = _silu_parts(accs[0])
        return sil, accs[1] * dsil, sil * accs[1]
    G, U, A, *arrived = mm([[(h, W[tag + "_w_gate"], "nt")], [(h, W[tag + "_w_up"], "nt")]], [BF16, BF16, BF16], tag + "_gate_up",
                           tn=DFF // 2, epi=swi, comm=stage.gather(tag + "_gate_up"), sub=4 if h.shape[0] % 1024 == 0 else 1)
    stage.gathered(tag + "_gate_up", arrived, W, p)
    H, y, *rest = mm_resid(A, W[tag + "_w_down"], x, gpost, FFN_RES, tag + "_down", comm=stage.gather(tag + "_down"), target=target)
    saved = (x, h, G, U, A, H)
    if target is not None:
        return y, saved, rest[0]
    stage.gathered(tag + "_down", rest, W, p)
    return y, saved


def _ffn_bwd(dy, saved, gpre, gpost, wg_t, wu_t, wd, tag, stage, gw):
    x, h, G, U, A, H = saved
    dH, dgpost = resid_bwd(H, gpost, dy, FFN_RES, tag + "_post_bwd")

    def dswi(accs, ex):
        return accs[0] * ex[1], accs[0] * ex[0]

    def hosted(where, call):
        comm = stage.scatter(tag + ":" + where, gw)
        res = call(comm)
        if comm is None:
            return res
        stage.scattered(tag + ":" + where, res[1])
        return res[0]

    res = hosted("down_bwd", lambda comm: (lambda r: r if comm is None else (r[:2], r[2:]))(
        mm([[(dH, wd, "nt")]], [BF16, BF16], tag + "_down_bwd", tn=DFF // 2, epi=dswi, extras=[G, U], comm=comm,
           sub=4 if dH.shape[0] % 1024 == 0 else 1)))
    dG, dU = res
    gw[tag + "_w_down"] = hosted("dwd", lambda comm: _tn(A, dH, tag + "_dwd", comm=comm))
    gw[tag + "_w_gate"] = hosted("dwg", lambda comm: _tn(dG, h, tag + "_dwg", comm=comm))
    gw[tag + "_w_up"] = hosted("dwu", lambda comm: _tn(dU, h, tag + "_dwu", comm=comm))
    dx, dgpre = hosted("gate_up_bwd", lambda comm: (lambda r: r[:2] if comm is None else (r[:2], r[2:]))(
        mm_rms_bwd([(dG, wg_t, "nn"), (dU, wu_t, "nn")], x, gpre, tag + "_gate_up_bwd", resid=dy, comm=comm)))
    return dx, dgpre, dgpost


def _local_step(x, mem, positions, tgt, W, p, stage=None):
    stage = stage or NoStage()
    nseq = x.shape[0]
    T = nseq * x.shape[1]
    x0 = x.reshape(T, D)
    mem2 = mem.reshape(-1, D)

    x1, ffn1 = _ffn_fwd(x0, p["ffn1_pre_g"], p["ffn1_post_g"], W, p, "ffn1", stage)

    w_in_t = W["w_in"]
    bounds = [0]
    for n in (SSD_INNER, CONV_CH, SSD_H, QR, KVR, ROPE, 2 * D):
        bounds.append(bounds[-1] + n)
    wt_z, wt_xbc, wt_dt, wt_q, wt_kv, wt_kr, wt_gate = [w_in_t[bounds[i]:bounds[i + 1]] for i in range(7)]
    wt_dt, wt_kr = _pad_rows(wt_dt, SLOT), _pad_rows(wt_kr, SLOT)
    wt_dtkr = jnp.concatenate([wt_dt, wt_kr], axis=0)
    hm = rms_fwd(x1, p["mix_pre_g"], "mix_pre")
    z = mm1(hm, wt_z, "nt", BF16, "in_z")
    xbc = mm1(hm, wt_xbc, "nt", BF16, "in_xbc")
    q_c = mm1(hm, wt_q, "nt", F32, "in_q", tn=QR)
    kv_c = mm1(hm, wt_kv, "nt", F32, "in_kv")
    dtkr = mm1(hm, wt_dtkr, "nt", F32, "in_dtkr")
    gl = mm1(hm, wt_gate, "nt", BF16, "in_gate")

    xbc_act = conv_fwd(xbc, p["conv_w"], p["conv_b"], nseq)
    y_ssd_core, prev, *arrived = ssd_fwd(xbc_act, dtkr, p["dt_bias"], p["a_log"], p["d_skip"], nseq, comm=stage.gather("ssd_fwd"))
    stage.gathered("ssd_fwd", arrived, W, p)
    yn = gated_norm_fwd(y_ssd_core, z, p["ssd_norm_g"], "ssd_norm")
    y_ssd = mm1(yn, W["w_ssd_proj"], "nn", BF16, "ssd_proj")

    slot_rows = lambda wt, per: jnp.pad(wt.reshape(MLA_H, per, -1), ((0, 0), (0, SLOT - per), (0, 0))).reshape(MLA_H * SLOT, -1)
    wq_s, wk_s, wv_s = slot_rows(W["w_uq"], QK), slot_rows(W["w_uk"], NOPE), slot_rows(W["w_uv"], VD)
    wo_s = slot_rows(W["w_mla_proj"], VD)
    qn = rms_fwd(q_c, p["q_norm_g"], "q_norm")
    rope_c, rope_s = rope_table(*_rope_inputs(positions))
    rope_args = [("rows", rope_c), ("rows", rope_s)]
    Qc, = mm([[(qn, wq_s, "nt")]], [BF16], "uq", epi=rope_q_epilogue, extras=rope_args, sub=4 if T % 1024 == 0 else 1)
    kvn = rms_fwd(kv_c, p["kv_norm_g"], "kv_norm")
    Kc, = mm([[(kvn, wk_s, "nt")]], [BF16], "uk", epi=rope_k_epilogue, extras=rope_args + [("rows", dtkr)],
             sub=4 if T % 1024 == 0 else 1)
    v_s = mm1(kvn, wv_s, "nt", BF16, "uv")
    o_s, lse, *arrived = attn_slot_fwd(Qc, Kc, v_s, nseq, comm=stage.gather("attn_fwd"))
    stage.gathered("attn_fwd", arrived, W, p)
    y_mla = mm1(o_s, wo_s, "nn", BF16, "mla_proj")

    merged = merge_fwd(gl, y_ssd, y_mla, p["gate_bias"], "merge")
    hmix, x2 = mm_resid(merged, W["w_out"], x1, p["mix_post_g"], 1.0, "mix_out")

    hq = rms_fwd(x2, p["xa_pre_g"], "xa_pre")
    mn = rms_fwd(mem2, p["mem_norm_g"], "mem_norm")
    xq = mm1(hq, W["w_xq"], "nn", BF16, "xq")
    xk = mm1(mn, W["w_xk"], "nn", BF16, "xk")
    xv = mm1(mn, W["w_xv"], "nn", BF16, "xv")
    xo = xattn_fwd(xq, xk, xv, nseq)
    ho, x3 = mm_resid(xo, W["w_xo"], x2, p["xa_post_g"], 1.0, "xo")

    dx4, ffn2, sq_cols = _ffn_fwd(x3, p["ffn2_pre_g"], p["ffn2_post_g"], W, p, "ffn2", stage, target=tgt.reshape(T, D))
    loss_row = (0.5 / D) * jnp.sum(sq_cols.reshape(-1, 128), axis=0, keepdims=True)

    gw, gs = {}, {}
    dx3, gs["ffn2_pre_g"], gs["ffn2_post_g"] = _ffn_bwd(
        dx4, ffn2, p["ffn2_pre_g"], p["ffn2_post_g"], W["ffn2_w_gate"], W["ffn2_w_up"], W["ffn2_w_down"], "ffn2", stage, gw)

    dho, gs["xa_post_g"] = resid_bwd(ho, p["xa_post_g"], dx3, 1.0, "xa_post_bwd")
    dxo = mm1(dho, W["w_xo"], "nt", BF16, "xo_bwd")
    gw["w_xo"] = _tn(xo, dho, "d_w_xo")
    dxq, dxk, dxv = xattn_bwd(xq, xk, xv, dxo, nseq)
    dx2, gs["xa_pre_g"] = mm_rms_bwd([(dxq, W["w_xq"], "nt")], x2, p["xa_pre_g"], "xq_bwd", resid=dx3)
    gw["w_xq"] = _tn(hq, dxq, "d_w_xq")
    dmn = mm([[(dxk, W["w_xk"], "nt"), (dxv, W["w_xv"], "nt")]], [F32], "xkv_bwd")[0]
    gw["w_xk"] = _tn(mn, dxk, "d_w_xk")
    gw["w_xv"] = _tn(mn, dxv, "d_w_xv")
    _, gs["mem_norm_g"] = rms_bwd(mem2, p["mem_norm_g"], dmn, "mem_norm_bwd", dx_dtype=BF16)

    dhmix, gs["mix_post_g"] = resid_bwd(hmix, p["mix_post_g"], dx2, 1.0, "mix_post_bwd")
    dmerged = mm1(dhmix, W["w_out"], "nt", F32, "mix_out_bwd")
    gw["w_out"] = _tn(merged, dhmix, "d_w_out")
    dys, dym, dgl, gs["gate_bias"] = merge_bwd(gl, y_ssd, y_mla, dmerged, p["gate_bias"], "merge_bwd")

    unslot = lambda g, per: g.reshape(MLA_H, SLOT, -1)[:, :per].reshape(MLA_H * per, -1)
    do_s = mm1(dym, wo_s, "nt", BF16, "mla_proj_bwd")
    gw["w_mla_proj"] = unslot(_tn(o_s, dym, "d_w_mla_proj"), VD)
    dQc, dKc, dv_s, *sent = attn_slot_bwd(Qc, Kc, v_s, o_s, lse, do_s, nseq, comm=stage.scatter("attn_bwd", gw))
    stage.scattered("attn_bwd", sent)
    dq_s, dkn_s, dkr = rope_slot_bwd(dQc, dKc, rope_c, rope_s, "rope_bwd")
    dq_c, gs["q_norm_g"] = mm_rms_bwd([(dq_s, wq_s, "nn")], q_c, p["q_norm_g"], "uq_bwd", dx_dtype=BF16)
    gw["w_uq"] = unslot(_tn(dq_s, qn, "d_w_uq"), QK)
    dkv_c, gs["kv_norm_g"] = mm_rms_bwd([(dkn_s, wk_s, "nn"), (dv_s, wv_s, "nn")], kv_c, p["kv_norm_g"], "ukv_bwd", dx_dtype=BF16)
    gw["w_uk"] = unslot(_tn(dkn_s, kvn, "d_w_uk"), NOPE)
    gw["w_uv"] = unslot(_tn(dv_s, kvn, "d_w_uv"), VD)

    dyn = mm1(dys, W["w_ssd_proj"], "nt", F32, "ssd_proj_bwd")
    gw["w_ssd_proj"] = _tn(yn, dys, "d_w_ssd_proj")
    dyc, dz, gs["ssd_norm_g"] = gated_norm_bwd(y_ssd_core, z, dyn, p["ssd_norm_g"], "ssd_norm_bwd")
    dxbc_act, ddtr, gs["dt_bias"], gs["a_log"], gs["d_skip"], *sent = ssd_bwd(
        xbc_act, dtkr, p["dt_bias"], p["a_log"], p["d_skip"], prev, dyc, nseq, comm=stage.scatter("ssd_bwd", gw))
    stage.scattered("ssd_bwd", sent)
    dxbc, gs["conv_w"], gs["conv_b"] = conv_bwd(xbc, p["conv_w"], p["conv_b"], dxbc_act, nseq)

    gw["w_in"] = jnp.concatenate([_tn(dz, hm, "d_w_in_z"), _tn(dxbc, hm, "d_w_in_xbc"), _tn(ddtr, hm, "d_w_in_dt")[:SSD_H],
                                  _tn(dq_c, hm, "d_w_in_q"), _tn(dkv_c, hm, "d_w_in_kv"), _tn(dkr, hm, "d_w_in_kr")[:ROPE],
                                  _tn(dgl, hm, "d_w_in_gate")], axis=0)
    dx1, gs["mix_pre_g"], *sent = mm_rms_bwd(
        [(dz, wt_z, "nn"), (dxbc, wt_xbc, "nn"), (ddtr, wt_dt, "nn"), (dq_c, wt_q, "nn"), (dkv_c, wt_kv, "nn"),
         (dkr, wt_kr, "nn"), (dgl, wt_gate, "nn")], x1, p["mix_pre_g"], "in_bwd", resid=dx2, comm=stage.scatter("in_bwd", gw))
    stage.scattered("in_bwd", sent)

    dx0, gs["ffn1_pre_g"], gs["ffn1_post_g"] = _ffn_bwd(
        dx1, ffn1, p["ffn1_pre_g"], p["ffn1_post_g"], W["ffn1_w_gate"], W["ffn1_w_up"], W["ffn1_w_down"], "ffn1", stage, gw)
    return loss_row, dx0.reshape(x.shape), gw, gs


def kernel(x, mem, positions, ffn1_pre_g, ffn1_w_gate, ffn1_w_up, ffn1_w_down, ffn1_post_g, mix_pre_g, w_in, conv_w, conv_b, dt_bias, a_log, d_skip, ssd_norm_g, w_ssd_proj, q_norm_g, w_uq, kv_norm_g, w_uk, w_uv, w_mla_proj, gate_bias, w_out, mix_post_g, xa_pre_g, mem_norm_g, w_xq, w_xk, w_xv, w_xo, xa_post_g, ffn2_pre_g, ffn2_w_gate, ffn2_w_up, ffn2_w_down, ffn2_post_g, loss_target, m_ffn1_pre_g, m_ffn1_w_gate, m_ffn1_w_up, m_ffn1_w_down, m_ffn1_post_g, m_mix_pre_g, m_w_in, m_conv_w, m_conv_b, m_dt_bias, m_a_log, m_d_skip, m_ssd_norm_g, m_w_ssd_proj, m_q_norm_g, m_w_uq, m_kv_norm_g, m_w_uk, m_w_uv, m_w_mla_proj, m_gate_bias, m_w_out, m_mix_post_g, m_xa_pre_g, m_mem_norm_g, m_w_xq, m_w_xk, m_w_xv, m_w_xo, m_xa_post_g, m_ffn2_pre_g, m_ffn2_w_gate, m_ffn2_w_up, m_ffn2_w_down, m_ffn2_post_g, v_ffn1_pre_g, v_ffn1_w_gate, v_ffn1_w_up, v_ffn1_w_down, v_ffn1_post_g, v_mix_pre_g, v_w_in, v_conv_w, v_conv_b, v_dt_bias, v_a_log, v_d_skip, v_ssd_norm_g, v_w_ssd_proj, v_q_norm_g, v_w_uq, v_kv_norm_g, v_w_uk, v_w_uv, v_w_mla_proj, v_gate_bias, v_w_out, v_mix_post_g, v_xa_pre_g, v_mem_norm_g, v_w_xq, v_w_xk, v_w_xv, v_w_xo, v_xa_post_g, v_ffn2_pre_g, v_ffn2_w_gate, v_ffn2_w_up, v_ffn2_w_down, v_ffn2_post_g):
    a = dict(locals())
    w = {n: a[n] for n in WEIGHTS}
    m = {n: a["m_" + n] for n in WEIGHTS}
    v = {n: a["v_" + n] for n in WEIGHTS}

    stage = Stage(w)
    W, p = {}, {n: w[n] for n in SMALL}
    stage.gathered("first", run_comm(stage.gather("first"), "allgather_first"), W, p)

    loss_row, grad_x, gw, gs = _local_step(x, mem, positions, loss_target, W, p, stage)

    sm = _pack_small([gs[n] for n in SMALL], loss_row=loss_row, conv_w=gs["conv_w"])
    *recv_last, srecv = run_comm(ScatterComm(stage.pieces("last", gw) + [[jnp.broadcast_to(sm[None], (N_DEV,) + sm.shape)]]),
                                 "exchange_last")
    stage.scattered("last", recv_last)
    s_rows = sum_slots(srecv, "sum_small", tr=sm.shape[0])
    grads, delta, new_m, new_v = {}, {}, {}, {}

    def finish(n, buf, piece):
        col = KIND[n] == "col"
        turn = (lambda t: t.T) if col else (lambda t: t)
        K = w[n].shape[1]
        if col and buf.shape[2] != K:
            buf = buf.reshape(buf.shape[0], -1, K)
        res = adamw_from_slots(buf, piece, turn(w[n][0]), turn(m[n][0]), turn(v[n][0]), "adamw_" + n)
        grads[n], delta[n], new_m[n], new_v[n] = [turn(r)[None] for r in res]

    parts = {}
    for tag, groups in SCATTER_PLAN.items():
        for names, buf in zip(groups, stage.recv[tag]):
            for piece, n in enumerate(names):
                if n in PARTS:
                    parts[n] = sum_slots(buf, "sum_" + n.replace("#", "_"), tr=buf.shape[1])
                else:
                    finish(n, buf, piece)
    for base in sorted({PARTS[pn][0] for pn in parts}):
        rows = jnp.concatenate([parts[pn] for pn in _parts_of(base, "#")], axis=0)
        finish(base, rows[None], 0)
    conv_w_full = p["conv_w"]
    small = adamw_small(s_rows, [w[n] for n in SMALL], [m[n] for n in SMALL], [v[n] for n in SMALL])
    for t, vals in zip((grads, delta, new_m, new_v), small):
        t.update(zip(SMALL, vals))
    r1 = sum(-(-w[n].shape[1] // 128) for n in SMALL)
    ncw = math.prod(conv_w_full.shape) // 128
    cw_grad_full = s_rows[r1:r1 + ncw].reshape(conv_w_full.shape)
    wsh = conv_w.shape[2]
    grads["conv_w"] = lax.dynamic_slice_in_dim(cw_grad_full, _dev_index() * wsh, wsh, axis=1)[None]
    loss = jnp.sum(s_rows[r1 + ncw])
    d_, m_, v_ = adamw(conv_w[0], grads["conv_w"][0], m["conv_w"][0], v["conv_w"][0], "adamw_conv_w")
    delta["conv_w"], new_m["conv_w"], new_v["conv_w"] = d_[None], m_[None], v_[None]
    return (loss, grad_x, *[grads[n] for n in WEIGHTS], *[delta[n] for n in WEIGHTS],
            *[new_m[n] for n in WEIGHTS], *[new_v[n] for n in WEIGHTS])
```

```python
import functools
import math

import jax
import jax.numpy as jnp
from jax import lax
from jax.experimental import pallas as pl
from jax.experimental.pallas import tpu as pltpu

F32, BF16 = jnp.float32, jnp.bfloat16
HI = lax.Precision.HIGHEST
MESH = pl.DeviceIdType.MESH
N_DEV = 8

D = 1024
DFF = 2816
SSD_H, SSD_P, SSD_G, SSD_N, SSD_L = 16, 64, 2, 128, 128
SSD_INNER = SSD_H * SSD_P
CONV_K, CONV_CH = 4, 1536
MLA_H, QR, KVR, NOPE, ROPE, VD = 16, 384, 256, 64, 32, 64
QK = NOPE + ROPE
ROPE_THETA = 10000.0
XA_H, XA_D = 4, 256
EPS = 1e-6
FFN_RES = 0.5
LR, B1, B2, AEPS, WD, STEP = 0.001, 0.9, 0.999, 1e-08, 0.01, 10

VMEM_LIMIT = 56 * 2**20


def _cp(*sem):
    return pltpu.CompilerParams(dimension_semantics=sem, vmem_limit_bytes=VMEM_LIMIT)


def _sigmoid(x):
    return 1.0 / (1.0 + jnp.exp(-x))


def _softplus(x):
    return jnp.where(x > 20.0, x, jnp.log(1.0 + jnp.exp(jnp.minimum(x, 20.0))))


def _dot(a, b, dims="nn"):
    ca = 0 if dims[0] == "t" else 1
    cb = 1 if dims[1] == "t" else 0
    return lax.dot_general(a.astype(BF16), b.astype(BF16), (((ca,), (cb,)), ((), ())), preferred_element_type=F32)


def _dot_sel(a, b, dims="nn", split="a", terms=3):
    r = (a if split == "a" else b).astype(F32)
    out = None
    for t in range(terms):
        piece = r.astype(BF16)
        if t + 1 < terms:
            r = r - piece.astype(F32)
        d = _dot(piece, b, dims) if split == "a" else _dot(a, piece, dims)
        out = d if out is None else out + d
    return out


def _ssd_common(dtr, dtb, alog):
    L = dtr.shape[0]
    dt = _softplus(dtr + dtb)
    a = -jnp.exp(alog)
    adt = dt * a
    r = lax.broadcasted_iota(jnp.int32, (L, L), 0)
    c = lax.broadcasted_iota(jnp.int32, (L, L), 1)
    lower = r >= c
    tri = lower.astype(F32)
    cs = _dot_sel(tri, adt, "nn", split="b")
    cs_t = _dot_sel(adt, tri, "tt")
    return dt, a, cs, cs_t, lower


def _head_expand():
    hh = lax.broadcasted_iota(jnp.int32, (SSD_H, SSD_INNER), 0)
    jj = lax.broadcasted_iota(jnp.int32, (SSD_H, SSD_INNER), 1)
    return ((jj >= hh * SSD_P) & (jj < hh * SSD_P + SSD_P)).astype(F32)


def _head_reduce():
    hh = lax.broadcasted_iota(jnp.int32, (SSD_INNER, SSD_H), 1)
    jj = lax.broadcasted_iota(jnp.int32, (SSD_INNER, SSD_H), 0)
    return ((jj >= hh * SSD_P) & (jj < hh * SSD_P + SSD_P)).astype(F32)


def ssd_fwd(xbc, dtr, dtb, alog, dsk, nseq, comm=None):
    T = xbc.shape[0]
    S = T // nseq
    C = S // SSD_L
    L = SSD_L
    NP = SSD_H // 2

    def body(x_ref, b_ref, c_ref, dtr_ref, dtb_ref, alog_ref, dsk_ref, y_ref, prev_ref, st_ref):
        ci = pl.program_id(1)

        @pl.when(ci == 0)
        def _():
            st_ref[...] = jnp.zeros_like(st_ref)

        dt, a, cs, cs_t, lower = _ssd_common(dtr_ref[:, 0:SSD_H], dtb_ref[...], alog_ref[...])
        E = _head_expand()
        X = x_ref[...].astype(F32)
        dt_e = _dot_sel(dt, E)
        cs_e = _dot_sel(cs, E)
        csl_e = cs_e[L - 1:L, :]
        Xd = X * dt_e
        Xf = Xd * jnp.exp(csl_e - cs_e)
        e_e = jnp.exp(cs_e)
        skip = _dot_sel(dsk_ref[...], E) * X
        lane = lax.broadcasted_iota(jnp.int32, (1, 2 * SSD_P), 1)
        rowp = lax.broadcasted_iota(jnp.int32, (2 * SSD_P, 1), 0)
        for g in range(SSD_G):
            Bg = b_ref[:, g * SSD_N:(g + 1) * SSD_N]
            Cg = c_ref[:, g * SSD_N:(g + 1) * SSD_N]
            cb = _dot(Cg, Bg, "nt")
            for pp in range(NP // SSD_G):
                p = g * (NP // SSD_G) + pp
                sl = slice(p * 2 * SSD_P, (p + 1) * 2 * SSD_P)
                Xd_p = Xd[:, sl]
                yd = jnp.zeros((L, 2 * SSD_P), F32)
                for q in range(2):
                    h = 2 * p + q
                    m = jnp.where(lower, jnp.exp(jnp.minimum(cs[:, h:h + 1] - cs_t[h:h + 1, :], 0.0)), 0.0)
                    mask = (lane >= q * SSD_P) & (lane < (q + 1) * SSD_P)
                    yd = yd + _dot(cb * m, jnp.where(mask, Xd_p, 0.0))
                S0 = st_ref[p]
                prev_ref[0, 0, p] = S0
                z = _dot(Cg, S0, "nt")
                y_ref[:, sl] = (skip[:, sl] + yd + z * e_e[:, sl]).astype(y_ref.dtype)
                h0 = 2 * p
                dec = jnp.where(rowp < SSD_P, jnp.exp(cs[L - 1:L, h0:h0 + 1]), jnp.exp(cs[L - 1:L, h0 + 1:h0 + 2]))
                st_ref[p] = S0 * dec + _dot(Xf[:, sl], Bg, "tn")

    row = lambda b, c: (b * C + c, 0)
    small = pl.BlockSpec((1, SSD_H), lambda b, c: (0, 0))
    return _call_with_comm(
        body, (nseq, C), "ssd_fwd",
        [pl.BlockSpec((L, SSD_INNER), row),
         pl.BlockSpec((L, SSD_G * SSD_N), lambda b, c: (b * C + c, SSD_INNER // (SSD_G * SSD_N))),
         pl.BlockSpec((L, SSD_G * SSD_N), lambda b, c: (b * C + c, SSD_INNER // (SSD_G * SSD_N) + 1)),
         pl.BlockSpec((L, 128), row), small, small, small],
        [xbc, xbc, xbc, dtr, dtb, alog, dsk],
        [pl.BlockSpec((L, SSD_INNER), row), pl.BlockSpec((1, 1, NP, 2 * SSD_P, SSD_N), lambda b, c: (b, c, 0, 0, 0))],
        [jax.ShapeDtypeStruct((T, SSD_INNER), BF16), jax.ShapeDtypeStruct((nseq, C, NP, 2 * SSD_P, SSD_N), F32)],
        comm, scratch=[pltpu.VMEM((NP, 2 * SSD_P, SSD_N), F32)], sem=("parallel", "arbitrary"))


def ssd_bwd(xbc, dtr, dtb, alog, dsk, prev, dy, nseq, comm=None):
    T = xbc.shape[0]
    S = T // nseq
    C = S // SSD_L
    L = SSD_L
    NP = SSD_H // 2

    def body(x_ref, b_ref, c_ref, dtr_ref, dtb_ref, alog_ref, dsk_ref, prev_ref, dy_ref,
             dxbc_ref, ddtr_ref, ddtb_ref, dalog_ref, ddsk_ref, ds_ref, stg_ref):
        bi = pl.program_id(0)
        ci = pl.program_id(1)

        @pl.when(ci == 0)
        def _():
            ds_ref[...] = jnp.zeros_like(ds_ref)

        @pl.when((ci == 0) & (bi == 0))
        def _():
            ddtb_ref[...] = jnp.zeros_like(ddtb_ref)
            dalog_ref[...] = jnp.zeros_like(dalog_ref)
            ddsk_ref[...] = jnp.zeros_like(ddsk_ref)

        dtr = dtr_ref[:, 0:SSD_H]
        dtb = dtb_ref[...]
        dt, a, cs, cs_t, lower = _ssd_common(dtr, dtb, alog_ref[...])
        upper = lax.broadcasted_iota(jnp.int32, (L, L), 1) >= lax.broadcasted_iota(jnp.int32, (L, L), 0)
        E = _head_expand()
        ET = _head_reduce()
        X = x_ref[...].astype(F32)
        dY = dy_ref[...].astype(F32)
        dt_e = _dot_sel(dt, E)
        cs_e = _dot_sel(cs, E)
        csl_e = cs_e[L - 1:L, :]
        f_e = jnp.exp(csl_e - cs_e)
        e_e = jnp.exp(cs_e)
        dsk_e = _dot_sel(dsk_ref[...], E)
        Xd = X * dt_e
        Xf = Xd * f_e
        lane = lax.broadcasted_iota(jnp.int32, (1, 2 * SSD_P), 1)
        rowp = lax.broadcasted_iota(jnp.int32, (2 * SSD_P, 1), 0)
        hsel = lax.broadcasted_iota(jnp.int32, (1, SSD_H), 1)
        dcs = jnp.zeros((L, SSD_H), F32)
        dcsl = jnp.zeros((1, SSD_H), F32)
        for g in range(SSD_G):
            Bg = b_ref[:, g * SSD_N:(g + 1) * SSD_N]
            Cg = c_ref[:, g * SSD_N:(g + 1) * SSD_N]
            cb = _dot(Cg, Bg, "nt")
            cbt = _dot(Bg, Cg, "nt")
            dB = jnp.zeros((L, SSD_N), F32)
            dC = jnp.zeros((L, SSD_N), F32)
            for pp in range(NP // SSD_G):
                p = g * (NP // SSD_G) + pp
                sl = slice(p * 2 * SSD_P, (p + 1) * 2 * SSD_P)
                Xd_p = Xd[:, sl]
                dY_p = dY[:, sl]
                dXd_p = jnp.zeros((L, 2 * SSD_P), F32)
                for q in range(2):
                    h = 2 * p + q
                    mask = (lane >= q * SSD_P) & (lane < (q + 1) * SSD_P)
                    col = cs[:, h:h + 1]
                    rw = cs_t[h:h + 1, :]
                    m = jnp.where(lower, jnp.exp(jnp.minimum(col - rw, 0.0)), 0.0)
                    mt = jnp.where(upper, jnp.exp(jnp.minimum(rw - col, 0.0)), 0.0)
                    dYm = jnp.where(mask, dY_p, 0.0)
                    dW = _dot(dYm, Xd_p, "nt")
                    dWt = _dot(Xd_p, dYm, "nt")
                    w = cb * m
                    wt = cbt * mt
                    dC = dC + _dot(dW * m, Bg)
                    dB = dB + _dot(dWt * mt, Cg)
                    dXd_p = dXd_p + jnp.where(mask, _dot(wt, dY_p), 0.0)
                    qcol = jnp.sum(dW * w, axis=1, keepdims=True) - jnp.sum(dWt * wt, axis=1, keepdims=True)
                    dcs = dcs + qcol * (hsel == h).astype(F32)
                S0 = prev_ref[0, 0, p]
                dSn = ds_ref[p]
                dZ = dY_p * e_e[:, sl]
                dC = dC + _dot(dZ, S0)
                h0 = 2 * p
                el0 = jnp.exp(cs[L - 1:L, h0:h0 + 1])
                el1 = jnp.exp(cs[L - 1:L, h0 + 1:h0 + 2])
                dec = jnp.where(rowp < SSD_P, el0, el1)
                ds_ref[p] = dSn * dec + _dot(dZ, Cg, "tn")
                dXf_p = _dot(Bg, dSn, "nt")
                dB = dB + _dot(Xf[:, sl], dSn)
                rs = jnp.sum(dSn * S0, axis=1, keepdims=True)
                s0 = jnp.sum(jnp.where(rowp < SSD_P, rs, 0.0), axis=0, keepdims=True) * el0
                s1 = jnp.sum(jnp.where(rowp >= SSD_P, rs, 0.0), axis=0, keepdims=True) * el1
                dcsl = dcsl + s0 * (hsel == h0).astype(F32) + s1 * (hsel == h0 + 1).astype(F32)
                y_off = _dot(Cg, S0, "nt") * e_e[:, sl]
                t1 = dY_p * y_off - dXf_p * Xf[:, sl]
                r1 = jnp.where(lane < SSD_P, t1, 0.0)
                c0 = jnp.sum(r1, axis=1, keepdims=True)
                c1 = jnp.sum(t1 - r1, axis=1, keepdims=True)
                dcs = dcs + c0 * (hsel == h0).astype(F32) + c1 * (hsel == h0 + 1).astype(F32)
                t2 = dXf_p * Xf[:, sl]
                r2 = jnp.where(lane < SSD_P, t2, 0.0)
                dcsl = dcsl + jnp.sum(r2, keepdims=True) * (hsel == h0).astype(F32) \
                    + jnp.sum(t2 - r2, keepdims=True) * (hsel == h0 + 1).astype(F32)
                stg_ref[:, sl] = dXd_p + dXf_p * f_e[:, sl]
            dxbc_ref[:, SSD_INNER + g * SSD_N:SSD_INNER + (g + 1) * SSD_N] = dB.astype(dxbc_ref.dtype)
            dxbc_ref[:, SSD_INNER + (SSD_G + g) * SSD_N:SSD_INNER + (SSD_G + g + 1) * SSD_N] = dC.astype(dxbc_ref.dtype)
        dXd = stg_ref[...]
        dxbc_ref[:, 0:SSD_INNER] = (dXd * dt_e + dsk_e * dY).astype(dxbc_ref.dtype)
        rowl = lax.broadcasted_iota(jnp.int32, (L, 1), 0)
        dcs = dcs + jnp.where(rowl == L - 1, dcsl, 0.0)
        dalpha = _dot_sel(upper.astype(F32), dcs, split="b")
        ddt = _dot_sel(dXd * X, ET, terms=2) + dalpha * a
        dalog_ref[...] += jnp.sum(dalpha * dt, axis=0, keepdims=True) * a
        ddtr = ddt * _sigmoid(dtr + dtb)
        spread = (lax.broadcasted_iota(jnp.int32, (SSD_H, 128), 0) == lax.broadcasted_iota(jnp.int32, (SSD_H, 128), 1)).astype(F32)
        ddtr_ref[...] = _dot(ddtr, spread).astype(ddtr_ref.dtype)
        ddtb_ref[...] += jnp.sum(ddtr, axis=0, keepdims=True)
        ddsk_ref[...] += jnp.sum(_dot_sel(dY * X, ET, terms=2), axis=0, keepdims=True)

    rowr = lambda b, c: (b * C + (C - 1 - c), 0)
    small = pl.BlockSpec((1, SSD_H), lambda b, c: (0, 0))
    return _call_with_comm(
        body, (nseq, C), "ssd_bwd",
        [pl.BlockSpec((L, SSD_INNER), rowr),
         pl.BlockSpec((L, SSD_G * SSD_N), lambda b, c: (b * C + (C - 1 - c), SSD_INNER // (SSD_G * SSD_N))),
         pl.BlockSpec((L, SSD_G * SSD_N), lambda b, c: (b * C + (C - 1 - c), SSD_INNER // (SSD_G * SSD_N) + 1)),
         pl.BlockSpec((L, 128), rowr), small, small, small,
         pl.BlockSpec((1, 1, NP, 2 * SSD_P, SSD_N), lambda b, c: (b, C - 1 - c, 0, 0, 0)),
         pl.BlockSpec((L, SSD_INNER), rowr)],
        [xbc, xbc, xbc, dtr, dtb, alog, dsk, prev, dy],
        [pl.BlockSpec((L, CONV_CH), rowr), pl.BlockSpec((L, 128), rowr), small, small, small],
        [jax.ShapeDtypeStruct((T, CONV_CH), BF16), jax.ShapeDtypeStruct((T, 128), BF16),
         jax.ShapeDtypeStruct((1, SSD_H), F32), jax.ShapeDtypeStruct((1, SSD_H), F32), jax.ShapeDtypeStruct((1, SSD_H), F32)],
        comm, scratch=[pltpu.VMEM((NP, 2 * SSD_P, SSD_N), F32), pltpu.VMEM((L, SSD_INNER), F32)], sem=("arbitrary", "arbitrary"))


SLOT = 128
ATT_T = 512
ATT_HP = 1
LOG2E = math.log2(math.e)
Q_SCALE = QK ** -0.5 * LOG2E


def _col_to_row(col):
    n = col.shape[0]
    eye = lax.broadcasted_iota(jnp.int32, (n, n), 0) == lax.broadcasted_iota(jnp.int32, (n, n), 1)
    return jnp.sum(jnp.where(eye, col, 0.0), axis=0, keepdims=True)


def attn_slot_fwd(q, k, v, nseq, comm=None):
    T = q.shape[0]
    S = T // nseq
    t = min(ATT_T, S)
    nb = S // t
    cols = [slice(h * SLOT, (h + 1) * SLOT) for h in range(ATT_HP)]

    def body(q_ref, k_ref, v_ref, o_ref, lse_ref):
        causal = lax.broadcasted_iota(jnp.int32, (t, t), 1) <= lax.broadcasted_iota(jnp.int32, (t, t), 0)
        for qi in range(nb):
            rows = slice(qi * t, (qi + 1) * t)
            state = [None] * ATT_HP
            for kj in range(qi + 1):
                keys = slice(kj * t, (kj + 1) * t)
                for h, c in enumerate(cols):
                    s = _dot(q_ref[rows, c], k_ref[keys, c], "nt")
                    if kj == qi:
                        s = jnp.where(causal, s, -1e30)
                    bm = jnp.max(s, axis=1, keepdims=True)
                    if kj == 0:
                        p = jnp.exp2(s - bm)
                        state[h] = (bm, jnp.sum(p, axis=1, keepdims=True), _dot(p, v_ref[keys, c]))
                    else:
                        m, l, acc = state[h]
                        m_new = jnp.maximum(m, bm)
                        corr = jnp.exp2(m - m_new)
                        p = jnp.exp2(s - m_new)
                        state[h] = (m_new, l * corr + jnp.sum(p, axis=1, keepdims=True), acc * corr + _dot(p, v_ref[keys, c]))
            for h, c in enumerate(cols):
                m, l, acc = state[h]
                o_ref[rows, c] = (acc / l).astype(o_ref.dtype)
                lse_ref[0, h, :, rows] = _col_to_row(m + jnp.log2(l))

    blk = pl.BlockSpec((S, ATT_HP * SLOT), lambda b, h: (b, h))
    return _call_with_comm(
        body, (nseq, MLA_H // ATT_HP), "attn_fwd", [blk, blk, blk], [q, k, v],
        [blk, pl.BlockSpec((1, ATT_HP, 1, S), lambda b, h: (b, h, 0, 0))],
        [jax.ShapeDtypeStruct((T, MLA_H * SLOT), BF16), jax.ShapeDtypeStruct((nseq, MLA_H, 1, S), F32)], comm)


def attn_slot_bwd(q, k, v, o, lse, do, nseq, comm=None):
    T = q.shape[0]
    S = T // nseq
    t = min(ATT_T, S)
    nb = S // t
    scale = QK ** -0.5
    cols = [slice(h * SLOT, (h + 1) * SLOT) for h in range(ATT_HP)]

    def body(q_ref, k_ref, v_ref, o_ref, lse_ref, do_ref, dq_ref, dk_ref, dv_ref, dqa_ref):
        causal_t = lax.broadcasted_iota(jnp.int32, (t, t), 0) <= lax.broadcasted_iota(jnp.int32, (t, t), 1)
        ones = jnp.ones((8, SLOT), F32)
        delta = {}
        for qi in range(nb):
            sl = slice(qi * t, (qi + 1) * t)
            for h, c in enumerate(cols):
                prod = do_ref[sl, c].astype(F32) * o_ref[sl, c].astype(F32)
                delta[h, qi] = _dot_sel(ones, prod, "nt", split="b", terms=2)[0:1, :]
        for kj in range(nb):
            ks = slice(kj * t, (kj + 1) * t)
            dk = [None] * ATT_HP
            dv = [None] * ATT_HP
            for qi in range(kj, nb):
                sl = slice(qi * t, (qi + 1) * t)
                for h, c in enumerate(cols):
                    kb, vb, qb, dob = k_ref[ks, c], v_ref[ks, c], q_ref[sl, c], do_ref[sl, c]
                    st = _dot(kb, qb, "nt")
                    pt = jnp.exp2(st - lse_ref[0, h, :, sl])
                    if qi == kj:
                        pt = jnp.where(causal_t, pt, 0.0)
                    dpt = _dot(vb, dob, "nt")
                    dst = (pt * (dpt - delta[h, qi])).astype(BF16)
                    dvc = _dot(pt, dob)
                    dkc = _dot(dst, qb) * (1.0 / LOG2E)
                    dv[h] = dvc if dv[h] is None else dv[h] + dvc
                    dk[h] = dkc if dk[h] is None else dk[h] + dkc
                    dqc = _dot(dst, kb, "tn") * scale
                    if kj > 0:
                        dqc = dqc + dqa_ref[sl, c]
                    if qi == kj:
                        dq_ref[sl, c] = dqc.astype(dq_ref.dtype)
                    else:
                        dqa_ref[sl, c] = dqc
            for h, c in enumerate(cols):
                dk_ref[ks, c] = dk[h].astype(dk_ref.dtype)
                dv_ref[ks, c] = dv[h].astype(dv_ref.dtype)

    blk = pl.BlockSpec((S, ATT_HP * SLOT), lambda b, h: (b, h))
    lse_spec = pl.BlockSpec((1, ATT_HP, 1, S), lambda b, h: (b, h, 0, 0))
    W = MLA_H * SLOT
    return _call_with_comm(
        body, (nseq, MLA_H // ATT_HP), "attn_bwd", [blk, blk, blk, blk, lse_spec, blk], [q, k, v, o, lse, do], [blk, blk, blk],
        [jax.ShapeDtypeStruct((T, W), BF16)] * 3, comm, scratch=[pltpu.VMEM((S, ATT_HP * SLOT), F32)])


def _rope_coeffs(pos, inv):
    half = ROPE // 2
    ang = pos * inv
    lane = lax.broadcasted_iota(jnp.int32, (1, SLOT), 1)
    sn = jnp.sin(ang)
    C = jnp.where(lane < NOPE, 1.0, jnp.where(lane < QK, jnp.cos(ang), 0.0))
    Sg = jnp.where((lane >= NOPE) & (lane < NOPE + half), -sn, jnp.where((lane >= NOPE + half) & (lane < QK), sn, 0.0))
    return C, Sg


def _rope_inputs(positions):
    half = ROPE // 2
    inv = ROPE_THETA ** (-jnp.arange(0, ROPE, 2, dtype=F32) / ROPE)
    row = jnp.zeros((1, SLOT), F32).at[0, NOPE:NOPE + half].set(inv).at[0, NOPE + half:QK].set(inv)
    return positions.astype(F32).reshape(-1, 1), row


def _place_k_rope(kr_lanes):
    r = lax.broadcasted_iota(jnp.int32, (SLOT, SLOT), 0)
    c = lax.broadcasted_iota(jnp.int32, (SLOT, SLOT), 1)
    return _dot_sel(kr_lanes, ((c == r + NOPE) & (r < ROPE)).astype(F32))


def rope_table(pos, inv):
    return rowwise(_rope_coeffs, [pos], [inv], [(SLOT, F32), (SLOT, F32)], [], "rope_table")


def rope_q_epilogue(accs, ex):
    C, Sg = ex[0], ex[1]
    reps = accs[0].shape[1] // SLOT
    return ((accs[0] * jnp.tile(C, (1, reps)) + _rope_swap(accs[0]) * jnp.tile(Sg, (1, reps))) * Q_SCALE,)


def rope_k_epilogue(accs, ex):
    C, Sg = ex[0], ex[1]
    kr = _place_k_rope(ex[2][:, SLOT:2 * SLOT])
    kr = kr * C + _rope_swap(kr) * Sg
    return (accs[0] + jnp.tile(kr, (1, accs[0].shape[1] // SLOT)),)


def _rope_swap(x):
    W = x.shape[1]
    half = ROPE // 2
    lane = lax.broadcasted_iota(jnp.int32, (1, W), 1) & (SLOT - 1)
    up = pltpu.roll(x, W - half, axis=1)
    dn = pltpu.roll(x, half, axis=1)
    return jnp.where((lane >= NOPE) & (lane < NOPE + half), up, jnp.where((lane >= NOPE + half) & (lane < QK), dn, 0.0))


def rope_slot_bwd(dq, dk, C, Sg, name):
    def fn(dqv, dkv, C, Sg):
        ct, stl = jnp.tile(C, (1, MLA_H)), jnp.tile(Sg, (1, MLA_H))
        dqo = dqv * ct - _rope_swap(dqv) * stl
        tot = dkv[:, 0:SLOT]
        for h in range(1, MLA_H):
            tot = tot + dkv[:, h * SLOT:(h + 1) * SLOT]
        u = tot * C - _rope_swap(tot) * Sg
        r = lax.broadcasted_iota(jnp.int32, (SLOT, SLOT), 0)
        c = lax.broadcasted_iota(jnp.int32, (SLOT, SLOT), 1)
        unplace = ((r == c + NOPE) & (c < ROPE)).astype(F32)
        return dqo, dkv, _dot_sel(u, unplace, terms=2)
    W = MLA_H * SLOT
    return rowwise(fn, [dq, dk, C, Sg], [], [(W, BF16), (W, BF16), (SLOT, BF16)], [], name)


XA_BLK = 512


def xattn_fwd(q, k, v, nseq):
    T = q.shape[0]
    S = T // nseq
    M = k.shape[0] // nseq
    tq = min(XA_BLK, S)
    nq = S // tq
    scale = XA_D ** -0.5

    def body(q_ref, k_ref, v_ref, o_ref):
        s = _dot(q_ref[...], k_ref[...], "nt") * scale
        p = jnp.exp(s - jnp.max(s, axis=1, keepdims=True))
        p = p / jnp.sum(p, axis=1, keepdims=True)
        o_ref[...] = _dot(p, v_ref[...]).astype(o_ref.dtype)

    qs = pl.BlockSpec((tq, XA_D), lambda b, h, i: (b * nq + i, h))
    ks = pl.BlockSpec((M, XA_D), lambda b, h, i: (b, h))
    return pl.pallas_call(
        body, grid=(nseq, XA_H, nq), name="xattn_fwd", in_specs=[qs, ks, ks], out_specs=qs,
        out_shape=jax.ShapeDtypeStruct((T, XA_H * XA_D), BF16),
        compiler_params=_cp("parallel", "parallel", "parallel"),
    )(q, k, v)


def xattn_bwd(q, k, v, do, nseq):
    T = q.shape[0]
    S = T // nseq
    M = k.shape[0] // nseq
    tq = min(XA_BLK, S)
    nq = S // tq
    scale = XA_D ** -0.5

    def body(q_ref, k_ref, v_ref, do_ref, dq_ref, dk_ref, dv_ref):
        @pl.when(pl.program_id(2) == 0)
        def _():
            dk_ref[...] = jnp.zeros_like(dk_ref)
            dv_ref[...] = jnp.zeros_like(dv_ref)

        qb, kb, vb, dob = q_ref[...], k_ref[...], v_ref[...], do_ref[...]
        s = _dot(qb, kb, "nt") * scale
        p = jnp.exp(s - jnp.max(s, axis=1, keepdims=True))
        p = p / jnp.sum(p, axis=1, keepdims=True)
        dp = _dot(dob, vb, "nt")
        ds = p * (dp - jnp.sum(dp * p, axis=1, keepdims=True)) * scale
        dq_ref[...] = _dot(ds, kb).astype(dq_ref.dtype)
        dk_ref[...] += _dot(ds, qb, "tn")
        dv_ref[...] += _dot(p, dob, "tn")

    qs = pl.BlockSpec((tq, XA_D), lambda b, h, i: (b * nq + i, h))
    ks = pl.BlockSpec((M, XA_D), lambda b, h, i: (b, h))
    return pl.pallas_call(
        body, grid=(nseq, XA_H, nq), name="xattn_bwd", in_specs=[qs, ks, ks, qs], out_specs=[qs, ks, ks],
        out_shape=[jax.ShapeDtypeStruct((T, XA_H * XA_D), BF16), jax.ShapeDtypeStruct(k.shape, F32),
                   jax.ShapeDtypeStruct(k.shape, F32)],
        compiler_params=_cp("parallel", "parallel", "arbitrary"),
    )(q, k, v, do)


CONV_BLK = 256


def _shift_down(x, s, rows):
    if s == 0:
        return x
    return jnp.where(rows >= s, pltpu.roll(x, s, axis=0), 0.0)


def _shift_up(x, s, rows):
    if s == 0:
        return x
    S = x.shape[0]
    return jnp.where(rows < S - s, pltpu.roll(x, S - s, axis=0), 0.0)


def conv_fwd(x, w, b, nseq):
    T, CH = x.shape
    S = T // nseq

    def body(x_ref, w_ref, b_ref, o_ref):
        xv = x_ref[...].astype(F32)
        rows = lax.broadcasted_iota(jnp.int32, (S, 1), 0)
        c = jnp.zeros_like(xv) + b_ref[...]
        for kk in range(CONV_K):
            c = c + w_ref[kk:kk + 1, :] * _shift_down(xv, CONV_K - 1 - kk, rows)
        o_ref[...] = (c * _sigmoid(c)).astype(o_ref.dtype)

    xs = pl.BlockSpec((S, CONV_BLK), lambda j, bb: (bb, j))
    return pl.pallas_call(
        body, grid=(CH // CONV_BLK, nseq), name="conv_fwd",
        in_specs=[xs, pl.BlockSpec((CONV_K, CONV_BLK), lambda j, bb: (0, j)), pl.BlockSpec((1, CONV_BLK), lambda j, bb: (0, j))],
        out_specs=xs, out_shape=jax.ShapeDtypeStruct((T, CH), BF16),
        compiler_params=_cp("parallel", "parallel"),
    )(x, w, b)


def conv_bwd(x, w, b, dout, nseq):
    T, CH = x.shape
    S = T // nseq

    def body(x_ref, w_ref, b_ref, do_ref, dx_ref, dw_ref, db_ref):
        @pl.when(pl.program_id(1) == 0)
        def _():
            dw_ref[...] = jnp.zeros_like(dw_ref)
            db_ref[...] = jnp.zeros_like(db_ref)

        xv = x_ref[...].astype(F32)
        rows = lax.broadcasted_iota(jnp.int32, (S, 1), 0)
        c = jnp.zeros_like(xv) + b_ref[...]
        sh = [_shift_down(xv, CONV_K - 1 - kk, rows) for kk in range(CONV_K)]
        for kk in range(CONV_K):
            c = c + w_ref[kk:kk + 1, :] * sh[kk]
        sg = _sigmoid(c)
        dc = do_ref[...].astype(F32) * sg * (1.0 + c * (1.0 - sg))
        dx = jnp.zeros_like(xv)
        for kk in range(CONV_K):
            dx = dx + w_ref[kk:kk + 1, :] * _shift_up(dc, CONV_K - 1 - kk, rows)
            dw_ref[kk:kk + 1, :] += jnp.sum(dc * sh[kk], axis=0, keepdims=True)
        dx_ref[...] = dx.astype(dx_ref.dtype)
        db_ref[...] += jnp.sum(dc, axis=0, keepdims=True)

    xs = pl.BlockSpec((S, CONV_BLK), lambda j, bb: (bb, j))
    ws = pl.BlockSpec((CONV_K, CONV_BLK), lambda j, bb: (0, j))
    bs = pl.BlockSpec((1, CONV_BLK), lambda j, bb: (0, j))
    return pl.pallas_call(
        body, grid=(CH // CONV_BLK, nseq), name="conv_bwd",
        in_specs=[xs, ws, bs, xs], out_specs=[xs, ws, bs],
        out_shape=[jax.ShapeDtypeStruct((T, CH), BF16), jax.ShapeDtypeStruct((CONV_K, CH), F32),
                   jax.ShapeDtypeStruct((1, CH), F32)],
        compiler_params=_cp("parallel", "arbitrary"),
    )(x, w, b, dout)


def _dims(a, b, mode):
    M = a.shape[1] if mode[0] == "t" else a.shape[0]
    K = a.shape[0] if mode[0] == "t" else a.shape[1]
    N = b.shape[0] if mode[1] == "t" else b.shape[1]
    return M, K, N


def _tile(dim, prefs):
    for p in prefs:
        if dim % p == 0:
            return p
    return dim


def mm(groups, out_dtypes, name, tm=None, tn=None, tk=None, epi=None, extras=(), comm=None, sub=1, n_sum=0):
    a0, b0, m0 = groups[0][0]
    M, K0, N = _dims(a0, b0, m0)
    tm = tm or _tile(M, (1024, 512, 256, 128))
    tn = tn or _tile(N, (1024, 512, 256, 128))
    flat = [p for g in groups for p in g]
    nk = 1 if tk is None else K0 // tk
    in_specs, args = [], []
    for a, b, mode in flat:
        _, K, _ = _dims(a, b, mode)
        kb = K if tk is None else tk
        in_specs.append(pl.BlockSpec((kb, tm), lambda i, j, k: (k, i)) if mode[0] == "t"
                        else pl.BlockSpec((tm, kb), lambda i, j, k: (i, k)))
        in_specs.append(pl.BlockSpec((tn, kb), lambda i, j, k: (j, k)) if mode[1] == "t"
                        else pl.BlockSpec((kb, tn), lambda i, j, k: (k, j)))
        args += [a, b]
    kinds = []
    for e in extras:
        kind, e = e if isinstance(e, tuple) else ("vec" if e.shape[0] == 1 and M != 1 else "tile", e)
        in_specs.append({"tile": pl.BlockSpec((tm, tn), lambda i, j, k: (i, j)),
                         "vec": pl.BlockSpec((1, tn), lambda i, j, k: (0, j)),
                         "rows": pl.BlockSpec((tm, e.shape[1]), lambda i, j, k: (i, 0)),
                         "whole": pl.BlockSpec(e.shape, lambda i, j, k: (0, 0))}[kind])
        kinds.append(kind)
        args.append(e)
    n_in = len(args)
    n_main = len(out_dtypes)
    n_out = n_main + n_sum
    assert n_sum == 0 or (tn == N and tk is None)
    ng = len(groups)
    sizes = [len(g) for g in groups]

    def body(*refs):
        ins, outs, accs = refs[:n_in], refs[n_in:n_in + n_out], refs[n_in + n_out:]
        kk = pl.program_id(2)

        def dots(rs):
            vals, pos = [], 0
            for gi in range(ng):
                acc = None
                for _ in range(sizes[gi]):
                    mode = flat[pos // 2][2]
                    av = ins[pos][:, rs] if mode[0] == "t" else ins[pos][rs, :]
                    d = _dot(av, ins[pos + 1][...], mode)
                    acc = d if acc is None else acc + d
                    pos += 2
                vals.append(acc)
            return vals

        def finish(accv, rs, first_chunk=True):
            ex = [(r[rs, :] if kind in ("tile", "rows") else r[...]).astype(F32) for kind, r in zip(kinds, ins[2 * len(flat):])]
            res = epi(accv, ex) if epi is not None else tuple(accv)
            for o, r in zip(outs[:n_main], res[:n_main]):
                o[rs, :] = r.astype(o.dtype)
            for o, r in zip(outs[n_main:], res[n_main:]):
                if first_chunk:
                    @pl.when(pl.program_id(0) == 0)
                    def _():
                        o[...] = r

                    @pl.when(pl.program_id(0) > 0)
                    def _():
                        o[...] += r
                else:
                    o[...] += r

        if nk == 1:
            for r in range(sub):
                rs = slice(r * (tm // sub), (r + 1) * (tm // sub))
                finish(dots(rs), rs, r == 0)
        else:
            vals = dots(slice(0, tm))
            finish = functools.partial(finish, rs=slice(0, tm))
            @pl.when(kk == 0)
            def _():
                for ar, vv in zip(accs, vals):
                    ar[...] = vv

            @pl.when(kk > 0)
            def _():
                for ar, vv in zip(accs, vals):
                    ar[...] += vv

            @pl.when(kk == nk - 1)
            def _():
                finish([ar[...] for ar in accs])

    grid = (M // tm, N // tn, nk)
    out_specs = [pl.BlockSpec((tm, tn), lambda i, j, k: (i, j)) for _ in out_dtypes] \
        + [pl.BlockSpec((1, tn), lambda i, j, k: (0, j))] * n_sum
    out_shape = [jax.ShapeDtypeStruct((M, N), dt) for dt in out_dtypes] + [jax.ShapeDtypeStruct((1, N), F32)] * n_sum
    scratch = [pltpu.VMEM((tm, tn), F32) for _ in range(ng if nk > 1 else 0)]
    sem = ("arbitrary" if n_sum else "parallel", "parallel", "arbitrary")
    if comm is not None:
        body = _attach(comm, body, n_in, n_out, *_grid_ends(grid))
        in_specs, args = in_specs + [HBM_SPEC] * len(comm.inputs), args + comm.inputs
        out_specs, out_shape = out_specs + [HBM_SPEC] * len(comm.out_shapes), out_shape + comm.out_shapes
        scratch, sem = scratch + comm.sems, ("arbitrary",) * 3
    return pl.pallas_call(body, grid=grid, name=name, in_specs=in_specs, out_specs=out_specs, out_shape=out_shape,
                          scratch_shapes=scratch, compiler_params=_cp(*sem))(*args)


def mm1(a, b, mode, out_dtype, name, **kw):
    return mm([[(a, b, mode)]], [out_dtype], name, **kw)[0]


ROW_BLK = 512


def rowwise(fn, rows, consts, outs, accs, name, tb=ROW_BLK):
    rows = [r if isinstance(r, tuple) else (r, r.shape[1], 0) for r in rows]
    T = rows[0][0].shape[0]
    tb = min(tb, T)
    n_r, n_c, n_o, n_a = len(rows), len(consts), len(outs), len(accs)

    def body(*refs):
        vals = [r[...].astype(F32) for r in refs[:n_r + n_c]]
        res = fn(*vals)
        o_refs = refs[n_r + n_c:n_r + n_c + n_o]
        a_refs = refs[n_r + n_c + n_o:]
        for o, r in zip(o_refs, res[:n_o]):
            o[...] = r.astype(o.dtype)
        if n_a:
            @pl.when(pl.program_id(0) == 0)
            def _():
                for ar in a_refs:
                    ar[...] = jnp.zeros_like(ar)
            for ar, r in zip(a_refs, res[n_o:]):
                ar[...] += r

    return pl.pallas_call(
        body, grid=(T // tb,), name=name,
        in_specs=[pl.BlockSpec((tb, w), functools.partial(lambda i, j: (i, j), j=j)) for _, w, j in rows]
        + [pl.BlockSpec(c.shape, lambda i: (0, 0)) for c in consts],
        out_specs=[pl.BlockSpec((tb, d), lambda i: (i, 0)) for d, _ in outs]
        + [pl.BlockSpec(s, lambda i: (0, 0)) for s in accs],
        out_shape=[jax.ShapeDtypeStruct((T, d), dt) for d, dt in outs]
        + [jax.ShapeDtypeStruct(s, F32) for s in accs],
        compiler_params=_cp("arbitrary" if n_a else "parallel"),
    )(*[r[0] for r in rows], *consts)


def _rms_stats(x):
    r = lax.rsqrt(jnp.mean(x * x, axis=-1, keepdims=True) + EPS)
    return r, x * r


def _rms_bwd(x, g, dy):
    r, xn = _rms_stats(x)
    dyg = dy * g
    dx = r * (dyg - xn * jnp.mean(dyg * xn, axis=-1, keepdims=True))
    return dx, jnp.sum(dy * xn, axis=0, keepdims=True)


def rms_fwd(x, g, name):
    return rowwise(lambda xv, gv: (_rms_stats(xv)[1] * gv,), [x], [g], [(x.shape[1], BF16)], [], name)[0]


def rms_bwd(x, g, dy, name, resid=None, dx_dtype=F32):
    def fn(*v):
        if resid is None:
            xv, dyv, gv = v
            dx, dg = _rms_bwd(xv, gv, dyv)
        else:
            xv, dyv, rv, gv = v
            dx, dg = _rms_bwd(xv, gv, dyv)
            dx = dx + rv
        return dx, dg
    rows = [x, dy] + ([] if resid is None else [resid])
    return rowwise(fn, rows, [g], [(x.shape[1], dx_dtype)], [(1, x.shape[1])], name)


def mm_rms_bwd(pairs, x, g, name, resid=None, dx_dtype=F32, comm=None):
    def epi(accs, ex):
        dx, dg = _rms_bwd(ex[0], ex[-1], accs[0])
        return (dx if resid is None else dx + ex[1]), dg
    extras = [x] + ([] if resid is None else [resid]) + [g]
    return mm([pairs], [dx_dtype], name, tm=min(256, x.shape[0]), tn=x.shape[1], epi=epi, extras=extras, comm=comm, n_sum=1)


def mm_resid(a, b, x, g, wgt, name, comm=None, target=None):
    def epi(accs, ex):
        y = ex[0] + wgt * _rms_stats(accs[0])[1] * ex[1]
        if target is None:
            return accs[0], y
        d = y - ex[2]
        return accs[0], d / D, jnp.sum(d * d, axis=0, keepdims=True)
    return mm([[(a, b, "nn")]], [F32, F32], name, tm=min(512, a.shape[0]), tn=b.shape[1], epi=epi,
              extras=[x, g] + ([] if target is None else [target]), sub=2, comm=comm, n_sum=0 if target is None else 1)


def resid_bwd(h, g, dy, wgt, name):
    def fn(hv, dyv, gv):
        dx, dg = _rms_bwd(hv, gv, dyv)
        return wgt * dx, wgt * dg
    return rowwise(fn, [h, dy], [g], [(h.shape[1], BF16)], [(1, h.shape[1])], name)


def _silu_parts(g):
    s = _sigmoid(g)
    return g * s, s * (1.0 + g * (1.0 - s))


def gated_norm_fwd(y, z, g, name):
    W = SSD_INNER // SSD_G

    def fn(yv, zv, gv):
        yg = yv * _silu_parts(zv)[0]
        return (jnp.concatenate([_rms_stats(yg[:, i * W:(i + 1) * W])[1] for i in range(SSD_G)], axis=1) * gv,)
    return rowwise(fn, [y, z], [g], [(SSD_INNER, BF16)], [], name)[0]


def gated_norm_bwd(y, z, dyn, g, name):
    W = SSD_INNER // SSD_G

    def fn(yv, zv, dv, gv):
        sil, dsil = _silu_parts(zv)
        yg = yv * sil
        parts = [_rms_bwd(yg[:, i * W:(i + 1) * W], gv[:, i * W:(i + 1) * W], dv[:, i * W:(i + 1) * W]) for i in range(SSD_G)]
        dyg = jnp.concatenate([p[0] for p in parts], axis=1)
        dg = jnp.concatenate([p[1] for p in parts], axis=1)
        return dyg * sil, dyg * yv * dsil, dg
    return rowwise(fn, [y, z, dyn], [g], [(SSD_INNER, BF16), (SSD_INNER, BF16)], [(1, SSD_INNER)], name)


def merge_fwd(gl, ys, ym, gb, name):
    def fn(glv, ysv, ymv, gbv):
        gt = _sigmoid(glv + gbv)
        return (gt[:, :D] * ysv + gt[:, D:] * ymv,)
    return rowwise(fn, [gl, ys, ym], [gb], [(D, BF16)], [], name)[0]


def merge_bwd(gl, ys, ym, dm, gb, name):
    def fn(glv, ysv, ymv, dmv, gbv):
        gt = _sigmoid(glv + gbv)
        gs, gm = gt[:, :D], gt[:, D:]
        dgl = jnp.concatenate([dmv * ysv * gs * (1.0 - gs), dmv * ymv * gm * (1.0 - gm)], axis=1)
        return dmv * gs, dmv * gm, dgl, jnp.sum(dgl, axis=0, keepdims=True)
    return rowwise(fn, [gl, ys, ym, dm], [gb], [(D, BF16), (D, BF16), (2 * D, BF16)], [(1, 2 * D)], name)


def loss_head(y, tgt, name):
    def fn(yv, tv):
        d = yv - tv
        part = 0.5 * jnp.sum(jnp.sum(d * d, axis=1, keepdims=True), axis=0, keepdims=True) / D
        return d / D, jnp.broadcast_to(part, (1, 128))
    return rowwise(fn, [y, tgt], [], [(D, F32)], [(1, 128)], name)


def _adamw_math(wv, gv, mv, vv):
    mn = B1 * mv + (1.0 - B1) * gv
    vn = B2 * vv + (1.0 - B2) * (gv * gv)
    mh = mn / (1.0 - B1 ** STEP)
    vh = vn / (1.0 - B2 ** STEP)
    return -LR * (mh / (jnp.sqrt(vh) + AEPS) + WD * wv), mn, vn


def adamw(w, g, m, v, name):
    R, C = w.shape
    tb = _tile(R, (256, 128, 64, 32, 16, 8))
    return rowwise(_adamw_math, [w, g, m, v], [], [(C, F32)] * 3, [], name, tb=tb)


def adamw_small(packed, ws, ms, vs):
    k = len(ws)
    sizes = [x.shape[1] for x in ws]

    def body(*refs):
        p_ref, w_refs, m_refs, v_refs = refs[0], refs[1:1 + k], refs[1 + k:1 + 2 * k], refs[1 + 2 * k:1 + 3 * k]
        outs = refs[1 + 3 * k:]
        r0 = 0
        for i, n in enumerate(sizes):
            nr = -(-n // 128)
            g = jnp.concatenate([p_ref[r0 + r:r0 + r + 1, :] for r in range(nr)], axis=1)[:, :n]
            r0 += nr
            outs[i][...] = g
            outs[k + i][...], outs[2 * k + i][...], outs[3 * k + i][...] = _adamw_math(w_refs[i][...], g, m_refs[i][...], v_refs[i][...])

    res = pl.pallas_call(body, name="adamw_small",
                         out_shape=[jax.ShapeDtypeStruct((1, n), F32) for _ in range(4) for n in sizes])(packed, *ws, *ms, *vs)
    return [res[j * k:(j + 1) * k] for j in range(4)]


def adamw_from_slots(recv, piece, w, m, v, name, token=None):
    K, n = w.shape
    ns = recv.shape[0]
    assert recv.shape[2] == n and recv.shape[1] % K == 0
    tb = _tile(K, (256, 176, 128, 64, 32, 16, 8)) if K % 8 == 0 else K
    r_spec = pl.BlockSpec((ns, tb, n), lambda i: (0, piece * (K // tb) + i, 0))
    w_spec = pl.BlockSpec((tb, n), lambda i: (i, 0))

    def body(r_ref, w_ref, m_ref, v_ref, *rest):
        g_ref, d_ref, mo_ref, vo_ref = rest[-4:]
        g = r_ref[0].astype(F32)
        for s in range(1, ns):
            g = g + r_ref[s].astype(F32)
        g_ref[...] = g
        d_ref[...], mo_ref[...], vo_ref[...] = _adamw_math(w_ref[...], g, m_ref[...], v_ref[...])

    extra = [] if token is None else [token]
    return pl.pallas_call(
        body, grid=(K // tb,), name=name,
        in_specs=[r_spec, w_spec, w_spec, w_spec] + [pl.BlockSpec(t.shape, lambda i: (0, 0)) for t in extra], out_specs=[w_spec] * 4,
        out_shape=[jax.ShapeDtypeStruct((K, n), F32)] * 4, compiler_params=_cp("parallel"),
    )(recv, w, m, v, *extra)


def _me():
    return lax.axis_index("x"), lax.axis_index("y"), lax.axis_index("c")


def _dev_index():
    x, y, c = _me()
    return 4 * x + 2 * y + c


HBM_SPEC = pl.BlockSpec(memory_space=pl.ANY)


class GatherComm:
    def __init__(self, shards):
        self.inputs = [s for s, _ in shards]
        self.rows = [list(r) for _, r in shards]
        n = len(shards)
        self.out_shapes = [jax.ShapeDtypeStruct((N_DEV, r, s.shape[1]), s.dtype) for s, rows in shards for r in rows]
        self.sems = [pltpu.SemaphoreType.DMA((7 * n,)), pltpu.SemaphoreType.DMA((7 * n,)), pltpu.SemaphoreType.DMA((n,))]

    def _plan(self, x_refs, out_refs, sems):
        send_sems, recv_sems, local_sems = sems
        x, y, c = _me()
        me, sibling = (x, y, c), (x, y, 1 - c)
        chips = [(1 - x, y), (x, 1 - y), (1 - x, 1 - y)]
        index = lambda px, py, pc: 4 * px + 2 * py + pc
        mine, first, passed, whole = [], [], [], []
        pos = 0
        for i, rows in enumerate(self.rows):
            kw = lambda k: dict(send_sem=send_sems.at[7 * i + k], recv_sem=recv_sems.at[7 * i + k], device_id_type=MESH)
            r0 = 0
            fwd = [[] for _ in chips]
            for j, nr in enumerate(rows):
                out, src = out_refs[pos + j], x_refs[i].at[pl.ds(r0, nr)]
                mine.append(pltpu.make_async_copy(src, out.at[index(*me)], local_sems.at[i]))
                first.append(pltpu.make_async_remote_copy(src_ref=src, dst_ref=out.at[index(*me)], device_id=sibling, **kw(0)))
                for jj, chip in enumerate(chips):
                    first.append(pltpu.make_async_remote_copy(src_ref=src, dst_ref=out.at[index(*me)], device_id=(*chip, c),
                                                              **kw(1 + jj)))
                    blk = out.at[index(*chip, c)]
                    fwd[jj].append(pltpu.make_async_remote_copy(src_ref=blk, dst_ref=blk, device_id=sibling, **kw(4 + jj)))
                r0 += nr
            passed.append(fwd)
            whole.append([pltpu.make_async_remote_copy(src_ref=x_refs[i], dst_ref=x_refs[i], device_id=sibling, **kw(k))
                          for k in range(7)])
            pos += len(rows)
        return mine, first, passed, whole

    def start(self, x_refs, out_refs, sems):
        mine, first, _, _ = self._plan(x_refs, out_refs, sems)
        for cp in mine + first:
            cp.start()

    def finish(self, x_refs, out_refs, sems):
        _, _, passed, whole = self._plan(x_refs, out_refs, sems)
        local_sems = sems[2]
        for i, fwd in enumerate(passed):
            for jj in range(3):
                whole[i][1 + jj].wait_recv()
                for cp in fwd[jj]:
                    cp.start()
        for i in range(len(passed)):
            whole[i][0].wait_recv()
            for jj in range(3):
                whole[i][4 + jj].wait_recv()
        for i in range(len(passed)):
            for k in range(7):
                whole[i][k].wait_send()
            pltpu.make_async_copy(x_refs[i], x_refs[i], local_sems.at[i]).wait()


def run_comm(comm, name):
    n_in, n_out = len(comm.inputs), len(comm.out_shapes)

    def body(*refs):
        ins, outs, sems = refs[:n_in], refs[n_in:n_in + n_out], refs[n_in + n_out:]
        comm.start(ins, outs, sems)
        comm.finish(ins, outs, sems)

    return pl.pallas_call(body, name=name, out_shape=comm.out_shapes, in_specs=[HBM_SPEC] * n_in,
                          out_specs=[HBM_SPEC] * n_out, scratch_shapes=comm.sems)(*comm.inputs)


def _attach(comm, body, n_in, n_out, first, last):
    if comm is None:
        return body
    ci, co, cs = len(comm.inputs), len(comm.out_shapes), len(comm.sems)

    def wrapped(*refs):
        h_in, c_in = refs[:n_in], refs[n_in:n_in + ci]
        h_out, c_out = refs[n_in + ci:n_in + ci + n_out], refs[n_in + ci + n_out:n_in + ci + n_out + co]
        rest = refs[n_in + ci + n_out + co:]
        h_scr, c_sem = rest[:len(rest) - cs], rest[len(rest) - cs:]

        @pl.when(first())
        def _():
            comm.start(c_in, c_out, c_sem)

        body(*h_in, *h_out, *h_scr)

        @pl.when(last())
        def _():
            comm.finish(c_in, c_out, c_sem)

    return wrapped


def _grid_ends(grid):
    first = lambda: functools.reduce(lambda a, b: a & b, [pl.program_id(i) == 0 for i in range(len(grid))])
    last = lambda: functools.reduce(lambda a, b: a & b, [pl.program_id(i) == g - 1 for i, g in enumerate(grid)])
    return first, last


def _call_with_comm(body, grid, name, in_specs, args, out_specs, out_shape, comm, scratch=(), sem=None):
    sem = sem or ("parallel",) * len(grid)
    scratch = list(scratch)
    if comm is not None:
        body = _attach(comm, body, len(args), len(out_shape), *_grid_ends(grid))
        in_specs, args = in_specs + [HBM_SPEC] * len(comm.inputs), args + comm.inputs
        out_specs, out_shape = out_specs + [HBM_SPEC] * len(comm.out_shapes), out_shape + comm.out_shapes
        scratch, sem = scratch + comm.sems, ("arbitrary",) * len(grid)
    return pl.pallas_call(body, grid=grid, name=name, in_specs=in_specs, out_specs=out_specs, out_shape=out_shape,
                          scratch_shapes=scratch, compiler_params=_cp(*sem))(*args)


class ScatterComm:
    def __init__(self, groups):
        self.sizes = [len(g) for g in groups]
        self.rows = [[pc.shape[1] for pc in g] for g in groups]
        ng = len(groups)
        self.inputs = [pc for g in groups for pc in g]
        self.out_shapes = [jax.ShapeDtypeStruct((N_DEV, sum(self.rows[gi]), g[0].shape[2]), g[0].dtype) for gi, g in enumerate(groups)]
        self.sems = [pltpu.SemaphoreType.DMA((7 * ng,)), pltpu.SemaphoreType.DMA((7 * ng,)), pltpu.SemaphoreType.DMA((ng,))]

    def _peers(self):
        x, y, c = _me()
        out = []
        for k in range(1, N_DEV):
            px = 1 - x if k & 4 else x
            py = 1 - y if k & 2 else y
            pc = 1 - c if k & 1 else c
            out.append((k, 4 * px + 2 * py + pc, dict(device_id=(px, py, pc), device_id_type=MESH)))
        return 4 * x + 2 * y + c, out

    def start(self, ins, outs, sems):
        send_sems, recv_sems, local_sems = sems
        me, peers = self._peers()
        pos = 0
        for gi, size in enumerate(self.sizes):
            for i, pc in enumerate(ins[pos:pos + size]):
                dst = outs[gi].at[me, pl.ds(sum(self.rows[gi][:i]), self.rows[gi][i])]
                pltpu.make_async_copy(pc.at[me], dst, local_sems.at[gi]).start()
                for k, peer, kw in peers:
                    pltpu.make_async_remote_copy(src_ref=pc.at[peer], dst_ref=dst, send_sem=send_sems.at[7 * gi + k - 1],
                                                 recv_sem=recv_sems.at[7 * gi + k - 1], **kw).start()
            pos += size

    def finish(self, ins, outs, sems):
        send_sems, recv_sems, local_sems = sems
        me, peers = self._peers()
        whole = [pltpu.make_async_remote_copy(src_ref=outs[gi].at[peer], dst_ref=outs[gi].at[peer],
                                              send_sem=send_sems.at[7 * gi + k - 1], recv_sem=recv_sems.at[7 * gi + k - 1], **kw)
                 for gi in range(len(self.sizes)) for k, peer, kw in peers]
        for cp in whole:
            cp.wait_recv()
        for cp in whole:
            cp.wait_send()
        for gi in range(len(self.sizes)):
            pltpu.make_async_copy(outs[gi].at[me], outs[gi].at[me], local_sems.at[gi]).wait()


def _peer_list():
    x, y, c = _me()
    out = []
    for k in range(1, N_DEV):
        px = 1 - x if k & 4 else x
        py = 1 - y if k & 2 else y
        pc = 1 - c if k & 1 else c
        out.append((k, 4 * px + 2 * py + pc, dict(device_id=(px, py, pc), device_id_type=MESH)))
    return 4 * x + 2 * y + c, out


SEM_SPEC = pl.BlockSpec(memory_space=pltpu.SEMAPHORE)
HBM_ONLY = pl.BlockSpec(memory_space=pltpu.HBM)
N_SPLIT_SEMS = 2 * (N_DEV - 1)


def exchange_start(piece, after):
    def body(piece_ref, land_ref, after_ref, *outs):
        sems, token = outs[:N_SPLIT_SEMS], outs[-1]
        me, peers = _peer_list()
        for k, peer, kw in peers:
            pltpu.make_async_remote_copy(src_ref=piece_ref.at[peer], dst_ref=land_ref.at[me], send_sem=sems[k - 1],
                                         recv_sem=sems[N_DEV - 2 + k], **kw).start()
        token[...] = jnp.zeros_like(token)

    res = pl.pallas_call(
        body, name="exchange_last_start",
        out_shape=(pltpu.SemaphoreType.DMA(()),) * N_SPLIT_SEMS + (pltpu.HBM(piece.shape, piece.dtype), pltpu.HBM(piece.shape, piece.dtype),
                                                                   jax.ShapeDtypeStruct((8, 128), F32)),
        in_specs=(HBM_ONLY, HBM_ONLY, HBM_SPEC),
        out_specs=(SEM_SPEC,) * N_SPLIT_SEMS + (HBM_ONLY, HBM_ONLY, pl.BlockSpec(memory_space=pltpu.VMEM)),
        input_output_aliases={0: N_SPLIT_SEMS, 1: N_SPLIT_SEMS + 1},
        compiler_params=pltpu.CompilerParams(has_side_effects=pltpu.SideEffectType.DATAFLOW_SIDE_EFFECTING),
    )(pltpu.with_memory_space_constraint(piece, pltpu.HBM),
      pltpu.with_memory_space_constraint(lax.empty(piece.shape, piece.dtype), pltpu.HBM), after)
    return res[:N_SPLIT_SEMS], res[N_SPLIT_SEMS], res[N_SPLIT_SEMS + 1], res[N_SPLIT_SEMS + 2]


def exchange_wait(sems, piece, land, after):
    def body(piece_ref, land_ref, *rest):
        sem_refs = rest[:N_SPLIT_SEMS]
        me, peers = _peer_list()
        for k, peer, kw in peers:
            cp = pltpu.make_async_remote_copy(src_ref=piece_ref.at[peer], dst_ref=land_ref.at[peer], send_sem=sem_refs[k - 1],
                                              recv_sem=sem_refs[N_DEV - 2 + k], **kw)
            cp.wait_send()
            cp.wait_recv()

    return pl.pallas_call(
        body, name="exchange_last_wait",
        out_shape=(pltpu.HBM(piece.shape, piece.dtype), pltpu.HBM(land.shape, land.dtype)),
        in_specs=(HBM_ONLY, HBM_ONLY) + (SEM_SPEC,) * N_SPLIT_SEMS + (HBM_SPEC,), out_specs=(HBM_ONLY, HBM_ONLY),
        input_output_aliases={0: 0, 1: 1},
        compiler_params=pltpu.CompilerParams(has_side_effects=pltpu.SideEffectType.DATAFLOW_SIDE_EFFECTING),
    )(piece, land, *sems, after)[1]


def sum_slots(recv, name, tr):
    n, R, C = recv.shape

    def body(r_ref, o_ref):
        acc = r_ref[0].astype(F32)
        for s in range(1, n):
            acc = acc + r_ref[s].astype(F32)
        o_ref[...] = acc

    return pl.pallas_call(
        body, grid=(R // tr,), name=name,
        in_specs=[pl.BlockSpec((n, tr, C), lambda i: (0, i, 0))], out_specs=pl.BlockSpec((tr, C), lambda i: (i, 0)),
        out_shape=jax.ShapeDtypeStruct((R, C), F32), compiler_params=_cp("parallel"),
    )(recv)


PACK_W, FLAT_W = 1024, 128
MAIN = [
    ("ffn1_w_gate", "col"), ("ffn1_w_up", "col"), ("ffn1_w_down", "row"),
    ("ffn2_w_gate", "col"), ("ffn2_w_up", "col"), ("ffn2_w_down", "row"),
    ("w_ssd_proj", "row"), ("w_mla_proj", "row"), ("w_out", "row"),
    ("w_xq", "row"), ("w_xk", "row"), ("w_xv", "row"), ("w_xo", "row"),
    ("w_uk", "col"), ("w_uv", "col"),
]
FLAT = [("w_in", "col"), ("w_uq", "col")]
BIG = MAIN + FLAT
SMALL = ["ffn1_pre_g", "ffn1_post_g", "mix_pre_g", "conv_b", "dt_bias", "a_log", "d_skip", "ssd_norm_g", "q_norm_g",
         "kv_norm_g", "gate_bias", "mix_post_g", "xa_pre_g", "mem_norm_g", "xa_post_g", "ffn2_pre_g", "ffn2_post_g"]
WEIGHTS = ['ffn1_pre_g', 'ffn1_w_gate', 'ffn1_w_up', 'ffn1_w_down', 'ffn1_post_g', 'mix_pre_g', 'w_in', 'conv_w', 'conv_b',
           'dt_bias', 'a_log', 'd_skip', 'ssd_norm_g', 'w_ssd_proj', 'q_norm_g', 'w_uq', 'kv_norm_g', 'w_uk', 'w_uv',
           'w_mla_proj', 'gate_bias', 'w_out', 'mix_post_g', 'xa_pre_g', 'mem_norm_g', 'w_xq', 'w_xk', 'w_xv', 'w_xo',
           'xa_post_g', 'ffn2_pre_g', 'ffn2_w_gate', 'ffn2_w_up', 'ffn2_w_down', 'ffn2_post_g']


def _pack_rows(w, kind, width):
    m = w[0].T if kind == "col" else w[0]
    return m.reshape(-1, width)


KIND = dict(BIG)
GATHER_PLAN = {
    "first": (["ffn1_w_gate", "ffn1_w_up"], []),
    "ffn1_gate_up": (["ffn1_w_down"], ["w_in@0"]),
    "ffn1_down": ([], ["w_in@1"]),
    "ssd_fwd": (["w_ssd_proj", "w_mla_proj", "w_out", "w_uk", "w_uv"], ["w_uq"]),
    "attn_fwd": (["w_xq", "w_xk", "w_xv", "w_xo", "ffn2_w_gate", "ffn2_w_up", "ffn2_w_down"], []),
}
CONV_RIDES_WITH = "w_in@1"
LAST_EXCHANGE = "last"
SCATTER_PLAN = {
    "attn_bwd": [["ffn2_w_gate", "ffn2_w_up", "ffn2_w_down"], ["w_xq", "w_xk", "w_xv", "w_xo"]],
    "ssd_bwd": [["w_ssd_proj", "w_mla_proj", "w_out"], ["w_uk", "w_uv"], ["w_uq"]],
    "in_bwd": [["w_in#0"]],
    "ffn1:down_bwd": [["w_in#1"]],
    "ffn1:dwd": [["w_in#2"]],
    "ffn1:dwg": [["ffn1_w_down#0"]],
    "ffn1:dwu": [["ffn1_w_down#1"]],
    "ffn1:gate_up_bwd": [["ffn1_w_gate"]],
    "last": [["ffn1_w_up"]],
}
PARTS = {"w_in@0": ("w_in", 0, 2656), "w_in@1": ("w_in", 2656, 5296),
         "w_in#0": ("w_in", 0, 2656), "w_in#1": ("w_in", 2656, 3984), "w_in#2": ("w_in", 3984, 5296),
         "ffn1_w_down#0": ("ffn1_w_down", 0, 176), "ffn1_w_down#1": ("ffn1_w_down", 176, 352)}


def _parts_of(base, mark):
    return sorted(pn for pn, (b, _, _) in PARTS.items() if b == base and mark in pn)


class Stage:
    def __init__(self, w):
        self.w = w
        self.width = {n: PACK_W if (n, k) in MAIN else FLAT_W for n, k in BIG}
        self.nrows = {n: math.prod(w[n].shape) // self.width[n] for n, _ in BIG}
        self.recv = {}
        self.arrived_parts = {}

    def _rows(self, n):
        return PARTS[n][2] - PARTS[n][1] if n in PARTS else self.nrows[n]

    def _shards(self, tag):
        names_main, names_flat = GATHER_PLAN[tag]

        def pack(n):
            base, r0, r1 = PARTS.get(n, (n, 0, None))
            return _pack_rows(self.w[base], KIND[base], self.width[base])[r0:r1].astype(BF16)
        shards = []
        if names_main:
            pieces = [pack(n) for n in names_main]
            shards.append((jnp.concatenate(pieces, axis=0), [pc.shape[0] for pc in pieces]))
        if names_flat:
            pieces = [pack(n) for n in names_flat]
            if CONV_RIDES_WITH in names_flat:
                pieces.append(_pad_rows(lax.bitcast_convert_type(self.w["conv_w"][0], BF16).reshape(-1, FLAT_W), 16))
            shards.append((jnp.concatenate(pieces, axis=0), [pc.shape[0] for pc in pieces]))
        return shards

    def gather(self, tag):
        return GatherComm(self._shards(tag)) if tag in GATHER_PLAN else None

    def gathered(self, tag, outs, W, p):
        if tag not in GATHER_PLAN:
            return
        names_main, names_flat = GATHER_PLAN[tag]
        outs = list(outs)
        for n in names_main + names_flat:
            rows = outs.pop(0)
            if n in PARTS:
                self.arrived_parts[n] = rows
                base = PARTS[n][0]
                mine = _parts_of(base, "@")
                if not all(pn in self.arrived_parts for pn in mine):
                    continue
                n, rows = base, jnp.concatenate([self.arrived_parts[pn] for pn in mine], axis=1)
            K = self.w[n].shape[1] if KIND[n] == "col" else PACK_W
            W[n] = rows.reshape(-1, K)
        if CONV_RIDES_WITH in names_flat:
            cw = self.w["conv_w"]
            nbits = 2 * math.prod(cw.shape) // FLAT_W
            bits = outs.pop(0)[:, :nbits].reshape((N_DEV,) + cw.shape[1:] + (2,))
            p["conv_w"] = lax.bitcast_convert_type(bits, F32).transpose(1, 0, 2).reshape(cw.shape[1], -1)

    def pieces(self, tag, gw):
        def piece(n):
            if n in PARTS:
                base, r0, r1 = PARTS[n]
                return gw[base].reshape(N_DEV, self.nrows[base], self.width[base])[:, r0:r1]
            return gw[n].reshape(N_DEV, self.nrows[n], self.width[n])
        return [[piece(n) for n in names] for names in SCATTER_PLAN[tag]]

    def scatter(self, tag, gw):
        return ScatterComm(self.pieces(tag, gw)) if tag in SCATTER_PLAN else None

    def scattered(self, tag, outs):
        if tag in SCATTER_PLAN:
            self.recv[tag] = outs


def _pad_rows(a, mult):
    r = (-a.shape[0]) % mult
    return a if r == 0 else jnp.concatenate([a, jnp.zeros((r,) + a.shape[1:], a.dtype)], axis=0)


def _pack_small(vals, loss_row=None, conv_w=None):
    rows = []
    for v in vals:
        f = v.reshape(-1)
        f = jnp.concatenate([f, jnp.zeros(((-f.shape[0]) % 128,), F32)])
        rows.append(f.reshape(-1, 128))
    if conv_w is not None:
        rows.append(conv_w.reshape(-1, 128))
    if loss_row is not None:
        rows.append(loss_row)
    return _pad_rows(jnp.concatenate(rows, axis=0), 8)


def _unpack_small(buf, shapes):
    out, r = [], 0
    for shp in shapes:
        n = math.prod(shp)
        nr = -(-n // 128)
        out.append(buf[r:r + nr].reshape(-1)[:n].reshape(shp))
        r += nr
    return out, r


def _tn(a, b, name, out_dtype=BF16, comm=None):
    M, N = a.shape[1], b.shape[1]
    T = a.shape[0]
    tm = M if M <= 1536 else M // 2
    tk = 1024 if T % 1024 == 0 and T > 1024 else None
    res = mm([[(a, b, "tn")]], [out_dtype], name, tm=tm, tn=N, tk=tk, comm=comm)
    return res[0] if comm is None else (res[0], res[1:])


class NoStage:
    def gather(self, tag):
        return None

    def gathered(self, tag, outs, W, p):
        pass

    def scatter(self, tag, gw):
        return None

    def scattered(self, tag, outs):
        pass


def _ffn_fwd(x, gpre, gpost, W, p, tag, stage, target=None):
    h = rms_fwd(x, gpre, tag + "_pre")

    def swi(accs, ex):
        sil, dsil = _silu_parts(accs[0])
        return sil, accs[1] * dsil, sil * accs[1]
    G, U, A, *arrived = mm([[(h, W[tag + "_w_gate"], "nt")], [(h, W[tag + "_w_up"], "nt")]], [BF16, BF16, BF16], tag + "_gate_up",
                           tn=DFF // 2, epi=swi, comm=stage.gather(tag + "_gate_up"), sub=4 if h.shape[0] % 1024 == 0 else 1)
    stage.gathered(tag + "_gate_up", arrived, W, p)
    H, y, *rest = mm_resid(A, W[tag + "_w_down"], x, gpost, FFN_RES, tag + "_down", comm=stage.gather(tag + "_down"), target=target)
    saved = (x, h, G, U, A, H)
    if target is not None:
        return y, saved, rest[0]
    stage.gathered(tag + "_down", rest, W, p)
    return y, saved


def _ffn_bwd(dy, saved, gpre, gpost, wg_t, wu_t, wd, tag, stage, gw):
    x, h, G, U, A, H = saved
    dH, dgpost = resid_bwd(H, gpost, dy, FFN_RES, tag + "_post_bwd")

    def dswi(accs, ex):
        return accs[0] * ex[1], accs[0] * ex[0]

    def hosted(where, call):
        comm = stage.scatter(tag + ":" + where, gw)
        res = call(comm)
        if comm is None:
            return res
        stage.scattered(tag + ":" + where, res[1])
        return res[0]

    res = hosted("down_bwd", lambda comm: (lambda r: r if comm is None else (r[:2], r[2:]))(
        mm([[(dH, wd, "nt")]], [BF16, BF16], tag + "_down_bwd", tn=DFF // 2, epi=dswi, extras=[G, U], comm=comm,
           sub=4 if dH.shape[0] % 1024 == 0 else 1)))
    dG, dU = res
    gw[tag + "_w_down"] = hosted("dwd", lambda comm: _tn(A, dH, tag + "_dwd", comm=comm))
    gw[tag + "_w_gate"] = hosted("dwg", lambda comm: _tn(dG, h, tag + "_dwg", comm=comm))
    gw[tag + "_w_up"] = hosted("dwu", lambda comm: _tn(dU, h, tag + "_dwu", comm=comm))
    dx, dgpre = hosted("gate_up_bwd", lambda comm: (lambda r: r[:2] if comm is None else (r[:2], r[2:]))(
        mm_rms_bwd([(dG, wg_t, "nn"), (dU, wu_t, "nn")], x, gpre, tag + "_gate_up_bwd", resid=dy, comm=comm)))
    return dx, dgpre, dgpost


def _local_step(x, mem, positions, tgt, W, p, stage=None):
    stage = stage or NoStage()
    nseq = x.shape[0]
    T = nseq * x.shape[1]
    x0 = x.reshape(T, D)
    mem2 = mem.reshape(-1, D)

    x1, ffn1 = _ffn_fwd(x0, p["ffn1_pre_g"], p["ffn1_post_g"], W, p, "ffn1", stage)

    w_in_t = W["w_in"]
    bounds = [0]
    for n in (SSD_INNER, CONV_CH, SSD_H, QR, KVR, ROPE, 2 * D):
        bounds.append(bounds[-1] + n)
    wt_z, wt_xbc, wt_dt, wt_q, wt_kv, wt_kr, wt_gate = [w_in_t[bounds[i]:bounds[i + 1]] for i in range(7)]
    wt_dt, wt_kr = _pad_rows(wt_dt, SLOT), _pad_rows(wt_kr, SLOT)
    wt_dtkr = jnp.concatenate([wt_dt, wt_kr], axis=0)
    hm = rms_fwd(x1, p["mix_pre_g"], "mix_pre")
    z = mm1(hm, wt_z, "nt", BF16, "in_z")
    xbc = mm1(hm, wt_xbc, "nt", BF16, "in_xbc")
    q_c = mm1(hm, wt_q, "nt", F32, "in_q", tn=QR)
    kv_c = mm1(hm, wt_kv, "nt", F32, "in_kv")
    dtkr = mm1(hm, wt_dtkr, "nt", F32, "in_dtkr")
    gl = mm1(hm, wt_gate, "nt", BF16, "in_gate")

    xbc_act = conv_fwd(xbc, p["conv_w"], p["conv_b"], nseq)
    y_ssd_core, prev, *arrived = ssd_fwd(xbc_act, dtkr, p["dt_bias"], p["a_log"], p["d_skip"], nseq, comm=stage.gather("ssd_fwd"))
    stage.gathered("ssd_fwd", arrived, W, p)
    yn = gated_norm_fwd(y_ssd_core, z, p["ssd_norm_g"], "ssd_norm")
    y_ssd = mm1(yn, W["w_ssd_proj"], "nn", BF16, "ssd_proj")

    slot_rows = lambda wt, per: jnp.pad(wt.reshape(MLA_H, per, -1), ((0, 0), (0, SLOT - per), (0, 0))).reshape(MLA_H * SLOT, -1)
    wq_s, wk_s, wv_s = slot_rows(W["w_uq"], QK), slot_rows(W["w_uk"], NOPE), slot_rows(W["w_uv"], VD)
    wo_s = slot_rows(W["w_mla_proj"], VD)
    qn = rms_fwd(q_c, p["q_norm_g"], "q_norm")
    rope_c, rope_s = rope_table(*_rope_inputs(positions))
    rope_args = [("rows", rope_c), ("rows", rope_s)]
    Qc, = mm([[(qn, wq_s, "nt")]], [BF16], "uq", epi=rope_q_epilogue, extras=rope_args, sub=4 if T % 1024 == 0 else 1)
    kvn = rms_fwd(kv_c, p["kv_norm_g"], "kv_norm")
    Kc, = mm([[(kvn, wk_s, "nt")]], [BF16], "uk", epi=rope_k_epilogue, extras=rope_args + [("rows", dtkr)],
             sub=4 if T % 1024 == 0 else 1)
    v_s = mm1(kvn, wv_s, "nt", BF16, "uv")
    o_s, lse, *arrived = attn_slot_fwd(Qc, Kc, v_s, nseq, comm=stage.gather("attn_fwd"))
    stage.gathered("attn_fwd", arrived, W, p)
    y_mla = mm1(o_s, wo_s, "nn", BF16, "mla_proj")

    merged = merge_fwd(gl, y_ssd, y_mla, p["gate_bias"], "merge")
    hmix, x2 = mm_resid(merged, W["w_out"], x1, p["mix_post_g"], 1.0, "mix_out")

    hq = rms_fwd(x2, p["xa_pre_g"], "xa_pre")
    mn = rms_fwd(mem2, p["mem_norm_g"], "mem_norm")
    xq = mm1(hq, W["w_xq"], "nn", BF16, "xq")
    xk = mm1(mn, W["w_xk"], "nn", BF16, "xk")
    xv = mm1(mn, W["w_xv"], "nn", BF16, "xv")
    xo = xattn_fwd(xq, xk, xv, nseq)
    ho, x3 = mm_resid(xo, W["w_xo"], x2, p["xa_post_g"], 1.0, "xo")

    dx4, ffn2, sq_cols = _ffn_fwd(x3, p["ffn2_pre_g"], p["ffn2_post_g"], W, p, "ffn2", stage, target=tgt.reshape(T, D))
    loss_row = (0.5 / D) * jnp.sum(sq_cols.reshape(-1, 128), axis=0, keepdims=True)

    gw, gs = {}, {}
    dx3, gs["ffn2_pre_g"], gs["ffn2_post_g"] = _ffn_bwd(
        dx4, ffn2, p["ffn2_pre_g"], p["ffn2_post_g"], W["ffn2_w_gate"], W["ffn2_w_up"], W["ffn2_w_down"], "ffn2", stage, gw)

    dho, gs["xa_post_g"] = resid_bwd(ho, p["xa_post_g"], dx3, 1.0, "xa_post_bwd")
    dxo = mm1(dho, W["w_xo"], "nt", BF16, "xo_bwd")
    gw["w_xo"] = _tn(xo, dho, "d_w_xo")
    dxq, dxk, dxv = xattn_bwd(xq, xk, xv, dxo, nseq)
    dx2, gs["xa_pre_g"] = mm_rms_bwd([(dxq, W["w_xq"], "nt")], x2, p["xa_pre_g"], "xq_bwd", resid=dx3)
    gw["w_xq"] = _tn(hq, dxq, "d_w_xq")
    dmn = mm([[(dxk, W["w_xk"], "nt"), (dxv, W["w_xv"], "nt")]], [F32], "xkv_bwd")[0]
    gw["w_xk"] = _tn(mn, dxk, "d_w_xk")
    gw["w_xv"] = _tn(mn, dxv, "d_w_xv")
    _, gs["mem_norm_g"] = rms_bwd(mem2, p["mem_norm_g"], dmn, "mem_norm_bwd", dx_dtype=BF16)

    dhmix, gs["mix_post_g"] = resid_bwd(hmix, p["mix_post_g"], dx2, 1.0, "mix_post_bwd")
    dmerged = mm1(dhmix, W["w_out"], "nt", F32, "mix_out_bwd")
    gw["w_out"] = _tn(merged, dhmix, "d_w_out")
    dys, dym, dgl, gs["gate_bias"] = merge_bwd(gl, y_ssd, y_mla, dmerged, p["gate_bias"], "merge_bwd")

    unslot = lambda g, per: g.reshape(MLA_H, SLOT, -1)[:, :per].reshape(MLA_H * per, -1)
    do_s = mm1(dym, wo_s, "nt", BF16, "mla_proj_bwd")
    gw["w_mla_proj"] = unslot(_tn(o_s, dym, "d_w_mla_proj"), VD)
    dQc, dKc, dv_s, *sent = attn_slot_bwd(Qc, Kc, v_s, o_s, lse, do_s, nseq, comm=stage.scatter("attn_bwd", gw))
    stage.scattered("attn_bwd", sent)
    dq_s, dkn_s, dkr = rope_slot_bwd(dQc, dKc, rope_c, rope_s, "rope_bwd")
    dq_c, gs["q_norm_g"] = mm_rms_bwd([(dq_s, wq_s, "nn")], q_c, p["q_norm_g"], "uq_bwd", dx_dtype=BF16)
    gw["w_uq"] = unslot(_tn(dq_s, qn, "d_w_uq"), QK)
    dkv_c, gs["kv_norm_g"] = mm_rms_bwd([(dkn_s, wk_s, "nn"), (dv_s, wv_s, "nn")], kv_c, p["kv_norm_g"], "ukv_bwd", dx_dtype=BF16)
    gw["w_uk"] = unslot(_tn(dkn_s, kvn, "d_w_uk"), NOPE)
    gw["w_uv"] = unslot(_tn(dv_s, kvn, "d_w_uv"), VD)

    dyn = mm1(dys, W["w_ssd_proj"], "nt", F32, "ssd_proj_bwd")
    gw["w_ssd_proj"] = _tn(yn, dys, "d_w_ssd_proj")
    dyc, dz, gs["ssd_norm_g"] = gated_norm_bwd(y_ssd_core, z, dyn, p["ssd_norm_g"], "ssd_norm_bwd")
    dxbc_act, ddtr, gs["dt_bias"], gs["a_log"], gs["d_skip"], *sent = ssd_bwd(
        xbc_act, dtkr, p["dt_bias"], p["a_log"], p["d_skip"], prev, dyc, nseq, comm=stage.scatter("ssd_bwd", gw))
    stage.scattered("ssd_bwd", sent)
    dxbc, gs["conv_w"], gs["conv_b"] = conv_bwd(xbc, p["conv_w"], p["conv_b"], dxbc_act, nseq)

    gw["w_in"] = jnp.concatenate([_tn(dz, hm, "d_w_in_z"), _tn(dxbc, hm, "d_w_in_xbc"), _tn(ddtr, hm, "d_w_in_dt")[:SSD_H],
                                  _tn(dq_c, hm, "d_w_in_q"), _tn(dkv_c, hm, "d_w_in_kv"), _tn(dkr, hm, "d_w_in_kr")[:ROPE],
                                  _tn(dgl, hm, "d_w_in_gate")], axis=0)
    dx1, gs["mix_pre_g"], *sent = mm_rms_bwd(
        [(dz, wt_z, "nn"), (dxbc, wt_xbc, "nn"), (ddtr, wt_dt, "nn"), (dq_c, wt_q, "nn"), (dkv_c, wt_kv, "nn"),
         (dkr, wt_kr, "nn"), (dgl, wt_gate, "nn")], x1, p["mix_pre_g"], "in_bwd", resid=dx2, comm=stage.scatter("in_bwd", gw))
    stage.scattered("in_bwd", sent)

    dx0, gs["ffn1_pre_g"], gs["ffn1_post_g"] = _ffn_bwd(
        dx1, ffn1, p["ffn1_pre_g"], p["ffn1_post_g"], W["ffn1_w_gate"], W["ffn1_w_up"], W["ffn1_w_down"], "ffn1", stage, gw)
    return loss_row, dx0.reshape(x.shape), gw, gs


def kernel(x, mem, positions, ffn1_pre_g, ffn1_w_gate, ffn1_w_up, ffn1_w_down, ffn1_post_g, mix_pre_g, w_in, conv_w, conv_b, dt_bias, a_log, d_skip, ssd_norm_g, w_ssd_proj, q_norm_g, w_uq, kv_norm_g, w_uk, w_uv, w_mla_proj, gate_bias, w_out, mix_post_g, xa_pre_g, mem_norm_g, w_xq, w_xk, w_xv, w_xo, xa_post_g, ffn2_pre_g, ffn2_w_gate, ffn2_w_up, ffn2_w_down, ffn2_post_g, loss_target, m_ffn1_pre_g, m_ffn1_w_gate, m_ffn1_w_up, m_ffn1_w_down, m_ffn1_post_g, m_mix_pre_g, m_w_in, m_conv_w, m_conv_b, m_dt_bias, m_a_log, m_d_skip, m_ssd_norm_g, m_w_ssd_proj, m_q_norm_g, m_w_uq, m_kv_norm_g, m_w_uk, m_w_uv, m_w_mla_proj, m_gate_bias, m_w_out, m_mix_post_g, m_xa_pre_g, m_mem_norm_g, m_w_xq, m_w_xk, m_w_xv, m_w_xo, m_xa_post_g, m_ffn2_pre_g, m_ffn2_w_gate, m_ffn2_w_up, m_ffn2_w_down, m_ffn2_post_g, v_ffn1_pre_g, v_ffn1_w_gate, v_ffn1_w_up, v_ffn1_w_down, v_ffn1_post_g, v_mix_pre_g, v_w_in, v_conv_w, v_conv_b, v_dt_bias, v_a_log, v_d_skip, v_ssd_norm_g, v_w_ssd_proj, v_q_norm_g, v_w_uq, v_kv_norm_g, v_w_uk, v_w_uv, v_w_mla_proj, v_gate_bias, v_w_out, v_mix_post_g, v_xa_pre_g, v_mem_norm_g, v_w_xq, v_w_xk, v_w_xv, v_w_xo, v_xa_post_g, v_ffn2_pre_g, v_ffn2_w_gate, v_ffn2_w_up, v_ffn2_w_down, v_ffn2_post_g):
    a = dict(locals())
    w = {n: a[n] for n in WEIGHTS}
    m = {n: a["m_" + n] for n in WEIGHTS}
    v = {n: a["v_" + n] for n in WEIGHTS}

    stage = Stage(w)
    W, p = {}, {n: w[n] for n in SMALL}
    stage.gathered("first", run_comm(stage.gather("first"), "allgather_first"), W, p)

    loss_row, grad_x, gw, gs = _local_step(x, mem, positions, loss_target, W, p, stage)

    last_piece, = stage.pieces(LAST_EXCHANGE, gw)[0]
    sems, last_piece, landed, token = exchange_start(last_piece, gs["ffn1_pre_g"])
    sm = _pack_small([gs[n] for n in SMALL], loss_row=loss_row, conv_w=gs["conv_w"])
    srecv, = run_comm(ScatterComm([[jnp.broadcast_to(sm[None], (N_DEV,) + sm.shape)]]), "exchange_small")
    s_rows = sum_slots(srecv, "sum_small", tr=sm.shape[0])
    grads, delta, new_m, new_v = {}, {}, {}, {}
    raw_results = []

    def finish(n, buf, piece, token=None):
        col = KIND[n] == "col"
        turn = (lambda t: t.T) if col else (lambda t: t)
        K = w[n].shape[1]
        if col and buf.shape[2] != K:
            buf = buf.reshape(buf.shape[0], -1, K)
        res = adamw_from_slots(buf, piece, turn(w[n][0]), turn(m[n][0]), turn(v[n][0]), "adamw_" + n, token=token)
        raw_results.append(res[3])
        grads[n], delta[n], new_m[n], new_v[n] = [turn(r)[None] for r in res]

    parts = {}
    for tag, groups in SCATTER_PLAN.items():
        if tag == LAST_EXCHANGE:
            continue
        for names, buf in zip(groups, stage.recv[tag]):
            for piece, n in enumerate(names):
                if n in PARTS:
                    parts[n] = sum_slots(buf, "sum_" + n.replace("#", "_"), tr=buf.shape[1])
                else:
                    finish(n, buf, piece, token)
    for base in sorted({PARTS[pn][0] for pn in parts}):
        rows = jnp.concatenate([parts[pn] for pn in _parts_of(base, "#")], axis=0)
        finish(base, rows[None], 0, token)
    landed = exchange_wait(sems, last_piece, landed, after=raw_results[-1])
    me = _dev_index()
    landed = lax.dynamic_update_index_in_dim(landed, lax.dynamic_index_in_dim(last_piece, me, 0, keepdims=False), me, 0)
    finish(SCATTER_PLAN[LAST_EXCHANGE][0][0], landed, 0)
    conv_w_full = p["conv_w"]
    small = adamw_small(s_rows, [w[n] for n in SMALL], [m[n] for n in SMALL], [v[n] for n in SMALL])
    for t, vals in zip((grads, delta, new_m, new_v), small):
        t.update(zip(SMALL, vals))
    r1 = sum(-(-w[n].shape[1] // 128) for n in SMALL)
    ncw = math.prod(conv_w_full.shape) // 128
    cw_grad_full = s_rows[r1:r1 + ncw].reshape(conv_w_full.shape)
    wsh = conv_w.shape[2]
    grads["conv_w"] = lax.dynamic_slice_in_dim(cw_grad_full, _dev_index() * wsh, wsh, axis=1)[None]
    loss = jnp.sum(s_rows[r1 + ncw])
    d_, m_, v_ = adamw(conv_w[0], grads["conv_w"][0], m["conv_w"][0], v["conv_w"][0], "adamw_conv_w")
    delta["conv_w"], new_m["conv_w"], new_v["conv_w"] = d_[None], m_[None], v_[None]
    return (loss, grad_x, *[grads[n] for n in WEIGHTS], *[delta[n] for n in WEIGHTS],
            *[new_m[n] for n in WEIGHTS], *[new_v[n] for n in WEIGHTS])
```

```python
import functools
import math

import jax
import jax.numpy as jnp
from jax import lax
from jax.experimental import pallas as pl
from jax.experimental.pallas import tpu as pltpu

F32, BF16 = jnp.float32, jnp.bfloat16
HI = lax.Precision.HIGHEST
MESH = pl.DeviceIdType.MESH
N_DEV = 8

D = 1024
DFF = 2816
SSD_H, SSD_P, SSD_G, SSD_N, SSD_L = 16, 64, 2, 128, 128
SSD_INNER = SSD_H * SSD_P
CONV_K, CONV_CH = 4, 1536
MLA_H, QR, KVR, NOPE, ROPE, VD = 16, 384, 256, 64, 32, 64
QK = NOPE + ROPE
ROPE_THETA = 10000.0
XA_H, XA_D = 4, 256
EPS = 1e-6
FFN_RES = 0.5
LR, B1, B2, AEPS, WD, STEP = 0.001, 0.9, 0.999, 1e-08, 0.01, 10

VMEM_LIMIT = 56 * 2**20


def _cp(*sem):
    return pltpu.CompilerParams(dimension_semantics=sem, vmem_limit_bytes=VMEM_LIMIT)


def _sigmoid(x):
    return 1.0 / (1.0 + jnp.exp(-x))


def _softplus(x):
    return jnp.where(x > 20.0, x, jnp.log(1.0 + jnp.exp(jnp.minimum(x, 20.0))))


def _dot(a, b, dims="nn"):
    ca = 0 if dims[0] == "t" else 1
    cb = 1 if dims[1] == "t" else 0
    return lax.dot_general(a.astype(BF16), b.astype(BF16), (((ca,), (cb,)), ((), ())), preferred_element_type=F32)


def _dot_sel(a, b, dims="nn", split="a", terms=3):
    r = (a if split == "a" else b).astype(F32)
    out = None
    for t in range(terms):
        piece = r.astype(BF16)
        if t + 1 < terms:
            r = r - piece.astype(F32)
        d = _dot(piece, b, dims) if split == "a" else _dot(a, piece, dims)
        out = d if out is None else out + d
    return out


def _ssd_common(dtr, dtb, alog):
    L = dtr.shape[0]
    dt = _softplus(dtr + dtb)
    a = -jnp.exp(alog)
    adt = dt * a
    r = lax.broadcasted_iota(jnp.int32, (L, L), 0)
    c = lax.broadcasted_iota(jnp.int32, (L, L), 1)
    lower = r >= c
    tri = lower.astype(F32)
    cs = _dot_sel(tri, adt, "nn", split="b")
    cs_t = _dot_sel(adt, tri, "tt")
    return dt, a, cs, cs_t, lower


def _head_expand():
    hh = lax.broadcasted_iota(jnp.int32, (SSD_H, SSD_INNER), 0)
    jj = lax.broadcasted_iota(jnp.int32, (SSD_H, SSD_INNER), 1)
    return ((jj >= hh * SSD_P) & (jj < hh * SSD_P + SSD_P)).astype(F32)


def _head_reduce():
    hh = lax.broadcasted_iota(jnp.int32, (SSD_INNER, SSD_H), 1)
    jj = lax.broadcasted_iota(jnp.int32, (SSD_INNER, SSD_H), 0)
    return ((jj >= hh * SSD_P) & (jj < hh * SSD_P + SSD_P)).astype(F32)


def ssd_fwd(xbc, dtr, dtb, alog, dsk, nseq, comm=None):
    T = xbc.shape[0]
    S = T // nseq
    C = S // SSD_L
    L = SSD_L
    NP = SSD_H // 2

    def body(x_ref, b_ref, c_ref, dtr_ref, dtb_ref, alog_ref, dsk_ref, y_ref, prev_ref, st_ref):
        ci = pl.program_id(1)

        @pl.when(ci == 0)
        def _():
            st_ref[...] = jnp.zeros_like(st_ref)

        dt, a, cs, cs_t, lower = _ssd_common(dtr_ref[:, 0:SSD_H], dtb_ref[...], alog_ref[...])
        E = _head_expand()
        X = x_ref[...].astype(F32)
        dt_e = _dot_sel(dt, E)
        cs_e = _dot_sel(cs, E)
        csl_e = cs_e[L - 1:L, :]
        Xd = X * dt_e
        Xf = Xd * jnp.exp(csl_e - cs_e)
        e_e = jnp.exp(cs_e)
        skip = _dot_sel(dsk_ref[...], E) * X
        lane = lax.broadcasted_iota(jnp.int32, (1, 2 * SSD_P), 1)
        rowp = lax.broadcasted_iota(jnp.int32, (2 * SSD_P, 1), 0)
        for g in range(SSD_G):
            Bg = b_ref[:, g * SSD_N:(g + 1) * SSD_N]
            Cg = c_ref[:, g * SSD_N:(g + 1) * SSD_N]
            cb = _dot(Cg, Bg, "nt")
            for pp in range(NP // SSD_G):
                p = g * (NP // SSD_G) + pp
                sl = slice(p * 2 * SSD_P, (p + 1) * 2 * SSD_P)
                Xd_p = Xd[:, sl]
                yd = jnp.zeros((L, 2 * SSD_P), F32)
                for q in range(2):
                    h = 2 * p + q
                    m = jnp.where(lower, jnp.exp(jnp.minimum(cs[:, h:h + 1] - cs_t[h:h + 1, :], 0.0)), 0.0)
                    mask = (lane >= q * SSD_P) & (lane < (q + 1) * SSD_P)
                    yd = yd + _dot(cb * m, jnp.where(mask, Xd_p, 0.0))
                S0 = st_ref[p]
                prev_ref[0, 0, p] = S0
                z = _dot(Cg, S0, "nt")
                y_ref[:, sl] = (skip[:, sl] + yd + z * e_e[:, sl]).astype(y_ref.dtype)
                h0 = 2 * p
                dec = jnp.where(rowp < SSD_P, jnp.exp(cs[L - 1:L, h0:h0 + 1]), jnp.exp(cs[L - 1:L, h0 + 1:h0 + 2]))
                st_ref[p] = S0 * dec + _dot(Xf[:, sl], Bg, "tn")

    row = lambda b, c: (b * C + c, 0)
    small = pl.BlockSpec((1, SSD_H), lambda b, c: (0, 0))
    return _call_with_comm(
        body, (nseq, C), "ssd_fwd",
        [pl.BlockSpec((L, SSD_INNER), row),
         pl.BlockSpec((L, SSD_G * SSD_N), lambda b, c: (b * C + c, SSD_INNER // (SSD_G * SSD_N))),
         pl.BlockSpec((L, SSD_G * SSD_N), lambda b, c: (b * C + c, SSD_INNER // (SSD_G * SSD_N) + 1)),
         pl.BlockSpec((L, 128), row), small, small, small],
        [xbc, xbc, xbc, dtr, dtb, alog, dsk],
        [pl.BlockSpec((L, SSD_INNER), row), pl.BlockSpec((1, 1, NP, 2 * SSD_P, SSD_N), lambda b, c: (b, c, 0, 0, 0))],
        [jax.ShapeDtypeStruct((T, SSD_INNER), BF16), jax.ShapeDtypeStruct((nseq, C, NP, 2 * SSD_P, SSD_N), F32)],
        comm, scratch=[pltpu.VMEM((NP, 2 * SSD_P, SSD_N), F32)], sem=("parallel", "arbitrary"))


def ssd_bwd(xbc, dtr, dtb, alog, dsk, prev, dy, nseq, comm=None):
    T = xbc.shape[0]
    S = T // nseq
    C = S // SSD_L
    L = SSD_L
    NP = SSD_H // 2

    def body(x_ref, b_ref, c_ref, dtr_ref, dtb_ref, alog_ref, dsk_ref, prev_ref, dy_ref,
             dxbc_ref, ddtr_ref, ddtb_ref, dalog_ref, ddsk_ref, ds_ref, stg_ref):
        bi = pl.program_id(0)
        ci = pl.program_id(1)

        @pl.when(ci == 0)
        def _():
            ds_ref[...] = jnp.zeros_like(ds_ref)

        @pl.when((ci == 0) & (bi == 0))
        def _():
            ddtb_ref[...] = jnp.zeros_like(ddtb_ref)
            dalog_ref[...] = jnp.zeros_like(dalog_ref)
            ddsk_ref[...] = jnp.zeros_like(ddsk_ref)

        dtr = dtr_ref[:, 0:SSD_H]
        dtb = dtb_ref[...]
        dt, a, cs, cs_t, lower = _ssd_common(dtr, dtb, alog_ref[...])
        upper = lax.broadcasted_iota(jnp.int32, (L, L), 1) >= lax.broadcasted_iota(jnp.int32, (L, L), 0)
        E = _head_expand()
        ET = _head_reduce()
        X = x_ref[...].astype(F32)
        dY = dy_ref[...].astype(F32)
        dt_e = _dot_sel(dt, E)
        cs_e = _dot_sel(cs, E)
        csl_e = cs_e[L - 1:L, :]
        f_e = jnp.exp(csl_e - cs_e)
        e_e = jnp.exp(cs_e)
        dsk_e = _dot_sel(dsk_ref[...], E)
        Xd = X * dt_e
        Xf = Xd * f_e
        lane = lax.broadcasted_iota(jnp.int32, (1, 2 * SSD_P), 1)
        rowp = lax.broadcasted_iota(jnp.int32, (2 * SSD_P, 1), 0)
        hsel = lax.broadcasted_iota(jnp.int32, (1, SSD_H), 1)
        dcs = jnp.zeros((L, SSD_H), F32)
        dcsl = jnp.zeros((1, SSD_H), F32)
        for g in range(SSD_G):
            Bg = b_ref[:, g * SSD_N:(g + 1) * SSD_N]
            Cg = c_ref[:, g * SSD_N:(g + 1) * SSD_N]
            cb = _dot(Cg, Bg, "nt")
            cbt = _dot(Bg, Cg, "nt")
            dB = jnp.zeros((L, SSD_N), F32)
            dC = jnp.zeros((L, SSD_N), F32)
            for pp in range(NP // SSD_G):
                p = g * (NP // SSD_G) + pp
                sl = slice(p * 2 * SSD_P, (p + 1) * 2 * SSD_P)
                Xd_p = Xd[:, sl]
                dY_p = dY[:, sl]
                dXd_p = jnp.zeros((L, 2 * SSD_P), F32)
                for q in range(2):
                    h = 2 * p + q
                    mask = (lane >= q * SSD_P) & (lane < (q + 1) * SSD_P)
                    col = cs[:, h:h + 1]
                    rw = cs_t[h:h + 1, :]
                    m = jnp.where(lower, jnp.exp(jnp.minimum(col - rw, 0.0)), 0.0)
                    mt = jnp.where(upper, jnp.exp(jnp.minimum(rw - col, 0.0)), 0.0)
                    dYm = jnp.where(mask, dY_p, 0.0)
                    dW = _dot(dYm, Xd_p, "nt")
                    dWt = _dot(Xd_p, dYm, "nt")
                    w = cb * m
                    wt = cbt * mt
                    dC = dC + _dot(dW * m, Bg)
                    dB = dB + _dot(dWt * mt, Cg)
                    dXd_p = dXd_p + jnp.where(mask, _dot(wt, dY_p), 0.0)
                    qcol = jnp.sum(dW * w, axis=1, keepdims=True) - jnp.sum(dWt * wt, axis=1, keepdims=True)
                    dcs = dcs + qcol * (hsel == h).astype(F32)
                S0 = prev_ref[0, 0, p]
                dSn = ds_ref[p]
                dZ = dY_p * e_e[:, sl]
                dC = dC + _dot(dZ, S0)
                h0 = 2 * p
                el0 = jnp.exp(cs[L - 1:L, h0:h0 + 1])
                el1 = jnp.exp(cs[L - 1:L, h0 + 1:h0 + 2])
                dec = jnp.where(rowp < SSD_P, el0, el1)
                ds_ref[p] = dSn * dec + _dot(dZ, Cg, "tn")
                dXf_p = _dot(Bg, dSn, "nt")
                dB = dB + _dot(Xf[:, sl], dSn)
                rs = jnp.sum(dSn * S0, axis=1, keepdims=True)
                s0 = jnp.sum(jnp.where(rowp < SSD_P, rs, 0.0), axis=0, keepdims=True) * el0
                s1 = jnp.sum(jnp.where(rowp >= SSD_P, rs, 0.0), axis=0, keepdims=True) * el1
                dcsl = dcsl + s0 * (hsel == h0).astype(F32) + s1 * (hsel == h0 + 1).astype(F32)
                y_off = _dot(Cg, S0, "nt") * e_e[:, sl]
                t1 = dY_p * y_off - dXf_p * Xf[:, sl]
                r1 = jnp.where(lane < SSD_P, t1, 0.0)
                c0 = jnp.sum(r1, axis=1, keepdims=True)
                c1 = jnp.sum(t1 - r1, axis=1, keepdims=True)
                dcs = dcs + c0 * (hsel == h0).astype(F32) + c1 * (hsel == h0 + 1).astype(F32)
                t2 = dXf_p * Xf[:, sl]
                r2 = jnp.where(lane < SSD_P, t2, 0.0)
                dcsl = dcsl + jnp.sum(r2, keepdims=True) * (hsel == h0).astype(F32) \
                    + jnp.sum(t2 - r2, keepdims=True) * (hsel == h0 + 1).astype(F32)
                stg_ref[:, sl] = dXd_p + dXf_p * f_e[:, sl]
            dxbc_ref[:, SSD_INNER + g * SSD_N:SSD_INNER + (g + 1) * SSD_N] = dB.astype(dxbc_ref.dtype)
            dxbc_ref[:, SSD_INNER + (SSD_G + g) * SSD_N:SSD_INNER + (SSD_G + g + 1) * SSD_N] = dC.astype(dxbc_ref.dtype)
        dXd = stg_ref[...]
        dxbc_ref[:, 0:SSD_INNER] = (dXd * dt_e + dsk_e * dY).astype(dxbc_ref.dtype)
        rowl = lax.broadcasted_iota(jnp.int32, (L, 1), 0)
        dcs = dcs + jnp.where(rowl == L - 1, dcsl, 0.0)
        dalpha = _dot_sel(upper.astype(F32), dcs, split="b")
        ddt = _dot_sel(dXd * X, ET, terms=2) + dalpha * a
        dalog_ref[...] += jnp.sum(dalpha * dt, axis=0, keepdims=True) * a
        ddtr = ddt * _sigmoid(dtr + dtb)
        spread = (lax.broadcasted_iota(jnp.int32, (SSD_H, 128), 0) == lax.broadcasted_iota(jnp.int32, (SSD_H, 128), 1)).astype(F32)
        ddtr_ref[...] = _dot(ddtr, spread).astype(ddtr_ref.dtype)
        ddtb_ref[...] += jnp.sum(ddtr, axis=0, keepdims=True)
        ddsk_ref[...] += jnp.sum(_dot_sel(dY * X, ET, terms=2), axis=0, keepdims=True)

    rowr = lambda b, c: (b * C + (C - 1 - c), 0)
    small = pl.BlockSpec((1, SSD_H), lambda b, c: (0, 0))
    return _call_with_comm(
        body, (nseq, C), "ssd_bwd",
        [pl.BlockSpec((L, SSD_INNER), rowr),
         pl.BlockSpec((L, SSD_G * SSD_N), lambda b, c: (b * C + (C - 1 - c), SSD_INNER // (SSD_G * SSD_N))),
         pl.BlockSpec((L, SSD_G * SSD_N), lambda b, c: (b * C + (C - 1 - c), SSD_INNER // (SSD_G * SSD_N) + 1)),
         pl.BlockSpec((L, 128), rowr), small, small, small,
         pl.BlockSpec((1, 1, NP, 2 * SSD_P, SSD_N), lambda b, c: (b, C - 1 - c, 0, 0, 0)),
         pl.BlockSpec((L, SSD_INNER), rowr)],
        [xbc, xbc, xbc, dtr, dtb, alog, dsk, prev, dy],
        [pl.BlockSpec((L, CONV_CH), rowr), pl.BlockSpec((L, 128), rowr), small, small, small],
        [jax.ShapeDtypeStruct((T, CONV_CH), BF16), jax.ShapeDtypeStruct((T, 128), BF16),
         jax.ShapeDtypeStruct((1, SSD_H), F32), jax.ShapeDtypeStruct((1, SSD_H), F32), jax.ShapeDtypeStruct((1, SSD_H), F32)],
        comm, scratch=[pltpu.VMEM((NP, 2 * SSD_P, SSD_N), F32), pltpu.VMEM((L, SSD_INNER), F32)], sem=("arbitrary", "arbitrary"))


SLOT = 128
ATT_T = 512
ATT_HP = 1
LOG2E = math.log2(math.e)
Q_SCALE = QK ** -0.5 * LOG2E


def _col_to_row(col):
    n = col.shape[0]
    eye = lax.broadcasted_iota(jnp.int32, (n, n), 0) == lax.broadcasted_iota(jnp.int32, (n, n), 1)
    return jnp.sum(jnp.where(eye, col, 0.0), axis=0, keepdims=True)


def attn_slot_fwd(q, k, v, nseq, comm=None):
    T = q.shape[0]
    S = T // nseq
    t = min(ATT_T, S)
    nb = S // t
    cols = [slice(h * SLOT, (h + 1) * SLOT) for h in range(ATT_HP)]

    def body(q_ref, k_ref, v_ref, o_ref, lse_ref):
        causal = lax.broadcasted_iota(jnp.int32, (t, t), 1) <= lax.broadcasted_iota(jnp.int32, (t, t), 0)
        for qi in range(nb):
            rows = slice(qi * t, (qi + 1) * t)
            state = [None] * ATT_HP
            for kj in range(qi + 1):
                keys = slice(kj * t, (kj + 1) * t)
                for h, c in enumerate(cols):
                    s = _dot(q_ref[rows, c], k_ref[keys, c], "nt")
                    if kj == qi:
                        s = jnp.where(causal, s, -1e30)
                    bm = jnp.max(s, axis=1, keepdims=True)
                    if kj == 0:
                        p = jnp.exp2(s - bm)
                        state[h] = (bm, jnp.sum(p, axis=1, keepdims=True), _dot(p, v_ref[keys, c]))
                    else:
                        m, l, acc = state[h]
                        m_new = jnp.maximum(m, bm)
                        corr = jnp.exp2(m - m_new)
                        p = jnp.exp2(s - m_new)
                        state[h] = (m_new, l * corr + jnp.sum(p, axis=1, keepdims=True), acc * corr + _dot(p, v_ref[keys, c]))
            for h, c in enumerate(cols):
                m, l, acc = state[h]
                o_ref[rows, c] = (acc / l).astype(o_ref.dtype)
                lse_ref[0, h, :, rows] = _col_to_row(m + jnp.log2(l))

    blk = pl.BlockSpec((S, ATT_HP * SLOT), lambda b, h: (b, h))
    return _call_with_comm(
        body, (nseq, MLA_H // ATT_HP), "attn_fwd", [blk, blk, blk], [q, k, v],
        [blk, pl.BlockSpec((1, ATT_HP, 1, S), lambda b, h: (b, h, 0, 0))],
        [jax.ShapeDtypeStruct((T, MLA_H * SLOT), BF16), jax.ShapeDtypeStruct((nseq, MLA_H, 1, S), F32)], comm)


def attn_slot_bwd(q, k, v, o, lse, do, nseq, comm=None):
    T = q.shape[0]
    S = T // nseq
    t = min(ATT_T, S)
    nb = S // t
    scale = QK ** -0.5
    cols = [slice(h * SLOT, (h + 1) * SLOT) for h in range(ATT_HP)]

    def body(q_ref, k_ref, v_ref, o_ref, lse_ref, do_ref, dq_ref, dk_ref, dv_ref, dqa_ref):
        causal_t = lax.broadcasted_iota(jnp.int32, (t, t), 0) <= lax.broadcasted_iota(jnp.int32, (t, t), 1)
        ones = jnp.ones((8, SLOT), F32)
        delta = {}
        for qi in range(nb):
            sl = slice(qi * t, (qi + 1) * t)
            for h, c in enumerate(cols):
                prod = do_ref[sl, c].astype(F32) * o_ref[sl, c].astype(F32)
                delta[h, qi] = _dot_sel(ones, prod, "nt", split="b", terms=2)[0:1, :]
        for kj in range(nb):
            ks = slice(kj * t, (kj + 1) * t)
            dk = [None] * ATT_HP
            dv = [None] * ATT_HP
            for qi in range(kj, nb):
                sl = slice(qi * t, (qi + 1) * t)
                for h, c in enumerate(cols):
                    kb, vb, qb, dob = k_ref[ks, c], v_ref[ks, c], q_ref[sl, c], do_ref[sl, c]
                    st = _dot(kb, qb, "nt")
                    pt = jnp.exp2(st - lse_ref[0, h, :, sl])
                    if qi == kj:
                        pt = jnp.where(causal_t, pt, 0.0)
                    dpt = _dot(vb, dob, "nt")
                    dst = (pt * (dpt - delta[h, qi])).astype(BF16)
                    dvc = _dot(pt, dob)
                    dkc = _dot(dst, qb) * (1.0 / LOG2E)
                    dv[h] = dvc if dv[h] is None else dv[h] + dvc
                    dk[h] = dkc if dk[h] is None else dk[h] + dkc
                    dqc = _dot(dst, kb, "tn") * scale
                    if kj > 0:
                        dqc = dqc + dqa_ref[sl, c]
                    if qi == kj:
                        dq_ref[sl, c] = dqc.astype(dq_ref.dtype)
                    else:
                        dqa_ref[sl, c] = dqc
            for h, c in enumerate(cols):
                dk_ref[ks, c] = dk[h].astype(dk_ref.dtype)
                dv_ref[ks, c] = dv[h].astype(dv_ref.dtype)

    blk = pl.BlockSpec((S, ATT_HP * SLOT), lambda b, h: (b, h))
    lse_spec = pl.BlockSpec((1, ATT_HP, 1, S), lambda b, h: (b, h, 0, 0))
    W = MLA_H * SLOT
    return _call_with_comm(
        body, (nseq, MLA_H // ATT_HP), "attn_bwd", [blk, blk, blk, blk, lse_spec, blk], [q, k, v, o, lse, do], [blk, blk, blk],
        [jax.ShapeDtypeStruct((T, W), BF16)] * 3, comm, scratch=[pltpu.VMEM((S, ATT_HP * SLOT), F32)])


def _rope_coeffs(pos, inv):
    half = ROPE // 2
    ang = pos * inv
    lane = lax.broadcasted_iota(jnp.int32, (1, SLOT), 1)
    sn = jnp.sin(ang)
    C = jnp.where(lane < NOPE, 1.0, jnp.where(lane < QK, jnp.cos(ang), 0.0))
    Sg = jnp.where((lane >= NOPE) & (lane < NOPE + half), -sn, jnp.where((lane >= NOPE + half) & (lane < QK), sn, 0.0))
    return C, Sg


def _rope_inputs(positions):
    half = ROPE // 2
    inv = ROPE_THETA ** (-jnp.arange(0, ROPE, 2, dtype=F32) / ROPE)
    row = jnp.zeros((1, SLOT), F32).at[0, NOPE:NOPE + half].set(inv).at[0, NOPE + half:QK].set(inv)
    return positions.astype(F32).reshape(-1, 1), row


def _place_k_rope(kr_lanes):
    r = lax.broadcasted_iota(jnp.int32, (SLOT, SLOT), 0)
    c = lax.broadcasted_iota(jnp.int32, (SLOT, SLOT), 1)
    return _dot_sel(kr_lanes, ((c == r + NOPE) & (r < ROPE)).astype(F32))


def rope_table(pos, inv):
    return rowwise(_rope_coeffs, [pos], [inv], [(SLOT, F32), (SLOT, F32)], [], "rope_table")


def rope_q_epilogue(accs, ex):
    C, Sg = ex[0], ex[1]
    reps = accs[0].shape[1] // SLOT
    return ((accs[0] * jnp.tile(C, (1, reps)) + _rope_swap(accs[0]) * jnp.tile(Sg, (1, reps))) * Q_SCALE,)


def rope_k_epilogue(accs, ex):
    C, Sg = ex[0], ex[1]
    kr = _place_k_rope(ex[2][:, SLOT:2 * SLOT])
    kr = kr * C + _rope_swap(kr) * Sg
    return (accs[0] + jnp.tile(kr, (1, accs[0].shape[1] // SLOT)),)


def _rope_swap(x):
    W = x.shape[1]
    half = ROPE // 2
    lane = lax.broadcasted_iota(jnp.int32, (1, W), 1) & (SLOT - 1)
    up = pltpu.roll(x, W - half, axis=1)
    dn = pltpu.roll(x, half, axis=1)
    return jnp.where((lane >= NOPE) & (lane < NOPE + half), up, jnp.where((lane >= NOPE + half) & (lane < QK), dn, 0.0))


def rope_slot_bwd(dq, dk, C, Sg, name):
    def fn(dqv, dkv, C, Sg):
        ct, stl = jnp.tile(C, (1, MLA_H)), jnp.tile(Sg, (1, MLA_H))
        dqo = dqv * ct - _rope_swap(dqv) * stl
        tot = dkv[:, 0:SLOT]
        for h in range(1, MLA_H):
            tot = tot + dkv[:, h * SLOT:(h + 1) * SLOT]
        u = tot * C - _rope_swap(tot) * Sg
        r = lax.broadcasted_iota(jnp.int32, (SLOT, SLOT), 0)
        c = lax.broadcasted_iota(jnp.int32, (SLOT, SLOT), 1)
        unplace = ((r == c + NOPE) & (c < ROPE)).astype(F32)
        return dqo, dkv, _dot_sel(u, unplace, terms=2)
    W = MLA_H * SLOT
    return rowwise(fn, [dq, dk, C, Sg], [], [(W, BF16), (W, BF16), (SLOT, BF16)], [], name)


XA_BLK = 512


def xattn_fwd(q, k, v, nseq):
    T = q.shape[0]
    S = T // nseq
    M = k.shape[0] // nseq
    tq = min(XA_BLK, S)
    nq = S // tq
    scale = XA_D ** -0.5

    def body(q_ref, k_ref, v_ref, o_ref):
        s = _dot(q_ref[...], k_ref[...], "nt") * scale
        p = jnp.exp(s - jnp.max(s, axis=1, keepdims=True))
        p = p / jnp.sum(p, axis=1, keepdims=True)
        o_ref[...] = _dot(p, v_ref[...]).astype(o_ref.dtype)

    qs = pl.BlockSpec((tq, XA_D), lambda b, h, i: (b * nq + i, h))
    ks = pl.BlockSpec((M, XA_D), lambda b, h, i: (b, h))
    return pl.pallas_call(
        body, grid=(nseq, XA_H, nq), name="xattn_fwd", in_specs=[qs, ks, ks], out_specs=qs,
        out_shape=jax.ShapeDtypeStruct((T, XA_H * XA_D), BF16),
        compiler_params=_cp("parallel", "parallel", "parallel"),
    )(q, k, v)


def xattn_bwd(q, k, v, do, nseq):
    T = q.shape[0]
    S = T // nseq
    M = k.shape[0] // nseq
    tq = min(XA_BLK, S)
    nq = S // tq
    scale = XA_D ** -0.5

    def body(q_ref, k_ref, v_ref, do_ref, dq_ref, dk_ref, dv_ref):
        @pl.when(pl.program_id(2) == 0)
        def _():
            dk_ref[...] = jnp.zeros_like(dk_ref)
            dv_ref[...] = jnp.zeros_like(dv_ref)

        qb, kb, vb, dob = q_ref[...], k_ref[...], v_ref[...], do_ref[...]
        s = _dot(qb, kb, "nt") * scale
        p = jnp.exp(s - jnp.max(s, axis=1, keepdims=True))
        p = p / jnp.sum(p, axis=1, keepdims=True)
        dp = _dot(dob, vb, "nt")
        ds = p * (dp - jnp.sum(dp * p, axis=1, keepdims=True)) * scale
        dq_ref[...] = _dot(ds, kb).astype(dq_ref.dtype)
        dk_ref[...] += _dot(ds, qb, "tn")
        dv_ref[...] += _dot(p, dob, "tn")

    qs = pl.BlockSpec((tq, XA_D), lambda b, h, i: (b * nq + i, h))
    ks = pl.BlockSpec((M, XA_D), lambda b, h, i: (b, h))
    return pl.pallas_call(
        body, grid=(nseq, XA_H, nq), name="xattn_bwd", in_specs=[qs, ks, ks, qs], out_specs=[qs, ks, ks],
        out_shape=[jax.ShapeDtypeStruct((T, XA_H * XA_D), BF16), jax.ShapeDtypeStruct(k.shape, F32),
                   jax.ShapeDtypeStruct(k.shape, F32)],
        compiler_params=_cp("parallel", "parallel", "arbitrary"),
    )(q, k, v, do)


CONV_BLK = 256


def _shift_down(x, s, rows):
    if s == 0:
        return x
    return jnp.where(rows >= s, pltpu.roll(x, s, axis=0), 0.0)


def _shift_up(x, s, rows):
    if s == 0:
        return x
    S = x.shape[0]
    return jnp.where(rows < S - s, pltpu.roll(x, S - s, axis=0), 0.0)


def conv_fwd(x, w, b, nseq):
    T, CH = x.shape
    S = T // nseq

    def body(x_ref, w_ref, b_ref, o_ref):
        xv = x_ref[...].astype(F32)
        rows = lax.broadcasted_iota(jnp.int32, (S, 1), 0)
        c = jnp.zeros_like(xv) + b_ref[...]
        for kk in range(CONV_K):
            c = c + w_ref[kk:kk + 1, :] * _shift_down(xv, CONV_K - 1 - kk, rows)
        o_ref[...] = (c * _sigmoid(c)).astype(o_ref.dtype)

    xs = pl.BlockSpec((S, CONV_BLK), lambda j, bb: (bb, j))
    return pl.pallas_call(
        body, grid=(CH // CONV_BLK, nseq), name="conv_fwd",
        in_specs=[xs, pl.BlockSpec((CONV_K, CONV_BLK), lambda j, bb: (0, j)), pl.BlockSpec((1, CONV_BLK), lambda j, bb: (0, j))],
        out_specs=xs, out_shape=jax.ShapeDtypeStruct((T, CH), BF16),
        compiler_params=_cp("parallel", "parallel"),
    )(x, w, b)


def conv_bwd(x, w, b, dout, nseq):
    T, CH = x.shape
    S = T // nseq

    def body(x_ref, w_ref, b_ref, do_ref, dx_ref, dw_ref, db_ref):
        @pl.when(pl.program_id(1) == 0)
        def _():
            dw_ref[...] = jnp.zeros_like(dw_ref)
            db_ref[...] = jnp.zeros_like(db_ref)

        xv = x_ref[...].astype(F32)
        rows = lax.broadcasted_iota(jnp.int32, (S, 1), 0)
        c = jnp.zeros_like(xv) + b_ref[...]
        sh = [_shift_down(xv, CONV_K - 1 - kk, rows) for kk in range(CONV_K)]
        for kk in range(CONV_K):
            c = c + w_ref[kk:kk + 1, :] * sh[kk]
        sg = _sigmoid(c)
        dc = do_ref[...].astype(F32) * sg * (1.0 + c * (1.0 - sg))
        dx = jnp.zeros_like(xv)
        for kk in range(CONV_K):
            dx = dx + w_ref[kk:kk + 1, :] * _shift_up(dc, CONV_K - 1 - kk, rows)
            dw_ref[kk:kk + 1, :] += jnp.sum(dc * sh[kk], axis=0, keepdims=True)
        dx_ref[...] = dx.astype(dx_ref.dtype)
        db_ref[...] += jnp.sum(dc, axis=0, keepdims=True)

    xs = pl.BlockSpec((S, CONV_BLK), lambda j, bb: (bb, j))
    ws = pl.BlockSpec((CONV_K, CONV_BLK), lambda j, bb: (0, j))
    bs = pl.BlockSpec((1, CONV_BLK), lambda j, bb: (0, j))
    return pl.pallas_call(
        body, grid=(CH // CONV_BLK, nseq), name="conv_bwd",
        in_specs=[xs, ws, bs, xs], out_specs=[xs, ws, bs],
        out_shape=[jax.ShapeDtypeStruct((T, CH), BF16), jax.ShapeDtypeStruct((CONV_K, CH), F32),
                   jax.ShapeDtypeStruct((1, CH), F32)],
        compiler_params=_cp("parallel", "arbitrary"),
    )(x, w, b, dout)


def _dims(a, b, mode):
    M = a.shape[1] if mode[0] == "t" else a.shape[0]
    K = a.shape[0] if mode[0] == "t" else a.shape[1]
    N = b.shape[0] if mode[1] == "t" else b.shape[1]
    return M, K, N


def _tile(dim, prefs):
    for p in prefs:
        if dim % p == 0:
            return p
    return dim


def mm(groups, out_dtypes, name, tm=None, tn=None, tk=None, epi=None, extras=(), comm=None, sub=1, n_sum=0):
    a0, b0, m0 = groups[0][0]
    M, K0, N = _dims(a0, b0, m0)
    tm = tm or _tile(M, (1024, 512, 256, 128))
    tn = tn or _tile(N, (1024, 512, 256, 128))
    flat = [p for g in groups for p in g]
    nk = 1 if tk is None else K0 // tk
    in_specs, args = [], []
    for a, b, mode in flat:
        _, K, _ = _dims(a, b, mode)
        kb = K if tk is None else tk
        in_specs.append(pl.BlockSpec((kb, tm), lambda i, j, k: (k, i)) if mode[0] == "t"
                        else pl.BlockSpec((tm, kb), lambda i, j, k: (i, k)))
        in_specs.append(pl.BlockSpec((tn, kb), lambda i, j, k: (j, k)) if mode[1] == "t"
                        else pl.BlockSpec((kb, tn), lambda i, j, k: (k, j)))
        args += [a, b]
    kinds = []
    for e in extras:
        kind, e = e if isinstance(e, tuple) else ("vec" if e.shape[0] == 1 and M != 1 else "tile", e)
        in_specs.append({"tile": pl.BlockSpec((tm, tn), lambda i, j, k: (i, j)),
                         "vec": pl.BlockSpec((1, tn), lambda i, j, k: (0, j)),
                         "rows": pl.BlockSpec((tm, e.shape[1]), lambda i, j, k: (i, 0)),
                         "whole": pl.BlockSpec(e.shape, lambda i, j, k: (0, 0))}[kind])
        kinds.append(kind)
        args.append(e)
    n_in = len(args)
    n_main = len(out_dtypes)
    n_out = n_main + n_sum
    assert n_sum == 0 or (tn == N and tk is None)
    ng = len(groups)
    sizes = [len(g) for g in groups]

    def body(*refs):
        ins, outs, accs = refs[:n_in], refs[n_in:n_in + n_out], refs[n_in + n_out:]
        kk = pl.program_id(2)

        def dots(rs):
            vals, pos = [], 0
            for gi in range(ng):
                acc = None
                for _ in range(sizes[gi]):
                    mode = flat[pos // 2][2]
                    av = ins[pos][:, rs] if mode[0] == "t" else ins[pos][rs, :]
                    d = _dot(av, ins[pos + 1][...], mode)
                    acc = d if acc is None else acc + d
                    pos += 2
                vals.append(acc)
            return vals

        def finish(accv, rs, first_chunk=True):
            ex = [(r[rs, :] if kind in ("tile", "rows") else r[...]).astype(F32) for kind, r in zip(kinds, ins[2 * len(flat):])]
            res = epi(accv, ex) if epi is not None else tuple(accv)
            for o, r in zip(outs[:n_main], res[:n_main]):
                o[rs, :] = r.astype(o.dtype)
            for o, r in zip(outs[n_main:], res[n_main:]):
                if first_chunk:
                    @pl.when(pl.program_id(0) == 0)
                    def _():
                        o[...] = r

                    @pl.when(pl.program_id(0) > 0)
                    def _():
                        o[...] += r
                else:
                    o[...] += r

        if nk == 1:
            for r in range(sub):
                rs = slice(r * (tm // sub), (r + 1) * (tm // sub))
                finish(dots(rs), rs, r == 0)
        else:
            vals = dots(slice(0, tm))
            finish = functools.partial(finish, rs=slice(0, tm))
            @pl.when(kk == 0)
            def _():
                for ar, vv in zip(accs, vals):
                    ar[...] = vv

            @pl.when(kk > 0)
            def _():
                for ar, vv in zip(accs, vals):
                    ar[...] += vv

            @pl.when(kk == nk - 1)
            def _():
                finish([ar[...] for ar in accs])

    grid = (M // tm, N // tn, nk)
    out_specs = [pl.BlockSpec((tm, tn), lambda i, j, k: (i, j)) for _ in out_dtypes] \
        + [pl.BlockSpec((1, tn), lambda i, j, k: (0, j))] * n_sum
    out_shape = [jax.ShapeDtypeStruct((M, N), dt) for dt in out_dtypes] + [jax.ShapeDtypeStruct((1, N), F32)] * n_sum
    scratch = [pltpu.VMEM((tm, tn), F32) for _ in range(ng if nk > 1 else 0)]
    sem = ("arbitrary" if n_sum else "parallel", "parallel", "arbitrary")
    if comm is not None:
        body = _attach(comm, body, n_in, n_out, *_grid_ends(grid))
        in_specs, args = in_specs + [HBM_SPEC] * len(comm.inputs), args + comm.inputs
        out_specs, out_shape = out_specs + [HBM_SPEC] * len(comm.out_shapes), out_shape + comm.out_shapes
        scratch, sem = scratch + comm.sems, ("arbitrary",) * 3
    return pl.pallas_call(body, grid=grid, name=name, in_specs=in_specs, out_specs=out_specs, out_shape=out_shape,
                          scratch_shapes=scratch, compiler_params=_cp(*sem))(*args)


def mm1(a, b, mode, out_dtype, name, **kw):
    return mm([[(a, b, mode)]], [out_dtype], name, **kw)[0]


ROW_BLK = 512


def rowwise(fn, rows, consts, outs, accs, name, tb=ROW_BLK, comm=None):
    rows = [r if isinstance(r, tuple) else (r, r.shape[1], 0) for r in rows]
    T = rows[0][0].shape[0]
    tb = min(tb, T)
    n_r, n_c, n_o, n_a = len(rows), len(consts), len(outs), len(accs)

    def body(*refs):
        vals = [r[...].astype(F32) for r in refs[:n_r + n_c]]
        res = fn(*vals)
        o_refs = refs[n_r + n_c:n_r + n_c + n_o]
        a_refs = refs[n_r + n_c + n_o:]
        for o, r in zip(o_refs, res[:n_o]):
            o[...] = r.astype(o.dtype)
        if n_a:
            @pl.when(pl.program_id(0) == 0)
            def _():
                for ar in a_refs:
                    ar[...] = jnp.zeros_like(ar)
            for ar, r in zip(a_refs, res[n_o:]):
                ar[...] += r

    return _call_with_comm(
        body, (T // tb,), name,
        [pl.BlockSpec((tb, w), functools.partial(lambda i, j: (i, j), j=j)) for _, w, j in rows]
        + [pl.BlockSpec(c.shape, lambda i: (0, 0)) for c in consts],
        [r[0] for r in rows] + list(consts),
        [pl.BlockSpec((tb, d), lambda i: (i, 0)) for d, _ in outs] + [pl.BlockSpec(s, lambda i: (0, 0)) for s in accs],
        [jax.ShapeDtypeStruct((T, d), dt) for d, dt in outs] + [jax.ShapeDtypeStruct(s, F32) for s in accs],
        comm, sem=("arbitrary" if n_a else "parallel",))


def _rms_stats(x):
    r = lax.rsqrt(jnp.mean(x * x, axis=-1, keepdims=True) + EPS)
    return r, x * r


def _rms_bwd(x, g, dy):
    r, xn = _rms_stats(x)
    dyg = dy * g
    dx = r * (dyg - xn * jnp.mean(dyg * xn, axis=-1, keepdims=True))
    return dx, jnp.sum(dy * xn, axis=0, keepdims=True)


def rms_fwd(x, g, name, comm=None):
    res = rowwise(lambda xv, gv: (_rms_stats(xv)[1] * gv,), [x], [g], [(x.shape[1], BF16)], [], name, comm=comm)
    return res[0] if comm is None else (res[0], res[1:])


def rms_bwd(x, g, dy, name, resid=None, dx_dtype=F32):
    def fn(*v):
        if resid is None:
            xv, dyv, gv = v
            dx, dg = _rms_bwd(xv, gv, dyv)
        else:
            xv, dyv, rv, gv = v
            dx, dg = _rms_bwd(xv, gv, dyv)
            dx = dx + rv
        return dx, dg
    rows = [x, dy] + ([] if resid is None else [resid])
    return rowwise(fn, rows, [g], [(x.shape[1], dx_dtype)], [(1, x.shape[1])], name)


def mm_rms_bwd(pairs, x, g, name, resid=None, dx_dtype=F32, comm=None):
    def epi(accs, ex):
        dx, dg = _rms_bwd(ex[0], ex[-1], accs[0])
        return (dx if resid is None else dx + ex[1]), dg
    extras = [x] + ([] if resid is None else [resid]) + [g]
    return mm([pairs], [dx_dtype], name, tm=min(256, x.shape[0]), tn=x.shape[1], epi=epi, extras=extras, comm=comm, n_sum=1)


def mm_resid(a, b, x, g, wgt, name, comm=None, target=None):
    def epi(accs, ex):
        y = ex[0] + wgt * _rms_stats(accs[0])[1] * ex[1]
        if target is None:
            return accs[0], y
        d = y - ex[2]
        return accs[0], d / D, jnp.sum(d * d, axis=0, keepdims=True)
    return mm([[(a, b, "nn")]], [F32, F32], name, tm=min(512, a.shape[0]), tn=b.shape[1], epi=epi,
              extras=[x, g] + ([] if target is None else [target]), sub=2, comm=comm, n_sum=0 if target is None else 1)


def resid_bwd(h, g, dy, wgt, name):
    def fn(hv, dyv, gv):
        dx, dg = _rms_bwd(hv, gv, dyv)
        return wgt * dx, wgt * dg
    return rowwise(fn, [h, dy], [g], [(h.shape[1], BF16)], [(1, h.shape[1])], name)


def _silu_parts(g):
    s = _sigmoid(g)
    return g * s, s * (1.0 + g * (1.0 - s))


def gated_norm_fwd(y, z, g, name):
    W = SSD_INNER // SSD_G

    def fn(yv, zv, gv):
        yg = yv * _silu_parts(zv)[0]
        return (jnp.concatenate([_rms_stats(yg[:, i * W:(i + 1) * W])[1] for i in range(SSD_G)], axis=1) * gv,)
    return rowwise(fn, [y, z], [g], [(SSD_INNER, BF16)], [], name)[0]


def gated_norm_bwd(y, z, dyn, g, name):
    W = SSD_INNER // SSD_G

    def fn(yv, zv, dv, gv):
        sil, dsil = _silu_parts(zv)
        yg = yv * sil
        parts = [_rms_bwd(yg[:, i * W:(i + 1) * W], gv[:, i * W:(i + 1) * W], dv[:, i * W:(i + 1) * W]) for i in range(SSD_G)]
        dyg = jnp.concatenate([p[0] for p in parts], axis=1)
        dg = jnp.concatenate([p[1] for p in parts], axis=1)
        return dyg * sil, dyg * yv * dsil, dg
    return rowwise(fn, [y, z, dyn], [g], [(SSD_INNER, BF16), (SSD_INNER, BF16)], [(1, SSD_INNER)], name)


def merge_fwd(gl, ys, ym, gb, name):
    def fn(glv, ysv, ymv, gbv):
        gt = _sigmoid(glv + gbv)
        return (gt[:, :D] * ysv + gt[:, D:] * ymv,)
    return rowwise(fn, [gl, ys, ym], [gb], [(D, BF16)], [], name)[0]


def merge_bwd(gl, ys, ym, dm, gb, name):
    def fn(glv, ysv, ymv, dmv, gbv):
        gt = _sigmoid(glv + gbv)
        gs, gm = gt[:, :D], gt[:, D:]
        dgl = jnp.concatenate([dmv * ysv * gs * (1.0 - gs), dmv * ymv * gm * (1.0 - gm)], axis=1)
        return dmv * gs, dmv * gm, dgl, jnp.sum(dgl, axis=0, keepdims=True)
    return rowwise(fn, [gl, ys, ym, dm], [gb], [(D, BF16), (D, BF16), (2 * D, BF16)], [(1, 2 * D)], name)


def loss_head(y, tgt, name):
    def fn(yv, tv):
        d = yv - tv
        part = 0.5 * jnp.sum(jnp.sum(d * d, axis=1, keepdims=True), axis=0, keepdims=True) / D
        return d / D, jnp.broadcast_to(part, (1, 128))
    return rowwise(fn, [y, tgt], [], [(D, F32)], [(1, 128)], name)


def _adamw_math(wv, gv, mv, vv):
    mn = B1 * mv + (1.0 - B1) * gv
    vn = B2 * vv + (1.0 - B2) * (gv * gv)
    mh = mn / (1.0 - B1 ** STEP)
    vh = vn / (1.0 - B2 ** STEP)
    return -LR * (mh / (jnp.sqrt(vh) + AEPS) + WD * wv), mn, vn


def adamw(w, g, m, v, name):
    R, C = w.shape
    tb = _tile(R, (256, 128, 64, 32, 16, 8))
    return rowwise(_adamw_math, [w, g, m, v], [], [(C, F32)] * 3, [], name, tb=tb)


def adamw_small(packed, ws, ms, vs):
    k = len(ws)
    sizes = [x.shape[1] for x in ws]

    def body(*refs):
        p_ref, w_refs, m_refs, v_refs = refs[0], refs[1:1 + k], refs[1 + k:1 + 2 * k], refs[1 + 2 * k:1 + 3 * k]
        outs = refs[1 + 3 * k:]
        r0 = 0
        for i, n in enumerate(sizes):
            nr = -(-n // 128)
            g = jnp.concatenate([p_ref[r0 + r:r0 + r + 1, :] for r in range(nr)], axis=1)[:, :n]
            r0 += nr
            outs[i][...] = g
            outs[k + i][...], outs[2 * k + i][...], outs[3 * k + i][...] = _adamw_math(w_refs[i][...], g, m_refs[i][...], v_refs[i][...])

    res = pl.pallas_call(body, name="adamw_small",
                         out_shape=[jax.ShapeDtypeStruct((1, n), F32) for _ in range(4) for n in sizes])(packed, *ws, *ms, *vs)
    return [res[j * k:(j + 1) * k] for j in range(4)]


def adamw_from_slots(recv, piece, w, m, v, name, token=None):
    K, n = w.shape
    ns = recv.shape[0]
    assert recv.shape[2] == n and recv.shape[1] % K == 0
    tb = _tile(K, (256, 176, 128, 64, 32, 16, 8)) if K % 8 == 0 else K
    r_spec = pl.BlockSpec((ns, tb, n), lambda i: (0, piece * (K // tb) + i, 0))
    w_spec = pl.BlockSpec((tb, n), lambda i: (i, 0))

    def body(r_ref, w_ref, m_ref, v_ref, *rest):
        g_ref, d_ref, mo_ref, vo_ref = rest[-4:]
        g = r_ref[0].astype(F32)
        for s in range(1, ns):
            g = g + r_ref[s].astype(F32)
        g_ref[...] = g
        d_ref[...], mo_ref[...], vo_ref[...] = _adamw_math(w_ref[...], g, m_ref[...], v_ref[...])

    extra = [] if token is None else [token]
    return pl.pallas_call(
        body, grid=(K // tb,), name=name,
        in_specs=[r_spec, w_spec, w_spec, w_spec] + [pl.BlockSpec(t.shape, lambda i: (0, 0)) for t in extra], out_specs=[w_spec] * 4,
        out_shape=[jax.ShapeDtypeStruct((K, n), F32)] * 4, compiler_params=_cp("parallel"),
    )(recv, w, m, v, *extra)


def _me():
    return lax.axis_index("x"), lax.axis_index("y"), lax.axis_index("c")


def _dev_index():
    x, y, c = _me()
    return 4 * x + 2 * y + c


HBM_SPEC = pl.BlockSpec(memory_space=pl.ANY)


class GatherComm:
    def __init__(self, shards):
        self.inputs = [s for s, _ in shards]
        self.rows = [list(r) for _, r in shards]
        n = len(shards)
        self.out_shapes = [jax.ShapeDtypeStruct((N_DEV, r, s.shape[1]), s.dtype) for s, rows in shards for r in rows]
        self.sems = [pltpu.SemaphoreType.DMA((7 * n,)), pltpu.SemaphoreType.DMA((7 * n,)), pltpu.SemaphoreType.DMA((n,))]

    def _plan(self, x_refs, out_refs, sems):
        send_sems, recv_sems, local_sems = sems
        x, y, c = _me()
        me, sibling = (x, y, c), (x, y, 1 - c)
        chips = [(1 - x, y), (x, 1 - y), (1 - x, 1 - y)]
        index = lambda px, py, pc: 4 * px + 2 * py + pc
        mine, first, passed, whole = [], [], [], []
        pos = 0
        for i, rows in enumerate(self.rows):
            kw = lambda k: dict(send_sem=send_sems.at[7 * i + k], recv_sem=recv_sems.at[7 * i + k], device_id_type=MESH)
            r0 = 0
            fwd = [[] for _ in chips]
            for j, nr in enumerate(rows):
                out, src = out_refs[pos + j], x_refs[i].at[pl.ds(r0, nr)]
                mine.append(pltpu.make_async_copy(src, out.at[index(*me)], local_sems.at[i]))
                first.append(pltpu.make_async_remote_copy(src_ref=src, dst_ref=out.at[index(*me)], device_id=sibling, **kw(0)))
                for jj, chip in enumerate(chips):
                    first.append(pltpu.make_async_remote_copy(src_ref=src, dst_ref=out.at[index(*me)], device_id=(*chip, c),
                                                              **kw(1 + jj)))
                    blk = out.at[index(*chip, c)]
                    fwd[jj].append(pltpu.make_async_remote_copy(src_ref=blk, dst_ref=blk, device_id=sibling, **kw(4 + jj)))
                r0 += nr
            passed.append(fwd)
            whole.append([pltpu.make_async_remote_copy(src_ref=x_refs[i], dst_ref=x_refs[i], device_id=sibling, **kw(k))
                          for k in range(7)])
            pos += len(rows)
        return mine, first, passed, whole

    def start(self, x_refs, out_refs, sems):
        mine, first, _, _ = self._plan(x_refs, out_refs, sems)
        for cp in mine + first:
            cp.start()

    def finish(self, x_refs, out_refs, sems):
        _, _, passed, whole = self._plan(x_refs, out_refs, sems)
        local_sems = sems[2]
        for i, fwd in enumerate(passed):
            for jj in range(3):
                whole[i][1 + jj].wait_recv()
                for cp in fwd[jj]:
                    cp.start()
        for i in range(len(passed)):
            whole[i][0].wait_recv()
            for jj in range(3):
                whole[i][4 + jj].wait_recv()
        for i in range(len(passed)):
            for k in range(7):
                whole[i][k].wait_send()
            pltpu.make_async_copy(x_refs[i], x_refs[i], local_sems.at[i]).wait()


def run_comm(comm, name):
    n_in, n_out = len(comm.inputs), len(comm.out_shapes)

    def body(*refs):
        ins, outs, sems = refs[:n_in], refs[n_in:n_in + n_out], refs[n_in + n_out:]
        comm.start(ins, outs, sems)
        comm.finish(ins, outs, sems)

    return pl.pallas_call(body, name=name, out_shape=comm.out_shapes, in_specs=[HBM_SPEC] * n_in,
                          out_specs=[HBM_SPEC] * n_out, scratch_shapes=comm.sems)(*comm.inputs)


def _attach(comm, body, n_in, n_out, first, last):
    if comm is None:
        return body
    ci, co, cs = len(comm.inputs), len(comm.out_shapes), len(comm.sems)

    def wrapped(*refs):
        h_in, c_in = refs[:n_in], refs[n_in:n_in + ci]
        h_out, c_out = refs[n_in + ci:n_in + ci + n_out], refs[n_in + ci + n_out:n_in + ci + n_out + co]
        rest = refs[n_in + ci + n_out + co:]
        h_scr, c_sem = rest[:len(rest) - cs], rest[len(rest) - cs:]

        @pl.when(first())
        def _():
            comm.start(c_in, c_out, c_sem)

        body(*h_in, *h_out, *h_scr)

        @pl.when(last())
        def _():
            comm.finish(c_in, c_out, c_sem)

    return wrapped


def _grid_ends(grid):
    first = lambda: functools.reduce(lambda a, b: a & b, [pl.program_id(i) == 0 for i in range(len(grid))])
    last = lambda: functools.reduce(lambda a, b: a & b, [pl.program_id(i) == g - 1 for i, g in enumerate(grid)])
    return first, last


def _call_with_comm(body, grid, name, in_specs, args, out_specs, out_shape, comm, scratch=(), sem=None):
    sem = sem or ("parallel",) * len(grid)
    scratch = list(scratch)
    if comm is not None:
        body = _attach(comm, body, len(args), len(out_shape), *_grid_ends(grid))
        in_specs, args = in_specs + [HBM_SPEC] * len(comm.inputs), args + comm.inputs
        out_specs, out_shape = out_specs + [HBM_SPEC] * len(comm.out_shapes), out_shape + comm.out_shapes
        scratch, sem = scratch + comm.sems, ("arbitrary",) * len(grid)
    return pl.pallas_call(body, grid=grid, name=name, in_specs=in_specs, out_specs=out_specs, out_shape=out_shape,
                          scratch_shapes=scratch, compiler_params=_cp(*sem))(*args)


class ScatterComm:
    def __init__(self, groups):
        self.sizes = [len(g) for g in groups]
        self.rows = [[pc.shape[1] for pc in g] for g in groups]
        ng = len(groups)
        self.inputs = [pc for g in groups for pc in g]
        self.out_shapes = [jax.ShapeDtypeStruct((N_DEV, sum(self.rows[gi]), g[0].shape[2]), g[0].dtype) for gi, g in enumerate(groups)]
        self.sems = [pltpu.SemaphoreType.DMA((7 * ng,)), pltpu.SemaphoreType.DMA((7 * ng,)), pltpu.SemaphoreType.DMA((ng,))]

    def _peers(self):
        x, y, c = _me()
        out = []
        for k in range(1, N_DEV):
            px = 1 - x if k & 4 else x
            py = 1 - y if k & 2 else y
            pc = 1 - c if k & 1 else c
            out.append((k, 4 * px + 2 * py + pc, dict(device_id=(px, py, pc), device_id_type=MESH)))
        return 4 * x + 2 * y + c, out

    def start(self, ins, outs, sems):
        send_sems, recv_sems, local_sems = sems
        me, peers = self._peers()
        pos = 0
        for gi, size in enumerate(self.sizes):
            for i, pc in enumerate(ins[pos:pos + size]):
                dst = outs[gi].at[me, pl.ds(sum(self.rows[gi][:i]), self.rows[gi][i])]
                pltpu.make_async_copy(pc.at[me], dst, local_sems.at[gi]).start()
                for k, peer, kw in peers:
                    pltpu.make_async_remote_copy(src_ref=pc.at[peer], dst_ref=dst, send_sem=send_sems.at[7 * gi + k - 1],
                                                 recv_sem=recv_sems.at[7 * gi + k - 1], **kw).start()
            pos += size

    def finish(self, ins, outs, sems):
        send_sems, recv_sems, local_sems = sems
        me, peers = self._peers()
        whole = [pltpu.make_async_remote_copy(src_ref=outs[gi].at[peer], dst_ref=outs[gi].at[peer],
                                              send_sem=send_sems.at[7 * gi + k - 1], recv_sem=recv_sems.at[7 * gi + k - 1], **kw)
                 for gi in range(len(self.sizes)) for k, peer, kw in peers]
        for cp in whole:
            cp.wait_recv()
        for cp in whole:
            cp.wait_send()
        for gi in range(len(self.sizes)):
            pltpu.make_async_copy(outs[gi].at[me], outs[gi].at[me], local_sems.at[gi]).wait()


def _peer_list():
    x, y, c = _me()
    out = []
    for k in range(1, N_DEV):
        px = 1 - x if k & 4 else x
        py = 1 - y if k & 2 else y
        pc = 1 - c if k & 1 else c
        out.append((k, 4 * px + 2 * py + pc, dict(device_id=(px, py, pc), device_id_type=MESH)))
    return 4 * x + 2 * y + c, out


SEM_SPEC = pl.BlockSpec(memory_space=pltpu.SEMAPHORE)
HBM_ONLY = pl.BlockSpec(memory_space=pltpu.HBM)
N_SPLIT_SEMS = 2 * (N_DEV - 1)


def exchange_start(piece, after):
    def body(piece_ref, land_ref, after_ref, *outs):
        sems, token = outs[:N_SPLIT_SEMS], outs[-1]
        me, peers = _peer_list()
        for k, peer, kw in peers:
            pltpu.make_async_remote_copy(src_ref=piece_ref.at[peer], dst_ref=land_ref.at[me], send_sem=sems[k - 1],
                                         recv_sem=sems[N_DEV - 2 + k], **kw).start()
        token[...] = jnp.zeros_like(token)

    res = pl.pallas_call(
        body, name="exchange_last_start",
        out_shape=(pltpu.SemaphoreType.DMA(()),) * N_SPLIT_SEMS + (pltpu.HBM(piece.shape, piece.dtype), pltpu.HBM(piece.shape, piece.dtype),
                                                                   jax.ShapeDtypeStruct((8, 128), F32)),
        in_specs=(HBM_ONLY, HBM_ONLY, HBM_SPEC),
        out_specs=(SEM_SPEC,) * N_SPLIT_SEMS + (HBM_ONLY, HBM_ONLY, pl.BlockSpec(memory_space=pltpu.VMEM)),
        input_output_aliases={0: N_SPLIT_SEMS, 1: N_SPLIT_SEMS + 1},
        compiler_params=pltpu.CompilerParams(has_side_effects=pltpu.SideEffectType.DATAFLOW_SIDE_EFFECTING),
    )(pltpu.with_memory_space_constraint(piece, pltpu.HBM),
      pltpu.with_memory_space_constraint(lax.empty(piece.shape, piece.dtype), pltpu.HBM), after)
    return res[:N_SPLIT_SEMS], res[N_SPLIT_SEMS], res[N_SPLIT_SEMS + 1], res[N_SPLIT_SEMS + 2]


def exchange_wait(sems, piece, land, after):
    def body(piece_ref, land_ref, *rest):
        sem_refs = rest[:N_SPLIT_SEMS]
        me, peers = _peer_list()
        for k, peer, kw in peers:
            cp = pltpu.make_async_remote_copy(src_ref=piece_ref.at[peer], dst_ref=land_ref.at[peer], send_sem=sem_refs[k - 1],
                                              recv_sem=sem_refs[N_DEV - 2 + k], **kw)
            cp.wait_send()
            cp.wait_recv()

    return pl.pallas_call(
        body, name="exchange_last_wait",
        out_shape=(pltpu.HBM(piece.shape, piece.dtype), pltpu.HBM(land.shape, land.dtype)),
        in_specs=(HBM_ONLY, HBM_ONLY) + (SEM_SPEC,) * N_SPLIT_SEMS + (HBM_SPEC,), out_specs=(HBM_ONLY, HBM_ONLY),
        input_output_aliases={0: 0, 1: 1},
        compiler_params=pltpu.CompilerParams(has_side_effects=pltpu.SideEffectType.DATAFLOW_SIDE_EFFECTING),
    )(piece, land, *sems, after)[1]


def sum_slots(recv, name, tr):
    n, R, C = recv.shape

    def body(r_ref, o_ref):
        acc = r_ref[0].astype(F32)
        for s in range(1, n):
            acc = acc + r_ref[s].astype(F32)
        o_ref[...] = acc

    return pl.pallas_call(
        body, grid=(R // tr,), name=name,
        in_specs=[pl.BlockSpec((n, tr, C), lambda i: (0, i, 0))], out_specs=pl.BlockSpec((tr, C), lambda i: (i, 0)),
        out_shape=jax.ShapeDtypeStruct((R, C), F32), compiler_params=_cp("parallel"),
    )(recv)


PACK_W, FLAT_W = 1024, 128
MAIN = [
    ("ffn1_w_gate", "col"), ("ffn1_w_up", "col"), ("ffn1_w_down", "row"),
    ("ffn2_w_gate", "col"), ("ffn2_w_up", "col"), ("ffn2_w_down", "row"),
    ("w_ssd_proj", "row"), ("w_mla_proj", "row"), ("w_out", "row"),
    ("w_xq", "row"), ("w_xk", "row"), ("w_xv", "row"), ("w_xo", "row"),
    ("w_uk", "col"), ("w_uv", "col"),
]
FLAT = [("w_in", "col"), ("w_uq", "col")]
BIG = MAIN + FLAT
SMALL = ["ffn1_pre_g", "ffn1_post_g", "mix_pre_g", "conv_b", "dt_bias", "a_log", "d_skip", "ssd_norm_g", "q_norm_g",
         "kv_norm_g", "gate_bias", "mix_post_g", "xa_pre_g", "mem_norm_g", "xa_post_g", "ffn2_pre_g", "ffn2_post_g"]
WEIGHTS = ['ffn1_pre_g', 'ffn1_w_gate', 'ffn1_w_up', 'ffn1_w_down', 'ffn1_post_g', 'mix_pre_g', 'w_in', 'conv_w', 'conv_b',
           'dt_bias', 'a_log', 'd_skip', 'ssd_norm_g', 'w_ssd_proj', 'q_norm_g', 'w_uq', 'kv_norm_g', 'w_uk', 'w_uv',
           'w_mla_proj', 'gate_bias', 'w_out', 'mix_post_g', 'xa_pre_g', 'mem_norm_g', 'w_xq', 'w_xk', 'w_xv', 'w_xo',
           'xa_post_g', 'ffn2_pre_g', 'ffn2_w_gate', 'ffn2_w_up', 'ffn2_w_down', 'ffn2_post_g']


def _pack_rows(w, kind, width):
    m = w[0].T if kind == "col" else w[0]
    return m.reshape(-1, width)


KIND = dict(BIG)
GATHER_PLAN = {
    "ffn1_pre": (["ffn1_w_gate", "ffn1_w_up"], []),
    "ffn1_gate_up": (["ffn1_w_down"], ["w_in@0"]),
    "ffn1_down": ([], ["w_in@1"]),
    "ssd_fwd": (["w_ssd_proj", "w_mla_proj", "w_out", "w_uk", "w_uv"], ["w_uq"]),
    "attn_fwd": (["w_xq", "w_xk", "w_xv", "w_xo", "ffn2_w_gate", "ffn2_w_up", "ffn2_w_down"], []),
}
CONV_RIDES_WITH = "w_in@1"
LAST_EXCHANGE = "last"
SCATTER_PLAN = {
    "attn_bwd": [["ffn2_w_gate", "ffn2_w_up", "ffn2_w_down"], ["w_xq", "w_xk", "w_xv", "w_xo"]],
    "ssd_bwd": [["w_ssd_proj", "w_mla_proj", "w_out"], ["w_uk", "w_uv"], ["w_uq"]],
    "in_bwd": [["w_in#0"]],
    "ffn1:down_bwd": [["w_in#1"]],
    "ffn1:dwd": [["w_in#2"]],
    "ffn1:dwg": [["ffn1_w_down#0"]],
    "ffn1:dwu": [["ffn1_w_down#1"]],
    "ffn1:gate_up_bwd": [["ffn1_w_gate"]],
    "last": [["ffn1_w_up"]],
}
PARTS = {"w_in@0": ("w_in", 0, 2656), "w_in@1": ("w_in", 2656, 5296),
         "w_in#0": ("w_in", 0, 2656), "w_in#1": ("w_in", 2656, 3984), "w_in#2": ("w_in", 3984, 5296),
         "ffn1_w_down#0": ("ffn1_w_down", 0, 176), "ffn1_w_down#1": ("ffn1_w_down", 176, 352)}


def _parts_of(base, mark):
    return sorted(pn for pn, (b, _, _) in PARTS.items() if b == base and mark in pn)


class Stage:
    def __init__(self, w):
        self.w = w
        self.width = {n: PACK_W if (n, k) in MAIN else FLAT_W for n, k in BIG}
        self.nrows = {n: math.prod(w[n].shape) // self.width[n] for n, _ in BIG}
        self.recv = {}
        self.arrived_parts = {}

    def _rows(self, n):
        return PARTS[n][2] - PARTS[n][1] if n in PARTS else self.nrows[n]

    def _shards(self, tag):
        names_main, names_flat = GATHER_PLAN[tag]

        def pack(n):
            base, r0, r1 = PARTS.get(n, (n, 0, None))
            return _pack_rows(self.w[base], KIND[base], self.width[base])[r0:r1].astype(BF16)
        shards = []
        if names_main:
            pieces = [pack(n) for n in names_main]
            shards.append((jnp.concatenate(pieces, axis=0), [pc.shape[0] for pc in pieces]))
        if names_flat:
            pieces = [pack(n) for n in names_flat]
            if CONV_RIDES_WITH in names_flat:
                pieces.append(_pad_rows(lax.bitcast_convert_type(self.w["conv_w"][0], BF16).reshape(-1, FLAT_W), 16))
            shards.append((jnp.concatenate(pieces, axis=0), [pc.shape[0] for pc in pieces]))
        return shards

    def gather(self, tag):
        return GatherComm(self._shards(tag)) if tag in GATHER_PLAN else None

    def gathered(self, tag, outs, W, p):
        if tag not in GATHER_PLAN:
            return
        names_main, names_flat = GATHER_PLAN[tag]
        outs = list(outs)
        for n in names_main + names_flat:
            rows = outs.pop(0)
            if n in PARTS:
                self.arrived_parts[n] = rows
                base = PARTS[n][0]
                mine = _parts_of(base, "@")
                if not all(pn in self.arrived_parts for pn in mine):
                    continue
                n, rows = base, jnp.concatenate([self.arrived_parts[pn] for pn in mine], axis=1)
            K = self.w[n].shape[1] if KIND[n] == "col" else PACK_W
            W[n] = rows.reshape(-1, K)
        if CONV_RIDES_WITH in names_flat:
            cw = self.w["conv_w"]
            nbits = 2 * math.prod(cw.shape) // FLAT_W
            bits = outs.pop(0)[:, :nbits].reshape((N_DEV,) + cw.shape[1:] + (2,))
            p["conv_w"] = lax.bitcast_convert_type(bits, F32).transpose(1, 0, 2).reshape(cw.shape[1], -1)

    def pieces(self, tag, gw):
        def piece(n):
            if n in PARTS:
                base, r0, r1 = PARTS[n]
                return gw[base].reshape(N_DEV, self.nrows[base], self.width[base])[:, r0:r1]
            return gw[n].reshape(N_DEV, self.nrows[n], self.width[n])
        return [[piece(n) for n in names] for names in SCATTER_PLAN[tag]]

    def scatter(self, tag, gw):
        return ScatterComm(self.pieces(tag, gw)) if tag in SCATTER_PLAN else None

    def scattered(self, tag, outs):
        if tag in SCATTER_PLAN:
            self.recv[tag] = outs


def _pad_rows(a, mult):
    r = (-a.shape[0]) % mult
    return a if r == 0 else jnp.concatenate([a, jnp.zeros((r,) + a.shape[1:], a.dtype)], axis=0)


def _pack_small(vals, loss_row=None, conv_w=None):
    rows = []
    for v in vals:
        f = v.reshape(-1)
        f = jnp.concatenate([f, jnp.zeros(((-f.shape[0]) % 128,), F32)])
        rows.append(f.reshape(-1, 128))
    if conv_w is not None:
        rows.append(conv_w.reshape(-1, 128))
    if loss_row is not None:
        rows.append(loss_row)
    return _pad_rows(jnp.concatenate(rows, axis=0), 8)


def _unpack_small(buf, shapes):
    out, r = [], 0
    for shp in shapes:
        n = math.prod(shp)
        nr = -(-n // 128)
        out.append(buf[r:r + nr].reshape(-1)[:n].reshape(shp))
        r += nr
    return out, r


def _tn(a, b, name, out_dtype=BF16, comm=None):
    M, N = a.shape[1], b.shape[1]
    T = a.shape[0]
    tm = M if M <= 1536 else M // 2
    tk = 1024 if T % 1024 == 0 and T > 1024 else None
    res = mm([[(a, b, "tn")]], [out_dtype], name, tm=tm, tn=N, tk=tk, comm=comm)
    return res[0] if comm is None else (res[0], res[1:])


class NoStage:
    def gather(self, tag):
        return None

    def gathered(self, tag, outs, W, p):
        pass

    def scatter(self, tag, gw):
        return None

    def scattered(self, tag, outs):
        pass


def _ffn_fwd(x, gpre, gpost, W, p, tag, stage, target=None):
    comm = stage.gather(tag + "_pre")
    h = rms_fwd(x, gpre, tag + "_pre", comm=comm)
    if comm is not None:
        h, arrived = h
        stage.gathered(tag + "_pre", arrived, W, p)

    def swi(accs, ex):
        sil, dsil = _silu_parts(accs[0])
        return sil, accs[1] * dsil, sil * accs[1]
    G, U, A, *arrived = mm([[(h, W[tag + "_w_gate"], "nt")], [(h, W[tag + "_w_up"], "nt")]], [BF16, BF16, BF16], tag + "_gate_up",
                           tn=DFF // 2, epi=swi, comm=stage.gather(tag + "_gate_up"), sub=4 if h.shape[0] % 1024 == 0 else 1)
    stage.gathered(tag + "_gate_up", arrived, W, p)
    H, y, *rest = mm_resid(A, W[tag + "_w_down"], x, gpost, FFN_RES, tag + "_down", comm=stage.gather(tag + "_down"), target=target)
    saved = (x, h, G, U, A, H)
    if target is not None:
        return y, saved, rest[0]
    stage.gathered(tag + "_down", rest, W, p)
    return y, saved


def _ffn_bwd(dy, saved, gpre, gpost, wg_t, wu_t, wd, tag, stage, gw):
    x, h, G, U, A, H = saved
    dH, dgpost = resid_bwd(H, gpost, dy, FFN_RES, tag + "_post_bwd")

    def dswi(accs, ex):
        return accs[0] * ex[1], accs[0] * ex[0]

    def hosted(where, call):
        comm = stage.scatter(tag + ":" + where, gw)
        res = call(comm)
        if comm is None:
            return res
        stage.scattered(tag + ":" + where, res[1])
        return res[0]

    res = hosted("down_bwd", lambda comm: (lambda r: r if comm is None else (r[:2], r[2:]))(
        mm([[(dH, wd, "nt")]], [BF16, BF16], tag + "_down_bwd", tn=DFF // 2, epi=dswi, extras=[G, U], comm=comm,
           sub=4 if dH.shape[0] % 1024 == 0 else 1)))
    dG, dU = res
    gw[tag + "_w_down"] = hosted("dwd", lambda comm: _tn(A, dH, tag + "_dwd", comm=comm))
    gw[tag + "_w_gate"] = hosted("dwg", lambda comm: _tn(dG, h, tag + "_dwg", comm=comm))
    gw[tag + "_w_up"] = hosted("dwu", lambda comm: _tn(dU, h, tag + "_dwu", comm=comm))
    dx, dgpre = hosted("gate_up_bwd", lambda comm: (lambda r: r[:2] if comm is None else (r[:2], r[2:]))(
        mm_rms_bwd([(dG, wg_t, "nn"), (dU, wu_t, "nn")], x, gpre, tag + "_gate_up_bwd", resid=dy, comm=comm)))
    return dx, dgpre, dgpost


def _local_step(x, mem, positions, tgt, W, p, stage=None):
    stage = stage or NoStage()
    nseq = x.shape[0]
    T = nseq * x.shape[1]
    x0 = x.reshape(T, D)
    mem2 = mem.reshape(-1, D)

    x1, ffn1 = _ffn_fwd(x0, p["ffn1_pre_g"], p["ffn1_post_g"], W, p, "ffn1", stage)

    w_in_t = W["w_in"]
    bounds = [0]
    for n in (SSD_INNER, CONV_CH, SSD_H, QR, KVR, ROPE, 2 * D):
        bounds.append(bounds[-1] + n)
    wt_z, wt_xbc, wt_dt, wt_q, wt_kv, wt_kr, wt_gate = [w_in_t[bounds[i]:bounds[i + 1]] for i in range(7)]
    wt_dt, wt_kr = _pad_rows(wt_dt, SLOT), _pad_rows(wt_kr, SLOT)
    wt_dtkr = jnp.concatenate([wt_dt, wt_kr], axis=0)
    hm = rms_fwd(x1, p["mix_pre_g"], "mix_pre")
    z = mm1(hm, wt_z, "nt", BF16, "in_z")
    xbc = mm1(hm, wt_xbc, "nt", BF16, "in_xbc")
    q_c = mm1(hm, wt_q, "nt", F32, "in_q", tn=QR)
    kv_c = mm1(hm, wt_kv, "nt", F32, "in_kv")
    dtkr = mm1(hm, wt_dtkr, "nt", F32, "in_dtkr")
    gl = mm1(hm, wt_gate, "nt", BF16, "in_gate")

    xbc_act = conv_fwd(xbc, p["conv_w"], p["conv_b"], nseq)
    y_ssd_core, prev, *arrived = ssd_fwd(xbc_act, dtkr, p["dt_bias"], p["a_log"], p["d_skip"], nseq, comm=stage.gather("ssd_fwd"))
    stage.gathered("ssd_fwd", arrived, W, p)
    yn = gated_norm_fwd(y_ssd_core, z, p["ssd_norm_g"], "ssd_norm")
    y_ssd = mm1(yn, W["w_ssd_proj"], "nn", BF16, "ssd_proj")

    slot_rows = lambda wt, per: jnp.pad(wt.reshape(MLA_H, per, -1), ((0, 0), (0, SLOT - per), (0, 0))).reshape(MLA_H * SLOT, -1)
    wq_s, wk_s, wv_s = slot_rows(W["w_uq"], QK), slot_rows(W["w_uk"], NOPE), slot_rows(W["w_uv"], VD)
    wo_s = slot_rows(W["w_mla_proj"], VD)
    qn = rms_fwd(q_c, p["q_norm_g"], "q_norm")
    rope_c, rope_s = rope_table(*_rope_inputs(positions))
    rope_args = [("rows", rope_c), ("rows", rope_s)]
    Qc, = mm([[(qn, wq_s, "nt")]], [BF16], "uq", epi=rope_q_epilogue, extras=rope_args, sub=4 if T % 1024 == 0 else 1)
    kvn = rms_fwd(kv_c, p["kv_norm_g"], "kv_norm")
    Kc, = mm([[(kvn, wk_s, "nt")]], [BF16], "uk", epi=rope_k_epilogue, extras=rope_args + [("rows", dtkr)],
             sub=4 if T % 1024 == 0 else 1)
    v_s = mm1(kvn, wv_s, "nt", BF16, "uv")
    o_s, lse, *arrived = attn_slot_fwd(Qc, Kc, v_s, nseq, comm=stage.gather("attn_fwd"))
    stage.gathered("attn_fwd", arrived, W, p)
    y_mla = mm1(o_s, wo_s, "nn", BF16, "mla_proj")

    merged = merge_fwd(gl, y_ssd, y_mla, p["gate_bias"], "merge")
    hmix, x2 = mm_resid(merged, W["w_out"], x1, p["mix_post_g"], 1.0, "mix_out")

    hq = rms_fwd(x2, p["xa_pre_g"], "xa_pre")
    mn = rms_fwd(mem2, p["mem_norm_g"], "mem_norm")
    xq = mm1(hq, W["w_xq"], "nn", BF16, "xq")
    xk = mm1(mn, W["w_xk"], "nn", BF16, "xk")
    xv = mm1(mn, W["w_xv"], "nn", BF16, "xv")
    xo = xattn_fwd(xq, xk, xv, nseq)
    ho, x3 = mm_resid(xo, W["w_xo"], x2, p["xa_post_g"], 1.0, "xo")

    dx4, ffn2, sq_cols = _ffn_fwd(x3, p["ffn2_pre_g"], p["ffn2_post_g"], W, p, "ffn2", stage, target=tgt.reshape(T, D))
    loss_row = (0.5 / D) * jnp.sum(sq_cols.reshape(-1, 128), axis=0, keepdims=True)

    gw, gs = {}, {}
    dx3, gs["ffn2_pre_g"], gs["ffn2_post_g"] = _ffn_bwd(
        dx4, ffn2, p["ffn2_pre_g"], p["ffn2_post_g"], W["ffn2_w_gate"], W["ffn2_w_up"], W["ffn2_w_down"], "ffn2", stage, gw)

    dho, gs["xa_post_g"] = resid_bwd(ho, p["xa_post_g"], dx3, 1.0, "xa_post_bwd")
    dxo = mm1(dho, W["w_xo"], "nt", BF16, "xo_bwd")
    gw["w_xo"] = _tn(xo, dho, "d_w_xo")
    dxq, dxk, dxv = xattn_bwd(xq, xk, xv, dxo, nseq)
    dx2, gs["xa_pre_g"] = mm_rms_bwd([(dxq, W["w_xq"], "nt")], x2, p["xa_pre_g"], "xq_bwd", resid=dx3)
    gw["w_xq"] = _tn(hq, dxq, "d_w_xq")
    dmn = mm([[(dxk, W["w_xk"], "nt"), (dxv, W["w_xv"], "nt")]], [F32], "xkv_bwd")[0]
    gw["w_xk"] = _tn(mn, dxk, "d_w_xk")
    gw["w_xv"] = _tn(mn, dxv, "d_w_xv")
    _, gs["mem_norm_g"] = rms_bwd(mem2, p["mem_norm_g"], dmn, "mem_norm_bwd", dx_dtype=BF16)

    dhmix, gs["mix_post_g"] = resid_bwd(hmix, p["mix_post_g"], dx2, 1.0, "mix_post_bwd")
    dmerged = mm1(dhmix, W["w_out"], "nt", F32, "mix_out_bwd")
    gw["w_out"] = _tn(merged, dhmix, "d_w_out")
    dys, dym, dgl, gs["gate_bias"] = merge_bwd(gl, y_ssd, y_mla, dmerged, p["gate_bias"], "merge_bwd")

    unslot = lambda g, per: g.reshape(MLA_H, SLOT, -1)[:, :per].reshape(MLA_H * per, -1)
    do_s = mm1(dym, wo_s, "nt", BF16, "mla_proj_bwd")
    gw["w_mla_proj"] = unslot(_tn(o_s, dym, "d_w_mla_proj"), VD)
    dQc, dKc, dv_s, *sent = attn_slot_bwd(Qc, Kc, v_s, o_s, lse, do_s, nseq, comm=stage.scatter("attn_bwd", gw))
    stage.scattered("attn_bwd", sent)
    dq_s, dkn_s, dkr = rope_slot_bwd(dQc, dKc, rope_c, rope_s, "rope_bwd")
    dq_c, gs["q_norm_g"] = mm_rms_bwd([(dq_s, wq_s, "nn")], q_c, p["q_norm_g"], "uq_bwd", dx_dtype=BF16)
    gw["w_uq"] = unslot(_tn(dq_s, qn, "d_w_uq"), QK)
    dkv_c, gs["kv_norm_g"] = mm_rms_bwd([(dkn_s, wk_s, "nn"), (dv_s, wv_s, "nn")], kv_c, p["kv_norm_g"], "ukv_bwd", dx_dtype=BF16)
    gw["w_uk"] = unslot(_tn(dkn_s, kvn, "d_w_uk"), NOPE)
    gw["w_uv"] = unslot(_tn(dv_s, kvn, "d_w_uv"), VD)

    dyn = mm1(dys, W["w_ssd_proj"], "nt", F32, "ssd_proj_bwd")
    gw["w_ssd_proj"] = _tn(yn, dys, "d_w_ssd_proj")
    dyc, dz, gs["ssd_norm_g"] = gated_norm_bwd(y_ssd_core, z, dyn, p["ssd_norm_g"], "ssd_norm_bwd")
    dxbc_act, ddtr, gs["dt_bias"], gs["a_log"], gs["d_skip"], *sent = ssd_bwd(
        xbc_act, dtkr, p["dt_bias"], p["a_log"], p["d_skip"], prev, dyc, nseq, comm=stage.scatter("ssd_bwd", gw))
    stage.scattered("ssd_bwd", sent)
    dxbc, gs["conv_w"], gs["conv_b"] = conv_bwd(xbc, p["conv_w"], p["conv_b"], dxbc_act, nseq)

    gw["w_in"] = jnp.concatenate([_tn(dz, hm, "d_w_in_z"), _tn(dxbc, hm, "d_w_in_xbc"), _tn(ddtr, hm, "d_w_in_dt")[:SSD_H],
                                  _tn(dq_c, hm, "d_w_in_q"), _tn(dkv_c, hm, "d_w_in_kv"), _tn(dkr, hm, "d_w_in_kr")[:ROPE],
                                  _tn(dgl, hm, "d_w_in_gate")], axis=0)
    dx1, gs["mix_pre_g"], *sent = mm_rms_bwd(
        [(dz, wt_z, "nn"), (dxbc, wt_xbc, "nn"), (ddtr, wt_dt, "nn"), (dq_c, wt_q, "nn"), (dkv_c, wt_kv, "nn"),
         (dkr, wt_kr, "nn"), (dgl, wt_gate, "nn")], x1, p["mix_pre_g"], "in_bwd", resid=dx2, comm=stage.scatter("in_bwd", gw))
    stage.scattered("in_bwd", sent)

    dx0, gs["ffn1_pre_g"], gs["ffn1_post_g"] = _ffn_bwd(
        dx1, ffn1, p["ffn1_pre_g"], p["ffn1_post_g"], W["ffn1_w_gate"], W["ffn1_w_up"], W["ffn1_w_down"], "ffn1", stage, gw)
    return loss_row, dx0.reshape(x.shape), gw, gs


def kernel(x, mem, positions, ffn1_pre_g, ffn1_w_gate, ffn1_w_up, ffn1_w_down, ffn1_post_g, mix_pre_g, w_in, conv_w, conv_b, dt_bias, a_log, d_skip, ssd_norm_g, w_ssd_proj, q_norm_g, w_uq, kv_norm_g, w_uk, w_uv, w_mla_proj, gate_bias, w_out, mix_post_g, xa_pre_g, mem_norm_g, w_xq, w_xk, w_xv, w_xo, xa_post_g, ffn2_pre_g, ffn2_w_gate, ffn2_w_up, ffn2_w_down, ffn2_post_g, loss_target, m_ffn1_pre_g, m_ffn1_w_gate, m_ffn1_w_up, m_ffn1_w_down, m_ffn1_post_g, m_mix_pre_g, m_w_in, m_conv_w, m_conv_b, m_dt_bias, m_a_log, m_d_skip, m_ssd_norm_g, m_w_ssd_proj, m_q_norm_g, m_w_uq, m_kv_norm_g, m_w_uk, m_w_uv, m_w_mla_proj, m_gate_bias, m_w_out, m_mix_post_g, m_xa_pre_g, m_mem_norm_g, m_w_xq, m_w_xk, m_w_xv, m_w_xo, m_xa_post_g, m_ffn2_pre_g, m_ffn2_w_gate, m_ffn2_w_up, m_ffn2_w_down, m_ffn2_post_g, v_ffn1_pre_g, v_ffn1_w_gate, v_ffn1_w_up, v_ffn1_w_down, v_ffn1_post_g, v_mix_pre_g, v_w_in, v_conv_w, v_conv_b, v_dt_bias, v_a_log, v_d_skip, v_ssd_norm_g, v_w_ssd_proj, v_q_norm_g, v_w_uq, v_kv_norm_g, v_w_uk, v_w_uv, v_w_mla_proj, v_gate_bias, v_w_out, v_mix_post_g, v_xa_pre_g, v_mem_norm_g, v_w_xq, v_w_xk, v_w_xv, v_w_xo, v_xa_post_g, v_ffn2_pre_g, v_ffn2_w_gate, v_ffn2_w_up, v_ffn2_w_down, v_ffn2_post_g):
    a = dict(locals())
    w = {n: a[n] for n in WEIGHTS}
    m = {n: a["m_" + n] for n in WEIGHTS}
    v = {n: a["v_" + n] for n in WEIGHTS}

    stage = Stage(w)
    W, p = {}, {n: w[n] for n in SMALL}
    loss_row, grad_x, gw, gs = _local_step(x, mem, positions, loss_target, W, p, stage)

    sm = _pack_small([gs[n] for n in SMALL], loss_row=loss_row, conv_w=gs["conv_w"])
    srecv, = run_comm(ScatterComm([[jnp.broadcast_to(sm[None], (N_DEV,) + sm.shape)]]), "exchange_small")
    s_rows = sum_slots(srecv, "sum_small", tr=sm.shape[0])
    last_piece, = stage.pieces(LAST_EXCHANGE, gw)[0]
    sems, last_piece, landed, token = exchange_start(last_piece, s_rows)
    grads, delta, new_m, new_v = {}, {}, {}, {}
    raw_results = []

    def finish(n, buf, piece, token=None):
        col = KIND[n] == "col"
        turn = (lambda t: t.T) if col else (lambda t: t)
        K = w[n].shape[1]
        if col and buf.shape[2] != K:
            buf = buf.reshape(buf.shape[0], -1, K)
        res = adamw_from_slots(buf, piece, turn(w[n][0]), turn(m[n][0]), turn(v[n][0]), "adamw_" + n, token=token)
        raw_results.append(res[3])
        grads[n], delta[n], new_m[n], new_v[n] = [turn(r)[None] for r in res]

    parts = {}
    for tag, groups in SCATTER_PLAN.items():
        if tag == LAST_EXCHANGE:
            continue
        for names, buf in zip(groups, stage.recv[tag]):
            for piece, n in enumerate(names):
                if n in PARTS:
                    parts[n] = sum_slots(buf, "sum_" + n.replace("#", "_"), tr=buf.shape[1])
                else:
                    finish(n, buf, piece, token)
    for base in sorted({PARTS[pn][0] for pn in parts}):
        rows = jnp.concatenate([parts[pn] for pn in _parts_of(base, "#")], axis=0)
        finish(base, rows[None], 0, token)
    landed = exchange_wait(sems, last_piece, landed, after=raw_results[-1])
    me = _dev_index()
    landed = lax.dynamic_update_index_in_dim(landed, lax.dynamic_index_in_dim(last_piece, me, 0, keepdims=False), me, 0)
    finish(SCATTER_PLAN[LAST_EXCHANGE][0][0], landed, 0)
    conv_w_full = p["conv_w"]
    small = adamw_small(s_rows, [w[n] for n in SMALL], [m[n] for n in SMALL], [v[n] for n in SMALL])
    for t, vals in zip((grads, delta, new_m, new_v), small):
        t.update(zip(SMALL, vals))
    r1 = sum(-(-w[n].shape[1] // 128) for n in SMALL)
    ncw = math.prod(conv_w_full.shape) // 128
    cw_grad_full = s_rows[r1:r1 + ncw].reshape(conv_w_full.shape)
    wsh = conv_w.shape[2]
    grads["conv_w"] = lax.dynamic_slice_in_dim(cw_grad_full, _dev_index() * wsh, wsh, axis=1)[None]
    loss = jnp.sum(s_rows[r1 + ncw])
    d_, m_, v_ = adamw(conv_w[0], grads["conv_w"][0], m["conv_w"][0], v["conv_w"][0], "adamw_conv_w")
    delta["conv_w"], new_m["conv_w"], new_v["conv_w"] = d_[None], m_[None], v_[None]
    return (loss, grad_x, *[grads[n] for n in WEIGHTS], *[delta[n] for n in WEIGHTS],
            *[new_m[n] for n in WEIGHTS], *[new_v[n] for n in WEIGHTS])
```

```python
import functools
import math

import jax
import jax.numpy as jnp
from jax import lax
from jax.experimental import pallas as pl
from jax.experimental.pallas import tpu as pltpu

F32, BF16 = jnp.float32, jnp.bfloat16
HI = lax.Precision.HIGHEST
MESH = pl.DeviceIdType.MESH
N_DEV = 8

D = 1024
DFF = 2816
SSD_H, SSD_P, SSD_G, SSD_N, SSD_L = 16, 64, 2, 128, 128
SSD_INNER = SSD_H * SSD_P
CONV_K, CONV_CH = 4, 1536
MLA_H, QR, KVR, NOPE, ROPE, VD = 16, 384, 256, 64, 32, 64
QK = NOPE + ROPE
ROPE_THETA = 10000.0
XA_H, XA_D = 4, 256
EPS = 1e-6
FFN_RES = 0.5
LR, B1, B2, AEPS, WD, STEP = 0.001, 0.9, 0.999, 1e-08, 0.01, 10

VMEM_LIMIT = 56 * 2**20


def _cp(*sem):
    return pltpu.CompilerParams(dimension_semantics=sem, vmem_limit_bytes=VMEM_LIMIT)


def _sigmoid(x):
    return 1.0 / (1.0 + jnp.exp(-x))


def _softplus(x):
    return jnp.where(x > 20.0, x, jnp.log(1.0 + jnp.exp(jnp.minimum(x, 20.0))))


def _dot(a, b, dims="nn"):
    ca = 0 if dims[0] == "t" else 1
    cb = 1 if dims[1] == "t" else 0
    return lax.dot_general(a.astype(BF16), b.astype(BF16), (((ca,), (cb,)), ((), ())), preferred_element_type=F32)


def _dot_sel(a, b, dims="nn", split="a", terms=3):
    r = (a if split == "a" else b).astype(F32)
    out = None
    for t in range(terms):
        piece = r.astype(BF16)
        if t + 1 < terms:
            r = r - piece.astype(F32)
        d = _dot(piece, b, dims) if split == "a" else _dot(a, piece, dims)
        out = d if out is None else out + d
    return out


def _ssd_common(dtr, dtb, alog):
    L = dtr.shape[0]
    dt = _softplus(dtr + dtb)
    a = -jnp.exp(alog)
    adt = dt * a
    r = lax.broadcasted_iota(jnp.int32, (L, L), 0)
    c = lax.broadcasted_iota(jnp.int32, (L, L), 1)
    lower = r >= c
    tri = lower.astype(F32)
    cs = _dot_sel(tri, adt, "nn", split="b")
    cs_t = _dot_sel(adt, tri, "tt")
    return dt, a, cs, cs_t, lower


def _head_expand():
    hh = lax.broadcasted_iota(jnp.int32, (SSD_H, SSD_INNER), 0)
    jj = lax.broadcasted_iota(jnp.int32, (SSD_H, SSD_INNER), 1)
    return ((jj >= hh * SSD_P) & (jj < hh * SSD_P + SSD_P)).astype(F32)


def _head_reduce():
    hh = lax.broadcasted_iota(jnp.int32, (SSD_INNER, SSD_H), 1)
    jj = lax.broadcasted_iota(jnp.int32, (SSD_INNER, SSD_H), 0)
    return ((jj >= hh * SSD_P) & (jj < hh * SSD_P + SSD_P)).astype(F32)


def ssd_fwd(xbc, dtr, dtb, alog, dsk, nseq, comm=None):
    T = xbc.shape[0]
    S = T // nseq
    C = S // SSD_L
    L = SSD_L
    NP = SSD_H // 2

    def body(x_ref, b_ref, c_ref, dtr_ref, dtb_ref, alog_ref, dsk_ref, y_ref, prev_ref, st_ref):
        ci = pl.program_id(1)

        @pl.when(ci == 0)
        def _():
            st_ref[...] = jnp.zeros_like(st_ref)

        dt, a, cs, cs_t, lower = _ssd_common(dtr_ref[:, 0:SSD_H], dtb_ref[...], alog_ref[...])
        E = _head_expand()
        X = x_ref[...].astype(F32)
        dt_e = _dot_sel(dt, E)
        cs_e = _dot_sel(cs, E)
        csl_e = cs_e[L - 1:L, :]
        Xd = X * dt_e
        Xf = Xd * jnp.exp(csl_e - cs_e)
        e_e = jnp.exp(cs_e)
        skip = _dot_sel(dsk_ref[...], E) * X
        lane = lax.broadcasted_iota(jnp.int32, (1, 2 * SSD_P), 1)
        rowp = lax.broadcasted_iota(jnp.int32, (2 * SSD_P, 1), 0)
        for g in range(SSD_G):
            Bg = b_ref[:, g * SSD_N:(g + 1) * SSD_N]
            Cg = c_ref[:, g * SSD_N:(g + 1) * SSD_N]
            cb = _dot(Cg, Bg, "nt")
            for pp in range(NP // SSD_G):
                p = g * (NP // SSD_G) + pp
                sl = slice(p * 2 * SSD_P, (p + 1) * 2 * SSD_P)
                Xd_p = Xd[:, sl]
                yd = jnp.zeros((L, 2 * SSD_P), F32)
                for q in range(2):
                    h = 2 * p + q
                    m = jnp.where(lower, jnp.exp(jnp.minimum(cs[:, h:h + 1] - cs_t[h:h + 1, :], 0.0)), 0.0)
                    mask = (lane >= q * SSD_P) & (lane < (q + 1) * SSD_P)
                    yd = yd + _dot(cb * m, jnp.where(mask, Xd_p, 0.0))
                S0 = st_ref[p]
                prev_ref[0, 0, p] = S0
                z = _dot(Cg, S0, "nt")
                y_ref[:, sl] = (skip[:, sl] + yd + z * e_e[:, sl]).astype(y_ref.dtype)
                h0 = 2 * p
                dec = jnp.where(rowp < SSD_P, jnp.exp(cs[L - 1:L, h0:h0 + 1]), jnp.exp(cs[L - 1:L, h0 + 1:h0 + 2]))
                st_ref[p] = S0 * dec + _dot(Xf[:, sl], Bg, "tn")

    row = lambda b, c: (b * C + c, 0)
    small = pl.BlockSpec((1, SSD_H), lambda b, c: (0, 0))
    return _call_with_comm(
        body, (nseq, C), "ssd_fwd",
        [pl.BlockSpec((L, SSD_INNER), row),
         pl.BlockSpec((L, SSD_G * SSD_N), lambda b, c: (b * C + c, SSD_INNER // (SSD_G * SSD_N))),
         pl.BlockSpec((L, SSD_G * SSD_N), lambda b, c: (b * C + c, SSD_INNER // (SSD_G * SSD_N) + 1)),
         pl.BlockSpec((L, 128), row), small, small, small],
        [xbc, xbc, xbc, dtr, dtb, alog, dsk],
        [pl.BlockSpec((L, SSD_INNER), row), pl.BlockSpec((1, 1, NP, 2 * SSD_P, SSD_N), lambda b, c: (b, c, 0, 0, 0))],
        [jax.ShapeDtypeStruct((T, SSD_INNER), BF16), jax.ShapeDtypeStruct((nseq, C, NP, 2 * SSD_P, SSD_N), F32)],
        comm, scratch=[pltpu.VMEM((NP, 2 * SSD_P, SSD_N), F32)], sem=("parallel", "arbitrary"))


def ssd_bwd(xbc, dtr, dtb, alog, dsk, prev, dy, nseq, comm=None):
    T = xbc.shape[0]
    S = T // nseq
    C = S // SSD_L
    L = SSD_L
    NP = SSD_H // 2

    def body(x_ref, b_ref, c_ref, dtr_ref, dtb_ref, alog_ref, dsk_ref, prev_ref, dy_ref,
             dxbc_ref, ddtr_ref, ddtb_ref, dalog_ref, ddsk_ref, ds_ref, stg_ref):
        bi = pl.program_id(0)
        ci = pl.program_id(1)

        @pl.when(ci == 0)
        def _():
            ds_ref[...] = jnp.zeros_like(ds_ref)

        @pl.when((ci == 0) & (bi == 0))
        def _():
            ddtb_ref[...] = jnp.zeros_like(ddtb_ref)
            dalog_ref[...] = jnp.zeros_like(dalog_ref)
            ddsk_ref[...] = jnp.zeros_like(ddsk_ref)

        dtr = dtr_ref[:, 0:SSD_H]
        dtb = dtb_ref[...]
        dt, a, cs, cs_t, lower = _ssd_common(dtr, dtb, alog_ref[...])
        upper = lax.broadcasted_iota(jnp.int32, (L, L), 1) >= lax.broadcasted_iota(jnp.int32, (L, L), 0)
        E = _head_expand()
        ET = _head_reduce()
        X = x_ref[...].astype(F32)
        dY = dy_ref[...].astype(F32)
        dt_e = _dot_sel(dt, E)
        cs_e = _dot_sel(cs, E)
        csl_e = cs_e[L - 1:L, :]
        f_e = jnp.exp(csl_e - cs_e)
        e_e = jnp.exp(cs_e)
        dsk_e = _dot_sel(dsk_ref[...], E)
        Xd = X * dt_e
        Xf = Xd * f_e
        lane = lax.broadcasted_iota(jnp.int32, (1, 2 * SSD_P), 1)
        rowp = lax.broadcasted_iota(jnp.int32, (2 * SSD_P, 1), 0)
        hsel = lax.broadcasted_iota(jnp.int32, (1, SSD_H), 1)
        dcs = jnp.zeros((L, SSD_H), F32)
        dcsl = jnp.zeros((1, SSD_H), F32)
        for g in range(SSD_G):
            Bg = b_ref[:, g * SSD_N:(g + 1) * SSD_N]
            Cg = c_ref[:, g * SSD_N:(g + 1) * SSD_N]
            cb = _dot(Cg, Bg, "nt")
            cbt = _dot(Bg, Cg, "nt")
            dB = jnp.zeros((L, SSD_N), F32)
            dC = jnp.zeros((L, SSD_N), F32)
            for pp in range(NP // SSD_G):
                p = g * (NP // SSD_G) + pp
                sl = slice(p * 2 * SSD_P, (p + 1) * 2 * SSD_P)
                Xd_p = Xd[:, sl]
                dY_p = dY[:, sl]
                dXd_p = jnp.zeros((L, 2 * SSD_P), F32)
                for q in range(2):
                    h = 2 * p + q
                    mask = (lane >= q * SSD_P) & (lane < (q + 1) * SSD_P)
                    col = cs[:, h:h + 1]
                    rw = cs_t[h:h + 1, :]
                    m = jnp.where(lower, jnp.exp(jnp.minimum(col - rw, 0.0)), 0.0)
                    mt = jnp.where(upper, jnp.exp(jnp.minimum(rw - col, 0.0)), 0.0)
                    dYm = jnp.where(mask, dY_p, 0.0)
                    dW = _dot(dYm, Xd_p, "nt")
                    dWt = _dot(Xd_p, dYm, "nt")
                    w = cb * m
                    wt = cbt * mt
                    dC = dC + _dot(dW * m, Bg)
                    dB = dB + _dot(dWt * mt, Cg)
                    dXd_p = dXd_p + jnp.where(mask, _dot(wt, dY_p), 0.0)
                    qcol = jnp.sum(dW * w, axis=1, keepdims=True) - jnp.sum(dWt * wt, axis=1, keepdims=True)
                    dcs = dcs + qcol * (hsel == h).astype(F32)
                S0 = prev_ref[0, 0, p]
                dSn = ds_ref[p]
                dZ = dY_p * e_e[:, sl]
                dC = dC + _dot(dZ, S0)
                h0 = 2 * p
                el0 = jnp.exp(cs[L - 1:L, h0:h0 + 1])
                el1 = jnp.exp(cs[L - 1:L, h0 + 1:h0 + 2])
                dec = jnp.where(rowp < SSD_P, el0, el1)
                ds_ref[p] = dSn * dec + _dot(dZ, Cg, "tn")
                dXf_p = _dot(Bg, dSn, "nt")
                dB = dB + _dot(Xf[:, sl], dSn)
                rs = jnp.sum(dSn * S0, axis=1, keepdims=True)
                s0 = jnp.sum(jnp.where(rowp < SSD_P, rs, 0.0), axis=0, keepdims=True) * el0
                s1 = jnp.sum(jnp.where(rowp >= SSD_P, rs, 0.0), axis=0, keepdims=True) * el1
                dcsl = dcsl + s0 * (hsel == h0).astype(F32) + s1 * (hsel == h0 + 1).astype(F32)
                y_off = _dot(Cg, S0, "nt") * e_e[:, sl]
                t1 = dY_p * y_off - dXf_p * Xf[:, sl]
                r1 = jnp.where(lane < SSD_P, t1, 0.0)
                c0 = jnp.sum(r1, axis=1, keepdims=True)
                c1 = jnp.sum(t1 - r1, axis=1, keepdims=True)
                dcs = dcs + c0 * (hsel == h0).astype(F32) + c1 * (hsel == h0 + 1).astype(F32)
                t2 = dXf_p * Xf[:, sl]
                r2 = jnp.where(lane < SSD_P, t2, 0.0)
                dcsl = dcsl + jnp.sum(r2, keepdims=True) * (hsel == h0).astype(F32) \
                    + jnp.sum(t2 - r2, keepdims=True) * (hsel == h0 + 1).astype(F32)
                stg_ref[:, sl] = dXd_p + dXf_p * f_e[:, sl]
            dxbc_ref[:, SSD_INNER + g * SSD_N:SSD_INNER + (g + 1) * SSD_N] = dB.astype(dxbc_ref.dtype)
            dxbc_ref[:, SSD_INNER + (SSD_G + g) * SSD_N:SSD_INNER + (SSD_G + g + 1) * SSD_N] = dC.astype(dxbc_ref.dtype)
        dXd = stg_ref[...]
        dxbc_ref[:, 0:SSD_INNER] = (dXd * dt_e + dsk_e * dY).astype(dxbc_ref.dtype)
        rowl = lax.broadcasted_iota(jnp.int32, (L, 1), 0)
        dcs = dcs + jnp.where(rowl == L - 1, dcsl, 0.0)
        dalpha = _dot_sel(upper.astype(F32), dcs, split="b")
        ddt = _dot_sel(dXd * X, ET, terms=2) + dalpha * a
        dalog_ref[...] += jnp.sum(dalpha * dt, axis=0, keepdims=True) * a
        ddtr = ddt * _sigmoid(dtr + dtb)
        spread = (lax.broadcasted_iota(jnp.int32, (SSD_H, 128), 0) == lax.broadcasted_iota(jnp.int32, (SSD_H, 128), 1)).astype(F32)
        ddtr_ref[...] = _dot(ddtr, spread).astype(ddtr_ref.dtype)
        ddtb_ref[...] += jnp.sum(ddtr, axis=0, keepdims=True)
        ddsk_ref[...] += jnp.sum(_dot_sel(dY * X, ET, terms=2), axis=0, keepdims=True)

    rowr = lambda b, c: (b * C + (C - 1 - c), 0)
    small = pl.BlockSpec((1, SSD_H), lambda b, c: (0, 0))
    return _call_with_comm(
        body, (nseq, C), "ssd_bwd",
        [pl.BlockSpec((L, SSD_INNER), rowr),
         pl.BlockSpec((L, SSD_G * SSD_N), lambda b, c: (b * C + (C - 1 - c), SSD_INNER // (SSD_G * SSD_N))),
         pl.BlockSpec((L, SSD_G * SSD_N), lambda b, c: (b * C + (C - 1 - c), SSD_INNER // (SSD_G * SSD_N) + 1)),
         pl.BlockSpec((L, 128), rowr), small, small, small,
         pl.BlockSpec((1, 1, NP, 2 * SSD_P, SSD_N), lambda b, c: (b, C - 1 - c, 0, 0, 0)),
         pl.BlockSpec((L, SSD_INNER), rowr)],
        [xbc, xbc, xbc, dtr, dtb, alog, dsk, prev, dy],
        [pl.BlockSpec((L, CONV_CH), rowr), pl.BlockSpec((L, 128), rowr), small, small, small],
        [jax.ShapeDtypeStruct((T, CONV_CH), BF16), jax.ShapeDtypeStruct((T, 128), BF16),
         jax.ShapeDtypeStruct((1, SSD_H), F32), jax.ShapeDtypeStruct((1, SSD_H), F32), jax.ShapeDtypeStruct((1, SSD_H), F32)],
        comm, scratch=[pltpu.VMEM((NP, 2 * SSD_P, SSD_N), F32), pltpu.VMEM((L, SSD_INNER), F32)], sem=("arbitrary", "arbitrary"))


SLOT = 128
ATT_T = 512
ATT_HP = 1
LOG2E = math.log2(math.e)
Q_SCALE = QK ** -0.5 * LOG2E


def _col_to_row(col):
    n = col.shape[0]
    eye = lax.broadcasted_iota(jnp.int32, (n, n), 0) == lax.broadcasted_iota(jnp.int32, (n, n), 1)
    return jnp.sum(jnp.where(eye, col, 0.0), axis=0, keepdims=True)


def attn_slot_fwd(q, k, v, nseq, comm=None):
    T = q.shape[0]
    S = T // nseq
    t = min(ATT_T, S)
    nb = S // t
    cols = [slice(h * SLOT, (h + 1) * SLOT) for h in range(ATT_HP)]

    def body(q_ref, k_ref, v_ref, o_ref, lse_ref):
        causal = lax.broadcasted_iota(jnp.int32, (t, t), 1) <= lax.broadcasted_iota(jnp.int32, (t, t), 0)
        for qi in range(nb):
            rows = slice(qi * t, (qi + 1) * t)
            state = [None] * ATT_HP
            for kj in range(qi + 1):
                keys = slice(kj * t, (kj + 1) * t)
                for h, c in enumerate(cols):
                    s = _dot(q_ref[rows, c], k_ref[keys, c], "nt")
                    if kj == qi:
                        s = jnp.where(causal, s, -1e30)
                    bm = jnp.max(s, axis=1, keepdims=True)
                    if kj == 0:
                        p = jnp.exp2(s - bm)
                        state[h] = (bm, jnp.sum(p, axis=1, keepdims=True), _dot(p, v_ref[keys, c]))
                    else:
                        m, l, acc = state[h]
                        m_new = jnp.maximum(m, bm)
                        corr = jnp.exp2(m - m_new)
                        p = jnp.exp2(s - m_new)
                        state[h] = (m_new, l * corr + jnp.sum(p, axis=1, keepdims=True), acc * corr + _dot(p, v_ref[keys, c]))
            for h, c in enumerate(cols):
                m, l, acc = state[h]
                o_ref[rows, c] = (acc / l).astype(o_ref.dtype)
                lse_ref[0, h, :, rows] = _col_to_row(m + jnp.log2(l))

    blk = pl.BlockSpec((S, ATT_HP * SLOT), lambda b, h: (b, h))
    return _call_with_comm(
        body, (nseq, MLA_H // ATT_HP), "attn_fwd", [blk, blk, blk], [q, k, v],
        [blk, pl.BlockSpec((1, ATT_HP, 1, S), lambda b, h: (b, h, 0, 0))],
        [jax.ShapeDtypeStruct((T, MLA_H * SLOT), BF16), jax.ShapeDtypeStruct((nseq, MLA_H, 1, S), F32)], comm)


def attn_slot_bwd(q, k, v, o, lse, do, nseq, comm=None):
    T = q.shape[0]
    S = T // nseq
    t = min(ATT_T, S)
    nb = S // t
    scale = QK ** -0.5
    cols = [slice(h * SLOT, (h + 1) * SLOT) for h in range(ATT_HP)]

    def body(q_ref, k_ref, v_ref, o_ref, lse_ref, do_ref, dq_ref, dk_ref, dv_ref, dqa_ref):
        causal_t = lax.broadcasted_iota(jnp.int32, (t, t), 0) <= lax.broadcasted_iota(jnp.int32, (t, t), 1)
        ones = jnp.ones((8, SLOT), F32)
        delta = {}
        for qi in range(nb):
            sl = slice(qi * t, (qi + 1) * t)
            for h, c in enumerate(cols):
                prod = do_ref[sl, c].astype(F32) * o_ref[sl, c].astype(F32)
                delta[h, qi] = _dot_sel(ones, prod, "nt", split="b", terms=2)[0:1, :]
        for kj in range(nb):
            ks = slice(kj * t, (kj + 1) * t)
            dk = [None] * ATT_HP
            dv = [None] * ATT_HP
            for qi in range(kj, nb):
                sl = slice(qi * t, (qi + 1) * t)
                for h, c in enumerate(cols):
                    kb, vb, qb, dob = k_ref[ks, c], v_ref[ks, c], q_ref[sl, c], do_ref[sl, c]
                    st = _dot(kb, qb, "nt")
                    pt = jnp.exp2(st - lse_ref[0, h, :, sl])
                    if qi == kj:
                        pt = jnp.where(causal_t, pt, 0.0)
                    dpt = _dot(vb, dob, "nt")
                    dst = (pt * (dpt - delta[h, qi])).astype(BF16)
                    dvc = _dot(pt, dob)
                    dkc = _dot(dst, qb) * (1.0 / LOG2E)
                    dv[h] = dvc if dv[h] is None else dv[h] + dvc
                    dk[h] = dkc if dk[h] is None else dk[h] + dkc
                    dqc = _dot(dst, kb, "tn") * scale
                    if kj > 0:
                        dqc = dqc + dqa_ref[sl, c]
                    if qi == kj:
                        dq_ref[sl, c] = dqc.astype(dq_ref.dtype)
                    else:
                        dqa_ref[sl, c] = dqc
            for h, c in enumerate(cols):
                dk_ref[ks, c] = dk[h].astype(dk_ref.dtype)
                dv_ref[ks, c] = dv[h].astype(dv_ref.dtype)

    blk = pl.BlockSpec((S, ATT_HP * SLOT), lambda b, h: (b, h))
    lse_spec = pl.BlockSpec((1, ATT_HP, 1, S), lambda b, h: (b, h, 0, 0))
    W = MLA_H * SLOT
    return _call_with_comm(
        body, (nseq, MLA_H // ATT_HP), "attn_bwd", [blk, blk, blk, blk, lse_spec, blk], [q, k, v, o, lse, do], [blk, blk, blk],
        [jax.ShapeDtypeStruct((T, W), BF16)] * 3, comm, scratch=[pltpu.VMEM((S, ATT_HP * SLOT), F32)])


def _rope_coeffs(pos, inv):
    half = ROPE // 2
    ang = pos * inv
    lane = lax.broadcasted_iota(jnp.int32, (1, SLOT), 1)
    sn = jnp.sin(ang)
    C = jnp.where(lane < NOPE, 1.0, jnp.where(lane < QK, jnp.cos(ang), 0.0))
    Sg = jnp.where((lane >= NOPE) & (lane < NOPE + half), -sn, jnp.where((lane >= NOPE + half) & (lane < QK), sn, 0.0))
    return C, Sg


def _rope_inputs(positions):
    half = ROPE // 2
    inv = ROPE_THETA ** (-jnp.arange(0, ROPE, 2, dtype=F32) / ROPE)
    row = jnp.zeros((1, SLOT), F32).at[0, NOPE:NOPE + half].set(inv).at[0, NOPE + half:QK].set(inv)
    return positions.astype(F32).reshape(-1, 1), row


def _place_k_rope(kr_lanes):
    r = lax.broadcasted_iota(jnp.int32, (SLOT, SLOT), 0)
    c = lax.broadcasted_iota(jnp.int32, (SLOT, SLOT), 1)
    return _dot_sel(kr_lanes, ((c == r + NOPE) & (r < ROPE)).astype(F32))


def rope_table(pos, inv):
    return rowwise(_rope_coeffs, [pos], [inv], [(SLOT, F32), (SLOT, F32)], [], "rope_table")


def rope_q_epilogue(accs, ex):
    C, Sg = ex[0], ex[1]
    reps = accs[0].shape[1] // SLOT
    return ((accs[0] * jnp.tile(C, (1, reps)) + _rope_swap(accs[0]) * jnp.tile(Sg, (1, reps))) * Q_SCALE,)


def rope_k_epilogue(accs, ex):
    C, Sg = ex[0], ex[1]
    kr = _place_k_rope(ex[2][:, SLOT:2 * SLOT])
    kr = kr * C + _rope_swap(kr) * Sg
    return (accs[0] + jnp.tile(kr, (1, accs[0].shape[1] // SLOT)),)


def _rope_swap(x):
    W = x.shape[1]
    half = ROPE // 2
    lane = lax.broadcasted_iota(jnp.int32, (1, W), 1) & (SLOT - 1)
    up = pltpu.roll(x, W - half, axis=1)
    dn = pltpu.roll(x, half, axis=1)
    return jnp.where((lane >= NOPE) & (lane < NOPE + half), up, jnp.where((lane >= NOPE + half) & (lane < QK), dn, 0.0))


def rope_slot_bwd(dq, dk, C, Sg, name):
    def fn(dqv, dkv, C, Sg):
        ct, stl = jnp.tile(C, (1, MLA_H)), jnp.tile(Sg, (1, MLA_H))
        dqo = dqv * ct - _rope_swap(dqv) * stl
        tot = dkv[:, 0:SLOT]
        for h in range(1, MLA_H):
            tot = tot + dkv[:, h * SLOT:(h + 1) * SLOT]
        u = tot * C - _rope_swap(tot) * Sg
        r = lax.broadcasted_iota(jnp.int32, (SLOT, SLOT), 0)
        c = lax.broadcasted_iota(jnp.int32, (SLOT, SLOT), 1)
        unplace = ((r == c + NOPE) & (c < ROPE)).astype(F32)
        return dqo, dkv, _dot_sel(u, unplace, terms=2)
    W = MLA_H * SLOT
    return rowwise(fn, [dq, dk, C, Sg], [], [(W, BF16), (W, BF16), (SLOT, BF16)], [], name)


XA_BLK = 512


def xattn_fwd(q, k, v, nseq, comm=None):
    T = q.shape[0]
    S = T // nseq
    M = k.shape[0] // nseq
    tq = min(XA_BLK, S)
    nq = S // tq
    scale = XA_D ** -0.5

    def body(q_ref, k_ref, v_ref, o_ref):
        s = _dot(q_ref[...], k_ref[...], "nt") * scale
        p = jnp.exp(s - jnp.max(s, axis=1, keepdims=True))
        p = p / jnp.sum(p, axis=1, keepdims=True)
        o_ref[...] = _dot(p, v_ref[...]).astype(o_ref.dtype)

    qs = pl.BlockSpec((tq, XA_D), lambda b, h, i: (b * nq + i, h))
    ks = pl.BlockSpec((M, XA_D), lambda b, h, i: (b, h))
    return _call_with_comm(body, (nseq, XA_H, nq), "xattn_fwd", [qs, ks, ks], [q, k, v], [qs],
                           [jax.ShapeDtypeStruct((T, XA_H * XA_D), BF16)], comm)


def xattn_bwd(q, k, v, do, nseq):
    T = q.shape[0]
    S = T // nseq
    M = k.shape[0] // nseq
    tq = min(XA_BLK, S)
    nq = S // tq
    scale = XA_D ** -0.5

    def body(q_ref, k_ref, v_ref, do_ref, dq_ref, dk_ref, dv_ref):
        @pl.when(pl.program_id(2) == 0)
        def _():
            dk_ref[...] = jnp.zeros_like(dk_ref)
            dv_ref[...] = jnp.zeros_like(dv_ref)

        qb, kb, vb, dob = q_ref[...], k_ref[...], v_ref[...], do_ref[...]
        s = _dot(qb, kb, "nt") * scale
        p = jnp.exp(s - jnp.max(s, axis=1, keepdims=True))
        p = p / jnp.sum(p, axis=1, keepdims=True)
        dp = _dot(dob, vb, "nt")
        ds = p * (dp - jnp.sum(dp * p, axis=1, keepdims=True)) * scale
        dq_ref[...] = _dot(ds, kb).astype(dq_ref.dtype)
        dk_ref[...] += _dot(ds, qb, "tn")
        dv_ref[...] += _dot(p, dob, "tn")

    qs = pl.BlockSpec((tq, XA_D), lambda b, h, i: (b * nq + i, h))
    ks = pl.BlockSpec((M, XA_D), lambda b, h, i: (b, h))
    return pl.pallas_call(
        body, grid=(nseq, XA_H, nq), name="xattn_bwd", in_specs=[qs, ks, ks, qs], out_specs=[qs, ks, ks],
        out_shape=[jax.ShapeDtypeStruct((T, XA_H * XA_D), BF16), jax.ShapeDtypeStruct(k.shape, F32),
                   jax.ShapeDtypeStruct(k.shape, F32)],
        compiler_params=_cp("parallel", "parallel", "arbitrary"),
    )(q, k, v, do)


CONV_BLK = 256


def _shift_down(x, s, rows):
    if s == 0:
        return x
    return jnp.where(rows >= s, pltpu.roll(x, s, axis=0), 0.0)


def _shift_up(x, s, rows):
    if s == 0:
        return x
    S = x.shape[0]
    return jnp.where(rows < S - s, pltpu.roll(x, S - s, axis=0), 0.0)


def conv_fwd(x, w, b, nseq):
    T, CH = x.shape
    S = T // nseq

    def body(x_ref, w_ref, b_ref, o_ref):
        xv = x_ref[...].astype(F32)
        rows = lax.broadcasted_iota(jnp.int32, (S, 1), 0)
        c = jnp.zeros_like(xv) + b_ref[...]
        for kk in range(CONV_K):
            c = c + w_ref[kk:kk + 1, :] * _shift_down(xv, CONV_K - 1 - kk, rows)
        o_ref[...] = (c * _sigmoid(c)).astype(o_ref.dtype)

    xs = pl.BlockSpec((S, CONV_BLK), lambda j, bb: (bb, j))
    return pl.pallas_call(
        body, grid=(CH // CONV_BLK, nseq), name="conv_fwd",
        in_specs=[xs, pl.BlockSpec((CONV_K, CONV_BLK), lambda j, bb: (0, j)), pl.BlockSpec((1, CONV_BLK), lambda j, bb: (0, j))],
        out_specs=xs, out_shape=jax.ShapeDtypeStruct((T, CH), BF16),
        compiler_params=_cp("parallel", "parallel"),
    )(x, w, b)


def conv_bwd(x, w, b, dout, nseq):
    T, CH = x.shape
    S = T // nseq

    def body(x_ref, w_ref, b_ref, do_ref, dx_ref, dw_ref, db_ref):
        @pl.when(pl.program_id(1) == 0)
        def _():
            dw_ref[...] = jnp.zeros_like(dw_ref)
            db_ref[...] = jnp.zeros_like(db_ref)

        xv = x_ref[...].astype(F32)
        rows = lax.broadcasted_iota(jnp.int32, (S, 1), 0)
        c = jnp.zeros_like(xv) + b_ref[...]
        sh = [_shift_down(xv, CONV_K - 1 - kk, rows) for kk in range(CONV_K)]
        for kk in range(CONV_K):
            c = c + w_ref[kk:kk + 1, :] * sh[kk]
        sg = _sigmoid(c)
        dc = do_ref[...].astype(F32) * sg * (1.0 + c * (1.0 - sg))
        dx = jnp.zeros_like(xv)
        for kk in range(CONV_K):
            dx = dx + w_ref[kk:kk + 1, :] * _shift_up(dc, CONV_K - 1 - kk, rows)
            dw_ref[kk:kk + 1, :] += jnp.sum(dc * sh[kk], axis=0, keepdims=True)
        dx_ref[...] = dx.astype(dx_ref.dtype)
        db_ref[...] += jnp.sum(dc, axis=0, keepdims=True)

    xs = pl.BlockSpec((S, CONV_BLK), lambda j, bb: (bb, j))
    ws = pl.BlockSpec((CONV_K, CONV_BLK), lambda j, bb: (0, j))
    bs = pl.BlockSpec((1, CONV_BLK), lambda j, bb: (0, j))
    return pl.pallas_call(
        body, grid=(CH // CONV_BLK, nseq), name="conv_bwd",
        in_specs=[xs, ws, bs, xs], out_specs=[xs, ws, bs],
        out_shape=[jax.ShapeDtypeStruct((T, CH), BF16), jax.ShapeDtypeStruct((CONV_K, CH), F32),
                   jax.ShapeDtypeStruct((1, CH), F32)],
        compiler_params=_cp("parallel", "arbitrary"),
    )(x, w, b, dout)


def _dims(a, b, mode):
    M = a.shape[1] if mode[0] == "t" else a.shape[0]
    K = a.shape[0] if mode[0] == "t" else a.shape[1]
    N = b.shape[0] if mode[1] == "t" else b.shape[1]
    return M, K, N


def _tile(dim, prefs):
    for p in prefs:
        if dim % p == 0:
            return p
    return dim


def mm(groups, out_dtypes, name, tm=None, tn=None, tk=None, epi=None, extras=(), comm=None, sub=1, n_sum=0):
    a0, b0, m0 = groups[0][0]
    M, K0, N = _dims(a0, b0, m0)
    tm = tm or _tile(M, (1024, 512, 256, 128))
    tn = tn or _tile(N, (1024, 512, 256, 128))
    flat = [p for g in groups for p in g]
    nk = 1 if tk is None else K0 // tk
    in_specs, args = [], []
    for a, b, mode in flat:
        _, K, _ = _dims(a, b, mode)
        kb = K if tk is None else tk
        in_specs.append(pl.BlockSpec((kb, tm), lambda i, j, k: (k, i)) if mode[0] == "t"
                        else pl.BlockSpec((tm, kb), lambda i, j, k: (i, k)))
        in_specs.append(pl.BlockSpec((tn, kb), lambda i, j, k: (j, k)) if mode[1] == "t"
                        else pl.BlockSpec((kb, tn), lambda i, j, k: (k, j)))
        args += [a, b]
    kinds = []
    for e in extras:
        kind, e = e if isinstance(e, tuple) else ("vec" if e.shape[0] == 1 and M != 1 else "tile", e)
        in_specs.append({"tile": pl.BlockSpec((tm, tn), lambda i, j, k: (i, j)),
                         "vec": pl.BlockSpec((1, tn), lambda i, j, k: (0, j)),
                         "rows": pl.BlockSpec((tm, e.shape[1]), lambda i, j, k: (i, 0)),
                         "whole": pl.BlockSpec(e.shape, lambda i, j, k: (0, 0))}[kind])
        kinds.append(kind)
        args.append(e)
    n_in = len(args)
    n_main = len(out_dtypes)
    n_out = n_main + n_sum
    assert n_sum == 0 or (tn == N and tk is None)
    ng = len(groups)
    sizes = [len(g) for g in groups]

    def body(*refs):
        ins, outs, accs = refs[:n_in], refs[n_in:n_in + n_out], refs[n_in + n_out:]
        kk = pl.program_id(2)

        def dots(rs):
            vals, pos = [], 0
            for gi in range(ng):
                acc = None
                for _ in range(sizes[gi]):
                    mode = flat[pos // 2][2]
                    av = ins[pos][:, rs] if mode[0] == "t" else ins[pos][rs, :]
                    d = _dot(av, ins[pos + 1][...], mode)
                    acc = d if acc is None else acc + d
                    pos += 2
                vals.append(acc)
            return vals

        def finish(accv, rs, first_chunk=True):
            ex = [(r[rs, :] if kind in ("tile", "rows") else r[...]).astype(F32) for kind, r in zip(kinds, ins[2 * len(flat):])]
            res = epi(accv, ex) if epi is not None else tuple(accv)
            for o, r in zip(outs[:n_main], res[:n_main]):
                o[rs, :] = r.astype(o.dtype)
            for o, r in zip(outs[n_main:], res[n_main:]):
                if first_chunk:
                    @pl.when(pl.program_id(0) == 0)
                    def _():
                        o[...] = r

                    @pl.when(pl.program_id(0) > 0)
                    def _():
                        o[...] += r
                else:
                    o[...] += r

        if nk == 1:
            for r in range(sub):
                rs = slice(r * (tm // sub), (r + 1) * (tm // sub))
                finish(dots(rs), rs, r == 0)
        else:
            vals = dots(slice(0, tm))
            finish = functools.partial(finish, rs=slice(0, tm))
            @pl.when(kk == 0)
            def _():
                for ar, vv in zip(accs, vals):
                    ar[...] = vv

            @pl.when(kk > 0)
            def _():
                for ar, vv in zip(accs, vals):
                    ar[...] += vv

            @pl.when(kk == nk - 1)
            def _():
                finish([ar[...] for ar in accs])

    grid = (M // tm, N // tn, nk)
    out_specs = [pl.BlockSpec((tm, tn), lambda i, j, k: (i, j)) for _ in out_dtypes] \
        + [pl.BlockSpec((1, tn), lambda i, j, k: (0, j))] * n_sum
    out_shape = [jax.ShapeDtypeStruct((M, N), dt) for dt in out_dtypes] + [jax.ShapeDtypeStruct((1, N), F32)] * n_sum
    scratch = [pltpu.VMEM((tm, tn), F32) for _ in range(ng if nk > 1 else 0)]
    sem = ("arbitrary" if n_sum else "parallel", "parallel", "arbitrary")
    if comm is not None:
        body = _attach(comm, body, n_in, n_out, *_grid_ends(grid))
        in_specs, args = in_specs + [HBM_SPEC] * len(comm.inputs), args + comm.inputs
        out_specs, out_shape = out_specs + [HBM_SPEC] * len(comm.out_shapes), out_shape + comm.out_shapes
        scratch, sem = scratch + comm.sems, ("arbitrary",) * 3
    return pl.pallas_call(body, grid=grid, name=name, in_specs=in_specs, out_specs=out_specs, out_shape=out_shape,
                          scratch_shapes=scratch, compiler_params=_cp(*sem))(*args)


def mm1(a, b, mode, out_dtype, name, **kw):
    return mm([[(a, b, mode)]], [out_dtype], name, **kw)[0]


ROW_BLK = 512


def rowwise(fn, rows, consts, outs, accs, name, tb=ROW_BLK, comm=None):
    rows = [r if isinstance(r, tuple) else (r, r.shape[1], 0) for r in rows]
    T = rows[0][0].shape[0]
    tb = min(tb, T)
    n_r, n_c, n_o, n_a = len(rows), len(consts), len(outs), len(accs)

    def body(*refs):
        vals = [r[...].astype(F32) for r in refs[:n_r + n_c]]
        res = fn(*vals)
        o_refs = refs[n_r + n_c:n_r + n_c + n_o]
        a_refs = refs[n_r + n_c + n_o:]
        for o, r in zip(o_refs, res[:n_o]):
            o[...] = r.astype(o.dtype)
        if n_a:
            @pl.when(pl.program_id(0) == 0)
            def _():
                for ar in a_refs:
                    ar[...] = jnp.zeros_like(ar)
            for ar, r in zip(a_refs, res[n_o:]):
                ar[...] += r

    return _call_with_comm(
        body, (T // tb,), name,
        [pl.BlockSpec((tb, w), functools.partial(lambda i, j: (i, j), j=j)) for _, w, j in rows]
        + [pl.BlockSpec(c.shape, lambda i: (0, 0)) for c in consts],
        [r[0] for r in rows] + list(consts),
        [pl.BlockSpec((tb, d), lambda i: (i, 0)) for d, _ in outs] + [pl.BlockSpec(s, lambda i: (0, 0)) for s in accs],
        [jax.ShapeDtypeStruct((T, d), dt) for d, dt in outs] + [jax.ShapeDtypeStruct(s, F32) for s in accs],
        comm, sem=("arbitrary" if n_a else "parallel",))


def _rms_stats(x):
    r = lax.rsqrt(jnp.mean(x * x, axis=-1, keepdims=True) + EPS)
    return r, x * r


def _rms_bwd(x, g, dy):
    r, xn = _rms_stats(x)
    dyg = dy * g
    dx = r * (dyg - xn * jnp.mean(dyg * xn, axis=-1, keepdims=True))
    return dx, jnp.sum(dy * xn, axis=0, keepdims=True)


def rms_fwd(x, g, name, comm=None):
    res = rowwise(lambda xv, gv: (_rms_stats(xv)[1] * gv,), [x], [g], [(x.shape[1], BF16)], [], name, comm=comm)
    return res[0] if comm is None else (res[0], res[1:])


def rms_bwd(x, g, dy, name, resid=None, dx_dtype=F32):
    def fn(*v):
        if resid is None:
            xv, dyv, gv = v
            dx, dg = _rms_bwd(xv, gv, dyv)
        else:
            xv, dyv, rv, gv = v
            dx, dg = _rms_bwd(xv, gv, dyv)
            dx = dx + rv
        return dx, dg
    rows = [x, dy] + ([] if resid is None else [resid])
    return rowwise(fn, rows, [g], [(x.shape[1], dx_dtype)], [(1, x.shape[1])], name)


def mm_rms_bwd(pairs, x, g, name, resid=None, dx_dtype=F32, comm=None):
    def epi(accs, ex):
        dx, dg = _rms_bwd(ex[0], ex[-1], accs[0])
        return (dx if resid is None else dx + ex[1]), dg
    extras = [x] + ([] if resid is None else [resid]) + [g]
    return mm([pairs], [dx_dtype], name, tm=min(256, x.shape[0]), tn=x.shape[1], epi=epi, extras=extras, comm=comm, n_sum=1)


def mm_resid(a, b, x, g, wgt, name, comm=None, target=None):
    def epi(accs, ex):
        y = ex[0] + wgt * _rms_stats(accs[0])[1] * ex[1]
        if target is None:
            return accs[0], y
        d = y - ex[2]
        return accs[0], d / D, jnp.sum(d * d, axis=0, keepdims=True)
    return mm([[(a, b, "nn")]], [F32, F32], name, tm=min(512, a.shape[0]), tn=b.shape[1], epi=epi,
              extras=[x, g] + ([] if target is None else [target]), sub=2, comm=comm, n_sum=0 if target is None else 1)


def resid_bwd(h, g, dy, wgt, name):
    def fn(hv, dyv, gv):
        dx, dg = _rms_bwd(hv, gv, dyv)
        return wgt * dx, wgt * dg
    return rowwise(fn, [h, dy], [g], [(h.shape[1], BF16)], [(1, h.shape[1])], name)


def _silu_parts(g):
    s = _sigmoid(g)
    return g * s, s * (1.0 + g * (1.0 - s))


def gated_norm_fwd(y, z, g, name):
    W = SSD_INNER // SSD_G

    def fn(yv, zv, gv):
        yg = yv * _silu_parts(zv)[0]
        return (jnp.concatenate([_rms_stats(yg[:, i * W:(i + 1) * W])[1] for i in range(SSD_G)], axis=1) * gv,)
    return rowwise(fn, [y, z], [g], [(SSD_INNER, BF16)], [], name)[0]


def gated_norm_bwd(y, z, dyn, g, name):
    W = SSD_INNER // SSD_G

    def fn(yv, zv, dv, gv):
        sil, dsil = _silu_parts(zv)
        yg = yv * sil
        parts = [_rms_bwd(yg[:, i * W:(i + 1) * W], gv[:, i * W:(i + 1) * W], dv[:, i * W:(i + 1) * W]) for i in range(SSD_G)]
        dyg = jnp.concatenate([p[0] for p in parts], axis=1)
        dg = jnp.concatenate([p[1] for p in parts], axis=1)
        return dyg * sil, dyg * yv * dsil, dg
    return rowwise(fn, [y, z, dyn], [g], [(SSD_INNER, BF16), (SSD_INNER, BF16)], [(1, SSD_INNER)], name)


def merge_fwd(gl, ys, ym, gb, name):
    def fn(glv, ysv, ymv, gbv):
        gt = _sigmoid(glv + gbv)
        return (gt[:, :D] * ysv + gt[:, D:] * ymv,)
    return rowwise(fn, [gl, ys, ym], [gb], [(D, BF16)], [], name)[0]


def merge_bwd(gl, ys, ym, dm, gb, name):
    def fn(glv, ysv, ymv, dmv, gbv):
        gt = _sigmoid(glv + gbv)
        gs, gm = gt[:, :D], gt[:, D:]
        dgl = jnp.concatenate([dmv * ysv * gs * (1.0 - gs), dmv * ymv * gm * (1.0 - gm)], axis=1)
        return dmv * gs, dmv * gm, dgl, jnp.sum(dgl, axis=0, keepdims=True)
    return rowwise(fn, [gl, ys, ym, dm], [gb], [(D, BF16), (D, BF16), (2 * D, BF16)], [(1, 2 * D)], name)


def loss_head(y, tgt, name):
    def fn(yv, tv):
        d = yv - tv
        part = 0.5 * jnp.sum(jnp.sum(d * d, axis=1, keepdims=True), axis=0, keepdims=True) / D
        return d / D, jnp.broadcast_to(part, (1, 128))
    return rowwise(fn, [y, tgt], [], [(D, F32)], [(1, 128)], name)


def _adamw_math(wv, gv, mv, vv):
    mn = B1 * mv + (1.0 - B1) * gv
    vn = B2 * vv + (1.0 - B2) * (gv * gv)
    mh = mn / (1.0 - B1 ** STEP)
    vh = vn / (1.0 - B2 ** STEP)
    return -LR * (mh / (jnp.sqrt(vh) + AEPS) + WD * wv), mn, vn


def adamw(w, g, m, v, name):
    R, C = w.shape
    tb = _tile(R, (256, 128, 64, 32, 16, 8))
    return rowwise(_adamw_math, [w, g, m, v], [], [(C, F32)] * 3, [], name, tb=tb)


def adamw_small(packed, ws, ms, vs):
    k = len(ws)
    sizes = [x.shape[1] for x in ws]

    def body(*refs):
        p_ref, w_refs, m_refs, v_refs = refs[0], refs[1:1 + k], refs[1 + k:1 + 2 * k], refs[1 + 2 * k:1 + 3 * k]
        outs = refs[1 + 3 * k:]
        r0 = 0
        for i, n in enumerate(sizes):
            nr = -(-n // 128)
            g = jnp.concatenate([p_ref[r0 + r:r0 + r + 1, :] for r in range(nr)], axis=1)[:, :n]
            r0 += nr
            outs[i][...] = g
            outs[k + i][...], outs[2 * k + i][...], outs[3 * k + i][...] = _adamw_math(w_refs[i][...], g, m_refs[i][...], v_refs[i][...])

    res = pl.pallas_call(body, name="adamw_small",
                         out_shape=[jax.ShapeDtypeStruct((1, n), F32) for _ in range(4) for n in sizes])(packed, *ws, *ms, *vs)
    return [res[j * k:(j + 1) * k] for j in range(4)]


def adamw_from_slots(recv, piece, w, m, v, name, token=None):
    K, n = w.shape
    ns = recv.shape[0]
    assert recv.shape[2] == n and recv.shape[1] % K == 0
    tb = _tile(K, (256, 176, 128, 64, 32, 16, 8)) if K % 8 == 0 else K
    r_spec = pl.BlockSpec((ns, tb, n), lambda i: (0, piece * (K // tb) + i, 0))
    w_spec = pl.BlockSpec((tb, n), lambda i: (i, 0))

    def body(r_ref, w_ref, m_ref, v_ref, *rest):
        g_ref, d_ref, mo_ref, vo_ref = rest[-4:]
        g = r_ref[0].astype(F32)
        for s in range(1, ns):
            g = g + r_ref[s].astype(F32)
        g_ref[...] = g
        d_ref[...], mo_ref[...], vo_ref[...] = _adamw_math(w_ref[...], g, m_ref[...], v_ref[...])

    extra = [] if token is None else [token]
    return pl.pallas_call(
        body, grid=(K // tb,), name=name,
        in_specs=[r_spec, w_spec, w_spec, w_spec] + [pl.BlockSpec(t.shape, lambda i: (0, 0)) for t in extra], out_specs=[w_spec] * 4,
        out_shape=[jax.ShapeDtypeStruct((K, n), F32)] * 4, compiler_params=_cp("parallel"),
    )(recv, w, m, v, *extra)


def _me():
    return lax.axis_index("x"), lax.axis_index("y"), lax.axis_index("c")


def _dev_index():
    x, y, c = _me()
    return 4 * x + 2 * y + c


HBM_SPEC = pl.BlockSpec(memory_space=pl.ANY)


class GatherComm:
    def __init__(self, shards):
        self.inputs = [s for s, _ in shards]
        self.rows = [list(r) for _, r in shards]
        n = len(shards)
        self.out_shapes = [jax.ShapeDtypeStruct((N_DEV, r, s.shape[1]), s.dtype) for s, rows in shards for r in rows]
        self.sems = [pltpu.SemaphoreType.DMA((7 * n,)), pltpu.SemaphoreType.DMA((7 * n,)), pltpu.SemaphoreType.DMA((n,))]

    def _plan(self, x_refs, out_refs, sems):
        send_sems, recv_sems, local_sems = sems
        x, y, c = _me()
        me, sibling = (x, y, c), (x, y, 1 - c)
        chips = [(1 - x, y), (x, 1 - y), (1 - x, 1 - y)]
        index = lambda px, py, pc: 4 * px + 2 * py + pc
        mine, first, passed, whole = [], [], [], []
        pos = 0
        for i, rows in enumerate(self.rows):
            kw = lambda k: dict(send_sem=send_sems.at[7 * i + k], recv_sem=recv_sems.at[7 * i + k], device_id_type=MESH)
            r0 = 0
            fwd = [[] for _ in chips]
            for j, nr in enumerate(rows):
                out, src = out_refs[pos + j], x_refs[i].at[pl.ds(r0, nr)]
                mine.append(pltpu.make_async_copy(src, out.at[index(*me)], local_sems.at[i]))
                first.append(pltpu.make_async_remote_copy(src_ref=src, dst_ref=out.at[index(*me)], device_id=sibling, **kw(0)))
                for jj, chip in enumerate(chips):
                    first.append(pltpu.make_async_remote_copy(src_ref=src, dst_ref=out.at[index(*me)], device_id=(*chip, c),
                                                              **kw(1 + jj)))
                    blk = out.at[index(*chip, c)]
                    fwd[jj].append(pltpu.make_async_remote_copy(src_ref=blk, dst_ref=blk, device_id=sibling, **kw(4 + jj)))
                r0 += nr
            passed.append(fwd)
            whole.append([pltpu.make_async_remote_copy(src_ref=x_refs[i], dst_ref=x_refs[i], device_id=sibling, **kw(k))
                          for k in range(7)])
            pos += len(rows)
        return mine, first, passed, whole

    def start(self, x_refs, out_refs, sems):
        mine, first, _, _ = self._plan(x_refs, out_refs, sems)
        for cp in mine + first:
            cp.start()

    def finish(self, x_refs, out_refs, sems):
        _, _, passed, whole = self._plan(x_refs, out_refs, sems)
        local_sems = sems[2]
        for i, fwd in enumerate(passed):
            for jj in range(3):
                whole[i][1 + jj].wait_recv()
                for cp in fwd[jj]:
                    cp.start()
        for i in range(len(passed)):
            whole[i][0].wait_recv()
            for jj in range(3):
                whole[i][4 + jj].wait_recv()
        for i in range(len(passed)):
            for k in range(7):
                whole[i][k].wait_send()
            pltpu.make_async_copy(x_refs[i], x_refs[i], local_sems.at[i]).wait()


def run_comm(comm, name):
    n_in, n_out = len(comm.inputs), len(comm.out_shapes)

    def body(*refs):
        ins, outs, sems = refs[:n_in], refs[n_in:n_in + n_out], refs[n_in + n_out:]
        comm.start(ins, outs, sems)
        comm.finish(ins, outs, sems)

    return pl.pallas_call(body, name=name, out_shape=comm.out_shapes, in_specs=[HBM_SPEC] * n_in,
                          out_specs=[HBM_SPEC] * n_out, scratch_shapes=comm.sems)(*comm.inputs)


def _attach(comm, body, n_in, n_out, first, last):
    if comm is None:
        return body
    ci, co, cs = len(comm.inputs), len(comm.out_shapes), len(comm.sems)

    def wrapped(*refs):
        h_in, c_in = refs[:n_in], refs[n_in:n_in + ci]
        h_out, c_out = refs[n_in + ci:n_in + ci + n_out], refs[n_in + ci + n_out:n_in + ci + n_out + co]
        rest = refs[n_in + ci + n_out + co:]
        h_scr, c_sem = rest[:len(rest) - cs], rest[len(rest) - cs:]

        @pl.when(first())
        def _():
            comm.start(c_in, c_out, c_sem)

        body(*h_in, *h_out, *h_scr)

        @pl.when(last())
        def _():
            comm.finish(c_in, c_out, c_sem)

    return wrapped


def _grid_ends(grid):
    first = lambda: functools.reduce(lambda a, b: a & b, [pl.program_id(i) == 0 for i in range(len(grid))])
    last = lambda: functools.reduce(lambda a, b: a & b, [pl.program_id(i) == g - 1 for i, g in enumerate(grid)])
    return first, last


def _call_with_comm(body, grid, name, in_specs, args, out_specs, out_shape, comm, scratch=(), sem=None):
    sem = sem or ("parallel",) * len(grid)
    scratch = list(scratch)
    if comm is not None:
        body = _attach(comm, body, len(args), len(out_shape), *_grid_ends(grid))
        in_specs, args = in_specs + [HBM_SPEC] * len(comm.inputs), args + comm.inputs
        out_specs, out_shape = out_specs + [HBM_SPEC] * len(comm.out_shapes), out_shape + comm.out_shapes
        scratch, sem = scratch + comm.sems, ("arbitrary",) * len(grid)
    return pl.pallas_call(body, grid=grid, name=name, in_specs=in_specs, out_specs=out_specs, out_shape=out_shape,
                          scratch_shapes=scratch, compiler_params=_cp(*sem))(*args)


class ScatterComm:
    def __init__(self, groups):
        self.sizes = [len(g) for g in groups]
        self.rows = [[pc.shape[1] for pc in g] for g in groups]
        ng = len(groups)
        self.inputs = [pc for g in groups for pc in g]
        self.out_shapes = [jax.ShapeDtypeStruct((N_DEV, sum(self.rows[gi]), g[0].shape[2]), g[0].dtype) for gi, g in enumerate(groups)]
        self.sems = [pltpu.SemaphoreType.DMA((7 * ng,)), pltpu.SemaphoreType.DMA((7 * ng,)), pltpu.SemaphoreType.DMA((ng,))]

    def _peers(self):
        x, y, c = _me()
        out = []
        for k in range(1, N_DEV):
            px = 1 - x if k & 4 else x
            py = 1 - y if k & 2 else y
            pc = 1 - c if k & 1 else c
            out.append((k, 4 * px + 2 * py + pc, dict(device_id=(px, py, pc), device_id_type=MESH)))
        return 4 * x + 2 * y + c, out

    def start(self, ins, outs, sems):
        send_sems, recv_sems, local_sems = sems
        me, peers = self._peers()
        pos = 0
        for gi, size in enumerate(self.sizes):
            for i, pc in enumerate(ins[pos:pos + size]):
                dst = outs[gi].at[me, pl.ds(sum(self.rows[gi][:i]), self.rows[gi][i])]
                pltpu.make_async_copy(pc.at[me], dst, local_sems.at[gi]).start()
                for k, peer, kw in peers:
                    pltpu.make_async_remote_copy(src_ref=pc.at[peer], dst_ref=dst, send_sem=send_sems.at[7 * gi + k - 1],
                                                 recv_sem=recv_sems.at[7 * gi + k - 1], **kw).start()
            pos += size

    def finish(self, ins, outs, sems):
        send_sems, recv_sems, local_sems = sems
        me, peers = self._peers()
        whole = [pltpu.make_async_remote_copy(src_ref=outs[gi].at[peer], dst_ref=outs[gi].at[peer],
                                              send_sem=send_sems.at[7 * gi + k - 1], recv_sem=recv_sems.at[7 * gi + k - 1], **kw)
                 for gi in range(len(self.sizes)) for k, peer, kw in peers]
        for cp in whole:
            cp.wait_recv()
        for cp in whole:
            cp.wait_send()
        for gi in range(len(self.sizes)):
            pltpu.make_async_copy(outs[gi].at[me], outs[gi].at[me], local_sems.at[gi]).wait()


def _peer_list():
    x, y, c = _me()
    out = []
    for k in range(1, N_DEV):
        px = 1 - x if k & 4 else x
        py = 1 - y if k & 2 else y
        pc = 1 - c if k & 1 else c
        out.append((k, 4 * px + 2 * py + pc, dict(device_id=(px, py, pc), device_id_type=MESH)))
    return 4 * x + 2 * y + c, out


SEM_SPEC = pl.BlockSpec(memory_space=pltpu.SEMAPHORE)
HBM_ONLY = pl.BlockSpec(memory_space=pltpu.HBM)
N_SPLIT_SEMS = 2 * (N_DEV - 1)


def exchange_start(piece, after):
    def body(piece_ref, land_ref, after_ref, *outs):
        sems, token = outs[:N_SPLIT_SEMS], outs[-1]
        me, peers = _peer_list()
        for k, peer, kw in peers:
            pltpu.make_async_remote_copy(src_ref=piece_ref.at[peer], dst_ref=land_ref.at[me], send_sem=sems[k - 1],
                                         recv_sem=sems[N_DEV - 2 + k], **kw).start()
        token[...] = jnp.zeros_like(token)

    res = pl.pallas_call(
        body, name="exchange_last_start",
        out_shape=(pltpu.SemaphoreType.DMA(()),) * N_SPLIT_SEMS + (pltpu.HBM(piece.shape, piece.dtype), pltpu.HBM(piece.shape, piece.dtype),
                                                                   jax.ShapeDtypeStruct((8, 128), F32)),
        in_specs=(HBM_ONLY, HBM_ONLY, HBM_SPEC),
        out_specs=(SEM_SPEC,) * N_SPLIT_SEMS + (HBM_ONLY, HBM_ONLY, pl.BlockSpec(memory_space=pltpu.VMEM)),
        input_output_aliases={0: N_SPLIT_SEMS, 1: N_SPLIT_SEMS + 1},
        compiler_params=pltpu.CompilerParams(has_side_effects=pltpu.SideEffectType.DATAFLOW_SIDE_EFFECTING),
    )(pltpu.with_memory_space_constraint(piece, pltpu.HBM),
      pltpu.with_memory_space_constraint(lax.empty(piece.shape, piece.dtype), pltpu.HBM), after)
    return res[:N_SPLIT_SEMS], res[N_SPLIT_SEMS], res[N_SPLIT_SEMS + 1], res[N_SPLIT_SEMS + 2]


def exchange_wait(sems, piece, land, after):
    def body(piece_ref, land_ref, *rest):
        sem_refs = rest[:N_SPLIT_SEMS]
        me, peers = _peer_list()
        for k, peer, kw in peers:
            cp = pltpu.make_async_remote_copy(src_ref=piece_ref.at[peer], dst_ref=land_ref.at[peer], send_sem=sem_refs[k - 1],
                                              recv_sem=sem_refs[N_DEV - 2 + k], **kw)
            cp.wait_send()
            cp.wait_recv()

    return pl.pallas_call(
        body, name="exchange_last_wait",
        out_shape=(pltpu.HBM(piece.shape, piece.dtype), pltpu.HBM(land.shape, land.dtype)),
        in_specs=(HBM_ONLY, HBM_ONLY) + (SEM_SPEC,) * N_SPLIT_SEMS + (HBM_SPEC,), out_specs=(HBM_ONLY, HBM_ONLY),
        input_output_aliases={0: 0, 1: 1},
        compiler_params=pltpu.CompilerParams(has_side_effects=pltpu.SideEffectType.DATAFLOW_SIDE_EFFECTING),
    )(piece, land, *sems, after)[1]


def sum_slots(recv, name, tr):
    n, R, C = recv.shape

    def body(r_ref, o_ref):
        acc = r_ref[0].astype(F32)
        for s in range(1, n):
            acc = acc + r_ref[s].astype(F32)
        o_ref[...] = acc

    return pl.pallas_call(
        body, grid=(R // tr,), name=name,
        in_specs=[pl.BlockSpec((n, tr, C), lambda i: (0, i, 0))], out_specs=pl.BlockSpec((tr, C), lambda i: (i, 0)),
        out_shape=jax.ShapeDtypeStruct((R, C), F32), compiler_params=_cp("parallel"),
    )(recv)


PACK_W, FLAT_W = 1024, 128
MAIN = [
    ("ffn1_w_gate", "col"), ("ffn1_w_up", "col"), ("ffn1_w_down", "row"),
    ("ffn2_w_gate", "col"), ("ffn2_w_up", "col"), ("ffn2_w_down", "row"),
    ("w_ssd_proj", "row"), ("w_mla_proj", "row"), ("w_out", "row"),
    ("w_xq", "row"), ("w_xk", "row"), ("w_xv", "row"), ("w_xo", "row"),
    ("w_uk", "col"), ("w_uv", "col"),
]
FLAT = [("w_in", "col"), ("w_uq", "col")]
BIG = MAIN + FLAT
SMALL = ["ffn1_pre_g", "ffn1_post_g", "mix_pre_g", "conv_b", "dt_bias", "a_log", "d_skip", "ssd_norm_g", "q_norm_g",
         "kv_norm_g", "gate_bias", "mix_post_g", "xa_pre_g", "mem_norm_g", "xa_post_g", "ffn2_pre_g", "ffn2_post_g"]
WEIGHTS = ['ffn1_pre_g', 'ffn1_w_gate', 'ffn1_w_up', 'ffn1_w_down', 'ffn1_post_g', 'mix_pre_g', 'w_in', 'conv_w', 'conv_b',
           'dt_bias', 'a_log', 'd_skip', 'ssd_norm_g', 'w_ssd_proj', 'q_norm_g', 'w_uq', 'kv_norm_g', 'w_uk', 'w_uv',
           'w_mla_proj', 'gate_bias', 'w_out', 'mix_post_g', 'xa_pre_g', 'mem_norm_g', 'w_xq', 'w_xk', 'w_xv', 'w_xo',
           'xa_post_g', 'ffn2_pre_g', 'ffn2_w_gate', 'ffn2_w_up', 'ffn2_w_down', 'ffn2_post_g']


def _pack_rows(w, kind, width):
    m = w[0].T if kind == "col" else w[0]
    return m.reshape(-1, width)


KIND = dict(BIG)
GATHER_PLAN = {
    "ffn1_pre": (["ffn1_w_gate", "ffn1_w_up"], []),
    "ffn1_gate_up": (["ffn1_w_down"], ["w_in@0"]),
    "ffn1_down": ([], ["w_in@1"]),
    "ssd_fwd": (["w_ssd_proj", "w_mla_proj", "w_out", "w_uk", "w_uv"], ["w_uq"]),
    "attn_fwd": (["w_xq", "w_xk", "w_xv", "w_xo", "ffn2_w_gate", "ffn2_w_up"], []),
    "xattn_fwd": (["ffn2_w_down"], []),
}
CONV_RIDES_WITH = "w_in@1"
LAST_EXCHANGE = "last"
SCATTER_PLAN = {
    "attn_bwd": [["ffn2_w_gate", "ffn2_w_up", "ffn2_w_down"], ["w_xq", "w_xk", "w_xv", "w_xo"]],
    "ssd_bwd": [["w_ssd_proj", "w_mla_proj", "w_out"], ["w_uk", "w_uv"], ["w_uq"]],
    "in_bwd": [["w_in#0"]],
    "ffn1:down_bwd": [["w_in#1"]],
    "ffn1:dwd": [["w_in#2"]],
    "ffn1:dwg": [["ffn1_w_down#0"]],
    "ffn1:dwu": [["ffn1_w_down#1"]],
    "ffn1:gate_up_bwd": [["ffn1_w_gate"]],
    "last": [["ffn1_w_up"]],
}
PARTS = {"w_in@0": ("w_in", 0, 2656), "w_in@1": ("w_in", 2656, 5296),
         "w_in#0": ("w_in", 0, 2656), "w_in#1": ("w_in", 2656, 3984), "w_in#2": ("w_in", 3984, 5296),
         "ffn1_w_down#0": ("ffn1_w_down", 0, 176), "ffn1_w_down#1": ("ffn1_w_down", 176, 352)}


def _parts_of(base, mark):
    return sorted(pn for pn, (b, _, _) in PARTS.items() if b == base and mark in pn)


class Stage:
    def __init__(self, w):
        self.w = w
        self.width = {n: PACK_W if (n, k) in MAIN else FLAT_W for n, k in BIG}
        self.nrows = {n: math.prod(w[n].shape) // self.width[n] for n, _ in BIG}
        self.recv = {}
        self.arrived_parts = {}

    def _rows(self, n):
        return PARTS[n][2] - PARTS[n][1] if n in PARTS else self.nrows[n]

    def _shards(self, tag):
        names_main, names_flat = GATHER_PLAN[tag]

        def pack(n):
            base, r0, r1 = PARTS.get(n, (n, 0, None))
            return _pack_rows(self.w[base], KIND[base], self.width[base])[r0:r1].astype(BF16)
        shards = []
        if names_main:
            pieces = [pack(n) for n in names_main]
            shards.append((jnp.concatenate(pieces, axis=0), [pc.shape[0] for pc in pieces]))
        if names_flat:
            pieces = [pack(n) for n in names_flat]
            if CONV_RIDES_WITH in names_flat:
                pieces.append(_pad_rows(lax.bitcast_convert_type(self.w["conv_w"][0], BF16).reshape(-1, FLAT_W), 16))
            shards.append((jnp.concatenate(pieces, axis=0), [pc.shape[0] for pc in pieces]))
        return shards

    def gather(self, tag):
        return GatherComm(self._shards(tag)) if tag in GATHER_PLAN else None

    def gathered(self, tag, outs, W, p):
        if tag not in GATHER_PLAN:
            return
        names_main, names_flat = GATHER_PLAN[tag]
        outs = list(outs)
        for n in names_main + names_flat:
            rows = outs.pop(0)
            if n in PARTS:
                self.arrived_parts[n] = rows
                base = PARTS[n][0]
                mine = _parts_of(base, "@")
                if not all(pn in self.arrived_parts for pn in mine):
                    continue
                n, rows = base, jnp.concatenate([self.arrived_parts[pn] for pn in mine], axis=1)
            K = self.w[n].shape[1] if KIND[n] == "col" else PACK_W
            W[n] = rows.reshape(-1, K)
        if CONV_RIDES_WITH in names_flat:
            cw = self.w["conv_w"]
            nbits = 2 * math.prod(cw.shape) // FLAT_W
            bits = outs.pop(0)[:, :nbits].reshape((N_DEV,) + cw.shape[1:] + (2,))
            p["conv_w"] = lax.bitcast_convert_type(bits, F32).transpose(1, 0, 2).reshape(cw.shape[1], -1)

    def pieces(self, tag, gw):
        def piece(n):
            if n in PARTS:
                base, r0, r1 = PARTS[n]
                return gw[base].reshape(N_DEV, self.nrows[base], self.width[base])[:, r0:r1]
            return gw[n].reshape(N_DEV, self.nrows[n], self.width[n])
        return [[piece(n) for n in names] for names in SCATTER_PLAN[tag]]

    def scatter(self, tag, gw):
        return ScatterComm(self.pieces(tag, gw)) if tag in SCATTER_PLAN else None

    def scattered(self, tag, outs):
        if tag in SCATTER_PLAN:
            self.recv[tag] = outs


def _pad_rows(a, mult):
    r = (-a.shape[0]) % mult
    return a if r == 0 else jnp.concatenate([a, jnp.zeros((r,) + a.shape[1:], a.dtype)], axis=0)


def _pack_small(vals, loss_row=None, conv_w=None):
    rows = []
    for v in vals:
        f = v.reshape(-1)
        f = jnp.concatenate([f, jnp.zeros(((-f.shape[0]) % 128,), F32)])
        rows.append(f.reshape(-1, 128))
    if conv_w is not None:
        rows.append(conv_w.reshape(-1, 128))
    if loss_row is not None:
        rows.append(loss_row)
    return _pad_rows(jnp.concatenate(rows, axis=0), 8)


def _unpack_small(buf, shapes):
    out, r = [], 0
    for shp in shapes:
        n = math.prod(shp)
        nr = -(-n // 128)
        out.append(buf[r:r + nr].reshape(-1)[:n].reshape(shp))
        r += nr
    return out, r


def _tn(a, b, name, out_dtype=BF16, comm=None):
    M, N = a.shape[1], b.shape[1]
    T = a.shape[0]
    tm = M if M <= 1536 else M // 2
    tk = 1024 if T % 1024 == 0 and T > 1024 else None
    res = mm([[(a, b, "tn")]], [out_dtype], name, tm=tm, tn=N, tk=tk, comm=comm)
    return res[0] if comm is None else (res[0], res[1:])


class NoStage:
    def gather(self, tag):
        return None

    def gathered(self, tag, outs, W, p):
        pass

    def scatter(self, tag, gw):
        return None

    def scattered(self, tag, outs):
        pass


def _ffn_fwd(x, gpre, gpost, W, p, tag, stage, target=None):
    comm = stage.gather(tag + "_pre")
    h = rms_fwd(x, gpre, tag + "_pre", comm=comm)
    if comm is not None:
        h, arrived = h
        stage.gathered(tag + "_pre", arrived, W, p)

    def swi(accs, ex):
        sil, dsil = _silu_parts(accs[0])
        return sil, accs[1] * dsil, sil * accs[1]
    G, U, A, *arrived = mm([[(h, W[tag + "_w_gate"], "nt")], [(h, W[tag + "_w_up"], "nt")]], [BF16, BF16, BF16], tag + "_gate_up",
                           tn=DFF // 2, epi=swi, comm=stage.gather(tag + "_gate_up"), sub=4 if h.shape[0] % 1024 == 0 else 1)
    stage.gathered(tag + "_gate_up", arrived, W, p)
    H, y, *rest = mm_resid(A, W[tag + "_w_down"], x, gpost, FFN_RES, tag + "_down", comm=stage.gather(tag + "_down"), target=target)
    saved = (x, h, G, U, A, H)
    if target is not None:
        return y, saved, rest[0]
    stage.gathered(tag + "_down", rest, W, p)
    return y, saved


def _ffn_bwd(dy, saved, gpre, gpost, wg_t, wu_t, wd, tag, stage, gw):
    x, h, G, U, A, H = saved
    dH, dgpost = resid_bwd(H, gpost, dy, FFN_RES, tag + "_post_bwd")

    def dswi(accs, ex):
        return accs[0] * ex[1], accs[0] * ex[0]

    def hosted(where, call):
        comm = stage.scatter(tag + ":" + where, gw)
        res = call(comm)
        if comm is None:
            return res
        stage.scattered(tag + ":" + where, res[1])
        return res[0]

    res = hosted("down_bwd", lambda comm: (lambda r: r if comm is None else (r[:2], r[2:]))(
        mm([[(dH, wd, "nt")]], [BF16, BF16], tag + "_down_bwd", tn=DFF // 2, epi=dswi, extras=[G, U], comm=comm,
           sub=4 if dH.shape[0] % 1024 == 0 else 1)))
    dG, dU = res
    gw[tag + "_w_down"] = hosted("dwd", lambda comm: _tn(A, dH, tag + "_dwd", comm=comm))
    gw[tag + "_w_gate"] = hosted("dwg", lambda comm: _tn(dG, h, tag + "_dwg", comm=comm))
    gw[tag + "_w_up"] = hosted("dwu", lambda comm: _tn(dU, h, tag + "_dwu", comm=comm))
    dx, dgpre = hosted("gate_up_bwd", lambda comm: (lambda r: r[:2] if comm is None else (r[:2], r[2:]))(
        mm_rms_bwd([(dG, wg_t, "nn"), (dU, wu_t, "nn")], x, gpre, tag + "_gate_up_bwd", resid=dy, comm=comm)))
    return dx, dgpre, dgpost


def _local_step(x, mem, positions, tgt, W, p, stage=None):
    stage = stage or NoStage()
    nseq = x.shape[0]
    T = nseq * x.shape[1]
    x0 = x.reshape(T, D)
    mem2 = mem.reshape(-1, D)

    x1, ffn1 = _ffn_fwd(x0, p["ffn1_pre_g"], p["ffn1_post_g"], W, p, "ffn1", stage)

    w_in_t = W["w_in"]
    bounds = [0]
    for n in (SSD_INNER, CONV_CH, SSD_H, QR, KVR, ROPE, 2 * D):
        bounds.append(bounds[-1] + n)
    wt_z, wt_xbc, wt_dt, wt_q, wt_kv, wt_kr, wt_gate = [w_in_t[bounds[i]:bounds[i + 1]] for i in range(7)]
    wt_dt, wt_kr = _pad_rows(wt_dt, SLOT), _pad_rows(wt_kr, SLOT)
    wt_dtkr = jnp.concatenate([wt_dt, wt_kr], axis=0)
    hm = rms_fwd(x1, p["mix_pre_g"], "mix_pre")
    z = mm1(hm, wt_z, "nt", BF16, "in_z")
    xbc = mm1(hm, wt_xbc, "nt", BF16, "in_xbc")
    q_c = mm1(hm, wt_q, "nt", F32, "in_q", tn=QR)
    kv_c = mm1(hm, wt_kv, "nt", F32, "in_kv")
    dtkr = mm1(hm, wt_dtkr, "nt", F32, "in_dtkr")
    gl = mm1(hm, wt_gate, "nt", BF16, "in_gate")

    xbc_act = conv_fwd(xbc, p["conv_w"], p["conv_b"], nseq)
    y_ssd_core, prev, *arrived = ssd_fwd(xbc_act, dtkr, p["dt_bias"], p["a_log"], p["d_skip"], nseq, comm=stage.gather("ssd_fwd"))
    stage.gathered("ssd_fwd", arrived, W, p)
    yn = gated_norm_fwd(y_ssd_core, z, p["ssd_norm_g"], "ssd_norm")
    y_ssd = mm1(yn, W["w_ssd_proj"], "nn", BF16, "ssd_proj")

    slot_rows = lambda wt, per: jnp.pad(wt.reshape(MLA_H, per, -1), ((0, 0), (0, SLOT - per), (0, 0))).reshape(MLA_H * SLOT, -1)
    wq_s, wk_s, wv_s = slot_rows(W["w_uq"], QK), slot_rows(W["w_uk"], NOPE), slot_rows(W["w_uv"], VD)
    wo_s = slot_rows(W["w_mla_proj"], VD)
    qn = rms_fwd(q_c, p["q_norm_g"], "q_norm")
    rope_c, rope_s = rope_table(*_rope_inputs(positions))
    rope_args = [("rows", rope_c), ("rows", rope_s)]
    Qc, = mm([[(qn, wq_s, "nt")]], [BF16], "uq", epi=rope_q_epilogue, extras=rope_args, sub=4 if T % 1024 == 0 else 1)
    kvn = rms_fwd(kv_c, p["kv_norm_g"], "kv_norm")
    Kc, = mm([[(kvn, wk_s, "nt")]], [BF16], "uk", epi=rope_k_epilogue, extras=rope_args + [("rows", dtkr)],
             sub=4 if T % 1024 == 0 else 1)
    v_s = mm1(kvn, wv_s, "nt", BF16, "uv")
    o_s, lse, *arrived = attn_slot_fwd(Qc, Kc, v_s, nseq, comm=stage.gather("attn_fwd"))
    stage.gathered("attn_fwd", arrived, W, p)
    y_mla = mm1(o_s, wo_s, "nn", BF16, "mla_proj")

    merged = merge_fwd(gl, y_ssd, y_mla, p["gate_bias"], "merge")
    hmix, x2 = mm_resid(merged, W["w_out"], x1, p["mix_post_g"], 1.0, "mix_out")

    hq = rms_fwd(x2, p["xa_pre_g"], "xa_pre")
    mn = rms_fwd(mem2, p["mem_norm_g"], "mem_norm")
    xq = mm1(hq, W["w_xq"], "nn", BF16, "xq")
    xk = mm1(mn, W["w_xk"], "nn", BF16, "xk")
    xv = mm1(mn, W["w_xv"], "nn", BF16, "xv")
    xo, *arrived = xattn_fwd(xq, xk, xv, nseq, comm=stage.gather("xattn_fwd"))
    stage.gathered("xattn_fwd", arrived, W, p)
    ho, x3 = mm_resid(xo, W["w_xo"], x2, p["xa_post_g"], 1.0, "xo")

    dx4, ffn2, sq_cols = _ffn_fwd(x3, p["ffn2_pre_g"], p["ffn2_post_g"], W, p, "ffn2", stage, target=tgt.reshape(T, D))
    loss_row = (0.5 / D) * jnp.sum(sq_cols.reshape(-1, 128), axis=0, keepdims=True)

    gw, gs = {}, {}
    dx3, gs["ffn2_pre_g"], gs["ffn2_post_g"] = _ffn_bwd(
        dx4, ffn2, p["ffn2_pre_g"], p["ffn2_post_g"], W["ffn2_w_gate"], W["ffn2_w_up"], W["ffn2_w_down"], "ffn2", stage, gw)

    dho, gs["xa_post_g"] = resid_bwd(ho, p["xa_post_g"], dx3, 1.0, "xa_post_bwd")
    dxo = mm1(dho, W["w_xo"], "nt", BF16, "xo_bwd")
    gw["w_xo"] = _tn(xo, dho, "d_w_xo")
    dxq, dxk, dxv = xattn_bwd(xq, xk, xv, dxo, nseq)
    dx2, gs["xa_pre_g"] = mm_rms_bwd([(dxq, W["w_xq"], "nt")], x2, p["xa_pre_g"], "xq_bwd", resid=dx3)
    gw["w_xq"] = _tn(hq, dxq, "d_w_xq")
    dmn = mm([[(dxk, W["w_xk"], "nt"), (dxv, W["w_xv"], "nt")]], [F32], "xkv_bwd")[0]
    gw["w_xk"] = _tn(mn, dxk, "d_w_xk")
    gw["w_xv"] = _tn(mn, dxv, "d_w_xv")
    _, gs["mem_norm_g"] = rms_bwd(mem2, p["mem_norm_g"], dmn, "mem_norm_bwd", dx_dtype=BF16)

    dhmix, gs["mix_post_g"] = resid_bwd(hmix, p["mix_post_g"], dx2, 1.0, "mix_post_bwd")
    dmerged = mm1(dhmix, W["w_out"], "nt", F32, "mix_out_bwd")
    gw["w_out"] = _tn(merged, dhmix, "d_w_out")
    dys, dym, dgl, gs["gate_bias"] = merge_bwd(gl, y_ssd, y_mla, dmerged, p["gate_bias"], "merge_bwd")

    unslot = lambda g, per: g.reshape(MLA_H, SLOT, -1)[:, :per].reshape(MLA_H * per, -1)
    do_s = mm1(dym, wo_s, "nt", BF16, "mla_proj_bwd")
    gw["w_mla_proj"] = unslot(_tn(o_s, dym, "d_w_mla_proj"), VD)
    dQc, dKc, dv_s, *sent = attn_slot_bwd(Qc, Kc, v_s, o_s, lse, do_s, nseq, comm=stage.scatter("attn_bwd", gw))
    stage.scattered("attn_bwd", sent)
    dq_s, dkn_s, dkr = rope_slot_bwd(dQc, dKc, rope_c, rope_s, "rope_bwd")
    dq_c, gs["q_norm_g"] = mm_rms_bwd([(dq_s, wq_s, "nn")], q_c, p["q_norm_g"], "uq_bwd", dx_dtype=BF16)
    gw["w_uq"] = unslot(_tn(dq_s, qn, "d_w_uq"), QK)
    dkv_c, gs["kv_norm_g"] = mm_rms_bwd([(dkn_s, wk_s, "nn"), (dv_s, wv_s, "nn")], kv_c, p["kv_norm_g"], "ukv_bwd", dx_dtype=BF16)
    gw["w_uk"] = unslot(_tn(dkn_s, kvn, "d_w_uk"), NOPE)
    gw["w_uv"] = unslot(_tn(dv_s, kvn, "d_w_uv"), VD)

    dyn = mm1(dys, W["w_ssd_proj"], "nt", F32, "ssd_proj_bwd")
    gw["w_ssd_proj"] = _tn(yn, dys, "d_w_ssd_proj")
    dyc, dz, gs["ssd_norm_g"] = gated_norm_bwd(y_ssd_core, z, dyn, p["ssd_norm_g"], "ssd_norm_bwd")
    dxbc_act, ddtr, gs["dt_bias"], gs["a_log"], gs["d_skip"], *sent = ssd_bwd(
        xbc_act, dtkr, p["dt_bias"], p["a_log"], p["d_skip"], prev, dyc, nseq, comm=stage.scatter("ssd_bwd", gw))
    stage.scattered("ssd_bwd", sent)
    dxbc, gs["conv_w"], gs["conv_b"] = conv_bwd(xbc, p["conv_w"], p["conv_b"], dxbc_act, nseq)

    gw["w_in"] = jnp.concatenate([_tn(dz, hm, "d_w_in_z"), _tn(dxbc, hm, "d_w_in_xbc"), _tn(ddtr, hm, "d_w_in_dt")[:SSD_H],
                                  _tn(dq_c, hm, "d_w_in_q"), _tn(dkv_c, hm, "d_w_in_kv"), _tn(dkr, hm, "d_w_in_kr")[:ROPE],
                                  _tn(dgl, hm, "d_w_in_gate")], axis=0)
    dx1, gs["mix_pre_g"], *sent = mm_rms_bwd(
        [(dz, wt_z, "nn"), (dxbc, wt_xbc, "nn"), (ddtr, wt_dt, "nn"), (dq_c, wt_q, "nn"), (dkv_c, wt_kv, "nn"),
         (dkr, wt_kr, "nn"), (dgl, wt_gate, "nn")], x1, p["mix_pre_g"], "in_bwd", resid=dx2, comm=stage.scatter("in_bwd", gw))
    stage.scattered("in_bwd", sent)

    dx0, gs["ffn1_pre_g"], gs["ffn1_post_g"] = _ffn_bwd(
        dx1, ffn1, p["ffn1_pre_g"], p["ffn1_post_g"], W["ffn1_w_gate"], W["ffn1_w_up"], W["ffn1_w_down"], "ffn1", stage, gw)
    return loss_row, dx0.reshape(x.shape), gw, gs


def kernel(x, mem, positions, ffn1_pre_g, ffn1_w_gate, ffn1_w_up, ffn1_w_down, ffn1_post_g, mix_pre_g, w_in, conv_w, conv_b, dt_bias, a_log, d_skip, ssd_norm_g, w_ssd_proj, q_norm_g, w_uq, kv_norm_g, w_uk, w_uv, w_mla_proj, gate_bias, w_out, mix_post_g, xa_pre_g, mem_norm_g, w_xq, w_xk, w_xv, w_xo, xa_post_g, ffn2_pre_g, ffn2_w_gate, ffn2_w_up, ffn2_w_down, ffn2_post_g, loss_target, m_ffn1_pre_g, m_ffn1_w_gate, m_ffn1_w_up, m_ffn1_w_down, m_ffn1_post_g, m_mix_pre_g, m_w_in, m_conv_w, m_conv_b, m_dt_bias, m_a_log, m_d_skip, m_ssd_norm_g, m_w_ssd_proj, m_q_norm_g, m_w_uq, m_kv_norm_g, m_w_uk, m_w_uv, m_w_mla_proj, m_gate_bias, m_w_out, m_mix_post_g, m_xa_pre_g, m_mem_norm_g, m_w_xq, m_w_xk, m_w_xv, m_w_xo, m_xa_post_g, m_ffn2_pre_g, m_ffn2_w_gate, m_ffn2_w_up, m_ffn2_w_down, m_ffn2_post_g, v_ffn1_pre_g, v_ffn1_w_gate, v_ffn1_w_up, v_ffn1_w_down, v_ffn1_post_g, v_mix_pre_g, v_w_in, v_conv_w, v_conv_b, v_dt_bias, v_a_log, v_d_skip, v_ssd_norm_g, v_w_ssd_proj, v_q_norm_g, v_w_uq, v_kv_norm_g, v_w_uk, v_w_uv, v_w_mla_proj, v_gate_bias, v_w_out, v_mix_post_g, v_xa_pre_g, v_mem_norm_g, v_w_xq, v_w_xk, v_w_xv, v_w_xo, v_xa_post_g, v_ffn2_pre_g, v_ffn2_w_gate, v_ffn2_w_up, v_ffn2_w_down, v_ffn2_post_g):
    a = dict(locals())
    w = {n: a[n] for n in WEIGHTS}
    m = {n: a["m_" + n] for n in WEIGHTS}
    v = {n: a["v_" + n] for n in WEIGHTS}

    stage = Stage(w)
    W, p = {}, {n: w[n] for n in SMALL}
    loss_row, grad_x, gw, gs = _local_step(x, mem, positions, loss_target, W, p, stage)

    sm = _pack_small([gs[n] for n in SMALL], loss_row=loss_row, conv_w=gs["conv_w"])
    srecv, = run_comm(ScatterComm([[jnp.broadcast_to(sm[None], (N_DEV,) + sm.shape)]]), "exchange_small")
    s_rows = sum_slots(srecv, "sum_small", tr=sm.shape[0])
    last_piece, = stage.pieces(LAST_EXCHANGE, gw)[0]
    sems, last_piece, landed, token = exchange_start(last_piece, s_rows)
    grads, delta, new_m, new_v = {}, {}, {}, {}
    raw_results = []

    def finish(n, buf, piece, token=None):
        col = KIND[n] == "col"
        turn = (lambda t: t.T) if col else (lambda t: t)
        K = w[n].shape[1]
        if col and buf.shape[2] != K:
            buf = buf.reshape(buf.shape[0], -1, K)
        res = adamw_from_slots(buf, piece, turn(w[n][0]), turn(m[n][0]), turn(v[n][0]), "adamw_" + n, token=token)
        raw_results.append(res[3])
        grads[n], delta[n], new_m[n], new_v[n] = [turn(r)[None] for r in res]

    parts = {}
    for tag, groups in SCATTER_PLAN.items():
        if tag == LAST_EXCHANGE:
            continue
        for names, buf in zip(groups, stage.recv[tag]):
            for piece, n in enumerate(names):
                if n in PARTS:
                    parts[n] = sum_slots(buf, "sum_" + n.replace("#", "_"), tr=buf.shape[1])
                else:
                    finish(n, buf, piece, token)
    for base in sorted({PARTS[pn][0] for pn in parts}):
        rows = jnp.concatenate([parts[pn] for pn in _parts_of(base, "#")], axis=0)
        finish(base, rows[None], 0, token)
    landed = exchange_wait(sems, last_piece, landed, after=raw_results[-1])
    me = _dev_index()
    landed = lax.dynamic_update_index_in_dim(landed, lax.dynamic_index_in_dim(last_piece, me, 0, keepdims=False), me, 0)
    finish(SCATTER_PLAN[LAST_EXCHANGE][0][0], landed, 0)
    conv_w_full = p["conv_w"]
    small = adamw_small(s_rows, [w[n] for n in SMALL], [m[n] for n in SMALL], [v[n] for n in SMALL])
    for t, vals in zip((grads, delta, new_m, new_v), small):
        t.update(zip(SMALL, vals))
    r1 = sum(-(-w[n].shape[1] // 128) for n in SMALL)
    ncw = math.prod(conv_w_full.shape) // 128
    cw_grad_full = s_rows[r1:r1 + ncw].reshape(conv_w_full.shape)
    wsh = conv_w.shape[2]
    grads["conv_w"] = lax.dynamic_slice_in_dim(cw_grad_full, _dev_index() * wsh, wsh, axis=1)[None]
    loss = jnp.sum(s_rows[r1 + ncw])
    d_, m_, v_ = adamw(conv_w[0], grads["conv_w"][0], m["conv_w"][0], v["conv_w"][0], "adamw_conv_w")
    delta["conv_w"], new_m["conv_w"], new_v["conv_w"] = d_[None], m_[None], v_[None]
    return (loss, grad_x, *[grads[n] for n in WEIGHTS], *[delta[n] for n in WEIGHTS],
            *[new_m[n] for n in WEIGHTS], *[new_v[n] for n in WEIGHTS])
```

```python
import functools
import math

import jax
import jax.numpy as jnp
from jax import lax
from jax.experimental import pallas as pl
from jax.experimental.pallas import tpu as pltpu

F32, BF16 = jnp.float32, jnp.bfloat16
HI = lax.Precision.HIGHEST
MESH = pl.DeviceIdType.MESH
N_DEV = 8

D = 1024
DFF = 2816
SSD_H, SSD_P, SSD_G, SSD_N, SSD_L = 16, 64, 2, 128, 128
SSD_INNER = SSD_H * SSD_P
CONV_K, CONV_CH = 4, 1536
MLA_H, QR, KVR, NOPE, ROPE, VD = 16, 384, 256, 64, 32, 64
QK = NOPE + ROPE
ROPE_THETA = 10000.0
XA_H, XA_D = 4, 256
EPS = 1e-6
FFN_RES = 0.5
LR, B1, B2, AEPS, WD, STEP = 0.001, 0.9, 0.999, 1e-08, 0.01, 10

VMEM_LIMIT = 56 * 2**20


def _cp(*sem):
    return pltpu.CompilerParams(dimension_semantics=sem, vmem_limit_bytes=VMEM_LIMIT)


def _sigmoid(x):
    return 1.0 / (1.0 + jnp.exp(-x))


def _softplus(x):
    return jnp.where(x > 20.0, x, jnp.log(1.0 + jnp.exp(jnp.minimum(x, 20.0))))


def _dot(a, b, dims="nn"):
    ca = 0 if dims[0] == "t" else 1
    cb = 1 if dims[1] == "t" else 0
    return lax.dot_general(a.astype(BF16), b.astype(BF16), (((ca,), (cb,)), ((), ())), preferred_element_type=F32)


def _dot_sel(a, b, dims="nn", split="a", terms=3):
    r = (a if split == "a" else b).astype(F32)
    out = None
    for t in range(terms):
        piece = r.astype(BF16)
        if t + 1 < terms:
            r = r - piece.astype(F32)
        d = _dot(piece, b, dims) if split == "a" else _dot(a, piece, dims)
        out = d if out is None else out + d
    return out


def _ssd_common(dtr, dtb, alog):
    L = dtr.shape[0]
    dt = _softplus(dtr + dtb)
    a = -jnp.exp(alog)
    adt = dt * a
    r = lax.broadcasted_iota(jnp.int32, (L, L), 0)
    c = lax.broadcasted_iota(jnp.int32, (L, L), 1)
    lower = r >= c
    tri = lower.astype(F32)
    cs = _dot_sel(tri, adt, "nn", split="b")
    cs_t = _dot_sel(adt, tri, "tt")
    return dt, a, cs, cs_t, lower


def _head_expand():
    hh = lax.broadcasted_iota(jnp.int32, (SSD_H, SSD_INNER), 0)
    jj = lax.broadcasted_iota(jnp.int32, (SSD_H, SSD_INNER), 1)
    return ((jj >= hh * SSD_P) & (jj < hh * SSD_P + SSD_P)).astype(F32)


def _head_reduce():
    hh = lax.broadcasted_iota(jnp.int32, (SSD_INNER, SSD_H), 1)
    jj = lax.broadcasted_iota(jnp.int32, (SSD_INNER, SSD_H), 0)
    return ((jj >= hh * SSD_P) & (jj < hh * SSD_P + SSD_P)).astype(F32)


def ssd_fwd(xbc, dtr, dtb, alog, dsk, nseq, comm=None):
    T = xbc.shape[0]
    S = T // nseq
    C = S // SSD_L
    L = SSD_L
    NP = SSD_H // 2

    def body(x_ref, b_ref, c_ref, dtr_ref, dtb_ref, alog_ref, dsk_ref, y_ref, prev_ref, st_ref):
        ci = pl.program_id(1)

        @pl.when(ci == 0)
        def _():
            st_ref[...] = jnp.zeros_like(st_ref)

        dt, a, cs, cs_t, lower = _ssd_common(dtr_ref[:, 0:SSD_H], dtb_ref[...], alog_ref[...])
        E = _head_expand()
        X = x_ref[...].astype(F32)
        dt_e = _dot_sel(dt, E)
        cs_e = _dot_sel(cs, E)
        csl_e = cs_e[L - 1:L, :]
        Xd = X * dt_e
        Xf = Xd * jnp.exp(csl_e - cs_e)
        e_e = jnp.exp(cs_e)
        skip = _dot_sel(dsk_ref[...], E) * X
        lane = lax.broadcasted_iota(jnp.int32, (1, 2 * SSD_P), 1)
        rowp = lax.broadcasted_iota(jnp.int32, (2 * SSD_P, 1), 0)
        for g in range(SSD_G):
            Bg = b_ref[:, g * SSD_N:(g + 1) * SSD_N]
            Cg = c_ref[:, g * SSD_N:(g + 1) * SSD_N]
            cb = _dot(Cg, Bg, "nt")
            for pp in range(NP // SSD_G):
                p = g * (NP // SSD_G) + pp
                sl = slice(p * 2 * SSD_P, (p + 1) * 2 * SSD_P)
                Xd_p = Xd[:, sl]
                yd = jnp.zeros((L, 2 * SSD_P), F32)
                for q in range(2):
                    h = 2 * p + q
                    m = jnp.where(lower, jnp.exp(jnp.minimum(cs[:, h:h + 1] - cs_t[h:h + 1, :], 0.0)), 0.0)
                    mask = (lane >= q * SSD_P) & (lane < (q + 1) * SSD_P)
                    yd = yd + _dot(cb * m, jnp.where(mask, Xd_p, 0.0))
                S0 = st_ref[p]
                prev_ref[0, 0, p] = S0
                z = _dot(Cg, S0, "nt")
                y_ref[:, sl] = (skip[:, sl] + yd + z * e_e[:, sl]).astype(y_ref.dtype)
                h0 = 2 * p
                dec = jnp.where(rowp < SSD_P, jnp.exp(cs[L - 1:L, h0:h0 + 1]), jnp.exp(cs[L - 1:L, h0 + 1:h0 + 2]))
                st_ref[p] = S0 * dec + _dot(Xf[:, sl], Bg, "tn")

    row = lambda b, c: (b * C + c, 0)
    small = pl.BlockSpec((1, SSD_H), lambda b, c: (0, 0))
    return _call_with_comm(
        body, (nseq, C), "ssd_fwd",
        [pl.BlockSpec((L, SSD_INNER), row),
         pl.BlockSpec((L, SSD_G * SSD_N), lambda b, c: (b * C + c, SSD_INNER // (SSD_G * SSD_N))),
         pl.BlockSpec((L, SSD_G * SSD_N), lambda b, c: (b * C + c, SSD_INNER // (SSD_G * SSD_N) + 1)),
         pl.BlockSpec((L, 128), row), small, small, small],
        [xbc, xbc, xbc, dtr, dtb, alog, dsk],
        [pl.BlockSpec((L, SSD_INNER), row), pl.BlockSpec((1, 1, NP, 2 * SSD_P, SSD_N), lambda b, c: (b, c, 0, 0, 0))],
        [jax.ShapeDtypeStruct((T, SSD_INNER), BF16), jax.ShapeDtypeStruct((nseq, C, NP, 2 * SSD_P, SSD_N), F32)],
        comm, scratch=[pltpu.VMEM((NP, 2 * SSD_P, SSD_N), F32)], sem=("parallel", "arbitrary"))


def ssd_bwd(xbc, dtr, dtb, alog, dsk, prev, dy, nseq, comm=None):
    T = xbc.shape[0]
    S = T // nseq
    C = S // SSD_L
    L = SSD_L
    NP = SSD_H // 2

    def body(x_ref, b_ref, c_ref, dtr_ref, dtb_ref, alog_ref, dsk_ref, prev_ref, dy_ref,
             dxbc_ref, ddtr_ref, ddtb_ref, dalog_ref, ddsk_ref, ds_ref, stg_ref):
        bi = pl.program_id(0)
        ci = pl.program_id(1)

        @pl.when(ci == 0)
        def _():
            ds_ref[...] = jnp.zeros_like(ds_ref)

        @pl.when((ci == 0) & (bi == 0))
        def _():
            ddtb_ref[...] = jnp.zeros_like(ddtb_ref)
            dalog_ref[...] = jnp.zeros_like(dalog_ref)
            ddsk_ref[...] = jnp.zeros_like(ddsk_ref)

        dtr = dtr_ref[:, 0:SSD_H]
        dtb = dtb_ref[...]
        dt, a, cs, cs_t, lower = _ssd_common(dtr, dtb, alog_ref[...])
        upper = lax.broadcasted_iota(jnp.int32, (L, L), 1) >= lax.broadcasted_iota(jnp.int32, (L, L), 0)
        E = _head_expand()
        ET = _head_reduce()
        X = x_ref[...].astype(F32)
        dY = dy_ref[...].astype(F32)
        dt_e = _dot_sel(dt, E)
        cs_e = _dot_sel(cs, E)
        csl_e = cs_e[L - 1:L, :]
        f_e = jnp.exp(csl_e - cs_e)
        e_e = jnp.exp(cs_e)
        dsk_e = _dot_sel(dsk_ref[...], E)
        Xd = X * dt_e
        Xf = Xd * f_e
        lane = lax.broadcasted_iota(jnp.int32, (1, 2 * SSD_P), 1)
        rowp = lax.broadcasted_iota(jnp.int32, (2 * SSD_P, 1), 0)
        hsel = lax.broadcasted_iota(jnp.int32, (1, SSD_H), 1)
        dcs = jnp.zeros((L, SSD_H), F32)
        dcsl = jnp.zeros((1, SSD_H), F32)
        ones_l = jnp.ones((L, 128), F32)
        for g in range(SSD_G):
            Bg = b_ref[:, g * SSD_N:(g + 1) * SSD_N]
            Cg = c_ref[:, g * SSD_N:(g + 1) * SSD_N]
            cb = _dot(Cg, Bg, "nt")
            dB = jnp.zeros((L, SSD_N), F32)
            dC = jnp.zeros((L, SSD_N), F32)
            for pp in range(NP // SSD_G):
                p = g * (NP // SSD_G) + pp
                sl = slice(p * 2 * SSD_P, (p + 1) * 2 * SSD_P)
                Xd_p = Xd[:, sl]
                dY_p = dY[:, sl]
                dXd_p = jnp.zeros((L, 2 * SSD_P), F32)
                for q in range(2):
                    h = 2 * p + q
                    mask = (lane >= q * SSD_P) & (lane < (q + 1) * SSD_P)
                    col = cs[:, h:h + 1]
                    rw = cs_t[h:h + 1, :]
                    m = jnp.where(lower, jnp.exp(jnp.minimum(col - rw, 0.0)), 0.0)
                    dYm = jnp.where(mask, dY_p, 0.0)
                    dW = _dot(dYm, Xd_p, "nt")
                    w = cb * m
                    dcb = dW * m
                    dC = dC + _dot(dcb, Bg)
                    dB = dB + _dot(dcb, Cg, "tn")
                    dXd_p = dXd_p + jnp.where(mask, _dot(w, dY_p, "tn"), 0.0)
                    qq = dW * w
                    qcol = jnp.sum(qq, axis=1, keepdims=True) - _dot_sel(qq, ones_l, "tn", terms=2)[:, 0:1]
                    dcs = dcs + qcol * (hsel == h).astype(F32)
                S0 = prev_ref[0, 0, p]
                dSn = ds_ref[p]
                dZ = dY_p * e_e[:, sl]
                dC = dC + _dot(dZ, S0)
                h0 = 2 * p
                el0 = jnp.exp(cs[L - 1:L, h0:h0 + 1])
                el1 = jnp.exp(cs[L - 1:L, h0 + 1:h0 + 2])
                dec = jnp.where(rowp < SSD_P, el0, el1)
                ds_ref[p] = dSn * dec + _dot(dZ, Cg, "tn")
                dXf_p = _dot(Bg, dSn, "nt")
                dB = dB + _dot(Xf[:, sl], dSn)
                rs = jnp.sum(dSn * S0, axis=1, keepdims=True)
                s0 = jnp.sum(jnp.where(rowp < SSD_P, rs, 0.0), axis=0, keepdims=True) * el0
                s1 = jnp.sum(jnp.where(rowp >= SSD_P, rs, 0.0), axis=0, keepdims=True) * el1
                dcsl = dcsl + s0 * (hsel == h0).astype(F32) + s1 * (hsel == h0 + 1).astype(F32)
                y_off = _dot(Cg, S0, "nt") * e_e[:, sl]
                t1 = dY_p * y_off - dXf_p * Xf[:, sl]
                r1 = jnp.where(lane < SSD_P, t1, 0.0)
                c0 = jnp.sum(r1, axis=1, keepdims=True)
                c1 = jnp.sum(t1 - r1, axis=1, keepdims=True)
                dcs = dcs + c0 * (hsel == h0).astype(F32) + c1 * (hsel == h0 + 1).astype(F32)
                t2 = dXf_p * Xf[:, sl]
                r2 = jnp.where(lane < SSD_P, t2, 0.0)
                dcsl = dcsl + jnp.sum(r2, keepdims=True) * (hsel == h0).astype(F32) \
                    + jnp.sum(t2 - r2, keepdims=True) * (hsel == h0 + 1).astype(F32)
                stg_ref[:, sl] = dXd_p + dXf_p * f_e[:, sl]
            dxbc_ref[:, SSD_INNER + g * SSD_N:SSD_INNER + (g + 1) * SSD_N] = dB.astype(dxbc_ref.dtype)
            dxbc_ref[:, SSD_INNER + (SSD_G + g) * SSD_N:SSD_INNER + (SSD_G + g + 1) * SSD_N] = dC.astype(dxbc_ref.dtype)
        dXd = stg_ref[...]
        dxbc_ref[:, 0:SSD_INNER] = (dXd * dt_e + dsk_e * dY).astype(dxbc_ref.dtype)
        rowl = lax.broadcasted_iota(jnp.int32, (L, 1), 0)
        dcs = dcs + jnp.where(rowl == L - 1, dcsl, 0.0)
        dalpha = _dot_sel(upper.astype(F32), dcs, split="b")
        ddt = _dot_sel(dXd * X, ET, terms=2) + dalpha * a
        dalog_ref[...] += jnp.sum(dalpha * dt, axis=0, keepdims=True) * a
        ddtr = ddt * _sigmoid(dtr + dtb)
        spread = (lax.broadcasted_iota(jnp.int32, (SSD_H, 128), 0) == lax.broadcasted_iota(jnp.int32, (SSD_H, 128), 1)).astype(F32)
        ddtr_ref[...] = _dot(ddtr, spread).astype(ddtr_ref.dtype)
        ddtb_ref[...] += jnp.sum(ddtr, axis=0, keepdims=True)
        ddsk_ref[...] += jnp.sum(_dot_sel(dY * X, ET, terms=2), axis=0, keepdims=True)

    rowr = lambda b, c: (b * C + (C - 1 - c), 0)
    small = pl.BlockSpec((1, SSD_H), lambda b, c: (0, 0))
    return _call_with_comm(
        body, (nseq, C), "ssd_bwd",
        [pl.BlockSpec((L, SSD_INNER), rowr),
         pl.BlockSpec((L, SSD_G * SSD_N), lambda b, c: (b * C + (C - 1 - c), SSD_INNER // (SSD_G * SSD_N))),
         pl.BlockSpec((L, SSD_G * SSD_N), lambda b, c: (b * C + (C - 1 - c), SSD_INNER // (SSD_G * SSD_N) + 1)),
         pl.BlockSpec((L, 128), rowr), small, small, small,
         pl.BlockSpec((1, 1, NP, 2 * SSD_P, SSD_N), lambda b, c: (b, C - 1 - c, 0, 0, 0)),
         pl.BlockSpec((L, SSD_INNER), rowr)],
        [xbc, xbc, xbc, dtr, dtb, alog, dsk, prev, dy],
        [pl.BlockSpec((L, CONV_CH), rowr), pl.BlockSpec((L, 128), rowr), small, small, small],
        [jax.ShapeDtypeStruct((T, CONV_CH), BF16), jax.ShapeDtypeStruct((T, 128), BF16),
         jax.ShapeDtypeStruct((1, SSD_H), F32), jax.ShapeDtypeStruct((1, SSD_H), F32), jax.ShapeDtypeStruct((1, SSD_H), F32)],
        comm, scratch=[pltpu.VMEM((NP, 2 * SSD_P, SSD_N), F32), pltpu.VMEM((L, SSD_INNER), F32)], sem=("arbitrary", "arbitrary"))


SLOT = 128
ATT_T = 512
ATT_HP = 1
LOG2E = math.log2(math.e)
Q_SCALE = QK ** -0.5 * LOG2E


def _col_to_row(col):
    n = col.shape[0]
    eye = lax.broadcasted_iota(jnp.int32, (n, n), 0) == lax.broadcasted_iota(jnp.int32, (n, n), 1)
    return jnp.sum(jnp.where(eye, col, 0.0), axis=0, keepdims=True)


def attn_slot_fwd(q, k, v, nseq, comm=None):
    T = q.shape[0]
    S = T // nseq
    t = min(ATT_T, S)
    nb = S // t
    cols = [slice(h * SLOT, (h + 1) * SLOT) for h in range(ATT_HP)]

    def body(q_ref, k_ref, v_ref, o_ref, lse_ref):
        causal = lax.broadcasted_iota(jnp.int32, (t, t), 1) <= lax.broadcasted_iota(jnp.int32, (t, t), 0)
        for qi in range(nb):
            rows = slice(qi * t, (qi + 1) * t)
            state = [None] * ATT_HP
            for kj in range(qi + 1):
                keys = slice(kj * t, (kj + 1) * t)
                for h, c in enumerate(cols):
                    s = _dot(q_ref[rows, c], k_ref[keys, c], "nt")
                    if kj == qi:
                        s = jnp.where(causal, s, -1e30)
                    bm = jnp.max(s, axis=1, keepdims=True)
                    if kj == 0:
                        p = jnp.exp2(s - bm)
                        state[h] = (bm, jnp.sum(p, axis=1, keepdims=True), _dot(p, v_ref[keys, c]))
                    else:
                        m, l, acc = state[h]
                        m_new = jnp.maximum(m, bm)
                        corr = jnp.exp2(m - m_new)
                        p = jnp.exp2(s - m_new)
                        state[h] = (m_new, l * corr + jnp.sum(p, axis=1, keepdims=True), acc * corr + _dot(p, v_ref[keys, c]))
            for h, c in enumerate(cols):
                m, l, acc = state[h]
                o_ref[rows, c] = (acc / l).astype(o_ref.dtype)
                lse_ref[0, h, :, rows] = _col_to_row(m + jnp.log2(l))

    blk = pl.BlockSpec((S, ATT_HP * SLOT), lambda b, h: (b, h))
    return _call_with_comm(
        body, (nseq, MLA_H // ATT_HP), "attn_fwd", [blk, blk, blk], [q, k, v],
        [blk, pl.BlockSpec((1, ATT_HP, 1, S), lambda b, h: (b, h, 0, 0))],
        [jax.ShapeDtypeStruct((T, MLA_H * SLOT), BF16), jax.ShapeDtypeStruct((nseq, MLA_H, 1, S), F32)], comm)


def attn_slot_bwd(q, k, v, o, lse, do, nseq, comm=None):
    T = q.shape[0]
    S = T // nseq
    t = min(ATT_T, S)
    nb = S // t
    scale = QK ** -0.5
    cols = [slice(h * SLOT, (h + 1) * SLOT) for h in range(ATT_HP)]

    def body(q_ref, k_ref, v_ref, o_ref, lse_ref, do_ref, dq_ref, dk_ref, dv_ref, dqa_ref):
        causal_t = lax.broadcasted_iota(jnp.int32, (t, t), 0) <= lax.broadcasted_iota(jnp.int32, (t, t), 1)
        ones = jnp.ones((8, SLOT), F32)
        delta = {}
        for qi in range(nb):
            sl = slice(qi * t, (qi + 1) * t)
            for h, c in enumerate(cols):
                prod = do_ref[sl, c].astype(F32) * o_ref[sl, c].astype(F32)
                delta[h, qi] = _dot_sel(ones, prod, "nt", split="b", terms=2)[0:1, :]
        for kj in range(nb):
            ks = slice(kj * t, (kj + 1) * t)
            dk = [None] * ATT_HP
            dv = [None] * ATT_HP
            for qi in range(kj, nb):
                sl = slice(qi * t, (qi + 1) * t)
                for h, c in enumerate(cols):
                    kb, vb, qb, dob = k_ref[ks, c], v_ref[ks, c], q_ref[sl, c], do_ref[sl, c]
                    st = _dot(kb, qb, "nt")
                    pt = jnp.exp2(st - lse_ref[0, h, :, sl])
                    if qi == kj:
                        pt = jnp.where(causal_t, pt, 0.0)
                    dpt = _dot(vb, dob, "nt")
                    dst = (pt * (dpt - delta[h, qi])).astype(BF16)
                    dvc = _dot(pt, dob)
                    dkc = _dot(dst, qb) * (1.0 / LOG2E)
                    dv[h] = dvc if dv[h] is None else dv[h] + dvc
                    dk[h] = dkc if dk[h] is None else dk[h] + dkc
                    dqc = _dot(dst, kb, "tn") * scale
                    if kj > 0:
                        dqc = dqc + dqa_ref[sl, c]
                    if qi == kj:
                        dq_ref[sl, c] = dqc.astype(dq_ref.dtype)
                    else:
                        dqa_ref[sl, c] = dqc
            for h, c in enumerate(cols):
                dk_ref[ks, c] = dk[h].astype(dk_ref.dtype)
                dv_ref[ks, c] = dv[h].astype(dv_ref.dtype)

    blk = pl.BlockSpec((S, ATT_HP * SLOT), lambda b, h: (b, h))
    lse_spec = pl.BlockSpec((1, ATT_HP, 1, S), lambda b, h: (b, h, 0, 0))
    W = MLA_H * SLOT
    return _call_with_comm(
        body, (nseq, MLA_H // ATT_HP), "attn_bwd", [blk, blk, blk, blk, lse_spec, blk], [q, k, v, o, lse, do], [blk, blk, blk],
        [jax.ShapeDtypeStruct((T, W), BF16)] * 3, comm, scratch=[pltpu.VMEM((S, ATT_HP * SLOT), F32)])


def _rope_coeffs(pos, inv):
    half = ROPE // 2
    ang = pos * inv
    lane = lax.broadcasted_iota(jnp.int32, (1, SLOT), 1)
    sn = jnp.sin(ang)
    C = jnp.where(lane < NOPE, 1.0, jnp.where(lane < QK, jnp.cos(ang), 0.0))
    Sg = jnp.where((lane >= NOPE) & (lane < NOPE + half), -sn, jnp.where((lane >= NOPE + half) & (lane < QK), sn, 0.0))
    return C, Sg


def _rope_inputs(positions):
    half = ROPE // 2
    inv = ROPE_THETA ** (-jnp.arange(0, ROPE, 2, dtype=F32) / ROPE)
    row = jnp.zeros((1, SLOT), F32).at[0, NOPE:NOPE + half].set(inv).at[0, NOPE + half:QK].set(inv)
    return positions.astype(F32).reshape(-1, 1), row


def _place_k_rope(kr_lanes):
    r = lax.broadcasted_iota(jnp.int32, (SLOT, SLOT), 0)
    c = lax.broadcasted_iota(jnp.int32, (SLOT, SLOT), 1)
    return _dot_sel(kr_lanes, ((c == r + NOPE) & (r < ROPE)).astype(F32))


def rope_table(pos, inv):
    return rowwise(_rope_coeffs, [pos], [inv], [(SLOT, F32), (SLOT, F32)], [], "rope_table")


def rope_q_epilogue(accs, ex):
    C, Sg = ex[0], ex[1]
    reps = accs[0].shape[1] // SLOT
    return ((accs[0] * jnp.tile(C, (1, reps)) + _rope_swap(accs[0]) * jnp.tile(Sg, (1, reps))) * Q_SCALE,)


def rope_k_epilogue(accs, ex):
    C, Sg = ex[0], ex[1]
    kr = _place_k_rope(ex[2][:, SLOT:2 * SLOT])
    kr = kr * C + _rope_swap(kr) * Sg
    return (accs[0] + jnp.tile(kr, (1, accs[0].shape[1] // SLOT)),)


def _rope_swap(x):
    W = x.shape[1]
    half = ROPE // 2
    lane = lax.broadcasted_iota(jnp.int32, (1, W), 1) & (SLOT - 1)
    up = pltpu.roll(x, W - half, axis=1)
    dn = pltpu.roll(x, half, axis=1)
    return jnp.where((lane >= NOPE) & (lane < NOPE + half), up, jnp.where((lane >= NOPE + half) & (lane < QK), dn, 0.0))


def rope_slot_bwd(dq, dk, C, Sg, name):
    def fn(dqv, dkv, C, Sg):
        ct, stl = jnp.tile(C, (1, MLA_H)), jnp.tile(Sg, (1, MLA_H))
        dqo = dqv * ct - _rope_swap(dqv) * stl
        tot = dkv[:, 0:SLOT]
        for h in range(1, MLA_H):
            tot = tot + dkv[:, h * SLOT:(h + 1) * SLOT]
        u = tot * C - _rope_swap(tot) * Sg
        r = lax.broadcasted_iota(jnp.int32, (SLOT, SLOT), 0)
        c = lax.broadcasted_iota(jnp.int32, (SLOT, SLOT), 1)
        unplace = ((r == c + NOPE) & (c < ROPE)).astype(F32)
        return dqo, dkv, _dot_sel(u, unplace, terms=2)
    W = MLA_H * SLOT
    return rowwise(fn, [dq, dk, C, Sg], [], [(W, BF16), (W, BF16), (SLOT, BF16)], [], name)


XA_BLK = 512


def xattn_fwd(q, k, v, nseq, comm=None):
    T = q.shape[0]
    S = T // nseq
    M = k.shape[0] // nseq
    tq = min(XA_BLK, S)
    nq = S // tq
    scale = XA_D ** -0.5

    def body(q_ref, k_ref, v_ref, o_ref):
        s = _dot(q_ref[...], k_ref[...], "nt") * scale
        p = jnp.exp(s - jnp.max(s, axis=1, keepdims=True))
        p = p / jnp.sum(p, axis=1, keepdims=True)
        o_ref[...] = _dot(p, v_ref[...]).astype(o_ref.dtype)

    qs = pl.BlockSpec((tq, XA_D), lambda b, h, i: (b * nq + i, h))
    ks = pl.BlockSpec((M, XA_D), lambda b, h, i: (b, h))
    return _call_with_comm(body, (nseq, XA_H, nq), "xattn_fwd", [qs, ks, ks], [q, k, v], [qs],
                           [jax.ShapeDtypeStruct((T, XA_H * XA_D), BF16)], comm)


def xattn_bwd(q, k, v, do, nseq):
    T = q.shape[0]
    S = T // nseq
    M = k.shape[0] // nseq
    tq = min(XA_BLK, S)
    nq = S // tq
    scale = XA_D ** -0.5

    def body(q_ref, k_ref, v_ref, do_ref, dq_ref, dk_ref, dv_ref):
        @pl.when(pl.program_id(2) == 0)
        def _():
            dk_ref[...] = jnp.zeros_like(dk_ref)
            dv_ref[...] = jnp.zeros_like(dv_ref)

        qb, kb, vb, dob = q_ref[...], k_ref[...], v_ref[...], do_ref[...]
        s = _dot(qb, kb, "nt") * scale
        p = jnp.exp(s - jnp.max(s, axis=1, keepdims=True))
        p = p / jnp.sum(p, axis=1, keepdims=True)
        dp = _dot(dob, vb, "nt")
        ds = p * (dp - jnp.sum(dp * p, axis=1, keepdims=True)) * scale
        dq_ref[...] = _dot(ds, kb).astype(dq_ref.dtype)
        dk_ref[...] += _dot(ds, qb, "tn")
        dv_ref[...] += _dot(p, dob, "tn")

    qs = pl.BlockSpec((tq, XA_D), lambda b, h, i: (b * nq + i, h))
    ks = pl.BlockSpec((M, XA_D), lambda b, h, i: (b, h))
    return pl.pallas_call(
        body, grid=(nseq, XA_H, nq), name="xattn_bwd", in_specs=[qs, ks, ks, qs], out_specs=[qs, ks, ks],
        out_shape=[jax.ShapeDtypeStruct((T, XA_H * XA_D), BF16), jax.ShapeDtypeStruct(k.shape, F32),
                   jax.ShapeDtypeStruct(k.shape, F32)],
        compiler_params=_cp("parallel", "parallel", "arbitrary"),
    )(q, k, v, do)


CONV_BLK = 256


def _shift_down(x, s, rows):
    if s == 0:
        return x
    return jnp.where(rows >= s, pltpu.roll(x, s, axis=0), 0.0)


def _shift_up(x, s, rows):
    if s == 0:
        return x
    S = x.shape[0]
    return jnp.where(rows < S - s, pltpu.roll(x, S - s, axis=0), 0.0)


def conv_fwd(x, w, b, nseq):
    T, CH = x.shape
    S = T // nseq

    def body(x_ref, w_ref, b_ref, o_ref):
        xv = x_ref[...].astype(F32)
        rows = lax.broadcasted_iota(jnp.int32, (S, 1), 0)
        c = jnp.zeros_like(xv) + b_ref[...]
        for kk in range(CONV_K):
            c = c + w_ref[kk:kk + 1, :] * _shift_down(xv, CONV_K - 1 - kk, rows)
        o_ref[...] = (c * _sigmoid(c)).astype(o_ref.dtype)

    xs = pl.BlockSpec((S, CONV_BLK), lambda j, bb: (bb, j))
    return pl.pallas_call(
        body, grid=(CH // CONV_BLK, nseq), name="conv_fwd",
        in_specs=[xs, pl.BlockSpec((CONV_K, CONV_BLK), lambda j, bb: (0, j)), pl.BlockSpec((1, CONV_BLK), lambda j, bb: (0, j))],
        out_specs=xs, out_shape=jax.ShapeDtypeStruct((T, CH), BF16),
        compiler_params=_cp("parallel", "parallel"),
    )(x, w, b)


def conv_bwd(x, w, b, dout, nseq):
    T, CH = x.shape
    S = T // nseq

    def body(x_ref, w_ref, b_ref, do_ref, dx_ref, dw_ref, db_ref):
        @pl.when(pl.program_id(1) == 0)
        def _():
            dw_ref[...] = jnp.zeros_like(dw_ref)
            db_ref[...] = jnp.zeros_like(db_ref)

        xv = x_ref[...].astype(F32)
        rows = lax.broadcasted_iota(jnp.int32, (S, 1), 0)
        c = jnp.zeros_like(xv) + b_ref[...]
        sh = [_shift_down(xv, CONV_K - 1 - kk, rows) for kk in range(CONV_K)]
        for kk in range(CONV_K):
            c = c + w_ref[kk:kk + 1, :] * sh[kk]
        sg = _sigmoid(c)
        dc = do_ref[...].astype(F32) * sg * (1.0 + c * (1.0 - sg))
        dx = jnp.zeros_like(xv)
        for kk in range(CONV_K):
            dx = dx + w_ref[kk:kk + 1, :] * _shift_up(dc, CONV_K - 1 - kk, rows)
            dw_ref[kk:kk + 1, :] += jnp.sum(dc * sh[kk], axis=0, keepdims=True)
        dx_ref[...] = dx.astype(dx_ref.dtype)
        db_ref[...] += jnp.sum(dc, axis=0, keepdims=True)

    xs = pl.BlockSpec((S, CONV_BLK), lambda j, bb: (bb, j))
    ws = pl.BlockSpec((CONV_K, CONV_BLK), lambda j, bb: (0, j))
    bs = pl.BlockSpec((1, CONV_BLK), lambda j, bb: (0, j))
    return pl.pallas_call(
        body, grid=(CH // CONV_BLK, nseq), name="conv_bwd",
        in_specs=[xs, ws, bs, xs], out_specs=[xs, ws, bs],
        out_shape=[jax.ShapeDtypeStruct((T, CH), BF16), jax.ShapeDtypeStruct((CONV_K, CH), F32),
                   jax.ShapeDtypeStruct((1, CH), F32)],
        compiler_params=_cp("parallel", "arbitrary"),
    )(x, w, b, dout)


def _dims(a, b, mode):
    M = a.shape[1] if mode[0] == "t" else a.shape[0]
    K = a.shape[0] if mode[0] == "t" else a.shape[1]
    N = b.shape[0] if mode[1] == "t" else b.shape[1]
    return M, K, N


def _tile(dim, prefs):
    for p in prefs:
        if dim % p == 0:
            return p
    return dim


def mm(groups, out_dtypes, name, tm=None, tn=None, tk=None, epi=None, extras=(), comm=None, sub=1, n_sum=0):
    a0, b0, m0 = groups[0][0]
    M, K0, N = _dims(a0, b0, m0)
    tm = tm or _tile(M, (1024, 512, 256, 128))
    tn = tn or _tile(N, (1024, 512, 256, 128))
    flat = [p for g in groups for p in g]
    nk = 1 if tk is None else K0 // tk
    in_specs, args = [], []
    for a, b, mode in flat:
        _, K, _ = _dims(a, b, mode)
        kb = K if tk is None else tk
        in_specs.append(pl.BlockSpec((kb, tm), lambda i, j, k: (k, i)) if mode[0] == "t"
                        else pl.BlockSpec((tm, kb), lambda i, j, k: (i, k)))
        in_specs.append(pl.BlockSpec((tn, kb), lambda i, j, k: (j, k)) if mode[1] == "t"
                        else pl.BlockSpec((kb, tn), lambda i, j, k: (k, j)))
        args += [a, b]
    kinds = []
    for e in extras:
        kind, e = e if isinstance(e, tuple) else ("vec" if e.shape[0] == 1 and M != 1 else "tile", e)
        in_specs.append({"tile": pl.BlockSpec((tm, tn), lambda i, j, k: (i, j)),
                         "vec": pl.BlockSpec((1, tn), lambda i, j, k: (0, j)),
                         "rows": pl.BlockSpec((tm, e.shape[1]), lambda i, j, k: (i, 0)),
                         "whole": pl.BlockSpec(e.shape, lambda i, j, k: (0, 0))}[kind])
        kinds.append(kind)
        args.append(e)
    n_in = len(args)
    n_main = len(out_dtypes)
    n_out = n_main + n_sum
    assert n_sum == 0 or (tn == N and tk is None)
    ng = len(groups)
    sizes = [len(g) for g in groups]

    def body(*refs):
        ins, outs, accs = refs[:n_in], refs[n_in:n_in + n_out], refs[n_in + n_out:]
        kk = pl.program_id(2)

        def dots(rs):
            vals, pos = [], 0
            for gi in range(ng):
                acc = None
                for _ in range(sizes[gi]):
                    mode = flat[pos // 2][2]
                    av = ins[pos][:, rs] if mode[0] == "t" else ins[pos][rs, :]
                    d = _dot(av, ins[pos + 1][...], mode)
                    acc = d if acc is None else acc + d
                    pos += 2
                vals.append(acc)
            return vals

        def finish(accv, rs, first_chunk=True):
            ex = [(r[rs, :] if kind in ("tile", "rows") else r[...]).astype(F32) for kind, r in zip(kinds, ins[2 * len(flat):])]
            res = epi(accv, ex) if epi is not None else tuple(accv)
            for o, r in zip(outs[:n_main], res[:n_main]):
                o[rs, :] = r.astype(o.dtype)
            for o, r in zip(outs[n_main:], res[n_main:]):
                if first_chunk:
                    @pl.when(pl.program_id(0) == 0)
                    def _():
                        o[...] = r

                    @pl.when(pl.program_id(0) > 0)
                    def _():
                        o[...] += r
                else:
                    o[...] += r

        if nk == 1:
            for r in range(sub):
                rs = slice(r * (tm // sub), (r + 1) * (tm // sub))
                finish(dots(rs), rs, r == 0)
        else:
            vals = dots(slice(0, tm))
            finish = functools.partial(finish, rs=slice(0, tm))
            @pl.when(kk == 0)
            def _():
                for ar, vv in zip(accs, vals):
                    ar[...] = vv

            @pl.when(kk > 0)
            def _():
                for ar, vv in zip(accs, vals):
                    ar[...] += vv

            @pl.when(kk == nk - 1)
            def _():
                finish([ar[...] for ar in accs])

    grid = (M // tm, N // tn, nk)
    out_specs = [pl.BlockSpec((tm, tn), lambda i, j, k: (i, j)) for _ in out_dtypes] \
        + [pl.BlockSpec((1, tn), lambda i, j, k: (0, j))] * n_sum
    out_shape = [jax.ShapeDtypeStruct((M, N), dt) for dt in out_dtypes] + [jax.ShapeDtypeStruct((1, N), F32)] * n_sum
    scratch = [pltpu.VMEM((tm, tn), F32) for _ in range(ng if nk > 1 else 0)]
    sem = ("arbitrary" if n_sum else "parallel", "parallel", "arbitrary")
    if comm is not None:
        body = _attach(comm, body, n_in, n_out, *_grid_ends(grid))
        in_specs, args = in_specs + [HBM_SPEC] * len(comm.inputs), args + comm.inputs
        out_specs, out_shape = out_specs + [HBM_SPEC] * len(comm.out_shapes), out_shape + comm.out_shapes
        scratch, sem = scratch + comm.sems, ("arbitrary",) * 3
    return pl.pallas_call(body, grid=grid, name=name, in_specs=in_specs, out_specs=out_specs, out_shape=out_shape,
                          scratch_shapes=scratch, compiler_params=_cp(*sem))(*args)


def mm1(a, b, mode, out_dtype, name, **kw):
    return mm([[(a, b, mode)]], [out_dtype], name, **kw)[0]


ROW_BLK = 512


def rowwise(fn, rows, consts, outs, accs, name, tb=ROW_BLK, comm=None):
    rows = [r if isinstance(r, tuple) else (r, r.shape[1], 0) for r in rows]
    T = rows[0][0].shape[0]
    tb = min(tb, T)
    n_r, n_c, n_o, n_a = len(rows), len(consts), len(outs), len(accs)

    def body(*refs):
        vals = [r[...].astype(F32) for r in refs[:n_r + n_c]]
        res = fn(*vals)
        o_refs = refs[n_r + n_c:n_r + n_c + n_o]
        a_refs = refs[n_r + n_c + n_o:]
        for o, r in zip(o_refs, res[:n_o]):
            o[...] = r.astype(o.dtype)
        if n_a:
            @pl.when(pl.program_id(0) == 0)
            def _():
                for ar in a_refs:
                    ar[...] = jnp.zeros_like(ar)
            for ar, r in zip(a_refs, res[n_o:]):
                ar[...] += r

    return _call_with_comm(
        body, (T // tb,), name,
        [pl.BlockSpec((tb, w), functools.partial(lambda i, j: (i, j), j=j)) for _, w, j in rows]
        + [pl.BlockSpec(c.shape, lambda i: (0, 0)) for c in consts],
        [r[0] for r in rows] + list(consts),
        [pl.BlockSpec((tb, d), lambda i: (i, 0)) for d, _ in outs] + [pl.BlockSpec(s, lambda i: (0, 0)) for s in accs],
        [jax.ShapeDtypeStruct((T, d), dt) for d, dt in outs] + [jax.ShapeDtypeStruct(s, F32) for s in accs],
        comm, sem=("arbitrary" if n_a else "parallel",))


def _rms_stats(x):
    r = lax.rsqrt(jnp.mean(x * x, axis=-1, keepdims=True) + EPS)
    return r, x * r


def _rms_bwd(x, g, dy):
    r, xn = _rms_stats(x)
    dyg = dy * g
    dx = r * (dyg - xn * jnp.mean(dyg * xn, axis=-1, keepdims=True))
    return dx, jnp.sum(dy * xn, axis=0, keepdims=True)


def rms_fwd(x, g, name, comm=None):
    res = rowwise(lambda xv, gv: (_rms_stats(xv)[1] * gv,), [x], [g], [(x.shape[1], BF16)], [], name, comm=comm)
    return res[0] if comm is None else (res[0], res[1:])


def rms_bwd(x, g, dy, name, resid=None, dx_dtype=F32):
    def fn(*v):
        if resid is None:
            xv, dyv, gv = v
            dx, dg = _rms_bwd(xv, gv, dyv)
        else:
            xv, dyv, rv, gv = v
            dx, dg = _rms_bwd(xv, gv, dyv)
            dx = dx + rv
        return dx, dg
    rows = [x, dy] + ([] if resid is None else [resid])
    return rowwise(fn, rows, [g], [(x.shape[1], dx_dtype)], [(1, x.shape[1])], name)


def mm_rms_bwd(pairs, x, g, name, resid=None, dx_dtype=F32, comm=None):
    def epi(accs, ex):
        dx, dg = _rms_bwd(ex[0], ex[-1], accs[0])
        return (dx if resid is None else dx + ex[1]), dg
    extras = [x] + ([] if resid is None else [resid]) + [g]
    return mm([pairs], [dx_dtype], name, tm=min(256, x.shape[0]), tn=x.shape[1], epi=epi, extras=extras, comm=comm, n_sum=1)


def mm_resid(a, b, x, g, wgt, name, comm=None, target=None):
    def epi(accs, ex):
        y = ex[0] + wgt * _rms_stats(accs[0])[1] * ex[1]
        if target is None:
            return accs[0], y
        d = y - ex[2]
        return accs[0], d / D, jnp.sum(d * d, axis=0, keepdims=True)
    return mm([[(a, b, "nn")]], [F32, F32], name, tm=min(512, a.shape[0]), tn=b.shape[1], epi=epi,
              extras=[x, g] + ([] if target is None else [target]), sub=2, comm=comm, n_sum=0 if target is None else 1)


def resid_bwd(h, g, dy, wgt, name):
    def fn(hv, dyv, gv):
        dx, dg = _rms_bwd(hv, gv, dyv)
        return wgt * dx, wgt * dg
    return rowwise(fn, [h, dy], [g], [(h.shape[1], BF16)], [(1, h.shape[1])], name)


def _silu_parts(g):
    s = _sigmoid(g)
    return g * s, s * (1.0 + g * (1.0 - s))


def gated_norm_fwd(y, z, g, name):
    W = SSD_INNER // SSD_G

    def fn(yv, zv, gv):
        yg = yv * _silu_parts(zv)[0]
        return (jnp.concatenate([_rms_stats(yg[:, i * W:(i + 1) * W])[1] for i in range(SSD_G)], axis=1) * gv,)
    return rowwise(fn, [y, z], [g], [(SSD_INNER, BF16)], [], name)[0]


def gated_norm_bwd(y, z, dyn, g, name):
    W = SSD_INNER // SSD_G

    def fn(yv, zv, dv, gv):
        sil, dsil = _silu_parts(zv)
        yg = yv * sil
        parts = [_rms_bwd(yg[:, i * W:(i + 1) * W], gv[:, i * W:(i + 1) * W], dv[:, i * W:(i + 1) * W]) for i in range(SSD_G)]
        dyg = jnp.concatenate([p[0] for p in parts], axis=1)
        dg = jnp.concatenate([p[1] for p in parts], axis=1)
        return dyg * sil, dyg * yv * dsil, dg
    return rowwise(fn, [y, z, dyn], [g], [(SSD_INNER, BF16), (SSD_INNER, BF16)], [(1, SSD_INNER)], name)


def merge_fwd(gl, ys, ym, gb, name):
    def fn(glv, ysv, ymv, gbv):
        gt = _sigmoid(glv + gbv)
        return (gt[:, :D] * ysv + gt[:, D:] * ymv,)
    return rowwise(fn, [gl, ys, ym], [gb], [(D, BF16)], [], name)[0]


def merge_bwd(gl, ys, ym, dm, gb, name):
    def fn(glv, ysv, ymv, dmv, gbv):
        gt = _sigmoid(glv + gbv)
        gs, gm = gt[:, :D], gt[:, D:]
        dgl = jnp.concatenate([dmv * ysv * gs * (1.0 - gs), dmv * ymv * gm * (1.0 - gm)], axis=1)
        return dmv * gs, dmv * gm, dgl, jnp.sum(dgl, axis=0, keepdims=True)
    return rowwise(fn, [gl, ys, ym, dm], [gb], [(D, BF16), (D, BF16), (2 * D, BF16)], [(1, 2 * D)], name)


def loss_head(y, tgt, name):
    def fn(yv, tv):
        d = yv - tv
        part = 0.5 * jnp.sum(jnp.sum(d * d, axis=1, keepdims=True), axis=0, keepdims=True) / D
        return d / D, jnp.broadcast_to(part, (1, 128))
    return rowwise(fn, [y, tgt], [], [(D, F32)], [(1, 128)], name)


def _adamw_math(wv, gv, mv, vv):
    mn = B1 * mv + (1.0 - B1) * gv
    vn = B2 * vv + (1.0 - B2) * (gv * gv)
    mh = mn / (1.0 - B1 ** STEP)
    vh = vn / (1.0 - B2 ** STEP)
    return -LR * (mh / (jnp.sqrt(vh) + AEPS) + WD * wv), mn, vn


def adamw(w, g, m, v, name):
    R, C = w.shape
    tb = _tile(R, (256, 128, 64, 32, 16, 8))
    return rowwise(_adamw_math, [w, g, m, v], [], [(C, F32)] * 3, [], name, tb=tb)


def adamw_small(packed, ws, ms, vs):
    k = len(ws)
    sizes = [x.shape[1] for x in ws]

    def body(*refs):
        p_ref, w_refs, m_refs, v_refs = refs[0], refs[1:1 + k], refs[1 + k:1 + 2 * k], refs[1 + 2 * k:1 + 3 * k]
        outs = refs[1 + 3 * k:]
        r0 = 0
        for i, n in enumerate(sizes):
            nr = -(-n // 128)
            g = jnp.concatenate([p_ref[r0 + r:r0 + r + 1, :] for r in range(nr)], axis=1)[:, :n]
            r0 += nr
            outs[i][...] = g
            outs[k + i][...], outs[2 * k + i][...], outs[3 * k + i][...] = _adamw_math(w_refs[i][...], g, m_refs[i][...], v_refs[i][...])

    res = pl.pallas_call(body, name="adamw_small",
                         out_shape=[jax.ShapeDtypeStruct((1, n), F32) for _ in range(4) for n in sizes])(packed, *ws, *ms, *vs)
    return [res[j * k:(j + 1) * k] for j in range(4)]


def adamw_from_slots(recv, piece, w, m, v, name, token=None):
    K, n = w.shape
    ns = recv.shape[0]
    assert recv.shape[2] == n and recv.shape[1] % K == 0
    tb = _tile(K, (256, 176, 128, 64, 32, 16, 8)) if K % 8 == 0 else K
    r_spec = pl.BlockSpec((ns, tb, n), lambda i: (0, piece * (K // tb) + i, 0))
    w_spec = pl.BlockSpec((tb, n), lambda i: (i, 0))

    def body(r_ref, w_ref, m_ref, v_ref, *rest):
        g_ref, d_ref, mo_ref, vo_ref = rest[-4:]
        g = r_ref[0].astype(F32)
        for s in range(1, ns):
            g = g + r_ref[s].astype(F32)
        g_ref[...] = g
        d_ref[...], mo_ref[...], vo_ref[...] = _adamw_math(w_ref[...], g, m_ref[...], v_ref[...])

    extra = [] if token is None else [token]
    return pl.pallas_call(
        body, grid=(K // tb,), name=name,
        in_specs=[r_spec, w_spec, w_spec, w_spec] + [pl.BlockSpec(t.shape, lambda i: (0, 0)) for t in extra], out_specs=[w_spec] * 4,
        out_shape=[jax.ShapeDtypeStruct((K, n), F32)] * 4, compiler_params=_cp("parallel"),
    )(recv, w, m, v, *extra)


def _me():
    return lax.axis_index("x"), lax.axis_index("y"), lax.axis_index("c")


def _dev_index():
    x, y, c = _me()
    return 4 * x + 2 * y + c


HBM_SPEC = pl.BlockSpec(memory_space=pl.ANY)


class GatherComm:
    def __init__(self, shards):
        self.inputs = [s for s, _ in shards]
        self.rows = [list(r) for _, r in shards]
        n = len(shards)
        self.out_shapes = [jax.ShapeDtypeStruct((N_DEV, r, s.shape[1]), s.dtype) for s, rows in shards for r in rows]
        self.sems = [pltpu.SemaphoreType.DMA((7 * n,)), pltpu.SemaphoreType.DMA((7 * n,)), pltpu.SemaphoreType.DMA((n,))]

    def _plan(self, x_refs, out_refs, sems):
        send_sems, recv_sems, local_sems = sems
        x, y, c = _me()
        me, sibling = (x, y, c), (x, y, 1 - c)
        chips = [(1 - x, y), (x, 1 - y), (1 - x, 1 - y)]
        index = lambda px, py, pc: 4 * px + 2 * py + pc
        mine, first, passed, whole = [], [], [], []
        pos = 0
        for i, rows in enumerate(self.rows):
            kw = lambda k: dict(send_sem=send_sems.at[7 * i + k], recv_sem=recv_sems.at[7 * i + k], device_id_type=MESH)
            r0 = 0
            fwd = [[] for _ in chips]
            for j, nr in enumerate(rows):
                out, src = out_refs[pos + j], x_refs[i].at[pl.ds(r0, nr)]
                mine.append(pltpu.make_async_copy(src, out.at[index(*me)], local_sems.at[i]))
                first.append(pltpu.make_async_remote_copy(src_ref=src, dst_ref=out.at[index(*me)], device_id=sibling, **kw(0)))
                for jj, chip in enumerate(chips):
                    first.append(pltpu.make_async_remote_copy(src_ref=src, dst_ref=out.at[index(*me)], device_id=(*chip, c),
                                                              **kw(1 + jj)))
                    blk = out.at[index(*chip, c)]
                    fwd[jj].append(pltpu.make_async_remote_copy(src_ref=blk, dst_ref=blk, device_id=sibling, **kw(4 + jj)))
                r0 += nr
            passed.append(fwd)
            whole.append([pltpu.make_async_remote_copy(src_ref=x_refs[i], dst_ref=x_refs[i], device_id=sibling, **kw(k))
                          for k in range(7)])
            pos += len(rows)
        return mine, first, passed, whole

    def start(self, x_refs, out_refs, sems):
        mine, first, _, _ = self._plan(x_refs, out_refs, sems)
        for cp in mine + first:
            cp.start()

    def finish(self, x_refs, out_refs, sems):
        _, _, passed, whole = self._plan(x_refs, out_refs, sems)
        local_sems = sems[2]
        for i, fwd in enumerate(passed):
            for jj in range(3):
                whole[i][1 + jj].wait_recv()
                for cp in fwd[jj]:
                    cp.start()
        for i in range(len(passed)):
            whole[i][0].wait_recv()
            for jj in range(3):
                whole[i][4 + jj].wait_recv()
        for i in range(len(passed)):
            for k in range(7):
                whole[i][k].wait_send()
            pltpu.make_async_copy(x_refs[i], x_refs[i], local_sems.at[i]).wait()


def run_comm(comm, name):
    n_in, n_out = len(comm.inputs), len(comm.out_shapes)

    def body(*refs):
        ins, outs, sems = refs[:n_in], refs[n_in:n_in + n_out], refs[n_in + n_out:]
        comm.start(ins, outs, sems)
        comm.finish(ins, outs, sems)

    return pl.pallas_call(body, name=name, out_shape=comm.out_shapes, in_specs=[HBM_SPEC] * n_in,
                          out_specs=[HBM_SPEC] * n_out, scratch_shapes=comm.sems)(*comm.inputs)


def _attach(comm, body, n_in, n_out, first, last):
    if comm is None:
        return body
    ci, co, cs = len(comm.inputs), len(comm.out_shapes), len(comm.sems)

    def wrapped(*refs):
        h_in, c_in = refs[:n_in], refs[n_in:n_in + ci]
        h_out, c_out = refs[n_in + ci:n_in + ci + n_out], refs[n_in + ci + n_out:n_in + ci + n_out + co]
        rest = refs[n_in + ci + n_out + co:]
        h_scr, c_sem = rest[:len(rest) - cs], rest[len(rest) - cs:]

        @pl.when(first())
        def _():
            comm.start(c_in, c_out, c_sem)

        body(*h_in, *h_out, *h_scr)

        @pl.when(last())
        def _():
            comm.finish(c_in, c_out, c_sem)

    return wrapped


def _grid_ends(grid):
    first = lambda: functools.reduce(lambda a, b: a & b, [pl.program_id(i) == 0 for i in range(len(grid))])
    last = lambda: functools.reduce(lambda a, b: a & b, [pl.program_id(i) == g - 1 for i, g in enumerate(grid)])
    return first, last


def _call_with_comm(body, grid, name, in_specs, args, out_specs, out_shape, comm, scratch=(), sem=None):
    sem = sem or ("parallel",) * len(grid)
    scratch = list(scratch)
    if comm is not None:
        body = _attach(comm, body, len(args), len(out_shape), *_grid_ends(grid))
        in_specs, args = in_specs + [HBM_SPEC] * len(comm.inputs), args + comm.inputs
        out_specs, out_shape = out_specs + [HBM_SPEC] * len(comm.out_shapes), out_shape + comm.out_shapes
        scratch, sem = scratch + comm.sems, ("arbitrary",) * len(grid)
    return pl.pallas_call(body, grid=grid, name=name, in_specs=in_specs, out_specs=out_specs, out_shape=out_shape,
                          scratch_shapes=scratch, compiler_params=_cp(*sem))(*args)


class ScatterComm:
    def __init__(self, groups):
        self.sizes = [len(g) for g in groups]
        self.rows = [[pc.shape[1] for pc in g] for g in groups]
        ng = len(groups)
        self.inputs = [pc for g in groups for pc in g]
        self.out_shapes = [jax.ShapeDtypeStruct((N_DEV, sum(self.rows[gi]), g[0].shape[2]), g[0].dtype) for gi, g in enumerate(groups)]
        self.sems = [pltpu.SemaphoreType.DMA((7 * ng,)), pltpu.SemaphoreType.DMA((7 * ng,)), pltpu.SemaphoreType.DMA((ng,))]

    def _peers(self):
        x, y, c = _me()
        out = []
        for k in range(1, N_DEV):
            px = 1 - x if k & 4 else x
            py = 1 - y if k & 2 else y
            pc = 1 - c if k & 1 else c
            out.append((k, 4 * px + 2 * py + pc, dict(device_id=(px, py, pc), device_id_type=MESH)))
        return 4 * x + 2 * y + c, out

    def start(self, ins, outs, sems):
        send_sems, recv_sems, local_sems = sems
        me, peers = self._peers()
        pos = 0
        for gi, size in enumerate(self.sizes):
            for i, pc in enumerate(ins[pos:pos + size]):
                dst = outs[gi].at[me, pl.ds(sum(self.rows[gi][:i]), self.rows[gi][i])]
                pltpu.make_async_copy(pc.at[me], dst, local_sems.at[gi]).start()
                for k, peer, kw in peers:
                    pltpu.make_async_remote_copy(src_ref=pc.at[peer], dst_ref=dst, send_sem=send_sems.at[7 * gi + k - 1],
                                                 recv_sem=recv_sems.at[7 * gi + k - 1], **kw).start()
            pos += size

    def finish(self, ins, outs, sems):
        send_sems, recv_sems, local_sems = sems
        me, peers = self._peers()
        whole = [pltpu.make_async_remote_copy(src_ref=outs[gi].at[peer], dst_ref=outs[gi].at[peer],
                                              send_sem=send_sems.at[7 * gi + k - 1], recv_sem=recv_sems.at[7 * gi + k - 1], **kw)
                 for gi in range(len(self.sizes)) for k, peer, kw in peers]
        for cp in whole:
            cp.wait_recv()
        for cp in whole:
            cp.wait_send()
        for gi in range(len(self.sizes)):
            pltpu.make_async_copy(outs[gi].at[me], outs[gi].at[me], local_sems.at[gi]).wait()


def _peer_list():
    x, y, c = _me()
    out = []
    for k in range(1, N_DEV):
        px = 1 - x if k & 4 else x
        py = 1 - y if k & 2 else y
        pc = 1 - c if k & 1 else c
        out.append((k, 4 * px + 2 * py + pc, dict(device_id=(px, py, pc), device_id_type=MESH)))
    return 4 * x + 2 * y + c, out


SEM_SPEC = pl.BlockSpec(memory_space=pltpu.SEMAPHORE)
HBM_ONLY = pl.BlockSpec(memory_space=pltpu.HBM)
N_SPLIT_SEMS = 2 * (N_DEV - 1)


def exchange_start(piece, after):
    def body(piece_ref, land_ref, after_ref, *outs):
        sems, token = outs[:N_SPLIT_SEMS], outs[-1]
        me, peers = _peer_list()
        for k, peer, kw in peers:
            pltpu.make_async_remote_copy(src_ref=piece_ref.at[peer], dst_ref=land_ref.at[me], send_sem=sems[k - 1],
                                         recv_sem=sems[N_DEV - 2 + k], **kw).start()
        token[...] = jnp.zeros_like(token)

    res = pl.pallas_call(
        body, name="exchange_last_start",
        out_shape=(pltpu.SemaphoreType.DMA(()),) * N_SPLIT_SEMS + (pltpu.HBM(piece.shape, piece.dtype), pltpu.HBM(piece.shape, piece.dtype),
                                                                   jax.ShapeDtypeStruct((8, 128), F32)),
        in_specs=(HBM_ONLY, HBM_ONLY, HBM_SPEC),
        out_specs=(SEM_SPEC,) * N_SPLIT_SEMS + (HBM_ONLY, HBM_ONLY, pl.BlockSpec(memory_space=pltpu.VMEM)),
        input_output_aliases={0: N_SPLIT_SEMS, 1: N_SPLIT_SEMS + 1},
        compiler_params=pltpu.CompilerParams(has_side_effects=pltpu.SideEffectType.DATAFLOW_SIDE_EFFECTING),
    )(pltpu.with_memory_space_constraint(piece, pltpu.HBM),
      pltpu.with_memory_space_constraint(lax.empty(piece.shape, piece.dtype), pltpu.HBM), after)
    return res[:N_SPLIT_SEMS], res[N_SPLIT_SEMS], res[N_SPLIT_SEMS + 1], res[N_SPLIT_SEMS + 2]


def exchange_wait(sems, piece, land, after):
    def body(piece_ref, land_ref, *rest):
        sem_refs = rest[:N_SPLIT_SEMS]
        me, peers = _peer_list()
        for k, peer, kw in peers:
            cp = pltpu.make_async_remote_copy(src_ref=piece_ref.at[peer], dst_ref=land_ref.at[peer], send_sem=sem_refs[k - 1],
                                              recv_sem=sem_refs[N_DEV - 2 + k], **kw)
            cp.wait_send()
            cp.wait_recv()

    return pl.pallas_call(
        body, name="exchange_last_wait",
        out_shape=(pltpu.HBM(piece.shape, piece.dtype), pltpu.HBM(land.shape, land.dtype)),
        in_specs=(HBM_ONLY, HBM_ONLY) + (SEM_SPEC,) * N_SPLIT_SEMS + (HBM_SPEC,), out_specs=(HBM_ONLY, HBM_ONLY),
        input_output_aliases={0: 0, 1: 1},
        compiler_params=pltpu.CompilerParams(has_side_effects=pltpu.SideEffectType.DATAFLOW_SIDE_EFFECTING),
    )(piece, land, *sems, after)[1]


def sum_slots(recv, name, tr):
    n, R, C = recv.shape

    def body(r_ref, o_ref):
        acc = r_ref[0].astype(F32)
        for s in range(1, n):
            acc = acc + r_ref[s].astype(F32)
        o_ref[...] = acc

    return pl.pallas_call(
        body, grid=(R // tr,), name=name,
        in_specs=[pl.BlockSpec((n, tr, C), lambda i: (0, i, 0))], out_specs=pl.BlockSpec((tr, C), lambda i: (i, 0)),
        out_shape=jax.ShapeDtypeStruct((R, C), F32), compiler_params=_cp("parallel"),
    )(recv)


PACK_W, FLAT_W = 1024, 128
MAIN = [
    ("ffn1_w_gate", "col"), ("ffn1_w_up", "col"), ("ffn1_w_down", "row"),
    ("ffn2_w_gate", "col"), ("ffn2_w_up", "col"), ("ffn2_w_down", "row"),
    ("w_ssd_proj", "row"), ("w_mla_proj", "row"), ("w_out", "row"),
    ("w_xq", "row"), ("w_xk", "row"), ("w_xv", "row"), ("w_xo", "row"),
    ("w_uk", "col"), ("w_uv", "col"),
]
FLAT = [("w_in", "col"), ("w_uq", "col")]
BIG = MAIN + FLAT
SMALL = ["ffn1_pre_g", "ffn1_post_g", "mix_pre_g", "conv_b", "dt_bias", "a_log", "d_skip", "ssd_norm_g", "q_norm_g",
         "kv_norm_g", "gate_bias", "mix_post_g", "xa_pre_g", "mem_norm_g", "xa_post_g", "ffn2_pre_g", "ffn2_post_g"]
WEIGHTS = ['ffn1_pre_g', 'ffn1_w_gate', 'ffn1_w_up', 'ffn1_w_down', 'ffn1_post_g', 'mix_pre_g', 'w_in', 'conv_w', 'conv_b',
           'dt_bias', 'a_log', 'd_skip', 'ssd_norm_g', 'w_ssd_proj', 'q_norm_g', 'w_uq', 'kv_norm_g', 'w_uk', 'w_uv',
           'w_mla_proj', 'gate_bias', 'w_out', 'mix_post_g', 'xa_pre_g', 'mem_norm_g', 'w_xq', 'w_xk', 'w_xv', 'w_xo',
           'xa_post_g', 'ffn2_pre_g', 'ffn2_w_gate', 'ffn2_w_up', 'ffn2_w_down', 'ffn2_post_g']


def _pack_rows(w, kind, width):
    m = w[0].T if kind == "col" else w[0]
    return m.reshape(-1, width)


KIND = dict(BIG)
GATHER_PLAN = {
    "ffn1_pre": (["ffn1_w_gate", "ffn1_w_up"], []),
    "ffn1_gate_up": (["ffn1_w_down"], ["w_in@0"]),
    "ffn1_down": ([], ["w_in@1"]),
    "ssd_fwd": (["w_ssd_proj", "w_mla_proj", "w_out", "w_uk", "w_uv"], ["w_uq"]),
    "attn_fwd": (["w_xq", "w_xk", "w_xv", "w_xo", "ffn2_w_gate", "ffn2_w_up", "ffn2_w_down"], []),
}
CONV_RIDES_WITH = "w_in@1"
LAST_EXCHANGE = "last"
SCATTER_PLAN = {
    "attn_bwd": [["ffn2_w_gate", "ffn2_w_up", "ffn2_w_down"], ["w_xq", "w_xk", "w_xv", "w_xo"]],
    "ssd_bwd": [["w_ssd_proj", "w_mla_proj", "w_out"], ["w_uk", "w_uv"], ["w_uq"]],
    "in_bwd": [["w_in#0"]],
    "ffn1:down_bwd": [["w_in#1"]],
    "ffn1:dwd": [["w_in#2"]],
    "ffn1:dwg": [["ffn1_w_down#0"]],
    "ffn1:dwu": [["ffn1_w_down#1"]],
    "ffn1:gate_up_bwd": [["ffn1_w_gate"]],
    "last": [["ffn1_w_up"]],
}
PARTS = {"w_in@0": ("w_in", 0, 2656), "w_in@1": ("w_in", 2656, 5296),
         "w_in#0": ("w_in", 0, 2656), "w_in#1": ("w_in", 2656, 3984), "w_in#2": ("w_in", 3984, 5296),
         "ffn1_w_down#0": ("ffn1_w_down", 0, 176), "ffn1_w_down#1": ("ffn1_w_down", 176, 352)}


def _parts_of(base, mark):
    return sorted(pn for pn, (b, _, _) in PARTS.items() if b == base and mark in pn)


class Stage:
    def __init__(self, w):
        self.w = w
        self.width = {n: PACK_W if (n, k) in MAIN else FLAT_W for n, k in BIG}
        self.nrows = {n: math.prod(w[n].shape) // self.width[n] for n, _ in BIG}
        self.recv = {}
        self.arrived_parts = {}

    def _rows(self, n):
        return PARTS[n][2] - PARTS[n][1] if n in PARTS else self.nrows[n]

    def _shards(self, tag):
        names_main, names_flat = GATHER_PLAN[tag]

        def pack(n):
            base, r0, r1 = PARTS.get(n, (n, 0, None))
            return _pack_rows(self.w[base], KIND[base], self.width[base])[r0:r1].astype(BF16)
        shards = []
        if names_main:
            pieces = [pack(n) for n in names_main]
            shards.append((jnp.concatenate(pieces, axis=0), [pc.shape[0] for pc in pieces]))
        if names_flat:
            pieces = [pack(n) for n in names_flat]
            if CONV_RIDES_WITH in names_flat:
                pieces.append(_pad_rows(lax.bitcast_convert_type(self.w["conv_w"][0], BF16).reshape(-1, FLAT_W), 16))
            shards.append((jnp.concatenate(pieces, axis=0), [pc.shape[0] for pc in pieces]))
        return shards

    def gather(self, tag):
        return GatherComm(self._shards(tag)) if tag in GATHER_PLAN else None

    def gathered(self, tag, outs, W, p):
        if tag not in GATHER_PLAN:
            return
        names_main, names_flat = GATHER_PLAN[tag]
        outs = list(outs)
        for n in names_main + names_flat:
            rows = outs.pop(0)
            if n in PARTS:
                self.arrived_parts[n] = rows
                base = PARTS[n][0]
                mine = _parts_of(base, "@")
                if not all(pn in self.arrived_parts for pn in mine):
                    continue
                n, rows = base, jnp.concatenate([self.arrived_parts[pn] for pn in mine], axis=1)
            K = self.w[n].shape[1] if KIND[n] == "col" else PACK_W
            W[n] = rows.reshape(-1, K)
        if CONV_RIDES_WITH in names_flat:
            cw = self.w["conv_w"]
            nbits = 2 * math.prod(cw.shape) // FLAT_W
            bits = outs.pop(0)[:, :nbits].reshape((N_DEV,) + cw.shape[1:] + (2,))
            p["conv_w"] = lax.bitcast_convert_type(bits, F32).transpose(1, 0, 2).reshape(cw.shape[1], -1)

    def pieces(self, tag, gw):
        def piece(n):
            if n in PARTS:
                base, r0, r1 = PARTS[n]
                return gw[base].reshape(N_DEV, self.nrows[base], self.width[base])[:, r0:r1]
            return gw[n].reshape(N_DEV, self.nrows[n], self.width[n])
        return [[piece(n) for n in names] for names in SCATTER_PLAN[tag]]

    def scatter(self, tag, gw):
        return ScatterComm(self.pieces(tag, gw)) if tag in SCATTER_PLAN else None

    def scattered(self, tag, outs):
        if tag in SCATTER_PLAN:
            self.recv[tag] = outs


def _pad_rows(a, mult):
    r = (-a.shape[0]) % mult
    return a if r == 0 else jnp.concatenate([a, jnp.zeros((r,) + a.shape[1:], a.dtype)], axis=0)


def _pack_small(vals, loss_row=None, conv_w=None):
    rows = []
    for v in vals:
        f = v.reshape(-1)
        f = jnp.concatenate([f, jnp.zeros(((-f.shape[0]) % 128,), F32)])
        rows.append(f.reshape(-1, 128))
    if conv_w is not None:
        rows.append(conv_w.reshape(-1, 128))
    if loss_row is not None:
        rows.append(loss_row)
    return _pad_rows(jnp.concatenate(rows, axis=0), 8)


def _unpack_small(buf, shapes):
    out, r = [], 0
    for shp in shapes:
        n = math.prod(shp)
        nr = -(-n // 128)
        out.append(buf[r:r + nr].reshape(-1)[:n].reshape(shp))
        r += nr
    return out, r


def _tn(a, b, name, out_dtype=BF16, comm=None):
    M, N = a.shape[1], b.shape[1]
    T = a.shape[0]
    tm = M if M <= 1536 else M // 2
    tk = 1024 if T % 1024 == 0 and T > 1024 else None
    res = mm([[(a, b, "tn")]], [out_dtype], name, tm=tm, tn=N, tk=tk, comm=comm)
    return res[0] if comm is None else (res[0], res[1:])


class NoStage:
    def gather(self, tag):
        return None

    def gathered(self, tag, outs, W, p):
        pass

    def scatter(self, tag, gw):
        return None

    def scattered(self, tag, outs):
        pass


def _ffn_fwd(x, gpre, gpost, W, p, tag, stage, target=None):
    comm = stage.gather(tag + "_pre")
    h = rms_fwd(x, gpre, tag + "_pre", comm=comm)
    if comm is not None:
        h, arrived = h
        stage.gathered(tag + "_pre", arrived, W, p)

    def swi(accs, ex):
        sil, dsil = _silu_parts(accs[0])
        return sil, accs[1] * dsil, sil * accs[1]
    G, U, A, *arrived = mm([[(h, W[tag + "_w_gate"], "nt")], [(h, W[tag + "_w_up"], "nt")]], [BF16, BF16, BF16], tag + "_gate_up",
                           tn=DFF // 2, epi=swi, comm=stage.gather(tag + "_gate_up"), sub=4 if h.shape[0] % 1024 == 0 else 1)
    stage.gathered(tag + "_gate_up", arrived, W, p)
    H, y, *rest = mm_resid(A, W[tag + "_w_down"], x, gpost, FFN_RES, tag + "_down", comm=stage.gather(tag + "_down"), target=target)
    saved = (x, h, G, U, A, H)
    if target is not None:
        return y, saved, rest[0]
    stage.gathered(tag + "_down", rest, W, p)
    return y, saved


def _ffn_bwd(dy, saved, gpre, gpost, wg_t, wu_t, wd, tag, stage, gw):
    x, h, G, U, A, H = saved
    dH, dgpost = resid_bwd(H, gpost, dy, FFN_RES, tag + "_post_bwd")

    def dswi(accs, ex):
        return accs[0] * ex[1], accs[0] * ex[0]

    def hosted(where, call):
        comm = stage.scatter(tag + ":" + where, gw)
        res = call(comm)
        if comm is None:
            return res
        stage.scattered(tag + ":" + where, res[1])
        return res[0]

    res = hosted("down_bwd", lambda comm: (lambda r: r if comm is None else (r[:2], r[2:]))(
        mm([[(dH, wd, "nt")]], [BF16, BF16], tag + "_down_bwd", tn=DFF // 2, epi=dswi, extras=[G, U], comm=comm,
           sub=4 if dH.shape[0] % 1024 == 0 else 1)))
    dG, dU = res
    gw[tag + "_w_down"] = hosted("dwd", lambda comm: _tn(A, dH, tag + "_dwd", comm=comm))
    gw[tag + "_w_gate"] = hosted("dwg", lambda comm: _tn(dG, h, tag + "_dwg", comm=comm))
    gw[tag + "_w_up"] = hosted("dwu", lambda comm: _tn(dU, h, tag + "_dwu", comm=comm))
    dx, dgpre = hosted("gate_up_bwd", lambda comm: (lambda r: r[:2] if comm is None else (r[:2], r[2:]))(
        mm_rms_bwd([(dG, wg_t, "nn"), (dU, wu_t, "nn")], x, gpre, tag + "_gate_up_bwd", resid=dy, comm=comm)))
    return dx, dgpre, dgpost


def _local_step(x, mem, positions, tgt, W, p, stage=None):
    stage = stage or NoStage()
    nseq = x.shape[0]
    T = nseq * x.shape[1]
    x0 = x.reshape(T, D)
    mem2 = mem.reshape(-1, D)

    x1, ffn1 = _ffn_fwd(x0, p["ffn1_pre_g"], p["ffn1_post_g"], W, p, "ffn1", stage)

    w_in_t = W["w_in"]
    bounds = [0]
    for n in (SSD_INNER, CONV_CH, SSD_H, QR, KVR, ROPE, 2 * D):
        bounds.append(bounds[-1] + n)
    wt_z, wt_xbc, wt_dt, wt_q, wt_kv, wt_kr, wt_gate = [w_in_t[bounds[i]:bounds[i + 1]] for i in range(7)]
    wt_dt, wt_kr = _pad_rows(wt_dt, SLOT), _pad_rows(wt_kr, SLOT)
    wt_dtkr = jnp.concatenate([wt_dt, wt_kr], axis=0)
    hm = rms_fwd(x1, p["mix_pre_g"], "mix_pre")
    z = mm1(hm, wt_z, "nt", BF16, "in_z")
    xbc = mm1(hm, wt_xbc, "nt", BF16, "in_xbc")
    q_c = mm1(hm, wt_q, "nt", F32, "in_q", tn=QR)
    kv_c = mm1(hm, wt_kv, "nt", F32, "in_kv")
    dtkr = mm1(hm, wt_dtkr, "nt", F32, "in_dtkr")
    gl = mm1(hm, wt_gate, "nt", BF16, "in_gate")

    xbc_act = conv_fwd(xbc, p["conv_w"], p["conv_b"], nseq)
    y_ssd_core, prev, *arrived = ssd_fwd(xbc_act, dtkr, p["dt_bias"], p["a_log"], p["d_skip"], nseq, comm=stage.gather("ssd_fwd"))
    stage.gathered("ssd_fwd", arrived, W, p)
    yn = gated_norm_fwd(y_ssd_core, z, p["ssd_norm_g"], "ssd_norm")
    y_ssd = mm1(yn, W["w_ssd_proj"], "nn", BF16, "ssd_proj")

    slot_rows = lambda wt, per: jnp.pad(wt.reshape(MLA_H, per, -1), ((0, 0), (0, SLOT - per), (0, 0))).reshape(MLA_H * SLOT, -1)
    wq_s, wk_s, wv_s = slot_rows(W["w_uq"], QK), slot_rows(W["w_uk"], NOPE), slot_rows(W["w_uv"], VD)
    wo_s = slot_rows(W["w_mla_proj"], VD)
    qn = rms_fwd(q_c, p["q_norm_g"], "q_norm")
    rope_c, rope_s = rope_table(*_rope_inputs(positions))
    rope_args = [("rows", rope_c), ("rows", rope_s)]
    Qc, = mm([[(qn, wq_s, "nt")]], [BF16], "uq", epi=rope_q_epilogue, extras=rope_args, sub=4 if T % 1024 == 0 else 1)
    kvn = rms_fwd(kv_c, p["kv_norm_g"], "kv_norm")
    Kc, = mm([[(kvn, wk_s, "nt")]], [BF16], "uk", epi=rope_k_epilogue, extras=rope_args + [("rows", dtkr)],
             sub=4 if T % 1024 == 0 else 1)
    v_s = mm1(kvn, wv_s, "nt", BF16, "uv")
    o_s, lse, *arrived = attn_slot_fwd(Qc, Kc, v_s, nseq, comm=stage.gather("attn_fwd"))
    stage.gathered("attn_fwd", arrived, W, p)
    y_mla = mm1(o_s, wo_s, "nn", BF16, "mla_proj")

    merged = merge_fwd(gl, y_ssd, y_mla, p["gate_bias"], "merge")
    hmix, x2 = mm_resid(merged, W["w_out"], x1, p["mix_post_g"], 1.0, "mix_out")

    hq = rms_fwd(x2, p["xa_pre_g"], "xa_pre")
    mn = rms_fwd(mem2, p["mem_norm_g"], "mem_norm")
    xq = mm1(hq, W["w_xq"], "nn", BF16, "xq")
    xk = mm1(mn, W["w_xk"], "nn", BF16, "xk")
    xv = mm1(mn, W["w_xv"], "nn", BF16, "xv")
    xo, *arrived = xattn_fwd(xq, xk, xv, nseq, comm=stage.gather("xattn_fwd"))
    stage.gathered("xattn_fwd", arrived, W, p)
    ho, x3 = mm_resid(xo, W["w_xo"], x2, p["xa_post_g"], 1.0, "xo")

    dx4, ffn2, sq_cols = _ffn_fwd(x3, p["ffn2_pre_g"], p["ffn2_post_g"], W, p, "ffn2", stage, target=tgt.reshape(T, D))
    loss_row = (0.5 / D) * jnp.sum(sq_cols.reshape(-1, 128), axis=0, keepdims=True)

    gw, gs = {}, {}
    dx3, gs["ffn2_pre_g"], gs["ffn2_post_g"] = _ffn_bwd(
        dx4, ffn2, p["ffn2_pre_g"], p["ffn2_post_g"], W["ffn2_w_gate"], W["ffn2_w_up"], W["ffn2_w_down"], "ffn2", stage, gw)

    dho, gs["xa_post_g"] = resid_bwd(ho, p["xa_post_g"], dx3, 1.0, "xa_post_bwd")
    dxo = mm1(dho, W["w_xo"], "nt", BF16, "xo_bwd")
    gw["w_xo"] = _tn(xo, dho, "d_w_xo")
    dxq, dxk, dxv = xattn_bwd(xq, xk, xv, dxo, nseq)
    dx2, gs["xa_pre_g"] = mm_rms_bwd([(dxq, W["w_xq"], "nt")], x2, p["xa_pre_g"], "xq_bwd", resid=dx3)
    gw["w_xq"] = _tn(hq, dxq, "d_w_xq")
    dmn = mm([[(dxk, W["w_xk"], "nt"), (dxv, W["w_xv"], "nt")]], [F32], "xkv_bwd")[0]
    gw["w_xk"] = _tn(mn, dxk, "d_w_xk")
    gw["w_xv"] = _tn(mn, dxv, "d_w_xv")
    _, gs["mem_norm_g"] = rms_bwd(mem2, p["mem_norm_g"], dmn, "mem_norm_bwd", dx_dtype=BF16)

    dhmix, gs["mix_post_g"] = resid_bwd(hmix, p["mix_post_g"], dx2, 1.0, "mix_post_bwd")
    dmerged = mm1(dhmix, W["w_out"], "nt", F32, "mix_out_bwd")
    gw["w_out"] = _tn(merged, dhmix, "d_w_out")
    dys, dym, dgl, gs["gate_bias"] = merge_bwd(gl, y_ssd, y_mla, dmerged, p["gate_bias"], "merge_bwd")

    unslot = lambda g, per: g.reshape(MLA_H, SLOT, -1)[:, :per].reshape(MLA_H * per, -1)
    do_s = mm1(dym, wo_s, "nt", BF16, "mla_proj_bwd")
    gw["w_mla_proj"] = unslot(_tn(o_s, dym, "d_w_mla_proj"), VD)
    dQc, dKc, dv_s, *sent = attn_slot_bwd(Qc, Kc, v_s, o_s, lse, do_s, nseq, comm=stage.scatter("attn_bwd", gw))
    stage.scattered("attn_bwd", sent)
    dq_s, dkn_s, dkr = rope_slot_bwd(dQc, dKc, rope_c, rope_s, "rope_bwd")
    dq_c, gs["q_norm_g"] = mm_rms_bwd([(dq_s, wq_s, "nn")], q_c, p["q_norm_g"], "uq_bwd", dx_dtype=BF16)
    gw["w_uq"] = unslot(_tn(dq_s, qn, "d_w_uq"), QK)
    dkv_c, gs["kv_norm_g"] = mm_rms_bwd([(dkn_s, wk_s, "nn"), (dv_s, wv_s, "nn")], kv_c, p["kv_norm_g"], "ukv_bwd", dx_dtype=BF16)
    gw["w_uk"] = unslot(_tn(dkn_s, kvn, "d_w_uk"), NOPE)
    gw["w_uv"] = unslot(_tn(dv_s, kvn, "d_w_uv"), VD)

    dyn = mm1(dys, W["w_ssd_proj"], "nt", F32, "ssd_proj_bwd")
    gw["w_ssd_proj"] = _tn(yn, dys, "d_w_ssd_proj")
    dyc, dz, gs["ssd_norm_g"] = gated_norm_bwd(y_ssd_core, z, dyn, p["ssd_norm_g"], "ssd_norm_bwd")
    dxbc_act, ddtr, gs["dt_bias"], gs["a_log"], gs["d_skip"], *sent = ssd_bwd(
        xbc_act, dtkr, p["dt_bias"], p["a_log"], p["d_skip"], prev, dyc, nseq, comm=stage.scatter("ssd_bwd", gw))
    stage.scattered("ssd_bwd", sent)
    dxbc, gs["conv_w"], gs["conv_b"] = conv_bwd(xbc, p["conv_w"], p["conv_b"], dxbc_act, nseq)

    gw["w_in"] = jnp.concatenate([_tn(dz, hm, "d_w_in_z"), _tn(dxbc, hm, "d_w_in_xbc"), _tn(ddtr, hm, "d_w_in_dt")[:SSD_H],
                                  _tn(dq_c, hm, "d_w_in_q"), _tn(dkv_c, hm, "d_w_in_kv"), _tn(dkr, hm, "d_w_in_kr")[:ROPE],
                                  _tn(dgl, hm, "d_w_in_gate")], axis=0)
    dx1, gs["mix_pre_g"], *sent = mm_rms_bwd(
        [(dz, wt_z, "nn"), (dxbc, wt_xbc, "nn"), (ddtr, wt_dt, "nn"), (dq_c, wt_q, "nn"), (dkv_c, wt_kv, "nn"),
         (dkr, wt_kr, "nn"), (dgl, wt_gate, "nn")], x1, p["mix_pre_g"], "in_bwd", resid=dx2, comm=stage.scatter("in_bwd", gw))
    stage.scattered("in_bwd", sent)

    dx0, gs["ffn1_pre_g"], gs["ffn1_post_g"] = _ffn_bwd(
        dx1, ffn1, p["ffn1_pre_g"], p["ffn1_post_g"], W["ffn1_w_gate"], W["ffn1_w_up"], W["ffn1_w_down"], "ffn1", stage, gw)
    return loss_row, dx0.reshape(x.shape), gw, gs


def kernel(x, mem, positions, ffn1_pre_g, ffn1_w_gate, ffn1_w_up, ffn1_w_down, ffn1_post_g, mix_pre_g, w_in, conv_w, conv_b, dt_bias, a_log, d_skip, ssd_norm_g, w_ssd_proj, q_norm_g, w_uq, kv_norm_g, w_uk, w_uv, w_mla_proj, gate_bias, w_out, mix_post_g, xa_pre_g, mem_norm_g, w_xq, w_xk, w_xv, w_xo, xa_post_g, ffn2_pre_g, ffn2_w_gate, ffn2_w_up, ffn2_w_down, ffn2_post_g, loss_target, m_ffn1_pre_g, m_ffn1_w_gate, m_ffn1_w_up, m_ffn1_w_down, m_ffn1_post_g, m_mix_pre_g, m_w_in, m_conv_w, m_conv_b, m_dt_bias, m_a_log, m_d_skip, m_ssd_norm_g, m_w_ssd_proj, m_q_norm_g, m_w_uq, m_kv_norm_g, m_w_uk, m_w_uv, m_w_mla_proj, m_gate_bias, m_w_out, m_mix_post_g, m_xa_pre_g, m_mem_norm_g, m_w_xq, m_w_xk, m_w_xv, m_w_xo, m_xa_post_g, m_ffn2_pre_g, m_ffn2_w_gate, m_ffn2_w_up, m_ffn2_w_down, m_ffn2_post_g, v_ffn1_pre_g, v_ffn1_w_gate, v_ffn1_w_up, v_ffn1_w_down, v_ffn1_post_g, v_mix_pre_g, v_w_in, v_conv_w, v_conv_b, v_dt_bias, v_a_log, v_d_skip, v_ssd_norm_g, v_w_ssd_proj, v_q_norm_g, v_w_uq, v_kv_norm_g, v_w_uk, v_w_uv, v_w_mla_proj, v_gate_bias, v_w_out, v_mix_post_g, v_xa_pre_g, v_mem_norm_g, v_w_xq, v_w_xk, v_w_xv, v_w_xo, v_xa_post_g, v_ffn2_pre_g, v_ffn2_w_gate, v_ffn2_w_up, v_ffn2_w_down, v_ffn2_post_g):
    a = dict(locals())
    w = {n: a[n] for n in WEIGHTS}
    m = {n: a["m_" + n] for n in WEIGHTS}
    v = {n: a["v_" + n] for n in WEIGHTS}

    stage = Stage(w)
    W, p = {}, {n: w[n] for n in SMALL}
    loss_row, grad_x, gw, gs = _local_step(x, mem, positions, loss_target, W, p, stage)

    sm = _pack_small([gs[n] for n in SMALL], loss_row=loss_row, conv_w=gs["conv_w"])
    srecv, = run_comm(ScatterComm([[jnp.broadcast_to(sm[None], (N_DEV,) + sm.shape)]]), "exchange_small")
    s_rows = sum_slots(srecv, "sum_small", tr=sm.shape[0])
    last_piece, = stage.pieces(LAST_EXCHANGE, gw)[0]
    sems, last_piece, landed, token = exchange_start(last_piece, s_rows)
    grads, delta, new_m, new_v = {}, {}, {}, {}
    raw_results = []

    def finish(n, buf, piece, token=None):
        col = KIND[n] == "col"
        turn = (lambda t: t.T) if col else (lambda t: t)
        K = w[n].shape[1]
        if col and buf.shape[2] != K:
            buf = buf.reshape(buf.shape[0], -1, K)
        res = adamw_from_slots(buf, piece, turn(w[n][0]), turn(m[n][0]), turn(v[n][0]), "adamw_" + n, token=token)
        raw_results.append(res[3])
        grads[n], delta[n], new_m[n], new_v[n] = [turn(r)[None] for r in res]

    parts = {}
    for tag, groups in SCATTER_PLAN.items():
        if tag == LAST_EXCHANGE:
            continue
        for names, buf in zip(groups, stage.recv[tag]):
            for piece, n in enumerate(names):
                if n in PARTS:
                    parts[n] = sum_slots(buf, "sum_" + n.replace("#", "_"), tr=buf.shape[1])
                else:
                    finish(n, buf, piece, token)
    for base in sorted({PARTS[pn][0] for pn in parts}):
        rows = jnp.concatenate([parts[pn] for pn in _parts_of(base, "#")], axis=0)
        finish(base, rows[None], 0, token)
    landed = exchange_wait(sems, last_piece, landed, after=raw_results[-1])
    me = _dev_index()
    landed = lax.dynamic_update_index_in_dim(landed, lax.dynamic_index_in_dim(last_piece, me, 0, keepdims=False), me, 0)
    finish(SCATTER_PLAN[LAST_EXCHANGE][0][0], landed, 0)
    conv_w_full = p["conv_w"]
    small = adamw_small(s_rows, [w[n] for n in SMALL], [m[n] for n in SMALL], [v[n] for n in SMALL])
    for t, vals in zip((grads, delta, new_m, new_v), small):
        t.update(zip(SMALL, vals))
    r1 = sum(-(-w[n].shape[1] // 128) for n in SMALL)
    ncw = math.prod(conv_w_full.shape) // 128
    cw_grad_full = s_rows[r1:r1 + ncw].reshape(conv_w_full.shape)
    wsh = conv_w.shape[2]
    grads["conv_w"] = lax.dynamic_slice_in_dim(cw_grad_full, _dev_index() * wsh, wsh, axis=1)[None]
    loss = jnp.sum(s_rows[r1 + ncw])
    d_, m_, v_ = adamw(conv_w[0], grads["conv_w"][0], m["conv_w"][0], v["conv_w"][0], "adamw_conv_w")
    delta["conv_w"], new_m["conv_w"], new_v["conv_w"] = d_[None], m_[None], v_[None]
    return (loss, grad_x, *[grads[n] for n in WEIGHTS], *[delta[n] for n in WEIGHTS],
            *[new_m[n] for n in WEIGHTS], *[new_v[n] for n in WEIGHTS])
```

```python
import functools
import math

import jax
import jax.numpy as jnp
from jax import lax
from jax.experimental import pallas as pl
from jax.experimental.pallas import tpu as pltpu

F32, BF16 = jnp.float32, jnp.bfloat16
HI = lax.Precision.HIGHEST
MESH = pl.DeviceIdType.MESH
N_DEV = 8

D = 1024
DFF = 2816
SSD_H, SSD_P, SSD_G, SSD_N, SSD_L = 16, 64, 2, 128, 128
SSD_INNER = SSD_H * SSD_P
CONV_K, CONV_CH = 4, 1536
MLA_H, QR, KVR, NOPE, ROPE, VD = 16, 384, 256, 64, 32, 64
QK = NOPE + ROPE
ROPE_THETA = 10000.0
XA_H, XA_D = 4, 256
EPS = 1e-6
FFN_RES = 0.5
LR, B1, B2, AEPS, WD, STEP = 0.001, 0.9, 0.999, 1e-08, 0.01, 10

VMEM_LIMIT = 56 * 2**20


def _cp(*sem):
    return pltpu.CompilerParams(dimension_semantics=sem, vmem_limit_bytes=VMEM_LIMIT)


def _sigmoid(x):
    return 1.0 / (1.0 + jnp.exp(-x))


def _softplus(x):
    return jnp.where(x > 20.0, x, jnp.log(1.0 + jnp.exp(jnp.minimum(x, 20.0))))


def _dot(a, b, dims="nn"):
    ca = 0 if dims[0] == "t" else 1
    cb = 1 if dims[1] == "t" else 0
    return lax.dot_general(a.astype(BF16), b.astype(BF16), (((ca,), (cb,)), ((), ())), preferred_element_type=F32)


def _dot_sel(a, b, dims="nn", split="a", terms=3):
    r = (a if split == "a" else b).astype(F32)
    out = None
    for t in range(terms):
        piece = r.astype(BF16)
        if t + 1 < terms:
            r = r - piece.astype(F32)
        d = _dot(piece, b, dims) if split == "a" else _dot(a, piece, dims)
        out = d if out is None else out + d
    return out


def _ssd_common(dtr, dtb, alog):
    L = dtr.shape[0]
    dt = _softplus(dtr + dtb)
    a = -jnp.exp(alog)
    adt = dt * a
    r = lax.broadcasted_iota(jnp.int32, (L, L), 0)
    c = lax.broadcasted_iota(jnp.int32, (L, L), 1)
    lower = r >= c
    tri = lower.astype(F32)
    cs = _dot_sel(tri, adt, "nn", split="b")
    cs_t = _dot_sel(adt, tri, "tt")
    return dt, a, cs, cs_t, lower


def _head_expand():
    hh = lax.broadcasted_iota(jnp.int32, (SSD_H, SSD_INNER), 0)
    jj = lax.broadcasted_iota(jnp.int32, (SSD_H, SSD_INNER), 1)
    return ((jj >= hh * SSD_P) & (jj < hh * SSD_P + SSD_P)).astype(F32)


def _head_reduce():
    hh = lax.broadcasted_iota(jnp.int32, (SSD_INNER, SSD_H), 1)
    jj = lax.broadcasted_iota(jnp.int32, (SSD_INNER, SSD_H), 0)
    return ((jj >= hh * SSD_P) & (jj < hh * SSD_P + SSD_P)).astype(F32)


def ssd_fwd(xbc, dtr, dtb, alog, dsk, nseq, comm=None):
    T = xbc.shape[0]
    S = T // nseq
    C = S // SSD_L
    L = SSD_L
    NP = SSD_H // 2

    def body(x_ref, b_ref, c_ref, dtr_ref, dtb_ref, alog_ref, dsk_ref, y_ref, prev_ref, st_ref):
        ci = pl.program_id(1)

        @pl.when(ci == 0)
        def _():
            st_ref[...] = jnp.zeros_like(st_ref)

        dt, a, cs, cs_t, lower = _ssd_common(dtr_ref[:, 0:SSD_H], dtb_ref[...], alog_ref[...])
        E = _head_expand()
        X = x_ref[...].astype(F32)
        dt_e = _dot_sel(dt, E)
        cs_e = _dot_sel(cs, E)
        csl_e = cs_e[L - 1:L, :]
        Xd = X * dt_e
        Xf = Xd * jnp.exp(csl_e - cs_e)
        e_e = jnp.exp(cs_e)
        skip = _dot_sel(dsk_ref[...], E) * X
        lane = lax.broadcasted_iota(jnp.int32, (1, 2 * SSD_P), 1)
        rowp = lax.broadcasted_iota(jnp.int32, (2 * SSD_P, 1), 0)
        for g in range(SSD_G):
            Bg = b_ref[:, g * SSD_N:(g + 1) * SSD_N]
            Cg = c_ref[:, g * SSD_N:(g + 1) * SSD_N]
            cb = _dot(Cg, Bg, "nt")
            for pp in range(NP // SSD_G):
                p = g * (NP // SSD_G) + pp
                sl = slice(p * 2 * SSD_P, (p + 1) * 2 * SSD_P)
                Xd_p = Xd[:, sl]
                yd = jnp.zeros((L, 2 * SSD_P), F32)
                for q in range(2):
                    h = 2 * p + q
                    m = jnp.where(lower, jnp.exp(jnp.minimum(cs[:, h:h + 1] - cs_t[h:h + 1, :], 0.0)), 0.0)
                    mask = (lane >= q * SSD_P) & (lane < (q + 1) * SSD_P)
                    yd = yd + _dot(cb * m, jnp.where(mask, Xd_p, 0.0))
                S0 = st_ref[p]
                prev_ref[0, 0, p] = S0
                z = _dot(Cg, S0, "nt")
                y_ref[:, sl] = (skip[:, sl] + yd + z * e_e[:, sl]).astype(y_ref.dtype)
                h0 = 2 * p
                dec = jnp.where(rowp < SSD_P, jnp.exp(cs[L - 1:L, h0:h0 + 1]), jnp.exp(cs[L - 1:L, h0 + 1:h0 + 2]))
                st_ref[p] = S0 * dec + _dot(Xf[:, sl], Bg, "tn")

    row = lambda b, c: (b * C + c, 0)
    small = pl.BlockSpec((1, SSD_H), lambda b, c: (0, 0))
    return _call_with_comm(
        body, (nseq, C), "ssd_fwd",
        [pl.BlockSpec((L, SSD_INNER), row),
         pl.BlockSpec((L, SSD_G * SSD_N), lambda b, c: (b * C + c, SSD_INNER // (SSD_G * SSD_N))),
         pl.BlockSpec((L, SSD_G * SSD_N), lambda b, c: (b * C + c, SSD_INNER // (SSD_G * SSD_N) + 1)),
         pl.BlockSpec((L, 128), row), small, small, small],
        [xbc, xbc, xbc, dtr, dtb, alog, dsk],
        [pl.BlockSpec((L, SSD_INNER), row), pl.BlockSpec((1, 1, NP, 2 * SSD_P, SSD_N), lambda b, c: (b, c, 0, 0, 0))],
        [jax.ShapeDtypeStruct((T, SSD_INNER), BF16), jax.ShapeDtypeStruct((nseq, C, NP, 2 * SSD_P, SSD_N), F32)],
        comm, scratch=[pltpu.VMEM((NP, 2 * SSD_P, SSD_N), F32)], sem=("parallel", "arbitrary"))


def ssd_bwd(xbc, dtr, dtb, alog, dsk, prev, dy, nseq, comm=None):
    T = xbc.shape[0]
    S = T // nseq
    C = S // SSD_L
    L = SSD_L
    NP = SSD_H // 2

    def body(x_ref, b_ref, c_ref, dtr_ref, dtb_ref, alog_ref, dsk_ref, prev_ref, dy_ref,
             dxbc_ref, ddtr_ref, ddtb_ref, dalog_ref, ddsk_ref, ds_ref, stg_ref):
        bi = pl.program_id(0)
        ci = pl.program_id(1)

        @pl.when(ci == 0)
        def _():
            ds_ref[...] = jnp.zeros_like(ds_ref)

        @pl.when((ci == 0) & (bi == 0))
        def _():
            ddtb_ref[...] = jnp.zeros_like(ddtb_ref)
            dalog_ref[...] = jnp.zeros_like(dalog_ref)
            ddsk_ref[...] = jnp.zeros_like(ddsk_ref)

        dtr = dtr_ref[:, 0:SSD_H]
        dtb = dtb_ref[...]
        dt, a, cs, cs_t, lower = _ssd_common(dtr, dtb, alog_ref[...])
        upper = lax.broadcasted_iota(jnp.int32, (L, L), 1) >= lax.broadcasted_iota(jnp.int32, (L, L), 0)
        E = _head_expand()
        ET = _head_reduce()
        X = x_ref[...].astype(F32)
        dY = dy_ref[...].astype(F32)
        dt_e = _dot_sel(dt, E)
        cs_e = _dot_sel(cs, E)
        csl_e = cs_e[L - 1:L, :]
        f_e = jnp.exp(csl_e - cs_e)
        e_e = jnp.exp(cs_e)
        dsk_e = _dot_sel(dsk_ref[...], E)
        Xd = X * dt_e
        Xf = Xd * f_e
        lane = lax.broadcasted_iota(jnp.int32, (1, 2 * SSD_P), 1)
        rowp = lax.broadcasted_iota(jnp.int32, (2 * SSD_P, 1), 0)
        hsel = lax.broadcasted_iota(jnp.int32, (1, SSD_H), 1)
        dcs = jnp.zeros((L, SSD_H), F32)
        dcsl = jnp.zeros((1, SSD_H), F32)
        for g in range(SSD_G):
            Bg = b_ref[:, g * SSD_N:(g + 1) * SSD_N]
            Cg = c_ref[:, g * SSD_N:(g + 1) * SSD_N]
            cb = _dot(Cg, Bg, "nt")
            cbt = _dot(Bg, Cg, "nt")
            dB = jnp.zeros((L, SSD_N), F32)
            dC = jnp.zeros((L, SSD_N), F32)
            for pp in range(NP // SSD_G):
                p = g * (NP // SSD_G) + pp
                sl = slice(p * 2 * SSD_P, (p + 1) * 2 * SSD_P)
                Xd_p = Xd[:, sl]
                dY_p = dY[:, sl]
                dXd_p = jnp.zeros((L, 2 * SSD_P), F32)
                for q in range(2):
                    h = 2 * p + q
                    mask = (lane >= q * SSD_P) & (lane < (q + 1) * SSD_P)
                    col = cs[:, h:h + 1]
                    rw = cs_t[h:h + 1, :]
                    m = jnp.where(lower, jnp.exp(jnp.minimum(col - rw, 0.0)), 0.0)
                    mt = jnp.where(upper, jnp.exp(jnp.minimum(rw - col, 0.0)), 0.0)
                    dYm = jnp.where(mask, dY_p, 0.0)
                    dW = _dot(dYm, Xd_p, "nt")
                    dWt = _dot(Xd_p, dYm, "nt")
                    w = cb * m
                    wt = cbt * mt
                    dC = dC + _dot(dW * m, Bg)
                    dB = dB + _dot(dWt * mt, Cg)
                    dXd_p = dXd_p + jnp.where(mask, _dot(wt, dY_p), 0.0)
                    qcol = jnp.sum(dW * w, axis=1, keepdims=True) - jnp.sum(dWt * wt, axis=1, keepdims=True)
                    dcs = dcs + qcol * (hsel == h).astype(F32)
                S0 = prev_ref[0, 0, p]
                dSn = ds_ref[p]
                dZ = dY_p * e_e[:, sl]
                dC = dC + _dot(dZ, S0)
                h0 = 2 * p
                el0 = jnp.exp(cs[L - 1:L, h0:h0 + 1])
                el1 = jnp.exp(cs[L - 1:L, h0 + 1:h0 + 2])
                dec = jnp.where(rowp < SSD_P, el0, el1)
                ds_ref[p] = dSn * dec + _dot(dZ, Cg, "tn")
                dXf_p = _dot(Bg, dSn, "nt")
                dB = dB + _dot(Xf[:, sl], dSn)
                rs = jnp.sum(dSn * S0, axis=1, keepdims=True)
                s0 = jnp.sum(jnp.where(rowp < SSD_P, rs, 0.0), axis=0, keepdims=True) * el0
                s1 = jnp.sum(jnp.where(rowp >= SSD_P, rs, 0.0), axis=0, keepdims=True) * el1
                dcsl = dcsl + s0 * (hsel == h0).astype(F32) + s1 * (hsel == h0 + 1).astype(F32)
                y_off = _dot(Cg, S0, "nt") * e_e[:, sl]
                t1 = dY_p * y_off - dXf_p * Xf[:, sl]
                r1 = jnp.where(lane < SSD_P, t1, 0.0)
                c0 = jnp.sum(r1, axis=1, keepdims=True)
                c1 = jnp.sum(t1 - r1, axis=1, keepdims=True)
                dcs = dcs + c0 * (hsel == h0).astype(F32) + c1 * (hsel == h0 + 1).astype(F32)
                t2 = dXf_p * Xf[:, sl]
                r2 = jnp.where(lane < SSD_P, t2, 0.0)
                dcsl = dcsl + jnp.sum(r2, keepdims=True) * (hsel == h0).astype(F32) \
                    + jnp.sum(t2 - r2, keepdims=True) * (hsel == h0 + 1).astype(F32)
                stg_ref[:, sl] = dXd_p + dXf_p * f_e[:, sl]
            dxbc_ref[:, SSD_INNER + g * SSD_N:SSD_INNER + (g + 1) * SSD_N] = dB.astype(dxbc_ref.dtype)
            dxbc_ref[:, SSD_INNER + (SSD_G + g) * SSD_N:SSD_INNER + (SSD_G + g + 1) * SSD_N] = dC.astype(dxbc_ref.dtype)
        dXd = stg_ref[...]
        dxbc_ref[:, 0:SSD_INNER] = (dXd * dt_e + dsk_e * dY).astype(dxbc_ref.dtype)
        rowl = lax.broadcasted_iota(jnp.int32, (L, 1), 0)
        dcs = dcs + jnp.where(rowl == L - 1, dcsl, 0.0)
        dalpha = _dot_sel(upper.astype(F32), dcs, split="b")
        ddt = _dot_sel(dXd * X, ET, terms=2) + dalpha * a
        dalog_ref[...] += jnp.sum(dalpha * dt, axis=0, keepdims=True) * a
        ddtr = ddt * _sigmoid(dtr + dtb)
        spread = (lax.broadcasted_iota(jnp.int32, (SSD_H, 128), 0) == lax.broadcasted_iota(jnp.int32, (SSD_H, 128), 1)).astype(F32)
        ddtr_ref[...] = _dot(ddtr, spread).astype(ddtr_ref.dtype)
        ddtb_ref[...] += jnp.sum(ddtr, axis=0, keepdims=True)
        ddsk_ref[...] += jnp.sum(_dot_sel(dY * X, ET, terms=2), axis=0, keepdims=True)

    rowr = lambda b, c: (b * C + (C - 1 - c), 0)
    small = pl.BlockSpec((1, SSD_H), lambda b, c: (0, 0))
    return _call_with_comm(
        body, (nseq, C), "ssd_bwd",
        [pl.BlockSpec((L, SSD_INNER), rowr),
         pl.BlockSpec((L, SSD_G * SSD_N), lambda b, c: (b * C + (C - 1 - c), SSD_INNER // (SSD_G * SSD_N))),
         pl.BlockSpec((L, SSD_G * SSD_N), lambda b, c: (b * C + (C - 1 - c), SSD_INNER // (SSD_G * SSD_N) + 1)),
         pl.BlockSpec((L, 128), rowr), small, small, small,
         pl.BlockSpec((1, 1, NP, 2 * SSD_P, SSD_N), lambda b, c: (b, C - 1 - c, 0, 0, 0)),
         pl.BlockSpec((L, SSD_INNER), rowr)],
        [xbc, xbc, xbc, dtr, dtb, alog, dsk, prev, dy],
        [pl.BlockSpec((L, CONV_CH), rowr), pl.BlockSpec((L, 128), rowr), small, small, small],
        [jax.ShapeDtypeStruct((T, CONV_CH), BF16), jax.ShapeDtypeStruct((T, 128), BF16),
         jax.ShapeDtypeStruct((1, SSD_H), F32), jax.ShapeDtypeStruct((1, SSD_H), F32), jax.ShapeDtypeStruct((1, SSD_H), F32)],
        comm, scratch=[pltpu.VMEM((NP, 2 * SSD_P, SSD_N), F32), pltpu.VMEM((L, SSD_INNER), F32)], sem=("arbitrary", "arbitrary"))


SLOT = 128
ATT_T = 512
ATT_HP = 1
LOG2E = math.log2(math.e)
Q_SCALE = QK ** -0.5 * LOG2E


def _col_to_row(col):
    n = col.shape[0]
    eye = lax.broadcasted_iota(jnp.int32, (n, n), 0) == lax.broadcasted_iota(jnp.int32, (n, n), 1)
    return jnp.sum(jnp.where(eye, col, 0.0), axis=0, keepdims=True)


def attn_slot_fwd(q, k, v, nseq, comm=None):
    T = q.shape[0]
    S = T // nseq
    t = min(ATT_T, S)
    nb = S // t
    cols = [slice(h * SLOT, (h + 1) * SLOT) for h in range(ATT_HP)]

    def body(q_ref, k_ref, v_ref, o_ref, lse_ref):
        causal = lax.broadcasted_iota(jnp.int32, (t, t), 1) <= lax.broadcasted_iota(jnp.int32, (t, t), 0)
        for qi in range(nb):
            rows = slice(qi * t, (qi + 1) * t)
            state = [None] * ATT_HP
            for kj in range(qi + 1):
                keys = slice(kj * t, (kj + 1) * t)
                for h, c in enumerate(cols):
                    s = _dot(q_ref[rows, c], k_ref[keys, c], "nt")
                    if kj == qi:
                        s = jnp.where(causal, s, -1e30)
                    bm = jnp.max(s, axis=1, keepdims=True)
                    if kj == 0:
                        p = jnp.exp2(s - bm)
                        state[h] = (bm, jnp.sum(p, axis=1, keepdims=True), _dot(p, v_ref[keys, c]))
                    else:
                        m, l, acc = state[h]
                        m_new = jnp.maximum(m, bm)
                        corr = jnp.exp2(m - m_new)
                        p = jnp.exp2(s - m_new)
                        state[h] = (m_new, l * corr + jnp.sum(p, axis=1, keepdims=True), acc * corr + _dot(p, v_ref[keys, c]))
            for h, c in enumerate(cols):
                m, l, acc = state[h]
                o_ref[rows, c] = (acc / l).astype(o_ref.dtype)
                lse_ref[0, h, :, rows] = _col_to_row(m + jnp.log2(l))

    blk = pl.BlockSpec((S, ATT_HP * SLOT), lambda b, h: (b, h))
    return _call_with_comm(
        body, (nseq, MLA_H // ATT_HP), "attn_fwd", [blk, blk, blk], [q, k, v],
        [blk, pl.BlockSpec((1, ATT_HP, 1, S), lambda b, h: (b, h, 0, 0))],
        [jax.ShapeDtypeStruct((T, MLA_H * SLOT), BF16), jax.ShapeDtypeStruct((nseq, MLA_H, 1, S), F32)], comm)


def attn_slot_bwd(q, k, v, o, lse, do, nseq, comm=None):
    T = q.shape[0]
    S = T // nseq
    t = min(ATT_T, S)
    nb = S // t
    scale = QK ** -0.5
    cols = [slice(h * SLOT, (h + 1) * SLOT) for h in range(ATT_HP)]

    def body(q_ref, k_ref, v_ref, o_ref, lse_ref, do_ref, dq_ref, dk_ref, dv_ref, dqa_ref):
        causal_t = lax.broadcasted_iota(jnp.int32, (t, t), 0) <= lax.broadcasted_iota(jnp.int32, (t, t), 1)
        ones = jnp.ones((8, SLOT), F32)
        delta = {}
        for qi in range(nb):
            sl = slice(qi * t, (qi + 1) * t)
            for h, c in enumerate(cols):
                prod = do_ref[sl, c].astype(F32) * o_ref[sl, c].astype(F32)
                delta[h, qi] = _dot_sel(ones, prod, "nt", split="b", terms=2)[0:1, :]
        for kj in range(nb):
            ks = slice(kj * t, (kj + 1) * t)
            dk = [None] * ATT_HP
            dv = [None] * ATT_HP
            for qi in range(kj, nb):
                sl = slice(qi * t, (qi + 1) * t)
                for h, c in enumerate(cols):
                    kb, vb, qb, dob = k_ref[ks, c], v_ref[ks, c], q_ref[sl, c], do_ref[sl, c]
                    st = _dot(kb, qb, "nt")
                    pt = jnp.exp2(st - lse_ref[0, h, :, sl])
                    if qi == kj:
                        pt = jnp.where(causal_t, pt, 0.0)
                    dpt = _dot(vb, dob, "nt")
                    dst = (pt * (dpt - delta[h, qi])).astype(BF16)
                    dvc = _dot(pt, dob)
                    dkc = _dot(dst, qb) * (1.0 / LOG2E)
                    dv[h] = dvc if dv[h] is None else dv[h] + dvc
                    dk[h] = dkc if dk[h] is None else dk[h] + dkc
                    dqc = _dot(dst, kb, "tn") * scale
                    if kj > 0:
                        dqc = dqc + dqa_ref[sl, c]
                    if qi == kj:
                        dq_ref[sl, c] = dqc.astype(dq_ref.dtype)
                    else:
                        dqa_ref[sl, c] = dqc
            for h, c in enumerate(cols):
                dk_ref[ks, c] = dk[h].astype(dk_ref.dtype)
                dv_ref[ks, c] = dv[h].astype(dv_ref.dtype)

    blk = pl.BlockSpec((S, ATT_HP * SLOT), lambda b, h: (b, h))
    lse_spec = pl.BlockSpec((1, ATT_HP, 1, S), lambda b, h: (b, h, 0, 0))
    W = MLA_H * SLOT
    return _call_with_comm(
        body, (nseq, MLA_H // ATT_HP), "attn_bwd", [blk, blk, blk, blk, lse_spec, blk], [q, k, v, o, lse, do], [blk, blk, blk],
        [jax.ShapeDtypeStruct((T, W), BF16)] * 3, comm, scratch=[pltpu.VMEM((S, ATT_HP * SLOT), F32)])


def _rope_coeffs(pos, inv):
    half = ROPE // 2
    ang = pos * inv
    lane = lax.broadcasted_iota(jnp.int32, (1, SLOT), 1)
    sn = jnp.sin(ang)
    C = jnp.where(lane < NOPE, 1.0, jnp.where(lane < QK, jnp.cos(ang), 0.0))
    Sg = jnp.where((lane >= NOPE) & (lane < NOPE + half), -sn, jnp.where((lane >= NOPE + half) & (lane < QK), sn, 0.0))
    return C, Sg


def _rope_inputs(positions):
    half = ROPE // 2
    inv = ROPE_THETA ** (-jnp.arange(0, ROPE, 2, dtype=F32) / ROPE)
    row = jnp.zeros((1, SLOT), F32).at[0, NOPE:NOPE + half].set(inv).at[0, NOPE + half:QK].set(inv)
    return positions.astype(F32).reshape(-1, 1), row


def _place_k_rope(kr_lanes):
    r = lax.broadcasted_iota(jnp.int32, (SLOT, SLOT), 0)
    c = lax.broadcasted_iota(jnp.int32, (SLOT, SLOT), 1)
    return _dot_sel(kr_lanes, ((c == r + NOPE) & (r < ROPE)).astype(F32))


def rope_table(pos, inv):
    return rowwise(_rope_coeffs, [pos], [inv], [(SLOT, F32), (SLOT, F32)], [], "rope_table")


def rope_q_epilogue(accs, ex):
    C, Sg = ex[0], ex[1]
    reps = accs[0].shape[1] // SLOT
    return ((accs[0] * jnp.tile(C, (1, reps)) + _rope_swap(accs[0]) * jnp.tile(Sg, (1, reps))) * Q_SCALE,)


def rope_k_epilogue(accs, ex):
    C, Sg = ex[0], ex[1]
    kr = _place_k_rope(ex[2][:, SLOT:2 * SLOT])
    kr = kr * C + _rope_swap(kr) * Sg
    return (accs[0] + jnp.tile(kr, (1, accs[0].shape[1] // SLOT)),)


def _rope_swap(x):
    W = x.shape[1]
    half = ROPE // 2
    lane = lax.broadcasted_iota(jnp.int32, (1, W), 1) & (SLOT - 1)
    up = pltpu.roll(x, W - half, axis=1)
    dn = pltpu.roll(x, half, axis=1)
    return jnp.where((lane >= NOPE) & (lane < NOPE + half), up, jnp.where((lane >= NOPE + half) & (lane < QK), dn, 0.0))


def rope_slot_bwd(dq, dk, C, Sg, name):
    def fn(dqv, dkv, C, Sg):
        ct, stl = jnp.tile(C, (1, MLA_H)), jnp.tile(Sg, (1, MLA_H))
        dqo = dqv * ct - _rope_swap(dqv) * stl
        tot = dkv[:, 0:SLOT]
        for h in range(1, MLA_H):
            tot = tot + dkv[:, h * SLOT:(h + 1) * SLOT]
        u = tot * C - _rope_swap(tot) * Sg
        r = lax.broadcasted_iota(jnp.int32, (SLOT, SLOT), 0)
        c = lax.broadcasted_iota(jnp.int32, (SLOT, SLOT), 1)
        unplace = ((r == c + NOPE) & (c < ROPE)).astype(F32)
        return dqo, dkv, _dot_sel(u, unplace, terms=2)
    W = MLA_H * SLOT
    return rowwise(fn, [dq, dk, C, Sg], [], [(W, BF16), (W, BF16), (SLOT, BF16)], [], name)


XA_BLK = 512


def xattn_fwd(q, k, v, nseq, comm=None):
    T = q.shape[0]
    S = T // nseq
    M = k.shape[0] // nseq
    tq = min(XA_BLK, S)
    nq = S // tq
    scale = XA_D ** -0.5

    def body(q_ref, k_ref, v_ref, o_ref):
        s = _dot(q_ref[...], k_ref[...], "nt") * scale
        p = jnp.exp(s - jnp.max(s, axis=1, keepdims=True))
        p = p / jnp.sum(p, axis=1, keepdims=True)
        o_ref[...] = _dot(p, v_ref[...]).astype(o_ref.dtype)

    qs = pl.BlockSpec((tq, XA_D), lambda b, h, i: (b * nq + i, h))
    ks = pl.BlockSpec((M, XA_D), lambda b, h, i: (b, h))
    return _call_with_comm(body, (nseq, XA_H, nq), "xattn_fwd", [qs, ks, ks], [q, k, v], [qs],
                           [jax.ShapeDtypeStruct((T, XA_H * XA_D), BF16)], comm)


def xattn_bwd(q, k, v, do, nseq):
    T = q.shape[0]
    S = T // nseq
    M = k.shape[0] // nseq
    tq = min(XA_BLK, S)
    nq = S // tq
    scale = XA_D ** -0.5

    def body(q_ref, k_ref, v_ref, do_ref, dq_ref, dk_ref, dv_ref):
        @pl.when(pl.program_id(2) == 0)
        def _():
            dk_ref[...] = jnp.zeros_like(dk_ref)
            dv_ref[...] = jnp.zeros_like(dv_ref)

        qb, kb, vb, dob = q_ref[...], k_ref[...], v_ref[...], do_ref[...]
        s = _dot(qb, kb, "nt") * scale
        p = jnp.exp(s - jnp.max(s, axis=1, keepdims=True))
        p = p / jnp.sum(p, axis=1, keepdims=True)
        dp = _dot(dob, vb, "nt")
        ds = p * (dp - jnp.sum(dp * p, axis=1, keepdims=True)) * scale
        dq_ref[...] = _dot(ds, kb).astype(dq_ref.dtype)
        dk_ref[...] += _dot(ds, qb, "tn")
        dv_ref[...] += _dot(p, dob, "tn")

    qs = pl.BlockSpec((tq, XA_D), lambda b, h, i: (b * nq + i, h))
    ks = pl.BlockSpec((M, XA_D), lambda b, h, i: (b, h))
    return pl.pallas_call(
        body, grid=(nseq, XA_H, nq), name="xattn_bwd", in_specs=[qs, ks, ks, qs], out_specs=[qs, ks, ks],
        out_shape=[jax.ShapeDtypeStruct((T, XA_H * XA_D), BF16), jax.ShapeDtypeStruct(k.shape, F32),
                   jax.ShapeDtypeStruct(k.shape, F32)],
        compiler_params=_cp("parallel", "parallel", "arbitrary"),
    )(q, k, v, do)


CONV_BLK = 256


def _shift_down(x, s, rows):
    if s == 0:
        return x
    return jnp.where(rows >= s, pltpu.roll(x, s, axis=0), 0.0)


def _shift_up(x, s, rows):
    if s == 0:
        return x
    S = x.shape[0]
    return jnp.where(rows < S - s, pltpu.roll(x, S - s, axis=0), 0.0)


def conv_fwd(x, w, b, nseq):
    T, CH = x.shape
    S = T // nseq

    def body(x_ref, w_ref, b_ref, o_ref):
        xv = x_ref[...].astype(F32)
        rows = lax.broadcasted_iota(jnp.int32, (S, 1), 0)
        c = jnp.zeros_like(xv) + b_ref[...]
        for kk in range(CONV_K):
            c = c + w_ref[kk:kk + 1, :] * _shift_down(xv, CONV_K - 1 - kk, rows)
        o_ref[...] = (c * _sigmoid(c)).astype(o_ref.dtype)

    xs = pl.BlockSpec((S, CONV_BLK), lambda j, bb: (bb, j))
    return pl.pallas_call(
        body, grid=(CH // CONV_BLK, nseq), name="conv_fwd",
        in_specs=[xs, pl.BlockSpec((CONV_K, CONV_BLK), lambda j, bb: (0, j)), pl.BlockSpec((1, CONV_BLK), lambda j, bb: (0, j))],
        out_specs=xs, out_shape=jax.ShapeDtypeStruct((T, CH), BF16),
        compiler_params=_cp("parallel", "parallel"),
    )(x, w, b)


def conv_bwd(x, w, b, dout, nseq):
    T, CH = x.shape
    S = T // nseq

    def body(x_ref, w_ref, b_ref, do_ref, dx_ref, dw_ref, db_ref):
        @pl.when(pl.program_id(1) == 0)
        def _():
            dw_ref[...] = jnp.zeros_like(dw_ref)
            db_ref[...] = jnp.zeros_like(db_ref)

        xv = x_ref[...].astype(F32)
        rows = lax.broadcasted_iota(jnp.int32, (S, 1), 0)
        c = jnp.zeros_like(xv) + b_ref[...]
        sh = [_shift_down(xv, CONV_K - 1 - kk, rows) for kk in range(CONV_K)]
        for kk in range(CONV_K):
            c = c + w_ref[kk:kk + 1, :] * sh[kk]
        sg = _sigmoid(c)
        dc = do_ref[...].astype(F32) * sg * (1.0 + c * (1.0 - sg))
        dx = jnp.zeros_like(xv)
        for kk in range(CONV_K):
            dx = dx + w_ref[kk:kk + 1, :] * _shift_up(dc, CONV_K - 1 - kk, rows)
            dw_ref[kk:kk + 1, :] += jnp.sum(dc * sh[kk], axis=0, keepdims=True)
        dx_ref[...] = dx.astype(dx_ref.dtype)
        db_ref[...] += jnp.sum(dc, axis=0, keepdims=True)

    xs = pl.BlockSpec((S, CONV_BLK), lambda j, bb: (bb, j))
    ws = pl.BlockSpec((CONV_K, CONV_BLK), lambda j, bb: (0, j))
    bs = pl.BlockSpec((1, CONV_BLK), lambda j, bb: (0, j))
    return pl.pallas_call(
        body, grid=(CH // CONV_BLK, nseq), name="conv_bwd",
        in_specs=[xs, ws, bs, xs], out_specs=[xs, ws, bs],
        out_shape=[jax.ShapeDtypeStruct((T, CH), BF16), jax.ShapeDtypeStruct((CONV_K, CH), F32),
                   jax.ShapeDtypeStruct((1, CH), F32)],
        compiler_params=_cp("parallel", "arbitrary"),
    )(x, w, b, dout)


def _dims(a, b, mode):
    M = a.shape[1] if mode[0] == "t" else a.shape[0]
    K = a.shape[0] if mode[0] == "t" else a.shape[1]
    N = b.shape[0] if mode[1] == "t" else b.shape[1]
    return M, K, N


def _tile(dim, prefs):
    for p in prefs:
        if dim % p == 0:
            return p
    return dim


def mm(groups, out_dtypes, name, tm=None, tn=None, tk=None, epi=None, extras=(), comm=None, sub=1, n_sum=0):
    a0, b0, m0 = groups[0][0]
    M, K0, N = _dims(a0, b0, m0)
    tm = tm or _tile(M, (1024, 512, 256, 128))
    tn = tn or _tile(N, (1024, 512, 256, 128))
    flat = [p for g in groups for p in g]
    nk = 1 if tk is None else K0 // tk
    in_specs, args = [], []
    for a, b, mode in flat:
        _, K, _ = _dims(a, b, mode)
        kb = K if tk is None else tk
        in_specs.append(pl.BlockSpec((kb, tm), lambda i, j, k: (k, i)) if mode[0] == "t"
                        else pl.BlockSpec((tm, kb), lambda i, j, k: (i, k)))
        in_specs.append(pl.BlockSpec((tn, kb), lambda i, j, k: (j, k)) if mode[1] == "t"
                        else pl.BlockSpec((kb, tn), lambda i, j, k: (k, j)))
        args += [a, b]
    kinds = []
    for e in extras:
        kind, e = e if isinstance(e, tuple) else ("vec" if e.shape[0] == 1 and M != 1 else "tile", e)
        in_specs.append({"tile": pl.BlockSpec((tm, tn), lambda i, j, k: (i, j)),
                         "vec": pl.BlockSpec((1, tn), lambda i, j, k: (0, j)),
                         "rows": pl.BlockSpec((tm, e.shape[1]), lambda i, j, k: (i, 0)),
                         "whole": pl.BlockSpec(e.shape, lambda i, j, k: (0, 0))}[kind])
        kinds.append(kind)
        args.append(e)
    n_in = len(args)
    n_main = len(out_dtypes)
    n_out = n_main + n_sum
    assert n_sum == 0 or (tn == N and tk is None)
    ng = len(groups)
    sizes = [len(g) for g in groups]

    def body(*refs):
        ins, outs, accs = refs[:n_in], refs[n_in:n_in + n_out], refs[n_in + n_out:]
        kk = pl.program_id(2)

        def dots(rs):
            vals, pos = [], 0
            for gi in range(ng):
                acc = None
                for _ in range(sizes[gi]):
                    mode = flat[pos // 2][2]
                    av = ins[pos][:, rs] if mode[0] == "t" else ins[pos][rs, :]
                    d = _dot(av, ins[pos + 1][...], mode)
                    acc = d if acc is None else acc + d
                    pos += 2
                vals.append(acc)
            return vals

        def finish(accv, rs, first_chunk=True):
            ex = [(r[rs, :] if kind in ("tile", "rows") else r[...]).astype(F32) for kind, r in zip(kinds, ins[2 * len(flat):])]
            res = epi(accv, ex) if epi is not None else tuple(accv)
            for o, r in zip(outs[:n_main], res[:n_main]):
                o[rs, :] = r.astype(o.dtype)
            for o, r in zip(outs[n_main:], res[n_main:]):
                if first_chunk:
                    @pl.when(pl.program_id(0) == 0)
                    def _():
                        o[...] = r

                    @pl.when(pl.program_id(0) > 0)
                    def _():
                        o[...] += r
                else:
                    o[...] += r

        if nk == 1:
            for r in range(sub):
                rs = slice(r * (tm // sub), (r + 1) * (tm // sub))
                finish(dots(rs), rs, r == 0)
        else:
            vals = dots(slice(0, tm))
            finish = functools.partial(finish, rs=slice(0, tm))
            @pl.when(kk == 0)
            def _():
                for ar, vv in zip(accs, vals):
                    ar[...] = vv

            @pl.when(kk > 0)
            def _():
                for ar, vv in zip(accs, vals):
                    ar[...] += vv

            @pl.when(kk == nk - 1)
            def _():
                finish([ar[...] for ar in accs])

    grid = (M // tm, N // tn, nk)
    out_specs = [pl.BlockSpec((tm, tn), lambda i, j, k: (i, j)) for _ in out_dtypes] \
        + [pl.BlockSpec((1, tn), lambda i, j, k: (0, j))] * n_sum
    out_shape = [jax.ShapeDtypeStruct((M, N), dt) for dt in out_dtypes] + [jax.ShapeDtypeStruct((1, N), F32)] * n_sum
    scratch = [pltpu.VMEM((tm, tn), F32) for _ in range(ng if nk > 1 else 0)]
    sem = ("arbitrary" if n_sum else "parallel", "parallel", "arbitrary")
    if comm is not None:
        body = _attach(comm, body, n_in, n_out, *_grid_ends(grid))
        in_specs, args = in_specs + [HBM_SPEC] * len(comm.inputs), args + comm.inputs
        out_specs, out_shape = out_specs + [HBM_SPEC] * len(comm.out_shapes), out_shape + comm.out_shapes
        scratch, sem = scratch + comm.sems, ("arbitrary",) * 3
    return pl.pallas_call(body, grid=grid, name=name, in_specs=in_specs, out_specs=out_specs, out_shape=out_shape,
                          scratch_shapes=scratch, compiler_params=_cp(*sem))(*args)


def mm1(a, b, mode, out_dtype, name, **kw):
    return mm([[(a, b, mode)]], [out_dtype], name, **kw)[0]


ROW_BLK = 512


def rowwise(fn, rows, consts, outs, accs, name, tb=ROW_BLK, comm=None):
    rows = [r if isinstance(r, tuple) else (r, r.shape[1], 0) for r in rows]
    T = rows[0][0].shape[0]
    tb = min(tb, T)
    n_r, n_c, n_o, n_a = len(rows), len(consts), len(outs), len(accs)

    def body(*refs):
        vals = [r[...].astype(F32) for r in refs[:n_r + n_c]]
        res = fn(*vals)
        o_refs = refs[n_r + n_c:n_r + n_c + n_o]
        a_refs = refs[n_r + n_c + n_o:]
        for o, r in zip(o_refs, res[:n_o]):
            o[...] = r.astype(o.dtype)
        if n_a:
            @pl.when(pl.program_id(0) == 0)
            def _():
                for ar in a_refs:
                    ar[...] = jnp.zeros_like(ar)
            for ar, r in zip(a_refs, res[n_o:]):
                ar[...] += r

    return _call_with_comm(
        body, (T // tb,), name,
        [pl.BlockSpec((tb, w), functools.partial(lambda i, j: (i, j), j=j)) for _, w, j in rows]
        + [pl.BlockSpec(c.shape, lambda i: (0, 0)) for c in consts],
        [r[0] for r in rows] + list(consts),
        [pl.BlockSpec((tb, d), lambda i: (i, 0)) for d, _ in outs] + [pl.BlockSpec(s, lambda i: (0, 0)) for s in accs],
        [jax.ShapeDtypeStruct((T, d), dt) for d, dt in outs] + [jax.ShapeDtypeStruct(s, F32) for s in accs],
        comm, sem=("arbitrary" if n_a else "parallel",))


def _rms_stats(x):
    r = lax.rsqrt(jnp.mean(x * x, axis=-1, keepdims=True) + EPS)
    return r, x * r


def _rms_bwd(x, g, dy):
    r, xn = _rms_stats(x)
    dyg = dy * g
    dx = r * (dyg - xn * jnp.mean(dyg * xn, axis=-1, keepdims=True))
    return dx, jnp.sum(dy * xn, axis=0, keepdims=True)


def rms_fwd(x, g, name, comm=None):
    res = rowwise(lambda xv, gv: (_rms_stats(xv)[1] * gv,), [x], [g], [(x.shape[1], BF16)], [], name, comm=comm)
    return res[0] if comm is None else (res[0], res[1:])


def rms_bwd(x, g, dy, name, resid=None, dx_dtype=F32):
    def fn(*v):
        if resid is None:
            xv, dyv, gv = v
            dx, dg = _rms_bwd(xv, gv, dyv)
        else:
            xv, dyv, rv, gv = v
            dx, dg = _rms_bwd(xv, gv, dyv)
            dx = dx + rv
        return dx, dg
    rows = [x, dy] + ([] if resid is None else [resid])
    return rowwise(fn, rows, [g], [(x.shape[1], dx_dtype)], [(1, x.shape[1])], name)


def mm_rms_bwd(pairs, x, g, name, resid=None, dx_dtype=F32, comm=None):
    def epi(accs, ex):
        dx, dg = _rms_bwd(ex[0], ex[-1], accs[0])
        return (dx if resid is None else dx + ex[1]), dg
    extras = [x] + ([] if resid is None else [resid]) + [g]
    return mm([pairs], [dx_dtype], name, tm=min(256, x.shape[0]), tn=x.shape[1], epi=epi, extras=extras, comm=comm, n_sum=1)


def mm_resid(a, b, x, g, wgt, name, comm=None, target=None):
    def epi(accs, ex):
        y = ex[0] + wgt * _rms_stats(accs[0])[1] * ex[1]
        if target is None:
            return accs[0], y
        d = y - ex[2]
        return accs[0], d / D, jnp.sum(d * d, axis=0, keepdims=True)
    return mm([[(a, b, "nn")]], [F32, F32], name, tm=min(512, a.shape[0]), tn=b.shape[1], epi=epi,
              extras=[x, g] + ([] if target is None else [target]), sub=2, comm=comm, n_sum=0 if target is None else 1)


def resid_bwd(h, g, dy, wgt, name):
    def fn(hv, dyv, gv):
        dx, dg = _rms_bwd(hv, gv, dyv)
        return wgt * dx, wgt * dg
    return rowwise(fn, [h, dy], [g], [(h.shape[1], BF16)], [(1, h.shape[1])], name)


def _silu_parts(g):
    s = _sigmoid(g)
    return g * s, s * (1.0 + g * (1.0 - s))


def gated_norm_fwd(y, z, g, name):
    W = SSD_INNER // SSD_G

    def fn(yv, zv, gv):
        yg = yv * _silu_parts(zv)[0]
        return (jnp.concatenate([_rms_stats(yg[:, i * W:(i + 1) * W])[1] for i in range(SSD_G)], axis=1) * gv,)
    return rowwise(fn, [y, z], [g], [(SSD_INNER, BF16)], [], name)[0]


def gated_norm_bwd(y, z, dyn, g, name):
    W = SSD_INNER // SSD_G

    def fn(yv, zv, dv, gv):
        sil, dsil = _silu_parts(zv)
        yg = yv * sil
        parts = [_rms_bwd(yg[:, i * W:(i + 1) * W], gv[:, i * W:(i + 1) * W], dv[:, i * W:(i + 1) * W]) for i in range(SSD_G)]
        dyg = jnp.concatenate([p[0] for p in parts], axis=1)
        dg = jnp.concatenate([p[1] for p in parts], axis=1)
        return dyg * sil, dyg * yv * dsil, dg
    return rowwise(fn, [y, z, dyn], [g], [(SSD_INNER, BF16), (SSD_INNER, BF16)], [(1, SSD_INNER)], name)


def merge_fwd(gl, ys, ym, gb, name):
    def fn(glv, ysv, ymv, gbv):
        gt = _sigmoid(glv + gbv)
        return (gt[:, :D] * ysv + gt[:, D:] * ymv,)
    return rowwise(fn, [gl, ys, ym], [gb], [(D, BF16)], [], name)[0]


def merge_bwd(gl, ys, ym, dm, gb, name):
    def fn(glv, ysv, ymv, dmv, gbv):
        gt = _sigmoid(glv + gbv)
        gs, gm = gt[:, :D], gt[:, D:]
        dgl = jnp.concatenate([dmv * ysv * gs * (1.0 - gs), dmv * ymv * gm * (1.0 - gm)], axis=1)
        return dmv * gs, dmv * gm, dgl, jnp.sum(dgl, axis=0, keepdims=True)
    return rowwise(fn, [gl, ys, ym, dm], [gb], [(D, BF16), (D, BF16), (2 * D, BF16)], [(1, 2 * D)], name)


def loss_head(y, tgt, name):
    def fn(yv, tv):
        d = yv - tv
        part = 0.5 * jnp.sum(jnp.sum(d * d, axis=1, keepdims=True), axis=0, keepdims=True) / D
        return d / D, jnp.broadcast_to(part, (1, 128))
    return rowwise(fn, [y, tgt], [], [(D, F32)], [(1, 128)], name)


def _adamw_math(wv, gv, mv, vv):
    mn = B1 * mv + (1.0 - B1) * gv
    vn = B2 * vv + (1.0 - B2) * (gv * gv)
    mh = mn / (1.0 - B1 ** STEP)
    vh = vn / (1.0 - B2 ** STEP)
    return -LR * (mh / (jnp.sqrt(vh) + AEPS) + WD * wv), mn, vn


def adamw(w, g, m, v, name):
    R, C = w.shape
    tb = _tile(R, (256, 128, 64, 32, 16, 8))
    return rowwise(_adamw_math, [w, g, m, v], [], [(C, F32)] * 3, [], name, tb=tb)


def adamw_small(packed, ws, ms, vs):
    k = len(ws)
    sizes = [x.shape[1] for x in ws]

    def body(*refs):
        p_ref, w_refs, m_refs, v_refs = refs[0], refs[1:1 + k], refs[1 + k:1 + 2 * k], refs[1 + 2 * k:1 + 3 * k]
        outs = refs[1 + 3 * k:]
        r0 = 0
        for i, n in enumerate(sizes):
            nr = -(-n // 128)
            g = jnp.concatenate([p_ref[r0 + r:r0 + r + 1, :] for r in range(nr)], axis=1)[:, :n]
            r0 += nr
            outs[i][...] = g
            outs[k + i][...], outs[2 * k + i][...], outs[3 * k + i][...] = _adamw_math(w_refs[i][...], g, m_refs[i][...], v_refs[i][...])

    res = pl.pallas_call(body, name="adamw_small",
                         out_shape=[jax.ShapeDtypeStruct((1, n), F32) for _ in range(4) for n in sizes])(packed, *ws, *ms, *vs)
    return [res[j * k:(j + 1) * k] for j in range(4)]


def adamw_from_slots(recv, piece, w, m, v, name, token=None):
    K, n = w.shape
    ns = recv.shape[0]
    assert recv.shape[2] == n and recv.shape[1] % K == 0
    tb = _tile(K, (256, 176, 128, 64, 32, 16, 8)) if K % 8 == 0 else K
    r_spec = pl.BlockSpec((ns, tb, n), lambda i: (0, piece * (K // tb) + i, 0))
    w_spec = pl.BlockSpec((tb, n), lambda i: (i, 0))

    def body(r_ref, w_ref, m_ref, v_ref, *rest):
        g_ref, d_ref, mo_ref, vo_ref = rest[-4:]
        g = r_ref[0].astype(F32)
        for s in range(1, ns):
            g = g + r_ref[s].astype(F32)
        g_ref[...] = g
        d_ref[...], mo_ref[...], vo_ref[...] = _adamw_math(w_ref[...], g, m_ref[...], v_ref[...])

    extra = [] if token is None else [token]
    return pl.pallas_call(
        body, grid=(K // tb,), name=name,
        in_specs=[r_spec, w_spec, w_spec, w_spec] + [pl.BlockSpec(t.shape, lambda i: (0, 0)) for t in extra], out_specs=[w_spec] * 4,
        out_shape=[jax.ShapeDtypeStruct((K, n), F32)] * 4, compiler_params=_cp("parallel"),
    )(recv, w, m, v, *extra)


def _me():
    return lax.axis_index("x"), lax.axis_index("y"), lax.axis_index("c")


def _dev_index():
    x, y, c = _me()
    return 4 * x + 2 * y + c


HBM_SPEC = pl.BlockSpec(memory_space=pl.ANY)


class GatherComm:
    def __init__(self, shards):
        self.inputs = [s for s, _ in shards]
        self.rows = [list(r) for _, r in shards]
        n = len(shards)
        self.out_shapes = [jax.ShapeDtypeStruct((N_DEV, r, s.shape[1]), s.dtype) for s, rows in shards for r in rows]
        self.sems = [pltpu.SemaphoreType.DMA((7 * n,)), pltpu.SemaphoreType.DMA((7 * n,)), pltpu.SemaphoreType.DMA((n,))]

    def _plan(self, x_refs, out_refs, sems):
        send_sems, recv_sems, local_sems = sems
        x, y, c = _me()
        me, sibling = (x, y, c), (x, y, 1 - c)
        chips = [(1 - x, y), (x, 1 - y), (1 - x, 1 - y)]
        index = lambda px, py, pc: 4 * px + 2 * py + pc
        mine, first, passed, whole = [], [], [], []
        pos = 0
        for i, rows in enumerate(self.rows):
            kw = lambda k: dict(send_sem=send_sems.at[7 * i + k], recv_sem=recv_sems.at[7 * i + k], device_id_type=MESH)
            r0 = 0
            fwd = [[] for _ in chips]
            for j, nr in enumerate(rows):
                out, src = out_refs[pos + j], x_refs[i].at[pl.ds(r0, nr)]
                mine.append(pltpu.make_async_copy(src, out.at[index(*me)], local_sems.at[i]))
                first.append(pltpu.make_async_remote_copy(src_ref=src, dst_ref=out.at[index(*me)], device_id=sibling, **kw(0)))
                for jj, chip in enumerate(chips):
                    first.append(pltpu.make_async_remote_copy(src_ref=src, dst_ref=out.at[index(*me)], device_id=(*chip, c),
                                                              **kw(1 + jj)))
                    blk = out.at[index(*chip, c)]
                    fwd[jj].append(pltpu.make_async_remote_copy(src_ref=blk, dst_ref=blk, device_id=sibling, **kw(4 + jj)))
                r0 += nr
            passed.append(fwd)
            whole.append([pltpu.make_async_remote_copy(src_ref=x_refs[i], dst_ref=x_refs[i], device_id=sibling, **kw(k))
                          for k in range(7)])
            pos += len(rows)
        return mine, first, passed, whole

    def start(self, x_refs, out_refs, sems):
        mine, first, _, _ = self._plan(x_refs, out_refs, sems)
        for cp in mine + first:
            cp.start()

    def finish(self, x_refs, out_refs, sems):
        _, _, passed, whole = self._plan(x_refs, out_refs, sems)
        local_sems = sems[2]
        for i, fwd in enumerate(passed):
            for jj in range(3):
                whole[i][1 + jj].wait_recv()
                for cp in fwd[jj]:
                    cp.start()
        for i in range(len(passed)):
            whole[i][0].wait_recv()
            for jj in range(3):
                whole[i][4 + jj].wait_recv()
        for i in range(len(passed)):
            for k in range(7):
                whole[i][k].wait_send()
            pltpu.make_async_copy(x_refs[i], x_refs[i], local_sems.at[i]).wait()


def run_comm(comm, name):
    n_in, n_out = len(comm.inputs), len(comm.out_shapes)

    def body(*refs):
        ins, outs, sems = refs[:n_in], refs[n_in:n_in + n_out], refs[n_in + n_out:]
        comm.start(ins, outs, sems)
        comm.finish(ins, outs, sems)

    return pl.pallas_call(body, name=name, out_shape=comm.out_shapes, in_specs=[HBM_SPEC] * n_in,
                          out_specs=[HBM_SPEC] * n_out, scratch_shapes=comm.sems)(*comm.inputs)


def _attach(comm, body, n_in, n_out, first, last):
    if comm is None:
        return body
    ci, co, cs = len(comm.inputs), len(comm.out_shapes), len(comm.sems)

    def wrapped(*refs):
        h_in, c_in = refs[:n_in], refs[n_in:n_in + ci]
        h_out, c_out = refs[n_in + ci:n_in + ci + n_out], refs[n_in + ci + n_out:n_in + ci + n_out + co]
        rest = refs[n_in + ci + n_out + co:]
        h_scr, c_sem = rest[:len(rest) - cs], rest[len(rest) - cs:]

        @pl.when(first())
        def _():
            comm.start(c_in, c_out, c_sem)

        body(*h_in, *h_out, *h_scr)

        @pl.when(last())
        def _():
            comm.finish(c_in, c_out, c_sem)

    return wrapped


def _grid_ends(grid):
    first = lambda: functools.reduce(lambda a, b: a & b, [pl.program_id(i) == 0 for i in range(len(grid))])
    last = lambda: functools.reduce(lambda a, b: a & b, [pl.program_id(i) == g - 1 for i, g in enumerate(grid)])
    return first, last


def _call_with_comm(body, grid, name, in_specs, args, out_specs, out_shape, comm, scratch=(), sem=None):
    sem = sem or ("parallel",) * len(grid)
    scratch = list(scratch)
    if comm is not None:
        body = _attach(comm, body, len(args), len(out_shape), *_grid_ends(grid))
        in_specs, args = in_specs + [HBM_SPEC] * len(comm.inputs), args + comm.inputs
        out_specs, out_shape = out_specs + [HBM_SPEC] * len(comm.out_shapes), out_shape + comm.out_shapes
        scratch, sem = scratch + comm.sems, ("arbitrary",) * len(grid)
    return pl.pallas_call(body, grid=grid, name=name, in_specs=in_specs, out_specs=out_specs, out_shape=out_shape,
                          scratch_shapes=scratch, compiler_params=_cp(*sem))(*args)


class ScatterComm:
    def __init__(self, groups):
        self.sizes = [len(g) for g in groups]
        self.rows = [[pc.shape[1] for pc in g] for g in groups]
        ng = len(groups)
        self.inputs = [pc for g in groups for pc in g]
        self.out_shapes = [jax.ShapeDtypeStruct((N_DEV, sum(self.rows[gi]), g[0].shape[2]), g[0].dtype) for gi, g in enumerate(groups)]
        self.sems = [pltpu.SemaphoreType.DMA((7 * ng,)), pltpu.SemaphoreType.DMA((7 * ng,)), pltpu.SemaphoreType.DMA((ng,))]

    def _peers(self):
        x, y, c = _me()
        out = []
        for k in range(1, N_DEV):
            px = 1 - x if k & 4 else x
            py = 1 - y if k & 2 else y
            pc = 1 - c if k & 1 else c
            out.append((k, 4 * px + 2 * py + pc, dict(device_id=(px, py, pc), device_id_type=MESH)))
        return 4 * x + 2 * y + c, out

    def start(self, ins, outs, sems):
        send_sems, recv_sems, local_sems = sems
        me, peers = self._peers()
        pos = 0
        for gi, size in enumerate(self.sizes):
            for i, pc in enumerate(ins[pos:pos + size]):
                dst = outs[gi].at[me, pl.ds(sum(self.rows[gi][:i]), self.rows[gi][i])]
                pltpu.make_async_copy(pc.at[me], dst, local_sems.at[gi]).start()
                for k, peer, kw in peers:
                    pltpu.make_async_remote_copy(src_ref=pc.at[peer], dst_ref=dst, send_sem=send_sems.at[7 * gi + k - 1],
                                                 recv_sem=recv_sems.at[7 * gi + k - 1], **kw).start()
            pos += size

    def finish(self, ins, outs, sems):
        send_sems, recv_sems, local_sems = sems
        me, peers = self._peers()
        whole = [pltpu.make_async_remote_copy(src_ref=outs[gi].at[peer], dst_ref=outs[gi].at[peer],
                                              send_sem=send_sems.at[7 * gi + k - 1], recv_sem=recv_sems.at[7 * gi + k - 1], **kw)
                 for gi in range(len(self.sizes)) for k, peer, kw in peers]
        for cp in whole:
            cp.wait_recv()
        for cp in whole:
            cp.wait_send()
        for gi in range(len(self.sizes)):
            pltpu.make_async_copy(outs[gi].at[me], outs[gi].at[me], local_sems.at[gi]).wait()


def _peer_list():
    x, y, c = _me()
    out = []
    for k in range(1, N_DEV):
        px = 1 - x if k & 4 else x
        py = 1 - y if k & 2 else y
        pc = 1 - c if k & 1 else c
        out.append((k, 4 * px + 2 * py + pc, dict(device_id=(px, py, pc), device_id_type=MESH)))
    return 4 * x + 2 * y + c, out


SEM_SPEC = pl.BlockSpec(memory_space=pltpu.SEMAPHORE)
HBM_ONLY = pl.BlockSpec(memory_space=pltpu.HBM)
N_SPLIT_SEMS = 2 * (N_DEV - 1)


def exchange_start(piece, after):
    def body(piece_ref, land_ref, after_ref, *outs):
        sems, token = outs[:N_SPLIT_SEMS], outs[-1]
        me, peers = _peer_list()
        for k, peer, kw in peers:
            pltpu.make_async_remote_copy(src_ref=piece_ref.at[peer], dst_ref=land_ref.at[me], send_sem=sems[k - 1],
                                         recv_sem=sems[N_DEV - 2 + k], **kw).start()
        token[...] = jnp.zeros_like(token)

    res = pl.pallas_call(
        body, name="exchange_last_start",
        out_shape=(pltpu.SemaphoreType.DMA(()),) * N_SPLIT_SEMS + (pltpu.HBM(piece.shape, piece.dtype), pltpu.HBM(piece.shape, piece.dtype),
                                                                   jax.ShapeDtypeStruct((8, 128), F32)),
        in_specs=(HBM_ONLY, HBM_ONLY, HBM_SPEC),
        out_specs=(SEM_SPEC,) * N_SPLIT_SEMS + (HBM_ONLY, HBM_ONLY, pl.BlockSpec(memory_space=pltpu.VMEM)),
        input_output_aliases={0: N_SPLIT_SEMS, 1: N_SPLIT_SEMS + 1},
        compiler_params=pltpu.CompilerParams(has_side_effects=pltpu.SideEffectType.DATAFLOW_SIDE_EFFECTING),
    )(pltpu.with_memory_space_constraint(piece, pltpu.HBM),
      pltpu.with_memory_space_constraint(lax.empty(piece.shape, piece.dtype), pltpu.HBM), after)
    return res[:N_SPLIT_SEMS], res[N_SPLIT_SEMS], res[N_SPLIT_SEMS + 1], res[N_SPLIT_SEMS + 2]


def exchange_wait(sems, piece, land, after):
    def body(piece_ref, land_ref, *rest):
        sem_refs = rest[:N_SPLIT_SEMS]
        me, peers = _peer_list()
        for k, peer, kw in peers:
            cp = pltpu.make_async_remote_copy(src_ref=piece_ref.at[peer], dst_ref=land_ref.at[peer], send_sem=sem_refs[k - 1],
                                              recv_sem=sem_refs[N_DEV - 2 + k], **kw)
            cp.wait_send()
            cp.wait_recv()

    return pl.pallas_call(
        body, name="exchange_last_wait",
        out_shape=(pltpu.HBM(piece.shape, piece.dtype), pltpu.HBM(land.shape, land.dtype)),
        in_specs=(HBM_ONLY, HBM_ONLY) + (SEM_SPEC,) * N_SPLIT_SEMS + (HBM_SPEC,), out_specs=(HBM_ONLY, HBM_ONLY),
        input_output_aliases={0: 0, 1: 1},
        compiler_params=pltpu.CompilerParams(has_side_effects=pltpu.SideEffectType.DATAFLOW_SIDE_EFFECTING),
    )(piece, land, *sems, after)[1]


def sum_slots(recv, name, tr):
    n, R, C = recv.shape

    def body(r_ref, o_ref):
        acc = r_ref[0].astype(F32)
        for s in range(1, n):
            acc = acc + r_ref[s].astype(F32)
        o_ref[...] = acc

    return pl.pallas_call(
        body, grid=(R // tr,), name=name,
        in_specs=[pl.BlockSpec((n, tr, C), lambda i: (0, i, 0))], out_specs=pl.BlockSpec((tr, C), lambda i: (i, 0)),
        out_shape=jax.ShapeDtypeStruct((R, C), F32), compiler_params=_cp("parallel"),
    )(recv)


PACK_W, FLAT_W = 1024, 128
MAIN = [
    ("ffn1_w_gate", "col"), ("ffn1_w_up", "col"), ("ffn1_w_down", "row"),
    ("ffn2_w_gate", "col"), ("ffn2_w_up", "col"), ("ffn2_w_down", "row"),
    ("w_ssd_proj", "row"), ("w_mla_proj", "row"), ("w_out", "row"),
    ("w_xq", "row"), ("w_xk", "row"), ("w_xv", "row"), ("w_xo", "row"),
    ("w_uk", "col"), ("w_uv", "col"), ("w_in", "col"),
]
FLAT = [("w_uq", "col")]
BIG = MAIN + FLAT
SMALL = ["ffn1_pre_g", "ffn1_post_g", "mix_pre_g", "conv_b", "dt_bias", "a_log", "d_skip", "ssd_norm_g", "q_norm_g",
         "kv_norm_g", "gate_bias", "mix_post_g", "xa_pre_g", "mem_norm_g", "xa_post_g", "ffn2_pre_g", "ffn2_post_g"]
WEIGHTS = ['ffn1_pre_g', 'ffn1_w_gate', 'ffn1_w_up', 'ffn1_w_down', 'ffn1_post_g', 'mix_pre_g', 'w_in', 'conv_w', 'conv_b',
           'dt_bias', 'a_log', 'd_skip', 'ssd_norm_g', 'w_ssd_proj', 'q_norm_g', 'w_uq', 'kv_norm_g', 'w_uk', 'w_uv',
           'w_mla_proj', 'gate_bias', 'w_out', 'mix_post_g', 'xa_pre_g', 'mem_norm_g', 'w_xq', 'w_xk', 'w_xv', 'w_xo',
           'xa_post_g', 'ffn2_pre_g', 'ffn2_w_gate', 'ffn2_w_up', 'ffn2_w_down', 'ffn2_post_g']


def _pack_rows(w, kind, width):
    m = w[0].T if kind == "col" else w[0]
    return m.reshape(-1, width)


KIND = dict(BIG)
GATHER_PLAN = {
    "ffn1_pre": (["ffn1_w_gate", "ffn1_w_up"], []),
    "ffn1_gate_up": (["ffn1_w_down", "w_in@0"], []),
    "ffn1_down": (["w_in@1"], ["conv_w"]),
    "ssd_fwd": (["w_ssd_proj", "w_mla_proj", "w_out", "w_uk", "w_uv"], ["w_uq"]),
    "attn_fwd": (["w_xq", "w_xk", "w_xv", "w_xo", "ffn2_w_gate", "ffn2_w_up", "ffn2_w_down"], []),
}
LAST_EXCHANGE = "last"
SCATTER_PLAN = {
    "attn_bwd": [["ffn2_w_gate", "ffn2_w_up", "ffn2_w_down"], ["w_xq", "w_xk", "w_xv", "w_xo"]],
    "ssd_bwd": [["w_ssd_proj", "w_mla_proj", "w_out"], ["w_uk", "w_uv"], ["w_uq"]],
    "in_bwd": [["w_in#0"]],
    "ffn1:down_bwd": [["w_in#1"]],
    "ffn1:dwd": [["w_in#2"]],
    "ffn1:dwg": [["ffn1_w_down#0"]],
    "ffn1:dwu": [["ffn1_w_down#1"]],
    "ffn1:gate_up_bwd": [["ffn1_w_gate"]],
    "last": [["ffn1_w_up"]],
}
PARTS = {"w_in@0": ("w_in", 0, 336), "w_in@1": ("w_in", 336, 662),
         "w_in#0": ("w_in", 0, 336), "w_in#1": ("w_in", 336, 496), "w_in#2": ("w_in", 496, 662),
         "ffn1_w_down#0": ("ffn1_w_down", 0, 176), "ffn1_w_down#1": ("ffn1_w_down", 176, 352)}


def _parts_of(base, mark):
    return sorted(pn for pn, (b, _, _) in PARTS.items() if b == base and mark in pn)


class Stage:
    def __init__(self, w):
        self.w = w
        self.width = {n: PACK_W if (n, k) in MAIN else FLAT_W for n, k in BIG}
        self.nrows = {n: math.prod(w[n].shape) // self.width[n] for n, _ in BIG}
        self.recv = {}
        self.arrived_parts = {}

    def _rows(self, n):
        return PARTS[n][2] - PARTS[n][1] if n in PARTS else self.nrows[n]

    def _shards(self, tag):
        names_main, names_flat = GATHER_PLAN[tag]

        def pack(n):
            if n == "conv_w":
                return _pad_rows(lax.bitcast_convert_type(self.w[n][0], BF16).reshape(-1, FLAT_W), 16)
            base, r0, r1 = PARTS.get(n, (n, 0, None))
            return _pack_rows(self.w[base], KIND[base], self.width[base])[r0:r1].astype(BF16)
        shards = []
        if names_main:
            pieces = [pack(n) for n in names_main]
            shards.append((jnp.concatenate(pieces, axis=0), [pc.shape[0] for pc in pieces]))
        if names_flat:
            pieces = [pack(n) for n in names_flat]
            shards.append((jnp.concatenate(pieces, axis=0), [pc.shape[0] for pc in pieces]))
        return shards

    def gather(self, tag):
        return GatherComm(self._shards(tag)) if tag in GATHER_PLAN else None

    def gathered(self, tag, outs, W, p):
        if tag not in GATHER_PLAN:
            return
        names_main, names_flat = GATHER_PLAN[tag]
        outs = list(outs)
        for n in names_main + names_flat:
            rows = outs.pop(0)
            if n == "conv_w":
                cw = self.w[n]
                bits = rows[:, :2 * math.prod(cw.shape) // FLAT_W].reshape((N_DEV,) + cw.shape[1:] + (2,))
                p[n] = lax.bitcast_convert_type(bits, F32).transpose(1, 0, 2).reshape(cw.shape[1], -1)
                continue
            if n in PARTS:
                self.arrived_parts[n] = rows
                base = PARTS[n][0]
                mine = _parts_of(base, "@")
                if not all(pn in self.arrived_parts for pn in mine):
                    continue
                n, rows = base, jnp.concatenate([self.arrived_parts[pn] for pn in mine], axis=1)
            K = self.w[n].shape[1] if KIND[n] == "col" else PACK_W
            W[n] = rows.reshape(-1, K)

    def pieces(self, tag, gw):
        def piece(n):
            if n in PARTS:
                base, r0, r1 = PARTS[n]
                return gw[base].reshape(N_DEV, self.nrows[base], self.width[base])[:, r0:r1]
            return gw[n].reshape(N_DEV, self.nrows[n], self.width[n])
        return [[piece(n) for n in names] for names in SCATTER_PLAN[tag]]

    def scatter(self, tag, gw):
        return ScatterComm(self.pieces(tag, gw)) if tag in SCATTER_PLAN else None

    def scattered(self, tag, outs):
        if tag in SCATTER_PLAN:
            self.recv[tag] = outs


def _pad_rows(a, mult):
    r = (-a.shape[0]) % mult
    return a if r == 0 else jnp.concatenate([a, jnp.zeros((r,) + a.shape[1:], a.dtype)], axis=0)


def _pack_small(vals, loss_row=None, conv_w=None):
    rows = []
    for v in vals:
        f = v.reshape(-1)
        f = jnp.concatenate([f, jnp.zeros(((-f.shape[0]) % 128,), F32)])
        rows.append(f.reshape(-1, 128))
    if conv_w is not None:
        rows.append(conv_w.reshape(-1, 128))
    if loss_row is not None:
        rows.append(loss_row)
    return _pad_rows(jnp.concatenate(rows, axis=0), 8)


def _unpack_small(buf, shapes):
    out, r = [], 0
    for shp in shapes:
        n = math.prod(shp)
        nr = -(-n // 128)
        out.append(buf[r:r + nr].reshape(-1)[:n].reshape(shp))
        r += nr
    return out, r


def _tn(a, b, name, out_dtype=BF16, comm=None):
    M, N = a.shape[1], b.shape[1]
    T = a.shape[0]
    tm = M if M <= 1536 else M // 2
    tk = 1024 if T % 1024 == 0 and T > 1024 else None
    res = mm([[(a, b, "tn")]], [out_dtype], name, tm=tm, tn=N, tk=tk, comm=comm)
    return res[0] if comm is None else (res[0], res[1:])


class NoStage:
    def gather(self, tag):
        return None

    def gathered(self, tag, outs, W, p):
        pass

    def scatter(self, tag, gw):
        return None

    def scattered(self, tag, outs):
        pass


def _ffn_fwd(x, gpre, gpost, W, p, tag, stage, target=None):
    comm = stage.gather(tag + "_pre")
    h = rms_fwd(x, gpre, tag + "_pre", comm=comm)
    if comm is not None:
        h, arrived = h
        stage.gathered(tag + "_pre", arrived, W, p)

    def swi(accs, ex):
        sil, dsil = _silu_parts(accs[0])
        return sil, accs[1] * dsil, sil * accs[1]
    G, U, A, *arrived = mm([[(h, W[tag + "_w_gate"], "nt")], [(h, W[tag + "_w_up"], "nt")]], [BF16, BF16, BF16], tag + "_gate_up",
                           tn=DFF // 2, epi=swi, comm=stage.gather(tag + "_gate_up"), sub=4 if h.shape[0] % 1024 == 0 else 1)
    stage.gathered(tag + "_gate_up", arrived, W, p)
    H, y, *rest = mm_resid(A, W[tag + "_w_down"], x, gpost, FFN_RES, tag + "_down", comm=stage.gather(tag + "_down"), target=target)
    saved = (x, h, G, U, A, H)
    if target is not None:
        return y, saved, rest[0]
    stage.gathered(tag + "_down", rest, W, p)
    return y, saved


def _ffn_bwd(dy, saved, gpre, gpost, wg_t, wu_t, wd, tag, stage, gw):
    x, h, G, U, A, H = saved
    dH, dgpost = resid_bwd(H, gpost, dy, FFN_RES, tag + "_post_bwd")

    def dswi(accs, ex):
        return accs[0] * ex[1], accs[0] * ex[0]

    def hosted(where, call):
        comm = stage.scatter(tag + ":" + where, gw)
        res = call(comm)
        if comm is None:
            return res
        stage.scattered(tag + ":" + where, res[1])
        return res[0]

    res = hosted("down_bwd", lambda comm: (lambda r: r if comm is None else (r[:2], r[2:]))(
        mm([[(dH, wd, "nt")]], [BF16, BF16], tag + "_down_bwd", tn=DFF // 2, epi=dswi, extras=[G, U], comm=comm,
           sub=4 if dH.shape[0] % 1024 == 0 else 1)))
    dG, dU = res
    gw[tag + "_w_down"] = hosted("dwd", lambda comm: _tn(A, dH, tag + "_dwd", comm=comm))
    gw[tag + "_w_gate"] = hosted("dwg", lambda comm: _tn(dG, h, tag + "_dwg", comm=comm))
    gw[tag + "_w_up"] = hosted("dwu", lambda comm: _tn(dU, h, tag + "_dwu", comm=comm))
    dx, dgpre = hosted("gate_up_bwd", lambda comm: (lambda r: r[:2] if comm is None else (r[:2], r[2:]))(
        mm_rms_bwd([(dG, wg_t, "nn"), (dU, wu_t, "nn")], x, gpre, tag + "_gate_up_bwd", resid=dy, comm=comm)))
    return dx, dgpre, dgpost


def _local_step(x, mem, positions, tgt, W, p, stage=None):
    stage = stage or NoStage()
    nseq = x.shape[0]
    T = nseq * x.shape[1]
    x0 = x.reshape(T, D)
    mem2 = mem.reshape(-1, D)

    x1, ffn1 = _ffn_fwd(x0, p["ffn1_pre_g"], p["ffn1_post_g"], W, p, "ffn1", stage)

    w_in_t = W["w_in"]
    bounds = [0]
    for n in (SSD_INNER, CONV_CH, SSD_H, QR, KVR, ROPE, 2 * D):
        bounds.append(bounds[-1] + n)
    wt_z, wt_xbc, wt_dt, wt_q, wt_kv, wt_kr, wt_gate = [w_in_t[bounds[i]:bounds[i + 1]] for i in range(7)]
    wt_dt, wt_kr = _pad_rows(wt_dt, SLOT), _pad_rows(wt_kr, SLOT)
    wt_dtkr = jnp.concatenate([wt_dt, wt_kr], axis=0)
    hm = rms_fwd(x1, p["mix_pre_g"], "mix_pre")
    z = mm1(hm, wt_z, "nt", BF16, "in_z")
    xbc = mm1(hm, wt_xbc, "nt", BF16, "in_xbc")
    q_c = mm1(hm, wt_q, "nt", F32, "in_q", tn=QR)
    kv_c = mm1(hm, wt_kv, "nt", F32, "in_kv")
    dtkr = mm1(hm, wt_dtkr, "nt", F32, "in_dtkr")
    gl = mm1(hm, wt_gate, "nt", BF16, "in_gate")

    xbc_act = conv_fwd(xbc, p["conv_w"], p["conv_b"], nseq)
    y_ssd_core, prev, *arrived = ssd_fwd(xbc_act, dtkr, p["dt_bias"], p["a_log"], p["d_skip"], nseq, comm=stage.gather("ssd_fwd"))
    stage.gathered("ssd_fwd", arrived, W, p)
    yn = gated_norm_fwd(y_ssd_core, z, p["ssd_norm_g"], "ssd_norm")
    y_ssd = mm1(yn, W["w_ssd_proj"], "nn", BF16, "ssd_proj")

    slot_rows = lambda wt, per: jnp.pad(wt.reshape(MLA_H, per, -1), ((0, 0), (0, SLOT - per), (0, 0))).reshape(MLA_H * SLOT, -1)
    wq_s, wk_s, wv_s = slot_rows(W["w_uq"], QK), slot_rows(W["w_uk"], NOPE), slot_rows(W["w_uv"], VD)
    wo_s = slot_rows(W["w_mla_proj"], VD)
    qn = rms_fwd(q_c, p["q_norm_g"], "q_norm")
    rope_c, rope_s = rope_table(*_rope_inputs(positions))
    rope_args = [("rows", rope_c), ("rows", rope_s)]
    Qc, = mm([[(qn, wq_s, "nt")]], [BF16], "uq", epi=rope_q_epilogue, extras=rope_args, sub=4 if T % 1024 == 0 else 1)
    kvn = rms_fwd(kv_c, p["kv_norm_g"], "kv_norm")
    Kc, = mm([[(kvn, wk_s, "nt")]], [BF16], "uk", epi=rope_k_epilogue, extras=rope_args + [("rows", dtkr)],
             sub=4 if T % 1024 == 0 else 1)
    v_s = mm1(kvn, wv_s, "nt", BF16, "uv")
    o_s, lse, *arrived = attn_slot_fwd(Qc, Kc, v_s, nseq, comm=stage.gather("attn_fwd"))
    stage.gathered("attn_fwd", arrived, W, p)
    y_mla = mm1(o_s, wo_s, "nn", BF16, "mla_proj")

    merged = merge_fwd(gl, y_ssd, y_mla, p["gate_bias"], "merge")
    hmix, x2 = mm_resid(merged, W["w_out"], x1, p["mix_post_g"], 1.0, "mix_out")

    hq = rms_fwd(x2, p["xa_pre_g"], "xa_pre")
    mn = rms_fwd(mem2, p["mem_norm_g"], "mem_norm")
    xq = mm1(hq, W["w_xq"], "nn", BF16, "xq")
    xk = mm1(mn, W["w_xk"], "nn", BF16, "xk")
    xv = mm1(mn, W["w_xv"], "nn", BF16, "xv")
    xo, *arrived = xattn_fwd(xq, xk, xv, nseq, comm=stage.gather("xattn_fwd"))
    stage.gathered("xattn_fwd", arrived, W, p)
    ho, x3 = mm_resid(xo, W["w_xo"], x2, p["xa_post_g"], 1.0, "xo")

    dx4, ffn2, sq_cols = _ffn_fwd(x3, p["ffn2_pre_g"], p["ffn2_post_g"], W, p, "ffn2", stage, target=tgt.reshape(T, D))
    loss_row = (0.5 / D) * jnp.sum(sq_cols.reshape(-1, 128), axis=0, keepdims=True)

    gw, gs = {}, {}
    dx3, gs["ffn2_pre_g"], gs["ffn2_post_g"] = _ffn_bwd(
        dx4, ffn2, p["ffn2_pre_g"], p["ffn2_post_g"], W["ffn2_w_gate"], W["ffn2_w_up"], W["ffn2_w_down"], "ffn2", stage, gw)

    dho, gs["xa_post_g"] = resid_bwd(ho, p["xa_post_g"], dx3, 1.0, "xa_post_bwd")
    dxo = mm1(dho, W["w_xo"], "nt", BF16, "xo_bwd")
    gw["w_xo"] = _tn(xo, dho, "d_w_xo")
    dxq, dxk, dxv = xattn_bwd(xq, xk, xv, dxo, nseq)
    dx2, gs["xa_pre_g"] = mm_rms_bwd([(dxq, W["w_xq"], "nt")], x2, p["xa_pre_g"], "xq_bwd", resid=dx3)
    gw["w_xq"] = _tn(hq, dxq, "d_w_xq")
    dmn = mm([[(dxk, W["w_xk"], "nt"), (dxv, W["w_xv"], "nt")]], [F32], "xkv_bwd")[0]
    gw["w_xk"] = _tn(mn, dxk, "d_w_xk")
    gw["w_xv"] = _tn(mn, dxv, "d_w_xv")
    _, gs["mem_norm_g"] = rms_bwd(mem2, p["mem_norm_g"], dmn, "mem_norm_bwd", dx_dtype=BF16)

    dhmix, gs["mix_post_g"] = resid_bwd(hmix, p["mix_post_g"], dx2, 1.0, "mix_post_bwd")
    dmerged = mm1(dhmix, W["w_out"], "nt", F32, "mix_out_bwd")
    gw["w_out"] = _tn(merged, dhmix, "d_w_out")
    dys, dym, dgl, gs["gate_bias"] = merge_bwd(gl, y_ssd, y_mla, dmerged, p["gate_bias"], "merge_bwd")

    unslot = lambda g, per: g.reshape(MLA_H, SLOT, -1)[:, :per].reshape(MLA_H * per, -1)
    do_s = mm1(dym, wo_s, "nt", BF16, "mla_proj_bwd")
    gw["w_mla_proj"] = unslot(_tn(o_s, dym, "d_w_mla_proj"), VD)
    dQc, dKc, dv_s, *sent = attn_slot_bwd(Qc, Kc, v_s, o_s, lse, do_s, nseq, comm=stage.scatter("attn_bwd", gw))
    stage.scattered("attn_bwd", sent)
    dq_s, dkn_s, dkr = rope_slot_bwd(dQc, dKc, rope_c, rope_s, "rope_bwd")
    dq_c, gs["q_norm_g"] = mm_rms_bwd([(dq_s, wq_s, "nn")], q_c, p["q_norm_g"], "uq_bwd", dx_dtype=BF16)
    gw["w_uq"] = unslot(_tn(dq_s, qn, "d_w_uq"), QK)
    dkv_c, gs["kv_norm_g"] = mm_rms_bwd([(dkn_s, wk_s, "nn"), (dv_s, wv_s, "nn")], kv_c, p["kv_norm_g"], "ukv_bwd", dx_dtype=BF16)
    gw["w_uk"] = unslot(_tn(dkn_s, kvn, "d_w_uk"), NOPE)
    gw["w_uv"] = unslot(_tn(dv_s, kvn, "d_w_uv"), VD)

    dyn = mm1(dys, W["w_ssd_proj"], "nt", F32, "ssd_proj_bwd")
    gw["w_ssd_proj"] = _tn(yn, dys, "d_w_ssd_proj")
    dyc, dz, gs["ssd_norm_g"] = gated_norm_bwd(y_ssd_core, z, dyn, p["ssd_norm_g"], "ssd_norm_bwd")
    dxbc_act, ddtr, gs["dt_bias"], gs["a_log"], gs["d_skip"], *sent = ssd_bwd(
        xbc_act, dtkr, p["dt_bias"], p["a_log"], p["d_skip"], prev, dyc, nseq, comm=stage.scatter("ssd_bwd", gw))
    stage.scattered("ssd_bwd", sent)
    dxbc, gs["conv_w"], gs["conv_b"] = conv_bwd(xbc, p["conv_w"], p["conv_b"], dxbc_act, nseq)

    gw["w_in"] = jnp.concatenate([_tn(dz, hm, "d_w_in_z"), _tn(dxbc, hm, "d_w_in_xbc"), _tn(ddtr, hm, "d_w_in_dt")[:SSD_H],
                                  _tn(dq_c, hm, "d_w_in_q"), _tn(dkv_c, hm, "d_w_in_kv"), _tn(dkr, hm, "d_w_in_kr")[:ROPE],
                                  _tn(dgl, hm, "d_w_in_gate")], axis=0)
    dx1, gs["mix_pre_g"], *sent = mm_rms_bwd(
        [(dz, wt_z, "nn"), (dxbc, wt_xbc, "nn"), (ddtr, wt_dt, "nn"), (dq_c, wt_q, "nn"), (dkv_c, wt_kv, "nn"),
         (dkr, wt_kr, "nn"), (dgl, wt_gate, "nn")], x1, p["mix_pre_g"], "in_bwd", resid=dx2, comm=stage.scatter("in_bwd", gw))
    stage.scattered("in_bwd", sent)

    dx0, gs["ffn1_pre_g"], gs["ffn1_post_g"] = _ffn_bwd(
        dx1, ffn1, p["ffn1_pre_g"], p["ffn1_post_g"], W["ffn1_w_gate"], W["ffn1_w_up"], W["ffn1_w_down"], "ffn1", stage, gw)
    return loss_row, dx0.reshape(x.shape), gw, gs


def kernel(x, mem, positions, ffn1_pre_g, ffn1_w_gate, ffn1_w_up, ffn1_w_down, ffn1_post_g, mix_pre_g, w_in, conv_w, conv_b, dt_bias, a_log, d_skip, ssd_norm_g, w_ssd_proj, q_norm_g, w_uq, kv_norm_g, w_uk, w_uv, w_mla_proj, gate_bias, w_out, mix_post_g, xa_pre_g, mem_norm_g, w_xq, w_xk, w_xv, w_xo, xa_post_g, ffn2_pre_g, ffn2_w_gate, ffn2_w_up, ffn2_w_down, ffn2_post_g, loss_target, m_ffn1_pre_g, m_ffn1_w_gate, m_ffn1_w_up, m_ffn1_w_down, m_ffn1_post_g, m_mix_pre_g, m_w_in, m_conv_w, m_conv_b, m_dt_bias, m_a_log, m_d_skip, m_ssd_norm_g, m_w_ssd_proj, m_q_norm_g, m_w_uq, m_kv_norm_g, m_w_uk, m_w_uv, m_w_mla_proj, m_gate_bias, m_w_out, m_mix_post_g, m_xa_pre_g, m_mem_norm_g, m_w_xq, m_w_xk, m_w_xv, m_w_xo, m_xa_post_g, m_ffn2_pre_g, m_ffn2_w_gate, m_ffn2_w_up, m_ffn2_w_down, m_ffn2_post_g, v_ffn1_pre_g, v_ffn1_w_gate, v_ffn1_w_up, v_ffn1_w_down, v_ffn1_post_g, v_mix_pre_g, v_w_in, v_conv_w, v_conv_b, v_dt_bias, v_a_log, v_d_skip, v_ssd_norm_g, v_w_ssd_proj, v_q_norm_g, v_w_uq, v_kv_norm_g, v_w_uk, v_w_uv, v_w_mla_proj, v_gate_bias, v_w_out, v_mix_post_g, v_xa_pre_g, v_mem_norm_g, v_w_xq, v_w_xk, v_w_xv, v_w_xo, v_xa_post_g, v_ffn2_pre_g, v_ffn2_w_gate, v_ffn2_w_up, v_ffn2_w_down, v_ffn2_post_g):
    a = dict(locals())
    w = {n: a[n] for n in WEIGHTS}
    m = {n: a["m_" + n] for n in WEIGHTS}
    v = {n: a["v_" + n] for n in WEIGHTS}

    stage = Stage(w)
    W, p = {}, {n: w[n] for n in SMALL}
    loss_row, grad_x, gw, gs = _local_step(x, mem, positions, loss_target, W, p, stage)

    sm = _pack_small([gs[n] for n in SMALL], loss_row=loss_row, conv_w=gs["conv_w"])
    srecv, = run_comm(ScatterComm([[jnp.broadcast_to(sm[None], (N_DEV,) + sm.shape)]]), "exchange_small")
    s_rows = sum_slots(srecv, "sum_small", tr=sm.shape[0])
    last_piece, = stage.pieces(LAST_EXCHANGE, gw)[0]
    sems, last_piece, landed, token = exchange_start(last_piece, s_rows)
    grads, delta, new_m, new_v = {}, {}, {}, {}
    raw_results = []

    def finish(n, buf, piece, token=None):
        col = KIND[n] == "col"
        turn = (lambda t: t.T) if col else (lambda t: t)
        K = w[n].shape[1]
        if col and buf.shape[2] != K:
            buf = buf.reshape(buf.shape[0], -1, K)
        res = adamw_from_slots(buf, piece, turn(w[n][0]), turn(m[n][0]), turn(v[n][0]), "adamw_" + n, token=token)
        raw_results.append(res[3])
        grads[n], delta[n], new_m[n], new_v[n] = [turn(r)[None] for r in res]

    parts = {}
    for tag, groups in SCATTER_PLAN.items():
        if tag == LAST_EXCHANGE:
            continue
        for names, buf in zip(groups, stage.recv[tag]):
            for piece, n in enumerate(names):
                if n in PARTS:
                    parts[n] = sum_slots(buf, "sum_" + n.replace("#", "_"), tr=buf.shape[1])
                else:
                    finish(n, buf, piece, token)
    for base in sorted({PARTS[pn][0] for pn in parts}):
        rows = jnp.concatenate([parts[pn] for pn in _parts_of(base, "#")], axis=0)
        finish(base, rows[None], 0, token)
    landed = exchange_wait(sems, last_piece, landed, after=raw_results[-1])
    me = _dev_index()
    landed = lax.dynamic_update_index_in_dim(landed, lax.dynamic_index_in_dim(last_piece, me, 0, keepdims=False), me, 0)
    finish(SCATTER_PLAN[LAST_EXCHANGE][0][0], landed, 0)
    conv_w_full = p["conv_w"]
    small = adamw_small(s_rows, [w[n] for n in SMALL], [m[n] for n in SMALL], [v[n] for n in SMALL])
    for t, vals in zip((grads, delta, new_m, new_v), small):
        t.update(zip(SMALL, vals))
    r1 = sum(-(-w[n].shape[1] // 128) for n in SMALL)
    ncw = math.prod(conv_w_full.shape) // 128
    cw_grad_full = s_rows[r1:r1 + ncw].reshape(conv_w_full.shape)
    wsh = conv_w.shape[2]
    grads["conv_w"] = lax.dynamic_slice_in_dim(cw_grad_full, _dev_index() * wsh, wsh, axis=1)[None]
    loss = jnp.sum(s_rows[r1 + ncw])
    d_, m_, v_ = adamw(conv_w[0], grads["conv_w"][0], m["conv_w"][0], v["conv_w"][0], "adamw_conv_w")
    delta["conv_w"], new_m["conv_w"], new_v["conv_w"] = d_[None], m_[None], v_[None]
    return (loss, grad_x, *[grads[n] for n in WEIGHTS], *[delta[n] for n in WEIGHTS],
            *[new_m[n] for n in WEIGHTS], *[new_v[n] for n in WEIGHTS])
```

```python
import functools
import math

import jax
import jax.numpy as jnp
from jax import lax
from jax.experimental import pallas as pl
from jax.experimental.pallas import tpu as pltpu

F32, BF16 = jnp.float32, jnp.bfloat16
HI = lax.Precision.HIGHEST
MESH = pl.DeviceIdType.MESH
N_DEV = 8

D = 1024
DFF = 2816
SSD_H, SSD_P, SSD_G, SSD_N, SSD_L = 16, 64, 2, 128, 128
SSD_INNER = SSD_H * SSD_P
CONV_K, CONV_CH = 4, 1536
MLA_H, QR, KVR, NOPE, ROPE, VD = 16, 384, 256, 64, 32, 64
QK = NOPE + ROPE
ROPE_THETA = 10000.0
XA_H, XA_D = 4, 256
EPS = 1e-6
FFN_RES = 0.5
LR, B1, B2, AEPS, WD, STEP = 0.001, 0.9, 0.999, 1e-08, 0.01, 10

VMEM_LIMIT = 56 * 2**20


def _cp(*sem):
    return pltpu.CompilerParams(dimension_semantics=sem, vmem_limit_bytes=VMEM_LIMIT)


def _sigmoid(x):
    return 1.0 / (1.0 + jnp.exp(-x))


def _softplus(x):
    return jnp.where(x > 20.0, x, jnp.log(1.0 + jnp.exp(jnp.minimum(x, 20.0))))


def _dot(a, b, dims="nn"):
    ca = 0 if dims[0] == "t" else 1
    cb = 1 if dims[1] == "t" else 0
    return lax.dot_general(a.astype(BF16), b.astype(BF16), (((ca,), (cb,)), ((), ())), preferred_element_type=F32)


def _dot_sel(a, b, dims="nn", split="a", terms=3):
    r = (a if split == "a" else b).astype(F32)
    out = None
    for t in range(terms):
        piece = r.astype(BF16)
        if t + 1 < terms:
            r = r - piece.astype(F32)
        d = _dot(piece, b, dims) if split == "a" else _dot(a, piece, dims)
        out = d if out is None else out + d
    return out


def _ssd_common(dtr, dtb, alog):
    L = dtr.shape[0]
    dt = _softplus(dtr + dtb)
    a = -jnp.exp(alog)
    adt = dt * a
    r = lax.broadcasted_iota(jnp.int32, (L, L), 0)
    c = lax.broadcasted_iota(jnp.int32, (L, L), 1)
    lower = r >= c
    tri = lower.astype(F32)
    cs = _dot_sel(tri, adt, "nn", split="b")
    cs_t = _dot_sel(adt, tri, "tt")
    return dt, a, cs, cs_t, lower


def _head_expand():
    hh = lax.broadcasted_iota(jnp.int32, (SSD_H, SSD_INNER), 0)
    jj = lax.broadcasted_iota(jnp.int32, (SSD_H, SSD_INNER), 1)
    return ((jj >= hh * SSD_P) & (jj < hh * SSD_P + SSD_P)).astype(F32)


def _head_reduce():
    hh = lax.broadcasted_iota(jnp.int32, (SSD_INNER, SSD_H), 1)
    jj = lax.broadcasted_iota(jnp.int32, (SSD_INNER, SSD_H), 0)
    return ((jj >= hh * SSD_P) & (jj < hh * SSD_P + SSD_P)).astype(F32)


def ssd_fwd(xbc, dtr, dtb, alog, dsk, nseq, comm=None):
    T = xbc.shape[0]
    S = T // nseq
    C = S // SSD_L
    L = SSD_L
    NP = SSD_H // 2

    def body(x_ref, b_ref, c_ref, dtr_ref, dtb_ref, alog_ref, dsk_ref, y_ref, prev_ref, st_ref):
        ci = pl.program_id(1)

        @pl.when(ci == 0)
        def _():
            st_ref[...] = jnp.zeros_like(st_ref)

        dt, a, cs, cs_t, lower = _ssd_common(dtr_ref[:, 0:SSD_H], dtb_ref[...], alog_ref[...])
        E = _head_expand()
        X = x_ref[...].astype(F32)
        dt_e = _dot_sel(dt, E)
        cs_e = _dot_sel(cs, E)
        csl_e = cs_e[L - 1:L, :]
        Xd = X * dt_e
        Xf = Xd * jnp.exp(csl_e - cs_e)
        e_e = jnp.exp(cs_e)
        skip = _dot_sel(dsk_ref[...], E) * X
        lane = lax.broadcasted_iota(jnp.int32, (1, 2 * SSD_P), 1)
        rowp = lax.broadcasted_iota(jnp.int32, (2 * SSD_P, 1), 0)
        for g in range(SSD_G):
            Bg = b_ref[:, g * SSD_N:(g + 1) * SSD_N]
            Cg = c_ref[:, g * SSD_N:(g + 1) * SSD_N]
            cb = _dot(Cg, Bg, "nt")
            for pp in range(NP // SSD_G):
                p = g * (NP // SSD_G) + pp
                sl = slice(p * 2 * SSD_P, (p + 1) * 2 * SSD_P)
                Xd_p = Xd[:, sl]
                yd = jnp.zeros((L, 2 * SSD_P), F32)
                for q in range(2):
                    h = 2 * p + q
                    m = jnp.where(lower, jnp.exp(jnp.minimum(cs[:, h:h + 1] - cs_t[h:h + 1, :], 0.0)), 0.0)
                    mask = (lane >= q * SSD_P) & (lane < (q + 1) * SSD_P)
                    yd = yd + _dot(cb * m, jnp.where(mask, Xd_p, 0.0))
                S0 = st_ref[p]
                prev_ref[0, 0, p] = S0
                z = _dot(Cg, S0, "nt")
                y_ref[:, sl] = (skip[:, sl] + yd + z * e_e[:, sl]).astype(y_ref.dtype)
                h0 = 2 * p
                dec = jnp.where(rowp < SSD_P, jnp.exp(cs[L - 1:L, h0:h0 + 1]), jnp.exp(cs[L - 1:L, h0 + 1:h0 + 2]))
                st_ref[p] = S0 * dec + _dot(Xf[:, sl], Bg, "tn")

    row = lambda b, c: (b * C + c, 0)
    small = pl.BlockSpec((1, SSD_H), lambda b, c: (0, 0))
    return _call_with_comm(
        body, (nseq, C), "ssd_fwd",
        [pl.BlockSpec((L, SSD_INNER), row),
         pl.BlockSpec((L, SSD_G * SSD_N), lambda b, c: (b * C + c, SSD_INNER // (SSD_G * SSD_N))),
         pl.BlockSpec((L, SSD_G * SSD_N), lambda b, c: (b * C + c, SSD_INNER // (SSD_G * SSD_N) + 1)),
         pl.BlockSpec((L, 128), row), small, small, small],
        [xbc, xbc, xbc, dtr, dtb, alog, dsk],
        [pl.BlockSpec((L, SSD_INNER), row), pl.BlockSpec((1, 1, NP, 2 * SSD_P, SSD_N), lambda b, c: (b, c, 0, 0, 0))],
        [jax.ShapeDtypeStruct((T, SSD_INNER), BF16), jax.ShapeDtypeStruct((nseq, C, NP, 2 * SSD_P, SSD_N), F32)],
        comm, scratch=[pltpu.VMEM((NP, 2 * SSD_P, SSD_N), F32)], sem=("parallel", "arbitrary"))


def ssd_bwd(xbc, dtr, dtb, alog, dsk, prev, dy, nseq, comm=None):
    T = xbc.shape[0]
    S = T // nseq
    C = S // SSD_L
    L = SSD_L
    NP = SSD_H // 2

    def body(x_ref, b_ref, c_ref, dtr_ref, dtb_ref, alog_ref, dsk_ref, prev_ref, dy_ref,
             dxbc_ref, ddtr_ref, ddtb_ref, dalog_ref, ddsk_ref, ds_ref, stg_ref):
        bi = pl.program_id(0)
        ci = pl.program_id(1)

        @pl.when(ci == 0)
        def _():
            ds_ref[...] = jnp.zeros_like(ds_ref)

        @pl.when((ci == 0) & (bi == 0))
        def _():
            ddtb_ref[...] = jnp.zeros_like(ddtb_ref)
            dalog_ref[...] = jnp.zeros_like(dalog_ref)
            ddsk_ref[...] = jnp.zeros_like(ddsk_ref)

        dtr = dtr_ref[:, 0:SSD_H]
        dtb = dtb_ref[...]
        dt, a, cs, cs_t, lower = _ssd_common(dtr, dtb, alog_ref[...])
        upper = lax.broadcasted_iota(jnp.int32, (L, L), 1) >= lax.broadcasted_iota(jnp.int32, (L, L), 0)
        E = _head_expand()
        ET = _head_reduce()
        X = x_ref[...].astype(F32)
        dY = dy_ref[...].astype(F32)
        dt_e = _dot_sel(dt, E)
        cs_e = _dot_sel(cs, E)
        csl_e = cs_e[L - 1:L, :]
        f_e = jnp.exp(csl_e - cs_e)
        e_e = jnp.exp(cs_e)
        dsk_e = _dot_sel(dsk_ref[...], E)
        Xd = X * dt_e
        Xf = Xd * f_e
        lane = lax.broadcasted_iota(jnp.int32, (1, 2 * SSD_P), 1)
        rowp = lax.broadcasted_iota(jnp.int32, (2 * SSD_P, 1), 0)
        hsel = lax.broadcasted_iota(jnp.int32, (1, SSD_H), 1)
        dcs = jnp.zeros((L, SSD_H), F32)
        dcsl = jnp.zeros((1, SSD_H), F32)
        for g in range(SSD_G):
            Bg = b_ref[:, g * SSD_N:(g + 1) * SSD_N]
            Cg = c_ref[:, g * SSD_N:(g + 1) * SSD_N]
            cb = _dot(Cg, Bg, "nt")
            cbt = _dot(Bg, Cg, "nt")
            dB = jnp.zeros((L, SSD_N), F32)
            dC = jnp.zeros((L, SSD_N), F32)
            for pp in range(NP // SSD_G):
                p = g * (NP // SSD_G) + pp
                sl = slice(p * 2 * SSD_P, (p + 1) * 2 * SSD_P)
                Xd_p = Xd[:, sl]
                dY_p = dY[:, sl]
                dXd_p = jnp.zeros((L, 2 * SSD_P), F32)
                for q in range(2):
                    h = 2 * p + q
                    mask = (lane >= q * SSD_P) & (lane < (q + 1) * SSD_P)
                    col = cs[:, h:h + 1]
                    rw = cs_t[h:h + 1, :]
                    m = jnp.where(lower, jnp.exp(jnp.minimum(col - rw, 0.0)), 0.0)
                    mt = jnp.where(upper, jnp.exp(jnp.minimum(rw - col, 0.0)), 0.0)
                    dYm = jnp.where(mask, dY_p, 0.0)
                    dW = _dot(dYm, Xd_p, "nt")
                    dWt = _dot(Xd_p, dYm, "nt")
                    w = cb * m
                    wt = cbt * mt
                    dC = dC + _dot(dW * m, Bg)
                    dB = dB + _dot(dWt * mt, Cg)
                    dXd_p = dXd_p + jnp.where(mask, _dot(wt, dY_p), 0.0)
                    qcol = jnp.sum(dW * w, axis=1, keepdims=True) - jnp.sum(dWt * wt, axis=1, keepdims=True)
                    dcs = dcs + qcol * (hsel == h).astype(F32)
                S0 = prev_ref[0, 0, p]
                dSn = ds_ref[p]
                dZ = dY_p * e_e[:, sl]
                dC = dC + _dot(dZ, S0)
                h0 = 2 * p
                el0 = jnp.exp(cs[L - 1:L, h0:h0 + 1])
                el1 = jnp.exp(cs[L - 1:L, h0 + 1:h0 + 2])
                dec = jnp.where(rowp < SSD_P, el0, el1)
                ds_ref[p] = dSn * dec + _dot(dZ, Cg, "tn")
                dXf_p = _dot(Bg, dSn, "nt")
                dB = dB + _dot(Xf[:, sl], dSn)
                rs = jnp.sum(dSn * S0, axis=1, keepdims=True)
                s0 = jnp.sum(jnp.where(rowp < SSD_P, rs, 0.0), axis=0, keepdims=True) * el0
                s1 = jnp.sum(jnp.where(rowp >= SSD_P, rs, 0.0), axis=0, keepdims=True) * el1
                dcsl = dcsl + s0 * (hsel == h0).astype(F32) + s1 * (hsel == h0 + 1).astype(F32)
                y_off = _dot(Cg, S0, "nt") * e_e[:, sl]
                t1 = dY_p * y_off - dXf_p * Xf[:, sl]
                r1 = jnp.where(lane < SSD_P, t1, 0.0)
                c0 = jnp.sum(r1, axis=1, keepdims=True)
                c1 = jnp.sum(t1 - r1, axis=1, keepdims=True)
                dcs = dcs + c0 * (hsel == h0).astype(F32) + c1 * (hsel == h0 + 1).astype(F32)
                t2 = dXf_p * Xf[:, sl]
                r2 = jnp.where(lane < SSD_P, t2, 0.0)
                dcsl = dcsl + jnp.sum(r2, keepdims=True) * (hsel == h0).astype(F32) \
                    + jnp.sum(t2 - r2, keepdims=True) * (hsel == h0 + 1).astype(F32)
                stg_ref[:, sl] = dXd_p + dXf_p * f_e[:, sl]
            dxbc_ref[:, SSD_INNER + g * SSD_N:SSD_INNER + (g + 1) * SSD_N] = dB.astype(dxbc_ref.dtype)
            dxbc_ref[:, SSD_INNER + (SSD_G + g) * SSD_N:SSD_INNER + (SSD_G + g + 1) * SSD_N] = dC.astype(dxbc_ref.dtype)
        dXd = stg_ref[...]
        dxbc_ref[:, 0:SSD_INNER] = (dXd * dt_e + dsk_e * dY).astype(dxbc_ref.dtype)
        rowl = lax.broadcasted_iota(jnp.int32, (L, 1), 0)
        dcs = dcs + jnp.where(rowl == L - 1, dcsl, 0.0)
        dalpha = _dot_sel(upper.astype(F32), dcs, split="b")
        ddt = _dot_sel(dXd * X, ET, terms=2) + dalpha * a
        dalog_ref[...] += jnp.sum(dalpha * dt, axis=0, keepdims=True) * a
        ddtr = ddt * _sigmoid(dtr + dtb)
        spread = (lax.broadcasted_iota(jnp.int32, (SSD_H, 128), 0) == lax.broadcasted_iota(jnp.int32, (SSD_H, 128), 1)).astype(F32)
        ddtr_ref[...] = _dot(ddtr, spread).astype(ddtr_ref.dtype)
        ddtb_ref[...] += jnp.sum(ddtr, axis=0, keepdims=True)
        ddsk_ref[...] += jnp.sum(_dot_sel(dY * X, ET, terms=2), axis=0, keepdims=True)

    rowr = lambda b, c: (b * C + (C - 1 - c), 0)
    small = pl.BlockSpec((1, SSD_H), lambda b, c: (0, 0))
    return _call_with_comm(
        body, (nseq, C), "ssd_bwd",
        [pl.BlockSpec((L, SSD_INNER), rowr),
         pl.BlockSpec((L, SSD_G * SSD_N), lambda b, c: (b * C + (C - 1 - c), SSD_INNER // (SSD_G * SSD_N))),
         pl.BlockSpec((L, SSD_G * SSD_N), lambda b, c: (b * C + (C - 1 - c), SSD_INNER // (SSD_G * SSD_N) + 1)),
         pl.BlockSpec((L, 128), rowr), small, small, small,
         pl.BlockSpec((1, 1, NP, 2 * SSD_P, SSD_N), lambda b, c: (b, C - 1 - c, 0, 0, 0)),
         pl.BlockSpec((L, SSD_INNER), rowr)],
        [xbc, xbc, xbc, dtr, dtb, alog, dsk, prev, dy],
        [pl.BlockSpec((L, CONV_CH), rowr), pl.BlockSpec((L, 128), rowr), small, small, small],
        [jax.ShapeDtypeStruct((T, CONV_CH), BF16), jax.ShapeDtypeStruct((T, 128), BF16),
         jax.ShapeDtypeStruct((1, SSD_H), F32), jax.ShapeDtypeStruct((1, SSD_H), F32), jax.ShapeDtypeStruct((1, SSD_H), F32)],
        comm, scratch=[pltpu.VMEM((NP, 2 * SSD_P, SSD_N), F32), pltpu.VMEM((L, SSD_INNER), F32)], sem=("arbitrary", "arbitrary"))


SLOT = 128
ATT_T = 512
ATT_HP = 1
LOG2E = math.log2(math.e)
Q_SCALE = QK ** -0.5 * LOG2E


def _col_to_row(col):
    n = col.shape[0]
    eye = lax.broadcasted_iota(jnp.int32, (n, n), 0) == lax.broadcasted_iota(jnp.int32, (n, n), 1)
    return jnp.sum(jnp.where(eye, col, 0.0), axis=0, keepdims=True)


def attn_slot_fwd(q, k, v, nseq, comm=None):
    T = q.shape[0]
    S = T // nseq
    t = min(ATT_T, S)
    nb = S // t
    cols = [slice(h * SLOT, (h + 1) * SLOT) for h in range(ATT_HP)]

    def body(q_ref, k_ref, v_ref, o_ref, lse_ref):
        causal = lax.broadcasted_iota(jnp.int32, (t, t), 1) <= lax.broadcasted_iota(jnp.int32, (t, t), 0)
        for qi in range(nb):
            rows = slice(qi * t, (qi + 1) * t)
            state = [None] * ATT_HP
            for kj in range(qi + 1):
                keys = slice(kj * t, (kj + 1) * t)
                for h, c in enumerate(cols):
                    s = _dot(q_ref[rows, c], k_ref[keys, c], "nt")
                    if kj == qi:
                        s = jnp.where(causal, s, -1e30)
                    bm = jnp.max(s, axis=1, keepdims=True)
                    if kj == 0:
                        p = jnp.exp2(s - bm)
                        state[h] = (bm, jnp.sum(p, axis=1, keepdims=True), _dot(p, v_ref[keys, c]))
                    else:
                        m, l, acc = state[h]
                        m_new = jnp.maximum(m, bm)
                        corr = jnp.exp2(m - m_new)
                        p = jnp.exp2(s - m_new)
                        state[h] = (m_new, l * corr + jnp.sum(p, axis=1, keepdims=True), acc * corr + _dot(p, v_ref[keys, c]))
            for h, c in enumerate(cols):
                m, l, acc = state[h]
                o_ref[rows, c] = (acc / l).astype(o_ref.dtype)
                lse_ref[0, h, :, rows] = _col_to_row(m + jnp.log2(l))

    blk = pl.BlockSpec((S, ATT_HP * SLOT), lambda b, h: (b, h))
    return _call_with_comm(
        body, (nseq, MLA_H // ATT_HP), "attn_fwd", [blk, blk, blk], [q, k, v],
        [blk, pl.BlockSpec((1, ATT_HP, 1, S), lambda b, h: (b, h, 0, 0))],
        [jax.ShapeDtypeStruct((T, MLA_H * SLOT), BF16), jax.ShapeDtypeStruct((nseq, MLA_H, 1, S), F32)], comm)


def attn_slot_bwd(q, k, v, o, lse, do, nseq, comm=None):
    T = q.shape[0]
    S = T // nseq
    t = min(ATT_T, S)
    nb = S // t
    scale = QK ** -0.5
    cols = [slice(h * SLOT, (h + 1) * SLOT) for h in range(ATT_HP)]

    def body(q_ref, k_ref, v_ref, o_ref, lse_ref, do_ref, dq_ref, dk_ref, dv_ref, dqa_ref):
        causal_t = lax.broadcasted_iota(jnp.int32, (t, t), 0) <= lax.broadcasted_iota(jnp.int32, (t, t), 1)
        ones = jnp.ones((8, SLOT), F32)
        delta = {}
        for qi in range(nb):
            sl = slice(qi * t, (qi + 1) * t)
            for h, c in enumerate(cols):
                prod = do_ref[sl, c].astype(F32) * o_ref[sl, c].astype(F32)
                delta[h, qi] = _dot_sel(ones, prod, "nt", split="b", terms=2)[0:1, :]
        for kj in range(nb):
            ks = slice(kj * t, (kj + 1) * t)
            dk = [None] * ATT_HP
            dv = [None] * ATT_HP
            for qi in range(kj, nb):
                sl = slice(qi * t, (qi + 1) * t)
                for h, c in enumerate(cols):
                    kb, vb, qb, dob = k_ref[ks, c], v_ref[ks, c], q_ref[sl, c], do_ref[sl, c]
                    st = _dot(kb, qb, "nt")
                    pt = jnp.exp2(st - lse_ref[0, h, :, sl])
                    if qi == kj:
                        pt = jnp.where(causal_t, pt, 0.0)
                    dpt = _dot(vb, dob, "nt")
                    dst = (pt * (dpt - delta[h, qi])).astype(BF16)
                    dvc = _dot(pt, dob)
                    dkc = _dot(dst, qb) * (1.0 / LOG2E)
                    dv[h] = dvc if dv[h] is None else dv[h] + dvc
                    dk[h] = dkc if dk[h] is None else dk[h] + dkc
                    dqc = _dot(dst, kb, "tn") * scale
                    if kj > 0:
                        dqc = dqc + dqa_ref[sl, c]
                    if qi == kj:
                        dq_ref[sl, c] = dqc.astype(dq_ref.dtype)
                    else:
                        dqa_ref[sl, c] = dqc
            for h, c in enumerate(cols):
                dk_ref[ks, c] = dk[h].astype(dk_ref.dtype)
                dv_ref[ks, c] = dv[h].astype(dv_ref.dtype)

    blk = pl.BlockSpec((S, ATT_HP * SLOT), lambda b, h: (b, h))
    lse_spec = pl.BlockSpec((1, ATT_HP, 1, S), lambda b, h: (b, h, 0, 0))
    W = MLA_H * SLOT
    return _call_with_comm(
        body, (nseq, MLA_H // ATT_HP), "attn_bwd", [blk, blk, blk, blk, lse_spec, blk], [q, k, v, o, lse, do], [blk, blk, blk],
        [jax.ShapeDtypeStruct((T, W), BF16)] * 3, comm, scratch=[pltpu.VMEM((S, ATT_HP * SLOT), F32)])


def _rope_coeffs(pos, inv):
    half = ROPE // 2
    ang = pos * inv
    lane = lax.broadcasted_iota(jnp.int32, (1, SLOT), 1)
    sn = jnp.sin(ang)
    C = jnp.where(lane < NOPE, 1.0, jnp.where(lane < QK, jnp.cos(ang), 0.0))
    Sg = jnp.where((lane >= NOPE) & (lane < NOPE + half), -sn, jnp.where((lane >= NOPE + half) & (lane < QK), sn, 0.0))
    return C, Sg


def _rope_inputs(positions):
    half = ROPE // 2
    inv = ROPE_THETA ** (-jnp.arange(0, ROPE, 2, dtype=F32) / ROPE)
    row = jnp.zeros((1, SLOT), F32).at[0, NOPE:NOPE + half].set(inv).at[0, NOPE + half:QK].set(inv)
    return positions.astype(F32).reshape(-1, 1), row


def _place_k_rope(kr_lanes):
    r = lax.broadcasted_iota(jnp.int32, (SLOT, SLOT), 0)
    c = lax.broadcasted_iota(jnp.int32, (SLOT, SLOT), 1)
    return _dot_sel(kr_lanes, ((c == r + NOPE) & (r < ROPE)).astype(F32))


def rope_table(pos, inv):
    return rowwise(_rope_coeffs, [pos], [inv], [(SLOT, F32), (SLOT, F32)], [], "rope_table")


def rope_q_epilogue(accs, ex):
    C, Sg = ex[0], ex[1]
    reps = accs[0].shape[1] // SLOT
    return ((accs[0] * jnp.tile(C, (1, reps)) + _rope_swap(accs[0]) * jnp.tile(Sg, (1, reps))) * Q_SCALE,)


def rope_k_epilogue(accs, ex):
    C, Sg = ex[0], ex[1]
    kr = _place_k_rope(ex[2][:, SLOT:2 * SLOT])
    kr = kr * C + _rope_swap(kr) * Sg
    return (accs[0] + jnp.tile(kr, (1, accs[0].shape[1] // SLOT)),)


def _rope_swap(x):
    W = x.shape[1]
    half = ROPE // 2
    lane = lax.broadcasted_iota(jnp.int32, (1, W), 1) & (SLOT - 1)
    up = pltpu.roll(x, W - half, axis=1)
    dn = pltpu.roll(x, half, axis=1)
    return jnp.where((lane >= NOPE) & (lane < NOPE + half), up, jnp.where((lane >= NOPE + half) & (lane < QK), dn, 0.0))


def rope_slot_bwd(dq, dk, C, Sg, name):
    def fn(dqv, dkv, C, Sg):
        ct, stl = jnp.tile(C, (1, MLA_H)), jnp.tile(Sg, (1, MLA_H))
        dqo = dqv * ct - _rope_swap(dqv) * stl
        tot = dkv[:, 0:SLOT]
        for h in range(1, MLA_H):
            tot = tot + dkv[:, h * SLOT:(h + 1) * SLOT]
        u = tot * C - _rope_swap(tot) * Sg
        r = lax.broadcasted_iota(jnp.int32, (SLOT, SLOT), 0)
        c = lax.broadcasted_iota(jnp.int32, (SLOT, SLOT), 1)
        unplace = ((r == c + NOPE) & (c < ROPE)).astype(F32)
        return dqo, dkv, _dot_sel(u, unplace, terms=2)
    W = MLA_H * SLOT
    return rowwise(fn, [dq, dk, C, Sg], [], [(W, BF16), (W, BF16), (SLOT, BF16)], [], name)


XA_BLK = 512


def xattn_fwd(q, k, v, nseq, comm=None):
    T = q.shape[0]
    S = T // nseq
    M = k.shape[0] // nseq
    tq = min(XA_BLK, S)
    nq = S // tq
    scale = XA_D ** -0.5

    def body(q_ref, k_ref, v_ref, o_ref):
        s = _dot(q_ref[...], k_ref[...], "nt") * scale
        p = jnp.exp(s - jnp.max(s, axis=1, keepdims=True))
        p = p / jnp.sum(p, axis=1, keepdims=True)
        o_ref[...] = _dot(p, v_ref[...]).astype(o_ref.dtype)

    qs = pl.BlockSpec((tq, XA_D), lambda b, h, i: (b * nq + i, h))
    ks = pl.BlockSpec((M, XA_D), lambda b, h, i: (b, h))
    return _call_with_comm(body, (nseq, XA_H, nq), "xattn_fwd", [qs, ks, ks], [q, k, v], [qs],
                           [jax.ShapeDtypeStruct((T, XA_H * XA_D), BF16)], comm)


def xattn_bwd(q, k, v, do, nseq):
    T = q.shape[0]
    S = T // nseq
    M = k.shape[0] // nseq
    tq = min(XA_BLK, S)
    nq = S // tq
    scale = XA_D ** -0.5

    def body(q_ref, k_ref, v_ref, do_ref, dq_ref, dk_ref, dv_ref):
        @pl.when(pl.program_id(2) == 0)
        def _():
            dk_ref[...] = jnp.zeros_like(dk_ref)
            dv_ref[...] = jnp.zeros_like(dv_ref)

        qb, kb, vb, dob = q_ref[...], k_ref[...], v_ref[...], do_ref[...]
        s = _dot(qb, kb, "nt") * scale
        p = jnp.exp(s - jnp.max(s, axis=1, keepdims=True))
        p = p / jnp.sum(p, axis=1, keepdims=True)
        dp = _dot(dob, vb, "nt")
        ds = p * (dp - jnp.sum(dp * p, axis=1, keepdims=True)) * scale
        dq_ref[...] = _dot(ds, kb).astype(dq_ref.dtype)
        dk_ref[...] += _dot(ds, qb, "tn")
        dv_ref[...] += _dot(p, dob, "tn")

    qs = pl.BlockSpec((tq, XA_D), lambda b, h, i: (b * nq + i, h))
    ks = pl.BlockSpec((M, XA_D), lambda b, h, i: (b, h))
    return pl.pallas_call(
        body, grid=(nseq, XA_H, nq), name="xattn_bwd", in_specs=[qs, ks, ks, qs], out_specs=[qs, ks, ks],
        out_shape=[jax.ShapeDtypeStruct((T, XA_H * XA_D), BF16), jax.ShapeDtypeStruct(k.shape, F32),
                   jax.ShapeDtypeStruct(k.shape, F32)],
        compiler_params=_cp("parallel", "parallel", "arbitrary"),
    )(q, k, v, do)


CONV_BLK = 256


def _shift_down(x, s, rows):
    if s == 0:
        return x
    return jnp.where(rows >= s, pltpu.roll(x, s, axis=0), 0.0)


def _shift_up(x, s, rows):
    if s == 0:
        return x
    S = x.shape[0]
    return jnp.where(rows < S - s, pltpu.roll(x, S - s, axis=0), 0.0)


def conv_fwd(x, w, b, nseq):
    T, CH = x.shape
    S = T // nseq

    def body(x_ref, w_ref, b_ref, o_ref):
        xv = x_ref[...].astype(F32)
        rows = lax.broadcasted_iota(jnp.int32, (S, 1), 0)
        c = jnp.zeros_like(xv) + b_ref[...]
        for kk in range(CONV_K):
            c = c + w_ref[kk:kk + 1, :] * _shift_down(xv, CONV_K - 1 - kk, rows)
        o_ref[...] = (c * _sigmoid(c)).astype(o_ref.dtype)

    xs = pl.BlockSpec((S, CONV_BLK), lambda j, bb: (bb, j))
    return pl.pallas_call(
        body, grid=(CH // CONV_BLK, nseq), name="conv_fwd",
        in_specs=[xs, pl.BlockSpec((CONV_K, CONV_BLK), lambda j, bb: (0, j)), pl.BlockSpec((1, CONV_BLK), lambda j, bb: (0, j))],
        out_specs=xs, out_shape=jax.ShapeDtypeStruct((T, CH), BF16),
        compiler_params=_cp("parallel", "parallel"),
    )(x, w, b)


def conv_bwd(x, w, b, dout, nseq):
    T, CH = x.shape
    S = T // nseq

    def body(x_ref, w_ref, b_ref, do_ref, dx_ref, dw_ref, db_ref):
        @pl.when(pl.program_id(1) == 0)
        def _():
            dw_ref[...] = jnp.zeros_like(dw_ref)
            db_ref[...] = jnp.zeros_like(db_ref)

        xv = x_ref[...].astype(F32)
        rows = lax.broadcasted_iota(jnp.int32, (S, 1), 0)
        c = jnp.zeros_like(xv) + b_ref[...]
        sh = [_shift_down(xv, CONV_K - 1 - kk, rows) for kk in range(CONV_K)]
        for kk in range(CONV_K):
            c = c + w_ref[kk:kk + 1, :] * sh[kk]
        sg = _sigmoid(c)
        dc = do_ref[...].astype(F32) * sg * (1.0 + c * (1.0 - sg))
        dx = jnp.zeros_like(xv)
        for kk in range(CONV_K):
            dx = dx + w_ref[kk:kk + 1, :] * _shift_up(dc, CONV_K - 1 - kk, rows)
            dw_ref[kk:kk + 1, :] += jnp.sum(dc * sh[kk], axis=0, keepdims=True)
        dx_ref[...] = dx.astype(dx_ref.dtype)
        db_ref[...] += jnp.sum(dc, axis=0, keepdims=True)

    xs = pl.BlockSpec((S, CONV_BLK), lambda j, bb: (bb, j))
    ws = pl.BlockSpec((CONV_K, CONV_BLK), lambda j, bb: (0, j))
    bs = pl.BlockSpec((1, CONV_BLK), lambda j, bb: (0, j))
    return pl.pallas_call(
        body, grid=(CH // CONV_BLK, nseq), name="conv_bwd",
        in_specs=[xs, ws, bs, xs], out_specs=[xs, ws, bs],
        out_shape=[jax.ShapeDtypeStruct((T, CH), BF16), jax.ShapeDtypeStruct((CONV_K, CH), F32),
                   jax.ShapeDtypeStruct((1, CH), F32)],
        compiler_params=_cp("parallel", "arbitrary"),
    )(x, w, b, dout)


def _dims(a, b, mode):
    M = a.shape[1] if mode[0] == "t" else a.shape[0]
    K = a.shape[0] if mode[0] == "t" else a.shape[1]
    N = b.shape[0] if mode[1] == "t" else b.shape[1]
    return M, K, N


def _tile(dim, prefs):
    for p in prefs:
        if dim % p == 0:
            return p
    return dim


def mm(groups, out_dtypes, name, tm=None, tn=None, tk=None, epi=None, extras=(), comm=None, sub=1, n_sum=0):
    a0, b0, m0 = groups[0][0]
    M, K0, N = _dims(a0, b0, m0)
    tm = tm or _tile(M, (1024, 512, 256, 128))
    tn = tn or _tile(N, (1024, 512, 256, 128))
    flat = [p for g in groups for p in g]
    nk = 1 if tk is None else K0 // tk
    in_specs, args = [], []
    for a, b, mode in flat:
        _, K, _ = _dims(a, b, mode)
        kb = K if tk is None else tk
        in_specs.append(pl.BlockSpec((kb, tm), lambda i, j, k: (k, i)) if mode[0] == "t"
                        else pl.BlockSpec((tm, kb), lambda i, j, k: (i, k)))
        in_specs.append(pl.BlockSpec((tn, kb), lambda i, j, k: (j, k)) if mode[1] == "t"
                        else pl.BlockSpec((kb, tn), lambda i, j, k: (k, j)))
        args += [a, b]
    kinds = []
    for e in extras:
        kind, e = e if isinstance(e, tuple) else ("vec" if e.shape[0] == 1 and M != 1 else "tile", e)
        in_specs.append({"tile": pl.BlockSpec((tm, tn), lambda i, j, k: (i, j)),
                         "vec": pl.BlockSpec((1, tn), lambda i, j, k: (0, j)),
                         "rows": pl.BlockSpec((tm, e.shape[1]), lambda i, j, k: (i, 0)),
                         "whole": pl.BlockSpec(e.shape, lambda i, j, k: (0, 0))}[kind])
        kinds.append(kind)
        args.append(e)
    n_in = len(args)
    n_main = len(out_dtypes)
    n_out = n_main + n_sum
    assert n_sum == 0 or (tn == N and tk is None)
    ng = len(groups)
    sizes = [len(g) for g in groups]

    def body(*refs):
        ins, outs, accs = refs[:n_in], refs[n_in:n_in + n_out], refs[n_in + n_out:]
        kk = pl.program_id(2)

        def dots(rs):
            vals, pos = [], 0
            for gi in range(ng):
                acc = None
                for _ in range(sizes[gi]):
                    mode = flat[pos // 2][2]
                    av = ins[pos][:, rs] if mode[0] == "t" else ins[pos][rs, :]
                    d = _dot(av, ins[pos + 1][...], mode)
                    acc = d if acc is None else acc + d
                    pos += 2
                vals.append(acc)
            return vals

        def finish(accv, rs, first_chunk=True):
            ex = [(r[rs, :] if kind in ("tile", "rows") else r[...]).astype(F32) for kind, r in zip(kinds, ins[2 * len(flat):])]
            res = epi(accv, ex) if epi is not None else tuple(accv)
            for o, r in zip(outs[:n_main], res[:n_main]):
                o[rs, :] = r.astype(o.dtype)
            for o, r in zip(outs[n_main:], res[n_main:]):
                if first_chunk:
                    @pl.when(pl.program_id(0) == 0)
                    def _():
                        o[...] = r

                    @pl.when(pl.program_id(0) > 0)
                    def _():
                        o[...] += r
                else:
                    o[...] += r

        if nk == 1:
            for r in range(sub):
                rs = slice(r * (tm // sub), (r + 1) * (tm // sub))
                finish(dots(rs), rs, r == 0)
        else:
            vals = dots(slice(0, tm))
            finish = functools.partial(finish, rs=slice(0, tm))
            @pl.when(kk == 0)
            def _():
                for ar, vv in zip(accs, vals):
                    ar[...] = vv

            @pl.when(kk > 0)
            def _():
                for ar, vv in zip(accs, vals):
                    ar[...] += vv

            @pl.when(kk == nk - 1)
            def _():
                finish([ar[...] for ar in accs])

    grid = (M // tm, N // tn, nk)
    out_specs = [pl.BlockSpec((tm, tn), lambda i, j, k: (i, j)) for _ in out_dtypes] \
        + [pl.BlockSpec((1, tn), lambda i, j, k: (0, j))] * n_sum
    out_shape = [jax.ShapeDtypeStruct((M, N), dt) for dt in out_dtypes] + [jax.ShapeDtypeStruct((1, N), F32)] * n_sum
    scratch = [pltpu.VMEM((tm, tn), F32) for _ in range(ng if nk > 1 else 0)]
    sem = ("arbitrary" if n_sum else "parallel", "parallel", "arbitrary")
    if comm is not None:
        body = _attach(comm, body, n_in, n_out, *_grid_ends(grid))
        in_specs, args = in_specs + [HBM_SPEC] * len(comm.inputs), args + comm.inputs
        out_specs, out_shape = out_specs + [HBM_SPEC] * len(comm.out_shapes), out_shape + comm.out_shapes
        scratch, sem = scratch + comm.sems, ("arbitrary",) * 3
    return pl.pallas_call(body, grid=grid, name=name, in_specs=in_specs, out_specs=out_specs, out_shape=out_shape,
                          scratch_shapes=scratch, compiler_params=_cp(*sem))(*args)


def mm1(a, b, mode, out_dtype, name, **kw):
    return mm([[(a, b, mode)]], [out_dtype], name, **kw)[0]


ROW_BLK = 512


def rowwise(fn, rows, consts, outs, accs, name, tb=ROW_BLK, comm=None):
    rows = [r if isinstance(r, tuple) else (r, r.shape[1], 0) for r in rows]
    T = rows[0][0].shape[0]
    tb = min(tb, T)
    n_r, n_c, n_o, n_a = len(rows), len(consts), len(outs), len(accs)

    def body(*refs):
        vals = [r[...].astype(F32) for r in refs[:n_r + n_c]]
        res = fn(*vals)
        o_refs = refs[n_r + n_c:n_r + n_c + n_o]
        a_refs = refs[n_r + n_c + n_o:]
        for o, r in zip(o_refs, res[:n_o]):
            o[...] = r.astype(o.dtype)
        if n_a:
            @pl.when(pl.program_id(0) == 0)
            def _():
                for ar in a_refs:
                    ar[...] = jnp.zeros_like(ar)
            for ar, r in zip(a_refs, res[n_o:]):
                ar[...] += r

    return _call_with_comm(
        body, (T // tb,), name,
        [pl.BlockSpec((tb, w), functools.partial(lambda i, j: (i, j), j=j)) for _, w, j in rows]
        + [pl.BlockSpec(c.shape, lambda i: (0, 0)) for c in consts],
        [r[0] for r in rows] + list(consts),
        [pl.BlockSpec((tb, d), lambda i: (i, 0)) for d, _ in outs] + [pl.BlockSpec(s, lambda i: (0, 0)) for s in accs],
        [jax.ShapeDtypeStruct((T, d), dt) for d, dt in outs] + [jax.ShapeDtypeStruct(s, F32) for s in accs],
        comm, sem=("arbitrary" if n_a else "parallel",))


def _rms_stats(x):
    r = lax.rsqrt(jnp.mean(x * x, axis=-1, keepdims=True) + EPS)
    return r, x * r


def _rms_bwd(x, g, dy):
    r, xn = _rms_stats(x)
    dyg = dy * g
    dx = r * (dyg - xn * jnp.mean(dyg * xn, axis=-1, keepdims=True))
    return dx, jnp.sum(dy * xn, axis=0, keepdims=True)


def rms_fwd(x, g, name, comm=None):
    res = rowwise(lambda xv, gv: (_rms_stats(xv)[1] * gv,), [x], [g], [(x.shape[1], BF16)], [], name, comm=comm)
    return res[0] if comm is None else (res[0], res[1:])


def rms_bwd(x, g, dy, name, resid=None, dx_dtype=F32):
    def fn(*v):
        if resid is None:
            xv, dyv, gv = v
            dx, dg = _rms_bwd(xv, gv, dyv)
        else:
            xv, dyv, rv, gv = v
            dx, dg = _rms_bwd(xv, gv, dyv)
            dx = dx + rv
        return dx, dg
    rows = [x, dy] + ([] if resid is None else [resid])
    return rowwise(fn, rows, [g], [(x.shape[1], dx_dtype)], [(1, x.shape[1])], name)


def mm_rms_bwd(pairs, x, g, name, resid=None, dx_dtype=F32, comm=None):
    def epi(accs, ex):
        dx, dg = _rms_bwd(ex[0], ex[-1], accs[0])
        return (dx if resid is None else dx + ex[1]), dg
    extras = [x] + ([] if resid is None else [resid]) + [g]
    return mm([pairs], [dx_dtype], name, tm=min(256, x.shape[0]), tn=x.shape[1], epi=epi, extras=extras, comm=comm, n_sum=1)


def mm_resid(a, b, x, g, wgt, name, comm=None, target=None):
    def epi(accs, ex):
        y = ex[0] + wgt * _rms_stats(accs[0])[1] * ex[1]
        if target is None:
            return accs[0], y
        d = y - ex[2]
        return accs[0], d / D, jnp.sum(d * d, axis=0, keepdims=True)
    return mm([[(a, b, "nn")]], [F32, F32], name, tm=min(512, a.shape[0]), tn=b.shape[1], epi=epi,
              extras=[x, g] + ([] if target is None else [target]), sub=2, comm=comm, n_sum=0 if target is None else 1)


def resid_bwd(h, g, dy, wgt, name):
    def fn(hv, dyv, gv):
        dx, dg = _rms_bwd(hv, gv, dyv)
        return wgt * dx, wgt * dg
    return rowwise(fn, [h, dy], [g], [(h.shape[1], BF16)], [(1, h.shape[1])], name)


def _silu_parts(g):
    s = _sigmoid(g)
    return g * s, s * (1.0 + g * (1.0 - s))


def gated_norm_fwd(y, z, g, name):
    W = SSD_INNER // SSD_G

    def fn(yv, zv, gv):
        yg = yv * _silu_parts(zv)[0]
        return (jnp.concatenate([_rms_stats(yg[:, i * W:(i + 1) * W])[1] for i in range(SSD_G)], axis=1) * gv,)
    return rowwise(fn, [y, z], [g], [(SSD_INNER, BF16)], [], name)[0]


def gated_norm_bwd(y, z, dyn, g, name):
    W = SSD_INNER // SSD_G

    def fn(yv, zv, dv, gv):
        sil, dsil = _silu_parts(zv)
        yg = yv * sil
        parts = [_rms_bwd(yg[:, i * W:(i + 1) * W], gv[:, i * W:(i + 1) * W], dv[:, i * W:(i + 1) * W]) for i in range(SSD_G)]
        dyg = jnp.concatenate([p[0] for p in parts], axis=1)
        dg = jnp.concatenate([p[1] for p in parts], axis=1)
        return dyg * sil, dyg * yv * dsil, dg
    return rowwise(fn, [y, z, dyn], [g], [(SSD_INNER, BF16), (SSD_INNER, BF16)], [(1, SSD_INNER)], name)


def merge_fwd(gl, ys, ym, gb, name):
    def fn(glv, ysv, ymv, gbv):
        gt = _sigmoid(glv + gbv)
        return (gt[:, :D] * ysv + gt[:, D:] * ymv,)
    return rowwise(fn, [gl, ys, ym], [gb], [(D, BF16)], [], name)[0]


def merge_bwd(gl, ys, ym, dm, gb, name):
    def fn(glv, ysv, ymv, dmv, gbv):
        gt = _sigmoid(glv + gbv)
        gs, gm = gt[:, :D], gt[:, D:]
        dgl = jnp.concatenate([dmv * ysv * gs * (1.0 - gs), dmv * ymv * gm * (1.0 - gm)], axis=1)
        return dmv * gs, dmv * gm, dgl, jnp.sum(dgl, axis=0, keepdims=True)
    return rowwise(fn, [gl, ys, ym, dm], [gb], [(D, BF16), (D, BF16), (2 * D, BF16)], [(1, 2 * D)], name)


def loss_head(y, tgt, name):
    def fn(yv, tv):
        d = yv - tv
        part = 0.5 * jnp.sum(jnp.sum(d * d, axis=1, keepdims=True), axis=0, keepdims=True) / D
        return d / D, jnp.broadcast_to(part, (1, 128))
    return rowwise(fn, [y, tgt], [], [(D, F32)], [(1, 128)], name)


def _adamw_math(wv, gv, mv, vv):
    mn = B1 * mv + (1.0 - B1) * gv
    vn = B2 * vv + (1.0 - B2) * (gv * gv)
    mh = mn / (1.0 - B1 ** STEP)
    vh = vn / (1.0 - B2 ** STEP)
    return -LR * (mh / (jnp.sqrt(vh) + AEPS) + WD * wv), mn, vn


def adamw(w, g, m, v, name):
    R, C = w.shape
    tb = _tile(R, (256, 128, 64, 32, 16, 8))
    return rowwise(_adamw_math, [w, g, m, v], [], [(C, F32)] * 3, [], name, tb=tb)


def adamw_small(packed, ws, ms, vs):
    k = len(ws)
    sizes = [x.shape[1] for x in ws]

    def body(*refs):
        p_ref, w_refs, m_refs, v_refs = refs[0], refs[1:1 + k], refs[1 + k:1 + 2 * k], refs[1 + 2 * k:1 + 3 * k]
        outs = refs[1 + 3 * k:]
        r0 = 0
        for i, n in enumerate(sizes):
            nr = -(-n // 128)
            g = jnp.concatenate([p_ref[r0 + r:r0 + r + 1, :] for r in range(nr)], axis=1)[:, :n]
            r0 += nr
            outs[i][...] = g
            outs[k + i][...], outs[2 * k + i][...], outs[3 * k + i][...] = _adamw_math(w_refs[i][...], g, m_refs[i][...], v_refs[i][...])

    res = pl.pallas_call(body, name="adamw_small",
                         out_shape=[jax.ShapeDtypeStruct((1, n), F32) for _ in range(4) for n in sizes])(packed, *ws, *ms, *vs)
    return [res[j * k:(j + 1) * k] for j in range(4)]


def adamw_from_slots(recv, piece, w, m, v, name, token=None):
    K, n = w.shape
    ns = recv.shape[0]
    assert recv.shape[2] == n and recv.shape[1] % K == 0
    tb = _tile(K, (256, 176, 128, 64, 32, 16, 8)) if K % 8 == 0 else K
    r_spec = pl.BlockSpec((ns, tb, n), lambda i: (0, piece * (K // tb) + i, 0))
    w_spec = pl.BlockSpec((tb, n), lambda i: (i, 0))

    def body(r_ref, w_ref, m_ref, v_ref, *rest):
        g_ref, d_ref, mo_ref, vo_ref = rest[-4:]
        g = r_ref[0].astype(F32)
        for s in range(1, ns):
            g = g + r_ref[s].astype(F32)
        g_ref[...] = g
        d_ref[...], mo_ref[...], vo_ref[...] = _adamw_math(w_ref[...], g, m_ref[...], v_ref[...])

    extra = [] if token is None else [token]
    return pl.pallas_call(
        body, grid=(K // tb,), name=name,
        in_specs=[r_spec, w_spec, w_spec, w_spec] + [pl.BlockSpec(t.shape, lambda i: (0, 0)) for t in extra], out_specs=[w_spec] * 4,
        out_shape=[jax.ShapeDtypeStruct((K, n), F32)] * 4, compiler_params=_cp("parallel"),
    )(recv, w, m, v, *extra)


def _me():
    return lax.axis_index("x"), lax.axis_index("y"), lax.axis_index("c")


def _dev_index():
    x, y, c = _me()
    return 4 * x + 2 * y + c


HBM_SPEC = pl.BlockSpec(memory_space=pl.ANY)


class GatherComm:
    def __init__(self, shards):
        self.inputs = [s for s, _ in shards]
        self.rows = [list(r) for _, r in shards]
        n = len(shards)
        self.out_shapes = [jax.ShapeDtypeStruct((N_DEV, r, s.shape[1]), s.dtype) for s, rows in shards for r in rows]
        self.sems = [pltpu.SemaphoreType.DMA((7 * n,)), pltpu.SemaphoreType.DMA((7 * n,)), pltpu.SemaphoreType.DMA((n,))]

    def _plan(self, x_refs, out_refs, sems):
        send_sems, recv_sems, local_sems = sems
        x, y, c = _me()
        me, sibling = (x, y, c), (x, y, 1 - c)
        chips = [(1 - x, y), (x, 1 - y), (1 - x, 1 - y)]
        index = lambda px, py, pc: 4 * px + 2 * py + pc
        mine, first, passed, whole = [], [], [], []
        pos = 0
        for i, rows in enumerate(self.rows):
            kw = lambda k: dict(send_sem=send_sems.at[7 * i + k], recv_sem=recv_sems.at[7 * i + k], device_id_type=MESH)
            r0 = 0
            fwd = [[] for _ in chips]
            for j, nr in enumerate(rows):
                out, src = out_refs[pos + j], x_refs[i].at[pl.ds(r0, nr)]
                mine.append(pltpu.make_async_copy(src, out.at[index(*me)], local_sems.at[i]))
                first.append(pltpu.make_async_remote_copy(src_ref=src, dst_ref=out.at[index(*me)], device_id=sibling, **kw(0)))
                for jj, chip in enumerate(chips):
                    first.append(pltpu.make_async_remote_copy(src_ref=src, dst_ref=out.at[index(*me)], device_id=(*chip, c),
                                                              **kw(1 + jj)))
                    blk = out.at[index(*chip, c)]
                    fwd[jj].append(pltpu.make_async_remote_copy(src_ref=blk, dst_ref=blk, device_id=sibling, **kw(4 + jj)))
                r0 += nr
            passed.append(fwd)
            whole.append([pltpu.make_async_remote_copy(src_ref=x_refs[i], dst_ref=x_refs[i], device_id=sibling, **kw(k))
                          for k in range(7)])
            pos += len(rows)
        return mine, first, passed, whole

    def start(self, x_refs, out_refs, sems):
        mine, first, _, _ = self._plan(x_refs, out_refs, sems)
        for cp in mine + first:
            cp.start()

    def finish(self, x_refs, out_refs, sems):
        _, _, passed, whole = self._plan(x_refs, out_refs, sems)
        local_sems = sems[2]
        for i, fwd in enumerate(passed):
            for jj in range(3):
                whole[i][1 + jj].wait_recv()
                for cp in fwd[jj]:
                    cp.start()
        for i in range(len(passed)):
            whole[i][0].wait_recv()
            for jj in range(3):
                whole[i][4 + jj].wait_recv()
        for i in range(len(passed)):
            for k in range(7):
                whole[i][k].wait_send()
            pltpu.make_async_copy(x_refs[i], x_refs[i], local_sems.at[i]).wait()


def run_comm(comm, name):
    n_in, n_out = len(comm.inputs), len(comm.out_shapes)

    def body(*refs):
        ins, outs, sems = refs[:n_in], refs[n_in:n_in + n_out], refs[n_in + n_out:]
        comm.start(ins, outs, sems)
        comm.finish(ins, outs, sems)

    return pl.pallas_call(body, name=name, out_shape=comm.out_shapes, in_specs=[HBM_SPEC] * n_in,
                          out_specs=[HBM_SPEC] * n_out, scratch_shapes=comm.sems)(*comm.inputs)


def _attach(comm, body, n_in, n_out, first, last):
    if comm is None:
        return body
    ci, co, cs = len(comm.inputs), len(comm.out_shapes), len(comm.sems)

    def wrapped(*refs):
        h_in, c_in = refs[:n_in], refs[n_in:n_in + ci]
        h_out, c_out = refs[n_in + ci:n_in + ci + n_out], refs[n_in + ci + n_out:n_in + ci + n_out + co]
        rest = refs[n_in + ci + n_out + co:]
        h_scr, c_sem = rest[:len(rest) - cs], rest[len(rest) - cs:]

        @pl.when(first())
        def _():
            comm.start(c_in, c_out, c_sem)

        body(*h_in, *h_out, *h_scr)

        @pl.when(last())
        def _():
            comm.finish(c_in, c_out, c_sem)

    return wrapped


def _grid_ends(grid):
    first = lambda: functools.reduce(lambda a, b: a & b, [pl.program_id(i) == 0 for i in range(len(grid))])
    last = lambda: functools.reduce(lambda a, b: a & b, [pl.program_id(i) == g - 1 for i, g in enumerate(grid)])
    return first, last


def _call_with_comm(body, grid, name, in_specs, args, out_specs, out_shape, comm, scratch=(), sem=None):
    sem = sem or ("parallel",) * len(grid)
    scratch = list(scratch)
    if comm is not None:
        body = _attach(comm, body, len(args), len(out_shape), *_grid_ends(grid))
        in_specs, args = in_specs + [HBM_SPEC] * len(comm.inputs), args + comm.inputs
        out_specs, out_shape = out_specs + [HBM_SPEC] * len(comm.out_shapes), out_shape + comm.out_shapes
        scratch, sem = scratch + comm.sems, ("arbitrary",) * len(grid)
    return pl.pallas_call(body, grid=grid, name=name, in_specs=in_specs, out_specs=out_specs, out_shape=out_shape,
                          scratch_shapes=scratch, compiler_params=_cp(*sem))(*args)


class ScatterComm:
    def __init__(self, groups):
        self.sizes = [len(g) for g in groups]
        self.rows = [[pc.shape[1] for pc in g] for g in groups]
        ng = len(groups)
        self.inputs = [pc for g in groups for pc in g]
        self.out_shapes = [jax.ShapeDtypeStruct((N_DEV, sum(self.rows[gi]), g[0].shape[2]), g[0].dtype) for gi, g in enumerate(groups)]
        self.sems = [pltpu.SemaphoreType.DMA((7 * ng,)), pltpu.SemaphoreType.DMA((7 * ng,)), pltpu.SemaphoreType.DMA((ng,))]

    def _peers(self):
        x, y, c = _me()
        out = []
        for k in range(1, N_DEV):
            px = 1 - x if k & 4 else x
            py = 1 - y if k & 2 else y
            pc = 1 - c if k & 1 else c
            out.append((k, 4 * px + 2 * py + pc, dict(device_id=(px, py, pc), device_id_type=MESH)))
        return 4 * x + 2 * y + c, out

    def start(self, ins, outs, sems):
        send_sems, recv_sems, local_sems = sems
        me, peers = self._peers()
        pos = 0
        for gi, size in enumerate(self.sizes):
            for i, pc in enumerate(ins[pos:pos + size]):
                dst = outs[gi].at[me, pl.ds(sum(self.rows[gi][:i]), self.rows[gi][i])]
                pltpu.make_async_copy(pc.at[me], dst, local_sems.at[gi]).start()
                for k, peer, kw in peers:
                    pltpu.make_async_remote_copy(src_ref=pc.at[peer], dst_ref=dst, send_sem=send_sems.at[7 * gi + k - 1],
                                                 recv_sem=recv_sems.at[7 * gi + k - 1], **kw).start()
            pos += size

    def finish(self, ins, outs, sems):
        send_sems, recv_sems, local_sems = sems
        me, peers = self._peers()
        whole = [pltpu.make_async_remote_copy(src_ref=outs[gi].at[peer], dst_ref=outs[gi].at[peer],
                                              send_sem=send_sems.at[7 * gi + k - 1], recv_sem=recv_sems.at[7 * gi + k - 1], **kw)
                 for gi in range(len(self.sizes)) for k, peer, kw in peers]
        for cp in whole:
            cp.wait_recv()
        for cp in whole:
            cp.wait_send()
        for gi in range(len(self.sizes)):
            pltpu.make_async_copy(outs[gi].at[me], outs[gi].at[me], local_sems.at[gi]).wait()


def _peer_list():
    x, y, c = _me()
    out = []
    for k in range(1, N_DEV):
        px = 1 - x if k & 4 else x
        py = 1 - y if k & 2 else y
        pc = 1 - c if k & 1 else c
        out.append((k, 4 * px + 2 * py + pc, dict(device_id=(px, py, pc), device_id_type=MESH)))
    return 4 * x + 2 * y + c, out


SEM_SPEC = pl.BlockSpec(memory_space=pltpu.SEMAPHORE)
HBM_ONLY = pl.BlockSpec(memory_space=pltpu.HBM)
N_SPLIT_SEMS = 2 * (N_DEV - 1)


def exchange_start(piece, after):
    def body(piece_ref, land_ref, after_ref, *outs):
        sems, token = outs[:N_SPLIT_SEMS], outs[-1]
        me, peers = _peer_list()
        for k, peer, kw in peers:
            pltpu.make_async_remote_copy(src_ref=piece_ref.at[peer], dst_ref=land_ref.at[me], send_sem=sems[k - 1],
                                         recv_sem=sems[N_DEV - 2 + k], **kw).start()
        token[...] = jnp.zeros_like(token)

    res = pl.pallas_call(
        body, name="exchange_last_start",
        out_shape=(pltpu.SemaphoreType.DMA(()),) * N_SPLIT_SEMS + (pltpu.HBM(piece.shape, piece.dtype), pltpu.HBM(piece.shape, piece.dtype),
                                                                   jax.ShapeDtypeStruct((8, 128), F32)),
        in_specs=(HBM_ONLY, HBM_ONLY, HBM_SPEC),
        out_specs=(SEM_SPEC,) * N_SPLIT_SEMS + (HBM_ONLY, HBM_ONLY, pl.BlockSpec(memory_space=pltpu.VMEM)),
        input_output_aliases={0: N_SPLIT_SEMS, 1: N_SPLIT_SEMS + 1},
        compiler_params=pltpu.CompilerParams(has_side_effects=pltpu.SideEffectType.DATAFLOW_SIDE_EFFECTING),
    )(pltpu.with_memory_space_constraint(piece, pltpu.HBM),
      pltpu.with_memory_space_constraint(lax.empty(piece.shape, piece.dtype), pltpu.HBM), after)
    return res[:N_SPLIT_SEMS], res[N_SPLIT_SEMS], res[N_SPLIT_SEMS + 1], res[N_SPLIT_SEMS + 2]


def exchange_wait(sems, piece, land, after):
    def body(piece_ref, land_ref, *rest):
        sem_refs = rest[:N_SPLIT_SEMS]
        me, peers = _peer_list()
        for k, peer, kw in peers:
            cp = pltpu.make_async_remote_copy(src_ref=piece_ref.at[peer], dst_ref=land_ref.at[peer], send_sem=sem_refs[k - 1],
                                              recv_sem=sem_refs[N_DEV - 2 + k], **kw)
            cp.wait_send()
            cp.wait_recv()

    return pl.pallas_call(
        body, name="exchange_last_wait",
        out_shape=(pltpu.HBM(piece.shape, piece.dtype), pltpu.HBM(land.shape, land.dtype)),
        in_specs=(HBM_ONLY, HBM_ONLY) + (SEM_SPEC,) * N_SPLIT_SEMS + (HBM_SPEC,), out_specs=(HBM_ONLY, HBM_ONLY),
        input_output_aliases={0: 0, 1: 1},
        compiler_params=pltpu.CompilerParams(has_side_effects=pltpu.SideEffectType.DATAFLOW_SIDE_EFFECTING),
    )(piece, land, *sems, after)[1]


def sum_slots(recv, name, tr):
    n, R, C = recv.shape

    def body(r_ref, o_ref):
        acc = r_ref[0].astype(F32)
        for s in range(1, n):
            acc = acc + r_ref[s].astype(F32)
        o_ref[...] = acc

    return pl.pallas_call(
        body, grid=(R // tr,), name=name,
        in_specs=[pl.BlockSpec((n, tr, C), lambda i: (0, i, 0))], out_specs=pl.BlockSpec((tr, C), lambda i: (i, 0)),
        out_shape=jax.ShapeDtypeStruct((R, C), F32), compiler_params=_cp("parallel"),
    )(recv)


PACK_W, FLAT_W = 1024, 128
MAIN = [
    ("ffn1_w_gate", "col"), ("ffn1_w_up", "col"), ("ffn1_w_down", "row"),
    ("ffn2_w_gate", "col"), ("ffn2_w_up", "col"), ("ffn2_w_down", "row"),
    ("w_ssd_proj", "row"), ("w_mla_proj", "row"), ("w_out", "row"),
    ("w_xq", "row"), ("w_xk", "row"), ("w_xv", "row"), ("w_xo", "row"),
    ("w_uk", "col"), ("w_uv", "col"), ("w_in", "col"),
]
FLAT = [("w_uq", "col")]
BIG = MAIN + FLAT
SMALL = ["ffn1_pre_g", "ffn1_post_g", "mix_pre_g", "conv_b", "dt_bias", "a_log", "d_skip", "ssd_norm_g", "q_norm_g",
         "kv_norm_g", "gate_bias", "mix_post_g", "xa_pre_g", "mem_norm_g", "xa_post_g", "ffn2_pre_g", "ffn2_post_g"]
WEIGHTS = ['ffn1_pre_g', 'ffn1_w_gate', 'ffn1_w_up', 'ffn1_w_down', 'ffn1_post_g', 'mix_pre_g', 'w_in', 'conv_w', 'conv_b',
           'dt_bias', 'a_log', 'd_skip', 'ssd_norm_g', 'w_ssd_proj', 'q_norm_g', 'w_uq', 'kv_norm_g', 'w_uk', 'w_uv',
           'w_mla_proj', 'gate_bias', 'w_out', 'mix_post_g', 'xa_pre_g', 'mem_norm_g', 'w_xq', 'w_xk', 'w_xv', 'w_xo',
           'xa_post_g', 'ffn2_pre_g', 'ffn2_w_gate', 'ffn2_w_up', 'ffn2_w_down', 'ffn2_post_g']


def _pack_rows(w, kind, width):
    m = w[0].T if kind == "col" else w[0]
    return m.reshape(-1, width)


KIND = dict(BIG)
GATHER_PLAN = {
    "ffn1_pre": (["ffn1_w_gate", "ffn1_w_up"], []),
    "ffn1_gate_up": (["ffn1_w_down", "w_in@0"], []),
    "ffn1_down": (["w_in@1"], ["conv_w"]),
    "ssd_fwd": (["w_ssd_proj", "w_mla_proj", "w_out", "w_uk", "w_uv"], ["w_uq"]),
    "attn_fwd": (["w_xq", "w_xk", "w_xv", "w_xo", "ffn2_w_gate", "ffn2_w_up", "ffn2_w_down"], []),
}
LAST_EXCHANGE = "last"
SCATTER_PLAN = {
    "attn_bwd": [["ffn2_w_gate", "ffn2_w_up", "ffn2_w_down"], ["w_xq", "w_xk", "w_xv", "w_xo"]],
    "ssd_bwd": [["w_ssd_proj", "w_mla_proj", "w_out"], ["w_uk", "w_uv"], ["w_uq"]],
    "in_bwd": [["w_in#0"]],
    "ffn1:down_bwd": [["w_in#1"]],
    "ffn1:dwd": [["w_in#2"]],
    "ffn1:dwg": [["ffn1_w_down#0"]],
    "ffn1:dwu": [["ffn1_w_down#1"]],
    "ffn1:gate_up_bwd": [["ffn1_w_gate"]],
    "last": [["ffn1_w_up"]],
}
PARTS = {"w_in@0": ("w_in", 0, 336), "w_in@1": ("w_in", 336, 662),
         "w_in#0": ("w_in", 0, 336), "w_in#1": ("w_in", 336, 496), "w_in#2": ("w_in", 496, 662),
         "ffn1_w_down#0": ("ffn1_w_down", 0, 176), "ffn1_w_down#1": ("ffn1_w_down", 176, 352)}


def _parts_of(base, mark):
    return sorted(pn for pn, (b, _, _) in PARTS.items() if b == base and mark in pn)


class Stage:
    def __init__(self, w):
        self.w = w
        self.width = {n: PACK_W if (n, k) in MAIN else FLAT_W for n, k in BIG}
        self.nrows = {n: math.prod(w[n].shape) // self.width[n] for n, _ in BIG}
        self.recv = {}
        self.arrived_parts = {}

    def _rows(self, n):
        return PARTS[n][2] - PARTS[n][1] if n in PARTS else self.nrows[n]

    def _shards(self, tag):
        names_main, names_flat = GATHER_PLAN[tag]

        def pack(n):
            if n == "conv_w":
                return _pad_rows(lax.bitcast_convert_type(self.w[n][0], BF16).reshape(-1, FLAT_W), 16)
            base, r0, r1 = PARTS.get(n, (n, 0, None))
            return _pack_rows(self.w[base], KIND[base], self.width[base])[r0:r1].astype(BF16)
        shards = []
        if names_main:
            pieces = [pack(n) for n in names_main]
            shards.append((jnp.concatenate(pieces, axis=0), [pc.shape[0] for pc in pieces]))
        if names_flat:
            pieces = [pack(n) for n in names_flat]
            shards.append((jnp.concatenate(pieces, axis=0), [pc.shape[0] for pc in pieces]))
        return shards

    def gather(self, tag):
        return GatherComm(self._shards(tag)) if tag in GATHER_PLAN else None

    def gathered(self, tag, outs, W, p):
        if tag not in GATHER_PLAN:
            return
        names_main, names_flat = GATHER_PLAN[tag]
        outs = list(outs)
        for n in names_main + names_flat:
            rows = outs.pop(0)
            if n == "conv_w":
                cw = self.w[n]
                bits = rows[:, :2 * math.prod(cw.shape) // FLAT_W].reshape((N_DEV,) + cw.shape[1:] + (2,))
                p[n] = lax.bitcast_convert_type(bits, F32).transpose(1, 0, 2).reshape(cw.shape[1], -1)
                continue
            if n in PARTS:
                self.arrived_parts[n] = rows
                base = PARTS[n][0]
                mine = _parts_of(base, "@")
                if not all(pn in self.arrived_parts for pn in mine):
                    continue
                n, rows = base, jnp.concatenate([self.arrived_parts[pn] for pn in mine], axis=1)
            K = self.w[n].shape[1] if KIND[n] == "col" else PACK_W
            W[n] = rows.reshape(-1, K)

    def pieces(self, tag, gw):
        def piece(n):
            if n in PARTS:
                base, r0, r1 = PARTS[n]
                return gw[base].reshape(N_DEV, self.nrows[base], self.width[base])[:, r0:r1]
            return gw[n].reshape(N_DEV, self.nrows[n], self.width[n])
        return [[piece(n) for n in names] for names in SCATTER_PLAN[tag]]

    def scatter(self, tag, gw):
        return ScatterComm(self.pieces(tag, gw)) if tag in SCATTER_PLAN else None

    def scattered(self, tag, outs):
        if tag in SCATTER_PLAN:
            self.recv[tag] = outs


def _pad_rows(a, mult):
    r = (-a.shape[0]) % mult
    return a if r == 0 else jnp.concatenate([a, jnp.zeros((r,) + a.shape[1:], a.dtype)], axis=0)


def _pack_small(vals, loss_row=None, conv_w=None):
    rows = []
    for v in vals:
        f = v.reshape(-1)
        f = jnp.concatenate([f, jnp.zeros(((-f.shape[0]) % 128,), F32)])
        rows.append(f.reshape(-1, 128))
    if conv_w is not None:
        rows.append(conv_w.reshape(-1, 128))
    if loss_row is not None:
        rows.append(loss_row)
    return _pad_rows(jnp.concatenate(rows, axis=0), 8)


def _unpack_small(buf, shapes):
    out, r = [], 0
    for shp in shapes:
        n = math.prod(shp)
        nr = -(-n // 128)
        out.append(buf[r:r + nr].reshape(-1)[:n].reshape(shp))
        r += nr
    return out, r


def _tn(a, b, name, out_dtype=BF16, comm=None):
    M, N = a.shape[1], b.shape[1]
    T = a.shape[0]
    tm = M if M <= 1536 else M // 2
    tk = 2048 if T % 2048 == 0 and T > 2048 else None
    res = mm([[(a, b, "tn")]], [out_dtype], name, tm=tm, tn=N, tk=tk, comm=comm)
    return res[0] if comm is None else (res[0], res[1:])


class NoStage:
    def gather(self, tag):
        return None

    def gathered(self, tag, outs, W, p):
        pass

    def scatter(self, tag, gw):
        return None

    def scattered(self, tag, outs):
        pass


def _ffn_fwd(x, gpre, gpost, W, p, tag, stage, target=None):
    comm = stage.gather(tag + "_pre")
    h = rms_fwd(x, gpre, tag + "_pre", comm=comm)
    if comm is not None:
        h, arrived = h
        stage.gathered(tag + "_pre", arrived, W, p)

    def swi(accs, ex):
        sil, dsil = _silu_parts(accs[0])
        return sil, accs[1] * dsil, sil * accs[1]
    G, U, A, *arrived = mm([[(h, W[tag + "_w_gate"], "nt")], [(h, W[tag + "_w_up"], "nt")]], [BF16, BF16, BF16], tag + "_gate_up",
                           tn=DFF // 2, epi=swi, comm=stage.gather(tag + "_gate_up"), sub=4 if h.shape[0] % 1024 == 0 else 1)
    stage.gathered(tag + "_gate_up", arrived, W, p)
    H, y, *rest = mm_resid(A, W[tag + "_w_down"], x, gpost, FFN_RES, tag + "_down", comm=stage.gather(tag + "_down"), target=target)
    saved = (x, h, G, U, A, H)
    if target is not None:
        return y, saved, rest[0]
    stage.gathered(tag + "_down", rest, W, p)
    return y, saved


def _ffn_bwd(dy, saved, gpre, gpost, wg_t, wu_t, wd, tag, stage, gw):
    x, h, G, U, A, H = saved
    dH, dgpost = resid_bwd(H, gpost, dy, FFN_RES, tag + "_post_bwd")

    def dswi(accs, ex):
        return accs[0] * ex[1], accs[0] * ex[0]

    def hosted(where, call):
        comm = stage.scatter(tag + ":" + where, gw)
        res = call(comm)
        if comm is None:
            return res
        stage.scattered(tag + ":" + where, res[1])
        return res[0]

    res = hosted("down_bwd", lambda comm: (lambda r: r if comm is None else (r[:2], r[2:]))(
        mm([[(dH, wd, "nt")]], [BF16, BF16], tag + "_down_bwd", tn=DFF // 2, epi=dswi, extras=[G, U], comm=comm,
           sub=4 if dH.shape[0] % 1024 == 0 else 1)))
    dG, dU = res
    gw[tag + "_w_down"] = hosted("dwd", lambda comm: _tn(A, dH, tag + "_dwd", comm=comm))
    gw[tag + "_w_gate"] = hosted("dwg", lambda comm: _tn(dG, h, tag + "_dwg", comm=comm))
    gw[tag + "_w_up"] = hosted("dwu", lambda comm: _tn(dU, h, tag + "_dwu", comm=comm))
    dx, dgpre = hosted("gate_up_bwd", lambda comm: (lambda r: r[:2] if comm is None else (r[:2], r[2:]))(
        mm_rms_bwd([(dG, wg_t, "nn"), (dU, wu_t, "nn")], x, gpre, tag + "_gate_up_bwd", resid=dy, comm=comm)))
    return dx, dgpre, dgpost


def _local_step(x, mem, positions, tgt, W, p, stage=None):
    stage = stage or NoStage()
    nseq = x.shape[0]
    T = nseq * x.shape[1]
    x0 = x.reshape(T, D)
    mem2 = mem.reshape(-1, D)

    x1, ffn1 = _ffn_fwd(x0, p["ffn1_pre_g"], p["ffn1_post_g"], W, p, "ffn1", stage)

    w_in_t = W["w_in"]
    bounds = [0]
    for n in (SSD_INNER, CONV_CH, SSD_H, QR, KVR, ROPE, 2 * D):
        bounds.append(bounds[-1] + n)
    wt_z, wt_xbc, wt_dt, wt_q, wt_kv, wt_kr, wt_gate = [w_in_t[bounds[i]:bounds[i + 1]] for i in range(7)]
    wt_dt, wt_kr = _pad_rows(wt_dt, SLOT), _pad_rows(wt_kr, SLOT)
    wt_dtkr = jnp.concatenate([wt_dt, wt_kr], axis=0)
    hm = rms_fwd(x1, p["mix_pre_g"], "mix_pre")
    z = mm1(hm, wt_z, "nt", BF16, "in_z")
    xbc = mm1(hm, wt_xbc, "nt", BF16, "in_xbc")
    q_c = mm1(hm, wt_q, "nt", F32, "in_q", tn=QR)
    kv_c = mm1(hm, wt_kv, "nt", F32, "in_kv")
    dtkr = mm1(hm, wt_dtkr, "nt", F32, "in_dtkr")
    gl = mm1(hm, wt_gate, "nt", BF16, "in_gate")

    xbc_act = conv_fwd(xbc, p["conv_w"], p["conv_b"], nseq)
    y_ssd_core, prev, *arrived = ssd_fwd(xbc_act, dtkr, p["dt_bias"], p["a_log"], p["d_skip"], nseq, comm=stage.gather("ssd_fwd"))
    stage.gathered("ssd_fwd", arrived, W, p)
    yn = gated_norm_fwd(y_ssd_core, z, p["ssd_norm_g"], "ssd_norm")
    y_ssd = mm1(yn, W["w_ssd_proj"], "nn", BF16, "ssd_proj")

    slot_rows = lambda wt, per: jnp.pad(wt.reshape(MLA_H, per, -1), ((0, 0), (0, SLOT - per), (0, 0))).reshape(MLA_H * SLOT, -1)
    wq_s, wk_s, wv_s = slot_rows(W["w_uq"], QK), slot_rows(W["w_uk"], NOPE), slot_rows(W["w_uv"], VD)
    wo_s = slot_rows(W["w_mla_proj"], VD)
    qn = rms_fwd(q_c, p["q_norm_g"], "q_norm")
    rope_c, rope_s = rope_table(*_rope_inputs(positions))
    rope_args = [("rows", rope_c), ("rows", rope_s)]
    Qc, = mm([[(qn, wq_s, "nt")]], [BF16], "uq", epi=rope_q_epilogue, extras=rope_args, sub=4 if T % 1024 == 0 else 1)
    kvn = rms_fwd(kv_c, p["kv_norm_g"], "kv_norm")
    Kc, = mm([[(kvn, wk_s, "nt")]], [BF16], "uk", epi=rope_k_epilogue, extras=rope_args + [("rows", dtkr)],
             sub=4 if T % 1024 == 0 else 1)
    v_s = mm1(kvn, wv_s, "nt", BF16, "uv")
    o_s, lse, *arrived = attn_slot_fwd(Qc, Kc, v_s, nseq, comm=stage.gather("attn_fwd"))
    stage.gathered("attn_fwd", arrived, W, p)
    y_mla = mm1(o_s, wo_s, "nn", BF16, "mla_proj")

    merged = merge_fwd(gl, y_ssd, y_mla, p["gate_bias"], "merge")
    hmix, x2 = mm_resid(merged, W["w_out"], x1, p["mix_post_g"], 1.0, "mix_out")

    hq = rms_fwd(x2, p["xa_pre_g"], "xa_pre")
    mn = rms_fwd(mem2, p["mem_norm_g"], "mem_norm")
    xq = mm1(hq, W["w_xq"], "nn", BF16, "xq")
    xk = mm1(mn, W["w_xk"], "nn", BF16, "xk")
    xv = mm1(mn, W["w_xv"], "nn", BF16, "xv")
    xo, *arrived = xattn_fwd(xq, xk, xv, nseq, comm=stage.gather("xattn_fwd"))
    stage.gathered("xattn_fwd", arrived, W, p)
    ho, x3 = mm_resid(xo, W["w_xo"], x2, p["xa_post_g"], 1.0, "xo")

    dx4, ffn2, sq_cols = _ffn_fwd(x3, p["ffn2_pre_g"], p["ffn2_post_g"], W, p, "ffn2", stage, target=tgt.reshape(T, D))
    loss_row = (0.5 / D) * jnp.sum(sq_cols.reshape(-1, 128), axis=0, keepdims=True)

    gw, gs = {}, {}
    dx3, gs["ffn2_pre_g"], gs["ffn2_post_g"] = _ffn_bwd(
        dx4, ffn2, p["ffn2_pre_g"], p["ffn2_post_g"], W["ffn2_w_gate"], W["ffn2_w_up"], W["ffn2_w_down"], "ffn2", stage, gw)

    dho, gs["xa_post_g"] = resid_bwd(ho, p["xa_post_g"], dx3, 1.0, "xa_post_bwd")
    dxo = mm1(dho, W["w_xo"], "nt", BF16, "xo_bwd")
    gw["w_xo"] = _tn(xo, dho, "d_w_xo")
    dxq, dxk, dxv = xattn_bwd(xq, xk, xv, dxo, nseq)
    dx2, gs["xa_pre_g"] = mm_rms_bwd([(dxq, W["w_xq"], "nt")], x2, p["xa_pre_g"], "xq_bwd", resid=dx3)
    gw["w_xq"] = _tn(hq, dxq, "d_w_xq")
    dmn = mm([[(dxk, W["w_xk"], "nt"), (dxv, W["w_xv"], "nt")]], [F32], "xkv_bwd")[0]
    gw["w_xk"] = _tn(mn, dxk, "d_w_xk")
    gw["w_xv"] = _tn(mn, dxv, "d_w_xv")
    _, gs["mem_norm_g"] = rms_bwd(mem2, p["mem_norm_g"], dmn, "mem_norm_bwd", dx_dtype=BF16)

    dhmix, gs["mix_post_g"] = resid_bwd(hmix, p["mix_post_g"], dx2, 1.0, "mix_post_bwd")
    dmerged = mm1(dhmix, W["w_out"], "nt", BF16, "mix_out_bwd")
    gw["w_out"] = _tn(merged, dhmix, "d_w_out")
    dys, dym, dgl, gs["gate_bias"] = merge_bwd(gl, y_ssd, y_mla, dmerged, p["gate_bias"], "merge_bwd")

    unslot = lambda g, per: g.reshape(MLA_H, SLOT, -1)[:, :per].reshape(MLA_H * per, -1)
    do_s = mm1(dym, wo_s, "nt", BF16, "mla_proj_bwd")
    gw["w_mla_proj"] = unslot(_tn(o_s, dym, "d_w_mla_proj"), VD)
    dQc, dKc, dv_s, *sent = attn_slot_bwd(Qc, Kc, v_s, o_s, lse, do_s, nseq, comm=stage.scatter("attn_bwd", gw))
    stage.scattered("attn_bwd", sent)
    dq_s, dkn_s, dkr = rope_slot_bwd(dQc, dKc, rope_c, rope_s, "rope_bwd")
    dq_c, gs["q_norm_g"] = mm_rms_bwd([(dq_s, wq_s, "nn")], q_c, p["q_norm_g"], "uq_bwd", dx_dtype=BF16)
    gw["w_uq"] = unslot(_tn(dq_s, qn, "d_w_uq"), QK)
    dkv_c, gs["kv_norm_g"] = mm_rms_bwd([(dkn_s, wk_s, "nn"), (dv_s, wv_s, "nn")], kv_c, p["kv_norm_g"], "ukv_bwd", dx_dtype=BF16)
    gw["w_uk"] = unslot(_tn(dkn_s, kvn, "d_w_uk"), NOPE)
    gw["w_uv"] = unslot(_tn(dv_s, kvn, "d_w_uv"), VD)

    dyn = mm1(dys, W["w_ssd_proj"], "nt", BF16, "ssd_proj_bwd")
    gw["w_ssd_proj"] = _tn(yn, dys, "d_w_ssd_proj")
    dyc, dz, gs["ssd_norm_g"] = gated_norm_bwd(y_ssd_core, z, dyn, p["ssd_norm_g"], "ssd_norm_bwd")
    dxbc_act, ddtr, gs["dt_bias"], gs["a_log"], gs["d_skip"], *sent = ssd_bwd(
        xbc_act, dtkr, p["dt_bias"], p["a_log"], p["d_skip"], prev, dyc, nseq, comm=stage.scatter("ssd_bwd", gw))
    stage.scattered("ssd_bwd", sent)
    dxbc, gs["conv_w"], gs["conv_b"] = conv_bwd(xbc, p["conv_w"], p["conv_b"], dxbc_act, nseq)

    gw["w_in"] = jnp.concatenate([_tn(dz, hm, "d_w_in_z"), _tn(dxbc, hm, "d_w_in_xbc"), _tn(ddtr, hm, "d_w_in_dt")[:SSD_H],
                                  _tn(dq_c, hm, "d_w_in_q"), _tn(dkv_c, hm, "d_w_in_kv"), _tn(dkr, hm, "d_w_in_kr")[:ROPE],
                                  _tn(dgl, hm, "d_w_in_gate")], axis=0)
    dx1, gs["mix_pre_g"], *sent = mm_rms_bwd(
        [(dz, wt_z, "nn"), (dxbc, wt_xbc, "nn"), (ddtr, wt_dt, "nn"), (dq_c, wt_q, "nn"), (dkv_c, wt_kv, "nn"),
         (dkr, wt_kr, "nn"), (dgl, wt_gate, "nn")], x1, p["mix_pre_g"], "in_bwd", resid=dx2, comm=stage.scatter("in_bwd", gw))
    stage.scattered("in_bwd", sent)

    dx0, gs["ffn1_pre_g"], gs["ffn1_post_g"] = _ffn_bwd(
        dx1, ffn1, p["ffn1_pre_g"], p["ffn1_post_g"], W["ffn1_w_gate"], W["ffn1_w_up"], W["ffn1_w_down"], "ffn1", stage, gw)
    return loss_row, dx0.reshape(x.shape), gw, gs


def kernel(x, mem, positions, ffn1_pre_g, ffn1_w_gate, ffn1_w_up, ffn1_w_down, ffn1_post_g, mix_pre_g, w_in, conv_w, conv_b, dt_bias, a_log, d_skip, ssd_norm_g, w_ssd_proj, q_norm_g, w_uq, kv_norm_g, w_uk, w_uv, w_mla_proj, gate_bias, w_out, mix_post_g, xa_pre_g, mem_norm_g, w_xq, w_xk, w_xv, w_xo, xa_post_g, ffn2_pre_g, ffn2_w_gate, ffn2_w_up, ffn2_w_down, ffn2_post_g, loss_target, m_ffn1_pre_g, m_ffn1_w_gate, m_ffn1_w_up, m_ffn1_w_down, m_ffn1_post_g, m_mix_pre_g, m_w_in, m_conv_w, m_conv_b, m_dt_bias, m_a_log, m_d_skip, m_ssd_norm_g, m_w_ssd_proj, m_q_norm_g, m_w_uq, m_kv_norm_g, m_w_uk, m_w_uv, m_w_mla_proj, m_gate_bias, m_w_out, m_mix_post_g, m_xa_pre_g, m_mem_norm_g, m_w_xq, m_w_xk, m_w_xv, m_w_xo, m_xa_post_g, m_ffn2_pre_g, m_ffn2_w_gate, m_ffn2_w_up, m_ffn2_w_down, m_ffn2_post_g, v_ffn1_pre_g, v_ffn1_w_gate, v_ffn1_w_up, v_ffn1_w_down, v_ffn1_post_g, v_mix_pre_g, v_w_in, v_conv_w, v_conv_b, v_dt_bias, v_a_log, v_d_skip, v_ssd_norm_g, v_w_ssd_proj, v_q_norm_g, v_w_uq, v_kv_norm_g, v_w_uk, v_w_uv, v_w_mla_proj, v_gate_bias, v_w_out, v_mix_post_g, v_xa_pre_g, v_mem_norm_g, v_w_xq, v_w_xk, v_w_xv, v_w_xo, v_xa_post_g, v_ffn2_pre_g, v_ffn2_w_gate, v_ffn2_w_up, v_ffn2_w_down, v_ffn2_post_g):
    a = dict(locals())
    w = {n: a[n] for n in WEIGHTS}
    m = {n: a["m_" + n] for n in WEIGHTS}
    v = {n: a["v_" + n] for n in WEIGHTS}

    stage = Stage(w)
    W, p = {}, {n: w[n] for n in SMALL}
    loss_row, grad_x, gw, gs = _local_step(x, mem, positions, loss_target, W, p, stage)

    sm = _pack_small([gs[n] for n in SMALL], loss_row=loss_row, conv_w=gs["conv_w"])
    srecv, = run_comm(ScatterComm([[jnp.broadcast_to(sm[None], (N_DEV,) + sm.shape)]]), "exchange_small")
    s_rows = sum_slots(srecv, "sum_small", tr=sm.shape[0])
    last_piece, = stage.pieces(LAST_EXCHANGE, gw)[0]
    sems, last_piece, landed, token = exchange_start(last_piece, s_rows)
    grads, delta, new_m, new_v = {}, {}, {}, {}
    raw_results = []

    def finish(n, buf, piece, token=None):
        col = KIND[n] == "col"
        turn = (lambda t: t.T) if col else (lambda t: t)
        K = w[n].shape[1]
        if col and buf.shape[2] != K:
            buf = buf.reshape(buf.shape[0], -1, K)
        res = adamw_from_slots(buf, piece, turn(w[n][0]), turn(m[n][0]), turn(v[n][0]), "adamw_" + n, token=token)
        raw_results.append(res[3])
        grads[n], delta[n], new_m[n], new_v[n] = [turn(r)[None] for r in res]

    parts = {}
    for tag, groups in SCATTER_PLAN.items():
        if tag == LAST_EXCHANGE:
            continue
        for names, buf in zip(groups, stage.recv[tag]):
            for piece, n in enumerate(names):
                if n in PARTS:
                    parts[n] = sum_slots(buf, "sum_" + n.replace("#", "_"), tr=buf.shape[1])
                else:
                    finish(n, buf, piece, token)
    for base in sorted({PARTS[pn][0] for pn in parts}):
        rows = jnp.concatenate([parts[pn] for pn in _parts_of(base, "#")], axis=0)
        finish(base, rows[None], 0, token)
    landed = exchange_wait(sems, last_piece, landed, after=raw_results[-1])
    me = _dev_index()
    landed = lax.dynamic_update_index_in_dim(landed, lax.dynamic_index_in_dim(last_piece, me, 0, keepdims=False), me, 0)
    finish(SCATTER_PLAN[LAST_EXCHANGE][0][0], landed, 0)
    conv_w_full = p["conv_w"]
    small = adamw_small(s_rows, [w[n] for n in SMALL], [m[n] for n in SMALL], [v[n] for n in SMALL])
    for t, vals in zip((grads, delta, new_m, new_v), small):
        t.update(zip(SMALL, vals))
    r1 = sum(-(-w[n].shape[1] // 128) for n in SMALL)
    ncw = math.prod(conv_w_full.shape) // 128
    cw_grad_full = s_rows[r1:r1 + ncw].reshape(conv_w_full.shape)
    wsh = conv_w.shape[2]
    grads["conv_w"] = lax.dynamic_slice_in_dim(cw_grad_full, _dev_index() * wsh, wsh, axis=1)[None]
    loss = jnp.sum(s_rows[r1 + ncw])
    d_, m_, v_ = adamw(conv_w[0], grads["conv_w"][0], m["conv_w"][0], v["conv_w"][0], "adamw_conv_w")
    delta["conv_w"], new_m["conv_w"], new_v["conv_w"] = d_[None], m_[None], v_[None]
    return (loss, grad_x, *[grads[n] for n in WEIGHTS], *[delta[n] for n in WEIGHTS],
            *[new_m[n] for n in WEIGHTS], *[new_v[n] for n in WEIGHTS])
```

```python
import functools
import math

import jax
import jax.numpy as jnp
from jax import lax
from jax.experimental import pallas as pl
from jax.experimental.pallas import tpu as pltpu

F32, BF16 = jnp.float32, jnp.bfloat16
MESH = pl.DeviceIdType.MESH
N_DEV = 8

D = 1024
DFF = 2816
SSD_H, SSD_P, SSD_G, SSD_N, SSD_L = 16, 64, 2, 128, 128
SSD_INNER = SSD_H * SSD_P
CONV_K, CONV_CH = 4, 1536
MLA_H, QR, KVR, NOPE, ROPE, VD = 16, 384, 256, 64, 32, 64
QK = NOPE + ROPE
ROPE_THETA = 10000.0
XA_H, XA_D = 4, 256
EPS = 1e-6
FFN_RES = 0.5
LR, B1, B2, AEPS, WD, STEP = 0.001, 0.9, 0.999, 1e-08, 0.01, 10

VMEM_LIMIT = 56 * 2**20


def _cp(*sem):
    return pltpu.CompilerParams(dimension_semantics=sem, vmem_limit_bytes=VMEM_LIMIT)


def _sigmoid(x):
    return 1.0 / (1.0 + jnp.exp(-x))


def _softplus(x):
    return jnp.where(x > 20.0, x, jnp.log(1.0 + jnp.exp(jnp.minimum(x, 20.0))))


def _dot(a, b, dims="nn"):
    ca = 0 if dims[0] == "t" else 1
    cb = 1 if dims[1] == "t" else 0
    return lax.dot_general(a.astype(BF16), b.astype(BF16), (((ca,), (cb,)), ((), ())), preferred_element_type=F32)


def _dot_sel(a, b, dims="nn", split="a", terms=3):
    r = (a if split == "a" else b).astype(F32)
    out = None
    for t in range(terms):
        piece = r.astype(BF16)
        if t + 1 < terms:
            r = r - piece.astype(F32)
        d = _dot(piece, b, dims) if split == "a" else _dot(a, piece, dims)
        out = d if out is None else out + d
    return out


def _ssd_common(dtr, dtb, alog):
    L = dtr.shape[0]
    dt = _softplus(dtr + dtb)
    a = -jnp.exp(alog)
    adt = dt * a
    r = lax.broadcasted_iota(jnp.int32, (L, L), 0)
    c = lax.broadcasted_iota(jnp.int32, (L, L), 1)
    lower = r >= c
    tri = lower.astype(F32)
    cs = _dot_sel(tri, adt, "nn", split="b")
    cs_t = _dot_sel(adt, tri, "tt")
    return dt, a, cs, cs_t, lower


def _head_expand():
    hh = lax.broadcasted_iota(jnp.int32, (SSD_H, SSD_INNER), 0)
    jj = lax.broadcasted_iota(jnp.int32, (SSD_H, SSD_INNER), 1)
    return ((jj >= hh * SSD_P) & (jj < hh * SSD_P + SSD_P)).astype(F32)


def _head_reduce():
    hh = lax.broadcasted_iota(jnp.int32, (SSD_INNER, SSD_H), 1)
    jj = lax.broadcasted_iota(jnp.int32, (SSD_INNER, SSD_H), 0)
    return ((jj >= hh * SSD_P) & (jj < hh * SSD_P + SSD_P)).astype(F32)


def ssd_fwd(xbc, dtr, dtb, alog, dsk, nseq, comm=None):
    T = xbc.shape[0]
    S = T // nseq
    C = S // SSD_L
    L = SSD_L
    NP = SSD_H // 2

    def body(x_ref, b_ref, c_ref, dtr_ref, dtb_ref, alog_ref, dsk_ref, y_ref, prev_ref, st_ref):
        ci = pl.program_id(1)

        @pl.when(ci == 0)
        def _():
            st_ref[...] = jnp.zeros_like(st_ref)

        dt, a, cs, cs_t, lower = _ssd_common(dtr_ref[:, 0:SSD_H], dtb_ref[...], alog_ref[...])
        E = _head_expand()
        X = x_ref[...].astype(F32)
        dt_e = _dot_sel(dt, E)
        cs_e = _dot_sel(cs, E)
        csl_e = cs_e[L - 1:L, :]
        Xd = X * dt_e
        Xf = Xd * jnp.exp(csl_e - cs_e)
        e_e = jnp.exp(cs_e)
        skip = _dot_sel(dsk_ref[...], E) * X
        lane = lax.broadcasted_iota(jnp.int32, (1, 2 * SSD_P), 1)
        rowp = lax.broadcasted_iota(jnp.int32, (2 * SSD_P, 1), 0)
        for g in range(SSD_G):
            Bg = b_ref[:, g * SSD_N:(g + 1) * SSD_N]
            Cg = c_ref[:, g * SSD_N:(g + 1) * SSD_N]
            cb = _dot(Cg, Bg, "nt")
            for pp in range(NP // SSD_G):
                p = g * (NP // SSD_G) + pp
                sl = slice(p * 2 * SSD_P, (p + 1) * 2 * SSD_P)
                Xd_p = Xd[:, sl]
                yd = jnp.zeros((L, 2 * SSD_P), F32)
                for q in range(2):
                    h = 2 * p + q
                    m = jnp.where(lower, jnp.exp(jnp.minimum(cs[:, h:h + 1] - cs_t[h:h + 1, :], 0.0)), 0.0)
                    mask = (lane >= q * SSD_P) & (lane < (q + 1) * SSD_P)
                    yd = yd + _dot(cb * m, jnp.where(mask, Xd_p, 0.0))
                S0 = st_ref[p]
                prev_ref[0, 0, p] = S0
                z = _dot(Cg, S0, "nt")
                y_ref[:, sl] = (skip[:, sl] + yd + z * e_e[:, sl]).astype(y_ref.dtype)
                h0 = 2 * p
                dec = jnp.where(rowp < SSD_P, jnp.exp(cs[L - 1:L, h0:h0 + 1]), jnp.exp(cs[L - 1:L, h0 + 1:h0 + 2]))
                st_ref[p] = S0 * dec + _dot(Xf[:, sl], Bg, "tn")

    row = lambda b, c: (b * C + c, 0)
    small = pl.BlockSpec((1, SSD_H), lambda b, c: (0, 0))
    return _call_with_comm(
        body, (nseq, C), "ssd_fwd",
        [pl.BlockSpec((L, SSD_INNER), row),
         pl.BlockSpec((L, SSD_G * SSD_N), lambda b, c: (b * C + c, SSD_INNER // (SSD_G * SSD_N))),
         pl.BlockSpec((L, SSD_G * SSD_N), lambda b, c: (b * C + c, SSD_INNER // (SSD_G * SSD_N) + 1)),
         pl.BlockSpec((L, 128), row), small, small, small],
        [xbc, xbc, xbc, dtr, dtb, alog, dsk],
        [pl.BlockSpec((L, SSD_INNER), row), pl.BlockSpec((1, 1, NP, 2 * SSD_P, SSD_N), lambda b, c: (b, c, 0, 0, 0))],
        [jax.ShapeDtypeStruct((T, SSD_INNER), BF16), jax.ShapeDtypeStruct((nseq, C, NP, 2 * SSD_P, SSD_N), F32)],
        comm, scratch=[pltpu.VMEM((NP, 2 * SSD_P, SSD_N), F32)], sem=("parallel", "arbitrary"))


def ssd_bwd(xbc, dtr, dtb, alog, dsk, prev, dy, nseq, comm=None):
    T = xbc.shape[0]
    S = T // nseq
    C = S // SSD_L
    L = SSD_L
    NP = SSD_H // 2

    def body(x_ref, b_ref, c_ref, dtr_ref, dtb_ref, alog_ref, dsk_ref, prev_ref, dy_ref,
             dxbc_ref, ddtr_ref, ddtb_ref, dalog_ref, ddsk_ref, ds_ref, stg_ref):
        bi = pl.program_id(0)
        ci = pl.program_id(1)

        @pl.when(ci == 0)
        def _():
            ds_ref[...] = jnp.zeros_like(ds_ref)

        @pl.when((ci == 0) & (bi == 0))
        def _():
            ddtb_ref[...] = jnp.zeros_like(ddtb_ref)
            dalog_ref[...] = jnp.zeros_like(dalog_ref)
            ddsk_ref[...] = jnp.zeros_like(ddsk_ref)

        dtr = dtr_ref[:, 0:SSD_H]
        dtb = dtb_ref[...]
        dt, a, cs, cs_t, lower = _ssd_common(dtr, dtb, alog_ref[...])
        upper = lax.broadcasted_iota(jnp.int32, (L, L), 1) >= lax.broadcasted_iota(jnp.int32, (L, L), 0)
        E = _head_expand()
        ET = _head_reduce()
        X = x_ref[...].astype(F32)
        dY = dy_ref[...].astype(F32)
        dt_e = _dot_sel(dt, E)
        cs_e = _dot_sel(cs, E)
        csl_e = cs_e[L - 1:L, :]
        f_e = jnp.exp(csl_e - cs_e)
        e_e = jnp.exp(cs_e)
        dsk_e = _dot_sel(dsk_ref[...], E)
        Xd = X * dt_e
        Xf = Xd * f_e
        lane = lax.broadcasted_iota(jnp.int32, (1, 2 * SSD_P), 1)
        rowp = lax.broadcasted_iota(jnp.int32, (2 * SSD_P, 1), 0)
        hsel = lax.broadcasted_iota(jnp.int32, (1, SSD_H), 1)
        dcs = jnp.zeros((L, SSD_H), F32)
        dcsl = jnp.zeros((1, SSD_H), F32)
        for g in range(SSD_G):
            Bg = b_ref[:, g * SSD_N:(g + 1) * SSD_N]
            Cg = c_ref[:, g * SSD_N:(g + 1) * SSD_N]
            cb = _dot(Cg, Bg, "nt")
            cbt = _dot(Bg, Cg, "nt")
            dB = jnp.zeros((L, SSD_N), F32)
            dC = jnp.zeros((L, SSD_N), F32)
            for pp in range(NP // SSD_G):
                p = g * (NP // SSD_G) + pp
                sl = slice(p * 2 * SSD_P, (p + 1) * 2 * SSD_P)
                Xd_p = Xd[:, sl]
                dY_p = dY[:, sl]
                dXd_p = jnp.zeros((L, 2 * SSD_P), F32)
                for q in range(2):
                    h = 2 * p + q
                    mask = (lane >= q * SSD_P) & (lane < (q + 1) * SSD_P)
                    col = cs[:, h:h + 1]
                    rw = cs_t[h:h + 1, :]
                    m = jnp.where(lower, jnp.exp(jnp.minimum(col - rw, 0.0)), 0.0)
                    mt = jnp.where(upper, jnp.exp(jnp.minimum(rw - col, 0.0)), 0.0)
                    dYm = jnp.where(mask, dY_p, 0.0)
                    dW = _dot(dYm, Xd_p, "nt")
                    dWt = _dot(Xd_p, dYm, "nt")
                    w = cb * m
                    wt = cbt * mt
                    dC = dC + _dot(dW * m, Bg)
                    dB = dB + _dot(dWt * mt, Cg)
                    dXd_p = dXd_p + jnp.where(mask, _dot(wt, dY_p), 0.0)
                    qcol = jnp.sum(dW * w, axis=1, keepdims=True) - jnp.sum(dWt * wt, axis=1, keepdims=True)
                    dcs = dcs + qcol * (hsel == h).astype(F32)
                S0 = prev_ref[0, 0, p]
                dSn = ds_ref[p]
                dZ = dY_p * e_e[:, sl]
                dC = dC + _dot(dZ, S0)
                h0 = 2 * p
                el0 = jnp.exp(cs[L - 1:L, h0:h0 + 1])
                el1 = jnp.exp(cs[L - 1:L, h0 + 1:h0 + 2])
                dec = jnp.where(rowp < SSD_P, el0, el1)
                ds_ref[p] = dSn * dec + _dot(dZ, Cg, "tn")
                dXf_p = _dot(Bg, dSn, "nt")
                dB = dB + _dot(Xf[:, sl], dSn)
                rs = jnp.sum(dSn * S0, axis=1, keepdims=True)
                s0 = jnp.sum(jnp.where(rowp < SSD_P, rs, 0.0), axis=0, keepdims=True) * el0
                s1 = jnp.sum(jnp.where(rowp >= SSD_P, rs, 0.0), axis=0, keepdims=True) * el1
                dcsl = dcsl + s0 * (hsel == h0).astype(F32) + s1 * (hsel == h0 + 1).astype(F32)
                y_off = _dot(Cg, S0, "nt") * e_e[:, sl]
                t1 = dY_p * y_off - dXf_p * Xf[:, sl]
                r1 = jnp.where(lane < SSD_P, t1, 0.0)
                c0 = jnp.sum(r1, axis=1, keepdims=True)
                c1 = jnp.sum(t1 - r1, axis=1, keepdims=True)
                dcs = dcs + c0 * (hsel == h0).astype(F32) + c1 * (hsel == h0 + 1).astype(F32)
                t2 = dXf_p * Xf[:, sl]
                r2 = jnp.where(lane < SSD_P, t2, 0.0)
                dcsl = dcsl + jnp.sum(r2, keepdims=True) * (hsel == h0).astype(F32) \
                    + jnp.sum(t2 - r2, keepdims=True) * (hsel == h0 + 1).astype(F32)
                stg_ref[:, sl] = dXd_p + dXf_p * f_e[:, sl]
            dxbc_ref[:, SSD_INNER + g * SSD_N:SSD_INNER + (g + 1) * SSD_N] = dB.astype(dxbc_ref.dtype)
            dxbc_ref[:, SSD_INNER + (SSD_G + g) * SSD_N:SSD_INNER + (SSD_G + g + 1) * SSD_N] = dC.astype(dxbc_ref.dtype)
        dXd = stg_ref[...]
        dxbc_ref[:, 0:SSD_INNER] = (dXd * dt_e + dsk_e * dY).astype(dxbc_ref.dtype)
        rowl = lax.broadcasted_iota(jnp.int32, (L, 1), 0)
        dcs = dcs + jnp.where(rowl == L - 1, dcsl, 0.0)
        dalpha = _dot_sel(upper.astype(F32), dcs, split="b")
        ddt = _dot_sel(dXd * X, ET, terms=2) + dalpha * a
        dalog_ref[...] += jnp.sum(dalpha * dt, axis=0, keepdims=True) * a
        ddtr = ddt * _sigmoid(dtr + dtb)
        spread = (lax.broadcasted_iota(jnp.int32, (SSD_H, 128), 0) == lax.broadcasted_iota(jnp.int32, (SSD_H, 128), 1)).astype(F32)
        ddtr_ref[...] = _dot(ddtr, spread).astype(ddtr_ref.dtype)
        ddtb_ref[...] += jnp.sum(ddtr, axis=0, keepdims=True)
        ddsk_ref[...] += jnp.sum(_dot_sel(dY * X, ET, terms=2), axis=0, keepdims=True)

    rowr = lambda b, c: (b * C + (C - 1 - c), 0)
    small = pl.BlockSpec((1, SSD_H), lambda b, c: (0, 0))
    return _call_with_comm(
        body, (nseq, C), "ssd_bwd",
        [pl.BlockSpec((L, SSD_INNER), rowr),
         pl.BlockSpec((L, SSD_G * SSD_N), lambda b, c: (b * C + (C - 1 - c), SSD_INNER // (SSD_G * SSD_N))),
         pl.BlockSpec((L, SSD_G * SSD_N), lambda b, c: (b * C + (C - 1 - c), SSD_INNER // (SSD_G * SSD_N) + 1)),
         pl.BlockSpec((L, 128), rowr), small, small, small,
         pl.BlockSpec((1, 1, NP, 2 * SSD_P, SSD_N), lambda b, c: (b, C - 1 - c, 0, 0, 0)),
         pl.BlockSpec((L, SSD_INNER), rowr)],
        [xbc, xbc, xbc, dtr, dtb, alog, dsk, prev, dy],
        [pl.BlockSpec((L, CONV_CH), rowr), pl.BlockSpec((L, 128), rowr), small, small, small],
        [jax.ShapeDtypeStruct((T, CONV_CH), BF16), jax.ShapeDtypeStruct((T, 128), BF16),
         jax.ShapeDtypeStruct((1, SSD_H), F32), jax.ShapeDtypeStruct((1, SSD_H), F32), jax.ShapeDtypeStruct((1, SSD_H), F32)],
        comm, scratch=[pltpu.VMEM((NP, 2 * SSD_P, SSD_N), F32), pltpu.VMEM((L, SSD_INNER), F32)], sem=("arbitrary", "arbitrary"))


SLOT = 128
ATT_T = 512
ATT_HP = 1
LOG2E = math.log2(math.e)
Q_SCALE = QK ** -0.5 * LOG2E


def _col_to_row(col):
    n = col.shape[0]
    eye = lax.broadcasted_iota(jnp.int32, (n, n), 0) == lax.broadcasted_iota(jnp.int32, (n, n), 1)
    return jnp.sum(jnp.where(eye, col, 0.0), axis=0, keepdims=True)


def attn_slot_fwd(q, k, v, nseq, comm=None):
    T = q.shape[0]
    S = T // nseq
    t = min(ATT_T, S)
    nb = S // t
    cols = [slice(h * SLOT, (h + 1) * SLOT) for h in range(ATT_HP)]

    def body(q_ref, k_ref, v_ref, o_ref, lse_ref):
        causal = lax.broadcasted_iota(jnp.int32, (t, t), 1) <= lax.broadcasted_iota(jnp.int32, (t, t), 0)
        for qi in range(nb):
            rows = slice(qi * t, (qi + 1) * t)
            state = [None] * ATT_HP
            for kj in range(qi + 1):
                keys = slice(kj * t, (kj + 1) * t)
                for h, c in enumerate(cols):
                    s = _dot(q_ref[rows, c], k_ref[keys, c], "nt")
                    if kj == qi:
                        s = jnp.where(causal, s, -1e30)
                    bm = jnp.max(s, axis=1, keepdims=True)
                    if kj == 0:
                        p = jnp.exp2(s - bm)
                        state[h] = (bm, jnp.sum(p, axis=1, keepdims=True), _dot(p, v_ref[keys, c]))
                    else:
                        m, l, acc = state[h]
                        m_new = jnp.maximum(m, bm)
                        corr = jnp.exp2(m - m_new)
                        p = jnp.exp2(s - m_new)
                        state[h] = (m_new, l * corr + jnp.sum(p, axis=1, keepdims=True), acc * corr + _dot(p, v_ref[keys, c]))
            for h, c in enumerate(cols):
                m, l, acc = state[h]
                o_ref[rows, c] = (acc / l).astype(o_ref.dtype)
                lse_ref[0, h, :, rows] = _col_to_row(m + jnp.log2(l))

    blk = pl.BlockSpec((S, ATT_HP * SLOT), lambda b, h: (b, h))
    return _call_with_comm(
        body, (nseq, MLA_H // ATT_HP), "attn_fwd", [blk, blk, blk], [q, k, v],
        [blk, pl.BlockSpec((1, ATT_HP, 1, S), lambda b, h: (b, h, 0, 0))],
        [jax.ShapeDtypeStruct((T, MLA_H * SLOT), BF16), jax.ShapeDtypeStruct((nseq, MLA_H, 1, S), F32)], comm)


def attn_slot_bwd(q, k, v, o, lse, do, nseq, comm=None):
    T = q.shape[0]
    S = T // nseq
    t = min(ATT_T, S)
    nb = S // t
    scale = QK ** -0.5
    cols = [slice(h * SLOT, (h + 1) * SLOT) for h in range(ATT_HP)]

    def body(q_ref, k_ref, v_ref, o_ref, lse_ref, do_ref, dq_ref, dk_ref, dv_ref, dqa_ref):
        causal_t = lax.broadcasted_iota(jnp.int32, (t, t), 0) <= lax.broadcasted_iota(jnp.int32, (t, t), 1)
        ones = jnp.ones((8, SLOT), F32)
        delta = {}
        for qi in range(nb):
            sl = slice(qi * t, (qi + 1) * t)
            for h, c in enumerate(cols):
                prod = do_ref[sl, c].astype(F32) * o_ref[sl, c].astype(F32)
                delta[h, qi] = _dot_sel(ones, prod, "nt", split="b", terms=2)[0:1, :]
        for kj in range(nb):
            ks = slice(kj * t, (kj + 1) * t)
            dk = [None] * ATT_HP
            dv = [None] * ATT_HP
            for qi in range(kj, nb):
                sl = slice(qi * t, (qi + 1) * t)
                for h, c in enumerate(cols):
                    kb, vb, qb, dob = k_ref[ks, c], v_ref[ks, c], q_ref[sl, c], do_ref[sl, c]
                    st = _dot(kb, qb, "nt")
                    pt = jnp.exp2(st - lse_ref[0, h, :, sl])
                    if qi == kj:
                        pt = jnp.where(causal_t, pt, 0.0)
                    dpt = _dot(vb, dob, "nt")
                    dst = (pt * (dpt - delta[h, qi])).astype(BF16)
                    dvc = _dot(pt, dob)
                    dkc = _dot(dst, qb) * (1.0 / LOG2E)
                    dv[h] = dvc if dv[h] is None else dv[h] + dvc
                    dk[h] = dkc if dk[h] is None else dk[h] + dkc
                    dqc = _dot(dst, kb, "tn") * scale
                    if kj > 0:
                        dqc = dqc + dqa_ref[sl, c]
                    if qi == kj:
                        dq_ref[sl, c] = dqc.astype(dq_ref.dtype)
                    else:
                        dqa_ref[sl, c] = dqc
            for h, c in enumerate(cols):
                dk_ref[ks, c] = dk[h].astype(dk_ref.dtype)
                dv_ref[ks, c] = dv[h].astype(dv_ref.dtype)

    blk = pl.BlockSpec((S, ATT_HP * SLOT), lambda b, h: (b, h))
    lse_spec = pl.BlockSpec((1, ATT_HP, 1, S), lambda b, h: (b, h, 0, 0))
    W = MLA_H * SLOT
    return _call_with_comm(
        body, (nseq, MLA_H // ATT_HP), "attn_bwd", [blk, blk, blk, blk, lse_spec, blk], [q, k, v, o, lse, do], [blk, blk, blk],
        [jax.ShapeDtypeStruct((T, W), BF16)] * 3, comm, scratch=[pltpu.VMEM((S, ATT_HP * SLOT), F32)])


def _rope_coeffs(pos, inv):
    half = ROPE // 2
    ang = pos * inv
    lane = lax.broadcasted_iota(jnp.int32, (1, SLOT), 1)
    sn = jnp.sin(ang)
    C = jnp.where(lane < NOPE, 1.0, jnp.where(lane < QK, jnp.cos(ang), 0.0))
    Sg = jnp.where((lane >= NOPE) & (lane < NOPE + half), -sn, jnp.where((lane >= NOPE + half) & (lane < QK), sn, 0.0))
    return C, Sg


def _rope_inputs(positions):
    half = ROPE // 2
    inv = ROPE_THETA ** (-jnp.arange(0, ROPE, 2, dtype=F32) / ROPE)
    row = jnp.zeros((1, SLOT), F32).at[0, NOPE:NOPE + half].set(inv).at[0, NOPE + half:QK].set(inv)
    return positions.astype(F32).reshape(-1, 1), row


def _place_k_rope(kr_lanes):
    r = lax.broadcasted_iota(jnp.int32, (SLOT, SLOT), 0)
    c = lax.broadcasted_iota(jnp.int32, (SLOT, SLOT), 1)
    return _dot_sel(kr_lanes, ((c == r + NOPE) & (r < ROPE)).astype(F32))


def rope_table(pos, inv):
    return rowwise(_rope_coeffs, [pos], [inv], [(SLOT, F32), (SLOT, F32)], [], "rope_table")


def rope_q_epilogue(accs, ex):
    C, Sg = ex[0], ex[1]
    reps = accs[0].shape[1] // SLOT
    return ((accs[0] * jnp.tile(C, (1, reps)) + _rope_swap(accs[0]) * jnp.tile(Sg, (1, reps))) * Q_SCALE,)


def rope_k_epilogue(accs, ex):
    C, Sg = ex[0], ex[1]
    kr = _place_k_rope(ex[2][:, SLOT:2 * SLOT])
    kr = kr * C + _rope_swap(kr) * Sg
    return (accs[0] + jnp.tile(kr, (1, accs[0].shape[1] // SLOT)),)


def _rope_swap(x):
    W = x.shape[1]
    half = ROPE // 2
    lane = lax.broadcasted_iota(jnp.int32, (1, W), 1) & (SLOT - 1)
    up = pltpu.roll(x, W - half, axis=1)
    dn = pltpu.roll(x, half, axis=1)
    return jnp.where((lane >= NOPE) & (lane < NOPE + half), up, jnp.where((lane >= NOPE + half) & (lane < QK), dn, 0.0))


def rope_slot_bwd(dq, dk, C, Sg, name):
    def fn(dqv, dkv, C, Sg):
        ct, stl = jnp.tile(C, (1, MLA_H)), jnp.tile(Sg, (1, MLA_H))
        dqo = dqv * ct - _rope_swap(dqv) * stl
        tot = dkv[:, 0:SLOT]
        for h in range(1, MLA_H):
            tot = tot + dkv[:, h * SLOT:(h + 1) * SLOT]
        u = tot * C - _rope_swap(tot) * Sg
        r = lax.broadcasted_iota(jnp.int32, (SLOT, SLOT), 0)
        c = lax.broadcasted_iota(jnp.int32, (SLOT, SLOT), 1)
        unplace = ((r == c + NOPE) & (c < ROPE)).astype(F32)
        return dqo, dkv, _dot_sel(u, unplace, terms=2)
    W = MLA_H * SLOT
    return rowwise(fn, [dq, dk, C, Sg], [], [(W, BF16), (W, BF16), (SLOT, BF16)], [], name)


XA_BLK = 512


def xattn_fwd(q, k, v, nseq, comm=None):
    T = q.shape[0]
    S = T // nseq
    M = k.shape[0] // nseq
    tq = min(XA_BLK, S)
    nq = S // tq
    scale = XA_D ** -0.5

    def body(q_ref, k_ref, v_ref, o_ref):
        s = _dot(q_ref[...], k_ref[...], "nt") * scale
        p = jnp.exp(s - jnp.max(s, axis=1, keepdims=True))
        p = p / jnp.sum(p, axis=1, keepdims=True)
        o_ref[...] = _dot(p, v_ref[...]).astype(o_ref.dtype)

    qs = pl.BlockSpec((tq, XA_D), lambda b, h, i: (b * nq + i, h))
    ks = pl.BlockSpec((M, XA_D), lambda b, h, i: (b, h))
    return _call_with_comm(body, (nseq, XA_H, nq), "xattn_fwd", [qs, ks, ks], [q, k, v], [qs],
                           [jax.ShapeDtypeStruct((T, XA_H * XA_D), BF16)], comm)


def xattn_bwd(q, k, v, do, nseq):
    T = q.shape[0]
    S = T // nseq
    M = k.shape[0] // nseq
    tq = min(XA_BLK, S)
    nq = S // tq
    scale = XA_D ** -0.5

    def body(q_ref, k_ref, v_ref, do_ref, dq_ref, dk_ref, dv_ref):
        @pl.when(pl.program_id(2) == 0)
        def _():
            dk_ref[...] = jnp.zeros_like(dk_ref)
            dv_ref[...] = jnp.zeros_like(dv_ref)

        qb, kb, vb, dob = q_ref[...], k_ref[...], v_ref[...], do_ref[...]
        s = _dot(qb, kb, "nt") * scale
        p = jnp.exp(s - jnp.max(s, axis=1, keepdims=True))
        p = p / jnp.sum(p, axis=1, keepdims=True)
        dp = _dot(dob, vb, "nt")
        ds = p * (dp - jnp.sum(dp * p, axis=1, keepdims=True)) * scale
        dq_ref[...] = _dot(ds, kb).astype(dq_ref.dtype)
        dk_ref[...] += _dot(ds, qb, "tn")
        dv_ref[...] += _dot(p, dob, "tn")

    qs = pl.BlockSpec((tq, XA_D), lambda b, h, i: (b * nq + i, h))
    ks = pl.BlockSpec((M, XA_D), lambda b, h, i: (b, h))
    return pl.pallas_call(
        body, grid=(nseq, XA_H, nq), name="xattn_bwd", in_specs=[qs, ks, ks, qs], out_specs=[qs, ks, ks],
        out_shape=[jax.ShapeDtypeStruct((T, XA_H * XA_D), BF16), jax.ShapeDtypeStruct(k.shape, F32),
                   jax.ShapeDtypeStruct(k.shape, F32)],
        compiler_params=_cp("parallel", "parallel", "arbitrary"),
    )(q, k, v, do)


CONV_BLK = 256


def _shift_down(x, s, rows):
    if s == 0:
        return x
    return jnp.where(rows >= s, pltpu.roll(x, s, axis=0), 0.0)


def _shift_up(x, s, rows):
    if s == 0:
        return x
    S = x.shape[0]
    return jnp.where(rows < S - s, pltpu.roll(x, S - s, axis=0), 0.0)


def conv_fwd(x, w, b, nseq):
    T, CH = x.shape
    S = T // nseq

    def body(x_ref, w_ref, b_ref, o_ref):
        xv = x_ref[...].astype(F32)
        rows = lax.broadcasted_iota(jnp.int32, (S, 1), 0)
        c = jnp.zeros_like(xv) + b_ref[...]
        for kk in range(CONV_K):
            c = c + w_ref[kk:kk + 1, :] * _shift_down(xv, CONV_K - 1 - kk, rows)
        o_ref[...] = (c * _sigmoid(c)).astype(o_ref.dtype)

    xs = pl.BlockSpec((S, CONV_BLK), lambda j, bb: (bb, j))
    return pl.pallas_call(
        body, grid=(CH // CONV_BLK, nseq), name="conv_fwd",
        in_specs=[xs, pl.BlockSpec((CONV_K, CONV_BLK), lambda j, bb: (0, j)), pl.BlockSpec((1, CONV_BLK), lambda j, bb: (0, j))],
        out_specs=xs, out_shape=jax.ShapeDtypeStruct((T, CH), BF16),
        compiler_params=_cp("parallel", "parallel"),
    )(x, w, b)


def conv_bwd(x, w, b, dout, nseq):
    T, CH = x.shape
    S = T // nseq

    def body(x_ref, w_ref, b_ref, do_ref, dx_ref, dw_ref, db_ref):
        @pl.when(pl.program_id(1) == 0)
        def _():
            dw_ref[...] = jnp.zeros_like(dw_ref)
            db_ref[...] = jnp.zeros_like(db_ref)

        xv = x_ref[...].astype(F32)
        rows = lax.broadcasted_iota(jnp.int32, (S, 1), 0)
        c = jnp.zeros_like(xv) + b_ref[...]
        sh = [_shift_down(xv, CONV_K - 1 - kk, rows) for kk in range(CONV_K)]
        for kk in range(CONV_K):
            c = c + w_ref[kk:kk + 1, :] * sh[kk]
        sg = _sigmoid(c)
        dc = do_ref[...].astype(F32) * sg * (1.0 + c * (1.0 - sg))
        dx = jnp.zeros_like(xv)
        for kk in range(CONV_K):
            dx = dx + w_ref[kk:kk + 1, :] * _shift_up(dc, CONV_K - 1 - kk, rows)
            dw_ref[kk:kk + 1, :] += jnp.sum(dc * sh[kk], axis=0, keepdims=True)
        dx_ref[...] = dx.astype(dx_ref.dtype)
        db_ref[...] += jnp.sum(dc, axis=0, keepdims=True)

    xs = pl.BlockSpec((S, CONV_BLK), lambda j, bb: (bb, j))
    ws = pl.BlockSpec((CONV_K, CONV_BLK), lambda j, bb: (0, j))
    bs = pl.BlockSpec((1, CONV_BLK), lambda j, bb: (0, j))
    return pl.pallas_call(
        body, grid=(CH // CONV_BLK, nseq), name="conv_bwd",
        in_specs=[xs, ws, bs, xs], out_specs=[xs, ws, bs],
        out_shape=[jax.ShapeDtypeStruct((T, CH), BF16), jax.ShapeDtypeStruct((CONV_K, CH), F32),
                   jax.ShapeDtypeStruct((1, CH), F32)],
        compiler_params=_cp("parallel", "arbitrary"),
    )(x, w, b, dout)


def _dims(a, b, mode):
    M = a.shape[1] if mode[0] == "t" else a.shape[0]
    K = a.shape[0] if mode[0] == "t" else a.shape[1]
    N = b.shape[0] if mode[1] == "t" else b.shape[1]
    return M, K, N


def _tile(dim, prefs):
    for p in prefs:
        if dim % p == 0:
            return p
    return dim


def mm(groups, out_dtypes, name, tm=None, tn=None, tk=None, epi=None, extras=(), comm=None, sub=1, n_sum=0):
    a0, b0, m0 = groups[0][0]
    M, K0, N = _dims(a0, b0, m0)
    tm = tm or _tile(M, (1024, 512, 256, 128))
    tn = tn or _tile(N, (1024, 512, 256, 128))
    flat = [p for g in groups for p in g]
    nk = 1 if tk is None else K0 // tk
    in_specs, args = [], []
    for a, b, mode in flat:
        _, K, _ = _dims(a, b, mode)
        kb = K if tk is None else tk
        in_specs.append(pl.BlockSpec((kb, tm), lambda i, j, k: (k, i)) if mode[0] == "t"
                        else pl.BlockSpec((tm, kb), lambda i, j, k: (i, k)))
        in_specs.append(pl.BlockSpec((tn, kb), lambda i, j, k: (j, k)) if mode[1] == "t"
                        else pl.BlockSpec((kb, tn), lambda i, j, k: (k, j)))
        args += [a, b]
    kinds = []
    for e in extras:
        kind, e = e if isinstance(e, tuple) else ("vec" if e.shape[0] == 1 and M != 1 else "tile", e)
        in_specs.append({"tile": pl.BlockSpec((tm, tn), lambda i, j, k: (i, j)),
                         "vec": pl.BlockSpec((1, tn), lambda i, j, k: (0, j)),
                         "rows": pl.BlockSpec((tm, e.shape[1]), lambda i, j, k: (i, 0)),
                         "whole": pl.BlockSpec(e.shape, lambda i, j, k: (0, 0))}[kind])
        kinds.append(kind)
        args.append(e)
    n_in = len(args)
    n_main = len(out_dtypes)
    n_out = n_main + n_sum
    assert n_sum == 0 or (tn == N and tk is None)
    ng = len(groups)
    sizes = [len(g) for g in groups]

    def body(*refs):
        ins, outs, accs = refs[:n_in], refs[n_in:n_in + n_out], refs[n_in + n_out:]
        kk = pl.program_id(2)

        def dots(rs):
            vals, pos = [], 0
            for gi in range(ng):
                acc = None
                for _ in range(sizes[gi]):
                    mode = flat[pos // 2][2]
                    av = ins[pos][:, rs] if mode[0] == "t" else ins[pos][rs, :]
                    d = _dot(av, ins[pos + 1][...], mode)
                    acc = d if acc is None else acc + d
                    pos += 2
                vals.append(acc)
            return vals

        def finish(accv, rs, first_chunk=True):
            ex = [(r[rs, :] if kind in ("tile", "rows") else r[...]).astype(F32) for kind, r in zip(kinds, ins[2 * len(flat):])]
            res = epi(accv, ex) if epi is not None else tuple(accv)
            for o, r in zip(outs[:n_main], res[:n_main]):
                o[rs, :] = r.astype(o.dtype)
            for o, r in zip(outs[n_main:], res[n_main:]):
                if first_chunk:
                    @pl.when(pl.program_id(0) == 0)
                    def _():
                        o[...] = r

                    @pl.when(pl.program_id(0) > 0)
                    def _():
                        o[...] += r
                else:
                    o[...] += r

        if nk == 1:
            for r in range(sub):
                rs = slice(r * (tm // sub), (r + 1) * (tm // sub))
                finish(dots(rs), rs, r == 0)
        else:
            vals = dots(slice(0, tm))
            finish = functools.partial(finish, rs=slice(0, tm))
            @pl.when(kk == 0)
            def _():
                for ar, vv in zip(accs, vals):
                    ar[...] = vv

            @pl.when(kk > 0)
            def _():
                for ar, vv in zip(accs, vals):
                    ar[...] += vv

            @pl.when(kk == nk - 1)
            def _():
                finish([ar[...] for ar in accs])

    grid = (M // tm, N // tn, nk)
    out_specs = [pl.BlockSpec((tm, tn), lambda i, j, k: (i, j)) for _ in out_dtypes] \
        + [pl.BlockSpec((1, tn), lambda i, j, k: (0, j))] * n_sum
    out_shape = [jax.ShapeDtypeStruct((M, N), dt) for dt in out_dtypes] + [jax.ShapeDtypeStruct((1, N), F32)] * n_sum
    scratch = [pltpu.VMEM((tm, tn), F32) for _ in range(ng if nk > 1 else 0)]
    sem = ("arbitrary" if n_sum else "parallel", "parallel", "arbitrary")
    if comm is not None:
        body = _attach(comm, body, n_in, n_out, *_grid_ends(grid))
        in_specs, args = in_specs + [HBM_SPEC] * len(comm.inputs), args + comm.inputs
        out_specs, out_shape = out_specs + [HBM_SPEC] * len(comm.out_shapes), out_shape + comm.out_shapes
        scratch, sem = scratch + comm.sems, ("arbitrary",) * 3
    return pl.pallas_call(body, grid=grid, name=name, in_specs=in_specs, out_specs=out_specs, out_shape=out_shape,
                          scratch_shapes=scratch, compiler_params=_cp(*sem))(*args)


def mm1(a, b, mode, out_dtype, name, **kw):
    return mm([[(a, b, mode)]], [out_dtype], name, **kw)[0]


ROW_BLK = 512


def rowwise(fn, rows, consts, outs, accs, name, tb=ROW_BLK, comm=None):
    rows = [r if isinstance(r, tuple) else (r, r.shape[1], 0) for r in rows]
    T = rows[0][0].shape[0]
    tb = min(tb, T)
    n_r, n_c, n_o, n_a = len(rows), len(consts), len(outs), len(accs)

    def body(*refs):
        vals = [r[...].astype(F32) for r in refs[:n_r + n_c]]
        res = fn(*vals)
        o_refs = refs[n_r + n_c:n_r + n_c + n_o]
        a_refs = refs[n_r + n_c + n_o:]
        for o, r in zip(o_refs, res[:n_o]):
            o[...] = r.astype(o.dtype)
        if n_a:
            @pl.when(pl.program_id(0) == 0)
            def _():
                for ar in a_refs:
                    ar[...] = jnp.zeros_like(ar)
            for ar, r in zip(a_refs, res[n_o:]):
                ar[...] += r

    return _call_with_comm(
        body, (T // tb,), name,
        [pl.BlockSpec((tb, w), functools.partial(lambda i, j: (i, j), j=j)) for _, w, j in rows]
        + [pl.BlockSpec(c.shape, lambda i: (0, 0)) for c in consts],
        [r[0] for r in rows] + list(consts),
        [pl.BlockSpec((tb, d), lambda i: (i, 0)) for d, _ in outs] + [pl.BlockSpec(s, lambda i: (0, 0)) for s in accs],
        [jax.ShapeDtypeStruct((T, d), dt) for d, dt in outs] + [jax.ShapeDtypeStruct(s, F32) for s in accs],
        comm, sem=("arbitrary" if n_a else "parallel",))


def _rms_stats(x):
    r = lax.rsqrt(jnp.mean(x * x, axis=-1, keepdims=True) + EPS)
    return r, x * r


def _rms_bwd(x, g, dy):
    r, xn = _rms_stats(x)
    dyg = dy * g
    dx = r * (dyg - xn * jnp.mean(dyg * xn, axis=-1, keepdims=True))
    return dx, jnp.sum(dy * xn, axis=0, keepdims=True)


def rms_fwd(x, g, name, comm=None):
    res = rowwise(lambda xv, gv: (_rms_stats(xv)[1] * gv,), [x], [g], [(x.shape[1], BF16)], [], name, comm=comm)
    return res[0] if comm is None else (res[0], res[1:])


def rms_bwd(x, g, dy, name, resid=None, dx_dtype=F32):
    def fn(*v):
        if resid is None:
            xv, dyv, gv = v
            dx, dg = _rms_bwd(xv, gv, dyv)
        else:
            xv, dyv, rv, gv = v
            dx, dg = _rms_bwd(xv, gv, dyv)
            dx = dx + rv
        return dx, dg
    rows = [x, dy] + ([] if resid is None else [resid])
    return rowwise(fn, rows, [g], [(x.shape[1], dx_dtype)], [(1, x.shape[1])], name)


def mm_rms_bwd(pairs, x, g, name, resid=None, dx_dtype=F32, comm=None, token=None):
    def epi(accs, ex):
        dx, dg = _rms_bwd(ex[0], ex[-1], accs[0])
        return (dx if resid is None else dx + ex[1]), dg
    extras = [x] + ([] if resid is None else [resid]) + ([] if token is None else [("whole", token)]) + [g]
    return mm([pairs], [dx_dtype], name, tm=min(256, x.shape[0]), tn=x.shape[1], epi=epi, extras=extras, comm=comm, n_sum=1)


def mm_resid(a, b, x, g, wgt, name, comm=None, target=None):
    def epi(accs, ex):
        y = ex[0] + wgt * _rms_stats(accs[0])[1] * ex[1]
        if target is None:
            return accs[0], y
        d = y - ex[2]
        return accs[0], d / D, jnp.sum(d * d, axis=0, keepdims=True)
    return mm([[(a, b, "nn")]], [F32, F32], name, tm=min(512, a.shape[0]), tn=b.shape[1], epi=epi,
              extras=[x, g] + ([] if target is None else [target]), sub=2, comm=comm, n_sum=0 if target is None else 1)


def resid_bwd(h, g, dy, wgt, name):
    def fn(hv, dyv, gv):
        dx, dg = _rms_bwd(hv, gv, dyv)
        return wgt * dx, wgt * dg
    return rowwise(fn, [h, dy], [g], [(h.shape[1], BF16)], [(1, h.shape[1])], name)


def _silu_parts(g):
    s = _sigmoid(g)
    return g * s, s * (1.0 + g * (1.0 - s))


def gated_norm_fwd(y, z, g, name):
    W = SSD_INNER // SSD_G

    def fn(yv, zv, gv):
        yg = yv * _silu_parts(zv)[0]
        return (jnp.concatenate([_rms_stats(yg[:, i * W:(i + 1) * W])[1] for i in range(SSD_G)], axis=1) * gv,)
    return rowwise(fn, [y, z], [g], [(SSD_INNER, BF16)], [], name)[0]


def gated_norm_bwd(y, z, dyn, g, name):
    W = SSD_INNER // SSD_G

    def fn(yv, zv, dv, gv):
        sil, dsil = _silu_parts(zv)
        yg = yv * sil
        parts = [_rms_bwd(yg[:, i * W:(i + 1) * W], gv[:, i * W:(i + 1) * W], dv[:, i * W:(i + 1) * W]) for i in range(SSD_G)]
        dyg = jnp.concatenate([p[0] for p in parts], axis=1)
        dg = jnp.concatenate([p[1] for p in parts], axis=1)
        return dyg * sil, dyg * yv * dsil, dg
    return rowwise(fn, [y, z, dyn], [g], [(SSD_INNER, BF16), (SSD_INNER, BF16)], [(1, SSD_INNER)], name)


def merge_fwd(gl, ys, ym, gb, name):
    def fn(glv, ysv, ymv, gbv):
        gt = _sigmoid(glv + gbv)
        return (gt[:, :D] * ysv + gt[:, D:] * ymv,)
    return rowwise(fn, [gl, ys, ym], [gb], [(D, BF16)], [], name)[0]


def merge_bwd(gl, ys, ym, dm, gb, name):
    def fn(glv, ysv, ymv, dmv, gbv):
        gt = _sigmoid(glv + gbv)
        gs, gm = gt[:, :D], gt[:, D:]
        dgl = jnp.concatenate([dmv * ysv * gs * (1.0 - gs), dmv * ymv * gm * (1.0 - gm)], axis=1)
        return dmv * gs, dmv * gm, dgl, jnp.sum(dgl, axis=0, keepdims=True)
    return rowwise(fn, [gl, ys, ym, dm], [gb], [(D, BF16), (D, BF16), (2 * D, BF16)], [(1, 2 * D)], name)


def _adamw_math(wv, gv, mv, vv):
    mn = B1 * mv + (1.0 - B1) * gv
    vn = B2 * vv + (1.0 - B2) * (gv * gv)
    mh = mn / (1.0 - B1 ** STEP)
    vh = vn / (1.0 - B2 ** STEP)
    return -LR * (mh / (jnp.sqrt(vh) + AEPS) + WD * wv), mn, vn


def adamw(w, g, m, v, name):
    R, C = w.shape
    tb = _tile(R, (256, 128, 64, 32, 16, 8))
    return rowwise(_adamw_math, [w, g, m, v], [], [(C, F32)] * 3, [], name, tb=tb)


def adamw_small(packed, ws, ms, vs):
    k = len(ws)
    sizes = [x.shape[1] for x in ws]

    def body(*refs):
        p_ref, w_refs, m_refs, v_refs = refs[0], refs[1:1 + k], refs[1 + k:1 + 2 * k], refs[1 + 2 * k:1 + 3 * k]
        outs = refs[1 + 3 * k:]
        r0 = 0
        for i, n in enumerate(sizes):
            nr = -(-n // 128)
            g = jnp.concatenate([p_ref[r0 + r:r0 + r + 1, :] for r in range(nr)], axis=1)[:, :n]
            r0 += nr
            outs[i][...] = g
            outs[k + i][...], outs[2 * k + i][...], outs[3 * k + i][...] = _adamw_math(w_refs[i][...], g, m_refs[i][...], v_refs[i][...])

    res = pl.pallas_call(body, name="adamw_small",
                         out_shape=[jax.ShapeDtypeStruct((1, n), F32) for _ in range(4) for n in sizes])(packed, *ws, *ms, *vs)
    return [res[j * k:(j + 1) * k] for j in range(4)]


def adamw_from_slots(recv, piece, w, m, v, name, token=None):
    K, n = w.shape
    ns = recv.shape[0]
    assert recv.shape[2] == n and recv.shape[1] % K == 0
    tb = _tile(K, (256, 176, 128, 64, 32, 16, 8)) if K % 8 == 0 else K
    r_spec = pl.BlockSpec((ns, tb, n), lambda i: (0, piece * (K // tb) + i, 0))
    w_spec = pl.BlockSpec((tb, n), lambda i: (i, 0))

    def body(r_ref, w_ref, m_ref, v_ref, *rest):
        g_ref, d_ref, mo_ref, vo_ref = rest[-4:]
        g = r_ref[0].astype(F32)
        for s in range(1, ns):
            g = g + r_ref[s].astype(F32)
        g_ref[...] = g
        d_ref[...], mo_ref[...], vo_ref[...] = _adamw_math(w_ref[...], g, m_ref[...], v_ref[...])

    extra = [] if token is None else [token]
    return pl.pallas_call(
        body, grid=(K // tb,), name=name,
        in_specs=[r_spec, w_spec, w_spec, w_spec] + [pl.BlockSpec(t.shape, lambda i: (0, 0)) for t in extra], out_specs=[w_spec] * 4,
        out_shape=[jax.ShapeDtypeStruct((K, n), F32)] * 4, compiler_params=_cp("parallel"),
    )(recv, w, m, v, *extra)


def _me():
    return lax.axis_index("x"), lax.axis_index("y"), lax.axis_index("c")


def _dev_index():
    x, y, c = _me()
    return 4 * x + 2 * y + c


HBM_SPEC = pl.BlockSpec(memory_space=pl.ANY)


class GatherComm:
    def __init__(self, shards):
        self.inputs = [s for s, _ in shards]
        self.rows = [list(r) for _, r in shards]
        n = len(shards)
        self.out_shapes = [jax.ShapeDtypeStruct((N_DEV, r, s.shape[1]), s.dtype) for s, rows in shards for r in rows]
        self.sems = [pltpu.SemaphoreType.DMA((7 * n,)), pltpu.SemaphoreType.DMA((7 * n,)), pltpu.SemaphoreType.DMA((n,))]

    def _plan(self, x_refs, out_refs, sems):
        send_sems, recv_sems, local_sems = sems
        x, y, c = _me()
        me, sibling = (x, y, c), (x, y, 1 - c)
        chips = [(1 - x, y), (x, 1 - y), (1 - x, 1 - y)]
        index = lambda px, py, pc: 4 * px + 2 * py + pc
        mine, first, passed, whole = [], [], [], []
        pos = 0
        for i, rows in enumerate(self.rows):
            kw = lambda k: dict(send_sem=send_sems.at[7 * i + k], recv_sem=recv_sems.at[7 * i + k], device_id_type=MESH)
            r0 = 0
            fwd = [[] for _ in chips]
            for j, nr in enumerate(rows):
                out, src = out_refs[pos + j], x_refs[i].at[pl.ds(r0, nr)]
                mine.append(pltpu.make_async_copy(src, out.at[index(*me)], local_sems.at[i]))
                first.append(pltpu.make_async_remote_copy(src_ref=src, dst_ref=out.at[index(*me)], device_id=sibling, **kw(0)))
                for jj, chip in enumerate(chips):
                    first.append(pltpu.make_async_remote_copy(src_ref=src, dst_ref=out.at[index(*me)], device_id=(*chip, c),
                                                              **kw(1 + jj)))
                    blk = out.at[index(*chip, c)]
                    fwd[jj].append(pltpu.make_async_remote_copy(src_ref=blk, dst_ref=blk, device_id=sibling, **kw(4 + jj)))
                r0 += nr
            passed.append(fwd)
            whole.append([pltpu.make_async_remote_copy(src_ref=x_refs[i], dst_ref=x_refs[i], device_id=sibling, **kw(k))
                          for k in range(7)])
            pos += len(rows)
        return mine, first, passed, whole

    def start(self, x_refs, out_refs, sems):
        mine, first, _, _ = self._plan(x_refs, out_refs, sems)
        for cp in mine + first:
            cp.start()

    def finish(self, x_refs, out_refs, sems):
        _, _, passed, whole = self._plan(x_refs, out_refs, sems)
        local_sems = sems[2]
        for i, fwd in enumerate(passed):
            for jj in range(3):
                whole[i][1 + jj].wait_recv()
                for cp in fwd[jj]:
                    cp.start()
        for i in range(len(passed)):
            whole[i][0].wait_recv()
            for jj in range(3):
                whole[i][4 + jj].wait_recv()
        for i in range(len(passed)):
            for k in range(7):
                whole[i][k].wait_send()
            pltpu.make_async_copy(x_refs[i], x_refs[i], local_sems.at[i]).wait()


def run_comm(comm, name):
    n_in, n_out = len(comm.inputs), len(comm.out_shapes)

    def body(*refs):
        ins, outs, sems = refs[:n_in], refs[n_in:n_in + n_out], refs[n_in + n_out:]
        comm.start(ins, outs, sems)
        comm.finish(ins, outs, sems)

    return pl.pallas_call(body, name=name, out_shape=comm.out_shapes, in_specs=[HBM_SPEC] * n_in,
                          out_specs=[HBM_SPEC] * n_out, scratch_shapes=comm.sems)(*comm.inputs)


def _attach(comm, body, n_in, n_out, first, last):
    if comm is None:
        return body
    ci, co, cs = len(comm.inputs), len(comm.out_shapes), len(comm.sems)

    def wrapped(*refs):
        h_in, c_in = refs[:n_in], refs[n_in:n_in + ci]
        h_out, c_out = refs[n_in + ci:n_in + ci + n_out], refs[n_in + ci + n_out:n_in + ci + n_out + co]
        rest = refs[n_in + ci + n_out + co:]
        h_scr, c_sem = rest[:len(rest) - cs], rest[len(rest) - cs:]

        @pl.when(first())
        def _():
            comm.start(c_in, c_out, c_sem)

        body(*h_in, *h_out, *h_scr)

        @pl.when(last())
        def _():
            comm.finish(c_in, c_out, c_sem)

    return wrapped


def _grid_ends(grid):
    first = lambda: functools.reduce(lambda a, b: a & b, [pl.program_id(i) == 0 for i in range(len(grid))])
    last = lambda: functools.reduce(lambda a, b: a & b, [pl.program_id(i) == g - 1 for i, g in enumerate(grid)])
    return first, last


def _call_with_comm(body, grid, name, in_specs, args, out_specs, out_shape, comm, scratch=(), sem=None):
    sem = sem or ("parallel",) * len(grid)
    scratch = list(scratch)
    if comm is not None:
        body = _attach(comm, body, len(args), len(out_shape), *_grid_ends(grid))
        in_specs, args = in_specs + [HBM_SPEC] * len(comm.inputs), args + comm.inputs
        out_specs, out_shape = out_specs + [HBM_SPEC] * len(comm.out_shapes), out_shape + comm.out_shapes
        scratch, sem = scratch + comm.sems, ("arbitrary",) * len(grid)
    return pl.pallas_call(body, grid=grid, name=name, in_specs=in_specs, out_specs=out_specs, out_shape=out_shape,
                          scratch_shapes=scratch, compiler_params=_cp(*sem))(*args)


class ScatterComm:
    def __init__(self, groups):
        self.sizes = [len(g) for g in groups]
        self.rows = [[pc.shape[1] for pc in g] for g in groups]
        ng = len(groups)
        self.inputs = [pc for g in groups for pc in g]
        self.out_shapes = [jax.ShapeDtypeStruct((N_DEV, sum(self.rows[gi]), g[0].shape[2]), g[0].dtype) for gi, g in enumerate(groups)]
        self.sems = [pltpu.SemaphoreType.DMA((7 * ng,)), pltpu.SemaphoreType.DMA((7 * ng,)), pltpu.SemaphoreType.DMA((ng,))]

    def _peers(self):
        x, y, c = _me()
        out = []
        for k in range(1, N_DEV):
            px = 1 - x if k & 4 else x
            py = 1 - y if k & 2 else y
            pc = 1 - c if k & 1 else c
            out.append((k, 4 * px + 2 * py + pc, dict(device_id=(px, py, pc), device_id_type=MESH)))
        return 4 * x + 2 * y + c, out

    def start(self, ins, outs, sems):
        send_sems, recv_sems, local_sems = sems
        me, peers = self._peers()
        pos = 0
        for gi, size in enumerate(self.sizes):
            for i, pc in enumerate(ins[pos:pos + size]):
                dst = outs[gi].at[me, pl.ds(sum(self.rows[gi][:i]), self.rows[gi][i])]
                pltpu.make_async_copy(pc.at[me], dst, local_sems.at[gi]).start()
                for k, peer, kw in peers:
                    pltpu.make_async_remote_copy(src_ref=pc.at[peer], dst_ref=dst, send_sem=send_sems.at[7 * gi + k - 1],
                                                 recv_sem=recv_sems.at[7 * gi + k - 1], **kw).start()
            pos += size

    def finish(self, ins, outs, sems):
        send_sems, recv_sems, local_sems = sems
        me, peers = self._peers()
        whole = [pltpu.make_async_remote_copy(src_ref=outs[gi].at[peer], dst_ref=outs[gi].at[peer],
                                              send_sem=send_sems.at[7 * gi + k - 1], recv_sem=recv_sems.at[7 * gi + k - 1], **kw)
                 for gi in range(len(self.sizes)) for k, peer, kw in peers]
        for cp in whole:
            cp.wait_recv()
        for cp in whole:
            cp.wait_send()
        for gi in range(len(self.sizes)):
            pltpu.make_async_copy(outs[gi].at[me], outs[gi].at[me], local_sems.at[gi]).wait()


def _peer_list():
    x, y, c = _me()
    out = []
    for k in range(1, N_DEV):
        px = 1 - x if k & 4 else x
        py = 1 - y if k & 2 else y
        pc = 1 - c if k & 1 else c
        out.append((k, 4 * px + 2 * py + pc, dict(device_id=(px, py, pc), device_id_type=MESH)))
    return 4 * x + 2 * y + c, out


SEM_SPEC = pl.BlockSpec(memory_space=pltpu.SEMAPHORE)
HBM_ONLY = pl.BlockSpec(memory_space=pltpu.HBM)
N_SPLIT_SEMS = 2 * (N_DEV - 1)


def exchange_start(piece, after, name):
    def body(piece_ref, land_ref, after_ref, *outs):
        sems, token = outs[:N_SPLIT_SEMS], outs[-1]
        me, peers = _peer_list()
        for k, peer, kw in peers:
            pltpu.make_async_remote_copy(src_ref=piece_ref.at[peer], dst_ref=land_ref.at[me], send_sem=sems[k - 1],
                                         recv_sem=sems[N_DEV - 2 + k], **kw).start()
        token[...] = jnp.zeros_like(token)

    res = pl.pallas_call(
        body, name=name + "_start",
        out_shape=(pltpu.SemaphoreType.DMA(()),) * N_SPLIT_SEMS + (pltpu.HBM(piece.shape, piece.dtype), pltpu.HBM(piece.shape, piece.dtype),
                                                                   jax.ShapeDtypeStruct((8, 128), F32)),
        in_specs=(HBM_ONLY, HBM_ONLY, HBM_SPEC),
        out_specs=(SEM_SPEC,) * N_SPLIT_SEMS + (HBM_ONLY, HBM_ONLY, pl.BlockSpec(memory_space=pltpu.VMEM)),
        input_output_aliases={0: N_SPLIT_SEMS, 1: N_SPLIT_SEMS + 1},
        compiler_params=pltpu.CompilerParams(has_side_effects=pltpu.SideEffectType.DATAFLOW_SIDE_EFFECTING),
    )(pltpu.with_memory_space_constraint(piece, pltpu.HBM),
      pltpu.with_memory_space_constraint(lax.empty(piece.shape, piece.dtype), pltpu.HBM), after)
    return res[:N_SPLIT_SEMS], res[N_SPLIT_SEMS], res[N_SPLIT_SEMS + 1], res[N_SPLIT_SEMS + 2]


def exchange_wait(sems, piece, land, after, name):
    def body(piece_ref, land_ref, *rest):
        sem_refs = rest[:N_SPLIT_SEMS]
        me, peers = _peer_list()
        for k, peer, kw in peers:
            cp = pltpu.make_async_remote_copy(src_ref=piece_ref.at[peer], dst_ref=land_ref.at[peer], send_sem=sem_refs[k - 1],
                                              recv_sem=sem_refs[N_DEV - 2 + k], **kw)
            cp.wait_send()
            cp.wait_recv()

    return pl.pallas_call(
        body, name=name + "_wait",
        out_shape=(pltpu.HBM(piece.shape, piece.dtype), pltpu.HBM(land.shape, land.dtype)),
        in_specs=(HBM_ONLY, HBM_ONLY) + (SEM_SPEC,) * N_SPLIT_SEMS + (HBM_SPEC,), out_specs=(HBM_ONLY, HBM_ONLY),
        input_output_aliases={0: 0, 1: 1},
        compiler_params=pltpu.CompilerParams(has_side_effects=pltpu.SideEffectType.DATAFLOW_SIDE_EFFECTING),
    )(piece, land, *sems, after)[1]


def sum_slots(recv, name, tr):
    n, R, C = recv.shape

    def body(r_ref, o_ref):
        acc = r_ref[0].astype(F32)
        for s in range(1, n):
            acc = acc + r_ref[s].astype(F32)
        o_ref[...] = acc

    return pl.pallas_call(
        body, grid=(R // tr,), name=name,
        in_specs=[pl.BlockSpec((n, tr, C), lambda i: (0, i, 0))], out_specs=pl.BlockSpec((tr, C), lambda i: (i, 0)),
        out_shape=jax.ShapeDtypeStruct((R, C), F32), compiler_params=_cp("parallel"),
    )(recv)


PACK_W, FLAT_W = 1024, 128
MAIN = [
    ("ffn1_w_gate", "col"), ("ffn1_w_up", "col"), ("ffn1_w_down", "row"),
    ("ffn2_w_gate", "col"), ("ffn2_w_up", "col"), ("ffn2_w_down", "row"),
    ("w_ssd_proj", "row"), ("w_mla_proj", "row"), ("w_out", "row"),
    ("w_xq", "row"), ("w_xk", "row"), ("w_xv", "row"), ("w_xo", "row"),
    ("w_uk", "col"), ("w_uv", "col"), ("w_in", "col"),
]
FLAT = [("w_uq", "col")]
BIG = MAIN + FLAT
SMALL = ["ffn1_pre_g", "ffn1_post_g", "mix_pre_g", "conv_b", "dt_bias", "a_log", "d_skip", "ssd_norm_g", "q_norm_g",
         "kv_norm_g", "gate_bias", "mix_post_g", "xa_pre_g", "mem_norm_g", "xa_post_g", "ffn2_pre_g", "ffn2_post_g"]
WEIGHTS = ['ffn1_pre_g', 'ffn1_w_gate', 'ffn1_w_up', 'ffn1_w_down', 'ffn1_post_g', 'mix_pre_g', 'w_in', 'conv_w', 'conv_b',
           'dt_bias', 'a_log', 'd_skip', 'ssd_norm_g', 'w_ssd_proj', 'q_norm_g', 'w_uq', 'kv_norm_g', 'w_uk', 'w_uv',
           'w_mla_proj', 'gate_bias', 'w_out', 'mix_post_g', 'xa_pre_g', 'mem_norm_g', 'w_xq', 'w_xk', 'w_xv', 'w_xo',
           'xa_post_g', 'ffn2_pre_g', 'ffn2_w_gate', 'ffn2_w_up', 'ffn2_w_down', 'ffn2_post_g']


def _pack_rows(w, kind, width):
    m = w[0].T if kind == "col" else w[0]
    return m.reshape(-1, width)


KIND = dict(BIG)
GATHER_PLAN = {
    "ffn1_pre": (["ffn1_w_gate", "ffn1_w_up"], []),
    "ffn1_gate_up": (["ffn1_w_down", "w_in@0"], []),
    "ffn1_down": (["w_in@1"], ["conv_w"]),
    "ssd_fwd": (["w_ssd_proj", "w_mla_proj", "w_out", "w_uk", "w_uv"], ["w_uq"]),
    "attn_fwd": (["w_xq", "w_xk", "w_xv", "w_xo", "ffn2_w_gate", "ffn2_w_up", "ffn2_w_down"], []),
}
EARLY_EXCHANGE, LAST_EXCHANGE = "early", "last"
SCATTER_PLAN = {
    "attn_bwd": [["ffn2_w_gate", "ffn2_w_up", "ffn2_w_down"], ["w_xq", "w_xk", "w_xv", "w_xo"]],
    "ssd_bwd": [["w_ssd_proj", "w_mla_proj", "w_out"], ["w_uk", "w_uv"], ["w_uq"]],
    "in_bwd": [["w_in#0"]],
    "ffn1:down_bwd": [["w_in#1"]],
    "ffn1:dwd": [["w_in#2"]],
    "ffn1:dwg": [["ffn1_w_down#0"]],
    "ffn1:dwu": [["ffn1_w_down#1"]],
    "early": [["ffn1_w_gate"]],
    "last": [["ffn1_w_up"]],
}
PARTS = {"w_in@0": ("w_in", 0, 336), "w_in@1": ("w_in", 336, 662),
         "w_in#0": ("w_in", 0, 336), "w_in#1": ("w_in", 336, 496), "w_in#2": ("w_in", 496, 662),
         "ffn1_w_down#0": ("ffn1_w_down", 0, 176), "ffn1_w_down#1": ("ffn1_w_down", 176, 352)}


def _parts_of(base, mark):
    return sorted(pn for pn, (b, _, _) in PARTS.items() if b == base and mark in pn)


class Stage:
    def __init__(self, w):
        self.w = w
        self.width = {n: PACK_W if (n, k) in MAIN else FLAT_W for n, k in BIG}
        self.nrows = {n: math.prod(w[n].shape) // self.width[n] for n, _ in BIG}
        self.recv = {}
        self.split = {}
        self.arrived_parts = {}

    def _shards(self, tag):
        names_main, names_flat = GATHER_PLAN[tag]

        def pack(n):
            if n == "conv_w":
                return _pad_rows(lax.bitcast_convert_type(self.w[n][0], BF16).reshape(-1, FLAT_W), 16)
            base, r0, r1 = PARTS.get(n, (n, 0, None))
            return _pack_rows(self.w[base], KIND[base], self.width[base])[r0:r1].astype(BF16)
        shards = []
        if names_main:
            pieces = [pack(n) for n in names_main]
            shards.append((jnp.concatenate(pieces, axis=0), [pc.shape[0] for pc in pieces]))
        if names_flat:
            pieces = [pack(n) for n in names_flat]
            shards.append((jnp.concatenate(pieces, axis=0), [pc.shape[0] for pc in pieces]))
        return shards

    def gather(self, tag):
        return GatherComm(self._shards(tag)) if tag in GATHER_PLAN else None

    def gathered(self, tag, outs, W, p):
        if tag not in GATHER_PLAN:
            return
        names_main, names_flat = GATHER_PLAN[tag]
        outs = list(outs)
        for n in names_main + names_flat:
            rows = outs.pop(0)
            if n == "conv_w":
                cw = self.w[n]
                bits = rows[:, :2 * math.prod(cw.shape) // FLAT_W].reshape((N_DEV,) + cw.shape[1:] + (2,))
                p[n] = lax.bitcast_convert_type(bits, F32).transpose(1, 0, 2).reshape(cw.shape[1], -1)
                continue
            if n in PARTS:
                self.arrived_parts[n] = rows
                base = PARTS[n][0]
                mine = _parts_of(base, "@")
                if not all(pn in self.arrived_parts for pn in mine):
                    continue
                n, rows = base, jnp.concatenate([self.arrived_parts[pn] for pn in mine], axis=1)
            K = self.w[n].shape[1] if KIND[n] == "col" else PACK_W
            W[n] = rows.reshape(-1, K)

    def pieces(self, tag, gw):
        def piece(n):
            if n in PARTS:
                base, r0, r1 = PARTS[n]
                return gw[base].reshape(N_DEV, self.nrows[base], self.width[base])[:, r0:r1]
            return gw[n].reshape(N_DEV, self.nrows[n], self.width[n])
        return [[piece(n) for n in names] for names in SCATTER_PLAN[tag]]

    def scatter(self, tag, gw):
        return ScatterComm(self.pieces(tag, gw)) if tag in SCATTER_PLAN else None

    def scattered(self, tag, outs):
        if tag in SCATTER_PLAN:
            self.recv[tag] = outs

    def split_start(self, tag, gw, after=None, behind=None):
        after = after if behind is None else self.recv[behind][0]
        piece, = self.pieces(tag, gw)[0]
        sems, piece, land, token = exchange_start(piece, after, "exchange_" + tag)
        self.split[tag] = (sems, piece, land)
        return token

    def split_wait(self, tag, after):
        sems, piece, land = self.split.pop(tag)
        land = exchange_wait(sems, piece, land, after, "exchange_" + tag)
        me = _dev_index()
        return lax.dynamic_update_index_in_dim(land, lax.dynamic_index_in_dim(piece, me, 0, keepdims=False), me, 0)


def _pad_rows(a, mult):
    r = (-a.shape[0]) % mult
    return a if r == 0 else jnp.concatenate([a, jnp.zeros((r,) + a.shape[1:], a.dtype)], axis=0)


def _pack_small(vals, loss_row=None, conv_w=None):
    rows = []
    for v in vals:
        f = v.reshape(-1)
        f = jnp.concatenate([f, jnp.zeros(((-f.shape[0]) % 128,), F32)])
        rows.append(f.reshape(-1, 128))
    if conv_w is not None:
        rows.append(conv_w.reshape(-1, 128))
    if loss_row is not None:
        rows.append(loss_row)
    return _pad_rows(jnp.concatenate(rows, axis=0), 8)


def _tn(a, b, name, out_dtype=BF16, comm=None):
    M, N = a.shape[1], b.shape[1]
    T = a.shape[0]
    tm = M if M <= 1536 else M // 2
    tk = 2048 if T % 2048 == 0 and T > 2048 else None
    res = mm([[(a, b, "tn")]], [out_dtype], name, tm=tm, tn=N, tk=tk, comm=comm)
    return res[0] if comm is None else (res[0], res[1:])


class NoStage:
    def gather(self, tag):
        return None

    def gathered(self, tag, outs, W, p):
        pass

    def scatter(self, tag, gw):
        return None

    def scattered(self, tag, outs):
        pass

    def split_start(self, tag, gw, after=None, behind=None):
        return None


def _ffn_fwd(x, gpre, gpost, W, p, tag, stage, target=None):
    comm = stage.gather(tag + "_pre")
    h = rms_fwd(x, gpre, tag + "_pre", comm=comm)
    if comm is not None:
        h, arrived = h
        stage.gathered(tag + "_pre", arrived, W, p)

    def swi(accs, ex):
        sil, dsil = _silu_parts(accs[0])
        return sil, accs[1] * dsil, sil * accs[1]
    G, U, A, *arrived = mm([[(h, W[tag + "_w_gate"], "nt")], [(h, W[tag + "_w_up"], "nt")]], [BF16, BF16, BF16], tag + "_gate_up",
                           tn=DFF // 2, epi=swi, comm=stage.gather(tag + "_gate_up"), sub=4 if h.shape[0] % 1024 == 0 else 1)
    stage.gathered(tag + "_gate_up", arrived, W, p)
    H, y, *rest = mm_resid(A, W[tag + "_w_down"], x, gpost, FFN_RES, tag + "_down", comm=stage.gather(tag + "_down"), target=target)
    saved = (x, h, G, U, A, H)
    if target is not None:
        return y, saved, rest[0]
    stage.gathered(tag + "_down", rest, W, p)
    return y, saved


def _ffn_bwd(dy, saved, gpre, gpost, wg_t, wu_t, wd, tag, stage, gw):
    x, h, G, U, A, H = saved
    dH, dgpost = resid_bwd(H, gpost, dy, FFN_RES, tag + "_post_bwd")

    def dswi(accs, ex):
        return accs[0] * ex[1], accs[0] * ex[0]

    def hosted(where, call):
        comm = stage.scatter(tag + ":" + where, gw)
        res = call(comm)
        if comm is None:
            return res
        stage.scattered(tag + ":" + where, res[1])
        return res[0]

    res = hosted("down_bwd", lambda comm: (lambda r: r if comm is None else (r[:2], r[2:]))(
        mm([[(dH, wd, "nt")]], [BF16, BF16], tag + "_down_bwd", tn=DFF // 2, epi=dswi, extras=[G, U], comm=comm,
           sub=4 if dH.shape[0] % 1024 == 0 else 1)))
    dG, dU = res
    gw[tag + "_w_down"] = hosted("dwd", lambda comm: _tn(A, dH, tag + "_dwd", comm=comm))
    gw[tag + "_w_gate"] = hosted("dwg", lambda comm: _tn(dG, h, tag + "_dwg", comm=comm))
    gw[tag + "_w_up"] = hosted("dwu", lambda comm: _tn(dU, h, tag + "_dwu", comm=comm))
    token = stage.split_start(EARLY_EXCHANGE, gw, behind=tag + ":dwu") if tag == "ffn1" else None
    dx, dgpre = mm_rms_bwd([(dG, wg_t, "nn"), (dU, wu_t, "nn")], x, gpre, tag + "_gate_up_bwd", resid=dy, token=token)
    return dx, dgpre, dgpost


def _local_step(x, mem, positions, tgt, W, p, stage=None):
    stage = stage or NoStage()
    nseq = x.shape[0]
    T = nseq * x.shape[1]
    x0 = x.reshape(T, D)
    mem2 = mem.reshape(-1, D)

    x1, ffn1 = _ffn_fwd(x0, p["ffn1_pre_g"], p["ffn1_post_g"], W, p, "ffn1", stage)

    w_in_t = W["w_in"]
    bounds = [0]
    for n in (SSD_INNER, CONV_CH, SSD_H, QR, KVR, ROPE, 2 * D):
        bounds.append(bounds[-1] + n)
    wt_z, wt_xbc, wt_dt, wt_q, wt_kv, wt_kr, wt_gate = [w_in_t[bounds[i]:bounds[i + 1]] for i in range(7)]
    wt_dt, wt_kr = _pad_rows(wt_dt, SLOT), _pad_rows(wt_kr, SLOT)
    wt_dtkr = jnp.concatenate([wt_dt, wt_kr], axis=0)
    hm = rms_fwd(x1, p["mix_pre_g"], "mix_pre")
    z = mm1(hm, wt_z, "nt", BF16, "in_z")
    xbc = mm1(hm, wt_xbc, "nt", BF16, "in_xbc")
    q_c = mm1(hm, wt_q, "nt", F32, "in_q", tn=QR)
    kv_c = mm1(hm, wt_kv, "nt", F32, "in_kv")
    dtkr = mm1(hm, wt_dtkr, "nt", F32, "in_dtkr")
    gl = mm1(hm, wt_gate, "nt", BF16, "in_gate")

    xbc_act = conv_fwd(xbc, p["conv_w"], p["conv_b"], nseq)
    y_ssd_core, prev, *arrived = ssd_fwd(xbc_act, dtkr, p["dt_bias"], p["a_log"], p["d_skip"], nseq, comm=stage.gather("ssd_fwd"))
    stage.gathered("ssd_fwd", arrived, W, p)
    yn = gated_norm_fwd(y_ssd_core, z, p["ssd_norm_g"], "ssd_norm")
    y_ssd = mm1(yn, W["w_ssd_proj"], "nn", BF16, "ssd_proj")

    slot_rows = lambda wt, per: jnp.pad(wt.reshape(MLA_H, per, -1), ((0, 0), (0, SLOT - per), (0, 0))).reshape(MLA_H * SLOT, -1)
    wq_s, wk_s, wv_s = slot_rows(W["w_uq"], QK), slot_rows(W["w_uk"], NOPE), slot_rows(W["w_uv"], VD)
    wo_s = slot_rows(W["w_mla_proj"], VD)
    qn = rms_fwd(q_c, p["q_norm_g"], "q_norm")
    rope_c, rope_s = rope_table(*_rope_inputs(positions))
    rope_args = [("rows", rope_c), ("rows", rope_s)]
    Qc, = mm([[(qn, wq_s, "nt")]], [BF16], "uq", epi=rope_q_epilogue, extras=rope_args, sub=4 if T % 1024 == 0 else 1)
    kvn = rms_fwd(kv_c, p["kv_norm_g"], "kv_norm")
    Kc, = mm([[(kvn, wk_s, "nt")]], [BF16], "uk", epi=rope_k_epilogue, extras=rope_args + [("rows", dtkr)],
             sub=4 if T % 1024 == 0 else 1)
    v_s = mm1(kvn, wv_s, "nt", BF16, "uv")
    o_s, lse, *arrived = attn_slot_fwd(Qc, Kc, v_s, nseq, comm=stage.gather("attn_fwd"))
    stage.gathered("attn_fwd", arrived, W, p)
    y_mla = mm1(o_s, wo_s, "nn", BF16, "mla_proj")

    merged = merge_fwd(gl, y_ssd, y_mla, p["gate_bias"], "merge")
    hmix, x2 = mm_resid(merged, W["w_out"], x1, p["mix_post_g"], 1.0, "mix_out")

    hq = rms_fwd(x2, p["xa_pre_g"], "xa_pre")
    mn = rms_fwd(mem2, p["mem_norm_g"], "mem_norm")
    xq = mm1(hq, W["w_xq"], "nn", BF16, "xq")
    xk = mm1(mn, W["w_xk"], "nn", BF16, "xk")
    xv = mm1(mn, W["w_xv"], "nn", BF16, "xv")
    xo, *arrived = xattn_fwd(xq, xk, xv, nseq, comm=stage.gather("xattn_fwd"))
    stage.gathered("xattn_fwd", arrived, W, p)
    ho, x3 = mm_resid(xo, W["w_xo"], x2, p["xa_post_g"], 1.0, "xo")

    dx4, ffn2, sq_cols = _ffn_fwd(x3, p["ffn2_pre_g"], p["ffn2_post_g"], W, p, "ffn2", stage, target=tgt.reshape(T, D))
    loss_row = (0.5 / D) * jnp.sum(sq_cols.reshape(-1, 128), axis=0, keepdims=True)

    gw, gs = {}, {}
    dx3, gs["ffn2_pre_g"], gs["ffn2_post_g"] = _ffn_bwd(
        dx4, ffn2, p["ffn2_pre_g"], p["ffn2_post_g"], W["ffn2_w_gate"], W["ffn2_w_up"], W["ffn2_w_down"], "ffn2", stage, gw)

    dho, gs["xa_post_g"] = resid_bwd(ho, p["xa_post_g"], dx3, 1.0, "xa_post_bwd")
    dxo = mm1(dho, W["w_xo"], "nt", BF16, "xo_bwd")
    gw["w_xo"] = _tn(xo, dho, "d_w_xo")
    dxq, dxk, dxv = xattn_bwd(xq, xk, xv, dxo, nseq)
    dx2, gs["xa_pre_g"] = mm_rms_bwd([(dxq, W["w_xq"], "nt")], x2, p["xa_pre_g"], "xq_bwd", resid=dx3)
    gw["w_xq"] = _tn(hq, dxq, "d_w_xq")
    dmn = mm([[(dxk, W["w_xk"], "nt"), (dxv, W["w_xv"], "nt")]], [F32], "xkv_bwd")[0]
    gw["w_xk"] = _tn(mn, dxk, "d_w_xk")
    gw["w_xv"] = _tn(mn, dxv, "d_w_xv")
    _, gs["mem_norm_g"] = rms_bwd(mem2, p["mem_norm_g"], dmn, "mem_norm_bwd", dx_dtype=BF16)

    dhmix, gs["mix_post_g"] = resid_bwd(hmix, p["mix_post_g"], dx2, 1.0, "mix_post_bwd")
    dmerged = mm1(dhmix, W["w_out"], "nt", BF16, "mix_out_bwd")
    gw["w_out"] = _tn(merged, dhmix, "d_w_out")
    dys, dym, dgl, gs["gate_bias"] = merge_bwd(gl, y_ssd, y_mla, dmerged, p["gate_bias"], "merge_bwd")

    unslot = lambda g, per: g.reshape(MLA_H, SLOT, -1)[:, :per].reshape(MLA_H * per, -1)
    do_s = mm1(dym, wo_s, "nt", BF16, "mla_proj_bwd")
    gw["w_mla_proj"] = unslot(_tn(o_s, dym, "d_w_mla_proj"), VD)
    dQc, dKc, dv_s, *sent = attn_slot_bwd(Qc, Kc, v_s, o_s, lse, do_s, nseq, comm=stage.scatter("attn_bwd", gw))
    stage.scattered("attn_bwd", sent)
    dq_s, dkn_s, dkr = rope_slot_bwd(dQc, dKc, rope_c, rope_s, "rope_bwd")
    dq_c, gs["q_norm_g"] = mm_rms_bwd([(dq_s, wq_s, "nn")], q_c, p["q_norm_g"], "uq_bwd", dx_dtype=BF16)
    gw["w_uq"] = unslot(_tn(dq_s, qn, "d_w_uq"), QK)
    dkv_c, gs["kv_norm_g"] = mm_rms_bwd([(dkn_s, wk_s, "nn"), (dv_s, wv_s, "nn")], kv_c, p["kv_norm_g"], "ukv_bwd", dx_dtype=BF16)
    gw["w_uk"] = unslot(_tn(dkn_s, kvn, "d_w_uk"), NOPE)
    gw["w_uv"] = unslot(_tn(dv_s, kvn, "d_w_uv"), VD)

    dyn = mm1(dys, W["w_ssd_proj"], "nt", BF16, "ssd_proj_bwd")
    gw["w_ssd_proj"] = _tn(yn, dys, "d_w_ssd_proj")
    dyc, dz, gs["ssd_norm_g"] = gated_norm_bwd(y_ssd_core, z, dyn, p["ssd_norm_g"], "ssd_norm_bwd")
    dxbc_act, ddtr, gs["dt_bias"], gs["a_log"], gs["d_skip"], *sent = ssd_bwd(
        xbc_act, dtkr, p["dt_bias"], p["a_log"], p["d_skip"], prev, dyc, nseq, comm=stage.scatter("ssd_bwd", gw))
    stage.scattered("ssd_bwd", sent)
    dxbc, gs["conv_w"], gs["conv_b"] = conv_bwd(xbc, p["conv_w"], p["conv_b"], dxbc_act, nseq)

    gw["w_in"] = jnp.concatenate([_tn(dz, hm, "d_w_in_z"), _tn(dxbc, hm, "d_w_in_xbc"), _tn(ddtr, hm, "d_w_in_dt")[:SSD_H],
                                  _tn(dq_c, hm, "d_w_in_q"), _tn(dkv_c, hm, "d_w_in_kv"), _tn(dkr, hm, "d_w_in_kr")[:ROPE],
                                  _tn(dgl, hm, "d_w_in_gate")], axis=0)
    dx1, gs["mix_pre_g"], *sent = mm_rms_bwd(
        [(dz, wt_z, "nn"), (dxbc, wt_xbc, "nn"), (ddtr, wt_dt, "nn"), (dq_c, wt_q, "nn"), (dkv_c, wt_kv, "nn"),
         (dkr, wt_kr, "nn"), (dgl, wt_gate, "nn")], x1, p["mix_pre_g"], "in_bwd", resid=dx2, comm=stage.scatter("in_bwd", gw))
    stage.scattered("in_bwd", sent)

    dx0, gs["ffn1_pre_g"], gs["ffn1_post_g"] = _ffn_bwd(
        dx1, ffn1, p["ffn1_pre_g"], p["ffn1_post_g"], W["ffn1_w_gate"], W["ffn1_w_up"], W["ffn1_w_down"], "ffn1", stage, gw)
    return loss_row, dx0.reshape(x.shape), gw, gs


def kernel(x, mem, positions, ffn1_pre_g, ffn1_w_gate, ffn1_w_up, ffn1_w_down, ffn1_post_g, mix_pre_g, w_in, conv_w, conv_b, dt_bias, a_log, d_skip, ssd_norm_g, w_ssd_proj, q_norm_g, w_uq, kv_norm_g, w_uk, w_uv, w_mla_proj, gate_bias, w_out, mix_post_g, xa_pre_g, mem_norm_g, w_xq, w_xk, w_xv, w_xo, xa_post_g, ffn2_pre_g, ffn2_w_gate, ffn2_w_up, ffn2_w_down, ffn2_post_g, loss_target, m_ffn1_pre_g, m_ffn1_w_gate, m_ffn1_w_up, m_ffn1_w_down, m_ffn1_post_g, m_mix_pre_g, m_w_in, m_conv_w, m_conv_b, m_dt_bias, m_a_log, m_d_skip, m_ssd_norm_g, m_w_ssd_proj, m_q_norm_g, m_w_uq, m_kv_norm_g, m_w_uk, m_w_uv, m_w_mla_proj, m_gate_bias, m_w_out, m_mix_post_g, m_xa_pre_g, m_mem_norm_g, m_w_xq, m_w_xk, m_w_xv, m_w_xo, m_xa_post_g, m_ffn2_pre_g, m_ffn2_w_gate, m_ffn2_w_up, m_ffn2_w_down, m_ffn2_post_g, v_ffn1_pre_g, v_ffn1_w_gate, v_ffn1_w_up, v_ffn1_w_down, v_ffn1_post_g, v_mix_pre_g, v_w_in, v_conv_w, v_conv_b, v_dt_bias, v_a_log, v_d_skip, v_ssd_norm_g, v_w_ssd_proj, v_q_norm_g, v_w_uq, v_kv_norm_g, v_w_uk, v_w_uv, v_w_mla_proj, v_gate_bias, v_w_out, v_mix_post_g, v_xa_pre_g, v_mem_norm_g, v_w_xq, v_w_xk, v_w_xv, v_w_xo, v_xa_post_g, v_ffn2_pre_g, v_ffn2_w_gate, v_ffn2_w_up, v_ffn2_w_down, v_ffn2_post_g):
    a = dict(locals())
    w = {n: a[n] for n in WEIGHTS}
    m = {n: a["m_" + n] for n in WEIGHTS}
    v = {n: a["v_" + n] for n in WEIGHTS}

    stage = Stage(w)
    W, p = {}, {n: w[n] for n in SMALL}
    loss_row, grad_x, gw, gs = _local_step(x, mem, positions, loss_target, W, p, stage)

    sm = _pack_small([gs[n] for n in SMALL], loss_row=loss_row, conv_w=gs["conv_w"])
    srecv, = run_comm(ScatterComm([[jnp.broadcast_to(sm[None], (N_DEV,) + sm.shape)]]), "exchange_small")
    s_rows = sum_slots(srecv, "sum_small", tr=sm.shape[0])
    token = stage.split_start(LAST_EXCHANGE, gw, after=s_rows)
    grads, delta, new_m, new_v = {}, {}, {}, {}
    raw_results = []

    def finish(n, buf, piece, token=None):
        col = KIND[n] == "col"
        turn = (lambda t: t.T) if col else (lambda t: t)
        K = w[n].shape[1]
        if col and buf.shape[2] != K:
            buf = buf.reshape(buf.shape[0], -1, K)
        res = adamw_from_slots(buf, piece, turn(w[n][0]), turn(m[n][0]), turn(v[n][0]), "adamw_" + n, token=token)
        raw_results.append(res[3])
        grads[n], delta[n], new_m[n], new_v[n] = [turn(r)[None] for r in res]

    finish(SCATTER_PLAN[EARLY_EXCHANGE][0][0], stage.split_wait(EARLY_EXCHANGE, after=s_rows), 0, token)
    parts = {}
    for tag, groups in SCATTER_PLAN.items():
        if tag in (EARLY_EXCHANGE, LAST_EXCHANGE):
            continue
        for names, buf in zip(groups, stage.recv[tag]):
            for piece, n in enumerate(names):
                if n in PARTS:
                    parts[n] = sum_slots(buf, "sum_" + n.replace("#", "_"), tr=buf.shape[1])
                else:
                    finish(n, buf, piece, token)
    for base in sorted({PARTS[pn][0] for pn in parts}):
        rows = jnp.concatenate([parts[pn] for pn in _parts_of(base, "#")], axis=0)
        finish(base, rows[None], 0, token)
    finish(SCATTER_PLAN[LAST_EXCHANGE][0][0], stage.split_wait(LAST_EXCHANGE, after=raw_results[-1]), 0)
    conv_w_full = p["conv_w"]
    small = adamw_small(s_rows, [w[n] for n in SMALL], [m[n] for n in SMALL], [v[n] for n in SMALL])
    for t, vals in zip((grads, delta, new_m, new_v), small):
        t.update(zip(SMALL, vals))
    r1 = sum(-(-w[n].shape[1] // 128) for n in SMALL)
    ncw = math.prod(conv_w_full.shape) // 128
    cw_grad_full = s_rows[r1:r1 + ncw].reshape(conv_w_full.shape)
    wsh = conv_w.shape[2]
    grads["conv_w"] = lax.dynamic_slice_in_dim(cw_grad_full, _dev_index() * wsh, wsh, axis=1)[None]
    loss = jnp.sum(s_rows[r1 + ncw])
    d_, m_, v_ = adamw(conv_w[0], grads["conv_w"][0], m["conv_w"][0], v["conv_w"][0], "adamw_conv_w")
    delta["conv_w"], new_m["conv_w"], new_v["conv_w"] = d_[None], m_[None], v_[None]
    return (loss, grad_x, *[grads[n] for n in WEIGHTS], *[delta[n] for n in WEIGHTS],
            *[new_m[n] for n in WEIGHTS], *[new_v[n] for n in WEIGHTS])
```

```python
import functools
import math

import jax
import jax.numpy as jnp
from jax import lax
from jax.experimental import pallas as pl
from jax.experimental.pallas import tpu as pltpu

F32, BF16 = jnp.float32, jnp.bfloat16
MESH = pl.DeviceIdType.MESH
N_DEV = 8

D = 1024
DFF = 2816
SSD_H, SSD_P, SSD_G, SSD_N, SSD_L = 16, 64, 2, 128, 128
SSD_INNER = SSD_H * SSD_P
CONV_K, CONV_CH = 4, 1536
MLA_H, QR, KVR, NOPE, ROPE, VD = 16, 384, 256, 64, 32, 64
QK = NOPE + ROPE
ROPE_THETA = 10000.0
XA_H, XA_D = 4, 256
EPS = 1e-6
FFN_RES = 0.5
LR, B1, B2, AEPS, WD, STEP = 0.001, 0.9, 0.999, 1e-08, 0.01, 10

VMEM_LIMIT = 56 * 2**20


def _cp(*sem):
    return pltpu.CompilerParams(dimension_semantics=sem, vmem_limit_bytes=VMEM_LIMIT)


def _sigmoid(x):
    return 1.0 / (1.0 + jnp.exp(-x))


def _softplus(x):
    return jnp.where(x > 20.0, x, jnp.log(1.0 + jnp.exp(jnp.minimum(x, 20.0))))


def _dot(a, b, dims="nn"):
    ca = 0 if dims[0] == "t" else 1
    cb = 1 if dims[1] == "t" else 0
    return lax.dot_general(a.astype(BF16), b.astype(BF16), (((ca,), (cb,)), ((), ())), preferred_element_type=F32)


def _dot_sel(a, b, dims="nn", split="a", terms=3):
    r = (a if split == "a" else b).astype(F32)
    out = None
    for t in range(terms):
        piece = r.astype(BF16)
        if t + 1 < terms:
            r = r - piece.astype(F32)
        d = _dot(piece, b, dims) if split == "a" else _dot(a, piece, dims)
        out = d if out is None else out + d
    return out


def _ssd_common(dtr, dtb, alog):
    L = dtr.shape[0]
    dt = _softplus(dtr + dtb)
    a = -jnp.exp(alog)
    adt = dt * a
    r = lax.broadcasted_iota(jnp.int32, (L, L), 0)
    c = lax.broadcasted_iota(jnp.int32, (L, L), 1)
    lower = r >= c
    tri = lower.astype(F32)
    cs = _dot_sel(tri, adt, "nn", split="b")
    cs_t = _dot_sel(adt, tri, "tt")
    return dt, a, cs, cs_t, lower


def _head_expand():
    hh = lax.broadcasted_iota(jnp.int32, (SSD_H, SSD_INNER), 0)
    jj = lax.broadcasted_iota(jnp.int32, (SSD_H, SSD_INNER), 1)
    return ((jj >= hh * SSD_P) & (jj < hh * SSD_P + SSD_P)).astype(F32)


def _head_reduce():
    hh = lax.broadcasted_iota(jnp.int32, (SSD_INNER, SSD_H), 1)
    jj = lax.broadcasted_iota(jnp.int32, (SSD_INNER, SSD_H), 0)
    return ((jj >= hh * SSD_P) & (jj < hh * SSD_P + SSD_P)).astype(F32)


def ssd_fwd(xbc, dtr, dtb, alog, dsk, nseq, comm=None):
    T = xbc.shape[0]
    S = T // nseq
    C = S // SSD_L
    L = SSD_L
    NP = SSD_H // 2

    def body(x_ref, b_ref, c_ref, dtr_ref, dtb_ref, alog_ref, dsk_ref, y_ref, prev_ref, st_ref):
        ci = pl.program_id(1)

        @pl.when(ci == 0)
        def _():
            st_ref[...] = jnp.zeros_like(st_ref)

        dt, a, cs, cs_t, lower = _ssd_common(dtr_ref[:, 0:SSD_H], dtb_ref[...], alog_ref[...])
        E = _head_expand()
        X = x_ref[...].astype(F32)
        dt_e = _dot_sel(dt, E)
        cs_e = _dot_sel(cs, E)
        csl_e = cs_e[L - 1:L, :]
        Xd = X * dt_e
        Xf = Xd * jnp.exp(csl_e - cs_e)
        e_e = jnp.exp(cs_e)
        skip = _dot_sel(dsk_ref[...], E) * X
        lane = lax.broadcasted_iota(jnp.int32, (1, 2 * SSD_P), 1)
        rowp = lax.broadcasted_iota(jnp.int32, (2 * SSD_P, 1), 0)
        for g in range(SSD_G):
            Bg = b_ref[:, g * SSD_N:(g + 1) * SSD_N]
            Cg = c_ref[:, g * SSD_N:(g + 1) * SSD_N]
            cb = _dot(Cg, Bg, "nt")
            for pp in range(NP // SSD_G):
                p = g * (NP // SSD_G) + pp
                sl = slice(p * 2 * SSD_P, (p + 1) * 2 * SSD_P)
                Xd_p = Xd[:, sl]
                yd = jnp.zeros((L, 2 * SSD_P), F32)
                for q in range(2):
                    h = 2 * p + q
                    m = jnp.where(lower, jnp.exp(jnp.minimum(cs[:, h:h + 1] - cs_t[h:h + 1, :], 0.0)), 0.0)
                    mask = (lane >= q * SSD_P) & (lane < (q + 1) * SSD_P)
                    yd = yd + _dot(cb * m, jnp.where(mask, Xd_p, 0.0))
                S0 = st_ref[p]
                prev_ref[0, 0, p] = S0
                z = _dot(Cg, S0, "nt")
                y_ref[:, sl] = (skip[:, sl] + yd + z * e_e[:, sl]).astype(y_ref.dtype)
                h0 = 2 * p
                dec = jnp.where(rowp < SSD_P, jnp.exp(cs[L - 1:L, h0:h0 + 1]), jnp.exp(cs[L - 1:L, h0 + 1:h0 + 2]))
                st_ref[p] = S0 * dec + _dot(Xf[:, sl], Bg, "tn")

    row = lambda b, c: (b * C + c, 0)
    small = pl.BlockSpec((1, SSD_H), lambda b, c: (0, 0))
    return _call_with_comm(
        body, (nseq, C), "ssd_fwd",
        [pl.BlockSpec((L, SSD_INNER), row),
         pl.BlockSpec((L, SSD_G * SSD_N), lambda b, c: (b * C + c, SSD_INNER // (SSD_G * SSD_N))),
         pl.BlockSpec((L, SSD_G * SSD_N), lambda b, c: (b * C + c, SSD_INNER // (SSD_G * SSD_N) + 1)),
         pl.BlockSpec((L, 128), row), small, small, small],
        [xbc, xbc, xbc, dtr, dtb, alog, dsk],
        [pl.BlockSpec((L, SSD_INNER), row), pl.BlockSpec((1, 1, NP, 2 * SSD_P, SSD_N), lambda b, c: (b, c, 0, 0, 0))],
        [jax.ShapeDtypeStruct((T, SSD_INNER), BF16), jax.ShapeDtypeStruct((nseq, C, NP, 2 * SSD_P, SSD_N), F32)],
        comm, scratch=[pltpu.VMEM((NP, 2 * SSD_P, SSD_N), F32)], sem=("parallel", "arbitrary"))


def ssd_bwd(xbc, dtr, dtb, alog, dsk, prev, dy, nseq, comm=None):
    T = xbc.shape[0]
    S = T // nseq
    C = S // SSD_L
    L = SSD_L
    NP = SSD_H // 2

    def body(x_ref, b_ref, c_ref, dtr_ref, dtb_ref, alog_ref, dsk_ref, prev_ref, dy_ref,
             dxbc_ref, ddtr_ref, ddtb_ref, dalog_ref, ddsk_ref, ds_ref, stg_ref):
        bi = pl.program_id(0)
        ci = pl.program_id(1)

        @pl.when(ci == 0)
        def _():
            ds_ref[...] = jnp.zeros_like(ds_ref)

        @pl.when((ci == 0) & (bi == 0))
        def _():
            ddtb_ref[...] = jnp.zeros_like(ddtb_ref)
            dalog_ref[...] = jnp.zeros_like(dalog_ref)
            ddsk_ref[...] = jnp.zeros_like(ddsk_ref)

        dtr = dtr_ref[:, 0:SSD_H]
        dtb = dtb_ref[...]
        dt, a, cs, cs_t, lower = _ssd_common(dtr, dtb, alog_ref[...])
        upper = lax.broadcasted_iota(jnp.int32, (L, L), 1) >= lax.broadcasted_iota(jnp.int32, (L, L), 0)
        E = _head_expand()
        ET = _head_reduce()
        X = x_ref[...].astype(F32)
        dY = dy_ref[...].astype(F32)
        dt_e = _dot_sel(dt, E)
        cs_e = _dot_sel(cs, E)
        csl_e = cs_e[L - 1:L, :]
        f_e = jnp.exp(csl_e - cs_e)
        e_e = jnp.exp(cs_e)
        dsk_e = _dot_sel(dsk_ref[...], E)
        Xd = X * dt_e
        Xf = Xd * f_e
        lane = lax.broadcasted_iota(jnp.int32, (1, 2 * SSD_P), 1)
        rowp = lax.broadcasted_iota(jnp.int32, (2 * SSD_P, 1), 0)
        hsel = lax.broadcasted_iota(jnp.int32, (1, SSD_H), 1)
        dcs = jnp.zeros((L, SSD_H), F32)
        dcsl = jnp.zeros((1, SSD_H), F32)
        for g in range(SSD_G):
            Bg = b_ref[:, g * SSD_N:(g + 1) * SSD_N]
            Cg = c_ref[:, g * SSD_N:(g + 1) * SSD_N]
            cb = _dot(Cg, Bg, "nt")
            cbt = _dot(Bg, Cg, "nt")
            dB = jnp.zeros((L, SSD_N), F32)
            dC = jnp.zeros((L, SSD_N), F32)
            for pp in range(NP // SSD_G):
                p = g * (NP // SSD_G) + pp
                sl = slice(p * 2 * SSD_P, (p + 1) * 2 * SSD_P)
                Xd_p = Xd[:, sl]
                dY_p = dY[:, sl]
                dXd_p = jnp.zeros((L, 2 * SSD_P), F32)
                for q in range(2):
                    h = 2 * p + q
                    mask = (lane >= q * SSD_P) & (lane < (q + 1) * SSD_P)
                    col = cs[:, h:h + 1]
                    rw = cs_t[h:h + 1, :]
                    m = jnp.where(lower, jnp.exp(jnp.minimum(col - rw, 0.0)), 0.0)
                    mt = jnp.where(upper, jnp.exp(jnp.minimum(rw - col, 0.0)), 0.0)
                    dYm = jnp.where(mask, dY_p, 0.0)
                    dW = _dot(dYm, Xd_p, "nt")
                    dWt = _dot(Xd_p, dYm, "nt")
                    w = cb * m
                    wt = cbt * mt
                    dC = dC + _dot(dW * m, Bg)
                    dB = dB + _dot(dWt * mt, Cg)
                    dXd_p = dXd_p + jnp.where(mask, _dot(wt, dY_p), 0.0)
                    qcol = jnp.sum(dW * w, axis=1, keepdims=True) - jnp.sum(dWt * wt, axis=1, keepdims=True)
                    dcs = dcs + qcol * (hsel == h).astype(F32)
                S0 = prev_ref[0, 0, p]
                dSn = ds_ref[p]
                dZ = dY_p * e_e[:, sl]
                dC = dC + _dot(dZ, S0)
                h0 = 2 * p
                el0 = jnp.exp(cs[L - 1:L, h0:h0 + 1])
                el1 = jnp.exp(cs[L - 1:L, h0 + 1:h0 + 2])
                dec = jnp.where(rowp < SSD_P, el0, el1)
                ds_ref[p] = dSn * dec + _dot(dZ, Cg, "tn")
                dXf_p = _dot(Bg, dSn, "nt")
                dB = dB + _dot(Xf[:, sl], dSn)
                rs = jnp.sum(dSn * S0, axis=1, keepdims=True)
                s0 = jnp.sum(jnp.where(rowp < SSD_P, rs, 0.0), axis=0, keepdims=True) * el0
                s1 = jnp.sum(jnp.where(rowp >= SSD_P, rs, 0.0), axis=0, keepdims=True) * el1
                dcsl = dcsl + s0 * (hsel == h0).astype(F32) + s1 * (hsel == h0 + 1).astype(F32)
                y_off = _dot(Cg, S0, "nt") * e_e[:, sl]
                t1 = dY_p * y_off - dXf_p * Xf[:, sl]
                r1 = jnp.where(lane < SSD_P, t1, 0.0)
                c0 = jnp.sum(r1, axis=1, keepdims=True)
                c1 = jnp.sum(t1 - r1, axis=1, keepdims=True)
                dcs = dcs + c0 * (hsel == h0).astype(F32) + c1 * (hsel == h0 + 1).astype(F32)
                t2 = dXf_p * Xf[:, sl]
                r2 = jnp.where(lane < SSD_P, t2, 0.0)
                dcsl = dcsl + jnp.sum(r2, keepdims=True) * (hsel == h0).astype(F32) \
                    + jnp.sum(t2 - r2, keepdims=True) * (hsel == h0 + 1).astype(F32)
                stg_ref[:, sl] = dXd_p + dXf_p * f_e[:, sl]
            dxbc_ref[:, SSD_INNER + g * SSD_N:SSD_INNER + (g + 1) * SSD_N] = dB.astype(dxbc_ref.dtype)
            dxbc_ref[:, SSD_INNER + (SSD_G + g) * SSD_N:SSD_INNER + (SSD_G + g + 1) * SSD_N] = dC.astype(dxbc_ref.dtype)
        dXd = stg_ref[...]
        dxbc_ref[:, 0:SSD_INNER] = (dXd * dt_e + dsk_e * dY).astype(dxbc_ref.dtype)
        rowl = lax.broadcasted_iota(jnp.int32, (L, 1), 0)
        dcs = dcs + jnp.where(rowl == L - 1, dcsl, 0.0)
        dalpha = _dot_sel(upper.astype(F32), dcs, split="b")
        ddt = _dot_sel(dXd * X, ET, terms=2) + dalpha * a
        dalog_ref[...] += jnp.sum(dalpha * dt, axis=0, keepdims=True) * a
        ddtr = ddt * _sigmoid(dtr + dtb)
        spread = (lax.broadcasted_iota(jnp.int32, (SSD_H, 128), 0) == lax.broadcasted_iota(jnp.int32, (SSD_H, 128), 1)).astype(F32)
        ddtr_ref[...] = _dot(ddtr, spread).astype(ddtr_ref.dtype)
        ddtb_ref[...] += jnp.sum(ddtr, axis=0, keepdims=True)
        ddsk_ref[...] += jnp.sum(_dot_sel(dY * X, ET, terms=2), axis=0, keepdims=True)

    rowr = lambda b, c: (b * C + (C - 1 - c), 0)
    small = pl.BlockSpec((1, SSD_H), lambda b, c: (0, 0))
    return _call_with_comm(
        body, (nseq, C), "ssd_bwd",
        [pl.BlockSpec((L, SSD_INNER), rowr),
         pl.BlockSpec((L, SSD_G * SSD_N), lambda b, c: (b * C + (C - 1 - c), SSD_INNER // (SSD_G * SSD_N))),
         pl.BlockSpec((L, SSD_G * SSD_N), lambda b, c: (b * C + (C - 1 - c), SSD_INNER // (SSD_G * SSD_N) + 1)),
         pl.BlockSpec((L, 128), rowr), small, small, small,
         pl.BlockSpec((1, 1, NP, 2 * SSD_P, SSD_N), lambda b, c: (b, C - 1 - c, 0, 0, 0)),
         pl.BlockSpec((L, SSD_INNER), rowr)],
        [xbc, xbc, xbc, dtr, dtb, alog, dsk, prev, dy],
        [pl.BlockSpec((L, CONV_CH), rowr), pl.BlockSpec((L, 128), rowr), small, small, small],
        [jax.ShapeDtypeStruct((T, CONV_CH), BF16), jax.ShapeDtypeStruct((T, 128), BF16),
         jax.ShapeDtypeStruct((1, SSD_H), F32), jax.ShapeDtypeStruct((1, SSD_H), F32), jax.ShapeDtypeStruct((1, SSD_H), F32)],
        comm, scratch=[pltpu.VMEM((NP, 2 * SSD_P, SSD_N), F32), pltpu.VMEM((L, SSD_INNER), F32)], sem=("arbitrary", "arbitrary"))


SLOT = 128
ATT_T = 512
ATT_HP = 1
LOG2E = math.log2(math.e)
Q_SCALE = QK ** -0.5 * LOG2E


def _col_to_row(col):
    n = col.shape[0]
    eye = lax.broadcasted_iota(jnp.int32, (n, n), 0) == lax.broadcasted_iota(jnp.int32, (n, n), 1)
    return jnp.sum(jnp.where(eye, col, 0.0), axis=0, keepdims=True)


def attn_slot_fwd(q, k, v, nseq, comm=None):
    T = q.shape[0]
    S = T // nseq
    t = min(ATT_T, S)
    nb = S // t
    cols = [slice(h * SLOT, (h + 1) * SLOT) for h in range(ATT_HP)]

    def body(q_ref, k_ref, v_ref, o_ref, lse_ref):
        causal = lax.broadcasted_iota(jnp.int32, (t, t), 1) <= lax.broadcasted_iota(jnp.int32, (t, t), 0)
        for qi in range(nb):
            rows = slice(qi * t, (qi + 1) * t)
            state = [None] * ATT_HP
            for kj in range(qi + 1):
                keys = slice(kj * t, (kj + 1) * t)
                for h, c in enumerate(cols):
                    s = _dot(q_ref[rows, c], k_ref[keys, c], "nt")
                    if kj == qi:
                        s = jnp.where(causal, s, -1e30)
                    bm = jnp.max(s, axis=1, keepdims=True)
                    if kj == 0:
                        p = jnp.exp2(s - bm)
                        state[h] = (bm, jnp.sum(p, axis=1, keepdims=True), _dot(p, v_ref[keys, c]))
                    else:
                        m, l, acc = state[h]
                        m_new = jnp.maximum(m, bm)
                        corr = jnp.exp2(m - m_new)
                        p = jnp.exp2(s - m_new)
                        state[h] = (m_new, l * corr + jnp.sum(p, axis=1, keepdims=True), acc * corr + _dot(p, v_ref[keys, c]))
            for h, c in enumerate(cols):
                m, l, acc = state[h]
                o_ref[rows, c] = (acc / l).astype(o_ref.dtype)
                lse_ref[0, h, :, rows] = _col_to_row(m + jnp.log2(l))

    blk = pl.BlockSpec((S, ATT_HP * SLOT), lambda b, h: (b, h))
    return _call_with_comm(
        body, (nseq, MLA_H // ATT_HP), "attn_fwd", [blk, blk, blk], [q, k, v],
        [blk, pl.BlockSpec((1, ATT_HP, 1, S), lambda b, h: (b, h, 0, 0))],
        [jax.ShapeDtypeStruct((T, MLA_H * SLOT), BF16), jax.ShapeDtypeStruct((nseq, MLA_H, 1, S), F32)], comm)


def attn_slot_bwd(q, k, v, o, lse, do, nseq, comm=None):
    T = q.shape[0]
    S = T // nseq
    t = min(ATT_T, S)
    nb = S // t
    scale = QK ** -0.5
    cols = [slice(h * SLOT, (h + 1) * SLOT) for h in range(ATT_HP)]

    def body(q_ref, k_ref, v_ref, o_ref, lse_ref, do_ref, dq_ref, dk_ref, dv_ref, dqa_ref):
        causal_t = lax.broadcasted_iota(jnp.int32, (t, t), 0) <= lax.broadcasted_iota(jnp.int32, (t, t), 1)
        ones = jnp.ones((8, SLOT), F32)
        delta = {}
        for qi in range(nb):
            sl = slice(qi * t, (qi + 1) * t)
            for h, c in enumerate(cols):
                prod = do_ref[sl, c].astype(F32) * o_ref[sl, c].astype(F32)
                delta[h, qi] = _dot_sel(ones, prod, "nt", split="b", terms=2)[0:1, :]
        for kj in range(nb):
            ks = slice(kj * t, (kj + 1) * t)
            dk = [None] * ATT_HP
            dv = [None] * ATT_HP
            for qi in range(kj, nb):
                sl = slice(qi * t, (qi + 1) * t)
                for h, c in enumerate(cols):
                    kb, vb, qb, dob = k_ref[ks, c], v_ref[ks, c], q_ref[sl, c], do_ref[sl, c]
                    st = _dot(kb, qb, "nt")
                    pt = jnp.exp2(st - lse_ref[0, h, :, sl])
                    if qi == kj:
                        pt = jnp.where(causal_t, pt, 0.0)
                    dpt = _dot(vb, dob, "nt")
                    dst = (pt * (dpt - delta[h, qi])).astype(BF16)
                    dvc = _dot(pt, dob)
                    dkc = _dot(dst, qb) * (1.0 / LOG2E)
                    dv[h] = dvc if dv[h] is None else dv[h] + dvc
                    dk[h] = dkc if dk[h] is None else dk[h] + dkc
                    dqc = _dot(dst, kb, "tn") * scale
                    if kj > 0:
                        dqc = dqc + dqa_ref[sl, c]
                    if qi == kj:
                        dq_ref[sl, c] = dqc.astype(dq_ref.dtype)
                    else:
                        dqa_ref[sl, c] = dqc
            for h, c in enumerate(cols):
                dk_ref[ks, c] = dk[h].astype(dk_ref.dtype)
                dv_ref[ks, c] = dv[h].astype(dv_ref.dtype)

    blk = pl.BlockSpec((S, ATT_HP * SLOT), lambda b, h: (b, h))
    lse_spec = pl.BlockSpec((1, ATT_HP, 1, S), lambda b, h: (b, h, 0, 0))
    W = MLA_H * SLOT
    return _call_with_comm(
        body, (nseq, MLA_H // ATT_HP), "attn_bwd", [blk, blk, blk, blk, lse_spec, blk], [q, k, v, o, lse, do], [blk, blk, blk],
        [jax.ShapeDtypeStruct((T, W), BF16)] * 3, comm, scratch=[pltpu.VMEM((S, ATT_HP * SLOT), F32)])


def _rope_coeffs(pos, inv):
    half = ROPE // 2
    ang = pos * inv
    lane = lax.broadcasted_iota(jnp.int32, (1, SLOT), 1)
    sn = jnp.sin(ang)
    C = jnp.where(lane < NOPE, 1.0, jnp.where(lane < QK, jnp.cos(ang), 0.0))
    Sg = jnp.where((lane >= NOPE) & (lane < NOPE + half), -sn, jnp.where((lane >= NOPE + half) & (lane < QK), sn, 0.0))
    return C, Sg


def _rope_inputs(positions):
    half = ROPE // 2
    inv = ROPE_THETA ** (-jnp.arange(0, ROPE, 2, dtype=F32) / ROPE)
    row = jnp.zeros((1, SLOT), F32).at[0, NOPE:NOPE + half].set(inv).at[0, NOPE + half:QK].set(inv)
    return positions.astype(F32).reshape(-1, 1), row


def _place_k_rope(kr_lanes):
    r = lax.broadcasted_iota(jnp.int32, (SLOT, SLOT), 0)
    c = lax.broadcasted_iota(jnp.int32, (SLOT, SLOT), 1)
    return _dot_sel(kr_lanes, ((c == r + NOPE) & (r < ROPE)).astype(F32))


def rope_table(pos, inv):
    return rowwise(_rope_coeffs, [pos], [inv], [(SLOT, F32), (SLOT, F32)], [], "rope_table")


def rope_q_epilogue(accs, ex):
    C, Sg = ex[0], ex[1]
    reps = accs[0].shape[1] // SLOT
    return ((accs[0] * jnp.tile(C, (1, reps)) + _rope_swap(accs[0]) * jnp.tile(Sg, (1, reps))) * Q_SCALE,)


def rope_k_epilogue(accs, ex):
    C, Sg = ex[0], ex[1]
    kr = _place_k_rope(ex[2][:, SLOT:2 * SLOT])
    kr = kr * C + _rope_swap(kr) * Sg
    return (accs[0] + jnp.tile(kr, (1, accs[0].shape[1] // SLOT)),)


def _rope_swap(x):
    W = x.shape[1]
    half = ROPE // 2
    lane = lax.broadcasted_iota(jnp.int32, (1, W), 1) & (SLOT - 1)
    up = pltpu.roll(x, W - half, axis=1)
    dn = pltpu.roll(x, half, axis=1)
    return jnp.where((lane >= NOPE) & (lane < NOPE + half), up, jnp.where((lane >= NOPE + half) & (lane < QK), dn, 0.0))


def rope_slot_bwd(dq, dk, C, Sg, name):
    def fn(dqv, dkv, C, Sg):
        ct, stl = jnp.tile(C, (1, MLA_H)), jnp.tile(Sg, (1, MLA_H))
        dqo = dqv * ct - _rope_swap(dqv) * stl
        tot = dkv[:, 0:SLOT]
        for h in range(1, MLA_H):
            tot = tot + dkv[:, h * SLOT:(h + 1) * SLOT]
        u = tot * C - _rope_swap(tot) * Sg
        r = lax.broadcasted_iota(jnp.int32, (SLOT, SLOT), 0)
        c = lax.broadcasted_iota(jnp.int32, (SLOT, SLOT), 1)
        unplace = ((r == c + NOPE) & (c < ROPE)).astype(F32)
        return dqo, dkv, _dot_sel(u, unplace, terms=2)
    W = MLA_H * SLOT
    return rowwise(fn, [dq, dk, C, Sg], [], [(W, BF16), (W, BF16), (SLOT, BF16)], [], name)


XA_BLK = 512


def xattn_fwd(q, k, v, nseq, comm=None):
    T = q.shape[0]
    S = T // nseq
    M = k.shape[0] // nseq
    tq = min(XA_BLK, S)
    nq = S // tq
    scale = XA_D ** -0.5

    def body(q_ref, k_ref, v_ref, o_ref):
        s = _dot(q_ref[...], k_ref[...], "nt") * scale
        p = jnp.exp(s - jnp.max(s, axis=1, keepdims=True))
        p = p / jnp.sum(p, axis=1, keepdims=True)
        o_ref[...] = _dot(p, v_ref[...]).astype(o_ref.dtype)

    qs = pl.BlockSpec((tq, XA_D), lambda b, h, i: (b * nq + i, h))
    ks = pl.BlockSpec((M, XA_D), lambda b, h, i: (b, h))
    return _call_with_comm(body, (nseq, XA_H, nq), "xattn_fwd", [qs, ks, ks], [q, k, v], [qs],
                           [jax.ShapeDtypeStruct((T, XA_H * XA_D), BF16)], comm)


def xattn_bwd(q, k, v, do, nseq):
    T = q.shape[0]
    S = T // nseq
    M = k.shape[0] // nseq
    tq = min(XA_BLK, S)
    nq = S // tq
    scale = XA_D ** -0.5

    def body(q_ref, k_ref, v_ref, do_ref, dq_ref, dk_ref, dv_ref):
        @pl.when(pl.program_id(2) == 0)
        def _():
            dk_ref[...] = jnp.zeros_like(dk_ref)
            dv_ref[...] = jnp.zeros_like(dv_ref)

        qb, kb, vb, dob = q_ref[...], k_ref[...], v_ref[...], do_ref[...]
        s = _dot(qb, kb, "nt") * scale
        p = jnp.exp(s - jnp.max(s, axis=1, keepdims=True))
        p = p / jnp.sum(p, axis=1, keepdims=True)
        dp = _dot(dob, vb, "nt")
        ds = p * (dp - jnp.sum(dp * p, axis=1, keepdims=True)) * scale
        dq_ref[...] = _dot(ds, kb).astype(dq_ref.dtype)
        dk_ref[...] += _dot(ds, qb, "tn")
        dv_ref[...] += _dot(p, dob, "tn")

    qs = pl.BlockSpec((tq, XA_D), lambda b, h, i: (b * nq + i, h))
    ks = pl.BlockSpec((M, XA_D), lambda b, h, i: (b, h))
    return pl.pallas_call(
        body, grid=(nseq, XA_H, nq), name="xattn_bwd", in_specs=[qs, ks, ks, qs], out_specs=[qs, ks, ks],
        out_shape=[jax.ShapeDtypeStruct((T, XA_H * XA_D), BF16), jax.ShapeDtypeStruct(k.shape, F32),
                   jax.ShapeDtypeStruct(k.shape, F32)],
        compiler_params=_cp("parallel", "parallel", "arbitrary"),
    )(q, k, v, do)


CONV_BLK = 256


def _shift_down(x, s, rows):
    if s == 0:
        return x
    return jnp.where(rows >= s, pltpu.roll(x, s, axis=0), 0.0)


def _shift_up(x, s, rows):
    if s == 0:
        return x
    S = x.shape[0]
    return jnp.where(rows < S - s, pltpu.roll(x, S - s, axis=0), 0.0)


def conv_fwd(x, w, b, nseq):
    T, CH = x.shape
    S = T // nseq

    def body(x_ref, w_ref, b_ref, o_ref):
        xv = x_ref[...].astype(F32)
        rows = lax.broadcasted_iota(jnp.int32, (S, 1), 0)
        c = jnp.zeros_like(xv) + b_ref[...]
        for kk in range(CONV_K):
            c = c + w_ref[kk:kk + 1, :] * _shift_down(xv, CONV_K - 1 - kk, rows)
        o_ref[...] = (c * _sigmoid(c)).astype(o_ref.dtype)

    xs = pl.BlockSpec((S, CONV_BLK), lambda j, bb: (bb, j))
    return pl.pallas_call(
        body, grid=(CH // CONV_BLK, nseq), name="conv_fwd",
        in_specs=[xs, pl.BlockSpec((CONV_K, CONV_BLK), lambda j, bb: (0, j)), pl.BlockSpec((1, CONV_BLK), lambda j, bb: (0, j))],
        out_specs=xs, out_shape=jax.ShapeDtypeStruct((T, CH), BF16),
        compiler_params=_cp("parallel", "parallel"),
    )(x, w, b)


def conv_bwd(x, w, b, dout, nseq):
    T, CH = x.shape
    S = T // nseq

    def body(x_ref, w_ref, b_ref, do_ref, dx_ref, dw_ref, db_ref):
        @pl.when(pl.program_id(1) == 0)
        def _():
            dw_ref[...] = jnp.zeros_like(dw_ref)
            db_ref[...] = jnp.zeros_like(db_ref)

        xv = x_ref[...].astype(F32)
        rows = lax.broadcasted_iota(jnp.int32, (S, 1), 0)
        c = jnp.zeros_like(xv) + b_ref[...]
        sh = [_shift_down(xv, CONV_K - 1 - kk, rows) for kk in range(CONV_K)]
        for kk in range(CONV_K):
            c = c + w_ref[kk:kk + 1, :] * sh[kk]
        sg = _sigmoid(c)
        dc = do_ref[...].astype(F32) * sg * (1.0 + c * (1.0 - sg))
        dx = jnp.zeros_like(xv)
        for kk in range(CONV_K):
            dx = dx + w_ref[kk:kk + 1, :] * _shift_up(dc, CONV_K - 1 - kk, rows)
            dw_ref[kk:kk + 1, :] += jnp.sum(dc * sh[kk], axis=0, keepdims=True)
        dx_ref[...] = dx.astype(dx_ref.dtype)
        db_ref[...] += jnp.sum(dc, axis=0, keepdims=True)

    xs = pl.BlockSpec((S, CONV_BLK), lambda j, bb: (bb, j))
    ws = pl.BlockSpec((CONV_K, CONV_BLK), lambda j, bb: (0, j))
    bs = pl.BlockSpec((1, CONV_BLK), lambda j, bb: (0, j))
    return pl.pallas_call(
        body, grid=(CH // CONV_BLK, nseq), name="conv_bwd",
        in_specs=[xs, ws, bs, xs], out_specs=[xs, ws, bs],
        out_shape=[jax.ShapeDtypeStruct((T, CH), BF16), jax.ShapeDtypeStruct((CONV_K, CH), F32),
                   jax.ShapeDtypeStruct((1, CH), F32)],
        compiler_params=_cp("parallel", "arbitrary"),
    )(x, w, b, dout)


def _dims(a, b, mode):
    M = a.shape[1] if mode[0] == "t" else a.shape[0]
    K = a.shape[0] if mode[0] == "t" else a.shape[1]
    N = b.shape[0] if mode[1] == "t" else b.shape[1]
    return M, K, N


def _tile(dim, prefs):
    for p in prefs:
        if dim % p == 0:
            return p
    return dim


def mm(groups, out_dtypes, name, tm=None, tn=None, tk=None, epi=None, extras=(), comm=None, sub=1, n_sum=0):
    a0, b0, m0 = groups[0][0]
    M, K0, N = _dims(a0, b0, m0)
    tm = tm or _tile(M, (1024, 512, 256, 128))
    tn = tn or _tile(N, (1024, 512, 256, 128))
    flat = [p for g in groups for p in g]
    nk = 1 if tk is None else K0 // tk
    in_specs, args = [], []
    for a, b, mode in flat:
        _, K, _ = _dims(a, b, mode)
        kb = K if tk is None else tk
        in_specs.append(pl.BlockSpec((kb, tm), lambda i, j, k: (k, i)) if mode[0] == "t"
                        else pl.BlockSpec((tm, kb), lambda i, j, k: (i, k)))
        in_specs.append(pl.BlockSpec((tn, kb), lambda i, j, k: (j, k)) if mode[1] == "t"
                        else pl.BlockSpec((kb, tn), lambda i, j, k: (k, j)))
        args += [a, b]
    kinds = []
    for e in extras:
        kind, e = e if isinstance(e, tuple) else ("vec" if e.shape[0] == 1 and M != 1 else "tile", e)
        in_specs.append({"tile": pl.BlockSpec((tm, tn), lambda i, j, k: (i, j)),
                         "vec": pl.BlockSpec((1, tn), lambda i, j, k: (0, j)),
                         "rows": pl.BlockSpec((tm, e.shape[1]), lambda i, j, k: (i, 0)),
                         "whole": pl.BlockSpec(e.shape, lambda i, j, k: (0, 0))}[kind])
        kinds.append(kind)
        args.append(e)
    n_in = len(args)
    n_main = len(out_dtypes)
    n_out = n_main + n_sum
    assert n_sum == 0 or (tn == N and tk is None)
    ng = len(groups)
    sizes = [len(g) for g in groups]

    def body(*refs):
        ins, outs, accs = refs[:n_in], refs[n_in:n_in + n_out], refs[n_in + n_out:]
        kk = pl.program_id(2)

        def dots(rs):
            vals, pos = [], 0
            for gi in range(ng):
                acc = None
                for _ in range(sizes[gi]):
                    mode = flat[pos // 2][2]
                    av = ins[pos][:, rs] if mode[0] == "t" else ins[pos][rs, :]
                    d = _dot(av, ins[pos + 1][...], mode)
                    acc = d if acc is None else acc + d
                    pos += 2
                vals.append(acc)
            return vals

        def finish(accv, rs, first_chunk=True):
            ex = [(r[rs, :] if kind in ("tile", "rows") else r[...]).astype(F32) for kind, r in zip(kinds, ins[2 * len(flat):])]
            res = epi(accv, ex) if epi is not None else tuple(accv)
            for o, r in zip(outs[:n_main], res[:n_main]):
                o[rs, :] = r.astype(o.dtype)
            for o, r in zip(outs[n_main:], res[n_main:]):
                if first_chunk:
                    @pl.when(pl.program_id(0) == 0)
                    def _():
                        o[...] = r

                    @pl.when(pl.program_id(0) > 0)
                    def _():
                        o[...] += r
                else:
                    o[...] += r

        if nk == 1:
            for r in range(sub):
                rs = slice(r * (tm // sub), (r + 1) * (tm // sub))
                finish(dots(rs), rs, r == 0)
        else:
            vals = dots(slice(0, tm))
            finish = functools.partial(finish, rs=slice(0, tm))
            @pl.when(kk == 0)
            def _():
                for ar, vv in zip(accs, vals):
                    ar[...] = vv

            @pl.when(kk > 0)
            def _():
                for ar, vv in zip(accs, vals):
                    ar[...] += vv

            @pl.when(kk == nk - 1)
            def _():
                finish([ar[...] for ar in accs])

    grid = (M // tm, N // tn, nk)
    out_specs = [pl.BlockSpec((tm, tn), lambda i, j, k: (i, j)) for _ in out_dtypes] \
        + [pl.BlockSpec((1, tn), lambda i, j, k: (0, j))] * n_sum
    out_shape = [jax.ShapeDtypeStruct((M, N), dt) for dt in out_dtypes] + [jax.ShapeDtypeStruct((1, N), F32)] * n_sum
    scratch = [pltpu.VMEM((tm, tn), F32) for _ in range(ng if nk > 1 else 0)]
    sem = ("arbitrary" if n_sum else "parallel", "parallel", "arbitrary")
    if comm is not None:
        body = _attach(comm, body, n_in, n_out, *_grid_ends(grid))
        in_specs, args = in_specs + [HBM_SPEC] * len(comm.inputs), args + comm.inputs
        out_specs, out_shape = out_specs + [HBM_SPEC] * len(comm.out_shapes), out_shape + comm.out_shapes
        scratch, sem = scratch + comm.sems, ("arbitrary",) * 3
    return pl.pallas_call(body, grid=grid, name=name, in_specs=in_specs, out_specs=out_specs, out_shape=out_shape,
                          scratch_shapes=scratch, compiler_params=_cp(*sem))(*args)


def mm1(a, b, mode, out_dtype, name, **kw):
    return mm([[(a, b, mode)]], [out_dtype], name, **kw)[0]


ROW_BLK = 512


def rowwise(fn, rows, consts, outs, accs, name, tb=ROW_BLK, comm=None):
    rows = [r if isinstance(r, tuple) else (r, r.shape[1], 0) for r in rows]
    T = rows[0][0].shape[0]
    tb = min(tb, T)
    n_r, n_c, n_o, n_a = len(rows), len(consts), len(outs), len(accs)

    def body(*refs):
        vals = [r[...].astype(F32) for r in refs[:n_r + n_c]]
        res = fn(*vals)
        o_refs = refs[n_r + n_c:n_r + n_c + n_o]
        a_refs = refs[n_r + n_c + n_o:]
        for o, r in zip(o_refs, res[:n_o]):
            o[...] = r.astype(o.dtype)
        if n_a:
            @pl.when(pl.program_id(0) == 0)
            def _():
                for ar in a_refs:
                    ar[...] = jnp.zeros_like(ar)
            for ar, r in zip(a_refs, res[n_o:]):
                ar[...] += r

    return _call_with_comm(
        body, (T // tb,), name,
        [pl.BlockSpec((tb, w), functools.partial(lambda i, j: (i, j), j=j)) for _, w, j in rows]
        + [pl.BlockSpec(c.shape, lambda i: (0, 0)) for c in consts],
        [r[0] for r in rows] + list(consts),
        [pl.BlockSpec((tb, d), lambda i: (i, 0)) for d, _ in outs] + [pl.BlockSpec(s, lambda i: (0, 0)) for s in accs],
        [jax.ShapeDtypeStruct((T, d), dt) for d, dt in outs] + [jax.ShapeDtypeStruct(s, F32) for s in accs],
        comm, sem=("arbitrary" if n_a else "parallel",))


def _rms_stats(x):
    r = lax.rsqrt(jnp.mean(x * x, axis=-1, keepdims=True) + EPS)
    return r, x * r


def _rms_bwd(x, g, dy):
    r, xn = _rms_stats(x)
    dyg = dy * g
    dx = r * (dyg - xn * jnp.mean(dyg * xn, axis=-1, keepdims=True))
    return dx, jnp.sum(dy * xn, axis=0, keepdims=True)


def rms_fwd(x, g, name, comm=None):
    res = rowwise(lambda xv, gv: (_rms_stats(xv)[1] * gv,), [x], [g], [(x.shape[1], BF16)], [], name, comm=comm)
    return res[0] if comm is None else (res[0], res[1:])


def rms_bwd(x, g, dy, name, resid=None, dx_dtype=F32):
    def fn(*v):
        if resid is None:
            xv, dyv, gv = v
            dx, dg = _rms_bwd(xv, gv, dyv)
        else:
            xv, dyv, rv, gv = v
            dx, dg = _rms_bwd(xv, gv, dyv)
            dx = dx + rv
        return dx, dg
    rows = [x, dy] + ([] if resid is None else [resid])
    return rowwise(fn, rows, [g], [(x.shape[1], dx_dtype)], [(1, x.shape[1])], name)


def mm_rms_bwd(pairs, x, g, name, resid=None, dx_dtype=F32, comm=None, token=None):
    def epi(accs, ex):
        dx, dg = _rms_bwd(ex[0], ex[-1], accs[0])
        return (dx if resid is None else dx + ex[1]), dg
    extras = [x] + ([] if resid is None else [resid]) + ([] if token is None else [("whole", token)]) + [g]
    return mm([pairs], [dx_dtype], name, tm=min(256, x.shape[0]), tn=x.shape[1], epi=epi, extras=extras, comm=comm, n_sum=1)


def mm_resid(a, b, x, g, wgt, name, comm=None, target=None):
    def epi(accs, ex):
        y = ex[0] + wgt * _rms_stats(accs[0])[1] * ex[1]
        if target is None:
            return accs[0], y
        d = y - ex[2]
        return accs[0], d / D, jnp.sum(d * d, axis=0, keepdims=True)
    return mm([[(a, b, "nn")]], [F32, F32], name, tm=min(512, a.shape[0]), tn=b.shape[1], epi=epi,
              extras=[x, g] + ([] if target is None else [target]), sub=2, comm=comm, n_sum=0 if target is None else 1)


def resid_bwd(h, g, dy, wgt, name):
    def fn(hv, dyv, gv):
        dx, dg = _rms_bwd(hv, gv, dyv)
        return wgt * dx, wgt * dg
    return rowwise(fn, [h, dy], [g], [(h.shape[1], BF16)], [(1, h.shape[1])], name)


def _silu_parts(g):
    s = _sigmoid(g)
    return g * s, s * (1.0 + g * (1.0 - s))


def gated_norm_fwd(y, z, g, name):
    W = SSD_INNER // SSD_G

    def fn(yv, zv, gv):
        yg = yv * _silu_parts(zv)[0]
        return (jnp.concatenate([_rms_stats(yg[:, i * W:(i + 1) * W])[1] for i in range(SSD_G)], axis=1) * gv,)
    return rowwise(fn, [y, z], [g], [(SSD_INNER, BF16)], [], name)[0]


def gated_norm_bwd(y, z, dyn, g, name):
    W = SSD_INNER // SSD_G

    def fn(yv, zv, dv, gv):
        sil, dsil = _silu_parts(zv)
        yg = yv * sil
        parts = [_rms_bwd(yg[:, i * W:(i + 1) * W], gv[:, i * W:(i + 1) * W], dv[:, i * W:(i + 1) * W]) for i in range(SSD_G)]
        dyg = jnp.concatenate([p[0] for p in parts], axis=1)
        dg = jnp.concatenate([p[1] for p in parts], axis=1)
        return dyg * sil, dyg * yv * dsil, dg
    return rowwise(fn, [y, z, dyn], [g], [(SSD_INNER, BF16), (SSD_INNER, BF16)], [(1, SSD_INNER)], name)


def merge_fwd(gl, ys, ym, gb, name):
    def fn(glv, ysv, ymv, gbv):
        gt = _sigmoid(glv + gbv)
        return (gt[:, :D] * ysv + gt[:, D:] * ymv,)
    return rowwise(fn, [gl, ys, ym], [gb], [(D, BF16)], [], name)[0]


def merge_bwd(gl, ys, ym, dm, gb, name):
    def fn(glv, ysv, ymv, dmv, gbv):
        gt = _sigmoid(glv + gbv)
        gs, gm = gt[:, :D], gt[:, D:]
        dgl = jnp.concatenate([dmv * ysv * gs * (1.0 - gs), dmv * ymv * gm * (1.0 - gm)], axis=1)
        return dmv * gs, dmv * gm, dgl, jnp.sum(dgl, axis=0, keepdims=True)
    return rowwise(fn, [gl, ys, ym, dm], [gb], [(D, BF16), (D, BF16), (2 * D, BF16)], [(1, 2 * D)], name)


def _adamw_math(wv, gv, mv, vv):
    mn = B1 * mv + (1.0 - B1) * gv
    vn = B2 * vv + (1.0 - B2) * (gv * gv)
    mh = mn / (1.0 - B1 ** STEP)
    vh = vn / (1.0 - B2 ** STEP)
    return -LR * (mh / (jnp.sqrt(vh) + AEPS) + WD * wv), mn, vn


def adamw(w, g, m, v, name):
    R, C = w.shape
    tb = _tile(R, (256, 128, 64, 32, 16, 8))
    return rowwise(_adamw_math, [w, g, m, v], [], [(C, F32)] * 3, [], name, tb=tb)


def adamw_small(packed, ws, ms, vs):
    k = len(ws)
    sizes = [x.shape[1] for x in ws]

    def body(*refs):
        p_ref, w_refs, m_refs, v_refs = refs[0], refs[1:1 + k], refs[1 + k:1 + 2 * k], refs[1 + 2 * k:1 + 3 * k]
        outs = refs[1 + 3 * k:]
        r0 = 0
        for i, n in enumerate(sizes):
            nr = -(-n // 128)
            g = jnp.concatenate([p_ref[r0 + r:r0 + r + 1, :] for r in range(nr)], axis=1)[:, :n]
            r0 += nr
            outs[i][...] = g
            outs[k + i][...], outs[2 * k + i][...], outs[3 * k + i][...] = _adamw_math(w_refs[i][...], g, m_refs[i][...], v_refs[i][...])

    res = pl.pallas_call(body, name="adamw_small",
                         out_shape=[jax.ShapeDtypeStruct((1, n), F32) for _ in range(4) for n in sizes])(packed, *ws, *ms, *vs)
    return [res[j * k:(j + 1) * k] for j in range(4)]


def adamw_from_slots(recv, piece, w, m, v, name, token=None):
    K, n = w.shape
    ns = recv.shape[0]
    assert recv.shape[2] == n and recv.shape[1] % K == 0
    tb = _tile(K, (256, 176, 128, 64, 32, 16, 8)) if K % 8 == 0 else K
    r_spec = pl.BlockSpec((ns, tb, n), lambda i: (0, piece * (K // tb) + i, 0))
    w_spec = pl.BlockSpec((tb, n), lambda i: (i, 0))

    def body(r_ref, w_ref, m_ref, v_ref, *rest):
        g_ref, d_ref, mo_ref, vo_ref = rest[-4:]
        g = r_ref[0].astype(F32)
        for s in range(1, ns):
            g = g + r_ref[s].astype(F32)
        g_ref[...] = g
        d_ref[...], mo_ref[...], vo_ref[...] = _adamw_math(w_ref[...], g, m_ref[...], v_ref[...])

    extra = [] if token is None else [token]
    return pl.pallas_call(
        body, grid=(K // tb,), name=name,
        in_specs=[r_spec, w_spec, w_spec, w_spec] + [pl.BlockSpec(t.shape, lambda i: (0, 0)) for t in extra], out_specs=[w_spec] * 4,
        out_shape=[jax.ShapeDtypeStruct((K, n), F32)] * 4, compiler_params=_cp("parallel"),
    )(recv, w, m, v, *extra)


def _me():
    return lax.axis_index("x"), lax.axis_index("y"), lax.axis_index("c")


def _dev_index():
    x, y, c = _me()
    return 4 * x + 2 * y + c


HBM_SPEC = pl.BlockSpec(memory_space=pl.ANY)


class GatherComm:
    def __init__(self, shards):
        self.inputs = [s for s, _ in shards]
        self.rows = [list(r) for _, r in shards]
        n = len(shards)
        self.out_shapes = [jax.ShapeDtypeStruct((N_DEV, r, s.shape[1]), s.dtype) for s, rows in shards for r in rows]
        self.sems = [pltpu.SemaphoreType.DMA((7 * n,)), pltpu.SemaphoreType.DMA((7 * n,)), pltpu.SemaphoreType.DMA((n,))]

    def _plan(self, x_refs, out_refs, sems):
        send_sems, recv_sems, local_sems = sems
        x, y, c = _me()
        me, sibling = (x, y, c), (x, y, 1 - c)
        chips = [(1 - x, y), (x, 1 - y), (1 - x, 1 - y)]
        index = lambda px, py, pc: 4 * px + 2 * py + pc
        mine, first, passed, whole = [], [], [], []
        pos = 0
        for i, rows in enumerate(self.rows):
            kw = lambda k: dict(send_sem=send_sems.at[7 * i + k], recv_sem=recv_sems.at[7 * i + k], device_id_type=MESH)
            r0 = 0
            fwd = [[] for _ in chips]
            for j, nr in enumerate(rows):
                out, src = out_refs[pos + j], x_refs[i].at[pl.ds(r0, nr)]
                mine.append(pltpu.make_async_copy(src, out.at[index(*me)], local_sems.at[i]))
                first.append(pltpu.make_async_remote_copy(src_ref=src, dst_ref=out.at[index(*me)], device_id=sibling, **kw(0)))
                for jj, chip in enumerate(chips):
                    first.append(pltpu.make_async_remote_copy(src_ref=src, dst_ref=out.at[index(*me)], device_id=(*chip, c),
                                                              **kw(1 + jj)))
                    blk = out.at[index(*chip, c)]
                    fwd[jj].append(pltpu.make_async_remote_copy(src_ref=blk, dst_ref=blk, device_id=sibling, **kw(4 + jj)))
                r0 += nr
            passed.append(fwd)
            whole.append([pltpu.make_async_remote_copy(src_ref=x_refs[i], dst_ref=x_refs[i], device_id=sibling, **kw(k))
                          for k in range(7)])
            pos += len(rows)
        return mine, first, passed, whole

    def start(self, x_refs, out_refs, sems):
        mine, first, _, _ = self._plan(x_refs, out_refs, sems)
        for cp in mine + first:
            cp.start()

    def finish(self, x_refs, out_refs, sems):
        _, _, passed, whole = self._plan(x_refs, out_refs, sems)
        local_sems = sems[2]
        for i, fwd in enumerate(passed):
            for jj in range(3):
                whole[i][1 + jj].wait_recv()
                for cp in fwd[jj]:
                    cp.start()
        for i in range(len(passed)):
            whole[i][0].wait_recv()
            for jj in range(3):
                whole[i][4 + jj].wait_recv()
        for i in range(len(passed)):
            for k in range(7):
                whole[i][k].wait_send()
            pltpu.make_async_copy(x_refs[i], x_refs[i], local_sems.at[i]).wait()


def run_comm(comm, name):
    n_in, n_out = len(comm.inputs), len(comm.out_shapes)

    def body(*refs):
        ins, outs, sems = refs[:n_in], refs[n_in:n_in + n_out], refs[n_in + n_out:]
        comm.start(ins, outs, sems)
        comm.finish(ins, outs, sems)

    return pl.pallas_call(body, name=name, out_shape=comm.out_shapes, in_specs=[HBM_SPEC] * n_in,
                          out_specs=[HBM_SPEC] * n_out, scratch_shapes=comm.sems)(*comm.inputs)


def _attach(comm, body, n_in, n_out, first, last):
    if comm is None:
        return body
    ci, co, cs = len(comm.inputs), len(comm.out_shapes), len(comm.sems)

    def wrapped(*refs):
        h_in, c_in = refs[:n_in], refs[n_in:n_in + ci]
        h_out, c_out = refs[n_in + ci:n_in + ci + n_out], refs[n_in + ci + n_out:n_in + ci + n_out + co]
        rest = refs[n_in + ci + n_out + co:]
        h_scr, c_sem = rest[:len(rest) - cs], rest[len(rest) - cs:]

        @pl.when(first())
        def _():
            comm.start(c_in, c_out, c_sem)

        body(*h_in, *h_out, *h_scr)

        @pl.when(last())
        def _():
            comm.finish(c_in, c_out, c_sem)

    return wrapped


def _grid_ends(grid):
    first = lambda: functools.reduce(lambda a, b: a & b, [pl.program_id(i) == 0 for i in range(len(grid))])
    last = lambda: functools.reduce(lambda a, b: a & b, [pl.program_id(i) == g - 1 for i, g in enumerate(grid)])
    return first, last


def _call_with_comm(body, grid, name, in_specs, args, out_specs, out_shape, comm, scratch=(), sem=None):
    sem = sem or ("parallel",) * len(grid)
    scratch = list(scratch)
    if comm is not None:
        body = _attach(comm, body, len(args), len(out_shape), *_grid_ends(grid))
        in_specs, args = in_specs + [HBM_SPEC] * len(comm.inputs), args + comm.inputs
        out_specs, out_shape = out_specs + [HBM_SPEC] * len(comm.out_shapes), out_shape + comm.out_shapes
        scratch, sem = scratch + comm.sems, ("arbitrary",) * len(grid)
    return pl.pallas_call(body, grid=grid, name=name, in_specs=in_specs, out_specs=out_specs, out_shape=out_shape,
                          scratch_shapes=scratch, compiler_params=_cp(*sem))(*args)


class ScatterComm:
    def __init__(self, groups):
        self.sizes = [len(g) for g in groups]
        self.rows = [[pc.shape[1] for pc in g] for g in groups]
        ng = len(groups)
        self.inputs = [pc for g in groups for pc in g]
        self.out_shapes = [jax.ShapeDtypeStruct((N_DEV, sum(self.rows[gi]), g[0].shape[2]), g[0].dtype) for gi, g in enumerate(groups)]
        self.sems = [pltpu.SemaphoreType.DMA((7 * ng,)), pltpu.SemaphoreType.DMA((7 * ng,)), pltpu.SemaphoreType.DMA((ng,))]

    def _peers(self):
        x, y, c = _me()
        out = []
        for k in range(1, N_DEV):
            px = 1 - x if k & 4 else x
            py = 1 - y if k & 2 else y
            pc = 1 - c if k & 1 else c
            out.append((k, 4 * px + 2 * py + pc, dict(device_id=(px, py, pc), device_id_type=MESH)))
        return 4 * x + 2 * y + c, out

    def start(self, ins, outs, sems):
        send_sems, recv_sems, local_sems = sems
        me, peers = self._peers()
        pos = 0
        for gi, size in enumerate(self.sizes):
            for i, pc in enumerate(ins[pos:pos + size]):
                dst = outs[gi].at[me, pl.ds(sum(self.rows[gi][:i]), self.rows[gi][i])]
                pltpu.make_async_copy(pc.at[me], dst, local_sems.at[gi]).start()
                for k, peer, kw in peers:
                    pltpu.make_async_remote_copy(src_ref=pc.at[peer], dst_ref=dst, send_sem=send_sems.at[7 * gi + k - 1],
                                                 recv_sem=recv_sems.at[7 * gi + k - 1], **kw).start()
            pos += size

    def finish(self, ins, outs, sems):
        send_sems, recv_sems, local_sems = sems
        me, peers = self._peers()
        whole = [pltpu.make_async_remote_copy(src_ref=outs[gi].at[peer], dst_ref=outs[gi].at[peer],
                                              send_sem=send_sems.at[7 * gi + k - 1], recv_sem=recv_sems.at[7 * gi + k - 1], **kw)
                 for gi in range(len(self.sizes)) for k, peer, kw in peers]
        for cp in whole:
            cp.wait_recv()
        for cp in whole:
            cp.wait_send()
        for gi in range(len(self.sizes)):
            pltpu.make_async_copy(outs[gi].at[me], outs[gi].at[me], local_sems.at[gi]).wait()


def _peer_list():
    x, y, c = _me()
    out = []
    for k in range(1, N_DEV):
        px = 1 - x if k & 4 else x
        py = 1 - y if k & 2 else y
        pc = 1 - c if k & 1 else c
        out.append((k, 4 * px + 2 * py + pc, dict(device_id=(px, py, pc), device_id_type=MESH)))
    return 4 * x + 2 * y + c, out


SEM_SPEC = pl.BlockSpec(memory_space=pltpu.SEMAPHORE)
HBM_ONLY = pl.BlockSpec(memory_space=pltpu.HBM)
N_SPLIT_SEMS = 2 * (N_DEV - 1)


def exchange_start(piece, after, name):
    def body(piece_ref, land_ref, after_ref, *outs):
        sems, token = outs[:N_SPLIT_SEMS], outs[-1]
        me, peers = _peer_list()
        for k, peer, kw in peers:
            pltpu.make_async_remote_copy(src_ref=piece_ref.at[peer], dst_ref=land_ref.at[me], send_sem=sems[k - 1],
                                         recv_sem=sems[N_DEV - 2 + k], **kw).start()
        token[...] = jnp.zeros_like(token)

    res = pl.pallas_call(
        body, name=name + "_start",
        out_shape=(pltpu.SemaphoreType.DMA(()),) * N_SPLIT_SEMS + (pltpu.HBM(piece.shape, piece.dtype), pltpu.HBM(piece.shape, piece.dtype),
                                                                   jax.ShapeDtypeStruct((8, 128), F32)),
        in_specs=(HBM_ONLY, HBM_ONLY, HBM_SPEC),
        out_specs=(SEM_SPEC,) * N_SPLIT_SEMS + (HBM_ONLY, HBM_ONLY, pl.BlockSpec(memory_space=pltpu.VMEM)),
        input_output_aliases={0: N_SPLIT_SEMS, 1: N_SPLIT_SEMS + 1},
        compiler_params=pltpu.CompilerParams(has_side_effects=pltpu.SideEffectType.DATAFLOW_SIDE_EFFECTING),
    )(pltpu.with_memory_space_constraint(piece, pltpu.HBM),
      pltpu.with_memory_space_constraint(lax.empty(piece.shape, piece.dtype), pltpu.HBM), after)
    return res[:N_SPLIT_SEMS], res[N_SPLIT_SEMS], res[N_SPLIT_SEMS + 1], res[N_SPLIT_SEMS + 2]


def exchange_wait(sems, piece, land, after, name):
    after = list(after)
    def body(piece_ref, land_ref, *rest):
        sem_refs = rest[:N_SPLIT_SEMS]
        me, peers = _peer_list()
        for k, peer, kw in peers:
            cp = pltpu.make_async_remote_copy(src_ref=piece_ref.at[peer], dst_ref=land_ref.at[peer], send_sem=sem_refs[k - 1],
                                              recv_sem=sem_refs[N_DEV - 2 + k], **kw)
            cp.wait_send()
            cp.wait_recv()

    return pl.pallas_call(
        body, name=name + "_wait",
        out_shape=(pltpu.HBM(piece.shape, piece.dtype), pltpu.HBM(land.shape, land.dtype)),
        in_specs=(HBM_ONLY, HBM_ONLY) + (SEM_SPEC,) * N_SPLIT_SEMS + (HBM_SPEC,) * len(after), out_specs=(HBM_ONLY, HBM_ONLY),
        input_output_aliases={0: 0, 1: 1},
        compiler_params=pltpu.CompilerParams(has_side_effects=pltpu.SideEffectType.DATAFLOW_SIDE_EFFECTING),
    )(piece, land, *sems, *after)[1]


def sum_slots(recv, name, tr):
    n, R, C = recv.shape

    def body(r_ref, o_ref):
        acc = r_ref[0].astype(F32)
        for s in range(1, n):
            acc = acc + r_ref[s].astype(F32)
        o_ref[...] = acc

    return pl.pallas_call(
        body, grid=(R // tr,), name=name,
        in_specs=[pl.BlockSpec((n, tr, C), lambda i: (0, i, 0))], out_specs=pl.BlockSpec((tr, C), lambda i: (i, 0)),
        out_shape=jax.ShapeDtypeStruct((R, C), F32), compiler_params=_cp("parallel"),
    )(recv)


PACK_W, FLAT_W = 1024, 128
MAIN = [
    ("ffn1_w_gate", "col"), ("ffn1_w_up", "col"), ("ffn1_w_down", "row"),
    ("ffn2_w_gate", "col"), ("ffn2_w_up", "col"), ("ffn2_w_down", "row"),
    ("w_ssd_proj", "row"), ("w_mla_proj", "row"), ("w_out", "row"),
    ("w_xq", "row"), ("w_xk", "row"), ("w_xv", "row"), ("w_xo", "row"),
    ("w_uk", "col"), ("w_uv", "col"), ("w_in", "col"),
]
FLAT = [("w_uq", "col")]
BIG = MAIN + FLAT
SMALL = ["ffn1_pre_g", "ffn1_post_g", "mix_pre_g", "conv_b", "dt_bias", "a_log", "d_skip", "ssd_norm_g", "q_norm_g",
         "kv_norm_g", "gate_bias", "mix_post_g", "xa_pre_g", "mem_norm_g", "xa_post_g", "ffn2_pre_g", "ffn2_post_g"]
WEIGHTS = ['ffn1_pre_g', 'ffn1_w_gate', 'ffn1_w_up', 'ffn1_w_down', 'ffn1_post_g', 'mix_pre_g', 'w_in', 'conv_w', 'conv_b',
           'dt_bias', 'a_log', 'd_skip', 'ssd_norm_g', 'w_ssd_proj', 'q_norm_g', 'w_uq', 'kv_norm_g', 'w_uk', 'w_uv',
           'w_mla_proj', 'gate_bias', 'w_out', 'mix_post_g', 'xa_pre_g', 'mem_norm_g', 'w_xq', 'w_xk', 'w_xv', 'w_xo',
           'xa_post_g', 'ffn2_pre_g', 'ffn2_w_gate', 'ffn2_w_up', 'ffn2_w_down', 'ffn2_post_g']


def _pack_rows(w, kind, width):
    m = w[0].T if kind == "col" else w[0]
    return m.reshape(-1, width)


KIND = dict(BIG)
GATHER_PLAN = {
    "ffn1_pre": (["ffn1_w_gate", "ffn1_w_up"], []),
    "ffn1_gate_up": (["ffn1_w_down", "w_in@0"], []),
    "ffn1_down": (["w_in@1"], ["conv_w"]),
    "ssd_fwd": (["w_ssd_proj", "w_mla_proj", "w_out", "w_uk", "w_uv"], ["w_uq"]),
    "attn_fwd": (["w_xq", "w_xk", "w_xv", "w_xo", "ffn2_w_gate", "ffn2_w_up", "ffn2_w_down"], []),
}
EARLY_EXCHANGE, LAST_EXCHANGE = "early", "last"
SCATTER_PLAN = {
    "attn_bwd": [["ffn2_w_gate", "ffn2_w_up", "ffn2_w_down"], ["w_xq", "w_xk", "w_xv", "w_xo"]],
    "ssd_bwd": [["w_ssd_proj", "w_mla_proj", "w_out"], ["w_uk", "w_uv"], ["w_uq"]],
    "in_bwd": [["w_in#0"]],
    "ffn1:down_bwd": [["w_in#1"]],
    "ffn1:dwd": [["w_in#2"]],
    "ffn1:dwg": [["ffn1_w_down#0"]],
    "ffn1:dwu": [["ffn1_w_down#1"]],
    "early": [["ffn1_w_gate"]],
    "last": [["ffn1_w_up"]],
}
PARTS = {"w_in@0": ("w_in", 0, 336), "w_in@1": ("w_in", 336, 662),
         "w_in#0": ("w_in", 0, 336), "w_in#1": ("w_in", 336, 496), "w_in#2": ("w_in", 496, 662),
         "ffn1_w_down#0": ("ffn1_w_down", 0, 176), "ffn1_w_down#1": ("ffn1_w_down", 176, 352)}


def _parts_of(base, mark):
    return sorted(pn for pn, (b, _, _) in PARTS.items() if b == base and mark in pn)


class Stage:
    def __init__(self, w):
        self.w = w
        self.width = {n: PACK_W if (n, k) in MAIN else FLAT_W for n, k in BIG}
        self.nrows = {n: math.prod(w[n].shape) // self.width[n] for n, _ in BIG}
        self.recv = {}
        self.split = {}
        self.arrived_parts = {}

    def _shards(self, tag):
        names_main, names_flat = GATHER_PLAN[tag]

        def pack(n):
            if n == "conv_w":
                return _pad_rows(lax.bitcast_convert_type(self.w[n][0], BF16).reshape(-1, FLAT_W), 16)
            base, r0, r1 = PARTS.get(n, (n, 0, None))
            return _pack_rows(self.w[base], KIND[base], self.width[base])[r0:r1].astype(BF16)
        shards = []
        if names_main:
            pieces = [pack(n) for n in names_main]
            shards.append((jnp.concatenate(pieces, axis=0), [pc.shape[0] for pc in pieces]))
        if names_flat:
            pieces = [pack(n) for n in names_flat]
            shards.append((jnp.concatenate(pieces, axis=0), [pc.shape[0] for pc in pieces]))
        return shards

    def gather(self, tag):
        return GatherComm(self._shards(tag)) if tag in GATHER_PLAN else None

    def gathered(self, tag, outs, W, p):
        if tag not in GATHER_PLAN:
            return
        names_main, names_flat = GATHER_PLAN[tag]
        outs = list(outs)
        for n in names_main + names_flat:
            rows = outs.pop(0)
            if n == "conv_w":
                cw = self.w[n]
                bits = rows[:, :2 * math.prod(cw.shape) // FLAT_W].reshape((N_DEV,) + cw.shape[1:] + (2,))
                p[n] = lax.bitcast_convert_type(bits, F32).transpose(1, 0, 2).reshape(cw.shape[1], -1)
                continue
            if n in PARTS:
                self.arrived_parts[n] = rows
                base = PARTS[n][0]
                mine = _parts_of(base, "@")
                if not all(pn in self.arrived_parts for pn in mine):
                    continue
                n, rows = base, jnp.concatenate([self.arrived_parts[pn] for pn in mine], axis=1)
            K = self.w[n].shape[1] if KIND[n] == "col" else PACK_W
            W[n] = rows.reshape(-1, K)

    def pieces(self, tag, gw):
        def piece(n):
            if n in PARTS:
                base, r0, r1 = PARTS[n]
                return gw[base].reshape(N_DEV, self.nrows[base], self.width[base])[:, r0:r1]
            return gw[n].reshape(N_DEV, self.nrows[n], self.width[n])
        return [[piece(n) for n in names] for names in SCATTER_PLAN[tag]]

    def scatter(self, tag, gw):
        return ScatterComm(self.pieces(tag, gw)) if tag in SCATTER_PLAN else None

    def scattered(self, tag, outs):
        if tag in SCATTER_PLAN:
            self.recv[tag] = outs

    def split_start(self, tag, gw, after=None, behind=None):
        after = after if behind is None else self.recv[behind][0]
        piece, = self.pieces(tag, gw)[0]
        sems, piece, land, token = exchange_start(piece, after, "exchange_" + tag)
        self.split[tag] = (sems, piece, land)
        return token

    def split_wait(self, tag, after):
        sems, piece, land = self.split.pop(tag)
        land = exchange_wait(sems, piece, land, after, "exchange_" + tag)
        me = _dev_index()
        return lax.dynamic_update_index_in_dim(land, lax.dynamic_index_in_dim(piece, me, 0, keepdims=False), me, 0)


def _pad_rows(a, mult):
    r = (-a.shape[0]) % mult
    return a if r == 0 else jnp.concatenate([a, jnp.zeros((r,) + a.shape[1:], a.dtype)], axis=0)


def _pack_small(vals, loss_row=None, conv_w=None):
    rows = []
    for v in vals:
        f = v.reshape(-1)
        f = jnp.concatenate([f, jnp.zeros(((-f.shape[0]) % 128,), F32)])
        rows.append(f.reshape(-1, 128))
    if conv_w is not None:
        rows.append(conv_w.reshape(-1, 128))
    if loss_row is not None:
        rows.append(loss_row)
    return _pad_rows(jnp.concatenate(rows, axis=0), 8)


def _tn(a, b, name, out_dtype=BF16, comm=None):
    M, N = a.shape[1], b.shape[1]
    T = a.shape[0]
    tm = M if M <= 1536 else M // 2
    tk = 2048 if T % 2048 == 0 and T > 2048 else None
    res = mm([[(a, b, "tn")]], [out_dtype], name, tm=tm, tn=N, tk=tk, comm=comm)
    return res[0] if comm is None else (res[0], res[1:])


class NoStage:
    def gather(self, tag):
        return None

    def gathered(self, tag, outs, W, p):
        pass

    def scatter(self, tag, gw):
        return None

    def scattered(self, tag, outs):
        pass

    def split_start(self, tag, gw, after=None, behind=None):
        return None


def _ffn_fwd(x, gpre, gpost, W, p, tag, stage, target=None):
    comm = stage.gather(tag + "_pre")
    h = rms_fwd(x, gpre, tag + "_pre", comm=comm)
    if comm is not None:
        h, arrived = h
        stage.gathered(tag + "_pre", arrived, W, p)

    def swi(accs, ex):
        sil, dsil = _silu_parts(accs[0])
        return sil, accs[1] * dsil, sil * accs[1]
    G, U, A, *arrived = mm([[(h, W[tag + "_w_gate"], "nt")], [(h, W[tag + "_w_up"], "nt")]], [BF16, BF16, BF16], tag + "_gate_up",
                           tn=DFF // 2, epi=swi, comm=stage.gather(tag + "_gate_up"), sub=4 if h.shape[0] % 1024 == 0 else 1)
    stage.gathered(tag + "_gate_up", arrived, W, p)
    H, y, *rest = mm_resid(A, W[tag + "_w_down"], x, gpost, FFN_RES, tag + "_down", comm=stage.gather(tag + "_down"), target=target)
    saved = (x, h, G, U, A, H)
    if target is not None:
        return y, saved, rest[0]
    stage.gathered(tag + "_down", rest, W, p)
    return y, saved


def _ffn_bwd(dy, saved, gpre, gpost, wg_t, wu_t, wd, tag, stage, gw):
    x, h, G, U, A, H = saved
    dH, dgpost = resid_bwd(H, gpost, dy, FFN_RES, tag + "_post_bwd")

    def dswi(accs, ex):
        return accs[0] * ex[1], accs[0] * ex[0]

    def hosted(where, call):
        comm = stage.scatter(tag + ":" + where, gw)
        res = call(comm)
        if comm is None:
            return res
        stage.scattered(tag + ":" + where, res[1])
        return res[0]

    res = hosted("down_bwd", lambda comm: (lambda r: r if comm is None else (r[:2], r[2:]))(
        mm([[(dH, wd, "nt")]], [BF16, BF16], tag + "_down_bwd", tn=DFF // 2, epi=dswi, extras=[G, U], comm=comm,
           sub=4 if dH.shape[0] % 1024 == 0 else 1)))
    dG, dU = res
    gw[tag + "_w_down"] = hosted("dwd", lambda comm: _tn(A, dH, tag + "_dwd", comm=comm))
    gw[tag + "_w_gate"] = hosted("dwg", lambda comm: _tn(dG, h, tag + "_dwg", comm=comm))
    gw[tag + "_w_up"] = hosted("dwu", lambda comm: _tn(dU, h, tag + "_dwu", comm=comm))
    token = stage.split_start(EARLY_EXCHANGE, gw, behind=tag + ":dwu") if tag == "ffn1" else None
    dx, dgpre = mm_rms_bwd([(dG, wg_t, "nn"), (dU, wu_t, "nn")], x, gpre, tag + "_gate_up_bwd", resid=dy, token=token)
    return dx, dgpre, dgpost


def _local_step(x, mem, positions, tgt, W, p, stage=None):
    stage = stage or NoStage()
    nseq = x.shape[0]
    T = nseq * x.shape[1]
    x0 = x.reshape(T, D)
    mem2 = mem.reshape(-1, D)

    x1, ffn1 = _ffn_fwd(x0, p["ffn1_pre_g"], p["ffn1_post_g"], W, p, "ffn1", stage)

    w_in_t = W["w_in"]
    bounds = [0]
    for n in (SSD_INNER, CONV_CH, SSD_H, QR, KVR, ROPE, 2 * D):
        bounds.append(bounds[-1] + n)
    wt_z, wt_xbc, wt_dt, wt_q, wt_kv, wt_kr, wt_gate = [w_in_t[bounds[i]:bounds[i + 1]] for i in range(7)]
    wt_dt, wt_kr = _pad_rows(wt_dt, SLOT), _pad_rows(wt_kr, SLOT)
    wt_dtkr = jnp.concatenate([wt_dt, wt_kr], axis=0)
    hm = rms_fwd(x1, p["mix_pre_g"], "mix_pre")
    z = mm1(hm, wt_z, "nt", BF16, "in_z")
    xbc = mm1(hm, wt_xbc, "nt", BF16, "in_xbc")
    q_c = mm1(hm, wt_q, "nt", F32, "in_q", tn=QR)
    kv_c = mm1(hm, wt_kv, "nt", F32, "in_kv")
    dtkr = mm1(hm, wt_dtkr, "nt", F32, "in_dtkr")
    gl = mm1(hm, wt_gate, "nt", BF16, "in_gate")

    xbc_act = conv_fwd(xbc, p["conv_w"], p["conv_b"], nseq)
    y_ssd_core, prev, *arrived = ssd_fwd(xbc_act, dtkr, p["dt_bias"], p["a_log"], p["d_skip"], nseq, comm=stage.gather("ssd_fwd"))
    stage.gathered("ssd_fwd", arrived, W, p)
    yn = gated_norm_fwd(y_ssd_core, z, p["ssd_norm_g"], "ssd_norm")
    y_ssd = mm1(yn, W["w_ssd_proj"], "nn", BF16, "ssd_proj")

    slot_rows = lambda wt, per: jnp.pad(wt.reshape(MLA_H, per, -1), ((0, 0), (0, SLOT - per), (0, 0))).reshape(MLA_H * SLOT, -1)
    wq_s, wk_s, wv_s = slot_rows(W["w_uq"], QK), slot_rows(W["w_uk"], NOPE), slot_rows(W["w_uv"], VD)
    wo_s = slot_rows(W["w_mla_proj"], VD)
    qn = rms_fwd(q_c, p["q_norm_g"], "q_norm")
    rope_c, rope_s = rope_table(*_rope_inputs(positions))
    rope_args = [("rows", rope_c), ("rows", rope_s)]
    Qc, = mm([[(qn, wq_s, "nt")]], [BF16], "uq", epi=rope_q_epilogue, extras=rope_args, sub=4 if T % 1024 == 0 else 1)
    kvn = rms_fwd(kv_c, p["kv_norm_g"], "kv_norm")
    Kc, = mm([[(kvn, wk_s, "nt")]], [BF16], "uk", epi=rope_k_epilogue, extras=rope_args + [("rows", dtkr)],
             sub=4 if T % 1024 == 0 else 1)
    v_s = mm1(kvn, wv_s, "nt", BF16, "uv")
    o_s, lse, *arrived = attn_slot_fwd(Qc, Kc, v_s, nseq, comm=stage.gather("attn_fwd"))
    stage.gathered("attn_fwd", arrived, W, p)
    y_mla = mm1(o_s, wo_s, "nn", BF16, "mla_proj")

    merged = merge_fwd(gl, y_ssd, y_mla, p["gate_bias"], "merge")
    hmix, x2 = mm_resid(merged, W["w_out"], x1, p["mix_post_g"], 1.0, "mix_out")

    hq = rms_fwd(x2, p["xa_pre_g"], "xa_pre")
    mn = rms_fwd(mem2, p["mem_norm_g"], "mem_norm")
    xq = mm1(hq, W["w_xq"], "nn", BF16, "xq")
    xk = mm1(mn, W["w_xk"], "nn", BF16, "xk")
    xv = mm1(mn, W["w_xv"], "nn", BF16, "xv")
    xo, *arrived = xattn_fwd(xq, xk, xv, nseq, comm=stage.gather("xattn_fwd"))
    stage.gathered("xattn_fwd", arrived, W, p)
    ho, x3 = mm_resid(xo, W["w_xo"], x2, p["xa_post_g"], 1.0, "xo")

    dx4, ffn2, sq_cols = _ffn_fwd(x3, p["ffn2_pre_g"], p["ffn2_post_g"], W, p, "ffn2", stage, target=tgt.reshape(T, D))
    loss_row = (0.5 / D) * jnp.sum(sq_cols.reshape(-1, 128), axis=0, keepdims=True)

    gw, gs = {}, {}
    dx3, gs["ffn2_pre_g"], gs["ffn2_post_g"] = _ffn_bwd(
        dx4, ffn2, p["ffn2_pre_g"], p["ffn2_post_g"], W["ffn2_w_gate"], W["ffn2_w_up"], W["ffn2_w_down"], "ffn2", stage, gw)

    dho, gs["xa_post_g"] = resid_bwd(ho, p["xa_post_g"], dx3, 1.0, "xa_post_bwd")
    dxo = mm1(dho, W["w_xo"], "nt", BF16, "xo_bwd")
    gw["w_xo"] = _tn(xo, dho, "d_w_xo")
    dxq, dxk, dxv = xattn_bwd(xq, xk, xv, dxo, nseq)
    dx2, gs["xa_pre_g"] = mm_rms_bwd([(dxq, W["w_xq"], "nt")], x2, p["xa_pre_g"], "xq_bwd", resid=dx3)
    gw["w_xq"] = _tn(hq, dxq, "d_w_xq")
    dmn = mm([[(dxk, W["w_xk"], "nt"), (dxv, W["w_xv"], "nt")]], [F32], "xkv_bwd")[0]
    gw["w_xk"] = _tn(mn, dxk, "d_w_xk")
    gw["w_xv"] = _tn(mn, dxv, "d_w_xv")
    _, gs["mem_norm_g"] = rms_bwd(mem2, p["mem_norm_g"], dmn, "mem_norm_bwd", dx_dtype=BF16)

    dhmix, gs["mix_post_g"] = resid_bwd(hmix, p["mix_post_g"], dx2, 1.0, "mix_post_bwd")
    dmerged = mm1(dhmix, W["w_out"], "nt", BF16, "mix_out_bwd")
    gw["w_out"] = _tn(merged, dhmix, "d_w_out")
    dys, dym, dgl, gs["gate_bias"] = merge_bwd(gl, y_ssd, y_mla, dmerged, p["gate_bias"], "merge_bwd")

    unslot = lambda g, per: g.reshape(MLA_H, SLOT, -1)[:, :per].reshape(MLA_H * per, -1)
    do_s = mm1(dym, wo_s, "nt", BF16, "mla_proj_bwd")
    gw["w_mla_proj"] = unslot(_tn(o_s, dym, "d_w_mla_proj"), VD)
    dQc, dKc, dv_s, *sent = attn_slot_bwd(Qc, Kc, v_s, o_s, lse, do_s, nseq, comm=stage.scatter("attn_bwd", gw))
    stage.scattered("attn_bwd", sent)
    dq_s, dkn_s, dkr = rope_slot_bwd(dQc, dKc, rope_c, rope_s, "rope_bwd")
    dq_c, gs["q_norm_g"] = mm_rms_bwd([(dq_s, wq_s, "nn")], q_c, p["q_norm_g"], "uq_bwd", dx_dtype=BF16)
    gw["w_uq"] = unslot(_tn(dq_s, qn, "d_w_uq"), QK)
    dkv_c, gs["kv_norm_g"] = mm_rms_bwd([(dkn_s, wk_s, "nn"), (dv_s, wv_s, "nn")], kv_c, p["kv_norm_g"], "ukv_bwd", dx_dtype=BF16)
    gw["w_uk"] = unslot(_tn(dkn_s, kvn, "d_w_uk"), NOPE)
    gw["w_uv"] = unslot(_tn(dv_s, kvn, "d_w_uv"), VD)

    dyn = mm1(dys, W["w_ssd_proj"], "nt", BF16, "ssd_proj_bwd")
    gw["w_ssd_proj"] = _tn(yn, dys, "d_w_ssd_proj")
    dyc, dz, gs["ssd_norm_g"] = gated_norm_bwd(y_ssd_core, z, dyn, p["ssd_norm_g"], "ssd_norm_bwd")
    dxbc_act, ddtr, gs["dt_bias"], gs["a_log"], gs["d_skip"], *sent = ssd_bwd(
        xbc_act, dtkr, p["dt_bias"], p["a_log"], p["d_skip"], prev, dyc, nseq, comm=stage.scatter("ssd_bwd", gw))
    stage.scattered("ssd_bwd", sent)
    dxbc, gs["conv_w"], gs["conv_b"] = conv_bwd(xbc, p["conv_w"], p["conv_b"], dxbc_act, nseq)

    gw["w_in"] = jnp.concatenate([_tn(dz, hm, "d_w_in_z"), _tn(dxbc, hm, "d_w_in_xbc"), _tn(ddtr, hm, "d_w_in_dt")[:SSD_H],
                                  _tn(dq_c, hm, "d_w_in_q"), _tn(dkv_c, hm, "d_w_in_kv"), _tn(dkr, hm, "d_w_in_kr")[:ROPE],
                                  _tn(dgl, hm, "d_w_in_gate")], axis=0)
    dx1, gs["mix_pre_g"], *sent = mm_rms_bwd(
        [(dz, wt_z, "nn"), (dxbc, wt_xbc, "nn"), (ddtr, wt_dt, "nn"), (dq_c, wt_q, "nn"), (dkv_c, wt_kv, "nn"),
         (dkr, wt_kr, "nn"), (dgl, wt_gate, "nn")], x1, p["mix_pre_g"], "in_bwd", resid=dx2, comm=stage.scatter("in_bwd", gw))
    stage.scattered("in_bwd", sent)

    dx0, gs["ffn1_pre_g"], gs["ffn1_post_g"] = _ffn_bwd(
        dx1, ffn1, p["ffn1_pre_g"], p["ffn1_post_g"], W["ffn1_w_gate"], W["ffn1_w_up"], W["ffn1_w_down"], "ffn1", stage, gw)
    return loss_row, dx0.reshape(x.shape), gw, gs


def kernel(x, mem, positions, ffn1_pre_g, ffn1_w_gate, ffn1_w_up, ffn1_w_down, ffn1_post_g, mix_pre_g, w_in, conv_w, conv_b, dt_bias, a_log, d_skip, ssd_norm_g, w_ssd_proj, q_norm_g, w_uq, kv_norm_g, w_uk, w_uv, w_mla_proj, gate_bias, w_out, mix_post_g, xa_pre_g, mem_norm_g, w_xq, w_xk, w_xv, w_xo, xa_post_g, ffn2_pre_g, ffn2_w_gate, ffn2_w_up, ffn2_w_down, ffn2_post_g, loss_target, m_ffn1_pre_g, m_ffn1_w_gate, m_ffn1_w_up, m_ffn1_w_down, m_ffn1_post_g, m_mix_pre_g, m_w_in, m_conv_w, m_conv_b, m_dt_bias, m_a_log, m_d_skip, m_ssd_norm_g, m_w_ssd_proj, m_q_norm_g, m_w_uq, m_kv_norm_g, m_w_uk, m_w_uv, m_w_mla_proj, m_gate_bias, m_w_out, m_mix_post_g, m_xa_pre_g, m_mem_norm_g, m_w_xq, m_w_xk, m_w_xv, m_w_xo, m_xa_post_g, m_ffn2_pre_g, m_ffn2_w_gate, m_ffn2_w_up, m_ffn2_w_down, m_ffn2_post_g, v_ffn1_pre_g, v_ffn1_w_gate, v_ffn1_w_up, v_ffn1_w_down, v_ffn1_post_g, v_mix_pre_g, v_w_in, v_conv_w, v_conv_b, v_dt_bias, v_a_log, v_d_skip, v_ssd_norm_g, v_w_ssd_proj, v_q_norm_g, v_w_uq, v_kv_norm_g, v_w_uk, v_w_uv, v_w_mla_proj, v_gate_bias, v_w_out, v_mix_post_g, v_xa_pre_g, v_mem_norm_g, v_w_xq, v_w_xk, v_w_xv, v_w_xo, v_xa_post_g, v_ffn2_pre_g, v_ffn2_w_gate, v_ffn2_w_up, v_ffn2_w_down, v_ffn2_post_g):
    a = dict(locals())
    w = {n: a[n] for n in WEIGHTS}
    m = {n: a["m_" + n] for n in WEIGHTS}
    v = {n: a["v_" + n] for n in WEIGHTS}

    stage = Stage(w)
    W, p = {}, {n: w[n] for n in SMALL}
    loss_row, grad_x, gw, gs = _local_step(x, mem, positions, loss_target, W, p, stage)

    sm = _pack_small([gs[n] for n in SMALL], loss_row=loss_row, conv_w=gs["conv_w"])
    srecv, = run_comm(ScatterComm([[jnp.broadcast_to(sm[None], (N_DEV,) + sm.shape)]]), "exchange_small")
    s_rows = sum_slots(srecv, "sum_small", tr=sm.shape[0])
    token = stage.split_start(LAST_EXCHANGE, gw, after=s_rows)
    grads, delta, new_m, new_v = {}, {}, {}, {}
    raw_results = []

    def finish(n, buf, piece, token=None):
        col = KIND[n] == "col"
        turn = (lambda t: t.T) if col else (lambda t: t)
        K = w[n].shape[1]
        if col and buf.shape[2] != K:
            buf = buf.reshape(buf.shape[0], -1, K)
        res = adamw_from_slots(buf, piece, turn(w[n][0]), turn(m[n][0]), turn(v[n][0]), "adamw_" + n, token=token)
        raw_results.append(res[3])
        grads[n], delta[n], new_m[n], new_v[n] = [turn(r)[None] for r in res]

    finish(SCATTER_PLAN[EARLY_EXCHANGE][0][0], stage.split_wait(EARLY_EXCHANGE, after=[s_rows]), 0, token)
    parts = {}
    for tag, groups in SCATTER_PLAN.items():
        if tag in (EARLY_EXCHANGE, LAST_EXCHANGE):
            continue
        for names, buf in zip(groups, stage.recv[tag]):
            for piece, n in enumerate(names):
                if n in PARTS:
                    parts[n] = sum_slots(buf, "sum_" + n.replace("#", "_"), tr=buf.shape[1])
                else:
                    finish(n, buf, piece, token)
    for base in sorted({PARTS[pn][0] for pn in parts}):
        rows = jnp.concatenate([parts[pn] for pn in _parts_of(base, "#")], axis=0)
        finish(base, rows[None], 0, token)
    conv_w_full = p["conv_w"]
    small = adamw_small(s_rows, [w[n] for n in SMALL], [m[n] for n in SMALL], [v[n] for n in SMALL])
    for t, vals in zip((grads, delta, new_m, new_v), small):
        t.update(zip(SMALL, vals))
    r1 = sum(-(-w[n].shape[1] // 128) for n in SMALL)
    ncw = math.prod(conv_w_full.shape) // 128
    cw_grad_full = s_rows[r1:r1 + ncw].reshape(conv_w_full.shape)
    wsh = conv_w.shape[2]
    grads["conv_w"] = lax.dynamic_slice_in_dim(cw_grad_full, _dev_index() * wsh, wsh, axis=1)[None]
    loss = jnp.sum(s_rows[r1 + ncw])
    d_, m_, v_ = adamw(conv_w[0], grads["conv_w"][0], m["conv_w"][0], v["conv_w"][0], "adamw_conv_w")
    delta["conv_w"], new_m["conv_w"], new_v["conv_w"] = d_[None], m_[None], v_[None]
    finish(SCATTER_PLAN[LAST_EXCHANGE][0][0], stage.split_wait(LAST_EXCHANGE, after=raw_results + [small[3][0], v_]), 0)
    return (loss, grad_x, *[grads[n] for n in WEIGHTS], *[delta[n] for n in WEIGHTS],
            *[new_m[n] for n in WEIGHTS], *[new_v[n] for n in WEIGHTS])
```

```python
import functools
import math

import jax
import jax.numpy as jnp
from jax import lax
from jax.experimental import pallas as pl
from jax.experimental.pallas import tpu as pltpu

F32, BF16 = jnp.float32, jnp.bfloat16
MESH = pl.DeviceIdType.MESH
N_DEV = 8

D = 1024
DFF = 2816
SSD_H, SSD_P, SSD_G, SSD_N, SSD_L = 16, 64, 2, 128, 128
SSD_INNER = SSD_H * SSD_P
CONV_K, CONV_CH = 4, 1536
MLA_H, QR, KVR, NOPE, ROPE, VD = 16, 384, 256, 64, 32, 64
QK = NOPE + ROPE
ROPE_THETA = 10000.0
XA_H, XA_D = 4, 256
EPS = 1e-6
FFN_RES = 0.5
LR, B1, B2, AEPS, WD, STEP = 0.001, 0.9, 0.999, 1e-08, 0.01, 10

VMEM_LIMIT = 56 * 2**20


def _cp(*sem):
    return pltpu.CompilerParams(dimension_semantics=sem, vmem_limit_bytes=VMEM_LIMIT)


def _sigmoid(x):
    return 1.0 / (1.0 + jnp.exp(-x))


def _softplus(x):
    return jnp.where(x > 20.0, x, jnp.log(1.0 + jnp.exp(jnp.minimum(x, 20.0))))


def _dot(a, b, dims="nn"):
    ca = 0 if dims[0] == "t" else 1
    cb = 1 if dims[1] == "t" else 0
    return lax.dot_general(a.astype(BF16), b.astype(BF16), (((ca,), (cb,)), ((), ())), preferred_element_type=F32)


def _dot_sel(a, b, dims="nn", split="a", terms=3):
    r = (a if split == "a" else b).astype(F32)
    out = None
    for t in range(terms):
        piece = r.astype(BF16)
        if t + 1 < terms:
            r = r - piece.astype(F32)
        d = _dot(piece, b, dims) if split == "a" else _dot(a, piece, dims)
        out = d if out is None else out + d
    return out


def _ssd_common(dtr, dtb, alog):
    L = dtr.shape[0]
    dt = _softplus(dtr + dtb)
    a = -jnp.exp(alog)
    adt = dt * a
    r = lax.broadcasted_iota(jnp.int32, (L, L), 0)
    c = lax.broadcasted_iota(jnp.int32, (L, L), 1)
    lower = r >= c
    tri = lower.astype(F32)
    cs = _dot_sel(tri, adt, "nn", split="b")
    cs_t = _dot_sel(adt, tri, "tt")
    return dt, a, cs, cs_t, lower


def _head_expand():
    hh = lax.broadcasted_iota(jnp.int32, (SSD_H, SSD_INNER), 0)
    jj = lax.broadcasted_iota(jnp.int32, (SSD_H, SSD_INNER), 1)
    return ((jj >= hh * SSD_P) & (jj < hh * SSD_P + SSD_P)).astype(F32)


def _head_reduce():
    hh = lax.broadcasted_iota(jnp.int32, (SSD_INNER, SSD_H), 1)
    jj = lax.broadcasted_iota(jnp.int32, (SSD_INNER, SSD_H), 0)
    return ((jj >= hh * SSD_P) & (jj < hh * SSD_P + SSD_P)).astype(F32)


def ssd_fwd(xbc, dtr, dtb, alog, dsk, nseq, comm=None):
    T = xbc.shape[0]
    S = T // nseq
    C = S // SSD_L
    L = SSD_L
    NP = SSD_H // 2

    def body(x_ref, b_ref, c_ref, dtr_ref, dtb_ref, alog_ref, dsk_ref, y_ref, prev_ref, st_ref):
        ci = pl.program_id(1)

        @pl.when(ci == 0)
        def _():
            st_ref[...] = jnp.zeros_like(st_ref)

        dt, a, cs, cs_t, lower = _ssd_common(dtr_ref[:, 0:SSD_H], dtb_ref[...], alog_ref[...])
        E = _head_expand()
        X = x_ref[...].astype(F32)
        dt_e = _dot_sel(dt, E)
        cs_e = _dot_sel(cs, E)
        csl_e = cs_e[L - 1:L, :]
        Xd = X * dt_e
        Xf = Xd * jnp.exp(csl_e - cs_e)
        e_e = jnp.exp(cs_e)
        skip = _dot_sel(dsk_ref[...], E) * X
        lane = lax.broadcasted_iota(jnp.int32, (1, 2 * SSD_P), 1)
        rowp = lax.broadcasted_iota(jnp.int32, (2 * SSD_P, 1), 0)
        for g in range(SSD_G):
            Bg = b_ref[:, g * SSD_N:(g + 1) * SSD_N]
            Cg = c_ref[:, g * SSD_N:(g + 1) * SSD_N]
            cb = _dot(Cg, Bg, "nt")
            for pp in range(NP // SSD_G):
                p = g * (NP // SSD_G) + pp
                sl = slice(p * 2 * SSD_P, (p + 1) * 2 * SSD_P)
                Xd_p = Xd[:, sl]
                yd = jnp.zeros((L, 2 * SSD_P), F32)
                for q in range(2):
                    h = 2 * p + q
                    m = jnp.where(lower, jnp.exp(jnp.minimum(cs[:, h:h + 1] - cs_t[h:h + 1, :], 0.0)), 0.0)
                    mask = (lane >= q * SSD_P) & (lane < (q + 1) * SSD_P)
                    yd = yd + _dot(cb * m, jnp.where(mask, Xd_p, 0.0))
                S0 = st_ref[p]
                prev_ref[0, 0, p] = S0
                z = _dot(Cg, S0, "nt")
                y_ref[:, sl] = (skip[:, sl] + yd + z * e_e[:, sl]).astype(y_ref.dtype)
                h0 = 2 * p
                dec = jnp.where(rowp < SSD_P, jnp.exp(cs[L - 1:L, h0:h0 + 1]), jnp.exp(cs[L - 1:L, h0 + 1:h0 + 2]))
                st_ref[p] = S0 * dec + _dot(Xf[:, sl], Bg, "tn")

    row = lambda b, c: (b * C + c, 0)
    small = pl.BlockSpec((1, SSD_H), lambda b, c: (0, 0))
    return _call_with_comm(
        body, (nseq, C), "ssd_fwd",
        [pl.BlockSpec((L, SSD_INNER), row),
         pl.BlockSpec((L, SSD_G * SSD_N), lambda b, c: (b * C + c, SSD_INNER // (SSD_G * SSD_N))),
         pl.BlockSpec((L, SSD_G * SSD_N), lambda b, c: (b * C + c, SSD_INNER // (SSD_G * SSD_N) + 1)),
         pl.BlockSpec((L, 128), row), small, small, small],
        [xbc, xbc, xbc, dtr, dtb, alog, dsk],
        [pl.BlockSpec((L, SSD_INNER), row), pl.BlockSpec((1, 1, NP, 2 * SSD_P, SSD_N), lambda b, c: (b, c, 0, 0, 0))],
        [jax.ShapeDtypeStruct((T, SSD_INNER), BF16), jax.ShapeDtypeStruct((nseq, C, NP, 2 * SSD_P, SSD_N), F32)],
        comm, scratch=[pltpu.VMEM((NP, 2 * SSD_P, SSD_N), F32)], sem=("parallel", "arbitrary"))


def ssd_bwd(xbc, dtr, dtb, alog, dsk, prev, dy, nseq, comm=None):
    T = xbc.shape[0]
    S = T // nseq
    C = S // SSD_L
    L = SSD_L
    NP = SSD_H // 2

    def body(x_ref, b_ref, c_ref, dtr_ref, dtb_ref, alog_ref, dsk_ref, prev_ref, dy_ref,
             dxbc_ref, ddtr_ref, ddtb_ref, dalog_ref, ddsk_ref, ds_ref, stg_ref):
        bi = pl.program_id(0)
        ci = pl.program_id(1)

        @pl.when(ci == 0)
        def _():
            ds_ref[...] = jnp.zeros_like(ds_ref)

        @pl.when((ci == 0) & (bi == 0))
        def _():
            ddtb_ref[...] = jnp.zeros_like(ddtb_ref)
            dalog_ref[...] = jnp.zeros_like(dalog_ref)
            ddsk_ref[...] = jnp.zeros_like(ddsk_ref)

        dtr = dtr_ref[:, 0:SSD_H]
        dtb = dtb_ref[...]
        dt, a, cs, cs_t, lower = _ssd_common(dtr, dtb, alog_ref[...])
        upper = lax.broadcasted_iota(jnp.int32, (L, L), 1) >= lax.broadcasted_iota(jnp.int32, (L, L), 0)
        E = _head_expand()
        ET = _head_reduce()
        X = x_ref[...].astype(F32)
        dY = dy_ref[...].astype(F32)
        dt_e = _dot_sel(dt, E)
        cs_e = _dot_sel(cs, E)
        csl_e = cs_e[L - 1:L, :]
        f_e = jnp.exp(csl_e - cs_e)
        e_e = jnp.exp(cs_e)
        dsk_e = _dot_sel(dsk_ref[...], E)
        Xd = X * dt_e
        Xf = Xd * f_e
        lane = lax.broadcasted_iota(jnp.int32, (1, 2 * SSD_P), 1)
        rowp = lax.broadcasted_iota(jnp.int32, (2 * SSD_P, 1), 0)
        hsel = lax.broadcasted_iota(jnp.int32, (1, SSD_H), 1)
        dcs = jnp.zeros((L, SSD_H), F32)
        dcsl = jnp.zeros((1, SSD_H), F32)
        for g in range(SSD_G):
            Bg = b_ref[:, g * SSD_N:(g + 1) * SSD_N]
            Cg = c_ref[:, g * SSD_N:(g + 1) * SSD_N]
            cb = _dot(Cg, Bg, "nt")
            cbt = _dot(Bg, Cg, "nt")
            dB = jnp.zeros((L, SSD_N), F32)
            dC = jnp.zeros((L, SSD_N), F32)
            for pp in range(NP // SSD_G):
                p = g * (NP // SSD_G) + pp
                sl = slice(p * 2 * SSD_P, (p + 1) * 2 * SSD_P)
                Xd_p = Xd[:, sl]
                dY_p = dY[:, sl]
                dXd_p = jnp.zeros((L, 2 * SSD_P), F32)
                for q in range(2):
                    h = 2 * p + q
                    mask = (lane >= q * SSD_P) & (lane < (q + 1) * SSD_P)
                    col = cs[:, h:h + 1]
                    rw = cs_t[h:h + 1, :]
                    m = jnp.where(lower, jnp.exp(jnp.minimum(col - rw, 0.0)), 0.0)
                    mt = jnp.where(upper, jnp.exp(jnp.minimum(rw - col, 0.0)), 0.0)
                    dYm = jnp.where(mask, dY_p, 0.0)
                    dW = _dot(dYm, Xd_p, "nt")
                    dWt = _dot(Xd_p, dYm, "nt")
                    w = cb * m
                    wt = cbt * mt
                    dC = dC + _dot(dW * m, Bg)
                    dB = dB + _dot(dWt * mt, Cg)
                    dXd_p = dXd_p + jnp.where(mask, _dot(wt, dY_p), 0.0)
                    qcol = jnp.sum(dW * w, axis=1, keepdims=True) - jnp.sum(dWt * wt, axis=1, keepdims=True)
                    dcs = dcs + qcol * (hsel == h).astype(F32)
                S0 = prev_ref[0, 0, p]
                dSn = ds_ref[p]
                dZ = dY_p * e_e[:, sl]
                dC = dC + _dot(dZ, S0)
                h0 = 2 * p
                el0 = jnp.exp(cs[L - 1:L, h0:h0 + 1])
                el1 = jnp.exp(cs[L - 1:L, h0 + 1:h0 + 2])
                dec = jnp.where(rowp < SSD_P, el0, el1)
                ds_ref[p] = dSn * dec + _dot(dZ, Cg, "tn")
                dXf_p = _dot(Bg, dSn, "nt")
                dB = dB + _dot(Xf[:, sl], dSn)
                rs = jnp.sum(dSn * S0, axis=1, keepdims=True)
                s0 = jnp.sum(jnp.where(rowp < SSD_P, rs, 0.0), axis=0, keepdims=True) * el0
                s1 = jnp.sum(jnp.where(rowp >= SSD_P, rs, 0.0), axis=0, keepdims=True) * el1
                dcsl = dcsl + s0 * (hsel == h0).astype(F32) + s1 * (hsel == h0 + 1).astype(F32)
                y_off = _dot(Cg, S0, "nt") * e_e[:, sl]
                t1 = dY_p * y_off - dXf_p * Xf[:, sl]
                r1 = jnp.where(lane < SSD_P, t1, 0.0)
                c0 = jnp.sum(r1, axis=1, keepdims=True)
                c1 = jnp.sum(t1 - r1, axis=1, keepdims=True)
                dcs = dcs + c0 * (hsel == h0).astype(F32) + c1 * (hsel == h0 + 1).astype(F32)
                t2 = dXf_p * Xf[:, sl]
                r2 = jnp.where(lane < SSD_P, t2, 0.0)
                dcsl = dcsl + jnp.sum(r2, keepdims=True) * (hsel == h0).astype(F32) \
                    + jnp.sum(t2 - r2, keepdims=True) * (hsel == h0 + 1).astype(F32)
                stg_ref[:, sl] = dXd_p + dXf_p * f_e[:, sl]
            dxbc_ref[:, SSD_INNER + g * SSD_N:SSD_INNER + (g + 1) * SSD_N] = dB.astype(dxbc_ref.dtype)
            dxbc_ref[:, SSD_INNER + (SSD_G + g) * SSD_N:SSD_INNER + (SSD_G + g + 1) * SSD_N] = dC.astype(dxbc_ref.dtype)
        dXd = stg_ref[...]
        dxbc_ref[:, 0:SSD_INNER] = (dXd * dt_e + dsk_e * dY).astype(dxbc_ref.dtype)
        rowl = lax.broadcasted_iota(jnp.int32, (L, 1), 0)
        dcs = dcs + jnp.where(rowl == L - 1, dcsl, 0.0)
        dalpha = _dot_sel(upper.astype(F32), dcs, split="b")
        ddt = _dot_sel(dXd * X, ET, terms=2) + dalpha * a
        dalog_ref[...] += jnp.sum(dalpha * dt, axis=0, keepdims=True) * a
        ddtr = ddt * _sigmoid(dtr + dtb)
        spread = (lax.broadcasted_iota(jnp.int32, (SSD_H, 128), 0) == lax.broadcasted_iota(jnp.int32, (SSD_H, 128), 1)).astype(F32)
        ddtr_ref[...] = _dot(ddtr, spread).astype(ddtr_ref.dtype)
        ddtb_ref[...] += jnp.sum(ddtr, axis=0, keepdims=True)
        ddsk_ref[...] += jnp.sum(_dot_sel(dY * X, ET, terms=2), axis=0, keepdims=True)

    rowr = lambda b, c: (b * C + (C - 1 - c), 0)
    small = pl.BlockSpec((1, SSD_H), lambda b, c: (0, 0))
    return _call_with_comm(
        body, (nseq, C), "ssd_bwd",
        [pl.BlockSpec((L, SSD_INNER), rowr),
         pl.BlockSpec((L, SSD_G * SSD_N), lambda b, c: (b * C + (C - 1 - c), SSD_INNER // (SSD_G * SSD_N))),
         pl.BlockSpec((L, SSD_G * SSD_N), lambda b, c: (b * C + (C - 1 - c), SSD_INNER // (SSD_G * SSD_N) + 1)),
         pl.BlockSpec((L, 128), rowr), small, small, small,
         pl.BlockSpec((1, 1, NP, 2 * SSD_P, SSD_N), lambda b, c: (b, C - 1 - c, 0, 0, 0)),
         pl.BlockSpec((L, SSD_INNER), rowr)],
        [xbc, xbc, xbc, dtr, dtb, alog, dsk, prev, dy],
        [pl.BlockSpec((L, CONV_CH), rowr), pl.BlockSpec((L, 128), rowr), small, small, small],
        [jax.ShapeDtypeStruct((T, CONV_CH), BF16), jax.ShapeDtypeStruct((T, 128), BF16),
         jax.ShapeDtypeStruct((1, SSD_H), F32), jax.ShapeDtypeStruct((1, SSD_H), F32), jax.ShapeDtypeStruct((1, SSD_H), F32)],
        comm, scratch=[pltpu.VMEM((NP, 2 * SSD_P, SSD_N), F32), pltpu.VMEM((L, SSD_INNER), F32)], sem=("arbitrary", "arbitrary"))


SLOT = 128
ATT_T = 512
ATT_HP = 1
LOG2E = math.log2(math.e)
Q_SCALE = QK ** -0.5 * LOG2E


def _col_to_row(col):
    n = col.shape[0]
    eye = lax.broadcasted_iota(jnp.int32, (n, n), 0) == lax.broadcasted_iota(jnp.int32, (n, n), 1)
    return jnp.sum(jnp.where(eye, col, 0.0), axis=0, keepdims=True)


def attn_slot_fwd(q, k, v, nseq, comm=None):
    T = q.shape[0]
    S = T // nseq
    t = min(ATT_T, S)
    nb = S // t
    cols = [slice(h * SLOT, (h + 1) * SLOT) for h in range(ATT_HP)]

    def body(q_ref, k_ref, v_ref, o_ref, lse_ref):
        causal = lax.broadcasted_iota(jnp.int32, (t, t), 1) <= lax.broadcasted_iota(jnp.int32, (t, t), 0)
        for qi in range(nb):
            rows = slice(qi * t, (qi + 1) * t)
            state = [None] * ATT_HP
            for kj in range(qi + 1):
                keys = slice(kj * t, (kj + 1) * t)
                for h, c in enumerate(cols):
                    s = _dot(q_ref[rows, c], k_ref[keys, c], "nt")
                    if kj == qi:
                        s = jnp.where(causal, s, -1e30)
                    bm = jnp.max(s, axis=1, keepdims=True)
                    if kj == 0:
                        p = jnp.exp2(s - bm)
                        state[h] = (bm, jnp.sum(p, axis=1, keepdims=True), _dot(p, v_ref[keys, c]))
                    else:
                        m, l, acc = state[h]
                        m_new = jnp.maximum(m, bm)
                        corr = jnp.exp2(m - m_new)
                        p = jnp.exp2(s - m_new)
                        state[h] = (m_new, l * corr + jnp.sum(p, axis=1, keepdims=True), acc * corr + _dot(p, v_ref[keys, c]))
            for h, c in enumerate(cols):
                m, l, acc = state[h]
                o_ref[rows, c] = (acc / l).astype(o_ref.dtype)
                lse_ref[0, h, :, rows] = _col_to_row(m + jnp.log2(l))

    blk = pl.BlockSpec((S, ATT_HP * SLOT), lambda b, h: (b, h))
    return _call_with_comm(
        body, (nseq, MLA_H // ATT_HP), "attn_fwd", [blk, blk, blk], [q, k, v],
        [blk, pl.BlockSpec((1, ATT_HP, 1, S), lambda b, h: (b, h, 0, 0))],
        [jax.ShapeDtypeStruct((T, MLA_H * SLOT), BF16), jax.ShapeDtypeStruct((nseq, MLA_H, 1, S), F32)], comm)


def attn_slot_bwd(q, k, v, o, lse, do, nseq, comm=None):
    T = q.shape[0]
    S = T // nseq
    t = min(ATT_T, S)
    nb = S // t
    scale = QK ** -0.5
    cols = [slice(h * SLOT, (h + 1) * SLOT) for h in range(ATT_HP)]

    def body(q_ref, k_ref, v_ref, o_ref, lse_ref, do_ref, dq_ref, dk_ref, dv_ref, dqa_ref):
        causal_t = lax.broadcasted_iota(jnp.int32, (t, t), 0) <= lax.broadcasted_iota(jnp.int32, (t, t), 1)
        ones = jnp.ones((8, SLOT), F32)
        delta = {}
        for qi in range(nb):
            sl = slice(qi * t, (qi + 1) * t)
            for h, c in enumerate(cols):
                prod = do_ref[sl, c].astype(F32) * o_ref[sl, c].astype(F32)
                delta[h, qi] = _dot_sel(ones, prod, "nt", split="b", terms=2)[0:1, :]
        for kj in range(nb):
            ks = slice(kj * t, (kj + 1) * t)
            dk = [None] * ATT_HP
            dv = [None] * ATT_HP
            for qi in range(kj, nb):
                sl = slice(qi * t, (qi + 1) * t)
                for h, c in enumerate(cols):
                    kb, vb, qb, dob = k_ref[ks, c], v_ref[ks, c], q_ref[sl, c], do_ref[sl, c]
                    st = _dot(kb, qb, "nt")
                    pt = jnp.exp2(st - lse_ref[0, h, :, sl])
                    if qi == kj:
                        pt = jnp.where(causal_t, pt, 0.0)
                    dpt = _dot(vb, dob, "nt")
                    dst = (pt * (dpt - delta[h, qi])).astype(BF16)
                    dvc = _dot(pt, dob)
                    dkc = _dot(dst, qb) * (1.0 / LOG2E)
                    dv[h] = dvc if dv[h] is None else dv[h] + dvc
                    dk[h] = dkc if dk[h] is None else dk[h] + dkc
                    dqc = _dot(dst, kb, "tn") * scale
                    if kj > 0:
                        dqc = dqc + dqa_ref[sl, c]
                    if qi == kj:
                        dq_ref[sl, c] = dqc.astype(dq_ref.dtype)
                    else:
                        dqa_ref[sl, c] = dqc
            for h, c in enumerate(cols):
                dk_ref[ks, c] = dk[h].astype(dk_ref.dtype)
                dv_ref[ks, c] = dv[h].astype(dv_ref.dtype)

    blk = pl.BlockSpec((S, ATT_HP * SLOT), lambda b, h: (b, h))
    lse_spec = pl.BlockSpec((1, ATT_HP, 1, S), lambda b, h: (b, h, 0, 0))
    W = MLA_H * SLOT
    return _call_with_comm(
        body, (nseq, MLA_H // ATT_HP), "attn_bwd", [blk, blk, blk, blk, lse_spec, blk], [q, k, v, o, lse, do], [blk, blk, blk],
        [jax.ShapeDtypeStruct((T, W), BF16)] * 3, comm, scratch=[pltpu.VMEM((S, ATT_HP * SLOT), F32)])


def _rope_coeffs(pos, inv):
    half = ROPE // 2
    ang = pos * inv
    lane = lax.broadcasted_iota(jnp.int32, (1, SLOT), 1)
    sn = jnp.sin(ang)
    C = jnp.where(lane < NOPE, 1.0, jnp.where(lane < QK, jnp.cos(ang), 0.0))
    Sg = jnp.where((lane >= NOPE) & (lane < NOPE + half), -sn, jnp.where((lane >= NOPE + half) & (lane < QK), sn, 0.0))
    return C, Sg


def _rope_inputs(positions):
    half = ROPE // 2
    inv = ROPE_THETA ** (-jnp.arange(0, ROPE, 2, dtype=F32) / ROPE)
    row = jnp.zeros((1, SLOT), F32).at[0, NOPE:NOPE + half].set(inv).at[0, NOPE + half:QK].set(inv)
    return positions.astype(F32).reshape(-1, 1), row


def _place_k_rope(kr_lanes):
    r = lax.broadcasted_iota(jnp.int32, (SLOT, SLOT), 0)
    c = lax.broadcasted_iota(jnp.int32, (SLOT, SLOT), 1)
    return _dot_sel(kr_lanes, ((c == r + NOPE) & (r < ROPE)).astype(F32))


def rope_table(pos, inv):
    return rowwise(_rope_coeffs, [pos], [inv], [(SLOT, F32), (SLOT, F32)], [], "rope_table")


def rope_q_epilogue(accs, ex):
    C, Sg = ex[0], ex[1]
    reps = accs[0].shape[1] // SLOT
    return ((accs[0] * jnp.tile(C, (1, reps)) + _rope_swap(accs[0]) * jnp.tile(Sg, (1, reps))) * Q_SCALE,)


def rope_k_epilogue(accs, ex):
    C, Sg = ex[0], ex[1]
    kr = _place_k_rope(ex[2][:, SLOT:2 * SLOT])
    kr = kr * C + _rope_swap(kr) * Sg
    return (accs[0] + jnp.tile(kr, (1, accs[0].shape[1] // SLOT)),)


def _rope_swap(x):
    W = x.shape[1]
    half = ROPE // 2
    lane = lax.broadcasted_iota(jnp.int32, (1, W), 1) & (SLOT - 1)
    up = pltpu.roll(x, W - half, axis=1)
    dn = pltpu.roll(x, half, axis=1)
    return jnp.where((lane >= NOPE) & (lane < NOPE + half), up, jnp.where((lane >= NOPE + half) & (lane < QK), dn, 0.0))


def rope_slot_bwd(dq, dk, C, Sg, name):
    def fn(dqv, dkv, C, Sg):
        ct, stl = jnp.tile(C, (1, MLA_H)), jnp.tile(Sg, (1, MLA_H))
        dqo = dqv * ct - _rope_swap(dqv) * stl
        tot = dkv[:, 0:SLOT]
        for h in range(1, MLA_H):
            tot = tot + dkv[:, h * SLOT:(h + 1) * SLOT]
        u = tot * C - _rope_swap(tot) * Sg
        r = lax.broadcasted_iota(jnp.int32, (SLOT, SLOT), 0)
        c = lax.broadcasted_iota(jnp.int32, (SLOT, SLOT), 1)
        unplace = ((r == c + NOPE) & (c < ROPE)).astype(F32)
        return dqo, dkv, _dot_sel(u, unplace, terms=2)
    W = MLA_H * SLOT
    return rowwise(fn, [dq, dk, C, Sg], [], [(W, BF16), (W, BF16), (SLOT, BF16)], [], name)


XA_BLK = 512


def xattn_fwd(q, k, v, nseq, comm=None):
    T = q.shape[0]
    S = T // nseq
    M = k.shape[0] // nseq
    tq = min(XA_BLK, S)
    nq = S // tq
    scale = XA_D ** -0.5

    def body(q_ref, k_ref, v_ref, o_ref):
        s = _dot(q_ref[...], k_ref[...], "nt") * scale
        p = jnp.exp(s - jnp.max(s, axis=1, keepdims=True))
        p = p / jnp.sum(p, axis=1, keepdims=True)
        o_ref[...] = _dot(p, v_ref[...]).astype(o_ref.dtype)

    qs = pl.BlockSpec((tq, XA_D), lambda b, h, i: (b * nq + i, h))
    ks = pl.BlockSpec((M, XA_D), lambda b, h, i: (b, h))
    return _call_with_comm(body, (nseq, XA_H, nq), "xattn_fwd", [qs, ks, ks], [q, k, v], [qs],
                           [jax.ShapeDtypeStruct((T, XA_H * XA_D), BF16)], comm)


def xattn_bwd(q, k, v, do, nseq):
    T = q.shape[0]
    S = T // nseq
    M = k.shape[0] // nseq
    tq = min(XA_BLK, S)
    nq = S // tq
    scale = XA_D ** -0.5

    def body(q_ref, k_ref, v_ref, do_ref, dq_ref, dk_ref, dv_ref):
        @pl.when(pl.program_id(2) == 0)
        def _():
            dk_ref[...] = jnp.zeros_like(dk_ref)
            dv_ref[...] = jnp.zeros_like(dv_ref)

        qb, kb, vb, dob = q_ref[...], k_ref[...], v_ref[...], do_ref[...]
        s = _dot(qb, kb, "nt") * scale
        p = jnp.exp(s - jnp.max(s, axis=1, keepdims=True))
        p = p / jnp.sum(p, axis=1, keepdims=True)
        dp = _dot(dob, vb, "nt")
        ds = p * (dp - jnp.sum(dp * p, axis=1, keepdims=True)) * scale
        dq_ref[...] = _dot(ds, kb).astype(dq_ref.dtype)
        dk_ref[...] += _dot(ds, qb, "tn")
        dv_ref[...] += _dot(p, dob, "tn")

    qs = pl.BlockSpec((tq, XA_D), lambda b, h, i: (b * nq + i, h))
    ks = pl.BlockSpec((M, XA_D), lambda b, h, i: (b, h))
    return pl.pallas_call(
        body, grid=(nseq, XA_H, nq), name="xattn_bwd", in_specs=[qs, ks, ks, qs], out_specs=[qs, ks, ks],
        out_shape=[jax.ShapeDtypeStruct((T, XA_H * XA_D), BF16), jax.ShapeDtypeStruct(k.shape, F32),
                   jax.ShapeDtypeStruct(k.shape, F32)],
        compiler_params=_cp("parallel", "parallel", "arbitrary"),
    )(q, k, v, do)


CONV_BLK = 256


def _shift_down(x, s, rows):
    if s == 0:
        return x
    return jnp.where(rows >= s, pltpu.roll(x, s, axis=0), 0.0)


def _shift_up(x, s, rows):
    if s == 0:
        return x
    S = x.shape[0]
    return jnp.where(rows < S - s, pltpu.roll(x, S - s, axis=0), 0.0)


def conv_fwd(x, w, b, nseq):
    T, CH = x.shape
    S = T // nseq

    def body(x_ref, w_ref, b_ref, o_ref):
        xv = x_ref[...].astype(F32)
        rows = lax.broadcasted_iota(jnp.int32, (S, 1), 0)
        c = jnp.zeros_like(xv) + b_ref[...]
        for kk in range(CONV_K):
            c = c + w_ref[kk:kk + 1, :] * _shift_down(xv, CONV_K - 1 - kk, rows)
        o_ref[...] = (c * _sigmoid(c)).astype(o_ref.dtype)

    xs = pl.BlockSpec((S, CONV_BLK), lambda j, bb: (bb, j))
    return pl.pallas_call(
        body, grid=(CH // CONV_BLK, nseq), name="conv_fwd",
        in_specs=[xs, pl.BlockSpec((CONV_K, CONV_BLK), lambda j, bb: (0, j)), pl.BlockSpec((1, CONV_BLK), lambda j, bb: (0, j))],
        out_specs=xs, out_shape=jax.ShapeDtypeStruct((T, CH), BF16),
        compiler_params=_cp("parallel", "parallel"),
    )(x, w, b)


def conv_bwd(x, w, b, dout, nseq):
    T, CH = x.shape
    S = T // nseq

    def body(x_ref, w_ref, b_ref, do_ref, dx_ref, dw_ref, db_ref):
        @pl.when(pl.program_id(1) == 0)
        def _():
            dw_ref[...] = jnp.zeros_like(dw_ref)
            db_ref[...] = jnp.zeros_like(db_ref)

        xv = x_ref[...].astype(F32)
        rows = lax.broadcasted_iota(jnp.int32, (S, 1), 0)
        c = jnp.zeros_like(xv) + b_ref[...]
        sh = [_shift_down(xv, CONV_K - 1 - kk, rows) for kk in range(CONV_K)]
        for kk in range(CONV_K):
            c = c + w_ref[kk:kk + 1, :] * sh[kk]
        sg = _sigmoid(c)
        dc = do_ref[...].astype(F32) * sg * (1.0 + c * (1.0 - sg))
        dx = jnp.zeros_like(xv)
        for kk in range(CONV_K):
            dx = dx + w_ref[kk:kk + 1, :] * _shift_up(dc, CONV_K - 1 - kk, rows)
            dw_ref[kk:kk + 1, :] += jnp.sum(dc * sh[kk], axis=0, keepdims=True)
        dx_ref[...] = dx.astype(dx_ref.dtype)
        db_ref[...] += jnp.sum(dc, axis=0, keepdims=True)

    xs = pl.BlockSpec((S, CONV_BLK), lambda j, bb: (bb, j))
    ws = pl.BlockSpec((CONV_K, CONV_BLK), lambda j, bb: (0, j))
    bs = pl.BlockSpec((1, CONV_BLK), lambda j, bb: (0, j))
    return pl.pallas_call(
        body, grid=(CH // CONV_BLK, nseq), name="conv_bwd",
        in_specs=[xs, ws, bs, xs], out_specs=[xs, ws, bs],
        out_shape=[jax.ShapeDtypeStruct((T, CH), BF16), jax.ShapeDtypeStruct((CONV_K, CH), F32),
                   jax.ShapeDtypeStruct((1, CH), F32)],
        compiler_params=_cp("parallel", "arbitrary"),
    )(x, w, b, dout)


def _dims(a, b, mode):
    M = a.shape[1] if mode[0] == "t" else a.shape[0]
    K = a.shape[0] if mode[0] == "t" else a.shape[1]
    N = b.shape[0] if mode[1] == "t" else b.shape[1]
    return M, K, N


def _tile(dim, prefs):
    for p in prefs:
        if dim % p == 0:
            return p
    return dim


def mm(groups, out_dtypes, name, tm=None, tn=None, tk=None, epi=None, extras=(), comm=None, sub=1, n_sum=0):
    a0, b0, m0 = groups[0][0]
    M, K0, N = _dims(a0, b0, m0)
    tm = tm or _tile(M, (1024, 512, 256, 128))
    tn = tn or _tile(N, (1024, 512, 256, 128))
    flat = [p for g in groups for p in g]
    nk = 1 if tk is None else K0 // tk
    in_specs, args = [], []
    for a, b, mode in flat:
        _, K, _ = _dims(a, b, mode)
        kb = K if tk is None else tk
        in_specs.append(pl.BlockSpec((kb, tm), lambda i, j, k: (k, i)) if mode[0] == "t"
                        else pl.BlockSpec((tm, kb), lambda i, j, k: (i, k)))
        in_specs.append(pl.BlockSpec((tn, kb), lambda i, j, k: (j, k)) if mode[1] == "t"
                        else pl.BlockSpec((kb, tn), lambda i, j, k: (k, j)))
        args += [a, b]
    kinds = []
    for e in extras:
        kind, e = e if isinstance(e, tuple) else ("vec" if e.shape[0] == 1 and M != 1 else "tile", e)
        in_specs.append({"tile": pl.BlockSpec((tm, tn), lambda i, j, k: (i, j)),
                         "vec": pl.BlockSpec((1, tn), lambda i, j, k: (0, j)),
                         "rows": pl.BlockSpec((tm, e.shape[1]), lambda i, j, k: (i, 0)),
                         "whole": pl.BlockSpec(e.shape, lambda i, j, k: (0, 0))}[kind])
        kinds.append(kind)
        args.append(e)
    n_in = len(args)
    n_main = len(out_dtypes)
    n_out = n_main + n_sum
    assert n_sum == 0 or (tn == N and tk is None)
    ng = len(groups)
    sizes = [len(g) for g in groups]

    def body(*refs):
        ins, outs, accs = refs[:n_in], refs[n_in:n_in + n_out], refs[n_in + n_out:]
        kk = pl.program_id(2)

        def dots(rs):
            vals, pos = [], 0
            for gi in range(ng):
                acc = None
                for _ in range(sizes[gi]):
                    mode = flat[pos // 2][2]
                    av = ins[pos][:, rs] if mode[0] == "t" else ins[pos][rs, :]
                    d = _dot(av, ins[pos + 1][...], mode)
                    acc = d if acc is None else acc + d
                    pos += 2
                vals.append(acc)
            return vals

        def finish(accv, rs, first_chunk=True):
            ex = [(r[rs, :] if kind in ("tile", "rows") else r[...]).astype(F32) for kind, r in zip(kinds, ins[2 * len(flat):])]
            res = epi(accv, ex) if epi is not None else tuple(accv)
            for o, r in zip(outs[:n_main], res[:n_main]):
                o[rs, :] = r.astype(o.dtype)
            for o, r in zip(outs[n_main:], res[n_main:]):
                if first_chunk:
                    @pl.when(pl.program_id(0) == 0)
                    def _():
                        o[...] = r

                    @pl.when(pl.program_id(0) > 0)
                    def _():
                        o[...] += r
                else:
                    o[...] += r

        if nk == 1:
            for r in range(sub):
                rs = slice(r * (tm // sub), (r + 1) * (tm // sub))
                finish(dots(rs), rs, r == 0)
        else:
            vals = dots(slice(0, tm))
            finish = functools.partial(finish, rs=slice(0, tm))
            @pl.when(kk == 0)
            def _():
                for ar, vv in zip(accs, vals):
                    ar[...] = vv

            @pl.when(kk > 0)
            def _():
                for ar, vv in zip(accs, vals):
                    ar[...] += vv

            @pl.when(kk == nk - 1)
            def _():
                finish([ar[...] for ar in accs])

    grid = (M // tm, N // tn, nk)
    out_specs = [pl.BlockSpec((tm, tn), lambda i, j, k: (i, j)) for _ in out_dtypes] \
        + [pl.BlockSpec((1, tn), lambda i, j, k: (0, j))] * n_sum
    out_shape = [jax.ShapeDtypeStruct((M, N), dt) for dt in out_dtypes] + [jax.ShapeDtypeStruct((1, N), F32)] * n_sum
    scratch = [pltpu.VMEM((tm, tn), F32) for _ in range(ng if nk > 1 else 0)]
    sem = ("arbitrary" if n_sum else "parallel", "parallel", "arbitrary")
    if comm is not None:
        body = _attach(comm, body, n_in, n_out, *_grid_ends(grid))
        in_specs, args = in_specs + [HBM_SPEC] * len(comm.inputs), args + comm.inputs
        out_specs, out_shape = out_specs + [HBM_SPEC] * len(comm.out_shapes), out_shape + comm.out_shapes
        scratch, sem = scratch + comm.sems, ("arbitrary",) * 3
    return pl.pallas_call(body, grid=grid, name=name, in_specs=in_specs, out_specs=out_specs, out_shape=out_shape,
                          scratch_shapes=scratch, compiler_params=_cp(*sem))(*args)


def mm1(a, b, mode, out_dtype, name, **kw):
    return mm([[(a, b, mode)]], [out_dtype], name, **kw)[0]


ROW_BLK = 512


def rowwise(fn, rows, consts, outs, accs, name, tb=ROW_BLK, comm=None):
    rows = [r if isinstance(r, tuple) else (r, r.shape[1], 0) for r in rows]
    T = rows[0][0].shape[0]
    tb = min(tb, T)
    n_r, n_c, n_o, n_a = len(rows), len(consts), len(outs), len(accs)

    def body(*refs):
        vals = [r[...].astype(F32) for r in refs[:n_r + n_c]]
        res = fn(*vals)
        o_refs = refs[n_r + n_c:n_r + n_c + n_o]
        a_refs = refs[n_r + n_c + n_o:]
        for o, r in zip(o_refs, res[:n_o]):
            o[...] = r.astype(o.dtype)
        if n_a:
            @pl.when(pl.program_id(0) == 0)
            def _():
                for ar in a_refs:
                    ar[...] = jnp.zeros_like(ar)
            for ar, r in zip(a_refs, res[n_o:]):
                ar[...] += r

    return _call_with_comm(
        body, (T // tb,), name,
        [pl.BlockSpec((tb, w), functools.partial(lambda i, j: (i, j), j=j)) for _, w, j in rows]
        + [pl.BlockSpec(c.shape, lambda i: (0, 0)) for c in consts],
        [r[0] for r in rows] + list(consts),
        [pl.BlockSpec((tb, d), lambda i: (i, 0)) for d, _ in outs] + [pl.BlockSpec(s, lambda i: (0, 0)) for s in accs],
        [jax.ShapeDtypeStruct((T, d), dt) for d, dt in outs] + [jax.ShapeDtypeStruct(s, F32) for s in accs],
        comm, sem=("arbitrary" if n_a else "parallel",))


def _rms_stats(x):
    r = lax.rsqrt(jnp.mean(x * x, axis=-1, keepdims=True) + EPS)
    return r, x * r


def _rms_bwd(x, g, dy):
    r, xn = _rms_stats(x)
    dyg = dy * g
    dx = r * (dyg - xn * jnp.mean(dyg * xn, axis=-1, keepdims=True))
    return dx, jnp.sum(dy * xn, axis=0, keepdims=True)


def rms_fwd(x, g, name, comm=None):
    res = rowwise(lambda xv, gv: (_rms_stats(xv)[1] * gv,), [x], [g], [(x.shape[1], BF16)], [], name, comm=comm)
    return res[0] if comm is None else (res[0], res[1:])


def rms_bwd(x, g, dy, name, resid=None, dx_dtype=F32):
    def fn(*v):
        if resid is None:
            xv, dyv, gv = v
            dx, dg = _rms_bwd(xv, gv, dyv)
        else:
            xv, dyv, rv, gv = v
            dx, dg = _rms_bwd(xv, gv, dyv)
            dx = dx + rv
        return dx, dg
    rows = [x, dy] + ([] if resid is None else [resid])
    return rowwise(fn, rows, [g], [(x.shape[1], dx_dtype)], [(1, x.shape[1])], name)


def mm_rms_bwd(pairs, x, g, name, resid=None, dx_dtype=F32, comm=None, token=None):
    def epi(accs, ex):
        dx, dg = _rms_bwd(ex[0], ex[-1], accs[0])
        return (dx if resid is None else dx + ex[1]), dg
    extras = [x] + ([] if resid is None else [resid]) + ([] if token is None else [("whole", token)]) + [g]
    return mm([pairs], [dx_dtype], name, tm=min(256, x.shape[0]), tn=x.shape[1], epi=epi, extras=extras, comm=comm, n_sum=1)


def mm_resid(a, b, x, g, wgt, name, comm=None, target=None):
    def epi(accs, ex):
        y = ex[0] + wgt * _rms_stats(accs[0])[1] * ex[1]
        if target is None:
            return accs[0], y
        d = y - ex[2]
        return accs[0], d / D, jnp.sum(d * d, axis=0, keepdims=True)
    return mm([[(a, b, "nn")]], [F32, F32], name, tm=min(512, a.shape[0]), tn=b.shape[1], epi=epi,
              extras=[x, g] + ([] if target is None else [target]), sub=2, comm=comm, n_sum=0 if target is None else 1)


def resid_bwd(h, g, dy, wgt, name):
    def fn(hv, dyv, gv):
        dx, dg = _rms_bwd(hv, gv, dyv)
        return wgt * dx, wgt * dg
    return rowwise(fn, [h, dy], [g], [(h.shape[1], BF16)], [(1, h.shape[1])], name)


def _silu_parts(g):
    s = _sigmoid(g)
    return g * s, s * (1.0 + g * (1.0 - s))


def gated_norm_fwd(y, z, g, name):
    W = SSD_INNER // SSD_G

    def fn(yv, zv, gv):
        yg = yv * _silu_parts(zv)[0]
        return (jnp.concatenate([_rms_stats(yg[:, i * W:(i + 1) * W])[1] for i in range(SSD_G)], axis=1) * gv,)
    return rowwise(fn, [y, z], [g], [(SSD_INNER, BF16)], [], name)[0]


def gated_norm_bwd(y, z, dyn, g, name):
    W = SSD_INNER // SSD_G

    def fn(yv, zv, dv, gv):
        sil, dsil = _silu_parts(zv)
        yg = yv * sil
        parts = [_rms_bwd(yg[:, i * W:(i + 1) * W], gv[:, i * W:(i + 1) * W], dv[:, i * W:(i + 1) * W]) for i in range(SSD_G)]
        dyg = jnp.concatenate([p[0] for p in parts], axis=1)
        dg = jnp.concatenate([p[1] for p in parts], axis=1)
        return dyg * sil, dyg * yv * dsil, dg
    return rowwise(fn, [y, z, dyn], [g], [(SSD_INNER, BF16), (SSD_INNER, BF16)], [(1, SSD_INNER)], name)


def merge_fwd(gl, ys, ym, gb, name):
    def fn(glv, ysv, ymv, gbv):
        gt = _sigmoid(glv + gbv)
        return (gt[:, :D] * ysv + gt[:, D:] * ymv,)
    return rowwise(fn, [gl, ys, ym], [gb], [(D, BF16)], [], name)[0]


def merge_bwd(gl, ys, ym, dm, gb, name):
    def fn(glv, ysv, ymv, dmv, gbv):
        gt = _sigmoid(glv + gbv)
        gs, gm = gt[:, :D], gt[:, D:]
        dgl = jnp.concatenate([dmv * ysv * gs * (1.0 - gs), dmv * ymv * gm * (1.0 - gm)], axis=1)
        return dmv * gs, dmv * gm, dgl, jnp.sum(dgl, axis=0, keepdims=True)
    return rowwise(fn, [gl, ys, ym, dm], [gb], [(D, BF16), (D, BF16), (2 * D, BF16)], [(1, 2 * D)], name)


def _adamw_math(wv, gv, mv, vv):
    mn = B1 * mv + (1.0 - B1) * gv
    vn = B2 * vv + (1.0 - B2) * (gv * gv)
    mh = mn / (1.0 - B1 ** STEP)
    vh = vn / (1.0 - B2 ** STEP)
    return -LR * (mh / (jnp.sqrt(vh) + AEPS) + WD * wv), mn, vn


def adamw(w, g, m, v, name):
    R, C = w.shape
    tb = _tile(R, (256, 128, 64, 32, 16, 8))
    return rowwise(_adamw_math, [w, g, m, v], [], [(C, F32)] * 3, [], name, tb=tb)


def adamw_small(packed, ws, ms, vs):
    k = len(ws)
    sizes = [x.shape[1] for x in ws]

    def body(*refs):
        p_ref, w_refs, m_refs, v_refs = refs[0], refs[1:1 + k], refs[1 + k:1 + 2 * k], refs[1 + 2 * k:1 + 3 * k]
        outs = refs[1 + 3 * k:]
        r0 = 0
        for i, n in enumerate(sizes):
            nr = -(-n // 128)
            g = jnp.concatenate([p_ref[r0 + r:r0 + r + 1, :] for r in range(nr)], axis=1)[:, :n]
            r0 += nr
            outs[i][...] = g
            outs[k + i][...], outs[2 * k + i][...], outs[3 * k + i][...] = _adamw_math(w_refs[i][...], g, m_refs[i][...], v_refs[i][...])

    res = pl.pallas_call(body, name="adamw_small",
                         out_shape=[jax.ShapeDtypeStruct((1, n), F32) for _ in range(4) for n in sizes])(packed, *ws, *ms, *vs)
    return [res[j * k:(j + 1) * k] for j in range(4)]


def adamw_from_slots(recv, piece, w, m, v, name, token=None):
    K, n = w.shape
    ns = recv.shape[0]
    assert recv.shape[2] == n and recv.shape[1] % K == 0
    tb = _tile(K, (256, 176, 128, 64, 32, 16, 8)) if K % 8 == 0 else K
    r_spec = pl.BlockSpec((ns, tb, n), lambda i: (0, piece * (K // tb) + i, 0))
    w_spec = pl.BlockSpec((tb, n), lambda i: (i, 0))

    def body(r_ref, w_ref, m_ref, v_ref, *rest):
        g_ref, d_ref, mo_ref, vo_ref = rest[-4:]
        g = r_ref[0].astype(F32)
        for s in range(1, ns):
            g = g + r_ref[s].astype(F32)
        g_ref[...] = g
        d_ref[...], mo_ref[...], vo_ref[...] = _adamw_math(w_ref[...], g, m_ref[...], v_ref[...])

    extra = [] if token is None else [token]
    return pl.pallas_call(
        body, grid=(K // tb,), name=name,
        in_specs=[r_spec, w_spec, w_spec, w_spec] + [pl.BlockSpec(t.shape, lambda i: (0, 0)) for t in extra], out_specs=[w_spec] * 4,
        out_shape=[jax.ShapeDtypeStruct((K, n), F32)] * 4, compiler_params=_cp("parallel"),
    )(recv, w, m, v, *extra)


def _me():
    return lax.axis_index("x"), lax.axis_index("y"), lax.axis_index("c")


def _dev_index():
    x, y, c = _me()
    return 4 * x + 2 * y + c


HBM_SPEC = pl.BlockSpec(memory_space=pl.ANY)


class GatherComm:
    def __init__(self, shards):
        self.inputs = [s for s, _ in shards]
        self.rows = [list(r) for _, r in shards]
        n = len(shards)
        self.out_shapes = [jax.ShapeDtypeStruct((N_DEV, r, s.shape[1]), s.dtype) for s, rows in shards for r in rows]
        self.sems = [pltpu.SemaphoreType.DMA((7 * n,)), pltpu.SemaphoreType.DMA((7 * n,)), pltpu.SemaphoreType.DMA((n,))]

    def _plan(self, x_refs, out_refs, sems):
        send_sems, recv_sems, local_sems = sems
        x, y, c = _me()
        me, sibling = (x, y, c), (x, y, 1 - c)
        chips = [(1 - x, y), (x, 1 - y), (1 - x, 1 - y)]
        index = lambda px, py, pc: 4 * px + 2 * py + pc
        mine, first, passed, whole = [], [], [], []
        pos = 0
        for i, rows in enumerate(self.rows):
            kw = lambda k: dict(send_sem=send_sems.at[7 * i + k], recv_sem=recv_sems.at[7 * i + k], device_id_type=MESH)
            r0 = 0
            fwd = [[] for _ in chips]
            for j, nr in enumerate(rows):
                out, src = out_refs[pos + j], x_refs[i].at[pl.ds(r0, nr)]
                mine.append(pltpu.make_async_copy(src, out.at[index(*me)], local_sems.at[i]))
                first.append(pltpu.make_async_remote_copy(src_ref=src, dst_ref=out.at[index(*me)], device_id=sibling, **kw(0)))
                for jj, chip in enumerate(chips):
                    first.append(pltpu.make_async_remote_copy(src_ref=src, dst_ref=out.at[index(*me)], device_id=(*chip, c),
                                                              **kw(1 + jj)))
                    blk = out.at[index(*chip, c)]
                    fwd[jj].append(pltpu.make_async_remote_copy(src_ref=blk, dst_ref=blk, device_id=sibling, **kw(4 + jj)))
                r0 += nr
            passed.append(fwd)
            whole.append([pltpu.make_async_remote_copy(src_ref=x_refs[i], dst_ref=x_refs[i], device_id=sibling, **kw(k))
                          for k in range(7)])
            pos += len(rows)
        return mine, first, passed, whole

    def start(self, x_refs, out_refs, sems):
        mine, first, _, _ = self._plan(x_refs, out_refs, sems)
        for cp in mine + first:
            cp.start()

    def finish(self, x_refs, out_refs, sems):
        _, _, passed, whole = self._plan(x_refs, out_refs, sems)
        local_sems = sems[2]
        for i, fwd in enumerate(passed):
            for jj in range(3):
                whole[i][1 + jj].wait_recv()
                for cp in fwd[jj]:
                    cp.start()
        for i in range(len(passed)):
            whole[i][0].wait_recv()
            for jj in range(3):
                whole[i][4 + jj].wait_recv()
        for i in range(len(passed)):
            for k in range(7):
                whole[i][k].wait_send()
            pltpu.make_async_copy(x_refs[i], x_refs[i], local_sems.at[i]).wait()


def run_comm(comm, name):
    n_in, n_out = len(comm.inputs), len(comm.out_shapes)

    def body(*refs):
        ins, outs, sems = refs[:n_in], refs[n_in:n_in + n_out], refs[n_in + n_out:]
        comm.start(ins, outs, sems)
        comm.finish(ins, outs, sems)

    return pl.pallas_call(body, name=name, out_shape=comm.out_shapes, in_specs=[HBM_SPEC] * n_in,
                          out_specs=[HBM_SPEC] * n_out, scratch_shapes=comm.sems)(*comm.inputs)


def _attach(comm, body, n_in, n_out, first, last):
    if comm is None:
        return body
    ci, co, cs = len(comm.inputs), len(comm.out_shapes), len(comm.sems)

    def wrapped(*refs):
        h_in, c_in = refs[:n_in], refs[n_in:n_in + ci]
        h_out, c_out = refs[n_in + ci:n_in + ci + n_out], refs[n_in + ci + n_out:n_in + ci + n_out + co]
        rest = refs[n_in + ci + n_out + co:]
        h_scr, c_sem = rest[:len(rest) - cs], rest[len(rest) - cs:]

        @pl.when(first())
        def _():
            comm.start(c_in, c_out, c_sem)

        body(*h_in, *h_out, *h_scr)

        @pl.when(last())
        def _():
            comm.finish(c_in, c_out, c_sem)

    return wrapped


def _grid_ends(grid):
    first = lambda: functools.reduce(lambda a, b: a & b, [pl.program_id(i) == 0 for i in range(len(grid))])
    last = lambda: functools.reduce(lambda a, b: a & b, [pl.program_id(i) == g - 1 for i, g in enumerate(grid)])
    return first, last


def _call_with_comm(body, grid, name, in_specs, args, out_specs, out_shape, comm, scratch=(), sem=None):
    sem = sem or ("parallel",) * len(grid)
    scratch = list(scratch)
    if comm is not None:
        body = _attach(comm, body, len(args), len(out_shape), *_grid_ends(grid))
        in_specs, args = in_specs + [HBM_SPEC] * len(comm.inputs), args + comm.inputs
        out_specs, out_shape = out_specs + [HBM_SPEC] * len(comm.out_shapes), out_shape + comm.out_shapes
        scratch, sem = scratch + comm.sems, ("arbitrary",) * len(grid)
    return pl.pallas_call(body, grid=grid, name=name, in_specs=in_specs, out_specs=out_specs, out_shape=out_shape,
                          scratch_shapes=scratch, compiler_params=_cp(*sem))(*args)


class ScatterComm:
    def __init__(self, groups):
        self.sizes = [len(g) for g in groups]
        self.rows = [[pc.shape[1] for pc in g] for g in groups]
        ng = len(groups)
        self.inputs = [pc for g in groups for pc in g]
        self.out_shapes = [jax.ShapeDtypeStruct((N_DEV, sum(self.rows[gi]), g[0].shape[2]), g[0].dtype) for gi, g in enumerate(groups)]
        self.sems = [pltpu.SemaphoreType.DMA((7 * ng,)), pltpu.SemaphoreType.DMA((7 * ng,)), pltpu.SemaphoreType.DMA((ng,))]

    def _peers(self):
        x, y, c = _me()
        out = []
        for k in range(1, N_DEV):
            px = 1 - x if k & 4 else x
            py = 1 - y if k & 2 else y
            pc = 1 - c if k & 1 else c
            out.append((k, 4 * px + 2 * py + pc, dict(device_id=(px, py, pc), device_id_type=MESH)))
        return 4 * x + 2 * y + c, out

    def start(self, ins, outs, sems):
        send_sems, recv_sems, local_sems = sems
        me, peers = self._peers()
        pos = 0
        for gi, size in enumerate(self.sizes):
            for i, pc in enumerate(ins[pos:pos + size]):
                dst = outs[gi].at[me, pl.ds(sum(self.rows[gi][:i]), self.rows[gi][i])]
                pltpu.make_async_copy(pc.at[me], dst, local_sems.at[gi]).start()
                for k, peer, kw in peers:
                    pltpu.make_async_remote_copy(src_ref=pc.at[peer], dst_ref=dst, send_sem=send_sems.at[7 * gi + k - 1],
                                                 recv_sem=recv_sems.at[7 * gi + k - 1], **kw).start()
            pos += size

    def finish(self, ins, outs, sems):
        send_sems, recv_sems, local_sems = sems
        me, peers = self._peers()
        whole = [pltpu.make_async_remote_copy(src_ref=outs[gi].at[peer], dst_ref=outs[gi].at[peer],
                                              send_sem=send_sems.at[7 * gi + k - 1], recv_sem=recv_sems.at[7 * gi + k - 1], **kw)
                 for gi in range(len(self.sizes)) for k, peer, kw in peers]
        for cp in whole:
            cp.wait_recv()
        for cp in whole:
            cp.wait_send()
        for gi in range(len(self.sizes)):
            pltpu.make_async_copy(outs[gi].at[me], outs[gi].at[me], local_sems.at[gi]).wait()


def _peer_list():
    x, y, c = _me()
    out = []
    for k in range(1, N_DEV):
        px = 1 - x if k & 4 else x
        py = 1 - y if k & 2 else y
        pc = 1 - c if k & 1 else c
        out.append((k, 4 * px + 2 * py + pc, dict(device_id=(px, py, pc), device_id_type=MESH)))
    return 4 * x + 2 * y + c, out


SEM_SPEC = pl.BlockSpec(memory_space=pltpu.SEMAPHORE)
HBM_ONLY = pl.BlockSpec(memory_space=pltpu.HBM)
N_SPLIT_SEMS = 2 * (N_DEV - 1)


def exchange_start(piece, after, name):
    def body(piece_ref, land_ref, after_ref, *outs):
        sems, token = outs[:N_SPLIT_SEMS], outs[-1]
        me, peers = _peer_list()
        for k, peer, kw in peers:
            pltpu.make_async_remote_copy(src_ref=piece_ref.at[peer], dst_ref=land_ref.at[me], send_sem=sems[k - 1],
                                         recv_sem=sems[N_DEV - 2 + k], **kw).start()
        token[...] = jnp.zeros_like(token)

    res = pl.pallas_call(
        body, name=name + "_start",
        out_shape=(pltpu.SemaphoreType.DMA(()),) * N_SPLIT_SEMS + (pltpu.HBM(piece.shape, piece.dtype), jax.ShapeDtypeStruct((8, 128), F32)),
        in_specs=(HBM_SPEC, HBM_ONLY, HBM_SPEC),
        out_specs=(SEM_SPEC,) * N_SPLIT_SEMS + (HBM_ONLY, pl.BlockSpec(memory_space=pltpu.VMEM)),
        input_output_aliases={1: N_SPLIT_SEMS},
        compiler_params=pltpu.CompilerParams(has_side_effects=pltpu.SideEffectType.DATAFLOW_SIDE_EFFECTING),
    )(piece, pltpu.with_memory_space_constraint(lax.empty(piece.shape, piece.dtype), pltpu.HBM), after)
    return res[:N_SPLIT_SEMS], res[N_SPLIT_SEMS], res[N_SPLIT_SEMS + 1]


def exchange_wait(sems, piece, land, after, name):
    after = list(after)
    def body(piece_ref, land_ref, *rest):
        sem_refs = rest[:N_SPLIT_SEMS]
        me, peers = _peer_list()
        for k, peer, kw in peers:
            cp = pltpu.make_async_remote_copy(src_ref=piece_ref.at[peer], dst_ref=land_ref.at[peer], send_sem=sem_refs[k - 1],
                                              recv_sem=sem_refs[N_DEV - 2 + k], **kw)
            cp.wait_send()
            cp.wait_recv()

    return pl.pallas_call(
        body, name=name + "_wait",
        out_shape=(pltpu.HBM(land.shape, land.dtype),),
        in_specs=(HBM_SPEC, HBM_ONLY) + (SEM_SPEC,) * N_SPLIT_SEMS + (HBM_SPEC,) * len(after), out_specs=(HBM_ONLY,),
        input_output_aliases={1: 0},
        compiler_params=pltpu.CompilerParams(has_side_effects=pltpu.SideEffectType.DATAFLOW_SIDE_EFFECTING),
    )(piece, land, *sems, *after)[0]


def sum_slots(recv, name, tr):
    n, R, C = recv.shape

    def body(r_ref, o_ref):
        acc = r_ref[0].astype(F32)
        for s in range(1, n):
            acc = acc + r_ref[s].astype(F32)
        o_ref[...] = acc

    return pl.pallas_call(
        body, grid=(R // tr,), name=name,
        in_specs=[pl.BlockSpec((n, tr, C), lambda i: (0, i, 0))], out_specs=pl.BlockSpec((tr, C), lambda i: (i, 0)),
        out_shape=jax.ShapeDtypeStruct((R, C), F32), compiler_params=_cp("parallel"),
    )(recv)


PACK_W, FLAT_W = 1024, 128
MAIN = [
    ("ffn1_w_gate", "col"), ("ffn1_w_up", "col"), ("ffn1_w_down", "row"),
    ("ffn2_w_gate", "col"), ("ffn2_w_up", "col"), ("ffn2_w_down", "row"),
    ("w_ssd_proj", "row"), ("w_mla_proj", "row"), ("w_out", "row"),
    ("w_xq", "row"), ("w_xk", "row"), ("w_xv", "row"), ("w_xo", "row"),
    ("w_uk", "col"), ("w_uv", "col"), ("w_in", "col"),
]
FLAT = [("w_uq", "col")]
BIG = MAIN + FLAT
SMALL = ["ffn1_pre_g", "ffn1_post_g", "mix_pre_g", "conv_b", "dt_bias", "a_log", "d_skip", "ssd_norm_g", "q_norm_g",
         "kv_norm_g", "gate_bias", "mix_post_g", "xa_pre_g", "mem_norm_g", "xa_post_g", "ffn2_pre_g", "ffn2_post_g"]
WEIGHTS = ['ffn1_pre_g', 'ffn1_w_gate', 'ffn1_w_up', 'ffn1_w_down', 'ffn1_post_g', 'mix_pre_g', 'w_in', 'conv_w', 'conv_b',
           'dt_bias', 'a_log', 'd_skip', 'ssd_norm_g', 'w_ssd_proj', 'q_norm_g', 'w_uq', 'kv_norm_g', 'w_uk', 'w_uv',
           'w_mla_proj', 'gate_bias', 'w_out', 'mix_post_g', 'xa_pre_g', 'mem_norm_g', 'w_xq', 'w_xk', 'w_xv', 'w_xo',
           'xa_post_g', 'ffn2_pre_g', 'ffn2_w_gate', 'ffn2_w_up', 'ffn2_w_down', 'ffn2_post_g']


def _pack_rows(w, kind, width):
    m = w[0].T if kind == "col" else w[0]
    return m.reshape(-1, width)


KIND = dict(BIG)
GATHER_PLAN = {
    "ffn1_pre": (["ffn1_w_gate", "ffn1_w_up"], []),
    "ffn1_gate_up": (["ffn1_w_down", "w_in@0"], []),
    "ffn1_down": (["w_in@1"], ["conv_w"]),
    "ssd_fwd": (["w_ssd_proj", "w_mla_proj", "w_out", "w_uk", "w_uv"], ["w_uq"]),
    "attn_fwd": (["w_xq", "w_xk", "w_xv", "w_xo", "ffn2_w_gate", "ffn2_w_up", "ffn2_w_down"], []),
}
EARLY_EXCHANGE, LAST_EXCHANGE = "early", "last"
SCATTER_PLAN = {
    "attn_bwd": [["ffn2_w_gate", "ffn2_w_up", "ffn2_w_down"], ["w_xq", "w_xk", "w_xv", "w_xo"]],
    "ssd_bwd": [["w_ssd_proj", "w_mla_proj", "w_out"], ["w_uk", "w_uv"], ["w_uq"]],
    "in_bwd": [["w_in#0"]],
    "ffn1:down_bwd": [["w_in#1"]],
    "ffn1:dwd": [["w_in#2"]],
    "ffn1:dwg": [["ffn1_w_down#0"]],
    "ffn1:dwu": [["ffn1_w_down#1"]],
    "early": [["ffn1_w_gate"]],
    "last": [["ffn1_w_up"]],
}
PARTS = {"w_in@0": ("w_in", 0, 336), "w_in@1": ("w_in", 336, 662),
         "w_in#0": ("w_in", 0, 336), "w_in#1": ("w_in", 336, 496), "w_in#2": ("w_in", 496, 662),
         "ffn1_w_down#0": ("ffn1_w_down", 0, 176), "ffn1_w_down#1": ("ffn1_w_down", 176, 352)}


def _parts_of(base, mark):
    return sorted(pn for pn, (b, _, _) in PARTS.items() if b == base and mark in pn)


class Stage:
    def __init__(self, w):
        self.w = w
        self.width = {n: PACK_W if (n, k) in MAIN else FLAT_W for n, k in BIG}
        self.nrows = {n: math.prod(w[n].shape) // self.width[n] for n, _ in BIG}
        self.recv = {}
        self.split = {}
        self.arrived_parts = {}

    def _shards(self, tag):
        names_main, names_flat = GATHER_PLAN[tag]

        def pack(n):
            if n == "conv_w":
                return _pad_rows(lax.bitcast_convert_type(self.w[n][0], BF16).reshape(-1, FLAT_W), 16)
            base, r0, r1 = PARTS.get(n, (n, 0, None))
            return _pack_rows(self.w[base], KIND[base], self.width[base])[r0:r1].astype(BF16)
        shards = []
        if names_main:
            pieces = [pack(n) for n in names_main]
            shards.append((jnp.concatenate(pieces, axis=0), [pc.shape[0] for pc in pieces]))
        if names_flat:
            pieces = [pack(n) for n in names_flat]
            shards.append((jnp.concatenate(pieces, axis=0), [pc.shape[0] for pc in pieces]))
        return shards

    def gather(self, tag):
        return GatherComm(self._shards(tag)) if tag in GATHER_PLAN else None

    def gathered(self, tag, outs, W, p):
        if tag not in GATHER_PLAN:
            return
        names_main, names_flat = GATHER_PLAN[tag]
        outs = list(outs)
        for n in names_main + names_flat:
            rows = outs.pop(0)
            if n == "conv_w":
                cw = self.w[n]
                bits = rows[:, :2 * math.prod(cw.shape) // FLAT_W].reshape((N_DEV,) + cw.shape[1:] + (2,))
                p[n] = lax.bitcast_convert_type(bits, F32).transpose(1, 0, 2).reshape(cw.shape[1], -1)
                continue
            if n in PARTS:
                self.arrived_parts[n] = rows
                base = PARTS[n][0]
                mine = _parts_of(base, "@")
                if not all(pn in self.arrived_parts for pn in mine):
                    continue
                n, rows = base, jnp.concatenate([self.arrived_parts[pn] for pn in mine], axis=1)
            K = self.w[n].shape[1] if KIND[n] == "col" else PACK_W
            W[n] = rows.reshape(-1, K)

    def pieces(self, tag, gw):
        def piece(n):
            if n in PARTS:
                base, r0, r1 = PARTS[n]
                return gw[base].reshape(N_DEV, self.nrows[base], self.width[base])[:, r0:r1]
            return gw[n].reshape(N_DEV, self.nrows[n], self.width[n])
        return [[piece(n) for n in names] for names in SCATTER_PLAN[tag]]

    def scatter(self, tag, gw):
        return ScatterComm(self.pieces(tag, gw)) if tag in SCATTER_PLAN else None

    def scattered(self, tag, outs):
        if tag in SCATTER_PLAN:
            self.recv[tag] = outs

    def split_start(self, tag, gw, after=None, behind=None):
        after = after if behind is None else self.recv[behind][0]
        piece, = self.pieces(tag, gw)[0]
        sems, land, token = exchange_start(piece, after, "exchange_" + tag)
        self.split[tag] = (sems, piece, land)
        return token

    def split_wait(self, tag, after):
        sems, piece, land = self.split.pop(tag)
        land = exchange_wait(sems, piece, land, after, "exchange_" + tag)
        me = _dev_index()
        return lax.dynamic_update_index_in_dim(land, lax.dynamic_index_in_dim(piece, me, 0, keepdims=False), me, 0)


def _pad_rows(a, mult):
    r = (-a.shape[0]) % mult
    return a if r == 0 else jnp.concatenate([a, jnp.zeros((r,) + a.shape[1:], a.dtype)], axis=0)


def _pack_small(vals, loss_row=None, conv_w=None):
    rows = []
    for v in vals:
        f = v.reshape(-1)
        f = jnp.concatenate([f, jnp.zeros(((-f.shape[0]) % 128,), F32)])
        rows.append(f.reshape(-1, 128))
    if conv_w is not None:
        rows.append(conv_w.reshape(-1, 128))
    if loss_row is not None:
        rows.append(loss_row)
    return _pad_rows(jnp.concatenate(rows, axis=0), 8)


def _tn(a, b, name, out_dtype=BF16, comm=None):
    M, N = a.shape[1], b.shape[1]
    T = a.shape[0]
    tm = M if M <= 1536 else M // 2
    tk = 2048 if T % 2048 == 0 and T > 2048 else None
    res = mm([[(a, b, "tn")]], [out_dtype], name, tm=tm, tn=N, tk=tk, comm=comm)
    return res[0] if comm is None else (res[0], res[1:])


class NoStage:
    def gather(self, tag):
        return None

    def gathered(self, tag, outs, W, p):
        pass

    def scatter(self, tag, gw):
        return None

    def scattered(self, tag, outs):
        pass

    def split_start(self, tag, gw, after=None, behind=None):
        return None


def _ffn_fwd(x, gpre, gpost, W, p, tag, stage, target=None):
    comm = stage.gather(tag + "_pre")
    h = rms_fwd(x, gpre, tag + "_pre", comm=comm)
    if comm is not None:
        h, arrived = h
        stage.gathered(tag + "_pre", arrived, W, p)

    def swi(accs, ex):
        sil, dsil = _silu_parts(accs[0])
        return sil, accs[1] * dsil, sil * accs[1]
    G, U, A, *arrived = mm([[(h, W[tag + "_w_gate"], "nt")], [(h, W[tag + "_w_up"], "nt")]], [BF16, BF16, BF16], tag + "_gate_up",
                           tn=DFF // 2, epi=swi, comm=stage.gather(tag + "_gate_up"), sub=4 if h.shape[0] % 1024 == 0 else 1)
    stage.gathered(tag + "_gate_up", arrived, W, p)
    H, y, *rest = mm_resid(A, W[tag + "_w_down"], x, gpost, FFN_RES, tag + "_down", comm=stage.gather(tag + "_down"), target=target)
    saved = (x, h, G, U, A, H)
    if target is not None:
        return y, saved, rest[0]
    stage.gathered(tag + "_down", rest, W, p)
    return y, saved


def _ffn_bwd(dy, saved, gpre, gpost, wg_t, wu_t, wd, tag, stage, gw):
    x, h, G, U, A, H = saved
    dH, dgpost = resid_bwd(H, gpost, dy, FFN_RES, tag + "_post_bwd")

    def dswi(accs, ex):
        return accs[0] * ex[1], accs[0] * ex[0]

    def hosted(where, call):
        comm = stage.scatter(tag + ":" + where, gw)
        res = call(comm)
        if comm is None:
            return res
        stage.scattered(tag + ":" + where, res[1])
        return res[0]

    res = hosted("down_bwd", lambda comm: (lambda r: r if comm is None else (r[:2], r[2:]))(
        mm([[(dH, wd, "nt")]], [BF16, BF16], tag + "_down_bwd", tn=DFF // 2, epi=dswi, extras=[G, U], comm=comm,
           sub=4 if dH.shape[0] % 1024 == 0 else 1)))
    dG, dU = res
    gw[tag + "_w_down"] = hosted("dwd", lambda comm: _tn(A, dH, tag + "_dwd", comm=comm))
    gw[tag + "_w_gate"] = hosted("dwg", lambda comm: _tn(dG, h, tag + "_dwg", comm=comm))
    gw[tag + "_w_up"] = hosted("dwu", lambda comm: _tn(dU, h, tag + "_dwu", comm=comm))
    token = stage.split_start(EARLY_EXCHANGE, gw, behind=tag + ":dwu") if tag == "ffn1" else None
    dx, dgpre = mm_rms_bwd([(dG, wg_t, "nn"), (dU, wu_t, "nn")], x, gpre, tag + "_gate_up_bwd", resid=dy, token=token)
    return dx, dgpre, dgpost


def _local_step(x, mem, positions, tgt, W, p, stage=None):
    stage = stage or NoStage()
    nseq = x.shape[0]
    T = nseq * x.shape[1]
    x0 = x.reshape(T, D)
    mem2 = mem.reshape(-1, D)

    x1, ffn1 = _ffn_fwd(x0, p["ffn1_pre_g"], p["ffn1_post_g"], W, p, "ffn1", stage)

    w_in_t = W["w_in"]
    bounds = [0]
    for n in (SSD_INNER, CONV_CH, SSD_H, QR, KVR, ROPE, 2 * D):
        bounds.append(bounds[-1] + n)
    wt_z, wt_xbc, wt_dt, wt_q, wt_kv, wt_kr, wt_gate = [w_in_t[bounds[i]:bounds[i + 1]] for i in range(7)]
    wt_dt, wt_kr = _pad_rows(wt_dt, SLOT), _pad_rows(wt_kr, SLOT)
    wt_dtkr = jnp.concatenate([wt_dt, wt_kr], axis=0)
    hm = rms_fwd(x1, p["mix_pre_g"], "mix_pre")
    z = mm1(hm, wt_z, "nt", BF16, "in_z")
    xbc = mm1(hm, wt_xbc, "nt", BF16, "in_xbc")
    q_c = mm1(hm, wt_q, "nt", F32, "in_q", tn=QR)
    kv_c = mm1(hm, wt_kv, "nt", F32, "in_kv")
    dtkr = mm1(hm, wt_dtkr, "nt", F32, "in_dtkr")
    gl = mm1(hm, wt_gate, "nt", BF16, "in_gate")

    xbc_act = conv_fwd(xbc, p["conv_w"], p["conv_b"], nseq)
    y_ssd_core, prev, *arrived = ssd_fwd(xbc_act, dtkr, p["dt_bias"], p["a_log"], p["d_skip"], nseq, comm=stage.gather("ssd_fwd"))
    stage.gathered("ssd_fwd", arrived, W, p)
    yn = gated_norm_fwd(y_ssd_core, z, p["ssd_norm_g"], "ssd_norm")
    y_ssd = mm1(yn, W["w_ssd_proj"], "nn", BF16, "ssd_proj")

    slot_rows = lambda wt, per: jnp.pad(wt.reshape(MLA_H, per, -1), ((0, 0), (0, SLOT - per), (0, 0))).reshape(MLA_H * SLOT, -1)
    wq_s, wk_s, wv_s = slot_rows(W["w_uq"], QK), slot_rows(W["w_uk"], NOPE), slot_rows(W["w_uv"], VD)
    wo_s = slot_rows(W["w_mla_proj"], VD)
    qn = rms_fwd(q_c, p["q_norm_g"], "q_norm")
    rope_c, rope_s = rope_table(*_rope_inputs(positions))
    rope_args = [("rows", rope_c), ("rows", rope_s)]
    Qc, = mm([[(qn, wq_s, "nt")]], [BF16], "uq", epi=rope_q_epilogue, extras=rope_args, sub=4 if T % 1024 == 0 else 1)
    kvn = rms_fwd(kv_c, p["kv_norm_g"], "kv_norm")
    Kc, = mm([[(kvn, wk_s, "nt")]], [BF16], "uk", epi=rope_k_epilogue, extras=rope_args + [("rows", dtkr)],
             sub=4 if T % 1024 == 0 else 1)
    v_s = mm1(kvn, wv_s, "nt", BF16, "uv")
    o_s, lse, *arrived = attn_slot_fwd(Qc, Kc, v_s, nseq, comm=stage.gather("attn_fwd"))
    stage.gathered("attn_fwd", arrived, W, p)
    y_mla = mm1(o_s, wo_s, "nn", BF16, "mla_proj")

    merged = merge_fwd(gl, y_ssd, y_mla, p["gate_bias"], "merge")
    hmix, x2 = mm_resid(merged, W["w_out"], x1, p["mix_post_g"], 1.0, "mix_out")

    hq = rms_fwd(x2, p["xa_pre_g"], "xa_pre")
    mn = rms_fwd(mem2, p["mem_norm_g"], "mem_norm")
    xq = mm1(hq, W["w_xq"], "nn", BF16, "xq")
    xk = mm1(mn, W["w_xk"], "nn", BF16, "xk")
    xv = mm1(mn, W["w_xv"], "nn", BF16, "xv")
    xo, *arrived = xattn_fwd(xq, xk, xv, nseq, comm=stage.gather("xattn_fwd"))
    stage.gathered("xattn_fwd", arrived, W, p)
    ho, x3 = mm_resid(xo, W["w_xo"], x2, p["xa_post_g"], 1.0, "xo")

    dx4, ffn2, sq_cols = _ffn_fwd(x3, p["ffn2_pre_g"], p["ffn2_post_g"], W, p, "ffn2", stage, target=tgt.reshape(T, D))
    loss_row = (0.5 / D) * jnp.sum(sq_cols.reshape(-1, 128), axis=0, keepdims=True)

    gw, gs = {}, {}
    dx3, gs["ffn2_pre_g"], gs["ffn2_post_g"] = _ffn_bwd(
        dx4, ffn2, p["ffn2_pre_g"], p["ffn2_post_g"], W["ffn2_w_gate"], W["ffn2_w_up"], W["ffn2_w_down"], "ffn2", stage, gw)

    dho, gs["xa_post_g"] = resid_bwd(ho, p["xa_post_g"], dx3, 1.0, "xa_post_bwd")
    dxo = mm1(dho, W["w_xo"], "nt", BF16, "xo_bwd")
    gw["w_xo"] = _tn(xo, dho, "d_w_xo")
    dxq, dxk, dxv = xattn_bwd(xq, xk, xv, dxo, nseq)
    dx2, gs["xa_pre_g"] = mm_rms_bwd([(dxq, W["w_xq"], "nt")], x2, p["xa_pre_g"], "xq_bwd", resid=dx3)
    gw["w_xq"] = _tn(hq, dxq, "d_w_xq")
    dmn = mm([[(dxk, W["w_xk"], "nt"), (dxv, W["w_xv"], "nt")]], [F32], "xkv_bwd")[0]
    gw["w_xk"] = _tn(mn, dxk, "d_w_xk")
    gw["w_xv"] = _tn(mn, dxv, "d_w_xv")
    _, gs["mem_norm_g"] = rms_bwd(mem2, p["mem_norm_g"], dmn, "mem_norm_bwd", dx_dtype=BF16)

    dhmix, gs["mix_post_g"] = resid_bwd(hmix, p["mix_post_g"], dx2, 1.0, "mix_post_bwd")
    dmerged = mm1(dhmix, W["w_out"], "nt", BF16, "mix_out_bwd")
    gw["w_out"] = _tn(merged, dhmix, "d_w_out")
    dys, dym, dgl, gs["gate_bias"] = merge_bwd(gl, y_ssd, y_mla, dmerged, p["gate_bias"], "merge_bwd")

    unslot = lambda g, per: g.reshape(MLA_H, SLOT, -1)[:, :per].reshape(MLA_H * per, -1)
    do_s = mm1(dym, wo_s, "nt", BF16, "mla_proj_bwd")
    gw["w_mla_proj"] = unslot(_tn(o_s, dym, "d_w_mla_proj"), VD)
    dQc, dKc, dv_s, *sent = attn_slot_bwd(Qc, Kc, v_s, o_s, lse, do_s, nseq, comm=stage.scatter("attn_bwd", gw))
    stage.scattered("attn_bwd", sent)
    dq_s, dkn_s, dkr = rope_slot_bwd(dQc, dKc, rope_c, rope_s, "rope_bwd")
    dq_c, gs["q_norm_g"] = mm_rms_bwd([(dq_s, wq_s, "nn")], q_c, p["q_norm_g"], "uq_bwd", dx_dtype=BF16)
    gw["w_uq"] = unslot(_tn(dq_s, qn, "d_w_uq"), QK)
    dkv_c, gs["kv_norm_g"] = mm_rms_bwd([(dkn_s, wk_s, "nn"), (dv_s, wv_s, "nn")], kv_c, p["kv_norm_g"], "ukv_bwd", dx_dtype=BF16)
    gw["w_uk"] = unslot(_tn(dkn_s, kvn, "d_w_uk"), NOPE)
    gw["w_uv"] = unslot(_tn(dv_s, kvn, "d_w_uv"), VD)

    dyn = mm1(dys, W["w_ssd_proj"], "nt", BF16, "ssd_proj_bwd")
    gw["w_ssd_proj"] = _tn(yn, dys, "d_w_ssd_proj")
    dyc, dz, gs["ssd_norm_g"] = gated_norm_bwd(y_ssd_core, z, dyn, p["ssd_norm_g"], "ssd_norm_bwd")
    dxbc_act, ddtr, gs["dt_bias"], gs["a_log"], gs["d_skip"], *sent = ssd_bwd(
        xbc_act, dtkr, p["dt_bias"], p["a_log"], p["d_skip"], prev, dyc, nseq, comm=stage.scatter("ssd_bwd", gw))
    stage.scattered("ssd_bwd", sent)
    dxbc, gs["conv_w"], gs["conv_b"] = conv_bwd(xbc, p["conv_w"], p["conv_b"], dxbc_act, nseq)

    gw["w_in"] = jnp.concatenate([_tn(dz, hm, "d_w_in_z"), _tn(dxbc, hm, "d_w_in_xbc"), _tn(ddtr, hm, "d_w_in_dt")[:SSD_H],
                                  _tn(dq_c, hm, "d_w_in_q"), _tn(dkv_c, hm, "d_w_in_kv"), _tn(dkr, hm, "d_w_in_kr")[:ROPE],
                                  _tn(dgl, hm, "d_w_in_gate")], axis=0)
    dx1, gs["mix_pre_g"], *sent = mm_rms_bwd(
        [(dz, wt_z, "nn"), (dxbc, wt_xbc, "nn"), (ddtr, wt_dt, "nn"), (dq_c, wt_q, "nn"), (dkv_c, wt_kv, "nn"),
         (dkr, wt_kr, "nn"), (dgl, wt_gate, "nn")], x1, p["mix_pre_g"], "in_bwd", resid=dx2, comm=stage.scatter("in_bwd", gw))
    stage.scattered("in_bwd", sent)

    dx0, gs["ffn1_pre_g"], gs["ffn1_post_g"] = _ffn_bwd(
        dx1, ffn1, p["ffn1_pre_g"], p["ffn1_post_g"], W["ffn1_w_gate"], W["ffn1_w_up"], W["ffn1_w_down"], "ffn1", stage, gw)
    return loss_row, dx0.reshape(x.shape), gw, gs


def kernel(x, mem, positions, ffn1_pre_g, ffn1_w_gate, ffn1_w_up, ffn1_w_down, ffn1_post_g, mix_pre_g, w_in, conv_w, conv_b, dt_bias, a_log, d_skip, ssd_norm_g, w_ssd_proj, q_norm_g, w_uq, kv_norm_g, w_uk, w_uv, w_mla_proj, gate_bias, w_out, mix_post_g, xa_pre_g, mem_norm_g, w_xq, w_xk, w_xv, w_xo, xa_post_g, ffn2_pre_g, ffn2_w_gate, ffn2_w_up, ffn2_w_down, ffn2_post_g, loss_target, m_ffn1_pre_g, m_ffn1_w_gate, m_ffn1_w_up, m_ffn1_w_down, m_ffn1_post_g, m_mix_pre_g, m_w_in, m_conv_w, m_conv_b, m_dt_bias, m_a_log, m_d_skip, m_ssd_norm_g, m_w_ssd_proj, m_q_norm_g, m_w_uq, m_kv_norm_g, m_w_uk, m_w_uv, m_w_mla_proj, m_gate_bias, m_w_out, m_mix_post_g, m_xa_pre_g, m_mem_norm_g, m_w_xq, m_w_xk, m_w_xv, m_w_xo, m_xa_post_g, m_ffn2_pre_g, m_ffn2_w_gate, m_ffn2_w_up, m_ffn2_w_down, m_ffn2_post_g, v_ffn1_pre_g, v_ffn1_w_gate, v_ffn1_w_up, v_ffn1_w_down, v_ffn1_post_g, v_mix_pre_g, v_w_in, v_conv_w, v_conv_b, v_dt_bias, v_a_log, v_d_skip, v_ssd_norm_g, v_w_ssd_proj, v_q_norm_g, v_w_uq, v_kv_norm_g, v_w_uk, v_w_uv, v_w_mla_proj, v_gate_bias, v_w_out, v_mix_post_g, v_xa_pre_g, v_mem_norm_g, v_w_xq, v_w_xk, v_w_xv, v_w_xo, v_xa_post_g, v_ffn2_pre_g, v_ffn2_w_gate, v_ffn2_w_up, v_ffn2_w_down, v_ffn2_post_g):
    a = dict(locals())
    w = {n: a[n] for n in WEIGHTS}
    m = {n: a["m_" + n] for n in WEIGHTS}
    v = {n: a["v_" + n] for n in WEIGHTS}

    stage = Stage(w)
    W, p = {}, {n: w[n] for n in SMALL}
    loss_row, grad_x, gw, gs = _local_step(x, mem, positions, loss_target, W, p, stage)

    sm = _pack_small([gs[n] for n in SMALL], loss_row=loss_row, conv_w=gs["conv_w"])
    srecv, = run_comm(ScatterComm([[jnp.broadcast_to(sm[None], (N_DEV,) + sm.shape)]]), "exchange_small")
    s_rows = sum_slots(srecv, "sum_small", tr=sm.shape[0])
    token = stage.split_start(LAST_EXCHANGE, gw, after=s_rows)
    grads, delta, new_m, new_v = {}, {}, {}, {}
    raw_results = []

    def finish(n, buf, piece, token=None):
        col = KIND[n] == "col"
        turn = (lambda t: t.T) if col else (lambda t: t)
        K = w[n].shape[1]
        if col and buf.shape[2] != K:
            buf = buf.reshape(buf.shape[0], -1, K)
        res = adamw_from_slots(buf, piece, turn(w[n][0]), turn(m[n][0]), turn(v[n][0]), "adamw_" + n, token=token)
        raw_results.append(res[3])
        grads[n], delta[n], new_m[n], new_v[n] = [turn(r)[None] for r in res]

    finish(SCATTER_PLAN[EARLY_EXCHANGE][0][0], stage.split_wait(EARLY_EXCHANGE, after=[s_rows]), 0, token)
    parts = {}
    for tag, groups in SCATTER_PLAN.items():
        if tag in (EARLY_EXCHANGE, LAST_EXCHANGE):
            continue
        for names, buf in zip(groups, stage.recv[tag]):
            for piece, n in enumerate(names):
                if n in PARTS:
                    parts[n] = sum_slots(buf, "sum_" + n.replace("#", "_"), tr=buf.shape[1])
                else:
                    finish(n, buf, piece, token)
    for base in sorted({PARTS[pn][0] for pn in parts}):
        rows = jnp.concatenate([parts[pn] for pn in _parts_of(base, "#")], axis=0)
        finish(base, rows[None], 0, token)
    conv_w_full = p["conv_w"]
    small = adamw_small(s_rows, [w[n] for n in SMALL], [m[n] for n in SMALL], [v[n] for n in SMALL])
    for t, vals in zip((grads, delta, new_m, new_v), small):
        t.update(zip(SMALL, vals))
    r1 = sum(-(-w[n].shape[1] // 128) for n in SMALL)
    ncw = math.prod(conv_w_full.shape) // 128
    cw_grad_full = s_rows[r1:r1 + ncw].reshape(conv_w_full.shape)
    wsh = conv_w.shape[2]
    grads["conv_w"] = lax.dynamic_slice_in_dim(cw_grad_full, _dev_index() * wsh, wsh, axis=1)[None]
    loss = jnp.sum(s_rows[r1 + ncw])
    d_, m_, v_ = adamw(conv_w[0], grads["conv_w"][0], m["conv_w"][0], v["conv_w"][0], "adamw_conv_w")
    delta["conv_w"], new_m["conv_w"], new_v["conv_w"] = d_[None], m_[None], v_[None]
    finish(SCATTER_PLAN[LAST_EXCHANGE][0][0], stage.split_wait(LAST_EXCHANGE, after=raw_results + [small[3][0], v_]), 0)
    return (loss, grad_x, *[grads[n] for n in WEIGHTS], *[delta[n] for n in WEIGHTS],
            *[new_m[n] for n in WEIGHTS], *[new_v[n] for n in WEIGHTS])
```

```python
import functools
import math

import jax
import jax.numpy as jnp
from jax import lax
from jax.experimental import pallas as pl
from jax.experimental.pallas import tpu as pltpu

F32, BF16 = jnp.float32, jnp.bfloat16
MESH = pl.DeviceIdType.MESH
N_DEV = 8

D = 1024
DFF = 2816
SSD_H, SSD_P, SSD_G, SSD_N, SSD_L = 16, 64, 2, 128, 128
SSD_INNER = SSD_H * SSD_P
CONV_K, CONV_CH = 4, 1536
MLA_H, QR, KVR, NOPE, ROPE, VD = 16, 384, 256, 64, 32, 64
QK = NOPE + ROPE
ROPE_THETA = 10000.0
XA_H, XA_D = 4, 256
EPS = 1e-6
FFN_RES = 0.5
LR, B1, B2, AEPS, WD, STEP = 0.001, 0.9, 0.999, 1e-08, 0.01, 10

VMEM_LIMIT = 56 * 2**20


def _cp(*sem):
    return pltpu.CompilerParams(dimension_semantics=sem, vmem_limit_bytes=VMEM_LIMIT)


def _sigmoid(x):
    return 1.0 / (1.0 + jnp.exp(-x))


def _softplus(x):
    return jnp.where(x > 20.0, x, jnp.log(1.0 + jnp.exp(jnp.minimum(x, 20.0))))


def _dot(a, b, dims="nn"):
    ca = 0 if dims[0] == "t" else 1
    cb = 1 if dims[1] == "t" else 0
    return lax.dot_general(a.astype(BF16), b.astype(BF16), (((ca,), (cb,)), ((), ())), preferred_element_type=F32)


def _dot_sel(a, b, dims="nn", split="a", terms=3):
    r = (a if split == "a" else b).astype(F32)
    out = None
    for t in range(terms):
        piece = r.astype(BF16)
        if t + 1 < terms:
            r = r - piece.astype(F32)
        d = _dot(piece, b, dims) if split == "a" else _dot(a, piece, dims)
        out = d if out is None else out + d
    return out


def _ssd_common(dtr, dtb, alog):
    L = dtr.shape[0]
    dt = _softplus(dtr + dtb)
    a = -jnp.exp(alog)
    adt = dt * a
    r = lax.broadcasted_iota(jnp.int32, (L, L), 0)
    c = lax.broadcasted_iota(jnp.int32, (L, L), 1)
    lower = r >= c
    tri = lower.astype(F32)
    cs = _dot_sel(tri, adt, "nn", split="b")
    cs_t = _dot_sel(adt, tri, "tt")
    return dt, a, cs, cs_t, lower


def _head_expand():
    hh = lax.broadcasted_iota(jnp.int32, (SSD_H, SSD_INNER), 0)
    jj = lax.broadcasted_iota(jnp.int32, (SSD_H, SSD_INNER), 1)
    return ((jj >= hh * SSD_P) & (jj < hh * SSD_P + SSD_P)).astype(F32)


def _head_reduce():
    hh = lax.broadcasted_iota(jnp.int32, (SSD_INNER, SSD_H), 1)
    jj = lax.broadcasted_iota(jnp.int32, (SSD_INNER, SSD_H), 0)
    return ((jj >= hh * SSD_P) & (jj < hh * SSD_P + SSD_P)).astype(F32)


def ssd_fwd(xbc, dtr, dtb, alog, dsk, nseq, comm=None):
    T = xbc.shape[0]
    S = T // nseq
    C = S // SSD_L
    L = SSD_L
    NP = SSD_H // 2

    def body(x_ref, b_ref, c_ref, dtr_ref, dtb_ref, alog_ref, dsk_ref, y_ref, prev_ref, st_ref):
        ci = pl.program_id(1)

        @pl.when(ci == 0)
        def _():
            st_ref[...] = jnp.zeros_like(st_ref)

        dt, a, cs, cs_t, lower = _ssd_common(dtr_ref[:, 0:SSD_H], dtb_ref[...], alog_ref[...])
        E = _head_expand()
        X = x_ref[...].astype(F32)
        dt_e = _dot_sel(dt, E)
        cs_e = _dot_sel(cs, E)
        csl_e = cs_e[L - 1:L, :]
        Xd = X * dt_e
        Xf = Xd * jnp.exp(csl_e - cs_e)
        e_e = jnp.exp(cs_e)
        skip = _dot_sel(dsk_ref[...], E) * X
        lane = lax.broadcasted_iota(jnp.int32, (1, 2 * SSD_P), 1)
        rowp = lax.broadcasted_iota(jnp.int32, (2 * SSD_P, 1), 0)
        for g in range(SSD_G):
            Bg = b_ref[:, g * SSD_N:(g + 1) * SSD_N]
            Cg = c_ref[:, g * SSD_N:(g + 1) * SSD_N]
            cb = _dot(Cg, Bg, "nt")
            for pp in range(NP // SSD_G):
                p = g * (NP // SSD_G) + pp
                sl = slice(p * 2 * SSD_P, (p + 1) * 2 * SSD_P)
                Xd_p = Xd[:, sl]
                yd = jnp.zeros((L, 2 * SSD_P), F32)
                for q in range(2):
                    h = 2 * p + q
                    m = jnp.where(lower, jnp.exp(jnp.minimum(cs[:, h:h + 1] - cs_t[h:h + 1, :], 0.0)), 0.0)
                    mask = (lane >= q * SSD_P) & (lane < (q + 1) * SSD_P)
                    yd = yd + _dot(cb * m, jnp.where(mask, Xd_p, 0.0))
                S0 = st_ref[p]
                prev_ref[0, 0, p] = S0
                z = _dot(Cg, S0, "nt")
                y_ref[:, sl] = (skip[:, sl] + yd + z * e_e[:, sl]).astype(y_ref.dtype)
                h0 = 2 * p
                dec = jnp.where(rowp < SSD_P, jnp.exp(cs[L - 1:L, h0:h0 + 1]), jnp.exp(cs[L - 1:L, h0 + 1:h0 + 2]))
                st_ref[p] = S0 * dec + _dot(Xf[:, sl], Bg, "tn")

    row = lambda b, c: (b * C + c, 0)
    small = pl.BlockSpec((1, SSD_H), lambda b, c: (0, 0))
    return _call_with_comm(
        body, (nseq, C), "ssd_fwd",
        [pl.BlockSpec((L, SSD_INNER), row),
         pl.BlockSpec((L, SSD_G * SSD_N), lambda b, c: (b * C + c, SSD_INNER // (SSD_G * SSD_N))),
         pl.BlockSpec((L, SSD_G * SSD_N), lambda b, c: (b * C + c, SSD_INNER // (SSD_G * SSD_N) + 1)),
         pl.BlockSpec((L, 128), row), small, small, small],
        [xbc, xbc, xbc, dtr, dtb, alog, dsk],
        [pl.BlockSpec((L, SSD_INNER), row), pl.BlockSpec((1, 1, NP, 2 * SSD_P, SSD_N), lambda b, c: (b, c, 0, 0, 0))],
        [jax.ShapeDtypeStruct((T, SSD_INNER), BF16), jax.ShapeDtypeStruct((nseq, C, NP, 2 * SSD_P, SSD_N), F32)],
        comm, scratch=[pltpu.VMEM((NP, 2 * SSD_P, SSD_N), F32)], sem=("parallel", "arbitrary"))


def ssd_bwd(xbc, dtr, dtb, alog, dsk, prev, dy, nseq, comm=None):
    T = xbc.shape[0]
    S = T // nseq
    C = S // SSD_L
    L = SSD_L
    NP = SSD_H // 2

    def body(x_ref, b_ref, c_ref, dtr_ref, dtb_ref, alog_ref, dsk_ref, prev_ref, dy_ref,
             dxbc_ref, ddtr_ref, ddtb_ref, dalog_ref, ddsk_ref, ds_ref, stg_ref):
        bi = pl.program_id(0)
        ci = pl.program_id(1)

        @pl.when(ci == 0)
        def _():
            ds_ref[...] = jnp.zeros_like(ds_ref)

        @pl.when((ci == 0) & (bi == 0))
        def _():
            ddtb_ref[...] = jnp.zeros_like(ddtb_ref)
            dalog_ref[...] = jnp.zeros_like(dalog_ref)
            ddsk_ref[...] = jnp.zeros_like(ddsk_ref)

        dtr = dtr_ref[:, 0:SSD_H]
        dtb = dtb_ref[...]
        dt, a, cs, cs_t, lower = _ssd_common(dtr, dtb, alog_ref[...])
        upper = lax.broadcasted_iota(jnp.int32, (L, L), 1) >= lax.broadcasted_iota(jnp.int32, (L, L), 0)
        E = _head_expand()
        ET = _head_reduce()
        X = x_ref[...].astype(F32)
        dY = dy_ref[...].astype(F32)
        dt_e = _dot_sel(dt, E)
        cs_e = _dot_sel(cs, E)
        csl_e = cs_e[L - 1:L, :]
        f_e = jnp.exp(csl_e - cs_e)
        e_e = jnp.exp(cs_e)
        dsk_e = _dot_sel(dsk_ref[...], E)
        Xd = X * dt_e
        Xf = Xd * f_e
        lane = lax.broadcasted_iota(jnp.int32, (1, 2 * SSD_P), 1)
        rowp = lax.broadcasted_iota(jnp.int32, (2 * SSD_P, 1), 0)
        hsel = lax.broadcasted_iota(jnp.int32, (1, SSD_H), 1)
        dcs = jnp.zeros((L, SSD_H), F32)
        dcsl = jnp.zeros((1, SSD_H), F32)
        for g in range(SSD_G):
            Bg = b_ref[:, g * SSD_N:(g + 1) * SSD_N]
            Cg = c_ref[:, g * SSD_N:(g + 1) * SSD_N]
            cb = _dot(Cg, Bg, "nt")
            cbt = _dot(Bg, Cg, "nt")
            dB = jnp.zeros((L, SSD_N), F32)
            dC = jnp.zeros((L, SSD_N), F32)
            for pp in range(NP // SSD_G):
                p = g * (NP // SSD_G) + pp
                sl = slice(p * 2 * SSD_P, (p + 1) * 2 * SSD_P)
                Xd_p = Xd[:, sl]
                dY_p = dY[:, sl]
                dXd_p = jnp.zeros((L, 2 * SSD_P), F32)
                for q in range(2):
                    h = 2 * p + q
                    mask = (lane >= q * SSD_P) & (lane < (q + 1) * SSD_P)
                    col = cs[:, h:h + 1]
                    rw = cs_t[h:h + 1, :]
                    m = jnp.where(lower, jnp.exp(jnp.minimum(col - rw, 0.0)), 0.0)
                    mt = jnp.where(upper, jnp.exp(jnp.minimum(rw - col, 0.0)), 0.0)
                    dYm = jnp.where(mask, dY_p, 0.0)
                    dW = _dot(dYm, Xd_p, "nt")
                    dWt = _dot(Xd_p, dYm, "nt")
                    w = cb * m
                    wt = cbt * mt
                    dC = dC + _dot(dW * m, Bg)
                    dB = dB + _dot(dWt * mt, Cg)
                    dXd_p = dXd_p + jnp.where(mask, _dot(wt, dY_p), 0.0)
                    qcol = jnp.sum(dW * w, axis=1, keepdims=True) - jnp.sum(dWt * wt, axis=1, keepdims=True)
                    dcs = dcs + qcol * (hsel == h).astype(F32)
                S0 = prev_ref[0, 0, p]
                dSn = ds_ref[p]
                dZ = dY_p * e_e[:, sl]
                dC = dC + _dot(dZ, S0)
                h0 = 2 * p
                el0 = jnp.exp(cs[L - 1:L, h0:h0 + 1])
                el1 = jnp.exp(cs[L - 1:L, h0 + 1:h0 + 2])
                dec = jnp.where(rowp < SSD_P, el0, el1)
                ds_ref[p] = dSn * dec + _dot(dZ, Cg, "tn")
                dXf_p = _dot(Bg, dSn, "nt")
                dB = dB + _dot(Xf[:, sl], dSn)
                rs = jnp.sum(dSn * S0, axis=1, keepdims=True)
                s0 = jnp.sum(jnp.where(rowp < SSD_P, rs, 0.0), axis=0, keepdims=True) * el0
                s1 = jnp.sum(jnp.where(rowp >= SSD_P, rs, 0.0), axis=0, keepdims=True) * el1
                dcsl = dcsl + s0 * (hsel == h0).astype(F32) + s1 * (hsel == h0 + 1).astype(F32)
                y_off = _dot(Cg, S0, "nt") * e_e[:, sl]
                t1 = dY_p * y_off - dXf_p * Xf[:, sl]
                r1 = jnp.where(lane < SSD_P, t1, 0.0)
                c0 = jnp.sum(r1, axis=1, keepdims=True)
                c1 = jnp.sum(t1 - r1, axis=1, keepdims=True)
                dcs = dcs + c0 * (hsel == h0).astype(F32) + c1 * (hsel == h0 + 1).astype(F32)
                t2 = dXf_p * Xf[:, sl]
                r2 = jnp.where(lane < SSD_P, t2, 0.0)
                dcsl = dcsl + jnp.sum(r2, keepdims=True) * (hsel == h0).astype(F32) \
                    + jnp.sum(t2 - r2, keepdims=True) * (hsel == h0 + 1).astype(F32)
                stg_ref[:, sl] = dXd_p + dXf_p * f_e[:, sl]
            dxbc_ref[:, SSD_INNER + g * SSD_N:SSD_INNER + (g + 1) * SSD_N] = dB.astype(dxbc_ref.dtype)
            dxbc_ref[:, SSD_INNER + (SSD_G + g) * SSD_N:SSD_INNER + (SSD_G + g + 1) * SSD_N] = dC.astype(dxbc_ref.dtype)
        dXd = stg_ref[...]
        dxbc_ref[:, 0:SSD_INNER] = (dXd * dt_e + dsk_e * dY).astype(dxbc_ref.dtype)
        rowl = lax.broadcasted_iota(jnp.int32, (L, 1), 0)
        dcs = dcs + jnp.where(rowl == L - 1, dcsl, 0.0)
        dalpha = _dot_sel(upper.astype(F32), dcs, split="b")
        ddt = _dot_sel(dXd * X, ET, terms=2) + dalpha * a
        dalog_ref[...] += jnp.sum(dalpha * dt, axis=0, keepdims=True) * a
        ddtr = ddt * _sigmoid(dtr + dtb)
        spread = (lax.broadcasted_iota(jnp.int32, (SSD_H, 128), 0) == lax.broadcasted_iota(jnp.int32, (SSD_H, 128), 1)).astype(F32)
        ddtr_ref[...] = _dot(ddtr, spread).astype(ddtr_ref.dtype)
        ddtb_ref[...] += jnp.sum(ddtr, axis=0, keepdims=True)
        ddsk_ref[...] += jnp.sum(_dot_sel(dY * X, ET, terms=2), axis=0, keepdims=True)

    rowr = lambda b, c: (b * C + (C - 1 - c), 0)
    small = pl.BlockSpec((1, SSD_H), lambda b, c: (0, 0))
    return _call_with_comm(
        body, (nseq, C), "ssd_bwd",
        [pl.BlockSpec((L, SSD_INNER), rowr),
         pl.BlockSpec((L, SSD_G * SSD_N), lambda b, c: (b * C + (C - 1 - c), SSD_INNER // (SSD_G * SSD_N))),
         pl.BlockSpec((L, SSD_G * SSD_N), lambda b, c: (b * C + (C - 1 - c), SSD_INNER // (SSD_G * SSD_N) + 1)),
         pl.BlockSpec((L, 128), rowr), small, small, small,
         pl.BlockSpec((1, 1, NP, 2 * SSD_P, SSD_N), lambda b, c: (b, C - 1 - c, 0, 0, 0)),
         pl.BlockSpec((L, SSD_INNER), rowr)],
        [xbc, xbc, xbc, dtr, dtb, alog, dsk, prev, dy],
        [pl.BlockSpec((L, CONV_CH), rowr), pl.BlockSpec((L, 128), rowr), small, small, small],
        [jax.ShapeDtypeStruct((T, CONV_CH), BF16), jax.ShapeDtypeStruct((T, 128), BF16),
         jax.ShapeDtypeStruct((1, SSD_H), F32), jax.ShapeDtypeStruct((1, SSD_H), F32), jax.ShapeDtypeStruct((1, SSD_H), F32)],
        comm, scratch=[pltpu.VMEM((NP, 2 * SSD_P, SSD_N), F32), pltpu.VMEM((L, SSD_INNER), F32)], sem=("arbitrary", "arbitrary"))


SLOT = 128
ATT_T = 512
ATT_HP = 1
LOG2E = math.log2(math.e)
Q_SCALE = QK ** -0.5 * LOG2E


def _col_to_row(col):
    n = col.shape[0]
    eye = lax.broadcasted_iota(jnp.int32, (n, n), 0) == lax.broadcasted_iota(jnp.int32, (n, n), 1)
    return jnp.sum(jnp.where(eye, col, 0.0), axis=0, keepdims=True)


def attn_slot_fwd(q, k, v, nseq, comm=None):
    T = q.shape[0]
    S = T // nseq
    t = min(ATT_T, S)
    nb = S // t
    cols = [slice(h * SLOT, (h + 1) * SLOT) for h in range(ATT_HP)]

    def body(q_ref, k_ref, v_ref, o_ref, lse_ref):
        causal = lax.broadcasted_iota(jnp.int32, (t, t), 1) <= lax.broadcasted_iota(jnp.int32, (t, t), 0)
        for qi in range(nb):
            rows = slice(qi * t, (qi + 1) * t)
            state = [None] * ATT_HP
            for kj in range(qi + 1):
                keys = slice(kj * t, (kj + 1) * t)
                for h, c in enumerate(cols):
                    s = _dot(q_ref[rows, c], k_ref[keys, c], "nt")
                    if kj == qi:
                        s = jnp.where(causal, s, -1e30)
                    bm = jnp.max(s, axis=1, keepdims=True)
                    if kj == 0:
                        p = jnp.exp2(s - bm)
                        state[h] = (bm, jnp.sum(p, axis=1, keepdims=True), _dot(p, v_ref[keys, c]))
                    else:
                        m, l, acc = state[h]
                        m_new = jnp.maximum(m, bm)
                        corr = jnp.exp2(m - m_new)
                        p = jnp.exp2(s - m_new)
                        state[h] = (m_new, l * corr + jnp.sum(p, axis=1, keepdims=True), acc * corr + _dot(p, v_ref[keys, c]))
            for h, c in enumerate(cols):
                m, l, acc = state[h]
                o_ref[rows, c] = (acc / l).astype(o_ref.dtype)
                lse_ref[0, h, :, rows] = _col_to_row(m + jnp.log2(l))

    blk = pl.BlockSpec((S, ATT_HP * SLOT), lambda b, h: (b, h))
    return _call_with_comm(
        body, (nseq, MLA_H // ATT_HP), "attn_fwd", [blk, blk, blk], [q, k, v],
        [blk, pl.BlockSpec((1, ATT_HP, 1, S), lambda b, h: (b, h, 0, 0))],
        [jax.ShapeDtypeStruct((T, MLA_H * SLOT), BF16), jax.ShapeDtypeStruct((nseq, MLA_H, 1, S), F32)], comm)


def attn_slot_bwd(q, k, v, o, lse, do, nseq, comm=None):
    T = q.shape[0]
    S = T // nseq
    t = min(ATT_T, S)
    nb = S // t
    scale = QK ** -0.5
    cols = [slice(h * SLOT, (h + 1) * SLOT) for h in range(ATT_HP)]

    def body(q_ref, k_ref, v_ref, o_ref, lse_ref, do_ref, dq_ref, dk_ref, dv_ref, dqa_ref):
        causal_t = lax.broadcasted_iota(jnp.int32, (t, t), 0) <= lax.broadcasted_iota(jnp.int32, (t, t), 1)
        ones = jnp.ones((8, SLOT), F32)
        delta = {}
        for qi in range(nb):
            sl = slice(qi * t, (qi + 1) * t)
            for h, c in enumerate(cols):
                prod = do_ref[sl, c].astype(F32) * o_ref[sl, c].astype(F32)
                delta[h, qi] = _dot_sel(ones, prod, "nt", split="b", terms=2)[0:1, :]
        for kj in range(nb):
            ks = slice(kj * t, (kj + 1) * t)
            dk = [None] * ATT_HP
            dv = [None] * ATT_HP
            for qi in range(kj, nb):
                sl = slice(qi * t, (qi + 1) * t)
                for h, c in enumerate(cols):
                    kb, vb, qb, dob = k_ref[ks, c], v_ref[ks, c], q_ref[sl, c], do_ref[sl, c]
                    st = _dot(kb, qb, "nt")
                    pt = jnp.exp2(st - lse_ref[0, h, :, sl])
                    if qi == kj:
                        pt = jnp.where(causal_t, pt, 0.0)
                    dpt = _dot(vb, dob, "nt")
                    dst = (pt * (dpt - delta[h, qi])).astype(BF16)
                    dvc = _dot(pt, dob)
                    dkc = _dot(dst, qb) * (1.0 / LOG2E)
                    dv[h] = dvc if dv[h] is None else dv[h] + dvc
                    dk[h] = dkc if dk[h] is None else dk[h] + dkc
                    dqc = _dot(dst, kb, "tn") * scale
                    if kj > 0:
                        dqc = dqc + dqa_ref[sl, c]
                    if qi == kj:
                        dq_ref[sl, c] = dqc.astype(dq_ref.dtype)
                    else:
                        dqa_ref[sl, c] = dqc
            for h, c in enumerate(cols):
                dk_ref[ks, c] = dk[h].astype(dk_ref.dtype)
                dv_ref[ks, c] = dv[h].astype(dv_ref.dtype)

    blk = pl.BlockSpec((S, ATT_HP * SLOT), lambda b, h: (b, h))
    lse_spec = pl.BlockSpec((1, ATT_HP, 1, S), lambda b, h: (b, h, 0, 0))
    W = MLA_H * SLOT
    return _call_with_comm(
        body, (nseq, MLA_H // ATT_HP), "attn_bwd", [blk, blk, blk, blk, lse_spec, blk], [q, k, v, o, lse, do], [blk, blk, blk],
        [jax.ShapeDtypeStruct((T, W), BF16)] * 3, comm, scratch=[pltpu.VMEM((S, ATT_HP * SLOT), F32)])


def _rope_coeffs(pos, inv):
    half = ROPE // 2
    ang = pos * inv
    lane = lax.broadcasted_iota(jnp.int32, (1, SLOT), 1)
    sn = jnp.sin(ang)
    C = jnp.where(lane < NOPE, 1.0, jnp.where(lane < QK, jnp.cos(ang), 0.0))
    Sg = jnp.where((lane >= NOPE) & (lane < NOPE + half), -sn, jnp.where((lane >= NOPE + half) & (lane < QK), sn, 0.0))
    return C, Sg


def _rope_inputs(positions):
    half = ROPE // 2
    inv = ROPE_THETA ** (-jnp.arange(0, ROPE, 2, dtype=F32) / ROPE)
    row = jnp.zeros((1, SLOT), F32).at[0, NOPE:NOPE + half].set(inv).at[0, NOPE + half:QK].set(inv)
    return positions.astype(F32).reshape(-1, 1), row


def _place_k_rope(kr_lanes):
    r = lax.broadcasted_iota(jnp.int32, (SLOT, SLOT), 0)
    c = lax.broadcasted_iota(jnp.int32, (SLOT, SLOT), 1)
    return _dot_sel(kr_lanes, ((c == r + NOPE) & (r < ROPE)).astype(F32))


def rope_table(pos, inv):
    return rowwise(_rope_coeffs, [pos], [inv], [(SLOT, F32), (SLOT, F32)], [], "rope_table")


def rope_q_epilogue(accs, ex):
    C, Sg = ex[0], ex[1]
    reps = accs[0].shape[1] // SLOT
    return ((accs[0] * jnp.tile(C, (1, reps)) + _rope_swap(accs[0]) * jnp.tile(Sg, (1, reps))) * Q_SCALE,)


def rope_k_epilogue(accs, ex):
    C, Sg = ex[0], ex[1]
    kr = _place_k_rope(ex[2][:, SLOT:2 * SLOT])
    kr = kr * C + _rope_swap(kr) * Sg
    return (accs[0] + jnp.tile(kr, (1, accs[0].shape[1] // SLOT)),)


def _rope_swap(x):
    W = x.shape[1]
    half = ROPE // 2
    lane = lax.broadcasted_iota(jnp.int32, (1, W), 1) & (SLOT - 1)
    up = pltpu.roll(x, W - half, axis=1)
    dn = pltpu.roll(x, half, axis=1)
    return jnp.where((lane >= NOPE) & (lane < NOPE + half), up, jnp.where((lane >= NOPE + half) & (lane < QK), dn, 0.0))


def rope_slot_bwd(dq, dk, C, Sg, name):
    def fn(dqv, dkv, C, Sg):
        ct, stl = jnp.tile(C, (1, MLA_H)), jnp.tile(Sg, (1, MLA_H))
        dqo = dqv * ct - _rope_swap(dqv) * stl
        tot = dkv[:, 0:SLOT]
        for h in range(1, MLA_H):
            tot = tot + dkv[:, h * SLOT:(h + 1) * SLOT]
        u = tot * C - _rope_swap(tot) * Sg
        r = lax.broadcasted_iota(jnp.int32, (SLOT, SLOT), 0)
        c = lax.broadcasted_iota(jnp.int32, (SLOT, SLOT), 1)
        unplace = ((r == c + NOPE) & (c < ROPE)).astype(F32)
        return dqo, dkv, _dot_sel(u, unplace, terms=2)
    W = MLA_H * SLOT
    return rowwise(fn, [dq, dk, C, Sg], [], [(W, BF16), (W, BF16), (SLOT, BF16)], [], name)


XA_BLK = 512


def xattn_fwd(q, k, v, nseq, comm=None):
    T = q.shape[0]
    S = T // nseq
    M = k.shape[0] // nseq
    tq = min(XA_BLK, S)
    nq = S // tq
    scale = XA_D ** -0.5

    def body(q_ref, k_ref, v_ref, o_ref):
        s = _dot(q_ref[...], k_ref[...], "nt") * scale
        p = jnp.exp(s - jnp.max(s, axis=1, keepdims=True))
        p = p / jnp.sum(p, axis=1, keepdims=True)
        o_ref[...] = _dot(p, v_ref[...]).astype(o_ref.dtype)

    qs = pl.BlockSpec((tq, XA_D), lambda b, h, i: (b * nq + i, h))
    ks = pl.BlockSpec((M, XA_D), lambda b, h, i: (b, h))
    return _call_with_comm(body, (nseq, XA_H, nq), "xattn_fwd", [qs, ks, ks], [q, k, v], [qs],
                           [jax.ShapeDtypeStruct((T, XA_H * XA_D), BF16)], comm)


def xattn_bwd(q, k, v, do, nseq):
    T = q.shape[0]
    S = T // nseq
    M = k.shape[0] // nseq
    tq = min(XA_BLK, S)
    nq = S // tq
    scale = XA_D ** -0.5

    def body(q_ref, k_ref, v_ref, do_ref, dq_ref, dk_ref, dv_ref):
        @pl.when(pl.program_id(2) == 0)
        def _():
            dk_ref[...] = jnp.zeros_like(dk_ref)
            dv_ref[...] = jnp.zeros_like(dv_ref)

        qb, kb, vb, dob = q_ref[...], k_ref[...], v_ref[...], do_ref[...]
        s = _dot(qb, kb, "nt") * scale
        p = jnp.exp(s - jnp.max(s, axis=1, keepdims=True))
        p = p / jnp.sum(p, axis=1, keepdims=True)
        dp = _dot(dob, vb, "nt")
        ds = p * (dp - jnp.sum(dp * p, axis=1, keepdims=True)) * scale
        dq_ref[...] = _dot(ds, kb).astype(dq_ref.dtype)
        dk_ref[...] += _dot(ds, qb, "tn")
        dv_ref[...] += _dot(p, dob, "tn")

    qs = pl.BlockSpec((tq, XA_D), lambda b, h, i: (b * nq + i, h))
    ks = pl.BlockSpec((M, XA_D), lambda b, h, i: (b, h))
    return pl.pallas_call(
        body, grid=(nseq, XA_H, nq), name="xattn_bwd", in_specs=[qs, ks, ks, qs], out_specs=[qs, ks, ks],
        out_shape=[jax.ShapeDtypeStruct((T, XA_H * XA_D), BF16), jax.ShapeDtypeStruct(k.shape, F32),
                   jax.ShapeDtypeStruct(k.shape, F32)],
        compiler_params=_cp("parallel", "parallel", "arbitrary"),
    )(q, k, v, do)


CONV_BLK = 256


def _shift_down(x, s, rows):
    if s == 0:
        return x
    return jnp.where(rows >= s, pltpu.roll(x, s, axis=0), 0.0)


def _shift_up(x, s, rows):
    if s == 0:
        return x
    S = x.shape[0]
    return jnp.where(rows < S - s, pltpu.roll(x, S - s, axis=0), 0.0)


def conv_fwd(x, w, b, nseq):
    T, CH = x.shape
    S = T // nseq

    def body(x_ref, w_ref, b_ref, o_ref):
        xv = x_ref[...].astype(F32)
        rows = lax.broadcasted_iota(jnp.int32, (S, 1), 0)
        c = jnp.zeros_like(xv) + b_ref[...]
        for kk in range(CONV_K):
            c = c + w_ref[kk:kk + 1, :] * _shift_down(xv, CONV_K - 1 - kk, rows)
        o_ref[...] = (c * _sigmoid(c)).astype(o_ref.dtype)

    xs = pl.BlockSpec((S, CONV_BLK), lambda j, bb: (bb, j))
    return pl.pallas_call(
        body, grid=(CH // CONV_BLK, nseq), name="conv_fwd",
        in_specs=[xs, pl.BlockSpec((CONV_K, CONV_BLK), lambda j, bb: (0, j)), pl.BlockSpec((1, CONV_BLK), lambda j, bb: (0, j))],
        out_specs=xs, out_shape=jax.ShapeDtypeStruct((T, CH), BF16),
        compiler_params=_cp("parallel", "parallel"),
    )(x, w, b)


def conv_bwd(x, w, b, dout, nseq):
    T, CH = x.shape
    S = T // nseq

    def body(x_ref, w_ref, b_ref, do_ref, dx_ref, dw_ref, db_ref):
        @pl.when(pl.program_id(1) == 0)
        def _():
            dw_ref[...] = jnp.zeros_like(dw_ref)
            db_ref[...] = jnp.zeros_like(db_ref)

        xv = x_ref[...].astype(F32)
        rows = lax.broadcasted_iota(jnp.int32, (S, 1), 0)
        c = jnp.zeros_like(xv) + b_ref[...]
        sh = [_shift_down(xv, CONV_K - 1 - kk, rows) for kk in range(CONV_K)]
        for kk in range(CONV_K):
            c = c + w_ref[kk:kk + 1, :] * sh[kk]
        sg = _sigmoid(c)
        dc = do_ref[...].astype(F32) * sg * (1.0 + c * (1.0 - sg))
        dx = jnp.zeros_like(xv)
        for kk in range(CONV_K):
            dx = dx + w_ref[kk:kk + 1, :] * _shift_up(dc, CONV_K - 1 - kk, rows)
            dw_ref[kk:kk + 1, :] += jnp.sum(dc * sh[kk], axis=0, keepdims=True)
        dx_ref[...] = dx.astype(dx_ref.dtype)
        db_ref[...] += jnp.sum(dc, axis=0, keepdims=True)

    xs = pl.BlockSpec((S, CONV_BLK), lambda j, bb: (bb, j))
    ws = pl.BlockSpec((CONV_K, CONV_BLK), lambda j, bb: (0, j))
    bs = pl.BlockSpec((1, CONV_BLK), lambda j, bb: (0, j))
    return pl.pallas_call(
        body, grid=(CH // CONV_BLK, nseq), name="conv_bwd",
        in_specs=[xs, ws, bs, xs], out_specs=[xs, ws, bs],
        out_shape=[jax.ShapeDtypeStruct((T, CH), BF16), jax.ShapeDtypeStruct((CONV_K, CH), F32),
                   jax.ShapeDtypeStruct((1, CH), F32)],
        compiler_params=_cp("parallel", "arbitrary"),
    )(x, w, b, dout)


def _dims(a, b, mode):
    M = a.shape[1] if mode[0] == "t" else a.shape[0]
    K = a.shape[0] if mode[0] == "t" else a.shape[1]
    N = b.shape[0] if mode[1] == "t" else b.shape[1]
    return M, K, N


def _tile(dim, prefs):
    for p in prefs:
        if dim % p == 0:
            return p
    return dim


def mm(groups, out_dtypes, name, tm=None, tn=None, tk=None, epi=None, extras=(), comm=None, sub=1, n_sum=0):
    a0, b0, m0 = groups[0][0]
    M, K0, N = _dims(a0, b0, m0)
    tm = tm or _tile(M, (1024, 512, 256, 128))
    tn = tn or _tile(N, (1024, 512, 256, 128))
    flat = [p for g in groups for p in g]
    nk = 1 if tk is None else K0 // tk
    in_specs, args = [], []
    for a, b, mode in flat:
        _, K, _ = _dims(a, b, mode)
        kb = K if tk is None else tk
        in_specs.append(pl.BlockSpec((kb, tm), lambda i, j, k: (k, i)) if mode[0] == "t"
                        else pl.BlockSpec((tm, kb), lambda i, j, k: (i, k)))
        in_specs.append(pl.BlockSpec((tn, kb), lambda i, j, k: (j, k)) if mode[1] == "t"
                        else pl.BlockSpec((kb, tn), lambda i, j, k: (k, j)))
        args += [a, b]
    kinds = []
    for e in extras:
        kind, e = e if isinstance(e, tuple) else ("vec" if e.shape[0] == 1 and M != 1 else "tile", e)
        in_specs.append({"tile": pl.BlockSpec((tm, tn), lambda i, j, k: (i, j)),
                         "vec": pl.BlockSpec((1, tn), lambda i, j, k: (0, j)),
                         "rows": pl.BlockSpec((tm, e.shape[1]), lambda i, j, k: (i, 0)),
                         "whole": pl.BlockSpec(e.shape, lambda i, j, k: (0, 0))}[kind])
        kinds.append(kind)
        args.append(e)
    n_in = len(args)
    n_main = len(out_dtypes)
    n_out = n_main + n_sum
    assert n_sum == 0 or (tn == N and tk is None)
    ng = len(groups)
    sizes = [len(g) for g in groups]

    def body(*refs):
        ins, outs, accs = refs[:n_in], refs[n_in:n_in + n_out], refs[n_in + n_out:]
        kk = pl.program_id(2)

        def dots(rs):
            vals, pos = [], 0
            for gi in range(ng):
                acc = None
                for _ in range(sizes[gi]):
                    mode = flat[pos // 2][2]
                    av = ins[pos][:, rs] if mode[0] == "t" else ins[pos][rs, :]
                    d = _dot(av, ins[pos + 1][...], mode)
                    acc = d if acc is None else acc + d
                    pos += 2
                vals.append(acc)
            return vals

        def finish(accv, rs, first_chunk=True):
            ex = [(r[rs, :] if kind in ("tile", "rows") else r[...]).astype(F32) for kind, r in zip(kinds, ins[2 * len(flat):])]
            res = epi(accv, ex) if epi is not None else tuple(accv)
            for o, r in zip(outs[:n_main], res[:n_main]):
                o[rs, :] = r.astype(o.dtype)
            for o, r in zip(outs[n_main:], res[n_main:]):
                if first_chunk:
                    @pl.when(pl.program_id(0) == 0)
                    def _():
                        o[...] = r

                    @pl.when(pl.program_id(0) > 0)
                    def _():
                        o[...] += r
                else:
                    o[...] += r

        if nk == 1:
            for r in range(sub):
                rs = slice(r * (tm // sub), (r + 1) * (tm // sub))
                finish(dots(rs), rs, r == 0)
        else:
            vals = dots(slice(0, tm))
            finish = functools.partial(finish, rs=slice(0, tm))
            @pl.when(kk == 0)
            def _():
                for ar, vv in zip(accs, vals):
                    ar[...] = vv

            @pl.when(kk > 0)
            def _():
                for ar, vv in zip(accs, vals):
                    ar[...] += vv

            @pl.when(kk == nk - 1)
            def _():
                finish([ar[...] for ar in accs])

    grid = (M // tm, N // tn, nk)
    out_specs = [pl.BlockSpec((tm, tn), lambda i, j, k: (i, j)) for _ in out_dtypes] \
        + [pl.BlockSpec((1, tn), lambda i, j, k: (0, j))] * n_sum
    out_shape = [jax.ShapeDtypeStruct((M, N), dt) for dt in out_dtypes] + [jax.ShapeDtypeStruct((1, N), F32)] * n_sum
    scratch = [pltpu.VMEM((tm, tn), F32) for _ in range(ng if nk > 1 else 0)]
    sem = ("arbitrary" if n_sum else "parallel", "parallel", "arbitrary")
    if comm is not None:
        body = _attach(comm, body, n_in, n_out, *_grid_ends(grid))
        in_specs, args = in_specs + [HBM_SPEC] * len(comm.inputs), args + comm.inputs
        out_specs, out_shape = out_specs + [HBM_SPEC] * len(comm.out_shapes), out_shape + comm.out_shapes
        scratch, sem = scratch + comm.sems, ("arbitrary",) * 3
    return pl.pallas_call(body, grid=grid, name=name, in_specs=in_specs, out_specs=out_specs, out_shape=out_shape,
                          scratch_shapes=scratch, compiler_params=_cp(*sem))(*args)


def mm1(a, b, mode, out_dtype, name, **kw):
    return mm([[(a, b, mode)]], [out_dtype], name, **kw)[0]


ROW_BLK = 512


def rowwise(fn, rows, consts, outs, accs, name, tb=ROW_BLK, comm=None):
    rows = [r if isinstance(r, tuple) else (r, r.shape[1], 0) for r in rows]
    T = rows[0][0].shape[0]
    tb = min(tb, T)
    n_r, n_c, n_o, n_a = len(rows), len(consts), len(outs), len(accs)

    def body(*refs):
        vals = [r[...].astype(F32) for r in refs[:n_r + n_c]]
        res = fn(*vals)
        o_refs = refs[n_r + n_c:n_r + n_c + n_o]
        a_refs = refs[n_r + n_c + n_o:]
        for o, r in zip(o_refs, res[:n_o]):
            o[...] = r.astype(o.dtype)
        if n_a:
            @pl.when(pl.program_id(0) == 0)
            def _():
                for ar in a_refs:
                    ar[...] = jnp.zeros_like(ar)
            for ar, r in zip(a_refs, res[n_o:]):
                ar[...] += r

    return _call_with_comm(
        body, (T // tb,), name,
        [pl.BlockSpec((tb, w), functools.partial(lambda i, j: (i, j), j=j)) for _, w, j in rows]
        + [pl.BlockSpec(c.shape, lambda i: (0, 0)) for c in consts],
        [r[0] for r in rows] + list(consts),
        [pl.BlockSpec((tb, d), lambda i: (i, 0)) for d, _ in outs] + [pl.BlockSpec(s, lambda i: (0, 0)) for s in accs],
        [jax.ShapeDtypeStruct((T, d), dt) for d, dt in outs] + [jax.ShapeDtypeStruct(s, F32) for s in accs],
        comm, sem=("arbitrary" if n_a else "parallel",))


def _rms_stats(x):
    r = lax.rsqrt(jnp.mean(x * x, axis=-1, keepdims=True) + EPS)
    return r, x * r


def _rms_bwd(x, g, dy):
    r, xn = _rms_stats(x)
    dyg = dy * g
    dx = r * (dyg - xn * jnp.mean(dyg * xn, axis=-1, keepdims=True))
    return dx, jnp.sum(dy * xn, axis=0, keepdims=True)


def rms_fwd(x, g, name, comm=None):
    res = rowwise(lambda xv, gv: (_rms_stats(xv)[1] * gv,), [x], [g], [(x.shape[1], BF16)], [], name, comm=comm)
    return res[0] if comm is None else (res[0], res[1:])


def rms_bwd(x, g, dy, name, resid=None, dx_dtype=F32):
    def fn(*v):
        if resid is None:
            xv, dyv, gv = v
            dx, dg = _rms_bwd(xv, gv, dyv)
        else:
            xv, dyv, rv, gv = v
            dx, dg = _rms_bwd(xv, gv, dyv)
            dx = dx + rv
        return dx, dg
    rows = [x, dy] + ([] if resid is None else [resid])
    return rowwise(fn, rows, [g], [(x.shape[1], dx_dtype)], [(1, x.shape[1])], name)


def mm_rms_bwd(pairs, x, g, name, resid=None, dx_dtype=F32, comm=None, token=None):
    def epi(accs, ex):
        dx, dg = _rms_bwd(ex[0], ex[-1], accs[0])
        return (dx if resid is None else dx + ex[1]), dg
    extras = [x] + ([] if resid is None else [resid]) + ([] if token is None else [("whole", token)]) + [g]
    return mm([pairs], [dx_dtype], name, tm=min(256, x.shape[0]), tn=x.shape[1], epi=epi, extras=extras, comm=comm, n_sum=1)


def mm_resid(a, b, x, g, wgt, name, comm=None, target=None):
    def epi(accs, ex):
        y = ex[0] + wgt * _rms_stats(accs[0])[1] * ex[1]
        if target is None:
            return accs[0], y
        d = y - ex[2]
        return accs[0], d / D, jnp.sum(d * d, axis=0, keepdims=True)
    return mm([[(a, b, "nn")]], [F32, F32], name, tm=min(512, a.shape[0]), tn=b.shape[1], epi=epi,
              extras=[x, g] + ([] if target is None else [target]), sub=2, comm=comm, n_sum=0 if target is None else 1)


def resid_bwd(h, g, dy, wgt, name):
    def fn(hv, dyv, gv):
        dx, dg = _rms_bwd(hv, gv, dyv)
        return wgt * dx, wgt * dg
    return rowwise(fn, [h, dy], [g], [(h.shape[1], BF16)], [(1, h.shape[1])], name)


def _silu_parts(g):
    s = _sigmoid(g)
    return g * s, s * (1.0 + g * (1.0 - s))


def gated_norm_fwd(y, z, g, name):
    W = SSD_INNER // SSD_G

    def fn(yv, zv, gv):
        yg = yv * _silu_parts(zv)[0]
        return (jnp.concatenate([_rms_stats(yg[:, i * W:(i + 1) * W])[1] for i in range(SSD_G)], axis=1) * gv,)
    return rowwise(fn, [y, z], [g], [(SSD_INNER, BF16)], [], name)[0]


def gated_norm_bwd(y, z, dyn, g, name):
    W = SSD_INNER // SSD_G

    def fn(yv, zv, dv, gv):
        sil, dsil = _silu_parts(zv)
        yg = yv * sil
        parts = [_rms_bwd(yg[:, i * W:(i + 1) * W], gv[:, i * W:(i + 1) * W], dv[:, i * W:(i + 1) * W]) for i in range(SSD_G)]
        dyg = jnp.concatenate([p[0] for p in parts], axis=1)
        dg = jnp.concatenate([p[1] for p in parts], axis=1)
        return dyg * sil, dyg * yv * dsil, dg
    return rowwise(fn, [y, z, dyn], [g], [(SSD_INNER, BF16), (SSD_INNER, BF16)], [(1, SSD_INNER)], name)


def merge_fwd(gl, ys, ym, gb, name):
    def fn(glv, ysv, ymv, gbv):
        gt = _sigmoid(glv + gbv)
        return (gt[:, :D] * ysv + gt[:, D:] * ymv,)
    return rowwise(fn, [gl, ys, ym], [gb], [(D, BF16)], [], name)[0]


def merge_bwd(gl, ys, ym, dm, gb, name):
    def fn(glv, ysv, ymv, dmv, gbv):
        gt = _sigmoid(glv + gbv)
        gs, gm = gt[:, :D], gt[:, D:]
        dgl = jnp.concatenate([dmv * ysv * gs * (1.0 - gs), dmv * ymv * gm * (1.0 - gm)], axis=1)
        return dmv * gs, dmv * gm, dgl, jnp.sum(dgl, axis=0, keepdims=True)
    return rowwise(fn, [gl, ys, ym, dm], [gb], [(D, BF16), (D, BF16), (2 * D, BF16)], [(1, 2 * D)], name)


def _adamw_math(wv, gv, mv, vv):
    mn = B1 * mv + (1.0 - B1) * gv
    vn = B2 * vv + (1.0 - B2) * (gv * gv)
    mh = mn / (1.0 - B1 ** STEP)
    vh = vn / (1.0 - B2 ** STEP)
    return -LR * (mh / (jnp.sqrt(vh) + AEPS) + WD * wv), mn, vn


def adamw(w, g, m, v, name):
    R, C = w.shape
    tb = _tile(R, (256, 128, 64, 32, 16, 8))
    return rowwise(_adamw_math, [w, g, m, v], [], [(C, F32)] * 3, [], name, tb=tb)


def adamw_small(packed, ws, ms, vs):
    k = len(ws)
    sizes = [x.shape[1] for x in ws]

    def body(*refs):
        p_ref, w_refs, m_refs, v_refs = refs[0], refs[1:1 + k], refs[1 + k:1 + 2 * k], refs[1 + 2 * k:1 + 3 * k]
        outs = refs[1 + 3 * k:]
        r0 = 0
        for i, n in enumerate(sizes):
            nr = -(-n // 128)
            g = jnp.concatenate([p_ref[r0 + r:r0 + r + 1, :] for r in range(nr)], axis=1)[:, :n]
            r0 += nr
            outs[i][...] = g
            outs[k + i][...], outs[2 * k + i][...], outs[3 * k + i][...] = _adamw_math(w_refs[i][...], g, m_refs[i][...], v_refs[i][...])

    res = pl.pallas_call(body, name="adamw_small",
                         out_shape=[jax.ShapeDtypeStruct((1, n), F32) for _ in range(4) for n in sizes])(packed, *ws, *ms, *vs)
    return [res[j * k:(j + 1) * k] for j in range(4)]


def adamw_from_slots(recv, piece, w, m, v, name, token=None):
    K, n = w.shape
    ns = recv.shape[0]
    assert recv.shape[2] == n and recv.shape[1] % K == 0
    tb = _tile(K, (256, 176, 128, 64, 32, 16, 8)) if K % 8 == 0 else K
    r_spec = pl.BlockSpec((ns, tb, n), lambda i: (0, piece * (K // tb) + i, 0))
    w_spec = pl.BlockSpec((tb, n), lambda i: (i, 0))

    def body(r_ref, w_ref, m_ref, v_ref, *rest):
        g_ref, d_ref, mo_ref, vo_ref = rest[-4:]
        g = r_ref[0].astype(F32)
        for s in range(1, ns):
            g = g + r_ref[s].astype(F32)
        g_ref[...] = g
        d_ref[...], mo_ref[...], vo_ref[...] = _adamw_math(w_ref[...], g, m_ref[...], v_ref[...])

    extra = [] if token is None else [token]
    return pl.pallas_call(
        body, grid=(K // tb,), name=name,
        in_specs=[r_spec, w_spec, w_spec, w_spec] + [pl.BlockSpec(t.shape, lambda i: (0, 0)) for t in extra], out_specs=[w_spec] * 4,
        out_shape=[jax.ShapeDtypeStruct((K, n), F32)] * 4, compiler_params=_cp("parallel"),
    )(recv, w, m, v, *extra)


def _me():
    return lax.axis_index("x"), lax.axis_index("y"), lax.axis_index("c")


def _dev_index():
    x, y, c = _me()
    return 4 * x + 2 * y + c


HBM_SPEC = pl.BlockSpec(memory_space=pl.ANY)


class GatherComm:
    def __init__(self, shards):
        self.inputs = [s for s, _ in shards]
        self.rows = [list(r) for _, r in shards]
        n = len(shards)
        self.out_shapes = [jax.ShapeDtypeStruct((N_DEV, r, s.shape[1]), s.dtype) for s, rows in shards for r in rows]
        self.sems = [pltpu.SemaphoreType.DMA((7 * n,)), pltpu.SemaphoreType.DMA((7 * n,)), pltpu.SemaphoreType.DMA((n,))]

    def _plan(self, x_refs, out_refs, sems):
        send_sems, recv_sems, local_sems = sems
        x, y, c = _me()
        me, sibling = (x, y, c), (x, y, 1 - c)
        chips = [(1 - x, y), (x, 1 - y), (1 - x, 1 - y)]
        index = lambda px, py, pc: 4 * px + 2 * py + pc
        mine, first, passed, whole = [], [], [], []
        pos = 0
        for i, rows in enumerate(self.rows):
            kw = lambda k: dict(send_sem=send_sems.at[7 * i + k], recv_sem=recv_sems.at[7 * i + k], device_id_type=MESH)
            r0 = 0
            fwd = [[] for _ in chips]
            for j, nr in enumerate(rows):
                out, src = out_refs[pos + j], x_refs[i].at[pl.ds(r0, nr)]
                mine.append(pltpu.make_async_copy(src, out.at[index(*me)], local_sems.at[i]))
                first.append(pltpu.make_async_remote_copy(src_ref=src, dst_ref=out.at[index(*me)], device_id=sibling, **kw(0)))
                for jj, chip in enumerate(chips):
                    first.append(pltpu.make_async_remote_copy(src_ref=src, dst_ref=out.at[index(*me)], device_id=(*chip, c),
                                                              **kw(1 + jj)))
                    blk = out.at[index(*chip, c)]
                    fwd[jj].append(pltpu.make_async_remote_copy(src_ref=blk, dst_ref=blk, device_id=sibling, **kw(4 + jj)))
                r0 += nr
            passed.append(fwd)
            whole.append([pltpu.make_async_remote_copy(src_ref=x_refs[i], dst_ref=x_refs[i], device_id=sibling, **kw(k))
                          for k in range(7)])
            pos += len(rows)
        return mine, first, passed, whole

    def start(self, x_refs, out_refs, sems):
        mine, first, _, _ = self._plan(x_refs, out_refs, sems)
        for cp in mine + first:
            cp.start()

    def finish(self, x_refs, out_refs, sems):
        _, _, passed, whole = self._plan(x_refs, out_refs, sems)
        local_sems = sems[2]
        for i, fwd in enumerate(passed):
            for jj in range(3):
                whole[i][1 + jj].wait_recv()
                for cp in fwd[jj]:
                    cp.start()
        for i in range(len(passed)):
            whole[i][0].wait_recv()
            for jj in range(3):
                whole[i][4 + jj].wait_recv()
        for i in range(len(passed)):
            for k in range(7):
                whole[i][k].wait_send()
            pltpu.make_async_copy(x_refs[i], x_refs[i], local_sems.at[i]).wait()


def run_comm(comm, name):
    n_in, n_out = len(comm.inputs), len(comm.out_shapes)

    def body(*refs):
        ins, outs, sems = refs[:n_in], refs[n_in:n_in + n_out], refs[n_in + n_out:]
        comm.start(ins, outs, sems)
        comm.finish(ins, outs, sems)

    return pl.pallas_call(body, name=name, out_shape=comm.out_shapes, in_specs=[HBM_SPEC] * n_in,
                          out_specs=[HBM_SPEC] * n_out, scratch_shapes=comm.sems)(*comm.inputs)


def _attach(comm, body, n_in, n_out, first, last):
    if comm is None:
        return body
    ci, co, cs = len(comm.inputs), len(comm.out_shapes), len(comm.sems)

    def wrapped(*refs):
        h_in, c_in = refs[:n_in], refs[n_in:n_in + ci]
        h_out, c_out = refs[n_in + ci:n_in + ci + n_out], refs[n_in + ci + n_out:n_in + ci + n_out + co]
        rest = refs[n_in + ci + n_out + co:]
        h_scr, c_sem = rest[:len(rest) - cs], rest[len(rest) - cs:]

        @pl.when(first())
        def _():
            comm.start(c_in, c_out, c_sem)

        body(*h_in, *h_out, *h_scr)

        @pl.when(last())
        def _():
            comm.finish(c_in, c_out, c_sem)

    return wrapped


def _grid_ends(grid):
    first = lambda: functools.reduce(lambda a, b: a & b, [pl.program_id(i) == 0 for i in range(len(grid))])
    last = lambda: functools.reduce(lambda a, b: a & b, [pl.program_id(i) == g - 1 for i, g in enumerate(grid)])
    return first, last


def _call_with_comm(body, grid, name, in_specs, args, out_specs, out_shape, comm, scratch=(), sem=None):
    sem = sem or ("parallel",) * len(grid)
    scratch = list(scratch)
    if comm is not None:
        body = _attach(comm, body, len(args), len(out_shape), *_grid_ends(grid))
        in_specs, args = in_specs + [HBM_SPEC] * len(comm.inputs), args + comm.inputs
        out_specs, out_shape = out_specs + [HBM_SPEC] * len(comm.out_shapes), out_shape + comm.out_shapes
        scratch, sem = scratch + comm.sems, ("arbitrary",) * len(grid)
    return pl.pallas_call(body, grid=grid, name=name, in_specs=in_specs, out_specs=out_specs, out_shape=out_shape,
                          scratch_shapes=scratch, compiler_params=_cp(*sem))(*args)


class ScatterComm:
    def __init__(self, groups):
        self.sizes = [len(g) for g in groups]
        self.rows = [[pc.shape[1] for pc in g] for g in groups]
        ng = len(groups)
        self.inputs = [pc for g in groups for pc in g]
        self.out_shapes = [jax.ShapeDtypeStruct((N_DEV, sum(self.rows[gi]), g[0].shape[2]), g[0].dtype) for gi, g in enumerate(groups)]
        self.sems = [pltpu.SemaphoreType.DMA((7 * ng,)), pltpu.SemaphoreType.DMA((7 * ng,)), pltpu.SemaphoreType.DMA((ng,))]

    def _peers(self):
        x, y, c = _me()
        out = []
        for k in range(1, N_DEV):
            px = 1 - x if k & 4 else x
            py = 1 - y if k & 2 else y
            pc = 1 - c if k & 1 else c
            out.append((k, 4 * px + 2 * py + pc, dict(device_id=(px, py, pc), device_id_type=MESH)))
        return 4 * x + 2 * y + c, out

    def start(self, ins, outs, sems):
        send_sems, recv_sems, local_sems = sems
        me, peers = self._peers()
        pos = 0
        for gi, size in enumerate(self.sizes):
            for i, pc in enumerate(ins[pos:pos + size]):
                dst = outs[gi].at[me, pl.ds(sum(self.rows[gi][:i]), self.rows[gi][i])]
                pltpu.make_async_copy(pc.at[me], dst, local_sems.at[gi]).start()
                for k, peer, kw in peers:
                    pltpu.make_async_remote_copy(src_ref=pc.at[peer], dst_ref=dst, send_sem=send_sems.at[7 * gi + k - 1],
                                                 recv_sem=recv_sems.at[7 * gi + k - 1], **kw).start()
            pos += size

    def finish(self, ins, outs, sems):
        send_sems, recv_sems, local_sems = sems
        me, peers = self._peers()
        whole = [pltpu.make_async_remote_copy(src_ref=outs[gi].at[peer], dst_ref=outs[gi].at[peer],
                                              send_sem=send_sems.at[7 * gi + k - 1], recv_sem=recv_sems.at[7 * gi + k - 1], **kw)
                 for gi in range(len(self.sizes)) for k, peer, kw in peers]
        for cp in whole:
            cp.wait_recv()
        for cp in whole:
            cp.wait_send()
        for gi in range(len(self.sizes)):
            pltpu.make_async_copy(outs[gi].at[me], outs[gi].at[me], local_sems.at[gi]).wait()


def _peer_list():
    x, y, c = _me()
    out = []
    for k in range(1, N_DEV):
        px = 1 - x if k & 4 else x
        py = 1 - y if k & 2 else y
        pc = 1 - c if k & 1 else c
        out.append((k, 4 * px + 2 * py + pc, dict(device_id=(px, py, pc), device_id_type=MESH)))
    return 4 * x + 2 * y + c, out


SEM_SPEC = pl.BlockSpec(memory_space=pltpu.SEMAPHORE)
HBM_ONLY = pl.BlockSpec(memory_space=pltpu.HBM)
N_SPLIT_SEMS = 2 * (N_DEV - 1)


def exchange_start(piece, after, name):
    def body(piece_ref, land_ref, after_ref, *rest):
        sems, token, local_sem = rest[:N_SPLIT_SEMS], rest[N_SPLIT_SEMS + 1], rest[N_SPLIT_SEMS + 2]
        me, peers = _peer_list()
        for k, peer, kw in peers:
            pltpu.make_async_remote_copy(src_ref=piece_ref.at[peer], dst_ref=land_ref.at[me], send_sem=sems[k - 1],
                                         recv_sem=sems[N_DEV - 2 + k], **kw).start()
        mine = pltpu.make_async_copy(piece_ref.at[me], land_ref.at[me], local_sem)
        mine.start()
        token[...] = jnp.zeros_like(token)
        mine.wait()

    res = pl.pallas_call(
        body, name=name + "_start",
        out_shape=(pltpu.SemaphoreType.DMA(()),) * N_SPLIT_SEMS + (pltpu.HBM(piece.shape, piece.dtype), jax.ShapeDtypeStruct((8, 128), F32)),
        in_specs=(HBM_SPEC, HBM_ONLY, HBM_SPEC),
        out_specs=(SEM_SPEC,) * N_SPLIT_SEMS + (HBM_ONLY, pl.BlockSpec(memory_space=pltpu.VMEM)),
        input_output_aliases={1: N_SPLIT_SEMS}, scratch_shapes=[pltpu.SemaphoreType.DMA(())],
        compiler_params=pltpu.CompilerParams(has_side_effects=pltpu.SideEffectType.DATAFLOW_SIDE_EFFECTING),
    )(piece, pltpu.with_memory_space_constraint(lax.empty(piece.shape, piece.dtype), pltpu.HBM), after)
    return res[:N_SPLIT_SEMS], res[N_SPLIT_SEMS], res[N_SPLIT_SEMS + 1]


def exchange_wait(sems, piece, land, after, name):
    after = list(after)
    def body(piece_ref, land_ref, *rest):
        sem_refs = rest[:N_SPLIT_SEMS]
        me, peers = _peer_list()
        for k, peer, kw in peers:
            cp = pltpu.make_async_remote_copy(src_ref=piece_ref.at[peer], dst_ref=land_ref.at[peer], send_sem=sem_refs[k - 1],
                                              recv_sem=sem_refs[N_DEV - 2 + k], **kw)
            cp.wait_send()
            cp.wait_recv()

    return pl.pallas_call(
        body, name=name + "_wait",
        out_shape=(pltpu.HBM(land.shape, land.dtype),),
        in_specs=(HBM_SPEC, HBM_ONLY) + (SEM_SPEC,) * N_SPLIT_SEMS + (HBM_SPEC,) * len(after), out_specs=(HBM_ONLY,),
        input_output_aliases={1: 0},
        compiler_params=pltpu.CompilerParams(has_side_effects=pltpu.SideEffectType.DATAFLOW_SIDE_EFFECTING),
    )(piece, land, *sems, *after)[0]


def sum_slots(recv, name, tr):
    n, R, C = recv.shape

    def body(r_ref, o_ref):
        acc = r_ref[0].astype(F32)
        for s in range(1, n):
            acc = acc + r_ref[s].astype(F32)
        o_ref[...] = acc

    return pl.pallas_call(
        body, grid=(R // tr,), name=name,
        in_specs=[pl.BlockSpec((n, tr, C), lambda i: (0, i, 0))], out_specs=pl.BlockSpec((tr, C), lambda i: (i, 0)),
        out_shape=jax.ShapeDtypeStruct((R, C), F32), compiler_params=_cp("parallel"),
    )(recv)


PACK_W, FLAT_W = 1024, 128
MAIN = [
    ("ffn1_w_gate", "col"), ("ffn1_w_up", "col"), ("ffn1_w_down", "row"),
    ("ffn2_w_gate", "col"), ("ffn2_w_up", "col"), ("ffn2_w_down", "row"),
    ("w_ssd_proj", "row"), ("w_mla_proj", "row"), ("w_out", "row"),
    ("w_xq", "row"), ("w_xk", "row"), ("w_xv", "row"), ("w_xo", "row"),
    ("w_uk", "col"), ("w_uv", "col"), ("w_in", "col"),
]
FLAT = [("w_uq", "col")]
BIG = MAIN + FLAT
SMALL = ["ffn1_pre_g", "ffn1_post_g", "mix_pre_g", "conv_b", "dt_bias", "a_log", "d_skip", "ssd_norm_g", "q_norm_g",
         "kv_norm_g", "gate_bias", "mix_post_g", "xa_pre_g", "mem_norm_g", "xa_post_g", "ffn2_pre_g", "ffn2_post_g"]
WEIGHTS = ['ffn1_pre_g', 'ffn1_w_gate', 'ffn1_w_up', 'ffn1_w_down', 'ffn1_post_g', 'mix_pre_g', 'w_in', 'conv_w', 'conv_b',
           'dt_bias', 'a_log', 'd_skip', 'ssd_norm_g', 'w_ssd_proj', 'q_norm_g', 'w_uq', 'kv_norm_g', 'w_uk', 'w_uv',
           'w_mla_proj', 'gate_bias', 'w_out', 'mix_post_g', 'xa_pre_g', 'mem_norm_g', 'w_xq', 'w_xk', 'w_xv', 'w_xo',
           'xa_post_g', 'ffn2_pre_g', 'ffn2_w_gate', 'ffn2_w_up', 'ffn2_w_down', 'ffn2_post_g']


def _pack_rows(w, kind, width):
    m = w[0].T if kind == "col" else w[0]
    return m.reshape(-1, width)


KIND = dict(BIG)
GATHER_PLAN = {
    "ffn1_pre": (["ffn1_w_gate", "ffn1_w_up"], []),
    "ffn1_gate_up": (["ffn1_w_down", "w_in@0"], []),
    "ffn1_down": (["w_in@1"], ["conv_w"]),
    "ssd_fwd": (["w_ssd_proj", "w_mla_proj", "w_out", "w_uk", "w_uv"], ["w_uq"]),
    "attn_fwd": (["w_xq", "w_xk", "w_xv", "w_xo", "ffn2_w_gate", "ffn2_w_up", "ffn2_w_down"], []),
}
EARLY_EXCHANGE, LAST_EXCHANGE = "early", "last"
SCATTER_PLAN = {
    "attn_bwd": [["ffn2_w_gate", "ffn2_w_up", "ffn2_w_down"], ["w_xq", "w_xk", "w_xv", "w_xo"]],
    "ssd_bwd": [["w_ssd_proj", "w_mla_proj", "w_out"], ["w_uk", "w_uv"], ["w_uq"]],
    "in_bwd": [["w_in#0"]],
    "ffn1:down_bwd": [["w_in#1"]],
    "ffn1:dwd": [["w_in#2"]],
    "ffn1:dwg": [["ffn1_w_down#0"]],
    "ffn1:dwu": [["ffn1_w_down#1"]],
    "early": [["ffn1_w_gate"]],
    "last": [["ffn1_w_up"]],
}
PARTS = {"w_in@0": ("w_in", 0, 336), "w_in@1": ("w_in", 336, 662),
         "w_in#0": ("w_in", 0, 336), "w_in#1": ("w_in", 336, 496), "w_in#2": ("w_in", 496, 662),
         "ffn1_w_down#0": ("ffn1_w_down", 0, 176), "ffn1_w_down#1": ("ffn1_w_down", 176, 352)}


def _parts_of(base, mark):
    return sorted(pn for pn, (b, _, _) in PARTS.items() if b == base and mark in pn)


class Stage:
    def __init__(self, w):
        self.w = w
        self.width = {n: PACK_W if (n, k) in MAIN else FLAT_W for n, k in BIG}
        self.nrows = {n: math.prod(w[n].shape) // self.width[n] for n, _ in BIG}
        self.recv = {}
        self.split = {}
        self.arrived_parts = {}

    def _shards(self, tag):
        names_main, names_flat = GATHER_PLAN[tag]

        def pack(n):
            if n == "conv_w":
                return _pad_rows(lax.bitcast_convert_type(self.w[n][0], BF16).reshape(-1, FLAT_W), 16)
            base, r0, r1 = PARTS.get(n, (n, 0, None))
            return _pack_rows(self.w[base], KIND[base], self.width[base])[r0:r1].astype(BF16)
        shards = []
        if names_main:
            pieces = [pack(n) for n in names_main]
            shards.append((jnp.concatenate(pieces, axis=0), [pc.shape[0] for pc in pieces]))
        if names_flat:
            pieces = [pack(n) for n in names_flat]
            shards.append((jnp.concatenate(pieces, axis=0), [pc.shape[0] for pc in pieces]))
        return shards

    def gather(self, tag):
        return GatherComm(self._shards(tag)) if tag in GATHER_PLAN else None

    def gathered(self, tag, outs, W, p):
        if tag not in GATHER_PLAN:
            return
        names_main, names_flat = GATHER_PLAN[tag]
        outs = list(outs)
        for n in names_main + names_flat:
            rows = outs.pop(0)
            if n == "conv_w":
                cw = self.w[n]
                bits = rows[:, :2 * math.prod(cw.shape) // FLAT_W].reshape((N_DEV,) + cw.shape[1:] + (2,))
                p[n] = lax.bitcast_convert_type(bits, F32).transpose(1, 0, 2).reshape(cw.shape[1], -1)
                continue
            if n in PARTS:
                self.arrived_parts[n] = rows
                base = PARTS[n][0]
                mine = _parts_of(base, "@")
                if not all(pn in self.arrived_parts for pn in mine):
                    continue
                n, rows = base, jnp.concatenate([self.arrived_parts[pn] for pn in mine], axis=1)
            K = self.w[n].shape[1] if KIND[n] == "col" else PACK_W
            W[n] = rows.reshape(-1, K)

    def pieces(self, tag, gw):
        def piece(n):
            if n in PARTS:
                base, r0, r1 = PARTS[n]
                return gw[base].reshape(N_DEV, self.nrows[base], self.width[base])[:, r0:r1]
            return gw[n].reshape(N_DEV, self.nrows[n], self.width[n])
        return [[piece(n) for n in names] for names in SCATTER_PLAN[tag]]

    def scatter(self, tag, gw):
        return ScatterComm(self.pieces(tag, gw)) if tag in SCATTER_PLAN else None

    def scattered(self, tag, outs):
        if tag in SCATTER_PLAN:
            self.recv[tag] = outs

    def split_start(self, tag, gw, after=None, behind=None):
        after = after if behind is None else self.recv[behind][0]
        piece, = self.pieces(tag, gw)[0]
        sems, land, token = exchange_start(piece, after, "exchange_" + tag)
        self.split[tag] = (sems, piece, land)
        return token

    def split_wait(self, tag, after):
        sems, piece, land = self.split.pop(tag)
        return exchange_wait(sems, piece, land, after, "exchange_" + tag)


def _pad_rows(a, mult):
    r = (-a.shape[0]) % mult
    return a if r == 0 else jnp.concatenate([a, jnp.zeros((r,) + a.shape[1:], a.dtype)], axis=0)


def _pack_small(vals, loss_row=None, conv_w=None):
    rows = []
    for v in vals:
        f = v.reshape(-1)
        f = jnp.concatenate([f, jnp.zeros(((-f.shape[0]) % 128,), F32)])
        rows.append(f.reshape(-1, 128))
    if conv_w is not None:
        rows.append(conv_w.reshape(-1, 128))
    if loss_row is not None:
        rows.append(loss_row)
    return _pad_rows(jnp.concatenate(rows, axis=0), 8)


def _tn(a, b, name, out_dtype=BF16, comm=None):
    M, N = a.shape[1], b.shape[1]
    T = a.shape[0]
    tm = M if M <= 1536 else M // 2
    tk = 2048 if T % 2048 == 0 and T > 2048 else None
    res = mm([[(a, b, "tn")]], [out_dtype], name, tm=tm, tn=N, tk=tk, comm=comm)
    return res[0] if comm is None else (res[0], res[1:])


class NoStage:
    def gather(self, tag):
        return None

    def gathered(self, tag, outs, W, p):
        pass

    def scatter(self, tag, gw):
        return None

    def scattered(self, tag, outs):
        pass

    def split_start(self, tag, gw, after=None, behind=None):
        return None


def _ffn_fwd(x, gpre, gpost, W, p, tag, stage, target=None):
    comm = stage.gather(tag + "_pre")
    h = rms_fwd(x, gpre, tag + "_pre", comm=comm)
    if comm is not None:
        h, arrived = h
        stage.gathered(tag + "_pre", arrived, W, p)

    def swi(accs, ex):
        sil, dsil = _silu_parts(accs[0])
        return sil, accs[1] * dsil, sil * accs[1]
    G, U, A, *arrived = mm([[(h, W[tag + "_w_gate"], "nt")], [(h, W[tag + "_w_up"], "nt")]], [BF16, BF16, BF16], tag + "_gate_up",
                           tn=DFF // 2, epi=swi, comm=stage.gather(tag + "_gate_up"), sub=4 if h.shape[0] % 1024 == 0 else 1)
    stage.gathered(tag + "_gate_up", arrived, W, p)
    H, y, *rest = mm_resid(A, W[tag + "_w_down"], x, gpost, FFN_RES, tag + "_down", comm=stage.gather(tag + "_down"), target=target)
    saved = (x, h, G, U, A, H)
    if target is not None:
        return y, saved, rest[0]
    stage.gathered(tag + "_down", rest, W, p)
    return y, saved


def _ffn_bwd(dy, saved, gpre, gpost, wg_t, wu_t, wd, tag, stage, gw):
    x, h, G, U, A, H = saved
    dH, dgpost = resid_bwd(H, gpost, dy, FFN_RES, tag + "_post_bwd")

    def dswi(accs, ex):
        return accs[0] * ex[1], accs[0] * ex[0]

    def hosted(where, call):
        comm = stage.scatter(tag + ":" + where, gw)
        res = call(comm)
        if comm is None:
            return res
        stage.scattered(tag + ":" + where, res[1])
        return res[0]

    res = hosted("down_bwd", lambda comm: (lambda r: r if comm is None else (r[:2], r[2:]))(
        mm([[(dH, wd, "nt")]], [BF16, BF16], tag + "_down_bwd", tn=DFF // 2, epi=dswi, extras=[G, U], comm=comm,
           sub=4 if dH.shape[0] % 1024 == 0 else 1)))
    dG, dU = res
    gw[tag + "_w_down"] = hosted("dwd", lambda comm: _tn(A, dH, tag + "_dwd", comm=comm))
    gw[tag + "_w_gate"] = hosted("dwg", lambda comm: _tn(dG, h, tag + "_dwg", comm=comm))
    gw[tag + "_w_up"] = hosted("dwu", lambda comm: _tn(dU, h, tag + "_dwu", comm=comm))
    token = stage.split_start(EARLY_EXCHANGE, gw, behind=tag + ":dwu") if tag == "ffn1" else None
    dx, dgpre = mm_rms_bwd([(dG, wg_t, "nn"), (dU, wu_t, "nn")], x, gpre, tag + "_gate_up_bwd", resid=dy, token=token)
    return dx, dgpre, dgpost


def _local_step(x, mem, positions, tgt, W, p, stage=None):
    stage = stage or NoStage()
    nseq = x.shape[0]
    T = nseq * x.shape[1]
    x0 = x.reshape(T, D)
    mem2 = mem.reshape(-1, D)

    x1, ffn1 = _ffn_fwd(x0, p["ffn1_pre_g"], p["ffn1_post_g"], W, p, "ffn1", stage)

    w_in_t = W["w_in"]
    bounds = [0]
    for n in (SSD_INNER, CONV_CH, SSD_H, QR, KVR, ROPE, 2 * D):
        bounds.append(bounds[-1] + n)
    wt_z, wt_xbc, wt_dt, wt_q, wt_kv, wt_kr, wt_gate = [w_in_t[bounds[i]:bounds[i + 1]] for i in range(7)]
    wt_dt, wt_kr = _pad_rows(wt_dt, SLOT), _pad_rows(wt_kr, SLOT)
    wt_dtkr = jnp.concatenate([wt_dt, wt_kr], axis=0)
    hm = rms_fwd(x1, p["mix_pre_g"], "mix_pre")
    z = mm1(hm, wt_z, "nt", BF16, "in_z")
    xbc = mm1(hm, wt_xbc, "nt", BF16, "in_xbc")
    q_c = mm1(hm, wt_q, "nt", F32, "in_q", tn=QR)
    kv_c = mm1(hm, wt_kv, "nt", F32, "in_kv")
    dtkr = mm1(hm, wt_dtkr, "nt", F32, "in_dtkr")
    gl = mm1(hm, wt_gate, "nt", BF16, "in_gate")

    xbc_act = conv_fwd(xbc, p["conv_w"], p["conv_b"], nseq)
    y_ssd_core, prev, *arrived = ssd_fwd(xbc_act, dtkr, p["dt_bias"], p["a_log"], p["d_skip"], nseq, comm=stage.gather("ssd_fwd"))
    stage.gathered("ssd_fwd", arrived, W, p)
    yn = gated_norm_fwd(y_ssd_core, z, p["ssd_norm_g"], "ssd_norm")
    y_ssd = mm1(yn, W["w_ssd_proj"], "nn", BF16, "ssd_proj")

    slot_rows = lambda wt, per: jnp.pad(wt.reshape(MLA_H, per, -1), ((0, 0), (0, SLOT - per), (0, 0))).reshape(MLA_H * SLOT, -1)
    wq_s, wk_s, wv_s = slot_rows(W["w_uq"], QK), slot_rows(W["w_uk"], NOPE), slot_rows(W["w_uv"], VD)
    wo_s = slot_rows(W["w_mla_proj"], VD)
    qn = rms_fwd(q_c, p["q_norm_g"], "q_norm")
    rope_c, rope_s = rope_table(*_rope_inputs(positions))
    rope_args = [("rows", rope_c), ("rows", rope_s)]
    Qc, = mm([[(qn, wq_s, "nt")]], [BF16], "uq", epi=rope_q_epilogue, extras=rope_args, sub=4 if T % 1024 == 0 else 1)
    kvn = rms_fwd(kv_c, p["kv_norm_g"], "kv_norm")
    Kc, = mm([[(kvn, wk_s, "nt")]], [BF16], "uk", epi=rope_k_epilogue, extras=rope_args + [("rows", dtkr)],
             sub=4 if T % 1024 == 0 else 1)
    v_s = mm1(kvn, wv_s, "nt", BF16, "uv")
    o_s, lse, *arrived = attn_slot_fwd(Qc, Kc, v_s, nseq, comm=stage.gather("attn_fwd"))
    stage.gathered("attn_fwd", arrived, W, p)
    y_mla = mm1(o_s, wo_s, "nn", BF16, "mla_proj")

    merged = merge_fwd(gl, y_ssd, y_mla, p["gate_bias"], "merge")
    hmix, x2 = mm_resid(merged, W["w_out"], x1, p["mix_post_g"], 1.0, "mix_out")

    hq = rms_fwd(x2, p["xa_pre_g"], "xa_pre")
    mn = rms_fwd(mem2, p["mem_norm_g"], "mem_norm")
    xq = mm1(hq, W["w_xq"], "nn", BF16, "xq")
    xk = mm1(mn, W["w_xk"], "nn", BF16, "xk")
    xv = mm1(mn, W["w_xv"], "nn", BF16, "xv")
    xo, *arrived = xattn_fwd(xq, xk, xv, nseq, comm=stage.gather("xattn_fwd"))
    stage.gathered("xattn_fwd", arrived, W, p)
    ho, x3 = mm_resid(xo, W["w_xo"], x2, p["xa_post_g"], 1.0, "xo")

    dx4, ffn2, sq_cols = _ffn_fwd(x3, p["ffn2_pre_g"], p["ffn2_post_g"], W, p, "ffn2", stage, target=tgt.reshape(T, D))
    loss_row = (0.5 / D) * jnp.sum(sq_cols.reshape(-1, 128), axis=0, keepdims=True)

    gw, gs = {}, {}
    dx3, gs["ffn2_pre_g"], gs["ffn2_post_g"] = _ffn_bwd(
        dx4, ffn2, p["ffn2_pre_g"], p["ffn2_post_g"], W["ffn2_w_gate"], W["ffn2_w_up"], W["ffn2_w_down"], "ffn2", stage, gw)

    dho, gs["xa_post_g"] = resid_bwd(ho, p["xa_post_g"], dx3, 1.0, "xa_post_bwd")
    dxo = mm1(dho, W["w_xo"], "nt", BF16, "xo_bwd")
    gw["w_xo"] = _tn(xo, dho, "d_w_xo")
    dxq, dxk, dxv = xattn_bwd(xq, xk, xv, dxo, nseq)
    dx2, gs["xa_pre_g"] = mm_rms_bwd([(dxq, W["w_xq"], "nt")], x2, p["xa_pre_g"], "xq_bwd", resid=dx3)
    gw["w_xq"] = _tn(hq, dxq, "d_w_xq")
    dmn = mm([[(dxk, W["w_xk"], "nt"), (dxv, W["w_xv"], "nt")]], [F32], "xkv_bwd")[0]
    gw["w_xk"] = _tn(mn, dxk, "d_w_xk")
    gw["w_xv"] = _tn(mn, dxv, "d_w_xv")
    _, gs["mem_norm_g"] = rms_bwd(mem2, p["mem_norm_g"], dmn, "mem_norm_bwd", dx_dtype=BF16)

    dhmix, gs["mix_post_g"] = resid_bwd(hmix, p["mix_post_g"], dx2, 1.0, "mix_post_bwd")
    dmerged = mm1(dhmix, W["w_out"], "nt", BF16, "mix_out_bwd")
    gw["w_out"] = _tn(merged, dhmix, "d_w_out")
    dys, dym, dgl, gs["gate_bias"] = merge_bwd(gl, y_ssd, y_mla, dmerged, p["gate_bias"], "merge_bwd")

    unslot = lambda g, per: g.reshape(MLA_H, SLOT, -1)[:, :per].reshape(MLA_H * per, -1)
    do_s = mm1(dym, wo_s, "nt", BF16, "mla_proj_bwd")
    gw["w_mla_proj"] = unslot(_tn(o_s, dym, "d_w_mla_proj"), VD)
    dQc, dKc, dv_s, *sent = attn_slot_bwd(Qc, Kc, v_s, o_s, lse, do_s, nseq, comm=stage.scatter("attn_bwd", gw))
    stage.scattered("attn_bwd", sent)
    dq_s, dkn_s, dkr = rope_slot_bwd(dQc, dKc, rope_c, rope_s, "rope_bwd")
    dq_c, gs["q_norm_g"] = mm_rms_bwd([(dq_s, wq_s, "nn")], q_c, p["q_norm_g"], "uq_bwd", dx_dtype=BF16)
    gw["w_uq"] = unslot(_tn(dq_s, qn, "d_w_uq"), QK)
    dkv_c, gs["kv_norm_g"] = mm_rms_bwd([(dkn_s, wk_s, "nn"), (dv_s, wv_s, "nn")], kv_c, p["kv_norm_g"], "ukv_bwd", dx_dtype=BF16)
    gw["w_uk"] = unslot(_tn(dkn_s, kvn, "d_w_uk"), NOPE)
    gw["w_uv"] = unslot(_tn(dv_s, kvn, "d_w_uv"), VD)

    dyn = mm1(dys, W["w_ssd_proj"], "nt", BF16, "ssd_proj_bwd")
    gw["w_ssd_proj"] = _tn(yn, dys, "d_w_ssd_proj")
    dyc, dz, gs["ssd_norm_g"] = gated_norm_bwd(y_ssd_core, z, dyn, p["ssd_norm_g"], "ssd_norm_bwd")
    dxbc_act, ddtr, gs["dt_bias"], gs["a_log"], gs["d_skip"], *sent = ssd_bwd(
        xbc_act, dtkr, p["dt_bias"], p["a_log"], p["d_skip"], prev, dyc, nseq, comm=stage.scatter("ssd_bwd", gw))
    stage.scattered("ssd_bwd", sent)
    dxbc, gs["conv_w"], gs["conv_b"] = conv_bwd(xbc, p["conv_w"], p["conv_b"], dxbc_act, nseq)

    gw["w_in"] = jnp.concatenate([_tn(dz, hm, "d_w_in_z"), _tn(dxbc, hm, "d_w_in_xbc"), _tn(ddtr, hm, "d_w_in_dt")[:SSD_H],
                                  _tn(dq_c, hm, "d_w_in_q"), _tn(dkv_c, hm, "d_w_in_kv"), _tn(dkr, hm, "d_w_in_kr")[:ROPE],
                                  _tn(dgl, hm, "d_w_in_gate")], axis=0)
    dx1, gs["mix_pre_g"], *sent = mm_rms_bwd(
        [(dz, wt_z, "nn"), (dxbc, wt_xbc, "nn"), (ddtr, wt_dt, "nn"), (dq_c, wt_q, "nn"), (dkv_c, wt_kv, "nn"),
         (dkr, wt_kr, "nn"), (dgl, wt_gate, "nn")], x1, p["mix_pre_g"], "in_bwd", resid=dx2, comm=stage.scatter("in_bwd", gw))
    stage.scattered("in_bwd", sent)

    dx0, gs["ffn1_pre_g"], gs["ffn1_post_g"] = _ffn_bwd(
        dx1, ffn1, p["ffn1_pre_g"], p["ffn1_post_g"], W["ffn1_w_gate"], W["ffn1_w_up"], W["ffn1_w_down"], "ffn1", stage, gw)
    return loss_row, dx0.reshape(x.shape), gw, gs


def kernel(x, mem, positions, ffn1_pre_g, ffn1_w_gate, ffn1_w_up, ffn1_w_down, ffn1_post_g, mix_pre_g, w_in, conv_w, conv_b, dt_bias, a_log, d_skip, ssd_norm_g, w_ssd_proj, q_norm_g, w_uq, kv_norm_g, w_uk, w_uv, w_mla_proj, gate_bias, w_out, mix_post_g, xa_pre_g, mem_norm_g, w_xq, w_xk, w_xv, w_xo, xa_post_g, ffn2_pre_g, ffn2_w_gate, ffn2_w_up, ffn2_w_down, ffn2_post_g, loss_target, m_ffn1_pre_g, m_ffn1_w_gate, m_ffn1_w_up, m_ffn1_w_down, m_ffn1_post_g, m_mix_pre_g, m_w_in, m_conv_w, m_conv_b, m_dt_bias, m_a_log, m_d_skip, m_ssd_norm_g, m_w_ssd_proj, m_q_norm_g, m_w_uq, m_kv_norm_g, m_w_uk, m_w_uv, m_w_mla_proj, m_gate_bias, m_w_out, m_mix_post_g, m_xa_pre_g, m_mem_norm_g, m_w_xq, m_w_xk, m_w_xv, m_w_xo, m_xa_post_g, m_ffn2_pre_g, m_ffn2_w_gate, m_ffn2_w_up, m_ffn2_w_down, m_ffn2_post_g, v_ffn1_pre_g, v_ffn1_w_gate, v_ffn1_w_up, v_ffn1_w_down, v_ffn1_post_g, v_mix_pre_g, v_w_in, v_conv_w, v_conv_b, v_dt_bias, v_a_log, v_d_skip, v_ssd_norm_g, v_w_ssd_proj, v_q_norm_g, v_w_uq, v_kv_norm_g, v_w_uk, v_w_uv, v_w_mla_proj, v_gate_bias, v_w_out, v_mix_post_g, v_xa_pre_g, v_mem_norm_g, v_w_xq, v_w_xk, v_w_xv, v_w_xo, v_xa_post_g, v_ffn2_pre_g, v_ffn2_w_gate, v_ffn2_w_up, v_ffn2_w_down, v_ffn2_post_g):
    a = dict(locals())
    w = {n: a[n] for n in WEIGHTS}
    m = {n: a["m_" + n] for n in WEIGHTS}
    v = {n: a["v_" + n] for n in WEIGHTS}

    stage = Stage(w)
    W, p = {}, {n: w[n] for n in SMALL}
    loss_row, grad_x, gw, gs = _local_step(x, mem, positions, loss_target, W, p, stage)

    sm = _pack_small([gs[n] for n in SMALL], loss_row=loss_row, conv_w=gs["conv_w"])
    srecv, = run_comm(ScatterComm([[jnp.broadcast_to(sm[None], (N_DEV,) + sm.shape)]]), "exchange_small")
    s_rows = sum_slots(srecv, "sum_small", tr=sm.shape[0])
    token = stage.split_start(LAST_EXCHANGE, gw, after=s_rows)
    grads, delta, new_m, new_v = {}, {}, {}, {}
    raw_results = []

    def finish(n, buf, piece, token=None):
        col = KIND[n] == "col"
        turn = (lambda t: t.T) if col else (lambda t: t)
        K = w[n].shape[1]
        if col and buf.shape[2] != K:
            buf = buf.reshape(buf.shape[0], -1, K)
        res = adamw_from_slots(buf, piece, turn(w[n][0]), turn(m[n][0]), turn(v[n][0]), "adamw_" + n, token=token)
        raw_results.append(res[3])
        grads[n], delta[n], new_m[n], new_v[n] = [turn(r)[None] for r in res]

    finish(SCATTER_PLAN[EARLY_EXCHANGE][0][0], stage.split_wait(EARLY_EXCHANGE, after=[s_rows]), 0, token)
    parts = {}
    for tag, groups in SCATTER_PLAN.items():
        if tag in (EARLY_EXCHANGE, LAST_EXCHANGE):
            continue
        for names, buf in zip(groups, stage.recv[tag]):
            for piece, n in enumerate(names):
                if n in PARTS:
                    parts[n] = sum_slots(buf, "sum_" + n.replace("#", "_"), tr=buf.shape[1])
                else:
                    finish(n, buf, piece, token)
    for base in sorted({PARTS[pn][0] for pn in parts}):
        rows = jnp.concatenate([parts[pn] for pn in _parts_of(base, "#")], axis=0)
        finish(base, rows[None], 0, token)
    conv_w_full = p["conv_w"]
    small = adamw_small(s_rows, [w[n] for n in SMALL], [m[n] for n in SMALL], [v[n] for n in SMALL])
    for t, vals in zip((grads, delta, new_m, new_v), small):
        t.update(zip(SMALL, vals))
    r1 = sum(-(-w[n].shape[1] // 128) for n in SMALL)
    ncw = math.prod(conv_w_full.shape) // 128
    cw_grad_full = s_rows[r1:r1 + ncw].reshape(conv_w_full.shape)
    wsh = conv_w.shape[2]
    grads["conv_w"] = lax.dynamic_slice_in_dim(cw_grad_full, _dev_index() * wsh, wsh, axis=1)[None]
    loss = jnp.sum(s_rows[r1 + ncw])
    d_, m_, v_ = adamw(conv_w[0], grads["conv_w"][0], m["conv_w"][0], v["conv_w"][0], "adamw_conv_w")
    delta["conv_w"], new_m["conv_w"], new_v["conv_w"] = d_[None], m_[None], v_[None]
    finish(SCATTER_PLAN[LAST_EXCHANGE][0][0], stage.split_wait(LAST_EXCHANGE, after=raw_results + [small[3][0], v_]), 0)
    return (loss, grad_x, *[grads[n] for n in WEIGHTS], *[delta[n] for n in WEIGHTS],
            *[new_m[n] for n in WEIGHTS], *[new_v[n] for n in WEIGHTS])
```

```python
import functools
import math

import jax
import jax.numpy as jnp
from jax import lax
from jax.experimental import pallas as pl
from jax.experimental.pallas import tpu as pltpu

F32, BF16 = jnp.float32, jnp.bfloat16
MESH = pl.DeviceIdType.MESH
N_DEV = 8

D = 1024
DFF = 2816
SSD_H, SSD_P, SSD_G, SSD_N, SSD_L = 16, 64, 2, 128, 128
SSD_INNER = SSD_H * SSD_P
CONV_K, CONV_CH = 4, 1536
MLA_H, QR, KVR, NOPE, ROPE, VD = 16, 384, 256, 64, 32, 64
QK = NOPE + ROPE
ROPE_THETA = 10000.0
XA_H, XA_D = 4, 256
EPS = 1e-6
FFN_RES = 0.5
LR, B1, B2, AEPS, WD, STEP = 0.001, 0.9, 0.999, 1e-08, 0.01, 10

VMEM_LIMIT = 56 * 2**20


def _cp(*sem):
    return pltpu.CompilerParams(dimension_semantics=sem, vmem_limit_bytes=VMEM_LIMIT)


def _sigmoid(x):
    return 1.0 / (1.0 + jnp.exp(-x))


def _softplus(x):
    return jnp.where(x > 20.0, x, jnp.log(1.0 + jnp.exp(jnp.minimum(x, 20.0))))


def _dot(a, b, dims="nn"):
    ca = 0 if dims[0] == "t" else 1
    cb = 1 if dims[1] == "t" else 0
    return lax.dot_general(a.astype(BF16), b.astype(BF16), (((ca,), (cb,)), ((), ())), preferred_element_type=F32)


def _dot_sel(a, b, dims="nn", split="a", terms=3):
    r = (a if split == "a" else b).astype(F32)
    out = None
    for t in range(terms):
        piece = r.astype(BF16)
        if t + 1 < terms:
            r = r - piece.astype(F32)
        d = _dot(piece, b, dims) if split == "a" else _dot(a, piece, dims)
        out = d if out is None else out + d
    return out


def _ssd_common(dtr, dtb, alog):
    L = dtr.shape[0]
    dt = _softplus(dtr + dtb)
    a = -jnp.exp(alog)
    adt = dt * a
    r = lax.broadcasted_iota(jnp.int32, (L, L), 0)
    c = lax.broadcasted_iota(jnp.int32, (L, L), 1)
    lower = r >= c
    tri = lower.astype(F32)
    cs = _dot_sel(tri, adt, "nn", split="b")
    cs_t = _dot_sel(adt, tri, "tt")
    return dt, a, cs, cs_t, lower


def _head_expand():
    hh = lax.broadcasted_iota(jnp.int32, (SSD_H, SSD_INNER), 0)
    jj = lax.broadcasted_iota(jnp.int32, (SSD_H, SSD_INNER), 1)
    return ((jj >= hh * SSD_P) & (jj < hh * SSD_P + SSD_P)).astype(F32)


def _head_reduce():
    hh = lax.broadcasted_iota(jnp.int32, (SSD_INNER, SSD_H), 1)
    jj = lax.broadcasted_iota(jnp.int32, (SSD_INNER, SSD_H), 0)
    return ((jj >= hh * SSD_P) & (jj < hh * SSD_P + SSD_P)).astype(F32)


def ssd_fwd(xbc, dtr, dtb, alog, dsk, nseq, comm=None):
    T = xbc.shape[0]
    S = T // nseq
    C = S // SSD_L
    L = SSD_L
    NP = SSD_H // 2

    def body(x_ref, b_ref, c_ref, dtr_ref, dtb_ref, alog_ref, dsk_ref, y_ref, prev_ref, st_ref):
        ci = pl.program_id(1)

        @pl.when(ci == 0)
        def _():
            st_ref[...] = jnp.zeros_like(st_ref)

        dt, a, cs, cs_t, lower = _ssd_common(dtr_ref[:, 0:SSD_H], dtb_ref[...], alog_ref[...])
        E = _head_expand()
        X = x_ref[...].astype(F32)
        dt_e = _dot_sel(dt, E)
        cs_e = _dot_sel(cs, E)
        csl_e = cs_e[L - 1:L, :]
        Xd = X * dt_e
        Xf = Xd * jnp.exp(csl_e - cs_e)
        e_e = jnp.exp(cs_e)
        skip = _dot_sel(dsk_ref[...], E) * X
        lane = lax.broadcasted_iota(jnp.int32, (1, 2 * SSD_P), 1)
        rowp = lax.broadcasted_iota(jnp.int32, (2 * SSD_P, 1), 0)
        for g in range(SSD_G):
            Bg = b_ref[:, g * SSD_N:(g + 1) * SSD_N]
            Cg = c_ref[:, g * SSD_N:(g + 1) * SSD_N]
            cb = _dot(Cg, Bg, "nt")
            for pp in range(NP // SSD_G):
                p = g * (NP // SSD_G) + pp
                sl = slice(p * 2 * SSD_P, (p + 1) * 2 * SSD_P)
                Xd_p = Xd[:, sl]
                yd = jnp.zeros((L, 2 * SSD_P), F32)
                for q in range(2):
                    h = 2 * p + q
                    m = jnp.where(lower, jnp.exp(jnp.minimum(cs[:, h:h + 1] - cs_t[h:h + 1, :], 0.0)), 0.0)
                    mask = (lane >= q * SSD_P) & (lane < (q + 1) * SSD_P)
                    yd = yd + _dot(cb * m, jnp.where(mask, Xd_p, 0.0))
                S0 = st_ref[p]
                prev_ref[0, 0, p] = S0
                z = _dot(Cg, S0, "nt")
                y_ref[:, sl] = (skip[:, sl] + yd + z * e_e[:, sl]).astype(y_ref.dtype)
                h0 = 2 * p
                dec = jnp.where(rowp < SSD_P, jnp.exp(cs[L - 1:L, h0:h0 + 1]), jnp.exp(cs[L - 1:L, h0 + 1:h0 + 2]))
                st_ref[p] = S0 * dec + _dot(Xf[:, sl], Bg, "tn")

    row = lambda b, c: (b * C + c, 0)
    small = pl.BlockSpec((1, SSD_H), lambda b, c: (0, 0))
    return _call_with_comm(
        body, (nseq, C), "ssd_fwd",
        [pl.BlockSpec((L, SSD_INNER), row),
         pl.BlockSpec((L, SSD_G * SSD_N), lambda b, c: (b * C + c, SSD_INNER // (SSD_G * SSD_N))),
         pl.BlockSpec((L, SSD_G * SSD_N), lambda b, c: (b * C + c, SSD_INNER // (SSD_G * SSD_N) + 1)),
         pl.BlockSpec((L, 128), row), small, small, small],
        [xbc, xbc, xbc, dtr, dtb, alog, dsk],
        [pl.BlockSpec((L, SSD_INNER), row), pl.BlockSpec((1, 1, NP, 2 * SSD_P, SSD_N), lambda b, c: (b, c, 0, 0, 0))],
        [jax.ShapeDtypeStruct((T, SSD_INNER), BF16), jax.ShapeDtypeStruct((nseq, C, NP, 2 * SSD_P, SSD_N), F32)],
        comm, scratch=[pltpu.VMEM((NP, 2 * SSD_P, SSD_N), F32)], sem=("parallel", "arbitrary"))


def ssd_bwd(xbc, dtr, dtb, alog, dsk, prev, dy, nseq, comm=None):
    T = xbc.shape[0]
    S = T // nseq
    C = S // SSD_L
    L = SSD_L
    NP = SSD_H // 2

    def body(x_ref, b_ref, c_ref, dtr_ref, dtb_ref, alog_ref, dsk_ref, prev_ref, dy_ref,
             dxbc_ref, ddtr_ref, ddtb_ref, dalog_ref, ddsk_ref, ds_ref, stg_ref):
        bi = pl.program_id(0)
        ci = pl.program_id(1)

        @pl.when(ci == 0)
        def _():
            ds_ref[...] = jnp.zeros_like(ds_ref)

        @pl.when((ci == 0) & (bi == 0))
        def _():
            ddtb_ref[...] = jnp.zeros_like(ddtb_ref)
            dalog_ref[...] = jnp.zeros_like(dalog_ref)
            ddsk_ref[...] = jnp.zeros_like(ddsk_ref)

        dtr = dtr_ref[:, 0:SSD_H]
        dtb = dtb_ref[...]
        dt, a, cs, cs_t, lower = _ssd_common(dtr, dtb, alog_ref[...])
        upper = lax.broadcasted_iota(jnp.int32, (L, L), 1) >= lax.broadcasted_iota(jnp.int32, (L, L), 0)
        E = _head_expand()
        ET = _head_reduce()
        X = x_ref[...].astype(F32)
        dY = dy_ref[...].astype(F32)
        dt_e = _dot_sel(dt, E)
        cs_e = _dot_sel(cs, E)
        csl_e = cs_e[L - 1:L, :]
        f_e = jnp.exp(csl_e - cs_e)
        e_e = jnp.exp(cs_e)
        dsk_e = _dot_sel(dsk_ref[...], E)
        Xd = X * dt_e
        Xf = Xd * f_e
        lane = lax.broadcasted_iota(jnp.int32, (1, 2 * SSD_P), 1)
        rowp = lax.broadcasted_iota(jnp.int32, (2 * SSD_P, 1), 0)
        hsel = lax.broadcasted_iota(jnp.int32, (1, SSD_H), 1)
        dcs = jnp.zeros((L, SSD_H), F32)
        dcsl = jnp.zeros((1, SSD_H), F32)
        for g in range(SSD_G):
            Bg = b_ref[:, g * SSD_N:(g + 1) * SSD_N]
            Cg = c_ref[:, g * SSD_N:(g + 1) * SSD_N]
            cb = _dot(Cg, Bg, "nt")
            cbt = _dot(Bg, Cg, "nt")
            dB = jnp.zeros((L, SSD_N), F32)
            dC = jnp.zeros((L, SSD_N), F32)
            for pp in range(NP // SSD_G):
                p = g * (NP // SSD_G) + pp
                sl = slice(p * 2 * SSD_P, (p + 1) * 2 * SSD_P)
                Xd_p = Xd[:, sl]
                dY_p = dY[:, sl]
                dXd_p = jnp.zeros((L, 2 * SSD_P), F32)
                for q in range(2):
                    h = 2 * p + q
                    mask = (lane >= q * SSD_P) & (lane < (q + 1) * SSD_P)
                    col = cs[:, h:h + 1]
                    rw = cs_t[h:h + 1, :]
                    m = jnp.where(lower, jnp.exp(jnp.minimum(col - rw, 0.0)), 0.0)
                    mt = jnp.where(upper, jnp.exp(jnp.minimum(rw - col, 0.0)), 0.0)
                    dYm = jnp.where(mask, dY_p, 0.0)
                    dW = _dot(dYm, Xd_p, "nt")
                    dWt = _dot(Xd_p, dYm, "nt")
                    w = cb * m
                    wt = cbt * mt
                    dC = dC + _dot(dW * m, Bg)
                    dB = dB + _dot(dWt * mt, Cg)
                    dXd_p = dXd_p + jnp.where(mask, _dot(wt, dY_p), 0.0)
                    qcol = jnp.sum(dW * w, axis=1, keepdims=True) - jnp.sum(dWt * wt, axis=1, keepdims=True)
                    dcs = dcs + qcol * (hsel == h).astype(F32)
                S0 = prev_ref[0, 0, p]
                dSn = ds_ref[p]
                dZ = dY_p * e_e[:, sl]
                dC = dC + _dot(dZ, S0)
                h0 = 2 * p
                el0 = jnp.exp(cs[L - 1:L, h0:h0 + 1])
                el1 = jnp.exp(cs[L - 1:L, h0 + 1:h0 + 2])
                dec = jnp.where(rowp < SSD_P, el0, el1)
                ds_ref[p] = dSn * dec + _dot(dZ, Cg, "tn")
                dXf_p = _dot(Bg, dSn, "nt")
                dB = dB + _dot(Xf[:, sl], dSn)
                rs = jnp.sum(dSn * S0, axis=1, keepdims=True)
                s0 = jnp.sum(jnp.where(rowp < SSD_P, rs, 0.0), axis=0, keepdims=True) * el0
                s1 = jnp.sum(jnp.where(rowp >= SSD_P, rs, 0.0), axis=0, keepdims=True) * el1
                dcsl = dcsl + s0 * (hsel == h0).astype(F32) + s1 * (hsel == h0 + 1).astype(F32)
                y_off = _dot(Cg, S0, "nt") * e_e[:, sl]
                t1 = dY_p * y_off - dXf_p * Xf[:, sl]
                r1 = jnp.where(lane < SSD_P, t1, 0.0)
                c0 = jnp.sum(r1, axis=1, keepdims=True)
                c1 = jnp.sum(t1 - r1, axis=1, keepdims=True)
                dcs = dcs + c0 * (hsel == h0).astype(F32) + c1 * (hsel == h0 + 1).astype(F32)
                t2 = dXf_p * Xf[:, sl]
                r2 = jnp.where(lane < SSD_P, t2, 0.0)
                dcsl = dcsl + jnp.sum(r2, keepdims=True) * (hsel == h0).astype(F32) \
                    + jnp.sum(t2 - r2, keepdims=True) * (hsel == h0 + 1).astype(F32)
                stg_ref[:, sl] = dXd_p + dXf_p * f_e[:, sl]
            dxbc_ref[:, SSD_INNER + g * SSD_N:SSD_INNER + (g + 1) * SSD_N] = dB.astype(dxbc_ref.dtype)
            dxbc_ref[:, SSD_INNER + (SSD_G + g) * SSD_N:SSD_INNER + (SSD_G + g + 1) * SSD_N] = dC.astype(dxbc_ref.dtype)
        dXd = stg_ref[...]
        dxbc_ref[:, 0:SSD_INNER] = (dXd * dt_e + dsk_e * dY).astype(dxbc_ref.dtype)
        rowl = lax.broadcasted_iota(jnp.int32, (L, 1), 0)
        dcs = dcs + jnp.where(rowl == L - 1, dcsl, 0.0)
        dalpha = _dot_sel(upper.astype(F32), dcs, split="b")
        ddt = _dot_sel(dXd * X, ET, terms=2) + dalpha * a
        dalog_ref[...] += jnp.sum(dalpha * dt, axis=0, keepdims=True) * a
        ddtr = ddt * _sigmoid(dtr + dtb)
        spread = (lax.broadcasted_iota(jnp.int32, (SSD_H, 128), 0) == lax.broadcasted_iota(jnp.int32, (SSD_H, 128), 1)).astype(F32)
        ddtr_ref[...] = _dot(ddtr, spread).astype(ddtr_ref.dtype)
        ddtb_ref[...] += jnp.sum(ddtr, axis=0, keepdims=True)
        ddsk_ref[...] += jnp.sum(_dot_sel(dY * X, ET, terms=2), axis=0, keepdims=True)

    rowr = lambda b, c: (b * C + (C - 1 - c), 0)
    small = pl.BlockSpec((1, SSD_H), lambda b, c: (0, 0))
    return _call_with_comm(
        body, (nseq, C), "ssd_bwd",
        [pl.BlockSpec((L, SSD_INNER), rowr),
         pl.BlockSpec((L, SSD_G * SSD_N), lambda b, c: (b * C + (C - 1 - c), SSD_INNER // (SSD_G * SSD_N))),
         pl.BlockSpec((L, SSD_G * SSD_N), lambda b, c: (b * C + (C - 1 - c), SSD_INNER // (SSD_G * SSD_N) + 1)),
         pl.BlockSpec((L, 128), rowr), small, small, small,
         pl.BlockSpec((1, 1, NP, 2 * SSD_P, SSD_N), lambda b, c: (b, C - 1 - c, 0, 0, 0)),
         pl.BlockSpec((L, SSD_INNER), rowr)],
        [xbc, xbc, xbc, dtr, dtb, alog, dsk, prev, dy],
        [pl.BlockSpec((L, CONV_CH), rowr), pl.BlockSpec((L, 128), rowr), small, small, small],
        [jax.ShapeDtypeStruct((T, CONV_CH), BF16), jax.ShapeDtypeStruct((T, 128), BF16),
         jax.ShapeDtypeStruct((1, SSD_H), F32), jax.ShapeDtypeStruct((1, SSD_H), F32), jax.ShapeDtypeStruct((1, SSD_H), F32)],
        comm, scratch=[pltpu.VMEM((NP, 2 * SSD_P, SSD_N), F32), pltpu.VMEM((L, SSD_INNER), F32)], sem=("arbitrary", "arbitrary"))


SLOT = 128
ATT_T = 512
ATT_HP = 1
LOG2E = math.log2(math.e)
Q_SCALE = QK ** -0.5 * LOG2E


def _col_to_row(col):
    n = col.shape[0]
    eye = lax.broadcasted_iota(jnp.int32, (n, n), 0) == lax.broadcasted_iota(jnp.int32, (n, n), 1)
    return jnp.sum(jnp.where(eye, col, 0.0), axis=0, keepdims=True)


def attn_slot_fwd(q, k, v, nseq, comm=None):
    T = q.shape[0]
    S = T // nseq
    t = min(ATT_T, S)
    nb = S // t
    cols = [slice(h * SLOT, (h + 1) * SLOT) for h in range(ATT_HP)]

    def body(q_ref, k_ref, v_ref, o_ref, lse_ref):
        causal = lax.broadcasted_iota(jnp.int32, (t, t), 1) <= lax.broadcasted_iota(jnp.int32, (t, t), 0)
        for qi in range(nb):
            rows = slice(qi * t, (qi + 1) * t)
            state = [None] * ATT_HP
            for kj in range(qi + 1):
                keys = slice(kj * t, (kj + 1) * t)
                for h, c in enumerate(cols):
                    s = _dot(q_ref[rows, c], k_ref[keys, c], "nt")
                    if kj == qi:
                        s = jnp.where(causal, s, -1e30)
                    bm = jnp.max(s, axis=1, keepdims=True)
                    if kj == 0:
                        p = jnp.exp2(s - bm)
                        state[h] = (bm, jnp.sum(p, axis=1, keepdims=True), _dot(p, v_ref[keys, c]))
                    else:
                        m, l, acc = state[h]
                        m_new = jnp.maximum(m, bm)
                        corr = jnp.exp2(m - m_new)
                        p = jnp.exp2(s - m_new)
                        state[h] = (m_new, l * corr + jnp.sum(p, axis=1, keepdims=True), acc * corr + _dot(p, v_ref[keys, c]))
            for h, c in enumerate(cols):
                m, l, acc = state[h]
                o_ref[rows, c] = (acc / l).astype(o_ref.dtype)
                lse_ref[0, h, :, rows] = _col_to_row(m + jnp.log2(l))

    blk = pl.BlockSpec((S, ATT_HP * SLOT), lambda b, h: (b, h))
    return _call_with_comm(
        body, (nseq, MLA_H // ATT_HP), "attn_fwd", [blk, blk, blk], [q, k, v],
        [blk, pl.BlockSpec((1, ATT_HP, 1, S), lambda b, h: (b, h, 0, 0))],
        [jax.ShapeDtypeStruct((T, MLA_H * SLOT), BF16), jax.ShapeDtypeStruct((nseq, MLA_H, 1, S), F32)], comm)


def attn_slot_bwd(q, k, v, o, lse, do, nseq, comm=None):
    T = q.shape[0]
    S = T // nseq
    t = min(ATT_T, S)
    nb = S // t
    scale = QK ** -0.5
    cols = [slice(h * SLOT, (h + 1) * SLOT) for h in range(ATT_HP)]

    def body(q_ref, k_ref, v_ref, o_ref, lse_ref, do_ref, dq_ref, dk_ref, dv_ref, dqa_ref):
        causal_t = lax.broadcasted_iota(jnp.int32, (t, t), 0) <= lax.broadcasted_iota(jnp.int32, (t, t), 1)
        ones = jnp.ones((8, SLOT), F32)
        delta = {}
        for qi in range(nb):
            sl = slice(qi * t, (qi + 1) * t)
            for h, c in enumerate(cols):
                prod = do_ref[sl, c].astype(F32) * o_ref[sl, c].astype(F32)
                delta[h, qi] = _dot_sel(ones, prod, "nt", split="b", terms=2)[0:1, :]
        for kj in range(nb):
            ks = slice(kj * t, (kj + 1) * t)
            dk = [None] * ATT_HP
            dv = [None] * ATT_HP
            for qi in range(kj, nb):
                sl = slice(qi * t, (qi + 1) * t)
                for h, c in enumerate(cols):
                    kb, vb, qb, dob = k_ref[ks, c], v_ref[ks, c], q_ref[sl, c], do_ref[sl, c]
                    st = _dot(kb, qb, "nt")
                    pt = jnp.exp2(st - lse_ref[0, h, :, sl])
                    if qi == kj:
                        pt = jnp.where(causal_t, pt, 0.0)
                    dpt = _dot(vb, dob, "nt")
                    dst = (pt * (dpt - delta[h, qi])).astype(BF16)
                    dvc = _dot(pt, dob)
                    dkc = _dot(dst, qb) * (1.0 / LOG2E)
                    dv[h] = dvc if dv[h] is None else dv[h] + dvc
                    dk[h] = dkc if dk[h] is None else dk[h] + dkc
                    dqc = _dot(dst, kb, "tn") * scale
                    if kj > 0:
                        dqc = dqc + dqa_ref[sl, c]
                    if qi == kj:
                        dq_ref[sl, c] = dqc.astype(dq_ref.dtype)
                    else:
                        dqa_ref[sl, c] = dqc
            for h, c in enumerate(cols):
                dk_ref[ks, c] = dk[h].astype(dk_ref.dtype)
                dv_ref[ks, c] = dv[h].astype(dv_ref.dtype)

    blk = pl.BlockSpec((S, ATT_HP * SLOT), lambda b, h: (b, h))
    lse_spec = pl.BlockSpec((1, ATT_HP, 1, S), lambda b, h: (b, h, 0, 0))
    W = MLA_H * SLOT
    return _call_with_comm(
        body, (nseq, MLA_H // ATT_HP), "attn_bwd", [blk, blk, blk, blk, lse_spec, blk], [q, k, v, o, lse, do], [blk, blk, blk],
        [jax.ShapeDtypeStruct((T, W), BF16)] * 3, comm, scratch=[pltpu.VMEM((S, ATT_HP * SLOT), F32)])


def _rope_coeffs(pos, inv):
    half = ROPE // 2
    ang = pos * inv
    lane = lax.broadcasted_iota(jnp.int32, (1, SLOT), 1)
    sn = jnp.sin(ang)
    C = jnp.where(lane < NOPE, 1.0, jnp.where(lane < QK, jnp.cos(ang), 0.0))
    Sg = jnp.where((lane >= NOPE) & (lane < NOPE + half), -sn, jnp.where((lane >= NOPE + half) & (lane < QK), sn, 0.0))
    return C, Sg


def _rope_inputs(positions):
    half = ROPE // 2
    inv = ROPE_THETA ** (-jnp.arange(0, ROPE, 2, dtype=F32) / ROPE)
    row = jnp.zeros((1, SLOT), F32).at[0, NOPE:NOPE + half].set(inv).at[0, NOPE + half:QK].set(inv)
    return positions.astype(F32).reshape(-1, 1), row


def _place_k_rope(kr_lanes):
    r = lax.broadcasted_iota(jnp.int32, (SLOT, SLOT), 0)
    c = lax.broadcasted_iota(jnp.int32, (SLOT, SLOT), 1)
    return _dot_sel(kr_lanes, ((c == r + NOPE) & (r < ROPE)).astype(F32))


def rope_table(pos, inv):
    return rowwise(_rope_coeffs, [pos], [inv], [(SLOT, F32), (SLOT, F32)], [], "rope_table")


def rope_q_epilogue(accs, ex):
    C, Sg = ex[0], ex[1]
    reps = accs[0].shape[1] // SLOT
    return ((accs[0] * jnp.tile(C, (1, reps)) + _rope_swap(accs[0]) * jnp.tile(Sg, (1, reps))) * Q_SCALE,)


def rope_k_epilogue(accs, ex):
    C, Sg = ex[0], ex[1]
    kr = _place_k_rope(ex[2][:, SLOT:2 * SLOT])
    kr = kr * C + _rope_swap(kr) * Sg
    return (accs[0] + jnp.tile(kr, (1, accs[0].shape[1] // SLOT)),)


def _rope_swap(x):
    W = x.shape[1]
    half = ROPE // 2
    lane = lax.broadcasted_iota(jnp.int32, (1, W), 1) & (SLOT - 1)
    up = pltpu.roll(x, W - half, axis=1)
    dn = pltpu.roll(x, half, axis=1)
    return jnp.where((lane >= NOPE) & (lane < NOPE + half), up, jnp.where((lane >= NOPE + half) & (lane < QK), dn, 0.0))


def rope_slot_bwd(dq, dk, C, Sg, name):
    def fn(dqv, dkv, C, Sg):
        ct, stl = jnp.tile(C, (1, MLA_H)), jnp.tile(Sg, (1, MLA_H))
        dqo = dqv * ct - _rope_swap(dqv) * stl
        tot = dkv[:, 0:SLOT]
        for h in range(1, MLA_H):
            tot = tot + dkv[:, h * SLOT:(h + 1) * SLOT]
        u = tot * C - _rope_swap(tot) * Sg
        r = lax.broadcasted_iota(jnp.int32, (SLOT, SLOT), 0)
        c = lax.broadcasted_iota(jnp.int32, (SLOT, SLOT), 1)
        unplace = ((r == c + NOPE) & (c < ROPE)).astype(F32)
        return dqo, dkv, _dot_sel(u, unplace, terms=2)
    W = MLA_H * SLOT
    return rowwise(fn, [dq, dk, C, Sg], [], [(W, BF16), (W, BF16), (SLOT, BF16)], [], name)


XA_BLK = 512


def xattn_fwd(q, k, v, nseq, comm=None):
    T = q.shape[0]
    S = T // nseq
    M = k.shape[0] // nseq
    tq = min(XA_BLK, S)
    nq = S // tq
    scale = XA_D ** -0.5

    def body(q_ref, k_ref, v_ref, o_ref):
        s = _dot(q_ref[...], k_ref[...], "nt") * scale
        p = jnp.exp(s - jnp.max(s, axis=1, keepdims=True))
        p = p / jnp.sum(p, axis=1, keepdims=True)
        o_ref[...] = _dot(p, v_ref[...]).astype(o_ref.dtype)

    qs = pl.BlockSpec((tq, XA_D), lambda b, h, i: (b * nq + i, h))
    ks = pl.BlockSpec((M, XA_D), lambda b, h, i: (b, h))
    return _call_with_comm(body, (nseq, XA_H, nq), "xattn_fwd", [qs, ks, ks], [q, k, v], [qs],
                           [jax.ShapeDtypeStruct((T, XA_H * XA_D), BF16)], comm)


def xattn_bwd(q, k, v, do, nseq):
    T = q.shape[0]
    S = T // nseq
    M = k.shape[0] // nseq
    tq = min(XA_BLK, S)
    nq = S // tq
    scale = XA_D ** -0.5

    def body(q_ref, k_ref, v_ref, do_ref, dq_ref, dk_ref, dv_ref):
        @pl.when(pl.program_id(2) == 0)
        def _():
            dk_ref[...] = jnp.zeros_like(dk_ref)
            dv_ref[...] = jnp.zeros_like(dv_ref)

        qb, kb, vb, dob = q_ref[...], k_ref[...], v_ref[...], do_ref[...]
        s = _dot(qb, kb, "nt") * scale
        p = jnp.exp(s - jnp.max(s, axis=1, keepdims=True))
        p = p / jnp.sum(p, axis=1, keepdims=True)
        dp = _dot(dob, vb, "nt")
        ds = p * (dp - jnp.sum(dp * p, axis=1, keepdims=True)) * scale
        dq_ref[...] = _dot(ds, kb).astype(dq_ref.dtype)
        dk_ref[...] += _dot(ds, qb, "tn")
        dv_ref[...] += _dot(p, dob, "tn")

    qs = pl.BlockSpec((tq, XA_D), lambda b, h, i: (b * nq + i, h))
    ks = pl.BlockSpec((M, XA_D), lambda b, h, i: (b, h))
    return pl.pallas_call(
        body, grid=(nseq, XA_H, nq), name="xattn_bwd", in_specs=[qs, ks, ks, qs], out_specs=[qs, ks, ks],
        out_shape=[jax.ShapeDtypeStruct((T, XA_H * XA_D), BF16), jax.ShapeDtypeStruct(k.shape, F32),
                   jax.ShapeDtypeStruct(k.shape, F32)],
        compiler_params=_cp("parallel", "parallel", "arbitrary"),
    )(q, k, v, do)


CONV_BLK = 256


def _shift_down(x, s, rows):
    if s == 0:
        return x
    return jnp.where(rows >= s, pltpu.roll(x, s, axis=0), 0.0)


def _shift_up(x, s, rows):
    if s == 0:
        return x
    S = x.shape[0]
    return jnp.where(rows < S - s, pltpu.roll(x, S - s, axis=0), 0.0)


def conv_fwd(x, w, b, nseq):
    T, CH = x.shape
    S = T // nseq

    def body(x_ref, w_ref, b_ref, o_ref):
        xv = x_ref[...].astype(F32)
        rows = lax.broadcasted_iota(jnp.int32, (S, 1), 0)
        c = jnp.zeros_like(xv) + b_ref[...]
        for kk in range(CONV_K):
            c = c + w_ref[kk:kk + 1, :] * _shift_down(xv, CONV_K - 1 - kk, rows)
        o_ref[...] = (c * _sigmoid(c)).astype(o_ref.dtype)

    xs = pl.BlockSpec((S, CONV_BLK), lambda j, bb: (bb, j))
    return pl.pallas_call(
        body, grid=(CH // CONV_BLK, nseq), name="conv_fwd",
        in_specs=[xs, pl.BlockSpec((CONV_K, CONV_BLK), lambda j, bb: (0, j)), pl.BlockSpec((1, CONV_BLK), lambda j, bb: (0, j))],
        out_specs=xs, out_shape=jax.ShapeDtypeStruct((T, CH), BF16),
        compiler_params=_cp("parallel", "parallel"),
    )(x, w, b)


def conv_bwd(x, w, b, dout, nseq):
    T, CH = x.shape
    S = T // nseq

    def body(x_ref, w_ref, b_ref, do_ref, dx_ref, dw_ref, db_ref):
        @pl.when(pl.program_id(1) == 0)
        def _():
            dw_ref[...] = jnp.zeros_like(dw_ref)
            db_ref[...] = jnp.zeros_like(db_ref)

        xv = x_ref[...].astype(F32)
        rows = lax.broadcasted_iota(jnp.int32, (S, 1), 0)
        c = jnp.zeros_like(xv) + b_ref[...]
        sh = [_shift_down(xv, CONV_K - 1 - kk, rows) for kk in range(CONV_K)]
        for kk in range(CONV_K):
            c = c + w_ref[kk:kk + 1, :] * sh[kk]
        sg = _sigmoid(c)
        dc = do_ref[...].astype(F32) * sg * (1.0 + c * (1.0 - sg))
        dx = jnp.zeros_like(xv)
        for kk in range(CONV_K):
            dx = dx + w_ref[kk:kk + 1, :] * _shift_up(dc, CONV_K - 1 - kk, rows)
            dw_ref[kk:kk + 1, :] += jnp.sum(dc * sh[kk], axis=0, keepdims=True)
        dx_ref[...] = dx.astype(dx_ref.dtype)
        db_ref[...] += jnp.sum(dc, axis=0, keepdims=True)

    xs = pl.BlockSpec((S, CONV_BLK), lambda j, bb: (bb, j))
    ws = pl.BlockSpec((CONV_K, CONV_BLK), lambda j, bb: (0, j))
    bs = pl.BlockSpec((1, CONV_BLK), lambda j, bb: (0, j))
    return pl.pallas_call(
        body, grid=(CH // CONV_BLK, nseq), name="conv_bwd",
        in_specs=[xs, ws, bs, xs], out_specs=[xs, ws, bs],
        out_shape=[jax.ShapeDtypeStruct((T, CH), BF16), jax.ShapeDtypeStruct((CONV_K, CH), F32),
                   jax.ShapeDtypeStruct((1, CH), F32)],
        compiler_params=_cp("parallel", "arbitrary"),
    )(x, w, b, dout)


def _dims(a, b, mode):
    M = a.shape[1] if mode[0] == "t" else a.shape[0]
    K = a.shape[0] if mode[0] == "t" else a.shape[1]
    N = b.shape[0] if mode[1] == "t" else b.shape[1]
    return M, K, N


def _tile(dim, prefs):
    for p in prefs:
        if dim % p == 0:
            return p
    return dim


def mm(groups, out_dtypes, name, tm=None, tn=None, tk=None, epi=None, extras=(), comm=None, sub=1, n_sum=0):
    a0, b0, m0 = groups[0][0]
    M, K0, N = _dims(a0, b0, m0)
    tm = tm or _tile(M, (1024, 512, 256, 128))
    tn = tn or _tile(N, (1024, 512, 256, 128))
    flat = [p for g in groups for p in g]
    nk = 1 if tk is None else K0 // tk
    in_specs, args = [], []
    for a, b, mode in flat:
        _, K, _ = _dims(a, b, mode)
        kb = K if tk is None else tk
        in_specs.append(pl.BlockSpec((kb, tm), lambda i, j, k: (k, i)) if mode[0] == "t"
                        else pl.BlockSpec((tm, kb), lambda i, j, k: (i, k)))
        in_specs.append(pl.BlockSpec((tn, kb), lambda i, j, k: (j, k)) if mode[1] == "t"
                        else pl.BlockSpec((kb, tn), lambda i, j, k: (k, j)))
        args += [a, b]
    kinds = []
    for e in extras:
        kind, e = e if isinstance(e, tuple) else ("vec" if e.shape[0] == 1 and M != 1 else "tile", e)
        in_specs.append({"tile": pl.BlockSpec((tm, tn), lambda i, j, k: (i, j)),
                         "vec": pl.BlockSpec((1, tn), lambda i, j, k: (0, j)),
                         "rows": pl.BlockSpec((tm, e.shape[1]), lambda i, j, k: (i, 0)),
                         "whole": pl.BlockSpec(e.shape, lambda i, j, k: (0, 0))}[kind])
        kinds.append(kind)
        args.append(e)
    n_in = len(args)
    n_main = len(out_dtypes)
    n_out = n_main + n_sum
    assert n_sum == 0 or (tn == N and tk is None)
    ng = len(groups)
    sizes = [len(g) for g in groups]

    def body(*refs):
        ins, outs, accs = refs[:n_in], refs[n_in:n_in + n_out], refs[n_in + n_out:]
        kk = pl.program_id(2)

        def dots(rs):
            vals, pos = [], 0
            for gi in range(ng):
                acc = None
                for _ in range(sizes[gi]):
                    mode = flat[pos // 2][2]
                    av = ins[pos][:, rs] if mode[0] == "t" else ins[pos][rs, :]
                    d = _dot(av, ins[pos + 1][...], mode)
                    acc = d if acc is None else acc + d
                    pos += 2
                vals.append(acc)
            return vals

        def finish(accv, rs, first_chunk=True):
            ex = [(r[rs, :] if kind in ("tile", "rows") else r[...]).astype(F32) for kind, r in zip(kinds, ins[2 * len(flat):])]
            res = epi(accv, ex) if epi is not None else tuple(accv)
            for o, r in zip(outs[:n_main], res[:n_main]):
                o[rs, :] = r.astype(o.dtype)
            for o, r in zip(outs[n_main:], res[n_main:]):
                if first_chunk:
                    @pl.when(pl.program_id(0) == 0)
                    def _():
                        o[...] = r

                    @pl.when(pl.program_id(0) > 0)
                    def _():
                        o[...] += r
                else:
                    o[...] += r

        if nk == 1:
            for r in range(sub):
                rs = slice(r * (tm // sub), (r + 1) * (tm // sub))
                finish(dots(rs), rs, r == 0)
        else:
            vals = dots(slice(0, tm))
            finish = functools.partial(finish, rs=slice(0, tm))
            @pl.when(kk == 0)
            def _():
                for ar, vv in zip(accs, vals):
                    ar[...] = vv

            @pl.when(kk > 0)
            def _():
                for ar, vv in zip(accs, vals):
                    ar[...] += vv

            @pl.when(kk == nk - 1)
            def _():
                finish([ar[...] for ar in accs])

    grid = (M // tm, N // tn, nk)
    out_specs = [pl.BlockSpec((tm, tn), lambda i, j, k: (i, j)) for _ in out_dtypes] \
        + [pl.BlockSpec((1, tn), lambda i, j, k: (0, j))] * n_sum
    out_shape = [jax.ShapeDtypeStruct((M, N), dt) for dt in out_dtypes] + [jax.ShapeDtypeStruct((1, N), F32)] * n_sum
    scratch = [pltpu.VMEM((tm, tn), F32) for _ in range(ng if nk > 1 else 0)]
    sem = ("arbitrary" if n_sum else "parallel", "parallel", "arbitrary")
    if comm is not None:
        body = _attach(comm, body, n_in, n_out, *_grid_ends(grid))
        in_specs, args = in_specs + [HBM_SPEC] * len(comm.inputs), args + comm.inputs
        out_specs, out_shape = out_specs + [HBM_SPEC] * len(comm.out_shapes), out_shape + comm.out_shapes
        scratch, sem = scratch + comm.sems, ("arbitrary",) * 3
    return pl.pallas_call(body, grid=grid, name=name, in_specs=in_specs, out_specs=out_specs, out_shape=out_shape,
                          scratch_shapes=scratch, compiler_params=_cp(*sem))(*args)


def mm1(a, b, mode, out_dtype, name, **kw):
    return mm([[(a, b, mode)]], [out_dtype], name, **kw)[0]


ROW_BLK = 512


def rowwise(fn, rows, consts, outs, accs, name, tb=ROW_BLK, comm=None):
    rows = [r if isinstance(r, tuple) else (r, r.shape[1], 0) for r in rows]
    T = rows[0][0].shape[0]
    tb = min(tb, T)
    n_r, n_c, n_o, n_a = len(rows), len(consts), len(outs), len(accs)

    def body(*refs):
        vals = [r[...].astype(F32) for r in refs[:n_r + n_c]]
        res = fn(*vals)
        o_refs = refs[n_r + n_c:n_r + n_c + n_o]
        a_refs = refs[n_r + n_c + n_o:]
        for o, r in zip(o_refs, res[:n_o]):
            o[...] = r.astype(o.dtype)
        if n_a:
            @pl.when(pl.program_id(0) == 0)
            def _():
                for ar in a_refs:
                    ar[...] = jnp.zeros_like(ar)
            for ar, r in zip(a_refs, res[n_o:]):
                ar[...] += r

    return _call_with_comm(
        body, (T // tb,), name,
        [pl.BlockSpec((tb, w), functools.partial(lambda i, j: (i, j), j=j)) for _, w, j in rows]
        + [pl.BlockSpec(c.shape, lambda i: (0, 0)) for c in consts],
        [r[0] for r in rows] + list(consts),
        [pl.BlockSpec((tb, d), lambda i: (i, 0)) for d, _ in outs] + [pl.BlockSpec(s, lambda i: (0, 0)) for s in accs],
        [jax.ShapeDtypeStruct((T, d), dt) for d, dt in outs] + [jax.ShapeDtypeStruct(s, F32) for s in accs],
        comm, sem=("arbitrary" if n_a else "parallel",))


def _rms_stats(x):
    r = lax.rsqrt(jnp.mean(x * x, axis=-1, keepdims=True) + EPS)
    return r, x * r


def _rms_bwd(x, g, dy):
    r, xn = _rms_stats(x)
    dyg = dy * g
    dx = r * (dyg - xn * jnp.mean(dyg * xn, axis=-1, keepdims=True))
    return dx, jnp.sum(dy * xn, axis=0, keepdims=True)


def rms_fwd(x, g, name, comm=None):
    res = rowwise(lambda xv, gv: (_rms_stats(xv)[1] * gv,), [x], [g], [(x.shape[1], BF16)], [], name, comm=comm)
    return res[0] if comm is None else (res[0], res[1:])


def rms_bwd(x, g, dy, name, resid=None, dx_dtype=F32):
    def fn(*v):
        if resid is None:
            xv, dyv, gv = v
            dx, dg = _rms_bwd(xv, gv, dyv)
        else:
            xv, dyv, rv, gv = v
            dx, dg = _rms_bwd(xv, gv, dyv)
            dx = dx + rv
        return dx, dg
    rows = [x, dy] + ([] if resid is None else [resid])
    return rowwise(fn, rows, [g], [(x.shape[1], dx_dtype)], [(1, x.shape[1])], name)


def mm_rms_bwd(pairs, x, g, name, resid=None, dx_dtype=F32, comm=None, token=None):
    def epi(accs, ex):
        dx, dg = _rms_bwd(ex[0], ex[-1], accs[0])
        return (dx if resid is None else dx + ex[1]), dg
    extras = [x] + ([] if resid is None else [resid]) + ([] if token is None else [("whole", token)]) + [g]
    return mm([pairs], [dx_dtype], name, tm=min(256, x.shape[0]), tn=x.shape[1], epi=epi, extras=extras, comm=comm, n_sum=1)


def mm_resid(a, b, x, g, wgt, name, comm=None, target=None):
    def epi(accs, ex):
        y = ex[0] + wgt * _rms_stats(accs[0])[1] * ex[1]
        if target is None:
            return accs[0], y
        d = y - ex[2]
        return accs[0], d / D, jnp.sum(d * d, axis=0, keepdims=True)
    return mm([[(a, b, "nn")]], [F32, F32], name, tm=min(512, a.shape[0]), tn=b.shape[1], epi=epi,
              extras=[x, g] + ([] if target is None else [target]), sub=2, comm=comm, n_sum=0 if target is None else 1)


def resid_bwd(h, g, dy, wgt, name):
    def fn(hv, dyv, gv):
        dx, dg = _rms_bwd(hv, gv, dyv)
        return wgt * dx, wgt * dg
    return rowwise(fn, [h, dy], [g], [(h.shape[1], BF16)], [(1, h.shape[1])], name)


def _silu_parts(g):
    s = _sigmoid(g)
    return g * s, s * (1.0 + g * (1.0 - s))


def gated_norm_fwd(y, z, g, name):
    W = SSD_INNER // SSD_G

    def fn(yv, zv, gv):
        yg = yv * _silu_parts(zv)[0]
        return (jnp.concatenate([_rms_stats(yg[:, i * W:(i + 1) * W])[1] for i in range(SSD_G)], axis=1) * gv,)
    return rowwise(fn, [y, z], [g], [(SSD_INNER, BF16)], [], name)[0]


def gated_norm_bwd(y, z, dyn, g, name):
    W = SSD_INNER // SSD_G

    def fn(yv, zv, dv, gv):
        sil, dsil = _silu_parts(zv)
        yg = yv * sil
        parts = [_rms_bwd(yg[:, i * W:(i + 1) * W], gv[:, i * W:(i + 1) * W], dv[:, i * W:(i + 1) * W]) for i in range(SSD_G)]
        dyg = jnp.concatenate([p[0] for p in parts], axis=1)
        dg = jnp.concatenate([p[1] for p in parts], axis=1)
        return dyg * sil, dyg * yv * dsil, dg
    return rowwise(fn, [y, z, dyn], [g], [(SSD_INNER, BF16), (SSD_INNER, BF16)], [(1, SSD_INNER)], name)


def merge_fwd(gl, ys, ym, gb, name):
    def fn(glv, ysv, ymv, gbv):
        gt = _sigmoid(glv + gbv)
        return (gt[:, :D] * ysv + gt[:, D:] * ymv,)
    return rowwise(fn, [gl, ys, ym], [gb], [(D, BF16)], [], name)[0]


def merge_bwd(gl, ys, ym, dm, gb, name):
    def fn(glv, ysv, ymv, dmv, gbv):
        gt = _sigmoid(glv + gbv)
        gs, gm = gt[:, :D], gt[:, D:]
        dgl = jnp.concatenate([dmv * ysv * gs * (1.0 - gs), dmv * ymv * gm * (1.0 - gm)], axis=1)
        return dmv * gs, dmv * gm, dgl, jnp.sum(dgl, axis=0, keepdims=True)
    return rowwise(fn, [gl, ys, ym, dm], [gb], [(D, BF16), (D, BF16), (2 * D, BF16)], [(1, 2 * D)], name)


def _adamw_math(wv, gv, mv, vv):
    mn = B1 * mv + (1.0 - B1) * gv
    vn = B2 * vv + (1.0 - B2) * (gv * gv)
    mh = mn / (1.0 - B1 ** STEP)
    vh = vn / (1.0 - B2 ** STEP)
    return -LR * (mh / (jnp.sqrt(vh) + AEPS) + WD * wv), mn, vn


def adamw(w, g, m, v, name):
    R, C = w.shape
    tb = _tile(R, (256, 128, 64, 32, 16, 8))
    return rowwise(_adamw_math, [w, g, m, v], [], [(C, F32)] * 3, [], name, tb=tb)


def adamw_small(packed, ws, ms, vs):
    k = len(ws)
    sizes = [x.shape[1] for x in ws]

    def body(*refs):
        p_ref, w_refs, m_refs, v_refs = refs[0], refs[1:1 + k], refs[1 + k:1 + 2 * k], refs[1 + 2 * k:1 + 3 * k]
        outs = refs[1 + 3 * k:]
        r0 = 0
        for i, n in enumerate(sizes):
            nr = -(-n // 128)
            g = jnp.concatenate([p_ref[r0 + r:r0 + r + 1, :] for r in range(nr)], axis=1)[:, :n]
            r0 += nr
            outs[i][...] = g
            outs[k + i][...], outs[2 * k + i][...], outs[3 * k + i][...] = _adamw_math(w_refs[i][...], g, m_refs[i][...], v_refs[i][...])

    res = pl.pallas_call(body, name="adamw_small",
                         out_shape=[jax.ShapeDtypeStruct((1, n), F32) for _ in range(4) for n in sizes])(packed, *ws, *ms, *vs)
    return [res[j * k:(j + 1) * k] for j in range(4)]


def adamw_from_slots(recv, piece, w, m, v, name, token=None):
    K, n = w.shape
    ns = recv.shape[0]
    assert recv.shape[2] == n and recv.shape[1] % K == 0
    tb = _tile(K, (256, 176, 128, 64, 32, 16, 8)) if K % 8 == 0 else K
    r_spec = pl.BlockSpec((ns, tb, n), lambda i: (0, piece * (K // tb) + i, 0))
    w_spec = pl.BlockSpec((tb, n), lambda i: (i, 0))

    def body(r_ref, w_ref, m_ref, v_ref, *rest):
        g_ref, d_ref, mo_ref, vo_ref = rest[-4:]
        g = r_ref[0].astype(F32)
        for s in range(1, ns):
            g = g + r_ref[s].astype(F32)
        g_ref[...] = g
        d_ref[...], mo_ref[...], vo_ref[...] = _adamw_math(w_ref[...], g, m_ref[...], v_ref[...])

    extra = [] if token is None else [token]
    return pl.pallas_call(
        body, grid=(K // tb,), name=name,
        in_specs=[r_spec, w_spec, w_spec, w_spec] + [pl.BlockSpec(t.shape, lambda i: (0, 0)) for t in extra], out_specs=[w_spec] * 4,
        out_shape=[jax.ShapeDtypeStruct((K, n), F32)] * 4, compiler_params=_cp("parallel"),
    )(recv, w, m, v, *extra)


def _me():
    return lax.axis_index("x"), lax.axis_index("y"), lax.axis_index("c")


def _dev_index():
    x, y, c = _me()
    return 4 * x + 2 * y + c


HBM_SPEC = pl.BlockSpec(memory_space=pl.ANY)


class GatherComm:
    def __init__(self, shards):
        self.inputs = [s for s, _ in shards]
        self.rows = [list(r) for _, r in shards]
        n = len(shards)
        self.out_shapes = [jax.ShapeDtypeStruct((N_DEV, r, s.shape[1]), s.dtype) for s, rows in shards for r in rows]
        self.sems = [pltpu.SemaphoreType.DMA((7 * n,)), pltpu.SemaphoreType.DMA((7 * n,)), pltpu.SemaphoreType.DMA((n,))]

    def _plan(self, x_refs, out_refs, sems):
        send_sems, recv_sems, local_sems = sems
        x, y, c = _me()
        me, sibling = (x, y, c), (x, y, 1 - c)
        chips = [(1 - x, y), (x, 1 - y), (1 - x, 1 - y)]
        index = lambda px, py, pc: 4 * px + 2 * py + pc
        mine, first, passed, whole = [], [], [], []
        pos = 0
        for i, rows in enumerate(self.rows):
            kw = lambda k: dict(send_sem=send_sems.at[7 * i + k], recv_sem=recv_sems.at[7 * i + k], device_id_type=MESH)
            r0 = 0
            fwd = [[] for _ in chips]
            for j, nr in enumerate(rows):
                out, src = out_refs[pos + j], x_refs[i].at[pl.ds(r0, nr)]
                mine.append(pltpu.make_async_copy(src, out.at[index(*me)], local_sems.at[i]))
                first.append(pltpu.make_async_remote_copy(src_ref=src, dst_ref=out.at[index(*me)], device_id=sibling, **kw(0)))
                for jj, chip in enumerate(chips):
                    first.append(pltpu.make_async_remote_copy(src_ref=src, dst_ref=out.at[index(*me)], device_id=(*chip, c),
                                                              **kw(1 + jj)))
                    blk = out.at[index(*chip, c)]
                    fwd[jj].append(pltpu.make_async_remote_copy(src_ref=blk, dst_ref=blk, device_id=sibling, **kw(4 + jj)))
                r0 += nr
            passed.append(fwd)
            whole.append([pltpu.make_async_remote_copy(src_ref=x_refs[i], dst_ref=x_refs[i], device_id=sibling, **kw(k))
                          for k in range(7)])
            pos += len(rows)
        return mine, first, passed, whole

    def start(self, x_refs, out_refs, sems):
        mine, first, _, _ = self._plan(x_refs, out_refs, sems)
        for cp in mine + first:
            cp.start()

    def finish(self, x_refs, out_refs, sems):
        _, _, passed, whole = self._plan(x_refs, out_refs, sems)
        local_sems = sems[2]
        for i, fwd in enumerate(passed):
            for jj in range(3):
                whole[i][1 + jj].wait_recv()
                for cp in fwd[jj]:
                    cp.start()
        for i in range(len(passed)):
            whole[i][0].wait_recv()
            for jj in range(3):
                whole[i][4 + jj].wait_recv()
        for i in range(len(passed)):
            for k in range(7):
                whole[i][k].wait_send()
            pltpu.make_async_copy(x_refs[i], x_refs[i], local_sems.at[i]).wait()


def run_comm(comm, name):
    n_in, n_out = len(comm.inputs), len(comm.out_shapes)

    def body(*refs):
        ins, outs, sems = refs[:n_in], refs[n_in:n_in + n_out], refs[n_in + n_out:]
        comm.start(ins, outs, sems)
        comm.finish(ins, outs, sems)

    return pl.pallas_call(body, name=name, out_shape=comm.out_shapes, in_specs=[HBM_SPEC] * n_in,
                          out_specs=[HBM_SPEC] * n_out, scratch_shapes=comm.sems)(*comm.inputs)


def _attach(comm, body, n_in, n_out, first, last):
    if comm is None:
        return body
    ci, co, cs = len(comm.inputs), len(comm.out_shapes), len(comm.sems)

    def wrapped(*refs):
        h_in, c_in = refs[:n_in], refs[n_in:n_in + ci]
        h_out, c_out = refs[n_in + ci:n_in + ci + n_out], refs[n_in + ci + n_out:n_in + ci + n_out + co]
        rest = refs[n_in + ci + n_out + co:]
        h_scr, c_sem = rest[:len(rest) - cs], rest[len(rest) - cs:]

        @pl.when(first())
        def _():
            comm.start(c_in, c_out, c_sem)

        body(*h_in, *h_out, *h_scr)

        @pl.when(last())
        def _():
            comm.finish(c_in, c_out, c_sem)

    return wrapped


def _grid_ends(grid):
    first = lambda: functools.reduce(lambda a, b: a & b, [pl.program_id(i) == 0 for i in range(len(grid))])
    last = lambda: functools.reduce(lambda a, b: a & b, [pl.program_id(i) == g - 1 for i, g in enumerate(grid)])
    return first, last


def _call_with_comm(body, grid, name, in_specs, args, out_specs, out_shape, comm, scratch=(), sem=None):
    sem = sem or ("parallel",) * len(grid)
    scratch = list(scratch)
    if comm is not None:
        body = _attach(comm, body, len(args), len(out_shape), *_grid_ends(grid))
        in_specs, args = in_specs + [HBM_SPEC] * len(comm.inputs), args + comm.inputs
        out_specs, out_shape = out_specs + [HBM_SPEC] * len(comm.out_shapes), out_shape + comm.out_shapes
        scratch, sem = scratch + comm.sems, ("arbitrary",) * len(grid)
    return pl.pallas_call(body, grid=grid, name=name, in_specs=in_specs, out_specs=out_specs, out_shape=out_shape,
                          scratch_shapes=scratch, compiler_params=_cp(*sem))(*args)


class ScatterComm:
    def __init__(self, groups):
        self.sizes = [len(g) for g in groups]
        self.rows = [[pc.shape[1] for pc in g] for g in groups]
        ng = len(groups)
        self.inputs = [pc for g in groups for pc in g]
        self.out_shapes = [jax.ShapeDtypeStruct((N_DEV, sum(self.rows[gi]), g[0].shape[2]), g[0].dtype) for gi, g in enumerate(groups)]
        self.sems = [pltpu.SemaphoreType.DMA((7 * ng,)), pltpu.SemaphoreType.DMA((7 * ng,)), pltpu.SemaphoreType.DMA((ng,))]

    def _peers(self):
        x, y, c = _me()
        out = []
        for k in range(1, N_DEV):
            px = 1 - x if k & 4 else x
            py = 1 - y if k & 2 else y
            pc = 1 - c if k & 1 else c
            out.append((k, 4 * px + 2 * py + pc, dict(device_id=(px, py, pc), device_id_type=MESH)))
        return 4 * x + 2 * y + c, out

    def start(self, ins, outs, sems):
        send_sems, recv_sems, local_sems = sems
        me, peers = self._peers()
        pos = 0
        for gi, size in enumerate(self.sizes):
            for i, pc in enumerate(ins[pos:pos + size]):
                dst = outs[gi].at[me, pl.ds(sum(self.rows[gi][:i]), self.rows[gi][i])]
                pltpu.make_async_copy(pc.at[me], dst, local_sems.at[gi]).start()
                for k, peer, kw in peers:
                    pltpu.make_async_remote_copy(src_ref=pc.at[peer], dst_ref=dst, send_sem=send_sems.at[7 * gi + k - 1],
                                                 recv_sem=recv_sems.at[7 * gi + k - 1], **kw).start()
            pos += size

    def finish(self, ins, outs, sems):
        send_sems, recv_sems, local_sems = sems
        me, peers = self._peers()
        whole = [pltpu.make_async_remote_copy(src_ref=outs[gi].at[peer], dst_ref=outs[gi].at[peer],
                                              send_sem=send_sems.at[7 * gi + k - 1], recv_sem=recv_sems.at[7 * gi + k - 1], **kw)
                 for gi in range(len(self.sizes)) for k, peer, kw in peers]
        for cp in whole:
            cp.wait_recv()
        for cp in whole:
            cp.wait_send()
        for gi in range(len(self.sizes)):
            pltpu.make_async_copy(outs[gi].at[me], outs[gi].at[me], local_sems.at[gi]).wait()


def _peer_list():
    x, y, c = _me()
    out = []
    for k in range(1, N_DEV):
        px = 1 - x if k & 4 else x
        py = 1 - y if k & 2 else y
        pc = 1 - c if k & 1 else c
        out.append((k, 4 * px + 2 * py + pc, dict(device_id=(px, py, pc), device_id_type=MESH)))
    return 4 * x + 2 * y + c, out


SEM_SPEC = pl.BlockSpec(memory_space=pltpu.SEMAPHORE)
HBM_ONLY = pl.BlockSpec(memory_space=pltpu.HBM)
N_SPLIT_SEMS = 2 * (N_DEV - 1)


def exchange_start(piece, after, name):
    def body(piece_ref, land_ref, after_ref, *outs):
        sems, token = outs[:N_SPLIT_SEMS], outs[-1]
        me, peers = _peer_list()
        for k, peer, kw in peers:
            pltpu.make_async_remote_copy(src_ref=piece_ref.at[peer], dst_ref=land_ref.at[me], send_sem=sems[k - 1],
                                         recv_sem=sems[N_DEV - 2 + k], **kw).start()
        token[...] = jnp.zeros_like(token)

    res = pl.pallas_call(
        body, name=name + "_start",
        out_shape=(pltpu.SemaphoreType.DMA(()),) * N_SPLIT_SEMS + (pltpu.HBM(piece.shape, piece.dtype), jax.ShapeDtypeStruct((8, 128), F32)),
        in_specs=(HBM_SPEC, HBM_ONLY, HBM_SPEC),
        out_specs=(SEM_SPEC,) * N_SPLIT_SEMS + (HBM_ONLY, pl.BlockSpec(memory_space=pltpu.VMEM)),
        input_output_aliases={1: N_SPLIT_SEMS},
        compiler_params=pltpu.CompilerParams(has_side_effects=pltpu.SideEffectType.DATAFLOW_SIDE_EFFECTING),
    )(piece, pltpu.with_memory_space_constraint(lax.empty(piece.shape, piece.dtype), pltpu.HBM), after)
    return res[:N_SPLIT_SEMS], res[N_SPLIT_SEMS], res[N_SPLIT_SEMS + 1]


def exchange_wait(sems, piece, land, after, name):
    after = list(after)
    def body(piece_ref, land_ref, *rest):
        sem_refs = rest[:N_SPLIT_SEMS]
        me, peers = _peer_list()
        for k, peer, kw in peers:
            cp = pltpu.make_async_remote_copy(src_ref=piece_ref.at[peer], dst_ref=land_ref.at[peer], send_sem=sem_refs[k - 1],
                                              recv_sem=sem_refs[N_DEV - 2 + k], **kw)
            cp.wait_send()
            cp.wait_recv()

    return pl.pallas_call(
        body, name=name + "_wait",
        out_shape=(pltpu.HBM(land.shape, land.dtype),),
        in_specs=(HBM_SPEC, HBM_ONLY) + (SEM_SPEC,) * N_SPLIT_SEMS + (HBM_SPEC,) * len(after), out_specs=(HBM_ONLY,),
        input_output_aliases={1: 0},
        compiler_params=pltpu.CompilerParams(has_side_effects=pltpu.SideEffectType.DATAFLOW_SIDE_EFFECTING),
    )(piece, land, *sems, *after)[0]


def sum_slots(recv, name, tr):
    n, R, C = recv.shape

    def body(r_ref, o_ref):
        acc = r_ref[0].astype(F32)
        for s in range(1, n):
            acc = acc + r_ref[s].astype(F32)
        o_ref[...] = acc

    return pl.pallas_call(
        body, grid=(R // tr,), name=name,
        in_specs=[pl.BlockSpec((n, tr, C), lambda i: (0, i, 0))], out_specs=pl.BlockSpec((tr, C), lambda i: (i, 0)),
        out_shape=jax.ShapeDtypeStruct((R, C), F32), compiler_params=_cp("parallel"),
    )(recv)


PACK_W, FLAT_W = 1024, 128
MAIN = [
    ("ffn1_w_gate", "col"), ("ffn1_w_up", "col"), ("ffn1_w_down", "row"),
    ("ffn2_w_gate", "col"), ("ffn2_w_up", "col"), ("ffn2_w_down", "row"),
    ("w_ssd_proj", "row"), ("w_mla_proj", "row"), ("w_out", "row"),
    ("w_xq", "row"), ("w_xk", "row"), ("w_xv", "row"), ("w_xo", "row"),
    ("w_uk", "col"), ("w_uv", "col"), ("w_in", "col"),
]
FLAT = [("w_uq", "col")]
BIG = MAIN + FLAT
SMALL = ["ffn1_pre_g", "ffn1_post_g", "mix_pre_g", "conv_b", "dt_bias", "a_log", "d_skip", "ssd_norm_g", "q_norm_g",
         "kv_norm_g", "gate_bias", "mix_post_g", "xa_pre_g", "mem_norm_g", "xa_post_g", "ffn2_pre_g", "ffn2_post_g"]
WEIGHTS = ['ffn1_pre_g', 'ffn1_w_gate', 'ffn1_w_up', 'ffn1_w_down', 'ffn1_post_g', 'mix_pre_g', 'w_in', 'conv_w', 'conv_b',
           'dt_bias', 'a_log', 'd_skip', 'ssd_norm_g', 'w_ssd_proj', 'q_norm_g', 'w_uq', 'kv_norm_g', 'w_uk', 'w_uv',
           'w_mla_proj', 'gate_bias', 'w_out', 'mix_post_g', 'xa_pre_g', 'mem_norm_g', 'w_xq', 'w_xk', 'w_xv', 'w_xo',
           'xa_post_g', 'ffn2_pre_g', 'ffn2_w_gate', 'ffn2_w_up', 'ffn2_w_down', 'ffn2_post_g']


def _pack_rows(w, kind, width):
    m = w[0].T if kind == "col" else w[0]
    return m.reshape(-1, width)


KIND = dict(BIG)
GATHER_PLAN = {
    "ffn1_pre": (["ffn1_w_gate", "ffn1_w_up"], []),
    "ffn1_gate_up": (["ffn1_w_down", "w_in@0"], []),
    "ffn1_down": (["w_in@1"], ["conv_w"]),
    "ssd_fwd": (["w_ssd_proj", "w_mla_proj", "w_out", "w_uk", "w_uv"], ["w_uq"]),
    "attn_fwd": (["w_xq", "w_xk", "w_xv", "w_xo", "ffn2_w_gate", "ffn2_w_up", "ffn2_w_down"], []),
}
EARLY_EXCHANGE, LAST_EXCHANGE = "early", "last"
SCATTER_PLAN = {
    "attn_bwd": [["ffn2_w_gate", "ffn2_w_up", "ffn2_w_down"], ["w_xq", "w_xk", "w_xv", "w_xo"]],
    "ssd_bwd": [["w_ssd_proj", "w_mla_proj", "w_out"], ["w_uk", "w_uv"], ["w_uq"]],
    "in_bwd": [["w_in#0"]],
    "ffn1:down_bwd": [["w_in#1"]],
    "ffn1:dwd": [["w_in#2"]],
    "ffn1:dwg": [["ffn1_w_down#0"]],
    "ffn1:dwu": [["ffn1_w_down#1"]],
    "early": [["ffn1_w_gate"]],
    "last": [["ffn1_w_up"]],
}
PARTS = {"w_in@0": ("w_in", 0, 336), "w_in@1": ("w_in", 336, 662),
         "w_in#0": ("w_in", 0, 336), "w_in#1": ("w_in", 336, 496), "w_in#2": ("w_in", 496, 662),
         "ffn1_w_down#0": ("ffn1_w_down", 0, 176), "ffn1_w_down#1": ("ffn1_w_down", 176, 352)}


def _parts_of(base, mark):
    return sorted(pn for pn, (b, _, _) in PARTS.items() if b == base and mark in pn)


class Stage:
    def __init__(self, w):
        self.w = w
        self.width = {n: PACK_W if (n, k) in MAIN else FLAT_W for n, k in BIG}
        self.nrows = {n: math.prod(w[n].shape) // self.width[n] for n, _ in BIG}
        self.recv = {}
        self.split = {}
        self.arrived_parts = {}

    def _shards(self, tag):
        names_main, names_flat = GATHER_PLAN[tag]

        def pack(n):
            if n == "conv_w":
                return _pad_rows(lax.bitcast_convert_type(self.w[n][0], BF16).reshape(-1, FLAT_W), 16)
            base, r0, r1 = PARTS.get(n, (n, 0, None))
            return _pack_rows(self.w[base], KIND[base], self.width[base])[r0:r1].astype(BF16)
        shards = []
        if names_main:
            pieces = [pack(n) for n in names_main]
            shards.append((jnp.concatenate(pieces, axis=0), [pc.shape[0] for pc in pieces]))
        if names_flat:
            pieces = [pack(n) for n in names_flat]
            shards.append((jnp.concatenate(pieces, axis=0), [pc.shape[0] for pc in pieces]))
        return shards

    def gather(self, tag):
        return GatherComm(self._shards(tag)) if tag in GATHER_PLAN else None

    def gathered(self, tag, outs, W, p):
        if tag not in GATHER_PLAN:
            return
        names_main, names_flat = GATHER_PLAN[tag]
        outs = list(outs)
        for n in names_main + names_flat:
            rows = outs.pop(0)
            if n == "conv_w":
                cw = self.w[n]
                bits = rows[:, :2 * math.prod(cw.shape) // FLAT_W].reshape((N_DEV,) + cw.shape[1:] + (2,))
                p[n] = lax.bitcast_convert_type(bits, F32).transpose(1, 0, 2).reshape(cw.shape[1], -1)
                continue
            if n in PARTS:
                self.arrived_parts[n] = rows
                base = PARTS[n][0]
                mine = _parts_of(base, "@")
                if not all(pn in self.arrived_parts for pn in mine):
                    continue
                n, rows = base, jnp.concatenate([self.arrived_parts[pn] for pn in mine], axis=1)
            K = self.w[n].shape[1] if KIND[n] == "col" else PACK_W
            W[n] = rows.reshape(-1, K)

    def pieces(self, tag, gw):
        def piece(n):
            if n in PARTS:
                base, r0, r1 = PARTS[n]
                return gw[base].reshape(N_DEV, self.nrows[base], self.width[base])[:, r0:r1]
            return gw[n].reshape(N_DEV, self.nrows[n], self.width[n])
        return [[piece(n) for n in names] for names in SCATTER_PLAN[tag]]

    def scatter(self, tag, gw):
        return ScatterComm(self.pieces(tag, gw)) if tag in SCATTER_PLAN else None

    def scattered(self, tag, outs):
        if tag in SCATTER_PLAN:
            self.recv[tag] = outs

    def split_start(self, tag, gw, after=None, behind=None):
        after = after if behind is None else self.recv[behind][0]
        piece, = self.pieces(tag, gw)[0]
        sems, land, token = exchange_start(piece, after, "exchange_" + tag)
        self.split[tag] = (sems, piece, land)
        return token

    def split_wait(self, tag, after):
        sems, piece, land = self.split.pop(tag)
        land = exchange_wait(sems, piece, land, after, "exchange_" + tag)
        me = _dev_index()
        return lax.dynamic_update_index_in_dim(land, lax.dynamic_index_in_dim(piece, me, 0, keepdims=False), me, 0)


def _pad_rows(a, mult):
    r = (-a.shape[0]) % mult
    return a if r == 0 else jnp.concatenate([a, jnp.zeros((r,) + a.shape[1:], a.dtype)], axis=0)


def _pack_small(vals, loss_row=None, conv_w=None):
    rows = []
    for v in vals:
        f = v.reshape(-1)
        f = jnp.concatenate([f, jnp.zeros(((-f.shape[0]) % 128,), F32)])
        rows.append(f.reshape(-1, 128))
    if conv_w is not None:
        rows.append(conv_w.reshape(-1, 128))
    if loss_row is not None:
        rows.append(loss_row)
    return _pad_rows(jnp.concatenate(rows, axis=0), 8)


def _tn(a, b, name, out_dtype=BF16, comm=None):
    M, N = a.shape[1], b.shape[1]
    T = a.shape[0]
    tm = M if M <= 1536 else M // 2
    tk = 2048 if T % 2048 == 0 and T > 2048 else None
    res = mm([[(a, b, "tn")]], [out_dtype], name, tm=tm, tn=N, tk=tk, comm=comm)
    return res[0] if comm is None else (res[0], res[1:])


class NoStage:
    def gather(self, tag):
        return None

    def gathered(self, tag, outs, W, p):
        pass

    def scatter(self, tag, gw):
        return None

    def scattered(self, tag, outs):
        pass

    def split_start(self, tag, gw, after=None, behind=None):
        return None


def _ffn_fwd(x, gpre, gpost, W, p, tag, stage, target=None):
    comm = stage.gather(tag + "_pre")
    h = rms_fwd(x, gpre, tag + "_pre", comm=comm)
    if comm is not None:
        h, arrived = h
        stage.gathered(tag + "_pre", arrived, W, p)

    def swi(accs, ex):
        sil, dsil = _silu_parts(accs[0])
        return sil, accs[1] * dsil, sil * accs[1]
    G, U, A, *arrived = mm([[(h, W[tag + "_w_gate"], "nt")], [(h, W[tag + "_w_up"], "nt")]], [BF16, BF16, BF16], tag + "_gate_up",
                           tn=DFF // 2, epi=swi, comm=stage.gather(tag + "_gate_up"), sub=4 if h.shape[0] % 1024 == 0 else 1)
    stage.gathered(tag + "_gate_up", arrived, W, p)
    H, y, *rest = mm_resid(A, W[tag + "_w_down"], x, gpost, FFN_RES, tag + "_down", comm=stage.gather(tag + "_down"), target=target)
    saved = (x, h, G, U, A, H)
    if target is not None:
        return y, saved, rest[0]
    stage.gathered(tag + "_down", rest, W, p)
    return y, saved


def _ffn_bwd(dy, saved, gpre, gpost, wg_t, wu_t, wd, tag, stage, gw):
    x, h, G, U, A, H = saved
    dH, dgpost = resid_bwd(H, gpost, dy, FFN_RES, tag + "_post_bwd")

    def dswi(accs, ex):
        return accs[0] * ex[1], accs[0] * ex[0]

    def hosted(where, call):
        comm = stage.scatter(tag + ":" + where, gw)
        res = call(comm)
        if comm is None:
            return res
        stage.scattered(tag + ":" + where, res[1])
        return res[0]

    res = hosted("down_bwd", lambda comm: (lambda r: r if comm is None else (r[:2], r[2:]))(
        mm([[(dH, wd, "nt")]], [BF16, BF16], tag + "_down_bwd", tn=DFF // 2, epi=dswi, extras=[G, U], comm=comm,
           sub=4 if dH.shape[0] % 1024 == 0 else 1)))
    dG, dU = res
    gw[tag + "_w_down"] = hosted("dwd", lambda comm: _tn(A, dH, tag + "_dwd", comm=comm))
    gw[tag + "_w_gate"] = hosted("dwg", lambda comm: _tn(dG, h, tag + "_dwg", comm=comm))
    gw[tag + "_w_up"] = hosted("dwu", lambda comm: _tn(dU, h, tag + "_dwu", comm=comm))
    token = stage.split_start(EARLY_EXCHANGE, gw, behind=tag + ":dwu") if tag == "ffn1" else None
    dx, dgpre = mm_rms_bwd([(dG, wg_t, "nn"), (dU, wu_t, "nn")], x, gpre, tag + "_gate_up_bwd", resid=dy, token=token)
    return dx, dgpre, dgpost


def _local_step(x, mem, positions, tgt, W, p, stage=None):
    stage = stage or NoStage()
    nseq = x.shape[0]
    T = nseq * x.shape[1]
    x0 = x.reshape(T, D)
    mem2 = mem.reshape(-1, D)

    x1, ffn1 = _ffn_fwd(x0, p["ffn1_pre_g"], p["ffn1_post_g"], W, p, "ffn1", stage)

    w_in_t = W["w_in"]
    bounds = [0]
    for n in (SSD_INNER, CONV_CH, SSD_H, QR, KVR, ROPE, 2 * D):
        bounds.append(bounds[-1] + n)
    wt_z, wt_xbc, wt_dt, wt_q, wt_kv, wt_kr, wt_gate = [w_in_t[bounds[i]:bounds[i + 1]] for i in range(7)]
    wt_dt, wt_kr = _pad_rows(wt_dt, SLOT), _pad_rows(wt_kr, SLOT)
    wt_dtkr = jnp.concatenate([wt_dt, wt_kr], axis=0)
    hm = rms_fwd(x1, p["mix_pre_g"], "mix_pre")
    z = mm1(hm, wt_z, "nt", BF16, "in_z")
    xbc = mm1(hm, wt_xbc, "nt", BF16, "in_xbc")
    q_c = mm1(hm, wt_q, "nt", F32, "in_q", tn=QR)
    kv_c = mm1(hm, wt_kv, "nt", F32, "in_kv")
    dtkr = mm1(hm, wt_dtkr, "nt", F32, "in_dtkr")
    gl = mm1(hm, wt_gate, "nt", BF16, "in_gate")

    xbc_act = conv_fwd(xbc, p["conv_w"], p["conv_b"], nseq)
    y_ssd_core, prev, *arrived = ssd_fwd(xbc_act, dtkr, p["dt_bias"], p["a_log"], p["d_skip"], nseq, comm=stage.gather("ssd_fwd"))
    stage.gathered("ssd_fwd", arrived, W, p)
    yn = gated_norm_fwd(y_ssd_core, z, p["ssd_norm_g"], "ssd_norm")
    y_ssd = mm1(yn, W["w_ssd_proj"], "nn", BF16, "ssd_proj")

    slot_rows = lambda wt, per: jnp.pad(wt.reshape(MLA_H, per, -1), ((0, 0), (0, SLOT - per), (0, 0))).reshape(MLA_H * SLOT, -1)
    wq_s, wk_s, wv_s = slot_rows(W["w_uq"], QK), slot_rows(W["w_uk"], NOPE), slot_rows(W["w_uv"], VD)
    wo_s = slot_rows(W["w_mla_proj"], VD)
    qn = rms_fwd(q_c, p["q_norm_g"], "q_norm")
    rope_c, rope_s = rope_table(*_rope_inputs(positions))
    rope_args = [("rows", rope_c), ("rows", rope_s)]
    Qc, = mm([[(qn, wq_s, "nt")]], [BF16], "uq", epi=rope_q_epilogue, extras=rope_args, sub=4 if T % 1024 == 0 else 1)
    kvn = rms_fwd(kv_c, p["kv_norm_g"], "kv_norm")
    Kc, = mm([[(kvn, wk_s, "nt")]], [BF16], "uk", epi=rope_k_epilogue, extras=rope_args + [("rows", dtkr)],
             sub=4 if T % 1024 == 0 else 1)
    v_s = mm1(kvn, wv_s, "nt", BF16, "uv")
    o_s, lse, *arrived = attn_slot_fwd(Qc, Kc, v_s, nseq, comm=stage.gather("attn_fwd"))
    stage.gathered("attn_fwd", arrived, W, p)
    y_mla = mm1(o_s, wo_s, "nn", BF16, "mla_proj")

    merged = merge_fwd(gl, y_ssd, y_mla, p["gate_bias"], "merge")
    hmix, x2 = mm_resid(merged, W["w_out"], x1, p["mix_post_g"], 1.0, "mix_out")

    hq = rms_fwd(x2, p["xa_pre_g"], "xa_pre")
    mn = rms_fwd(mem2, p["mem_norm_g"], "mem_norm")
    xq = mm1(hq, W["w_xq"], "nn", BF16, "xq")
    xk = mm1(mn, W["w_xk"], "nn", BF16, "xk")
    xv = mm1(mn, W["w_xv"], "nn", BF16, "xv")
    xo, *arrived = xattn_fwd(xq, xk, xv, nseq, comm=stage.gather("xattn_fwd"))
    stage.gathered("xattn_fwd", arrived, W, p)
    ho, x3 = mm_resid(xo, W["w_xo"], x2, p["xa_post_g"], 1.0, "xo")

    dx4, ffn2, sq_cols = _ffn_fwd(x3, p["ffn2_pre_g"], p["ffn2_post_g"], W, p, "ffn2", stage, target=tgt.reshape(T, D))
    loss_row = (0.5 / D) * jnp.sum(sq_cols.reshape(-1, 128), axis=0, keepdims=True)

    gw, gs = {}, {}
    dx3, gs["ffn2_pre_g"], gs["ffn2_post_g"] = _ffn_bwd(
        dx4, ffn2, p["ffn2_pre_g"], p["ffn2_post_g"], W["ffn2_w_gate"], W["ffn2_w_up"], W["ffn2_w_down"], "ffn2", stage, gw)

    dho, gs["xa_post_g"] = resid_bwd(ho, p["xa_post_g"], dx3, 1.0, "xa_post_bwd")
    dxo = mm1(dho, W["w_xo"], "nt", BF16, "xo_bwd")
    gw["w_xo"] = _tn(xo, dho, "d_w_xo")
    dxq, dxk, dxv = xattn_bwd(xq, xk, xv, dxo, nseq)
    dx2, gs["xa_pre_g"] = mm_rms_bwd([(dxq, W["w_xq"], "nt")], x2, p["xa_pre_g"], "xq_bwd", resid=dx3)
    gw["w_xq"] = _tn(hq, dxq, "d_w_xq")
    dmn = mm([[(dxk, W["w_xk"], "nt"), (dxv, W["w_xv"], "nt")]], [F32], "xkv_bwd")[0]
    gw["w_xk"] = _tn(mn, dxk, "d_w_xk")
    gw["w_xv"] = _tn(mn, dxv, "d_w_xv")
    _, gs["mem_norm_g"] = rms_bwd(mem2, p["mem_norm_g"], dmn, "mem_norm_bwd", dx_dtype=BF16)

    dhmix, gs["mix_post_g"] = resid_bwd(hmix, p["mix_post_g"], dx2, 1.0, "mix_post_bwd")
    dmerged = mm1(dhmix, W["w_out"], "nt", BF16, "mix_out_bwd")
    gw["w_out"] = _tn(merged, dhmix, "d_w_out")
    dys, dym, dgl, gs["gate_bias"] = merge_bwd(gl, y_ssd, y_mla, dmerged, p["gate_bias"], "merge_bwd")

    unslot = lambda g, per: g.reshape(MLA_H, SLOT, -1)[:, :per].reshape(MLA_H * per, -1)
    do_s = mm1(dym, wo_s, "nt", BF16, "mla_proj_bwd")
    gw["w_mla_proj"] = unslot(_tn(o_s, dym, "d_w_mla_proj"), VD)
    dQc, dKc, dv_s, *sent = attn_slot_bwd(Qc, Kc, v_s, o_s, lse, do_s, nseq, comm=stage.scatter("attn_bwd", gw))
    stage.scattered("attn_bwd", sent)
    dq_s, dkn_s, dkr = rope_slot_bwd(dQc, dKc, rope_c, rope_s, "rope_bwd")
    dq_c, gs["q_norm_g"] = mm_rms_bwd([(dq_s, wq_s, "nn")], q_c, p["q_norm_g"], "uq_bwd", dx_dtype=BF16)
    gw["w_uq"] = unslot(_tn(dq_s, qn, "d_w_uq"), QK)
    dkv_c, gs["kv_norm_g"] = mm_rms_bwd([(dkn_s, wk_s, "nn"), (dv_s, wv_s, "nn")], kv_c, p["kv_norm_g"], "ukv_bwd", dx_dtype=BF16)
    gw["w_uk"] = unslot(_tn(dkn_s, kvn, "d_w_uk"), NOPE)
    gw["w_uv"] = unslot(_tn(dv_s, kvn, "d_w_uv"), VD)

    dyn = mm1(dys, W["w_ssd_proj"], "nt", BF16, "ssd_proj_bwd")
    gw["w_ssd_proj"] = _tn(yn, dys, "d_w_ssd_proj")
    dyc, dz, gs["ssd_norm_g"] = gated_norm_bwd(y_ssd_core, z, dyn, p["ssd_norm_g"], "ssd_norm_bwd")
    dxbc_act, ddtr, gs["dt_bias"], gs["a_log"], gs["d_skip"], *sent = ssd_bwd(
        xbc_act, dtkr, p["dt_bias"], p["a_log"], p["d_skip"], prev, dyc, nseq, comm=stage.scatter("ssd_bwd", gw))
    stage.scattered("ssd_bwd", sent)
    dxbc, gs["conv_w"], gs["conv_b"] = conv_bwd(xbc, p["conv_w"], p["conv_b"], dxbc_act, nseq)

    gw["w_in"] = jnp.concatenate([_tn(dz, hm, "d_w_in_z"), _tn(dxbc, hm, "d_w_in_xbc"), _tn(ddtr, hm, "d_w_in_dt")[:SSD_H],
                                  _tn(dq_c, hm, "d_w_in_q"), _tn(dkv_c, hm, "d_w_in_kv"), _tn(dkr, hm, "d_w_in_kr")[:ROPE],
                                  _tn(dgl, hm, "d_w_in_gate")], axis=0)
    dx1, gs["mix_pre_g"], *sent = mm_rms_bwd(
        [(dz, wt_z, "nn"), (dxbc, wt_xbc, "nn"), (ddtr, wt_dt, "nn"), (dq_c, wt_q, "nn"), (dkv_c, wt_kv, "nn"),
         (dkr, wt_kr, "nn"), (dgl, wt_gate, "nn")], x1, p["mix_pre_g"], "in_bwd", resid=dx2, comm=stage.scatter("in_bwd", gw))
    stage.scattered("in_bwd", sent)

    dx0, gs["ffn1_pre_g"], gs["ffn1_post_g"] = _ffn_bwd(
        dx1, ffn1, p["ffn1_pre_g"], p["ffn1_post_g"], W["ffn1_w_gate"], W["ffn1_w_up"], W["ffn1_w_down"], "ffn1", stage, gw)
    return loss_row, dx0.reshape(x.shape), gw, gs


def kernel(x, mem, positions, ffn1_pre_g, ffn1_w_gate, ffn1_w_up, ffn1_w_down, ffn1_post_g, mix_pre_g, w_in, conv_w, conv_b, dt_bias, a_log, d_skip, ssd_norm_g, w_ssd_proj, q_norm_g, w_uq, kv_norm_g, w_uk, w_uv, w_mla_proj, gate_bias, w_out, mix_post_g, xa_pre_g, mem_norm_g, w_xq, w_xk, w_xv, w_xo, xa_post_g, ffn2_pre_g, ffn2_w_gate, ffn2_w_up, ffn2_w_down, ffn2_post_g, loss_target, m_ffn1_pre_g, m_ffn1_w_gate, m_ffn1_w_up, m_ffn1_w_down, m_ffn1_post_g, m_mix_pre_g, m_w_in, m_conv_w, m_conv_b, m_dt_bias, m_a_log, m_d_skip, m_ssd_norm_g, m_w_ssd_proj, m_q_norm_g, m_w_uq, m_kv_norm_g, m_w_uk, m_w_uv, m_w_mla_proj, m_gate_bias, m_w_out, m_mix_post_g, m_xa_pre_g, m_mem_norm_g, m_w_xq, m_w_xk, m_w_xv, m_w_xo, m_xa_post_g, m_ffn2_pre_g, m_ffn2_w_gate, m_ffn2_w_up, m_ffn2_w_down, m_ffn2_post_g, v_ffn1_pre_g, v_ffn1_w_gate, v_ffn1_w_up, v_ffn1_w_down, v_ffn1_post_g, v_mix_pre_g, v_w_in, v_conv_w, v_conv_b, v_dt_bias, v_a_log, v_d_skip, v_ssd_norm_g, v_w_ssd_proj, v_q_norm_g, v_w_uq, v_kv_norm_g, v_w_uk, v_w_uv, v_w_mla_proj, v_gate_bias, v_w_out, v_mix_post_g, v_xa_pre_g, v_mem_norm_g, v_w_xq, v_w_xk, v_w_xv, v_w_xo, v_xa_post_g, v_ffn2_pre_g, v_ffn2_w_gate, v_ffn2_w_up, v_ffn2_w_down, v_ffn2_post_g):
    a = dict(locals())
    w = {n: a[n] for n in WEIGHTS}
    m = {n: a["m_" + n] for n in WEIGHTS}
    v = {n: a["v_" + n] for n in WEIGHTS}

    stage = Stage(w)
    W, p = {}, {n: w[n] for n in SMALL}
    loss_row, grad_x, gw, gs = _local_step(x, mem, positions, loss_target, W, p, stage)

    sm = _pack_small([gs[n] for n in SMALL], loss_row=loss_row, conv_w=gs["conv_w"])
    sm_all = jnp.broadcast_to(sm[None], (N_DEV,) + sm.shape)
    sm_sems, sm_land, sm_token = exchange_start(sm_all, sm, "exchange_small")
    token = stage.split_start(LAST_EXCHANGE, gw, after=sm_token)
    grads, delta, new_m, new_v = {}, {}, {}, {}
    raw_results = []

    def finish(n, buf, piece, token=None):
        col = KIND[n] == "col"
        turn = (lambda t: t.T) if col else (lambda t: t)
        K = w[n].shape[1]
        if col and buf.shape[2] != K:
            buf = buf.reshape(buf.shape[0], -1, K)
        res = adamw_from_slots(buf, piece, turn(w[n][0]), turn(m[n][0]), turn(v[n][0]), "adamw_" + n, token=token)
        raw_results.append(res[3])
        grads[n], delta[n], new_m[n], new_v[n] = [turn(r)[None] for r in res]

    finish(SCATTER_PLAN[EARLY_EXCHANGE][0][0], stage.split_wait(EARLY_EXCHANGE, after=[token]), 0, token)
    parts = {}
    for tag, groups in SCATTER_PLAN.items():
        if tag in (EARLY_EXCHANGE, LAST_EXCHANGE):
            continue
        for names, buf in zip(groups, stage.recv[tag]):
            for piece, n in enumerate(names):
                if n in PARTS:
                    parts[n] = sum_slots(buf, "sum_" + n.replace("#", "_"), tr=buf.shape[1])
                else:
                    finish(n, buf, piece, token)
    for base in sorted({PARTS[pn][0] for pn in parts}):
        rows = jnp.concatenate([parts[pn] for pn in _parts_of(base, "#")], axis=0)
        finish(base, rows[None], 0, token)
    sm_land = exchange_wait(sm_sems, sm_all, sm_land, raw_results, "exchange_small")
    s_rows = sum_slots(lax.dynamic_update_index_in_dim(sm_land, sm, _dev_index(), 0), "sum_small", tr=sm.shape[0])
    conv_w_full = p["conv_w"]
    small = adamw_small(s_rows, [w[n] for n in SMALL], [m[n] for n in SMALL], [v[n] for n in SMALL])
    for t, vals in zip((grads, delta, new_m, new_v), small):
        t.update(zip(SMALL, vals))
    r1 = sum(-(-w[n].shape[1] // 128) for n in SMALL)
    ncw = math.prod(conv_w_full.shape) // 128
    cw_grad_full = s_rows[r1:r1 + ncw].reshape(conv_w_full.shape)
    wsh = conv_w.shape[2]
    grads["conv_w"] = lax.dynamic_slice_in_dim(cw_grad_full, _dev_index() * wsh, wsh, axis=1)[None]
    loss = jnp.sum(s_rows[r1 + ncw])
    d_, m_, v_ = adamw(conv_w[0], grads["conv_w"][0], m["conv_w"][0], v["conv_w"][0], "adamw_conv_w")
    delta["conv_w"], new_m["conv_w"], new_v["conv_w"] = d_[None], m_[None], v_[None]
    finish(SCATTER_PLAN[LAST_EXCHANGE][0][0], stage.split_wait(LAST_EXCHANGE, after=raw_results + [small[3][0], v_]), 0)
    return (loss, grad_x, *[grads[n] for n in WEIGHTS], *[delta[n] for n in WEIGHTS],
            *[new_m[n] for n in WEIGHTS], *[new_v[n] for n in WEIGHTS])
```
